```python
import jax, jax.numpy as jnp
from jax import lax
import numpy as np

D_MODEL = 1024
BATCH = 16
SEQ = 2048
DEPTH = 1

CHUNK = 64
HEAD_DIM = 64
A_HEADS = 8
A_PREV_CHUNKS = 8
A_MAX_REL = 128
B_Q_HEADS = 8
B_KV_HEADS = 2
B_GROUP = B_Q_HEADS // B_KV_HEADS
B_WINDOW = 128
B_PREV_CHUNKS = (B_WINDOW - 1 + CHUNK - 1) // CHUNK
A_WIDTH = A_HEADS * HEAD_DIM
B_Q_WIDTH = B_Q_HEADS * HEAD_DIM
B_KV_WIDTH = B_KV_HEADS * HEAD_DIM
IN_COLS = 3 * A_WIDTH + B_Q_WIDTH + 2 * B_KV_WIDTH
D_FF = 2816
PLE_DIM = 256
EPS = 1e-6
NEG_INF = -1e30

kernel_name = "hybrid_chunked_relpos_swa_sink_macaron_ple"


def rms_norm(x, gain):
    xf = x.astype(jnp.float32)
    y = xf * lax.rsqrt(jnp.mean(xf * xf, axis=-1, keepdims=True) + EPS)
    return (y * gain.astype(jnp.float32)).astype(x.dtype)


def swiglu_ffn(x, w_gu, w_down):
    g, u = jnp.split(x @ w_gu, 2, axis=-1)
    return (jax.nn.silu(g) * u) @ w_down


def alibi_slopes(n_heads):
    return np.array([2.0 ** (-8.0 * (h + 1) / n_heads) for h in range(n_heads)], dtype=np.float32)


def band_distance(n_prev):
    i = np.arange(CHUNK)[:, None]
    j = np.arange((n_prev + 1) * CHUNK)[None, :]
    return i + n_prev * CHUNK - j


def chunk_band_attention(q, k, v, n_prev, bias, sink):
    b, hkv, g, s, dh = q.shape
    n_chunks = s // CHUNK
    band = (n_prev + 1) * CHUNK
    pad = n_prev * CHUNK
    kp = jnp.pad(k, ((0, 0), (0, 0), (pad, 0), (0, 0)))
    vp = jnp.pad(v, ((0, 0), (0, 0), (pad, 0), (0, 0)))
    scale = dh ** -0.5
    key_idx = jnp.arange(band)

    def one_chunk(c):
        start = c * CHUNK
        qc = lax.dynamic_slice_in_dim(q, start, CHUNK, axis=3)
        kc = lax.dynamic_slice_in_dim(kp, start, band, axis=2)
        vc = lax.dynamic_slice_in_dim(vp, start, band, axis=2)
        scores = jnp.einsum('bkgqd,bksd->bkgqs', qc.astype(jnp.float32),
                            kc.astype(jnp.float32)) * scale + bias
        valid = key_idx >= (n_prev - c) * CHUNK
        scores = jnp.where(valid, scores, NEG_INF)
        if sink is None:
            probs = jax.nn.softmax(scores, axis=-1)
        else:
            sink_col = jnp.broadcast_to(sink.astype(jnp.float32).reshape(1, hkv, g, 1, 1),
                                        (b, hkv, g, CHUNK, 1))
            probs = jax.nn.softmax(jnp.concatenate([scores, sink_col], axis=-1), axis=-1)[..., :band]
        out = jnp.einsum('bkgqs,bksd->bkgqd', probs, vc.astype(jnp.float32))
        return out.astype(v.dtype)

    outs = lax.map(one_chunk, jnp.arange(n_chunks))
    outs = jnp.transpose(outs, (1, 0, 4, 2, 3, 5))
    return outs.reshape(b, s, hkv * g * dh)


def _fwd_setup_inputs(seed: int = 0) -> dict:
    key = jax.random.key(seed)
    ks = jax.random.split(key, 24)
    f32 = jnp.float32

    def w(k, shape, fan_in):
        return jax.random.normal(k, shape, f32) * (fan_in ** -0.5)

    def gain(k, n):
        return 1.0 + 0.05 * jax.random.normal(k, (DEPTH, n), f32)

    return {
        "x": jax.random.normal(ks[0], (BATCH, SEQ, D_MODEL), f32),
        "p": jax.random.normal(ks[1], (DEPTH, BATCH, SEQ, PLE_DIM), f32),
        "ffn1_norm": gain(ks[2], D_MODEL),
        "ffn1_w_gu": w(ks[3], (DEPTH, D_MODEL, 2 * D_FF), D_MODEL),
        "ffn1_w_down": w(ks[4], (DEPTH, D_FF, D_MODEL), D_FF),
        "mix_norm": gain(ks[5], D_MODEL),
        "w_in": w(ks[6], (DEPTH, D_MODEL, IN_COLS), D_MODEL),
        "a_q_norm": gain(ks[7], HEAD_DIM),
        "a_k_norm": gain(ks[8], HEAD_DIM),
        "a_rel_bias": 0.1 * jax.random.normal(ks[9], (DEPTH, A_HEADS, 2 * A_MAX_REL + 1), f32),
        "b_q_norm": gain(ks[10], HEAD_DIM),
        "b_k_norm": gain(ks[11], HEAD_DIM),
        "b_sinks": 0.5 * jax.random.normal(ks[12], (DEPTH, B_Q_HEADS), f32),
        "w_gate": w(ks[13], (DEPTH, D_MODEL, 2 * D_MODEL), D_MODEL),
        "w_proj_a": w(ks[14], (DEPTH, A_WIDTH, D_MODEL), A_WIDTH),
        "w_proj_b": w(ks[15], (DEPTH, B_Q_WIDTH, D_MODEL), B_Q_WIDTH),
        "w_out": w(ks[16], (DEPTH, D_MODEL, D_MODEL), D_MODEL),
        "ffn2_norm": gain(ks[17], D_MODEL),
        "ffn2_w_gu": w(ks[18], (DEPTH, D_MODEL, 2 * D_FF), D_MODEL),
        "ffn2_w_down": w(ks[19], (DEPTH, D_FF, D_MODEL), D_FF),
        "ple_norm": gain(ks[20], D_MODEL),
        "w_ple_gate": w(ks[21], (DEPTH, D_MODEL, D_MODEL), D_MODEL),
        "w_ple_proj": w(ks[22], (DEPTH, PLE_DIM, D_MODEL), PLE_DIM),
    }


def _fwd_reference(x, p, ffn1_norm, ffn1_w_gu, ffn1_w_down, mix_norm, w_in, a_q_norm, a_k_norm,
              a_rel_bias, b_q_norm, b_k_norm, b_sinks, w_gate, w_proj_a, w_proj_b, w_out,
              ffn2_norm, ffn2_w_gu, ffn2_w_down, ple_norm, w_ple_gate, w_ple_proj):
    b, s, _ = x.shape
    split_points = list(np.cumsum([A_WIDTH, A_WIDTH, A_WIDTH, B_Q_WIDTH, B_KV_WIDTH]))
    a_rel_idx = np.clip(band_distance(A_PREV_CHUNKS), -A_MAX_REL, A_MAX_REL) + A_MAX_REL
    b_dist = np.abs(band_distance(B_PREV_CHUNKS)).astype(np.float32)
    b_alibi = jnp.asarray((-alibi_slopes(B_Q_HEADS)[:, None, None] * b_dist[None])
                          .reshape(B_KV_HEADS, B_GROUP, CHUNK, -1))

    h = x
    for i in range(DEPTH):
        h = h + 0.5 * swiglu_ffn(rms_norm(h, ffn1_norm[i]), ffn1_w_gu[i], ffn1_w_down[i])

        u = rms_norm(h, mix_norm[i])
        qa, ka, va, qb, kb, vb = jnp.split(u @ w_in[i], split_points, axis=-1)

        qa = rms_norm(qa.reshape(b, s, A_HEADS, HEAD_DIM), a_q_norm[i])
        ka = rms_norm(ka.reshape(b, s, A_HEADS, HEAD_DIM), a_k_norm[i])
        qa = jnp.transpose(qa, (0, 2, 1, 3))[:, :, None]
        ka = jnp.transpose(ka, (0, 2, 1, 3))
        va = jnp.transpose(va.reshape(b, s, A_HEADS, HEAD_DIM), (0, 2, 1, 3))
        a_bias = a_rel_bias[i].astype(jnp.float32)[:, a_rel_idx][:, None]
        ya = chunk_band_attention(qa, ka, va, A_PREV_CHUNKS, a_bias, None)

        qb = rms_norm(qb.reshape(b, s, B_KV_HEADS, B_GROUP, HEAD_DIM), b_q_norm[i])
        kb = rms_norm(kb.reshape(b, s, B_KV_HEADS, HEAD_DIM), b_k_norm[i])
        qb = jnp.transpose(qb, (0, 2, 3, 1, 4))
        kb = jnp.transpose(kb, (0, 2, 1, 3))
        vb = jnp.transpose(vb.reshape(b, s, B_KV_HEADS, HEAD_DIM), (0, 2, 1, 3))
        yb = chunk_band_attention(qb, kb, vb, B_PREV_CHUNKS, b_alibi,
                                  b_sinks[i].reshape(B_KV_HEADS, B_GROUP))

        ga, gb = jnp.split(jax.nn.sigmoid(u @ w_gate[i]), 2, axis=-1)
        merged = ga * (ya @ w_proj_a[i]) + gb * (yb @ w_proj_b[i])
        h = h + merged @ w_out[i]

        h = h + 0.5 * swiglu_ffn(rms_norm(h, ffn2_norm[i]), ffn2_w_gu[i], ffn2_w_down[i])

        ple_gate = jax.nn.sigmoid(rms_norm(h, ple_norm[i]) @ w_ple_gate[i])
        h = h + ple_gate * (p[i] @ w_ple_proj[i])
    return h


import jax as _jax
import jax.numpy as _jnp

TWIN_FORMAT = 'train_step'
FWD_PARAMS = ['x', 'p', 'ffn1_norm', 'ffn1_w_gu', 'ffn1_w_down', 'mix_norm', 'w_in', 'a_q_norm', 'a_k_norm', 'a_rel_bias', 'b_q_norm', 'b_k_norm', 'b_sinks', 'w_gate', 'w_proj_a', 'w_proj_b', 'w_out', 'ffn2_norm', 'ffn2_w_gu', 'ffn2_w_down', 'ple_norm', 'w_ple_gate', 'w_ple_proj']
TWIN_WEIGHTS = ['ffn1_norm', 'ffn1_w_gu', 'ffn1_w_down', 'mix_norm', 'w_in', 'a_q_norm', 'a_k_norm', 'a_rel_bias', 'b_q_norm', 'b_k_norm', 'b_sinks', 'w_gate', 'w_proj_a', 'w_proj_b', 'w_out', 'ffn2_norm', 'ffn2_w_gu', 'ffn2_w_down', 'ple_norm', 'w_ple_gate', 'w_ple_proj']
TWIN_DIFF_INPUT = 'x'
TWIN_INPUTS = ['x', 'p', 'ffn1_norm', 'ffn1_w_gu', 'ffn1_w_down', 'mix_norm', 'w_in', 'a_q_norm', 'a_k_norm', 'a_rel_bias', 'b_q_norm', 'b_k_norm', 'b_sinks', 'w_gate', 'w_proj_a', 'w_proj_b', 'w_out', 'ffn2_norm', 'ffn2_w_gu', 'ffn2_w_down', 'ple_norm', 'w_ple_gate', 'w_ple_proj', 'loss_target', 'm_ffn1_norm', 'm_ffn1_w_gu', 'm_ffn1_w_down', 'm_mix_norm', 'm_w_in', 'm_a_q_norm', 'm_a_k_norm', 'm_a_rel_bias', 'm_b_q_norm', 'm_b_k_norm', 'm_b_sinks', 'm_w_gate', 'm_w_proj_a', 'm_w_proj_b', 'm_w_out', 'm_ffn2_norm', 'm_ffn2_w_gu', 'm_ffn2_w_down', 'm_ple_norm', 'm_w_ple_gate', 'm_w_ple_proj', 'v_ffn1_norm', 'v_ffn1_w_gu', 'v_ffn1_w_down', 'v_mix_norm', 'v_w_in', 'v_a_q_norm', 'v_a_k_norm', 'v_a_rel_bias', 'v_b_q_norm', 'v_b_k_norm', 'v_b_sinks', 'v_w_gate', 'v_w_proj_a', 'v_w_proj_b', 'v_w_out', 'v_ffn2_norm', 'v_ffn2_w_gu', 'v_ffn2_w_down', 'v_ple_norm', 'v_w_ple_gate', 'v_w_ple_proj']
TWIN_OUTPUTS = ['loss', 'grad_x', 'grad_ffn1_norm', 'grad_ffn1_w_gu', 'grad_ffn1_w_down', 'grad_mix_norm', 'grad_w_in', 'grad_a_q_norm', 'grad_a_k_norm', 'grad_a_rel_bias', 'grad_b_q_norm', 'grad_b_k_norm', 'grad_b_sinks', 'grad_w_gate', 'grad_w_proj_a', 'grad_w_proj_b', 'grad_w_out', 'grad_ffn2_norm', 'grad_ffn2_w_gu', 'grad_ffn2_w_down', 'grad_ple_norm', 'grad_w_ple_gate', 'grad_w_ple_proj', 'delta_ffn1_norm', 'delta_ffn1_w_gu', 'delta_ffn1_w_down', 'delta_mix_norm', 'delta_w_in', 'delta_a_q_norm', 'delta_a_k_norm', 'delta_a_rel_bias', 'delta_b_q_norm', 'delta_b_k_norm', 'delta_b_sinks', 'delta_w_gate', 'delta_w_proj_a', 'delta_w_proj_b', 'delta_w_out', 'delta_ffn2_norm', 'delta_ffn2_w_gu', 'delta_ffn2_w_down', 'delta_ple_norm', 'delta_w_ple_gate', 'delta_w_ple_proj', 'new_m_ffn1_norm', 'new_m_ffn1_w_gu', 'new_m_ffn1_w_down', 'new_m_mix_norm', 'new_m_w_in', 'new_m_a_q_norm', 'new_m_a_k_norm', 'new_m_a_rel_bias', 'new_m_b_q_norm', 'new_m_b_k_norm', 'new_m_b_sinks', 'new_m_w_gate', 'new_m_w_proj_a', 'new_m_w_proj_b', 'new_m_w_out', 'new_m_ffn2_norm', 'new_m_ffn2_w_gu', 'new_m_ffn2_w_down', 'new_m_ple_norm', 'new_m_w_ple_gate', 'new_m_w_ple_proj', 'new_v_ffn1_norm', 'new_v_ffn1_w_gu', 'new_v_ffn1_w_down', 'new_v_mix_norm', 'new_v_w_in', 'new_v_a_q_norm', 'new_v_a_k_norm', 'new_v_a_rel_bias', 'new_v_b_q_norm', 'new_v_b_k_norm', 'new_v_b_sinks', 'new_v_w_gate', 'new_v_w_proj_a', 'new_v_w_proj_b', 'new_v_w_out', 'new_v_ffn2_norm', 'new_v_ffn2_w_gu', 'new_v_ffn2_w_down', 'new_v_ple_norm', 'new_v_w_ple_gate', 'new_v_w_ple_proj']
TWIN_LEAF_KINDS = {'loss': 'loss', 'grad_x': 'grad_x', 'grad_ffn1_norm': 'grad_w', 'grad_ffn1_w_gu': 'grad_w', 'grad_ffn1_w_down': 'grad_w', 'grad_mix_norm': 'grad_w', 'grad_w_in': 'grad_w', 'grad_a_q_norm': 'grad_w', 'grad_a_k_norm': 'grad_w', 'grad_a_rel_bias': 'grad_w', 'grad_b_q_norm': 'grad_w', 'grad_b_k_norm': 'grad_w', 'grad_b_sinks': 'grad_w', 'grad_w_gate': 'grad_w', 'grad_w_proj_a': 'grad_w', 'grad_w_proj_b': 'grad_w', 'grad_w_out': 'grad_w', 'grad_ffn2_norm': 'grad_w', 'grad_ffn2_w_gu': 'grad_w', 'grad_ffn2_w_down': 'grad_w', 'grad_ple_norm': 'grad_w', 'grad_w_ple_gate': 'grad_w', 'grad_w_ple_proj': 'grad_w', 'delta_ffn1_norm': 'delta_w', 'delta_ffn1_w_gu': 'delta_w', 'delta_ffn1_w_down': 'delta_w', 'delta_mix_norm': 'delta_w', 'delta_w_in': 'delta_w', 'delta_a_q_norm': 'delta_w', 'delta_a_k_norm': 'delta_w', 'delta_a_rel_bias': 'delta_w', 'delta_b_q_norm': 'delta_w', 'delta_b_k_norm': 'delta_w', 'delta_b_sinks': 'delta_w', 'delta_w_gate': 'delta_w', 'delta_w_proj_a': 'delta_w', 'delta_w_proj_b': 'delta_w', 'delta_w_out': 'delta_w', 'delta_ffn2_norm': 'delta_w', 'delta_ffn2_w_gu': 'delta_w', 'delta_ffn2_w_down': 'delta_w', 'delta_ple_norm': 'delta_w', 'delta_w_ple_gate': 'delta_w', 'delta_w_ple_proj': 'delta_w', 'new_m_ffn1_norm': 'new_m', 'new_m_ffn1_w_gu': 'new_m', 'new_m_ffn1_w_down': 'new_m', 'new_m_mix_norm': 'new_m', 'new_m_w_in': 'new_m', 'new_m_a_q_norm': 'new_m', 'new_m_a_k_norm': 'new_m', 'new_m_a_rel_bias': 'new_m', 'new_m_b_q_norm': 'new_m', 'new_m_b_k_norm': 'new_m', 'new_m_b_sinks': 'new_m', 'new_m_w_gate': 'new_m', 'new_m_w_proj_a': 'new_m', 'new_m_w_proj_b': 'new_m', 'new_m_w_out': 'new_m', 'new_m_ffn2_norm': 'new_m', 'new_m_ffn2_w_gu': 'new_m', 'new_m_ffn2_w_down': 'new_m', 'new_m_ple_norm': 'new_m', 'new_m_w_ple_gate': 'new_m', 'new_m_w_ple_proj': 'new_m', 'new_v_ffn1_norm': 'new_v', 'new_v_ffn1_w_gu': 'new_v', 'new_v_ffn1_w_down': 'new_v', 'new_v_mix_norm': 'new_v', 'new_v_w_in': 'new_v', 'new_v_a_q_norm': 'new_v', 'new_v_a_k_norm': 'new_v', 'new_v_a_rel_bias': 'new_v', 'new_v_b_q_norm': 'new_v', 'new_v_b_k_norm': 'new_v', 'new_v_b_sinks': 'new_v', 'new_v_w_gate': 'new_v', 'new_v_w_proj_a': 'new_v', 'new_v_w_proj_b': 'new_v', 'new_v_w_out': 'new_v', 'new_v_ffn2_norm': 'new_v', 'new_v_ffn2_w_gu': 'new_v', 'new_v_ffn2_w_down': 'new_v', 'new_v_ple_norm': 'new_v', 'new_v_w_ple_gate': 'new_v', 'new_v_w_ple_proj': 'new_v'}


def _forward(args):
    return _fwd_reference(*[args[k] for k in FWD_PARAMS])


def _output_shape():
    out = _jax.eval_shape(lambda: _forward(_fwd_setup_inputs(0)))
    return out.shape, out.dtype

N_MICROBATCH = 1
ADAM_LR = 0.001
ADAM_B1 = 0.9
ADAM_B2 = 0.999
ADAM_EPS = 1e-08
ADAM_WD = 0.01
ADAM_STEP = 10
PER_EXAMPLE_BATCH_AXIS = {'x': 0, 'p': 1, 'loss_target': 0}
SHARED_INPUTS = []
_WEIGHT_DTYPES = {'ffn1_norm': _jnp.float32, 'ffn1_w_gu': _jnp.float32, 'ffn1_w_down': _jnp.float32, 'mix_norm': _jnp.float32, 'w_in': _jnp.float32, 'a_q_norm': _jnp.float32, 'a_k_norm': _jnp.float32, 'a_rel_bias': _jnp.float32, 'b_q_norm': _jnp.float32, 'b_k_norm': _jnp.float32, 'b_sinks': _jnp.float32, 'w_gate': _jnp.float32, 'w_proj_a': _jnp.float32, 'w_proj_b': _jnp.float32, 'w_out': _jnp.float32, 'ffn2_norm': _jnp.float32, 'ffn2_w_gu': _jnp.float32, 'ffn2_w_down': _jnp.float32, 'ple_norm': _jnp.float32, 'w_ple_gate': _jnp.float32, 'w_ple_proj': _jnp.float32}
MOMENT_SCALE = {'ffn1_norm': 6.186479e+00, 'ffn1_w_gu': 6.837919e-02, 'ffn1_w_down': 1.157330e-01, 'mix_norm': 5.153050e-01, 'w_in': 5.912108e-02, 'a_q_norm': 4.687664e-01, 'a_k_norm': 4.693587e-01, 'a_rel_bias': 1.593324e-02, 'b_q_norm': 4.697955e+00, 'b_k_norm': 4.735112e+00, 'b_sinks': 8.503573e+00, 'w_gate': 1.373017e-02, 'w_proj_a': 2.939431e-02, 'w_proj_b': 5.579255e-02, 'w_out': 5.353880e-02, 'ffn2_norm': 6.198000e+00, 'ffn2_w_gu': 6.426119e-02, 'ffn2_w_down': 1.102989e-01, 'ple_norm': 9.486867e-01, 'w_ple_gate': 7.078540e-02, 'w_ple_proj': 4.228320e-01}


def _to_microbatches(a, axis):
    t = _jnp.moveaxis(a, axis, 0)
    t = t.reshape((N_MICROBATCH, t.shape[0] // N_MICROBATCH) + t.shape[1:])
    return _jnp.moveaxis(t, 1, axis + 1)


def setup_inputs(seed: int = 0) -> dict:
    inp = _fwd_setup_inputs(seed)
    key = _jax.random.fold_in(_jax.random.key(seed), 7919)
    shape, _ = _output_shape()
    out = dict(inp)
    out["loss_target"] = _jax.random.normal(_jax.random.fold_in(key, 0), shape, _jnp.float32)
    for i, name in enumerate(TWIN_WEIGHTS):
        w = inp[name].astype(_jnp.float32)
        if MOMENT_SCALE is None:
            s = _jnp.sqrt(_jnp.mean(_jnp.square(w)) + 1e-30)
        else:
            s = MOMENT_SCALE[name]
        km, kv = _jax.random.split(_jax.random.fold_in(key, i + 1))
        out[name] = w
        out["m_" + name] = s * _jax.random.normal(km, w.shape, _jnp.float32)
        out["v_" + name] = (s * s) * _jax.random.uniform(kv, w.shape, _jnp.float32, 0.5, 1.5)
    if N_MICROBATCH > 1:
        for name, axis in PER_EXAMPLE_BATCH_AXIS.items():
            out[name] = _to_microbatches(out[name], axis)
    return {'x': out['x'], 'p': out['p'], 'ffn1_norm': out['ffn1_norm'], 'ffn1_w_gu': out['ffn1_w_gu'], 'ffn1_w_down': out['ffn1_w_down'], 'mix_norm': out['mix_norm'], 'w_in': out['w_in'], 'a_q_norm': out['a_q_norm'], 'a_k_norm': out['a_k_norm'], 'a_rel_bias': out['a_rel_bias'], 'b_q_norm': out['b_q_norm'], 'b_k_norm': out['b_k_norm'], 'b_sinks': out['b_sinks'], 'w_gate': out['w_gate'], 'w_proj_a': out['w_proj_a'], 'w_proj_b': out['w_proj_b'], 'w_out': out['w_out'], 'ffn2_norm': out['ffn2_norm'], 'ffn2_w_gu': out['ffn2_w_gu'], 'ffn2_w_down': out['ffn2_w_down'], 'ple_norm': out['ple_norm'], 'w_ple_gate': out['w_ple_gate'], 'w_ple_proj': out['w_ple_proj'], 'loss_target': out['loss_target'], 'm_ffn1_norm': out['m_ffn1_norm'], 'm_ffn1_w_gu': out['m_ffn1_w_gu'], 'm_ffn1_w_down': out['m_ffn1_w_down'], 'm_mix_norm': out['m_mix_norm'], 'm_w_in': out['m_w_in'], 'm_a_q_norm': out['m_a_q_norm'], 'm_a_k_norm': out['m_a_k_norm'], 'm_a_rel_bias': out['m_a_rel_bias'], 'm_b_q_norm': out['m_b_q_norm'], 'm_b_k_norm': out['m_b_k_norm'], 'm_b_sinks': out['m_b_sinks'], 'm_w_gate': out['m_w_gate'], 'm_w_proj_a': out['m_w_proj_a'], 'm_w_proj_b': out['m_w_proj_b'], 'm_w_out': out['m_w_out'], 'm_ffn2_norm': out['m_ffn2_norm'], 'm_ffn2_w_gu': out['m_ffn2_w_gu'], 'm_ffn2_w_down': out['m_ffn2_w_down'], 'm_ple_norm': out['m_ple_norm'], 'm_w_ple_gate': out['m_w_ple_gate'], 'm_w_ple_proj': out['m_w_ple_proj'], 'v_ffn1_norm': out['v_ffn1_norm'], 'v_ffn1_w_gu': out['v_ffn1_w_gu'], 'v_ffn1_w_down': out['v_ffn1_w_down'], 'v_mix_norm': out['v_mix_norm'], 'v_w_in': out['v_w_in'], 'v_a_q_norm': out['v_a_q_norm'], 'v_a_k_norm': out['v_a_k_norm'], 'v_a_rel_bias': out['v_a_rel_bias'], 'v_b_q_norm': out['v_b_q_norm'], 'v_b_k_norm': out['v_b_k_norm'], 'v_b_sinks': out['v_b_sinks'], 'v_w_gate': out['v_w_gate'], 'v_w_proj_a': out['v_w_proj_a'], 'v_w_proj_b': out['v_w_proj_b'], 'v_w_out': out['v_w_out'], 'v_ffn2_norm': out['v_ffn2_norm'], 'v_ffn2_w_gu': out['v_ffn2_w_gu'], 'v_ffn2_w_down': out['v_ffn2_w_down'], 'v_ple_norm': out['v_ple_norm'], 'v_w_ple_gate': out['v_w_ple_gate'], 'v_w_ple_proj': out['v_w_ple_proj']}


def _loss(weights, diff, rest, loss_target):
    with _jax.named_scope("forward"):
        args = {**rest, TWIN_DIFF_INPUT: diff, **{k: w.astype(_WEIGHT_DTYPES[k]) for k, w in weights.items()}}
        y = _forward(args)
    with _jax.named_scope("loss_head"):
        err = _jnp.square(y.astype(_jnp.float32) - loss_target)
        return 0.5 * _jnp.sum(_jnp.mean(err, axis=-1)) if err.ndim else 0.5 * err


def _adamw(w, g, m, v):
    m = ADAM_B1 * m + (1.0 - ADAM_B1) * g
    v = ADAM_B2 * v + (1.0 - ADAM_B2) * _jnp.square(g)
    m_hat = m / (1.0 - ADAM_B1 ** ADAM_STEP)
    v_hat = v / (1.0 - ADAM_B2 ** ADAM_STEP)
    delta = -ADAM_LR * (m_hat / (_jnp.sqrt(v_hat) + ADAM_EPS) + ADAM_WD * w)
    return delta, m, v


def reference(x, p, ffn1_norm, ffn1_w_gu, ffn1_w_down, mix_norm, w_in, a_q_norm, a_k_norm, a_rel_bias, b_q_norm, b_k_norm, b_sinks, w_gate, w_proj_a, w_proj_b, w_out, ffn2_norm, ffn2_w_gu, ffn2_w_down, ple_norm, w_ple_gate, w_ple_proj, loss_target, m_ffn1_norm, m_ffn1_w_gu, m_ffn1_w_down, m_mix_norm, m_w_in, m_a_q_norm, m_a_k_norm, m_a_rel_bias, m_b_q_norm, m_b_k_norm, m_b_sinks, m_w_gate, m_w_proj_a, m_w_proj_b, m_w_out, m_ffn2_norm, m_ffn2_w_gu, m_ffn2_w_down, m_ple_norm, m_w_ple_gate, m_w_ple_proj, v_ffn1_norm, v_ffn1_w_gu, v_ffn1_w_down, v_mix_norm, v_w_in, v_a_q_norm, v_a_k_norm, v_a_rel_bias, v_b_q_norm, v_b_k_norm, v_b_sinks, v_w_gate, v_w_proj_a, v_w_proj_b, v_w_out, v_ffn2_norm, v_ffn2_w_gu, v_ffn2_w_down, v_ple_norm, v_w_ple_gate, v_w_ple_proj):
    given = dict(x=x, p=p, ffn1_norm=ffn1_norm, ffn1_w_gu=ffn1_w_gu, ffn1_w_down=ffn1_w_down, mix_norm=mix_norm, w_in=w_in, a_q_norm=a_q_norm, a_k_norm=a_k_norm, a_rel_bias=a_rel_bias, b_q_norm=b_q_norm, b_k_norm=b_k_norm, b_sinks=b_sinks, w_gate=w_gate, w_proj_a=w_proj_a, w_proj_b=w_proj_b, w_out=w_out, ffn2_norm=ffn2_norm, ffn2_w_gu=ffn2_w_gu, ffn2_w_down=ffn2_w_down, ple_norm=ple_norm, w_ple_gate=w_ple_gate, w_ple_proj=w_ple_proj, loss_target=loss_target, m_ffn1_norm=m_ffn1_norm, m_ffn1_w_gu=m_ffn1_w_gu, m_ffn1_w_down=m_ffn1_w_down, m_mix_norm=m_mix_norm, m_w_in=m_w_in, m_a_q_norm=m_a_q_norm, m_a_k_norm=m_a_k_norm, m_a_rel_bias=m_a_rel_bias, m_b_q_norm=m_b_q_norm, m_b_k_norm=m_b_k_norm, m_b_sinks=m_b_sinks, m_w_gate=m_w_gate, m_w_proj_a=m_w_proj_a, m_w_proj_b=m_w_proj_b, m_w_out=m_w_out, m_ffn2_norm=m_ffn2_norm, m_ffn2_w_gu=m_ffn2_w_gu, m_ffn2_w_down=m_ffn2_w_down, m_ple_norm=m_ple_norm, m_w_ple_gate=m_w_ple_gate, m_w_ple_proj=m_w_ple_proj, v_ffn1_norm=v_ffn1_norm, v_ffn1_w_gu=v_ffn1_w_gu, v_ffn1_w_down=v_ffn1_w_down, v_mix_norm=v_mix_norm, v_w_in=v_w_in, v_a_q_norm=v_a_q_norm, v_a_k_norm=v_a_k_norm, v_a_rel_bias=v_a_rel_bias, v_b_q_norm=v_b_q_norm, v_b_k_norm=v_b_k_norm, v_b_sinks=v_b_sinks, v_w_gate=v_w_gate, v_w_proj_a=v_w_proj_a, v_w_proj_b=v_w_proj_b, v_w_out=v_w_out, v_ffn2_norm=v_ffn2_norm, v_ffn2_w_gu=v_ffn2_w_gu, v_ffn2_w_down=v_ffn2_w_down, v_ple_norm=v_ple_norm, v_w_ple_gate=v_w_ple_gate, v_w_ple_proj=v_w_ple_proj)
    weights = {n: given[n] for n in TWIN_WEIGHTS}
    shared = {n: given[n] for n in SHARED_INPUTS}
    per_example = {n: given[n] for n in ['x', 'p']}
    grad_fn = _jax.value_and_grad(_loss, argnums=(0, 1))

    def one_microbatch(ex, loss_target):
        ex = dict(ex)
        diff = ex.pop(TWIN_DIFF_INPUT)
        return grad_fn(weights, diff, {**shared, **ex}, loss_target)

    if N_MICROBATCH == 1:
        loss, (grad_w, grad_x) = one_microbatch(per_example, given["loss_target"])
    else:
        def body(carry, xs):
            loss_sum, grad_sum = carry
            l_k, (gw_k, gx_k) = one_microbatch(xs[0], xs[1])
            with _jax.named_scope("update"):
                return (loss_sum + l_k, _jax.tree.map(_jnp.add, grad_sum, gw_k)), gx_k

        init = (_jnp.zeros((), _jnp.float32), _jax.tree.map(_jnp.zeros_like, weights))
        (loss, grad_w), grad_x = _jax.lax.scan(body, init, (per_example, given["loss_target"]))
    with _jax.named_scope("update"):
        delta_w, new_m, new_v = {}, {}, {}
        for n in TWIN_WEIGHTS:
            delta_w[n], new_m[n], new_v[n] = _adamw(weights[n], grad_w[n], given["m_" + n], given["v_" + n])
    return (loss, grad_x, *[grad_w[n] for n in TWIN_WEIGHTS], *[delta_w[n] for n in TWIN_WEIGHTS],
            *[new_m[n] for n in TWIN_WEIGHTS], *[new_v[n] for n in TWIN_WEIGHTS])
```

```python
import functools

import jax
import jax.numpy as jnp
import numpy as np
from jax import lax
from jax.experimental import pallas as pl
from jax.experimental.pallas import tpu as pltpu

F32 = jnp.float32
BF16 = jnp.bfloat16

CHUNK = 64
HEAD_DIM = 64
A_HEADS = 8
A_PREV = 8
A_MAX_REL = 128
B_Q_HEADS = 8
B_KV_HEADS = 2
B_PREV = 2
A_WIDTH = A_HEADS * HEAD_DIM
B_Q_WIDTH = B_Q_HEADS * HEAD_DIM
B_KV_WIDTH = B_KV_HEADS * HEAD_DIM
IN_COLS = 3 * A_WIDTH + B_Q_WIDTH + 2 * B_KV_WIDTH
EPS = 1e-6
NEG_INF = -1e30
ADAM_LR = 0.001
ADAM_B1 = 0.9
ADAM_B2 = 0.999
ADAM_EPS = 1e-08
ADAM_WD = 0.01
ADAM_STEP = 10

N_DEV = 8
LANES = 128
QTILE = 2 * CHUNK
VMEM_LIMIT = 56 * 1024 * 1024

MESH_ID = pl.DeviceIdType.MESH
ANY = pl.BlockSpec(memory_space=pl.ANY)


def _dot(a, b):
    return jnp.dot(a, b, preferred_element_type=F32)


def _dot_nt(a, b):
    return lax.dot_general(a, b, (((1,), (1,)), ((), ())), preferred_element_type=F32)


def _dot_tn(a, b):
    return lax.dot_general(a, b, (((0,), (0,)), ((), ())), preferred_element_type=F32)


def _params(sem=None, vmem=VMEM_LIMIT):
    return pltpu.CompilerParams(dimension_semantics=sem, vmem_limit_bytes=vmem)


def _row_tile(t, want):
    while t % want:
        want //= 2
    return want


def _place():
    return lax.axis_index("x"), lax.axis_index("y"), lax.axis_index("c")


def _all_gather(shards, name):
    n = len(shards)

    def body(*refs):
        ins, outs = refs[:n], refs[n:2 * n]
        send_sems, recv_sems, local_sems = refs[2 * n:]
        x, y, c = _place()
        me, sib = (x, y, c), (x, y, 1 - c)
        chips = [(1 - x, y), (x, 1 - y), (1 - x, 1 - y)]

        def copy(w, k, block, to, src=None):
            px, py, pc = block
            dst = outs[w].at[4 * px + 2 * py + pc]
            return pltpu.make_async_remote_copy(
                src_ref=dst if src is None else src, dst_ref=dst,
                send_sem=send_sems.at[w * 7 + k], recv_sem=recv_sems.at[w * 7 + k],
                device_id=to, device_id_type=MESH_ID)

        mine = [pltpu.make_async_copy(ins[w], outs[w].at[4 * x + 2 * y + c], local_sems.at[w]) for w in range(n)]
        for cp in mine:
            cp.start()
        first = []
        for w in range(n):
            first.append(copy(w, 0, me, sib, src=ins[w]))
            first += [copy(w, 1 + j, me, (*chip, c), src=ins[w]) for j, chip in enumerate(chips)]
        for cp in first:
            cp.start()
        passed = []
        for j, chip in enumerate(chips):
            for w in range(n):
                copy(w, 1 + j, (*chip, c), me).wait_recv()
                fwd = copy(w, 4 + j, (*chip, c), sib)
                fwd.start()
                passed.append(fwd)
        for w in range(n):
            copy(w, 0, sib, me).wait_recv()
        for j, chip in enumerate(chips):
            for w in range(n):
                copy(w, 4 + j, (*chip, 1 - c), me).wait_recv()
        for cp in first + passed:
            cp.wait_send()
        for cp in mine:
            cp.wait()

    return pl.pallas_call(
        body, name=name,
        out_shape=[jax.ShapeDtypeStruct((N_DEV,) + s.shape, s.dtype) for s in shards],
        in_specs=[ANY] * n, out_specs=[ANY] * n,
        scratch_shapes=[pltpu.SemaphoreType.DMA((7 * n,)), pltpu.SemaphoreType.DMA((7 * n,)),
                        pltpu.SemaphoreType.DMA((n,))],
    )(*shards)


def _sibling_exchange(grads, name):
    n = len(grads)

    def body(*refs):
        ins, outs = refs[:n], refs[n:2 * n]
        send_sems, recv_sems = refs[2 * n:]
        x, y, c = _place()
        cps = [pltpu.make_async_remote_copy(
            src_ref=ins[w].at[:, pl.ds(1 - c, 1)], dst_ref=outs[w],
            send_sem=send_sems.at[w], recv_sem=recv_sems.at[w],
            device_id=(x, y, 1 - c), device_id_type=MESH_ID) for w in range(n)]
        for cp in cps:
            cp.start()
        for cp in cps:
            cp.wait()

    return pl.pallas_call(
        body, name=name,
        out_shape=[jax.ShapeDtypeStruct((4, 1) + g.shape[2:], g.dtype) for g in grads],
        in_specs=[ANY] * n, out_specs=[ANY] * n,
        scratch_shapes=[pltpu.SemaphoreType.DMA((n,)), pltpu.SemaphoreType.DMA((n,))],
    )(*grads)


def _chip_exchange(parts, name):
    n = len(parts)

    def body(*refs):
        ins, outs = refs[:n], refs[n:2 * n]
        send_sems, recv_sems = refs[2 * n:]
        x, y, c = _place()
        chips = [(1 - x, y), (x, 1 - y), (1 - x, 1 - y)]
        cps = []
        for w in range(n):
            for k, (qx, qy) in enumerate(chips):
                cps.append(pltpu.make_async_remote_copy(
                    src_ref=ins[w].at[2 * qx + qy], dst_ref=outs[w].at[k],
                    send_sem=send_sems.at[3 * w + k], recv_sem=recv_sems.at[3 * w + k],
                    device_id=(qx, qy, c), device_id_type=MESH_ID))
        for cp in cps:
            cp.start()
        for cp in cps:
            cp.wait()

    return pl.pallas_call(
        body, name=name,
        out_shape=[jax.ShapeDtypeStruct((3,) + p.shape[1:], p.dtype) for p in parts],
        in_specs=[ANY] * n, out_specs=[ANY] * n,
        scratch_shapes=[pltpu.SemaphoreType.DMA((3 * n,)), pltpu.SemaphoreType.DMA((3 * n,))],
    )(*parts)


def _adam(w, g, m, v):
    m2 = ADAM_B1 * m + (1.0 - ADAM_B1) * g
    v2 = ADAM_B2 * v + (1.0 - ADAM_B2) * (g * g)
    m_hat = m2 / (1.0 - ADAM_B1 ** ADAM_STEP)
    v_hat = v2 / (1.0 - ADAM_B2 ** ADAM_STEP)
    delta = -ADAM_LR * (m_hat / (jnp.sqrt(v_hat) + ADAM_EPS) + ADAM_WD * w)
    return delta, m2, v2


def _small_allreduce_adam(part, w, m, v, name):
    rows = part.shape[0]

    def body(p_ref, w_ref, m_ref, v_ref, g_ref, d_ref, mo_ref, vo_ref, buf, send_sems, recv_sems):
        x, y, c = _place()
        buf[0] = p_ref[...]
        cps = []
        for k in range(1, N_DEV):
            kx, ky, kc = (k >> 2) & 1, (k >> 1) & 1, k & 1
            peer = (x ^ kx, y ^ ky, c ^ kc)
            cps.append(pltpu.make_async_remote_copy(
                src_ref=p_ref, dst_ref=buf.at[k], send_sem=send_sems.at[k - 1], recv_sem=recv_sems.at[k - 1],
                device_id=peer, device_id_type=MESH_ID))
        for cp in cps:
            cp.start()
        for cp in cps:
            cp.wait()
        me = 4 * x + 2 * y + c
        total = buf[me]
        for d in range(1, N_DEV):
            total = total + buf[d ^ me]
        g_ref[...] = total
        delta, m2, v2 = _adam(w_ref[...], total, m_ref[...], v_ref[...])
        d_ref[...] = delta
        mo_ref[...] = m2
        vo_ref[...] = v2

    vm = pl.BlockSpec(memory_space=pltpu.VMEM)
    return pl.pallas_call(
        body, name=name,
        out_shape=[jax.ShapeDtypeStruct(part.shape, F32)] * 4,
        in_specs=[vm] * 4, out_specs=[vm] * 4,
        scratch_shapes=[pltpu.VMEM((N_DEV, rows, LANES), F32),
                        pltpu.SemaphoreType.DMA((N_DEV - 1,)), pltpu.SemaphoreType.DMA((N_DEV - 1,))],
    )(part, w, m, v)


def _pair_sum(g4, r1, core, name):
    _, _, r, c = g4.shape
    tr = _row_tile(r, 256) if r % 8 == 0 and r > 512 else r

    def body(core_ref, g_ref, r_ref, o_ref):
        o_ref[...] = (g_ref[...] + r_ref[...]).astype(BF16)

    return pl.pallas_call(
        body, name=name,
        out_shape=jax.ShapeDtypeStruct((4, r, c), BF16),
        grid_spec=pltpu.PrefetchScalarGridSpec(
            num_scalar_prefetch=1, grid=(4, r // tr),
            in_specs=[pl.BlockSpec((None, None, tr, c), lambda q, i, s: (q, s[0], i, 0)),
                      pl.BlockSpec((None, None, tr, c), lambda q, i, s: (q, 0, i, 0))],
            out_specs=pl.BlockSpec((None, tr, c), lambda q, i, s: (q, i, 0))),
        compiler_params=_params(("arbitrary", "arbitrary")),
    )(core, g4, r1)


def _final_adam(g4, r1, r2, w, m, v, place, name):
    _, _, r, c = g4.shape
    tr = _row_tile(r, 256) if r % 8 == 0 and r > 512 else r

    def body(place_ref, g_ref, r1_ref, r2_ref, w_ref, m_ref, v_ref, go_ref, d_ref, mo_ref, vo_ref):
        g = g_ref[...] + r1_ref[...]
        for k in range(3):
            g = g + r2_ref[k].astype(F32)
        go_ref[...] = g
        delta, m2, v2 = _adam(w_ref[...], g, m_ref[...], v_ref[...])
        d_ref[...] = delta
        mo_ref[...] = m2
        vo_ref[...] = v2

    plain = pl.BlockSpec((tr, c), lambda i, s: (i, 0))
    return pl.pallas_call(
        body, name=name,
        out_shape=[jax.ShapeDtypeStruct((r, c), F32)] * 4,
        grid_spec=pltpu.PrefetchScalarGridSpec(
            num_scalar_prefetch=1, grid=(r // tr,),
            in_specs=[pl.BlockSpec((None, None, tr, c), lambda i, s: (s[0], s[1], i, 0)),
                      pl.BlockSpec((None, None, tr, c), lambda i, s: (s[0], 0, i, 0)),
                      pl.BlockSpec((3, tr, c), lambda i, s: (0, i, 0)),
                      plain, plain, plain],
            out_specs=[plain] * 4),
        compiler_params=_params(("arbitrary",)),
    )(place, g4, r1, r2, w, m, v)


def _rms(x, gain):
    r = lax.rsqrt(jnp.mean(x * x, axis=-1, keepdims=True) + EPS)
    xh = x * r
    return xh * gain, xh, r


def _rms_bwd(xh, r, gain, dy):
    gdy = gain * dy
    dx = r * (gdy - xh * jnp.mean(xh * gdy, axis=-1, keepdims=True))
    return dx, jnp.sum(dy * xh, axis=0, keepdims=True)


def _load_weights(pairs, sems):
    cps = [pltpu.make_async_copy(src, dst, sems.at[i]) for i, (src, dst) in enumerate(pairs)]
    for cp in cps:
        cp.start()
    for cp in cps:
        cp.wait()


def _ffn_fwd(h, gain, wgu, wd, name):
    t, d = h.shape
    nb, _, nf = wgu.shape
    nh = nb // 2
    tm = _row_tile(t, 256)

    def body(h_ref, g_ref, wgu_hbm, wd_hbm, out_ref, gu_ref, wgu_v, wd_v, sems):
        @pl.when(pl.program_id(0) == 0)
        def _():
            _load_weights([(wgu_hbm, wgu_v), (wd_hbm, wd_v)], sems)

        x = h_ref[...]
        n, _, _ = _rms(x, g_ref[...])
        nbf = n.astype(BF16)
        acc = jnp.zeros((tm, d), F32)
        for j in range(nh):
            g = _dot(nbf, wgu_v[j])
            u = _dot(nbf, wgu_v[j + nh])
            gu_ref[j] = g.astype(BF16)
            gu_ref[j + nh] = u.astype(BF16)
            a = (g * jax.nn.sigmoid(g)) * u
            acc = acc + _dot(a.astype(BF16), wd_v[j])
        out_ref[...] = x + 0.5 * acc

    return pl.pallas_call(
        body, name=name, grid=(t // tm,),
        out_shape=[jax.ShapeDtypeStruct((t, d), F32), jax.ShapeDtypeStruct((nb, t, nf), BF16)],
        in_specs=[pl.BlockSpec((tm, d), lambda i: (i, 0)), pl.BlockSpec((1, d), lambda i: (0, 0)), ANY, ANY],
        out_specs=[pl.BlockSpec((tm, d), lambda i: (i, 0)), pl.BlockSpec((nb, tm, nf), lambda i: (0, i, 0))],
        scratch_shapes=[pltpu.VMEM(wgu.shape, BF16), pltpu.VMEM(wd.shape, BF16), pltpu.SemaphoreType.DMA((2,))],
        compiler_params=_params(("arbitrary",)),
    )(h, gain, wgu, wd)


def _ffn_bwd(dh, h, gain, gu, wgu, wd, name):
    t, d = h.shape
    nb, _, nf = wgu.shape
    nh = nb // 2
    tm = _row_tile(t, 256)

    def body(dh_ref, h_ref, g_ref, gu_ref, wgu_hbm, wd_hbm, dhp_ref, dgu_ref, a_ref, n_ref, dgain_ref,
             wgu_v, wd_v, sems):
        @pl.when(pl.program_id(0) == 0)
        def _():
            _load_weights([(wgu_hbm, wgu_v), (wd_hbm, wd_v)], sems)
            dgain_ref[...] = jnp.zeros_like(dgain_ref)

        x = h_ref[...]
        gain_v = g_ref[...]
        n, xh, r = _rms(x, gain_v)
        n_ref[...] = n.astype(BF16)
        dh_v = dh_ref[...]
        dfb = (0.5 * dh_v).astype(BF16)
        dn = jnp.zeros((tm, d), F32)
        for j in range(nh):
            da = _dot_nt(dfb, wd_v[j])
            g = gu_ref[j].astype(F32)
            u = gu_ref[j + nh].astype(F32)
            sg = jax.nn.sigmoid(g)
            si = g * sg
            dg = (da * u * (sg * (1.0 + g * (1.0 - sg)))).astype(BF16)
            du = (da * si).astype(BF16)
            a_ref[j] = (si * u).astype(BF16)
            dgu_ref[j] = dg
            dgu_ref[j + nh] = du
            dn = dn + _dot_nt(dg, wgu_v[j]) + _dot_nt(du, wgu_v[j + nh])
        dx, dgain = _rms_bwd(xh, r, gain_v, dn)
        dhp_ref[...] = dh_v + dx
        dgain_ref[...] += dgain

    row = pl.BlockSpec((tm, d), lambda i: (i, 0))
    vec = pl.BlockSpec((1, d), lambda i: (0, 0))
    return pl.pallas_call(
        body, name=name, grid=(t // tm,),
        out_shape=[jax.ShapeDtypeStruct((t, d), F32), jax.ShapeDtypeStruct((nb, t, nf), BF16),
                   jax.ShapeDtypeStruct((nh, t, nf), BF16), jax.ShapeDtypeStruct((t, d), BF16),
                   jax.ShapeDtypeStruct((1, d), F32)],
        in_specs=[row, row, vec, pl.BlockSpec((nb, tm, nf), lambda i: (0, i, 0)), ANY, ANY],
        out_specs=[row, pl.BlockSpec((nb, tm, nf), lambda i: (0, i, 0)),
                   pl.BlockSpec((nh, tm, nf), lambda i: (0, i, 0)), row, vec],
        scratch_shapes=[pltpu.VMEM(wgu.shape, BF16), pltpu.VMEM(wd.shape, BF16), pltpu.SemaphoreType.DMA((2,))],
        compiler_params=_params(("arbitrary",)),
    )(dh, h, gain, gu, wgu, wd)


def _dw(xa, dy, nb, n, name, scale=1.0):
    t, k = xa.shape[-2:]
    tt = _row_tile(t, 512)
    steps = t // tt
    if xa.ndim == 3:
        x_spec = pl.BlockSpec((None, tt, k), lambda j, i: (j, i, 0))
    else:
        x_spec = pl.BlockSpec((tt, k), lambda j, i: (i, 0))
    if dy.ndim == 3:
        dy_spec = pl.BlockSpec((None, tt, n), lambda j, i: (j, i, 0))
    elif dy.shape[1] == n:
        dy_spec = pl.BlockSpec((tt, n), lambda j, i: (i, 0))
    else:
        dy_spec = pl.BlockSpec((tt, n), lambda j, i: (i, j))

    def body(x_ref, dy_ref, o_ref):
        @pl.when(pl.program_id(1) == 0)
        def _():
            o_ref[...] = jnp.zeros_like(o_ref)

        o_ref[...] += _dot_tn(x_ref[...].astype(BF16), dy_ref[...].astype(BF16))

        if scale != 1.0:
            @pl.when(pl.program_id(1) == steps - 1)
            def _():
                o_ref[...] = o_ref[...] * scale

    return pl.pallas_call(
        body, name=name, grid=(nb, steps),
        out_shape=jax.ShapeDtypeStruct((nb, k, n), F32),
        in_specs=[x_spec, dy_spec],
        out_specs=pl.BlockSpec((None, k, n), lambda j, i: (j, 0, 0)),
        compiler_params=_params(("arbitrary", "arbitrary")),
    )(xa, dy)


def _proj_fwd(h, gain, win, wgate, name):
    t, d = h.shape
    tm = _row_tile(t, 256)
    nq, ng = win.shape[1], wgate.shape[1]

    def body(h_ref, g_ref, win_ref, wg_ref, un_ref, qkv_ref, gate_ref):
        n, _, _ = _rms(h_ref[...], g_ref[...])
        nbf = n.astype(BF16)
        un_ref[...] = nbf
        qkv_ref[...] = _dot(nbf, win_ref[...])
        gate_ref[...] = jax.nn.sigmoid(_dot(nbf, wg_ref[...]))

    full = lambda a: pl.BlockSpec(a.shape, lambda i: (0,) * a.ndim)
    return pl.pallas_call(
        body, name=name, grid=(t // tm,),
        out_shape=[jax.ShapeDtypeStruct((t, d), BF16), jax.ShapeDtypeStruct((t, nq), F32),
                   jax.ShapeDtypeStruct((t, ng), F32)],
        in_specs=[pl.BlockSpec((tm, d), lambda i: (i, 0)), full(gain), full(win), full(wgate)],
        out_specs=[pl.BlockSpec((tm, d), lambda i: (i, 0)), pl.BlockSpec((tm, nq), lambda i: (i, 0)),
                   pl.BlockSpec((tm, ng), lambda i: (i, 0))],
        compiler_params=_params(("arbitrary",)),
    )(h, gain, win, wgate)


def _proj_bwd(dh, h, gain, dzg, dqkv, win, wgate, name):
    t, d = h.shape
    tm = _row_tile(t, 256)
    nq, ng = win.shape[1], wgate.shape[1]

    def body(dh_ref, h_ref, g_ref, dzg_ref, dqkv_ref, win_ref, wg_ref, dhp_ref, dgain_ref):
        @pl.when(pl.program_id(0) == 0)
        def _():
            dgain_ref[...] = jnp.zeros_like(dgain_ref)

        gain_v = g_ref[...]
        _, xh, r = _rms(h_ref[...], gain_v)
        dun = _dot_nt(dzg_ref[...], wg_ref[...]) + _dot_nt(dqkv_ref[...].astype(BF16), win_ref[...])
        dx, dgain = _rms_bwd(xh, r, gain_v, dun)
        dhp_ref[...] = dh_ref[...] + dx
        dgain_ref[...] += dgain

    full = lambda a: pl.BlockSpec(a.shape, lambda i: (0,) * a.ndim)
    row = pl.BlockSpec((tm, d), lambda i: (i, 0))
    return pl.pallas_call(
        body, name=name, grid=(t // tm,),
        out_shape=[jax.ShapeDtypeStruct((t, d), F32), jax.ShapeDtypeStruct((1, d), F32)],
        in_specs=[row, row, full(gain), pl.BlockSpec((tm, ng), lambda i: (i, 0)),
                  pl.BlockSpec((tm, nq), lambda i: (i, 0)), full(win), full(wgate)],
        out_specs=[row, pl.BlockSpec((1, d), lambda i: (0, 0))],
        compiler_params=_params(("arbitrary",)),
    )(dh, h, gain, dzg, dqkv, win, wgate)


def _merge_fwd(h, ya, yb, gate, wpa, wpb, wout, name):
    t, d = h.shape
    tm = _row_tile(t, 256)

    def body(h_ref, ya_ref, yb_ref, ga_ref, gb_ref, wpa_ref, wpb_ref, wout_ref, out_ref, mg_ref, pa_ref, pb_ref):
        pa = _dot(ya_ref[...].astype(BF16), wpa_ref[...])
        pb = _dot(yb_ref[...].astype(BF16), wpb_ref[...])
        merged = (ga_ref[...] * pa + gb_ref[...] * pb).astype(BF16)
        pa_ref[...] = pa.astype(BF16)
        pb_ref[...] = pb.astype(BF16)
        mg_ref[...] = merged
        out_ref[...] = h_ref[...] + _dot(merged, wout_ref[...])

    full = lambda a: pl.BlockSpec(a.shape, lambda i: (0,) * a.ndim)
    row = pl.BlockSpec((tm, d), lambda i: (i, 0))
    yrow = pl.BlockSpec((tm, ya.shape[1]), lambda i: (i, 0))
    return pl.pallas_call(
        body, name=name, grid=(t // tm,),
        out_shape=[jax.ShapeDtypeStruct((t, d), F32)] + [jax.ShapeDtypeStruct((t, d), BF16)] * 3,
        in_specs=[row, yrow, yrow, pl.BlockSpec((tm, d), lambda i: (i, 0)), pl.BlockSpec((tm, d), lambda i: (i, 1)),
                  full(wpa), full(wpb), full(wout)],
        out_specs=[row] * 4,
        compiler_params=_params(("arbitrary",)),
    )(h, ya, yb, gate, gate, wpa, wpb, wout)


def _merge_bwd(dh, pa, pb, gate, wpa, wpb, wout, name):
    t, d = dh.shape
    tm = _row_tile(t, 256)
    wy = wpa.shape[0]

    def body(dh_ref, pa_ref, pb_ref, ga_ref, gb_ref, wpa_ref, wpb_ref, wout_ref,
             dpa_ref, dpb_ref, dzg_ref, dya_ref, dyb_ref):
        dm = _dot_nt(dh_ref[...].astype(BF16), wout_ref[...])
        ga, gb = ga_ref[...], gb_ref[...]
        dpa = (dm * ga).astype(BF16)
        dpb = (dm * gb).astype(BF16)
        dpa_ref[...] = dpa
        dpb_ref[...] = dpb
        dzg_ref[:, :d] = (dm * pa_ref[...].astype(F32) * ga * (1.0 - ga)).astype(BF16)
        dzg_ref[:, d:] = (dm * pb_ref[...].astype(F32) * gb * (1.0 - gb)).astype(BF16)
        dya_ref[...] = _dot_nt(dpa, wpa_ref[...])
        dyb_ref[...] = _dot_nt(dpb, wpb_ref[...])

    full = lambda a: pl.BlockSpec(a.shape, lambda i: (0,) * a.ndim)
    row = pl.BlockSpec((tm, d), lambda i: (i, 0))
    yrow = pl.BlockSpec((tm, wy), lambda i: (i, 0))
    return pl.pallas_call(
        body, name=name, grid=(t // tm,),
        out_shape=[jax.ShapeDtypeStruct((t, d), BF16), jax.ShapeDtypeStruct((t, d), BF16),
                   jax.ShapeDtypeStruct((t, 2 * d), BF16), jax.ShapeDtypeStruct((t, wy), F32),
                   jax.ShapeDtypeStruct((t, wy), F32)],
        in_specs=[row, row, row, pl.BlockSpec((tm, d), lambda i: (i, 0)), pl.BlockSpec((tm, d), lambda i: (i, 1)),
                  full(wpa), full(wpb), full(wout)],
        out_specs=[row, row, pl.BlockSpec((tm, 2 * d), lambda i: (i, 0)), yrow, yrow],
        compiler_params=_params(("arbitrary",)),
    )(dh, pa, pb, gate, gate, wpa, wpb, wout)


def _ple_loss(h, gain, p, target, wpg, wpe, name):
    t, d = h.shape
    tm = _row_tile(t, 256)
    pd = p.shape[1]

    def body(h_ref, g_ref, p_ref, t_ref, wpg_ref, wpe_ref, dh_ref, dz_ref, dpp_ref, n_ref, dgain_ref, loss_ref):
        @pl.when(pl.program_id(0) == 0)
        def _():
            dgain_ref[...] = jnp.zeros_like(dgain_ref)
            loss_ref[...] = jnp.zeros_like(loss_ref)

        x = h_ref[...]
        gain_v = g_ref[...]
        n, xh, r = _rms(x, gain_v)
        nbf = n.astype(BF16)
        n_ref[...] = nbf
        pg = jax.nn.sigmoid(_dot(nbf, wpg_ref[...]))
        pp = _dot(p_ref[...].astype(BF16), wpe_ref[...])
        err = (x + pg * pp) - t_ref[...]
        loss_ref[...] += 0.5 * jnp.sum(jnp.mean(err * err, axis=-1, keepdims=True))
        dy = err * (1.0 / d)
        dpp_ref[...] = (dy * pg).astype(BF16)
        dz = (dy * pp * pg * (1.0 - pg)).astype(BF16)
        dz_ref[...] = dz
        dn = _dot_nt(dz, wpg_ref[...])
        dx, dgain = _rms_bwd(xh, r, gain_v, dn)
        dh_ref[...] = dy + dx
        dgain_ref[...] += dgain

    full = lambda a: pl.BlockSpec(a.shape, lambda i: (0,) * a.ndim)
    row = pl.BlockSpec((tm, d), lambda i: (i, 0))
    return pl.pallas_call(
        body, name=name, grid=(t // tm,),
        out_shape=[jax.ShapeDtypeStruct((t, d), F32), jax.ShapeDtypeStruct((t, d), BF16),
                   jax.ShapeDtypeStruct((t, d), BF16), jax.ShapeDtypeStruct((t, d), BF16),
                   jax.ShapeDtypeStruct((1, d), F32), jax.ShapeDtypeStruct((8, LANES), F32)],
        in_specs=[row, full(gain), pl.BlockSpec((tm, pd), lambda i: (i, 0)), row, full(wpg), full(wpe)],
        out_specs=[row, row, row, row, pl.BlockSpec((1, d), lambda i: (0, 0)),
                   pl.BlockSpec((8, LANES), lambda i: (0, 0))],
        compiler_params=_params(("arbitrary",)),
    )(h, gain, p, target, wpg, wpe)


def _head_masks():
    lane = lax.broadcasted_iota(jnp.int32, (1, LANES), 1)
    m0 = (lane < HEAD_DIM).astype(F32)
    return m0, 1.0 - m0


def _head_mean(v, m0, m1):
    s0 = jnp.sum(v * m0, axis=-1, keepdims=True)
    s1 = jnp.sum(v * m1, axis=-1, keepdims=True)
    return (s0 * m0 + s1 * m1) * (1.0 / HEAD_DIM)


def _head_norm(x, gain, m0, m1):
    r = lax.rsqrt(_head_mean(x * x, m0, m1) + EPS)
    xh = x * r
    return xh * gain, xh, r


def _head_norm_bwd(xh, r, gain, dy, m0, m1):
    gdy = gain * dy
    dx = r * (gdy - xh * _head_mean(xh * gdy, m0, m1))
    return dx, jnp.sum(dy * xh, axis=0, keepdims=True)


def _attn_prep(mode, pair, s_len, padk, q_ref, k_ref, v_ref, gq_ref, gk_ref, qs, k0, k1, v0, v1):
    m0, m1 = _head_masks()
    zpad = jnp.zeros((padk, LANES), BF16)
    for buf in (k0, k1, v0, v1):
        buf[pl.ds(0, padk), :] = zpad
    first_kv = (pair // 2) == 0
    rt = _row_tile(s_len, 256)

    def step(i, carry):
        rows = pl.ds(pl.multiple_of(i * rt, rt), rt)
        dst = pl.ds(pl.multiple_of(padk + i * rt, QTILE), rt)
        qn, _, _ = _head_norm(q_ref[rows, :], gq_ref[...], m0, m1)
        kn, _, _ = _head_norm(k_ref[rows, :], gk_ref[...], m0, m1)
        vv = v_ref[rows, :]
        qs[rows, :] = (qn * (HEAD_DIM ** -0.5)).astype(BF16)
        if mode == "B":
            kn = jnp.where(first_kv, kn, pltpu.roll(kn, HEAD_DIM, 1))
            vv = jnp.where(first_kv, vv, pltpu.roll(vv, HEAD_DIM, 1))
            ka, va = kn * m0, vv * m0
            kb, vb = pltpu.roll(ka, HEAD_DIM, 1), pltpu.roll(va, HEAD_DIM, 1)
        else:
            ka, kb, va, vb = kn * m0, kn * m1, vv * m0, vv * m1
        k0[dst, :] = ka.astype(BF16)
        k1[dst, :] = kb.astype(BF16)
        v0[dst, :] = va.astype(BF16)
        v1[dst, :] = vb.astype(BF16)
        return carry

    lax.fori_loop(0, s_len // rt, step, 0)


def _attn_probs(mode, q2, kb, bias, ok, sink):
    s = _dot_nt(q2, kb) + bias
    s = jnp.where(ok, s, NEG_INF)
    mx = jnp.max(s, axis=-1, keepdims=True)
    if mode == "B":
        mx = jnp.maximum(mx, sink)
    e = jnp.exp(s - mx)
    l = jnp.sum(e, axis=-1, keepdims=True)
    if mode == "B":
        l = l + jnp.exp(sink - mx)
    return e, mx, l


def _attn_cols(mode):
    if mode == "A":
        return (lambda b, p: (b, p)), (lambda b, p: (b, 4 + p)), (lambda b, p: (b, 8 + p))
    return (lambda b, p: (b, 12 + p)), (lambda b, p: (b, 16)), (lambda b, p: (b, 17))


def _attn_fwd(mode, qkv, gq, gk, bias, sinks, bl, s_len, name):
    bw = bias.shape[-1]
    padk = bw - QTILE
    nt = s_len // QTILE
    qmap, kmap, vmap = _attn_cols(mode)

    def body(q_ref, k_ref, v_ref, gq_ref, gk_ref, bias_ref, sink_ref, o_ref, qs, k0, k1, v0, v1):
        pair = pl.program_id(1)
        _attn_prep(mode, pair, s_len, padk, q_ref, k_ref, v_ref, gq_ref, gk_ref, qs, k0, k1, v0, v1)
        col = lax.broadcasted_iota(jnp.int32, (QTILE, bw), 1)

        def tile(m, carry):
            r0 = pl.multiple_of(m * QTILE, QTILE)
            q2 = qs[pl.ds(r0, QTILE), :]
            ok = col >= (padk - r0)
            acc = jnp.zeros((QTILE, LANES), F32)
            for hh, (kk, vv) in enumerate(((k0, v0), (k1, v1))):
                sink = sink_ref[2 * pair + hh]
                e, _, l = _attn_probs(mode, q2, kk[pl.ds(r0, bw), :], bias_ref[hh], ok, sink)
                acc = acc + _dot(e.astype(BF16), vv[pl.ds(r0, bw), :]) / l
            o_ref[pl.ds(r0, QTILE), :] = acc
            return carry

        lax.fori_loop(0, nt, tile, 0)

    blk = lambda f: pl.BlockSpec((s_len, LANES), f)
    vec = pl.BlockSpec((1, LANES), lambda b, p: (0, 0))
    return pl.pallas_call(
        body, name=name, grid=(bl, 4),
        out_shape=jax.ShapeDtypeStruct((bl * s_len, 4 * LANES), F32),
        in_specs=[blk(qmap), blk(kmap), blk(vmap), vec, vec,
                  pl.BlockSpec((2, QTILE, bw), lambda b, p: (p, 0, 0)),
                  pl.BlockSpec(memory_space=pltpu.SMEM)],
        out_specs=pl.BlockSpec((s_len, LANES), lambda b, p: (b, p)),
        scratch_shapes=[pltpu.VMEM((s_len, LANES), BF16)] + [pltpu.VMEM((s_len + padk, LANES), BF16)] * 4,
        compiler_params=_params(("arbitrary", "arbitrary")),
    )(qkv, qkv, qkv, gq, gk, bias, sinks)


def _attn_bwd(mode, qkv, gq, gk, bias, sinks, y, dy, bl, s_len, name):
    bw = bias.shape[-1]
    padk = bw - QTILE
    nt = s_len // QTILE
    qmap, kmap, vmap = _attn_cols(mode)
    t = bl * s_len
    kvw = 4 * LANES if mode == "A" else LANES

    def body(q_ref, k_ref, v_ref, gq_ref, gk_ref, bias_ref, sink_ref, y_ref, dy_ref,
             dq_ref, dk_ref, dv_ref, dgq_ref, dgk_ref, dbias_ref, dsink_ref,
             qs, k0, k1, v0, v1, dqs, dk0, dk1, dv0, dv1):
        pair = pl.program_id(1)
        m0, m1 = _head_masks()
        _attn_prep(mode, pair, s_len, padk, q_ref, k_ref, v_ref, gq_ref, gk_ref, qs, k0, k1, v0, v1)
        for buf in (dk0, dk1, dv0, dv1):
            buf[...] = jnp.zeros_like(buf)
        dbias_ref[...] = jnp.zeros_like(dbias_ref)
        col = lax.broadcasted_iota(jnp.int32, (QTILE, bw), 1)
        lane8 = lax.broadcasted_iota(jnp.int32, (8, LANES), 1)

        def tile(m, dsink):
            r0 = pl.multiple_of(m * QTILE, QTILE)
            rows = pl.ds(r0, QTILE)
            band = pl.ds(r0, bw)
            q2 = qs[rows, :]
            do2 = dy_ref[rows, :]
            dd = do2 * y_ref[rows, :]
            dob = do2.astype(BF16)
            ok = col >= (padk - r0)
            dq = jnp.zeros((QTILE, LANES), F32)
            for hh, (kk, vv, dkk, dvv, mh) in enumerate(((k0, v0, dk0, dv0, m0), (k1, v1, dk1, dv1, m1))):
                sink = sink_ref[2 * pair + hh]
                kb = kk[band, :]
                e, mx, l = _attn_probs(mode, q2, kb, bias_ref[hh], ok, sink)
                inv = 1.0 / l
                pn = e * inv
                delta = jnp.sum(dd * mh, axis=-1, keepdims=True)
                dp = _dot_nt(dob, vv[band, :])
                ds = pn * (dp - delta)
                if mode == "A":
                    dbias_ref[hh] += ds
                else:
                    ps = jnp.exp(sink - mx) * inv
                    dsink = dsink + jnp.where(lane8 == hh, -jnp.sum(ps * delta), 0.0)
                dsb = ds.astype(BF16)
                dvv[band, :] += _dot_tn(pn.astype(BF16), dob)
                dkk[band, :] += _dot_tn(dsb, q2)
                dq = dq + _dot(dsb, kb)
            dqs[rows, :] = dq * (HEAD_DIM ** -0.5)
            return dsink

        dsink = lax.fori_loop(0, nt, tile, jnp.zeros((8, LANES), F32))
        dsink_ref[...] = dsink

        first_kv = (pair // 2) == 0
        rt = _row_tile(s_len, 256)

        def post(i, carry):
            dgq, dgk = carry
            rows = pl.ds(pl.multiple_of(i * rt, rt), rt)
            src = pl.ds(pl.multiple_of(padk + i * rt, QTILE), rt)
            gq_v, gk_v = gq_ref[...], gk_ref[...]
            _, qh, qr = _head_norm(q_ref[rows, :], gq_v, m0, m1)
            _, kh, kr = _head_norm(k_ref[rows, :], gk_v, m0, m1)
            dq_raw, dgq_i = _head_norm_bwd(qh, qr, gq_v, dqs[rows, :], m0, m1)
            if mode == "A":
                dkn = dk0[src, :] * m0 + dk1[src, :] * m1
                dvn = dv0[src, :] * m0 + dv1[src, :] * m1
            else:
                dkn = dk0[src, :] * m0 + pltpu.roll(dk1[src, :] * m1, HEAD_DIM, 1)
                dvn = dv0[src, :] * m0 + pltpu.roll(dv1[src, :] * m1, HEAD_DIM, 1)
                dkn = jnp.where(first_kv, dkn, pltpu.roll(dkn, HEAD_DIM, 1))
                dvn = jnp.where(first_kv, dvn, pltpu.roll(dvn, HEAD_DIM, 1))
            dk_raw, dgk_i = _head_norm_bwd(kh, kr, gk_v, dkn, m0, m1)
            dq_ref[rows, :] = dq_raw
            if mode == "A":
                dk_ref[rows, :] = dk_raw
                dv_ref[rows, :] = dvn
            else:
                @pl.when(pair == 0)
                def _():
                    dk_ref[rows, :] = dk_raw
                    dv_ref[rows, :] = dvn

                @pl.when(pair != 0)
                def _():
                    dk_ref[rows, :] += dk_raw
                    dv_ref[rows, :] += dvn
            return dgq + dgq_i, dgk + dgk_i

        z = jnp.zeros((1, LANES), F32)
        dgq, dgk = lax.fori_loop(0, s_len // rt, post, (z, z))
        dgq_ref[...] = jnp.broadcast_to(dgq, (8, LANES))
        dgk_ref[...] = jnp.broadcast_to(dgk, (8, LANES))

    blk = lambda f: pl.BlockSpec((s_len, LANES), f)
    vec = pl.BlockSpec((1, LANES), lambda b, p: (0, 0))
    small = pl.BlockSpec((None, None, 8, LANES), lambda b, p: (b, p, 0, 0))
    kvmap = (lambda b, p: (b, p)) if mode == "A" else (lambda b, p: (b, 0))
    pad_f32 = pltpu.VMEM((s_len + padk, LANES), F32)
    pad_bf = pltpu.VMEM((s_len + padk, LANES), BF16)
    return pl.pallas_call(
        body, name=name, grid=(bl, 4),
        out_shape=[jax.ShapeDtypeStruct((t, 4 * LANES), F32), jax.ShapeDtypeStruct((t, kvw), F32),
                   jax.ShapeDtypeStruct((t, kvw), F32),
                   jax.ShapeDtypeStruct((bl, 4, 8, LANES), F32), jax.ShapeDtypeStruct((bl, 4, 8, LANES), F32),
                   jax.ShapeDtypeStruct((bl, 8, QTILE, bw), F32), jax.ShapeDtypeStruct((bl, 4, 8, LANES), F32)],
        in_specs=[blk(qmap), blk(kmap), blk(vmap), vec, vec,
                  pl.BlockSpec((2, QTILE, bw), lambda b, p: (p, 0, 0)),
                  pl.BlockSpec(memory_space=pltpu.SMEM),
                  blk(lambda b, p: (b, p)), blk(lambda b, p: (b, p))],
        out_specs=[blk(lambda b, p: (b, p)), blk(kvmap), blk(kvmap), small, small,
                   pl.BlockSpec((None, 2, QTILE, bw), lambda b, p: (b, p, 0, 0)), small],
        scratch_shapes=[pltpu.VMEM((s_len, LANES), BF16), pad_bf, pad_bf, pad_bf, pad_bf,
                        pltpu.VMEM((s_len, LANES), F32), pad_f32, pad_f32, pad_f32, pad_f32],
        compiler_params=_params(("arbitrary", "arbitrary")),
    )(qkv, qkv, qkv, gq, gk, bias, sinks, y, dy)


def _band_geometry(prev):
    bw = QTILE + prev * CHUNK
    i = np.arange(QTILE)[:, None]
    j = np.arange(bw)[None, :]
    dist = i + prev * CHUNK - j
    valid = (j // CHUNK >= i // CHUNK) & (j // CHUNK <= i // CHUNK + prev)
    return dist, valid


A_VAR0 = (A_PREV * CHUNK - A_MAX_REL) // LANES * LANES


def _rel_bias_expand(table, name):
    dist, valid = _band_geometry(A_PREV)
    bw = dist.shape[1]
    idx = jnp.asarray(np.clip(dist, -A_MAX_REL, A_MAX_REL) + A_MAX_REL, jnp.int32)
    valid_f = jnp.asarray(valid.astype(np.float32))
    lo = A_MAX_REL - (CHUNK - 1)
    nvar = bw - A_VAR0

    def body(tab_ref, idx_ref, valid_ref, o_ref):
        h = pl.program_id(0)
        iv = idx_ref[:, A_VAR0:]

        def step(r, acc):
            return jnp.where(iv == r, tab_ref[h, r], acc)

        var = lax.fori_loop(lo, 2 * A_MAX_REL, step, jnp.full((QTILE, nvar), tab_ref[h, 2 * A_MAX_REL], F32))
        ok = valid_ref[...] > 0.5
        o_ref[:, :A_VAR0] = jnp.where(ok[:, :A_VAR0], tab_ref[h, 2 * A_MAX_REL], NEG_INF)
        o_ref[:, A_VAR0:] = jnp.where(ok[:, A_VAR0:], var, NEG_INF)

    full2 = pl.BlockSpec((QTILE, bw), lambda h: (0, 0))
    return pl.pallas_call(
        body, name=name, grid=(A_HEADS,),
        out_shape=jax.ShapeDtypeStruct((A_HEADS, QTILE, bw), F32),
        in_specs=[pl.BlockSpec(memory_space=pltpu.SMEM), full2, full2],
        out_specs=pl.BlockSpec((None, QTILE, bw), lambda h: (h, 0, 0)),
        compiler_params=_params(("arbitrary",)),
    )(table, idx, valid_f)


def _rel_bias_grad(dbias, name):
    bl = dbias.shape[0]
    dist, _ = _band_geometry(A_PREV)
    bw = dist.shape[1]
    idx = jnp.asarray(np.clip(dist, -A_MAX_REL, A_MAX_REL) + A_MAX_REL, jnp.int32)
    lo = A_MAX_REL - (CHUNK - 1)
    tw = 3 * LANES

    def body(db_ref, idx_ref, o_ref):
        g = db_ref[0]
        for b in range(1, bl):
            g = g + db_ref[b]
        iv = idx_ref[:, A_VAR0:]
        gv = g[:, A_VAR0:]
        lane = lax.broadcasted_iota(jnp.int32, (8, tw), 1)
        top = jnp.sum(g[:, :A_VAR0]) + jnp.sum(jnp.where(iv == 2 * A_MAX_REL, gv, 0.0))

        def step(r, acc):
            return jnp.where(lane == r, jnp.sum(jnp.where(iv == r, gv, 0.0)), acc)

        acc = jnp.where(lane == 2 * A_MAX_REL, top, 0.0)
        o_ref[...] = lax.fori_loop(lo, 2 * A_MAX_REL, step, acc)

    return pl.pallas_call(
        body, name=name, grid=(A_HEADS,),
        out_shape=jax.ShapeDtypeStruct((A_HEADS, 8, tw), F32),
        in_specs=[pl.BlockSpec((bl, None, QTILE, bw), lambda h: (0, h, 0, 0)),
                  pl.BlockSpec((QTILE, bw), lambda h: (0, 0))],
        out_specs=pl.BlockSpec((None, 8, tw), lambda h: (h, 0, 0)),
        compiler_params=_params(("arbitrary",)),
    )(dbias, idx)


def _alibi_bias():
    dist, valid = _band_geometry(B_PREV)
    slopes = np.array([2.0 ** (-8.0 * (h + 1) / B_Q_HEADS) for h in range(B_Q_HEADS)], dtype=np.float32)
    bias = -slopes[:, None, None] * np.abs(dist).astype(np.float32)[None]
    return jnp.asarray(np.where(valid[None], bias, np.float32(NEG_INF)).astype(np.float32))


SMALL_NAMES = ("ffn1_norm", "mix_norm", "ffn2_norm", "ple_norm", "a_q_norm", "a_k_norm", "b_q_norm", "b_k_norm",
               "a_rel_bias", "b_sinks", "loss")


def _pack_small(vals):
    rows = []
    for nme in SMALL_NAMES:
        v = vals[nme].astype(F32)
        if nme == "a_rel_bias":
            v = jnp.pad(v.reshape(A_HEADS, -1), ((0, 0), (0, 3 * LANES - (2 * A_MAX_REL + 1))))
        v = v.reshape(-1)
        v = jnp.pad(v, (0, (-v.shape[0]) % LANES))
        rows.append(v.reshape(-1, LANES))
    out = jnp.concatenate(rows, axis=0)
    return jnp.pad(out, ((0, (-out.shape[0]) % 8), (0, 0)))


def _unpack_small(packed, shapes):
    out, r = {}, 0
    for nme in SMALL_NAMES:
        shp = shapes[nme]
        if nme == "a_rel_bias":
            nr = A_HEADS * 3
            out[nme] = packed[r:r + nr].reshape(A_HEADS, 3 * LANES)[:, :2 * A_MAX_REL + 1].reshape(shp)
        else:
            size = int(np.prod(shp)) if shp else 1
            nr = -(-size // LANES)
            out[nme] = packed[r:r + nr].reshape(-1)[:size].reshape(shp)
        r += nr
    return out


BIG_NAMES = ("ffn1_w_gu", "ffn1_w_down", "w_in", "w_gate", "w_proj_a", "w_proj_b", "w_out",
             "ffn2_w_gu", "ffn2_w_down", "w_ple_gate", "w_ple_proj")
ROW_SHARDED = ("ffn1_w_down", "ffn2_w_down", "w_out", "w_ple_gate")
WEIGHT_ORDER = ("ffn1_norm", "ffn1_w_gu", "ffn1_w_down", "mix_norm", "w_in", "a_q_norm", "a_k_norm", "a_rel_bias",
                "b_q_norm", "b_k_norm", "b_sinks", "w_gate", "w_proj_a", "w_proj_b", "w_out", "ffn2_norm",
                "ffn2_w_gu", "ffn2_w_down", "ple_norm", "w_ple_gate", "w_ple_proj")


def _full_cols(wg):
    nb, k, n = wg.shape
    return jnp.transpose(wg, (1, 0, 2)).reshape(k, nb * n)


def _col_blocks(g, nb):
    k, n = g.shape
    return jnp.transpose(g.reshape(k, nb, n // nb), (1, 0, 2))


def _step(x, p, target, w, m, v):
    bl, s_len, d = x.shape
    t = bl * s_len
    h0 = x.reshape(t, d)
    pt = p.reshape(t, p.shape[-1])
    tgt = target.reshape(t, d)

    shards = [w[nme][0].astype(BF16) for nme in BIG_NAMES]
    gathered = dict(zip(BIG_NAMES, _all_gather(shards, "weights_all_gather")))
    wgu1 = gathered["ffn1_w_gu"]
    wgu2 = gathered["ffn2_w_gu"]
    nf = wgu1.shape[2]
    wd1 = gathered["ffn1_w_down"].reshape(N_DEV // 2, nf, d)
    wd2 = gathered["ffn2_w_down"].reshape(N_DEV // 2, nf, d)
    win = _full_cols(gathered["w_in"])
    wgate = _full_cols(gathered["w_gate"])
    wpa = _full_cols(gathered["w_proj_a"])
    wpb = _full_cols(gathered["w_proj_b"])
    wpe = _full_cols(gathered["w_ple_proj"])
    wout = gathered["w_out"].reshape(d, d)
    wpg = gathered["w_ple_gate"].reshape(d, d)

    g_ffn1, g_mix, g_ffn2, g_ple = w["ffn1_norm"], w["mix_norm"], w["ffn2_norm"], w["ple_norm"]
    tile2 = lambda a: jnp.tile(a.reshape(1, HEAD_DIM), (1, 2))
    gqa, gka, gqb, gkb = tile2(w["a_q_norm"]), tile2(w["a_k_norm"]), tile2(w["b_q_norm"]), tile2(w["b_k_norm"])
    sinks = w["b_sinks"].reshape(B_Q_HEADS)
    bias_a = _rel_bias_expand(w["a_rel_bias"][0], "rel_bias_expand")
    bias_b = _alibi_bias()

    h1, gu1 = _ffn_fwd(h0, g_ffn1, wgu1, wd1, "ffn1_fwd")
    un, qkv, gate = _proj_fwd(h1, g_mix, win, wgate, "proj_fwd")
    ya = _attn_fwd("A", qkv, gqa, gka, bias_a, sinks, bl, s_len, "attn_a_fwd")
    yb = _attn_fwd("B", qkv, gqb, gkb, bias_b, sinks, bl, s_len, "attn_b_fwd")
    h2, merged, pa, pb = _merge_fwd(h1, ya, yb, gate, wpa, wpb, wout, "merge_fwd")
    h3, gu2 = _ffn_fwd(h2, g_ffn2, wgu2, wd2, "ffn2_fwd")
    dh3, dz4, dpp, n4, dg_ple, loss_part = _ple_loss(h3, g_ple, pt, tgt, wpg, wpe, "ple_loss")

    grads = {}
    grads["w_ple_gate"] = _dw(n4, dz4, 1, d, "dw_ple_gate").reshape(N_DEV, d // N_DEV, d)
    grads["w_ple_proj"] = _dw(pt, dpp, N_DEV, d // N_DEV, "dw_ple_proj")

    dh2, dgu2, a2, n3, dg_ffn2 = _ffn_bwd(dh3, h2, g_ffn2, gu2, wgu2, wd2, "ffn2_bwd")
    grads["ffn2_w_gu"] = _dw(n3, dgu2, N_DEV, nf, "dw_ffn2_gu")
    grads["ffn2_w_down"] = _dw(a2, dh3, N_DEV // 2, d, "dw_ffn2_down", 0.5)

    dpa, dpb, dzg, dya, dyb = _merge_bwd(dh2, pa, pb, gate, wpa, wpb, wout, "merge_bwd")
    grads["w_out"] = _dw(merged, dh2, 1, d, "dw_out").reshape(N_DEV, d // N_DEV, d)
    grads["w_proj_a"] = _dw(ya, dpa, N_DEV, d // N_DEV, "dw_proj_a")
    grads["w_proj_b"] = _dw(yb, dpb, N_DEV, d // N_DEV, "dw_proj_b")
    grads["w_gate"] = _dw(un, dzg, N_DEV, 2 * d // N_DEV, "dw_gate")

    dqa, dka, dva, dgqa, dgka, dbias, _ = _attn_bwd("A", qkv, gqa, gka, bias_a, sinks, ya, dya, bl, s_len, "attn_a_bwd")
    dqb, dkb, dvb, dgqb, dgkb, _, dsink = _attn_bwd("B", qkv, gqb, gkb, bias_b, sinks, yb, dyb, bl, s_len, "attn_b_bwd")
    dqkv = jnp.concatenate([dqa, dka, dva, dqb, dkb, dvb], axis=1)
    dtab = _rel_bias_grad(dbias, "rel_bias_grad")

    dh1, dg_mix = _proj_bwd(dh2, h1, g_mix, dzg, dqkv, win, wgate, "proj_bwd")
    grads["w_in"] = _col_blocks(_dw(un, dqkv, 1, IN_COLS, "dw_in")[0], N_DEV)

    dh0, dgu1, a1, n1, dg_ffn1 = _ffn_bwd(dh1, h0, g_ffn1, gu1, wgu1, wd1, "ffn1_bwd")
    grads["ffn1_w_gu"] = _dw(n1, dgu1, N_DEV, nf, "dw_ffn1_gu")
    grads["ffn1_w_down"] = _dw(a1, dh1, N_DEV // 2, d, "dw_ffn1_down", 0.5)
    return dh0, loss_part, grads, (dg_ffn1, dg_mix, dg_ffn2, dg_ple, dgqa, dgka, dgqb, dgkb, dtab, dsink)


def _down_grad_layout(g, d):
    return g.reshape(N_DEV, -1, d)


def kernel(x, p, ffn1_norm, ffn1_w_gu, ffn1_w_down, mix_norm, w_in, a_q_norm, a_k_norm, a_rel_bias, b_q_norm, b_k_norm, b_sinks, w_gate, w_proj_a, w_proj_b, w_out, ffn2_norm, ffn2_w_gu, ffn2_w_down, ple_norm, w_ple_gate, w_ple_proj, loss_target, m_ffn1_norm, m_ffn1_w_gu, m_ffn1_w_down, m_mix_norm, m_w_in, m_a_q_norm, m_a_k_norm, m_a_rel_bias, m_b_q_norm, m_b_k_norm, m_b_sinks, m_w_gate, m_w_proj_a, m_w_proj_b, m_w_out, m_ffn2_norm, m_ffn2_w_gu, m_ffn2_w_down, m_ple_norm, m_w_ple_gate, m_w_ple_proj, v_ffn1_norm, v_ffn1_w_gu, v_ffn1_w_down, v_mix_norm, v_w_in, v_a_q_norm, v_a_k_norm, v_a_rel_bias, v_b_q_norm, v_b_k_norm, v_b_sinks, v_w_gate, v_w_proj_a, v_w_proj_b, v_w_out, v_ffn2_norm, v_ffn2_w_gu, v_ffn2_w_down, v_ple_norm, v_w_ple_gate, v_w_ple_proj):
    w = dict(ffn1_norm=ffn1_norm, ffn1_w_gu=ffn1_w_gu, ffn1_w_down=ffn1_w_down, mix_norm=mix_norm, w_in=w_in,
             a_q_norm=a_q_norm, a_k_norm=a_k_norm, a_rel_bias=a_rel_bias, b_q_norm=b_q_norm, b_k_norm=b_k_norm,
             b_sinks=b_sinks, w_gate=w_gate, w_proj_a=w_proj_a, w_proj_b=w_proj_b, w_out=w_out, ffn2_norm=ffn2_norm,
             ffn2_w_gu=ffn2_w_gu, ffn2_w_down=ffn2_w_down, ple_norm=ple_norm, w_ple_gate=w_ple_gate,
             w_ple_proj=w_ple_proj)
    m = dict(ffn1_norm=m_ffn1_norm, ffn1_w_gu=m_ffn1_w_gu, ffn1_w_down=m_ffn1_w_down, mix_norm=m_mix_norm,
             w_in=m_w_in, a_q_norm=m_a_q_norm, a_k_norm=m_a_k_norm, a_rel_bias=m_a_rel_bias, b_q_norm=m_b_q_norm,
             b_k_norm=m_b_k_norm, b_sinks=m_b_sinks, w_gate=m_w_gate, w_proj_a=m_w_proj_a, w_proj_b=m_w_proj_b,
             w_out=m_w_out, ffn2_norm=m_ffn2_norm, ffn2_w_gu=m_ffn2_w_gu, ffn2_w_down=m_ffn2_w_down,
             ple_norm=m_ple_norm, w_ple_gate=m_w_ple_gate, w_ple_proj=m_w_ple_proj)
    v = dict(ffn1_norm=v_ffn1_norm, ffn1_w_gu=v_ffn1_w_gu, ffn1_w_down=v_ffn1_w_down, mix_norm=v_mix_norm,
             w_in=v_w_in, a_q_norm=v_a_q_norm, a_k_norm=v_a_k_norm, a_rel_bias=v_a_rel_bias, b_q_norm=v_b_q_norm,
             b_k_norm=v_b_k_norm, b_sinks=v_b_sinks, w_gate=v_w_gate, w_proj_a=v_w_proj_a, w_proj_b=v_w_proj_b,
             w_out=v_w_out, ffn2_norm=v_ffn2_norm, ffn2_w_gu=v_ffn2_w_gu, ffn2_w_down=v_ffn2_w_down,
             ple_norm=v_ple_norm, w_ple_gate=v_w_ple_gate, w_ple_proj=v_w_ple_proj)
    bl, s_len, d = x.shape

    dh0, loss_part, grads, smalls = _step(x, p[0], loss_target, w, m, v)
    dg_ffn1, dg_mix, dg_ffn2, dg_ple, dgqa, dgka, dgqb, dgkb, dtab, dsink = smalls

    xi, yi, ci = _place()
    core = jnp.stack([ci]).astype(jnp.int32)
    place = jnp.stack([2 * xi + yi, ci]).astype(jnp.int32)
    g4 = []
    for nme in BIG_NAMES:
        g = grads[nme]
        if nme in ("ffn1_w_down", "ffn2_w_down"):
            g = _down_grad_layout(g, d)
        g4.append(g.reshape((4, 2) + g.shape[1:]))
    r1 = _sibling_exchange(g4, "grads_sibling_exchange")
    parts = [_pair_sum(g, r, core, "pair_sum_" + nme) for nme, g, r in zip(BIG_NAMES, g4, r1)]
    r2 = _chip_exchange(parts, "grads_chip_exchange")
    big = {}
    for nme, g, ra, rb in zip(BIG_NAMES, g4, r1, r2):
        outs = _final_adam(g, ra, rb, w[nme][0], m[nme][0], v[nme][0], place, "adam_" + nme)
        big[nme] = [o[None] for o in outs]

    fold = lambda a: (a[:, :, 0, :HEAD_DIM] + a[:, :, 0, HEAD_DIM:]).sum(axis=(0, 1))
    small_part = dict(
        ffn1_norm=dg_ffn1, mix_norm=dg_mix, ffn2_norm=dg_ffn2, ple_norm=dg_ple,
        a_q_norm=fold(dgqa), a_k_norm=fold(dgka), b_q_norm=fold(dgqb), b_k_norm=fold(dgkb),
        a_rel_bias=dtab[:, 0, :2 * A_MAX_REL + 1],
        b_sinks=dsink.sum(axis=0)[:, 0, :2].reshape(B_Q_HEADS),
        loss=loss_part[0, :1])
    zero1 = jnp.zeros((1,), F32)
    shapes = {nme: w[nme].shape for nme in SMALL_NAMES if nme != "loss"}
    shapes["loss"] = ()
    pk = lambda src: _pack_small({**{nme: src[nme] for nme in SMALL_NAMES if nme != "loss"}, "loss": zero1})
    sg, sd, sm, sv = _small_allreduce_adam(_pack_small(small_part), pk(w), pk(m), pk(v), "small_allreduce_adam")
    sg, sd, sm, sv = (_unpack_small(a, shapes) for a in (sg, sd, sm, sv))

    def pick(i):
        out = []
        for nme in WEIGHT_ORDER:
            out.append(big[nme][i] if nme in big else (sg, sd, sm, sv)[i][nme])
        return out

    return (sg["loss"], dh0.reshape(bl, s_len, d), *pick(0), *pick(1), *pick(2), *pick(3))
```

```python
import functools

import jax
import jax.numpy as jnp
import numpy as np
from jax import lax
from jax.experimental import pallas as pl
from jax.experimental.pallas import tpu as pltpu

F32 = jnp.float32
BF16 = jnp.bfloat16

CHUNK = 64
HEAD_DIM = 64
A_HEADS = 8
A_PREV = 8
A_MAX_REL = 128
B_Q_HEADS = 8
B_KV_HEADS = 2
B_PREV = 2
A_WIDTH = A_HEADS * HEAD_DIM
B_Q_WIDTH = B_Q_HEADS * HEAD_DIM
B_KV_WIDTH = B_KV_HEADS * HEAD_DIM
IN_COLS = 3 * A_WIDTH + B_Q_WIDTH + 2 * B_KV_WIDTH
EPS = 1e-6
NEG_INF = -1e30
ADAM_LR = 0.001
ADAM_B1 = 0.9
ADAM_B2 = 0.999
ADAM_EPS = 1e-08
ADAM_WD = 0.01
ADAM_STEP = 10

N_DEV = 8
LANES = 128
QTILE = 2 * CHUNK
VMEM_LIMIT = 56 * 1024 * 1024

MESH_ID = pl.DeviceIdType.MESH
ANY = pl.BlockSpec(memory_space=pl.ANY)


def _dot(a, b):
    return jnp.dot(a, b, preferred_element_type=F32)


def _dot_nt(a, b):
    return lax.dot_general(a, b, (((1,), (1,)), ((), ())), preferred_element_type=F32)


def _dot_tn(a, b):
    return lax.dot_general(a, b, (((0,), (0,)), ((), ())), preferred_element_type=F32)


def _params(sem=None, vmem=VMEM_LIMIT):
    return pltpu.CompilerParams(dimension_semantics=sem, vmem_limit_bytes=vmem)


def _row_tile(t, want):
    while t % want:
        want //= 2
    return want


def _place():
    return lax.axis_index("x"), lax.axis_index("y"), lax.axis_index("c")


def _all_gather(shards, name):
    n = len(shards)

    def body(*refs):
        ins, outs = refs[:n], refs[n:2 * n]
        send_sems, recv_sems, local_sems = refs[2 * n:]
        x, y, c = _place()
        me, sib = (x, y, c), (x, y, 1 - c)
        chips = [(1 - x, y), (x, 1 - y), (1 - x, 1 - y)]

        def copy(w, k, block, to, src=None):
            px, py, pc = block
            dst = outs[w].at[4 * px + 2 * py + pc]
            return pltpu.make_async_remote_copy(
                src_ref=dst if src is None else src, dst_ref=dst,
                send_sem=send_sems.at[w * 7 + k], recv_sem=recv_sems.at[w * 7 + k],
                device_id=to, device_id_type=MESH_ID)

        mine = [pltpu.make_async_copy(ins[w], outs[w].at[4 * x + 2 * y + c], local_sems.at[w]) for w in range(n)]
        for cp in mine:
            cp.start()
        first = []
        for w in range(n):
            first.append(copy(w, 0, me, sib, src=ins[w]))
            first += [copy(w, 1 + j, me, (*chip, c), src=ins[w]) for j, chip in enumerate(chips)]
        for cp in first:
            cp.start()
        passed = []
        for j, chip in enumerate(chips):
            for w in range(n):
                copy(w, 1 + j, (*chip, c), me).wait_recv()
                fwd = copy(w, 4 + j, (*chip, c), sib)
                fwd.start()
                passed.append(fwd)
        for w in range(n):
            copy(w, 0, sib, me).wait_recv()
        for j, chip in enumerate(chips):
            for w in range(n):
                copy(w, 4 + j, (*chip, 1 - c), me).wait_recv()
        for cp in first + passed:
            cp.wait_send()
        for cp in mine:
            cp.wait()

    return pl.pallas_call(
        body, name=name,
        out_shape=[jax.ShapeDtypeStruct((N_DEV,) + s.shape, s.dtype) for s in shards],
        in_specs=[ANY] * n, out_specs=[ANY] * n,
        scratch_shapes=[pltpu.SemaphoreType.DMA((7 * n,)), pltpu.SemaphoreType.DMA((7 * n,)),
                        pltpu.SemaphoreType.DMA((n,))],
    )(*shards)


def _sibling_exchange(grads, name):
    n = len(grads)

    def body(*refs):
        ins, outs = refs[:n], refs[n:2 * n]
        send_sems, recv_sems = refs[2 * n:]
        x, y, c = _place()
        cps = [pltpu.make_async_remote_copy(
            src_ref=ins[w].at[:, pl.ds(1 - c, 1)], dst_ref=outs[w],
            send_sem=send_sems.at[w], recv_sem=recv_sems.at[w],
            device_id=(x, y, 1 - c), device_id_type=MESH_ID) for w in range(n)]
        for cp in cps:
            cp.start()
        for cp in cps:
            cp.wait()

    return pl.pallas_call(
        body, name=name,
        out_shape=[jax.ShapeDtypeStruct((4, 1) + g.shape[2:], g.dtype) for g in grads],
        in_specs=[ANY] * n, out_specs=[ANY] * n,
        scratch_shapes=[pltpu.SemaphoreType.DMA((n,)), pltpu.SemaphoreType.DMA((n,))],
    )(*grads)


def _chip_exchange(parts, name):
    n = len(parts)

    def body(*refs):
        ins, outs = refs[:n], refs[n:2 * n]
        send_sems, recv_sems = refs[2 * n:]
        x, y, c = _place()
        chips = [(1 - x, y), (x, 1 - y), (1 - x, 1 - y)]
        cps = []
        for w in range(n):
            for k, (qx, qy) in enumerate(chips):
                cps.append(pltpu.make_async_remote_copy(
                    src_ref=ins[w].at[2 * qx + qy], dst_ref=outs[w].at[k],
                    send_sem=send_sems.at[3 * w + k], recv_sem=recv_sems.at[3 * w + k],
                    device_id=(qx, qy, c), device_id_type=MESH_ID))
        for cp in cps:
            cp.start()
        for cp in cps:
            cp.wait()

    return pl.pallas_call(
        body, name=name,
        out_shape=[jax.ShapeDtypeStruct((3,) + p.shape[1:], p.dtype) for p in parts],
        in_specs=[ANY] * n, out_specs=[ANY] * n,
        scratch_shapes=[pltpu.SemaphoreType.DMA((3 * n,)), pltpu.SemaphoreType.DMA((3 * n,))],
    )(*parts)


def _adam(w, g, m, v):
    m2 = ADAM_B1 * m + (1.0 - ADAM_B1) * g
    v2 = ADAM_B2 * v + (1.0 - ADAM_B2) * (g * g)
    m_hat = m2 / (1.0 - ADAM_B1 ** ADAM_STEP)
    v_hat = v2 / (1.0 - ADAM_B2 ** ADAM_STEP)
    delta = -ADAM_LR * (m_hat / (jnp.sqrt(v_hat) + ADAM_EPS) + ADAM_WD * w)
    return delta, m2, v2


def _small_allreduce_adam(part, w, m, v, name):
    rows = part.shape[0]

    def body(p_ref, w_ref, m_ref, v_ref, g_ref, d_ref, mo_ref, vo_ref, buf, send_sems, recv_sems):
        x, y, c = _place()
        buf[0] = p_ref[...]
        cps = []
        for k in range(1, N_DEV):
            kx, ky, kc = (k >> 2) & 1, (k >> 1) & 1, k & 1
            peer = (x ^ kx, y ^ ky, c ^ kc)
            cps.append(pltpu.make_async_remote_copy(
                src_ref=p_ref, dst_ref=buf.at[k], send_sem=send_sems.at[k - 1], recv_sem=recv_sems.at[k - 1],
                device_id=peer, device_id_type=MESH_ID))
        for cp in cps:
            cp.start()
        for cp in cps:
            cp.wait()
        me = 4 * x + 2 * y + c
        total = buf[me]
        for d in range(1, N_DEV):
            total = total + buf[d ^ me]
        g_ref[...] = total
        delta, m2, v2 = _adam(w_ref[...], total, m_ref[...], v_ref[...])
        d_ref[...] = delta
        mo_ref[...] = m2
        vo_ref[...] = v2

    vm = pl.BlockSpec(memory_space=pltpu.VMEM)
    return pl.pallas_call(
        body, name=name,
        out_shape=[jax.ShapeDtypeStruct(part.shape, F32)] * 4,
        in_specs=[vm] * 4, out_specs=[vm] * 4,
        scratch_shapes=[pltpu.VMEM((N_DEV, rows, LANES), F32),
                        pltpu.SemaphoreType.DMA((N_DEV - 1,)), pltpu.SemaphoreType.DMA((N_DEV - 1,))],
    )(part, w, m, v)


def _pair_sum(g4, r1, core, name):
    _, _, r, c = g4.shape
    tr = _row_tile(r, 256) if r % 8 == 0 and r > 512 else r

    def body(core_ref, g_ref, r_ref, o_ref):
        o_ref[...] = (g_ref[...] + r_ref[...]).astype(BF16)

    return pl.pallas_call(
        body, name=name,
        out_shape=jax.ShapeDtypeStruct((4, r, c), BF16),
        grid_spec=pltpu.PrefetchScalarGridSpec(
            num_scalar_prefetch=1, grid=(4, r // tr),
            in_specs=[pl.BlockSpec((None, None, tr, c), lambda q, i, s: (q, s[0], i, 0)),
                      pl.BlockSpec((None, None, tr, c), lambda q, i, s: (q, 0, i, 0))],
            out_specs=pl.BlockSpec((None, tr, c), lambda q, i, s: (q, i, 0))),
        compiler_params=_params(("arbitrary", "arbitrary")),
    )(core, g4, r1)


def _final_adam(g4, r1, r2, w, m, v, place, name):
    _, _, r, c = g4.shape
    tr = _row_tile(r, 256) if r % 8 == 0 and r > 512 else r

    def body(place_ref, g_ref, r1_ref, r2_ref, w_ref, m_ref, v_ref, go_ref, d_ref, mo_ref, vo_ref):
        g = g_ref[...] + r1_ref[...]
        for k in range(3):
            g = g + r2_ref[k].astype(F32)
        go_ref[...] = g
        delta, m2, v2 = _adam(w_ref[...], g, m_ref[...], v_ref[...])
        d_ref[...] = delta
        mo_ref[...] = m2
        vo_ref[...] = v2

    plain = pl.BlockSpec((tr, c), lambda i, s: (i, 0))
    return pl.pallas_call(
        body, name=name,
        out_shape=[jax.ShapeDtypeStruct((r, c), F32)] * 4,
        grid_spec=pltpu.PrefetchScalarGridSpec(
            num_scalar_prefetch=1, grid=(r // tr,),
            in_specs=[pl.BlockSpec((None, None, tr, c), lambda i, s: (s[0], s[1], i, 0)),
                      pl.BlockSpec((None, None, tr, c), lambda i, s: (s[0], 0, i, 0)),
                      pl.BlockSpec((3, tr, c), lambda i, s: (0, i, 0)),
                      plain, plain, plain],
            out_specs=[plain] * 4),
        compiler_params=_params(("arbitrary",)),
    )(place, g4, r1, r2, w, m, v)


def _rms(x, gain):
    r = lax.rsqrt(jnp.mean(x * x, axis=-1, keepdims=True) + EPS)
    xh = x * r
    return xh * gain, xh, r


def _rms_bwd(xh, r, gain, dy):
    gdy = gain * dy
    dx = r * (gdy - xh * jnp.mean(xh * gdy, axis=-1, keepdims=True))
    return dx, jnp.sum(dy * xh, axis=0, keepdims=True)


def _load_weights(pairs, sems):
    cps = [pltpu.make_async_copy(src, dst, sems.at[i]) for i, (src, dst) in enumerate(pairs)]
    for cp in cps:
        cp.start()
    for cp in cps:
        cp.wait()


def _ffn_fwd(h, gain, wgu, wd, name):
    t, d = h.shape
    nb, _, nf = wgu.shape
    nh = nb // 2
    tm = _row_tile(t, 256)

    def body(h_ref, g_ref, wgu_hbm, wd_hbm, out_ref, gu_ref, wgu_v, wd_v, sems):
        @pl.when(pl.program_id(0) == 0)
        def _():
            _load_weights([(wgu_hbm, wgu_v), (wd_hbm, wd_v)], sems)

        x = h_ref[...]
        n, _, _ = _rms(x, g_ref[...])
        nbf = n.astype(BF16)
        acc = jnp.zeros((tm, d), F32)
        for j in range(nh):
            g = _dot(nbf, wgu_v[j])
            u = _dot(nbf, wgu_v[j + nh])
            gu_ref[j] = g.astype(BF16)
            gu_ref[j + nh] = u.astype(BF16)
            a = (g * jax.nn.sigmoid(g)) * u
            acc = acc + _dot(a.astype(BF16), wd_v[j])
        out_ref[...] = x + 0.5 * acc

    return pl.pallas_call(
        body, name=name, grid=(t // tm,),
        out_shape=[jax.ShapeDtypeStruct((t, d), F32), jax.ShapeDtypeStruct((nb, t, nf), BF16)],
        in_specs=[pl.BlockSpec((tm, d), lambda i: (i, 0)), pl.BlockSpec((1, d), lambda i: (0, 0)), ANY, ANY],
        out_specs=[pl.BlockSpec((tm, d), lambda i: (i, 0)), pl.BlockSpec((nb, tm, nf), lambda i: (0, i, 0))],
        scratch_shapes=[pltpu.VMEM(wgu.shape, BF16), pltpu.VMEM(wd.shape, BF16), pltpu.SemaphoreType.DMA((2,))],
        compiler_params=_params(("arbitrary",)),
    )(h, gain, wgu, wd)


def _ffn_bwd(dh, h, gain, gu, wgu, wd, name):
    t, d = h.shape
    nb, _, nf = wgu.shape
    nh = nb // 2
    tm = _row_tile(t, 256)

    def body(dh_ref, h_ref, g_ref, gu_ref, wgu_hbm, wd_hbm, dhp_ref, dgu_ref, a_ref, n_ref, dgain_ref,
             wgu_v, wd_v, sems):
        @pl.when(pl.program_id(0) == 0)
        def _():
            _load_weights([(wgu_hbm, wgu_v), (wd_hbm, wd_v)], sems)
            dgain_ref[...] = jnp.zeros_like(dgain_ref)

        x = h_ref[...]
        gain_v = g_ref[...]
        n, xh, r = _rms(x, gain_v)
        n_ref[...] = n.astype(BF16)
        dh_v = dh_ref[...]
        dfb = (0.5 * dh_v).astype(BF16)
        dn = jnp.zeros((tm, d), F32)
        for j in range(nh):
            da = _dot_nt(dfb, wd_v[j])
            g = gu_ref[j].astype(F32)
            u = gu_ref[j + nh].astype(F32)
            sg = jax.nn.sigmoid(g)
            si = g * sg
            dg = (da * u * (sg * (1.0 + g * (1.0 - sg)))).astype(BF16)
            du = (da * si).astype(BF16)
            a_ref[j] = (si * u).astype(BF16)
            dgu_ref[j] = dg
            dgu_ref[j + nh] = du
            dn = dn + _dot_nt(dg, wgu_v[j]) + _dot_nt(du, wgu_v[j + nh])
        dx, dgain = _rms_bwd(xh, r, gain_v, dn)
        dhp_ref[...] = dh_v + dx
        dgain_ref[...] += dgain

    row = pl.BlockSpec((tm, d), lambda i: (i, 0))
    vec = pl.BlockSpec((1, d), lambda i: (0, 0))
    return pl.pallas_call(
        body, name=name, grid=(t // tm,),
        out_shape=[jax.ShapeDtypeStruct((t, d), F32), jax.ShapeDtypeStruct((nb, t, nf), BF16),
                   jax.ShapeDtypeStruct((nh, t, nf), BF16), jax.ShapeDtypeStruct((t, d), BF16),
                   jax.ShapeDtypeStruct((1, d), F32)],
        in_specs=[row, row, vec, pl.BlockSpec((nb, tm, nf), lambda i: (0, i, 0)), ANY, ANY],
        out_specs=[row, pl.BlockSpec((nb, tm, nf), lambda i: (0, i, 0)),
                   pl.BlockSpec((nh, tm, nf), lambda i: (0, i, 0)), row, vec],
        scratch_shapes=[pltpu.VMEM(wgu.shape, BF16), pltpu.VMEM(wd.shape, BF16), pltpu.SemaphoreType.DMA((2,))],
        compiler_params=_params(("arbitrary",)),
    )(dh, h, gain, gu, wgu, wd)


def _dw(xa, dy, nb, n, name, scale=1.0):
    t, k = xa.shape[-2:]
    tt = _row_tile(t, 512)
    steps = t // tt
    wide = dy.ndim == 2 and xa.ndim == 2
    if xa.ndim == 3:
        x_spec = pl.BlockSpec((nb, tt, k), lambda i: (0, i, 0))
    else:
        x_spec = pl.BlockSpec((tt, k), lambda i: (i, 0))
    if dy.ndim == 3:
        dy_spec = pl.BlockSpec((nb, tt, n), lambda i: (0, i, 0))
    else:
        dy_spec = pl.BlockSpec((tt, dy.shape[1]), lambda i: (i, 0))
    acc_shape = (k, nb * n) if wide else (nb, k, n)

    def body(x_ref, dy_ref, o_hbm, acc, sems):
        first = pl.program_id(0) == 0

        def accumulate(idx, val):
            @pl.when(first)
            def _():
                acc[idx] = val

            @pl.when(jnp.logical_not(first))
            def _():
                acc[idx] += val

        if wide:
            accumulate(Ellipsis, _dot(x_ref[...].astype(BF16).T, dy_ref[...].astype(BF16)))
        elif xa.ndim == 2:
            xt = x_ref[...].astype(BF16).T
            for j in range(nb):
                accumulate(j, _dot(xt, dy_ref[j].astype(BF16)))
        else:
            dyb = dy_ref[...].astype(BF16)
            for j in range(nb):
                accumulate(j, _dot_tn(x_ref[j].astype(BF16), dyb))

        @pl.when(pl.program_id(0) == steps - 1)
        def _():
            if scale != 1.0:
                acc[...] = acc[...] * scale
            if wide:
                cps = [pltpu.make_async_copy(acc.at[:, pl.ds(j * n, n)] if nb > 1 else acc, o_hbm.at[j], sems.at[j])
                       for j in range(nb)]
            else:
                cps = [pltpu.make_async_copy(acc, o_hbm, sems.at[0])]
            for cp in cps:
                cp.start()
            for cp in cps:
                cp.wait()

    return pl.pallas_call(
        body, name=name, grid=(steps,),
        out_shape=jax.ShapeDtypeStruct((nb, k, n), F32),
        in_specs=[x_spec, dy_spec],
        out_specs=ANY,
        scratch_shapes=[pltpu.VMEM(acc_shape, F32), pltpu.SemaphoreType.DMA((nb,))],
        compiler_params=_params(("arbitrary",)),
    )(xa, dy)


def _proj_fwd(h, gain, win, wgate, name):
    t, d = h.shape
    tm = _row_tile(t, 256)
    nq, ng = win.shape[1], wgate.shape[1]

    def body(h_ref, g_ref, win_ref, wg_ref, un_ref, qkv_ref, gate_ref):
        n, _, _ = _rms(h_ref[...], g_ref[...])
        nbf = n.astype(BF16)
        un_ref[...] = nbf
        qkv_ref[...] = _dot(nbf, win_ref[...])
        gate_ref[...] = jax.nn.sigmoid(_dot(nbf, wg_ref[...]))

    full = lambda a: pl.BlockSpec(a.shape, lambda i: (0,) * a.ndim)
    return pl.pallas_call(
        body, name=name, grid=(t // tm,),
        out_shape=[jax.ShapeDtypeStruct((t, d), BF16), jax.ShapeDtypeStruct((t, nq), F32),
                   jax.ShapeDtypeStruct((t, ng), F32)],
        in_specs=[pl.BlockSpec((tm, d), lambda i: (i, 0)), full(gain), full(win), full(wgate)],
        out_specs=[pl.BlockSpec((tm, d), lambda i: (i, 0)), pl.BlockSpec((tm, nq), lambda i: (i, 0)),
                   pl.BlockSpec((tm, ng), lambda i: (i, 0))],
        compiler_params=_params(("arbitrary",)),
    )(h, gain, win, wgate)


def _proj_bwd(dh, h, gain, dzg, dqkv, win, wgate, name):
    t, d = h.shape
    tm = _row_tile(t, 256)
    nq, ng = win.shape[1], wgate.shape[1]

    def body(dh_ref, h_ref, g_ref, dzg_ref, dqkv_ref, win_ref, wg_ref, dhp_ref, dgain_ref):
        @pl.when(pl.program_id(0) == 0)
        def _():
            dgain_ref[...] = jnp.zeros_like(dgain_ref)

        gain_v = g_ref[...]
        _, xh, r = _rms(h_ref[...], gain_v)
        dun = _dot_nt(dzg_ref[...], wg_ref[...]) + _dot_nt(dqkv_ref[...].astype(BF16), win_ref[...])
        dx, dgain = _rms_bwd(xh, r, gain_v, dun)
        dhp_ref[...] = dh_ref[...] + dx
        dgain_ref[...] += dgain

    full = lambda a: pl.BlockSpec(a.shape, lambda i: (0,) * a.ndim)
    row = pl.BlockSpec((tm, d), lambda i: (i, 0))
    return pl.pallas_call(
        body, name=name, grid=(t // tm,),
        out_shape=[jax.ShapeDtypeStruct((t, d), F32), jax.ShapeDtypeStruct((1, d), F32)],
        in_specs=[row, row, full(gain), pl.BlockSpec((tm, ng), lambda i: (i, 0)),
                  pl.BlockSpec((tm, nq), lambda i: (i, 0)), full(win), full(wgate)],
        out_specs=[row, pl.BlockSpec((1, d), lambda i: (0, 0))],
        compiler_params=_params(("arbitrary",)),
    )(dh, h, gain, dzg, dqkv, win, wgate)


def _merge_fwd(h, ya, yb, gate, wpa, wpb, wout, name):
    t, d = h.shape
    tm = _row_tile(t, 256)

    def body(h_ref, ya_ref, yb_ref, ga_ref, gb_ref, wpa_ref, wpb_ref, wout_ref, out_ref, mg_ref, pa_ref, pb_ref):
        pa = _dot(ya_ref[...].astype(BF16), wpa_ref[...])
        pb = _dot(yb_ref[...].astype(BF16), wpb_ref[...])
        merged = (ga_ref[...] * pa + gb_ref[...] * pb).astype(BF16)
        pa_ref[...] = pa.astype(BF16)
        pb_ref[...] = pb.astype(BF16)
        mg_ref[...] = merged
        out_ref[...] = h_ref[...] + _dot(merged, wout_ref[...])

    full = lambda a: pl.BlockSpec(a.shape, lambda i: (0,) * a.ndim)
    row = pl.BlockSpec((tm, d), lambda i: (i, 0))
    yrow = pl.BlockSpec((tm, ya.shape[1]), lambda i: (i, 0))
    return pl.pallas_call(
        body, name=name, grid=(t // tm,),
        out_shape=[jax.ShapeDtypeStruct((t, d), F32)] + [jax.ShapeDtypeStruct((t, d), BF16)] * 3,
        in_specs=[row, yrow, yrow, pl.BlockSpec((tm, d), lambda i: (i, 0)), pl.BlockSpec((tm, d), lambda i: (i, 1)),
                  full(wpa), full(wpb), full(wout)],
        out_specs=[row] * 4,
        compiler_params=_params(("arbitrary",)),
    )(h, ya, yb, gate, gate, wpa, wpb, wout)


def _merge_bwd(dh, pa, pb, gate, wpa, wpb, wout, name):
    t, d = dh.shape
    tm = _row_tile(t, 256)
    wy = wpa.shape[0]

    def body(dh_ref, pa_ref, pb_ref, ga_ref, gb_ref, wpa_ref, wpb_ref, wout_ref,
             dpa_ref, dpb_ref, dzg_ref, dya_ref, dyb_ref):
        dm = _dot_nt(dh_ref[...].astype(BF16), wout_ref[...])
        ga, gb = ga_ref[...], gb_ref[...]
        dpa = (dm * ga).astype(BF16)
        dpb = (dm * gb).astype(BF16)
        dpa_ref[...] = dpa
        dpb_ref[...] = dpb
        dzg_ref[:, :d] = (dm * pa_ref[...].astype(F32) * ga * (1.0 - ga)).astype(BF16)
        dzg_ref[:, d:] = (dm * pb_ref[...].astype(F32) * gb * (1.0 - gb)).astype(BF16)
        dya_ref[...] = _dot_nt(dpa, wpa_ref[...])
        dyb_ref[...] = _dot_nt(dpb, wpb_ref[...])

    full = lambda a: pl.BlockSpec(a.shape, lambda i: (0,) * a.ndim)
    row = pl.BlockSpec((tm, d), lambda i: (i, 0))
    yrow = pl.BlockSpec((tm, wy), lambda i: (i, 0))
    return pl.pallas_call(
        body, name=name, grid=(t // tm,),
        out_shape=[jax.ShapeDtypeStruct((t, d), BF16), jax.ShapeDtypeStruct((t, d), BF16),
                   jax.ShapeDtypeStruct((t, 2 * d), BF16), jax.ShapeDtypeStruct((t, wy), F32),
                   jax.ShapeDtypeStruct((t, wy), F32)],
        in_specs=[row, row, row, pl.BlockSpec((tm, d), lambda i: (i, 0)), pl.BlockSpec((tm, d), lambda i: (i, 1)),
                  full(wpa), full(wpb), full(wout)],
        out_specs=[row, row, pl.BlockSpec((tm, 2 * d), lambda i: (i, 0)), yrow, yrow],
        compiler_params=_params(("arbitrary",)),
    )(dh, pa, pb, gate, gate, wpa, wpb, wout)


def _ple_loss(h, gain, p, target, wpg, wpe, name):
    t, d = h.shape
    tm = _row_tile(t, 256)
    pd = p.shape[1]

    def body(h_ref, g_ref, p_ref, t_ref, wpg_ref, wpe_ref, dh_ref, dz_ref, dpp_ref, n_ref, dgain_ref, loss_ref):
        @pl.when(pl.program_id(0) == 0)
        def _():
            dgain_ref[...] = jnp.zeros_like(dgain_ref)
            loss_ref[...] = jnp.zeros_like(loss_ref)

        x = h_ref[...]
        gain_v = g_ref[...]
        n, xh, r = _rms(x, gain_v)
        nbf = n.astype(BF16)
        n_ref[...] = nbf
        pg = jax.nn.sigmoid(_dot(nbf, wpg_ref[...]))
        pp = _dot(p_ref[...].astype(BF16), wpe_ref[...])
        err = (x + pg * pp) - t_ref[...]
        loss_ref[...] += 0.5 * jnp.sum(jnp.mean(err * err, axis=-1, keepdims=True))
        dy = err * (1.0 / d)
        dpp_ref[...] = (dy * pg).astype(BF16)
        dz = (dy * pp * pg * (1.0 - pg)).astype(BF16)
        dz_ref[...] = dz
        dn = _dot_nt(dz, wpg_ref[...])
        dx, dgain = _rms_bwd(xh, r, gain_v, dn)
        dh_ref[...] = dy + dx
        dgain_ref[...] += dgain

    full = lambda a: pl.BlockSpec(a.shape, lambda i: (0,) * a.ndim)
    row = pl.BlockSpec((tm, d), lambda i: (i, 0))
    return pl.pallas_call(
        body, name=name, grid=(t // tm,),
        out_shape=[jax.ShapeDtypeStruct((t, d), F32), jax.ShapeDtypeStruct((t, d), BF16),
                   jax.ShapeDtypeStruct((t, d), BF16), jax.ShapeDtypeStruct((t, d), BF16),
                   jax.ShapeDtypeStruct((1, d), F32), jax.ShapeDtypeStruct((8, LANES), F32)],
        in_specs=[row, full(gain), pl.BlockSpec((tm, pd), lambda i: (i, 0)), row, full(wpg), full(wpe)],
        out_specs=[row, row, row, row, pl.BlockSpec((1, d), lambda i: (0, 0)),
                   pl.BlockSpec((8, LANES), lambda i: (0, 0))],
        compiler_params=_params(("arbitrary",)),
    )(h, gain, p, target, wpg, wpe)


def _head_masks():
    lane = lax.broadcasted_iota(jnp.int32, (1, LANES), 1)
    m0 = (lane < HEAD_DIM).astype(F32)
    return m0, 1.0 - m0


def _head_mean(v, m0, m1):
    s0 = jnp.sum(v * m0, axis=-1, keepdims=True)
    s1 = jnp.sum(v * m1, axis=-1, keepdims=True)
    return (s0 * m0 + s1 * m1) * (1.0 / HEAD_DIM)


def _head_norm(x, gain, m0, m1):
    r = lax.rsqrt(_head_mean(x * x, m0, m1) + EPS)
    xh = x * r
    return xh * gain, xh, r


def _head_norm_bwd(xh, r, gain, dy, m0, m1):
    gdy = gain * dy
    dx = r * (gdy - xh * _head_mean(xh * gdy, m0, m1))
    return dx, jnp.sum(dy * xh, axis=0, keepdims=True)


def _attn_prep(mode, pair, s_len, padk, q_ref, k_ref, v_ref, gq_ref, gk_ref, qs, k0, k1, v0, v1):
    m0, m1 = _head_masks()
    zpad = jnp.zeros((padk, LANES), BF16)
    for buf in (k0, k1, v0, v1):
        buf[pl.ds(0, padk), :] = zpad
    first_kv = (pair // 2) == 0
    rt = _row_tile(s_len, 256)

    def step(i, carry):
        rows = pl.ds(pl.multiple_of(i * rt, rt), rt)
        dst = pl.ds(pl.multiple_of(padk + i * rt, QTILE), rt)
        qn, _, _ = _head_norm(q_ref[rows, :], gq_ref[...], m0, m1)
        kn, _, _ = _head_norm(k_ref[rows, :], gk_ref[...], m0, m1)
        vv = v_ref[rows, :]
        qs[rows, :] = (qn * (HEAD_DIM ** -0.5)).astype(BF16)
        if mode == "B":
            kn = jnp.where(first_kv, kn, pltpu.roll(kn, HEAD_DIM, 1))
            vv = jnp.where(first_kv, vv, pltpu.roll(vv, HEAD_DIM, 1))
            ka, va = kn * m0, vv * m0
            kb, vb = pltpu.roll(ka, HEAD_DIM, 1), pltpu.roll(va, HEAD_DIM, 1)
        else:
            ka, kb, va, vb = kn * m0, kn * m1, vv * m0, vv * m1
        k0[dst, :] = ka.astype(BF16)
        k1[dst, :] = kb.astype(BF16)
        v0[dst, :] = va.astype(BF16)
        v1[dst, :] = vb.astype(BF16)
        return carry

    lax.fori_loop(0, s_len // rt, step, 0)


def _attn_probs(mode, q2, kb, bias, ok, sink):
    s = _dot_nt(q2, kb) + bias
    s = jnp.where(ok, s, NEG_INF)
    mx = jnp.max(s, axis=-1, keepdims=True)
    if mode == "B":
        mx = jnp.maximum(mx, sink)
    e = jnp.exp(s - mx)
    l = jnp.sum(e, axis=-1, keepdims=True)
    if mode == "B":
        l = l + jnp.exp(sink - mx)
    return e, mx, l


def _attn_cols(mode):
    if mode == "A":
        return (lambda b, p: (b, p)), (lambda b, p: (b, 4 + p)), (lambda b, p: (b, 8 + p))
    return (lambda b, p: (b, 12 + p)), (lambda b, p: (b, 16)), (lambda b, p: (b, 17))


def _attn_fwd(mode, qkv, gq, gk, bias, sinks, bl, s_len, name):
    bw = bias.shape[-1]
    padk = bw - QTILE
    nt = s_len // QTILE
    qmap, kmap, vmap = _attn_cols(mode)

    def body(q_ref, k_ref, v_ref, gq_ref, gk_ref, bias_ref, sink_ref, o_ref, qs, k0, k1, v0, v1):
        pair = pl.program_id(1)
        _attn_prep(mode, pair, s_len, padk, q_ref, k_ref, v_ref, gq_ref, gk_ref, qs, k0, k1, v0, v1)
        col = lax.broadcasted_iota(jnp.int32, (QTILE, bw), 1)

        def tile(m, carry):
            r0 = pl.multiple_of(m * QTILE, QTILE)
            q2 = qs[pl.ds(r0, QTILE), :]
            ok = col >= (padk - r0)
            acc = jnp.zeros((QTILE, LANES), F32)
            for hh, (kk, vv) in enumerate(((k0, v0), (k1, v1))):
                sink = sink_ref[2 * pair + hh]
                e, _, l = _attn_probs(mode, q2, kk[pl.ds(r0, bw), :], bias_ref[hh], ok, sink)
                acc = acc + _dot(e.astype(BF16), vv[pl.ds(r0, bw), :]) / l
            o_ref[pl.ds(r0, QTILE), :] = acc
            return carry

        lax.fori_loop(0, nt, tile, 0, unroll=2)

    blk = lambda f: pl.BlockSpec((s_len, LANES), f)
    vec = pl.BlockSpec((1, LANES), lambda b, p: (0, 0))
    return pl.pallas_call(
        body, name=name, grid=(bl, 4),
        out_shape=jax.ShapeDtypeStruct((bl * s_len, 4 * LANES), F32),
        in_specs=[blk(qmap), blk(kmap), blk(vmap), vec, vec,
                  pl.BlockSpec((2, QTILE, bw), lambda b, p: (p, 0, 0)),
                  pl.BlockSpec(memory_space=pltpu.SMEM)],
        out_specs=pl.BlockSpec((s_len, LANES), lambda b, p: (b, p)),
        scratch_shapes=[pltpu.VMEM((s_len, LANES), BF16)] + [pltpu.VMEM((s_len + padk, LANES), BF16)] * 4,
        compiler_params=_params(("arbitrary", "arbitrary")),
    )(qkv, qkv, qkv, gq, gk, bias, sinks)


def _attn_bwd(mode, qkv, gq, gk, bias, sinks, y, dy, bl, s_len, name):
    bw = bias.shape[-1]
    padk = bw - QTILE
    nt = s_len // QTILE
    qmap, kmap, vmap = _attn_cols(mode)
    t = bl * s_len
    kvw = 4 * LANES if mode == "A" else LANES

    def body(q_ref, k_ref, v_ref, gq_ref, gk_ref, bias_ref, sink_ref, y_ref, dy_ref,
             dq_ref, dk_ref, dv_ref, dgq_ref, dgk_ref, dbias_ref, dsink_ref,
             qs, k0, k1, v0, v1, dqs, dk0, dk1, dv0, dv1):
        pair = pl.program_id(1)
        m0, m1 = _head_masks()
        _attn_prep(mode, pair, s_len, padk, q_ref, k_ref, v_ref, gq_ref, gk_ref, qs, k0, k1, v0, v1)
        for buf in (dk0, dk1, dv0, dv1):
            buf[...] = jnp.zeros_like(buf)
        dbias_ref[...] = jnp.zeros_like(dbias_ref)
        col = lax.broadcasted_iota(jnp.int32, (QTILE, bw), 1)
        lane8 = lax.broadcasted_iota(jnp.int32, (8, LANES), 1)

        def tile(m, dsink):
            r0 = pl.multiple_of(m * QTILE, QTILE)
            rows = pl.ds(r0, QTILE)
            band = pl.ds(r0, bw)
            q2 = qs[rows, :]
            do2 = dy_ref[rows, :]
            dd = do2 * y_ref[rows, :]
            dob = do2.astype(BF16)
            ok = col >= (padk - r0)
            dq = jnp.zeros((QTILE, LANES), F32)
            for hh, (kk, vv, dkk, dvv, mh) in enumerate(((k0, v0, dk0, dv0, m0), (k1, v1, dk1, dv1, m1))):
                sink = sink_ref[2 * pair + hh]
                kb = kk[band, :]
                e, mx, l = _attn_probs(mode, q2, kb, bias_ref[hh], ok, sink)
                inv = 1.0 / l
                pn = e * inv
                delta = jnp.sum(dd * mh, axis=-1, keepdims=True)
                dp = _dot_nt(dob, vv[band, :])
                ds = pn * (dp - delta)
                if mode == "A":
                    dbias_ref[hh] += ds
                else:
                    ps = jnp.exp(sink - mx) * inv
                    dsink = dsink + jnp.where(lane8 == hh, -jnp.sum(ps * delta), 0.0)
                dsb = ds.astype(BF16)
                dvv[band, :] += _dot_tn(pn.astype(BF16), dob)
                dkk[band, :] += _dot_tn(dsb, q2)
                dq = dq + _dot(dsb, kb)
            dqs[rows, :] = dq * (HEAD_DIM ** -0.5)
            return dsink

        dsink = lax.fori_loop(0, nt, tile, jnp.zeros((8, LANES), F32), unroll=2)
        dsink_ref[...] = dsink

        first_kv = (pair // 2) == 0
        rt = _row_tile(s_len, 256)

        def post(i, carry):
            dgq, dgk = carry
            rows = pl.ds(pl.multiple_of(i * rt, rt), rt)
            src = pl.ds(pl.multiple_of(padk + i * rt, QTILE), rt)
            gq_v, gk_v = gq_ref[...], gk_ref[...]
            _, qh, qr = _head_norm(q_ref[rows, :], gq_v, m0, m1)
            _, kh, kr = _head_norm(k_ref[rows, :], gk_v, m0, m1)
            dq_raw, dgq_i = _head_norm_bwd(qh, qr, gq_v, dqs[rows, :], m0, m1)
            if mode == "A":
                dkn = dk0[src, :] * m0 + dk1[src, :] * m1
                dvn = dv0[src, :] * m0 + dv1[src, :] * m1
            else:
                dkn = dk0[src, :] * m0 + pltpu.roll(dk1[src, :] * m1, HEAD_DIM, 1)
                dvn = dv0[src, :] * m0 + pltpu.roll(dv1[src, :] * m1, HEAD_DIM, 1)
                dkn = jnp.where(first_kv, dkn, pltpu.roll(dkn, HEAD_DIM, 1))
                dvn = jnp.where(first_kv, dvn, pltpu.roll(dvn, HEAD_DIM, 1))
            dk_raw, dgk_i = _head_norm_bwd(kh, kr, gk_v, dkn, m0, m1)
            dq_ref[rows, :] = dq_raw
            if mode == "A":
                dk_ref[rows, :] = dk_raw
                dv_ref[rows, :] = dvn
            else:
                @pl.when(pair == 0)
                def _():
                    dk_ref[rows, :] = dk_raw
                    dv_ref[rows, :] = dvn

                @pl.when(pair != 0)
                def _():
                    dk_ref[rows, :] += dk_raw
                    dv_ref[rows, :] += dvn
            return dgq + dgq_i, dgk + dgk_i

        z = jnp.zeros((1, LANES), F32)
        dgq, dgk = lax.fori_loop(0, s_len // rt, post, (z, z))
        dgq_ref[...] = jnp.broadcast_to(dgq, (8, LANES))
        dgk_ref[...] = jnp.broadcast_to(dgk, (8, LANES))

    blk = lambda f: pl.BlockSpec((s_len, LANES), f)
    vec = pl.BlockSpec((1, LANES), lambda b, p: (0, 0))
    small = pl.BlockSpec((None, None, 8, LANES), lambda b, p: (b, p, 0, 0))
    kvmap = (lambda b, p: (b, p)) if mode == "A" else (lambda b, p: (b, 0))
    pad_f32 = pltpu.VMEM((s_len + padk, LANES), F32)
    pad_bf = pltpu.VMEM((s_len + padk, LANES), BF16)
    return pl.pallas_call(
        body, name=name, grid=(bl, 4),
        out_shape=[jax.ShapeDtypeStruct((t, 4 * LANES), F32), jax.ShapeDtypeStruct((t, kvw), F32),
                   jax.ShapeDtypeStruct((t, kvw), F32),
                   jax.ShapeDtypeStruct((bl, 4, 8, LANES), F32), jax.ShapeDtypeStruct((bl, 4, 8, LANES), F32),
                   jax.ShapeDtypeStruct((bl, 8, QTILE, bw), F32), jax.ShapeDtypeStruct((bl, 4, 8, LANES), F32)],
        in_specs=[blk(qmap), blk(kmap), blk(vmap), vec, vec,
                  pl.BlockSpec((2, QTILE, bw), lambda b, p: (p, 0, 0)),
                  pl.BlockSpec(memory_space=pltpu.SMEM),
                  blk(lambda b, p: (b, p)), blk(lambda b, p: (b, p))],
        out_specs=[blk(lambda b, p: (b, p)), blk(kvmap), blk(kvmap), small, small,
                   pl.BlockSpec((None, 2, QTILE, bw), lambda b, p: (b, p, 0, 0)), small],
        scratch_shapes=[pltpu.VMEM((s_len, LANES), BF16), pad_bf, pad_bf, pad_bf, pad_bf,
                        pltpu.VMEM((s_len, LANES), F32), pad_f32, pad_f32, pad_f32, pad_f32],
        compiler_params=_params(("arbitrary", "arbitrary")),
    )(qkv, qkv, qkv, gq, gk, bias, sinks, y, dy)


def _band_geometry(prev):
    bw = QTILE + prev * CHUNK
    i = np.arange(QTILE)[:, None]
    j = np.arange(bw)[None, :]
    dist = i + prev * CHUNK - j
    valid = (j // CHUNK >= i // CHUNK) & (j // CHUNK <= i // CHUNK + prev)
    return dist, valid


A_VAR0 = (A_PREV * CHUNK - A_MAX_REL) // LANES * LANES


A_NVAR = QTILE + A_PREV * CHUNK - A_VAR0


def _skew_rows(x, sign):
    rows, n = x.shape
    row = lax.broadcasted_iota(jnp.int32, x.shape, 0)
    b = 1
    while b < rows:
        x = jnp.where((row & b) != 0, pltpu.roll(x, (sign * b) % n, 1), x)
        b *= 2
    return x


def _rel_bias_expand(table, name):
    _, valid = _band_geometry(A_PREV)
    bw = valid.shape[1]
    valid_f = jnp.asarray(valid.astype(np.float32))
    rev = jnp.flip(table[:, 1:], axis=1).reshape(A_HEADS, 1, A_NVAR)

    def body(rev_ref, valid_ref, o_ref):
        rowv = jnp.broadcast_to(rev_ref[...], (QTILE, A_NVAR))
        top = rowv[:, 0:1]
        var = _skew_rows(rowv, 1)
        row = lax.broadcasted_iota(jnp.int32, (QTILE, A_NVAR), 0)
        colv = lax.broadcasted_iota(jnp.int32, (QTILE, A_NVAR), 1)
        var = jnp.where(colv < row, top, var)
        ok = valid_ref[...] > 0.5
        o_ref[:, :A_VAR0] = jnp.where(ok[:, :A_VAR0], top, NEG_INF)
        o_ref[:, A_VAR0:] = jnp.where(ok[:, A_VAR0:], var, NEG_INF)

    return pl.pallas_call(
        body, name=name, grid=(A_HEADS,),
        out_shape=jax.ShapeDtypeStruct((A_HEADS, QTILE, bw), F32),
        in_specs=[pl.BlockSpec((None, 1, A_NVAR), lambda h: (h, 0, 0)), pl.BlockSpec((QTILE, bw), lambda h: (0, 0))],
        out_specs=pl.BlockSpec((None, QTILE, bw), lambda h: (h, 0, 0)),
        compiler_params=_params(("arbitrary",)),
    )(rev, valid_f)


def _rel_bias_grad(dbias, name):
    bl = dbias.shape[0]
    bw = dbias.shape[-1]

    def body(db_ref, o_ref):
        g = db_ref[0]
        for b in range(1, bl):
            g = g + db_ref[b]
        sk = _skew_rows(g[:, A_VAR0:], -1)
        row = lax.broadcasted_iota(jnp.int32, (QTILE, A_NVAR), 0)
        colv = lax.broadcasted_iota(jnp.int32, (QTILE, A_NVAR), 1)
        wrapped = (row + colv) >= A_NVAR
        main = jnp.sum(jnp.where(wrapped, 0.0, sk), axis=0, keepdims=True)
        top = jnp.sum(g[:, :A_VAR0]) + jnp.sum(jnp.where(wrapped, sk, 0.0))
        o_ref[:, :A_NVAR] = jnp.broadcast_to(main, (8, A_NVAR))
        o_ref[:, A_NVAR:] = jnp.full((8, LANES), top, F32)

    out = pl.pallas_call(
        body, name=name, grid=(A_HEADS,),
        out_shape=jax.ShapeDtypeStruct((A_HEADS, 8, A_NVAR + LANES), F32),
        in_specs=[pl.BlockSpec((bl, None, QTILE, bw), lambda h: (0, h, 0, 0))],
        out_specs=pl.BlockSpec((None, 8, A_NVAR + LANES), lambda h: (h, 0, 0)),
        compiler_params=_params(("arbitrary",)),
    )(dbias)
    main, top = out[:, 0, :A_NVAR], out[:, 0, A_NVAR]
    fm = jnp.flip(main, axis=1)
    return jnp.concatenate([jnp.zeros((A_HEADS, 1), F32), fm[:, :-1], fm[:, -1:] + top[:, None]], axis=1)


def _alibi_bias():
    dist, valid = _band_geometry(B_PREV)
    slopes = np.array([2.0 ** (-8.0 * (h + 1) / B_Q_HEADS) for h in range(B_Q_HEADS)], dtype=np.float32)
    bias = -slopes[:, None, None] * np.abs(dist).astype(np.float32)[None]
    return jnp.asarray(np.where(valid[None], bias, np.float32(NEG_INF)).astype(np.float32))


SMALL_NAMES = ("ffn1_norm", "mix_norm", "ffn2_norm", "ple_norm", "a_q_norm", "a_k_norm", "b_q_norm", "b_k_norm",
               "a_rel_bias", "b_sinks", "loss")


def _pack_small(vals):
    rows = []
    for nme in SMALL_NAMES:
        v = vals[nme].astype(F32)
        if nme == "a_rel_bias":
            v = jnp.pad(v.reshape(A_HEADS, -1), ((0, 0), (0, 3 * LANES - (2 * A_MAX_REL + 1))))
        v = v.reshape(-1)
        v = jnp.pad(v, (0, (-v.shape[0]) % LANES))
        rows.append(v.reshape(-1, LANES))
    out = jnp.concatenate(rows, axis=0)
    return jnp.pad(out, ((0, (-out.shape[0]) % 8), (0, 0)))


def _unpack_small(packed, shapes):
    out, r = {}, 0
    for nme in SMALL_NAMES:
        shp = shapes[nme]
        if nme == "a_rel_bias":
            nr = A_HEADS * 3
            out[nme] = packed[r:r + nr].reshape(A_HEADS, 3 * LANES)[:, :2 * A_MAX_REL + 1].reshape(shp)
        else:
            size = int(np.prod(shp)) if shp else 1
            nr = -(-size // LANES)
            out[nme] = packed[r:r + nr].reshape(-1)[:size].reshape(shp)
        r += nr
    return out


BIG_NAMES = ("ffn1_w_gu", "ffn1_w_down", "w_in", "w_gate", "w_proj_a", "w_proj_b", "w_out",
             "ffn2_w_gu", "ffn2_w_down", "w_ple_gate", "w_ple_proj")
ROW_SHARDED = ("ffn1_w_down", "ffn2_w_down", "w_out", "w_ple_gate")
WEIGHT_ORDER = ("ffn1_norm", "ffn1_w_gu", "ffn1_w_down", "mix_norm", "w_in", "a_q_norm", "a_k_norm", "a_rel_bias",
                "b_q_norm", "b_k_norm", "b_sinks", "w_gate", "w_proj_a", "w_proj_b", "w_out", "ffn2_norm",
                "ffn2_w_gu", "ffn2_w_down", "ple_norm", "w_ple_gate", "w_ple_proj")


def _full_cols(wg):
    nb, k, n = wg.shape
    return jnp.transpose(wg, (1, 0, 2)).reshape(k, nb * n)


def _col_blocks(g, nb):
    k, n = g.shape
    return jnp.transpose(g.reshape(k, nb, n // nb), (1, 0, 2))


def _step(x, p, target, w, m, v):
    bl, s_len, d = x.shape
    t = bl * s_len
    h0 = x.reshape(t, d)
    pt = p.reshape(t, p.shape[-1])
    tgt = target.reshape(t, d)

    shards = [w[nme][0].astype(BF16) for nme in BIG_NAMES]
    gathered = dict(zip(BIG_NAMES, _all_gather(shards, "weights_all_gather")))
    wgu1 = gathered["ffn1_w_gu"]
    wgu2 = gathered["ffn2_w_gu"]
    nf = wgu1.shape[2]
    wd1 = gathered["ffn1_w_down"].reshape(N_DEV // 2, nf, d)
    wd2 = gathered["ffn2_w_down"].reshape(N_DEV // 2, nf, d)
    win = _full_cols(gathered["w_in"])
    wgate = _full_cols(gathered["w_gate"])
    wpa = _full_cols(gathered["w_proj_a"])
    wpb = _full_cols(gathered["w_proj_b"])
    wpe = _full_cols(gathered["w_ple_proj"])
    wout = gathered["w_out"].reshape(d, d)
    wpg = gathered["w_ple_gate"].reshape(d, d)

    g_ffn1, g_mix, g_ffn2, g_ple = w["ffn1_norm"], w["mix_norm"], w["ffn2_norm"], w["ple_norm"]
    tile2 = lambda a: jnp.tile(a.reshape(1, HEAD_DIM), (1, 2))
    gqa, gka, gqb, gkb = tile2(w["a_q_norm"]), tile2(w["a_k_norm"]), tile2(w["b_q_norm"]), tile2(w["b_k_norm"])
    sinks = w["b_sinks"].reshape(B_Q_HEADS)
    bias_a = _rel_bias_expand(w["a_rel_bias"][0], "rel_bias_expand")
    bias_b = _alibi_bias()

    h1, gu1 = _ffn_fwd(h0, g_ffn1, wgu1, wd1, "ffn1_fwd")
    un, qkv, gate = _proj_fwd(h1, g_mix, win, wgate, "proj_fwd")
    ya = _attn_fwd("A", qkv, gqa, gka, bias_a, sinks, bl, s_len, "attn_a_fwd")
    yb = _attn_fwd("B", qkv, gqb, gkb, bias_b, sinks, bl, s_len, "attn_b_fwd")
    h2, merged, pa, pb = _merge_fwd(h1, ya, yb, gate, wpa, wpb, wout, "merge_fwd")
    h3, gu2 = _ffn_fwd(h2, g_ffn2, wgu2, wd2, "ffn2_fwd")
    dh3, dz4, dpp, n4, dg_ple, loss_part = _ple_loss(h3, g_ple, pt, tgt, wpg, wpe, "ple_loss")

    grads = {}
    grads["w_ple_gate"] = _dw(n4, dz4, 1, d, "dw_ple_gate").reshape(N_DEV, d // N_DEV, d)
    grads["w_ple_proj"] = _dw(pt, dpp, N_DEV, d // N_DEV, "dw_ple_proj")

    dh2, dgu2, a2, n3, dg_ffn2 = _ffn_bwd(dh3, h2, g_ffn2, gu2, wgu2, wd2, "ffn2_bwd")
    grads["ffn2_w_gu"] = _dw(n3, dgu2, N_DEV, nf, "dw_ffn2_gu")
    grads["ffn2_w_down"] = _dw(a2, dh3, N_DEV // 2, d, "dw_ffn2_down", 0.5)

    dpa, dpb, dzg, dya, dyb = _merge_bwd(dh2, pa, pb, gate, wpa, wpb, wout, "merge_bwd")
    grads["w_out"] = _dw(merged, dh2, 1, d, "dw_out").reshape(N_DEV, d // N_DEV, d)
    grads["w_proj_a"] = _dw(ya, dpa, N_DEV, d // N_DEV, "dw_proj_a")
    grads["w_proj_b"] = _dw(yb, dpb, N_DEV, d // N_DEV, "dw_proj_b")
    grads["w_gate"] = _dw(un, dzg, N_DEV, 2 * d // N_DEV, "dw_gate")

    dqa, dka, dva, dgqa, dgka, dbias, _ = _attn_bwd("A", qkv, gqa, gka, bias_a, sinks, ya, dya, bl, s_len, "attn_a_bwd")
    dqb, dkb, dvb, dgqb, dgkb, _, dsink = _attn_bwd("B", qkv, gqb, gkb, bias_b, sinks, yb, dyb, bl, s_len, "attn_b_bwd")
    dqkv = jnp.concatenate([dqa, dka, dva, dqb, dkb, dvb], axis=1)
    dtab = _rel_bias_grad(dbias, "rel_bias_grad")

    dh1, dg_mix = _proj_bwd(dh2, h1, g_mix, dzg, dqkv, win, wgate, "proj_bwd")
    grads["w_in"] = _col_blocks(_dw(un, dqkv, 1, IN_COLS, "dw_in")[0], N_DEV)

    dh0, dgu1, a1, n1, dg_ffn1 = _ffn_bwd(dh1, h0, g_ffn1, gu1, wgu1, wd1, "ffn1_bwd")
    grads["ffn1_w_gu"] = _dw(n1, dgu1, N_DEV, nf, "dw_ffn1_gu")
    grads["ffn1_w_down"] = _dw(a1, dh1, N_DEV // 2, d, "dw_ffn1_down", 0.5)
    return dh0, loss_part, grads, (dg_ffn1, dg_mix, dg_ffn2, dg_ple, dgqa, dgka, dgqb, dgkb, dtab, dsink)


def _down_grad_layout(g, d):
    return g.reshape(N_DEV, -1, d)


def kernel(x, p, ffn1_norm, ffn1_w_gu, ffn1_w_down, mix_norm, w_in, a_q_norm, a_k_norm, a_rel_bias, b_q_norm, b_k_norm, b_sinks, w_gate, w_proj_a, w_proj_b, w_out, ffn2_norm, ffn2_w_gu, ffn2_w_down, ple_norm, w_ple_gate, w_ple_proj, loss_target, m_ffn1_norm, m_ffn1_w_gu, m_ffn1_w_down, m_mix_norm, m_w_in, m_a_q_norm, m_a_k_norm, m_a_rel_bias, m_b_q_norm, m_b_k_norm, m_b_sinks, m_w_gate, m_w_proj_a, m_w_proj_b, m_w_out, m_ffn2_norm, m_ffn2_w_gu, m_ffn2_w_down, m_ple_norm, m_w_ple_gate, m_w_ple_proj, v_ffn1_norm, v_ffn1_w_gu, v_ffn1_w_down, v_mix_norm, v_w_in, v_a_q_norm, v_a_k_norm, v_a_rel_bias, v_b_q_norm, v_b_k_norm, v_b_sinks, v_w_gate, v_w_proj_a, v_w_proj_b, v_w_out, v_ffn2_norm, v_ffn2_w_gu, v_ffn2_w_down, v_ple_norm, v_w_ple_gate, v_w_ple_proj):
    w = dict(ffn1_norm=ffn1_norm, ffn1_w_gu=ffn1_w_gu, ffn1_w_down=ffn1_w_down, mix_norm=mix_norm, w_in=w_in,
             a_q_norm=a_q_norm, a_k_norm=a_k_norm, a_rel_bias=a_rel_bias, b_q_norm=b_q_norm, b_k_norm=b_k_norm,
             b_sinks=b_sinks, w_gate=w_gate, w_proj_a=w_proj_a, w_proj_b=w_proj_b, w_out=w_out, ffn2_norm=ffn2_norm,
             ffn2_w_gu=ffn2_w_gu, ffn2_w_down=ffn2_w_down, ple_norm=ple_norm, w_ple_gate=w_ple_gate,
             w_ple_proj=w_ple_proj)
    m = dict(ffn1_norm=m_ffn1_norm, ffn1_w_gu=m_ffn1_w_gu, ffn1_w_down=m_ffn1_w_down, mix_norm=m_mix_norm,
             w_in=m_w_in, a_q_norm=m_a_q_norm, a_k_norm=m_a_k_norm, a_rel_bias=m_a_rel_bias, b_q_norm=m_b_q_norm,
             b_k_norm=m_b_k_norm, b_sinks=m_b_sinks, w_gate=m_w_gate, w_proj_a=m_w_proj_a, w_proj_b=m_w_proj_b,
             w_out=m_w_out, ffn2_norm=m_ffn2_norm, ffn2_w_gu=m_ffn2_w_gu, ffn2_w_down=m_ffn2_w_down,
             ple_norm=m_ple_norm, w_ple_gate=m_w_ple_gate, w_ple_proj=m_w_ple_proj)
    v = dict(ffn1_norm=v_ffn1_norm, ffn1_w_gu=v_ffn1_w_gu, ffn1_w_down=v_ffn1_w_down, mix_norm=v_mix_norm,
             w_in=v_w_in, a_q_norm=v_a_q_norm, a_k_norm=v_a_k_norm, a_rel_bias=v_a_rel_bias, b_q_norm=v_b_q_norm,
             b_k_norm=v_b_k_norm, b_sinks=v_b_sinks, w_gate=v_w_gate, w_proj_a=v_w_proj_a, w_proj_b=v_w_proj_b,
             w_out=v_w_out, ffn2_norm=v_ffn2_norm, ffn2_w_gu=v_ffn2_w_gu, ffn2_w_down=v_ffn2_w_down,
             ple_norm=v_ple_norm, w_ple_gate=v_w_ple_gate, w_ple_proj=v_w_ple_proj)
    bl, s_len, d = x.shape

    dh0, loss_part, grads, smalls = _step(x, p[0], loss_target, w, m, v)
    dg_ffn1, dg_mix, dg_ffn2, dg_ple, dgqa, dgka, dgqb, dgkb, dtab, dsink = smalls

    xi, yi, ci = _place()
    core = jnp.stack([ci]).astype(jnp.int32)
    place = jnp.stack([2 * xi + yi, ci]).astype(jnp.int32)
    g4 = []
    for nme in BIG_NAMES:
        g = grads[nme]
        if nme in ("ffn1_w_down", "ffn2_w_down"):
            g = _down_grad_layout(g, d)
        g4.append(g.reshape((4, 2) + g.shape[1:]))
    r1 = _sibling_exchange(g4, "grads_sibling_exchange")
    parts = [_pair_sum(g, r, core, "pair_sum_" + nme) for nme, g, r in zip(BIG_NAMES, g4, r1)]
    r2 = _chip_exchange(parts, "grads_chip_exchange")
    big = {}
    for nme, g, ra, rb in zip(BIG_NAMES, g4, r1, r2):
        outs = _final_adam(g, ra, rb, w[nme][0], m[nme][0], v[nme][0], place, "adam_" + nme)
        big[nme] = [o[None] for o in outs]

    fold = lambda a: (a[:, :, 0, :HEAD_DIM] + a[:, :, 0, HEAD_DIM:]).sum(axis=(0, 1))
    small_part = dict(
        ffn1_norm=dg_ffn1, mix_norm=dg_mix, ffn2_norm=dg_ffn2, ple_norm=dg_ple,
        a_q_norm=fold(dgqa), a_k_norm=fold(dgka), b_q_norm=fold(dgqb), b_k_norm=fold(dgkb),
        a_rel_bias=dtab,
        b_sinks=dsink.sum(axis=0)[:, 0, :2].reshape(B_Q_HEADS),
        loss=loss_part[0, :1])
    zero1 = jnp.zeros((1,), F32)
    shapes = {nme: w[nme].shape for nme in SMALL_NAMES if nme != "loss"}
    shapes["loss"] = ()
    pk = lambda src: _pack_small({**{nme: src[nme] for nme in SMALL_NAMES if nme != "loss"}, "loss": zero1})
    sg, sd, sm, sv = _small_allreduce_adam(_pack_small(small_part), pk(w), pk(m), pk(v), "small_allreduce_adam")
    sg, sd, sm, sv = (_unpack_small(a, shapes) for a in (sg, sd, sm, sv))

    def pick(i):
        out = []
        for nme in WEIGHT_ORDER:
            out.append(big[nme][i] if nme in big else (sg, sd, sm, sv)[i][nme])
        return out

    return (sg["loss"], dh0.reshape(bl, s_len, d), *pick(0), *pick(1), *pick(2), *pick(3))
```

```python
import functools

import jax
import jax.numpy as jnp
import numpy as np
from jax import lax
from jax.experimental import pallas as pl
from jax.experimental.pallas import tpu as pltpu

F32 = jnp.float32
BF16 = jnp.bfloat16

CHUNK = 64
HEAD_DIM = 64
A_HEADS = 8
A_PREV = 8
A_MAX_REL = 128
B_Q_HEADS = 8
B_KV_HEADS = 2
B_PREV = 2
A_WIDTH = A_HEADS * HEAD_DIM
B_Q_WIDTH = B_Q_HEADS * HEAD_DIM
B_KV_WIDTH = B_KV_HEADS * HEAD_DIM
IN_COLS = 3 * A_WIDTH + B_Q_WIDTH + 2 * B_KV_WIDTH
EPS = 1e-6
NEG_INF = -1e30
ADAM_LR = 0.001
ADAM_B1 = 0.9
ADAM_B2 = 0.999
ADAM_EPS = 1e-08
ADAM_WD = 0.01
ADAM_STEP = 10

N_DEV = 8
LANES = 128
QTILE = 2 * CHUNK
VMEM_LIMIT = 56 * 1024 * 1024

MESH_ID = pl.DeviceIdType.MESH
ANY = pl.BlockSpec(memory_space=pl.ANY)
HBM = pl.BlockSpec(memory_space=pltpu.HBM)
SEM = pl.BlockSpec(memory_space=pltpu.SEMAPHORE)
SIDE_EFFECT = pltpu.SideEffectType.DATAFLOW_SIDE_EFFECTING


def _dot(a, b):
    return jnp.dot(a, b, preferred_element_type=F32)


def _dot_nt(a, b):
    return lax.dot_general(a, b, (((1,), (1,)), ((), ())), preferred_element_type=F32)


def _dot_tn(a, b):
    return lax.dot_general(a, b, (((0,), (0,)), ((), ())), preferred_element_type=F32)


def _params(sem=None, vmem=VMEM_LIMIT):
    return pltpu.CompilerParams(dimension_semantics=sem, vmem_limit_bytes=vmem)


def _row_tile(t, want):
    while t % want:
        want //= 2
    return want


def _place():
    return lax.axis_index("x"), lax.axis_index("y"), lax.axis_index("c")


def _all_gather(shards, name):
    n = len(shards)

    def body(*refs):
        ins, outs = refs[:n], refs[n:2 * n]
        send_sems, recv_sems, local_sems = refs[2 * n:]
        x, y, c = _place()
        me, sib = (x, y, c), (x, y, 1 - c)
        chips = [(1 - x, y), (x, 1 - y), (1 - x, 1 - y)]

        def copy(w, k, block, to, src=None):
            px, py, pc = block
            dst = outs[w].at[4 * px + 2 * py + pc]
            return pltpu.make_async_remote_copy(
                src_ref=dst if src is None else src, dst_ref=dst,
                send_sem=send_sems.at[w * 7 + k], recv_sem=recv_sems.at[w * 7 + k],
                device_id=to, device_id_type=MESH_ID)

        mine = [pltpu.make_async_copy(ins[w], outs[w].at[4 * x + 2 * y + c], local_sems.at[w]) for w in range(n)]
        for cp in mine:
            cp.start()
        first = []
        for w in range(n):
            first.append(copy(w, 0, me, sib, src=ins[w]))
            first += [copy(w, 1 + j, me, (*chip, c), src=ins[w]) for j, chip in enumerate(chips)]
        for cp in first:
            cp.start()
        passed = []
        for j, chip in enumerate(chips):
            for w in range(n):
                copy(w, 1 + j, (*chip, c), me).wait_recv()
                fwd = copy(w, 4 + j, (*chip, c), sib)
                fwd.start()
                passed.append(fwd)
        for w in range(n):
            copy(w, 0, sib, me).wait_recv()
        for j, chip in enumerate(chips):
            for w in range(n):
                copy(w, 4 + j, (*chip, 1 - c), me).wait_recv()
        for cp in first + passed:
            cp.wait_send()
        for cp in mine:
            cp.wait()

    return pl.pallas_call(
        body, name=name,
        out_shape=[jax.ShapeDtypeStruct((N_DEV,) + s.shape, s.dtype) for s in shards],
        in_specs=[ANY] * n, out_specs=[ANY] * n,
        scratch_shapes=[pltpu.SemaphoreType.DMA((7 * n,)), pltpu.SemaphoreType.DMA((7 * n,)),
                        pltpu.SemaphoreType.DMA((n,))],
    )(*shards)


def _scatter_copies(parts, lands, send_sems, recv_sems):
    x, y, c = _place()
    cps = []
    for w, (part, land) in enumerate(zip(parts, lands)):
        for k in range(1, N_DEV):
            px, py, pc = x ^ ((k >> 2) & 1), y ^ ((k >> 1) & 1), c ^ (k & 1)
            cps.append(pltpu.make_async_remote_copy(
                src_ref=part.at[4 * px + 2 * py + pc], dst_ref=land.at[k - 1],
                send_sem=send_sems.at[7 * w + k - 1], recv_sem=recv_sems.at[7 * w + k - 1],
                device_id=(px, py, pc), device_id_type=MESH_ID))
    return cps


def _scatter_start(parts, after, name):
    n = len(parts)

    def body(*refs):
        ins, lands = refs[:n], refs[n:2 * n]
        send_sems, recv_sems = refs[2 * n + 1], refs[2 * n + 2]
        token = refs[-1]
        for cp in _scatter_copies(ins, lands, send_sems, recv_sems):
            cp.start()
        token[...] = jnp.zeros_like(token)

    land_shapes = [(N_DEV - 1,) + p.shape[1:] for p in parts]
    in_hbm = [pltpu.with_memory_space_constraint(p, pltpu.HBM) for p in parts]
    in_hbm += [pltpu.with_memory_space_constraint(lax.empty(s, p.dtype), pltpu.HBM) for s, p in zip(land_shapes, parts)]
    outs = pl.pallas_call(
        body, name=name,
        out_shape=(pltpu.SemaphoreType.DMA((7 * n,)), pltpu.SemaphoreType.DMA((7 * n,)),
                   *[pltpu.HBM(p.shape, p.dtype) for p in parts],
                   *[pltpu.HBM(s, p.dtype) for s, p in zip(land_shapes, parts)],
                   jax.ShapeDtypeStruct((8, LANES), F32)),
        in_specs=[HBM] * (2 * n) + [ANY],
        out_specs=(SEM, SEM, *[HBM] * (2 * n), pl.BlockSpec(memory_space=pltpu.VMEM)),
        input_output_aliases={i: 2 + i for i in range(2 * n)},
        compiler_params=pltpu.CompilerParams(has_side_effects=SIDE_EFFECT),
    )(*in_hbm, after)
    return outs[0], outs[1], list(outs[2:2 + n]), list(outs[2 + n:2 + 2 * n]), outs[-1]


def _scatter_wait(send_sems, recv_sems, parts, lands, after, name):
    n = len(parts)

    def body(*refs):
        ins, lnd = refs[:n], refs[n:2 * n]
        for cp in _scatter_copies(ins, lnd, refs[2 * n], refs[2 * n + 1]):
            cp.wait_send()
            cp.wait_recv()

    outs = pl.pallas_call(
        body, name=name,
        out_shape=tuple(pltpu.HBM(a.shape, a.dtype) for a in parts + lands),
        in_specs=[HBM] * (2 * n) + [SEM, SEM, ANY],
        out_specs=tuple([HBM] * (2 * n)),
        input_output_aliases={i: i for i in range(2 * n)},
        compiler_params=pltpu.CompilerParams(has_side_effects=SIDE_EFFECT),
    )(*parts, *lands, send_sems, recv_sems, after)
    return list(outs[n:])


def _adam(w, g, m, v):
    m2 = ADAM_B1 * m + (1.0 - ADAM_B1) * g
    v2 = ADAM_B2 * v + (1.0 - ADAM_B2) * (g * g)
    m_hat = m2 / (1.0 - ADAM_B1 ** ADAM_STEP)
    v_hat = v2 / (1.0 - ADAM_B2 ** ADAM_STEP)
    delta = -ADAM_LR * (m_hat / (jnp.sqrt(v_hat) + ADAM_EPS) + ADAM_WD * w)
    return delta, m2, v2


def _small_allreduce_adam(part, w, m, v, name):
    rows = part.shape[0]

    def body(p_ref, w_ref, m_ref, v_ref, g_ref, d_ref, mo_ref, vo_ref, buf, send_sems, recv_sems):
        x, y, c = _place()
        buf[0] = p_ref[...]
        cps = []
        for k in range(1, N_DEV):
            kx, ky, kc = (k >> 2) & 1, (k >> 1) & 1, k & 1
            peer = (x ^ kx, y ^ ky, c ^ kc)
            cps.append(pltpu.make_async_remote_copy(
                src_ref=p_ref, dst_ref=buf.at[k], send_sem=send_sems.at[k - 1], recv_sem=recv_sems.at[k - 1],
                device_id=peer, device_id_type=MESH_ID))
        for cp in cps:
            cp.start()
        for cp in cps:
            cp.wait()
        me = 4 * x + 2 * y + c
        total = buf[me]
        for d in range(1, N_DEV):
            total = total + buf[d ^ me]
        g_ref[...] = total
        delta, m2, v2 = _adam(w_ref[...], total, m_ref[...], v_ref[...])
        d_ref[...] = delta
        mo_ref[...] = m2
        vo_ref[...] = v2

    vm = pl.BlockSpec(memory_space=pltpu.VMEM)
    return pl.pallas_call(
        body, name=name,
        out_shape=[jax.ShapeDtypeStruct(part.shape, F32)] * 4,
        in_specs=[vm] * 4, out_specs=[vm] * 4,
        scratch_shapes=[pltpu.VMEM((N_DEV, rows, LANES), F32),
                        pltpu.SemaphoreType.DMA((N_DEV - 1,)), pltpu.SemaphoreType.DMA((N_DEV - 1,))],
    )(part, w, m, v)


def _final_adam(g8, land, w, m, v, me, name):
    _, r, c = g8.shape
    tr = _row_tile(r, 256) if r % 8 == 0 and r > 512 else r

    def body(me_ref, g_ref, land_ref, w_ref, m_ref, v_ref, go_ref, d_ref, mo_ref, vo_ref):
        g = g_ref[...]
        for k in range(N_DEV - 1):
            g = g + land_ref[k].astype(F32)
        go_ref[...] = g
        delta, m2, v2 = _adam(w_ref[...], g, m_ref[...], v_ref[...])
        d_ref[...] = delta
        mo_ref[...] = m2
        vo_ref[...] = v2

    plain = pl.BlockSpec((tr, c), lambda i, s: (i, 0))
    return pl.pallas_call(
        body, name=name,
        out_shape=[jax.ShapeDtypeStruct((r, c), F32)] * 4,
        grid_spec=pltpu.PrefetchScalarGridSpec(
            num_scalar_prefetch=1, grid=(r // tr,),
            in_specs=[pl.BlockSpec((None, tr, c), lambda i, s: (s[0], i, 0)),
                      pl.BlockSpec((N_DEV - 1, tr, c), lambda i, s: (0, i, 0)),
                      plain, plain, plain],
            out_specs=[plain] * 4),
        compiler_params=_params(("arbitrary",)),
    )(me, g8, land, w, m, v)


def _rms(x, gain):
    r = lax.rsqrt(jnp.mean(x * x, axis=-1, keepdims=True) + EPS)
    xh = x * r
    return xh * gain, xh, r


def _rms_bwd(xh, r, gain, dy):
    gdy = gain * dy
    dx = r * (gdy - xh * jnp.mean(xh * gdy, axis=-1, keepdims=True))
    return dx, jnp.sum(dy * xh, axis=0, keepdims=True)


def _load_weights(pairs, sems):
    cps = [pltpu.make_async_copy(src, dst, sems.at[i]) for i, (src, dst) in enumerate(pairs)]
    for cp in cps:
        cp.start()
    for cp in cps:
        cp.wait()


def _ffn_fwd(h, gain, wgu, wd, name):
    t, d = h.shape
    nb, _, nf = wgu.shape
    nh = nb // 2
    tm = _row_tile(t, 256)

    def body(h_ref, g_ref, wgu_hbm, wd_hbm, out_ref, gu_ref, wgu_v, wd_v, sems):
        @pl.when(pl.program_id(0) == 0)
        def _():
            _load_weights([(wgu_hbm, wgu_v), (wd_hbm, wd_v)], sems)

        x = h_ref[...]
        n, _, _ = _rms(x, g_ref[...])
        nbf = n.astype(BF16)
        acc = jnp.zeros((tm, d), F32)
        for j in range(nh):
            g = _dot(nbf, wgu_v[j])
            u = _dot(nbf, wgu_v[j + nh])
            gu_ref[j] = g.astype(BF16)
            gu_ref[j + nh] = u.astype(BF16)
            a = (g * jax.nn.sigmoid(g)) * u
            acc = acc + _dot(a.astype(BF16), wd_v[j])
        out_ref[...] = x + 0.5 * acc

    return pl.pallas_call(
        body, name=name, grid=(t // tm,),
        out_shape=[jax.ShapeDtypeStruct((t, d), F32), jax.ShapeDtypeStruct((nb, t, nf), BF16)],
        in_specs=[pl.BlockSpec((tm, d), lambda i: (i, 0)), pl.BlockSpec((1, d), lambda i: (0, 0)), ANY, ANY],
        out_specs=[pl.BlockSpec((tm, d), lambda i: (i, 0)), pl.BlockSpec((nb, tm, nf), lambda i: (0, i, 0))],
        scratch_shapes=[pltpu.VMEM(wgu.shape, BF16), pltpu.VMEM(wd.shape, BF16), pltpu.SemaphoreType.DMA((2,))],
        compiler_params=_params(("arbitrary",)),
    )(h, gain, wgu, wd)


def _ffn_bwd(dh, h, gain, gu, wgu, wd, name):
    t, d = h.shape
    nb, _, nf = wgu.shape
    nh = nb // 2
    tm = _row_tile(t, 256)

    def body(dh_ref, h_ref, g_ref, gu_ref, wgu_hbm, wd_hbm, dhp_ref, dgu_ref, a_ref, n_ref, dgain_ref,
             wgu_v, wd_v, sems):
        @pl.when(pl.program_id(0) == 0)
        def _():
            _load_weights([(wgu_hbm, wgu_v), (wd_hbm, wd_v)], sems)
            dgain_ref[...] = jnp.zeros_like(dgain_ref)

        x = h_ref[...]
        gain_v = g_ref[...]
        n, xh, r = _rms(x, gain_v)
        n_ref[...] = n.astype(BF16)
        dh_v = dh_ref[...]
        dfb = (0.5 * dh_v).astype(BF16)
        dn = jnp.zeros((tm, d), F32)
        for j in range(nh):
            da = _dot_nt(dfb, wd_v[j])
            g = gu_ref[j].astype(F32)
            u = gu_ref[j + nh].astype(F32)
            sg = jax.nn.sigmoid(g)
            si = g * sg
            dg = (da * u * (sg * (1.0 + g * (1.0 - sg)))).astype(BF16)
            du = (da * si).astype(BF16)
            a_ref[j] = (si * u).astype(BF16)
            dgu_ref[j] = dg
            dgu_ref[j + nh] = du
            dn = dn + _dot_nt(dg, wgu_v[j]) + _dot_nt(du, wgu_v[j + nh])
        dx, dgain = _rms_bwd(xh, r, gain_v, dn)
        dhp_ref[...] = dh_v + dx
        dgain_ref[...] += dgain

    row = pl.BlockSpec((tm, d), lambda i: (i, 0))
    vec = pl.BlockSpec((1, d), lambda i: (0, 0))
    return pl.pallas_call(
        body, name=name, grid=(t // tm,),
        out_shape=[jax.ShapeDtypeStruct((t, d), F32), jax.ShapeDtypeStruct((nb, t, nf), BF16),
                   jax.ShapeDtypeStruct((nh, t, nf), BF16), jax.ShapeDtypeStruct((t, d), BF16),
                   jax.ShapeDtypeStruct((1, d), F32)],
        in_specs=[row, row, vec, pl.BlockSpec((nb, tm, nf), lambda i: (0, i, 0)), ANY, ANY],
        out_specs=[row, pl.BlockSpec((nb, tm, nf), lambda i: (0, i, 0)),
                   pl.BlockSpec((nh, tm, nf), lambda i: (0, i, 0)), row, vec],
        scratch_shapes=[pltpu.VMEM(wgu.shape, BF16), pltpu.VMEM(wd.shape, BF16), pltpu.SemaphoreType.DMA((2,))],
        compiler_params=_params(("arbitrary",)),
    )(dh, h, gain, gu, wgu, wd)


def _dw(xa, dy, nb, n, name, scale=1.0, dep=None):
    t, k = xa.shape[-2:]
    tt = _row_tile(t, 512)
    steps = t // tt
    wide = dy.ndim == 2 and xa.ndim == 2
    if xa.ndim == 3:
        x_spec = pl.BlockSpec((nb, tt, k), lambda i: (0, i, 0))
    else:
        x_spec = pl.BlockSpec((tt, k), lambda i: (i, 0))
    if dy.ndim == 3:
        dy_spec = pl.BlockSpec((nb, tt, n), lambda i: (0, i, 0))
    else:
        dy_spec = pl.BlockSpec((tt, dy.shape[1]), lambda i: (i, 0))
    acc_shape = (k, nb * n) if wide else (nb, k, n)
    stage_shape = (k, nb * n) if wide else (k, n)

    def body(x_ref, dy_ref, *rest):
        o_hbm, ob_hbm, acc, stage, sems = rest[-5:]

        @pl.when(pl.program_id(0) == 0)
        def _():
            acc[...] = jnp.zeros_like(acc)

        if wide:
            acc[...] += _dot(x_ref[...].astype(BF16).T, dy_ref[...].astype(BF16))
        elif xa.ndim == 2:
            xt = x_ref[...].astype(BF16).T
            for j in range(nb):
                acc[j] += _dot(xt, dy_ref[j].astype(BF16))
        else:
            dyb = dy_ref[...].astype(BF16)
            for j in range(nb):
                acc[j] += _dot_tn(x_ref[j].astype(BF16), dyb)

        @pl.when(pl.program_id(0) == steps - 1)
        def _():
            if scale != 1.0:
                acc[...] = acc[...] * scale
            if wide:
                cps = [pltpu.make_async_copy(acc.at[:, pl.ds(j * n, n)] if nb > 1 else acc, o_hbm.at[j], sems.at[j])
                       for j in range(nb)]
            else:
                cps = [pltpu.make_async_copy(acc, o_hbm, sems.at[0])]
            for cp in cps:
                cp.start()
            if wide:
                stage[...] = acc[...].astype(BF16)
                bcs = [pltpu.make_async_copy(stage.at[:, pl.ds(j * n, n)] if nb > 1 else stage, ob_hbm.at[j],
                                             sems.at[nb + j]) for j in range(nb)]
                for cp in bcs:
                    cp.start()
                for cp in bcs:
                    cp.wait()
            else:
                for j in range(nb):
                    stage[...] = acc[j].astype(BF16)
                    cp = pltpu.make_async_copy(stage, ob_hbm.at[j], sems.at[nb])
                    cp.start()
                    cp.wait()
            for cp in cps:
                cp.wait()

    return pl.pallas_call(
        body, name=name, grid=(steps,),
        out_shape=[jax.ShapeDtypeStruct((nb, k, n), F32), jax.ShapeDtypeStruct((nb, k, n), BF16)],
        in_specs=[x_spec, dy_spec] + ([] if dep is None else [ANY]),
        out_specs=[ANY, ANY],
        scratch_shapes=[pltpu.VMEM(acc_shape, F32), pltpu.VMEM(stage_shape, BF16),
                        pltpu.SemaphoreType.DMA((2 * nb,))],
        compiler_params=_params(("arbitrary",)),
    )(*((xa, dy) if dep is None else (xa, dy, dep)))


def _proj_fwd(h, gain, win, wgate, name):
    t, d = h.shape
    tm = _row_tile(t, 256)
    nq, ng = win.shape[1], wgate.shape[1]

    def body(h_ref, g_ref, win_ref, wg_ref, un_ref, qkv_ref, gate_ref):
        n, _, _ = _rms(h_ref[...], g_ref[...])
        nbf = n.astype(BF16)
        un_ref[...] = nbf
        qkv_ref[...] = _dot(nbf, win_ref[...])
        gate_ref[...] = jax.nn.sigmoid(_dot(nbf, wg_ref[...]))

    full = lambda a: pl.BlockSpec(a.shape, lambda i: (0,) * a.ndim)
    return pl.pallas_call(
        body, name=name, grid=(t // tm,),
        out_shape=[jax.ShapeDtypeStruct((t, d), BF16), jax.ShapeDtypeStruct((t, nq), F32),
                   jax.ShapeDtypeStruct((t, ng), F32)],
        in_specs=[pl.BlockSpec((tm, d), lambda i: (i, 0)), full(gain), full(win), full(wgate)],
        out_specs=[pl.BlockSpec((tm, d), lambda i: (i, 0)), pl.BlockSpec((tm, nq), lambda i: (i, 0)),
                   pl.BlockSpec((tm, ng), lambda i: (i, 0))],
        compiler_params=_params(("arbitrary",)),
    )(h, gain, win, wgate)


def _proj_bwd(dh, h, gain, dzg, dqkv, win, wgate, name):
    t, d = h.shape
    tm = _row_tile(t, 256)
    nq, ng = win.shape[1], wgate.shape[1]

    def body(dh_ref, h_ref, g_ref, dzg_ref, dqkv_ref, win_ref, wg_ref, dhp_ref, dgain_ref):
        @pl.when(pl.program_id(0) == 0)
        def _():
            dgain_ref[...] = jnp.zeros_like(dgain_ref)

        gain_v = g_ref[...]
        _, xh, r = _rms(h_ref[...], gain_v)
        dun = _dot_nt(dzg_ref[...], wg_ref[...]) + _dot_nt(dqkv_ref[...].astype(BF16), win_ref[...])
        dx, dgain = _rms_bwd(xh, r, gain_v, dun)
        dhp_ref[...] = dh_ref[...] + dx
        dgain_ref[...] += dgain

    full = lambda a: pl.BlockSpec(a.shape, lambda i: (0,) * a.ndim)
    row = pl.BlockSpec((tm, d), lambda i: (i, 0))
    return pl.pallas_call(
        body, name=name, grid=(t // tm,),
        out_shape=[jax.ShapeDtypeStruct((t, d), F32), jax.ShapeDtypeStruct((1, d), F32)],
        in_specs=[row, row, full(gain), pl.BlockSpec((tm, ng), lambda i: (i, 0)),
                  pl.BlockSpec((tm, nq), lambda i: (i, 0)), full(win), full(wgate)],
        out_specs=[row, pl.BlockSpec((1, d), lambda i: (0, 0))],
        compiler_params=_params(("arbitrary",)),
    )(dh, h, gain, dzg, dqkv, win, wgate)


def _merge_fwd(h, ya, yb, gate, wpa, wpb, wout, name):
    t, d = h.shape
    tm = _row_tile(t, 256)

    def body(h_ref, ya_ref, yb_ref, ga_ref, gb_ref, wpa_ref, wpb_ref, wout_ref, out_ref, mg_ref, pa_ref, pb_ref):
        pa = _dot(ya_ref[...].astype(BF16), wpa_ref[...])
        pb = _dot(yb_ref[...].astype(BF16), wpb_ref[...])
        merged = (ga_ref[...] * pa + gb_ref[...] * pb).astype(BF16)
        pa_ref[...] = pa.astype(BF16)
        pb_ref[...] = pb.astype(BF16)
        mg_ref[...] = merged
        out_ref[...] = h_ref[...] + _dot(merged, wout_ref[...])

    full = lambda a: pl.BlockSpec(a.shape, lambda i: (0,) * a.ndim)
    row = pl.BlockSpec((tm, d), lambda i: (i, 0))
    yrow = pl.BlockSpec((tm, ya.shape[1]), lambda i: (i, 0))
    return pl.pallas_call(
        body, name=name, grid=(t // tm,),
        out_shape=[jax.ShapeDtypeStruct((t, d), F32)] + [jax.ShapeDtypeStruct((t, d), BF16)] * 3,
        in_specs=[row, yrow, yrow, pl.BlockSpec((tm, d), lambda i: (i, 0)), pl.BlockSpec((tm, d), lambda i: (i, 1)),
                  full(wpa), full(wpb), full(wout)],
        out_specs=[row] * 4,
        compiler_params=_params(("arbitrary",)),
    )(h, ya, yb, gate, gate, wpa, wpb, wout)


def _merge_bwd(dh, pa, pb, gate, wpa, wpb, wout, name):
    t, d = dh.shape
    tm = _row_tile(t, 256)
    wy = wpa.shape[0]

    def body(dh_ref, pa_ref, pb_ref, ga_ref, gb_ref, wpa_ref, wpb_ref, wout_ref,
             dpa_ref, dpb_ref, dzg_ref, dya_ref, dyb_ref):
        dm = _dot_nt(dh_ref[...].astype(BF16), wout_ref[...])
        ga, gb = ga_ref[...], gb_ref[...]
        dpa = (dm * ga).astype(BF16)
        dpb = (dm * gb).astype(BF16)
        dpa_ref[...] = dpa
        dpb_ref[...] = dpb
        dzg_ref[:, :d] = (dm * pa_ref[...].astype(F32) * ga * (1.0 - ga)).astype(BF16)
        dzg_ref[:, d:] = (dm * pb_ref[...].astype(F32) * gb * (1.0 - gb)).astype(BF16)
        dya_ref[...] = _dot_nt(dpa, wpa_ref[...])
        dyb_ref[...] = _dot_nt(dpb, wpb_ref[...])

    full = lambda a: pl.BlockSpec(a.shape, lambda i: (0,) * a.ndim)
    row = pl.BlockSpec((tm, d), lambda i: (i, 0))
    yrow = pl.BlockSpec((tm, wy), lambda i: (i, 0))
    return pl.pallas_call(
        body, name=name, grid=(t // tm,),
        out_shape=[jax.ShapeDtypeStruct((t, d), BF16), jax.ShapeDtypeStruct((t, d), BF16),
                   jax.ShapeDtypeStruct((t, 2 * d), BF16), jax.ShapeDtypeStruct((t, wy), F32),
                   jax.ShapeDtypeStruct((t, wy), F32)],
        in_specs=[row, row, row, pl.BlockSpec((tm, d), lambda i: (i, 0)), pl.BlockSpec((tm, d), lambda i: (i, 1)),
                  full(wpa), full(wpb), full(wout)],
        out_specs=[row, row, pl.BlockSpec((tm, 2 * d), lambda i: (i, 0)), yrow, yrow],
        compiler_params=_params(("arbitrary",)),
    )(dh, pa, pb, gate, gate, wpa, wpb, wout)


def _ple_loss(h, gain, p, target, wpg, wpe, name):
    t, d = h.shape
    tm = _row_tile(t, 256)
    pd = p.shape[1]

    def body(h_ref, g_ref, p_ref, t_ref, wpg_ref, wpe_ref, dh_ref, dz_ref, dpp_ref, n_ref, dgain_ref, loss_ref):
        @pl.when(pl.program_id(0) == 0)
        def _():
            dgain_ref[...] = jnp.zeros_like(dgain_ref)
            loss_ref[...] = jnp.zeros_like(loss_ref)

        x = h_ref[...]
        gain_v = g_ref[...]
        n, xh, r = _rms(x, gain_v)
        nbf = n.astype(BF16)
        n_ref[...] = nbf
        pg = jax.nn.sigmoid(_dot(nbf, wpg_ref[...]))
        pp = _dot(p_ref[...].astype(BF16), wpe_ref[...])
        err = (x + pg * pp) - t_ref[...]
        loss_ref[...] += 0.5 * jnp.sum(jnp.mean(err * err, axis=-1, keepdims=True))
        dy = err * (1.0 / d)
        dpp_ref[...] = (dy * pg).astype(BF16)
        dz = (dy * pp * pg * (1.0 - pg)).astype(BF16)
        dz_ref[...] = dz
        dn = _dot_nt(dz, wpg_ref[...])
        dx, dgain = _rms_bwd(xh, r, gain_v, dn)
        dh_ref[...] = dy + dx
        dgain_ref[...] += dgain

    full = lambda a: pl.BlockSpec(a.shape, lambda i: (0,) * a.ndim)
    row = pl.BlockSpec((tm, d), lambda i: (i, 0))
    return pl.pallas_call(
        body, name=name, grid=(t // tm,),
        out_shape=[jax.ShapeDtypeStruct((t, d), F32), jax.ShapeDtypeStruct((t, d), BF16),
                   jax.ShapeDtypeStruct((t, d), BF16), jax.ShapeDtypeStruct((t, d), BF16),
                   jax.ShapeDtypeStruct((1, d), F32), jax.ShapeDtypeStruct((8, LANES), F32)],
        in_specs=[row, full(gain), pl.BlockSpec((tm, pd), lambda i: (i, 0)), row, full(wpg), full(wpe)],
        out_specs=[row, row, row, row, pl.BlockSpec((1, d), lambda i: (0, 0)),
                   pl.BlockSpec((8, LANES), lambda i: (0, 0))],
        compiler_params=_params(("arbitrary",)),
    )(h, gain, p, target, wpg, wpe)


def _head_masks():
    lane = lax.broadcasted_iota(jnp.int32, (1, LANES), 1)
    m0 = (lane < HEAD_DIM).astype(F32)
    return m0, 1.0 - m0


def _head_mean(v, m0, m1):
    s0 = jnp.sum(v * m0, axis=-1, keepdims=True)
    s1 = jnp.sum(v * m1, axis=-1, keepdims=True)
    return (s0 * m0 + s1 * m1) * (1.0 / HEAD_DIM)


def _head_norm(x, gain, m0, m1):
    r = lax.rsqrt(_head_mean(x * x, m0, m1) + EPS)
    xh = x * r
    return xh * gain, xh, r


def _head_norm_bwd(xh, r, gain, dy, m0, m1):
    gdy = gain * dy
    dx = r * (gdy - xh * _head_mean(xh * gdy, m0, m1))
    return dx, jnp.sum(dy * xh, axis=0, keepdims=True)


def _attn_prep(mode, pair, s_len, padk, q_ref, k_ref, v_ref, gq_ref, gk_ref, qs, k0, k1, v0, v1):
    m0, m1 = _head_masks()
    zpad = jnp.zeros((padk, LANES), BF16)
    for buf in (k0, k1, v0, v1):
        buf[pl.ds(0, padk), :] = zpad
    first_kv = (pair // 2) == 0
    rt = _row_tile(s_len, 256)

    def step(i, carry):
        rows = pl.ds(pl.multiple_of(i * rt, rt), rt)
        dst = pl.ds(pl.multiple_of(padk + i * rt, QTILE), rt)
        qn, _, _ = _head_norm(q_ref[rows, :], gq_ref[...], m0, m1)
        kn, _, _ = _head_norm(k_ref[rows, :], gk_ref[...], m0, m1)
        vv = v_ref[rows, :]
        qs[rows, :] = (qn * (HEAD_DIM ** -0.5)).astype(BF16)
        if mode == "B":
            kn = jnp.where(first_kv, kn, pltpu.roll(kn, HEAD_DIM, 1))
            vv = jnp.where(first_kv, vv, pltpu.roll(vv, HEAD_DIM, 1))
            ka, va = kn * m0, vv * m0
            kb, vb = pltpu.roll(ka, HEAD_DIM, 1), pltpu.roll(va, HEAD_DIM, 1)
        else:
            ka, kb, va, vb = kn * m0, kn * m1, vv * m0, vv * m1
        k0[dst, :] = ka.astype(BF16)
        k1[dst, :] = kb.astype(BF16)
        v0[dst, :] = va.astype(BF16)
        v1[dst, :] = vb.astype(BF16)
        return carry

    lax.fori_loop(0, s_len // rt, step, 0)


def _attn_probs(mode, q2, kb, bias, ok, sink):
    s = _dot_nt(q2, kb) + bias
    s = jnp.where(ok, s, NEG_INF)
    mx = jnp.max(s, axis=-1, keepdims=True)
    if mode == "B":
        mx = jnp.maximum(mx, sink)
    e = jnp.exp(s - mx)
    l = jnp.sum(e, axis=-1, keepdims=True)
    if mode == "B":
        l = l + jnp.exp(sink - mx)
    return e, mx, l


def _attn_cols(mode):
    if mode == "A":
        return (lambda b, p: (b, p)), (lambda b, p: (b, 4 + p)), (lambda b, p: (b, 8 + p))
    return (lambda b, p: (b, 12 + p)), (lambda b, p: (b, 16)), (lambda b, p: (b, 17))


def _attn_fwd(mode, qkv, gq, gk, bias, sinks, bl, s_len, name):
    bw = bias.shape[-1]
    padk = bw - QTILE
    nt = s_len // QTILE
    qmap, kmap, vmap = _attn_cols(mode)

    def body(q_ref, k_ref, v_ref, gq_ref, gk_ref, bias_ref, sink_ref, o_ref, qs, k0, k1, v0, v1):
        pair = pl.program_id(1)
        _attn_prep(mode, pair, s_len, padk, q_ref, k_ref, v_ref, gq_ref, gk_ref, qs, k0, k1, v0, v1)
        col = lax.broadcasted_iota(jnp.int32, (QTILE, bw), 1)

        def tile(m, carry):
            r0 = pl.multiple_of(m * QTILE, QTILE)
            q2 = qs[pl.ds(r0, QTILE), :]
            ok = col >= (padk - r0)
            acc = jnp.zeros((QTILE, LANES), F32)
            for hh, (kk, vv) in enumerate(((k0, v0), (k1, v1))):
                sink = sink_ref[2 * pair + hh]
                e, _, l = _attn_probs(mode, q2, kk[pl.ds(r0, bw), :], bias_ref[hh], ok, sink)
                acc = acc + _dot(e.astype(BF16), vv[pl.ds(r0, bw), :]) / l
            o_ref[pl.ds(r0, QTILE), :] = acc
            return carry

        lax.fori_loop(0, nt, tile, 0, unroll=2)

    blk = lambda f: pl.BlockSpec((s_len, LANES), f)
    vec = pl.BlockSpec((1, LANES), lambda b, p: (0, 0))
    return pl.pallas_call(
        body, name=name, grid=(bl, 4),
        out_shape=jax.ShapeDtypeStruct((bl * s_len, 4 * LANES), F32),
        in_specs=[blk(qmap), blk(kmap), blk(vmap), vec, vec,
                  pl.BlockSpec((2, QTILE, bw), lambda b, p: (p, 0, 0)),
                  pl.BlockSpec(memory_space=pltpu.SMEM)],
        out_specs=pl.BlockSpec((s_len, LANES), lambda b, p: (b, p)),
        scratch_shapes=[pltpu.VMEM((s_len, LANES), BF16)] + [pltpu.VMEM((s_len + padk, LANES), BF16)] * 4,
        compiler_params=_params(("arbitrary", "arbitrary")),
    )(qkv, qkv, qkv, gq, gk, bias, sinks)


def _attn_bwd(mode, qkv, gq, gk, bias, sinks, y, dy, bl, s_len, name):
    bw = bias.shape[-1]
    padk = bw - QTILE
    nt = s_len // QTILE
    qmap, kmap, vmap = _attn_cols(mode)
    t = bl * s_len
    kvw = 4 * LANES if mode == "A" else LANES

    def body(q_ref, k_ref, v_ref, gq_ref, gk_ref, bias_ref, sink_ref, y_ref, dy_ref,
             dq_ref, dk_ref, dv_ref, dgq_ref, dgk_ref, dbias_ref, dsink_ref,
             qs, k0, k1, v0, v1, dqs, dk0, dk1, dv0, dv1):
        pair = pl.program_id(1)
        m0, m1 = _head_masks()
        _attn_prep(mode, pair, s_len, padk, q_ref, k_ref, v_ref, gq_ref, gk_ref, qs, k0, k1, v0, v1)
        for buf in (dk0, dk1, dv0, dv1):
            buf[...] = jnp.zeros_like(buf)
        dbias_ref[...] = jnp.zeros_like(dbias_ref)
        col = lax.broadcasted_iota(jnp.int32, (QTILE, bw), 1)
        lane8 = lax.broadcasted_iota(jnp.int32, (8, LANES), 1)

        def tile(m, dsink):
            r0 = pl.multiple_of(m * QTILE, QTILE)
            rows = pl.ds(r0, QTILE)
            band = pl.ds(r0, bw)
            q2 = qs[rows, :]
            do2 = dy_ref[rows, :]
            dd = do2 * y_ref[rows, :]
            dob = do2.astype(BF16)
            ok = col >= (padk - r0)
            dq = jnp.zeros((QTILE, LANES), F32)
            for hh, (kk, vv, dkk, dvv, mh) in enumerate(((k0, v0, dk0, dv0, m0), (k1, v1, dk1, dv1, m1))):
                sink = sink_ref[2 * pair + hh]
                kb = kk[band, :]
                e, mx, l = _attn_probs(mode, q2, kb, bias_ref[hh], ok, sink)
                inv = 1.0 / l
                pn = e * inv
                delta = jnp.sum(dd * mh, axis=-1, keepdims=True)
                dp = _dot_nt(dob, vv[band, :])
                ds = pn * (dp - delta)
                if mode == "A":
                    dbias_ref[hh] += ds
                else:
                    ps = jnp.exp(sink - mx) * inv
                    dsink = dsink + jnp.where(lane8 == hh, -jnp.sum(ps * delta), 0.0)
                dsb = ds.astype(BF16)
                dvv[band, :] += _dot_tn(pn.astype(BF16), dob)
                dkk[band, :] += _dot_tn(dsb, q2)
                dq = dq + _dot(dsb, kb)
            dqs[rows, :] = dq * (HEAD_DIM ** -0.5)
            return dsink

        dsink = lax.fori_loop(0, nt, tile, jnp.zeros((8, LANES), F32), unroll=2)
        dsink_ref[...] = dsink

        first_kv = (pair // 2) == 0
        rt = _row_tile(s_len, 256)

        def post(i, carry):
            dgq, dgk = carry
            rows = pl.ds(pl.multiple_of(i * rt, rt), rt)
            src = pl.ds(pl.multiple_of(padk + i * rt, QTILE), rt)
            gq_v, gk_v = gq_ref[...], gk_ref[...]
            _, qh, qr = _head_norm(q_ref[rows, :], gq_v, m0, m1)
            _, kh, kr = _head_norm(k_ref[rows, :], gk_v, m0, m1)
            dq_raw, dgq_i = _head_norm_bwd(qh, qr, gq_v, dqs[rows, :], m0, m1)
            if mode == "A":
                dkn = dk0[src, :] * m0 + dk1[src, :] * m1
                dvn = dv0[src, :] * m0 + dv1[src, :] * m1
            else:
                dkn = dk0[src, :] * m0 + pltpu.roll(dk1[src, :] * m1, HEAD_DIM, 1)
                dvn = dv0[src, :] * m0 + pltpu.roll(dv1[src, :] * m1, HEAD_DIM, 1)
                dkn = jnp.where(first_kv, dkn, pltpu.roll(dkn, HEAD_DIM, 1))
                dvn = jnp.where(first_kv, dvn, pltpu.roll(dvn, HEAD_DIM, 1))
            dk_raw, dgk_i = _head_norm_bwd(kh, kr, gk_v, dkn, m0, m1)
            dq_ref[rows, :] = dq_raw
            if mode == "A":
                dk_ref[rows, :] = dk_raw
                dv_ref[rows, :] = dvn
            else:
                @pl.when(pair == 0)
                def _():
                    dk_ref[rows, :] = dk_raw
                    dv_ref[rows, :] = dvn

                @pl.when(pair != 0)
                def _():
                    dk_ref[rows, :] += dk_raw
                    dv_ref[rows, :] += dvn
            return dgq + dgq_i, dgk + dgk_i

        z = jnp.zeros((1, LANES), F32)
        dgq, dgk = lax.fori_loop(0, s_len // rt, post, (z, z))
        dgq_ref[...] = jnp.broadcast_to(dgq, (8, LANES))
        dgk_ref[...] = jnp.broadcast_to(dgk, (8, LANES))

    blk = lambda f: pl.BlockSpec((s_len, LANES), f)
    vec = pl.BlockSpec((1, LANES), lambda b, p: (0, 0))
    small = pl.BlockSpec((None, None, 8, LANES), lambda b, p: (b, p, 0, 0))
    kvmap = (lambda b, p: (b, p)) if mode == "A" else (lambda b, p: (b, 0))
    pad_f32 = pltpu.VMEM((s_len + padk, LANES), F32)
    pad_bf = pltpu.VMEM((s_len + padk, LANES), BF16)
    return pl.pallas_call(
        body, name=name, grid=(bl, 4),
        out_shape=[jax.ShapeDtypeStruct((t, 4 * LANES), F32), jax.ShapeDtypeStruct((t, kvw), F32),
                   jax.ShapeDtypeStruct((t, kvw), F32),
                   jax.ShapeDtypeStruct((bl, 4, 8, LANES), F32), jax.ShapeDtypeStruct((bl, 4, 8, LANES), F32),
                   jax.ShapeDtypeStruct((bl, 8, QTILE, bw), F32), jax.ShapeDtypeStruct((bl, 4, 8, LANES), F32)],
        in_specs=[blk(qmap), blk(kmap), blk(vmap), vec, vec,
                  pl.BlockSpec((2, QTILE, bw), lambda b, p: (p, 0, 0)),
                  pl.BlockSpec(memory_space=pltpu.SMEM),
                  blk(lambda b, p: (b, p)), blk(lambda b, p: (b, p))],
        out_specs=[blk(lambda b, p: (b, p)), blk(kvmap), blk(kvmap), small, small,
                   pl.BlockSpec((None, 2, QTILE, bw), lambda b, p: (b, p, 0, 0)), small],
        scratch_shapes=[pltpu.VMEM((s_len, LANES), BF16), pad_bf, pad_bf, pad_bf, pad_bf,
                        pltpu.VMEM((s_len, LANES), F32), pad_f32, pad_f32, pad_f32, pad_f32],
        compiler_params=_params(("arbitrary", "arbitrary")),
    )(qkv, qkv, qkv, gq, gk, bias, sinks, y, dy)


def _band_geometry(prev):
    bw = QTILE + prev * CHUNK
    i = np.arange(QTILE)[:, None]
    j = np.arange(bw)[None, :]
    dist = i + prev * CHUNK - j
    valid = (j // CHUNK >= i // CHUNK) & (j // CHUNK <= i // CHUNK + prev)
    return dist, valid


A_VAR0 = (A_PREV * CHUNK - A_MAX_REL) // LANES * LANES


A_NVAR = QTILE + A_PREV * CHUNK - A_VAR0


def _skew_rows(x, sign):
    rows, n = x.shape
    row = lax.broadcasted_iota(jnp.int32, x.shape, 0)
    b = 1
    while b < rows:
        x = jnp.where((row & b) != 0, pltpu.roll(x, (sign * b) % n, 1), x)
        b *= 2
    return x


def _rel_bias_expand(table, name):
    _, valid = _band_geometry(A_PREV)
    bw = valid.shape[1]
    valid_f = jnp.asarray(valid.astype(np.float32))
    rev = jnp.flip(table[:, 1:], axis=1).reshape(A_HEADS, 1, A_NVAR)

    def body(rev_ref, valid_ref, o_ref):
        rowv = jnp.broadcast_to(rev_ref[...], (QTILE, A_NVAR))
        top = rowv[:, 0:1]
        var = _skew_rows(rowv, 1)
        row = lax.broadcasted_iota(jnp.int32, (QTILE, A_NVAR), 0)
        colv = lax.broadcasted_iota(jnp.int32, (QTILE, A_NVAR), 1)
        var = jnp.where(colv < row, top, var)
        ok = valid_ref[...] > 0.5
        o_ref[:, :A_VAR0] = jnp.where(ok[:, :A_VAR0], top, NEG_INF)
        o_ref[:, A_VAR0:] = jnp.where(ok[:, A_VAR0:], var, NEG_INF)

    return pl.pallas_call(
        body, name=name, grid=(A_HEADS,),
        out_shape=jax.ShapeDtypeStruct((A_HEADS, QTILE, bw), F32),
        in_specs=[pl.BlockSpec((None, 1, A_NVAR), lambda h: (h, 0, 0)), pl.BlockSpec((QTILE, bw), lambda h: (0, 0))],
        out_specs=pl.BlockSpec((None, QTILE, bw), lambda h: (h, 0, 0)),
        compiler_params=_params(("arbitrary",)),
    )(rev, valid_f)


def _rel_bias_grad(dbias, name):
    bl = dbias.shape[0]
    bw = dbias.shape[-1]

    def body(db_ref, o_ref):
        g = db_ref[0]
        for b in range(1, bl):
            g = g + db_ref[b]
        sk = _skew_rows(g[:, A_VAR0:], -1)
        row = lax.broadcasted_iota(jnp.int32, (QTILE, A_NVAR), 0)
        colv = lax.broadcasted_iota(jnp.int32, (QTILE, A_NVAR), 1)
        wrapped = (row + colv) >= A_NVAR
        main = jnp.sum(jnp.where(wrapped, 0.0, sk), axis=0, keepdims=True)
        top = jnp.sum(g[:, :A_VAR0]) + jnp.sum(jnp.where(wrapped, sk, 0.0))
        o_ref[:, :A_NVAR] = jnp.broadcast_to(main, (8, A_NVAR))
        o_ref[:, A_NVAR:] = jnp.full((8, LANES), top, F32)

    out = pl.pallas_call(
        body, name=name, grid=(A_HEADS,),
        out_shape=jax.ShapeDtypeStruct((A_HEADS, 8, A_NVAR + LANES), F32),
        in_specs=[pl.BlockSpec((bl, None, QTILE, bw), lambda h: (0, h, 0, 0))],
        out_specs=pl.BlockSpec((None, 8, A_NVAR + LANES), lambda h: (h, 0, 0)),
        compiler_params=_params(("arbitrary",)),
    )(dbias)
    main, top = out[:, 0, :A_NVAR], out[:, 0, A_NVAR]
    fm = jnp.flip(main, axis=1)
    return jnp.concatenate([jnp.zeros((A_HEADS, 1), F32), fm[:, :-1], fm[:, -1:] + top[:, None]], axis=1)


def _alibi_bias():
    dist, valid = _band_geometry(B_PREV)
    slopes = np.array([2.0 ** (-8.0 * (h + 1) / B_Q_HEADS) for h in range(B_Q_HEADS)], dtype=np.float32)
    bias = -slopes[:, None, None] * np.abs(dist).astype(np.float32)[None]
    return jnp.asarray(np.where(valid[None], bias, np.float32(NEG_INF)).astype(np.float32))


SMALL_NAMES = ("ffn1_norm", "mix_norm", "ffn2_norm", "ple_norm", "a_q_norm", "a_k_norm", "b_q_norm", "b_k_norm",
               "a_rel_bias", "b_sinks", "loss")


def _pack_small(vals):
    rows = []
    for nme in SMALL_NAMES:
        v = vals[nme].astype(F32)
        if nme == "a_rel_bias":
            v = jnp.pad(v.reshape(A_HEADS, -1), ((0, 0), (0, 3 * LANES - (2 * A_MAX_REL + 1))))
        v = v.reshape(-1)
        v = jnp.pad(v, (0, (-v.shape[0]) % LANES))
        rows.append(v.reshape(-1, LANES))
    out = jnp.concatenate(rows, axis=0)
    return jnp.pad(out, ((0, (-out.shape[0]) % 8), (0, 0)))


def _unpack_small(packed, shapes):
    out, r = {}, 0
    for nme in SMALL_NAMES:
        shp = shapes[nme]
        if nme == "a_rel_bias":
            nr = A_HEADS * 3
            out[nme] = packed[r:r + nr].reshape(A_HEADS, 3 * LANES)[:, :2 * A_MAX_REL + 1].reshape(shp)
        else:
            size = int(np.prod(shp)) if shp else 1
            nr = -(-size // LANES)
            out[nme] = packed[r:r + nr].reshape(-1)[:size].reshape(shp)
        r += nr
    return out


BIG_NAMES = ("ffn1_w_gu", "ffn1_w_down", "w_in", "w_gate", "w_proj_a", "w_proj_b", "w_out",
             "ffn2_w_gu", "ffn2_w_down", "w_ple_gate", "w_ple_proj")
ROW_SHARDED = ("ffn1_w_down", "ffn2_w_down", "w_out", "w_ple_gate")
WEIGHT_ORDER = ("ffn1_norm", "ffn1_w_gu", "ffn1_w_down", "mix_norm", "w_in", "a_q_norm", "a_k_norm", "a_rel_bias",
                "b_q_norm", "b_k_norm", "b_sinks", "w_gate", "w_proj_a", "w_proj_b", "w_out", "ffn2_norm",
                "ffn2_w_gu", "ffn2_w_down", "ple_norm", "w_ple_gate", "w_ple_proj")


def _full_cols(wg):
    nb, k, n = wg.shape
    return jnp.transpose(wg, (1, 0, 2)).reshape(k, nb * n)


def _col_blocks(g, nb):
    k, n = g.shape
    return jnp.transpose(g.reshape(k, nb, n // nb), (1, 0, 2))


def _step(x, p, target, w, m, v):
    bl, s_len, d = x.shape
    t = bl * s_len
    h0 = x.reshape(t, d)
    pt = p.reshape(t, p.shape[-1])
    tgt = target.reshape(t, d)

    shards = [w[nme][0].astype(BF16) for nme in BIG_NAMES]
    gathered = dict(zip(BIG_NAMES, _all_gather(shards, "weights_all_gather")))
    wgu1 = gathered["ffn1_w_gu"]
    wgu2 = gathered["ffn2_w_gu"]
    nf = wgu1.shape[2]
    wd1 = gathered["ffn1_w_down"].reshape(N_DEV // 2, nf, d)
    wd2 = gathered["ffn2_w_down"].reshape(N_DEV // 2, nf, d)
    win = _full_cols(gathered["w_in"])
    wgate = _full_cols(gathered["w_gate"])
    wpa = _full_cols(gathered["w_proj_a"])
    wpb = _full_cols(gathered["w_proj_b"])
    wpe = _full_cols(gathered["w_ple_proj"])
    wout = gathered["w_out"].reshape(d, d)
    wpg = gathered["w_ple_gate"].reshape(d, d)

    g_ffn1, g_mix, g_ffn2, g_ple = w["ffn1_norm"], w["mix_norm"], w["ffn2_norm"], w["ple_norm"]
    tile2 = lambda a: jnp.tile(a.reshape(1, HEAD_DIM), (1, 2))
    gqa, gka, gqb, gkb = tile2(w["a_q_norm"]), tile2(w["a_k_norm"]), tile2(w["b_q_norm"]), tile2(w["b_k_norm"])
    sinks = w["b_sinks"].reshape(B_Q_HEADS)
    bias_a = _rel_bias_expand(w["a_rel_bias"][0], "rel_bias_expand")
    bias_b = _alibi_bias()

    h1, gu1 = _ffn_fwd(h0, g_ffn1, wgu1, wd1, "ffn1_fwd")
    un, qkv, gate = _proj_fwd(h1, g_mix, win, wgate, "proj_fwd")
    ya = _attn_fwd("A", qkv, gqa, gka, bias_a, sinks, bl, s_len, "attn_a_fwd")
    yb = _attn_fwd("B", qkv, gqb, gkb, bias_b, sinks, bl, s_len, "attn_b_fwd")
    h2, merged, pa, pb = _merge_fwd(h1, ya, yb, gate, wpa, wpb, wout, "merge_fwd")
    h3, gu2 = _ffn_fwd(h2, g_ffn2, wgu2, wd2, "ffn2_fwd")
    dh3, dz4, dpp, n4, dg_ple, loss_part = _ple_loss(h3, g_ple, pt, tgt, wpg, wpe, "ple_loss")

    xi, yi, ci = _place()
    me = jnp.stack([4 * xi + 2 * yi + ci]).astype(jnp.int32)
    g32, g16, big = {}, {}, {}

    def keep(nme, pair, rows=None):
        for store, g in zip((g32, g16), pair):
            if nme == "w_in":
                g = _col_blocks(g[0], N_DEV)
            elif rows is not None:
                g = g.reshape(N_DEV, rows, d)
            store[nme] = g

    def start(names, after, tag):
        send, recv, parts, lands, token = _scatter_start([g16[nme] for nme in names], after, "grads_start_" + tag)
        return names, send, recv, parts, lands, token

    def finish(state, after, tag):
        names, send, recv, parts, lands, _ = state
        lands = _scatter_wait(send, recv, parts, lands, after, "grads_wait_" + tag)
        return names, lands

    def adam(done):
        for nme, land in zip(*done):
            outs = _final_adam(g32[nme], land, w[nme][0], m[nme][0], v[nme][0], me, "adam_" + nme)
            big[nme] = [o[None] for o in outs]

    keep("w_ple_gate", _dw(n4, dz4, 1, d, "dw_ple_gate"), d // N_DEV)
    keep("w_ple_proj", _dw(pt, dpp, N_DEV, d // N_DEV, "dw_ple_proj"))

    dh2, dgu2, a2, n3, dg_ffn2 = _ffn_bwd(dh3, h2, g_ffn2, gu2, wgu2, wd2, "ffn2_bwd")
    keep("ffn2_w_gu", _dw(n3, dgu2, N_DEV, nf, "dw_ffn2_gu"))
    keep("ffn2_w_down", _dw(a2, dh3, N_DEV // 2, d, "dw_ffn2_down", 0.5), nf // 2)
    flight = start(("w_ple_gate", "w_ple_proj", "ffn2_w_gu", "ffn2_w_down"), dh2, "ffn2")

    dpa, dpb, dzg, dya, dyb = _merge_bwd(dh2, pa, pb, gate, wpa, wpb, wout, "merge_bwd")
    keep("w_out", _dw(merged, dh2, 1, d, "dw_out"), d // N_DEV)
    keep("w_proj_a", _dw(ya, dpa, N_DEV, d // N_DEV, "dw_proj_a"))
    keep("w_proj_b", _dw(yb, dpb, N_DEV, d // N_DEV, "dw_proj_b"))
    keep("w_gate", _dw(un, dzg, N_DEV, 2 * d // N_DEV, "dw_gate"))

    tok = flight[-1][0, 0]
    dqa, dka, dva, dgqa, dgka, dbias, _ = _attn_bwd("A", qkv, gqa + tok, gka, bias_a, sinks, ya, dya, bl, s_len,
                                                     "attn_a_bwd")
    dqb, dkb, dvb, dgqb, dgkb, _, dsink = _attn_bwd("B", qkv, gqb, gkb, bias_b, sinks, yb, dyb, bl, s_len, "attn_b_bwd")
    dqkv = jnp.concatenate([dqa, dka, dva, dqb, dkb, dvb], axis=1)
    dtab = _rel_bias_grad(dbias, "rel_bias_grad")

    dh1, dg_mix = _proj_bwd(dh2, h1, g_mix, dzg, dqkv, win, wgate, "proj_bwd")
    keep("w_in", _dw(un, dqkv, 1, IN_COLS, "dw_in"))
    done = finish(flight, g32["w_in"], "ffn2")
    flight = start(("w_out", "w_proj_a", "w_proj_b", "w_gate", "w_in"), done[1][0], "mixer")
    adam(done)

    dh0, dgu1, a1, n1, dg_ffn1 = _ffn_bwd(dh1, h0, g_ffn1 + flight[-1][0, 0], gu1, wgu1, wd1, "ffn1_bwd")
    keep("ffn1_w_down", _dw(a1, dh1, N_DEV // 2, d, "dw_ffn1_down", 0.5), nf // 2)
    done = finish(flight, g32["ffn1_w_down"], "mixer")
    flight = start(("ffn1_w_down",), done[1][0], "ffn1_down")
    adam(done)

    keep("ffn1_w_gu", _dw(n1, dgu1, N_DEV, nf, "dw_ffn1_gu", dep=flight[-1]))
    done = finish(flight, g32["ffn1_w_gu"], "ffn1_down")
    flight = start(("ffn1_w_gu",), done[1][0], "ffn1_gu")
    adam(done)
    smalls = (dg_ffn1, dg_mix, dg_ffn2, dg_ple + flight[-1][0, 0], dgqa, dgka, dgqb, dgkb, dtab, dsink)
    return dh0, loss_part, big, smalls, flight, finish, adam


def kernel(x, p, ffn1_norm, ffn1_w_gu, ffn1_w_down, mix_norm, w_in, a_q_norm, a_k_norm, a_rel_bias, b_q_norm, b_k_norm, b_sinks, w_gate, w_proj_a, w_proj_b, w_out, ffn2_norm, ffn2_w_gu, ffn2_w_down, ple_norm, w_ple_gate, w_ple_proj, loss_target, m_ffn1_norm, m_ffn1_w_gu, m_ffn1_w_down, m_mix_norm, m_w_in, m_a_q_norm, m_a_k_norm, m_a_rel_bias, m_b_q_norm, m_b_k_norm, m_b_sinks, m_w_gate, m_w_proj_a, m_w_proj_b, m_w_out, m_ffn2_norm, m_ffn2_w_gu, m_ffn2_w_down, m_ple_norm, m_w_ple_gate, m_w_ple_proj, v_ffn1_norm, v_ffn1_w_gu, v_ffn1_w_down, v_mix_norm, v_w_in, v_a_q_norm, v_a_k_norm, v_a_rel_bias, v_b_q_norm, v_b_k_norm, v_b_sinks, v_w_gate, v_w_proj_a, v_w_proj_b, v_w_out, v_ffn2_norm, v_ffn2_w_gu, v_ffn2_w_down, v_ple_norm, v_w_ple_gate, v_w_ple_proj):
    w = dict(ffn1_norm=ffn1_norm, ffn1_w_gu=ffn1_w_gu, ffn1_w_down=ffn1_w_down, mix_norm=mix_norm, w_in=w_in,
             a_q_norm=a_q_norm, a_k_norm=a_k_norm, a_rel_bias=a_rel_bias, b_q_norm=b_q_norm, b_k_norm=b_k_norm,
             b_sinks=b_sinks, w_gate=w_gate, w_proj_a=w_proj_a, w_proj_b=w_proj_b, w_out=w_out, ffn2_norm=ffn2_norm,
             ffn2_w_gu=ffn2_w_gu, ffn2_w_down=ffn2_w_down, ple_norm=ple_norm, w_ple_gate=w_ple_gate,
             w_ple_proj=w_ple_proj)
    m = dict(ffn1_norm=m_ffn1_norm, ffn1_w_gu=m_ffn1_w_gu, ffn1_w_down=m_ffn1_w_down, mix_norm=m_mix_norm,
             w_in=m_w_in, a_q_norm=m_a_q_norm, a_k_norm=m_a_k_norm, a_rel_bias=m_a_rel_bias, b_q_norm=m_b_q_norm,
             b_k_norm=m_b_k_norm, b_sinks=m_b_sinks, w_gate=m_w_gate, w_proj_a=m_w_proj_a, w_proj_b=m_w_proj_b,
             w_out=m_w_out, ffn2_norm=m_ffn2_norm, ffn2_w_gu=m_ffn2_w_gu, ffn2_w_down=m_ffn2_w_down,
             ple_norm=m_ple_norm, w_ple_gate=m_w_ple_gate, w_ple_proj=m_w_ple_proj)
    v = dict(ffn1_norm=v_ffn1_norm, ffn1_w_gu=v_ffn1_w_gu, ffn1_w_down=v_ffn1_w_down, mix_norm=v_mix_norm,
             w_in=v_w_in, a_q_norm=v_a_q_norm, a_k_norm=v_a_k_norm, a_rel_bias=v_a_rel_bias, b_q_norm=v_b_q_norm,
             b_k_norm=v_b_k_norm, b_sinks=v_b_sinks, w_gate=v_w_gate, w_proj_a=v_w_proj_a, w_proj_b=v_w_proj_b,
             w_out=v_w_out, ffn2_norm=v_ffn2_norm, ffn2_w_gu=v_ffn2_w_gu, ffn2_w_down=v_ffn2_w_down,
             ple_norm=v_ple_norm, w_ple_gate=v_w_ple_gate, w_ple_proj=v_w_ple_proj)
    bl, s_len, d = x.shape

    dh0, loss_part, big, smalls, flight, finish, adam = _step(x, p[0], loss_target, w, m, v)
    dg_ffn1, dg_mix, dg_ffn2, dg_ple, dgqa, dgka, dgqb, dgkb, dtab, dsink = smalls

    fold = lambda a: (a[:, :, 0, :HEAD_DIM] + a[:, :, 0, HEAD_DIM:]).sum(axis=(0, 1))
    small_part = dict(
        ffn1_norm=dg_ffn1, mix_norm=dg_mix, ffn2_norm=dg_ffn2, ple_norm=dg_ple,
        a_q_norm=fold(dgqa), a_k_norm=fold(dgka), b_q_norm=fold(dgqb), b_k_norm=fold(dgkb),
        a_rel_bias=dtab,
        b_sinks=dsink.sum(axis=0)[:, 0, :2].reshape(B_Q_HEADS),
        loss=loss_part[0, :1])
    zero1 = jnp.zeros((1,), F32)
    shapes = {nme: w[nme].shape for nme in SMALL_NAMES if nme != "loss"}
    shapes["loss"] = ()
    pk = lambda src: _pack_small({**{nme: src[nme] for nme in SMALL_NAMES if nme != "loss"}, "loss": zero1})
    sg, sd, sm, sv = _small_allreduce_adam(_pack_small(small_part), pk(w), pk(m), pk(v), "small_allreduce_adam")
    adam(finish(flight, sg, "ffn1_gu"))
    sg, sd, sm, sv = (_unpack_small(a, shapes) for a in (sg, sd, sm, sv))

    def pick(i):
        out = []
        for nme in WEIGHT_ORDER:
            out.append(big[nme][i] if nme in big else (sg, sd, sm, sv)[i][nme])
        return out

    return (sg["loss"], dh0.reshape(bl, s_len, d), *pick(0), *pick(1), *pick(2), *pick(3))
```

```python
import functools

import jax
import jax.numpy as jnp
import numpy as np
from jax import lax
from jax.experimental import pallas as pl
from jax.experimental.pallas import tpu as pltpu

F32 = jnp.float32
BF16 = jnp.bfloat16

CHUNK = 64
HEAD_DIM = 64
A_HEADS = 8
A_PREV = 8
A_MAX_REL = 128
B_Q_HEADS = 8
B_KV_HEADS = 2
B_PREV = 2
A_WIDTH = A_HEADS * HEAD_DIM
B_Q_WIDTH = B_Q_HEADS * HEAD_DIM
B_KV_WIDTH = B_KV_HEADS * HEAD_DIM
IN_COLS = 3 * A_WIDTH + B_Q_WIDTH + 2 * B_KV_WIDTH
EPS = 1e-6
NEG_INF = -1e30
ADAM_LR = 0.001
ADAM_B1 = 0.9
ADAM_B2 = 0.999
ADAM_EPS = 1e-08
ADAM_WD = 0.01
ADAM_STEP = 10

N_DEV = 8
LANES = 128
QTILE = 2 * CHUNK
VMEM_LIMIT = 56 * 1024 * 1024

MESH_ID = pl.DeviceIdType.MESH
ANY = pl.BlockSpec(memory_space=pl.ANY)
HBM = pl.BlockSpec(memory_space=pltpu.HBM)
SEM = pl.BlockSpec(memory_space=pltpu.SEMAPHORE)
SIDE_EFFECT = pltpu.SideEffectType.DATAFLOW_SIDE_EFFECTING


def _dot(a, b):
    return jnp.dot(a, b, preferred_element_type=F32)


def _dot_nt(a, b):
    return lax.dot_general(a, b, (((1,), (1,)), ((), ())), preferred_element_type=F32)


def _dot_tn(a, b):
    return lax.dot_general(a, b, (((0,), (0,)), ((), ())), preferred_element_type=F32)


def _params(sem=None, vmem=VMEM_LIMIT):
    return pltpu.CompilerParams(dimension_semantics=sem, vmem_limit_bytes=vmem)


def _row_tile(t, want):
    while t % want:
        want //= 2
    return want


def _place():
    return lax.axis_index("x"), lax.axis_index("y"), lax.axis_index("c")


def _all_gather(shards, name):
    n = len(shards)

    def body(*refs):
        ins, outs = refs[:n], refs[n:2 * n]
        send_sems, recv_sems, local_sems = refs[2 * n:]
        x, y, c = _place()
        me, sib = (x, y, c), (x, y, 1 - c)
        chips = [(1 - x, y), (x, 1 - y), (1 - x, 1 - y)]

        def copy(w, k, block, to, src=None):
            px, py, pc = block
            dst = outs[w].at[4 * px + 2 * py + pc]
            return pltpu.make_async_remote_copy(
                src_ref=dst if src is None else src, dst_ref=dst,
                send_sem=send_sems.at[w * 7 + k], recv_sem=recv_sems.at[w * 7 + k],
                device_id=to, device_id_type=MESH_ID)

        mine = [pltpu.make_async_copy(ins[w], outs[w].at[4 * x + 2 * y + c], local_sems.at[w]) for w in range(n)]
        for cp in mine:
            cp.start()
        first = []
        for w in range(n):
            first.append(copy(w, 0, me, sib, src=ins[w]))
            first += [copy(w, 1 + j, me, (*chip, c), src=ins[w]) for j, chip in enumerate(chips)]
        for cp in first:
            cp.start()
        passed = []
        for j, chip in enumerate(chips):
            for w in range(n):
                copy(w, 1 + j, (*chip, c), me).wait_recv()
                fwd = copy(w, 4 + j, (*chip, c), sib)
                fwd.start()
                passed.append(fwd)
        for w in range(n):
            copy(w, 0, sib, me).wait_recv()
        for j, chip in enumerate(chips):
            for w in range(n):
                copy(w, 4 + j, (*chip, 1 - c), me).wait_recv()
        for cp in first + passed:
            cp.wait_send()
        for cp in mine:
            cp.wait()

    return pl.pallas_call(
        body, name=name,
        out_shape=[jax.ShapeDtypeStruct((N_DEV,) + s.shape, s.dtype) for s in shards],
        in_specs=[ANY] * n, out_specs=[ANY] * n,
        scratch_shapes=[pltpu.SemaphoreType.DMA((7 * n,)), pltpu.SemaphoreType.DMA((7 * n,)),
                        pltpu.SemaphoreType.DMA((n,))],
    )(*shards)


def _gather_level(bufs, send_sems, recv_sems, level):
    x, y, c = _place()
    me, sib = (x, y, c), (x, y, 1 - c)
    chips = [(1 - x, y), (x, 1 - y), (1 - x, 1 - y)]

    def copy(w, k, block, to):
        px, py, pc = block
        rows = bufs[w].at[4 * px + 2 * py + pc]
        return pltpu.make_async_remote_copy(src_ref=rows, dst_ref=rows, send_sem=send_sems.at[k], recv_sem=recv_sems.at[k],
                                            device_id=to, device_id_type=MESH_ID)

    out, arriving = [], []
    for w in range(len(bufs)):
        if level == 1:
            out.append(copy(w, 4 * w, me, sib))
            arriving.append(copy(w, 4 * w, sib, me))
        for j, chip in enumerate(chips):
            if level == 1:
                out.append(copy(w, 4 * w + 1 + j, me, (*chip, c)))
                arriving.append(copy(w, 4 * w + 1 + j, (*chip, c), me))
            else:
                out.append(copy(w, 3 * w + j, (*chip, c), sib))
                arriving.append(copy(w, 3 * w + j, (*chip, 1 - c), me))
    return out, arriving


def _split_call(body, name, bufs, sems_in, after, n_sems_out, token):
    n = len(bufs)
    out_shape = [pltpu.SemaphoreType.DMA((n_sems_out,))] * (2 if n_sems_out else 0)
    out_shape += [pltpu.HBM(a.shape, a.dtype) for a in bufs]
    out_specs = [SEM] * (2 if n_sems_out else 0) + [HBM] * n
    if token:
        out_shape.append(jax.ShapeDtypeStruct((8, LANES), F32))
        out_specs.append(pl.BlockSpec(memory_space=pltpu.VMEM))
    first = 2 if n_sems_out else 0
    return pl.pallas_call(
        body, name=name, out_shape=tuple(out_shape),
        in_specs=[HBM] * n + [SEM] * len(sems_in) + [ANY], out_specs=tuple(out_specs),
        input_output_aliases={i: first + i for i in range(n)},
        compiler_params=pltpu.CompilerParams(has_side_effects=SIDE_EFFECT),
    )(*bufs, *sems_in, after)


def _gather_start(shards, after, name):
    n = len(shards)
    xi, yi, ci = _place()
    me = 4 * xi + 2 * yi + ci
    bufs = [lax.dynamic_update_slice(lax.empty((N_DEV,) + s.shape, s.dtype), s[None], (me, 0, 0)) for s in shards]
    bufs = [pltpu.with_memory_space_constraint(a, pltpu.HBM) for a in bufs]

    def body(*refs):
        out, _ = _gather_level(refs[:n], refs[n + 1], refs[n + 2], 1)
        for cp in out:
            cp.start()
        refs[-1][...] = jnp.zeros_like(refs[-1])

    outs = _split_call(body, name, bufs, [], after, 4 * n, True)
    return outs[0], outs[1], list(outs[2:2 + n]), outs[-1]


def _gather_pass(send1, recv1, bufs, after, name):
    n = len(bufs)

    def body(*refs):
        out1, in1 = _gather_level(refs[:n], refs[n], refs[n + 1], 1)
        out2, _ = _gather_level(refs[:n], refs[n + 3], refs[n + 4], 2)
        for cp in in1:
            cp.wait_recv()
        for cp in out2:
            cp.start()
        for cp in out1:
            cp.wait_send()
        refs[-1][...] = jnp.zeros_like(refs[-1])

    outs = _split_call(body, name, bufs, [send1, recv1], after, 3 * n, True)
    return outs[0], outs[1], list(outs[2:2 + n]), outs[-1]


def _gather_wait(send2, recv2, bufs, after, name):
    n = len(bufs)

    def body(*refs):
        out2, in2 = _gather_level(refs[:n], refs[n], refs[n + 1], 2)
        for cp in in2:
            cp.wait_recv()
        for cp in out2:
            cp.wait_send()

    return list(_split_call(body, name, bufs, [send2, recv2], after, 0, False))


def _scatter_copies(parts, lands, send_sems, recv_sems):
    x, y, c = _place()
    cps = []
    for w, (part, land) in enumerate(zip(parts, lands)):
        for k in range(1, N_DEV):
            px, py, pc = x ^ ((k >> 2) & 1), y ^ ((k >> 1) & 1), c ^ (k & 1)
            cps.append(pltpu.make_async_remote_copy(
                src_ref=part.at[4 * px + 2 * py + pc], dst_ref=land.at[k - 1],
                send_sem=send_sems.at[7 * w + k - 1], recv_sem=recv_sems.at[7 * w + k - 1],
                device_id=(px, py, pc), device_id_type=MESH_ID))
    return cps


def _scatter_start(parts, after, name):
    n = len(parts)

    def body(*refs):
        ins, lands = refs[:n], refs[n:2 * n]
        send_sems, recv_sems = refs[2 * n + 1], refs[2 * n + 2]
        token = refs[-1]
        for cp in _scatter_copies(ins, lands, send_sems, recv_sems):
            cp.start()
        token[...] = jnp.zeros_like(token)

    land_shapes = [(N_DEV - 1,) + p.shape[1:] for p in parts]
    in_hbm = [pltpu.with_memory_space_constraint(p, pltpu.HBM) for p in parts]
    in_hbm += [pltpu.with_memory_space_constraint(lax.empty(s, p.dtype), pltpu.HBM) for s, p in zip(land_shapes, parts)]
    outs = pl.pallas_call(
        body, name=name,
        out_shape=(pltpu.SemaphoreType.DMA((7 * n,)), pltpu.SemaphoreType.DMA((7 * n,)),
                   *[pltpu.HBM(p.shape, p.dtype) for p in parts],
                   *[pltpu.HBM(s, p.dtype) for s, p in zip(land_shapes, parts)],
                   jax.ShapeDtypeStruct((8, LANES), F32)),
        in_specs=[HBM] * (2 * n) + [ANY],
        out_specs=(SEM, SEM, *[HBM] * (2 * n), pl.BlockSpec(memory_space=pltpu.VMEM)),
        input_output_aliases={i: 2 + i for i in range(2 * n)},
        compiler_params=pltpu.CompilerParams(has_side_effects=SIDE_EFFECT),
    )(*in_hbm, after)
    return outs[0], outs[1], list(outs[2:2 + n]), list(outs[2 + n:2 + 2 * n]), outs[-1]


def _scatter_wait(send_sems, recv_sems, parts, lands, after, name):
    n = len(parts)

    def body(*refs):
        ins, lnd = refs[:n], refs[n:2 * n]
        for cp in _scatter_copies(ins, lnd, refs[2 * n], refs[2 * n + 1]):
            cp.wait_send()
            cp.wait_recv()

    outs = pl.pallas_call(
        body, name=name,
        out_shape=tuple(pltpu.HBM(a.shape, a.dtype) for a in parts + lands),
        in_specs=[HBM] * (2 * n) + [SEM, SEM, ANY],
        out_specs=tuple([HBM] * (2 * n)),
        input_output_aliases={i: i for i in range(2 * n)},
        compiler_params=pltpu.CompilerParams(has_side_effects=SIDE_EFFECT),
    )(*parts, *lands, send_sems, recv_sems, after)
    return list(outs[n:])


def _adam(w, g, m, v):
    m2 = ADAM_B1 * m + (1.0 - ADAM_B1) * g
    v2 = ADAM_B2 * v + (1.0 - ADAM_B2) * (g * g)
    m_hat = m2 / (1.0 - ADAM_B1 ** ADAM_STEP)
    v_hat = v2 / (1.0 - ADAM_B2 ** ADAM_STEP)
    delta = -ADAM_LR * (m_hat / (jnp.sqrt(v_hat) + ADAM_EPS) + ADAM_WD * w)
    return delta, m2, v2


def _small_allreduce_adam(part, w, m, v, name):
    rows = part.shape[0]

    def body(p_ref, w_ref, m_ref, v_ref, g_ref, d_ref, mo_ref, vo_ref, buf, send_sems, recv_sems):
        x, y, c = _place()
        buf[0] = p_ref[...]
        cps = []
        for k in range(1, N_DEV):
            kx, ky, kc = (k >> 2) & 1, (k >> 1) & 1, k & 1
            peer = (x ^ kx, y ^ ky, c ^ kc)
            cps.append(pltpu.make_async_remote_copy(
                src_ref=p_ref, dst_ref=buf.at[k], send_sem=send_sems.at[k - 1], recv_sem=recv_sems.at[k - 1],
                device_id=peer, device_id_type=MESH_ID))
        for cp in cps:
            cp.start()
        for cp in cps:
            cp.wait()
        me = 4 * x + 2 * y + c
        total = buf[me]
        for d in range(1, N_DEV):
            total = total + buf[d ^ me]
        g_ref[...] = total
        delta, m2, v2 = _adam(w_ref[...], total, m_ref[...], v_ref[...])
        d_ref[...] = delta
        mo_ref[...] = m2
        vo_ref[...] = v2

    vm = pl.BlockSpec(memory_space=pltpu.VMEM)
    return pl.pallas_call(
        body, name=name,
        out_shape=[jax.ShapeDtypeStruct(part.shape, F32)] * 4,
        in_specs=[vm] * 4, out_specs=[vm] * 4,
        scratch_shapes=[pltpu.VMEM((N_DEV, rows, LANES), F32),
                        pltpu.SemaphoreType.DMA((N_DEV - 1,)), pltpu.SemaphoreType.DMA((N_DEV - 1,))],
    )(part, w, m, v)


def _final_adam(g8, land, w, m, v, me, name):
    _, r, c = g8.shape
    tr = _row_tile(r, 256) if r % 8 == 0 and r > 512 else r

    def body(me_ref, g_ref, land_ref, w_ref, m_ref, v_ref, go_ref, d_ref, mo_ref, vo_ref):
        g = g_ref[...]
        for k in range(N_DEV - 1):
            g = g + land_ref[k].astype(F32)
        go_ref[...] = g
        delta, m2, v2 = _adam(w_ref[...], g, m_ref[...], v_ref[...])
        d_ref[...] = delta
        mo_ref[...] = m2
        vo_ref[...] = v2

    plain = pl.BlockSpec((tr, c), lambda i, s: (i, 0))
    return pl.pallas_call(
        body, name=name,
        out_shape=[jax.ShapeDtypeStruct((r, c), F32)] * 4,
        grid_spec=pltpu.PrefetchScalarGridSpec(
            num_scalar_prefetch=1, grid=(r // tr,),
            in_specs=[pl.BlockSpec((None, tr, c), lambda i, s: (s[0], i, 0)),
                      pl.BlockSpec((N_DEV - 1, tr, c), lambda i, s: (0, i, 0)),
                      plain, plain, plain],
            out_specs=[plain] * 4),
        compiler_params=_params(("arbitrary",)),
    )(me, g8, land, w, m, v)


def _rms(x, gain):
    r = lax.rsqrt(jnp.mean(x * x, axis=-1, keepdims=True) + EPS)
    xh = x * r
    return xh * gain, xh, r


def _rms_bwd(xh, r, gain, dy):
    gdy = gain * dy
    dx = r * (gdy - xh * jnp.mean(xh * gdy, axis=-1, keepdims=True))
    return dx, jnp.sum(dy * xh, axis=0, keepdims=True)


def _load_weights(pairs, sems):
    cps = [pltpu.make_async_copy(src, dst, sems.at[i]) for i, (src, dst) in enumerate(pairs)]
    for cp in cps:
        cp.start()
    for cp in cps:
        cp.wait()


def _ffn_fwd(h, gain, wgu, wd, name):
    t, d = h.shape
    nb, _, nf = wgu.shape
    nh = nb // 2
    tm = _row_tile(t, 256)

    def body(h_ref, g_ref, wgu_hbm, wd_hbm, out_ref, gu_ref, wgu_v, wd_v, sems):
        @pl.when(pl.program_id(0) == 0)
        def _():
            _load_weights([(wgu_hbm, wgu_v), (wd_hbm, wd_v)], sems)

        x = h_ref[...]
        n, _, _ = _rms(x, g_ref[...])
        nbf = n.astype(BF16)
        acc = jnp.zeros((tm, d), F32)
        for j in range(nh):
            g = _dot(nbf, wgu_v[j])
            u = _dot(nbf, wgu_v[j + nh])
            gu_ref[j] = g.astype(BF16)
            gu_ref[j + nh] = u.astype(BF16)
            a = (g * jax.nn.sigmoid(g)) * u
            acc = acc + _dot(a.astype(BF16), wd_v[j])
        out_ref[...] = x + 0.5 * acc

    return pl.pallas_call(
        body, name=name, grid=(t // tm,),
        out_shape=[jax.ShapeDtypeStruct((t, d), F32), jax.ShapeDtypeStruct((nb, t, nf), BF16)],
        in_specs=[pl.BlockSpec((tm, d), lambda i: (i, 0)), pl.BlockSpec((1, d), lambda i: (0, 0)), ANY, ANY],
        out_specs=[pl.BlockSpec((tm, d), lambda i: (i, 0)), pl.BlockSpec((nb, tm, nf), lambda i: (0, i, 0))],
        scratch_shapes=[pltpu.VMEM(wgu.shape, BF16), pltpu.VMEM(wd.shape, BF16), pltpu.SemaphoreType.DMA((2,))],
        compiler_params=_params(("arbitrary",)),
    )(h, gain, wgu, wd)


def _ffn_bwd(dh, h, gain, gu, wgu, wd, name):
    t, d = h.shape
    nb, _, nf = wgu.shape
    nh = nb // 2
    tm = _row_tile(t, 256)

    def body(dh_ref, h_ref, g_ref, gu_ref, wgu_hbm, wd_hbm, dhp_ref, dgu_ref, a_ref, n_ref, dgain_ref,
             wgu_v, wd_v, sems):
        @pl.when(pl.program_id(0) == 0)
        def _():
            _load_weights([(wgu_hbm, wgu_v), (wd_hbm, wd_v)], sems)
            dgain_ref[...] = jnp.zeros_like(dgain_ref)

        x = h_ref[...]
        gain_v = g_ref[...]
        n, xh, r = _rms(x, gain_v)
        n_ref[...] = n.astype(BF16)
        dh_v = dh_ref[...]
        dfb = (0.5 * dh_v).astype(BF16)
        dn = jnp.zeros((tm, d), F32)
        for j in range(nh):
            da = _dot_nt(dfb, wd_v[j])
            g = gu_ref[j].astype(F32)
            u = gu_ref[j + nh].astype(F32)
            sg = jax.nn.sigmoid(g)
            si = g * sg
            dg = (da * u * (sg * (1.0 + g * (1.0 - sg)))).astype(BF16)
            du = (da * si).astype(BF16)
            a_ref[j] = (si * u).astype(BF16)
            dgu_ref[j] = dg
            dgu_ref[j + nh] = du
            dn = dn + _dot_nt(dg, wgu_v[j]) + _dot_nt(du, wgu_v[j + nh])
        dx, dgain = _rms_bwd(xh, r, gain_v, dn)
        dhp_ref[...] = dh_v + dx
        dgain_ref[...] += dgain

    row = pl.BlockSpec((tm, d), lambda i: (i, 0))
    vec = pl.BlockSpec((1, d), lambda i: (0, 0))
    return pl.pallas_call(
        body, name=name, grid=(t // tm,),
        out_shape=[jax.ShapeDtypeStruct((t, d), F32), jax.ShapeDtypeStruct((nb, t, nf), BF16),
                   jax.ShapeDtypeStruct((nh, t, nf), BF16), jax.ShapeDtypeStruct((t, d), BF16),
                   jax.ShapeDtypeStruct((1, d), F32)],
        in_specs=[row, row, vec, pl.BlockSpec((nb, tm, nf), lambda i: (0, i, 0)), ANY, ANY],
        out_specs=[row, pl.BlockSpec((nb, tm, nf), lambda i: (0, i, 0)),
                   pl.BlockSpec((nh, tm, nf), lambda i: (0, i, 0)), row, vec],
        scratch_shapes=[pltpu.VMEM(wgu.shape, BF16), pltpu.VMEM(wd.shape, BF16), pltpu.SemaphoreType.DMA((2,))],
        compiler_params=_params(("arbitrary",)),
    )(dh, h, gain, gu, wgu, wd)


def _dw(xa, dy, nb, n, name, scale=1.0, dep=None):
    t, k = xa.shape[-2:]
    tt = _row_tile(t, 512)
    steps = t // tt
    wide = dy.ndim == 2 and xa.ndim == 2
    if xa.ndim == 3:
        x_spec = pl.BlockSpec((nb, tt, k), lambda i: (0, i, 0))
    else:
        x_spec = pl.BlockSpec((tt, k), lambda i: (i, 0))
    if dy.ndim == 3:
        dy_spec = pl.BlockSpec((nb, tt, n), lambda i: (0, i, 0))
    else:
        dy_spec = pl.BlockSpec((tt, dy.shape[1]), lambda i: (i, 0))
    acc_shape = (k, nb * n) if wide else (nb, k, n)
    stage_shape = (k, nb * n) if wide else (k, n)

    def body(x_ref, dy_ref, *rest):
        o_hbm, ob_hbm, acc, stage, sems = rest[-5:]

        @pl.when(pl.program_id(0) == 0)
        def _():
            acc[...] = jnp.zeros_like(acc)

        if wide:
            acc[...] += _dot(x_ref[...].astype(BF16).T, dy_ref[...].astype(BF16))
        elif xa.ndim == 2:
            xt = x_ref[...].astype(BF16).T
            for j in range(nb):
                acc[j] += _dot(xt, dy_ref[j].astype(BF16))
        else:
            dyb = dy_ref[...].astype(BF16)
            for j in range(nb):
                acc[j] += _dot_tn(x_ref[j].astype(BF16), dyb)

        @pl.when(pl.program_id(0) == steps - 1)
        def _():
            if scale != 1.0:
                acc[...] = acc[...] * scale
            if wide:
                cps = [pltpu.make_async_copy(acc.at[:, pl.ds(j * n, n)] if nb > 1 else acc, o_hbm.at[j], sems.at[j])
                       for j in range(nb)]
            else:
                cps = [pltpu.make_async_copy(acc, o_hbm, sems.at[0])]
            for cp in cps:
                cp.start()
            if wide:
                stage[...] = acc[...].astype(BF16)
                bcs = [pltpu.make_async_copy(stage.at[:, pl.ds(j * n, n)] if nb > 1 else stage, ob_hbm.at[j],
                                             sems.at[nb + j]) for j in range(nb)]
                for cp in bcs:
                    cp.start()
                for cp in bcs:
                    cp.wait()
            else:
                for j in range(nb):
                    stage[...] = acc[j].astype(BF16)
                    cp = pltpu.make_async_copy(stage, ob_hbm.at[j], sems.at[nb])
                    cp.start()
                    cp.wait()
            for cp in cps:
                cp.wait()

    return pl.pallas_call(
        body, name=name, grid=(steps,),
        out_shape=[jax.ShapeDtypeStruct((nb, k, n), F32), jax.ShapeDtypeStruct((nb, k, n), BF16)],
        in_specs=[x_spec, dy_spec] + ([] if dep is None else [ANY]),
        out_specs=[ANY, ANY],
        scratch_shapes=[pltpu.VMEM(acc_shape, F32), pltpu.VMEM(stage_shape, BF16),
                        pltpu.SemaphoreType.DMA((2 * nb,))],
        compiler_params=_params(("arbitrary",)),
    )(*((xa, dy) if dep is None else (xa, dy, dep)))


def _proj_fwd(h, gain, win, wgate, name):
    t, d = h.shape
    tm = _row_tile(t, 256)
    nq, ng = win.shape[1], wgate.shape[1]

    def body(h_ref, g_ref, win_ref, wg_ref, un_ref, qkv_ref, gate_ref):
        n, _, _ = _rms(h_ref[...], g_ref[...])
        nbf = n.astype(BF16)
        un_ref[...] = nbf
        qkv_ref[...] = _dot(nbf, win_ref[...])
        gate_ref[...] = jax.nn.sigmoid(_dot(nbf, wg_ref[...]))

    full = lambda a: pl.BlockSpec(a.shape, lambda i: (0,) * a.ndim)
    return pl.pallas_call(
        body, name=name, grid=(t // tm,),
        out_shape=[jax.ShapeDtypeStruct((t, d), BF16), jax.ShapeDtypeStruct((t, nq), F32),
                   jax.ShapeDtypeStruct((t, ng), F32)],
        in_specs=[pl.BlockSpec((tm, d), lambda i: (i, 0)), full(gain), full(win), full(wgate)],
        out_specs=[pl.BlockSpec((tm, d), lambda i: (i, 0)), pl.BlockSpec((tm, nq), lambda i: (i, 0)),
                   pl.BlockSpec((tm, ng), lambda i: (i, 0))],
        compiler_params=_params(("arbitrary",)),
    )(h, gain, win, wgate)


def _proj_bwd(dh, h, gain, dzg, dqkv, win, wgate, name):
    t, d = h.shape
    tm = _row_tile(t, 256)
    nq, ng = win.shape[1], wgate.shape[1]

    def body(dh_ref, h_ref, g_ref, dzg_ref, dqkv_ref, win_ref, wg_ref, dhp_ref, dgain_ref):
        @pl.when(pl.program_id(0) == 0)
        def _():
            dgain_ref[...] = jnp.zeros_like(dgain_ref)

        gain_v = g_ref[...]
        _, xh, r = _rms(h_ref[...], gain_v)
        dun = _dot_nt(dzg_ref[...], wg_ref[...]) + _dot_nt(dqkv_ref[...].astype(BF16), win_ref[...])
        dx, dgain = _rms_bwd(xh, r, gain_v, dun)
        dhp_ref[...] = dh_ref[...] + dx
        dgain_ref[...] += dgain

    full = lambda a: pl.BlockSpec(a.shape, lambda i: (0,) * a.ndim)
    row = pl.BlockSpec((tm, d), lambda i: (i, 0))
    return pl.pallas_call(
        body, name=name, grid=(t // tm,),
        out_shape=[jax.ShapeDtypeStruct((t, d), F32), jax.ShapeDtypeStruct((1, d), F32)],
        in_specs=[row, row, full(gain), pl.BlockSpec((tm, ng), lambda i: (i, 0)),
                  pl.BlockSpec((tm, nq), lambda i: (i, 0)), full(win), full(wgate)],
        out_specs=[row, pl.BlockSpec((1, d), lambda i: (0, 0))],
        compiler_params=_params(("arbitrary",)),
    )(dh, h, gain, dzg, dqkv, win, wgate)


def _merge_fwd(h, ya, yb, gate, wpa, wpb, wout, name):
    t, d = h.shape
    tm = _row_tile(t, 256)

    def body(h_ref, ya_ref, yb_ref, ga_ref, gb_ref, wpa_ref, wpb_ref, wout_ref, out_ref, mg_ref, pa_ref, pb_ref):
        pa = _dot(ya_ref[...].astype(BF16), wpa_ref[...])
        pb = _dot(yb_ref[...].astype(BF16), wpb_ref[...])
        merged = (ga_ref[...] * pa + gb_ref[...] * pb).astype(BF16)
        pa_ref[...] = pa.astype(BF16)
        pb_ref[...] = pb.astype(BF16)
        mg_ref[...] = merged
        out_ref[...] = h_ref[...] + _dot(merged, wout_ref[...])

    full = lambda a: pl.BlockSpec(a.shape, lambda i: (0,) * a.ndim)
    row = pl.BlockSpec((tm, d), lambda i: (i, 0))
    yrow = pl.BlockSpec((tm, ya.shape[1]), lambda i: (i, 0))
    return pl.pallas_call(
        body, name=name, grid=(t // tm,),
        out_shape=[jax.ShapeDtypeStruct((t, d), F32)] + [jax.ShapeDtypeStruct((t, d), BF16)] * 3,
        in_specs=[row, yrow, yrow, pl.BlockSpec((tm, d), lambda i: (i, 0)), pl.BlockSpec((tm, d), lambda i: (i, 1)),
                  full(wpa), full(wpb), full(wout)],
        out_specs=[row] * 4,
        compiler_params=_params(("arbitrary",)),
    )(h, ya, yb, gate, gate, wpa, wpb, wout)


def _merge_bwd(dh, pa, pb, gate, wpa, wpb, wout, name):
    t, d = dh.shape
    tm = _row_tile(t, 256)
    wy = wpa.shape[0]

    def body(dh_ref, pa_ref, pb_ref, ga_ref, gb_ref, wpa_ref, wpb_ref, wout_ref,
             dpa_ref, dpb_ref, dzg_ref, dya_ref, dyb_ref):
        dm = _dot_nt(dh_ref[...].astype(BF16), wout_ref[...])
        ga, gb = ga_ref[...], gb_ref[...]
        dpa = (dm * ga).astype(BF16)
        dpb = (dm * gb).astype(BF16)
        dpa_ref[...] = dpa
        dpb_ref[...] = dpb
        dzg_ref[:, :d] = (dm * pa_ref[...].astype(F32) * ga * (1.0 - ga)).astype(BF16)
        dzg_ref[:, d:] = (dm * pb_ref[...].astype(F32) * gb * (1.0 - gb)).astype(BF16)
        dya_ref[...] = _dot_nt(dpa, wpa_ref[...])
        dyb_ref[...] = _dot_nt(dpb, wpb_ref[...])

    full = lambda a: pl.BlockSpec(a.shape, lambda i: (0,) * a.ndim)
    row = pl.BlockSpec((tm, d), lambda i: (i, 0))
    yrow = pl.BlockSpec((tm, wy), lambda i: (i, 0))
    return pl.pallas_call(
        body, name=name, grid=(t // tm,),
        out_shape=[jax.ShapeDtypeStruct((t, d), BF16), jax.ShapeDtypeStruct((t, d), BF16),
                   jax.ShapeDtypeStruct((t, 2 * d), BF16), jax.ShapeDtypeStruct((t, wy), F32),
                   jax.ShapeDtypeStruct((t, wy), F32)],
        in_specs=[row, row, row, pl.BlockSpec((tm, d), lambda i: (i, 0)), pl.BlockSpec((tm, d), lambda i: (i, 1)),
                  full(wpa), full(wpb), full(wout)],
        out_specs=[row, row, pl.BlockSpec((tm, 2 * d), lambda i: (i, 0)), yrow, yrow],
        compiler_params=_params(("arbitrary",)),
    )(dh, pa, pb, gate, gate, wpa, wpb, wout)


def _ple_loss(h, gain, p, target, wpg, wpe, name):
    t, d = h.shape
    tm = _row_tile(t, 256)
    pd = p.shape[1]

    def body(h_ref, g_ref, p_ref, t_ref, wpg_ref, wpe_ref, dh_ref, dz_ref, dpp_ref, n_ref, dgain_ref, loss_ref):
        @pl.when(pl.program_id(0) == 0)
        def _():
            dgain_ref[...] = jnp.zeros_like(dgain_ref)
            loss_ref[...] = jnp.zeros_like(loss_ref)

        x = h_ref[...]
        gain_v = g_ref[...]
        n, xh, r = _rms(x, gain_v)
        nbf = n.astype(BF16)
        n_ref[...] = nbf
        pg = jax.nn.sigmoid(_dot(nbf, wpg_ref[...]))
        pp = _dot(p_ref[...].astype(BF16), wpe_ref[...])
        err = (x + pg * pp) - t_ref[...]
        loss_ref[...] += 0.5 * jnp.sum(jnp.mean(err * err, axis=-1, keepdims=True))
        dy = err * (1.0 / d)
        dpp_ref[...] = (dy * pg).astype(BF16)
        dz = (dy * pp * pg * (1.0 - pg)).astype(BF16)
        dz_ref[...] = dz
        dn = _dot_nt(dz, wpg_ref[...])
        dx, dgain = _rms_bwd(xh, r, gain_v, dn)
        dh_ref[...] = dy + dx
        dgain_ref[...] += dgain

    full = lambda a: pl.BlockSpec(a.shape, lambda i: (0,) * a.ndim)
    row = pl.BlockSpec((tm, d), lambda i: (i, 0))
    return pl.pallas_call(
        body, name=name, grid=(t // tm,),
        out_shape=[jax.ShapeDtypeStruct((t, d), F32), jax.ShapeDtypeStruct((t, d), BF16),
                   jax.ShapeDtypeStruct((t, d), BF16), jax.ShapeDtypeStruct((t, d), BF16),
                   jax.ShapeDtypeStruct((1, d), F32), jax.ShapeDtypeStruct((8, LANES), F32)],
        in_specs=[row, full(gain), pl.BlockSpec((tm, pd), lambda i: (i, 0)), row, full(wpg), full(wpe)],
        out_specs=[row, row, row, row, pl.BlockSpec((1, d), lambda i: (0, 0)),
                   pl.BlockSpec((8, LANES), lambda i: (0, 0))],
        compiler_params=_params(("arbitrary",)),
    )(h, gain, p, target, wpg, wpe)


def _head_masks():
    lane = lax.broadcasted_iota(jnp.int32, (1, LANES), 1)
    m0 = (lane < HEAD_DIM).astype(F32)
    return m0, 1.0 - m0


def _head_mean(v, m0, m1):
    s0 = jnp.sum(v * m0, axis=-1, keepdims=True)
    s1 = jnp.sum(v * m1, axis=-1, keepdims=True)
    return (s0 * m0 + s1 * m1) * (1.0 / HEAD_DIM)


def _head_norm(x, gain, m0, m1):
    r = lax.rsqrt(_head_mean(x * x, m0, m1) + EPS)
    xh = x * r
    return xh * gain, xh, r


def _head_norm_bwd(xh, r, gain, dy, m0, m1):
    gdy = gain * dy
    dx = r * (gdy - xh * _head_mean(xh * gdy, m0, m1))
    return dx, jnp.sum(dy * xh, axis=0, keepdims=True)


def _attn_prep(mode, pair, s_len, padk, q_ref, k_ref, v_ref, gq_ref, gk_ref, qs, k0, k1, v0, v1):
    m0, m1 = _head_masks()
    zpad = jnp.zeros((padk, LANES), BF16)
    for buf in (k0, k1, v0, v1):
        buf[pl.ds(0, padk), :] = zpad
    first_kv = (pair // 2) == 0
    rt = _row_tile(s_len, 256)

    def step(i, carry):
        rows = pl.ds(pl.multiple_of(i * rt, rt), rt)
        dst = pl.ds(pl.multiple_of(padk + i * rt, QTILE), rt)
        qn, _, _ = _head_norm(q_ref[rows, :], gq_ref[...], m0, m1)
        kn, _, _ = _head_norm(k_ref[rows, :], gk_ref[...], m0, m1)
        vv = v_ref[rows, :]
        qs[rows, :] = (qn * (HEAD_DIM ** -0.5)).astype(BF16)
        if mode == "B":
            kn = jnp.where(first_kv, kn, pltpu.roll(kn, HEAD_DIM, 1))
            vv = jnp.where(first_kv, vv, pltpu.roll(vv, HEAD_DIM, 1))
            ka, va = kn * m0, vv * m0
            kb, vb = pltpu.roll(ka, HEAD_DIM, 1), pltpu.roll(va, HEAD_DIM, 1)
        else:
            ka, kb, va, vb = kn * m0, kn * m1, vv * m0, vv * m1
        k0[dst, :] = ka.astype(BF16)
        k1[dst, :] = kb.astype(BF16)
        v0[dst, :] = va.astype(BF16)
        v1[dst, :] = vb.astype(BF16)
        return carry

    lax.fori_loop(0, s_len // rt, step, 0)


def _attn_probs(mode, q2, kb, bias, ok, sink):
    s = _dot_nt(q2, kb) + bias
    s = jnp.where(ok, s, NEG_INF)
    mx = jnp.max(s, axis=-1, keepdims=True)
    if mode == "B":
        mx = jnp.maximum(mx, sink)
    e = jnp.exp(s - mx)
    l = jnp.sum(e, axis=-1, keepdims=True)
    if mode == "B":
        l = l + jnp.exp(sink - mx)
    return e, mx, l


def _attn_cols(mode):
    if mode == "A":
        return (lambda b, p: (b, p)), (lambda b, p: (b, 4 + p)), (lambda b, p: (b, 8 + p))
    return (lambda b, p: (b, 12 + p)), (lambda b, p: (b, 16)), (lambda b, p: (b, 17))


def _attn_fwd(mode, qkv, gq, gk, bias, sinks, bl, s_len, name):
    bw = bias.shape[-1]
    padk = bw - QTILE
    nt = s_len // QTILE
    qmap, kmap, vmap = _attn_cols(mode)

    def body(q_ref, k_ref, v_ref, gq_ref, gk_ref, bias_ref, sink_ref, o_ref, qs, k0, k1, v0, v1):
        pair = pl.program_id(1)
        _attn_prep(mode, pair, s_len, padk, q_ref, k_ref, v_ref, gq_ref, gk_ref, qs, k0, k1, v0, v1)
        col = lax.broadcasted_iota(jnp.int32, (QTILE, bw), 1)

        def tile(m, carry):
            r0 = pl.multiple_of(m * QTILE, QTILE)
            q2 = qs[pl.ds(r0, QTILE), :]
            ok = col >= (padk - r0)
            acc = jnp.zeros((QTILE, LANES), F32)
            for hh, (kk, vv) in enumerate(((k0, v0), (k1, v1))):
                sink = sink_ref[2 * pair + hh]
                e, _, l = _attn_probs(mode, q2, kk[pl.ds(r0, bw), :], bias_ref[hh], ok, sink)
                acc = acc + _dot(e.astype(BF16), vv[pl.ds(r0, bw), :]) / l
            o_ref[pl.ds(r0, QTILE), :] = acc
            return carry

        lax.fori_loop(0, nt, tile, 0, unroll=2)

    blk = lambda f: pl.BlockSpec((s_len, LANES), f)
    vec = pl.BlockSpec((1, LANES), lambda b, p: (0, 0))
    return pl.pallas_call(
        body, name=name, grid=(bl, 4),
        out_shape=jax.ShapeDtypeStruct((bl * s_len, 4 * LANES), F32),
        in_specs=[blk(qmap), blk(kmap), blk(vmap), vec, vec,
                  pl.BlockSpec((2, QTILE, bw), lambda b, p: (p, 0, 0)),
                  pl.BlockSpec(memory_space=pltpu.SMEM)],
        out_specs=pl.BlockSpec((s_len, LANES), lambda b, p: (b, p)),
        scratch_shapes=[pltpu.VMEM((s_len, LANES), BF16)] + [pltpu.VMEM((s_len + padk, LANES), BF16)] * 4,
        compiler_params=_params(("arbitrary", "arbitrary")),
    )(qkv, qkv, qkv, gq, gk, bias, sinks)


def _attn_bwd(mode, qkv, gq, gk, bias, sinks, y, dy, bl, s_len, name):
    bw = bias.shape[-1]
    padk = bw - QTILE
    nt = s_len // QTILE
    qmap, kmap, vmap = _attn_cols(mode)
    t = bl * s_len
    kvw = 4 * LANES if mode == "A" else LANES

    def body(q_ref, k_ref, v_ref, gq_ref, gk_ref, bias_ref, sink_ref, y_ref, dy_ref,
             dq_ref, dk_ref, dv_ref, dgq_ref, dgk_ref, dbias_ref, dsink_ref,
             qs, k0, k1, v0, v1, dqs, dk0, dk1, dv0, dv1):
        pair = pl.program_id(1)
        m0, m1 = _head_masks()
        _attn_prep(mode, pair, s_len, padk, q_ref, k_ref, v_ref, gq_ref, gk_ref, qs, k0, k1, v0, v1)
        for buf in (dk0, dk1, dv0, dv1):
            buf[...] = jnp.zeros_like(buf)
        dbias_ref[...] = jnp.zeros_like(dbias_ref)
        col = lax.broadcasted_iota(jnp.int32, (QTILE, bw), 1)
        lane8 = lax.broadcasted_iota(jnp.int32, (8, LANES), 1)

        def tile(m, dsink):
            r0 = pl.multiple_of(m * QTILE, QTILE)
            rows = pl.ds(r0, QTILE)
            band = pl.ds(r0, bw)
            q2 = qs[rows, :]
            do2 = dy_ref[rows, :]
            dd = do2 * y_ref[rows, :]
            dob = do2.astype(BF16)
            ok = col >= (padk - r0)
            dq = jnp.zeros((QTILE, LANES), F32)
            for hh, (kk, vv, dkk, dvv, mh) in enumerate(((k0, v0, dk0, dv0, m0), (k1, v1, dk1, dv1, m1))):
                sink = sink_ref[2 * pair + hh]
                kb = kk[band, :]
                e, mx, l = _attn_probs(mode, q2, kb, bias_ref[hh], ok, sink)
                inv = 1.0 / l
                pn = e * inv
                delta = jnp.sum(dd * mh, axis=-1, keepdims=True)
                dp = _dot_nt(dob, vv[band, :])
                ds = pn * (dp - delta)
                if mode == "A":
                    dbias_ref[hh] += ds
                else:
                    ps = jnp.exp(sink - mx) * inv
                    dsink = dsink + jnp.where(lane8 == hh, -jnp.sum(ps * delta), 0.0)
                dsb = ds.astype(BF16)
                dvv[band, :] += _dot_tn(pn.astype(BF16), dob)
                dkk[band, :] += _dot_tn(dsb, q2)
                dq = dq + _dot(dsb, kb)
            dqs[rows, :] = dq * (HEAD_DIM ** -0.5)
            return dsink

        dsink = lax.fori_loop(0, nt, tile, jnp.zeros((8, LANES), F32), unroll=2)
        dsink_ref[...] = dsink

        first_kv = (pair // 2) == 0
        rt = _row_tile(s_len, 256)

        def post(i, carry):
            dgq, dgk = carry
            rows = pl.ds(pl.multiple_of(i * rt, rt), rt)
            src = pl.ds(pl.multiple_of(padk + i * rt, QTILE), rt)
            gq_v, gk_v = gq_ref[...], gk_ref[...]
            _, qh, qr = _head_norm(q_ref[rows, :], gq_v, m0, m1)
            _, kh, kr = _head_norm(k_ref[rows, :], gk_v, m0, m1)
            dq_raw, dgq_i = _head_norm_bwd(qh, qr, gq_v, dqs[rows, :], m0, m1)
            if mode == "A":
                dkn = dk0[src, :] * m0 + dk1[src, :] * m1
                dvn = dv0[src, :] * m0 + dv1[src, :] * m1
            else:
                dkn = dk0[src, :] * m0 + pltpu.roll(dk1[src, :] * m1, HEAD_DIM, 1)
                dvn = dv0[src, :] * m0 + pltpu.roll(dv1[src, :] * m1, HEAD_DIM, 1)
                dkn = jnp.where(first_kv, dkn, pltpu.roll(dkn, HEAD_DIM, 1))
                dvn = jnp.where(first_kv, dvn, pltpu.roll(dvn, HEAD_DIM, 1))
            dk_raw, dgk_i = _head_norm_bwd(kh, kr, gk_v, dkn, m0, m1)
            dq_ref[rows, :] = dq_raw
            if mode == "A":
                dk_ref[rows, :] = dk_raw
                dv_ref[rows, :] = dvn
            else:
                @pl.when(pair == 0)
                def _():
                    dk_ref[rows, :] = dk_raw
                    dv_ref[rows, :] = dvn

                @pl.when(pair != 0)
                def _():
                    dk_ref[rows, :] += dk_raw
                    dv_ref[rows, :] += dvn
            return dgq + dgq_i, dgk + dgk_i

        z = jnp.zeros((1, LANES), F32)
        dgq, dgk = lax.fori_loop(0, s_len // rt, post, (z, z))
        dgq_ref[...] = jnp.broadcast_to(dgq, (8, LANES))
        dgk_ref[...] = jnp.broadcast_to(dgk, (8, LANES))

    blk = lambda f: pl.BlockSpec((s_len, LANES), f)
    vec = pl.BlockSpec((1, LANES), lambda b, p: (0, 0))
    small = pl.BlockSpec((None, None, 8, LANES), lambda b, p: (b, p, 0, 0))
    kvmap = (lambda b, p: (b, p)) if mode == "A" else (lambda b, p: (b, 0))
    pad_f32 = pltpu.VMEM((s_len + padk, LANES), F32)
    pad_bf = pltpu.VMEM((s_len + padk, LANES), BF16)
    return pl.pallas_call(
        body, name=name, grid=(bl, 4),
        out_shape=[jax.ShapeDtypeStruct((t, 4 * LANES), F32), jax.ShapeDtypeStruct((t, kvw), F32),
                   jax.ShapeDtypeStruct((t, kvw), F32),
                   jax.ShapeDtypeStruct((bl, 4, 8, LANES), F32), jax.ShapeDtypeStruct((bl, 4, 8, LANES), F32),
                   jax.ShapeDtypeStruct((bl, 8, QTILE, bw), F32), jax.ShapeDtypeStruct((bl, 4, 8, LANES), F32)],
        in_specs=[blk(qmap), blk(kmap), blk(vmap), vec, vec,
                  pl.BlockSpec((2, QTILE, bw), lambda b, p: (p, 0, 0)),
                  pl.BlockSpec(memory_space=pltpu.SMEM),
                  blk(lambda b, p: (b, p)), blk(lambda b, p: (b, p))],
        out_specs=[blk(lambda b, p: (b, p)), blk(kvmap), blk(kvmap), small, small,
                   pl.BlockSpec((None, 2, QTILE, bw), lambda b, p: (b, p, 0, 0)), small],
        scratch_shapes=[pltpu.VMEM((s_len, LANES), BF16), pad_bf, pad_bf, pad_bf, pad_bf,
                        pltpu.VMEM((s_len, LANES), F32), pad_f32, pad_f32, pad_f32, pad_f32],
        compiler_params=_params(("arbitrary", "arbitrary")),
    )(qkv, qkv, qkv, gq, gk, bias, sinks, y, dy)


def _band_geometry(prev):
    bw = QTILE + prev * CHUNK
    i = np.arange(QTILE)[:, None]
    j = np.arange(bw)[None, :]
    dist = i + prev * CHUNK - j
    valid = (j // CHUNK >= i // CHUNK) & (j // CHUNK <= i // CHUNK + prev)
    return dist, valid


A_VAR0 = (A_PREV * CHUNK - A_MAX_REL) // LANES * LANES


A_NVAR = QTILE + A_PREV * CHUNK - A_VAR0


def _skew_rows(x, sign):
    rows, n = x.shape
    row = lax.broadcasted_iota(jnp.int32, x.shape, 0)
    b = 1
    while b < rows:
        x = jnp.where((row & b) != 0, pltpu.roll(x, (sign * b) % n, 1), x)
        b *= 2
    return x


def _rel_bias_expand(table, name):
    _, valid = _band_geometry(A_PREV)
    bw = valid.shape[1]
    valid_f = jnp.asarray(valid.astype(np.float32))
    rev = jnp.flip(table[:, 1:], axis=1).reshape(A_HEADS, 1, A_NVAR)

    def body(rev_ref, valid_ref, o_ref):
        rowv = jnp.broadcast_to(rev_ref[...], (QTILE, A_NVAR))
        top = rowv[:, 0:1]
        var = _skew_rows(rowv, 1)
        row = lax.broadcasted_iota(jnp.int32, (QTILE, A_NVAR), 0)
        colv = lax.broadcasted_iota(jnp.int32, (QTILE, A_NVAR), 1)
        var = jnp.where(colv < row, top, var)
        ok = valid_ref[...] > 0.5
        o_ref[:, :A_VAR0] = jnp.where(ok[:, :A_VAR0], top, NEG_INF)
        o_ref[:, A_VAR0:] = jnp.where(ok[:, A_VAR0:], var, NEG_INF)

    return pl.pallas_call(
        body, name=name, grid=(A_HEADS,),
        out_shape=jax.ShapeDtypeStruct((A_HEADS, QTILE, bw), F32),
        in_specs=[pl.BlockSpec((None, 1, A_NVAR), lambda h: (h, 0, 0)), pl.BlockSpec((QTILE, bw), lambda h: (0, 0))],
        out_specs=pl.BlockSpec((None, QTILE, bw), lambda h: (h, 0, 0)),
        compiler_params=_params(("arbitrary",)),
    )(rev, valid_f)


def _rel_bias_grad(dbias, name):
    bl = dbias.shape[0]
    bw = dbias.shape[-1]

    def body(db_ref, o_ref):
        g = db_ref[0]
        for b in range(1, bl):
            g = g + db_ref[b]
        sk = _skew_rows(g[:, A_VAR0:], -1)
        row = lax.broadcasted_iota(jnp.int32, (QTILE, A_NVAR), 0)
        colv = lax.broadcasted_iota(jnp.int32, (QTILE, A_NVAR), 1)
        wrapped = (row + colv) >= A_NVAR
        main = jnp.sum(jnp.where(wrapped, 0.0, sk), axis=0, keepdims=True)
        top = jnp.sum(g[:, :A_VAR0]) + jnp.sum(jnp.where(wrapped, sk, 0.0))
        o_ref[:, :A_NVAR] = jnp.broadcast_to(main, (8, A_NVAR))
        o_ref[:, A_NVAR:] = jnp.full((8, LANES), top, F32)

    out = pl.pallas_call(
        body, name=name, grid=(A_HEADS,),
        out_shape=jax.ShapeDtypeStruct((A_HEADS, 8, A_NVAR + LANES), F32),
        in_specs=[pl.BlockSpec((bl, None, QTILE, bw), lambda h: (0, h, 0, 0))],
        out_specs=pl.BlockSpec((None, 8, A_NVAR + LANES), lambda h: (h, 0, 0)),
        compiler_params=_params(("arbitrary",)),
    )(dbias)
    main, top = out[:, 0, :A_NVAR], out[:, 0, A_NVAR]
    fm = jnp.flip(main, axis=1)
    return jnp.concatenate([jnp.zeros((A_HEADS, 1), F32), fm[:, :-1], fm[:, -1:] + top[:, None]], axis=1)


def _alibi_bias():
    dist, valid = _band_geometry(B_PREV)
    slopes = np.array([2.0 ** (-8.0 * (h + 1) / B_Q_HEADS) for h in range(B_Q_HEADS)], dtype=np.float32)
    bias = -slopes[:, None, None] * np.abs(dist).astype(np.float32)[None]
    return jnp.asarray(np.where(valid[None], bias, np.float32(NEG_INF)).astype(np.float32))


SMALL_NAMES = ("ffn1_norm", "mix_norm", "ffn2_norm", "ple_norm", "a_q_norm", "a_k_norm", "b_q_norm", "b_k_norm",
               "a_rel_bias", "b_sinks", "loss")


def _pack_small(vals):
    rows = []
    for nme in SMALL_NAMES:
        v = vals[nme].astype(F32)
        if nme == "a_rel_bias":
            v = jnp.pad(v.reshape(A_HEADS, -1), ((0, 0), (0, 3 * LANES - (2 * A_MAX_REL + 1))))
        v = v.reshape(-1)
        v = jnp.pad(v, (0, (-v.shape[0]) % LANES))
        rows.append(v.reshape(-1, LANES))
    out = jnp.concatenate(rows, axis=0)
    return jnp.pad(out, ((0, (-out.shape[0]) % 8), (0, 0)))


def _unpack_small(packed, shapes):
    out, r = {}, 0
    for nme in SMALL_NAMES:
        shp = shapes[nme]
        if nme == "a_rel_bias":
            nr = A_HEADS * 3
            out[nme] = packed[r:r + nr].reshape(A_HEADS, 3 * LANES)[:, :2 * A_MAX_REL + 1].reshape(shp)
        else:
            size = int(np.prod(shp)) if shp else 1
            nr = -(-size // LANES)
            out[nme] = packed[r:r + nr].reshape(-1)[:size].reshape(shp)
        r += nr
    return out


BIG_NAMES = ("ffn1_w_gu", "ffn1_w_down", "w_in", "w_gate", "w_proj_a", "w_proj_b", "w_out",
             "ffn2_w_gu", "ffn2_w_down", "w_ple_gate", "w_ple_proj")
ROW_SHARDED = ("ffn1_w_down", "ffn2_w_down", "w_out", "w_ple_gate")
WEIGHT_ORDER = ("ffn1_norm", "ffn1_w_gu", "ffn1_w_down", "mix_norm", "w_in", "a_q_norm", "a_k_norm", "a_rel_bias",
                "b_q_norm", "b_k_norm", "b_sinks", "w_gate", "w_proj_a", "w_proj_b", "w_out", "ffn2_norm",
                "ffn2_w_gu", "ffn2_w_down", "ple_norm", "w_ple_gate", "w_ple_proj")


def _full_cols(wg):
    nb, k, n = wg.shape
    return jnp.transpose(wg, (1, 0, 2)).reshape(k, nb * n)


def _col_blocks(g, nb):
    k, n = g.shape
    return jnp.transpose(g.reshape(k, nb, n // nb), (1, 0, 2))


def _step(x, p, target, w, m, v):
    bl, s_len, d = x.shape
    t = bl * s_len
    h0 = x.reshape(t, d)
    pt = p.reshape(t, p.shape[-1])
    tgt = target.reshape(t, d)

    g_ffn1, g_mix, g_ffn2, g_ple = w["ffn1_norm"], w["mix_norm"], w["ffn2_norm"], w["ple_norm"]
    tile2 = lambda a: jnp.tile(a.reshape(1, HEAD_DIM), (1, 2))
    gqa, gka, gqb, gkb = tile2(w["a_q_norm"]), tile2(w["a_k_norm"]), tile2(w["b_q_norm"]), tile2(w["b_k_norm"])
    sinks = w["b_sinks"].reshape(B_Q_HEADS)
    bias_a = _rel_bias_expand(w["a_rel_bias"][0], "rel_bias_expand")
    bias_b = _alibi_bias()

    shard = {nme: w[nme][0].astype(BF16) for nme in BIG_NAMES}
    wgu1, wd1 = _all_gather([shard["ffn1_w_gu"], shard["ffn1_w_down"]], "weights_gather_ffn1")
    nf = wgu1.shape[2]
    wd1 = wd1.reshape(N_DEV // 2, nf, d)
    mixer_names = ("w_in", "w_gate")
    rest_names = ("w_proj_a", "w_proj_b", "w_out", "ffn2_w_gu", "ffn2_w_down", "w_ple_gate", "w_ple_proj")
    send1, recv1, bufs, token = _gather_start([shard[nme] for nme in mixer_names], wgu1, "gather_start_mixer")

    h1, gu1 = _ffn_fwd(h0, g_ffn1 + token[0, 0], wgu1, wd1, "ffn1_fwd")
    send2, recv2, bufs, token = _gather_pass(send1, recv1, bufs, h1, "gather_pass_mixer")
    send1, recv1, rest_bufs, token = _gather_start([shard[nme] for nme in rest_names], token, "gather_start_rest")
    win, wgate = (_full_cols(a) for a in _gather_wait(send2, recv2, bufs, token, "gather_wait_mixer"))
    un, qkv, gate = _proj_fwd(h1, g_mix, win, wgate, "proj_fwd")
    ya = _attn_fwd("A", qkv, gqa, gka, bias_a, sinks, bl, s_len, "attn_a_fwd")
    yb = _attn_fwd("B", qkv, gqb, gkb, bias_b, sinks, bl, s_len, "attn_b_fwd")
    send2, recv2, rest_bufs, token = _gather_pass(send1, recv1, rest_bufs, yb, "gather_pass_rest")
    gathered = dict(zip(rest_names, _gather_wait(send2, recv2, rest_bufs, token, "gather_wait_rest")))
    wgu2 = gathered["ffn2_w_gu"]
    wd2 = gathered["ffn2_w_down"].reshape(N_DEV // 2, nf, d)
    wpa = _full_cols(gathered["w_proj_a"])
    wpb = _full_cols(gathered["w_proj_b"])
    wpe = _full_cols(gathered["w_ple_proj"])
    wout = gathered["w_out"].reshape(d, d)
    wpg = gathered["w_ple_gate"].reshape(d, d)
    h2, merged, pa, pb = _merge_fwd(h1, ya, yb, gate, wpa, wpb, wout, "merge_fwd")
    h3, gu2 = _ffn_fwd(h2, g_ffn2, wgu2, wd2, "ffn2_fwd")
    dh3, dz4, dpp, n4, dg_ple, loss_part = _ple_loss(h3, g_ple, pt, tgt, wpg, wpe, "ple_loss")

    xi, yi, ci = _place()
    me = jnp.stack([4 * xi + 2 * yi + ci]).astype(jnp.int32)
    g32, g16, big = {}, {}, {}

    def keep(nme, pair, rows=None):
        for store, g in zip((g32, g16), pair):
            if nme == "w_in":
                g = _col_blocks(g[0], N_DEV)
            elif rows is not None:
                g = g.reshape(N_DEV, rows, d)
            store[nme] = g

    def start(names, after, tag):
        send, recv, parts, lands, token = _scatter_start([g16[nme] for nme in names], after, "grads_start_" + tag)
        return names, send, recv, parts, lands, token

    def finish(state, after, tag):
        names, send, recv, parts, lands, _ = state
        lands = _scatter_wait(send, recv, parts, lands, after, "grads_wait_" + tag)
        return names, lands

    def adam(done):
        for nme, land in zip(*done):
            outs = _final_adam(g32[nme], land, w[nme][0], m[nme][0], v[nme][0], me, "adam_" + nme)
            big[nme] = [o[None] for o in outs]

    keep("w_ple_gate", _dw(n4, dz4, 1, d, "dw_ple_gate"), d // N_DEV)
    keep("w_ple_proj", _dw(pt, dpp, N_DEV, d // N_DEV, "dw_ple_proj"))

    dh2, dgu2, a2, n3, dg_ffn2 = _ffn_bwd(dh3, h2, g_ffn2, gu2, wgu2, wd2, "ffn2_bwd")
    keep("ffn2_w_gu", _dw(n3, dgu2, N_DEV, nf, "dw_ffn2_gu"))
    keep("ffn2_w_down", _dw(a2, dh3, N_DEV // 2, d, "dw_ffn2_down", 0.5), nf // 2)
    flight = start(("w_ple_gate", "w_ple_proj", "ffn2_w_gu", "ffn2_w_down"), dh2, "ffn2")

    dpa, dpb, dzg, dya, dyb = _merge_bwd(dh2, pa, pb, gate, wpa, wpb, wout, "merge_bwd")
    keep("w_out", _dw(merged, dh2, 1, d, "dw_out"), d // N_DEV)
    keep("w_proj_a", _dw(ya, dpa, N_DEV, d // N_DEV, "dw_proj_a"))
    keep("w_proj_b", _dw(yb, dpb, N_DEV, d // N_DEV, "dw_proj_b"))
    keep("w_gate", _dw(un, dzg, N_DEV, 2 * d // N_DEV, "dw_gate"))

    tok = flight[-1][0, 0]
    dqa, dka, dva, dgqa, dgka, dbias, _ = _attn_bwd("A", qkv, gqa + tok, gka, bias_a, sinks, ya, dya, bl, s_len,
                                                     "attn_a_bwd")
    dqb, dkb, dvb, dgqb, dgkb, _, dsink = _attn_bwd("B", qkv, gqb, gkb, bias_b, sinks, yb, dyb, bl, s_len, "attn_b_bwd")
    dqkv = jnp.concatenate([dqa, dka, dva, dqb, dkb, dvb], axis=1)
    dtab = _rel_bias_grad(dbias, "rel_bias_grad")

    dh1, dg_mix = _proj_bwd(dh2, h1, g_mix, dzg, dqkv, win, wgate, "proj_bwd")
    keep("w_in", _dw(un, dqkv, 1, IN_COLS, "dw_in"))
    done = finish(flight, g32["w_in"], "ffn2")
    flight = start(("w_out", "w_proj_a", "w_proj_b", "w_gate", "w_in"), done[1][0], "mixer")
    adam(done)

    dh0, dgu1, a1, n1, dg_ffn1 = _ffn_bwd(dh1, h0, g_ffn1 + flight[-1][0, 0], gu1, wgu1, wd1, "ffn1_bwd")
    keep("ffn1_w_down", _dw(a1, dh1, N_DEV // 2, d, "dw_ffn1_down", 0.5), nf // 2)
    done = finish(flight, g32["ffn1_w_down"], "mixer")
    flight = start(("ffn1_w_down",), done[1][0], "ffn1_down")
    adam(done)

    keep("ffn1_w_gu", _dw(n1, dgu1, N_DEV, nf, "dw_ffn1_gu", dep=flight[-1]))
    done = finish(flight, g32["ffn1_w_gu"], "ffn1_down")
    flight = start(("ffn1_w_gu",), done[1][0], "ffn1_gu")
    adam(done)
    smalls = (dg_ffn1, dg_mix, dg_ffn2, dg_ple + flight[-1][0, 0], dgqa, dgka, dgqb, dgkb, dtab, dsink)
    return dh0, loss_part, big, smalls, flight, finish, adam


def kernel(x, p, ffn1_norm, ffn1_w_gu, ffn1_w_down, mix_norm, w_in, a_q_norm, a_k_norm, a_rel_bias, b_q_norm, b_k_norm, b_sinks, w_gate, w_proj_a, w_proj_b, w_out, ffn2_norm, ffn2_w_gu, ffn2_w_down, ple_norm, w_ple_gate, w_ple_proj, loss_target, m_ffn1_norm, m_ffn1_w_gu, m_ffn1_w_down, m_mix_norm, m_w_in, m_a_q_norm, m_a_k_norm, m_a_rel_bias, m_b_q_norm, m_b_k_norm, m_b_sinks, m_w_gate, m_w_proj_a, m_w_proj_b, m_w_out, m_ffn2_norm, m_ffn2_w_gu, m_ffn2_w_down, m_ple_norm, m_w_ple_gate, m_w_ple_proj, v_ffn1_norm, v_ffn1_w_gu, v_ffn1_w_down, v_mix_norm, v_w_in, v_a_q_norm, v_a_k_norm, v_a_rel_bias, v_b_q_norm, v_b_k_norm, v_b_sinks, v_w_gate, v_w_proj_a, v_w_proj_b, v_w_out, v_ffn2_norm, v_ffn2_w_gu, v_ffn2_w_down, v_ple_norm, v_w_ple_gate, v_w_ple_proj):
    w = dict(ffn1_norm=ffn1_norm, ffn1_w_gu=ffn1_w_gu, ffn1_w_down=ffn1_w_down, mix_norm=mix_norm, w_in=w_in,
             a_q_norm=a_q_norm, a_k_norm=a_k_norm, a_rel_bias=a_rel_bias, b_q_norm=b_q_norm, b_k_norm=b_k_norm,
             b_sinks=b_sinks, w_gate=w_gate, w_proj_a=w_proj_a, w_proj_b=w_proj_b, w_out=w_out, ffn2_norm=ffn2_norm,
             ffn2_w_gu=ffn2_w_gu, ffn2_w_down=ffn2_w_down, ple_norm=ple_norm, w_ple_gate=w_ple_gate,
             w_ple_proj=w_ple_proj)
    m = dict(ffn1_norm=m_ffn1_norm, ffn1_w_gu=m_ffn1_w_gu, ffn1_w_down=m_ffn1_w_down, mix_norm=m_mix_norm,
             w_in=m_w_in, a_q_norm=m_a_q_norm, a_k_norm=m_a_k_norm, a_rel_bias=m_a_rel_bias, b_q_norm=m_b_q_norm,
             b_k_norm=m_b_k_norm, b_sinks=m_b_sinks, w_gate=m_w_gate, w_proj_a=m_w_proj_a, w_proj_b=m_w_proj_b,
             w_out=m_w_out, ffn2_norm=m_ffn2_norm, ffn2_w_gu=m_ffn2_w_gu, ffn2_w_down=m_ffn2_w_down,
             ple_norm=m_ple_norm, w_ple_gate=m_w_ple_gate, w_ple_proj=m_w_ple_proj)
    v = dict(ffn1_norm=v_ffn1_norm, ffn1_w_gu=v_ffn1_w_gu, ffn1_w_down=v_ffn1_w_down, mix_norm=v_mix_norm,
             w_in=v_w_in, a_q_norm=v_a_q_norm, a_k_norm=v_a_k_norm, a_rel_bias=v_a_rel_bias, b_q_norm=v_b_q_norm,
             b_k_norm=v_b_k_norm, b_sinks=v_b_sinks, w_gate=v_w_gate, w_proj_a=v_w_proj_a, w_proj_b=v_w_proj_b,
             w_out=v_w_out, ffn2_norm=v_ffn2_norm, ffn2_w_gu=v_ffn2_w_gu, ffn2_w_down=v_ffn2_w_down,
             ple_norm=v_ple_norm, w_ple_gate=v_w_ple_gate, w_ple_proj=v_w_ple_proj)
    bl, s_len, d = x.shape

    dh0, loss_part, big, smalls, flight, finish, adam = _step(x, p[0], loss_target, w, m, v)
    dg_ffn1, dg_mix, dg_ffn2, dg_ple, dgqa, dgka, dgqb, dgkb, dtab, dsink = smalls

    fold = lambda a: (a[:, :, 0, :HEAD_DIM] + a[:, :, 0, HEAD_DIM:]).sum(axis=(0, 1))
    small_part = dict(
        ffn1_norm=dg_ffn1, mix_norm=dg_mix, ffn2_norm=dg_ffn2, ple_norm=dg_ple,
        a_q_norm=fold(dgqa), a_k_norm=fold(dgka), b_q_norm=fold(dgqb), b_k_norm=fold(dgkb),
        a_rel_bias=dtab,
        b_sinks=dsink.sum(axis=0)[:, 0, :2].reshape(B_Q_HEADS),
        loss=loss_part[0, :1])
    zero1 = jnp.zeros((1,), F32)
    shapes = {nme: w[nme].shape for nme in SMALL_NAMES if nme != "loss"}
    shapes["loss"] = ()
    pk = lambda src: _pack_small({**{nme: src[nme] for nme in SMALL_NAMES if nme != "loss"}, "loss": zero1})
    sg, sd, sm, sv = _small_allreduce_adam(_pack_small(small_part), pk(w), pk(m), pk(v), "small_allreduce_adam")
    adam(finish(flight, sg, "ffn1_gu"))
    sg, sd, sm, sv = (_unpack_small(a, shapes) for a in (sg, sd, sm, sv))

    def pick(i):
        out = []
        for nme in WEIGHT_ORDER:
            out.append(big[nme][i] if nme in big else (sg, sd, sm, sv)[i][nme])
        return out

    return (sg["loss"], dh0.reshape(bl, s_len, d), *pick(0), *pick(1), *pick(2), *pick(3))
```

```python
import functools

import jax
import jax.numpy as jnp
import numpy as np
from jax import lax
from jax.experimental import pallas as pl
from jax.experimental.pallas import tpu as pltpu

F32 = jnp.float32
BF16 = jnp.bfloat16

CHUNK = 64
HEAD_DIM = 64
A_HEADS = 8
A_PREV = 8
A_MAX_REL = 128
B_Q_HEADS = 8
B_KV_HEADS = 2
B_PREV = 2
A_WIDTH = A_HEADS * HEAD_DIM
B_Q_WIDTH = B_Q_HEADS * HEAD_DIM
B_KV_WIDTH = B_KV_HEADS * HEAD_DIM
IN_COLS = 3 * A_WIDTH + B_Q_WIDTH + 2 * B_KV_WIDTH
EPS = 1e-6
NEG_INF = -1e30
ADAM_LR = 0.001
ADAM_B1 = 0.9
ADAM_B2 = 0.999
ADAM_EPS = 1e-08
ADAM_WD = 0.01
ADAM_STEP = 10

N_DEV = 8
LANES = 128
QTILE = 2 * CHUNK
VMEM_LIMIT = 56 * 1024 * 1024
ADAM_TILE_ELEMS = 256 * 1024

MESH_ID = pl.DeviceIdType.MESH
ANY = pl.BlockSpec(memory_space=pl.ANY)
HBM = pl.BlockSpec(memory_space=pltpu.HBM)
SEM = pl.BlockSpec(memory_space=pltpu.SEMAPHORE)
SIDE_EFFECT = pltpu.SideEffectType.DATAFLOW_SIDE_EFFECTING


def _dot(a, b):
    return jnp.dot(a, b, preferred_element_type=F32)


def _dot_nt(a, b):
    return lax.dot_general(a, b, (((1,), (1,)), ((), ())), preferred_element_type=F32)


def _dot_tn(a, b):
    return lax.dot_general(a, b, (((0,), (0,)), ((), ())), preferred_element_type=F32)


def _params(sem=None, vmem=VMEM_LIMIT):
    return pltpu.CompilerParams(dimension_semantics=sem, vmem_limit_bytes=vmem)


def _row_tile(t, want):
    while t % want:
        want //= 2
    return want


def _place():
    return lax.axis_index("x"), lax.axis_index("y"), lax.axis_index("c")


def _all_gather(shards, name):
    n = len(shards)

    def body(*refs):
        ins, outs = refs[:n], refs[n:2 * n]
        send_sems, recv_sems, local_sems = refs[2 * n:]
        x, y, c = _place()
        me, sib = (x, y, c), (x, y, 1 - c)
        chips = [(1 - x, y), (x, 1 - y), (1 - x, 1 - y)]

        def copy(w, k, block, to, src=None):
            px, py, pc = block
            dst = outs[w].at[4 * px + 2 * py + pc]
            return pltpu.make_async_remote_copy(
                src_ref=dst if src is None else src, dst_ref=dst,
                send_sem=send_sems.at[w * 7 + k], recv_sem=recv_sems.at[w * 7 + k],
                device_id=to, device_id_type=MESH_ID)

        mine = [pltpu.make_async_copy(ins[w], outs[w].at[4 * x + 2 * y + c], local_sems.at[w]) for w in range(n)]
        for cp in mine:
            cp.start()
        first = []
        for w in range(n):
            first.append(copy(w, 0, me, sib, src=ins[w]))
            first += [copy(w, 1 + j, me, (*chip, c), src=ins[w]) for j, chip in enumerate(chips)]
        for cp in first:
            cp.start()
        passed = []
        for j, chip in enumerate(chips):
            for w in range(n):
                copy(w, 1 + j, (*chip, c), me).wait_recv()
                fwd = copy(w, 4 + j, (*chip, c), sib)
                fwd.start()
                passed.append(fwd)
        for w in range(n):
            copy(w, 0, sib, me).wait_recv()
        for j, chip in enumerate(chips):
            for w in range(n):
                copy(w, 4 + j, (*chip, 1 - c), me).wait_recv()
        for cp in first + passed:
            cp.wait_send()
        for cp in mine:
            cp.wait()

    return pl.pallas_call(
        body, name=name,
        out_shape=[jax.ShapeDtypeStruct((N_DEV,) + s.shape, s.dtype) for s in shards],
        in_specs=[ANY] * n, out_specs=[ANY] * n,
        scratch_shapes=[pltpu.SemaphoreType.DMA((7 * n,)), pltpu.SemaphoreType.DMA((7 * n,)),
                        pltpu.SemaphoreType.DMA((n,))],
    )(*shards)


def _gather_level(bufs, send_sems, recv_sems, level):
    x, y, c = _place()
    me, sib = (x, y, c), (x, y, 1 - c)
    chips = [(1 - x, y), (x, 1 - y), (1 - x, 1 - y)]

    def copy(w, k, block, to):
        px, py, pc = block
        rows = bufs[w].at[4 * px + 2 * py + pc]
        return pltpu.make_async_remote_copy(src_ref=rows, dst_ref=rows, send_sem=send_sems.at[k], recv_sem=recv_sems.at[k],
                                            device_id=to, device_id_type=MESH_ID)

    out, arriving = [], []
    for w in range(len(bufs)):
        if level == 1:
            out.append(copy(w, 4 * w, me, sib))
            arriving.append(copy(w, 4 * w, sib, me))
        for j, chip in enumerate(chips):
            if level == 1:
                out.append(copy(w, 4 * w + 1 + j, me, (*chip, c)))
                arriving.append(copy(w, 4 * w + 1 + j, (*chip, c), me))
            else:
                out.append(copy(w, 3 * w + j, (*chip, c), sib))
                arriving.append(copy(w, 3 * w + j, (*chip, 1 - c), me))
    return out, arriving


def _split_call(body, name, bufs, sems_in, after, n_sems_out, token):
    n = len(bufs)
    out_shape = [pltpu.SemaphoreType.DMA((n_sems_out,))] * (2 if n_sems_out else 0)
    out_shape += [pltpu.HBM(a.shape, a.dtype) for a in bufs]
    out_specs = [SEM] * (2 if n_sems_out else 0) + [HBM] * n
    if token:
        out_shape.append(jax.ShapeDtypeStruct((8, LANES), F32))
        out_specs.append(pl.BlockSpec(memory_space=pltpu.VMEM))
    first = 2 if n_sems_out else 0
    return pl.pallas_call(
        body, name=name, out_shape=tuple(out_shape),
        in_specs=[HBM] * n + [SEM] * len(sems_in) + [ANY], out_specs=tuple(out_specs),
        input_output_aliases={i: first + i for i in range(n)},
        compiler_params=pltpu.CompilerParams(has_side_effects=SIDE_EFFECT),
    )(*bufs, *sems_in, after)


def _gather_start(shards, after, name):
    n = len(shards)
    xi, yi, ci = _place()
    me = 4 * xi + 2 * yi + ci
    bufs = [lax.dynamic_update_slice(lax.empty((N_DEV,) + s.shape, s.dtype), s[None], (me, 0, 0)) for s in shards]
    bufs = [pltpu.with_memory_space_constraint(a, pltpu.HBM) for a in bufs]

    def body(*refs):
        out, _ = _gather_level(refs[:n], refs[n + 1], refs[n + 2], 1)
        for cp in out:
            cp.start()
        refs[-1][...] = jnp.zeros_like(refs[-1])

    outs = _split_call(body, name, bufs, [], after, 4 * n, True)
    return outs[0], outs[1], list(outs[2:2 + n]), outs[-1]


def _gather_pass(send1, recv1, bufs, after, name):
    n = len(bufs)

    def body(*refs):
        out1, in1 = _gather_level(refs[:n], refs[n], refs[n + 1], 1)
        out2, _ = _gather_level(refs[:n], refs[n + 3], refs[n + 4], 2)
        for cp in in1:
            cp.wait_recv()
        for cp in out2:
            cp.start()
        for cp in out1:
            cp.wait_send()
        refs[-1][...] = jnp.zeros_like(refs[-1])

    outs = _split_call(body, name, bufs, [send1, recv1], after, 3 * n, True)
    return outs[0], outs[1], list(outs[2:2 + n]), outs[-1]


def _gather_wait(send2, recv2, bufs, after, name):
    n = len(bufs)

    def body(*refs):
        out2, in2 = _gather_level(refs[:n], refs[n], refs[n + 1], 2)
        for cp in in2:
            cp.wait_recv()
        for cp in out2:
            cp.wait_send()

    return list(_split_call(body, name, bufs, [send2, recv2], after, 0, False))


def _scatter_copies(parts, lands, send_sems, recv_sems):
    x, y, c = _place()
    cps = []
    for w, (part, land) in enumerate(zip(parts, lands)):
        for k in range(1, N_DEV):
            px, py, pc = x ^ ((k >> 2) & 1), y ^ ((k >> 1) & 1), c ^ (k & 1)
            cps.append(pltpu.make_async_remote_copy(
                src_ref=part.at[4 * px + 2 * py + pc], dst_ref=land.at[k - 1],
                send_sem=send_sems.at[7 * w + k - 1], recv_sem=recv_sems.at[7 * w + k - 1],
                device_id=(px, py, pc), device_id_type=MESH_ID))
    return cps


def _scatter_start(parts, after, name):
    n = len(parts)

    def body(*refs):
        ins, lands = refs[:n], refs[n:2 * n]
        send_sems, recv_sems = refs[2 * n + 1], refs[2 * n + 2]
        token = refs[-1]
        for cp in _scatter_copies(ins, lands, send_sems, recv_sems):
            cp.start()
        token[...] = jnp.zeros_like(token)

    land_shapes = [(N_DEV - 1,) + p.shape[1:] for p in parts]
    in_hbm = [pltpu.with_memory_space_constraint(p, pltpu.HBM) for p in parts]
    in_hbm += [pltpu.with_memory_space_constraint(lax.empty(s, p.dtype), pltpu.HBM) for s, p in zip(land_shapes, parts)]
    outs = pl.pallas_call(
        body, name=name,
        out_shape=(pltpu.SemaphoreType.DMA((7 * n,)), pltpu.SemaphoreType.DMA((7 * n,)),
                   *[pltpu.HBM(p.shape, p.dtype) for p in parts],
                   *[pltpu.HBM(s, p.dtype) for s, p in zip(land_shapes, parts)],
                   jax.ShapeDtypeStruct((8, LANES), F32)),
        in_specs=[HBM] * (2 * n) + [ANY],
        out_specs=(SEM, SEM, *[HBM] * (2 * n), pl.BlockSpec(memory_space=pltpu.VMEM)),
        input_output_aliases={i: 2 + i for i in range(2 * n)},
        compiler_params=pltpu.CompilerParams(has_side_effects=SIDE_EFFECT),
    )(*in_hbm, after)
    return outs[0], outs[1], list(outs[2:2 + n]), list(outs[2 + n:2 + 2 * n]), outs[-1]


def _scatter_wait(send_sems, recv_sems, parts, lands, after, name):
    n = len(parts)

    def body(*refs):
        ins, lnd = refs[:n], refs[n:2 * n]
        for cp in _scatter_copies(ins, lnd, refs[2 * n], refs[2 * n + 1]):
            cp.wait_send()
            cp.wait_recv()

    outs = pl.pallas_call(
        body, name=name,
        out_shape=tuple(pltpu.HBM(a.shape, a.dtype) for a in parts + lands),
        in_specs=[HBM] * (2 * n) + [SEM, SEM, ANY],
        out_specs=tuple([HBM] * (2 * n)),
        input_output_aliases={i: i for i in range(2 * n)},
        compiler_params=pltpu.CompilerParams(has_side_effects=SIDE_EFFECT),
    )(*parts, *lands, send_sems, recv_sems, after)
    return list(outs[n:])


def _adam(w, g, m, v):
    m2 = ADAM_B1 * m + (1.0 - ADAM_B1) * g
    v2 = ADAM_B2 * v + (1.0 - ADAM_B2) * (g * g)
    m_hat = m2 / (1.0 - ADAM_B1 ** ADAM_STEP)
    v_hat = v2 / (1.0 - ADAM_B2 ** ADAM_STEP)
    delta = -ADAM_LR * (m_hat / (jnp.sqrt(v_hat) + ADAM_EPS) + ADAM_WD * w)
    return delta, m2, v2


def _small_allreduce_adam(part, w, m, v, name):
    rows = part.shape[0]

    def body(p_ref, w_ref, m_ref, v_ref, g_ref, d_ref, mo_ref, vo_ref, buf, send_sems, recv_sems):
        x, y, c = _place()
        buf[0] = p_ref[...]
        cps = []
        for k in range(1, N_DEV):
            kx, ky, kc = (k >> 2) & 1, (k >> 1) & 1, k & 1
            peer = (x ^ kx, y ^ ky, c ^ kc)
            cps.append(pltpu.make_async_remote_copy(
                src_ref=p_ref, dst_ref=buf.at[k], send_sem=send_sems.at[k - 1], recv_sem=recv_sems.at[k - 1],
                device_id=peer, device_id_type=MESH_ID))
        for cp in cps:
            cp.start()
        for cp in cps:
            cp.wait()
        me = 4 * x + 2 * y + c
        total = buf[me]
        for d in range(1, N_DEV):
            total = total + buf[d ^ me]
        g_ref[...] = total
        delta, m2, v2 = _adam(w_ref[...], total, m_ref[...], v_ref[...])
        d_ref[...] = delta
        mo_ref[...] = m2
        vo_ref[...] = v2

    vm = pl.BlockSpec(memory_space=pltpu.VMEM)
    return pl.pallas_call(
        body, name=name,
        out_shape=[jax.ShapeDtypeStruct(part.shape, F32)] * 4,
        in_specs=[vm] * 4, out_specs=[vm] * 4,
        scratch_shapes=[pltpu.VMEM((N_DEV, rows, LANES), F32),
                        pltpu.SemaphoreType.DMA((N_DEV - 1,)), pltpu.SemaphoreType.DMA((N_DEV - 1,))],
    )(part, w, m, v)


def _final_adam(g8, land, w, m, v, me, name):
    _, r, c = g8.shape
    tr = max(q for q in range(16, r + 1, 16) if r % q == 0 and q * c <= ADAM_TILE_ELEMS)

    def body(me_ref, g_ref, land_ref, w_ref, m_ref, v_ref, go_ref, d_ref, mo_ref, vo_ref):
        g = g_ref[...]
        for k in range(N_DEV - 1):
            g = g + land_ref[k].astype(F32)
        go_ref[...] = g
        delta, m2, v2 = _adam(w_ref[...], g, m_ref[...], v_ref[...])
        d_ref[...] = delta
        mo_ref[...] = m2
        vo_ref[...] = v2

    plain = pl.BlockSpec((tr, c), lambda i, s: (i, 0))
    return pl.pallas_call(
        body, name=name,
        out_shape=[jax.ShapeDtypeStruct((r, c), F32)] * 4,
        grid_spec=pltpu.PrefetchScalarGridSpec(
            num_scalar_prefetch=1, grid=(r // tr,),
            in_specs=[pl.BlockSpec((None, tr, c), lambda i, s: (s[0], i, 0)),
                      pl.BlockSpec((N_DEV - 1, tr, c), lambda i, s: (0, i, 0)),
                      plain, plain, plain],
            out_specs=[plain] * 4),
        compiler_params=_params(("arbitrary",)),
    )(me, g8, land, w, m, v)


def _rms(x, gain):
    r = lax.rsqrt(jnp.mean(x * x, axis=-1, keepdims=True) + EPS)
    xh = x * r
    return xh * gain, xh, r


def _rms_bwd(xh, r, gain, dy):
    gdy = gain * dy
    dx = r * (gdy - xh * jnp.mean(xh * gdy, axis=-1, keepdims=True))
    return dx, jnp.sum(dy * xh, axis=0, keepdims=True)


def _load_weights(pairs, sems):
    cps = [pltpu.make_async_copy(src, dst, sems.at[i]) for i, (src, dst) in enumerate(pairs)]
    for cp in cps:
        cp.start()
    for cp in cps:
        cp.wait()


def _ffn_fwd(h, gain, wgu, wd, name):
    t, d = h.shape
    nb, nf, _ = wgu.shape
    nh = nb // 2
    tm = _row_tile(t, 256)

    def body(h_ref, g_ref, wgu_hbm, wd_hbm, out_ref, gu_ref, wgu_v, wd_v, sems):
        @pl.when(pl.program_id(0) == 0)
        def _():
            _load_weights([(wgu_hbm, wgu_v), (wd_hbm, wd_v)], sems)

        x = h_ref[...]
        n, _, _ = _rms(x, g_ref[...])
        nbf = n.astype(BF16)
        acc = jnp.zeros((tm, d), F32)
        for j in range(nh):
            g = _dot_nt(nbf, wgu_v[j])
            u = _dot_nt(nbf, wgu_v[j + nh])
            gu_ref[j] = g.astype(BF16)
            gu_ref[j + nh] = u.astype(BF16)
            a = (g * jax.nn.sigmoid(g)) * u
            acc = acc + _dot(a.astype(BF16), wd_v[j])
        out_ref[...] = x + 0.5 * acc

    return pl.pallas_call(
        body, name=name, grid=(t // tm,),
        out_shape=[jax.ShapeDtypeStruct((t, d), F32), jax.ShapeDtypeStruct((nb, t, nf), BF16)],
        in_specs=[pl.BlockSpec((tm, d), lambda i: (i, 0)), pl.BlockSpec((1, d), lambda i: (0, 0)), ANY, ANY],
        out_specs=[pl.BlockSpec((tm, d), lambda i: (i, 0)), pl.BlockSpec((nb, tm, nf), lambda i: (0, i, 0))],
        scratch_shapes=[pltpu.VMEM(wgu.shape, BF16), pltpu.VMEM(wd.shape, BF16), pltpu.SemaphoreType.DMA((2,))],
        compiler_params=_params(("arbitrary",)),
    )(h, gain, wgu, wd)


def _ffn_bwd(dh, h, gain, gu, wgu, wd, name):
    t, d = h.shape
    nb, nf, _ = wgu.shape
    nh = nb // 2
    tm = _row_tile(t, 256)

    def body(dh_ref, h_ref, g_ref, gu_ref, wgu_hbm, wd_hbm, dhp_ref, dgu_ref, a_ref, n_ref, dgain_ref,
             wgu_v, wd_v, sems):
        @pl.when(pl.program_id(0) == 0)
        def _():
            _load_weights([(wgu_hbm, wgu_v), (wd_hbm, wd_v)], sems)
            dgain_ref[...] = jnp.zeros_like(dgain_ref)

        x = h_ref[...]
        gain_v = g_ref[...]
        n, xh, r = _rms(x, gain_v)
        n_ref[...] = n.astype(BF16)
        dh_v = dh_ref[...]
        dfb = (0.5 * dh_v).astype(BF16)
        dn = jnp.zeros((tm, d), F32)
        for j in range(nh):
            da = _dot_nt(dfb, wd_v[j])
            g = gu_ref[j].astype(F32)
            u = gu_ref[j + nh].astype(F32)
            sg = jax.nn.sigmoid(g)
            si = g * sg
            dg = (da * u * (sg * (1.0 + g * (1.0 - sg)))).astype(BF16)
            du = (da * si).astype(BF16)
            a_ref[j] = (si * u).astype(BF16)
            dgu_ref[j] = dg
            dgu_ref[j + nh] = du
            dn = dn + _dot(dg, wgu_v[j]) + _dot(du, wgu_v[j + nh])
        dx, dgain = _rms_bwd(xh, r, gain_v, dn)
        dhp_ref[...] = dh_v + dx
        dgain_ref[...] += dgain

    row = pl.BlockSpec((tm, d), lambda i: (i, 0))
    vec = pl.BlockSpec((1, d), lambda i: (0, 0))
    return pl.pallas_call(
        body, name=name, grid=(t // tm,),
        out_shape=[jax.ShapeDtypeStruct((t, d), F32), jax.ShapeDtypeStruct((nb, t, nf), BF16),
                   jax.ShapeDtypeStruct((nh, t, nf), BF16), jax.ShapeDtypeStruct((t, d), BF16),
                   jax.ShapeDtypeStruct((1, d), F32)],
        in_specs=[row, row, vec, pl.BlockSpec((nb, tm, nf), lambda i: (0, i, 0)), ANY, ANY],
        out_specs=[row, pl.BlockSpec((nb, tm, nf), lambda i: (0, i, 0)),
                   pl.BlockSpec((nh, tm, nf), lambda i: (0, i, 0)), row, vec],
        scratch_shapes=[pltpu.VMEM(wgu.shape, BF16), pltpu.VMEM(wd.shape, BF16), pltpu.SemaphoreType.DMA((2,))],
        compiler_params=_params(("arbitrary",)),
    )(dh, h, gain, gu, wgu, wd)


def _dw(xa, dy, nb, n, name, scale=1.0, dep=None):
    t, k = xa.shape[-2:]
    tt = _row_tile(t, 512)
    steps = t // tt
    wide = dy.ndim == 2 and xa.ndim == 2
    if xa.ndim == 3:
        x_spec = pl.BlockSpec((nb, tt, k), lambda i: (0, i, 0))
    else:
        x_spec = pl.BlockSpec((tt, k), lambda i: (i, 0))
    if dy.ndim == 3:
        dy_spec = pl.BlockSpec((nb, tt, n), lambda i: (0, i, 0))
    else:
        dy_spec = pl.BlockSpec((tt, dy.shape[1]), lambda i: (i, 0))
    acc_shape = (k, nb * n) if wide else (nb, k, n)
    stage_shape = (k, nb * n) if wide else (k, n)

    def body(x_ref, dy_ref, *rest):
        o_hbm, ob_hbm, acc, stage, sems = rest[-5:]

        @pl.when(pl.program_id(0) == 0)
        def _():
            acc[...] = jnp.zeros_like(acc)

        if wide:
            acc[...] += _dot(x_ref[...].astype(BF16).T, dy_ref[...].astype(BF16))
        elif xa.ndim == 2:
            xt = x_ref[...].astype(BF16).T
            for j in range(nb):
                acc[j] += _dot(xt, dy_ref[j].astype(BF16))
        else:
            dyb = dy_ref[...].astype(BF16)
            for j in range(nb):
                acc[j] += _dot_tn(x_ref[j].astype(BF16), dyb)

        @pl.when(pl.program_id(0) == steps - 1)
        def _():
            if scale != 1.0:
                acc[...] = acc[...] * scale
            if wide:
                cps = [pltpu.make_async_copy(acc.at[:, pl.ds(j * n, n)] if nb > 1 else acc, o_hbm.at[j], sems.at[j])
                       for j in range(nb)]
            else:
                cps = [pltpu.make_async_copy(acc, o_hbm, sems.at[0])]
            for cp in cps:
                cp.start()
            if wide:
                stage[...] = acc[...].astype(BF16)
                bcs = [pltpu.make_async_copy(stage.at[:, pl.ds(j * n, n)] if nb > 1 else stage, ob_hbm.at[j],
                                             sems.at[nb + j]) for j in range(nb)]
                for cp in bcs:
                    cp.start()
                for cp in bcs:
                    cp.wait()
            else:
                for j in range(nb):
                    stage[...] = acc[j].astype(BF16)
                    cp = pltpu.make_async_copy(stage, ob_hbm.at[j], sems.at[nb])
                    cp.start()
                    cp.wait()
            for cp in cps:
                cp.wait()

    return pl.pallas_call(
        body, name=name, grid=(steps,),
        out_shape=[jax.ShapeDtypeStruct((nb, k, n), F32), jax.ShapeDtypeStruct((nb, k, n), BF16)],
        in_specs=[x_spec, dy_spec] + ([] if dep is None else [ANY]),
        out_specs=[ANY, ANY],
        scratch_shapes=[pltpu.VMEM(acc_shape, F32), pltpu.VMEM(stage_shape, BF16),
                        pltpu.SemaphoreType.DMA((2 * nb,))],
        compiler_params=_params(("arbitrary",)),
    )(*((xa, dy) if dep is None else (xa, dy, dep)))


def _proj_fwd(h, gain, win, wgate, name):
    t, d = h.shape
    tm = _row_tile(t, 256)
    nq, ng = win.shape[0], wgate.shape[1]

    def body(h_ref, g_ref, win_ref, wg_ref, un_ref, qkv_ref, gate_ref):
        n, _, _ = _rms(h_ref[...], g_ref[...])
        nbf = n.astype(BF16)
        un_ref[...] = nbf
        qkv_ref[...] = _dot_nt(nbf, win_ref[...])
        gate_ref[...] = jax.nn.sigmoid(_dot(nbf, wg_ref[...]))

    full = lambda a: pl.BlockSpec(a.shape, lambda i: (0,) * a.ndim)
    return pl.pallas_call(
        body, name=name, grid=(t // tm,),
        out_shape=[jax.ShapeDtypeStruct((t, d), BF16), jax.ShapeDtypeStruct((t, nq), F32),
                   jax.ShapeDtypeStruct((t, ng), F32)],
        in_specs=[pl.BlockSpec((tm, d), lambda i: (i, 0)), full(gain), full(win), full(wgate)],
        out_specs=[pl.BlockSpec((tm, d), lambda i: (i, 0)), pl.BlockSpec((tm, nq), lambda i: (i, 0)),
                   pl.BlockSpec((tm, ng), lambda i: (i, 0))],
        compiler_params=_params(("arbitrary",)),
    )(h, gain, win, wgate)


def _proj_bwd(dh, h, gain, dzg, dqkv, win, wgate, name):
    t, d = h.shape
    tm = _row_tile(t, 256)
    nq, ng = win.shape[0], wgate.shape[1]

    def body(dh_ref, h_ref, g_ref, dzg_ref, dqkv_ref, win_ref, wg_ref, dhp_ref, dgain_ref):
        @pl.when(pl.program_id(0) == 0)
        def _():
            dgain_ref[...] = jnp.zeros_like(dgain_ref)

        gain_v = g_ref[...]
        _, xh, r = _rms(h_ref[...], gain_v)
        dun = _dot_nt(dzg_ref[...], wg_ref[...]) + _dot(dqkv_ref[...].astype(BF16), win_ref[...])
        dx, dgain = _rms_bwd(xh, r, gain_v, dun)
        dhp_ref[...] = dh_ref[...] + dx
        dgain_ref[...] += dgain

    full = lambda a: pl.BlockSpec(a.shape, lambda i: (0,) * a.ndim)
    row = pl.BlockSpec((tm, d), lambda i: (i, 0))
    return pl.pallas_call(
        body, name=name, grid=(t // tm,),
        out_shape=[jax.ShapeDtypeStruct((t, d), F32), jax.ShapeDtypeStruct((1, d), F32)],
        in_specs=[row, row, full(gain), pl.BlockSpec((tm, ng), lambda i: (i, 0)),
                  pl.BlockSpec((tm, nq), lambda i: (i, 0)), full(win), full(wgate)],
        out_specs=[row, pl.BlockSpec((1, d), lambda i: (0, 0))],
        compiler_params=_params(("arbitrary",)),
    )(dh, h, gain, dzg, dqkv, win, wgate)


def _merge_fwd(h, ya, yb, gate, wpa, wpb, wout, name):
    t, d = h.shape
    tm = _row_tile(t, 256)

    def body(h_ref, ya_ref, yb_ref, ga_ref, gb_ref, wpa_ref, wpb_ref, wout_ref, out_ref, mg_ref, pa_ref, pb_ref):
        pa = _dot(ya_ref[...].astype(BF16), wpa_ref[...])
        pb = _dot(yb_ref[...].astype(BF16), wpb_ref[...])
        merged = (ga_ref[...] * pa + gb_ref[...] * pb).astype(BF16)
        pa_ref[...] = pa.astype(BF16)
        pb_ref[...] = pb.astype(BF16)
        mg_ref[...] = merged
        out_ref[...] = h_ref[...] + _dot(merged, wout_ref[...])

    full = lambda a: pl.BlockSpec(a.shape, lambda i: (0,) * a.ndim)
    row = pl.BlockSpec((tm, d), lambda i: (i, 0))
    yrow = pl.BlockSpec((tm, ya.shape[1]), lambda i: (i, 0))
    return pl.pallas_call(
        body, name=name, grid=(t // tm,),
        out_shape=[jax.ShapeDtypeStruct((t, d), F32)] + [jax.ShapeDtypeStruct((t, d), BF16)] * 3,
        in_specs=[row, yrow, yrow, pl.BlockSpec((tm, d), lambda i: (i, 0)), pl.BlockSpec((tm, d), lambda i: (i, 1)),
                  full(wpa), full(wpb), full(wout)],
        out_specs=[row] * 4,
        compiler_params=_params(("arbitrary",)),
    )(h, ya, yb, gate, gate, wpa, wpb, wout)


def _merge_bwd(dh, pa, pb, gate, wpa, wpb, wout, name):
    t, d = dh.shape
    tm = _row_tile(t, 256)
    wy = wpa.shape[0]

    def body(dh_ref, pa_ref, pb_ref, ga_ref, gb_ref, wpa_ref, wpb_ref, wout_ref,
             dpa_ref, dpb_ref, dzg_ref, dya_ref, dyb_ref):
        dm = _dot_nt(dh_ref[...].astype(BF16), wout_ref[...])
        ga, gb = ga_ref[...], gb_ref[...]
        dpa = (dm * ga).astype(BF16)
        dpb = (dm * gb).astype(BF16)
        dpa_ref[...] = dpa
        dpb_ref[...] = dpb
        dzg_ref[:, :d] = (dm * pa_ref[...].astype(F32) * ga * (1.0 - ga)).astype(BF16)
        dzg_ref[:, d:] = (dm * pb_ref[...].astype(F32) * gb * (1.0 - gb)).astype(BF16)
        dya_ref[...] = _dot_nt(dpa, wpa_ref[...])
        dyb_ref[...] = _dot_nt(dpb, wpb_ref[...])

    full = lambda a: pl.BlockSpec(a.shape, lambda i: (0,) * a.ndim)
    row = pl.BlockSpec((tm, d), lambda i: (i, 0))
    yrow = pl.BlockSpec((tm, wy), lambda i: (i, 0))
    return pl.pallas_call(
        body, name=name, grid=(t // tm,),
        out_shape=[jax.ShapeDtypeStruct((t, d), BF16), jax.ShapeDtypeStruct((t, d), BF16),
                   jax.ShapeDtypeStruct((t, 2 * d), BF16), jax.ShapeDtypeStruct((t, wy), F32),
                   jax.ShapeDtypeStruct((t, wy), F32)],
        in_specs=[row, row, row, pl.BlockSpec((tm, d), lambda i: (i, 0)), pl.BlockSpec((tm, d), lambda i: (i, 1)),
                  full(wpa), full(wpb), full(wout)],
        out_specs=[row, row, pl.BlockSpec((tm, 2 * d), lambda i: (i, 0)), yrow, yrow],
        compiler_params=_params(("arbitrary",)),
    )(dh, pa, pb, gate, gate, wpa, wpb, wout)


def _ple_loss(h, gain, p, target, wpg, wpe, name):
    t, d = h.shape
    tm = _row_tile(t, 256)
    pd = p.shape[1]

    def body(h_ref, g_ref, p_ref, t_ref, wpg_ref, wpe_ref, dh_ref, dz_ref, dpp_ref, n_ref, dgain_ref, loss_ref):
        @pl.when(pl.program_id(0) == 0)
        def _():
            dgain_ref[...] = jnp.zeros_like(dgain_ref)
            loss_ref[...] = jnp.zeros_like(loss_ref)

        x = h_ref[...]
        gain_v = g_ref[...]
        n, xh, r = _rms(x, gain_v)
        nbf = n.astype(BF16)
        n_ref[...] = nbf
        pg = jax.nn.sigmoid(_dot(nbf, wpg_ref[...]))
        pp = _dot(p_ref[...].astype(BF16), wpe_ref[...])
        err = (x + pg * pp) - t_ref[...]
        loss_ref[...] += 0.5 * jnp.sum(jnp.mean(err * err, axis=-1, keepdims=True))
        dy = err * (1.0 / d)
        dpp_ref[...] = (dy * pg).astype(BF16)
        dz = (dy * pp * pg * (1.0 - pg)).astype(BF16)
        dz_ref[...] = dz
        dn = _dot_nt(dz, wpg_ref[...])
        dx, dgain = _rms_bwd(xh, r, gain_v, dn)
        dh_ref[...] = dy + dx
        dgain_ref[...] += dgain

    full = lambda a: pl.BlockSpec(a.shape, lambda i: (0,) * a.ndim)
    row = pl.BlockSpec((tm, d), lambda i: (i, 0))
    return pl.pallas_call(
        body, name=name, grid=(t // tm,),
        out_shape=[jax.ShapeDtypeStruct((t, d), F32), jax.ShapeDtypeStruct((t, d), BF16),
                   jax.ShapeDtypeStruct((t, d), BF16), jax.ShapeDtypeStruct((t, d), BF16),
                   jax.ShapeDtypeStruct((1, d), F32), jax.ShapeDtypeStruct((8, LANES), F32)],
        in_specs=[row, full(gain), pl.BlockSpec((tm, pd), lambda i: (i, 0)), row, full(wpg), full(wpe)],
        out_specs=[row, row, row, row, pl.BlockSpec((1, d), lambda i: (0, 0)),
                   pl.BlockSpec((8, LANES), lambda i: (0, 0))],
        compiler_params=_params(("arbitrary",)),
    )(h, gain, p, target, wpg, wpe)


def _head_masks():
    lane = lax.broadcasted_iota(jnp.int32, (1, LANES), 1)
    m0 = (lane < HEAD_DIM).astype(F32)
    return m0, 1.0 - m0


def _head_mean(v, m0, m1):
    s0 = jnp.sum(v * m0, axis=-1, keepdims=True)
    s1 = jnp.sum(v * m1, axis=-1, keepdims=True)
    return (s0 * m0 + s1 * m1) * (1.0 / HEAD_DIM)


def _head_norm(x, gain, m0, m1):
    r = lax.rsqrt(_head_mean(x * x, m0, m1) + EPS)
    xh = x * r
    return xh * gain, xh, r


def _head_norm_bwd(xh, r, gain, dy, m0, m1):
    gdy = gain * dy
    dx = r * (gdy - xh * _head_mean(xh * gdy, m0, m1))
    return dx, jnp.sum(dy * xh, axis=0, keepdims=True)


def _attn_prep(mode, pair, s_len, padk, q_ref, k_ref, v_ref, gq_ref, gk_ref, qs, k0, k1, v0, v1):
    m0, m1 = _head_masks()
    zpad = jnp.zeros((padk, LANES), BF16)
    for buf in (k0, k1, v0, v1):
        buf[pl.ds(0, padk), :] = zpad
    first_kv = (pair // 2) == 0
    rt = _row_tile(s_len, 256)

    def step(i, carry):
        rows = pl.ds(pl.multiple_of(i * rt, rt), rt)
        dst = pl.ds(pl.multiple_of(padk + i * rt, QTILE), rt)
        qn, _, _ = _head_norm(q_ref[rows, :], gq_ref[...], m0, m1)
        kn, _, _ = _head_norm(k_ref[rows, :], gk_ref[...], m0, m1)
        vv = v_ref[rows, :]
        qs[rows, :] = (qn * (HEAD_DIM ** -0.5)).astype(BF16)
        if mode == "B":
            kn = jnp.where(first_kv, kn, pltpu.roll(kn, HEAD_DIM, 1))
            vv = jnp.where(first_kv, vv, pltpu.roll(vv, HEAD_DIM, 1))
            ka, va = kn * m0, vv * m0
            kb, vb = pltpu.roll(ka, HEAD_DIM, 1), pltpu.roll(va, HEAD_DIM, 1)
        else:
            ka, kb, va, vb = kn * m0, kn * m1, vv * m0, vv * m1
        k0[dst, :] = ka.astype(BF16)
        k1[dst, :] = kb.astype(BF16)
        v0[dst, :] = va.astype(BF16)
        v1[dst, :] = vb.astype(BF16)
        return carry

    lax.fori_loop(0, s_len // rt, step, 0)


def _attn_probs(mode, q2, kb, bias, ok, sink):
    s = _dot_nt(q2, kb) + bias
    s = jnp.where(ok, s, NEG_INF)
    mx = jnp.max(s, axis=-1, keepdims=True)
    if mode == "B":
        mx = jnp.maximum(mx, sink)
    e = jnp.exp(s - mx)
    l = jnp.sum(e, axis=-1, keepdims=True)
    if mode == "B":
        l = l + jnp.exp(sink - mx)
    return e, mx, l


def _attn_cols(mode):
    if mode == "A":
        return (lambda b, p: (b, p)), (lambda b, p: (b, 4 + p)), (lambda b, p: (b, 8 + p))
    return (lambda b, p: (b, 12 + p)), (lambda b, p: (b, 16)), (lambda b, p: (b, 17))


def _attn_fwd(mode, qkv, gq, gk, bias, sinks, bl, s_len, name):
    bw = bias.shape[-1]
    padk = bw - QTILE
    nt = s_len // QTILE
    qmap, kmap, vmap = _attn_cols(mode)

    def body(q_ref, k_ref, v_ref, gq_ref, gk_ref, bias_ref, sink_ref, o_ref, qs, k0, k1, v0, v1):
        pair = pl.program_id(1)
        _attn_prep(mode, pair, s_len, padk, q_ref, k_ref, v_ref, gq_ref, gk_ref, qs, k0, k1, v0, v1)
        col = lax.broadcasted_iota(jnp.int32, (QTILE, bw), 1)

        def tile(m, carry):
            r0 = pl.multiple_of(m * QTILE, QTILE)
            q2 = qs[pl.ds(r0, QTILE), :]
            ok = col >= (padk - r0)
            acc = jnp.zeros((QTILE, LANES), F32)
            for hh, (kk, vv) in enumerate(((k0, v0), (k1, v1))):
                sink = sink_ref[2 * pair + hh]
                e, _, l = _attn_probs(mode, q2, kk[pl.ds(r0, bw), :], bias_ref[hh], ok, sink)
                acc = acc + _dot(e.astype(BF16), vv[pl.ds(r0, bw), :]) / l
            o_ref[pl.ds(r0, QTILE), :] = acc
            return carry

        lax.fori_loop(0, nt, tile, 0, unroll=2)

    blk = lambda f: pl.BlockSpec((s_len, LANES), f)
    vec = pl.BlockSpec((1, LANES), lambda b, p: (0, 0))
    return pl.pallas_call(
        body, name=name, grid=(bl, 4),
        out_shape=jax.ShapeDtypeStruct((bl * s_len, 4 * LANES), F32),
        in_specs=[blk(qmap), blk(kmap), blk(vmap), vec, vec,
                  pl.BlockSpec((2, QTILE, bw), lambda b, p: (p, 0, 0)),
                  pl.BlockSpec(memory_space=pltpu.SMEM)],
        out_specs=pl.BlockSpec((s_len, LANES), lambda b, p: (b, p)),
        scratch_shapes=[pltpu.VMEM((s_len, LANES), BF16)] + [pltpu.VMEM((s_len + padk, LANES), BF16)] * 4,
        compiler_params=_params(("arbitrary", "arbitrary")),
    )(qkv, qkv, qkv, gq, gk, bias, sinks)


def _attn_bwd(mode, qkv, gq, gk, bias, sinks, y, dy, bl, s_len, name):
    bw = bias.shape[-1]
    padk = bw - QTILE
    nt = s_len // QTILE
    qmap, kmap, vmap = _attn_cols(mode)
    t = bl * s_len
    kvw = 4 * LANES if mode == "A" else LANES

    def body(q_ref, k_ref, v_ref, gq_ref, gk_ref, bias_ref, sink_ref, y_ref, dy_ref,
             dq_ref, dk_ref, dv_ref, dgq_ref, dgk_ref, dbias_ref, dsink_ref,
             qs, k0, k1, v0, v1, dqs, dk0, dk1, dv0, dv1):
        pair = pl.program_id(1)
        m0, m1 = _head_masks()
        _attn_prep(mode, pair, s_len, padk, q_ref, k_ref, v_ref, gq_ref, gk_ref, qs, k0, k1, v0, v1)
        for buf in (dk0, dk1, dv0, dv1):
            buf[...] = jnp.zeros_like(buf)
        dbias_ref[...] = jnp.zeros_like(dbias_ref)
        col = lax.broadcasted_iota(jnp.int32, (QTILE, bw), 1)
        lane8 = lax.broadcasted_iota(jnp.int32, (8, LANES), 1)

        def tile(m, dsink):
            r0 = pl.multiple_of(m * QTILE, QTILE)
            rows = pl.ds(r0, QTILE)
            band = pl.ds(r0, bw)
            q2 = qs[rows, :]
            do2 = dy_ref[rows, :]
            dd = do2 * y_ref[rows, :]
            dob = do2.astype(BF16)
            ok = col >= (padk - r0)
            dq = jnp.zeros((QTILE, LANES), F32)
            for hh, (kk, vv, dkk, dvv, mh) in enumerate(((k0, v0, dk0, dv0, m0), (k1, v1, dk1, dv1, m1))):
                sink = sink_ref[2 * pair + hh]
                kb = kk[band, :]
                e, mx, l = _attn_probs(mode, q2, kb, bias_ref[hh], ok, sink)
                inv = 1.0 / l
                pn = e * inv
                delta = jnp.sum(dd * mh, axis=-1, keepdims=True)
                dp = _dot_nt(dob, vv[band, :])
                ds = pn * (dp - delta)
                if mode == "A":
                    dbias_ref[hh] += ds
                else:
                    ps = jnp.exp(sink - mx) * inv
                    dsink = dsink + jnp.where(lane8 == hh, -jnp.sum(ps * delta), 0.0)
                dsb = ds.astype(BF16)
                dvv[band, :] += _dot_tn(pn.astype(BF16), dob)
                dkk[band, :] += _dot_tn(dsb, q2)
                dq = dq + _dot(dsb, kb)
            dqs[rows, :] = dq * (HEAD_DIM ** -0.5)
            return dsink

        dsink = lax.fori_loop(0, nt, tile, jnp.zeros((8, LANES), F32), unroll=2)
        dsink_ref[...] = dsink

        first_kv = (pair // 2) == 0
        rt = _row_tile(s_len, 256)

        def post(i, carry):
            dgq, dgk = carry
            rows = pl.ds(pl.multiple_of(i * rt, rt), rt)
            src = pl.ds(pl.multiple_of(padk + i * rt, QTILE), rt)
            gq_v, gk_v = gq_ref[...], gk_ref[...]
            _, qh, qr = _head_norm(q_ref[rows, :], gq_v, m0, m1)
            _, kh, kr = _head_norm(k_ref[rows, :], gk_v, m0, m1)
            dq_raw, dgq_i = _head_norm_bwd(qh, qr, gq_v, dqs[rows, :], m0, m1)
            if mode == "A":
                dkn = dk0[src, :] * m0 + dk1[src, :] * m1
                dvn = dv0[src, :] * m0 + dv1[src, :] * m1
            else:
                dkn = dk0[src, :] * m0 + pltpu.roll(dk1[src, :] * m1, HEAD_DIM, 1)
                dvn = dv0[src, :] * m0 + pltpu.roll(dv1[src, :] * m1, HEAD_DIM, 1)
                dkn = jnp.where(first_kv, dkn, pltpu.roll(dkn, HEAD_DIM, 1))
                dvn = jnp.where(first_kv, dvn, pltpu.roll(dvn, HEAD_DIM, 1))
            dk_raw, dgk_i = _head_norm_bwd(kh, kr, gk_v, dkn, m0, m1)
            dq_ref[rows, :] = dq_raw
            if mode == "A":
                dk_ref[rows, :] = dk_raw
                dv_ref[rows, :] = dvn
            else:
                @pl.when(pair == 0)
                def _():
                    dk_ref[rows, :] = dk_raw
                    dv_ref[rows, :] = dvn

                @pl.when(pair != 0)
                def _():
                    dk_ref[rows, :] += dk_raw
                    dv_ref[rows, :] += dvn
            return dgq + dgq_i, dgk + dgk_i

        z = jnp.zeros((1, LANES), F32)
        dgq, dgk = lax.fori_loop(0, s_len // rt, post, (z, z))
        dgq_ref[...] = jnp.broadcast_to(dgq, (8, LANES))
        dgk_ref[...] = jnp.broadcast_to(dgk, (8, LANES))

    blk = lambda f: pl.BlockSpec((s_len, LANES), f)
    vec = pl.BlockSpec((1, LANES), lambda b, p: (0, 0))
    small = pl.BlockSpec((None, None, 8, LANES), lambda b, p: (b, p, 0, 0))
    kvmap = (lambda b, p: (b, p)) if mode == "A" else (lambda b, p: (b, 0))
    pad_f32 = pltpu.VMEM((s_len + padk, LANES), F32)
    pad_bf = pltpu.VMEM((s_len + padk, LANES), BF16)
    return pl.pallas_call(
        body, name=name, grid=(bl, 4),
        out_shape=[jax.ShapeDtypeStruct((t, 4 * LANES), F32), jax.ShapeDtypeStruct((t, kvw), F32),
                   jax.ShapeDtypeStruct((t, kvw), F32),
                   jax.ShapeDtypeStruct((bl, 4, 8, LANES), F32), jax.ShapeDtypeStruct((bl, 4, 8, LANES), F32),
                   jax.ShapeDtypeStruct((bl, 8, QTILE, bw), F32), jax.ShapeDtypeStruct((bl, 4, 8, LANES), F32)],
        in_specs=[blk(qmap), blk(kmap), blk(vmap), vec, vec,
                  pl.BlockSpec((2, QTILE, bw), lambda b, p: (p, 0, 0)),
                  pl.BlockSpec(memory_space=pltpu.SMEM),
                  blk(lambda b, p: (b, p)), blk(lambda b, p: (b, p))],
        out_specs=[blk(lambda b, p: (b, p)), blk(kvmap), blk(kvmap), small, small,
                   pl.BlockSpec((None, 2, QTILE, bw), lambda b, p: (b, p, 0, 0)), small],
        scratch_shapes=[pltpu.VMEM((s_len, LANES), BF16), pad_bf, pad_bf, pad_bf, pad_bf,
                        pltpu.VMEM((s_len, LANES), F32), pad_f32, pad_f32, pad_f32, pad_f32],
        compiler_params=_params(("arbitrary", "arbitrary")),
    )(qkv, qkv, qkv, gq, gk, bias, sinks, y, dy)


def _band_geometry(prev):
    bw = QTILE + prev * CHUNK
    i = np.arange(QTILE)[:, None]
    j = np.arange(bw)[None, :]
    dist = i + prev * CHUNK - j
    valid = (j // CHUNK >= i // CHUNK) & (j // CHUNK <= i // CHUNK + prev)
    return dist, valid


A_VAR0 = (A_PREV * CHUNK - A_MAX_REL) // LANES * LANES


A_NVAR = QTILE + A_PREV * CHUNK - A_VAR0


def _skew_rows(x, sign):
    rows, n = x.shape
    row = lax.broadcasted_iota(jnp.int32, x.shape, 0)
    b = 1
    while b < rows:
        x = jnp.where((row & b) != 0, pltpu.roll(x, (sign * b) % n, 1), x)
        b *= 2
    return x


def _rel_bias_expand(table, name):
    _, valid = _band_geometry(A_PREV)
    bw = valid.shape[1]
    valid_f = jnp.asarray(valid.astype(np.float32))
    rev = jnp.flip(table[:, 1:], axis=1).reshape(A_HEADS, 1, A_NVAR)

    def body(rev_ref, valid_ref, o_ref):
        rowv = jnp.broadcast_to(rev_ref[...], (QTILE, A_NVAR))
        top = rowv[:, 0:1]
        var = _skew_rows(rowv, 1)
        row = lax.broadcasted_iota(jnp.int32, (QTILE, A_NVAR), 0)
        colv = lax.broadcasted_iota(jnp.int32, (QTILE, A_NVAR), 1)
        var = jnp.where(colv < row, top, var)
        ok = valid_ref[...] > 0.5
        o_ref[:, :A_VAR0] = jnp.where(ok[:, :A_VAR0], top, NEG_INF)
        o_ref[:, A_VAR0:] = jnp.where(ok[:, A_VAR0:], var, NEG_INF)

    return pl.pallas_call(
        body, name=name, grid=(A_HEADS,),
        out_shape=jax.ShapeDtypeStruct((A_HEADS, QTILE, bw), F32),
        in_specs=[pl.BlockSpec((None, 1, A_NVAR), lambda h: (h, 0, 0)), pl.BlockSpec((QTILE, bw), lambda h: (0, 0))],
        out_specs=pl.BlockSpec((None, QTILE, bw), lambda h: (h, 0, 0)),
        compiler_params=_params(("arbitrary",)),
    )(rev, valid_f)


def _rel_bias_grad(dbias, name):
    bl = dbias.shape[0]
    bw = dbias.shape[-1]

    def body(db_ref, o_ref):
        g = db_ref[0]
        for b in range(1, bl):
            g = g + db_ref[b]
        sk = _skew_rows(g[:, A_VAR0:], -1)
        row = lax.broadcasted_iota(jnp.int32, (QTILE, A_NVAR), 0)
        colv = lax.broadcasted_iota(jnp.int32, (QTILE, A_NVAR), 1)
        wrapped = (row + colv) >= A_NVAR
        main = jnp.sum(jnp.where(wrapped, 0.0, sk), axis=0, keepdims=True)
        top = jnp.sum(g[:, :A_VAR0]) + jnp.sum(jnp.where(wrapped, sk, 0.0))
        o_ref[:, :A_NVAR] = jnp.broadcast_to(main, (8, A_NVAR))
        o_ref[:, A_NVAR:] = jnp.full((8, LANES), top, F32)

    out = pl.pallas_call(
        body, name=name, grid=(A_HEADS,),
        out_shape=jax.ShapeDtypeStruct((A_HEADS, 8, A_NVAR + LANES), F32),
        in_specs=[pl.BlockSpec((bl, None, QTILE, bw), lambda h: (0, h, 0, 0))],
        out_specs=pl.BlockSpec((None, 8, A_NVAR + LANES), lambda h: (h, 0, 0)),
        compiler_params=_params(("arbitrary",)),
    )(dbias)
    main, top = out[:, 0, :A_NVAR], out[:, 0, A_NVAR]
    fm = jnp.flip(main, axis=1)
    return jnp.concatenate([jnp.zeros((A_HEADS, 1), F32), fm[:, :-1], fm[:, -1:] + top[:, None]], axis=1)


def _alibi_bias():
    dist, valid = _band_geometry(B_PREV)
    slopes = np.array([2.0 ** (-8.0 * (h + 1) / B_Q_HEADS) for h in range(B_Q_HEADS)], dtype=np.float32)
    bias = -slopes[:, None, None] * np.abs(dist).astype(np.float32)[None]
    return jnp.asarray(np.where(valid[None], bias, np.float32(NEG_INF)).astype(np.float32))


SMALL_NAMES = ("ffn1_norm", "mix_norm", "ffn2_norm", "ple_norm", "a_q_norm", "a_k_norm", "b_q_norm", "b_k_norm",
               "a_rel_bias", "b_sinks", "loss")


def _pack_small(vals):
    rows = []
    for nme in SMALL_NAMES:
        v = vals[nme].astype(F32)
        if nme == "a_rel_bias":
            v = jnp.pad(v.reshape(A_HEADS, -1), ((0, 0), (0, 3 * LANES - (2 * A_MAX_REL + 1))))
        v = v.reshape(-1)
        v = jnp.pad(v, (0, (-v.shape[0]) % LANES))
        rows.append(v.reshape(-1, LANES))
    out = jnp.concatenate(rows, axis=0)
    return jnp.pad(out, ((0, (-out.shape[0]) % 8), (0, 0)))


def _unpack_small(packed, shapes):
    out, r = {}, 0
    for nme in SMALL_NAMES:
        shp = shapes[nme]
        if nme == "a_rel_bias":
            nr = A_HEADS * 3
            out[nme] = packed[r:r + nr].reshape(A_HEADS, 3 * LANES)[:, :2 * A_MAX_REL + 1].reshape(shp)
        else:
            size = int(np.prod(shp)) if shp else 1
            nr = -(-size // LANES)
            out[nme] = packed[r:r + nr].reshape(-1)[:size].reshape(shp)
        r += nr
    return out


BIG_NAMES = ("ffn1_w_gu", "ffn1_w_down", "w_in", "w_gate", "w_proj_a", "w_proj_b", "w_out",
             "ffn2_w_gu", "ffn2_w_down", "w_ple_gate", "w_ple_proj")
ROW_SHARDED = ("ffn1_w_down", "ffn2_w_down", "w_out", "w_ple_gate")
WEIGHT_ORDER = ("ffn1_norm", "ffn1_w_gu", "ffn1_w_down", "mix_norm", "w_in", "a_q_norm", "a_k_norm", "a_rel_bias",
                "b_q_norm", "b_k_norm", "b_sinks", "w_gate", "w_proj_a", "w_proj_b", "w_out", "ffn2_norm",
                "ffn2_w_gu", "ffn2_w_down", "ple_norm", "w_ple_gate", "w_ple_proj")


TRANSPOSED = ("ffn1_w_gu", "ffn2_w_gu", "w_in")


def _local(a, nme):
    return a[0].T if nme in TRANSPOSED else a[0]


def _full_cols(wg):
    nb, k, n = wg.shape
    return jnp.transpose(wg, (1, 0, 2)).reshape(k, nb * n)


def _col_blocks(g, nb):
    k, n = g.shape
    return jnp.transpose(g.reshape(k, nb, n // nb), (1, 0, 2))


def _step(x, p, target, w, m, v):
    bl, s_len, d = x.shape
    t = bl * s_len
    h0 = x.reshape(t, d)
    pt = p.reshape(t, p.shape[-1])
    tgt = target.reshape(t, d)

    g_ffn1, g_mix, g_ffn2, g_ple = w["ffn1_norm"], w["mix_norm"], w["ffn2_norm"], w["ple_norm"]
    tile2 = lambda a: jnp.tile(a.reshape(1, HEAD_DIM), (1, 2))
    gqa, gka, gqb, gkb = tile2(w["a_q_norm"]), tile2(w["a_k_norm"]), tile2(w["b_q_norm"]), tile2(w["b_k_norm"])
    sinks = w["b_sinks"].reshape(B_Q_HEADS)
    bias_a = _rel_bias_expand(w["a_rel_bias"][0], "rel_bias_expand")
    bias_b = _alibi_bias()

    shard = {nme: _local(w[nme], nme).astype(BF16) for nme in BIG_NAMES}
    wgu1, wd1 = _all_gather([shard["ffn1_w_gu"], shard["ffn1_w_down"]], "weights_gather_ffn1")
    nf = wgu1.shape[1]
    wd1 = wd1.reshape(N_DEV // 2, nf, d)
    mixer_names = ("w_in", "w_gate")
    rest_names = ("w_proj_a", "w_proj_b", "w_out", "ffn2_w_gu", "ffn2_w_down", "w_ple_gate", "w_ple_proj")
    send1, recv1, bufs, token = _gather_start([shard[nme] for nme in mixer_names], wgu1, "gather_start_mixer")

    h1, gu1 = _ffn_fwd(h0, g_ffn1 + token[0, 0], wgu1, wd1, "ffn1_fwd")
    send2, recv2, bufs, token = _gather_pass(send1, recv1, bufs, h1, "gather_pass_mixer")
    send1, recv1, rest_bufs, token = _gather_start([shard[nme] for nme in rest_names], token, "gather_start_rest")
    win, wgate = _gather_wait(send2, recv2, bufs, token, "gather_wait_mixer")
    win, wgate = win.reshape(IN_COLS, d), _full_cols(wgate)
    un, qkv, gate = _proj_fwd(h1, g_mix, win, wgate, "proj_fwd")
    ya = _attn_fwd("A", qkv, gqa, gka, bias_a, sinks, bl, s_len, "attn_a_fwd")
    yb = _attn_fwd("B", qkv, gqb, gkb, bias_b, sinks, bl, s_len, "attn_b_fwd")
    send2, recv2, rest_bufs, token = _gather_pass(send1, recv1, rest_bufs, yb, "gather_pass_rest")
    gathered = dict(zip(rest_names, _gather_wait(send2, recv2, rest_bufs, token, "gather_wait_rest")))
    wgu2 = gathered["ffn2_w_gu"]
    wd2 = gathered["ffn2_w_down"].reshape(N_DEV // 2, nf, d)
    wpa = _full_cols(gathered["w_proj_a"])
    wpb = _full_cols(gathered["w_proj_b"])
    wpe = _full_cols(gathered["w_ple_proj"])
    wout = gathered["w_out"].reshape(d, d)
    wpg = gathered["w_ple_gate"].reshape(d, d)
    h2, merged, pa, pb = _merge_fwd(h1, ya, yb, gate, wpa, wpb, wout, "merge_fwd")
    h3, gu2 = _ffn_fwd(h2, g_ffn2, wgu2, wd2, "ffn2_fwd")
    dh3, dz4, dpp, n4, dg_ple, loss_part = _ple_loss(h3, g_ple, pt, tgt, wpg, wpe, "ple_loss")

    xi, yi, ci = _place()
    me = jnp.stack([4 * xi + 2 * yi + ci]).astype(jnp.int32)
    g32, g16, big = {}, {}, {}

    def keep(nme, pair, rows=None):
        for store, g in zip((g32, g16), pair):
            store[nme] = g if rows is None else g.reshape(N_DEV, rows, d)

    def start(names, after, tag):
        send, recv, parts, lands, token = _scatter_start([g16[nme] for nme in names], after, "grads_start_" + tag)
        return names, send, recv, parts, lands, token

    def finish(state, after, tag):
        names, send, recv, parts, lands, _ = state
        lands = _scatter_wait(send, recv, parts, lands, after, "grads_wait_" + tag)
        return names, lands

    def adam(done):
        for nme, land in zip(*done):
            outs = _final_adam(g32[nme], land, _local(w[nme], nme), _local(m[nme], nme), _local(v[nme], nme), me, "adam_" + nme)
            big[nme] = [(o.T if nme in TRANSPOSED else o)[None] for o in outs]

    keep("w_ple_gate", _dw(n4, dz4, 1, d, "dw_ple_gate"), d // N_DEV)
    keep("w_ple_proj", _dw(pt, dpp, N_DEV, d // N_DEV, "dw_ple_proj"))

    dh2, dgu2, a2, n3, dg_ffn2 = _ffn_bwd(dh3, h2, g_ffn2, gu2, wgu2, wd2, "ffn2_bwd")
    keep("ffn2_w_gu", _dw(dgu2, n3, N_DEV, d, "dw_ffn2_gu"))
    keep("ffn2_w_down", _dw(a2, dh3, N_DEV // 2, d, "dw_ffn2_down", 0.5), nf // 2)
    flight = start(("w_ple_gate", "w_ple_proj", "ffn2_w_gu", "ffn2_w_down"), dh2, "ffn2")

    dpa, dpb, dzg, dya, dyb = _merge_bwd(dh2, pa, pb, gate, wpa, wpb, wout, "merge_bwd")
    keep("w_out", _dw(merged, dh2, 1, d, "dw_out"), d // N_DEV)
    keep("w_proj_a", _dw(ya, dpa, N_DEV, d // N_DEV, "dw_proj_a"))
    keep("w_proj_b", _dw(yb, dpb, N_DEV, d // N_DEV, "dw_proj_b"))
    keep("w_gate", _dw(un, dzg, N_DEV, 2 * d // N_DEV, "dw_gate"))

    tok = flight[-1][0, 0]
    dqa, dka, dva, dgqa, dgka, dbias, _ = _attn_bwd("A", qkv, gqa + tok, gka, bias_a, sinks, ya, dya, bl, s_len,
                                                     "attn_a_bwd")
    dqb, dkb, dvb, dgqb, dgkb, _, dsink = _attn_bwd("B", qkv, gqb, gkb, bias_b, sinks, yb, dyb, bl, s_len, "attn_b_bwd")
    dqkv = jnp.concatenate([dqa, dka, dva, dqb, dkb, dvb], axis=1)
    dtab = _rel_bias_grad(dbias, "rel_bias_grad")

    dh1, dg_mix = _proj_bwd(dh2, h1, g_mix, dzg, dqkv, win, wgate, "proj_bwd")
    keep("w_in", _dw(dqkv, un, 1, d, "dw_in"), IN_COLS // N_DEV)
    done = finish(flight, g32["w_in"], "ffn2")
    flight = start(("w_out", "w_proj_a", "w_proj_b", "w_gate", "w_in"), done[1][0], "mixer")
    adam(done)

    dh0, dgu1, a1, n1, dg_ffn1 = _ffn_bwd(dh1, h0, g_ffn1 + flight[-1][0, 0], gu1, wgu1, wd1, "ffn1_bwd")
    keep("ffn1_w_down", _dw(a1, dh1, N_DEV // 2, d, "dw_ffn1_down", 0.5), nf // 2)
    done = finish(flight, g32["ffn1_w_down"], "mixer")
    flight = start(("ffn1_w_down",), done[1][0], "ffn1_down")
    adam(done)

    keep("ffn1_w_gu", _dw(dgu1, n1, N_DEV, d, "dw_ffn1_gu", dep=flight[-1]))
    done = finish(flight, g32["ffn1_w_gu"], "ffn1_down")
    flight = start(("ffn1_w_gu",), done[1][0], "ffn1_gu")
    adam(done)
    smalls = (dg_ffn1, dg_mix, dg_ffn2, dg_ple + flight[-1][0, 0], dgqa, dgka, dgqb, dgkb, dtab, dsink)
    return dh0, loss_part, big, smalls, flight, finish, adam


def kernel(x, p, ffn1_norm, ffn1_w_gu, ffn1_w_down, mix_norm, w_in, a_q_norm, a_k_norm, a_rel_bias, b_q_norm, b_k_norm, b_sinks, w_gate, w_proj_a, w_proj_b, w_out, ffn2_norm, ffn2_w_gu, ffn2_w_down, ple_norm, w_ple_gate, w_ple_proj, loss_target, m_ffn1_norm, m_ffn1_w_gu, m_ffn1_w_down, m_mix_norm, m_w_in, m_a_q_norm, m_a_k_norm, m_a_rel_bias, m_b_q_norm, m_b_k_norm, m_b_sinks, m_w_gate, m_w_proj_a, m_w_proj_b, m_w_out, m_ffn2_norm, m_ffn2_w_gu, m_ffn2_w_down, m_ple_norm, m_w_ple_gate, m_w_ple_proj, v_ffn1_norm, v_ffn1_w_gu, v_ffn1_w_down, v_mix_norm, v_w_in, v_a_q_norm, v_a_k_norm, v_a_rel_bias, v_b_q_norm, v_b_k_norm, v_b_sinks, v_w_gate, v_w_proj_a, v_w_proj_b, v_w_out, v_ffn2_norm, v_ffn2_w_gu, v_ffn2_w_down, v_ple_norm, v_w_ple_gate, v_w_ple_proj):
    w = dict(ffn1_norm=ffn1_norm, ffn1_w_gu=ffn1_w_gu, ffn1_w_down=ffn1_w_down, mix_norm=mix_norm, w_in=w_in,
             a_q_norm=a_q_norm, a_k_norm=a_k_norm, a_rel_bias=a_rel_bias, b_q_norm=b_q_norm, b_k_norm=b_k_norm,
             b_sinks=b_sinks, w_gate=w_gate, w_proj_a=w_proj_a, w_proj_b=w_proj_b, w_out=w_out, ffn2_norm=ffn2_norm,
             ffn2_w_gu=ffn2_w_gu, ffn2_w_down=ffn2_w_down, ple_norm=ple_norm, w_ple_gate=w_ple_gate,
             w_ple_proj=w_ple_proj)
    m = dict(ffn1_norm=m_ffn1_norm, ffn1_w_gu=m_ffn1_w_gu, ffn1_w_down=m_ffn1_w_down, mix_norm=m_mix_norm,
             w_in=m_w_in, a_q_norm=m_a_q_norm, a_k_norm=m_a_k_norm, a_rel_bias=m_a_rel_bias, b_q_norm=m_b_q_norm,
             b_k_norm=m_b_k_norm, b_sinks=m_b_sinks, w_gate=m_w_gate, w_proj_a=m_w_proj_a, w_proj_b=m_w_proj_b,
             w_out=m_w_out, ffn2_norm=m_ffn2_norm, ffn2_w_gu=m_ffn2_w_gu, ffn2_w_down=m_ffn2_w_down,
             ple_norm=m_ple_norm, w_ple_gate=m_w_ple_gate, w_ple_proj=m_w_ple_proj)
    v = dict(ffn1_norm=v_ffn1_norm, ffn1_w_gu=v_ffn1_w_gu, ffn1_w_down=v_ffn1_w_down, mix_norm=v_mix_norm,
             w_in=v_w_in, a_q_norm=v_a_q_norm, a_k_norm=v_a_k_norm, a_rel_bias=v_a_rel_bias, b_q_norm=v_b_q_norm,
             b_k_norm=v_b_k_norm, b_sinks=v_b_sinks, w_gate=v_w_gate, w_proj_a=v_w_proj_a, w_proj_b=v_w_proj_b,
             w_out=v_w_out, ffn2_norm=v_ffn2_norm, ffn2_w_gu=v_ffn2_w_gu, ffn2_w_down=v_ffn2_w_down,
             ple_norm=v_ple_norm, w_ple_gate=v_w_ple_gate, w_ple_proj=v_w_ple_proj)
    bl, s_len, d = x.shape

    dh0, loss_part, big, smalls, flight, finish, adam = _step(x, p[0], loss_target, w, m, v)
    dg_ffn1, dg_mix, dg_ffn2, dg_ple, dgqa, dgka, dgqb, dgkb, dtab, dsink = smalls

    fold = lambda a: (a[:, :, 0, :HEAD_DIM] + a[:, :, 0, HEAD_DIM:]).sum(axis=(0, 1))
    small_part = dict(
        ffn1_norm=dg_ffn1, mix_norm=dg_mix, ffn2_norm=dg_ffn2, ple_norm=dg_ple,
        a_q_norm=fold(dgqa), a_k_norm=fold(dgka), b_q_norm=fold(dgqb), b_k_norm=fold(dgkb),
        a_rel_bias=dtab,
        b_sinks=dsink.sum(axis=0)[:, 0, :2].reshape(B_Q_HEADS),
        loss=loss_part[0, :1])
    zero1 = jnp.zeros((1,), F32)
    shapes = {nme: w[nme].shape for nme in SMALL_NAMES if nme != "loss"}
    shapes["loss"] = ()
    pk = lambda src: _pack_small({**{nme: src[nme] for nme in SMALL_NAMES if nme != "loss"}, "loss": zero1})
    sg, sd, sm, sv = _small_allreduce_adam(_pack_small(small_part), pk(w), pk(m), pk(v), "small_allreduce_adam")
    adam(finish(flight, sg, "ffn1_gu"))
    sg, sd, sm, sv = (_unpack_small(a, shapes) for a in (sg, sd, sm, sv))

    def pick(i):
        out = []
        for nme in WEIGHT_ORDER:
            out.append(big[nme][i] if nme in big else (sg, sd, sm, sv)[i][nme])
        return out

    return (sg["loss"], dh0.reshape(bl, s_len, d), *pick(0), *pick(1), *pick(2), *pick(3))
```

```python
import functools

import jax
import jax.numpy as jnp
import numpy as np
from jax import lax
from jax.experimental import pallas as pl
from jax.experimental.pallas import tpu as pltpu

F32 = jnp.float32
BF16 = jnp.bfloat16

CHUNK = 64
HEAD_DIM = 64
A_HEADS = 8
A_PREV = 8
A_MAX_REL = 128
B_Q_HEADS = 8
B_KV_HEADS = 2
B_PREV = 2
A_WIDTH = A_HEADS * HEAD_DIM
B_Q_WIDTH = B_Q_HEADS * HEAD_DIM
B_KV_WIDTH = B_KV_HEADS * HEAD_DIM
IN_COLS = 3 * A_WIDTH + B_Q_WIDTH + 2 * B_KV_WIDTH
EPS = 1e-6
NEG_INF = -1e30
ADAM_LR = 0.001
ADAM_B1 = 0.9
ADAM_B2 = 0.999
ADAM_EPS = 1e-08
ADAM_WD = 0.01
ADAM_STEP = 10

N_DEV = 8
LANES = 128
QTILE = 2 * CHUNK
VMEM_LIMIT = 56 * 1024 * 1024
ADAM_TILE_ELEMS = 256 * 1024

MESH_ID = pl.DeviceIdType.MESH
ANY = pl.BlockSpec(memory_space=pl.ANY)
HBM = pl.BlockSpec(memory_space=pltpu.HBM)
SEM = pl.BlockSpec(memory_space=pltpu.SEMAPHORE)
SIDE_EFFECT = pltpu.SideEffectType.DATAFLOW_SIDE_EFFECTING


def _dot(a, b):
    return jnp.dot(a, b, preferred_element_type=F32)


def _dot_nt(a, b):
    return lax.dot_general(a, b, (((1,), (1,)), ((), ())), preferred_element_type=F32)


def _dot_tn(a, b):
    return lax.dot_general(a, b, (((0,), (0,)), ((), ())), preferred_element_type=F32)


def _params(sem=None, vmem=VMEM_LIMIT):
    return pltpu.CompilerParams(dimension_semantics=sem, vmem_limit_bytes=vmem)


def _row_tile(t, want):
    while t % want:
        want //= 2
    return want


def _place():
    return lax.axis_index("x"), lax.axis_index("y"), lax.axis_index("c")


def _all_gather(shards, name):
    n = len(shards)

    def body(*refs):
        ins, outs = refs[:n], refs[n:2 * n]
        send_sems, recv_sems, local_sems = refs[2 * n:]
        x, y, c = _place()
        me, sib = (x, y, c), (x, y, 1 - c)
        chips = [(1 - x, y), (x, 1 - y), (1 - x, 1 - y)]

        def copy(w, k, block, to, src=None):
            px, py, pc = block
            dst = outs[w].at[4 * px + 2 * py + pc]
            return pltpu.make_async_remote_copy(
                src_ref=dst if src is None else src, dst_ref=dst,
                send_sem=send_sems.at[w * 7 + k], recv_sem=recv_sems.at[w * 7 + k],
                device_id=to, device_id_type=MESH_ID)

        mine = [pltpu.make_async_copy(ins[w], outs[w].at[4 * x + 2 * y + c], local_sems.at[w]) for w in range(n)]
        for cp in mine:
            cp.start()
        first = []
        for w in range(n):
            first.append(copy(w, 0, me, sib, src=ins[w]))
            first += [copy(w, 1 + j, me, (*chip, c), src=ins[w]) for j, chip in enumerate(chips)]
        for cp in first:
            cp.start()
        passed = []
        for j, chip in enumerate(chips):
            for w in range(n):
                copy(w, 1 + j, (*chip, c), me).wait_recv()
                fwd = copy(w, 4 + j, (*chip, c), sib)
                fwd.start()
                passed.append(fwd)
        for w in range(n):
            copy(w, 0, sib, me).wait_recv()
        for j, chip in enumerate(chips):
            for w in range(n):
                copy(w, 4 + j, (*chip, 1 - c), me).wait_recv()
        for cp in first + passed:
            cp.wait_send()
        for cp in mine:
            cp.wait()

    return pl.pallas_call(
        body, name=name,
        out_shape=[jax.ShapeDtypeStruct((N_DEV,) + s.shape, s.dtype) for s in shards],
        in_specs=[ANY] * n, out_specs=[ANY] * n,
        scratch_shapes=[pltpu.SemaphoreType.DMA((7 * n,)), pltpu.SemaphoreType.DMA((7 * n,)),
                        pltpu.SemaphoreType.DMA((n,))],
    )(*shards)


def _gather_level(bufs, send_sems, recv_sems, level):
    x, y, c = _place()
    me, sib = (x, y, c), (x, y, 1 - c)
    chips = [(1 - x, y), (x, 1 - y), (1 - x, 1 - y)]

    def copy(w, k, block, to):
        px, py, pc = block
        rows = bufs[w].at[4 * px + 2 * py + pc]
        return pltpu.make_async_remote_copy(src_ref=rows, dst_ref=rows, send_sem=send_sems.at[k], recv_sem=recv_sems.at[k],
                                            device_id=to, device_id_type=MESH_ID)

    out, arriving = [], []
    for w in range(len(bufs)):
        if level == 1:
            out.append(copy(w, 4 * w, me, sib))
            arriving.append(copy(w, 4 * w, sib, me))
        for j, chip in enumerate(chips):
            if level == 1:
                out.append(copy(w, 4 * w + 1 + j, me, (*chip, c)))
                arriving.append(copy(w, 4 * w + 1 + j, (*chip, c), me))
            else:
                out.append(copy(w, 3 * w + j, (*chip, c), sib))
                arriving.append(copy(w, 3 * w + j, (*chip, 1 - c), me))
    return out, arriving


def _split_call(body, name, bufs, sems_in, after, n_sems_out, token):
    n = len(bufs)
    out_shape = [pltpu.SemaphoreType.DMA((n_sems_out,))] * (2 if n_sems_out else 0)
    out_shape += [pltpu.HBM(a.shape, a.dtype) for a in bufs]
    out_specs = [SEM] * (2 if n_sems_out else 0) + [HBM] * n
    if token:
        out_shape.append(jax.ShapeDtypeStruct((8, LANES), F32))
        out_specs.append(pl.BlockSpec(memory_space=pltpu.VMEM))
    first = 2 if n_sems_out else 0
    return pl.pallas_call(
        body, name=name, out_shape=tuple(out_shape),
        in_specs=[HBM] * n + [SEM] * len(sems_in) + [ANY], out_specs=tuple(out_specs),
        input_output_aliases={i: first + i for i in range(n)},
        compiler_params=pltpu.CompilerParams(has_side_effects=SIDE_EFFECT),
    )(*bufs, *sems_in, after)


def _gather_start(shards, after, name):
    n = len(shards)
    xi, yi, ci = _place()
    me = 4 * xi + 2 * yi + ci
    bufs = [lax.dynamic_update_slice(lax.empty((N_DEV,) + s.shape, s.dtype), s[None], (me, 0, 0)) for s in shards]
    bufs = [pltpu.with_memory_space_constraint(a, pltpu.HBM) for a in bufs]

    def body(*refs):
        out, _ = _gather_level(refs[:n], refs[n + 1], refs[n + 2], 1)
        for cp in out:
            cp.start()
        refs[-1][...] = jnp.zeros_like(refs[-1])

    outs = _split_call(body, name, bufs, [], after, 4 * n, True)
    return outs[0], outs[1], list(outs[2:2 + n]), outs[-1]


def _gather_pass(send1, recv1, bufs, after, name):
    n = len(bufs)

    def body(*refs):
        out1, in1 = _gather_level(refs[:n], refs[n], refs[n + 1], 1)
        out2, _ = _gather_level(refs[:n], refs[n + 3], refs[n + 4], 2)
        for cp in in1:
            cp.wait_recv()
        for cp in out2:
            cp.start()
        for cp in out1:
            cp.wait_send()
        refs[-1][...] = jnp.zeros_like(refs[-1])

    outs = _split_call(body, name, bufs, [send1, recv1], after, 3 * n, True)
    return outs[0], outs[1], list(outs[2:2 + n]), outs[-1]


def _gather_wait(send2, recv2, bufs, after, name):
    n = len(bufs)

    def body(*refs):
        out2, in2 = _gather_level(refs[:n], refs[n], refs[n + 1], 2)
        for cp in in2:
            cp.wait_recv()
        for cp in out2:
            cp.wait_send()

    return list(_split_call(body, name, bufs, [send2, recv2], after, 0, False))


def _scatter_copies(parts, lands, send_sems, recv_sems):
    x, y, c = _place()
    cps = []
    for w, (part, land) in enumerate(zip(parts, lands)):
        for k in range(1, N_DEV):
            px, py, pc = x ^ ((k >> 2) & 1), y ^ ((k >> 1) & 1), c ^ (k & 1)
            cps.append(pltpu.make_async_remote_copy(
                src_ref=part.at[4 * px + 2 * py + pc], dst_ref=land.at[k - 1],
                send_sem=send_sems.at[7 * w + k - 1], recv_sem=recv_sems.at[7 * w + k - 1],
                device_id=(px, py, pc), device_id_type=MESH_ID))
    return cps


def _scatter_start(parts, after, name):
    n = len(parts)

    def body(*refs):
        ins, lands = refs[:n], refs[n:2 * n]
        send_sems, recv_sems = refs[2 * n + 1], refs[2 * n + 2]
        token = refs[-1]
        for cp in _scatter_copies(ins, lands, send_sems, recv_sems):
            cp.start()
        token[...] = jnp.zeros_like(token)

    land_shapes = [(N_DEV - 1,) + p.shape[1:] for p in parts]
    in_hbm = [pltpu.with_memory_space_constraint(p, pltpu.HBM) for p in parts]
    in_hbm += [pltpu.with_memory_space_constraint(lax.empty(s, p.dtype), pltpu.HBM) for s, p in zip(land_shapes, parts)]
    outs = pl.pallas_call(
        body, name=name,
        out_shape=(pltpu.SemaphoreType.DMA((7 * n,)), pltpu.SemaphoreType.DMA((7 * n,)),
                   *[pltpu.HBM(p.shape, p.dtype) for p in parts],
                   *[pltpu.HBM(s, p.dtype) for s, p in zip(land_shapes, parts)],
                   jax.ShapeDtypeStruct((8, LANES), F32)),
        in_specs=[HBM] * (2 * n) + [ANY],
        out_specs=(SEM, SEM, *[HBM] * (2 * n), pl.BlockSpec(memory_space=pltpu.VMEM)),
        input_output_aliases={i: 2 + i for i in range(2 * n)},
        compiler_params=pltpu.CompilerParams(has_side_effects=SIDE_EFFECT),
    )(*in_hbm, after)
    return outs[0], outs[1], list(outs[2:2 + n]), list(outs[2 + n:2 + 2 * n]), outs[-1]


def _scatter_wait(send_sems, recv_sems, parts, lands, after, name):
    n = len(parts)

    def body(*refs):
        ins, lnd = refs[:n], refs[n:2 * n]
        for cp in _scatter_copies(ins, lnd, refs[2 * n], refs[2 * n + 1]):
            cp.wait_send()
            cp.wait_recv()

    outs = pl.pallas_call(
        body, name=name,
        out_shape=tuple(pltpu.HBM(a.shape, a.dtype) for a in parts + lands),
        in_specs=[HBM] * (2 * n) + [SEM, SEM, ANY],
        out_specs=tuple([HBM] * (2 * n)),
        input_output_aliases={i: i for i in range(2 * n)},
        compiler_params=pltpu.CompilerParams(has_side_effects=SIDE_EFFECT),
    )(*parts, *lands, send_sems, recv_sems, after)
    return list(outs[n:])


def _adam(w, g, m, v):
    m2 = ADAM_B1 * m + (1.0 - ADAM_B1) * g
    v2 = ADAM_B2 * v + (1.0 - ADAM_B2) * (g * g)
    m_hat = m2 / (1.0 - ADAM_B1 ** ADAM_STEP)
    v_hat = v2 / (1.0 - ADAM_B2 ** ADAM_STEP)
    delta = -ADAM_LR * (m_hat / (jnp.sqrt(v_hat) + ADAM_EPS) + ADAM_WD * w)
    return delta, m2, v2


def _small_allreduce_adam(part, w, m, v, name):
    rows = part.shape[0]

    def body(p_ref, w_ref, m_ref, v_ref, g_ref, d_ref, mo_ref, vo_ref, buf, send_sems, recv_sems):
        x, y, c = _place()
        buf[0] = p_ref[...]
        cps = []
        for k in range(1, N_DEV):
            kx, ky, kc = (k >> 2) & 1, (k >> 1) & 1, k & 1
            peer = (x ^ kx, y ^ ky, c ^ kc)
            cps.append(pltpu.make_async_remote_copy(
                src_ref=p_ref, dst_ref=buf.at[k], send_sem=send_sems.at[k - 1], recv_sem=recv_sems.at[k - 1],
                device_id=peer, device_id_type=MESH_ID))
        for cp in cps:
            cp.start()
        for cp in cps:
            cp.wait()
        me = 4 * x + 2 * y + c
        total = buf[me]
        for d in range(1, N_DEV):
            total = total + buf[d ^ me]
        g_ref[...] = total
        delta, m2, v2 = _adam(w_ref[...], total, m_ref[...], v_ref[...])
        d_ref[...] = delta
        mo_ref[...] = m2
        vo_ref[...] = v2

    vm = pl.BlockSpec(memory_space=pltpu.VMEM)
    return pl.pallas_call(
        body, name=name,
        out_shape=[jax.ShapeDtypeStruct(part.shape, F32)] * 4,
        in_specs=[vm] * 4, out_specs=[vm] * 4,
        scratch_shapes=[pltpu.VMEM((N_DEV, rows, LANES), F32),
                        pltpu.SemaphoreType.DMA((N_DEV - 1,)), pltpu.SemaphoreType.DMA((N_DEV - 1,))],
    )(part, w, m, v)


def _final_adam(g8, land, w, m, v, me, name):
    _, r, c = g8.shape
    tr = max(q for q in range(16, r + 1, 16) if r % q == 0 and q * c <= ADAM_TILE_ELEMS)

    def body(me_ref, g_ref, land_ref, w_ref, m_ref, v_ref, go_ref, d_ref, mo_ref, vo_ref):
        g = g_ref[...]
        for k in range(N_DEV - 1):
            g = g + land_ref[k].astype(F32)
        go_ref[...] = g
        delta, m2, v2 = _adam(w_ref[...], g, m_ref[...], v_ref[...])
        d_ref[...] = delta
        mo_ref[...] = m2
        vo_ref[...] = v2

    plain = pl.BlockSpec((tr, c), lambda i, s: (i, 0))
    return pl.pallas_call(
        body, name=name,
        out_shape=[jax.ShapeDtypeStruct((r, c), F32)] * 4,
        grid_spec=pltpu.PrefetchScalarGridSpec(
            num_scalar_prefetch=1, grid=(r // tr,),
            in_specs=[pl.BlockSpec((None, tr, c), lambda i, s: (s[0], i, 0)),
                      pl.BlockSpec((N_DEV - 1, tr, c), lambda i, s: (0, i, 0)),
                      plain, plain, plain],
            out_specs=[plain] * 4),
        compiler_params=_params(("arbitrary",)),
    )(me, g8, land, w, m, v)


def _rms(x, gain):
    r = lax.rsqrt(jnp.mean(x * x, axis=-1, keepdims=True) + EPS)
    xh = x * r
    return xh * gain, xh, r


def _rms_bwd(xh, r, gain, dy):
    gdy = gain * dy
    dx = r * (gdy - xh * jnp.mean(xh * gdy, axis=-1, keepdims=True))
    return dx, jnp.sum(dy * xh, axis=0, keepdims=True)


def _load_weights(pairs, sems):
    cps = [pltpu.make_async_copy(src, dst, sems.at[i]) for i, (src, dst) in enumerate(pairs)]
    for cp in cps:
        cp.start()
    for cp in cps:
        cp.wait()


def _ffn_fwd(h, gain, wgu, wd, name):
    t, d = h.shape
    nb, nf, _ = wgu.shape
    nh = nb // 2
    tm = _row_tile(t, 512)

    def body(h_ref, g_ref, wgu_hbm, wd_hbm, out_ref, gu_ref, wgu_v, wd_v, sems):
        @pl.when(pl.program_id(0) == 0)
        def _():
            _load_weights([(wgu_hbm, wgu_v), (wd_hbm, wd_v)], sems)

        x = h_ref[...]
        n, _, _ = _rms(x, g_ref[...])
        nbf = n.astype(BF16)
        acc = jnp.zeros((tm, d), F32)
        for j in range(nh):
            g = _dot_nt(nbf, wgu_v[j])
            u = _dot_nt(nbf, wgu_v[j + nh])
            gu_ref[j] = g.astype(BF16)
            gu_ref[j + nh] = u.astype(BF16)
            a = (g * jax.nn.sigmoid(g)) * u
            acc = acc + _dot(a.astype(BF16), wd_v[j])
        out_ref[...] = x + 0.5 * acc

    return pl.pallas_call(
        body, name=name, grid=(t // tm,),
        out_shape=[jax.ShapeDtypeStruct((t, d), F32), jax.ShapeDtypeStruct((nb, t, nf), BF16)],
        in_specs=[pl.BlockSpec((tm, d), lambda i: (i, 0)), pl.BlockSpec((1, d), lambda i: (0, 0)), ANY, ANY],
        out_specs=[pl.BlockSpec((tm, d), lambda i: (i, 0)), pl.BlockSpec((nb, tm, nf), lambda i: (0, i, 0))],
        scratch_shapes=[pltpu.VMEM(wgu.shape, BF16), pltpu.VMEM(wd.shape, BF16), pltpu.SemaphoreType.DMA((2,))],
        compiler_params=_params(("arbitrary",)),
    )(h, gain, wgu, wd)


def _ffn_bwd(dh, h, gain, gu, wgu, wd, name):
    t, d = h.shape
    nb, nf, _ = wgu.shape
    nh = nb // 2
    tm = _row_tile(t, 256)

    def body(dh_ref, h_ref, g_ref, gu_ref, wgu_hbm, wd_hbm, dhp_ref, dgu_ref, a_ref, n_ref, dgain_ref,
             wgu_v, wd_v, sems):
        @pl.when(pl.program_id(0) == 0)
        def _():
            _load_weights([(wgu_hbm, wgu_v), (wd_hbm, wd_v)], sems)
            dgain_ref[...] = jnp.zeros_like(dgain_ref)

        x = h_ref[...]
        gain_v = g_ref[...]
        n, xh, r = _rms(x, gain_v)
        n_ref[...] = n.astype(BF16)
        dh_v = dh_ref[...]
        dfb = (0.5 * dh_v).astype(BF16)
        dn = jnp.zeros((tm, d), F32)
        for j in range(nh):
            da = _dot_nt(dfb, wd_v[j])
            g = gu_ref[j].astype(F32)
            u = gu_ref[j + nh].astype(F32)
            sg = jax.nn.sigmoid(g)
            si = g * sg
            dg = (da * u * (sg * (1.0 + g * (1.0 - sg)))).astype(BF16)
            du = (da * si).astype(BF16)
            a_ref[j] = (si * u).astype(BF16)
            dgu_ref[j] = dg
            dgu_ref[j + nh] = du
            dn = dn + _dot(dg, wgu_v[j]) + _dot(du, wgu_v[j + nh])
        dx, dgain = _rms_bwd(xh, r, gain_v, dn)
        dhp_ref[...] = dh_v + dx
        dgain_ref[...] += dgain

    row = pl.BlockSpec((tm, d), lambda i: (i, 0))
    vec = pl.BlockSpec((1, d), lambda i: (0, 0))
    return pl.pallas_call(
        body, name=name, grid=(t // tm,),
        out_shape=[jax.ShapeDtypeStruct((t, d), F32), jax.ShapeDtypeStruct((nb, t, nf), BF16),
                   jax.ShapeDtypeStruct((nh, t, nf), BF16), jax.ShapeDtypeStruct((t, d), BF16),
                   jax.ShapeDtypeStruct((1, d), F32)],
        in_specs=[row, row, vec, pl.BlockSpec((nb, tm, nf), lambda i: (0, i, 0)), ANY, ANY],
        out_specs=[row, pl.BlockSpec((nb, tm, nf), lambda i: (0, i, 0)),
                   pl.BlockSpec((nh, tm, nf), lambda i: (0, i, 0)), row, vec],
        scratch_shapes=[pltpu.VMEM(wgu.shape, BF16), pltpu.VMEM(wd.shape, BF16), pltpu.SemaphoreType.DMA((2,))],
        compiler_params=_params(("arbitrary",)),
    )(dh, h, gain, gu, wgu, wd)


def _dw(xa, dy, nb, n, name, scale=1.0, dep=None):
    t, k = xa.shape[-2:]
    tt = _row_tile(t, 512)
    steps = t // tt
    wide = dy.ndim == 2 and xa.ndim == 2
    if xa.ndim == 3:
        x_spec = pl.BlockSpec((nb, tt, k), lambda i: (0, i, 0))
    else:
        x_spec = pl.BlockSpec((tt, k), lambda i: (i, 0))
    if dy.ndim == 3:
        dy_spec = pl.BlockSpec((nb, tt, n), lambda i: (0, i, 0))
    else:
        dy_spec = pl.BlockSpec((tt, dy.shape[1]), lambda i: (i, 0))
    acc_shape = (k, nb * n) if wide else (nb, k, n)
    stage_shape = (k, nb * n) if wide else (k, n)

    def body(x_ref, dy_ref, *rest):
        o_hbm, ob_hbm, acc, stage, sems = rest[-5:]

        @pl.when(pl.program_id(0) == 0)
        def _():
            acc[...] = jnp.zeros_like(acc)

        if wide:
            acc[...] += _dot(x_ref[...].astype(BF16).T, dy_ref[...].astype(BF16))
        elif xa.ndim == 2:
            xt = x_ref[...].astype(BF16).T
            for j in range(nb):
                acc[j] += _dot(xt, dy_ref[j].astype(BF16))
        else:
            dyb = dy_ref[...].astype(BF16)
            for j in range(nb):
                acc[j] += _dot_tn(x_ref[j].astype(BF16), dyb)

        @pl.when(pl.program_id(0) == steps - 1)
        def _():
            if scale != 1.0:
                acc[...] = acc[...] * scale
            if wide:
                cps = [pltpu.make_async_copy(acc.at[:, pl.ds(j * n, n)] if nb > 1 else acc, o_hbm.at[j], sems.at[j])
                       for j in range(nb)]
            else:
                cps = [pltpu.make_async_copy(acc, o_hbm, sems.at[0])]
            for cp in cps:
                cp.start()
            if wide:
                stage[...] = acc[...].astype(BF16)
                bcs = [pltpu.make_async_copy(stage.at[:, pl.ds(j * n, n)] if nb > 1 else stage, ob_hbm.at[j],
                                             sems.at[nb + j]) for j in range(nb)]
                for cp in bcs:
                    cp.start()
                for cp in bcs:
                    cp.wait()
            else:
                for j in range(nb):
                    stage[...] = acc[j].astype(BF16)
                    cp = pltpu.make_async_copy(stage, ob_hbm.at[j], sems.at[nb])
                    cp.start()
                    cp.wait()
            for cp in cps:
                cp.wait()

    return pl.pallas_call(
        body, name=name, grid=(steps,),
        out_shape=[jax.ShapeDtypeStruct((nb, k, n), F32), jax.ShapeDtypeStruct((nb, k, n), BF16)],
        in_specs=[x_spec, dy_spec] + ([] if dep is None else [ANY]),
        out_specs=[ANY, ANY],
        scratch_shapes=[pltpu.VMEM(acc_shape, F32), pltpu.VMEM(stage_shape, BF16),
                        pltpu.SemaphoreType.DMA((2 * nb,))],
        compiler_params=_params(("arbitrary",)),
    )(*((xa, dy) if dep is None else (xa, dy, dep)))


def _proj_fwd(h, gain, win, wgate, name):
    t, d = h.shape
    tm = _row_tile(t, 256)
    nq, ng = win.shape[0], wgate.shape[1]

    def body(h_ref, g_ref, win_ref, wg_ref, un_ref, qkv_ref, gate_ref):
        n, _, _ = _rms(h_ref[...], g_ref[...])
        nbf = n.astype(BF16)
        un_ref[...] = nbf
        qkv_ref[...] = _dot_nt(nbf, win_ref[...])
        gate_ref[...] = jax.nn.sigmoid(_dot(nbf, wg_ref[...]))

    full = lambda a: pl.BlockSpec(a.shape, lambda i: (0,) * a.ndim)
    return pl.pallas_call(
        body, name=name, grid=(t // tm,),
        out_shape=[jax.ShapeDtypeStruct((t, d), BF16), jax.ShapeDtypeStruct((t, nq), F32),
                   jax.ShapeDtypeStruct((t, ng), F32)],
        in_specs=[pl.BlockSpec((tm, d), lambda i: (i, 0)), full(gain), full(win), full(wgate)],
        out_specs=[pl.BlockSpec((tm, d), lambda i: (i, 0)), pl.BlockSpec((tm, nq), lambda i: (i, 0)),
                   pl.BlockSpec((tm, ng), lambda i: (i, 0))],
        compiler_params=_params(("arbitrary",)),
    )(h, gain, win, wgate)


def _proj_bwd(dh, h, gain, dzg, dqkv, win, wgate, name):
    t, d = h.shape
    tm = _row_tile(t, 256)
    nq, ng = win.shape[0], wgate.shape[1]

    def body(dh_ref, h_ref, g_ref, dzg_ref, dqkv_ref, win_ref, wg_ref, dhp_ref, dgain_ref):
        @pl.when(pl.program_id(0) == 0)
        def _():
            dgain_ref[...] = jnp.zeros_like(dgain_ref)

        gain_v = g_ref[...]
        _, xh, r = _rms(h_ref[...], gain_v)
        dun = _dot_nt(dzg_ref[...], wg_ref[...]) + _dot(dqkv_ref[...].astype(BF16), win_ref[...])
        dx, dgain = _rms_bwd(xh, r, gain_v, dun)
        dhp_ref[...] = dh_ref[...] + dx
        dgain_ref[...] += dgain

    full = lambda a: pl.BlockSpec(a.shape, lambda i: (0,) * a.ndim)
    row = pl.BlockSpec((tm, d), lambda i: (i, 0))
    return pl.pallas_call(
        body, name=name, grid=(t // tm,),
        out_shape=[jax.ShapeDtypeStruct((t, d), F32), jax.ShapeDtypeStruct((1, d), F32)],
        in_specs=[row, row, full(gain), pl.BlockSpec((tm, ng), lambda i: (i, 0)),
                  pl.BlockSpec((tm, nq), lambda i: (i, 0)), full(win), full(wgate)],
        out_specs=[row, pl.BlockSpec((1, d), lambda i: (0, 0))],
        compiler_params=_params(("arbitrary",)),
    )(dh, h, gain, dzg, dqkv, win, wgate)


def _merge_fwd(h, ya, yb, gate, wpa, wpb, wout, name):
    t, d = h.shape
    tm = _row_tile(t, 256)

    def body(h_ref, ya_ref, yb_ref, ga_ref, gb_ref, wpa_ref, wpb_ref, wout_ref, out_ref, mg_ref, pa_ref, pb_ref):
        pa = _dot(ya_ref[...].astype(BF16), wpa_ref[...])
        pb = _dot(yb_ref[...].astype(BF16), wpb_ref[...])
        merged = (ga_ref[...] * pa + gb_ref[...] * pb).astype(BF16)
        pa_ref[...] = pa.astype(BF16)
        pb_ref[...] = pb.astype(BF16)
        mg_ref[...] = merged
        out_ref[...] = h_ref[...] + _dot(merged, wout_ref[...])

    full = lambda a: pl.BlockSpec(a.shape, lambda i: (0,) * a.ndim)
    row = pl.BlockSpec((tm, d), lambda i: (i, 0))
    yrow = pl.BlockSpec((tm, ya.shape[1]), lambda i: (i, 0))
    return pl.pallas_call(
        body, name=name, grid=(t // tm,),
        out_shape=[jax.ShapeDtypeStruct((t, d), F32)] + [jax.ShapeDtypeStruct((t, d), BF16)] * 3,
        in_specs=[row, yrow, yrow, pl.BlockSpec((tm, d), lambda i: (i, 0)), pl.BlockSpec((tm, d), lambda i: (i, 1)),
                  full(wpa), full(wpb), full(wout)],
        out_specs=[row] * 4,
        compiler_params=_params(("arbitrary",)),
    )(h, ya, yb, gate, gate, wpa, wpb, wout)


def _merge_bwd(dh, pa, pb, gate, wpa, wpb, wout, name):
    t, d = dh.shape
    tm = _row_tile(t, 256)
    wy = wpa.shape[0]

    def body(dh_ref, pa_ref, pb_ref, ga_ref, gb_ref, wpa_ref, wpb_ref, wout_ref,
             dpa_ref, dpb_ref, dzg_ref, dya_ref, dyb_ref):
        dm = _dot_nt(dh_ref[...].astype(BF16), wout_ref[...])
        ga, gb = ga_ref[...], gb_ref[...]
        dpa = (dm * ga).astype(BF16)
        dpb = (dm * gb).astype(BF16)
        dpa_ref[...] = dpa
        dpb_ref[...] = dpb
        dzg_ref[:, :d] = (dm * pa_ref[...].astype(F32) * ga * (1.0 - ga)).astype(BF16)
        dzg_ref[:, d:] = (dm * pb_ref[...].astype(F32) * gb * (1.0 - gb)).astype(BF16)
        dya_ref[...] = _dot_nt(dpa, wpa_ref[...])
        dyb_ref[...] = _dot_nt(dpb, wpb_ref[...])

    full = lambda a: pl.BlockSpec(a.shape, lambda i: (0,) * a.ndim)
    row = pl.BlockSpec((tm, d), lambda i: (i, 0))
    yrow = pl.BlockSpec((tm, wy), lambda i: (i, 0))
    return pl.pallas_call(
        body, name=name, grid=(t // tm,),
        out_shape=[jax.ShapeDtypeStruct((t, d), BF16), jax.ShapeDtypeStruct((t, d), BF16),
                   jax.ShapeDtypeStruct((t, 2 * d), BF16), jax.ShapeDtypeStruct((t, wy), F32),
                   jax.ShapeDtypeStruct((t, wy), F32)],
        in_specs=[row, row, row, pl.BlockSpec((tm, d), lambda i: (i, 0)), pl.BlockSpec((tm, d), lambda i: (i, 1)),
                  full(wpa), full(wpb), full(wout)],
        out_specs=[row, row, pl.BlockSpec((tm, 2 * d), lambda i: (i, 0)), yrow, yrow],
        compiler_params=_params(("arbitrary",)),
    )(dh, pa, pb, gate, gate, wpa, wpb, wout)


def _ple_loss(h, gain, p, target, wpg, wpe, name):
    t, d = h.shape
    tm = _row_tile(t, 256)
    pd = p.shape[1]

    def body(h_ref, g_ref, p_ref, t_ref, wpg_ref, wpe_ref, dh_ref, dz_ref, dpp_ref, n_ref, dgain_ref, loss_ref):
        @pl.when(pl.program_id(0) == 0)
        def _():
            dgain_ref[...] = jnp.zeros_like(dgain_ref)
            loss_ref[...] = jnp.zeros_like(loss_ref)

        x = h_ref[...]
        gain_v = g_ref[...]
        n, xh, r = _rms(x, gain_v)
        nbf = n.astype(BF16)
        n_ref[...] = nbf
        pg = jax.nn.sigmoid(_dot(nbf, wpg_ref[...]))
        pp = _dot(p_ref[...].astype(BF16), wpe_ref[...])
        err = (x + pg * pp) - t_ref[...]
        loss_ref[...] += 0.5 * jnp.sum(jnp.mean(err * err, axis=-1, keepdims=True))
        dy = err * (1.0 / d)
        dpp_ref[...] = (dy * pg).astype(BF16)
        dz = (dy * pp * pg * (1.0 - pg)).astype(BF16)
        dz_ref[...] = dz
        dn = _dot_nt(dz, wpg_ref[...])
        dx, dgain = _rms_bwd(xh, r, gain_v, dn)
        dh_ref[...] = dy + dx
        dgain_ref[...] += dgain

    full = lambda a: pl.BlockSpec(a.shape, lambda i: (0,) * a.ndim)
    row = pl.BlockSpec((tm, d), lambda i: (i, 0))
    return pl.pallas_call(
        body, name=name, grid=(t // tm,),
        out_shape=[jax.ShapeDtypeStruct((t, d), F32), jax.ShapeDtypeStruct((t, d), BF16),
                   jax.ShapeDtypeStruct((t, d), BF16), jax.ShapeDtypeStruct((t, d), BF16),
                   jax.ShapeDtypeStruct((1, d), F32), jax.ShapeDtypeStruct((8, LANES), F32)],
        in_specs=[row, full(gain), pl.BlockSpec((tm, pd), lambda i: (i, 0)), row, full(wpg), full(wpe)],
        out_specs=[row, row, row, row, pl.BlockSpec((1, d), lambda i: (0, 0)),
                   pl.BlockSpec((8, LANES), lambda i: (0, 0))],
        compiler_params=_params(("arbitrary",)),
    )(h, gain, p, target, wpg, wpe)


def _head_masks():
    lane = lax.broadcasted_iota(jnp.int32, (1, LANES), 1)
    m0 = (lane < HEAD_DIM).astype(F32)
    return m0, 1.0 - m0


def _head_mean(v, m0, m1):
    del m0, m1
    r = lax.broadcasted_iota(jnp.int32, (LANES, LANES), 0) < HEAD_DIM
    c = lax.broadcasted_iota(jnp.int32, (LANES, LANES), 1) < HEAD_DIM
    same_head = (r == c).astype(BF16)
    hi = v.astype(BF16)
    lo = (v - hi.astype(F32)).astype(BF16)
    return (_dot(hi, same_head) + _dot(lo, same_head)) * (1.0 / HEAD_DIM)


def _head_norm(x, gain, m0, m1):
    r = lax.rsqrt(_head_mean(x * x, m0, m1) + EPS)
    xh = x * r
    return xh * gain, xh, r


def _head_norm_bwd(xh, r, gain, dy, m0, m1):
    gdy = gain * dy
    dx = r * (gdy - xh * _head_mean(xh * gdy, m0, m1))
    return dx, jnp.sum(dy * xh, axis=0, keepdims=True)


def _attn_prep(mode, pair, s_len, padk, q_ref, k_ref, v_ref, gq_ref, gk_ref, qs, k0, k1, v0, v1):
    m0, m1 = _head_masks()
    zpad = jnp.zeros((padk, LANES), BF16)
    for buf in (k0, k1, v0, v1):
        buf[pl.ds(0, padk), :] = zpad
    first_kv = (pair // 2) == 0
    rt = _row_tile(s_len, 256)

    def step(i, carry):
        rows = pl.ds(pl.multiple_of(i * rt, rt), rt)
        dst = pl.ds(pl.multiple_of(padk + i * rt, QTILE), rt)
        qn, _, _ = _head_norm(q_ref[rows, :], gq_ref[...], m0, m1)
        kn, _, _ = _head_norm(k_ref[rows, :], gk_ref[...], m0, m1)
        vv = v_ref[rows, :]
        qs[rows, :] = (qn * (HEAD_DIM ** -0.5)).astype(BF16)
        if mode == "B":
            kn = jnp.where(first_kv, kn, pltpu.roll(kn, HEAD_DIM, 1))
            vv = jnp.where(first_kv, vv, pltpu.roll(vv, HEAD_DIM, 1))
            ka, va = kn * m0, vv * m0
            kb, vb = pltpu.roll(ka, HEAD_DIM, 1), pltpu.roll(va, HEAD_DIM, 1)
        else:
            ka, kb, va, vb = kn * m0, kn * m1, vv * m0, vv * m1
        k0[dst, :] = ka.astype(BF16)
        k1[dst, :] = kb.astype(BF16)
        v0[dst, :] = va.astype(BF16)
        v1[dst, :] = vb.astype(BF16)
        return carry

    lax.fori_loop(0, s_len // rt, step, 0)


def _attn_probs(mode, q2, kb, bias, ok, sink):
    s = _dot_nt(q2, kb) + bias
    s = jnp.where(ok, s, NEG_INF)
    mx = jnp.max(s, axis=-1, keepdims=True)
    if mode == "B":
        mx = jnp.maximum(mx, sink)
    e = jnp.exp(s - mx)
    l = jnp.sum(e, axis=-1, keepdims=True)
    if mode == "B":
        l = l + jnp.exp(sink - mx)
    return e, mx, l


def _attn_cols(mode):
    if mode == "A":
        return (lambda b, p: (b, p)), (lambda b, p: (b, 4 + p)), (lambda b, p: (b, 8 + p))
    return (lambda b, p: (b, 12 + p)), (lambda b, p: (b, 16)), (lambda b, p: (b, 17))


def _attn_fwd(mode, qkv, gq, gk, bias, sinks, bl, s_len, name):
    bw = bias.shape[-1]
    padk = bw - QTILE
    nt = s_len // QTILE
    qmap, kmap, vmap = _attn_cols(mode)

    def body(q_ref, k_ref, v_ref, gq_ref, gk_ref, bias_ref, sink_ref, o_ref, qs, k0, k1, v0, v1):
        pair = pl.program_id(1)
        _attn_prep(mode, pair, s_len, padk, q_ref, k_ref, v_ref, gq_ref, gk_ref, qs, k0, k1, v0, v1)
        col = lax.broadcasted_iota(jnp.int32, (QTILE, bw), 1)

        def tile(m, carry):
            r0 = pl.multiple_of(m * QTILE, QTILE)
            q2 = qs[pl.ds(r0, QTILE), :]
            ok = col >= (padk - r0)
            acc = jnp.zeros((QTILE, LANES), F32)
            for hh, (kk, vv) in enumerate(((k0, v0), (k1, v1))):
                sink = sink_ref[2 * pair + hh]
                e, _, l = _attn_probs(mode, q2, kk[pl.ds(r0, bw), :], bias_ref[hh], ok, sink)
                acc = acc + _dot(e.astype(BF16), vv[pl.ds(r0, bw), :]) / l
            o_ref[pl.ds(r0, QTILE), :] = acc
            return carry

        lax.fori_loop(0, nt, tile, 0, unroll=2)

    blk = lambda f: pl.BlockSpec((s_len, LANES), f)
    vec = pl.BlockSpec((1, LANES), lambda b, p: (0, 0))
    return pl.pallas_call(
        body, name=name, grid=(bl, 4),
        out_shape=jax.ShapeDtypeStruct((bl * s_len, 4 * LANES), F32),
        in_specs=[blk(qmap), blk(kmap), blk(vmap), vec, vec,
                  pl.BlockSpec((2, QTILE, bw), lambda b, p: (p, 0, 0)),
                  pl.BlockSpec(memory_space=pltpu.SMEM)],
        out_specs=pl.BlockSpec((s_len, LANES), lambda b, p: (b, p)),
        scratch_shapes=[pltpu.VMEM((s_len, LANES), BF16)] + [pltpu.VMEM((s_len + padk, LANES), BF16)] * 4,
        compiler_params=_params(("arbitrary", "arbitrary")),
    )(qkv, qkv, qkv, gq, gk, bias, sinks)


def _attn_bwd(mode, qkv, gq, gk, bias, sinks, y, dy, bl, s_len, name):
    bw = bias.shape[-1]
    padk = bw - QTILE
    nt = s_len // QTILE
    qmap, kmap, vmap = _attn_cols(mode)
    t = bl * s_len
    kvw = 4 * LANES if mode == "A" else LANES

    def body(q_ref, k_ref, v_ref, gq_ref, gk_ref, bias_ref, sink_ref, y_ref, dy_ref,
             dq_ref, dk_ref, dv_ref, dgq_ref, dgk_ref, dbias_ref, dsink_ref,
             qs, k0, k1, v0, v1, dqs, dk0, dk1, dv0, dv1):
        pair = pl.program_id(1)
        m0, m1 = _head_masks()
        _attn_prep(mode, pair, s_len, padk, q_ref, k_ref, v_ref, gq_ref, gk_ref, qs, k0, k1, v0, v1)
        for buf in (dk0, dk1, dv0, dv1):
            buf[...] = jnp.zeros_like(buf)
        dbias_ref[...] = jnp.zeros_like(dbias_ref)
        col = lax.broadcasted_iota(jnp.int32, (QTILE, bw), 1)
        lane8 = lax.broadcasted_iota(jnp.int32, (8, LANES), 1)

        def tile(m, dsink):
            r0 = pl.multiple_of(m * QTILE, QTILE)
            rows = pl.ds(r0, QTILE)
            band = pl.ds(r0, bw)
            q2 = qs[rows, :]
            do2 = dy_ref[rows, :]
            dd = do2 * y_ref[rows, :]
            dob = do2.astype(BF16)
            ok = col >= (padk - r0)
            dq = jnp.zeros((QTILE, LANES), F32)
            for hh, (kk, vv, dkk, dvv, mh) in enumerate(((k0, v0, dk0, dv0, m0), (k1, v1, dk1, dv1, m1))):
                sink = sink_ref[2 * pair + hh]
                kb = kk[band, :]
                e, mx, l = _attn_probs(mode, q2, kb, bias_ref[hh], ok, sink)
                inv = 1.0 / l
                pn = e * inv
                delta = jnp.sum(dd * mh, axis=-1, keepdims=True)
                dp = _dot_nt(dob, vv[band, :])
                ds = pn * (dp - delta)
                if mode == "A":
                    dbias_ref[hh] += ds
                else:
                    ps = jnp.exp(sink - mx) * inv
                    dsink = dsink + jnp.where(lane8 == hh, -jnp.sum(ps * delta), 0.0)
                dsb = ds.astype(BF16)
                dvv[band, :] += _dot_tn(pn.astype(BF16), dob)
                dkk[band, :] += _dot_tn(dsb, q2)
                dq = dq + _dot(dsb, kb)
            dqs[rows, :] = dq * (HEAD_DIM ** -0.5)
            return dsink

        dsink = lax.fori_loop(0, nt, tile, jnp.zeros((8, LANES), F32), unroll=2)
        dsink_ref[...] = dsink

        first_kv = (pair // 2) == 0
        rt = _row_tile(s_len, 256)

        def post(i, carry):
            dgq, dgk = carry
            rows = pl.ds(pl.multiple_of(i * rt, rt), rt)
            src = pl.ds(pl.multiple_of(padk + i * rt, QTILE), rt)
            gq_v, gk_v = gq_ref[...], gk_ref[...]
            _, qh, qr = _head_norm(q_ref[rows, :], gq_v, m0, m1)
            _, kh, kr = _head_norm(k_ref[rows, :], gk_v, m0, m1)
            dq_raw, dgq_i = _head_norm_bwd(qh, qr, gq_v, dqs[rows, :], m0, m1)
            if mode == "A":
                dkn = dk0[src, :] * m0 + dk1[src, :] * m1
                dvn = dv0[src, :] * m0 + dv1[src, :] * m1
            else:
                dkn = dk0[src, :] * m0 + pltpu.roll(dk1[src, :] * m1, HEAD_DIM, 1)
                dvn = dv0[src, :] * m0 + pltpu.roll(dv1[src, :] * m1, HEAD_DIM, 1)
                dkn = jnp.where(first_kv, dkn, pltpu.roll(dkn, HEAD_DIM, 1))
                dvn = jnp.where(first_kv, dvn, pltpu.roll(dvn, HEAD_DIM, 1))
            dk_raw, dgk_i = _head_norm_bwd(kh, kr, gk_v, dkn, m0, m1)
            dq_ref[rows, :] = dq_raw
            if mode == "A":
                dk_ref[rows, :] = dk_raw
                dv_ref[rows, :] = dvn
            else:
                @pl.when(pair == 0)
                def _():
                    dk_ref[rows, :] = dk_raw
                    dv_ref[rows, :] = dvn

                @pl.when(pair != 0)
                def _():
                    dk_ref[rows, :] += dk_raw
                    dv_ref[rows, :] += dvn
            return dgq + dgq_i, dgk + dgk_i

        z = jnp.zeros((1, LANES), F32)
        dgq, dgk = lax.fori_loop(0, s_len // rt, post, (z, z))
        dgq_ref[...] = jnp.broadcast_to(dgq, (8, LANES))
        dgk_ref[...] = jnp.broadcast_to(dgk, (8, LANES))

    blk = lambda f: pl.BlockSpec((s_len, LANES), f)
    vec = pl.BlockSpec((1, LANES), lambda b, p: (0, 0))
    small = pl.BlockSpec((None, None, 8, LANES), lambda b, p: (b, p, 0, 0))
    kvmap = (lambda b, p: (b, p)) if mode == "A" else (lambda b, p: (b, 0))
    pad_f32 = pltpu.VMEM((s_len + padk, LANES), F32)
    pad_bf = pltpu.VMEM((s_len + padk, LANES), BF16)
    return pl.pallas_call(
        body, name=name, grid=(bl, 4),
        out_shape=[jax.ShapeDtypeStruct((t, 4 * LANES), F32), jax.ShapeDtypeStruct((t, kvw), F32),
                   jax.ShapeDtypeStruct((t, kvw), F32),
                   jax.ShapeDtypeStruct((bl, 4, 8, LANES), F32), jax.ShapeDtypeStruct((bl, 4, 8, LANES), F32),
                   jax.ShapeDtypeStruct((bl, 8, QTILE, bw), F32), jax.ShapeDtypeStruct((bl, 4, 8, LANES), F32)],
        in_specs=[blk(qmap), blk(kmap), blk(vmap), vec, vec,
                  pl.BlockSpec((2, QTILE, bw), lambda b, p: (p, 0, 0)),
                  pl.BlockSpec(memory_space=pltpu.SMEM),
                  blk(lambda b, p: (b, p)), blk(lambda b, p: (b, p))],
        out_specs=[blk(lambda b, p: (b, p)), blk(kvmap), blk(kvmap), small, small,
                   pl.BlockSpec((None, 2, QTILE, bw), lambda b, p: (b, p, 0, 0)), small],
        scratch_shapes=[pltpu.VMEM((s_len, LANES), BF16), pad_bf, pad_bf, pad_bf, pad_bf,
                        pltpu.VMEM((s_len, LANES), F32), pad_f32, pad_f32, pad_f32, pad_f32],
        compiler_params=_params(("arbitrary", "arbitrary")),
    )(qkv, qkv, qkv, gq, gk, bias, sinks, y, dy)


def _band_geometry(prev):
    bw = QTILE + prev * CHUNK
    i = np.arange(QTILE)[:, None]
    j = np.arange(bw)[None, :]
    dist = i + prev * CHUNK - j
    valid = (j // CHUNK >= i // CHUNK) & (j // CHUNK <= i // CHUNK + prev)
    return dist, valid


A_VAR0 = (A_PREV * CHUNK - A_MAX_REL) // LANES * LANES


A_NVAR = QTILE + A_PREV * CHUNK - A_VAR0


def _skew_rows(x, sign):
    rows, n = x.shape
    row = lax.broadcasted_iota(jnp.int32, x.shape, 0)
    b = 1
    while b < rows:
        x = jnp.where((row & b) != 0, pltpu.roll(x, (sign * b) % n, 1), x)
        b *= 2
    return x


def _rel_bias_expand(table, name):
    _, valid = _band_geometry(A_PREV)
    bw = valid.shape[1]
    valid_f = jnp.asarray(valid.astype(np.float32))
    rev = jnp.flip(table[:, 1:], axis=1).reshape(A_HEADS, 1, A_NVAR)

    def body(rev_ref, valid_ref, o_ref):
        rowv = jnp.broadcast_to(rev_ref[...], (QTILE, A_NVAR))
        top = rowv[:, 0:1]
        var = _skew_rows(rowv, 1)
        row = lax.broadcasted_iota(jnp.int32, (QTILE, A_NVAR), 0)
        colv = lax.broadcasted_iota(jnp.int32, (QTILE, A_NVAR), 1)
        var = jnp.where(colv < row, top, var)
        ok = valid_ref[...] > 0.5
        o_ref[:, :A_VAR0] = jnp.where(ok[:, :A_VAR0], top, NEG_INF)
        o_ref[:, A_VAR0:] = jnp.where(ok[:, A_VAR0:], var, NEG_INF)

    return pl.pallas_call(
        body, name=name, grid=(A_HEADS,),
        out_shape=jax.ShapeDtypeStruct((A_HEADS, QTILE, bw), F32),
        in_specs=[pl.BlockSpec((None, 1, A_NVAR), lambda h: (h, 0, 0)), pl.BlockSpec((QTILE, bw), lambda h: (0, 0))],
        out_specs=pl.BlockSpec((None, QTILE, bw), lambda h: (h, 0, 0)),
        compiler_params=_params(("arbitrary",)),
    )(rev, valid_f)


def _rel_bias_grad(dbias, name):
    bl = dbias.shape[0]
    bw = dbias.shape[-1]

    def body(db_ref, o_ref):
        g = db_ref[0]
        for b in range(1, bl):
            g = g + db_ref[b]
        sk = _skew_rows(g[:, A_VAR0:], -1)
        row = lax.broadcasted_iota(jnp.int32, (QTILE, A_NVAR), 0)
        colv = lax.broadcasted_iota(jnp.int32, (QTILE, A_NVAR), 1)
        wrapped = (row + colv) >= A_NVAR
        main = jnp.sum(jnp.where(wrapped, 0.0, sk), axis=0, keepdims=True)
        top = jnp.sum(g[:, :A_VAR0]) + jnp.sum(jnp.where(wrapped, sk, 0.0))
        o_ref[:, :A_NVAR] = jnp.broadcast_to(main, (8, A_NVAR))
        o_ref[:, A_NVAR:] = jnp.full((8, LANES), top, F32)

    out = pl.pallas_call(
        body, name=name, grid=(A_HEADS,),
        out_shape=jax.ShapeDtypeStruct((A_HEADS, 8, A_NVAR + LANES), F32),
        in_specs=[pl.BlockSpec((bl, None, QTILE, bw), lambda h: (0, h, 0, 0))],
        out_specs=pl.BlockSpec((None, 8, A_NVAR + LANES), lambda h: (h, 0, 0)),
        compiler_params=_params(("arbitrary",)),
    )(dbias)
    main, top = out[:, 0, :A_NVAR], out[:, 0, A_NVAR]
    fm = jnp.flip(main, axis=1)
    return jnp.concatenate([jnp.zeros((A_HEADS, 1), F32), fm[:, :-1], fm[:, -1:] + top[:, None]], axis=1)


def _alibi_bias():
    dist, valid = _band_geometry(B_PREV)
    slopes = np.array([2.0 ** (-8.0 * (h + 1) / B_Q_HEADS) for h in range(B_Q_HEADS)], dtype=np.float32)
    bias = -slopes[:, None, None] * np.abs(dist).astype(np.float32)[None]
    return jnp.asarray(np.where(valid[None], bias, np.float32(NEG_INF)).astype(np.float32))


SMALL_NAMES = ("ffn1_norm", "mix_norm", "ffn2_norm", "ple_norm", "a_q_norm", "a_k_norm", "b_q_norm", "b_k_norm",
               "a_rel_bias", "b_sinks", "loss")


def _pack_small(vals):
    rows = []
    for nme in SMALL_NAMES:
        v = vals[nme].astype(F32)
        if nme == "a_rel_bias":
            v = jnp.pad(v.reshape(A_HEADS, -1), ((0, 0), (0, 3 * LANES - (2 * A_MAX_REL + 1))))
        v = v.reshape(-1)
        v = jnp.pad(v, (0, (-v.shape[0]) % LANES))
        rows.append(v.reshape(-1, LANES))
    out = jnp.concatenate(rows, axis=0)
    return jnp.pad(out, ((0, (-out.shape[0]) % 8), (0, 0)))


def _unpack_small(packed, shapes):
    out, r = {}, 0
    for nme in SMALL_NAMES:
        shp = shapes[nme]
        if nme == "a_rel_bias":
            nr = A_HEADS * 3
            out[nme] = packed[r:r + nr].reshape(A_HEADS, 3 * LANES)[:, :2 * A_MAX_REL + 1].reshape(shp)
        else:
            size = int(np.prod(shp)) if shp else 1
            nr = -(-size // LANES)
            out[nme] = packed[r:r + nr].reshape(-1)[:size].reshape(shp)
        r += nr
    return out


BIG_NAMES = ("ffn1_w_gu", "ffn1_w_down", "w_in", "w_gate", "w_proj_a", "w_proj_b", "w_out",
             "ffn2_w_gu", "ffn2_w_down", "w_ple_gate", "w_ple_proj")
ROW_SHARDED = ("ffn1_w_down", "ffn2_w_down", "w_out", "w_ple_gate")
WEIGHT_ORDER = ("ffn1_norm", "ffn1_w_gu", "ffn1_w_down", "mix_norm", "w_in", "a_q_norm", "a_k_norm", "a_rel_bias",
                "b_q_norm", "b_k_norm", "b_sinks", "w_gate", "w_proj_a", "w_proj_b", "w_out", "ffn2_norm",
                "ffn2_w_gu", "ffn2_w_down", "ple_norm", "w_ple_gate", "w_ple_proj")


TRANSPOSED = ("ffn1_w_gu", "ffn2_w_gu", "w_in")


def _local(a, nme):
    return a[0].T if nme in TRANSPOSED else a[0]


def _full_cols(wg):
    nb, k, n = wg.shape
    return jnp.transpose(wg, (1, 0, 2)).reshape(k, nb * n)


def _col_blocks(g, nb):
    k, n = g.shape
    return jnp.transpose(g.reshape(k, nb, n // nb), (1, 0, 2))


def _step(x, p, target, w, m, v):
    bl, s_len, d = x.shape
    t = bl * s_len
    h0 = x.reshape(t, d)
    pt = p.reshape(t, p.shape[-1])
    tgt = target.reshape(t, d)

    g_ffn1, g_mix, g_ffn2, g_ple = w["ffn1_norm"], w["mix_norm"], w["ffn2_norm"], w["ple_norm"]
    tile2 = lambda a: jnp.tile(a.reshape(1, HEAD_DIM), (1, 2))
    gqa, gka, gqb, gkb = tile2(w["a_q_norm"]), tile2(w["a_k_norm"]), tile2(w["b_q_norm"]), tile2(w["b_k_norm"])
    sinks = w["b_sinks"].reshape(B_Q_HEADS)
    bias_a = _rel_bias_expand(w["a_rel_bias"][0], "rel_bias_expand")
    bias_b = _alibi_bias()

    shard = {nme: _local(w[nme], nme).astype(BF16) for nme in BIG_NAMES}
    wgu1, wd1 = _all_gather([shard["ffn1_w_gu"], shard["ffn1_w_down"]], "weights_gather_ffn1")
    nf = wgu1.shape[1]
    wd1 = wd1.reshape(N_DEV // 2, nf, d)
    mixer_names = ("w_in", "w_gate")
    rest_names = ("w_proj_a", "w_proj_b", "w_out", "ffn2_w_gu", "ffn2_w_down", "w_ple_gate", "w_ple_proj")
    send1, recv1, bufs, token = _gather_start([shard[nme] for nme in mixer_names], wgu1, "gather_start_mixer")

    h1, gu1 = _ffn_fwd(h0, g_ffn1 + token[0, 0], wgu1, wd1, "ffn1_fwd")
    send2, recv2, bufs, token = _gather_pass(send1, recv1, bufs, h1, "gather_pass_mixer")
    send1, recv1, rest_bufs, token = _gather_start([shard[nme] for nme in rest_names], token, "gather_start_rest")
    win, wgate = _gather_wait(send2, recv2, bufs, token, "gather_wait_mixer")
    win, wgate = win.reshape(IN_COLS, d), _full_cols(wgate)
    un, qkv, gate = _proj_fwd(h1, g_mix, win, wgate, "proj_fwd")
    ya = _attn_fwd("A", qkv, gqa, gka, bias_a, sinks, bl, s_len, "attn_a_fwd")
    yb = _attn_fwd("B", qkv, gqb, gkb, bias_b, sinks, bl, s_len, "attn_b_fwd")
    send2, recv2, rest_bufs, token = _gather_pass(send1, recv1, rest_bufs, yb, "gather_pass_rest")
    gathered = dict(zip(rest_names, _gather_wait(send2, recv2, rest_bufs, token, "gather_wait_rest")))
    wgu2 = gathered["ffn2_w_gu"]
    wd2 = gathered["ffn2_w_down"].reshape(N_DEV // 2, nf, d)
    wpa = _full_cols(gathered["w_proj_a"])
    wpb = _full_cols(gathered["w_proj_b"])
    wpe = _full_cols(gathered["w_ple_proj"])
    wout = gathered["w_out"].reshape(d, d)
    wpg = gathered["w_ple_gate"].reshape(d, d)
    h2, merged, pa, pb = _merge_fwd(h1, ya, yb, gate, wpa, wpb, wout, "merge_fwd")
    h3, gu2 = _ffn_fwd(h2, g_ffn2, wgu2, wd2, "ffn2_fwd")
    dh3, dz4, dpp, n4, dg_ple, loss_part = _ple_loss(h3, g_ple, pt, tgt, wpg, wpe, "ple_loss")

    xi, yi, ci = _place()
    me = jnp.stack([4 * xi + 2 * yi + ci]).astype(jnp.int32)
    g32, g16, big = {}, {}, {}

    def keep(nme, pair, rows=None):
        for store, g in zip((g32, g16), pair):
            store[nme] = g if rows is None else g.reshape(N_DEV, rows, d)

    def start(names, after, tag):
        send, recv, parts, lands, token = _scatter_start([g16[nme] for nme in names], after, "grads_start_" + tag)
        return names, send, recv, parts, lands, token

    def finish(state, after, tag):
        names, send, recv, parts, lands, _ = state
        lands = _scatter_wait(send, recv, parts, lands, after, "grads_wait_" + tag)
        return names, lands

    def adam(done):
        for nme, land in zip(*done):
            outs = _final_adam(g32[nme], land, _local(w[nme], nme), _local(m[nme], nme), _local(v[nme], nme), me, "adam_" + nme)
            big[nme] = [(o.T if nme in TRANSPOSED else o)[None] for o in outs]

    keep("w_ple_gate", _dw(n4, dz4, 1, d, "dw_ple_gate"), d // N_DEV)
    keep("w_ple_proj", _dw(pt, dpp, N_DEV, d // N_DEV, "dw_ple_proj"))

    dh2, dgu2, a2, n3, dg_ffn2 = _ffn_bwd(dh3, h2, g_ffn2, gu2, wgu2, wd2, "ffn2_bwd")
    keep("ffn2_w_gu", _dw(dgu2, n3, N_DEV, d, "dw_ffn2_gu"))
    keep("ffn2_w_down", _dw(a2, dh3, N_DEV // 2, d, "dw_ffn2_down", 0.5), nf // 2)
    flight = start(("w_ple_gate", "w_ple_proj", "ffn2_w_gu", "ffn2_w_down"), dh2, "ffn2")

    dpa, dpb, dzg, dya, dyb = _merge_bwd(dh2, pa, pb, gate, wpa, wpb, wout, "merge_bwd")
    keep("w_out", _dw(merged, dh2, 1, d, "dw_out"), d // N_DEV)
    keep("w_proj_a", _dw(ya, dpa, N_DEV, d // N_DEV, "dw_proj_a"))
    keep("w_proj_b", _dw(yb, dpb, N_DEV, d // N_DEV, "dw_proj_b"))
    keep("w_gate", _dw(un, dzg, N_DEV, 2 * d // N_DEV, "dw_gate"))

    tok = flight[-1][0, 0]
    dqa, dka, dva, dgqa, dgka, dbias, _ = _attn_bwd("A", qkv, gqa + tok, gka, bias_a, sinks, ya, dya, bl, s_len,
                                                     "attn_a_bwd")
    dqb, dkb, dvb, dgqb, dgkb, _, dsink = _attn_bwd("B", qkv, gqb, gkb, bias_b, sinks, yb, dyb, bl, s_len, "attn_b_bwd")
    dqkv = jnp.concatenate([dqa, dka, dva, dqb, dkb, dvb], axis=1)
    dtab = _rel_bias_grad(dbias, "rel_bias_grad")

    dh1, dg_mix = _proj_bwd(dh2, h1, g_mix, dzg, dqkv, win, wgate, "proj_bwd")
    keep("w_in", _dw(dqkv, un, 1, d, "dw_in"), IN_COLS // N_DEV)
    done = finish(flight, g32["w_in"], "ffn2")
    flight = start(("w_out", "w_proj_a", "w_proj_b", "w_gate", "w_in"), done[1][0], "mixer")
    adam(done)

    dh0, dgu1, a1, n1, dg_ffn1 = _ffn_bwd(dh1, h0, g_ffn1 + flight[-1][0, 0], gu1, wgu1, wd1, "ffn1_bwd")
    keep("ffn1_w_down", _dw(a1, dh1, N_DEV // 2, d, "dw_ffn1_down", 0.5), nf // 2)
    done = finish(flight, g32["ffn1_w_down"], "mixer")
    flight = start(("ffn1_w_down",), done[1][0], "ffn1_down")
    adam(done)

    keep("ffn1_w_gu", _dw(dgu1, n1, N_DEV, d, "dw_ffn1_gu", dep=flight[-1]))
    done = finish(flight, g32["ffn1_w_gu"], "ffn1_down")
    flight = start(("ffn1_w_gu",), done[1][0], "ffn1_gu")
    adam(done)
    smalls = (dg_ffn1, dg_mix, dg_ffn2, dg_ple + flight[-1][0, 0], dgqa, dgka, dgqb, dgkb, dtab, dsink)
    return dh0, loss_part, big, smalls, flight, finish, adam


def kernel(x, p, ffn1_norm, ffn1_w_gu, ffn1_w_down, mix_norm, w_in, a_q_norm, a_k_norm, a_rel_bias, b_q_norm, b_k_norm, b_sinks, w_gate, w_proj_a, w_proj_b, w_out, ffn2_norm, ffn2_w_gu, ffn2_w_down, ple_norm, w_ple_gate, w_ple_proj, loss_target, m_ffn1_norm, m_ffn1_w_gu, m_ffn1_w_down, m_mix_norm, m_w_in, m_a_q_norm, m_a_k_norm, m_a_rel_bias, m_b_q_norm, m_b_k_norm, m_b_sinks, m_w_gate, m_w_proj_a, m_w_proj_b, m_w_out, m_ffn2_norm, m_ffn2_w_gu, m_ffn2_w_down, m_ple_norm, m_w_ple_gate, m_w_ple_proj, v_ffn1_norm, v_ffn1_w_gu, v_ffn1_w_down, v_mix_norm, v_w_in, v_a_q_norm, v_a_k_norm, v_a_rel_bias, v_b_q_norm, v_b_k_norm, v_b_sinks, v_w_gate, v_w_proj_a, v_w_proj_b, v_w_out, v_ffn2_norm, v_ffn2_w_gu, v_ffn2_w_down, v_ple_norm, v_w_ple_gate, v_w_ple_proj):
    w = dict(ffn1_norm=ffn1_norm, ffn1_w_gu=ffn1_w_gu, ffn1_w_down=ffn1_w_down, mix_norm=mix_norm, w_in=w_in,
             a_q_norm=a_q_norm, a_k_norm=a_k_norm, a_rel_bias=a_rel_bias, b_q_norm=b_q_norm, b_k_norm=b_k_norm,
             b_sinks=b_sinks, w_gate=w_gate, w_proj_a=w_proj_a, w_proj_b=w_proj_b, w_out=w_out, ffn2_norm=ffn2_norm,
             ffn2_w_gu=ffn2_w_gu, ffn2_w_down=ffn2_w_down, ple_norm=ple_norm, w_ple_gate=w_ple_gate,
             w_ple_proj=w_ple_proj)
    m = dict(ffn1_norm=m_ffn1_norm, ffn1_w_gu=m_ffn1_w_gu, ffn1_w_down=m_ffn1_w_down, mix_norm=m_mix_norm,
             w_in=m_w_in, a_q_norm=m_a_q_norm, a_k_norm=m_a_k_norm, a_rel_bias=m_a_rel_bias, b_q_norm=m_b_q_norm,
             b_k_norm=m_b_k_norm, b_sinks=m_b_sinks, w_gate=m_w_gate, w_proj_a=m_w_proj_a, w_proj_b=m_w_proj_b,
             w_out=m_w_out, ffn2_norm=m_ffn2_norm, ffn2_w_gu=m_ffn2_w_gu, ffn2_w_down=m_ffn2_w_down,
             ple_norm=m_ple_norm, w_ple_gate=m_w_ple_gate, w_ple_proj=m_w_ple_proj)
    v = dict(ffn1_norm=v_ffn1_norm, ffn1_w_gu=v_ffn1_w_gu, ffn1_w_down=v_ffn1_w_down, mix_norm=v_mix_norm,
             w_in=v_w_in, a_q_norm=v_a_q_norm, a_k_norm=v_a_k_norm, a_rel_bias=v_a_rel_bias, b_q_norm=v_b_q_norm,
             b_k_norm=v_b_k_norm, b_sinks=v_b_sinks, w_gate=v_w_gate, w_proj_a=v_w_proj_a, w_proj_b=v_w_proj_b,
             w_out=v_w_out, ffn2_norm=v_ffn2_norm, ffn2_w_gu=v_ffn2_w_gu, ffn2_w_down=v_ffn2_w_down,
             ple_norm=v_ple_norm, w_ple_gate=v_w_ple_gate, w_ple_proj=v_w_ple_proj)
    bl, s_len, d = x.shape

    dh0, loss_part, big, smalls, flight, finish, adam = _step(x, p[0], loss_target, w, m, v)
    dg_ffn1, dg_mix, dg_ffn2, dg_ple, dgqa, dgka, dgqb, dgkb, dtab, dsink = smalls

    fold = lambda a: (a[:, :, 0, :HEAD_DIM] + a[:, :, 0, HEAD_DIM:]).sum(axis=(0, 1))
    small_part = dict(
        ffn1_norm=dg_ffn1, mix_norm=dg_mix, ffn2_norm=dg_ffn2, ple_norm=dg_ple,
        a_q_norm=fold(dgqa), a_k_norm=fold(dgka), b_q_norm=fold(dgqb), b_k_norm=fold(dgkb),
        a_rel_bias=dtab,
        b_sinks=dsink.sum(axis=0)[:, 0, :2].reshape(B_Q_HEADS),
        loss=loss_part[0, :1])
    zero1 = jnp.zeros((1,), F32)
    shapes = {nme: w[nme].shape for nme in SMALL_NAMES if nme != "loss"}
    shapes["loss"] = ()
    pk = lambda src: _pack_small({**{nme: src[nme] for nme in SMALL_NAMES if nme != "loss"}, "loss": zero1})
    sg, sd, sm, sv = _small_allreduce_adam(_pack_small(small_part), pk(w), pk(m), pk(v), "small_allreduce_adam")
    adam(finish(flight, sg, "ffn1_gu"))
    sg, sd, sm, sv = (_unpack_small(a, shapes) for a in (sg, sd, sm, sv))

    def pick(i):
        out = []
        for nme in WEIGHT_ORDER:
            out.append(big[nme][i] if nme in big else (sg, sd, sm, sv)[i][nme])
        return out

    return (sg["loss"], dh0.reshape(bl, s_len, d), *pick(0), *pick(1), *pick(2), *pick(3))
```

```python
import functools

import jax
import jax.numpy as jnp
import numpy as np
from jax import lax
from jax.experimental import pallas as pl
from jax.experimental.pallas import tpu as pltpu

F32 = jnp.float32
BF16 = jnp.bfloat16

CHUNK = 64
HEAD_DIM = 64
A_HEADS = 8
A_PREV = 8
A_MAX_REL = 128
B_Q_HEADS = 8
B_KV_HEADS = 2
B_PREV = 2
A_WIDTH = A_HEADS * HEAD_DIM
B_Q_WIDTH = B_Q_HEADS * HEAD_DIM
B_KV_WIDTH = B_KV_HEADS * HEAD_DIM
IN_COLS = 3 * A_WIDTH + B_Q_WIDTH + 2 * B_KV_WIDTH
EPS = 1e-6
NEG_INF = -1e30
ADAM_LR = 0.001
ADAM_B1 = 0.9
ADAM_B2 = 0.999
ADAM_EPS = 1e-08
ADAM_WD = 0.01
ADAM_STEP = 10

N_DEV = 8
LANES = 128
QTILE = 2 * CHUNK
VMEM_LIMIT = 56 * 1024 * 1024
ADAM_TILE_ELEMS = 256 * 1024

MESH_ID = pl.DeviceIdType.MESH
ANY = pl.BlockSpec(memory_space=pl.ANY)
HBM = pl.BlockSpec(memory_space=pltpu.HBM)
SEM = pl.BlockSpec(memory_space=pltpu.SEMAPHORE)
SIDE_EFFECT = pltpu.SideEffectType.DATAFLOW_SIDE_EFFECTING


def _dot(a, b):
    return jnp.dot(a, b, preferred_element_type=F32)


def _dot_nt(a, b):
    return lax.dot_general(a, b, (((1,), (1,)), ((), ())), preferred_element_type=F32)


def _dot_tn(a, b):
    return lax.dot_general(a, b, (((0,), (0,)), ((), ())), preferred_element_type=F32)


def _params(sem=None, vmem=VMEM_LIMIT):
    return pltpu.CompilerParams(dimension_semantics=sem, vmem_limit_bytes=vmem)


def _row_tile(t, want):
    while t % want:
        want //= 2
    return want


def _place():
    return lax.axis_index("x"), lax.axis_index("y"), lax.axis_index("c")


def _all_gather(shards, name):
    n = len(shards)

    def body(*refs):
        ins, outs = refs[:n], refs[n:2 * n]
        send_sems, recv_sems, local_sems = refs[2 * n:]
        x, y, c = _place()
        me, sib = (x, y, c), (x, y, 1 - c)
        chips = [(1 - x, y), (x, 1 - y), (1 - x, 1 - y)]

        def copy(w, k, block, to, src=None):
            px, py, pc = block
            dst = outs[w].at[4 * px + 2 * py + pc]
            return pltpu.make_async_remote_copy(
                src_ref=dst if src is None else src, dst_ref=dst,
                send_sem=send_sems.at[w * 7 + k], recv_sem=recv_sems.at[w * 7 + k],
                device_id=to, device_id_type=MESH_ID)

        mine = [pltpu.make_async_copy(ins[w], outs[w].at[4 * x + 2 * y + c], local_sems.at[w]) for w in range(n)]
        for cp in mine:
            cp.start()
        first = []
        for w in range(n):
            first.append(copy(w, 0, me, sib, src=ins[w]))
            first += [copy(w, 1 + j, me, (*chip, c), src=ins[w]) for j, chip in enumerate(chips)]
        for cp in first:
            cp.start()
        passed = []
        for j, chip in enumerate(chips):
            for w in range(n):
                copy(w, 1 + j, (*chip, c), me).wait_recv()
                fwd = copy(w, 4 + j, (*chip, c), sib)
                fwd.start()
                passed.append(fwd)
        for w in range(n):
            copy(w, 0, sib, me).wait_recv()
        for j, chip in enumerate(chips):
            for w in range(n):
                copy(w, 4 + j, (*chip, 1 - c), me).wait_recv()
        for cp in first + passed:
            cp.wait_send()
        for cp in mine:
            cp.wait()

    return pl.pallas_call(
        body, name=name,
        out_shape=[jax.ShapeDtypeStruct((N_DEV,) + s.shape, s.dtype) for s in shards],
        in_specs=[ANY] * n, out_specs=[ANY] * n,
        scratch_shapes=[pltpu.SemaphoreType.DMA((7 * n,)), pltpu.SemaphoreType.DMA((7 * n,)),
                        pltpu.SemaphoreType.DMA((n,))],
    )(*shards)


def _gather_level(bufs, send_sems, recv_sems, level):
    x, y, c = _place()
    me, sib = (x, y, c), (x, y, 1 - c)
    chips = [(1 - x, y), (x, 1 - y), (1 - x, 1 - y)]

    def copy(w, k, block, to):
        px, py, pc = block
        rows = bufs[w].at[4 * px + 2 * py + pc]
        return pltpu.make_async_remote_copy(src_ref=rows, dst_ref=rows, send_sem=send_sems.at[k], recv_sem=recv_sems.at[k],
                                            device_id=to, device_id_type=MESH_ID)

    out, arriving = [], []
    for w in range(len(bufs)):
        if level == 1:
            out.append(copy(w, 4 * w, me, sib))
            arriving.append(copy(w, 4 * w, sib, me))
        for j, chip in enumerate(chips):
            if level == 1:
                out.append(copy(w, 4 * w + 1 + j, me, (*chip, c)))
                arriving.append(copy(w, 4 * w + 1 + j, (*chip, c), me))
            else:
                out.append(copy(w, 3 * w + j, (*chip, c), sib))
                arriving.append(copy(w, 3 * w + j, (*chip, 1 - c), me))
    return out, arriving


def _split_call(body, name, bufs, sems_in, after, n_sems_out, token):
    n = len(bufs)
    out_shape = [pltpu.SemaphoreType.DMA((n_sems_out,))] * (2 if n_sems_out else 0)
    out_shape += [pltpu.HBM(a.shape, a.dtype) for a in bufs]
    out_specs = [SEM] * (2 if n_sems_out else 0) + [HBM] * n
    if token:
        out_shape.append(jax.ShapeDtypeStruct((8, LANES), F32))
        out_specs.append(pl.BlockSpec(memory_space=pltpu.VMEM))
    first = 2 if n_sems_out else 0
    return pl.pallas_call(
        body, name=name, out_shape=tuple(out_shape),
        in_specs=[HBM] * n + [SEM] * len(sems_in) + [ANY], out_specs=tuple(out_specs),
        input_output_aliases={i: first + i for i in range(n)},
        compiler_params=pltpu.CompilerParams(has_side_effects=SIDE_EFFECT),
    )(*bufs, *sems_in, after)


def _gather_start(shards, after, name):
    n = len(shards)
    xi, yi, ci = _place()
    me = 4 * xi + 2 * yi + ci
    bufs = [lax.dynamic_update_slice(lax.empty((N_DEV,) + s.shape, s.dtype), s[None], (me, 0, 0)) for s in shards]
    bufs = [pltpu.with_memory_space_constraint(a, pltpu.HBM) for a in bufs]

    def body(*refs):
        out, _ = _gather_level(refs[:n], refs[n + 1], refs[n + 2], 1)
        for cp in out:
            cp.start()
        refs[-1][...] = jnp.zeros_like(refs[-1])

    outs = _split_call(body, name, bufs, [], after, 4 * n, True)
    return outs[0], outs[1], list(outs[2:2 + n]), outs[-1]


def _gather_pass(send1, recv1, bufs, after, name):
    n = len(bufs)

    def body(*refs):
        out1, in1 = _gather_level(refs[:n], refs[n], refs[n + 1], 1)
        out2, _ = _gather_level(refs[:n], refs[n + 3], refs[n + 4], 2)
        for cp in in1:
            cp.wait_recv()
        for cp in out2:
            cp.start()
        for cp in out1:
            cp.wait_send()
        refs[-1][...] = jnp.zeros_like(refs[-1])

    outs = _split_call(body, name, bufs, [send1, recv1], after, 3 * n, True)
    return outs[0], outs[1], list(outs[2:2 + n]), outs[-1]


def _gather_wait(send2, recv2, bufs, after, name):
    n = len(bufs)

    def body(*refs):
        out2, in2 = _gather_level(refs[:n], refs[n], refs[n + 1], 2)
        for cp in in2:
            cp.wait_recv()
        for cp in out2:
            cp.wait_send()

    return list(_split_call(body, name, bufs, [send2, recv2], after, 0, False))


def _scatter_copies(parts, lands, send_sems, recv_sems):
    x, y, c = _place()
    cps = []
    for w, (part, land) in enumerate(zip(parts, lands)):
        for k in range(1, N_DEV):
            px, py, pc = x ^ ((k >> 2) & 1), y ^ ((k >> 1) & 1), c ^ (k & 1)
            cps.append(pltpu.make_async_remote_copy(
                src_ref=part.at[4 * px + 2 * py + pc], dst_ref=land.at[k - 1],
                send_sem=send_sems.at[7 * w + k - 1], recv_sem=recv_sems.at[7 * w + k - 1],
                device_id=(px, py, pc), device_id_type=MESH_ID))
    return cps


def _scatter_start(parts, after, name):
    n = len(parts)

    def body(*refs):
        ins, lands = refs[:n], refs[n:2 * n]
        send_sems, recv_sems = refs[2 * n + 1], refs[2 * n + 2]
        token = refs[-1]
        for cp in _scatter_copies(ins, lands, send_sems, recv_sems):
            cp.start()
        token[...] = jnp.zeros_like(token)

    land_shapes = [(N_DEV - 1,) + p.shape[1:] for p in parts]
    in_hbm = [pltpu.with_memory_space_constraint(p, pltpu.HBM) for p in parts]
    in_hbm += [pltpu.with_memory_space_constraint(lax.empty(s, p.dtype), pltpu.HBM) for s, p in zip(land_shapes, parts)]
    outs = pl.pallas_call(
        body, name=name,
        out_shape=(pltpu.SemaphoreType.DMA((7 * n,)), pltpu.SemaphoreType.DMA((7 * n,)),
                   *[pltpu.HBM(p.shape, p.dtype) for p in parts],
                   *[pltpu.HBM(s, p.dtype) for s, p in zip(land_shapes, parts)],
                   jax.ShapeDtypeStruct((8, LANES), F32)),
        in_specs=[HBM] * (2 * n) + [ANY],
        out_specs=(SEM, SEM, *[HBM] * (2 * n), pl.BlockSpec(memory_space=pltpu.VMEM)),
        input_output_aliases={i: 2 + i for i in range(2 * n)},
        compiler_params=pltpu.CompilerParams(has_side_effects=SIDE_EFFECT),
    )(*in_hbm, after)
    return outs[0], outs[1], list(outs[2:2 + n]), list(outs[2 + n:2 + 2 * n]), outs[-1]


def _scatter_wait(send_sems, recv_sems, parts, lands, after, name):
    n = len(parts)

    def body(*refs):
        ins, lnd = refs[:n], refs[n:2 * n]
        for cp in _scatter_copies(ins, lnd, refs[2 * n], refs[2 * n + 1]):
            cp.wait_send()
            cp.wait_recv()

    outs = pl.pallas_call(
        body, name=name,
        out_shape=tuple(pltpu.HBM(a.shape, a.dtype) for a in parts + lands),
        in_specs=[HBM] * (2 * n) + [SEM, SEM, ANY],
        out_specs=tuple([HBM] * (2 * n)),
        input_output_aliases={i: i for i in range(2 * n)},
        compiler_params=pltpu.CompilerParams(has_side_effects=SIDE_EFFECT),
    )(*parts, *lands, send_sems, recv_sems, after)
    return list(outs[n:])


def _adam(w, g, m, v):
    m2 = ADAM_B1 * m + (1.0 - ADAM_B1) * g
    v2 = ADAM_B2 * v + (1.0 - ADAM_B2) * (g * g)
    m_hat = m2 / (1.0 - ADAM_B1 ** ADAM_STEP)
    v_hat = v2 / (1.0 - ADAM_B2 ** ADAM_STEP)
    delta = -ADAM_LR * (m_hat / (jnp.sqrt(v_hat) + ADAM_EPS) + ADAM_WD * w)
    return delta, m2, v2


def _small_allreduce_adam(part, w, m, v, name):
    rows = part.shape[0]

    def body(p_ref, w_ref, m_ref, v_ref, g_ref, d_ref, mo_ref, vo_ref, buf, send_sems, recv_sems):
        x, y, c = _place()
        buf[0] = p_ref[...]
        cps = []
        for k in range(1, N_DEV):
            kx, ky, kc = (k >> 2) & 1, (k >> 1) & 1, k & 1
            peer = (x ^ kx, y ^ ky, c ^ kc)
            cps.append(pltpu.make_async_remote_copy(
                src_ref=p_ref, dst_ref=buf.at[k], send_sem=send_sems.at[k - 1], recv_sem=recv_sems.at[k - 1],
                device_id=peer, device_id_type=MESH_ID))
        for cp in cps:
            cp.start()
        for cp in cps:
            cp.wait()
        me = 4 * x + 2 * y + c
        total = buf[me]
        for d in range(1, N_DEV):
            total = total + buf[d ^ me]
        g_ref[...] = total
        delta, m2, v2 = _adam(w_ref[...], total, m_ref[...], v_ref[...])
        d_ref[...] = delta
        mo_ref[...] = m2
        vo_ref[...] = v2

    vm = pl.BlockSpec(memory_space=pltpu.VMEM)
    return pl.pallas_call(
        body, name=name,
        out_shape=[jax.ShapeDtypeStruct(part.shape, F32)] * 4,
        in_specs=[vm] * 4, out_specs=[vm] * 4,
        scratch_shapes=[pltpu.VMEM((N_DEV, rows, LANES), F32),
                        pltpu.SemaphoreType.DMA((N_DEV - 1,)), pltpu.SemaphoreType.DMA((N_DEV - 1,))],
    )(part, w, m, v)


def _final_adam(g8, land, w, m, v, me, name):
    _, r, c = g8.shape
    tr = max(q for q in range(16, r + 1, 16) if r % q == 0 and q * c <= ADAM_TILE_ELEMS)

    def body(me_ref, g_ref, land_ref, w_ref, m_ref, v_ref, go_ref, d_ref, mo_ref, vo_ref):
        g = g_ref[...]
        for k in range(N_DEV - 1):
            g = g + land_ref[k].astype(F32)
        go_ref[...] = g
        delta, m2, v2 = _adam(w_ref[...], g, m_ref[...], v_ref[...])
        d_ref[...] = delta
        mo_ref[...] = m2
        vo_ref[...] = v2

    plain = pl.BlockSpec((tr, c), lambda i, s: (i, 0))
    return pl.pallas_call(
        body, name=name,
        out_shape=[jax.ShapeDtypeStruct((r, c), F32)] * 4,
        grid_spec=pltpu.PrefetchScalarGridSpec(
            num_scalar_prefetch=1, grid=(r // tr,),
            in_specs=[pl.BlockSpec((None, tr, c), lambda i, s: (s[0], i, 0)),
                      pl.BlockSpec((N_DEV - 1, tr, c), lambda i, s: (0, i, 0)),
                      plain, plain, plain],
            out_specs=[plain] * 4),
        compiler_params=_params(("arbitrary",)),
    )(me, g8, land, w, m, v)


def _rms(x, gain):
    r = lax.rsqrt(jnp.mean(x * x, axis=-1, keepdims=True) + EPS)
    xh = x * r
    return xh * gain, xh, r


def _rms_bwd(xh, r, gain, dy):
    gdy = gain * dy
    dx = r * (gdy - xh * jnp.mean(xh * gdy, axis=-1, keepdims=True))
    return dx, jnp.sum(dy * xh, axis=0, keepdims=True)


def _load_weights(pairs, sems):
    cps = [pltpu.make_async_copy(src, dst, sems.at[i]) for i, (src, dst) in enumerate(pairs)]
    for cp in cps:
        cp.start()
    for cp in cps:
        cp.wait()


def _ffn_fwd(h, gain, wgu, wd, name):
    t, d = h.shape
    nb, nf, _ = wgu.shape
    nh = nb // 2
    tm = _row_tile(t, 512)

    def body(h_ref, g_ref, wgu_hbm, wd_hbm, out_ref, gu_ref, wgu_v, wd_v, sems):
        @pl.when(pl.program_id(0) == 0)
        def _():
            _load_weights([(wgu_hbm, wgu_v), (wd_hbm, wd_v)], sems)

        x = h_ref[...]
        n, _, _ = _rms(x, g_ref[...])
        nbf = n.astype(BF16)
        acc = jnp.zeros((tm, d), F32)
        for j in range(nh):
            g = _dot_nt(nbf, wgu_v[j])
            u = _dot_nt(nbf, wgu_v[j + nh])
            gu_ref[j] = g.astype(BF16)
            gu_ref[j + nh] = u.astype(BF16)
            a = (g * jax.nn.sigmoid(g)) * u
            acc = acc + _dot(a.astype(BF16), wd_v[j])
        out_ref[...] = x + 0.5 * acc

    return pl.pallas_call(
        body, name=name, grid=(t // tm,),
        out_shape=[jax.ShapeDtypeStruct((t, d), F32), jax.ShapeDtypeStruct((nb, t, nf), BF16)],
        in_specs=[pl.BlockSpec((tm, d), lambda i: (i, 0)), pl.BlockSpec((1, d), lambda i: (0, 0)), ANY, ANY],
        out_specs=[pl.BlockSpec((tm, d), lambda i: (i, 0)), pl.BlockSpec((nb, tm, nf), lambda i: (0, i, 0))],
        scratch_shapes=[pltpu.VMEM(wgu.shape, BF16), pltpu.VMEM(wd.shape, BF16), pltpu.SemaphoreType.DMA((2,))],
        compiler_params=_params(("arbitrary",)),
    )(h, gain, wgu, wd)


def _ffn_bwd(dh, h, gain, gu, wgu, wd, name):
    t, d = h.shape
    nb, nf, _ = wgu.shape
    nh = nb // 2
    tm = _row_tile(t, 256)

    def body(dh_ref, h_ref, g_ref, gu_ref, wgu_hbm, wd_hbm, dhp_ref, dgu_ref, a_ref, n_ref, dgain_ref,
             wgu_v, wd_v, sems):
        @pl.when(pl.program_id(0) == 0)
        def _():
            _load_weights([(wgu_hbm, wgu_v), (wd_hbm, wd_v)], sems)
            dgain_ref[...] = jnp.zeros_like(dgain_ref)

        x = h_ref[...]
        gain_v = g_ref[...]
        n, xh, r = _rms(x, gain_v)
        n_ref[...] = n.astype(BF16)
        dh_v = dh_ref[...]
        dfb = (0.5 * dh_v).astype(BF16)
        dn = jnp.zeros((tm, d), F32)
        for j in range(nh):
            da = _dot_nt(dfb, wd_v[j])
            g = gu_ref[j].astype(F32)
            u = gu_ref[j + nh].astype(F32)
            sg = jax.nn.sigmoid(g)
            si = g * sg
            dg = (da * u * (sg * (1.0 + g * (1.0 - sg)))).astype(BF16)
            du = (da * si).astype(BF16)
            a_ref[j] = (si * u).astype(BF16)
            dgu_ref[j] = dg
            dgu_ref[j + nh] = du
            dn = dn + _dot(dg, wgu_v[j]) + _dot(du, wgu_v[j + nh])
        dx, dgain = _rms_bwd(xh, r, gain_v, dn)
        dhp_ref[...] = dh_v + dx
        dgain_ref[...] += dgain

    row = pl.BlockSpec((tm, d), lambda i: (i, 0))
    vec = pl.BlockSpec((1, d), lambda i: (0, 0))
    return pl.pallas_call(
        body, name=name, grid=(t // tm,),
        out_shape=[jax.ShapeDtypeStruct((t, d), F32), jax.ShapeDtypeStruct((nb, t, nf), BF16),
                   jax.ShapeDtypeStruct((nh, t, nf), BF16), jax.ShapeDtypeStruct((t, d), BF16),
                   jax.ShapeDtypeStruct((1, d), F32)],
        in_specs=[row, row, vec, pl.BlockSpec((nb, tm, nf), lambda i: (0, i, 0)), ANY, ANY],
        out_specs=[row, pl.BlockSpec((nb, tm, nf), lambda i: (0, i, 0)),
                   pl.BlockSpec((nh, tm, nf), lambda i: (0, i, 0)), row, vec],
        scratch_shapes=[pltpu.VMEM(wgu.shape, BF16), pltpu.VMEM(wd.shape, BF16), pltpu.SemaphoreType.DMA((2,))],
        compiler_params=_params(("arbitrary",)),
    )(dh, h, gain, gu, wgu, wd)


def _dw(xa, dy, nb, n, name, scale=1.0, dep=None):
    t, k = xa.shape[-2:]
    tt = _row_tile(t, 512)
    steps = t // tt
    wide = dy.ndim == 2 and xa.ndim == 2
    if xa.ndim == 3:
        x_spec = pl.BlockSpec((nb, tt, k), lambda i: (0, i, 0))
    else:
        x_spec = pl.BlockSpec((tt, k), lambda i: (i, 0))
    if dy.ndim == 3:
        dy_spec = pl.BlockSpec((nb, tt, n), lambda i: (0, i, 0))
    else:
        dy_spec = pl.BlockSpec((tt, dy.shape[1]), lambda i: (i, 0))
    acc_shape = (k, nb * n) if wide else (nb, k, n)
    stage_shape = (k, nb * n) if wide else (k, n)

    def body(x_ref, dy_ref, *rest):
        o_hbm, ob_hbm, acc, stage, sems = rest[-5:]

        @pl.when(pl.program_id(0) == 0)
        def _():
            acc[...] = jnp.zeros_like(acc)

        if wide:
            acc[...] += _dot(x_ref[...].astype(BF16).T, dy_ref[...].astype(BF16))
        elif xa.ndim == 2:
            xt = x_ref[...].astype(BF16).T
            for j in range(nb):
                acc[j] += _dot(xt, dy_ref[j].astype(BF16))
        else:
            dyb = dy_ref[...].astype(BF16)
            for j in range(nb):
                acc[j] += _dot_tn(x_ref[j].astype(BF16), dyb)

        @pl.when(pl.program_id(0) == steps - 1)
        def _():
            if scale != 1.0:
                acc[...] = acc[...] * scale
            if wide:
                cps = [pltpu.make_async_copy(acc.at[:, pl.ds(j * n, n)] if nb > 1 else acc, o_hbm.at[j], sems.at[j])
                       for j in range(nb)]
            else:
                cps = [pltpu.make_async_copy(acc, o_hbm, sems.at[0])]
            for cp in cps:
                cp.start()
            if wide:
                stage[...] = acc[...].astype(BF16)
                bcs = [pltpu.make_async_copy(stage.at[:, pl.ds(j * n, n)] if nb > 1 else stage, ob_hbm.at[j],
                                             sems.at[nb + j]) for j in range(nb)]
                for cp in bcs:
                    cp.start()
                for cp in bcs:
                    cp.wait()
            else:
                for j in range(nb):
                    stage[...] = acc[j].astype(BF16)
                    cp = pltpu.make_async_copy(stage, ob_hbm.at[j], sems.at[nb])
                    cp.start()
                    cp.wait()
            for cp in cps:
                cp.wait()

    return pl.pallas_call(
        body, name=name, grid=(steps,),
        out_shape=[jax.ShapeDtypeStruct((nb, k, n), F32), jax.ShapeDtypeStruct((nb, k, n), BF16)],
        in_specs=[x_spec, dy_spec] + ([] if dep is None else [ANY]),
        out_specs=[ANY, ANY],
        scratch_shapes=[pltpu.VMEM(acc_shape, F32), pltpu.VMEM(stage_shape, BF16),
                        pltpu.SemaphoreType.DMA((2 * nb,))],
        compiler_params=_params(("arbitrary",)),
    )(*((xa, dy) if dep is None else (xa, dy, dep)))


def _proj_fwd(h, gain, win, wgate, name):
    t, d = h.shape
    tm = _row_tile(t, 256)
    nq, ng = win.shape[0], wgate.shape[1]

    def body(h_ref, g_ref, win_ref, wg_ref, un_ref, qkv_ref, gate_ref):
        n, _, _ = _rms(h_ref[...], g_ref[...])
        nbf = n.astype(BF16)
        un_ref[...] = nbf
        qkv_ref[...] = _dot_nt(nbf, win_ref[...])
        gate_ref[...] = jax.nn.sigmoid(_dot(nbf, wg_ref[...]))

    full = lambda a: pl.BlockSpec(a.shape, lambda i: (0,) * a.ndim)
    return pl.pallas_call(
        body, name=name, grid=(t // tm,),
        out_shape=[jax.ShapeDtypeStruct((t, d), BF16), jax.ShapeDtypeStruct((t, nq), F32),
                   jax.ShapeDtypeStruct((t, ng), F32)],
        in_specs=[pl.BlockSpec((tm, d), lambda i: (i, 0)), full(gain), full(win), full(wgate)],
        out_specs=[pl.BlockSpec((tm, d), lambda i: (i, 0)), pl.BlockSpec((tm, nq), lambda i: (i, 0)),
                   pl.BlockSpec((tm, ng), lambda i: (i, 0))],
        compiler_params=_params(("arbitrary",)),
    )(h, gain, win, wgate)


def _proj_bwd(dh, h, gain, dzg, dqkv, win, wgate, name):
    t, d = h.shape
    tm = _row_tile(t, 256)
    nq, ng = win.shape[0], wgate.shape[1]

    def body(dh_ref, h_ref, g_ref, dzg_ref, dqkv_ref, win_ref, wg_ref, dhp_ref, dgain_ref):
        @pl.when(pl.program_id(0) == 0)
        def _():
            dgain_ref[...] = jnp.zeros_like(dgain_ref)

        gain_v = g_ref[...]
        _, xh, r = _rms(h_ref[...], gain_v)
        dun = _dot_nt(dzg_ref[...], wg_ref[...]) + _dot(dqkv_ref[...].astype(BF16), win_ref[...])
        dx, dgain = _rms_bwd(xh, r, gain_v, dun)
        dhp_ref[...] = dh_ref[...] + dx
        dgain_ref[...] += dgain

    full = lambda a: pl.BlockSpec(a.shape, lambda i: (0,) * a.ndim)
    row = pl.BlockSpec((tm, d), lambda i: (i, 0))
    return pl.pallas_call(
        body, name=name, grid=(t // tm,),
        out_shape=[jax.ShapeDtypeStruct((t, d), F32), jax.ShapeDtypeStruct((1, d), F32)],
        in_specs=[row, row, full(gain), pl.BlockSpec((tm, ng), lambda i: (i, 0)),
                  pl.BlockSpec((tm, nq), lambda i: (i, 0)), full(win), full(wgate)],
        out_specs=[row, pl.BlockSpec((1, d), lambda i: (0, 0))],
        compiler_params=_params(("arbitrary",)),
    )(dh, h, gain, dzg, dqkv, win, wgate)


def _merge_fwd(h, ya, yb, gate, wpa, wpb, wout, name):
    t, d = h.shape
    tm = _row_tile(t, 256)

    def body(h_ref, ya_ref, yb_ref, ga_ref, gb_ref, wpa_ref, wpb_ref, wout_ref, out_ref, mg_ref, pa_ref, pb_ref):
        pa = _dot(ya_ref[...].astype(BF16), wpa_ref[...])
        pb = _dot(yb_ref[...].astype(BF16), wpb_ref[...])
        merged = (ga_ref[...] * pa + gb_ref[...] * pb).astype(BF16)
        pa_ref[...] = pa.astype(BF16)
        pb_ref[...] = pb.astype(BF16)
        mg_ref[...] = merged
        out_ref[...] = h_ref[...] + _dot(merged, wout_ref[...])

    full = lambda a: pl.BlockSpec(a.shape, lambda i: (0,) * a.ndim)
    row = pl.BlockSpec((tm, d), lambda i: (i, 0))
    yrow = pl.BlockSpec((tm, ya.shape[1]), lambda i: (i, 0))
    return pl.pallas_call(
        body, name=name, grid=(t // tm,),
        out_shape=[jax.ShapeDtypeStruct((t, d), F32)] + [jax.ShapeDtypeStruct((t, d), BF16)] * 3,
        in_specs=[row, yrow, yrow, pl.BlockSpec((tm, d), lambda i: (i, 0)), pl.BlockSpec((tm, d), lambda i: (i, 1)),
                  full(wpa), full(wpb), full(wout)],
        out_specs=[row] * 4,
        compiler_params=_params(("arbitrary",)),
    )(h, ya, yb, gate, gate, wpa, wpb, wout)


def _merge_bwd(dh, pa, pb, gate, wpa, wpb, wout, name):
    t, d = dh.shape
    tm = _row_tile(t, 256)
    wy = wpa.shape[0]

    def body(dh_ref, pa_ref, pb_ref, ga_ref, gb_ref, wpa_ref, wpb_ref, wout_ref,
             dpa_ref, dpb_ref, dzg_ref, dya_ref, dyb_ref):
        dm = _dot_nt(dh_ref[...].astype(BF16), wout_ref[...])
        ga, gb = ga_ref[...], gb_ref[...]
        dpa = (dm * ga).astype(BF16)
        dpb = (dm * gb).astype(BF16)
        dpa_ref[...] = dpa
        dpb_ref[...] = dpb
        dzg_ref[:, :d] = (dm * pa_ref[...].astype(F32) * ga * (1.0 - ga)).astype(BF16)
        dzg_ref[:, d:] = (dm * pb_ref[...].astype(F32) * gb * (1.0 - gb)).astype(BF16)
        dya_ref[...] = _dot_nt(dpa, wpa_ref[...])
        dyb_ref[...] = _dot_nt(dpb, wpb_ref[...])

    full = lambda a: pl.BlockSpec(a.shape, lambda i: (0,) * a.ndim)
    row = pl.BlockSpec((tm, d), lambda i: (i, 0))
    yrow = pl.BlockSpec((tm, wy), lambda i: (i, 0))
    return pl.pallas_call(
        body, name=name, grid=(t // tm,),
        out_shape=[jax.ShapeDtypeStruct((t, d), BF16), jax.ShapeDtypeStruct((t, d), BF16),
                   jax.ShapeDtypeStruct((t, 2 * d), BF16), jax.ShapeDtypeStruct((t, wy), F32),
                   jax.ShapeDtypeStruct((t, wy), F32)],
        in_specs=[row, row, row, pl.BlockSpec((tm, d), lambda i: (i, 0)), pl.BlockSpec((tm, d), lambda i: (i, 1)),
                  full(wpa), full(wpb), full(wout)],
        out_specs=[row, row, pl.BlockSpec((tm, 2 * d), lambda i: (i, 0)), yrow, yrow],
        compiler_params=_params(("arbitrary",)),
    )(dh, pa, pb, gate, gate, wpa, wpb, wout)


def _ple_loss(h, gain, p, target, wpg, wpe, name):
    t, d = h.shape
    tm = _row_tile(t, 256)
    pd = p.shape[1]

    def body(h_ref, g_ref, p_ref, t_ref, wpg_ref, wpe_ref, dh_ref, dz_ref, dpp_ref, n_ref, dgain_ref, loss_ref):
        @pl.when(pl.program_id(0) == 0)
        def _():
            dgain_ref[...] = jnp.zeros_like(dgain_ref)
            loss_ref[...] = jnp.zeros_like(loss_ref)

        x = h_ref[...]
        gain_v = g_ref[...]
        n, xh, r = _rms(x, gain_v)
        nbf = n.astype(BF16)
        n_ref[...] = nbf
        pg = jax.nn.sigmoid(_dot(nbf, wpg_ref[...]))
        pp = _dot(p_ref[...].astype(BF16), wpe_ref[...])
        err = (x + pg * pp) - t_ref[...]
        loss_ref[...] += 0.5 * jnp.sum(jnp.mean(err * err, axis=-1, keepdims=True))
        dy = err * (1.0 / d)
        dpp_ref[...] = (dy * pg).astype(BF16)
        dz = (dy * pp * pg * (1.0 - pg)).astype(BF16)
        dz_ref[...] = dz
        dn = _dot_nt(dz, wpg_ref[...])
        dx, dgain = _rms_bwd(xh, r, gain_v, dn)
        dh_ref[...] = dy + dx
        dgain_ref[...] += dgain

    full = lambda a: pl.BlockSpec(a.shape, lambda i: (0,) * a.ndim)
    row = pl.BlockSpec((tm, d), lambda i: (i, 0))
    return pl.pallas_call(
        body, name=name, grid=(t // tm,),
        out_shape=[jax.ShapeDtypeStruct((t, d), F32), jax.ShapeDtypeStruct((t, d), BF16),
                   jax.ShapeDtypeStruct((t, d), BF16), jax.ShapeDtypeStruct((t, d), BF16),
                   jax.ShapeDtypeStruct((1, d), F32), jax.ShapeDtypeStruct((8, LANES), F32)],
        in_specs=[row, full(gain), pl.BlockSpec((tm, pd), lambda i: (i, 0)), row, full(wpg), full(wpe)],
        out_specs=[row, row, row, row, pl.BlockSpec((1, d), lambda i: (0, 0)),
                   pl.BlockSpec((8, LANES), lambda i: (0, 0))],
        compiler_params=_params(("arbitrary",)),
    )(h, gain, p, target, wpg, wpe)


def _head_masks():
    lane = lax.broadcasted_iota(jnp.int32, (1, LANES), 1)
    m0 = (lane < HEAD_DIM).astype(F32)
    return m0, 1.0 - m0


def _head_mean(v, m0, m1):
    del m0, m1
    r = lax.broadcasted_iota(jnp.int32, (LANES, LANES), 0) < HEAD_DIM
    c = lax.broadcasted_iota(jnp.int32, (LANES, LANES), 1) < HEAD_DIM
    same_head = (r == c).astype(BF16)
    hi = v.astype(BF16)
    lo = (v - hi.astype(F32)).astype(BF16)
    return (_dot(hi, same_head) + _dot(lo, same_head)) * (1.0 / HEAD_DIM)


def _head_norm(x, gain, m0, m1):
    r = lax.rsqrt(_head_mean(x * x, m0, m1) + EPS)
    xh = x * r
    return xh * gain, xh, r


def _head_norm_bwd(xh, r, gain, dy, m0, m1):
    gdy = gain * dy
    dx = r * (gdy - xh * _head_mean(xh * gdy, m0, m1))
    return dx, jnp.sum(dy * xh, axis=0, keepdims=True)


STACK = 2 * QTILE


def _stack_heads(mode, first_kv, x, m0, m1):
    a0, a1 = x * m0, x * m1
    if mode == "A":
        return a0, a1
    r0, r1 = pltpu.roll(a0, HEAD_DIM, 1), pltpu.roll(a1, HEAD_DIM, 1)
    return jnp.where(first_kv, a0, r0), jnp.where(first_kv, r1, a1)


def _unstack_heads(mode, first_kv, t0, t1, m0, m1):
    if mode == "B":
        t0 = jnp.where(first_kv, t0, pltpu.roll(t0, HEAD_DIM, 1))
        t1 = jnp.where(first_kv, pltpu.roll(t1, HEAD_DIM, 1), t1)
    return t0 * m0 + t1 * m1


def _store_stacked(dst, i, h0, h1):
    for half in range(2):
        rows = slice(half * QTILE, (half + 1) * QTILE)
        dst[pl.ds((2 * i + half) * STACK, QTILE), :] = h0[rows].astype(dst.dtype)
        dst[pl.ds((2 * i + half) * STACK + QTILE, QTILE), :] = h1[rows].astype(dst.dtype)


def _attn_prep(mode, pair, s_len, padk, q_ref, k_ref, v_ref, gq_ref, gk_ref, qs, k2, v2, do_ref=None, dos=None):
    m0, m1 = _head_masks()
    zpad = jnp.zeros((padk, LANES), BF16)
    k2[pl.ds(0, padk), :] = zpad
    v2[pl.ds(0, padk), :] = zpad
    first_kv = (pair // 2) == 0
    rt = 2 * QTILE
    for i in range(s_len // rt):
        rows = pl.ds(i * rt, rt)
        qn, _, _ = _head_norm(q_ref[rows, :], gq_ref[...], m0, m1)
        kn, _, _ = _head_norm(k_ref[rows, :], gk_ref[...], m0, m1)
        _store_stacked(qs, i, *_stack_heads(mode, first_kv, qn * (HEAD_DIM ** -0.5), m0, m1))
        if dos is not None:
            _store_stacked(dos, i, *_stack_heads(mode, first_kv, do_ref[rows, :], m0, m1))
        k2[pl.ds(padk + i * rt, rt), :] = kn.astype(BF16)
        v2[pl.ds(padk + i * rt, rt), :] = v_ref[rows, :].astype(BF16)


def _attn_probs(mode, q_st, kb, bias, ok, sink):
    s = _dot_nt(q_st, kb) + bias
    s = jnp.where(ok, s, NEG_INF)
    mx = jnp.max(s, axis=-1, keepdims=True)
    if mode == "B":
        mx = jnp.maximum(mx, sink)
    e = jnp.exp(s - mx)
    l = jnp.sum(e, axis=-1, keepdims=True)
    if mode == "B":
        l = l + jnp.exp(sink - mx)
    return e, mx, l


def _sink_column(sink_ref, pair):
    row = lax.broadcasted_iota(jnp.int32, (STACK, 1), 0)
    return jnp.where(row < QTILE, sink_ref[2 * pair], sink_ref[2 * pair + 1])


def _attn_cols(mode):
    if mode == "A":
        return (lambda b, p: (b, p)), (lambda b, p: (b, 4 + p)), (lambda b, p: (b, 8 + p))
    return (lambda b, p: (b, 12 + p)), (lambda b, p: (b, 16)), (lambda b, p: (b, 17))


def _attn_fwd(mode, qkv, gq, gk, bias, sinks, bl, s_len, name):
    bw = bias.shape[-1]
    padk = bw - QTILE
    nt = s_len // QTILE
    qmap, kmap, vmap = _attn_cols(mode)

    def body(q_ref, k_ref, v_ref, gq_ref, gk_ref, bias_ref, sink_ref, o_ref, qs, k2, v2):
        pair = pl.program_id(1)
        m0, m1 = _head_masks()
        first_kv = (pair // 2) == 0
        _attn_prep(mode, pair, s_len, padk, q_ref, k_ref, v_ref, gq_ref, gk_ref, qs, k2, v2)
        col = lax.broadcasted_iota(jnp.int32, (STACK, bw), 1)
        sink = _sink_column(sink_ref, pair)

        def tile(m, carry):
            r0 = pl.multiple_of(m * QTILE, QTILE)
            q_st = qs[pl.ds(pl.multiple_of(m * STACK, STACK), STACK), :]
            ok = col >= (padk - r0)
            e, _, l = _attn_probs(mode, q_st, k2[pl.ds(r0, bw), :], bias_ref[...], ok, sink)
            o_st = _dot(e.astype(BF16), v2[pl.ds(r0, bw), :]) / l
            o_ref[pl.ds(r0, QTILE), :] = _unstack_heads(mode, first_kv, o_st[:QTILE], o_st[QTILE:], m0, m1)
            return carry

        lax.fori_loop(0, nt, tile, 0, unroll=2)

    blk = lambda f: pl.BlockSpec((s_len, LANES), f)
    vec = pl.BlockSpec((1, LANES), lambda b, p: (0, 0))
    return pl.pallas_call(
        body, name=name, grid=(bl, 4),
        out_shape=jax.ShapeDtypeStruct((bl * s_len, 4 * LANES), F32),
        in_specs=[blk(qmap), blk(kmap), blk(vmap), vec, vec,
                  pl.BlockSpec((STACK, bw), lambda b, p: (p, 0)),
                  pl.BlockSpec(memory_space=pltpu.SMEM)],
        out_specs=pl.BlockSpec((s_len, LANES), lambda b, p: (b, p)),
        scratch_shapes=[pltpu.VMEM((2 * s_len, LANES), BF16)] + [pltpu.VMEM((s_len + padk, LANES), BF16)] * 2,
        compiler_params=_params(("arbitrary", "arbitrary")),
    )(qkv, qkv, qkv, gq, gk, bias.reshape(-1, bw), sinks)


def _attn_bwd(mode, qkv, gq, gk, bias, sinks, y, dy, bl, s_len, name):
    bw = bias.shape[-1]
    padk = bw - QTILE
    nt = s_len // QTILE
    qmap, kmap, vmap = _attn_cols(mode)
    t = bl * s_len
    kvw = 4 * LANES if mode == "A" else LANES

    def body(q_ref, k_ref, v_ref, gq_ref, gk_ref, bias_ref, sink_ref, y_ref, dy_ref,
             dq_ref, dk_ref, dv_ref, dgq_ref, dgk_ref, dbias_ref, dsink_ref,
             qs, k2, v2, dos, dqs, dk, dv):
        pair = pl.program_id(1)
        m0, m1 = _head_masks()
        first_kv = (pair // 2) == 0
        _attn_prep(mode, pair, s_len, padk, q_ref, k_ref, v_ref, gq_ref, gk_ref, qs, k2, v2, dy_ref, dos)
        dk[...] = jnp.zeros_like(dk)
        dv[...] = jnp.zeros_like(dv)
        dbias_ref[...] = jnp.zeros_like(dbias_ref)
        col = lax.broadcasted_iota(jnp.int32, (STACK, bw), 1)
        lane8 = lax.broadcasted_iota(jnp.int32, (8, LANES), 1)
        sink = _sink_column(sink_ref, pair)

        def tile(m, dsink):
            r0 = pl.multiple_of(m * QTILE, QTILE)
            rows = pl.ds(r0, QTILE)
            stack = pl.ds(pl.multiple_of(m * STACK, STACK), STACK)
            band = pl.ds(r0, bw)
            q_st = qs[stack, :]
            do_st = dos[stack, :]
            dd = dy_ref[rows, :] * y_ref[rows, :]
            delta = jnp.concatenate([jnp.sum(dd * m0, axis=-1, keepdims=True),
                                     jnp.sum(dd * m1, axis=-1, keepdims=True)], axis=0)
            ok = col >= (padk - r0)
            kb = k2[band, :]
            e, mx, l = _attn_probs(mode, q_st, kb, bias_ref[...], ok, sink)
            inv = 1.0 / l
            pn = e * inv
            dp = _dot_nt(do_st, v2[band, :])
            ds = pn * (dp - delta)
            if mode == "A":
                dbias_ref[...] += ds
            else:
                part = jnp.exp(sink - mx) * inv * delta
                dsink = dsink - (jnp.where(lane8 == 0, jnp.sum(part[:QTILE]), 0.0)
                                 + jnp.where(lane8 == 1, jnp.sum(part[QTILE:]), 0.0))
            dsb = ds.astype(BF16)
            dv[band, :] += _dot_tn(pn.astype(BF16), do_st)
            dk[band, :] += _dot_tn(dsb, q_st)
            dqs[stack, :] = _dot(dsb, kb)
            return dsink

        dsink = lax.fori_loop(0, nt, tile, jnp.zeros((8, LANES), F32), unroll=2)
        dsink_ref[...] = dsink

        rt = 2 * QTILE
        dgq = jnp.zeros((1, LANES), F32)
        dgk = jnp.zeros((1, LANES), F32)
        for i in range(s_len // rt):
            rows = pl.ds(i * rt, rt)
            src = pl.ds(padk + i * rt, rt)
            gq_v, gk_v = gq_ref[...], gk_ref[...]
            _, qh, qr = _head_norm(q_ref[rows, :], gq_v, m0, m1)
            _, kh, kr = _head_norm(k_ref[rows, :], gk_v, m0, m1)
            dqn = jnp.concatenate(
                [_unstack_heads(mode, first_kv, dqs[pl.ds((2 * i + half) * STACK, QTILE), :],
                                dqs[pl.ds((2 * i + half) * STACK + QTILE, QTILE), :], m0, m1) for half in range(2)],
                axis=0) * (HEAD_DIM ** -0.5)
            dq_raw, dgq_i = _head_norm_bwd(qh, qr, gq_v, dqn, m0, m1)
            dk_raw, dgk_i = _head_norm_bwd(kh, kr, gk_v, dk[src, :], m0, m1)
            dvn = dv[src, :]
            dq_ref[rows, :] = dq_raw
            if mode == "A":
                dk_ref[rows, :] = dk_raw
                dv_ref[rows, :] = dvn
            else:
                @pl.when(pair == 0)
                def _():
                    dk_ref[rows, :] = dk_raw
                    dv_ref[rows, :] = dvn

                @pl.when(pair != 0)
                def _():
                    dk_ref[rows, :] += dk_raw
                    dv_ref[rows, :] += dvn
            dgq, dgk = dgq + dgq_i, dgk + dgk_i
        dgq_ref[...] = jnp.broadcast_to(dgq, (8, LANES))
        dgk_ref[...] = jnp.broadcast_to(dgk, (8, LANES))

    blk = lambda f: pl.BlockSpec((s_len, LANES), f)
    vec = pl.BlockSpec((1, LANES), lambda b, p: (0, 0))
    small = pl.BlockSpec((None, None, 8, LANES), lambda b, p: (b, p, 0, 0))
    kvmap = (lambda b, p: (b, p)) if mode == "A" else (lambda b, p: (b, 0))
    pad_f32 = pltpu.VMEM((s_len + padk, LANES), F32)
    pad_bf = pltpu.VMEM((s_len + padk, LANES), BF16)
    stack_bf = pltpu.VMEM((2 * s_len, LANES), BF16)
    outs = pl.pallas_call(
        body, name=name, grid=(bl, 4),
        out_shape=[jax.ShapeDtypeStruct((t, 4 * LANES), F32), jax.ShapeDtypeStruct((t, kvw), F32),
                   jax.ShapeDtypeStruct((t, kvw), F32),
                   jax.ShapeDtypeStruct((bl, 4, 8, LANES), F32), jax.ShapeDtypeStruct((bl, 4, 8, LANES), F32),
                   jax.ShapeDtypeStruct((bl, 4 * STACK, bw), F32), jax.ShapeDtypeStruct((bl, 4, 8, LANES), F32)],
        in_specs=[blk(qmap), blk(kmap), blk(vmap), vec, vec,
                  pl.BlockSpec((STACK, bw), lambda b, p: (p, 0)),
                  pl.BlockSpec(memory_space=pltpu.SMEM),
                  blk(lambda b, p: (b, p)), blk(lambda b, p: (b, p))],
        out_specs=[blk(lambda b, p: (b, p)), blk(kvmap), blk(kvmap), small, small,
                   pl.BlockSpec((None, STACK, bw), lambda b, p: (b, p, 0)), small],
        scratch_shapes=[stack_bf, pad_bf, pad_bf, stack_bf, pltpu.VMEM((2 * s_len, LANES), F32), pad_f32, pad_f32],
        compiler_params=_params(("arbitrary", "arbitrary")),
    )(qkv, qkv, qkv, gq, gk, bias.reshape(-1, bw), sinks, y, dy)
    outs = list(outs)
    outs[5] = outs[5].reshape(bl, 2 * 4, QTILE, bw)
    return outs


def _band_geometry(prev):
    bw = QTILE + prev * CHUNK
    i = np.arange(QTILE)[:, None]
    j = np.arange(bw)[None, :]
    dist = i + prev * CHUNK - j
    valid = (j // CHUNK >= i // CHUNK) & (j // CHUNK <= i // CHUNK + prev)
    return dist, valid


A_VAR0 = (A_PREV * CHUNK - A_MAX_REL) // LANES * LANES


A_NVAR = QTILE + A_PREV * CHUNK - A_VAR0


def _skew_rows(x, sign):
    rows, n = x.shape
    row = lax.broadcasted_iota(jnp.int32, x.shape, 0)
    b = 1
    while b < rows:
        x = jnp.where((row & b) != 0, pltpu.roll(x, (sign * b) % n, 1), x)
        b *= 2
    return x


def _rel_bias_expand(table, name):
    _, valid = _band_geometry(A_PREV)
    bw = valid.shape[1]
    valid_f = jnp.asarray(valid.astype(np.float32))
    rev = jnp.flip(table[:, 1:], axis=1).reshape(A_HEADS, 1, A_NVAR)

    def body(rev_ref, valid_ref, o_ref):
        rowv = jnp.broadcast_to(rev_ref[...], (QTILE, A_NVAR))
        top = rowv[:, 0:1]
        var = _skew_rows(rowv, 1)
        row = lax.broadcasted_iota(jnp.int32, (QTILE, A_NVAR), 0)
        colv = lax.broadcasted_iota(jnp.int32, (QTILE, A_NVAR), 1)
        var = jnp.where(colv < row, top, var)
        ok = valid_ref[...] > 0.5
        o_ref[:, :A_VAR0] = jnp.where(ok[:, :A_VAR0], top, NEG_INF)
        o_ref[:, A_VAR0:] = jnp.where(ok[:, A_VAR0:], var, NEG_INF)

    return pl.pallas_call(
        body, name=name, grid=(A_HEADS,),
        out_shape=jax.ShapeDtypeStruct((A_HEADS, QTILE, bw), F32),
        in_specs=[pl.BlockSpec((None, 1, A_NVAR), lambda h: (h, 0, 0)), pl.BlockSpec((QTILE, bw), lambda h: (0, 0))],
        out_specs=pl.BlockSpec((None, QTILE, bw), lambda h: (h, 0, 0)),
        compiler_params=_params(("arbitrary",)),
    )(rev, valid_f)


def _rel_bias_grad(dbias, name):
    bl = dbias.shape[0]
    bw = dbias.shape[-1]

    def body(db_ref, o_ref):
        g = db_ref[0]
        for b in range(1, bl):
            g = g + db_ref[b]
        sk = _skew_rows(g[:, A_VAR0:], -1)
        row = lax.broadcasted_iota(jnp.int32, (QTILE, A_NVAR), 0)
        colv = lax.broadcasted_iota(jnp.int32, (QTILE, A_NVAR), 1)
        wrapped = (row + colv) >= A_NVAR
        main = jnp.sum(jnp.where(wrapped, 0.0, sk), axis=0, keepdims=True)
        top = jnp.sum(g[:, :A_VAR0]) + jnp.sum(jnp.where(wrapped, sk, 0.0))
        o_ref[:, :A_NVAR] = jnp.broadcast_to(main, (8, A_NVAR))
        o_ref[:, A_NVAR:] = jnp.full((8, LANES), top, F32)

    out = pl.pallas_call(
        body, name=name, grid=(A_HEADS,),
        out_shape=jax.ShapeDtypeStruct((A_HEADS, 8, A_NVAR + LANES), F32),
        in_specs=[pl.BlockSpec((bl, None, QTILE, bw), lambda h: (0, h, 0, 0))],
        out_specs=pl.BlockSpec((None, 8, A_NVAR + LANES), lambda h: (h, 0, 0)),
        compiler_params=_params(("arbitrary",)),
    )(dbias)
    main, top = out[:, 0, :A_NVAR], out[:, 0, A_NVAR]
    fm = jnp.flip(main, axis=1)
    return jnp.concatenate([jnp.zeros((A_HEADS, 1), F32), fm[:, :-1], fm[:, -1:] + top[:, None]], axis=1)


def _alibi_bias():
    dist, valid = _band_geometry(B_PREV)
    slopes = np.array([2.0 ** (-8.0 * (h + 1) / B_Q_HEADS) for h in range(B_Q_HEADS)], dtype=np.float32)
    bias = -slopes[:, None, None] * np.abs(dist).astype(np.float32)[None]
    return jnp.asarray(np.where(valid[None], bias, np.float32(NEG_INF)).astype(np.float32))


SMALL_NAMES = ("ffn1_norm", "mix_norm", "ffn2_norm", "ple_norm", "a_q_norm", "a_k_norm", "b_q_norm", "b_k_norm",
               "a_rel_bias", "b_sinks", "loss")


def _pack_small(vals):
    rows = []
    for nme in SMALL_NAMES:
        v = vals[nme].astype(F32)
        if nme == "a_rel_bias":
            v = jnp.pad(v.reshape(A_HEADS, -1), ((0, 0), (0, 3 * LANES - (2 * A_MAX_REL + 1))))
        v = v.reshape(-1)
        v = jnp.pad(v, (0, (-v.shape[0]) % LANES))
        rows.append(v.reshape(-1, LANES))
    out = jnp.concatenate(rows, axis=0)
    return jnp.pad(out, ((0, (-out.shape[0]) % 8), (0, 0)))


def _unpack_small(packed, shapes):
    out, r = {}, 0
    for nme in SMALL_NAMES:
        shp = shapes[nme]
        if nme == "a_rel_bias":
            nr = A_HEADS * 3
            out[nme] = packed[r:r + nr].reshape(A_HEADS, 3 * LANES)[:, :2 * A_MAX_REL + 1].reshape(shp)
        else:
            size = int(np.prod(shp)) if shp else 1
            nr = -(-size // LANES)
            out[nme] = packed[r:r + nr].reshape(-1)[:size].reshape(shp)
        r += nr
    return out


BIG_NAMES = ("ffn1_w_gu", "ffn1_w_down", "w_in", "w_gate", "w_proj_a", "w_proj_b", "w_out",
             "ffn2_w_gu", "ffn2_w_down", "w_ple_gate", "w_ple_proj")
ROW_SHARDED = ("ffn1_w_down", "ffn2_w_down", "w_out", "w_ple_gate")
WEIGHT_ORDER = ("ffn1_norm", "ffn1_w_gu", "ffn1_w_down", "mix_norm", "w_in", "a_q_norm", "a_k_norm", "a_rel_bias",
                "b_q_norm", "b_k_norm", "b_sinks", "w_gate", "w_proj_a", "w_proj_b", "w_out", "ffn2_norm",
                "ffn2_w_gu", "ffn2_w_down", "ple_norm", "w_ple_gate", "w_ple_proj")


TRANSPOSED = ("ffn1_w_gu", "ffn2_w_gu", "w_in")


def _local(a, nme):
    return a[0].T if nme in TRANSPOSED else a[0]


def _full_cols(wg):
    nb, k, n = wg.shape
    return jnp.transpose(wg, (1, 0, 2)).reshape(k, nb * n)


def _col_blocks(g, nb):
    k, n = g.shape
    return jnp.transpose(g.reshape(k, nb, n // nb), (1, 0, 2))


def _step(x, p, target, w, m, v):
    bl, s_len, d = x.shape
    t = bl * s_len
    h0 = x.reshape(t, d)
    pt = p.reshape(t, p.shape[-1])
    tgt = target.reshape(t, d)

    g_ffn1, g_mix, g_ffn2, g_ple = w["ffn1_norm"], w["mix_norm"], w["ffn2_norm"], w["ple_norm"]
    tile2 = lambda a: jnp.tile(a.reshape(1, HEAD_DIM), (1, 2))
    gqa, gka, gqb, gkb = tile2(w["a_q_norm"]), tile2(w["a_k_norm"]), tile2(w["b_q_norm"]), tile2(w["b_k_norm"])
    sinks = w["b_sinks"].reshape(B_Q_HEADS)
    bias_a = _rel_bias_expand(w["a_rel_bias"][0], "rel_bias_expand")
    bias_b = _alibi_bias()

    shard = {nme: _local(w[nme], nme).astype(BF16) for nme in BIG_NAMES}
    wgu1, wd1 = _all_gather([shard["ffn1_w_gu"], shard["ffn1_w_down"]], "weights_gather_ffn1")
    nf = wgu1.shape[1]
    wd1 = wd1.reshape(N_DEV // 2, nf, d)
    mixer_names = ("w_in", "w_gate")
    rest_names = ("w_proj_a", "w_proj_b", "w_out", "ffn2_w_gu", "ffn2_w_down", "w_ple_gate", "w_ple_proj")
    send1, recv1, bufs, token = _gather_start([shard[nme] for nme in mixer_names], wgu1, "gather_start_mixer")

    h1, gu1 = _ffn_fwd(h0, g_ffn1 + token[0, 0], wgu1, wd1, "ffn1_fwd")
    send2, recv2, bufs, token = _gather_pass(send1, recv1, bufs, h1, "gather_pass_mixer")
    send1, recv1, rest_bufs, token = _gather_start([shard[nme] for nme in rest_names], token, "gather_start_rest")
    win, wgate = _gather_wait(send2, recv2, bufs, token, "gather_wait_mixer")
    win, wgate = win.reshape(IN_COLS, d), _full_cols(wgate)
    un, qkv, gate = _proj_fwd(h1, g_mix, win, wgate, "proj_fwd")
    ya = _attn_fwd("A", qkv, gqa, gka, bias_a, sinks, bl, s_len, "attn_a_fwd")
    yb = _attn_fwd("B", qkv, gqb, gkb, bias_b, sinks, bl, s_len, "attn_b_fwd")
    send2, recv2, rest_bufs, token = _gather_pass(send1, recv1, rest_bufs, yb, "gather_pass_rest")
    gathered = dict(zip(rest_names, _gather_wait(send2, recv2, rest_bufs, token, "gather_wait_rest")))
    wgu2 = gathered["ffn2_w_gu"]
    wd2 = gathered["ffn2_w_down"].reshape(N_DEV // 2, nf, d)
    wpa = _full_cols(gathered["w_proj_a"])
    wpb = _full_cols(gathered["w_proj_b"])
    wpe = _full_cols(gathered["w_ple_proj"])
    wout = gathered["w_out"].reshape(d, d)
    wpg = gathered["w_ple_gate"].reshape(d, d)
    h2, merged, pa, pb = _merge_fwd(h1, ya, yb, gate, wpa, wpb, wout, "merge_fwd")
    h3, gu2 = _ffn_fwd(h2, g_ffn2, wgu2, wd2, "ffn2_fwd")
    dh3, dz4, dpp, n4, dg_ple, loss_part = _ple_loss(h3, g_ple, pt, tgt, wpg, wpe, "ple_loss")

    xi, yi, ci = _place()
    me = jnp.stack([4 * xi + 2 * yi + ci]).astype(jnp.int32)
    g32, g16, big = {}, {}, {}

    def keep(nme, pair, rows=None):
        for store, g in zip((g32, g16), pair):
            store[nme] = g if rows is None else g.reshape(N_DEV, rows, d)

    def start(names, after, tag):
        send, recv, parts, lands, token = _scatter_start([g16[nme] for nme in names], after, "grads_start_" + tag)
        return names, send, recv, parts, lands, token

    def finish(state, after, tag):
        names, send, recv, parts, lands, _ = state
        lands = _scatter_wait(send, recv, parts, lands, after, "grads_wait_" + tag)
        return names, lands

    def adam(done):
        for nme, land in zip(*done):
            outs = _final_adam(g32[nme], land, _local(w[nme], nme), _local(m[nme], nme), _local(v[nme], nme), me, "adam_" + nme)
            big[nme] = [(o.T if nme in TRANSPOSED else o)[None] for o in outs]

    keep("w_ple_gate", _dw(n4, dz4, 1, d, "dw_ple_gate"), d // N_DEV)
    keep("w_ple_proj", _dw(pt, dpp, N_DEV, d // N_DEV, "dw_ple_proj"))

    dh2, dgu2, a2, n3, dg_ffn2 = _ffn_bwd(dh3, h2, g_ffn2, gu2, wgu2, wd2, "ffn2_bwd")
    keep("ffn2_w_gu", _dw(dgu2, n3, N_DEV, d, "dw_ffn2_gu"))
    keep("ffn2_w_down", _dw(a2, dh3, N_DEV // 2, d, "dw_ffn2_down", 0.5), nf // 2)
    flight = start(("w_ple_gate", "w_ple_proj", "ffn2_w_gu", "ffn2_w_down"), dh2, "ffn2")

    dpa, dpb, dzg, dya, dyb = _merge_bwd(dh2, pa, pb, gate, wpa, wpb, wout, "merge_bwd")
    keep("w_out", _dw(merged, dh2, 1, d, "dw_out"), d // N_DEV)
    keep("w_proj_a", _dw(ya, dpa, N_DEV, d // N_DEV, "dw_proj_a"))
    keep("w_proj_b", _dw(yb, dpb, N_DEV, d // N_DEV, "dw_proj_b"))
    keep("w_gate", _dw(un, dzg, N_DEV, 2 * d // N_DEV, "dw_gate"))

    tok = flight[-1][0, 0]
    dqa, dka, dva, dgqa, dgka, dbias, _ = _attn_bwd("A", qkv, gqa + tok, gka, bias_a, sinks, ya, dya, bl, s_len,
                                                     "attn_a_bwd")
    dqb, dkb, dvb, dgqb, dgkb, _, dsink = _attn_bwd("B", qkv, gqb, gkb, bias_b, sinks, yb, dyb, bl, s_len, "attn_b_bwd")
    dqkv = jnp.concatenate([dqa, dka, dva, dqb, dkb, dvb], axis=1)
    dtab = _rel_bias_grad(dbias, "rel_bias_grad")

    dh1, dg_mix = _proj_bwd(dh2, h1, g_mix, dzg, dqkv, win, wgate, "proj_bwd")
    keep("w_in", _dw(dqkv, un, 1, d, "dw_in"), IN_COLS // N_DEV)
    done = finish(flight, g32["w_in"], "ffn2")
    flight = start(("w_out", "w_proj_a", "w_proj_b", "w_gate", "w_in"), done[1][0], "mixer")
    adam(done)

    dh0, dgu1, a1, n1, dg_ffn1 = _ffn_bwd(dh1, h0, g_ffn1 + flight[-1][0, 0], gu1, wgu1, wd1, "ffn1_bwd")
    keep("ffn1_w_down", _dw(a1, dh1, N_DEV // 2, d, "dw_ffn1_down", 0.5), nf // 2)
    done = finish(flight, g32["ffn1_w_down"], "mixer")
    flight = start(("ffn1_w_down",), done[1][0], "ffn1_down")
    adam(done)

    keep("ffn1_w_gu", _dw(dgu1, n1, N_DEV, d, "dw_ffn1_gu", dep=flight[-1]))
    done = finish(flight, g32["ffn1_w_gu"], "ffn1_down")
    flight = start(("ffn1_w_gu",), done[1][0], "ffn1_gu")
    adam(done)
    smalls = (dg_ffn1, dg_mix, dg_ffn2, dg_ple + flight[-1][0, 0], dgqa, dgka, dgqb, dgkb, dtab, dsink)
    return dh0, loss_part, big, smalls, flight, finish, adam


def kernel(x, p, ffn1_norm, ffn1_w_gu, ffn1_w_down, mix_norm, w_in, a_q_norm, a_k_norm, a_rel_bias, b_q_norm, b_k_norm, b_sinks, w_gate, w_proj_a, w_proj_b, w_out, ffn2_norm, ffn2_w_gu, ffn2_w_down, ple_norm, w_ple_gate, w_ple_proj, loss_target, m_ffn1_norm, m_ffn1_w_gu, m_ffn1_w_down, m_mix_norm, m_w_in, m_a_q_norm, m_a_k_norm, m_a_rel_bias, m_b_q_norm, m_b_k_norm, m_b_sinks, m_w_gate, m_w_proj_a, m_w_proj_b, m_w_out, m_ffn2_norm, m_ffn2_w_gu, m_ffn2_w_down, m_ple_norm, m_w_ple_gate, m_w_ple_proj, v_ffn1_norm, v_ffn1_w_gu, v_ffn1_w_down, v_mix_norm, v_w_in, v_a_q_norm, v_a_k_norm, v_a_rel_bias, v_b_q_norm, v_b_k_norm, v_b_sinks, v_w_gate, v_w_proj_a, v_w_proj_b, v_w_out, v_ffn2_norm, v_ffn2_w_gu, v_ffn2_w_down, v_ple_norm, v_w_ple_gate, v_w_ple_proj):
    w = dict(ffn1_norm=ffn1_norm, ffn1_w_gu=ffn1_w_gu, ffn1_w_down=ffn1_w_down, mix_norm=mix_norm, w_in=w_in,
             a_q_norm=a_q_norm, a_k_norm=a_k_norm, a_rel_bias=a_rel_bias, b_q_norm=b_q_norm, b_k_norm=b_k_norm,
             b_sinks=b_sinks, w_gate=w_gate, w_proj_a=w_proj_a, w_proj_b=w_proj_b, w_out=w_out, ffn2_norm=ffn2_norm,
             ffn2_w_gu=ffn2_w_gu, ffn2_w_down=ffn2_w_down, ple_norm=ple_norm, w_ple_gate=w_ple_gate,
             w_ple_proj=w_ple_proj)
    m = dict(ffn1_norm=m_ffn1_norm, ffn1_w_gu=m_ffn1_w_gu, ffn1_w_down=m_ffn1_w_down, mix_norm=m_mix_norm,
             w_in=m_w_in, a_q_norm=m_a_q_norm, a_k_norm=m_a_k_norm, a_rel_bias=m_a_rel_bias, b_q_norm=m_b_q_norm,
             b_k_norm=m_b_k_norm, b_sinks=m_b_sinks, w_gate=m_w_gate, w_proj_a=m_w_proj_a, w_proj_b=m_w_proj_b,
             w_out=m_w_out, ffn2_norm=m_ffn2_norm, ffn2_w_gu=m_ffn2_w_gu, ffn2_w_down=m_ffn2_w_down,
             ple_norm=m_ple_norm, w_ple_gate=m_w_ple_gate, w_ple_proj=m_w_ple_proj)
    v = dict(ffn1_norm=v_ffn1_norm, ffn1_w_gu=v_ffn1_w_gu, ffn1_w_down=v_ffn1_w_down, mix_norm=v_mix_norm,
             w_in=v_w_in, a_q_norm=v_a_q_norm, a_k_norm=v_a_k_norm, a_rel_bias=v_a_rel_bias, b_q_norm=v_b_q_norm,
             b_k_norm=v_b_k_norm, b_sinks=v_b_sinks, w_gate=v_w_gate, w_proj_a=v_w_proj_a, w_proj_b=v_w_proj_b,
             w_out=v_w_out, ffn2_norm=v_ffn2_norm, ffn2_w_gu=v_ffn2_w_gu, ffn2_w_down=v_ffn2_w_down,
             ple_norm=v_ple_norm, w_ple_gate=v_w_ple_gate, w_ple_proj=v_w_ple_proj)
    bl, s_len, d = x.shape

    dh0, loss_part, big, smalls, flight, finish, adam = _step(x, p[0], loss_target, w, m, v)
    dg_ffn1, dg_mix, dg_ffn2, dg_ple, dgqa, dgka, dgqb, dgkb, dtab, dsink = smalls

    fold = lambda a: (a[:, :, 0, :HEAD_DIM] + a[:, :, 0, HEAD_DIM:]).sum(axis=(0, 1))
    small_part = dict(
        ffn1_norm=dg_ffn1, mix_norm=dg_mix, ffn2_norm=dg_ffn2, ple_norm=dg_ple,
        a_q_norm=fold(dgqa), a_k_norm=fold(dgka), b_q_norm=fold(dgqb), b_k_norm=fold(dgkb),
        a_rel_bias=dtab,
        b_sinks=dsink.sum(axis=0)[:, 0, :2].reshape(B_Q_HEADS),
        loss=loss_part[0, :1])
    zero1 = jnp.zeros((1,), F32)
    shapes = {nme: w[nme].shape for nme in SMALL_NAMES if nme != "loss"}
    shapes["loss"] = ()
    pk = lambda src: _pack_small({**{nme: src[nme] for nme in SMALL_NAMES if nme != "loss"}, "loss": zero1})
    sg, sd, sm, sv = _small_allreduce_adam(_pack_small(small_part), pk(w), pk(m), pk(v), "small_allreduce_adam")
    adam(finish(flight, sg, "ffn1_gu"))
    sg, sd, sm, sv = (_unpack_small(a, shapes) for a in (sg, sd, sm, sv))

    def pick(i):
        out = []
        for nme in WEIGHT_ORDER:
            out.append(big[nme][i] if nme in big else (sg, sd, sm, sv)[i][nme])
        return out

    return (sg["loss"], dh0.reshape(bl, s_len, d), *pick(0), *pick(1), *pick(2), *pick(3))
```

```python
import functools

import jax
import jax.numpy as jnp
import numpy as np
from jax import lax
from jax.experimental import pallas as pl
from jax.experimental.pallas import tpu as pltpu

F32 = jnp.float32
BF16 = jnp.bfloat16

CHUNK = 64
HEAD_DIM = 64
A_HEADS = 8
A_PREV = 8
A_MAX_REL = 128
B_Q_HEADS = 8
B_KV_HEADS = 2
B_PREV = 2
A_WIDTH = A_HEADS * HEAD_DIM
B_Q_WIDTH = B_Q_HEADS * HEAD_DIM
B_KV_WIDTH = B_KV_HEADS * HEAD_DIM
IN_COLS = 3 * A_WIDTH + B_Q_WIDTH + 2 * B_KV_WIDTH
EPS = 1e-6
NEG_INF = -1e30
ADAM_LR = 0.001
ADAM_B1 = 0.9
ADAM_B2 = 0.999
ADAM_EPS = 1e-08
ADAM_WD = 0.01
ADAM_STEP = 10

N_DEV = 8
LANES = 128
QTILE = 2 * CHUNK
VMEM_LIMIT = 56 * 1024 * 1024
ADAM_TILE_ELEMS = 256 * 1024

MESH_ID = pl.DeviceIdType.MESH
ANY = pl.BlockSpec(memory_space=pl.ANY)
HBM = pl.BlockSpec(memory_space=pltpu.HBM)
SEM = pl.BlockSpec(memory_space=pltpu.SEMAPHORE)
SIDE_EFFECT = pltpu.SideEffectType.DATAFLOW_SIDE_EFFECTING


def _dot(a, b):
    return jnp.dot(a, b, preferred_element_type=F32)


def _dot_nt(a, b):
    return lax.dot_general(a, b, (((1,), (1,)), ((), ())), preferred_element_type=F32)


def _dot_tn(a, b):
    return lax.dot_general(a, b, (((0,), (0,)), ((), ())), preferred_element_type=F32)


def _params(sem=None, vmem=VMEM_LIMIT):
    return pltpu.CompilerParams(dimension_semantics=sem, vmem_limit_bytes=vmem)


def _row_tile(t, want):
    while t % want:
        want //= 2
    return want


def _place():
    return lax.axis_index("x"), lax.axis_index("y"), lax.axis_index("c")


def _all_gather(shards, name):
    n = len(shards)

    def body(*refs):
        ins, outs = refs[:n], refs[n:2 * n]
        send_sems, recv_sems, local_sems = refs[2 * n:]
        x, y, c = _place()
        me, sib = (x, y, c), (x, y, 1 - c)
        chips = [(1 - x, y), (x, 1 - y), (1 - x, 1 - y)]

        def copy(w, k, block, to, src=None):
            px, py, pc = block
            dst = outs[w].at[4 * px + 2 * py + pc]
            return pltpu.make_async_remote_copy(
                src_ref=dst if src is None else src, dst_ref=dst,
                send_sem=send_sems.at[w * 7 + k], recv_sem=recv_sems.at[w * 7 + k],
                device_id=to, device_id_type=MESH_ID)

        mine = [pltpu.make_async_copy(ins[w], outs[w].at[4 * x + 2 * y + c], local_sems.at[w]) for w in range(n)]
        for cp in mine:
            cp.start()
        first = []
        for w in range(n):
            first.append(copy(w, 0, me, sib, src=ins[w]))
            first += [copy(w, 1 + j, me, (*chip, c), src=ins[w]) for j, chip in enumerate(chips)]
        for cp in first:
            cp.start()
        passed = []
        for j, chip in enumerate(chips):
            for w in range(n):
                copy(w, 1 + j, (*chip, c), me).wait_recv()
                fwd = copy(w, 4 + j, (*chip, c), sib)
                fwd.start()
                passed.append(fwd)
        for w in range(n):
            copy(w, 0, sib, me).wait_recv()
        for j, chip in enumerate(chips):
            for w in range(n):
                copy(w, 4 + j, (*chip, 1 - c), me).wait_recv()
        for cp in first + passed:
            cp.wait_send()
        for cp in mine:
            cp.wait()

    return pl.pallas_call(
        body, name=name,
        out_shape=[jax.ShapeDtypeStruct((N_DEV,) + s.shape, s.dtype) for s in shards],
        in_specs=[ANY] * n, out_specs=[ANY] * n,
        scratch_shapes=[pltpu.SemaphoreType.DMA((7 * n,)), pltpu.SemaphoreType.DMA((7 * n,)),
                        pltpu.SemaphoreType.DMA((n,))],
    )(*shards)


def _gather_level(bufs, send_sems, recv_sems, level):
    x, y, c = _place()
    me, sib = (x, y, c), (x, y, 1 - c)
    chips = [(1 - x, y), (x, 1 - y), (1 - x, 1 - y)]

    def copy(w, k, block, to):
        px, py, pc = block
        rows = bufs[w].at[4 * px + 2 * py + pc]
        return pltpu.make_async_remote_copy(src_ref=rows, dst_ref=rows, send_sem=send_sems.at[k], recv_sem=recv_sems.at[k],
                                            device_id=to, device_id_type=MESH_ID)

    out, arriving = [], []
    for w in range(len(bufs)):
        if level == 1:
            out.append(copy(w, 4 * w, me, sib))
            arriving.append(copy(w, 4 * w, sib, me))
        for j, chip in enumerate(chips):
            if level == 1:
                out.append(copy(w, 4 * w + 1 + j, me, (*chip, c)))
                arriving.append(copy(w, 4 * w + 1 + j, (*chip, c), me))
            else:
                out.append(copy(w, 3 * w + j, (*chip, c), sib))
                arriving.append(copy(w, 3 * w + j, (*chip, 1 - c), me))
    return out, arriving


def _split_call(body, name, bufs, sems_in, after, n_sems_out, token):
    n = len(bufs)
    out_shape = [pltpu.SemaphoreType.DMA((n_sems_out,))] * (2 if n_sems_out else 0)
    out_shape += [pltpu.HBM(a.shape, a.dtype) for a in bufs]
    out_specs = [SEM] * (2 if n_sems_out else 0) + [HBM] * n
    if token:
        out_shape.append(jax.ShapeDtypeStruct((8, LANES), F32))
        out_specs.append(pl.BlockSpec(memory_space=pltpu.VMEM))
    first = 2 if n_sems_out else 0
    return pl.pallas_call(
        body, name=name, out_shape=tuple(out_shape),
        in_specs=[HBM] * n + [SEM] * len(sems_in) + [ANY], out_specs=tuple(out_specs),
        input_output_aliases={i: first + i for i in range(n)},
        compiler_params=pltpu.CompilerParams(has_side_effects=SIDE_EFFECT),
    )(*bufs, *sems_in, after)


def _gather_start(shards, after, name):
    n = len(shards)
    xi, yi, ci = _place()
    me = 4 * xi + 2 * yi + ci
    bufs = [lax.dynamic_update_slice(lax.empty((N_DEV,) + s.shape, s.dtype), s[None], (me, 0, 0)) for s in shards]
    bufs = [pltpu.with_memory_space_constraint(a, pltpu.HBM) for a in bufs]

    def body(*refs):
        out, _ = _gather_level(refs[:n], refs[n + 1], refs[n + 2], 1)
        for cp in out:
            cp.start()
        refs[-1][...] = jnp.zeros_like(refs[-1])

    outs = _split_call(body, name, bufs, [], after, 4 * n, True)
    return outs[0], outs[1], list(outs[2:2 + n]), outs[-1]


def _gather_pass(send1, recv1, bufs, after, name):
    n = len(bufs)

    def body(*refs):
        out1, in1 = _gather_level(refs[:n], refs[n], refs[n + 1], 1)
        out2, _ = _gather_level(refs[:n], refs[n + 3], refs[n + 4], 2)
        for cp in in1:
            cp.wait_recv()
        for cp in out2:
            cp.start()
        for cp in out1:
            cp.wait_send()
        refs[-1][...] = jnp.zeros_like(refs[-1])

    outs = _split_call(body, name, bufs, [send1, recv1], after, 3 * n, True)
    return outs[0], outs[1], list(outs[2:2 + n]), outs[-1]


def _gather_wait(send2, recv2, bufs, after, name):
    n = len(bufs)

    def body(*refs):
        out2, in2 = _gather_level(refs[:n], refs[n], refs[n + 1], 2)
        for cp in in2:
            cp.wait_recv()
        for cp in out2:
            cp.wait_send()

    return list(_split_call(body, name, bufs, [send2, recv2], after, 0, False))


def _scatter_copies(parts, lands, send_sems, recv_sems):
    x, y, c = _place()
    cps = []
    for w, (part, land) in enumerate(zip(parts, lands)):
        for k in range(1, N_DEV):
            px, py, pc = x ^ ((k >> 2) & 1), y ^ ((k >> 1) & 1), c ^ (k & 1)
            cps.append(pltpu.make_async_remote_copy(
                src_ref=part.at[4 * px + 2 * py + pc], dst_ref=land.at[k - 1],
                send_sem=send_sems.at[7 * w + k - 1], recv_sem=recv_sems.at[7 * w + k - 1],
                device_id=(px, py, pc), device_id_type=MESH_ID))
    return cps


def _scatter_start(parts, after, name):
    n = len(parts)

    def body(*refs):
        ins, lands = refs[:n], refs[n:2 * n]
        send_sems, recv_sems = refs[2 * n + 1], refs[2 * n + 2]
        token = refs[-1]
        for cp in _scatter_copies(ins, lands, send_sems, recv_sems):
            cp.start()
        token[...] = jnp.zeros_like(token)

    land_shapes = [(N_DEV - 1,) + p.shape[1:] for p in parts]
    in_hbm = [pltpu.with_memory_space_constraint(p, pltpu.HBM) for p in parts]
    in_hbm += [pltpu.with_memory_space_constraint(lax.empty(s, p.dtype), pltpu.HBM) for s, p in zip(land_shapes, parts)]
    outs = pl.pallas_call(
        body, name=name,
        out_shape=(pltpu.SemaphoreType.DMA((7 * n,)), pltpu.SemaphoreType.DMA((7 * n,)),
                   *[pltpu.HBM(p.shape, p.dtype) for p in parts],
                   *[pltpu.HBM(s, p.dtype) for s, p in zip(land_shapes, parts)],
                   jax.ShapeDtypeStruct((8, LANES), F32)),
        in_specs=[HBM] * (2 * n) + [ANY],
        out_specs=(SEM, SEM, *[HBM] * (2 * n), pl.BlockSpec(memory_space=pltpu.VMEM)),
        input_output_aliases={i: 2 + i for i in range(2 * n)},
        compiler_params=pltpu.CompilerParams(has_side_effects=SIDE_EFFECT),
    )(*in_hbm, after)
    return outs[0], outs[1], list(outs[2:2 + n]), list(outs[2 + n:2 + 2 * n]), outs[-1]


def _scatter_wait(send_sems, recv_sems, parts, lands, after, name):
    n = len(parts)

    def body(*refs):
        ins, lnd = refs[:n], refs[n:2 * n]
        for cp in _scatter_copies(ins, lnd, refs[2 * n], refs[2 * n + 1]):
            cp.wait_send()
            cp.wait_recv()

    outs = pl.pallas_call(
        body, name=name,
        out_shape=tuple(pltpu.HBM(a.shape, a.dtype) for a in parts + lands),
        in_specs=[HBM] * (2 * n) + [SEM, SEM, ANY],
        out_specs=tuple([HBM] * (2 * n)),
        input_output_aliases={i: i for i in range(2 * n)},
        compiler_params=pltpu.CompilerParams(has_side_effects=SIDE_EFFECT),
    )(*parts, *lands, send_sems, recv_sems, after)
    return list(outs[n:])


def _adam(w, g, m, v):
    m2 = ADAM_B1 * m + (1.0 - ADAM_B1) * g
    v2 = ADAM_B2 * v + (1.0 - ADAM_B2) * (g * g)
    m_hat = m2 / (1.0 - ADAM_B1 ** ADAM_STEP)
    v_hat = v2 / (1.0 - ADAM_B2 ** ADAM_STEP)
    delta = -ADAM_LR * (m_hat / (jnp.sqrt(v_hat) + ADAM_EPS) + ADAM_WD * w)
    return delta, m2, v2


def _small_allreduce_adam(part, w, m, v, name):
    rows = part.shape[0]

    def body(p_ref, w_ref, m_ref, v_ref, g_ref, d_ref, mo_ref, vo_ref, buf, send_sems, recv_sems):
        x, y, c = _place()
        buf[0] = p_ref[...]
        cps = []
        for k in range(1, N_DEV):
            kx, ky, kc = (k >> 2) & 1, (k >> 1) & 1, k & 1
            peer = (x ^ kx, y ^ ky, c ^ kc)
            cps.append(pltpu.make_async_remote_copy(
                src_ref=p_ref, dst_ref=buf.at[k], send_sem=send_sems.at[k - 1], recv_sem=recv_sems.at[k - 1],
                device_id=peer, device_id_type=MESH_ID))
        for cp in cps:
            cp.start()
        for cp in cps:
            cp.wait()
        me = 4 * x + 2 * y + c
        total = buf[me]
        for d in range(1, N_DEV):
            total = total + buf[d ^ me]
        g_ref[...] = total
        delta, m2, v2 = _adam(w_ref[...], total, m_ref[...], v_ref[...])
        d_ref[...] = delta
        mo_ref[...] = m2
        vo_ref[...] = v2

    vm = pl.BlockSpec(memory_space=pltpu.VMEM)
    return pl.pallas_call(
        body, name=name,
        out_shape=[jax.ShapeDtypeStruct(part.shape, F32)] * 4,
        in_specs=[vm] * 4, out_specs=[vm] * 4,
        scratch_shapes=[pltpu.VMEM((N_DEV, rows, LANES), F32),
                        pltpu.SemaphoreType.DMA((N_DEV - 1,)), pltpu.SemaphoreType.DMA((N_DEV - 1,))],
    )(part, w, m, v)


def _final_adam(g8, land, w, m, v, me, name):
    _, r, c = g8.shape
    tr = max(q for q in range(16, r + 1, 16) if r % q == 0 and q * c <= ADAM_TILE_ELEMS)

    def body(me_ref, g_ref, land_ref, w_ref, m_ref, v_ref, go_ref, d_ref, mo_ref, vo_ref):
        g = g_ref[...]
        for k in range(N_DEV - 1):
            g = g + land_ref[k].astype(F32)
        go_ref[...] = g
        delta, m2, v2 = _adam(w_ref[...], g, m_ref[...], v_ref[...])
        d_ref[...] = delta
        mo_ref[...] = m2
        vo_ref[...] = v2

    plain = pl.BlockSpec((tr, c), lambda i, s: (i, 0))
    return pl.pallas_call(
        body, name=name,
        out_shape=[jax.ShapeDtypeStruct((r, c), F32)] * 4,
        grid_spec=pltpu.PrefetchScalarGridSpec(
            num_scalar_prefetch=1, grid=(r // tr,),
            in_specs=[pl.BlockSpec((None, tr, c), lambda i, s: (s[0], i, 0)),
                      pl.BlockSpec((N_DEV - 1, tr, c), lambda i, s: (0, i, 0)),
                      plain, plain, plain],
            out_specs=[plain] * 4),
        compiler_params=_params(("arbitrary",)),
    )(me, g8, land, w, m, v)


def _rms(x, gain):
    r = lax.rsqrt(jnp.mean(x * x, axis=-1, keepdims=True) + EPS)
    xh = x * r
    return xh * gain, xh, r


def _rms_bwd(xh, r, gain, dy):
    gdy = gain * dy
    dx = r * (gdy - xh * jnp.mean(xh * gdy, axis=-1, keepdims=True))
    return dx, jnp.sum(dy * xh, axis=0, keepdims=True)


def _load_weights(pairs, sems):
    cps = [pltpu.make_async_copy(src, dst, sems.at[i]) for i, (src, dst) in enumerate(pairs)]
    for cp in cps:
        cp.start()
    for cp in cps:
        cp.wait()


def _ffn_fwd(h, gain, wgu, wd, name):
    t, d = h.shape
    nb, nf, _ = wgu.shape
    nh = nb // 2
    tm = _row_tile(t, 512)

    def body(h_ref, g_ref, wgu_hbm, wd_hbm, out_ref, gu_ref, wgu_v, wd_v, sems):
        @pl.when(pl.program_id(0) == 0)
        def _():
            _load_weights([(wgu_hbm, wgu_v), (wd_hbm, wd_v)], sems)

        x = h_ref[...]
        n, _, _ = _rms(x, g_ref[...])
        nbf = n.astype(BF16)
        acc = jnp.zeros((tm, d), F32)
        for j in range(nh):
            g = _dot_nt(nbf, wgu_v[j])
            u = _dot_nt(nbf, wgu_v[j + nh])
            gu_ref[j] = g.astype(BF16)
            gu_ref[j + nh] = u.astype(BF16)
            a = (g * jax.nn.sigmoid(g)) * u
            acc = acc + _dot(a.astype(BF16), wd_v[j])
        out_ref[...] = x + 0.5 * acc

    return pl.pallas_call(
        body, name=name, grid=(t // tm,),
        out_shape=[jax.ShapeDtypeStruct((t, d), F32), jax.ShapeDtypeStruct((nb, t, nf), BF16)],
        in_specs=[pl.BlockSpec((tm, d), lambda i: (i, 0)), pl.BlockSpec((1, d), lambda i: (0, 0)), ANY, ANY],
        out_specs=[pl.BlockSpec((tm, d), lambda i: (i, 0)), pl.BlockSpec((nb, tm, nf), lambda i: (0, i, 0))],
        scratch_shapes=[pltpu.VMEM(wgu.shape, BF16), pltpu.VMEM(wd.shape, BF16), pltpu.SemaphoreType.DMA((2,))],
        compiler_params=_params(("arbitrary",)),
    )(h, gain, wgu, wd)


def _ffn_bwd(dh, h, gain, gu, wgu, wd, name):
    t, d = h.shape
    nb, nf, _ = wgu.shape
    nh = nb // 2
    tm = _row_tile(t, 256)

    def body(dh_ref, h_ref, g_ref, gu_ref, wgu_hbm, wd_hbm, dhp_ref, dgu_ref, a_ref, n_ref, dgain_ref,
             wgu_v, wd_v, sems):
        @pl.when(pl.program_id(0) == 0)
        def _():
            _load_weights([(wgu_hbm, wgu_v), (wd_hbm, wd_v)], sems)
            dgain_ref[...] = jnp.zeros_like(dgain_ref)

        x = h_ref[...]
        gain_v = g_ref[...]
        n, xh, r = _rms(x, gain_v)
        n_ref[...] = n.astype(BF16)
        dh_v = dh_ref[...]
        dfb = (0.5 * dh_v).astype(BF16)
        dn = jnp.zeros((tm, d), F32)
        for j in range(nh):
            da = _dot_nt(dfb, wd_v[j])
            g = gu_ref[j].astype(F32)
            u = gu_ref[j + nh].astype(F32)
            sg = jax.nn.sigmoid(g)
            si = g * sg
            dg = (da * u * (sg * (1.0 + g * (1.0 - sg)))).astype(BF16)
            du = (da * si).astype(BF16)
            a_ref[j] = (si * u).astype(BF16)
            dgu_ref[j] = dg
            dgu_ref[j + nh] = du
            dn = dn + _dot(dg, wgu_v[j]) + _dot(du, wgu_v[j + nh])
        dx, dgain = _rms_bwd(xh, r, gain_v, dn)
        dhp_ref[...] = dh_v + dx
        dgain_ref[...] += dgain

    row = pl.BlockSpec((tm, d), lambda i: (i, 0))
    vec = pl.BlockSpec((1, d), lambda i: (0, 0))
    return pl.pallas_call(
        body, name=name, grid=(t // tm,),
        out_shape=[jax.ShapeDtypeStruct((t, d), F32), jax.ShapeDtypeStruct((nb, t, nf), BF16),
                   jax.ShapeDtypeStruct((nh, t, nf), BF16), jax.ShapeDtypeStruct((t, d), BF16),
                   jax.ShapeDtypeStruct((1, d), F32)],
        in_specs=[row, row, vec, pl.BlockSpec((nb, tm, nf), lambda i: (0, i, 0)), ANY, ANY],
        out_specs=[row, pl.BlockSpec((nb, tm, nf), lambda i: (0, i, 0)),
                   pl.BlockSpec((nh, tm, nf), lambda i: (0, i, 0)), row, vec],
        scratch_shapes=[pltpu.VMEM(wgu.shape, BF16), pltpu.VMEM(wd.shape, BF16), pltpu.SemaphoreType.DMA((2,))],
        compiler_params=_params(("arbitrary",)),
    )(dh, h, gain, gu, wgu, wd)


def _dw(xa, dy, nb, n, name, scale=1.0, dep=None):
    t, k = xa.shape[-2:]
    tt = _row_tile(t, 512)
    steps = t // tt
    wide = dy.ndim == 2 and xa.ndim == 2
    if xa.ndim == 3:
        x_spec = pl.BlockSpec((nb, tt, k), lambda i: (0, i, 0))
    else:
        x_spec = pl.BlockSpec((tt, k), lambda i: (i, 0))
    if dy.ndim == 3:
        dy_spec = pl.BlockSpec((nb, tt, n), lambda i: (0, i, 0))
    else:
        dy_spec = pl.BlockSpec((tt, dy.shape[1]), lambda i: (i, 0))
    acc_shape = (k, nb * n) if wide else (nb, k, n)
    stage_shape = (k, nb * n) if wide else (k, n)

    def body(x_ref, dy_ref, *rest):
        o_hbm, ob_hbm, acc, stage, sems = rest[-5:]

        @pl.when(pl.program_id(0) == 0)
        def _():
            acc[...] = jnp.zeros_like(acc)

        if wide:
            acc[...] += _dot(x_ref[...].astype(BF16).T, dy_ref[...].astype(BF16))
        elif xa.ndim == 2:
            xt = x_ref[...].astype(BF16).T
            for j in range(nb):
                acc[j] += _dot(xt, dy_ref[j].astype(BF16))
        else:
            dyb = dy_ref[...].astype(BF16)
            for j in range(nb):
                acc[j] += _dot_tn(x_ref[j].astype(BF16), dyb)

        @pl.when(pl.program_id(0) == steps - 1)
        def _():
            if scale != 1.0:
                acc[...] = acc[...] * scale
            if wide:
                cps = [pltpu.make_async_copy(acc.at[:, pl.ds(j * n, n)] if nb > 1 else acc, o_hbm.at[j], sems.at[j])
                       for j in range(nb)]
            else:
                cps = [pltpu.make_async_copy(acc, o_hbm, sems.at[0])]
            for cp in cps:
                cp.start()
            if wide:
                stage[...] = acc[...].astype(BF16)
                bcs = [pltpu.make_async_copy(stage.at[:, pl.ds(j * n, n)] if nb > 1 else stage, ob_hbm.at[j],
                                             sems.at[nb + j]) for j in range(nb)]
                for cp in bcs:
                    cp.start()
                for cp in bcs:
                    cp.wait()
            else:
                for j in range(nb):
                    stage[...] = acc[j].astype(BF16)
                    cp = pltpu.make_async_copy(stage, ob_hbm.at[j], sems.at[nb])
                    cp.start()
                    cp.wait()
            for cp in cps:
                cp.wait()

    return pl.pallas_call(
        body, name=name, grid=(steps,),
        out_shape=[jax.ShapeDtypeStruct((nb, k, n), F32), jax.ShapeDtypeStruct((nb, k, n), BF16)],
        in_specs=[x_spec, dy_spec] + ([] if dep is None else [ANY]),
        out_specs=[ANY, ANY],
        scratch_shapes=[pltpu.VMEM(acc_shape, F32), pltpu.VMEM(stage_shape, BF16),
                        pltpu.SemaphoreType.DMA((2 * nb,))],
        compiler_params=_params(("arbitrary",)),
    )(*((xa, dy) if dep is None else (xa, dy, dep)))


def _proj_fwd(h, gain, win, wgate, name):
    t, d = h.shape
    tm = _row_tile(t, 256)
    nq, ng = win.shape[0], wgate.shape[1]

    def body(h_ref, g_ref, win_ref, wg_ref, un_ref, qkv_ref, gate_ref):
        n, _, _ = _rms(h_ref[...], g_ref[...])
        nbf = n.astype(BF16)
        un_ref[...] = nbf
        qkv_ref[...] = _dot_nt(nbf, win_ref[...])
        gate_ref[...] = jax.nn.sigmoid(_dot(nbf, wg_ref[...]))

    full = lambda a: pl.BlockSpec(a.shape, lambda i: (0,) * a.ndim)
    return pl.pallas_call(
        body, name=name, grid=(t // tm,),
        out_shape=[jax.ShapeDtypeStruct((t, d), BF16), jax.ShapeDtypeStruct((t, nq), F32),
                   jax.ShapeDtypeStruct((t, ng), F32)],
        in_specs=[pl.BlockSpec((tm, d), lambda i: (i, 0)), full(gain), full(win), full(wgate)],
        out_specs=[pl.BlockSpec((tm, d), lambda i: (i, 0)), pl.BlockSpec((tm, nq), lambda i: (i, 0)),
                   pl.BlockSpec((tm, ng), lambda i: (i, 0))],
        compiler_params=_params(("arbitrary",)),
    )(h, gain, win, wgate)


def _proj_bwd(dh, h, gain, dzg, dqkv, win, wgate, name):
    t, d = h.shape
    tm = _row_tile(t, 256)
    nq, ng = win.shape[0], wgate.shape[1]

    def body(dh_ref, h_ref, g_ref, dzg_ref, dqkv_ref, win_ref, wg_ref, dhp_ref, dgain_ref):
        @pl.when(pl.program_id(0) == 0)
        def _():
            dgain_ref[...] = jnp.zeros_like(dgain_ref)

        gain_v = g_ref[...]
        _, xh, r = _rms(h_ref[...], gain_v)
        dun = _dot_nt(dzg_ref[...], wg_ref[...]) + _dot(dqkv_ref[...].astype(BF16), win_ref[...])
        dx, dgain = _rms_bwd(xh, r, gain_v, dun)
        dhp_ref[...] = dh_ref[...] + dx
        dgain_ref[...] += dgain

    full = lambda a: pl.BlockSpec(a.shape, lambda i: (0,) * a.ndim)
    row = pl.BlockSpec((tm, d), lambda i: (i, 0))
    return pl.pallas_call(
        body, name=name, grid=(t // tm,),
        out_shape=[jax.ShapeDtypeStruct((t, d), F32), jax.ShapeDtypeStruct((1, d), F32)],
        in_specs=[row, row, full(gain), pl.BlockSpec((tm, ng), lambda i: (i, 0)),
                  pl.BlockSpec((tm, nq), lambda i: (i, 0)), full(win), full(wgate)],
        out_specs=[row, pl.BlockSpec((1, d), lambda i: (0, 0))],
        compiler_params=_params(("arbitrary",)),
    )(dh, h, gain, dzg, dqkv, win, wgate)


def _merge_fwd(h, ya, yb, gate, wpa, wpb, wout, name):
    t, d = h.shape
    tm = _row_tile(t, 256)

    def body(h_ref, ya_ref, yb_ref, ga_ref, gb_ref, wpa_ref, wpb_ref, wout_ref, out_ref, mg_ref, pa_ref, pb_ref):
        pa = _dot(ya_ref[...].astype(BF16), wpa_ref[...])
        pb = _dot(yb_ref[...].astype(BF16), wpb_ref[...])
        merged = (ga_ref[...] * pa + gb_ref[...] * pb).astype(BF16)
        pa_ref[...] = pa.astype(BF16)
        pb_ref[...] = pb.astype(BF16)
        mg_ref[...] = merged
        out_ref[...] = h_ref[...] + _dot(merged, wout_ref[...])

    full = lambda a: pl.BlockSpec(a.shape, lambda i: (0,) * a.ndim)
    row = pl.BlockSpec((tm, d), lambda i: (i, 0))
    yrow = pl.BlockSpec((tm, ya.shape[1]), lambda i: (i, 0))
    return pl.pallas_call(
        body, name=name, grid=(t // tm,),
        out_shape=[jax.ShapeDtypeStruct((t, d), F32)] + [jax.ShapeDtypeStruct((t, d), BF16)] * 3,
        in_specs=[row, yrow, yrow, pl.BlockSpec((tm, d), lambda i: (i, 0)), pl.BlockSpec((tm, d), lambda i: (i, 1)),
                  full(wpa), full(wpb), full(wout)],
        out_specs=[row] * 4,
        compiler_params=_params(("arbitrary",)),
    )(h, ya, yb, gate, gate, wpa, wpb, wout)


def _merge_bwd(dh, pa, pb, gate, wpa, wpb, wout, name):
    t, d = dh.shape
    tm = _row_tile(t, 256)
    wy = wpa.shape[0]

    def body(dh_ref, pa_ref, pb_ref, ga_ref, gb_ref, wpa_ref, wpb_ref, wout_ref,
             dpa_ref, dpb_ref, dzg_ref, dya_ref, dyb_ref):
        dm = _dot_nt(dh_ref[...].astype(BF16), wout_ref[...])
        ga, gb = ga_ref[...], gb_ref[...]
        dpa = (dm * ga).astype(BF16)
        dpb = (dm * gb).astype(BF16)
        dpa_ref[...] = dpa
        dpb_ref[...] = dpb
        dzg_ref[:, :d] = (dm * pa_ref[...].astype(F32) * ga * (1.0 - ga)).astype(BF16)
        dzg_ref[:, d:] = (dm * pb_ref[...].astype(F32) * gb * (1.0 - gb)).astype(BF16)
        dya_ref[...] = _dot_nt(dpa, wpa_ref[...])
        dyb_ref[...] = _dot_nt(dpb, wpb_ref[...])

    full = lambda a: pl.BlockSpec(a.shape, lambda i: (0,) * a.ndim)
    row = pl.BlockSpec((tm, d), lambda i: (i, 0))
    yrow = pl.BlockSpec((tm, wy), lambda i: (i, 0))
    return pl.pallas_call(
        body, name=name, grid=(t // tm,),
        out_shape=[jax.ShapeDtypeStruct((t, d), BF16), jax.ShapeDtypeStruct((t, d), BF16),
                   jax.ShapeDtypeStruct((t, 2 * d), BF16), jax.ShapeDtypeStruct((t, wy), F32),
                   jax.ShapeDtypeStruct((t, wy), F32)],
        in_specs=[row, row, row, pl.BlockSpec((tm, d), lambda i: (i, 0)), pl.BlockSpec((tm, d), lambda i: (i, 1)),
                  full(wpa), full(wpb), full(wout)],
        out_specs=[row, row, pl.BlockSpec((tm, 2 * d), lambda i: (i, 0)), yrow, yrow],
        compiler_params=_params(("arbitrary",)),
    )(dh, pa, pb, gate, gate, wpa, wpb, wout)


def _ple_loss(h, gain, p, target, wpg, wpe, name):
    t, d = h.shape
    tm = _row_tile(t, 256)
    pd = p.shape[1]

    def body(h_ref, g_ref, p_ref, t_ref, wpg_ref, wpe_ref, dh_ref, dz_ref, dpp_ref, n_ref, dgain_ref, loss_ref):
        @pl.when(pl.program_id(0) == 0)
        def _():
            dgain_ref[...] = jnp.zeros_like(dgain_ref)
            loss_ref[...] = jnp.zeros_like(loss_ref)

        x = h_ref[...]
        gain_v = g_ref[...]
        n, xh, r = _rms(x, gain_v)
        nbf = n.astype(BF16)
        n_ref[...] = nbf
        pg = jax.nn.sigmoid(_dot(nbf, wpg_ref[...]))
        pp = _dot(p_ref[...].astype(BF16), wpe_ref[...])
        err = (x + pg * pp) - t_ref[...]
        loss_ref[...] += 0.5 * jnp.sum(jnp.mean(err * err, axis=-1, keepdims=True))
        dy = err * (1.0 / d)
        dpp_ref[...] = (dy * pg).astype(BF16)
        dz = (dy * pp * pg * (1.0 - pg)).astype(BF16)
        dz_ref[...] = dz
        dn = _dot_nt(dz, wpg_ref[...])
        dx, dgain = _rms_bwd(xh, r, gain_v, dn)
        dh_ref[...] = dy + dx
        dgain_ref[...] += dgain

    full = lambda a: pl.BlockSpec(a.shape, lambda i: (0,) * a.ndim)
    row = pl.BlockSpec((tm, d), lambda i: (i, 0))
    return pl.pallas_call(
        body, name=name, grid=(t // tm,),
        out_shape=[jax.ShapeDtypeStruct((t, d), F32), jax.ShapeDtypeStruct((t, d), BF16),
                   jax.ShapeDtypeStruct((t, d), BF16), jax.ShapeDtypeStruct((t, d), BF16),
                   jax.ShapeDtypeStruct((1, d), F32), jax.ShapeDtypeStruct((8, LANES), F32)],
        in_specs=[row, full(gain), pl.BlockSpec((tm, pd), lambda i: (i, 0)), row, full(wpg), full(wpe)],
        out_specs=[row, row, row, row, pl.BlockSpec((1, d), lambda i: (0, 0)),
                   pl.BlockSpec((8, LANES), lambda i: (0, 0))],
        compiler_params=_params(("arbitrary",)),
    )(h, gain, p, target, wpg, wpe)


def _head_masks():
    lane = lax.broadcasted_iota(jnp.int32, (1, LANES), 1)
    m0 = (lane < HEAD_DIM).astype(F32)
    return m0, 1.0 - m0


def _head_mean(v, m0, m1):
    del m0, m1
    width = v.shape[-1]
    shift = HEAD_DIM.bit_length() - 1
    r = jnp.right_shift(lax.broadcasted_iota(jnp.int32, (width, width), 0), shift)
    c = jnp.right_shift(lax.broadcasted_iota(jnp.int32, (width, width), 1), shift)
    same_head = (r == c).astype(BF16)
    hi = v.astype(BF16)
    lo = (v - hi.astype(F32)).astype(BF16)
    return (_dot(hi, same_head) + _dot(lo, same_head)) * (1.0 / HEAD_DIM)


def _head_norm(x, gain, m0, m1):
    r = lax.rsqrt(_head_mean(x * x, m0, m1) + EPS)
    xh = x * r
    return xh * gain, xh, r


def _head_norm_bwd(xh, r, gain, dy, m0, m1):
    gdy = gain * dy
    dx = r * (gdy - xh * _head_mean(xh * gdy, m0, m1))
    return dx, jnp.sum(dy * xh, axis=0, keepdims=True)


GROUP = 4
QW = GROUP * HEAD_DIM
STACK = GROUP * QTILE


def _kv_width(mode):
    return QW if mode == "A" else LANES


def _q_scratch_shape(mode, s_len):
    return (s_len, QW) if mode == "A" else (GROUP * s_len, LANES)


def _group_masks(dtype=F32):
    lane = lax.broadcasted_iota(jnp.int32, (1, QW), 1)
    return [((lane >= h * HEAD_DIM) & (lane < (h + 1) * HEAD_DIM)).astype(dtype) for h in range(GROUP)]


def _stack_heads(first_kv, x, m0, m1):
    out = []
    for half in range(GROUP // 2):
        xh = x[:, half * LANES:(half + 1) * LANES]
        a0, a1 = xh * m0, xh * m1
        r0, r1 = pltpu.roll(a0, HEAD_DIM, 1), pltpu.roll(a1, HEAD_DIM, 1)
        out += [jnp.where(first_kv, a0, r0), jnp.where(first_kv, r1, a1)]
    return out


def _unstack_heads(mode, first_kv, ts, m0, m1):
    if mode == "A":
        masks = _group_masks()
        return sum(t * mk for t, mk in zip(ts, masks))
    halves = []
    for half in range(GROUP // 2):
        t0 = jnp.where(first_kv, ts[2 * half], pltpu.roll(ts[2 * half], HEAD_DIM, 1))
        t1 = jnp.where(first_kv, pltpu.roll(ts[2 * half + 1], HEAD_DIM, 1), ts[2 * half + 1])
        halves.append(t0 * m0 + t1 * m1)
    return jnp.concatenate(halves, axis=1)


def _store_stacked(dst, i, heads):
    for half in range(2):
        rows = slice(half * QTILE, (half + 1) * QTILE)
        for h, x in enumerate(heads):
            dst[pl.ds((2 * i + half) * STACK + h * QTILE, QTILE), :] = x[rows].astype(dst.dtype)


def _load_stacked(mode, ref, m):
    if mode == "B":
        return ref[pl.ds(pl.multiple_of(m * STACK, STACK), STACK), :]
    x = ref[pl.ds(pl.multiple_of(m * QTILE, QTILE), QTILE), :]
    return jnp.concatenate([x * mk for mk in _group_masks(x.dtype)], axis=0)


def _attn_prep(mode, group, s_len, padk, q_ref, k_ref, v_ref, gq_ref, gk_ref, qs, k2, v2, do_ref=None, dos=None):
    m0, m1 = _head_masks()
    zpad = jnp.zeros((padk, k2.shape[1]), BF16)
    k2[pl.ds(0, padk), :] = zpad
    v2[pl.ds(0, padk), :] = zpad
    first_kv = group == 0
    rt = 2 * QTILE
    for i in range(s_len // rt):
        rows = pl.ds(i * rt, rt)
        qn, _, _ = _head_norm(q_ref[rows, :], gq_ref[...], m0, m1)
        kn, _, _ = _head_norm(k_ref[rows, :], gk_ref[...], m0, m1)
        qn = qn * (HEAD_DIM ** -0.5)
        if mode == "A":
            qs[rows, :] = qn.astype(BF16)
            if dos is not None:
                dos[rows, :] = do_ref[rows, :].astype(BF16)
        else:
            _store_stacked(qs, i, _stack_heads(first_kv, qn, m0, m1))
            if dos is not None:
                _store_stacked(dos, i, _stack_heads(first_kv, do_ref[rows, :], m0, m1))
        k2[pl.ds(padk + i * rt, rt), :] = kn.astype(BF16)
        v2[pl.ds(padk + i * rt, rt), :] = v_ref[rows, :].astype(BF16)


def _attn_probs(mode, q_st, kb, bias, ok, sink):
    s = _dot_nt(q_st, kb) + bias
    s = jnp.where(ok, s, NEG_INF)
    mx = jnp.max(s, axis=-1, keepdims=True)
    if mode == "B":
        mx = jnp.maximum(mx, sink)
    e = jnp.exp(s - mx)
    l = jnp.sum(e, axis=-1, keepdims=True)
    if mode == "B":
        l = l + jnp.exp(sink - mx)
    return e, mx, l


def _sink_column(sink_ref, group):
    row = lax.broadcasted_iota(jnp.int32, (STACK, 1), 0)
    col = jnp.zeros((STACK, 1), F32)
    for h in range(GROUP):
        col = jnp.where((row >= h * QTILE) & (row < (h + 1) * QTILE), sink_ref[GROUP * group + h], col)
    return col


def _head_deltas(dd, m0, m1):
    cols = []
    for half in range(GROUP // 2):
        dh = dd[:, half * LANES:(half + 1) * LANES]
        cols += [jnp.sum(dh * m0, axis=-1, keepdims=True), jnp.sum(dh * m1, axis=-1, keepdims=True)]
    return jnp.concatenate(cols, axis=0)


def _attn_cols(mode):
    if mode == "A":
        return (lambda b, g: (b, g)), (lambda b, g: (b, 2 + g)), (lambda b, g: (b, 4 + g))
    return (lambda b, g: (b, 6 + g)), (lambda b, g: (b, 16)), (lambda b, g: (b, 17))


def _attn_fwd(mode, qkv, gq, gk, bias, sinks, bl, s_len, name):
    bw = bias.shape[-1]
    padk = bw - QTILE
    nt = s_len // QTILE
    qmap, kmap, vmap = _attn_cols(mode)

    kw = _kv_width(mode)

    def body(q_ref, k_ref, v_ref, gq_ref, gk_ref, bias_ref, sink_ref, o_ref, qs, k2, v2):
        group = pl.program_id(1)
        m0, m1 = _head_masks()
        first_kv = group == 0
        _attn_prep(mode, group, s_len, padk, q_ref, k_ref, v_ref, gq_ref, gk_ref, qs, k2, v2)
        col = lax.broadcasted_iota(jnp.int32, (STACK, bw), 1)
        sink = _sink_column(sink_ref, group)

        def tile(m, carry):
            r0 = pl.multiple_of(m * QTILE, QTILE)
            q_st = _load_stacked(mode, qs, m)
            ok = col >= (padk - r0)
            e, _, l = _attn_probs(mode, q_st, k2[pl.ds(r0, bw), :], bias_ref[...], ok, sink)
            o_st = _dot(e.astype(BF16), v2[pl.ds(r0, bw), :]) / l
            heads = [o_st[h * QTILE:(h + 1) * QTILE] for h in range(GROUP)]
            o_ref[pl.ds(r0, QTILE), :] = _unstack_heads(mode, first_kv, heads, m0, m1)
            return carry

        lax.fori_loop(0, nt, tile, 0, unroll=2)

    blk = lambda w, f: pl.BlockSpec((s_len, w), f)
    return pl.pallas_call(
        body, name=name, grid=(bl, B_Q_HEADS // GROUP),
        out_shape=jax.ShapeDtypeStruct((bl * s_len, B_Q_HEADS * HEAD_DIM), F32),
        in_specs=[blk(QW, qmap), blk(kw, kmap), blk(kw, vmap),
                  pl.BlockSpec((1, QW), lambda b, g: (0, 0)), pl.BlockSpec((1, kw), lambda b, g: (0, 0)),
                  pl.BlockSpec((STACK, bw), lambda b, g: (g, 0)),
                  pl.BlockSpec(memory_space=pltpu.SMEM)],
        out_specs=blk(QW, lambda b, g: (b, g)),
        scratch_shapes=[pltpu.VMEM(_q_scratch_shape(mode, s_len), BF16)] + [pltpu.VMEM((s_len + padk, kw), BF16)] * 2,
        compiler_params=_params(("arbitrary", "arbitrary")),
    )(qkv, qkv, qkv, gq, gk, bias.reshape(-1, bw), sinks)


def _attn_bwd(mode, qkv, gq, gk, bias, sinks, y, dy, bl, s_len, name):
    bw = bias.shape[-1]
    padk = bw - QTILE
    nt = s_len // QTILE
    qmap, kmap, vmap = _attn_cols(mode)
    t = bl * s_len
    kw = _kv_width(mode)
    kvw = 4 * LANES if mode == "A" else LANES

    def body(q_ref, k_ref, v_ref, gq_ref, gk_ref, bias_ref, sink_ref, y_ref, dy_ref,
             dq_ref, dk_ref, dv_ref, dgq_ref, dgk_ref, dbias_ref, dsink_ref,
             qs, k2, v2, dos, dqs, dk, dv):
        group = pl.program_id(1)
        m0, m1 = _head_masks()
        first_kv = group == 0
        _attn_prep(mode, group, s_len, padk, q_ref, k_ref, v_ref, gq_ref, gk_ref, qs, k2, v2, dy_ref, dos)
        dk[...] = jnp.zeros_like(dk)
        dv[...] = jnp.zeros_like(dv)
        dbias_ref[...] = jnp.zeros_like(dbias_ref)
        col = lax.broadcasted_iota(jnp.int32, (STACK, bw), 1)
        lane8 = lax.broadcasted_iota(jnp.int32, (8, LANES), 1)
        sink = _sink_column(sink_ref, group)

        def tile(m, dsink):
            r0 = pl.multiple_of(m * QTILE, QTILE)
            rows = pl.ds(r0, QTILE)
            band = pl.ds(r0, bw)
            q_st = _load_stacked(mode, qs, m)
            do_st = _load_stacked(mode, dos, m)
            delta = _head_deltas(dy_ref[rows, :] * y_ref[rows, :], m0, m1)
            ok = col >= (padk - r0)
            kb = k2[band, :]
            e, mx, l = _attn_probs(mode, q_st, kb, bias_ref[...], ok, sink)
            inv = 1.0 / l
            pn = e * inv
            dp = _dot_nt(do_st, v2[band, :])
            ds = pn * (dp - delta)
            if mode == "A":
                dbias_ref[...] += ds
            else:
                part = jnp.exp(sink - mx) * inv * delta
                for h in range(GROUP):
                    dsink = dsink - jnp.where(lane8 == h, jnp.sum(part[h * QTILE:(h + 1) * QTILE]), 0.0)
            dsb = ds.astype(BF16)
            dv[band, :] += _dot_tn(pn.astype(BF16), do_st)
            dk[band, :] += _dot_tn(dsb, q_st)
            dq_st = _dot(dsb, kb)
            if mode == "A":
                heads = [dq_st[h * QTILE:(h + 1) * QTILE] for h in range(GROUP)]
                dqs[rows, :] = _unstack_heads(mode, first_kv, heads, m0, m1)
            else:
                dqs[pl.ds(pl.multiple_of(m * STACK, STACK), STACK), :] = dq_st
            return dsink

        dsink = lax.fori_loop(0, nt, tile, jnp.zeros((8, LANES), F32), unroll=2)
        dsink_ref[...] = dsink

        rt = 2 * QTILE
        dgq = jnp.zeros((1, QW), F32)
        dgk = jnp.zeros((1, kw), F32)
        for i in range(s_len // rt):
            rows = pl.ds(i * rt, rt)
            src = pl.ds(padk + i * rt, rt)
            gq_v, gk_v = gq_ref[...], gk_ref[...]
            _, qh, qr = _head_norm(q_ref[rows, :], gq_v, m0, m1)
            _, kh, kr = _head_norm(k_ref[rows, :], gk_v, m0, m1)
            if mode == "A":
                dqn = dqs[rows, :] * (HEAD_DIM ** -0.5)
            else:
                dqn = jnp.concatenate(
                    [_unstack_heads(mode, first_kv, [dqs[pl.ds((2 * i + half) * STACK + h * QTILE, QTILE), :]
                                                     for h in range(GROUP)], m0, m1)
                     for half in range(2)], axis=0) * (HEAD_DIM ** -0.5)
            dq_raw, dgq_i = _head_norm_bwd(qh, qr, gq_v, dqn, m0, m1)
            dk_raw, dgk_i = _head_norm_bwd(kh, kr, gk_v, dk[src, :], m0, m1)
            dvn = dv[src, :]
            dq_ref[rows, :] = dq_raw
            if mode == "A":
                dk_ref[rows, :] = dk_raw
                dv_ref[rows, :] = dvn
            else:
                @pl.when(group == 0)
                def _():
                    dk_ref[rows, :] = dk_raw
                    dv_ref[rows, :] = dvn

                @pl.when(group != 0)
                def _():
                    dk_ref[rows, :] += dk_raw
                    dv_ref[rows, :] += dvn
            dgq, dgk = dgq + dgq_i, dgk + dgk_i
        dgq_ref[...] = jnp.broadcast_to(dgq, (8, QW))
        dgk_ref[...] = jnp.broadcast_to(dgk, (8, kw))

    ng = B_Q_HEADS // GROUP
    blk = lambda w, f: pl.BlockSpec((s_len, w), f)
    small = lambda w: pl.BlockSpec((None, None, 8, w), lambda b, g: (b, g, 0, 0))
    own = lambda b, g: (b, g)
    kvmap = own if mode == "A" else (lambda b, g: (b, 0))
    pad_f32 = pltpu.VMEM((s_len + padk, kw), F32)
    pad_bf = pltpu.VMEM((s_len + padk, kw), BF16)
    stack_bf = pltpu.VMEM(_q_scratch_shape(mode, s_len), BF16)
    outs = pl.pallas_call(
        body, name=name, grid=(bl, ng),
        out_shape=[jax.ShapeDtypeStruct((t, ng * QW), F32), jax.ShapeDtypeStruct((t, kvw), F32),
                   jax.ShapeDtypeStruct((t, kvw), F32),
                   jax.ShapeDtypeStruct((bl, ng, 8, QW), F32), jax.ShapeDtypeStruct((bl, ng, 8, kw), F32),
                   jax.ShapeDtypeStruct((bl, ng * STACK, bw), F32), jax.ShapeDtypeStruct((bl, ng, 8, LANES), F32)],
        in_specs=[blk(QW, qmap), blk(kw, kmap), blk(kw, vmap),
                  pl.BlockSpec((1, QW), lambda b, g: (0, 0)), pl.BlockSpec((1, kw), lambda b, g: (0, 0)),
                  pl.BlockSpec((STACK, bw), lambda b, g: (g, 0)),
                  pl.BlockSpec(memory_space=pltpu.SMEM),
                  blk(QW, own), blk(QW, own)],
        out_specs=[blk(QW, own), blk(kw, kvmap), blk(kw, kvmap), small(QW), small(kw),
                   pl.BlockSpec((None, STACK, bw), lambda b, g: (b, g, 0)), small(LANES)],
        scratch_shapes=[stack_bf, pad_bf, pad_bf, stack_bf, pltpu.VMEM(_q_scratch_shape(mode, s_len), F32),
                        pad_f32, pad_f32],
        compiler_params=_params(("arbitrary", "arbitrary")),
    )(qkv, qkv, qkv, gq, gk, bias.reshape(-1, bw), sinks, y, dy)
    outs = list(outs)
    outs[5] = outs[5].reshape(bl, B_Q_HEADS, QTILE, bw)
    return outs


def _band_geometry(prev):
    bw = QTILE + prev * CHUNK
    i = np.arange(QTILE)[:, None]
    j = np.arange(bw)[None, :]
    dist = i + prev * CHUNK - j
    valid = (j // CHUNK >= i // CHUNK) & (j // CHUNK <= i // CHUNK + prev)
    return dist, valid


A_VAR0 = (A_PREV * CHUNK - A_MAX_REL) // LANES * LANES


A_NVAR = QTILE + A_PREV * CHUNK - A_VAR0


def _skew_rows(x, sign):
    rows, n = x.shape
    row = lax.broadcasted_iota(jnp.int32, x.shape, 0)
    b = 1
    while b < rows:
        x = jnp.where((row & b) != 0, pltpu.roll(x, (sign * b) % n, 1), x)
        b *= 2
    return x


def _rel_bias_expand(table, name):
    _, valid = _band_geometry(A_PREV)
    bw = valid.shape[1]
    valid_f = jnp.asarray(valid.astype(np.float32))
    rev = jnp.flip(table[:, 1:], axis=1).reshape(A_HEADS, 1, A_NVAR)

    def body(rev_ref, valid_ref, o_ref):
        rowv = jnp.broadcast_to(rev_ref[...], (QTILE, A_NVAR))
        top = rowv[:, 0:1]
        var = _skew_rows(rowv, 1)
        row = lax.broadcasted_iota(jnp.int32, (QTILE, A_NVAR), 0)
        colv = lax.broadcasted_iota(jnp.int32, (QTILE, A_NVAR), 1)
        var = jnp.where(colv < row, top, var)
        ok = valid_ref[...] > 0.5
        o_ref[:, :A_VAR0] = jnp.where(ok[:, :A_VAR0], top, NEG_INF)
        o_ref[:, A_VAR0:] = jnp.where(ok[:, A_VAR0:], var, NEG_INF)

    return pl.pallas_call(
        body, name=name, grid=(A_HEADS,),
        out_shape=jax.ShapeDtypeStruct((A_HEADS, QTILE, bw), F32),
        in_specs=[pl.BlockSpec((None, 1, A_NVAR), lambda h: (h, 0, 0)), pl.BlockSpec((QTILE, bw), lambda h: (0, 0))],
        out_specs=pl.BlockSpec((None, QTILE, bw), lambda h: (h, 0, 0)),
        compiler_params=_params(("arbitrary",)),
    )(rev, valid_f)


def _rel_bias_grad(dbias, name):
    bl = dbias.shape[0]
    bw = dbias.shape[-1]

    def body(db_ref, o_ref):
        g = db_ref[0]
        for b in range(1, bl):
            g = g + db_ref[b]
        sk = _skew_rows(g[:, A_VAR0:], -1)
        row = lax.broadcasted_iota(jnp.int32, (QTILE, A_NVAR), 0)
        colv = lax.broadcasted_iota(jnp.int32, (QTILE, A_NVAR), 1)
        wrapped = (row + colv) >= A_NVAR
        main = jnp.sum(jnp.where(wrapped, 0.0, sk), axis=0, keepdims=True)
        top = jnp.sum(g[:, :A_VAR0]) + jnp.sum(jnp.where(wrapped, sk, 0.0))
        o_ref[:, :A_NVAR] = jnp.broadcast_to(main, (8, A_NVAR))
        o_ref[:, A_NVAR:] = jnp.full((8, LANES), top, F32)

    out = pl.pallas_call(
        body, name=name, grid=(A_HEADS,),
        out_shape=jax.ShapeDtypeStruct((A_HEADS, 8, A_NVAR + LANES), F32),
        in_specs=[pl.BlockSpec((bl, None, QTILE, bw), lambda h: (0, h, 0, 0))],
        out_specs=pl.BlockSpec((None, 8, A_NVAR + LANES), lambda h: (h, 0, 0)),
        compiler_params=_params(("arbitrary",)),
    )(dbias)
    main, top = out[:, 0, :A_NVAR], out[:, 0, A_NVAR]
    fm = jnp.flip(main, axis=1)
    return jnp.concatenate([jnp.zeros((A_HEADS, 1), F32), fm[:, :-1], fm[:, -1:] + top[:, None]], axis=1)


def _alibi_bias():
    dist, valid = _band_geometry(B_PREV)
    slopes = np.array([2.0 ** (-8.0 * (h + 1) / B_Q_HEADS) for h in range(B_Q_HEADS)], dtype=np.float32)
    bias = -slopes[:, None, None] * np.abs(dist).astype(np.float32)[None]
    return jnp.asarray(np.where(valid[None], bias, np.float32(NEG_INF)).astype(np.float32))


SMALL_NAMES = ("ffn1_norm", "mix_norm", "ffn2_norm", "ple_norm", "a_q_norm", "a_k_norm", "b_q_norm", "b_k_norm",
               "a_rel_bias", "b_sinks", "loss")


def _pack_small(vals):
    rows = []
    for nme in SMALL_NAMES:
        v = vals[nme].astype(F32)
        if nme == "a_rel_bias":
            v = jnp.pad(v.reshape(A_HEADS, -1), ((0, 0), (0, 3 * LANES - (2 * A_MAX_REL + 1))))
        v = v.reshape(-1)
        v = jnp.pad(v, (0, (-v.shape[0]) % LANES))
        rows.append(v.reshape(-1, LANES))
    out = jnp.concatenate(rows, axis=0)
    return jnp.pad(out, ((0, (-out.shape[0]) % 8), (0, 0)))


def _unpack_small(packed, shapes):
    out, r = {}, 0
    for nme in SMALL_NAMES:
        shp = shapes[nme]
        if nme == "a_rel_bias":
            nr = A_HEADS * 3
            out[nme] = packed[r:r + nr].reshape(A_HEADS, 3 * LANES)[:, :2 * A_MAX_REL + 1].reshape(shp)
        else:
            size = int(np.prod(shp)) if shp else 1
            nr = -(-size // LANES)
            out[nme] = packed[r:r + nr].reshape(-1)[:size].reshape(shp)
        r += nr
    return out


BIG_NAMES = ("ffn1_w_gu", "ffn1_w_down", "w_in", "w_gate", "w_proj_a", "w_proj_b", "w_out",
             "ffn2_w_gu", "ffn2_w_down", "w_ple_gate", "w_ple_proj")
ROW_SHARDED = ("ffn1_w_down", "ffn2_w_down", "w_out", "w_ple_gate")
WEIGHT_ORDER = ("ffn1_norm", "ffn1_w_gu", "ffn1_w_down", "mix_norm", "w_in", "a_q_norm", "a_k_norm", "a_rel_bias",
                "b_q_norm", "b_k_norm", "b_sinks", "w_gate", "w_proj_a", "w_proj_b", "w_out", "ffn2_norm",
                "ffn2_w_gu", "ffn2_w_down", "ple_norm", "w_ple_gate", "w_ple_proj")


TRANSPOSED = ("ffn1_w_gu", "ffn2_w_gu", "w_in")


def _local(a, nme):
    return a[0].T if nme in TRANSPOSED else a[0]


def _full_cols(wg):
    nb, k, n = wg.shape
    return jnp.transpose(wg, (1, 0, 2)).reshape(k, nb * n)


def _col_blocks(g, nb):
    k, n = g.shape
    return jnp.transpose(g.reshape(k, nb, n // nb), (1, 0, 2))


def _step(x, p, target, w, m, v):
    bl, s_len, d = x.shape
    t = bl * s_len
    h0 = x.reshape(t, d)
    pt = p.reshape(t, p.shape[-1])
    tgt = target.reshape(t, d)

    g_ffn1, g_mix, g_ffn2, g_ple = w["ffn1_norm"], w["mix_norm"], w["ffn2_norm"], w["ple_norm"]
    tiled = lambda a, width: jnp.tile(a.reshape(1, HEAD_DIM), (1, width // HEAD_DIM))
    gqa, gka = tiled(w["a_q_norm"], QW), tiled(w["a_k_norm"], _kv_width("A"))
    gqb, gkb = tiled(w["b_q_norm"], QW), tiled(w["b_k_norm"], _kv_width("B"))
    sinks = w["b_sinks"].reshape(B_Q_HEADS)
    bias_a = _rel_bias_expand(w["a_rel_bias"][0], "rel_bias_expand")
    bias_b = _alibi_bias()

    shard = {nme: _local(w[nme], nme).astype(BF16) for nme in BIG_NAMES}
    wgu1, wd1 = _all_gather([shard["ffn1_w_gu"], shard["ffn1_w_down"]], "weights_gather_ffn1")
    nf = wgu1.shape[1]
    wd1 = wd1.reshape(N_DEV // 2, nf, d)
    mixer_names = ("w_in", "w_gate")
    rest_names = ("w_proj_a", "w_proj_b", "w_out", "ffn2_w_gu", "ffn2_w_down", "w_ple_gate", "w_ple_proj")
    send1, recv1, bufs, token = _gather_start([shard[nme] for nme in mixer_names], wgu1, "gather_start_mixer")

    h1, gu1 = _ffn_fwd(h0, g_ffn1 + token[0, 0], wgu1, wd1, "ffn1_fwd")
    send2, recv2, bufs, token = _gather_pass(send1, recv1, bufs, h1, "gather_pass_mixer")
    send1, recv1, rest_bufs, token = _gather_start([shard[nme] for nme in rest_names], token, "gather_start_rest")
    win, wgate = _gather_wait(send2, recv2, bufs, token, "gather_wait_mixer")
    win, wgate = win.reshape(IN_COLS, d), _full_cols(wgate)
    un, qkv, gate = _proj_fwd(h1, g_mix, win, wgate, "proj_fwd")
    ya = _attn_fwd("A", qkv, gqa, gka, bias_a, sinks, bl, s_len, "attn_a_fwd")
    yb = _attn_fwd("B", qkv, gqb, gkb, bias_b, sinks, bl, s_len, "attn_b_fwd")
    send2, recv2, rest_bufs, token = _gather_pass(send1, recv1, rest_bufs, yb, "gather_pass_rest")
    gathered = dict(zip(rest_names, _gather_wait(send2, recv2, rest_bufs, token, "gather_wait_rest")))
    wgu2 = gathered["ffn2_w_gu"]
    wd2 = gathered["ffn2_w_down"].reshape(N_DEV // 2, nf, d)
    wpa = _full_cols(gathered["w_proj_a"])
    wpb = _full_cols(gathered["w_proj_b"])
    wpe = _full_cols(gathered["w_ple_proj"])
    wout = gathered["w_out"].reshape(d, d)
    wpg = gathered["w_ple_gate"].reshape(d, d)
    h2, merged, pa, pb = _merge_fwd(h1, ya, yb, gate, wpa, wpb, wout, "merge_fwd")
    h3, gu2 = _ffn_fwd(h2, g_ffn2, wgu2, wd2, "ffn2_fwd")
    dh3, dz4, dpp, n4, dg_ple, loss_part = _ple_loss(h3, g_ple, pt, tgt, wpg, wpe, "ple_loss")

    xi, yi, ci = _place()
    me = jnp.stack([4 * xi + 2 * yi + ci]).astype(jnp.int32)
    g32, g16, big = {}, {}, {}

    def keep(nme, pair, rows=None):
        for store, g in zip((g32, g16), pair):
            store[nme] = g if rows is None else g.reshape(N_DEV, rows, d)

    def start(names, after, tag):
        send, recv, parts, lands, token = _scatter_start([g16[nme] for nme in names], after, "grads_start_" + tag)
        return names, send, recv, parts, lands, token

    def finish(state, after, tag):
        names, send, recv, parts, lands, _ = state
        lands = _scatter_wait(send, recv, parts, lands, after, "grads_wait_" + tag)
        return names, lands

    def adam(done):
        for nme, land in zip(*done):
            outs = _final_adam(g32[nme], land, _local(w[nme], nme), _local(m[nme], nme), _local(v[nme], nme), me, "adam_" + nme)
            big[nme] = [(o.T if nme in TRANSPOSED else o)[None] for o in outs]

    keep("w_ple_gate", _dw(n4, dz4, 1, d, "dw_ple_gate"), d // N_DEV)
    keep("w_ple_proj", _dw(pt, dpp, N_DEV, d // N_DEV, "dw_ple_proj"))

    dh2, dgu2, a2, n3, dg_ffn2 = _ffn_bwd(dh3, h2, g_ffn2, gu2, wgu2, wd2, "ffn2_bwd")
    keep("ffn2_w_gu", _dw(dgu2, n3, N_DEV, d, "dw_ffn2_gu"))
    keep("ffn2_w_down", _dw(a2, dh3, N_DEV // 2, d, "dw_ffn2_down", 0.5), nf // 2)
    flight = start(("w_ple_gate", "w_ple_proj", "ffn2_w_gu", "ffn2_w_down"), dh2, "ffn2")

    dpa, dpb, dzg, dya, dyb = _merge_bwd(dh2, pa, pb, gate, wpa, wpb, wout, "merge_bwd")
    keep("w_out", _dw(merged, dh2, 1, d, "dw_out"), d // N_DEV)
    keep("w_proj_a", _dw(ya, dpa, N_DEV, d // N_DEV, "dw_proj_a"))
    keep("w_proj_b", _dw(yb, dpb, N_DEV, d // N_DEV, "dw_proj_b"))
    keep("w_gate", _dw(un, dzg, N_DEV, 2 * d // N_DEV, "dw_gate"))

    tok = flight[-1][0, 0]
    dqa, dka, dva, dgqa, dgka, dbias, _ = _attn_bwd("A", qkv, gqa + tok, gka, bias_a, sinks, ya, dya, bl, s_len,
                                                     "attn_a_bwd")
    dqb, dkb, dvb, dgqb, dgkb, _, dsink = _attn_bwd("B", qkv, gqb, gkb, bias_b, sinks, yb, dyb, bl, s_len, "attn_b_bwd")
    dqkv = jnp.concatenate([dqa, dka, dva, dqb, dkb, dvb], axis=1)
    dtab = _rel_bias_grad(dbias, "rel_bias_grad")

    dh1, dg_mix = _proj_bwd(dh2, h1, g_mix, dzg, dqkv, win, wgate, "proj_bwd")
    keep("w_in", _dw(dqkv, un, 1, d, "dw_in"), IN_COLS // N_DEV)
    done = finish(flight, g32["w_in"], "ffn2")
    flight = start(("w_out", "w_proj_a", "w_proj_b", "w_gate", "w_in"), done[1][0], "mixer")
    adam(done)

    dh0, dgu1, a1, n1, dg_ffn1 = _ffn_bwd(dh1, h0, g_ffn1 + flight[-1][0, 0], gu1, wgu1, wd1, "ffn1_bwd")
    keep("ffn1_w_down", _dw(a1, dh1, N_DEV // 2, d, "dw_ffn1_down", 0.5), nf // 2)
    done = finish(flight, g32["ffn1_w_down"], "mixer")
    flight = start(("ffn1_w_down",), done[1][0], "ffn1_down")
    adam(done)

    keep("ffn1_w_gu", _dw(dgu1, n1, N_DEV, d, "dw_ffn1_gu", dep=flight[-1]))
    done = finish(flight, g32["ffn1_w_gu"], "ffn1_down")
    flight = start(("ffn1_w_gu",), done[1][0], "ffn1_gu")
    adam(done)
    smalls = (dg_ffn1, dg_mix, dg_ffn2, dg_ple + flight[-1][0, 0], dgqa, dgka, dgqb, dgkb, dtab, dsink)
    return dh0, loss_part, big, smalls, flight, finish, adam


def kernel(x, p, ffn1_norm, ffn1_w_gu, ffn1_w_down, mix_norm, w_in, a_q_norm, a_k_norm, a_rel_bias, b_q_norm, b_k_norm, b_sinks, w_gate, w_proj_a, w_proj_b, w_out, ffn2_norm, ffn2_w_gu, ffn2_w_down, ple_norm, w_ple_gate, w_ple_proj, loss_target, m_ffn1_norm, m_ffn1_w_gu, m_ffn1_w_down, m_mix_norm, m_w_in, m_a_q_norm, m_a_k_norm, m_a_rel_bias, m_b_q_norm, m_b_k_norm, m_b_sinks, m_w_gate, m_w_proj_a, m_w_proj_b, m_w_out, m_ffn2_norm, m_ffn2_w_gu, m_ffn2_w_down, m_ple_norm, m_w_ple_gate, m_w_ple_proj, v_ffn1_norm, v_ffn1_w_gu, v_ffn1_w_down, v_mix_norm, v_w_in, v_a_q_norm, v_a_k_norm, v_a_rel_bias, v_b_q_norm, v_b_k_norm, v_b_sinks, v_w_gate, v_w_proj_a, v_w_proj_b, v_w_out, v_ffn2_norm, v_ffn2_w_gu, v_ffn2_w_down, v_ple_norm, v_w_ple_gate, v_w_ple_proj):
    w = dict(ffn1_norm=ffn1_norm, ffn1_w_gu=ffn1_w_gu, ffn1_w_down=ffn1_w_down, mix_norm=mix_norm, w_in=w_in,
             a_q_norm=a_q_norm, a_k_norm=a_k_norm, a_rel_bias=a_rel_bias, b_q_norm=b_q_norm, b_k_norm=b_k_norm,
             b_sinks=b_sinks, w_gate=w_gate, w_proj_a=w_proj_a, w_proj_b=w_proj_b, w_out=w_out, ffn2_norm=ffn2_norm,
             ffn2_w_gu=ffn2_w_gu, ffn2_w_down=ffn2_w_down, ple_norm=ple_norm, w_ple_gate=w_ple_gate,
             w_ple_proj=w_ple_proj)
    m = dict(ffn1_norm=m_ffn1_norm, ffn1_w_gu=m_ffn1_w_gu, ffn1_w_down=m_ffn1_w_down, mix_norm=m_mix_norm,
             w_in=m_w_in, a_q_norm=m_a_q_norm, a_k_norm=m_a_k_norm, a_rel_bias=m_a_rel_bias, b_q_norm=m_b_q_norm,
             b_k_norm=m_b_k_norm, b_sinks=m_b_sinks, w_gate=m_w_gate, w_proj_a=m_w_proj_a, w_proj_b=m_w_proj_b,
             w_out=m_w_out, ffn2_norm=m_ffn2_norm, ffn2_w_gu=m_ffn2_w_gu, ffn2_w_down=m_ffn2_w_down,
             ple_norm=m_ple_norm, w_ple_gate=m_w_ple_gate, w_ple_proj=m_w_ple_proj)
    v = dict(ffn1_norm=v_ffn1_norm, ffn1_w_gu=v_ffn1_w_gu, ffn1_w_down=v_ffn1_w_down, mix_norm=v_mix_norm,
             w_in=v_w_in, a_q_norm=v_a_q_norm, a_k_norm=v_a_k_norm, a_rel_bias=v_a_rel_bias, b_q_norm=v_b_q_norm,
             b_k_norm=v_b_k_norm, b_sinks=v_b_sinks, w_gate=v_w_gate, w_proj_a=v_w_proj_a, w_proj_b=v_w_proj_b,
             w_out=v_w_out, ffn2_norm=v_ffn2_norm, ffn2_w_gu=v_ffn2_w_gu, ffn2_w_down=v_ffn2_w_down,
             ple_norm=v_ple_norm, w_ple_gate=v_w_ple_gate, w_ple_proj=v_w_ple_proj)
    bl, s_len, d = x.shape

    dh0, loss_part, big, smalls, flight, finish, adam = _step(x, p[0], loss_target, w, m, v)
    dg_ffn1, dg_mix, dg_ffn2, dg_ple, dgqa, dgka, dgqb, dgkb, dtab, dsink = smalls

    fold = lambda a: a[:, :, 0, :].reshape(-1, HEAD_DIM).sum(axis=0)
    small_part = dict(
        ffn1_norm=dg_ffn1, mix_norm=dg_mix, ffn2_norm=dg_ffn2, ple_norm=dg_ple,
        a_q_norm=fold(dgqa), a_k_norm=fold(dgka), b_q_norm=fold(dgqb), b_k_norm=fold(dgkb),
        a_rel_bias=dtab,
        b_sinks=dsink.sum(axis=0)[:, 0, :GROUP].reshape(B_Q_HEADS),
        loss=loss_part[0, :1])
    zero1 = jnp.zeros((1,), F32)
    shapes = {nme: w[nme].shape for nme in SMALL_NAMES if nme != "loss"}
    shapes["loss"] = ()
    pk = lambda src: _pack_small({**{nme: src[nme] for nme in SMALL_NAMES if nme != "loss"}, "loss": zero1})
    sg, sd, sm, sv = _small_allreduce_adam(_pack_small(small_part), pk(w), pk(m), pk(v), "small_allreduce_adam")
    adam(finish(flight, sg, "ffn1_gu"))
    sg, sd, sm, sv = (_unpack_small(a, shapes) for a in (sg, sd, sm, sv))

    def pick(i):
        out = []
        for nme in WEIGHT_ORDER:
            out.append(big[nme][i] if nme in big else (sg, sd, sm, sv)[i][nme])
        return out

    return (sg["loss"], dh0.reshape(bl, s_len, d), *pick(0), *pick(1), *pick(2), *pick(3))
```

```python
import functools

import jax
import jax.numpy as jnp
import numpy as np
from jax import lax
from jax.experimental import pallas as pl
from jax.experimental.pallas import tpu as pltpu

F32 = jnp.float32
BF16 = jnp.bfloat16

CHUNK = 64
HEAD_DIM = 64
A_HEADS = 8
A_PREV = 8
A_MAX_REL = 128
B_Q_HEADS = 8
B_KV_HEADS = 2
B_PREV = 2
A_WIDTH = A_HEADS * HEAD_DIM
B_Q_WIDTH = B_Q_HEADS * HEAD_DIM
B_KV_WIDTH = B_KV_HEADS * HEAD_DIM
IN_COLS = 3 * A_WIDTH + B_Q_WIDTH + 2 * B_KV_WIDTH
EPS = 1e-6
NEG_INF = -1e30
ADAM_LR = 0.001
ADAM_B1 = 0.9
ADAM_B2 = 0.999
ADAM_EPS = 1e-08
ADAM_WD = 0.01
ADAM_STEP = 10

N_DEV = 8
LANES = 128
QTILE = 2 * CHUNK
VMEM_LIMIT = 56 * 1024 * 1024
ADAM_TILE_ELEMS = 256 * 1024

MESH_ID = pl.DeviceIdType.MESH
ANY = pl.BlockSpec(memory_space=pl.ANY)
HBM = pl.BlockSpec(memory_space=pltpu.HBM)
SEM = pl.BlockSpec(memory_space=pltpu.SEMAPHORE)
SIDE_EFFECT = pltpu.SideEffectType.DATAFLOW_SIDE_EFFECTING


def _dot(a, b):
    return jnp.dot(a, b, preferred_element_type=F32)


def _dot_nt(a, b):
    return lax.dot_general(a, b, (((1,), (1,)), ((), ())), preferred_element_type=F32)


def _dot_tn(a, b):
    return lax.dot_general(a, b, (((0,), (0,)), ((), ())), preferred_element_type=F32)


def _params(sem=None, vmem=VMEM_LIMIT):
    return pltpu.CompilerParams(dimension_semantics=sem, vmem_limit_bytes=vmem)


def _row_tile(t, want):
    while t % want:
        want //= 2
    return want


def _place():
    return lax.axis_index("x"), lax.axis_index("y"), lax.axis_index("c")


def _all_gather(shards, name):
    n = len(shards)

    def body(*refs):
        ins, outs = refs[:n], refs[n:2 * n]
        send_sems, recv_sems, local_sems = refs[2 * n:]
        x, y, c = _place()
        me, sib = (x, y, c), (x, y, 1 - c)
        chips = [(1 - x, y), (x, 1 - y), (1 - x, 1 - y)]

        def copy(w, k, block, to, src=None):
            px, py, pc = block
            dst = outs[w].at[4 * px + 2 * py + pc]
            return pltpu.make_async_remote_copy(
                src_ref=dst if src is None else src, dst_ref=dst,
                send_sem=send_sems.at[w * 7 + k], recv_sem=recv_sems.at[w * 7 + k],
                device_id=to, device_id_type=MESH_ID)

        mine = [pltpu.make_async_copy(ins[w], outs[w].at[4 * x + 2 * y + c], local_sems.at[w]) for w in range(n)]
        for cp in mine:
            cp.start()
        first = []
        for w in range(n):
            first.append(copy(w, 0, me, sib, src=ins[w]))
            first += [copy(w, 1 + j, me, (*chip, c), src=ins[w]) for j, chip in enumerate(chips)]
        for cp in first:
            cp.start()
        passed = []
        for j, chip in enumerate(chips):
            for w in range(n):
                copy(w, 1 + j, (*chip, c), me).wait_recv()
                fwd = copy(w, 4 + j, (*chip, c), sib)
                fwd.start()
                passed.append(fwd)
        for w in range(n):
            copy(w, 0, sib, me).wait_recv()
        for j, chip in enumerate(chips):
            for w in range(n):
                copy(w, 4 + j, (*chip, 1 - c), me).wait_recv()
        for cp in first + passed:
            cp.wait_send()
        for cp in mine:
            cp.wait()

    return pl.pallas_call(
        body, name=name,
        out_shape=[jax.ShapeDtypeStruct((N_DEV,) + s.shape, s.dtype) for s in shards],
        in_specs=[ANY] * n, out_specs=[ANY] * n,
        scratch_shapes=[pltpu.SemaphoreType.DMA((7 * n,)), pltpu.SemaphoreType.DMA((7 * n,)),
                        pltpu.SemaphoreType.DMA((n,))],
    )(*shards)


def _gather_level(bufs, send_sems, recv_sems, level, shards=None):
    x, y, c = _place()
    me, sib = (x, y, c), (x, y, 1 - c)
    chips = [(1 - x, y), (x, 1 - y), (1 - x, 1 - y)]

    def copy(w, k, block, to):
        px, py, pc = block
        rows = bufs[w].at[4 * px + 2 * py + pc]
        src = shards[w] if shards is not None and block is me else rows
        return pltpu.make_async_remote_copy(src_ref=src, dst_ref=rows, send_sem=send_sems.at[k], recv_sem=recv_sems.at[k],
                                            device_id=to, device_id_type=MESH_ID)

    n = len(bufs)
    own = []
    if level == 1:
        own = [pltpu.make_async_copy(bufs[w].at[4 * x + 2 * y + c] if shards is None else shards[w],
                                     bufs[w].at[4 * x + 2 * y + c], send_sems.at[4 * n + w]) for w in range(n)]
    out, arriving = [], []
    for w in range(len(bufs)):
        if level == 1:
            out.append(copy(w, 4 * w, me, sib))
            arriving.append(copy(w, 4 * w, sib, me))
        for j, chip in enumerate(chips):
            if level == 1:
                out.append(copy(w, 4 * w + 1 + j, me, (*chip, c)))
                arriving.append(copy(w, 4 * w + 1 + j, (*chip, c), me))
            else:
                out.append(copy(w, 3 * w + j, (*chip, c), sib))
                arriving.append(copy(w, 3 * w + j, (*chip, 1 - c), me))
    return out, arriving, own


def _split_call(body, name, bufs, sems_in, after, n_sems_out, token, extra=()):
    n = len(bufs)
    out_shape = [pltpu.SemaphoreType.DMA((n_sems_out,))] * (2 if n_sems_out else 0)
    out_shape += [pltpu.HBM(a.shape, a.dtype) for a in bufs]
    out_specs = [SEM] * (2 if n_sems_out else 0) + [HBM] * n
    if token:
        out_shape.append(jax.ShapeDtypeStruct((8, LANES), F32))
        out_specs.append(pl.BlockSpec(memory_space=pltpu.VMEM))
    first = 2 if n_sems_out else 0
    return pl.pallas_call(
        body, name=name, out_shape=tuple(out_shape),
        in_specs=[HBM] * (n + len(extra)) + [SEM] * len(sems_in) + [ANY], out_specs=tuple(out_specs),
        input_output_aliases={i: first + i for i in range(n)},
        compiler_params=pltpu.CompilerParams(has_side_effects=SIDE_EFFECT),
    )(*bufs, *extra, *sems_in, after)


def _gather_start(shards, after, name):
    n = len(shards)
    hbm = lambda a: pltpu.with_memory_space_constraint(a, pltpu.HBM)
    bufs = [hbm(lax.empty((N_DEV,) + s.shape, s.dtype)) for s in shards]

    def body(*refs):
        out, _, own = _gather_level(refs[:n], refs[2 * n + 1], refs[2 * n + 2], 1, shards=refs[n:2 * n])
        for cp in own + out:
            cp.start()
        refs[-1][...] = jnp.zeros_like(refs[-1])

    outs = _split_call(body, name, bufs + [hbm(s) for s in shards], [], after, 5 * n, True)
    return outs[0], outs[1], list(outs[2:2 + 2 * n]), outs[-1]


def _gather_pass(send1, recv1, bufs_and_shards, after, name):
    n = len(bufs_and_shards) // 2
    bufs = bufs_and_shards

    def body(*refs):
        refs = refs[:n] + refs[2 * n:]
        out1, in1, own = _gather_level(refs[:n], refs[n], refs[n + 1], 1)
        out2, _, _ = _gather_level(refs[:n], refs[n + 3], refs[n + 4], 2)
        for cp in in1:
            cp.wait_recv()
        for cp in out2:
            cp.start()
        for cp in out1:
            cp.wait_send()
        for cp in own:
            cp.wait()
        refs[-1][...] = jnp.zeros_like(refs[-1])

    outs = _split_call(body, name, bufs, [send1, recv1], after, 3 * n, True)
    return outs[0], outs[1], list(outs[2:2 + n]), outs[-1]


def _gather_wait(send2, recv2, bufs, after, name):
    n = len(bufs)

    def body(*refs):
        out2, in2, _ = _gather_level(refs[:n], refs[n], refs[n + 1], 2)
        for cp in in2:
            cp.wait_recv()
        for cp in out2:
            cp.wait_send()

    return list(_split_call(body, name, bufs, [send2, recv2], after, 0, False))


def _scatter_copies(parts, lands, send_sems, recv_sems):
    x, y, c = _place()
    cps = []
    for w, (part, land) in enumerate(zip(parts, lands)):
        for k in range(1, N_DEV):
            px, py, pc = x ^ ((k >> 2) & 1), y ^ ((k >> 1) & 1), c ^ (k & 1)
            cps.append(pltpu.make_async_remote_copy(
                src_ref=part.at[4 * px + 2 * py + pc], dst_ref=land.at[k - 1],
                send_sem=send_sems.at[7 * w + k - 1], recv_sem=recv_sems.at[7 * w + k - 1],
                device_id=(px, py, pc), device_id_type=MESH_ID))
    return cps


def _scatter_start(parts, after, name):
    n = len(parts)

    def body(*refs):
        ins, lands = refs[:n], refs[n:2 * n]
        send_sems, recv_sems = refs[2 * n + 1], refs[2 * n + 2]
        token = refs[-1]
        for cp in _scatter_copies(ins, lands, send_sems, recv_sems):
            cp.start()
        token[...] = jnp.zeros_like(token)

    land_shapes = [(N_DEV - 1,) + p.shape[1:] for p in parts]
    in_hbm = [pltpu.with_memory_space_constraint(p, pltpu.HBM) for p in parts]
    in_hbm += [pltpu.with_memory_space_constraint(lax.empty(s, p.dtype), pltpu.HBM) for s, p in zip(land_shapes, parts)]
    outs = pl.pallas_call(
        body, name=name,
        out_shape=(pltpu.SemaphoreType.DMA((7 * n,)), pltpu.SemaphoreType.DMA((7 * n,)),
                   *[pltpu.HBM(p.shape, p.dtype) for p in parts],
                   *[pltpu.HBM(s, p.dtype) for s, p in zip(land_shapes, parts)],
                   jax.ShapeDtypeStruct((8, LANES), F32)),
        in_specs=[HBM] * (2 * n) + [ANY],
        out_specs=(SEM, SEM, *[HBM] * (2 * n), pl.BlockSpec(memory_space=pltpu.VMEM)),
        input_output_aliases={i: 2 + i for i in range(2 * n)},
        compiler_params=pltpu.CompilerParams(has_side_effects=SIDE_EFFECT),
    )(*in_hbm, after)
    return outs[0], outs[1], list(outs[2:2 + n]), list(outs[2 + n:2 + 2 * n]), outs[-1]


def _scatter_wait(send_sems, recv_sems, parts, lands, after, name):
    n = len(parts)

    def body(*refs):
        ins, lnd = refs[:n], refs[n:2 * n]
        for cp in _scatter_copies(ins, lnd, refs[2 * n], refs[2 * n + 1]):
            cp.wait_send()
            cp.wait_recv()

    outs = pl.pallas_call(
        body, name=name,
        out_shape=tuple(pltpu.HBM(a.shape, a.dtype) for a in parts + lands),
        in_specs=[HBM] * (2 * n) + [SEM, SEM, ANY],
        out_specs=tuple([HBM] * (2 * n)),
        input_output_aliases={i: i for i in range(2 * n)},
        compiler_params=pltpu.CompilerParams(has_side_effects=SIDE_EFFECT),
    )(*parts, *lands, send_sems, recv_sems, after)
    return list(outs[n:])


def _adam(w, g, m, v):
    m2 = ADAM_B1 * m + (1.0 - ADAM_B1) * g
    v2 = ADAM_B2 * v + (1.0 - ADAM_B2) * (g * g)
    m_hat = m2 / (1.0 - ADAM_B1 ** ADAM_STEP)
    v_hat = v2 / (1.0 - ADAM_B2 ** ADAM_STEP)
    delta = -ADAM_LR * (m_hat / (jnp.sqrt(v_hat) + ADAM_EPS) + ADAM_WD * w)
    return delta, m2, v2


def _small_allreduce_adam(part, w, m, v, name):
    rows = part.shape[0]

    def body(p_ref, w_ref, m_ref, v_ref, g_ref, d_ref, mo_ref, vo_ref, buf, send_sems, recv_sems):
        x, y, c = _place()
        buf[0] = p_ref[...]
        cps = []
        for k in range(1, N_DEV):
            kx, ky, kc = (k >> 2) & 1, (k >> 1) & 1, k & 1
            peer = (x ^ kx, y ^ ky, c ^ kc)
            cps.append(pltpu.make_async_remote_copy(
                src_ref=p_ref, dst_ref=buf.at[k], send_sem=send_sems.at[k - 1], recv_sem=recv_sems.at[k - 1],
                device_id=peer, device_id_type=MESH_ID))
        for cp in cps:
            cp.start()
        for cp in cps:
            cp.wait()
        me = 4 * x + 2 * y + c
        total = buf[me]
        for d in range(1, N_DEV):
            total = total + buf[d ^ me]
        g_ref[...] = total
        delta, m2, v2 = _adam(w_ref[...], total, m_ref[...], v_ref[...])
        d_ref[...] = delta
        mo_ref[...] = m2
        vo_ref[...] = v2

    vm = pl.BlockSpec(memory_space=pltpu.VMEM)
    return pl.pallas_call(
        body, name=name,
        out_shape=[jax.ShapeDtypeStruct(part.shape, F32)] * 4,
        in_specs=[vm] * 4, out_specs=[vm] * 4,
        scratch_shapes=[pltpu.VMEM((N_DEV, rows, LANES), F32),
                        pltpu.SemaphoreType.DMA((N_DEV - 1,)), pltpu.SemaphoreType.DMA((N_DEV - 1,))],
    )(part, w, m, v)


def _final_adam(g8, land, w, m, v, me, dep, name):
    _, r, c = g8.shape
    tr = max(q for q in range(16, r + 1, 16) if r % q == 0 and q * c <= ADAM_TILE_ELEMS)

    def body(me_ref, g_ref, land_ref, w_ref, m_ref, v_ref, dep_ref, go_ref, d_ref, mo_ref, vo_ref):
        del dep_ref
        g = g_ref[...]
        for k in range(N_DEV - 1):
            g = g + land_ref[k].astype(F32)
        go_ref[...] = g
        delta, m2, v2 = _adam(w_ref[...], g, m_ref[...], v_ref[...])
        d_ref[...] = delta
        mo_ref[...] = m2
        vo_ref[...] = v2

    plain = pl.BlockSpec((tr, c), lambda i, s: (i, 0))
    return pl.pallas_call(
        body, name=name,
        out_shape=[jax.ShapeDtypeStruct((r, c), F32)] * 4,
        grid_spec=pltpu.PrefetchScalarGridSpec(
            num_scalar_prefetch=1, grid=(r // tr,),
            in_specs=[pl.BlockSpec((None, tr, c), lambda i, s: (s[0], i, 0)),
                      pl.BlockSpec((N_DEV - 1, tr, c), lambda i, s: (0, i, 0)),
                      plain, plain, plain, ANY],
            out_specs=[plain] * 4),
        compiler_params=_params(("arbitrary",)),
    )(me, g8, land, w, m, v, dep)


def _rms(x, gain):
    r = lax.rsqrt(jnp.mean(x * x, axis=-1, keepdims=True) + EPS)
    xh = x * r
    return xh * gain, xh, r


def _rms_bwd(xh, r, gain, dy):
    gdy = gain * dy
    dx = r * (gdy - xh * jnp.mean(xh * gdy, axis=-1, keepdims=True))
    return dx, jnp.sum(dy * xh, axis=0, keepdims=True)


def _load_weights(pairs, sems):
    cps = [pltpu.make_async_copy(src, dst, sems.at[i]) for i, (src, dst) in enumerate(pairs)]
    for cp in cps:
        cp.start()
    for cp in cps:
        cp.wait()


def _ffn_fwd(h, gain, wgu, wd, name):
    t, d = h.shape
    nb, nf, _ = wgu.shape
    nh = nb // 2
    tm = _row_tile(t, 512)

    def body(h_ref, g_ref, wgu_hbm, wd_hbm, out_ref, gu_ref, wgu_v, wd_v, sems):
        @pl.when(pl.program_id(0) == 0)
        def _():
            _load_weights([(wgu_hbm, wgu_v), (wd_hbm, wd_v)], sems)

        x = h_ref[...]
        n, _, _ = _rms(x, g_ref[...])
        nbf = n.astype(BF16)
        acc = jnp.zeros((tm, d), F32)
        for j in range(nh):
            g = _dot_nt(nbf, wgu_v[j])
            u = _dot_nt(nbf, wgu_v[j + nh])
            gu_ref[j] = g.astype(BF16)
            gu_ref[j + nh] = u.astype(BF16)
            a = (g * jax.nn.sigmoid(g)) * u
            acc = acc + _dot(a.astype(BF16), wd_v[j])
        out_ref[...] = x + 0.5 * acc

    return pl.pallas_call(
        body, name=name, grid=(t // tm,),
        out_shape=[jax.ShapeDtypeStruct((t, d), F32), jax.ShapeDtypeStruct((nb, t, nf), BF16)],
        in_specs=[pl.BlockSpec((tm, d), lambda i: (i, 0)), pl.BlockSpec((1, d), lambda i: (0, 0)), ANY, ANY],
        out_specs=[pl.BlockSpec((tm, d), lambda i: (i, 0)), pl.BlockSpec((nb, tm, nf), lambda i: (0, i, 0))],
        scratch_shapes=[pltpu.VMEM(wgu.shape, BF16), pltpu.VMEM(wd.shape, BF16), pltpu.SemaphoreType.DMA((2,))],
        compiler_params=_params(("arbitrary",)),
    )(h, gain, wgu, wd)


def _ffn_bwd(dh, h, gain, gu, wgu, wd, name):
    t, d = h.shape
    nb, nf, _ = wgu.shape
    nh = nb // 2
    tm = _row_tile(t, 256)

    def body(dh_ref, h_ref, g_ref, gu_ref, wgu_hbm, wd_hbm, dhp_ref, dgu_ref, a_ref, n_ref, dgain_ref,
             wgu_v, wd_v, sems):
        @pl.when(pl.program_id(0) == 0)
        def _():
            _load_weights([(wgu_hbm, wgu_v), (wd_hbm, wd_v)], sems)
            dgain_ref[...] = jnp.zeros_like(dgain_ref)

        x = h_ref[...]
        gain_v = g_ref[...]
        n, xh, r = _rms(x, gain_v)
        n_ref[...] = n.astype(BF16)
        dh_v = dh_ref[...]
        dfb = (0.5 * dh_v).astype(BF16)
        dn = jnp.zeros((tm, d), F32)
        for j in range(nh):
            da = _dot_nt(dfb, wd_v[j])
            g = gu_ref[j].astype(F32)
            u = gu_ref[j + nh].astype(F32)
            sg = jax.nn.sigmoid(g)
            si = g * sg
            dg = (da * u * (sg * (1.0 + g * (1.0 - sg)))).astype(BF16)
            du = (da * si).astype(BF16)
            a_ref[j] = (si * u).astype(BF16)
            dgu_ref[j] = dg
            dgu_ref[j + nh] = du
            dn = dn + _dot(dg, wgu_v[j]) + _dot(du, wgu_v[j + nh])
        dx, dgain = _rms_bwd(xh, r, gain_v, dn)
        dhp_ref[...] = dh_v + dx
        dgain_ref[...] += dgain

    row = pl.BlockSpec((tm, d), lambda i: (i, 0))
    vec = pl.BlockSpec((1, d), lambda i: (0, 0))
    return pl.pallas_call(
        body, name=name, grid=(t // tm,),
        out_shape=[jax.ShapeDtypeStruct((t, d), F32), jax.ShapeDtypeStruct((nb, t, nf), BF16),
                   jax.ShapeDtypeStruct((nh, t, nf), BF16), jax.ShapeDtypeStruct((t, d), BF16),
                   jax.ShapeDtypeStruct((1, d), F32)],
        in_specs=[row, row, vec, pl.BlockSpec((nb, tm, nf), lambda i: (0, i, 0)), ANY, ANY],
        out_specs=[row, pl.BlockSpec((nb, tm, nf), lambda i: (0, i, 0)),
                   pl.BlockSpec((nh, tm, nf), lambda i: (0, i, 0)), row, vec],
        scratch_shapes=[pltpu.VMEM(wgu.shape, BF16), pltpu.VMEM(wd.shape, BF16), pltpu.SemaphoreType.DMA((2,))],
        compiler_params=_params(("arbitrary",)),
    )(dh, h, gain, gu, wgu, wd)


def _dw(xa, dy, nb, n, name, scale=1.0, dep=None):
    t, k = xa.shape[-2:]
    tt = _row_tile(t, 512)
    steps = t // tt
    wide = dy.ndim == 2 and xa.ndim == 2
    if xa.ndim == 3:
        x_spec = pl.BlockSpec((nb, tt, k), lambda i: (0, i, 0))
    else:
        x_spec = pl.BlockSpec((tt, k), lambda i: (i, 0))
    if dy.ndim == 3:
        dy_spec = pl.BlockSpec((nb, tt, n), lambda i: (0, i, 0))
    else:
        dy_spec = pl.BlockSpec((tt, dy.shape[1]), lambda i: (i, 0))
    acc_shape = (k, nb * n) if wide else (nb, k, n)
    stage_shape = (k, nb * n) if wide else (k, n)

    def body(x_ref, dy_ref, *rest):
        o_hbm, ob_hbm, acc, stage, sems = rest[-5:]

        @pl.when(pl.program_id(0) == 0)
        def _():
            acc[...] = jnp.zeros_like(acc)

        if wide:
            acc[...] += _dot(x_ref[...].astype(BF16).T, dy_ref[...].astype(BF16))
        elif xa.ndim == 2:
            xt = x_ref[...].astype(BF16).T
            for j in range(nb):
                acc[j] += _dot(xt, dy_ref[j].astype(BF16))
        else:
            dyb = dy_ref[...].astype(BF16)
            for j in range(nb):
                acc[j] += _dot_tn(x_ref[j].astype(BF16), dyb)

        @pl.when(pl.program_id(0) == steps - 1)
        def _():
            if scale != 1.0:
                acc[...] = acc[...] * scale
            if wide:
                cps = [pltpu.make_async_copy(acc.at[:, pl.ds(j * n, n)] if nb > 1 else acc, o_hbm.at[j], sems.at[j])
                       for j in range(nb)]
            else:
                cps = [pltpu.make_async_copy(acc, o_hbm, sems.at[0])]
            for cp in cps:
                cp.start()
            if wide:
                stage[...] = acc[...].astype(BF16)
                bcs = [pltpu.make_async_copy(stage.at[:, pl.ds(j * n, n)] if nb > 1 else stage, ob_hbm.at[j],
                                             sems.at[nb + j]) for j in range(nb)]
                for cp in bcs:
                    cp.start()
                for cp in bcs:
                    cp.wait()
            else:
                for j in range(nb):
                    stage[...] = acc[j].astype(BF16)
                    cp = pltpu.make_async_copy(stage, ob_hbm.at[j], sems.at[nb])
                    cp.start()
                    cp.wait()
            for cp in cps:
                cp.wait()

    return pl.pallas_call(
        body, name=name, grid=(steps,),
        out_shape=[jax.ShapeDtypeStruct((nb, k, n), F32), jax.ShapeDtypeStruct((nb, k, n), BF16)],
        in_specs=[x_spec, dy_spec] + ([] if dep is None else [ANY]),
        out_specs=[ANY, ANY],
        scratch_shapes=[pltpu.VMEM(acc_shape, F32), pltpu.VMEM(stage_shape, BF16),
                        pltpu.SemaphoreType.DMA((2 * nb,))],
        compiler_params=_params(("arbitrary",)),
    )(*((xa, dy) if dep is None else (xa, dy, dep)))


def _proj_fwd(h, gain, win, wgate, name):
    t, d = h.shape
    tm = _row_tile(t, 256)
    nq, ng = win.shape[0], wgate.shape[1]

    def body(h_ref, g_ref, win_ref, wg_ref, un_ref, qkv_ref, gate_ref):
        n, _, _ = _rms(h_ref[...], g_ref[...])
        nbf = n.astype(BF16)
        un_ref[...] = nbf
        qkv_ref[...] = _dot_nt(nbf, win_ref[...])
        gate_ref[...] = jax.nn.sigmoid(_dot(nbf, wg_ref[...]))

    full = lambda a: pl.BlockSpec(a.shape, lambda i: (0,) * a.ndim)
    return pl.pallas_call(
        body, name=name, grid=(t // tm,),
        out_shape=[jax.ShapeDtypeStruct((t, d), BF16), jax.ShapeDtypeStruct((t, nq), F32),
                   jax.ShapeDtypeStruct((t, ng), F32)],
        in_specs=[pl.BlockSpec((tm, d), lambda i: (i, 0)), full(gain), full(win), full(wgate)],
        out_specs=[pl.BlockSpec((tm, d), lambda i: (i, 0)), pl.BlockSpec((tm, nq), lambda i: (i, 0)),
                   pl.BlockSpec((tm, ng), lambda i: (i, 0))],
        compiler_params=_params(("arbitrary",)),
    )(h, gain, win, wgate)


def _proj_bwd(dh, h, gain, dzg, dqkv, win, wgate, name):
    t, d = h.shape
    tm = _row_tile(t, 256)
    nq, ng = win.shape[0], wgate.shape[1]

    def body(dh_ref, h_ref, g_ref, dzg_ref, dqkv_ref, win_ref, wg_ref, dhp_ref, dgain_ref):
        @pl.when(pl.program_id(0) == 0)
        def _():
            dgain_ref[...] = jnp.zeros_like(dgain_ref)

        gain_v = g_ref[...]
        _, xh, r = _rms(h_ref[...], gain_v)
        dun = _dot_nt(dzg_ref[...], wg_ref[...]) + _dot(dqkv_ref[...].astype(BF16), win_ref[...])
        dx, dgain = _rms_bwd(xh, r, gain_v, dun)
        dhp_ref[...] = dh_ref[...] + dx
        dgain_ref[...] += dgain

    full = lambda a: pl.BlockSpec(a.shape, lambda i: (0,) * a.ndim)
    row = pl.BlockSpec((tm, d), lambda i: (i, 0))
    return pl.pallas_call(
        body, name=name, grid=(t // tm,),
        out_shape=[jax.ShapeDtypeStruct((t, d), F32), jax.ShapeDtypeStruct((1, d), F32)],
        in_specs=[row, row, full(gain), pl.BlockSpec((tm, ng), lambda i: (i, 0)),
                  pl.BlockSpec((tm, nq), lambda i: (i, 0)), full(win), full(wgate)],
        out_specs=[row, pl.BlockSpec((1, d), lambda i: (0, 0))],
        compiler_params=_params(("arbitrary",)),
    )(dh, h, gain, dzg, dqkv, win, wgate)


def _merge_fwd(h, ya, yb, gate, wpa, wpb, wout, name):
    t, d = h.shape
    tm = _row_tile(t, 256)

    def body(h_ref, ya_ref, yb_ref, ga_ref, gb_ref, wpa_ref, wpb_ref, wout_ref, out_ref, mg_ref, pa_ref, pb_ref):
        pa = _dot(ya_ref[...].astype(BF16), wpa_ref[...])
        pb = _dot(yb_ref[...].astype(BF16), wpb_ref[...])
        merged = (ga_ref[...] * pa + gb_ref[...] * pb).astype(BF16)
        pa_ref[...] = pa.astype(BF16)
        pb_ref[...] = pb.astype(BF16)
        mg_ref[...] = merged
        out_ref[...] = h_ref[...] + _dot(merged, wout_ref[...])

    full = lambda a: pl.BlockSpec(a.shape, lambda i: (0,) * a.ndim)
    row = pl.BlockSpec((tm, d), lambda i: (i, 0))
    yrow = pl.BlockSpec((tm, ya.shape[1]), lambda i: (i, 0))
    return pl.pallas_call(
        body, name=name, grid=(t // tm,),
        out_shape=[jax.ShapeDtypeStruct((t, d), F32)] + [jax.ShapeDtypeStruct((t, d), BF16)] * 3,
        in_specs=[row, yrow, yrow, pl.BlockSpec((tm, d), lambda i: (i, 0)), pl.BlockSpec((tm, d), lambda i: (i, 1)),
                  full(wpa), full(wpb), full(wout)],
        out_specs=[row] * 4,
        compiler_params=_params(("arbitrary",)),
    )(h, ya, yb, gate, gate, wpa, wpb, wout)


def _merge_bwd(dh, pa, pb, gate, wpa, wpb, wout, name):
    t, d = dh.shape
    tm = _row_tile(t, 256)
    wy = wpa.shape[0]

    def body(dh_ref, pa_ref, pb_ref, ga_ref, gb_ref, wpa_ref, wpb_ref, wout_ref,
             dpa_ref, dpb_ref, dzg_ref, dya_ref, dyb_ref):
        dm = _dot_nt(dh_ref[...].astype(BF16), wout_ref[...])
        ga, gb = ga_ref[...], gb_ref[...]
        dpa = (dm * ga).astype(BF16)
        dpb = (dm * gb).astype(BF16)
        dpa_ref[...] = dpa
        dpb_ref[...] = dpb
        dzg_ref[:, :d] = (dm * pa_ref[...].astype(F32) * ga * (1.0 - ga)).astype(BF16)
        dzg_ref[:, d:] = (dm * pb_ref[...].astype(F32) * gb * (1.0 - gb)).astype(BF16)
        dya_ref[...] = _dot_nt(dpa, wpa_ref[...])
        dyb_ref[...] = _dot_nt(dpb, wpb_ref[...])

    full = lambda a: pl.BlockSpec(a.shape, lambda i: (0,) * a.ndim)
    row = pl.BlockSpec((tm, d), lambda i: (i, 0))
    yrow = pl.BlockSpec((tm, wy), lambda i: (i, 0))
    return pl.pallas_call(
        body, name=name, grid=(t // tm,),
        out_shape=[jax.ShapeDtypeStruct((t, d), BF16), jax.ShapeDtypeStruct((t, d), BF16),
                   jax.ShapeDtypeStruct((t, 2 * d), BF16), jax.ShapeDtypeStruct((t, wy), F32),
                   jax.ShapeDtypeStruct((t, wy), F32)],
        in_specs=[row, row, row, pl.BlockSpec((tm, d), lambda i: (i, 0)), pl.BlockSpec((tm, d), lambda i: (i, 1)),
                  full(wpa), full(wpb), full(wout)],
        out_specs=[row, row, pl.BlockSpec((tm, 2 * d), lambda i: (i, 0)), yrow, yrow],
        compiler_params=_params(("arbitrary",)),
    )(dh, pa, pb, gate, gate, wpa, wpb, wout)


def _ple_loss(h, gain, p, target, wpg, wpe, name):
    t, d = h.shape
    tm = _row_tile(t, 256)
    pd = p.shape[1]

    def body(h_ref, g_ref, p_ref, t_ref, wpg_ref, wpe_ref, dh_ref, dz_ref, dpp_ref, n_ref, dgain_ref, loss_ref):
        @pl.when(pl.program_id(0) == 0)
        def _():
            dgain_ref[...] = jnp.zeros_like(dgain_ref)
            loss_ref[...] = jnp.zeros_like(loss_ref)

        x = h_ref[...]
        gain_v = g_ref[...]
        n, xh, r = _rms(x, gain_v)
        nbf = n.astype(BF16)
        n_ref[...] = nbf
        pg = jax.nn.sigmoid(_dot(nbf, wpg_ref[...]))
        pp = _dot(p_ref[...].astype(BF16), wpe_ref[...])
        err = (x + pg * pp) - t_ref[...]
        loss_ref[...] += 0.5 * jnp.sum(jnp.mean(err * err, axis=-1, keepdims=True))
        dy = err * (1.0 / d)
        dpp_ref[...] = (dy * pg).astype(BF16)
        dz = (dy * pp * pg * (1.0 - pg)).astype(BF16)
        dz_ref[...] = dz
        dn = _dot_nt(dz, wpg_ref[...])
        dx, dgain = _rms_bwd(xh, r, gain_v, dn)
        dh_ref[...] = dy + dx
        dgain_ref[...] += dgain

    full = lambda a: pl.BlockSpec(a.shape, lambda i: (0,) * a.ndim)
    row = pl.BlockSpec((tm, d), lambda i: (i, 0))
    return pl.pallas_call(
        body, name=name, grid=(t // tm,),
        out_shape=[jax.ShapeDtypeStruct((t, d), F32), jax.ShapeDtypeStruct((t, d), BF16),
                   jax.ShapeDtypeStruct((t, d), BF16), jax.ShapeDtypeStruct((t, d), BF16),
                   jax.ShapeDtypeStruct((1, d), F32), jax.ShapeDtypeStruct((8, LANES), F32)],
        in_specs=[row, full(gain), pl.BlockSpec((tm, pd), lambda i: (i, 0)), row, full(wpg), full(wpe)],
        out_specs=[row, row, row, row, pl.BlockSpec((1, d), lambda i: (0, 0)),
                   pl.BlockSpec((8, LANES), lambda i: (0, 0))],
        compiler_params=_params(("arbitrary",)),
    )(h, gain, p, target, wpg, wpe)


def _head_masks():
    lane = lax.broadcasted_iota(jnp.int32, (1, LANES), 1)
    m0 = (lane < HEAD_DIM).astype(F32)
    return m0, 1.0 - m0


def _head_mean(v, m0, m1):
    del m0, m1
    width = v.shape[-1]
    shift = HEAD_DIM.bit_length() - 1
    r = jnp.right_shift(lax.broadcasted_iota(jnp.int32, (width, width), 0), shift)
    c = jnp.right_shift(lax.broadcasted_iota(jnp.int32, (width, width), 1), shift)
    same_head = (r == c).astype(BF16)
    hi = v.astype(BF16)
    lo = (v - hi.astype(F32)).astype(BF16)
    return (_dot(hi, same_head) + _dot(lo, same_head)) * (1.0 / HEAD_DIM)


def _head_norm(x, gain, m0, m1):
    r = lax.rsqrt(_head_mean(x * x, m0, m1) + EPS)
    xh = x * r
    return xh * gain, xh, r


def _head_norm_bwd(xh, r, gain, dy, m0, m1):
    gdy = gain * dy
    dx = r * (gdy - xh * _head_mean(xh * gdy, m0, m1))
    return dx, jnp.sum(dy * xh, axis=0, keepdims=True)


GROUP = 4
QW = GROUP * HEAD_DIM
STACK = GROUP * QTILE


def _kv_width(mode):
    return QW if mode == "A" else LANES


def _q_scratch_shape(mode, s_len):
    return (s_len, QW) if mode == "A" else (GROUP * s_len, LANES)


def _group_masks(dtype=F32):
    lane = lax.broadcasted_iota(jnp.int32, (1, QW), 1)
    return [((lane >= h * HEAD_DIM) & (lane < (h + 1) * HEAD_DIM)).astype(dtype) for h in range(GROUP)]


def _stack_heads(first_kv, x, m0, m1):
    out = []
    for half in range(GROUP // 2):
        xh = x[:, half * LANES:(half + 1) * LANES]
        a0, a1 = xh * m0, xh * m1
        r0, r1 = pltpu.roll(a0, HEAD_DIM, 1), pltpu.roll(a1, HEAD_DIM, 1)
        out += [jnp.where(first_kv, a0, r0), jnp.where(first_kv, r1, a1)]
    return out


def _unstack_heads(mode, first_kv, ts, m0, m1):
    if mode == "A":
        masks = _group_masks()
        return sum(t * mk for t, mk in zip(ts, masks))
    halves = []
    for half in range(GROUP // 2):
        t0 = jnp.where(first_kv, ts[2 * half], pltpu.roll(ts[2 * half], HEAD_DIM, 1))
        t1 = jnp.where(first_kv, pltpu.roll(ts[2 * half + 1], HEAD_DIM, 1), ts[2 * half + 1])
        halves.append(t0 * m0 + t1 * m1)
    return jnp.concatenate(halves, axis=1)


def _store_stacked(dst, i, heads):
    for half in range(2):
        rows = slice(half * QTILE, (half + 1) * QTILE)
        for h, x in enumerate(heads):
            dst[pl.ds((2 * i + half) * STACK + h * QTILE, QTILE), :] = x[rows].astype(dst.dtype)


def _load_stacked(mode, ref, m):
    if mode == "B":
        return ref[pl.ds(pl.multiple_of(m * STACK, STACK), STACK), :]
    x = ref[pl.ds(pl.multiple_of(m * QTILE, QTILE), QTILE), :]
    return jnp.concatenate([x * mk for mk in _group_masks(x.dtype)], axis=0)


def _attn_prep(mode, group, s_len, padk, q_ref, k_ref, v_ref, gq_ref, gk_ref, qs, k2, v2, do_ref=None, dos=None):
    m0, m1 = _head_masks()
    zpad = jnp.zeros((padk, k2.shape[1]), BF16)
    k2[pl.ds(0, padk), :] = zpad
    v2[pl.ds(0, padk), :] = zpad
    first_kv = group == 0
    rt = 2 * QTILE
    for i in range(s_len // rt):
        rows = pl.ds(i * rt, rt)
        qn, _, _ = _head_norm(q_ref[rows, :], gq_ref[...], m0, m1)
        kn, _, _ = _head_norm(k_ref[rows, :], gk_ref[...], m0, m1)
        qn = qn * (HEAD_DIM ** -0.5)
        if mode == "A":
            qs[rows, :] = qn.astype(BF16)
            if dos is not None:
                dos[rows, :] = do_ref[rows, :].astype(BF16)
        else:
            _store_stacked(qs, i, _stack_heads(first_kv, qn, m0, m1))
            if dos is not None:
                _store_stacked(dos, i, _stack_heads(first_kv, do_ref[rows, :], m0, m1))
        k2[pl.ds(padk + i * rt, rt), :] = kn.astype(BF16)
        v2[pl.ds(padk + i * rt, rt), :] = v_ref[rows, :].astype(BF16)


def _attn_probs(mode, q_st, kb, bias, ok, sink):
    s = _dot_nt(q_st, kb) + bias
    s = jnp.where(ok, s, NEG_INF)
    mx = jnp.max(s, axis=-1, keepdims=True)
    if mode == "B":
        mx = jnp.maximum(mx, sink)
    e = jnp.exp(s - mx)
    l = jnp.sum(e, axis=-1, keepdims=True)
    if mode == "B":
        l = l + jnp.exp(sink - mx)
    return e, mx, l


def _sink_column(sink_ref, group):
    row = lax.broadcasted_iota(jnp.int32, (STACK, 1), 0)
    col = jnp.zeros((STACK, 1), F32)
    for h in range(GROUP):
        col = jnp.where((row >= h * QTILE) & (row < (h + 1) * QTILE), sink_ref[GROUP * group + h], col)
    return col


def _head_deltas(dd, m0, m1):
    cols = []
    for half in range(GROUP // 2):
        dh = dd[:, half * LANES:(half + 1) * LANES]
        cols += [jnp.sum(dh * m0, axis=-1, keepdims=True), jnp.sum(dh * m1, axis=-1, keepdims=True)]
    return jnp.concatenate(cols, axis=0)


def _attn_cols(mode):
    if mode == "A":
        return (lambda b, g: (b, g)), (lambda b, g: (b, 2 + g)), (lambda b, g: (b, 4 + g))
    return (lambda b, g: (b, 6 + g)), (lambda b, g: (b, 16)), (lambda b, g: (b, 17))


def _attn_fwd(mode, qkv, gq, gk, bias, sinks, bl, s_len, name):
    bw = bias.shape[-1]
    padk = bw - QTILE
    nt = s_len // QTILE
    qmap, kmap, vmap = _attn_cols(mode)

    kw = _kv_width(mode)

    def body(q_ref, k_ref, v_ref, gq_ref, gk_ref, bias_ref, sink_ref, o_ref, qs, k2, v2):
        group = pl.program_id(1)
        m0, m1 = _head_masks()
        first_kv = group == 0
        _attn_prep(mode, group, s_len, padk, q_ref, k_ref, v_ref, gq_ref, gk_ref, qs, k2, v2)
        col = lax.broadcasted_iota(jnp.int32, (STACK, bw), 1)
        sink = _sink_column(sink_ref, group)

        def tile(m, carry):
            r0 = pl.multiple_of(m * QTILE, QTILE)
            q_st = _load_stacked(mode, qs, m)
            ok = col >= (padk - r0)
            e, _, l = _attn_probs(mode, q_st, k2[pl.ds(r0, bw), :], bias_ref[...], ok, sink)
            o_st = _dot(e.astype(BF16), v2[pl.ds(r0, bw), :]) / l
            heads = [o_st[h * QTILE:(h + 1) * QTILE] for h in range(GROUP)]
            o_ref[pl.ds(r0, QTILE), :] = _unstack_heads(mode, first_kv, heads, m0, m1)
            return carry

        lax.fori_loop(0, nt, tile, 0, unroll=2)

    blk = lambda w, f: pl.BlockSpec((s_len, w), f)
    return pl.pallas_call(
        body, name=name, grid=(bl, B_Q_HEADS // GROUP),
        out_shape=jax.ShapeDtypeStruct((bl * s_len, B_Q_HEADS * HEAD_DIM), F32),
        in_specs=[blk(QW, qmap), blk(kw, kmap), blk(kw, vmap),
                  pl.BlockSpec((1, QW), lambda b, g: (0, 0)), pl.BlockSpec((1, kw), lambda b, g: (0, 0)),
                  pl.BlockSpec((STACK, bw), lambda b, g: (g, 0)),
                  pl.BlockSpec(memory_space=pltpu.SMEM)],
        out_specs=blk(QW, lambda b, g: (b, g)),
        scratch_shapes=[pltpu.VMEM(_q_scratch_shape(mode, s_len), BF16)] + [pltpu.VMEM((s_len + padk, kw), BF16)] * 2,
        compiler_params=_params(("arbitrary", "arbitrary")),
    )(qkv, qkv, qkv, gq, gk, bias.reshape(-1, bw), sinks)


def _attn_bwd(mode, qkv, gq, gk, bias, sinks, y, dy, bl, s_len, name):
    bw = bias.shape[-1]
    padk = bw - QTILE
    nt = s_len // QTILE
    qmap, kmap, vmap = _attn_cols(mode)
    t = bl * s_len
    kw = _kv_width(mode)
    kvw = 4 * LANES if mode == "A" else LANES

    def body(q_ref, k_ref, v_ref, gq_ref, gk_ref, bias_ref, sink_ref, y_ref, dy_ref,
             dq_ref, dk_ref, dv_ref, dgq_ref, dgk_ref, dbias_ref, dsink_ref,
             qs, k2, v2, dos, dqs, dk, dv):
        group = pl.program_id(1)
        m0, m1 = _head_masks()
        first_kv = group == 0
        _attn_prep(mode, group, s_len, padk, q_ref, k_ref, v_ref, gq_ref, gk_ref, qs, k2, v2, dy_ref, dos)
        dk[...] = jnp.zeros_like(dk)
        dv[...] = jnp.zeros_like(dv)
        dbias_ref[...] = jnp.zeros_like(dbias_ref)
        col = lax.broadcasted_iota(jnp.int32, (STACK, bw), 1)
        lane8 = lax.broadcasted_iota(jnp.int32, (8, LANES), 1)
        sink = _sink_column(sink_ref, group)

        def tile(m, dsink):
            r0 = pl.multiple_of(m * QTILE, QTILE)
            rows = pl.ds(r0, QTILE)
            band = pl.ds(r0, bw)
            q_st = _load_stacked(mode, qs, m)
            do_st = _load_stacked(mode, dos, m)
            delta = _head_deltas(dy_ref[rows, :] * y_ref[rows, :], m0, m1)
            ok = col >= (padk - r0)
            kb = k2[band, :]
            e, mx, l = _attn_probs(mode, q_st, kb, bias_ref[...], ok, sink)
            inv = 1.0 / l
            pn = e * inv
            dp = _dot_nt(do_st, v2[band, :])
            ds = pn * (dp - delta)
            if mode == "A":
                dbias_ref[...] += ds
            else:
                part = jnp.exp(sink - mx) * inv * delta
                for h in range(GROUP):
                    dsink = dsink - jnp.where(lane8 == h, jnp.sum(part[h * QTILE:(h + 1) * QTILE]), 0.0)
            dsb = ds.astype(BF16)
            dv[band, :] += _dot_tn(pn.astype(BF16), do_st)
            dk[band, :] += _dot_tn(dsb, q_st)
            dq_st = _dot(dsb, kb)
            if mode == "A":
                heads = [dq_st[h * QTILE:(h + 1) * QTILE] for h in range(GROUP)]
                dqs[rows, :] = _unstack_heads(mode, first_kv, heads, m0, m1)
            else:
                dqs[pl.ds(pl.multiple_of(m * STACK, STACK), STACK), :] = dq_st
            return dsink

        dsink = lax.fori_loop(0, nt, tile, jnp.zeros((8, LANES), F32), unroll=2)
        dsink_ref[...] = dsink

        rt = 2 * QTILE
        dgq = jnp.zeros((1, QW), F32)
        dgk = jnp.zeros((1, kw), F32)
        for i in range(s_len // rt):
            rows = pl.ds(i * rt, rt)
            src = pl.ds(padk + i * rt, rt)
            gq_v, gk_v = gq_ref[...], gk_ref[...]
            _, qh, qr = _head_norm(q_ref[rows, :], gq_v, m0, m1)
            _, kh, kr = _head_norm(k_ref[rows, :], gk_v, m0, m1)
            if mode == "A":
                dqn = dqs[rows, :] * (HEAD_DIM ** -0.5)
            else:
                dqn = jnp.concatenate(
                    [_unstack_heads(mode, first_kv, [dqs[pl.ds((2 * i + half) * STACK + h * QTILE, QTILE), :]
                                                     for h in range(GROUP)], m0, m1)
                     for half in range(2)], axis=0) * (HEAD_DIM ** -0.5)
            dq_raw, dgq_i = _head_norm_bwd(qh, qr, gq_v, dqn, m0, m1)
            dk_raw, dgk_i = _head_norm_bwd(kh, kr, gk_v, dk[src, :], m0, m1)
            dvn = dv[src, :]
            dq_ref[rows, :] = dq_raw
            if mode == "A":
                dk_ref[rows, :] = dk_raw
                dv_ref[rows, :] = dvn
            else:
                @pl.when(group == 0)
                def _():
                    dk_ref[rows, :] = dk_raw
                    dv_ref[rows, :] = dvn

                @pl.when(group != 0)
                def _():
                    dk_ref[rows, :] += dk_raw
                    dv_ref[rows, :] += dvn
            dgq, dgk = dgq + dgq_i, dgk + dgk_i
        dgq_ref[...] = jnp.broadcast_to(dgq, (8, QW))
        dgk_ref[...] = jnp.broadcast_to(dgk, (8, kw))

    ng = B_Q_HEADS // GROUP
    blk = lambda w, f: pl.BlockSpec((s_len, w), f)
    small = lambda w: pl.BlockSpec((None, None, 8, w), lambda b, g: (b, g, 0, 0))
    own = lambda b, g: (b, g)
    kvmap = own if mode == "A" else (lambda b, g: (b, 0))
    pad_f32 = pltpu.VMEM((s_len + padk, kw), F32)
    pad_bf = pltpu.VMEM((s_len + padk, kw), BF16)
    stack_bf = pltpu.VMEM(_q_scratch_shape(mode, s_len), BF16)
    outs = pl.pallas_call(
        body, name=name, grid=(bl, ng),
        out_shape=[jax.ShapeDtypeStruct((t, ng * QW), F32), jax.ShapeDtypeStruct((t, kvw), F32),
                   jax.ShapeDtypeStruct((t, kvw), F32),
                   jax.ShapeDtypeStruct((bl, ng, 8, QW), F32), jax.ShapeDtypeStruct((bl, ng, 8, kw), F32),
                   jax.ShapeDtypeStruct((bl, ng * STACK, bw), F32), jax.ShapeDtypeStruct((bl, ng, 8, LANES), F32)],
        in_specs=[blk(QW, qmap), blk(kw, kmap), blk(kw, vmap),
                  pl.BlockSpec((1, QW), lambda b, g: (0, 0)), pl.BlockSpec((1, kw), lambda b, g: (0, 0)),
                  pl.BlockSpec((STACK, bw), lambda b, g: (g, 0)),
                  pl.BlockSpec(memory_space=pltpu.SMEM),
                  blk(QW, own), blk(QW, own)],
        out_specs=[blk(QW, own), blk(kw, kvmap), blk(kw, kvmap), small(QW), small(kw),
                   pl.BlockSpec((None, STACK, bw), lambda b, g: (b, g, 0)), small(LANES)],
        scratch_shapes=[stack_bf, pad_bf, pad_bf, stack_bf, pltpu.VMEM(_q_scratch_shape(mode, s_len), F32),
                        pad_f32, pad_f32],
        compiler_params=_params(("arbitrary", "arbitrary")),
    )(qkv, qkv, qkv, gq, gk, bias.reshape(-1, bw), sinks, y, dy)
    outs = list(outs)
    outs[5] = outs[5].reshape(bl, B_Q_HEADS, QTILE, bw)
    return outs


def _band_geometry(prev):
    bw = QTILE + prev * CHUNK
    i = np.arange(QTILE)[:, None]
    j = np.arange(bw)[None, :]
    dist = i + prev * CHUNK - j
    valid = (j // CHUNK >= i // CHUNK) & (j // CHUNK <= i // CHUNK + prev)
    return dist, valid


A_VAR0 = (A_PREV * CHUNK - A_MAX_REL) // LANES * LANES


A_NVAR = QTILE + A_PREV * CHUNK - A_VAR0


def _skew_rows(x, sign):
    rows, n = x.shape
    row = lax.broadcasted_iota(jnp.int32, x.shape, 0)
    b = 1
    while b < rows:
        x = jnp.where((row & b) != 0, pltpu.roll(x, (sign * b) % n, 1), x)
        b *= 2
    return x


def _rel_bias_expand(table, name):
    _, valid = _band_geometry(A_PREV)
    bw = valid.shape[1]
    valid_f = jnp.asarray(valid.astype(np.float32))
    rev = jnp.flip(table[:, 1:], axis=1).reshape(A_HEADS, 1, A_NVAR)

    def body(rev_ref, valid_ref, o_ref):
        rowv = jnp.broadcast_to(rev_ref[...], (QTILE, A_NVAR))
        top = rowv[:, 0:1]
        var = _skew_rows(rowv, 1)
        row = lax.broadcasted_iota(jnp.int32, (QTILE, A_NVAR), 0)
        colv = lax.broadcasted_iota(jnp.int32, (QTILE, A_NVAR), 1)
        var = jnp.where(colv < row, top, var)
        ok = valid_ref[...] > 0.5
        o_ref[:, :A_VAR0] = jnp.where(ok[:, :A_VAR0], top, NEG_INF)
        o_ref[:, A_VAR0:] = jnp.where(ok[:, A_VAR0:], var, NEG_INF)

    return pl.pallas_call(
        body, name=name, grid=(A_HEADS,),
        out_shape=jax.ShapeDtypeStruct((A_HEADS, QTILE, bw), F32),
        in_specs=[pl.BlockSpec((None, 1, A_NVAR), lambda h: (h, 0, 0)), pl.BlockSpec((QTILE, bw), lambda h: (0, 0))],
        out_specs=pl.BlockSpec((None, QTILE, bw), lambda h: (h, 0, 0)),
        compiler_params=_params(("arbitrary",)),
    )(rev, valid_f)


def _rel_bias_grad(dbias, name):
    bl = dbias.shape[0]
    bw = dbias.shape[-1]

    def body(db_ref, o_ref):
        g = db_ref[0]
        for b in range(1, bl):
            g = g + db_ref[b]
        sk = _skew_rows(g[:, A_VAR0:], -1)
        row = lax.broadcasted_iota(jnp.int32, (QTILE, A_NVAR), 0)
        colv = lax.broadcasted_iota(jnp.int32, (QTILE, A_NVAR), 1)
        wrapped = (row + colv) >= A_NVAR
        main = jnp.sum(jnp.where(wrapped, 0.0, sk), axis=0, keepdims=True)
        top = jnp.sum(g[:, :A_VAR0]) + jnp.sum(jnp.where(wrapped, sk, 0.0))
        o_ref[:, :A_NVAR] = jnp.broadcast_to(main, (8, A_NVAR))
        o_ref[:, A_NVAR:] = jnp.full((8, LANES), top, F32)

    out = pl.pallas_call(
        body, name=name, grid=(A_HEADS,),
        out_shape=jax.ShapeDtypeStruct((A_HEADS, 8, A_NVAR + LANES), F32),
        in_specs=[pl.BlockSpec((bl, None, QTILE, bw), lambda h: (0, h, 0, 0))],
        out_specs=pl.BlockSpec((None, 8, A_NVAR + LANES), lambda h: (h, 0, 0)),
        compiler_params=_params(("arbitrary",)),
    )(dbias)
    main, top = out[:, 0, :A_NVAR], out[:, 0, A_NVAR]
    fm = jnp.flip(main, axis=1)
    return jnp.concatenate([jnp.zeros((A_HEADS, 1), F32), fm[:, :-1], fm[:, -1:] + top[:, None]], axis=1)


def _alibi_bias():
    dist, valid = _band_geometry(B_PREV)
    slopes = np.array([2.0 ** (-8.0 * (h + 1) / B_Q_HEADS) for h in range(B_Q_HEADS)], dtype=np.float32)
    bias = -slopes[:, None, None] * np.abs(dist).astype(np.float32)[None]
    return jnp.asarray(np.where(valid[None], bias, np.float32(NEG_INF)).astype(np.float32))


SMALL_NAMES = ("ffn1_norm", "mix_norm", "ffn2_norm", "ple_norm", "a_q_norm", "a_k_norm", "b_q_norm", "b_k_norm",
               "a_rel_bias", "b_sinks", "loss")


def _pack_small(vals):
    rows = []
    for nme in SMALL_NAMES:
        v = vals[nme].astype(F32)
        if nme == "a_rel_bias":
            v = jnp.pad(v.reshape(A_HEADS, -1), ((0, 0), (0, 3 * LANES - (2 * A_MAX_REL + 1))))
        v = v.reshape(-1)
        v = jnp.pad(v, (0, (-v.shape[0]) % LANES))
        rows.append(v.reshape(-1, LANES))
    out = jnp.concatenate(rows, axis=0)
    return jnp.pad(out, ((0, (-out.shape[0]) % 8), (0, 0)))


def _unpack_small(packed, shapes):
    out, r = {}, 0
    for nme in SMALL_NAMES:
        shp = shapes[nme]
        if nme == "a_rel_bias":
            nr = A_HEADS * 3
            out[nme] = packed[r:r + nr].reshape(A_HEADS, 3 * LANES)[:, :2 * A_MAX_REL + 1].reshape(shp)
        else:
            size = int(np.prod(shp)) if shp else 1
            nr = -(-size // LANES)
            out[nme] = packed[r:r + nr].reshape(-1)[:size].reshape(shp)
        r += nr
    return out


BIG_NAMES = ("ffn1_w_gu", "ffn1_w_down", "w_in", "w_gate", "w_proj_a", "w_proj_b", "w_out",
             "ffn2_w_gu", "ffn2_w_down", "w_ple_gate", "w_ple_proj")
ROW_SHARDED = ("ffn1_w_down", "ffn2_w_down", "w_out", "w_ple_gate")
WEIGHT_ORDER = ("ffn1_norm", "ffn1_w_gu", "ffn1_w_down", "mix_norm", "w_in", "a_q_norm", "a_k_norm", "a_rel_bias",
                "b_q_norm", "b_k_norm", "b_sinks", "w_gate", "w_proj_a", "w_proj_b", "w_out", "ffn2_norm",
                "ffn2_w_gu", "ffn2_w_down", "ple_norm", "w_ple_gate", "w_ple_proj")


TRANSPOSED = ("ffn1_w_gu", "ffn2_w_gu", "w_in")


def _local(a, nme):
    return a[0].T if nme in TRANSPOSED else a[0]


def _full_cols(wg):
    nb, k, n = wg.shape
    return jnp.transpose(wg, (1, 0, 2)).reshape(k, nb * n)


def _col_blocks(g, nb):
    k, n = g.shape
    return jnp.transpose(g.reshape(k, nb, n // nb), (1, 0, 2))


def _step(x, p, target, w, m, v):
    bl, s_len, d = x.shape
    t = bl * s_len
    h0 = x.reshape(t, d)
    pt = p.reshape(t, p.shape[-1])
    tgt = target.reshape(t, d)

    g_ffn1, g_mix, g_ffn2, g_ple = w["ffn1_norm"], w["mix_norm"], w["ffn2_norm"], w["ple_norm"]
    tiled = lambda a, width: jnp.tile(a.reshape(1, HEAD_DIM), (1, width // HEAD_DIM))
    gqa, gka = tiled(w["a_q_norm"], QW), tiled(w["a_k_norm"], _kv_width("A"))
    gqb, gkb = tiled(w["b_q_norm"], QW), tiled(w["b_k_norm"], _kv_width("B"))
    sinks = w["b_sinks"].reshape(B_Q_HEADS)
    bias_a = _rel_bias_expand(w["a_rel_bias"][0], "rel_bias_expand")
    bias_b = _alibi_bias()

    shard = {nme: _local(w[nme], nme).astype(BF16) for nme in BIG_NAMES}
    wgu1, wd1 = _all_gather([shard["ffn1_w_gu"], shard["ffn1_w_down"]], "weights_gather_ffn1")
    nf = wgu1.shape[1]
    wd1 = wd1.reshape(N_DEV // 2, nf, d)
    mixer_names = ("w_in", "w_gate")
    rest_names = ("w_proj_a", "w_proj_b", "w_out", "ffn2_w_gu", "ffn2_w_down", "w_ple_gate", "w_ple_proj")
    send1, recv1, bufs, token = _gather_start([shard[nme] for nme in mixer_names], wgu1, "gather_start_mixer")
    rsend1, rrecv1, rest_bufs, token = _gather_start([shard[nme] for nme in rest_names], token, "gather_start_rest")

    h1, gu1 = _ffn_fwd(h0, g_ffn1 + token[0, 0], wgu1, wd1, "ffn1_fwd")
    send2, recv2, bufs, token = _gather_pass(send1, recv1, bufs, h1, "gather_pass_mixer")
    win, wgate = _gather_wait(send2, recv2, bufs, token, "gather_wait_mixer")
    win, wgate = win.reshape(IN_COLS, d), _full_cols(wgate)
    un, qkv, gate = _proj_fwd(h1, g_mix, win, wgate, "proj_fwd")
    ya = _attn_fwd("A", qkv, gqa, gka, bias_a, sinks, bl, s_len, "attn_a_fwd")
    rsend2, rrecv2, rest_bufs, token = _gather_pass(rsend1, rrecv1, rest_bufs, ya, "gather_pass_rest")
    yb = _attn_fwd("B", qkv, gqb + token[0, 0], gkb, bias_b, sinks, bl, s_len, "attn_b_fwd")
    gathered = dict(zip(rest_names, _gather_wait(rsend2, rrecv2, rest_bufs, yb, "gather_wait_rest")))
    wgu2 = gathered["ffn2_w_gu"]
    wd2 = gathered["ffn2_w_down"].reshape(N_DEV // 2, nf, d)
    wpa = _full_cols(gathered["w_proj_a"])
    wpb = _full_cols(gathered["w_proj_b"])
    wpe = _full_cols(gathered["w_ple_proj"])
    wout = gathered["w_out"].reshape(d, d)
    wpg = gathered["w_ple_gate"].reshape(d, d)
    h2, merged, pa, pb = _merge_fwd(h1, ya, yb, gate, wpa, wpb, wout, "merge_fwd")
    h3, gu2 = _ffn_fwd(h2, g_ffn2, wgu2, wd2, "ffn2_fwd")
    dh3, dz4, dpp, n4, dg_ple, loss_part = _ple_loss(h3, g_ple, pt, tgt, wpg, wpe, "ple_loss")

    xi, yi, ci = _place()
    me = jnp.stack([4 * xi + 2 * yi + ci]).astype(jnp.int32)
    g32, g16, big = {}, {}, {}

    def keep(nme, pair, rows=None):
        for store, g in zip((g32, g16), pair):
            store[nme] = g if rows is None else g.reshape(N_DEV, rows, d)

    def start(names, after, tag):
        send, recv, parts, lands, token = _scatter_start([g16[nme] for nme in names], after, "grads_start_" + tag)
        return names, send, recv, parts, lands, token

    def finish(state, after, tag):
        names, send, recv, parts, lands, _ = state
        lands = _scatter_wait(send, recv, parts, lands, after, "grads_wait_" + tag)
        return names, lands

    def adam(done, dep):
        for nme, land in zip(*done):
            outs = _final_adam(g32[nme], land, _local(w[nme], nme), _local(m[nme], nme), _local(v[nme], nme), me, dep,
                               "adam_" + nme)
            big[nme] = [(o.T if nme in TRANSPOSED else o)[None] for o in outs]

    keep("w_ple_gate", _dw(n4, dz4, 1, d, "dw_ple_gate"), d // N_DEV)
    keep("w_ple_proj", _dw(pt, dpp, N_DEV, d // N_DEV, "dw_ple_proj"))

    dh2, dgu2, a2, n3, dg_ffn2 = _ffn_bwd(dh3, h2, g_ffn2, gu2, wgu2, wd2, "ffn2_bwd")
    keep("ffn2_w_gu", _dw(dgu2, n3, N_DEV, d, "dw_ffn2_gu"))
    keep("ffn2_w_down", _dw(a2, dh3, N_DEV // 2, d, "dw_ffn2_down", 0.5), nf // 2)
    flight = start(("w_ple_gate", "w_ple_proj", "ffn2_w_gu", "ffn2_w_down"), dh2, "ffn2")

    dpa, dpb, dzg, dya, dyb = _merge_bwd(dh2, pa, pb, gate, wpa, wpb, wout, "merge_bwd")
    keep("w_out", _dw(merged, dh2, 1, d, "dw_out"), d // N_DEV)
    keep("w_proj_a", _dw(ya, dpa, N_DEV, d // N_DEV, "dw_proj_a"))
    keep("w_proj_b", _dw(yb, dpb, N_DEV, d // N_DEV, "dw_proj_b"))
    keep("w_gate", _dw(un, dzg, N_DEV, 2 * d // N_DEV, "dw_gate"))

    tok = flight[-1][0, 0]
    dqa, dka, dva, dgqa, dgka, dbias, _ = _attn_bwd("A", qkv, gqa + tok, gka, bias_a, sinks, ya, dya, bl, s_len,
                                                     "attn_a_bwd")
    dqb, dkb, dvb, dgqb, dgkb, _, dsink = _attn_bwd("B", qkv, gqb, gkb, bias_b, sinks, yb, dyb, bl, s_len, "attn_b_bwd")
    dqkv = jnp.concatenate([dqa, dka, dva, dqb, dkb, dvb], axis=1)
    dtab = _rel_bias_grad(dbias, "rel_bias_grad")

    dh1, dg_mix = _proj_bwd(dh2, h1, g_mix, dzg, dqkv, win, wgate, "proj_bwd")
    keep("w_in", _dw(dqkv, un, 1, d, "dw_in"), IN_COLS // N_DEV)
    done = finish(flight, g32["w_in"], "ffn2")
    flight = start(("w_out", "w_proj_a", "w_proj_b", "w_gate", "w_in"), done[1][0], "mixer")
    waiting = [done]

    dh0, dgu1, a1, n1, dg_ffn1 = _ffn_bwd(dh1, h0, g_ffn1 + flight[-1][0, 0], gu1, wgu1, wd1, "ffn1_bwd")
    keep("ffn1_w_down", _dw(a1, dh1, N_DEV // 2, d, "dw_ffn1_down", 0.5), nf // 2)
    done = finish(flight, g32["ffn1_w_down"], "mixer")
    flight = start(("ffn1_w_down",), done[1][0], "ffn1_down")
    waiting.append(done)

    keep("ffn1_w_gu", _dw(dgu1, n1, N_DEV, d, "dw_ffn1_gu", dep=flight[-1]))
    done = finish(flight, g32["ffn1_w_gu"], "ffn1_down")
    flight = start(("ffn1_w_gu",), done[1][0], "ffn1_gu")
    for group in waiting + [done]:
        adam(group, flight[-1])
    behind = 0.0 * big["ffn1_w_down"][0][0, 0, :1]
    smalls = (dg_ffn1, dg_mix, dg_ffn2, dg_ple + behind, dgqa, dgka, dgqb, dgkb, dtab, dsink)
    return dh0, loss_part, big, smalls, flight, finish, adam


def kernel(x, p, ffn1_norm, ffn1_w_gu, ffn1_w_down, mix_norm, w_in, a_q_norm, a_k_norm, a_rel_bias, b_q_norm, b_k_norm, b_sinks, w_gate, w_proj_a, w_proj_b, w_out, ffn2_norm, ffn2_w_gu, ffn2_w_down, ple_norm, w_ple_gate, w_ple_proj, loss_target, m_ffn1_norm, m_ffn1_w_gu, m_ffn1_w_down, m_mix_norm, m_w_in, m_a_q_norm, m_a_k_norm, m_a_rel_bias, m_b_q_norm, m_b_k_norm, m_b_sinks, m_w_gate, m_w_proj_a, m_w_proj_b, m_w_out, m_ffn2_norm, m_ffn2_w_gu, m_ffn2_w_down, m_ple_norm, m_w_ple_gate, m_w_ple_proj, v_ffn1_norm, v_ffn1_w_gu, v_ffn1_w_down, v_mix_norm, v_w_in, v_a_q_norm, v_a_k_norm, v_a_rel_bias, v_b_q_norm, v_b_k_norm, v_b_sinks, v_w_gate, v_w_proj_a, v_w_proj_b, v_w_out, v_ffn2_norm, v_ffn2_w_gu, v_ffn2_w_down, v_ple_norm, v_w_ple_gate, v_w_ple_proj):
    w = dict(ffn1_norm=ffn1_norm, ffn1_w_gu=ffn1_w_gu, ffn1_w_down=ffn1_w_down, mix_norm=mix_norm, w_in=w_in,
             a_q_norm=a_q_norm, a_k_norm=a_k_norm, a_rel_bias=a_rel_bias, b_q_norm=b_q_norm, b_k_norm=b_k_norm,
             b_sinks=b_sinks, w_gate=w_gate, w_proj_a=w_proj_a, w_proj_b=w_proj_b, w_out=w_out, ffn2_norm=ffn2_norm,
             ffn2_w_gu=ffn2_w_gu, ffn2_w_down=ffn2_w_down, ple_norm=ple_norm, w_ple_gate=w_ple_gate,
             w_ple_proj=w_ple_proj)
    m = dict(ffn1_norm=m_ffn1_norm, ffn1_w_gu=m_ffn1_w_gu, ffn1_w_down=m_ffn1_w_down, mix_norm=m_mix_norm,
             w_in=m_w_in, a_q_norm=m_a_q_norm, a_k_norm=m_a_k_norm, a_rel_bias=m_a_rel_bias, b_q_norm=m_b_q_norm,
             b_k_norm=m_b_k_norm, b_sinks=m_b_sinks, w_gate=m_w_gate, w_proj_a=m_w_proj_a, w_proj_b=m_w_proj_b,
             w_out=m_w_out, ffn2_norm=m_ffn2_norm, ffn2_w_gu=m_ffn2_w_gu, ffn2_w_down=m_ffn2_w_down,
             ple_norm=m_ple_norm, w_ple_gate=m_w_ple_gate, w_ple_proj=m_w_ple_proj)
    v = dict(ffn1_norm=v_ffn1_norm, ffn1_w_gu=v_ffn1_w_gu, ffn1_w_down=v_ffn1_w_down, mix_norm=v_mix_norm,
             w_in=v_w_in, a_q_norm=v_a_q_norm, a_k_norm=v_a_k_norm, a_rel_bias=v_a_rel_bias, b_q_norm=v_b_q_norm,
             b_k_norm=v_b_k_norm, b_sinks=v_b_sinks, w_gate=v_w_gate, w_proj_a=v_w_proj_a, w_proj_b=v_w_proj_b,
             w_out=v_w_out, ffn2_norm=v_ffn2_norm, ffn2_w_gu=v_ffn2_w_gu, ffn2_w_down=v_ffn2_w_down,
             ple_norm=v_ple_norm, w_ple_gate=v_w_ple_gate, w_ple_proj=v_w_ple_proj)
    bl, s_len, d = x.shape

    dh0, loss_part, big, smalls, flight, finish, adam = _step(x, p[0], loss_target, w, m, v)
    dg_ffn1, dg_mix, dg_ffn2, dg_ple, dgqa, dgka, dgqb, dgkb, dtab, dsink = smalls

    fold = lambda a: a[:, :, 0, :].reshape(-1, HEAD_DIM).sum(axis=0)
    small_part = dict(
        ffn1_norm=dg_ffn1, mix_norm=dg_mix, ffn2_norm=dg_ffn2, ple_norm=dg_ple,
        a_q_norm=fold(dgqa), a_k_norm=fold(dgka), b_q_norm=fold(dgqb), b_k_norm=fold(dgkb),
        a_rel_bias=dtab,
        b_sinks=dsink.sum(axis=0)[:, 0, :GROUP].reshape(B_Q_HEADS),
        loss=loss_part[0, :1])
    zero1 = jnp.zeros((1,), F32)
    shapes = {nme: w[nme].shape for nme in SMALL_NAMES if nme != "loss"}
    shapes["loss"] = ()
    pk = lambda src: _pack_small({**{nme: src[nme] for nme in SMALL_NAMES if nme != "loss"}, "loss": zero1})
    sg, sd, sm, sv = _small_allreduce_adam(_pack_small(small_part), pk(w), pk(m), pk(v), "small_allreduce_adam")
    adam(finish(flight, sg, "ffn1_gu"), sg)
    sg, sd, sm, sv = (_unpack_small(a, shapes) for a in (sg, sd, sm, sv))

    def pick(i):
        out = []
        for nme in WEIGHT_ORDER:
            out.append(big[nme][i] if nme in big else (sg, sd, sm, sv)[i][nme])
        return out

    return (sg["loss"], dh0.reshape(bl, s_len, d), *pick(0), *pick(1), *pick(2), *pick(3))
```

```python
import functools

import jax
import jax.numpy as jnp
import numpy as np
from jax import lax
from jax.experimental import pallas as pl
from jax.experimental.pallas import tpu as pltpu

F32 = jnp.float32
BF16 = jnp.bfloat16

CHUNK = 64
HEAD_DIM = 64
A_HEADS = 8
A_PREV = 8
A_MAX_REL = 128
B_Q_HEADS = 8
B_KV_HEADS = 2
B_PREV = 2
A_WIDTH = A_HEADS * HEAD_DIM
B_Q_WIDTH = B_Q_HEADS * HEAD_DIM
B_KV_WIDTH = B_KV_HEADS * HEAD_DIM
IN_COLS = 3 * A_WIDTH + B_Q_WIDTH + 2 * B_KV_WIDTH
EPS = 1e-6
NEG_INF = -1e30
ADAM_LR = 0.001
ADAM_B1 = 0.9
ADAM_B2 = 0.999
ADAM_EPS = 1e-08
ADAM_WD = 0.01
ADAM_STEP = 10

N_DEV = 8
LANES = 128
QTILE = 2 * CHUNK
VMEM_LIMIT = 56 * 1024 * 1024
ADAM_TILE_ELEMS = 256 * 1024

MESH_ID = pl.DeviceIdType.MESH
ANY = pl.BlockSpec(memory_space=pl.ANY)
HBM = pl.BlockSpec(memory_space=pltpu.HBM)
SEM = pl.BlockSpec(memory_space=pltpu.SEMAPHORE)
SIDE_EFFECT = pltpu.SideEffectType.DATAFLOW_SIDE_EFFECTING


def _dot(a, b):
    return jnp.dot(a, b, preferred_element_type=F32)


def _dot_nt(a, b):
    return lax.dot_general(a, b, (((1,), (1,)), ((), ())), preferred_element_type=F32)


def _dot_tn(a, b):
    return lax.dot_general(a, b, (((0,), (0,)), ((), ())), preferred_element_type=F32)


def _params(sem=None, vmem=VMEM_LIMIT):
    return pltpu.CompilerParams(dimension_semantics=sem, vmem_limit_bytes=vmem)


def _row_tile(t, want):
    while t % want:
        want //= 2
    return want


def _place():
    return lax.axis_index("x"), lax.axis_index("y"), lax.axis_index("c")


def _all_gather(shards, name):
    n = len(shards)

    def body(*refs):
        ins, outs = refs[:n], refs[n:2 * n]
        send_sems, recv_sems, local_sems = refs[2 * n:]
        x, y, c = _place()
        me, sib = (x, y, c), (x, y, 1 - c)
        chips = [(1 - x, y), (x, 1 - y), (1 - x, 1 - y)]

        def copy(w, k, block, to, src=None):
            px, py, pc = block
            dst = outs[w].at[4 * px + 2 * py + pc]
            return pltpu.make_async_remote_copy(
                src_ref=dst if src is None else src, dst_ref=dst,
                send_sem=send_sems.at[w * 7 + k], recv_sem=recv_sems.at[w * 7 + k],
                device_id=to, device_id_type=MESH_ID)

        mine = [pltpu.make_async_copy(ins[w], outs[w].at[4 * x + 2 * y + c], local_sems.at[w]) for w in range(n)]
        for cp in mine:
            cp.start()
        first = []
        for w in range(n):
            first.append(copy(w, 0, me, sib, src=ins[w]))
            first += [copy(w, 1 + j, me, (*chip, c), src=ins[w]) for j, chip in enumerate(chips)]
        for cp in first:
            cp.start()
        passed = []
        for j, chip in enumerate(chips):
            for w in range(n):
                copy(w, 1 + j, (*chip, c), me).wait_recv()
                fwd = copy(w, 4 + j, (*chip, c), sib)
                fwd.start()
                passed.append(fwd)
        for w in range(n):
            copy(w, 0, sib, me).wait_recv()
        for j, chip in enumerate(chips):
            for w in range(n):
                copy(w, 4 + j, (*chip, 1 - c), me).wait_recv()
        for cp in first + passed:
            cp.wait_send()
        for cp in mine:
            cp.wait()

    return pl.pallas_call(
        body, name=name,
        out_shape=[jax.ShapeDtypeStruct((N_DEV,) + s.shape, s.dtype) for s in shards],
        in_specs=[ANY] * n, out_specs=[ANY] * n,
        scratch_shapes=[pltpu.SemaphoreType.DMA((7 * n,)), pltpu.SemaphoreType.DMA((7 * n,)),
                        pltpu.SemaphoreType.DMA((n,))],
    )(*shards)


def _gather_level(bufs, send_sems, recv_sems, level, shards=None):
    x, y, c = _place()
    me, sib = (x, y, c), (x, y, 1 - c)
    chips = [(1 - x, y), (x, 1 - y), (1 - x, 1 - y)]

    def copy(w, k, block, to):
        px, py, pc = block
        rows = bufs[w].at[4 * px + 2 * py + pc]
        src = shards[w] if shards is not None and block is me else rows
        return pltpu.make_async_remote_copy(src_ref=src, dst_ref=rows, send_sem=send_sems.at[k], recv_sem=recv_sems.at[k],
                                            device_id=to, device_id_type=MESH_ID)

    n = len(bufs)
    own = []
    if level == 1:
        own = [pltpu.make_async_copy(bufs[w].at[4 * x + 2 * y + c] if shards is None else shards[w],
                                     bufs[w].at[4 * x + 2 * y + c], send_sems.at[4 * n + w]) for w in range(n)]
    out, arriving = [], []
    for w in range(len(bufs)):
        if level == 1:
            out.append(copy(w, 4 * w, me, sib))
            arriving.append(copy(w, 4 * w, sib, me))
        for j, chip in enumerate(chips):
            if level == 1:
                out.append(copy(w, 4 * w + 1 + j, me, (*chip, c)))
                arriving.append(copy(w, 4 * w + 1 + j, (*chip, c), me))
            else:
                out.append(copy(w, 3 * w + j, (*chip, c), sib))
                arriving.append(copy(w, 3 * w + j, (*chip, 1 - c), me))
    return out, arriving, own


def _split_call(body, name, bufs, sems_in, after, n_sems_out, token, extra=()):
    n = len(bufs)
    out_shape = [pltpu.SemaphoreType.DMA((n_sems_out,))] * (2 if n_sems_out else 0)
    out_shape += [pltpu.HBM(a.shape, a.dtype) for a in bufs]
    out_specs = [SEM] * (2 if n_sems_out else 0) + [HBM] * n
    if token:
        out_shape.append(jax.ShapeDtypeStruct((8, LANES), F32))
        out_specs.append(pl.BlockSpec(memory_space=pltpu.VMEM))
    first = 2 if n_sems_out else 0
    return pl.pallas_call(
        body, name=name, out_shape=tuple(out_shape),
        in_specs=[HBM] * (n + len(extra)) + [SEM] * len(sems_in) + [ANY], out_specs=tuple(out_specs),
        input_output_aliases={i: first + i for i in range(n)},
        compiler_params=pltpu.CompilerParams(has_side_effects=SIDE_EFFECT),
    )(*bufs, *extra, *sems_in, after)


def _gather_start(shards, after, name):
    n = len(shards)
    hbm = lambda a: pltpu.with_memory_space_constraint(a, pltpu.HBM)
    bufs = [hbm(lax.empty((N_DEV,) + s.shape, s.dtype)) for s in shards]

    def body(*refs):
        out, _, own = _gather_level(refs[:n], refs[2 * n + 1], refs[2 * n + 2], 1, shards=refs[n:2 * n])
        for cp in own + out:
            cp.start()
        refs[-1][...] = jnp.zeros_like(refs[-1])

    outs = _split_call(body, name, bufs + [hbm(s) for s in shards], [], after, 5 * n, True)
    return outs[0], outs[1], list(outs[2:2 + 2 * n]), outs[-1]


def _gather_pass(send1, recv1, bufs_and_shards, after, name):
    n = len(bufs_and_shards) // 2
    bufs = bufs_and_shards

    def body(*refs):
        refs = refs[:n] + refs[2 * n:]
        out1, in1, own = _gather_level(refs[:n], refs[n], refs[n + 1], 1)
        out2, _, _ = _gather_level(refs[:n], refs[n + 3], refs[n + 4], 2)
        for cp in in1:
            cp.wait_recv()
        for cp in out2:
            cp.start()
        for cp in out1:
            cp.wait_send()
        for cp in own:
            cp.wait()
        refs[-1][...] = jnp.zeros_like(refs[-1])

    outs = _split_call(body, name, bufs, [send1, recv1], after, 3 * n, True)
    return outs[0], outs[1], list(outs[2:2 + n]), outs[-1]


def _gather_wait(send2, recv2, bufs, after, name):
    n = len(bufs)

    def body(*refs):
        out2, in2, _ = _gather_level(refs[:n], refs[n], refs[n + 1], 2)
        for cp in in2:
            cp.wait_recv()
        for cp in out2:
            cp.wait_send()

    return list(_split_call(body, name, bufs, [send2, recv2], after, 0, False))


def _scatter_copies(parts, lands, send_sems, recv_sems):
    x, y, c = _place()
    cps = []
    for w, (part, land) in enumerate(zip(parts, lands)):
        for k in range(1, N_DEV):
            px, py, pc = x ^ ((k >> 2) & 1), y ^ ((k >> 1) & 1), c ^ (k & 1)
            cps.append(pltpu.make_async_remote_copy(
                src_ref=part.at[4 * px + 2 * py + pc], dst_ref=land.at[k - 1],
                send_sem=send_sems.at[7 * w + k - 1], recv_sem=recv_sems.at[7 * w + k - 1],
                device_id=(px, py, pc), device_id_type=MESH_ID))
    return cps


def _scatter_start(parts, after, name):
    n = len(parts)

    def body(*refs):
        ins, lands = refs[:n], refs[n:2 * n]
        send_sems, recv_sems = refs[2 * n + 1], refs[2 * n + 2]
        token = refs[-1]
        for cp in _scatter_copies(ins, lands, send_sems, recv_sems):
            cp.start()
        token[...] = jnp.zeros_like(token)

    land_shapes = [(N_DEV - 1,) + p.shape[1:] for p in parts]
    in_hbm = [pltpu.with_memory_space_constraint(p, pltpu.HBM) for p in parts]
    in_hbm += [pltpu.with_memory_space_constraint(lax.empty(s, p.dtype), pltpu.HBM) for s, p in zip(land_shapes, parts)]
    outs = pl.pallas_call(
        body, name=name,
        out_shape=(pltpu.SemaphoreType.DMA((7 * n,)), pltpu.SemaphoreType.DMA((7 * n,)),
                   *[pltpu.HBM(p.shape, p.dtype) for p in parts],
                   *[pltpu.HBM(s, p.dtype) for s, p in zip(land_shapes, parts)],
                   jax.ShapeDtypeStruct((8, LANES), F32)),
        in_specs=[HBM] * (2 * n) + [ANY],
        out_specs=(SEM, SEM, *[HBM] * (2 * n), pl.BlockSpec(memory_space=pltpu.VMEM)),
        input_output_aliases={i: 2 + i for i in range(2 * n)},
        compiler_params=pltpu.CompilerParams(has_side_effects=SIDE_EFFECT),
    )(*in_hbm, after)
    return outs[0], outs[1], list(outs[2:2 + n]), list(outs[2 + n:2 + 2 * n]), outs[-1]


def _scatter_wait(send_sems, recv_sems, parts, lands, after, name):
    n = len(parts)

    def body(*refs):
        ins, lnd = refs[:n], refs[n:2 * n]
        for cp in _scatter_copies(ins, lnd, refs[2 * n], refs[2 * n + 1]):
            cp.wait_send()
            cp.wait_recv()

    outs = pl.pallas_call(
        body, name=name,
        out_shape=tuple(pltpu.HBM(a.shape, a.dtype) for a in parts + lands),
        in_specs=[HBM] * (2 * n) + [SEM, SEM, ANY],
        out_specs=tuple([HBM] * (2 * n)),
        input_output_aliases={i: i for i in range(2 * n)},
        compiler_params=pltpu.CompilerParams(has_side_effects=SIDE_EFFECT),
    )(*parts, *lands, send_sems, recv_sems, after)
    return list(outs[n:])


def _adam(w, g, m, v):
    m2 = ADAM_B1 * m + (1.0 - ADAM_B1) * g
    v2 = ADAM_B2 * v + (1.0 - ADAM_B2) * (g * g)
    m_hat = m2 / (1.0 - ADAM_B1 ** ADAM_STEP)
    v_hat = v2 / (1.0 - ADAM_B2 ** ADAM_STEP)
    delta = -ADAM_LR * (m_hat / (jnp.sqrt(v_hat) + ADAM_EPS) + ADAM_WD * w)
    return delta, m2, v2


def _small_allreduce_adam(part, w, m, v, name):
    rows = part.shape[0]

    def body(p_ref, w_ref, m_ref, v_ref, g_ref, d_ref, mo_ref, vo_ref, buf, send_sems, recv_sems):
        x, y, c = _place()
        buf[0] = p_ref[...]
        cps = []
        for k in range(1, N_DEV):
            kx, ky, kc = (k >> 2) & 1, (k >> 1) & 1, k & 1
            peer = (x ^ kx, y ^ ky, c ^ kc)
            cps.append(pltpu.make_async_remote_copy(
                src_ref=p_ref, dst_ref=buf.at[k], send_sem=send_sems.at[k - 1], recv_sem=recv_sems.at[k - 1],
                device_id=peer, device_id_type=MESH_ID))
        for cp in cps:
            cp.start()
        for cp in cps:
            cp.wait()
        me = 4 * x + 2 * y + c
        total = buf[me]
        for d in range(1, N_DEV):
            total = total + buf[d ^ me]
        g_ref[...] = total
        delta, m2, v2 = _adam(w_ref[...], total, m_ref[...], v_ref[...])
        d_ref[...] = delta
        mo_ref[...] = m2
        vo_ref[...] = v2

    vm = pl.BlockSpec(memory_space=pltpu.VMEM)
    return pl.pallas_call(
        body, name=name,
        out_shape=[jax.ShapeDtypeStruct(part.shape, F32)] * 4,
        in_specs=[vm] * 4, out_specs=[vm] * 4,
        scratch_shapes=[pltpu.VMEM((N_DEV, rows, LANES), F32),
                        pltpu.SemaphoreType.DMA((N_DEV - 1,)), pltpu.SemaphoreType.DMA((N_DEV - 1,))],
    )(part, w, m, v)


def _final_adam(g8, land, w, m, v, me, dep, name):
    _, r, c = g8.shape
    tr = max(q for q in range(16, r + 1, 16) if r % q == 0 and q * c <= ADAM_TILE_ELEMS)

    def body(me_ref, g_ref, land_ref, w_ref, m_ref, v_ref, dep_ref, go_ref, d_ref, mo_ref, vo_ref):
        del dep_ref
        g = g_ref[...]
        for k in range(N_DEV - 1):
            g = g + land_ref[k].astype(F32)
        go_ref[...] = g
        delta, m2, v2 = _adam(w_ref[...], g, m_ref[...], v_ref[...])
        d_ref[...] = delta
        mo_ref[...] = m2
        vo_ref[...] = v2

    plain = pl.BlockSpec((tr, c), lambda i, s: (i, 0))
    return pl.pallas_call(
        body, name=name,
        out_shape=[jax.ShapeDtypeStruct((r, c), F32)] * 4,
        grid_spec=pltpu.PrefetchScalarGridSpec(
            num_scalar_prefetch=1, grid=(r // tr,),
            in_specs=[pl.BlockSpec((None, tr, c), lambda i, s: (s[0], i, 0)),
                      pl.BlockSpec((N_DEV - 1, tr, c), lambda i, s: (0, i, 0)),
                      plain, plain, plain, ANY],
            out_specs=[plain] * 4),
        compiler_params=_params(("arbitrary",)),
    )(me, g8, land, w, m, v, dep)


def _rms(x, gain):
    r = lax.rsqrt(jnp.mean(x * x, axis=-1, keepdims=True) + EPS)
    xh = x * r
    return xh * gain, xh, r


def _rms_bwd(xh, r, gain, dy):
    gdy = gain * dy
    dx = r * (gdy - xh * jnp.mean(xh * gdy, axis=-1, keepdims=True))
    return dx, jnp.sum(dy * xh, axis=0, keepdims=True)


def _load_weights(pairs, sems):
    cps = [pltpu.make_async_copy(src, dst, sems.at[i]) for i, (src, dst) in enumerate(pairs)]
    for cp in cps:
        cp.start()
    for cp in cps:
        cp.wait()


def _ffn_fwd(h, gain, wgu, wd, name):
    t, d = h.shape
    nb, nf, _ = wgu.shape
    nh = nb // 2
    tm = _row_tile(t, 512)

    def body(h_ref, g_ref, wgu_hbm, wd_hbm, out_ref, gu_ref, wgu_v, wd_v, sems):
        @pl.when(pl.program_id(0) == 0)
        def _():
            _load_weights([(wgu_hbm, wgu_v), (wd_hbm, wd_v)], sems)

        x = h_ref[...]
        n, _, _ = _rms(x, g_ref[...])
        nbf = n.astype(BF16)
        acc = jnp.zeros((tm, d), F32)
        for j in range(nh):
            g = _dot_nt(nbf, wgu_v[j])
            u = _dot_nt(nbf, wgu_v[j + nh])
            gu_ref[j] = g.astype(BF16)
            gu_ref[j + nh] = u.astype(BF16)
            a = (g * jax.nn.sigmoid(g)) * u
            acc = acc + _dot(a.astype(BF16), wd_v[j])
        out_ref[...] = x + 0.5 * acc

    return pl.pallas_call(
        body, name=name, grid=(t // tm,),
        out_shape=[jax.ShapeDtypeStruct((t, d), F32), jax.ShapeDtypeStruct((nb, t, nf), BF16)],
        in_specs=[pl.BlockSpec((tm, d), lambda i: (i, 0)), pl.BlockSpec((1, d), lambda i: (0, 0)), ANY, ANY],
        out_specs=[pl.BlockSpec((tm, d), lambda i: (i, 0)), pl.BlockSpec((nb, tm, nf), lambda i: (0, i, 0))],
        scratch_shapes=[pltpu.VMEM(wgu.shape, BF16), pltpu.VMEM(wd.shape, BF16), pltpu.SemaphoreType.DMA((2,))],
        compiler_params=_params(("arbitrary",)),
    )(h, gain, wgu, wd)


def _ffn_bwd(dh, h, gain, gu, wgu, wd, name):
    t, d = h.shape
    nb, nf, _ = wgu.shape
    nh = nb // 2
    tm = _row_tile(t, 256)

    def body(dh_ref, h_ref, g_ref, gu_ref, wgu_hbm, wd_hbm, dhp_ref, dgu_ref, a_ref, n_ref, dgain_ref,
             wgu_v, wd_v, sems):
        @pl.when(pl.program_id(0) == 0)
        def _():
            _load_weights([(wgu_hbm, wgu_v), (wd_hbm, wd_v)], sems)
            dgain_ref[...] = jnp.zeros_like(dgain_ref)

        x = h_ref[...]
        gain_v = g_ref[...]
        n, xh, r = _rms(x, gain_v)
        n_ref[...] = n.astype(BF16)
        dh_v = dh_ref[...]
        dfb = (0.5 * dh_v).astype(BF16)
        dn = jnp.zeros((tm, d), F32)
        for j in range(nh):
            da = _dot_nt(dfb, wd_v[j])
            g = gu_ref[j].astype(F32)
            u = gu_ref[j + nh].astype(F32)
            sg = jax.nn.sigmoid(g)
            si = g * sg
            dg = (da * u * (sg * (1.0 + g * (1.0 - sg)))).astype(BF16)
            du = (da * si).astype(BF16)
            a_ref[j] = (si * u).astype(BF16)
            dgu_ref[j] = dg
            dgu_ref[j + nh] = du
            dn = dn + _dot(dg, wgu_v[j]) + _dot(du, wgu_v[j + nh])
        dx, dgain = _rms_bwd(xh, r, gain_v, dn)
        dhp_ref[...] = dh_v + dx
        dgain_ref[...] += dgain

    row = pl.BlockSpec((tm, d), lambda i: (i, 0))
    vec = pl.BlockSpec((1, d), lambda i: (0, 0))
    return pl.pallas_call(
        body, name=name, grid=(t // tm,),
        out_shape=[jax.ShapeDtypeStruct((t, d), F32), jax.ShapeDtypeStruct((nb, t, nf), BF16),
                   jax.ShapeDtypeStruct((nh, t, nf), BF16), jax.ShapeDtypeStruct((t, d), BF16),
                   jax.ShapeDtypeStruct((1, d), F32)],
        in_specs=[row, row, vec, pl.BlockSpec((nb, tm, nf), lambda i: (0, i, 0)), ANY, ANY],
        out_specs=[row, pl.BlockSpec((nb, tm, nf), lambda i: (0, i, 0)),
                   pl.BlockSpec((nh, tm, nf), lambda i: (0, i, 0)), row, vec],
        scratch_shapes=[pltpu.VMEM(wgu.shape, BF16), pltpu.VMEM(wd.shape, BF16), pltpu.SemaphoreType.DMA((2,))],
        compiler_params=_params(("arbitrary",)),
    )(dh, h, gain, gu, wgu, wd)


def _dw(xa, dy, nb, n, name, scale=1.0, dep=None):
    t, k = xa.shape[-2:]
    tt = _row_tile(t, 512)
    steps = t // tt
    wide = dy.ndim == 2 and xa.ndim == 2
    if xa.ndim == 3:
        x_spec = pl.BlockSpec((nb, tt, k), lambda i: (0, i, 0))
    else:
        x_spec = pl.BlockSpec((tt, k), lambda i: (i, 0))
    if dy.ndim == 3:
        dy_spec = pl.BlockSpec((nb, tt, n), lambda i: (0, i, 0))
    else:
        dy_spec = pl.BlockSpec((tt, dy.shape[1]), lambda i: (i, 0))
    acc_shape = (k, nb * n) if wide else (nb, k, n)
    stage_shape = (k, nb * n) if wide else (k, n)

    def body(x_ref, dy_ref, *rest):
        o_hbm, ob_hbm, acc, stage, sems = rest[-5:]

        @pl.when(pl.program_id(0) == 0)
        def _():
            acc[...] = jnp.zeros_like(acc)

        if wide:
            acc[...] += _dot(x_ref[...].astype(BF16).T, dy_ref[...].astype(BF16))
        elif xa.ndim == 2:
            xt = x_ref[...].astype(BF16).T
            for j in range(nb):
                acc[j] += _dot(xt, dy_ref[j].astype(BF16))
        else:
            dyb = dy_ref[...].astype(BF16)
            for j in range(nb):
                acc[j] += _dot_tn(x_ref[j].astype(BF16), dyb)

        @pl.when(pl.program_id(0) == steps - 1)
        def _():
            if scale != 1.0:
                acc[...] = acc[...] * scale
            if wide:
                cps = [pltpu.make_async_copy(acc.at[:, pl.ds(j * n, n)] if nb > 1 else acc, o_hbm.at[j], sems.at[j])
                       for j in range(nb)]
            else:
                cps = [pltpu.make_async_copy(acc, o_hbm, sems.at[0])]
            for cp in cps:
                cp.start()
            if wide:
                stage[...] = acc[...].astype(BF16)
                bcs = [pltpu.make_async_copy(stage.at[:, pl.ds(j * n, n)] if nb > 1 else stage, ob_hbm.at[j],
                                             sems.at[nb + j]) for j in range(nb)]
                for cp in bcs:
                    cp.start()
                for cp in bcs:
                    cp.wait()
            else:
                for j in range(nb):
                    stage[...] = acc[j].astype(BF16)
                    cp = pltpu.make_async_copy(stage, ob_hbm.at[j], sems.at[nb])
                    cp.start()
                    cp.wait()
            for cp in cps:
                cp.wait()

    return pl.pallas_call(
        body, name=name, grid=(steps,),
        out_shape=[jax.ShapeDtypeStruct((nb, k, n), F32), jax.ShapeDtypeStruct((nb, k, n), BF16)],
        in_specs=[x_spec, dy_spec] + ([] if dep is None else [ANY]),
        out_specs=[ANY, ANY],
        scratch_shapes=[pltpu.VMEM(acc_shape, F32), pltpu.VMEM(stage_shape, BF16),
                        pltpu.SemaphoreType.DMA((2 * nb,))],
        compiler_params=_params(("arbitrary",)),
    )(*((xa, dy) if dep is None else (xa, dy, dep)))


def _proj_fwd(h, gain, win, wgate, name):
    t, d = h.shape
    tm = _row_tile(t, 256)
    nq, ng = win.shape[0], wgate.shape[1]

    def body(h_ref, g_ref, win_ref, wg_ref, un_ref, qkv_ref, gate_ref):
        n, _, _ = _rms(h_ref[...], g_ref[...])
        nbf = n.astype(BF16)
        un_ref[...] = nbf
        qkv_ref[...] = _dot_nt(nbf, win_ref[...])
        gate_ref[...] = jax.nn.sigmoid(_dot(nbf, wg_ref[...]))

    full = lambda a: pl.BlockSpec(a.shape, lambda i: (0,) * a.ndim)
    return pl.pallas_call(
        body, name=name, grid=(t // tm,),
        out_shape=[jax.ShapeDtypeStruct((t, d), BF16), jax.ShapeDtypeStruct((t, nq), F32),
                   jax.ShapeDtypeStruct((t, ng), F32)],
        in_specs=[pl.BlockSpec((tm, d), lambda i: (i, 0)), full(gain), full(win), full(wgate)],
        out_specs=[pl.BlockSpec((tm, d), lambda i: (i, 0)), pl.BlockSpec((tm, nq), lambda i: (i, 0)),
                   pl.BlockSpec((tm, ng), lambda i: (i, 0))],
        compiler_params=_params(("arbitrary",)),
    )(h, gain, win, wgate)


def _proj_bwd(dh, h, gain, dzg, dqkv_parts, win, wgate, name):
    t, d = h.shape
    tm = _row_tile(t, 256)
    ng = wgate.shape[1]
    np_ = len(dqkv_parts)
    widths = [a.shape[1] for a in dqkv_parts]

    def body(dh_ref, h_ref, g_ref, dzg_ref, *rest):
        part_refs, (win_ref, wg_ref, dhp_ref, dgain_ref) = rest[:np_], rest[np_:]

        @pl.when(pl.program_id(0) == 0)
        def _():
            dgain_ref[...] = jnp.zeros_like(dgain_ref)

        gain_v = g_ref[...]
        _, xh, r = _rms(h_ref[...], gain_v)
        dun = _dot_nt(dzg_ref[...], wg_ref[...])
        off = 0
        for ref, wd in zip(part_refs, widths):
            dun = dun + _dot(ref[...].astype(BF16), win_ref[off:off + wd, :])
            off += wd
        dx, dgain = _rms_bwd(xh, r, gain_v, dun)
        dhp_ref[...] = dh_ref[...] + dx
        dgain_ref[...] += dgain

    full = lambda a: pl.BlockSpec(a.shape, lambda i: (0,) * a.ndim)
    row = pl.BlockSpec((tm, d), lambda i: (i, 0))
    return pl.pallas_call(
        body, name=name, grid=(t // tm,),
        out_shape=[jax.ShapeDtypeStruct((t, d), F32), jax.ShapeDtypeStruct((1, d), F32)],
        in_specs=[row, row, full(gain), pl.BlockSpec((tm, ng), lambda i: (i, 0))]
        + [pl.BlockSpec((tm, wd), lambda i: (i, 0)) for wd in widths] + [full(win), full(wgate)],
        out_specs=[row, pl.BlockSpec((1, d), lambda i: (0, 0))],
        compiler_params=_params(("arbitrary",)),
    )(dh, h, gain, dzg, *dqkv_parts, win, wgate)


def _dw_rows(parts, dy, name):
    t, n = dy.shape
    widths = [a.shape[1] for a in parts]
    k = sum(widths)
    tt = _row_tile(t, 512)
    steps = t // tt
    np_ = len(parts)

    def body(*refs):
        part_refs, dy_ref = refs[:np_], refs[np_]
        o_hbm, ob_hbm, acc, stage, sems = refs[np_ + 1:]

        @pl.when(pl.program_id(0) == 0)
        def _():
            acc[...] = jnp.zeros_like(acc)

        dyb = dy_ref[...].astype(BF16)
        off = 0
        for ref, wd in zip(part_refs, widths):
            acc[off:off + wd, :] += _dot(ref[...].astype(BF16).T, dyb)
            off += wd

        @pl.when(pl.program_id(0) == steps - 1)
        def _():
            stage[...] = acc[...].astype(BF16)
            cps = [pltpu.make_async_copy(acc, o_hbm.at[0], sems.at[0]),
                   pltpu.make_async_copy(stage, ob_hbm.at[0], sems.at[1])]
            for cp in cps:
                cp.start()
            for cp in cps:
                cp.wait()

    return pl.pallas_call(
        body, name=name, grid=(steps,),
        out_shape=[jax.ShapeDtypeStruct((1, k, n), F32), jax.ShapeDtypeStruct((1, k, n), BF16)],
        in_specs=[pl.BlockSpec((tt, wd), lambda i: (i, 0)) for wd in widths] + [pl.BlockSpec((tt, n), lambda i: (i, 0))],
        out_specs=[ANY, ANY],
        scratch_shapes=[pltpu.VMEM((k, n), F32), pltpu.VMEM((k, n), BF16), pltpu.SemaphoreType.DMA((2,))],
        compiler_params=_params(("arbitrary",)),
    )(*parts, dy)


def _merge_fwd(h, ya, yb, gate, wpa, wpb, wout, name):
    t, d = h.shape
    tm = _row_tile(t, 256)

    def body(h_ref, ya_ref, yb_ref, ga_ref, gb_ref, wpa_ref, wpb_ref, wout_ref, out_ref, mg_ref, pa_ref, pb_ref):
        pa = _dot(ya_ref[...].astype(BF16), wpa_ref[...])
        pb = _dot(yb_ref[...].astype(BF16), wpb_ref[...])
        merged = (ga_ref[...] * pa + gb_ref[...] * pb).astype(BF16)
        pa_ref[...] = pa.astype(BF16)
        pb_ref[...] = pb.astype(BF16)
        mg_ref[...] = merged
        out_ref[...] = h_ref[...] + _dot(merged, wout_ref[...])

    full = lambda a: pl.BlockSpec(a.shape, lambda i: (0,) * a.ndim)
    row = pl.BlockSpec((tm, d), lambda i: (i, 0))
    yrow = pl.BlockSpec((tm, ya.shape[1]), lambda i: (i, 0))
    return pl.pallas_call(
        body, name=name, grid=(t // tm,),
        out_shape=[jax.ShapeDtypeStruct((t, d), F32)] + [jax.ShapeDtypeStruct((t, d), BF16)] * 3,
        in_specs=[row, yrow, yrow, pl.BlockSpec((tm, d), lambda i: (i, 0)), pl.BlockSpec((tm, d), lambda i: (i, 1)),
                  full(wpa), full(wpb), full(wout)],
        out_specs=[row] * 4,
        compiler_params=_params(("arbitrary",)),
    )(h, ya, yb, gate, gate, wpa, wpb, wout)


def _merge_bwd(dh, pa, pb, gate, wpa, wpb, wout, name):
    t, d = dh.shape
    tm = _row_tile(t, 256)
    wy = wpa.shape[0]

    def body(dh_ref, pa_ref, pb_ref, ga_ref, gb_ref, wpa_ref, wpb_ref, wout_ref,
             dpa_ref, dpb_ref, dzg_ref, dya_ref, dyb_ref):
        dm = _dot_nt(dh_ref[...].astype(BF16), wout_ref[...])
        ga, gb = ga_ref[...], gb_ref[...]
        dpa = (dm * ga).astype(BF16)
        dpb = (dm * gb).astype(BF16)
        dpa_ref[...] = dpa
        dpb_ref[...] = dpb
        dzg_ref[:, :d] = (dm * pa_ref[...].astype(F32) * ga * (1.0 - ga)).astype(BF16)
        dzg_ref[:, d:] = (dm * pb_ref[...].astype(F32) * gb * (1.0 - gb)).astype(BF16)
        dya_ref[...] = _dot_nt(dpa, wpa_ref[...])
        dyb_ref[...] = _dot_nt(dpb, wpb_ref[...])

    full = lambda a: pl.BlockSpec(a.shape, lambda i: (0,) * a.ndim)
    row = pl.BlockSpec((tm, d), lambda i: (i, 0))
    yrow = pl.BlockSpec((tm, wy), lambda i: (i, 0))
    return pl.pallas_call(
        body, name=name, grid=(t // tm,),
        out_shape=[jax.ShapeDtypeStruct((t, d), BF16), jax.ShapeDtypeStruct((t, d), BF16),
                   jax.ShapeDtypeStruct((t, 2 * d), BF16), jax.ShapeDtypeStruct((t, wy), F32),
                   jax.ShapeDtypeStruct((t, wy), F32)],
        in_specs=[row, row, row, pl.BlockSpec((tm, d), lambda i: (i, 0)), pl.BlockSpec((tm, d), lambda i: (i, 1)),
                  full(wpa), full(wpb), full(wout)],
        out_specs=[row, row, pl.BlockSpec((tm, 2 * d), lambda i: (i, 0)), yrow, yrow],
        compiler_params=_params(("arbitrary",)),
    )(dh, pa, pb, gate, gate, wpa, wpb, wout)


def _ple_loss(h, gain, p, target, wpg, wpe, name):
    t, d = h.shape
    tm = _row_tile(t, 256)
    pd = p.shape[1]

    def body(h_ref, g_ref, p_ref, t_ref, wpg_ref, wpe_ref, dh_ref, dz_ref, dpp_ref, n_ref, dgain_ref, loss_ref):
        @pl.when(pl.program_id(0) == 0)
        def _():
            dgain_ref[...] = jnp.zeros_like(dgain_ref)
            loss_ref[...] = jnp.zeros_like(loss_ref)

        x = h_ref[...]
        gain_v = g_ref[...]
        n, xh, r = _rms(x, gain_v)
        nbf = n.astype(BF16)
        n_ref[...] = nbf
        pg = jax.nn.sigmoid(_dot(nbf, wpg_ref[...]))
        pp = _dot(p_ref[...].astype(BF16), wpe_ref[...])
        err = (x + pg * pp) - t_ref[...]
        loss_ref[...] += 0.5 * jnp.sum(jnp.mean(err * err, axis=-1, keepdims=True))
        dy = err * (1.0 / d)
        dpp_ref[...] = (dy * pg).astype(BF16)
        dz = (dy * pp * pg * (1.0 - pg)).astype(BF16)
        dz_ref[...] = dz
        dn = _dot_nt(dz, wpg_ref[...])
        dx, dgain = _rms_bwd(xh, r, gain_v, dn)
        dh_ref[...] = dy + dx
        dgain_ref[...] += dgain

    full = lambda a: pl.BlockSpec(a.shape, lambda i: (0,) * a.ndim)
    row = pl.BlockSpec((tm, d), lambda i: (i, 0))
    return pl.pallas_call(
        body, name=name, grid=(t // tm,),
        out_shape=[jax.ShapeDtypeStruct((t, d), F32), jax.ShapeDtypeStruct((t, d), BF16),
                   jax.ShapeDtypeStruct((t, d), BF16), jax.ShapeDtypeStruct((t, d), BF16),
                   jax.ShapeDtypeStruct((1, d), F32), jax.ShapeDtypeStruct((8, LANES), F32)],
        in_specs=[row, full(gain), pl.BlockSpec((tm, pd), lambda i: (i, 0)), row, full(wpg), full(wpe)],
        out_specs=[row, row, row, row, pl.BlockSpec((1, d), lambda i: (0, 0)),
                   pl.BlockSpec((8, LANES), lambda i: (0, 0))],
        compiler_params=_params(("arbitrary",)),
    )(h, gain, p, target, wpg, wpe)


def _head_masks():
    lane = lax.broadcasted_iota(jnp.int32, (1, LANES), 1)
    m0 = (lane < HEAD_DIM).astype(F32)
    return m0, 1.0 - m0


def _head_mean(v, m0, m1):
    del m0, m1
    width = v.shape[-1]
    shift = HEAD_DIM.bit_length() - 1
    r = jnp.right_shift(lax.broadcasted_iota(jnp.int32, (width, width), 0), shift)
    c = jnp.right_shift(lax.broadcasted_iota(jnp.int32, (width, width), 1), shift)
    same_head = (r == c).astype(BF16)
    hi = v.astype(BF16)
    lo = (v - hi.astype(F32)).astype(BF16)
    return (_dot(hi, same_head) + _dot(lo, same_head)) * (1.0 / HEAD_DIM)


def _head_norm(x, gain, m0, m1):
    r = lax.rsqrt(_head_mean(x * x, m0, m1) + EPS)
    xh = x * r
    return xh * gain, xh, r


def _head_norm_bwd(xh, r, gain, dy, m0, m1):
    gdy = gain * dy
    dx = r * (gdy - xh * _head_mean(xh * gdy, m0, m1))
    return dx, jnp.sum(dy * xh, axis=0, keepdims=True)


GROUP = 4
QW = GROUP * HEAD_DIM
STACK = GROUP * QTILE


def _kv_width(mode):
    return QW if mode == "A" else LANES


def _q_scratch_shape(mode, s_len):
    return (s_len, QW) if mode == "A" else (GROUP * s_len, LANES)


def _group_masks(dtype=F32):
    lane = lax.broadcasted_iota(jnp.int32, (1, QW), 1)
    return [((lane >= h * HEAD_DIM) & (lane < (h + 1) * HEAD_DIM)).astype(dtype) for h in range(GROUP)]


def _stack_heads(first_kv, x, m0, m1):
    out = []
    for half in range(GROUP // 2):
        xh = x[:, half * LANES:(half + 1) * LANES]
        a0, a1 = xh * m0, xh * m1
        r0, r1 = pltpu.roll(a0, HEAD_DIM, 1), pltpu.roll(a1, HEAD_DIM, 1)
        out += [jnp.where(first_kv, a0, r0), jnp.where(first_kv, r1, a1)]
    return out


def _unstack_heads(mode, first_kv, ts, m0, m1):
    if mode == "A":
        masks = _group_masks()
        return sum(t * mk for t, mk in zip(ts, masks))
    halves = []
    for half in range(GROUP // 2):
        t0 = jnp.where(first_kv, ts[2 * half], pltpu.roll(ts[2 * half], HEAD_DIM, 1))
        t1 = jnp.where(first_kv, pltpu.roll(ts[2 * half + 1], HEAD_DIM, 1), ts[2 * half + 1])
        halves.append(t0 * m0 + t1 * m1)
    return jnp.concatenate(halves, axis=1)


def _store_stacked(dst, i, heads):
    for half in range(2):
        rows = slice(half * QTILE, (half + 1) * QTILE)
        for h, x in enumerate(heads):
            dst[pl.ds((2 * i + half) * STACK + h * QTILE, QTILE), :] = x[rows].astype(dst.dtype)


def _load_stacked(mode, ref, m):
    if mode == "B":
        return ref[pl.ds(pl.multiple_of(m * STACK, STACK), STACK), :]
    x = ref[pl.ds(pl.multiple_of(m * QTILE, QTILE), QTILE), :]
    return jnp.concatenate([x * mk for mk in _group_masks(x.dtype)], axis=0)


def _attn_prep(mode, group, s_len, padk, q_ref, k_ref, v_ref, gq_ref, gk_ref, qs, k2, v2, do_ref=None, dos=None):
    m0, m1 = _head_masks()
    zpad = jnp.zeros((padk, k2.shape[1]), BF16)
    k2[pl.ds(0, padk), :] = zpad
    v2[pl.ds(0, padk), :] = zpad
    first_kv = group == 0
    rt = 2 * QTILE
    for i in range(s_len // rt):
        rows = pl.ds(i * rt, rt)
        qn, _, _ = _head_norm(q_ref[rows, :], gq_ref[...], m0, m1)
        kn, _, _ = _head_norm(k_ref[rows, :], gk_ref[...], m0, m1)
        qn = qn * (HEAD_DIM ** -0.5)
        if mode == "A":
            qs[rows, :] = qn.astype(BF16)
            if dos is not None:
                dos[rows, :] = do_ref[rows, :].astype(BF16)
        else:
            _store_stacked(qs, i, _stack_heads(first_kv, qn, m0, m1))
            if dos is not None:
                _store_stacked(dos, i, _stack_heads(first_kv, do_ref[rows, :], m0, m1))
        k2[pl.ds(padk + i * rt, rt), :] = kn.astype(BF16)
        v2[pl.ds(padk + i * rt, rt), :] = v_ref[rows, :].astype(BF16)


def _attn_probs(mode, q_st, kb, bias, ok, sink):
    s = _dot_nt(q_st, kb) + bias
    s = jnp.where(ok, s, NEG_INF)
    mx = jnp.max(s, axis=-1, keepdims=True)
    if mode == "B":
        mx = jnp.maximum(mx, sink)
    e = jnp.exp(s - mx)
    l = jnp.sum(e, axis=-1, keepdims=True)
    if mode == "B":
        l = l + jnp.exp(sink - mx)
    return e, mx, l


def _sink_column(sink_ref, group):
    row = lax.broadcasted_iota(jnp.int32, (STACK, 1), 0)
    col = jnp.zeros((STACK, 1), F32)
    for h in range(GROUP):
        col = jnp.where((row >= h * QTILE) & (row < (h + 1) * QTILE), sink_ref[GROUP * group + h], col)
    return col


def _head_deltas(dd, m0, m1):
    cols = []
    for half in range(GROUP // 2):
        dh = dd[:, half * LANES:(half + 1) * LANES]
        cols += [jnp.sum(dh * m0, axis=-1, keepdims=True), jnp.sum(dh * m1, axis=-1, keepdims=True)]
    return jnp.concatenate(cols, axis=0)


def _attn_cols(mode):
    if mode == "A":
        return (lambda b, g: (b, g)), (lambda b, g: (b, 2 + g)), (lambda b, g: (b, 4 + g))
    return (lambda b, g: (b, 6 + g)), (lambda b, g: (b, 16)), (lambda b, g: (b, 17))


def _attn_fwd(mode, qkv, gq, gk, bias, sinks, bl, s_len, name):
    bw = bias.shape[-1]
    padk = bw - QTILE
    nt = s_len // QTILE
    qmap, kmap, vmap = _attn_cols(mode)

    kw = _kv_width(mode)

    def body(q_ref, k_ref, v_ref, gq_ref, gk_ref, bias_ref, sink_ref, o_ref, qs, k2, v2):
        group = pl.program_id(1)
        m0, m1 = _head_masks()
        first_kv = group == 0
        _attn_prep(mode, group, s_len, padk, q_ref, k_ref, v_ref, gq_ref, gk_ref, qs, k2, v2)
        col = lax.broadcasted_iota(jnp.int32, (STACK, bw), 1)
        sink = _sink_column(sink_ref, group)

        def tile(m, carry):
            r0 = pl.multiple_of(m * QTILE, QTILE)
            q_st = _load_stacked(mode, qs, m)
            ok = col >= (padk - r0)
            e, _, l = _attn_probs(mode, q_st, k2[pl.ds(r0, bw), :], bias_ref[...], ok, sink)
            o_st = _dot(e.astype(BF16), v2[pl.ds(r0, bw), :]) / l
            heads = [o_st[h * QTILE:(h + 1) * QTILE] for h in range(GROUP)]
            o_ref[pl.ds(r0, QTILE), :] = _unstack_heads(mode, first_kv, heads, m0, m1)
            return carry

        lax.fori_loop(0, nt, tile, 0, unroll=2)

    blk = lambda w, f: pl.BlockSpec((s_len, w), f)
    return pl.pallas_call(
        body, name=name, grid=(bl, B_Q_HEADS // GROUP),
        out_shape=jax.ShapeDtypeStruct((bl * s_len, B_Q_HEADS * HEAD_DIM), F32),
        in_specs=[blk(QW, qmap), blk(kw, kmap), blk(kw, vmap),
                  pl.BlockSpec((1, QW), lambda b, g: (0, 0)), pl.BlockSpec((1, kw), lambda b, g: (0, 0)),
                  pl.BlockSpec((STACK, bw), lambda b, g: (g, 0)),
                  pl.BlockSpec(memory_space=pltpu.SMEM)],
        out_specs=blk(QW, lambda b, g: (b, g)),
        scratch_shapes=[pltpu.VMEM(_q_scratch_shape(mode, s_len), BF16)] + [pltpu.VMEM((s_len + padk, kw), BF16)] * 2,
        compiler_params=_params(("arbitrary", "arbitrary")),
    )(qkv, qkv, qkv, gq, gk, bias.reshape(-1, bw), sinks)


def _attn_bwd(mode, qkv, gq, gk, bias, sinks, y, dy, bl, s_len, name):
    bw = bias.shape[-1]
    padk = bw - QTILE
    nt = s_len // QTILE
    qmap, kmap, vmap = _attn_cols(mode)
    t = bl * s_len
    kw = _kv_width(mode)
    kvw = 4 * LANES if mode == "A" else LANES

    def body(q_ref, k_ref, v_ref, gq_ref, gk_ref, bias_ref, sink_ref, y_ref, dy_ref,
             dq_ref, dk_ref, dv_ref, dgq_ref, dgk_ref, dbias_ref, dsink_ref,
             qs, k2, v2, dos, dqs, dk, dv):
        group = pl.program_id(1)
        m0, m1 = _head_masks()
        first_kv = group == 0
        _attn_prep(mode, group, s_len, padk, q_ref, k_ref, v_ref, gq_ref, gk_ref, qs, k2, v2, dy_ref, dos)
        dk[...] = jnp.zeros_like(dk)
        dv[...] = jnp.zeros_like(dv)
        dbias_ref[...] = jnp.zeros_like(dbias_ref)
        col = lax.broadcasted_iota(jnp.int32, (STACK, bw), 1)
        lane8 = lax.broadcasted_iota(jnp.int32, (8, LANES), 1)
        sink = _sink_column(sink_ref, group)

        def tile(m, dsink):
            r0 = pl.multiple_of(m * QTILE, QTILE)
            rows = pl.ds(r0, QTILE)
            band = pl.ds(r0, bw)
            q_st = _load_stacked(mode, qs, m)
            do_st = _load_stacked(mode, dos, m)
            delta = _head_deltas(dy_ref[rows, :] * y_ref[rows, :], m0, m1)
            ok = col >= (padk - r0)
            kb = k2[band, :]
            e, mx, l = _attn_probs(mode, q_st, kb, bias_ref[...], ok, sink)
            inv = 1.0 / l
            pn = e * inv
            dp = _dot_nt(do_st, v2[band, :])
            ds = pn * (dp - delta)
            if mode == "A":
                dbias_ref[...] += ds
            else:
                part = jnp.exp(sink - mx) * inv * delta
                for h in range(GROUP):
                    dsink = dsink - jnp.where(lane8 == h, jnp.sum(part[h * QTILE:(h + 1) * QTILE]), 0.0)
            dsb = ds.astype(BF16)
            dv[band, :] += _dot_tn(pn.astype(BF16), do_st)
            dk[band, :] += _dot_tn(dsb, q_st)
            dq_st = _dot(dsb, kb)
            if mode == "A":
                heads = [dq_st[h * QTILE:(h + 1) * QTILE] for h in range(GROUP)]
                dqs[rows, :] = _unstack_heads(mode, first_kv, heads, m0, m1)
            else:
                dqs[pl.ds(pl.multiple_of(m * STACK, STACK), STACK), :] = dq_st
            return dsink

        dsink = lax.fori_loop(0, nt, tile, jnp.zeros((8, LANES), F32), unroll=2)
        dsink_ref[...] = dsink

        rt = 2 * QTILE
        dgq = jnp.zeros((1, QW), F32)
        dgk = jnp.zeros((1, kw), F32)
        for i in range(s_len // rt):
            rows = pl.ds(i * rt, rt)
            src = pl.ds(padk + i * rt, rt)
            gq_v, gk_v = gq_ref[...], gk_ref[...]
            _, qh, qr = _head_norm(q_ref[rows, :], gq_v, m0, m1)
            _, kh, kr = _head_norm(k_ref[rows, :], gk_v, m0, m1)
            if mode == "A":
                dqn = dqs[rows, :] * (HEAD_DIM ** -0.5)
            else:
                dqn = jnp.concatenate(
                    [_unstack_heads(mode, first_kv, [dqs[pl.ds((2 * i + half) * STACK + h * QTILE, QTILE), :]
                                                     for h in range(GROUP)], m0, m1)
                     for half in range(2)], axis=0) * (HEAD_DIM ** -0.5)
            dq_raw, dgq_i = _head_norm_bwd(qh, qr, gq_v, dqn, m0, m1)
            dk_raw, dgk_i = _head_norm_bwd(kh, kr, gk_v, dk[src, :], m0, m1)
            dvn = dv[src, :]
            dq_ref[rows, :] = dq_raw
            if mode == "A":
                dk_ref[rows, :] = dk_raw
                dv_ref[rows, :] = dvn
            else:
                @pl.when(group == 0)
                def _():
                    dk_ref[rows, :] = dk_raw
                    dv_ref[rows, :] = dvn

                @pl.when(group != 0)
                def _():
                    dk_ref[rows, :] += dk_raw
                    dv_ref[rows, :] += dvn
            dgq, dgk = dgq + dgq_i, dgk + dgk_i
        dgq_ref[...] = jnp.broadcast_to(dgq, (8, QW))
        dgk_ref[...] = jnp.broadcast_to(dgk, (8, kw))

    ng = B_Q_HEADS // GROUP
    blk = lambda w, f: pl.BlockSpec((s_len, w), f)
    small = lambda w: pl.BlockSpec((None, None, 8, w), lambda b, g: (b, g, 0, 0))
    own = lambda b, g: (b, g)
    kvmap = own if mode == "A" else (lambda b, g: (b, 0))
    pad_f32 = pltpu.VMEM((s_len + padk, kw), F32)
    pad_bf = pltpu.VMEM((s_len + padk, kw), BF16)
    stack_bf = pltpu.VMEM(_q_scratch_shape(mode, s_len), BF16)
    outs = pl.pallas_call(
        body, name=name, grid=(bl, ng),
        out_shape=[jax.ShapeDtypeStruct((t, ng * QW), F32), jax.ShapeDtypeStruct((t, kvw), F32),
                   jax.ShapeDtypeStruct((t, kvw), F32),
                   jax.ShapeDtypeStruct((bl, ng, 8, QW), F32), jax.ShapeDtypeStruct((bl, ng, 8, kw), F32),
                   jax.ShapeDtypeStruct((bl, ng * STACK, bw), F32), jax.ShapeDtypeStruct((bl, ng, 8, LANES), F32)],
        in_specs=[blk(QW, qmap), blk(kw, kmap), blk(kw, vmap),
                  pl.BlockSpec((1, QW), lambda b, g: (0, 0)), pl.BlockSpec((1, kw), lambda b, g: (0, 0)),
                  pl.BlockSpec((STACK, bw), lambda b, g: (g, 0)),
                  pl.BlockSpec(memory_space=pltpu.SMEM),
                  blk(QW, own), blk(QW, own)],
        out_specs=[blk(QW, own), blk(kw, kvmap), blk(kw, kvmap), small(QW), small(kw),
                   pl.BlockSpec((None, STACK, bw), lambda b, g: (b, g, 0)), small(LANES)],
        scratch_shapes=[stack_bf, pad_bf, pad_bf, stack_bf, pltpu.VMEM(_q_scratch_shape(mode, s_len), F32),
                        pad_f32, pad_f32],
        compiler_params=_params(("arbitrary", "arbitrary")),
    )(qkv, qkv, qkv, gq, gk, bias.reshape(-1, bw), sinks, y, dy)
    outs = list(outs)
    outs[5] = outs[5].reshape(bl, B_Q_HEADS, QTILE, bw)
    return outs


def _band_geometry(prev):
    bw = QTILE + prev * CHUNK
    i = np.arange(QTILE)[:, None]
    j = np.arange(bw)[None, :]
    dist = i + prev * CHUNK - j
    valid = (j // CHUNK >= i // CHUNK) & (j // CHUNK <= i // CHUNK + prev)
    return dist, valid


A_VAR0 = (A_PREV * CHUNK - A_MAX_REL) // LANES * LANES


A_NVAR = QTILE + A_PREV * CHUNK - A_VAR0


def _skew_rows(x, sign):
    rows, n = x.shape
    row = lax.broadcasted_iota(jnp.int32, x.shape, 0)
    b = 1
    while b < rows:
        x = jnp.where((row & b) != 0, pltpu.roll(x, (sign * b) % n, 1), x)
        b *= 2
    return x


def _rel_bias_expand(table, name):
    _, valid = _band_geometry(A_PREV)
    bw = valid.shape[1]
    valid_f = jnp.asarray(valid.astype(np.float32))
    rev = jnp.flip(table[:, 1:], axis=1).reshape(A_HEADS, 1, A_NVAR)

    def body(rev_ref, valid_ref, o_ref):
        rowv = jnp.broadcast_to(rev_ref[...], (QTILE, A_NVAR))
        top = rowv[:, 0:1]
        var = _skew_rows(rowv, 1)
        row = lax.broadcasted_iota(jnp.int32, (QTILE, A_NVAR), 0)
        colv = lax.broadcasted_iota(jnp.int32, (QTILE, A_NVAR), 1)
        var = jnp.where(colv < row, top, var)
        ok = valid_ref[...] > 0.5
        o_ref[:, :A_VAR0] = jnp.where(ok[:, :A_VAR0], top, NEG_INF)
        o_ref[:, A_VAR0:] = jnp.where(ok[:, A_VAR0:], var, NEG_INF)

    return pl.pallas_call(
        body, name=name, grid=(A_HEADS,),
        out_shape=jax.ShapeDtypeStruct((A_HEADS, QTILE, bw), F32),
        in_specs=[pl.BlockSpec((None, 1, A_NVAR), lambda h: (h, 0, 0)), pl.BlockSpec((QTILE, bw), lambda h: (0, 0))],
        out_specs=pl.BlockSpec((None, QTILE, bw), lambda h: (h, 0, 0)),
        compiler_params=_params(("arbitrary",)),
    )(rev, valid_f)


def _rel_bias_grad(dbias, name):
    bl = dbias.shape[0]
    bw = dbias.shape[-1]

    def body(db_ref, o_ref):
        g = db_ref[0]
        for b in range(1, bl):
            g = g + db_ref[b]
        sk = _skew_rows(g[:, A_VAR0:], -1)
        row = lax.broadcasted_iota(jnp.int32, (QTILE, A_NVAR), 0)
        colv = lax.broadcasted_iota(jnp.int32, (QTILE, A_NVAR), 1)
        wrapped = (row + colv) >= A_NVAR
        main = jnp.sum(jnp.where(wrapped, 0.0, sk), axis=0, keepdims=True)
        top = jnp.sum(g[:, :A_VAR0]) + jnp.sum(jnp.where(wrapped, sk, 0.0))
        o_ref[:, :A_NVAR] = jnp.broadcast_to(main, (8, A_NVAR))
        o_ref[:, A_NVAR:] = jnp.full((8, LANES), top, F32)

    out = pl.pallas_call(
        body, name=name, grid=(A_HEADS,),
        out_shape=jax.ShapeDtypeStruct((A_HEADS, 8, A_NVAR + LANES), F32),
        in_specs=[pl.BlockSpec((bl, None, QTILE, bw), lambda h: (0, h, 0, 0))],
        out_specs=pl.BlockSpec((None, 8, A_NVAR + LANES), lambda h: (h, 0, 0)),
        compiler_params=_params(("arbitrary",)),
    )(dbias)
    main, top = out[:, 0, :A_NVAR], out[:, 0, A_NVAR]
    fm = jnp.flip(main, axis=1)
    return jnp.concatenate([jnp.zeros((A_HEADS, 1), F32), fm[:, :-1], fm[:, -1:] + top[:, None]], axis=1)


def _alibi_bias():
    dist, valid = _band_geometry(B_PREV)
    slopes = np.array([2.0 ** (-8.0 * (h + 1) / B_Q_HEADS) for h in range(B_Q_HEADS)], dtype=np.float32)
    bias = -slopes[:, None, None] * np.abs(dist).astype(np.float32)[None]
    return jnp.asarray(np.where(valid[None], bias, np.float32(NEG_INF)).astype(np.float32))


SMALL_NAMES = ("ffn1_norm", "mix_norm", "ffn2_norm", "ple_norm", "a_q_norm", "a_k_norm", "b_q_norm", "b_k_norm",
               "a_rel_bias", "b_sinks", "loss")


def _pack_small(vals):
    rows = []
    for nme in SMALL_NAMES:
        v = vals[nme].astype(F32)
        if nme == "a_rel_bias":
            v = jnp.pad(v.reshape(A_HEADS, -1), ((0, 0), (0, 3 * LANES - (2 * A_MAX_REL + 1))))
        v = v.reshape(-1)
        v = jnp.pad(v, (0, (-v.shape[0]) % LANES))
        rows.append(v.reshape(-1, LANES))
    out = jnp.concatenate(rows, axis=0)
    return jnp.pad(out, ((0, (-out.shape[0]) % 8), (0, 0)))


def _unpack_small(packed, shapes):
    out, r = {}, 0
    for nme in SMALL_NAMES:
        shp = shapes[nme]
        if nme == "a_rel_bias":
            nr = A_HEADS * 3
            out[nme] = packed[r:r + nr].reshape(A_HEADS, 3 * LANES)[:, :2 * A_MAX_REL + 1].reshape(shp)
        else:
            size = int(np.prod(shp)) if shp else 1
            nr = -(-size // LANES)
            out[nme] = packed[r:r + nr].reshape(-1)[:size].reshape(shp)
        r += nr
    return out


BIG_NAMES = ("ffn1_w_gu", "ffn1_w_down", "w_in", "w_gate", "w_proj_a", "w_proj_b", "w_out",
             "ffn2_w_gu", "ffn2_w_down", "w_ple_gate", "w_ple_proj")
ROW_SHARDED = ("ffn1_w_down", "ffn2_w_down", "w_out", "w_ple_gate")
WEIGHT_ORDER = ("ffn1_norm", "ffn1_w_gu", "ffn1_w_down", "mix_norm", "w_in", "a_q_norm", "a_k_norm", "a_rel_bias",
                "b_q_norm", "b_k_norm", "b_sinks", "w_gate", "w_proj_a", "w_proj_b", "w_out", "ffn2_norm",
                "ffn2_w_gu", "ffn2_w_down", "ple_norm", "w_ple_gate", "w_ple_proj")


TRANSPOSED = ("ffn1_w_gu", "ffn2_w_gu", "w_in")


def _local(a, nme):
    return a[0].T if nme in TRANSPOSED else a[0]


def _full_cols(wg):
    nb, k, n = wg.shape
    return jnp.transpose(wg, (1, 0, 2)).reshape(k, nb * n)


def _col_blocks(g, nb):
    k, n = g.shape
    return jnp.transpose(g.reshape(k, nb, n // nb), (1, 0, 2))


def _step(x, p, target, w, m, v):
    bl, s_len, d = x.shape
    t = bl * s_len
    h0 = x.reshape(t, d)
    pt = p.reshape(t, p.shape[-1])
    tgt = target.reshape(t, d)

    g_ffn1, g_mix, g_ffn2, g_ple = w["ffn1_norm"], w["mix_norm"], w["ffn2_norm"], w["ple_norm"]
    tiled = lambda a, width: jnp.tile(a.reshape(1, HEAD_DIM), (1, width // HEAD_DIM))
    gqa, gka = tiled(w["a_q_norm"], QW), tiled(w["a_k_norm"], _kv_width("A"))
    gqb, gkb = tiled(w["b_q_norm"], QW), tiled(w["b_k_norm"], _kv_width("B"))
    sinks = w["b_sinks"].reshape(B_Q_HEADS)
    bias_a = _rel_bias_expand(w["a_rel_bias"][0], "rel_bias_expand")
    bias_b = _alibi_bias()

    shard = {nme: _local(w[nme], nme).astype(BF16) for nme in BIG_NAMES}
    wgu1, wd1 = _all_gather([shard["ffn1_w_gu"], shard["ffn1_w_down"]], "weights_gather_ffn1")
    nf = wgu1.shape[1]
    wd1 = wd1.reshape(N_DEV // 2, nf, d)
    mixer_names = ("w_in", "w_gate")
    rest_names = ("w_proj_a", "w_proj_b", "w_out", "ffn2_w_gu", "ffn2_w_down", "w_ple_gate", "w_ple_proj")
    send1, recv1, bufs, token = _gather_start([shard[nme] for nme in mixer_names], wgu1, "gather_start_mixer")
    rsend1, rrecv1, rest_bufs, token = _gather_start([shard[nme] for nme in rest_names], token, "gather_start_rest")

    h1, gu1 = _ffn_fwd(h0, g_ffn1 + token[0, 0], wgu1, wd1, "ffn1_fwd")
    send2, recv2, bufs, token = _gather_pass(send1, recv1, bufs, h1, "gather_pass_mixer")
    win, wgate = _gather_wait(send2, recv2, bufs, token, "gather_wait_mixer")
    win, wgate = win.reshape(IN_COLS, d), _full_cols(wgate)
    un, qkv, gate = _proj_fwd(h1, g_mix, win, wgate, "proj_fwd")
    ya = _attn_fwd("A", qkv, gqa, gka, bias_a, sinks, bl, s_len, "attn_a_fwd")
    rsend2, rrecv2, rest_bufs, token = _gather_pass(rsend1, rrecv1, rest_bufs, ya, "gather_pass_rest")
    yb = _attn_fwd("B", qkv, gqb + token[0, 0], gkb, bias_b, sinks, bl, s_len, "attn_b_fwd")
    gathered = dict(zip(rest_names, _gather_wait(rsend2, rrecv2, rest_bufs, yb, "gather_wait_rest")))
    wgu2 = gathered["ffn2_w_gu"]
    wd2 = gathered["ffn2_w_down"].reshape(N_DEV // 2, nf, d)
    wpa = _full_cols(gathered["w_proj_a"])
    wpb = _full_cols(gathered["w_proj_b"])
    wpe = _full_cols(gathered["w_ple_proj"])
    wout = gathered["w_out"].reshape(d, d)
    wpg = gathered["w_ple_gate"].reshape(d, d)
    h2, merged, pa, pb = _merge_fwd(h1, ya, yb, gate, wpa, wpb, wout, "merge_fwd")
    h3, gu2 = _ffn_fwd(h2, g_ffn2, wgu2, wd2, "ffn2_fwd")
    dh3, dz4, dpp, n4, dg_ple, loss_part = _ple_loss(h3, g_ple, pt, tgt, wpg, wpe, "ple_loss")

    xi, yi, ci = _place()
    me = jnp.stack([4 * xi + 2 * yi + ci]).astype(jnp.int32)
    g32, g16, big = {}, {}, {}

    def keep(nme, pair, rows=None):
        for store, g in zip((g32, g16), pair):
            store[nme] = g if rows is None else g.reshape(N_DEV, rows, d)

    def start(names, after, tag):
        send, recv, parts, lands, token = _scatter_start([g16[nme] for nme in names], after, "grads_start_" + tag)
        return names, send, recv, parts, lands, token

    def finish(state, after, tag):
        names, send, recv, parts, lands, _ = state
        lands = _scatter_wait(send, recv, parts, lands, after, "grads_wait_" + tag)
        return names, lands

    def adam(done, dep):
        for nme, land in zip(*done):
            outs = _final_adam(g32[nme], land, _local(w[nme], nme), _local(m[nme], nme), _local(v[nme], nme), me, dep,
                               "adam_" + nme)
            big[nme] = [(o.T if nme in TRANSPOSED else o)[None] for o in outs]

    keep("w_ple_gate", _dw(n4, dz4, 1, d, "dw_ple_gate"), d // N_DEV)
    keep("w_ple_proj", _dw(pt, dpp, N_DEV, d // N_DEV, "dw_ple_proj"))

    dh2, dgu2, a2, n3, dg_ffn2 = _ffn_bwd(dh3, h2, g_ffn2, gu2, wgu2, wd2, "ffn2_bwd")
    keep("ffn2_w_gu", _dw(dgu2, n3, N_DEV, d, "dw_ffn2_gu"))
    keep("ffn2_w_down", _dw(a2, dh3, N_DEV // 2, d, "dw_ffn2_down", 0.5), nf // 2)
    flight = start(("w_ple_gate", "w_ple_proj", "ffn2_w_gu", "ffn2_w_down"), dh2, "ffn2")

    dpa, dpb, dzg, dya, dyb = _merge_bwd(dh2, pa, pb, gate, wpa, wpb, wout, "merge_bwd")
    keep("w_out", _dw(merged, dh2, 1, d, "dw_out"), d // N_DEV)
    keep("w_proj_a", _dw(ya, dpa, N_DEV, d // N_DEV, "dw_proj_a"))
    keep("w_proj_b", _dw(yb, dpb, N_DEV, d // N_DEV, "dw_proj_b"))
    keep("w_gate", _dw(un, dzg, N_DEV, 2 * d // N_DEV, "dw_gate"))

    tok = flight[-1][0, 0]
    dqa, dka, dva, dgqa, dgka, dbias, _ = _attn_bwd("A", qkv, gqa + tok, gka, bias_a, sinks, ya, dya, bl, s_len,
                                                     "attn_a_bwd")
    dqb, dkb, dvb, dgqb, dgkb, _, dsink = _attn_bwd("B", qkv, gqb, gkb, bias_b, sinks, yb, dyb, bl, s_len, "attn_b_bwd")
    dqkv = [dqa, dka, dva, dqb, dkb, dvb]
    dtab = _rel_bias_grad(dbias, "rel_bias_grad")

    dh1, dg_mix = _proj_bwd(dh2, h1, g_mix, dzg, dqkv, win, wgate, "proj_bwd")
    keep("w_in", _dw_rows(dqkv, un, "dw_in"), IN_COLS // N_DEV)
    done = finish(flight, g32["w_in"], "ffn2")
    flight = start(("w_out", "w_proj_a", "w_proj_b", "w_gate", "w_in"), done[1][0], "mixer")
    waiting = [done]

    dh0, dgu1, a1, n1, dg_ffn1 = _ffn_bwd(dh1, h0, g_ffn1 + flight[-1][0, 0], gu1, wgu1, wd1, "ffn1_bwd")
    keep("ffn1_w_down", _dw(a1, dh1, N_DEV // 2, d, "dw_ffn1_down", 0.5), nf // 2)
    done = finish(flight, g32["ffn1_w_down"], "mixer")
    flight = start(("ffn1_w_down",), done[1][0], "ffn1_down")
    waiting.append(done)

    keep("ffn1_w_gu", _dw(dgu1, n1, N_DEV, d, "dw_ffn1_gu", dep=flight[-1]))
    done = finish(flight, g32["ffn1_w_gu"], "ffn1_down")
    flight = start(("ffn1_w_gu",), done[1][0], "ffn1_gu")
    for group in waiting + [done]:
        adam(group, flight[-1])
    behind = 0.0 * big["ffn1_w_down"][0][0, 0, :1]
    smalls = (dg_ffn1, dg_mix, dg_ffn2, dg_ple + behind, dgqa, dgka, dgqb, dgkb, dtab, dsink)
    return dh0, loss_part, big, smalls, flight, finish, adam


def kernel(x, p, ffn1_norm, ffn1_w_gu, ffn1_w_down, mix_norm, w_in, a_q_norm, a_k_norm, a_rel_bias, b_q_norm, b_k_norm, b_sinks, w_gate, w_proj_a, w_proj_b, w_out, ffn2_norm, ffn2_w_gu, ffn2_w_down, ple_norm, w_ple_gate, w_ple_proj, loss_target, m_ffn1_norm, m_ffn1_w_gu, m_ffn1_w_down, m_mix_norm, m_w_in, m_a_q_norm, m_a_k_norm, m_a_rel_bias, m_b_q_norm, m_b_k_norm, m_b_sinks, m_w_gate, m_w_proj_a, m_w_proj_b, m_w_out, m_ffn2_norm, m_ffn2_w_gu, m_ffn2_w_down, m_ple_norm, m_w_ple_gate, m_w_ple_proj, v_ffn1_norm, v_ffn1_w_gu, v_ffn1_w_down, v_mix_norm, v_w_in, v_a_q_norm, v_a_k_norm, v_a_rel_bias, v_b_q_norm, v_b_k_norm, v_b_sinks, v_w_gate, v_w_proj_a, v_w_proj_b, v_w_out, v_ffn2_norm, v_ffn2_w_gu, v_ffn2_w_down, v_ple_norm, v_w_ple_gate, v_w_ple_proj):
    w = dict(ffn1_norm=ffn1_norm, ffn1_w_gu=ffn1_w_gu, ffn1_w_down=ffn1_w_down, mix_norm=mix_norm, w_in=w_in,
             a_q_norm=a_q_norm, a_k_norm=a_k_norm, a_rel_bias=a_rel_bias, b_q_norm=b_q_norm, b_k_norm=b_k_norm,
             b_sinks=b_sinks, w_gate=w_gate, w_proj_a=w_proj_a, w_proj_b=w_proj_b, w_out=w_out, ffn2_norm=ffn2_norm,
             ffn2_w_gu=ffn2_w_gu, ffn2_w_down=ffn2_w_down, ple_norm=ple_norm, w_ple_gate=w_ple_gate,
             w_ple_proj=w_ple_proj)
    m = dict(ffn1_norm=m_ffn1_norm, ffn1_w_gu=m_ffn1_w_gu, ffn1_w_down=m_ffn1_w_down, mix_norm=m_mix_norm,
             w_in=m_w_in, a_q_norm=m_a_q_norm, a_k_norm=m_a_k_norm, a_rel_bias=m_a_rel_bias, b_q_norm=m_b_q_norm,
             b_k_norm=m_b_k_norm, b_sinks=m_b_sinks, w_gate=m_w_gate, w_proj_a=m_w_proj_a, w_proj_b=m_w_proj_b,
             w_out=m_w_out, ffn2_norm=m_ffn2_norm, ffn2_w_gu=m_ffn2_w_gu, ffn2_w_down=m_ffn2_w_down,
             ple_norm=m_ple_norm, w_ple_gate=m_w_ple_gate, w_ple_proj=m_w_ple_proj)
    v = dict(ffn1_norm=v_ffn1_norm, ffn1_w_gu=v_ffn1_w_gu, ffn1_w_down=v_ffn1_w_down, mix_norm=v_mix_norm,
             w_in=v_w_in, a_q_norm=v_a_q_norm, a_k_norm=v_a_k_norm, a_rel_bias=v_a_rel_bias, b_q_norm=v_b_q_norm,
             b_k_norm=v_b_k_norm, b_sinks=v_b_sinks, w_gate=v_w_gate, w_proj_a=v_w_proj_a, w_proj_b=v_w_proj_b,
             w_out=v_w_out, ffn2_norm=v_ffn2_norm, ffn2_w_gu=v_ffn2_w_gu, ffn2_w_down=v_ffn2_w_down,
             ple_norm=v_ple_norm, w_ple_gate=v_w_ple_gate, w_ple_proj=v_w_ple_proj)
    bl, s_len, d = x.shape

    dh0, loss_part, big, smalls, flight, finish, adam = _step(x, p[0], loss_target, w, m, v)
    dg_ffn1, dg_mix, dg_ffn2, dg_ple, dgqa, dgka, dgqb, dgkb, dtab, dsink = smalls

    fold = lambda a: a[:, :, 0, :].reshape(-1, HEAD_DIM).sum(axis=0)
    small_part = dict(
        ffn1_norm=dg_ffn1, mix_norm=dg_mix, ffn2_norm=dg_ffn2, ple_norm=dg_ple,
        a_q_norm=fold(dgqa), a_k_norm=fold(dgka), b_q_norm=fold(dgqb), b_k_norm=fold(dgkb),
        a_rel_bias=dtab,
        b_sinks=dsink.sum(axis=0)[:, 0, :GROUP].reshape(B_Q_HEADS),
        loss=loss_part[0, :1])
    zero1 = jnp.zeros((1,), F32)
    shapes = {nme: w[nme].shape for nme in SMALL_NAMES if nme != "loss"}
    shapes["loss"] = ()
    pk = lambda src: _pack_small({**{nme: src[nme] for nme in SMALL_NAMES if nme != "loss"}, "loss": zero1})
    sg, sd, sm, sv = _small_allreduce_adam(_pack_small(small_part), pk(w), pk(m), pk(v), "small_allreduce_adam")
    adam(finish(flight, sg, "ffn1_gu"), sg)
    sg, sd, sm, sv = (_unpack_small(a, shapes) for a in (sg, sd, sm, sv))

    def pick(i):
        out = []
        for nme in WEIGHT_ORDER:
            out.append(big[nme][i] if nme in big else (sg, sd, sm, sv)[i][nme])
        return out

    return (sg["loss"], dh0.reshape(bl, s_len, d), *pick(0), *pick(1), *pick(2), *pick(3))
```

```python
import functools

import jax
import jax.numpy as jnp
import numpy as np
from jax import lax
from jax.experimental import pallas as pl
from jax.experimental.pallas import tpu as pltpu

F32 = jnp.float32
BF16 = jnp.bfloat16

CHUNK = 64
HEAD_DIM = 64
A_HEADS = 8
A_PREV = 8
A_MAX_REL = 128
B_Q_HEADS = 8
B_KV_HEADS = 2
B_PREV = 2
A_WIDTH = A_HEADS * HEAD_DIM
B_Q_WIDTH = B_Q_HEADS * HEAD_DIM
B_KV_WIDTH = B_KV_HEADS * HEAD_DIM
IN_COLS = 3 * A_WIDTH + B_Q_WIDTH + 2 * B_KV_WIDTH
EPS = 1e-6
NEG_INF = -1e30
ADAM_LR = 0.001
ADAM_B1 = 0.9
ADAM_B2 = 0.999
ADAM_EPS = 1e-08
ADAM_WD = 0.01
ADAM_STEP = 10

N_DEV = 8
LANES = 128
QTILE = 2 * CHUNK
VMEM_LIMIT = 56 * 1024 * 1024
ADAM_TILE_ELEMS = 256 * 1024

MESH_ID = pl.DeviceIdType.MESH
ANY = pl.BlockSpec(memory_space=pl.ANY)
HBM = pl.BlockSpec(memory_space=pltpu.HBM)
SEM = pl.BlockSpec(memory_space=pltpu.SEMAPHORE)
SIDE_EFFECT = pltpu.SideEffectType.DATAFLOW_SIDE_EFFECTING


def _dot(a, b):
    return jnp.dot(a, b, preferred_element_type=F32)


def _dot_nt(a, b):
    return lax.dot_general(a, b, (((1,), (1,)), ((), ())), preferred_element_type=F32)


def _dot_tn(a, b):
    return lax.dot_general(a, b, (((0,), (0,)), ((), ())), preferred_element_type=F32)


def _params(sem=None, vmem=VMEM_LIMIT):
    return pltpu.CompilerParams(dimension_semantics=sem, vmem_limit_bytes=vmem)


def _row_tile(t, want):
    while t % want:
        want //= 2
    return want


def _place():
    return lax.axis_index("x"), lax.axis_index("y"), lax.axis_index("c")


def _all_gather(shards, name):
    n = len(shards)

    def body(*refs):
        ins, outs = refs[:n], refs[n:2 * n]
        send_sems, recv_sems, local_sems = refs[2 * n:]
        x, y, c = _place()
        me, sib = (x, y, c), (x, y, 1 - c)
        chips = [(1 - x, y), (x, 1 - y), (1 - x, 1 - y)]

        def copy(w, k, block, to, src=None):
            px, py, pc = block
            dst = outs[w].at[4 * px + 2 * py + pc]
            return pltpu.make_async_remote_copy(
                src_ref=dst if src is None else src, dst_ref=dst,
                send_sem=send_sems.at[w * 7 + k], recv_sem=recv_sems.at[w * 7 + k],
                device_id=to, device_id_type=MESH_ID)

        mine = [pltpu.make_async_copy(ins[w], outs[w].at[4 * x + 2 * y + c], local_sems.at[w]) for w in range(n)]
        for cp in mine:
            cp.start()
        first = []
        for w in range(n):
            first.append(copy(w, 0, me, sib, src=ins[w]))
            first += [copy(w, 1 + j, me, (*chip, c), src=ins[w]) for j, chip in enumerate(chips)]
        for cp in first:
            cp.start()
        passed = []
        for j, chip in enumerate(chips):
            for w in range(n):
                copy(w, 1 + j, (*chip, c), me).wait_recv()
                fwd = copy(w, 4 + j, (*chip, c), sib)
                fwd.start()
                passed.append(fwd)
        for w in range(n):
            copy(w, 0, sib, me).wait_recv()
        for j, chip in enumerate(chips):
            for w in range(n):
                copy(w, 4 + j, (*chip, 1 - c), me).wait_recv()
        for cp in first + passed:
            cp.wait_send()
        for cp in mine:
            cp.wait()

    return pl.pallas_call(
        body, name=name,
        out_shape=[jax.ShapeDtypeStruct((N_DEV,) + s.shape, s.dtype) for s in shards],
        in_specs=[ANY] * n, out_specs=[ANY] * n,
        scratch_shapes=[pltpu.SemaphoreType.DMA((7 * n,)), pltpu.SemaphoreType.DMA((7 * n,)),
                        pltpu.SemaphoreType.DMA((n,))],
    )(*shards)


def _gather_level(bufs, send_sems, recv_sems, level, shards=None):
    x, y, c = _place()
    me, sib = (x, y, c), (x, y, 1 - c)
    chips = [(1 - x, y), (x, 1 - y), (1 - x, 1 - y)]

    def copy(w, k, block, to):
        px, py, pc = block
        rows = bufs[w].at[4 * px + 2 * py + pc]
        src = shards[w] if shards is not None and block is me else rows
        return pltpu.make_async_remote_copy(src_ref=src, dst_ref=rows, send_sem=send_sems.at[k], recv_sem=recv_sems.at[k],
                                            device_id=to, device_id_type=MESH_ID)

    n = len(bufs)
    own = []
    if level == 1:
        own = [pltpu.make_async_copy(bufs[w].at[4 * x + 2 * y + c] if shards is None else shards[w],
                                     bufs[w].at[4 * x + 2 * y + c], send_sems.at[4 * n + w]) for w in range(n)]
    out, arriving = [], []
    for w in range(len(bufs)):
        if level == 1:
            out.append(copy(w, 4 * w, me, sib))
            arriving.append(copy(w, 4 * w, sib, me))
        for j, chip in enumerate(chips):
            if level == 1:
                out.append(copy(w, 4 * w + 1 + j, me, (*chip, c)))
                arriving.append(copy(w, 4 * w + 1 + j, (*chip, c), me))
            else:
                out.append(copy(w, 3 * w + j, (*chip, c), sib))
                arriving.append(copy(w, 3 * w + j, (*chip, 1 - c), me))
    return out, arriving, own


def _split_call(body, name, bufs, sems_in, after, n_sems_out, token, extra=()):
    n = len(bufs)
    out_shape = [pltpu.SemaphoreType.DMA((n_sems_out,))] * (2 if n_sems_out else 0)
    out_shape += [pltpu.HBM(a.shape, a.dtype) for a in bufs]
    out_specs = [SEM] * (2 if n_sems_out else 0) + [HBM] * n
    if token:
        out_shape.append(jax.ShapeDtypeStruct((8, LANES), F32))
        out_specs.append(pl.BlockSpec(memory_space=pltpu.VMEM))
    first = 2 if n_sems_out else 0
    return pl.pallas_call(
        body, name=name, out_shape=tuple(out_shape),
        in_specs=[HBM] * (n + len(extra)) + [SEM] * len(sems_in) + [ANY], out_specs=tuple(out_specs),
        input_output_aliases={i: first + i for i in range(n)},
        compiler_params=pltpu.CompilerParams(has_side_effects=SIDE_EFFECT),
    )(*bufs, *extra, *sems_in, after)


def _gather_start(shards, after, name):
    n = len(shards)
    hbm = lambda a: pltpu.with_memory_space_constraint(a, pltpu.HBM)
    bufs = [hbm(lax.empty((N_DEV,) + s.shape, s.dtype)) for s in shards]

    def body(*refs):
        out, _, own = _gather_level(refs[:n], refs[2 * n + 1], refs[2 * n + 2], 1, shards=refs[n:2 * n])
        for cp in own + out:
            cp.start()
        refs[-1][...] = jnp.zeros_like(refs[-1])

    outs = _split_call(body, name, bufs + [hbm(s) for s in shards], [], after, 5 * n, True)
    return outs[0], outs[1], list(outs[2:2 + 2 * n]), outs[-1]


def _gather_pass(send1, recv1, bufs_and_shards, after, name):
    n = len(bufs_and_shards) // 2
    bufs = bufs_and_shards

    def body(*refs):
        refs = refs[:n] + refs[2 * n:]
        out1, in1, own = _gather_level(refs[:n], refs[n], refs[n + 1], 1)
        out2, _, _ = _gather_level(refs[:n], refs[n + 3], refs[n + 4], 2)
        for cp in in1:
            cp.wait_recv()
        for cp in out2:
            cp.start()
        for cp in out1:
            cp.wait_send()
        for cp in own:
            cp.wait()
        refs[-1][...] = jnp.zeros_like(refs[-1])

    outs = _split_call(body, name, bufs, [send1, recv1], after, 3 * n, True)
    return outs[0], outs[1], list(outs[2:2 + n]), outs[-1]


def _gather_wait(send2, recv2, bufs, after, name):
    n = len(bufs)

    def body(*refs):
        out2, in2, _ = _gather_level(refs[:n], refs[n], refs[n + 1], 2)
        for cp in in2:
            cp.wait_recv()
        for cp in out2:
            cp.wait_send()

    return list(_split_call(body, name, bufs, [send2, recv2], after, 0, False))


def _scatter_copies(parts, lands, send_sems, recv_sems):
    x, y, c = _place()
    cps = []
    for w, (part, land) in enumerate(zip(parts, lands)):
        for k in range(1, N_DEV):
            px, py, pc = x ^ ((k >> 2) & 1), y ^ ((k >> 1) & 1), c ^ (k & 1)
            cps.append(pltpu.make_async_remote_copy(
                src_ref=part.at[4 * px + 2 * py + pc], dst_ref=land.at[k - 1],
                send_sem=send_sems.at[7 * w + k - 1], recv_sem=recv_sems.at[7 * w + k - 1],
                device_id=(px, py, pc), device_id_type=MESH_ID))
    return cps


def _scatter_start(parts, after, name):
    n = len(parts)

    def body(*refs):
        ins, lands = refs[:n], refs[n:2 * n]
        send_sems, recv_sems = refs[2 * n + 1], refs[2 * n + 2]
        token = refs[-1]
        for cp in _scatter_copies(ins, lands, send_sems, recv_sems):
            cp.start()
        token[...] = jnp.zeros_like(token)

    land_shapes = [(N_DEV - 1,) + p.shape[1:] for p in parts]
    in_hbm = [pltpu.with_memory_space_constraint(p, pltpu.HBM) for p in parts]
    in_hbm += [pltpu.with_memory_space_constraint(lax.empty(s, p.dtype), pltpu.HBM) for s, p in zip(land_shapes, parts)]
    outs = pl.pallas_call(
        body, name=name,
        out_shape=(pltpu.SemaphoreType.DMA((7 * n,)), pltpu.SemaphoreType.DMA((7 * n,)),
                   *[pltpu.HBM(p.shape, p.dtype) for p in parts],
                   *[pltpu.HBM(s, p.dtype) for s, p in zip(land_shapes, parts)],
                   jax.ShapeDtypeStruct((8, LANES), F32)),
        in_specs=[HBM] * (2 * n) + [ANY],
        out_specs=(SEM, SEM, *[HBM] * (2 * n), pl.BlockSpec(memory_space=pltpu.VMEM)),
        input_output_aliases={i: 2 + i for i in range(2 * n)},
        compiler_params=pltpu.CompilerParams(has_side_effects=SIDE_EFFECT),
    )(*in_hbm, after)
    return outs[0], outs[1], list(outs[2:2 + n]), list(outs[2 + n:2 + 2 * n]), outs[-1]


def _scatter_wait(send_sems, recv_sems, parts, lands, after, name):
    n = len(parts)

    def body(*refs):
        ins, lnd = refs[:n], refs[n:2 * n]
        for cp in _scatter_copies(ins, lnd, refs[2 * n], refs[2 * n + 1]):
            cp.wait_send()
            cp.wait_recv()

    outs = pl.pallas_call(
        body, name=name,
        out_shape=tuple(pltpu.HBM(a.shape, a.dtype) for a in parts + lands),
        in_specs=[HBM] * (2 * n) + [SEM, SEM, ANY],
        out_specs=tuple([HBM] * (2 * n)),
        input_output_aliases={i: i for i in range(2 * n)},
        compiler_params=pltpu.CompilerParams(has_side_effects=SIDE_EFFECT),
    )(*parts, *lands, send_sems, recv_sems, after)
    return list(outs[n:])


def _adam(w, g, m, v):
    m2 = ADAM_B1 * m + (1.0 - ADAM_B1) * g
    v2 = ADAM_B2 * v + (1.0 - ADAM_B2) * (g * g)
    m_hat = m2 / (1.0 - ADAM_B1 ** ADAM_STEP)
    v_hat = v2 / (1.0 - ADAM_B2 ** ADAM_STEP)
    delta = -ADAM_LR * (m_hat / (jnp.sqrt(v_hat) + ADAM_EPS) + ADAM_WD * w)
    return delta, m2, v2


def _small_allreduce_adam(part, w, m, v, name):
    rows = part.shape[0]

    def body(p_ref, w_ref, m_ref, v_ref, g_ref, d_ref, mo_ref, vo_ref, buf, send_sems, recv_sems):
        x, y, c = _place()
        buf[0] = p_ref[...]
        cps = []
        for k in range(1, N_DEV):
            kx, ky, kc = (k >> 2) & 1, (k >> 1) & 1, k & 1
            peer = (x ^ kx, y ^ ky, c ^ kc)
            cps.append(pltpu.make_async_remote_copy(
                src_ref=p_ref, dst_ref=buf.at[k], send_sem=send_sems.at[k - 1], recv_sem=recv_sems.at[k - 1],
                device_id=peer, device_id_type=MESH_ID))
        for cp in cps:
            cp.start()
        for cp in cps:
            cp.wait()
        me = 4 * x + 2 * y + c
        total = buf[me]
        for d in range(1, N_DEV):
            total = total + buf[d ^ me]
        g_ref[...] = total
        delta, m2, v2 = _adam(w_ref[...], total, m_ref[...], v_ref[...])
        d_ref[...] = delta
        mo_ref[...] = m2
        vo_ref[...] = v2

    vm = pl.BlockSpec(memory_space=pltpu.VMEM)
    return pl.pallas_call(
        body, name=name,
        out_shape=[jax.ShapeDtypeStruct(part.shape, F32)] * 4,
        in_specs=[vm] * 4, out_specs=[vm] * 4,
        scratch_shapes=[pltpu.VMEM((N_DEV, rows, LANES), F32),
                        pltpu.SemaphoreType.DMA((N_DEV - 1,)), pltpu.SemaphoreType.DMA((N_DEV - 1,))],
    )(part, w, m, v)


def _final_adam(g8, land, w, m, v, me, dep, name):
    _, r, c = g8.shape
    tr = max(q for q in range(16, r + 1, 16) if r % q == 0 and q * c <= ADAM_TILE_ELEMS)

    def body(me_ref, g_ref, land_ref, w_ref, m_ref, v_ref, dep_ref, go_ref, d_ref, mo_ref, vo_ref):
        del dep_ref
        g = g_ref[...]
        for k in range(N_DEV - 1):
            g = g + land_ref[k].astype(F32)
        go_ref[...] = g
        delta, m2, v2 = _adam(w_ref[...], g, m_ref[...], v_ref[...])
        d_ref[...] = delta
        mo_ref[...] = m2
        vo_ref[...] = v2

    plain = pl.BlockSpec((tr, c), lambda i, s: (i, 0))
    return pl.pallas_call(
        body, name=name,
        out_shape=[jax.ShapeDtypeStruct((r, c), F32)] * 4,
        grid_spec=pltpu.PrefetchScalarGridSpec(
            num_scalar_prefetch=1, grid=(r // tr,),
            in_specs=[pl.BlockSpec((None, tr, c), lambda i, s: (s[0], i, 0)),
                      pl.BlockSpec((N_DEV - 1, tr, c), lambda i, s: (0, i, 0)),
                      plain, plain, plain, ANY],
            out_specs=[plain] * 4),
        compiler_params=_params(("arbitrary",)),
    )(me, g8, land, w, m, v, dep)


def _rms(x, gain):
    r = lax.rsqrt(jnp.mean(x * x, axis=-1, keepdims=True) + EPS)
    xh = x * r
    return xh * gain, xh, r


def _rms_bwd(xh, r, gain, dy):
    gdy = gain * dy
    dx = r * (gdy - xh * jnp.mean(xh * gdy, axis=-1, keepdims=True))
    return dx, jnp.sum(dy * xh, axis=0, keepdims=True)


def _load_weights(pairs, sems):
    cps = [pltpu.make_async_copy(src, dst, sems.at[i]) for i, (src, dst) in enumerate(pairs)]
    for cp in cps:
        cp.start()
    for cp in cps:
        cp.wait()


def _ffn_fwd(h, gain, wgu, wd, name):
    t, d = h.shape
    nb, nf, _ = wgu.shape
    nh = nb // 2
    tm = _row_tile(t, 512)

    def body(h_ref, g_ref, wgu_hbm, wd_hbm, out_ref, gu_ref, wgu_v, wd_v, sems):
        @pl.when(pl.program_id(0) == 0)
        def _():
            _load_weights([(wgu_hbm, wgu_v), (wd_hbm, wd_v)], sems)

        x = h_ref[...]
        n, _, _ = _rms(x, g_ref[...])
        nbf = n.astype(BF16)
        acc = jnp.zeros((tm, d), F32)
        for j in range(nh):
            g = _dot_nt(nbf, wgu_v[j])
            u = _dot_nt(nbf, wgu_v[j + nh])
            gu_ref[j] = g.astype(BF16)
            gu_ref[j + nh] = u.astype(BF16)
            a = (g * jax.nn.sigmoid(g)) * u
            acc = acc + _dot(a.astype(BF16), wd_v[j])
        out_ref[...] = x + 0.5 * acc

    return pl.pallas_call(
        body, name=name, grid=(t // tm,),
        out_shape=[jax.ShapeDtypeStruct((t, d), F32), jax.ShapeDtypeStruct((nb, t, nf), BF16)],
        in_specs=[pl.BlockSpec((tm, d), lambda i: (i, 0)), pl.BlockSpec((1, d), lambda i: (0, 0)), ANY, ANY],
        out_specs=[pl.BlockSpec((tm, d), lambda i: (i, 0)), pl.BlockSpec((nb, tm, nf), lambda i: (0, i, 0))],
        scratch_shapes=[pltpu.VMEM(wgu.shape, BF16), pltpu.VMEM(wd.shape, BF16), pltpu.SemaphoreType.DMA((2,))],
        compiler_params=_params(("arbitrary",)),
    )(h, gain, wgu, wd)


def _ffn_bwd(dh, h, gain, gu, wgu, wd, name):
    t, d = h.shape
    nb, nf, _ = wgu.shape
    nh = nb // 2
    tm = _row_tile(t, 256)

    def body(dh_ref, h_ref, g_ref, gu_ref, wgu_hbm, wd_hbm, dhp_ref, dgu_ref, a_ref, n_ref, dgain_ref,
             wgu_v, wd_v, sems):
        @pl.when(pl.program_id(0) == 0)
        def _():
            _load_weights([(wgu_hbm, wgu_v), (wd_hbm, wd_v)], sems)
            dgain_ref[...] = jnp.zeros_like(dgain_ref)

        x = h_ref[...]
        gain_v = g_ref[...]
        n, xh, r = _rms(x, gain_v)
        n_ref[...] = n.astype(BF16)
        dh_v = dh_ref[...]
        dfb = (0.5 * dh_v).astype(BF16)
        dn = jnp.zeros((tm, d), F32)
        for j in range(nh):
            da = _dot_nt(dfb, wd_v[j])
            g = gu_ref[j].astype(F32)
            u = gu_ref[j + nh].astype(F32)
            sg = jax.nn.sigmoid(g)
            si = g * sg
            dg = (da * u * (sg * (1.0 + g * (1.0 - sg)))).astype(BF16)
            du = (da * si).astype(BF16)
            a_ref[j] = (si * u).astype(BF16)
            dgu_ref[j] = dg
            dgu_ref[j + nh] = du
            dn = dn + _dot(dg, wgu_v[j]) + _dot(du, wgu_v[j + nh])
        dx, dgain = _rms_bwd(xh, r, gain_v, dn)
        dhp_ref[...] = dh_v + dx
        dgain_ref[...] += dgain

    row = pl.BlockSpec((tm, d), lambda i: (i, 0))
    vec = pl.BlockSpec((1, d), lambda i: (0, 0))
    return pl.pallas_call(
        body, name=name, grid=(t // tm,),
        out_shape=[jax.ShapeDtypeStruct((t, d), F32), jax.ShapeDtypeStruct((nb, t, nf), BF16),
                   jax.ShapeDtypeStruct((nh, t, nf), BF16), jax.ShapeDtypeStruct((t, d), BF16),
                   jax.ShapeDtypeStruct((1, d), F32)],
        in_specs=[row, row, vec, pl.BlockSpec((nb, tm, nf), lambda i: (0, i, 0)), ANY, ANY],
        out_specs=[row, pl.BlockSpec((nb, tm, nf), lambda i: (0, i, 0)),
                   pl.BlockSpec((nh, tm, nf), lambda i: (0, i, 0)), row, vec],
        scratch_shapes=[pltpu.VMEM(wgu.shape, BF16), pltpu.VMEM(wd.shape, BF16), pltpu.SemaphoreType.DMA((2,))],
        compiler_params=_params(("arbitrary",)),
    )(dh, h, gain, gu, wgu, wd)


def _dw(xa, dy, nb, n, name, scale=1.0, dep=None):
    t, k = xa.shape[-2:]
    tt = _row_tile(t, 512)
    steps = t // tt
    wide = dy.ndim == 2 and xa.ndim == 2
    if xa.ndim == 3:
        x_spec = pl.BlockSpec((nb, tt, k), lambda i: (0, i, 0))
    else:
        x_spec = pl.BlockSpec((tt, k), lambda i: (i, 0))
    if dy.ndim == 3:
        dy_spec = pl.BlockSpec((nb, tt, n), lambda i: (0, i, 0))
    else:
        dy_spec = pl.BlockSpec((tt, dy.shape[1]), lambda i: (i, 0))
    acc_shape = (k, nb * n) if wide else (nb, k, n)
    stage_shape = (k, nb * n) if wide else (k, n)

    def body(x_ref, dy_ref, *rest):
        o_hbm, ob_hbm, acc, stage, sems = rest[-5:]

        @pl.when(pl.program_id(0) == 0)
        def _():
            acc[...] = jnp.zeros_like(acc)

        if wide:
            acc[...] += _dot(x_ref[...].astype(BF16).T, dy_ref[...].astype(BF16))
        elif xa.ndim == 2:
            xt = x_ref[...].astype(BF16).T
            for j in range(nb):
                acc[j] += _dot(xt, dy_ref[j].astype(BF16))
        else:
            dyb = dy_ref[...].astype(BF16)
            for j in range(nb):
                acc[j] += _dot_tn(x_ref[j].astype(BF16), dyb)

        @pl.when(pl.program_id(0) == steps - 1)
        def _():
            if scale != 1.0:
                acc[...] = acc[...] * scale
            if wide:
                cps = [pltpu.make_async_copy(acc.at[:, pl.ds(j * n, n)] if nb > 1 else acc, o_hbm.at[j], sems.at[j])
                       for j in range(nb)]
            else:
                cps = [pltpu.make_async_copy(acc, o_hbm, sems.at[0])]
            for cp in cps:
                cp.start()
            if wide:
                stage[...] = acc[...].astype(BF16)
                bcs = [pltpu.make_async_copy(stage.at[:, pl.ds(j * n, n)] if nb > 1 else stage, ob_hbm.at[j],
                                             sems.at[nb + j]) for j in range(nb)]
                for cp in bcs:
                    cp.start()
                for cp in bcs:
                    cp.wait()
            else:
                for j in range(nb):
                    stage[...] = acc[j].astype(BF16)
                    cp = pltpu.make_async_copy(stage, ob_hbm.at[j], sems.at[nb])
                    cp.start()
                    cp.wait()
            for cp in cps:
                cp.wait()

    return pl.pallas_call(
        body, name=name, grid=(steps,),
        out_shape=[jax.ShapeDtypeStruct((nb, k, n), F32), jax.ShapeDtypeStruct((nb, k, n), BF16)],
        in_specs=[x_spec, dy_spec] + ([] if dep is None else [ANY]),
        out_specs=[ANY, ANY],
        scratch_shapes=[pltpu.VMEM(acc_shape, F32), pltpu.VMEM(stage_shape, BF16),
                        pltpu.SemaphoreType.DMA((2 * nb,))],
        compiler_params=_params(("arbitrary",)),
    )(*((xa, dy) if dep is None else (xa, dy, dep)))


def _proj_fwd(h, gain, win, wgate, name):
    t, d = h.shape
    tm = _row_tile(t, 256)
    nq, ng = win.shape[0], wgate.shape[1]

    def body(h_ref, g_ref, win_ref, wg_ref, un_ref, qkv_ref, gate_ref):
        n, _, _ = _rms(h_ref[...], g_ref[...])
        nbf = n.astype(BF16)
        un_ref[...] = nbf
        qkv_ref[...] = _dot_nt(nbf, win_ref[...])
        gate_ref[...] = jax.nn.sigmoid(_dot(nbf, wg_ref[...]))

    full = lambda a: pl.BlockSpec(a.shape, lambda i: (0,) * a.ndim)
    return pl.pallas_call(
        body, name=name, grid=(t // tm,),
        out_shape=[jax.ShapeDtypeStruct((t, d), BF16), jax.ShapeDtypeStruct((t, nq), F32),
                   jax.ShapeDtypeStruct((t, ng), F32)],
        in_specs=[pl.BlockSpec((tm, d), lambda i: (i, 0)), full(gain), full(win), full(wgate)],
        out_specs=[pl.BlockSpec((tm, d), lambda i: (i, 0)), pl.BlockSpec((tm, nq), lambda i: (i, 0)),
                   pl.BlockSpec((tm, ng), lambda i: (i, 0))],
        compiler_params=_params(("arbitrary",)),
    )(h, gain, win, wgate)


def _proj_bwd(dh, h, gain, dzg, dqkv_parts, win, wgate, name):
    t, d = h.shape
    tm = _row_tile(t, 256)
    ng = wgate.shape[1]
    np_ = len(dqkv_parts)
    widths = [a.shape[1] for a in dqkv_parts]

    def body(dh_ref, h_ref, g_ref, dzg_ref, *rest):
        part_refs, (win_ref, wg_ref, dhp_ref, dgain_ref) = rest[:np_], rest[np_:]

        @pl.when(pl.program_id(0) == 0)
        def _():
            dgain_ref[...] = jnp.zeros_like(dgain_ref)

        gain_v = g_ref[...]
        _, xh, r = _rms(h_ref[...], gain_v)
        dun = _dot_nt(dzg_ref[...], wg_ref[...])
        off = 0
        for ref, wd in zip(part_refs, widths):
            dun = dun + _dot(ref[...].astype(BF16), win_ref[off:off + wd, :])
            off += wd
        dx, dgain = _rms_bwd(xh, r, gain_v, dun)
        dhp_ref[...] = dh_ref[...] + dx
        dgain_ref[...] += dgain

    full = lambda a: pl.BlockSpec(a.shape, lambda i: (0,) * a.ndim)
    row = pl.BlockSpec((tm, d), lambda i: (i, 0))
    return pl.pallas_call(
        body, name=name, grid=(t // tm,),
        out_shape=[jax.ShapeDtypeStruct((t, d), F32), jax.ShapeDtypeStruct((1, d), F32)],
        in_specs=[row, row, full(gain), pl.BlockSpec((tm, ng), lambda i: (i, 0))]
        + [pl.BlockSpec((tm, wd), lambda i: (i, 0)) for wd in widths] + [full(win), full(wgate)],
        out_specs=[row, pl.BlockSpec((1, d), lambda i: (0, 0))],
        compiler_params=_params(("arbitrary",)),
    )(dh, h, gain, dzg, *dqkv_parts, win, wgate)


def _dw_rows(parts, dy, name):
    t, n = dy.shape
    widths = [a.shape[1] for a in parts]
    k = sum(widths)
    tt = _row_tile(t, 512)
    steps = t // tt
    np_ = len(parts)

    def body(*refs):
        part_refs, dy_ref = refs[:np_], refs[np_]
        o_hbm, ob_hbm, acc, stage, sems = refs[np_ + 1:]

        @pl.when(pl.program_id(0) == 0)
        def _():
            acc[...] = jnp.zeros_like(acc)

        dyb = dy_ref[...].astype(BF16)
        off = 0
        for ref, wd in zip(part_refs, widths):
            acc[off:off + wd, :] += _dot(ref[...].astype(BF16).T, dyb)
            off += wd

        @pl.when(pl.program_id(0) == steps - 1)
        def _():
            stage[...] = acc[...].astype(BF16)
            cps = [pltpu.make_async_copy(acc, o_hbm.at[0], sems.at[0]),
                   pltpu.make_async_copy(stage, ob_hbm.at[0], sems.at[1])]
            for cp in cps:
                cp.start()
            for cp in cps:
                cp.wait()

    return pl.pallas_call(
        body, name=name, grid=(steps,),
        out_shape=[jax.ShapeDtypeStruct((1, k, n), F32), jax.ShapeDtypeStruct((1, k, n), BF16)],
        in_specs=[pl.BlockSpec((tt, wd), lambda i: (i, 0)) for wd in widths] + [pl.BlockSpec((tt, n), lambda i: (i, 0))],
        out_specs=[ANY, ANY],
        scratch_shapes=[pltpu.VMEM((k, n), F32), pltpu.VMEM((k, n), BF16), pltpu.SemaphoreType.DMA((2,))],
        compiler_params=_params(("arbitrary",)),
    )(*parts, dy)


def _merge_fwd(h, ya, yb, gate, wpa, wpb, wout, name):
    t, d = h.shape
    tm = _row_tile(t, 256)

    def body(h_ref, ya_ref, yb_ref, ga_ref, gb_ref, wpa_ref, wpb_ref, wout_ref, out_ref, mg_ref, pa_ref, pb_ref):
        pa = _dot(ya_ref[...].astype(BF16), wpa_ref[...])
        pb = _dot(yb_ref[...].astype(BF16), wpb_ref[...])
        merged = (ga_ref[...] * pa + gb_ref[...] * pb).astype(BF16)
        pa_ref[...] = pa.astype(BF16)
        pb_ref[...] = pb.astype(BF16)
        mg_ref[...] = merged
        out_ref[...] = h_ref[...] + _dot(merged, wout_ref[...])

    full = lambda a: pl.BlockSpec(a.shape, lambda i: (0,) * a.ndim)
    row = pl.BlockSpec((tm, d), lambda i: (i, 0))
    yrow = pl.BlockSpec((tm, ya.shape[1]), lambda i: (i, 0))
    return pl.pallas_call(
        body, name=name, grid=(t // tm,),
        out_shape=[jax.ShapeDtypeStruct((t, d), F32)] + [jax.ShapeDtypeStruct((t, d), BF16)] * 3,
        in_specs=[row, yrow, yrow, pl.BlockSpec((tm, d), lambda i: (i, 0)), pl.BlockSpec((tm, d), lambda i: (i, 1)),
                  full(wpa), full(wpb), full(wout)],
        out_specs=[row] * 4,
        compiler_params=_params(("arbitrary",)),
    )(h, ya, yb, gate, gate, wpa, wpb, wout)


def _merge_bwd(dh, pa, pb, gate, wpa, wpb, wout, name):
    t, d = dh.shape
    tm = _row_tile(t, 256)
    wy = wpa.shape[0]

    def body(dh_ref, pa_ref, pb_ref, ga_ref, gb_ref, wpa_ref, wpb_ref, wout_ref,
             dpa_ref, dpb_ref, dzg_ref, dya_ref, dyb_ref):
        dm = _dot_nt(dh_ref[...].astype(BF16), wout_ref[...])
        ga, gb = ga_ref[...], gb_ref[...]
        dpa = (dm * ga).astype(BF16)
        dpb = (dm * gb).astype(BF16)
        dpa_ref[...] = dpa
        dpb_ref[...] = dpb
        dzg_ref[:, :d] = (dm * pa_ref[...].astype(F32) * ga * (1.0 - ga)).astype(BF16)
        dzg_ref[:, d:] = (dm * pb_ref[...].astype(F32) * gb * (1.0 - gb)).astype(BF16)
        dya_ref[...] = _dot_nt(dpa, wpa_ref[...])
        dyb_ref[...] = _dot_nt(dpb, wpb_ref[...])

    full = lambda a: pl.BlockSpec(a.shape, lambda i: (0,) * a.ndim)
    row = pl.BlockSpec((tm, d), lambda i: (i, 0))
    yrow = pl.BlockSpec((tm, wy), lambda i: (i, 0))
    return pl.pallas_call(
        body, name=name, grid=(t // tm,),
        out_shape=[jax.ShapeDtypeStruct((t, d), BF16), jax.ShapeDtypeStruct((t, d), BF16),
                   jax.ShapeDtypeStruct((t, 2 * d), BF16), jax.ShapeDtypeStruct((t, wy), F32),
                   jax.ShapeDtypeStruct((t, wy), F32)],
        in_specs=[row, row, row, pl.BlockSpec((tm, d), lambda i: (i, 0)), pl.BlockSpec((tm, d), lambda i: (i, 1)),
                  full(wpa), full(wpb), full(wout)],
        out_specs=[row, row, pl.BlockSpec((tm, 2 * d), lambda i: (i, 0)), yrow, yrow],
        compiler_params=_params(("arbitrary",)),
    )(dh, pa, pb, gate, gate, wpa, wpb, wout)


def _ple_loss(h, gain, p, target, wpg, wpe, name):
    t, d = h.shape
    tm = _row_tile(t, 256)
    pd = p.shape[1]

    def body(h_ref, g_ref, p_ref, t_ref, wpg_ref, wpe_ref, dh_ref, dz_ref, dpp_ref, n_ref, dgain_ref, loss_ref):
        @pl.when(pl.program_id(0) == 0)
        def _():
            dgain_ref[...] = jnp.zeros_like(dgain_ref)
            loss_ref[...] = jnp.zeros_like(loss_ref)

        x = h_ref[...]
        gain_v = g_ref[...]
        n, xh, r = _rms(x, gain_v)
        nbf = n.astype(BF16)
        n_ref[...] = nbf
        pg = jax.nn.sigmoid(_dot(nbf, wpg_ref[...]))
        pp = _dot(p_ref[...].astype(BF16), wpe_ref[...])
        err = (x + pg * pp) - t_ref[...]
        loss_ref[...] += 0.5 * jnp.sum(jnp.mean(err * err, axis=-1, keepdims=True))
        dy = err * (1.0 / d)
        dpp_ref[...] = (dy * pg).astype(BF16)
        dz = (dy * pp * pg * (1.0 - pg)).astype(BF16)
        dz_ref[...] = dz
        dn = _dot_nt(dz, wpg_ref[...])
        dx, dgain = _rms_bwd(xh, r, gain_v, dn)
        dh_ref[...] = dy + dx
        dgain_ref[...] += dgain

    full = lambda a: pl.BlockSpec(a.shape, lambda i: (0,) * a.ndim)
    row = pl.BlockSpec((tm, d), lambda i: (i, 0))
    return pl.pallas_call(
        body, name=name, grid=(t // tm,),
        out_shape=[jax.ShapeDtypeStruct((t, d), F32), jax.ShapeDtypeStruct((t, d), BF16),
                   jax.ShapeDtypeStruct((t, d), BF16), jax.ShapeDtypeStruct((t, d), BF16),
                   jax.ShapeDtypeStruct((1, d), F32), jax.ShapeDtypeStruct((8, LANES), F32)],
        in_specs=[row, full(gain), pl.BlockSpec((tm, pd), lambda i: (i, 0)), row, full(wpg), full(wpe)],
        out_specs=[row, row, row, row, pl.BlockSpec((1, d), lambda i: (0, 0)),
                   pl.BlockSpec((8, LANES), lambda i: (0, 0))],
        compiler_params=_params(("arbitrary",)),
    )(h, gain, p, target, wpg, wpe)


def _head_masks():
    lane = lax.broadcasted_iota(jnp.int32, (1, LANES), 1)
    m0 = (lane < HEAD_DIM).astype(F32)
    return m0, 1.0 - m0


def _head_mean(v, m0, m1):
    del m0, m1
    width = v.shape[-1]
    shift = HEAD_DIM.bit_length() - 1
    r = jnp.right_shift(lax.broadcasted_iota(jnp.int32, (width, width), 0), shift)
    c = jnp.right_shift(lax.broadcasted_iota(jnp.int32, (width, width), 1), shift)
    same_head = (r == c).astype(BF16)
    hi = v.astype(BF16)
    lo = (v - hi.astype(F32)).astype(BF16)
    return (_dot(hi, same_head) + _dot(lo, same_head)) * (1.0 / HEAD_DIM)


def _head_norm(x, gain, m0, m1):
    r = lax.rsqrt(_head_mean(x * x, m0, m1) + EPS)
    xh = x * r
    return xh * gain, xh, r


def _head_norm_bwd(xh, r, gain, dy, m0, m1):
    gdy = gain * dy
    dx = r * (gdy - xh * _head_mean(xh * gdy, m0, m1))
    return dx, jnp.sum(dy * xh, axis=0, keepdims=True)


GROUP = 4
QW = GROUP * HEAD_DIM
STACK = GROUP * QTILE


def _kv_width(mode):
    return QW if mode == "A" else LANES


def _q_scratch_shape(mode, s_len):
    return (s_len, QW) if mode == "A" else (GROUP * s_len, LANES)


def _group_masks(dtype=F32):
    lane = lax.broadcasted_iota(jnp.int32, (1, QW), 1)
    return [((lane >= h * HEAD_DIM) & (lane < (h + 1) * HEAD_DIM)).astype(dtype) for h in range(GROUP)]


def _stack_heads(first_kv, x, m0, m1):
    out = []
    for half in range(GROUP // 2):
        xh = x[:, half * LANES:(half + 1) * LANES]
        a0, a1 = xh * m0, xh * m1
        r0, r1 = pltpu.roll(a0, HEAD_DIM, 1), pltpu.roll(a1, HEAD_DIM, 1)
        out += [jnp.where(first_kv, a0, r0), jnp.where(first_kv, r1, a1)]
    return out


def _unstack_heads(mode, first_kv, ts, m0, m1):
    if mode == "A":
        masks = _group_masks()
        return sum(t * mk for t, mk in zip(ts, masks))
    halves = []
    for half in range(GROUP // 2):
        t0 = jnp.where(first_kv, ts[2 * half], pltpu.roll(ts[2 * half], HEAD_DIM, 1))
        t1 = jnp.where(first_kv, pltpu.roll(ts[2 * half + 1], HEAD_DIM, 1), ts[2 * half + 1])
        halves.append(t0 * m0 + t1 * m1)
    return jnp.concatenate(halves, axis=1)


def _store_stacked(dst, i, heads):
    for half in range(2):
        rows = slice(half * QTILE, (half + 1) * QTILE)
        for h, x in enumerate(heads):
            dst[pl.ds((2 * i + half) * STACK + h * QTILE, QTILE), :] = x[rows].astype(dst.dtype)


def _load_stacked(mode, ref, m):
    if mode == "B":
        return ref[pl.ds(pl.multiple_of(m * STACK, STACK), STACK), :]
    x = ref[pl.ds(pl.multiple_of(m * QTILE, QTILE), QTILE), :]
    return jnp.concatenate([x * mk for mk in _group_masks(x.dtype)], axis=0)


def _attn_prep(mode, group, s_len, padk, q_ref, k_ref, v_ref, gq_ref, gk_ref, qs, k2, v2, do_ref=None, dos=None):
    m0, m1 = _head_masks()
    zpad = jnp.zeros((padk, k2.shape[1]), BF16)
    k2[pl.ds(0, padk), :] = zpad
    v2[pl.ds(0, padk), :] = zpad
    first_kv = group == 0
    rt = 2 * QTILE
    for i in range(s_len // rt):
        rows = pl.ds(i * rt, rt)
        qn, _, _ = _head_norm(q_ref[rows, :], gq_ref[...], m0, m1)
        kn, _, _ = _head_norm(k_ref[rows, :], gk_ref[...], m0, m1)
        qn = qn * (HEAD_DIM ** -0.5)
        if mode == "A":
            qs[rows, :] = qn.astype(BF16)
            if dos is not None:
                dos[rows, :] = do_ref[rows, :].astype(BF16)
        else:
            _store_stacked(qs, i, _stack_heads(first_kv, qn, m0, m1))
            if dos is not None:
                _store_stacked(dos, i, _stack_heads(first_kv, do_ref[rows, :], m0, m1))
        k2[pl.ds(padk + i * rt, rt), :] = kn.astype(BF16)
        v2[pl.ds(padk + i * rt, rt), :] = v_ref[rows, :].astype(BF16)


def _attn_probs(mode, q_st, kb, bias, ok, sink):
    s = _dot_nt(q_st, kb) + bias
    s = jnp.where(ok, s, NEG_INF)
    mx = jnp.max(s, axis=-1, keepdims=True)
    if mode == "B":
        mx = jnp.maximum(mx, sink)
    e = jnp.exp(s - mx)
    l = jnp.sum(e, axis=-1, keepdims=True)
    if mode == "B":
        l = l + jnp.exp(sink - mx)
    return e, mx, l


def _sink_column(sink_ref, group):
    row = lax.broadcasted_iota(jnp.int32, (STACK, 1), 0)
    col = jnp.zeros((STACK, 1), F32)
    for h in range(GROUP):
        col = jnp.where((row >= h * QTILE) & (row < (h + 1) * QTILE), sink_ref[GROUP * group + h], col)
    return col


def _head_deltas(dd, m0, m1):
    cols = []
    for half in range(GROUP // 2):
        dh = dd[:, half * LANES:(half + 1) * LANES]
        cols += [jnp.sum(dh * m0, axis=-1, keepdims=True), jnp.sum(dh * m1, axis=-1, keepdims=True)]
    return jnp.concatenate(cols, axis=0)


def _attn_cols(mode):
    if mode == "A":
        return (lambda b, g: (b, g)), (lambda b, g: (b, 2 + g)), (lambda b, g: (b, 4 + g))
    return (lambda b, g: (b, 6 + g)), (lambda b, g: (b, 16)), (lambda b, g: (b, 17))


def _attn_fwd(mode, qkv, gq, gk, bias, sinks, bl, s_len, name):
    bw = bias.shape[-1]
    padk = bw - QTILE
    nt = s_len // QTILE
    qmap, kmap, vmap = _attn_cols(mode)

    kw = _kv_width(mode)

    def body(q_ref, k_ref, v_ref, gq_ref, gk_ref, bias_ref, sink_ref, o_ref, qs, k2, v2):
        group = pl.program_id(1)
        m0, m1 = _head_masks()
        first_kv = group == 0
        _attn_prep(mode, group, s_len, padk, q_ref, k_ref, v_ref, gq_ref, gk_ref, qs, k2, v2)
        col = lax.broadcasted_iota(jnp.int32, (STACK, bw), 1)
        sink = _sink_column(sink_ref, group)

        def tile(m, carry):
            r0 = pl.multiple_of(m * QTILE, QTILE)
            q_st = _load_stacked(mode, qs, m)
            ok = col >= (padk - r0)
            e, _, l = _attn_probs(mode, q_st, k2[pl.ds(r0, bw), :], bias_ref[...], ok, sink)
            o_st = _dot(e.astype(BF16), v2[pl.ds(r0, bw), :]) / l
            heads = [o_st[h * QTILE:(h + 1) * QTILE] for h in range(GROUP)]
            o_ref[pl.ds(r0, QTILE), :] = _unstack_heads(mode, first_kv, heads, m0, m1)
            return carry

        lax.fori_loop(0, nt, tile, 0, unroll=2)

    blk = lambda w, f: pl.BlockSpec((s_len, w), f)
    return pl.pallas_call(
        body, name=name, grid=(bl, B_Q_HEADS // GROUP),
        out_shape=jax.ShapeDtypeStruct((bl * s_len, B_Q_HEADS * HEAD_DIM), F32),
        in_specs=[blk(QW, qmap), blk(kw, kmap), blk(kw, vmap),
                  pl.BlockSpec((1, QW), lambda b, g: (0, 0)), pl.BlockSpec((1, kw), lambda b, g: (0, 0)),
                  pl.BlockSpec((STACK, bw), lambda b, g: (g, 0)),
                  pl.BlockSpec(memory_space=pltpu.SMEM)],
        out_specs=blk(QW, lambda b, g: (b, g)),
        scratch_shapes=[pltpu.VMEM(_q_scratch_shape(mode, s_len), BF16)] + [pltpu.VMEM((s_len + padk, kw), BF16)] * 2,
        compiler_params=_params(("arbitrary", "arbitrary")),
    )(qkv, qkv, qkv, gq, gk, bias.reshape(-1, bw), sinks)


def _attn_bwd(mode, qkv, gq, gk, bias, sinks, y, dy, bl, s_len, name):
    bw = bias.shape[-1]
    padk = bw - QTILE
    nt = s_len // QTILE
    qmap, kmap, vmap = _attn_cols(mode)
    t = bl * s_len
    kw = _kv_width(mode)
    kvw = 4 * LANES if mode == "A" else LANES

    def body(q_ref, k_ref, v_ref, gq_ref, gk_ref, bias_ref, sink_ref, y_ref, dy_ref,
             dq_ref, dk_ref, dv_ref, dgq_ref, dgk_ref, dbias_ref, dsink_ref,
             qs, k2, v2, dos, dqs, dk, dv):
        group = pl.program_id(1)
        m0, m1 = _head_masks()
        first_kv = group == 0
        _attn_prep(mode, group, s_len, padk, q_ref, k_ref, v_ref, gq_ref, gk_ref, qs, k2, v2, dy_ref, dos)
        dk[...] = jnp.zeros_like(dk)
        dv[...] = jnp.zeros_like(dv)
        dbias_ref[...] = jnp.zeros_like(dbias_ref)
        col = lax.broadcasted_iota(jnp.int32, (STACK, bw), 1)
        lane8 = lax.broadcasted_iota(jnp.int32, (8, LANES), 1)
        sink = _sink_column(sink_ref, group)

        def tile(m, dsink):
            r0 = pl.multiple_of(m * QTILE, QTILE)
            rows = pl.ds(r0, QTILE)
            band = pl.ds(r0, bw)
            q_st = _load_stacked(mode, qs, m)
            do_st = _load_stacked(mode, dos, m)
            delta = _head_deltas(dy_ref[rows, :] * y_ref[rows, :], m0, m1)
            ok = col >= (padk - r0)
            kb = k2[band, :]
            e, mx, l = _attn_probs(mode, q_st, kb, bias_ref[...], ok, sink)
            inv = 1.0 / l
            pn = e * inv
            dp = _dot_nt(do_st, v2[band, :])
            ds = pn * (dp - delta)
            if mode == "A":
                dbias_ref[...] += ds
            else:
                part = jnp.exp(sink - mx) * inv * delta
                for h in range(GROUP):
                    dsink = dsink - jnp.where(lane8 == h, jnp.sum(part[h * QTILE:(h + 1) * QTILE]), 0.0)
            dsb = ds.astype(BF16)
            dv[band, :] += _dot_tn(pn.astype(BF16), do_st)
            dk[band, :] += _dot_tn(dsb, q_st)
            dq_st = _dot(dsb, kb)
            if mode == "A":
                heads = [dq_st[h * QTILE:(h + 1) * QTILE] for h in range(GROUP)]
                dqs[rows, :] = _unstack_heads(mode, first_kv, heads, m0, m1)
            else:
                dqs[pl.ds(pl.multiple_of(m * STACK, STACK), STACK), :] = dq_st
            return dsink

        dsink = lax.fori_loop(0, nt, tile, jnp.zeros((8, LANES), F32), unroll=2)
        dsink_ref[...] = dsink

        rt = 2 * QTILE
        dgq = jnp.zeros((1, QW), F32)
        dgk = jnp.zeros((1, kw), F32)
        for i in range(s_len // rt):
            rows = pl.ds(i * rt, rt)
            src = pl.ds(padk + i * rt, rt)
            gq_v, gk_v = gq_ref[...], gk_ref[...]
            _, qh, qr = _head_norm(q_ref[rows, :], gq_v, m0, m1)
            _, kh, kr = _head_norm(k_ref[rows, :], gk_v, m0, m1)
            if mode == "A":
                dqn = dqs[rows, :] * (HEAD_DIM ** -0.5)
            else:
                dqn = jnp.concatenate(
                    [_unstack_heads(mode, first_kv, [dqs[pl.ds((2 * i + half) * STACK + h * QTILE, QTILE), :]
                                                     for h in range(GROUP)], m0, m1)
                     for half in range(2)], axis=0) * (HEAD_DIM ** -0.5)
            dq_raw, dgq_i = _head_norm_bwd(qh, qr, gq_v, dqn, m0, m1)
            dk_raw, dgk_i = _head_norm_bwd(kh, kr, gk_v, dk[src, :], m0, m1)
            dvn = dv[src, :]
            dq_ref[rows, :] = dq_raw
            if mode == "A":
                dk_ref[rows, :] = dk_raw
                dv_ref[rows, :] = dvn
            else:
                @pl.when(group == 0)
                def _():
                    dk_ref[rows, :] = dk_raw
                    dv_ref[rows, :] = dvn

                @pl.when(group != 0)
                def _():
                    dk_ref[rows, :] += dk_raw
                    dv_ref[rows, :] += dvn
            dgq, dgk = dgq + dgq_i, dgk + dgk_i
        dgq_ref[...] = jnp.broadcast_to(dgq, (8, QW))
        dgk_ref[...] = jnp.broadcast_to(dgk, (8, kw))

    ng = B_Q_HEADS // GROUP
    blk = lambda w, f: pl.BlockSpec((s_len, w), f)
    small = lambda w: pl.BlockSpec((None, None, 8, w), lambda b, g: (b, g, 0, 0))
    own = lambda b, g: (b, g)
    kvmap = own if mode == "A" else (lambda b, g: (b, 0))
    pad_f32 = pltpu.VMEM((s_len + padk, kw), F32)
    pad_bf = pltpu.VMEM((s_len + padk, kw), BF16)
    stack_bf = pltpu.VMEM(_q_scratch_shape(mode, s_len), BF16)
    outs = pl.pallas_call(
        body, name=name, grid=(bl, ng),
        out_shape=[jax.ShapeDtypeStruct((t, ng * QW), F32), jax.ShapeDtypeStruct((t, kvw), F32),
                   jax.ShapeDtypeStruct((t, kvw), F32),
                   jax.ShapeDtypeStruct((bl, ng, 8, QW), F32), jax.ShapeDtypeStruct((bl, ng, 8, kw), F32),
                   jax.ShapeDtypeStruct((bl, ng * STACK, bw), F32), jax.ShapeDtypeStruct((bl, ng, 8, LANES), F32)],
        in_specs=[blk(QW, qmap), blk(kw, kmap), blk(kw, vmap),
                  pl.BlockSpec((1, QW), lambda b, g: (0, 0)), pl.BlockSpec((1, kw), lambda b, g: (0, 0)),
                  pl.BlockSpec((STACK, bw), lambda b, g: (g, 0)),
                  pl.BlockSpec(memory_space=pltpu.SMEM),
                  blk(QW, own), blk(QW, own)],
        out_specs=[blk(QW, own), blk(kw, kvmap), blk(kw, kvmap), small(QW), small(kw),
                   pl.BlockSpec((None, STACK, bw), lambda b, g: (b, g, 0)), small(LANES)],
        scratch_shapes=[stack_bf, pad_bf, pad_bf, stack_bf, pltpu.VMEM(_q_scratch_shape(mode, s_len), F32),
                        pad_f32, pad_f32],
        compiler_params=_params(("arbitrary", "arbitrary")),
    )(qkv, qkv, qkv, gq, gk, bias.reshape(-1, bw), sinks, y, dy)
    outs = list(outs)
    outs[5] = outs[5].reshape(bl, B_Q_HEADS, QTILE, bw)
    return outs


def _band_geometry(prev):
    bw = QTILE + prev * CHUNK
    i = np.arange(QTILE)[:, None]
    j = np.arange(bw)[None, :]
    dist = i + prev * CHUNK - j
    valid = (j // CHUNK >= i // CHUNK) & (j // CHUNK <= i // CHUNK + prev)
    return dist, valid


A_VAR0 = (A_PREV * CHUNK - A_MAX_REL) // LANES * LANES


A_NVAR = QTILE + A_PREV * CHUNK - A_VAR0


def _skew_rows(x, sign):
    rows, n = x.shape
    row = lax.broadcasted_iota(jnp.int32, x.shape, 0)
    b = 1
    while b < rows:
        x = jnp.where((row & b) != 0, pltpu.roll(x, (sign * b) % n, 1), x)
        b *= 2
    return x


def _rel_bias_expand(table, name):
    _, valid = _band_geometry(A_PREV)
    bw = valid.shape[1]
    valid_f = jnp.asarray(valid.astype(np.float32))
    rev = jnp.flip(table[:, 1:], axis=1).reshape(A_HEADS, 1, A_NVAR)

    def body(rev_ref, valid_ref, o_ref):
        rowv = jnp.broadcast_to(rev_ref[...], (QTILE, A_NVAR))
        top = rowv[:, 0:1]
        var = _skew_rows(rowv, 1)
        row = lax.broadcasted_iota(jnp.int32, (QTILE, A_NVAR), 0)
        colv = lax.broadcasted_iota(jnp.int32, (QTILE, A_NVAR), 1)
        var = jnp.where(colv < row, top, var)
        ok = valid_ref[...] > 0.5
        o_ref[:, :A_VAR0] = jnp.where(ok[:, :A_VAR0], top, NEG_INF)
        o_ref[:, A_VAR0:] = jnp.where(ok[:, A_VAR0:], var, NEG_INF)

    return pl.pallas_call(
        body, name=name, grid=(A_HEADS,),
        out_shape=jax.ShapeDtypeStruct((A_HEADS, QTILE, bw), F32),
        in_specs=[pl.BlockSpec((None, 1, A_NVAR), lambda h: (h, 0, 0)), pl.BlockSpec((QTILE, bw), lambda h: (0, 0))],
        out_specs=pl.BlockSpec((None, QTILE, bw), lambda h: (h, 0, 0)),
        compiler_params=_params(("arbitrary",)),
    )(rev, valid_f)


def _rel_bias_grad(dbias, name):
    bl = dbias.shape[0]
    bw = dbias.shape[-1]

    def body(db_ref, o_ref):
        g = db_ref[0]
        for b in range(1, bl):
            g = g + db_ref[b]
        sk = _skew_rows(g[:, A_VAR0:], -1)
        row = lax.broadcasted_iota(jnp.int32, (QTILE, A_NVAR), 0)
        colv = lax.broadcasted_iota(jnp.int32, (QTILE, A_NVAR), 1)
        wrapped = (row + colv) >= A_NVAR
        main = jnp.sum(jnp.where(wrapped, 0.0, sk), axis=0, keepdims=True)
        top = jnp.sum(g[:, :A_VAR0]) + jnp.sum(jnp.where(wrapped, sk, 0.0))
        o_ref[:, :A_NVAR] = jnp.broadcast_to(main, (8, A_NVAR))
        o_ref[:, A_NVAR:] = jnp.full((8, LANES), top, F32)

    out = pl.pallas_call(
        body, name=name, grid=(A_HEADS,),
        out_shape=jax.ShapeDtypeStruct((A_HEADS, 8, A_NVAR + LANES), F32),
        in_specs=[pl.BlockSpec((bl, None, QTILE, bw), lambda h: (0, h, 0, 0))],
        out_specs=pl.BlockSpec((None, 8, A_NVAR + LANES), lambda h: (h, 0, 0)),
        compiler_params=_params(("arbitrary",)),
    )(dbias)
    main, top = out[:, 0, :A_NVAR], out[:, 0, A_NVAR]
    fm = jnp.flip(main, axis=1)
    return jnp.concatenate([jnp.zeros((A_HEADS, 1), F32), fm[:, :-1], fm[:, -1:] + top[:, None]], axis=1)


def _alibi_bias():
    dist, valid = _band_geometry(B_PREV)
    slopes = np.array([2.0 ** (-8.0 * (h + 1) / B_Q_HEADS) for h in range(B_Q_HEADS)], dtype=np.float32)
    bias = -slopes[:, None, None] * np.abs(dist).astype(np.float32)[None]
    return jnp.asarray(np.where(valid[None], bias, np.float32(NEG_INF)).astype(np.float32))


SMALL_NAMES = ("ffn1_norm", "mix_norm", "ffn2_norm", "ple_norm", "a_q_norm", "a_k_norm", "b_q_norm", "b_k_norm",
               "a_rel_bias", "b_sinks", "loss")


def _pack_small(vals):
    rows = []
    for nme in SMALL_NAMES:
        v = vals[nme].astype(F32)
        if nme == "a_rel_bias":
            v = jnp.pad(v.reshape(A_HEADS, -1), ((0, 0), (0, 3 * LANES - (2 * A_MAX_REL + 1))))
        v = v.reshape(-1)
        v = jnp.pad(v, (0, (-v.shape[0]) % LANES))
        rows.append(v.reshape(-1, LANES))
    out = jnp.concatenate(rows, axis=0)
    return jnp.pad(out, ((0, (-out.shape[0]) % 8), (0, 0)))


def _unpack_small(packed, shapes):
    out, r = {}, 0
    for nme in SMALL_NAMES:
        shp = shapes[nme]
        if nme == "a_rel_bias":
            nr = A_HEADS * 3
            out[nme] = packed[r:r + nr].reshape(A_HEADS, 3 * LANES)[:, :2 * A_MAX_REL + 1].reshape(shp)
        else:
            size = int(np.prod(shp)) if shp else 1
            nr = -(-size // LANES)
            out[nme] = packed[r:r + nr].reshape(-1)[:size].reshape(shp)
        r += nr
    return out


BIG_NAMES = ("ffn1_w_gu", "ffn1_w_down", "w_in", "w_gate", "w_proj_a", "w_proj_b", "w_out",
             "ffn2_w_gu", "ffn2_w_down", "w_ple_gate", "w_ple_proj")
ROW_SHARDED = ("ffn1_w_down", "ffn2_w_down", "w_out", "w_ple_gate")
WEIGHT_ORDER = ("ffn1_norm", "ffn1_w_gu", "ffn1_w_down", "mix_norm", "w_in", "a_q_norm", "a_k_norm", "a_rel_bias",
                "b_q_norm", "b_k_norm", "b_sinks", "w_gate", "w_proj_a", "w_proj_b", "w_out", "ffn2_norm",
                "ffn2_w_gu", "ffn2_w_down", "ple_norm", "w_ple_gate", "w_ple_proj")


TRANSPOSED = ("ffn1_w_gu", "ffn2_w_gu", "w_in")


def _local(a, nme):
    return a[0].T if nme in TRANSPOSED else a[0]


def _full_cols(wg):
    nb, k, n = wg.shape
    return jnp.transpose(wg, (1, 0, 2)).reshape(k, nb * n)


def _col_blocks(g, nb):
    k, n = g.shape
    return jnp.transpose(g.reshape(k, nb, n // nb), (1, 0, 2))


def _step(x, p, target, w, m, v):
    bl, s_len, d = x.shape
    t = bl * s_len
    h0 = x.reshape(t, d)
    pt = p.reshape(t, p.shape[-1])
    tgt = target.reshape(t, d)

    g_ffn1, g_mix, g_ffn2, g_ple = w["ffn1_norm"], w["mix_norm"], w["ffn2_norm"], w["ple_norm"]
    tiled = lambda a, width: jnp.tile(a.reshape(1, HEAD_DIM), (1, width // HEAD_DIM))
    gqa, gka = tiled(w["a_q_norm"], QW), tiled(w["a_k_norm"], _kv_width("A"))
    gqb, gkb = tiled(w["b_q_norm"], QW), tiled(w["b_k_norm"], _kv_width("B"))
    sinks = w["b_sinks"].reshape(B_Q_HEADS)
    bias_a = _rel_bias_expand(w["a_rel_bias"][0], "rel_bias_expand")
    bias_b = _alibi_bias()

    shard = {nme: _local(w[nme], nme).astype(BF16) for nme in BIG_NAMES}
    wgu1, wd1 = _all_gather([shard["ffn1_w_gu"], shard["ffn1_w_down"]], "weights_gather_ffn1")
    nf = wgu1.shape[1]
    wd1 = wd1.reshape(N_DEV // 2, nf, d)
    mixer_names = ("w_in", "w_gate")
    rest_names = ("w_proj_a", "w_proj_b", "w_out", "ffn2_w_gu", "ffn2_w_down", "w_ple_gate", "w_ple_proj")
    send1, recv1, bufs, token = _gather_start([shard[nme] for nme in mixer_names], wgu1, "gather_start_mixer")
    rsend1, rrecv1, rest_bufs, token = _gather_start([shard[nme] for nme in rest_names], token, "gather_start_rest")

    h1, gu1 = _ffn_fwd(h0, g_ffn1 + token[0, 0], wgu1, wd1, "ffn1_fwd")
    send2, recv2, bufs, token = _gather_pass(send1, recv1, bufs, h1, "gather_pass_mixer")
    win, wgate = _gather_wait(send2, recv2, bufs, token, "gather_wait_mixer")
    win, wgate = win.reshape(IN_COLS, d), _full_cols(wgate)
    un, qkv, gate = _proj_fwd(h1, g_mix, win, wgate, "proj_fwd")
    ya = _attn_fwd("A", qkv, gqa, gka, bias_a, sinks, bl, s_len, "attn_a_fwd")
    rsend2, rrecv2, rest_bufs, token = _gather_pass(rsend1, rrecv1, rest_bufs, ya, "gather_pass_rest")
    yb = _attn_fwd("B", qkv, gqb + token[0, 0], gkb, bias_b, sinks, bl, s_len, "attn_b_fwd")
    gathered = dict(zip(rest_names, _gather_wait(rsend2, rrecv2, rest_bufs, yb, "gather_wait_rest")))
    wgu2 = gathered["ffn2_w_gu"]
    wd2 = gathered["ffn2_w_down"].reshape(N_DEV // 2, nf, d)
    wpa = _full_cols(gathered["w_proj_a"])
    wpb = _full_cols(gathered["w_proj_b"])
    wpe = _full_cols(gathered["w_ple_proj"])
    wout = gathered["w_out"].reshape(d, d)
    wpg = gathered["w_ple_gate"].reshape(d, d)
    h2, merged, pa, pb = _merge_fwd(h1, ya, yb, gate, wpa, wpb, wout, "merge_fwd")
    h3, gu2 = _ffn_fwd(h2, g_ffn2, wgu2, wd2, "ffn2_fwd")
    dh3, dz4, dpp, n4, dg_ple, loss_part = _ple_loss(h3, g_ple, pt, tgt, wpg, wpe, "ple_loss")

    xi, yi, ci = _place()
    me = jnp.stack([4 * xi + 2 * yi + ci]).astype(jnp.int32)
    g32, g16, big = {}, {}, {}

    def keep(nme, pair, rows=None):
        for store, g in zip((g32, g16), pair):
            store[nme] = g if rows is None else g.reshape(N_DEV, rows, d)

    def start(names, after, tag):
        send, recv, parts, lands, token = _scatter_start([g16[nme] for nme in names], after, "grads_start_" + tag)
        return names, send, recv, parts, lands, token

    def finish(state, after, tag):
        names, send, recv, parts, lands, _ = state
        lands = _scatter_wait(send, recv, parts, lands, after, "grads_wait_" + tag)
        return names, lands

    def adam(done, dep):
        for nme, land in zip(*done):
            outs = _final_adam(g32[nme], land, _local(w[nme], nme), _local(m[nme], nme), _local(v[nme], nme), me, dep,
                               "adam_" + nme)
            big[nme] = [(o.T if nme in TRANSPOSED else o)[None] for o in outs]

    flights = []

    def launch(names, after, tag):
        flights.append((start(names, after, tag), tag))
        return flights[-1][0][-1]

    keep("w_ple_gate", _dw(n4, dz4, 1, d, "dw_ple_gate"), d // N_DEV)
    keep("w_ple_proj", _dw(pt, dpp, N_DEV, d // N_DEV, "dw_ple_proj"))
    token = launch(("w_ple_gate", "w_ple_proj"), dh3, "ple")

    dh2, dgu2, a2, n3, dg_ffn2 = _ffn_bwd(dh3, h2, g_ffn2 + token[0, 0], gu2, wgu2, wd2, "ffn2_bwd")
    keep("ffn2_w_down", _dw(a2, dh3, N_DEV // 2, d, "dw_ffn2_down", 0.5), nf // 2)
    token = launch(("ffn2_w_down",), dh2, "ffn2_down")
    keep("ffn2_w_gu", _dw(dgu2, n3, N_DEV, d, "dw_ffn2_gu", dep=token))
    launch(("ffn2_w_gu",), dh2, "ffn2_gu")

    dpa, dpb, dzg, dya, dyb = _merge_bwd(dh2, pa, pb, gate, wpa, wpb, wout, "merge_bwd")
    keep("w_out", _dw(merged, dh2, 1, d, "dw_out"), d // N_DEV)
    keep("w_proj_a", _dw(ya, dpa, N_DEV, d // N_DEV, "dw_proj_a"))
    keep("w_proj_b", _dw(yb, dpb, N_DEV, d // N_DEV, "dw_proj_b"))
    keep("w_gate", _dw(un, dzg, N_DEV, 2 * d // N_DEV, "dw_gate"))
    token = launch(("w_out", "w_proj_a", "w_proj_b", "w_gate"), dya, "mixer")

    dqa, dka, dva, dgqa, dgka, dbias, _ = _attn_bwd("A", qkv, gqa + token[0, 0], gka, bias_a, sinks, ya, dya, bl, s_len,
                                                     "attn_a_bwd")
    dqb, dkb, dvb, dgqb, dgkb, _, dsink = _attn_bwd("B", qkv, gqb, gkb, bias_b, sinks, yb, dyb, bl, s_len, "attn_b_bwd")
    dqkv = [dqa, dka, dva, dqb, dkb, dvb]
    dtab = _rel_bias_grad(dbias, "rel_bias_grad")

    dh1, dg_mix = _proj_bwd(dh2, h1, g_mix, dzg, dqkv, win, wgate, "proj_bwd")
    keep("w_in", _dw_rows(dqkv, un, "dw_in"), IN_COLS // N_DEV)
    token = launch(("w_in",), dh1, "w_in")

    dh0, dgu1, a1, n1, dg_ffn1 = _ffn_bwd(dh1, h0, g_ffn1 + token[0, 0], gu1, wgu1, wd1, "ffn1_bwd")
    keep("ffn1_w_down", _dw(a1, dh1, N_DEV // 2, d, "dw_ffn1_down", 0.5), nf // 2)
    token = launch(("ffn1_w_down",), dh0, "ffn1_down")
    keep("ffn1_w_gu", _dw(dgu1, n1, N_DEV, d, "dw_ffn1_gu", dep=token))
    token = launch(("ffn1_w_gu",), dh0, "ffn1_gu")
    (flight, _), earlier = flights[-1], flights[:-1]
    for state, tag in earlier:
        adam(finish(state, token, tag), token)
    behind = 0.0 * big["ffn1_w_down"][0][0, 0, :1]
    smalls = (dg_ffn1, dg_mix, dg_ffn2, dg_ple + behind, dgqa, dgka, dgqb, dgkb, dtab, dsink)
    return dh0, loss_part, big, smalls, flight, finish, adam


def kernel(x, p, ffn1_norm, ffn1_w_gu, ffn1_w_down, mix_norm, w_in, a_q_norm, a_k_norm, a_rel_bias, b_q_norm, b_k_norm, b_sinks, w_gate, w_proj_a, w_proj_b, w_out, ffn2_norm, ffn2_w_gu, ffn2_w_down, ple_norm, w_ple_gate, w_ple_proj, loss_target, m_ffn1_norm, m_ffn1_w_gu, m_ffn1_w_down, m_mix_norm, m_w_in, m_a_q_norm, m_a_k_norm, m_a_rel_bias, m_b_q_norm, m_b_k_norm, m_b_sinks, m_w_gate, m_w_proj_a, m_w_proj_b, m_w_out, m_ffn2_norm, m_ffn2_w_gu, m_ffn2_w_down, m_ple_norm, m_w_ple_gate, m_w_ple_proj, v_ffn1_norm, v_ffn1_w_gu, v_ffn1_w_down, v_mix_norm, v_w_in, v_a_q_norm, v_a_k_norm, v_a_rel_bias, v_b_q_norm, v_b_k_norm, v_b_sinks, v_w_gate, v_w_proj_a, v_w_proj_b, v_w_out, v_ffn2_norm, v_ffn2_w_gu, v_ffn2_w_down, v_ple_norm, v_w_ple_gate, v_w_ple_proj):
    w = dict(ffn1_norm=ffn1_norm, ffn1_w_gu=ffn1_w_gu, ffn1_w_down=ffn1_w_down, mix_norm=mix_norm, w_in=w_in,
             a_q_norm=a_q_norm, a_k_norm=a_k_norm, a_rel_bias=a_rel_bias, b_q_norm=b_q_norm, b_k_norm=b_k_norm,
             b_sinks=b_sinks, w_gate=w_gate, w_proj_a=w_proj_a, w_proj_b=w_proj_b, w_out=w_out, ffn2_norm=ffn2_norm,
             ffn2_w_gu=ffn2_w_gu, ffn2_w_down=ffn2_w_down, ple_norm=ple_norm, w_ple_gate=w_ple_gate,
             w_ple_proj=w_ple_proj)
    m = dict(ffn1_norm=m_ffn1_norm, ffn1_w_gu=m_ffn1_w_gu, ffn1_w_down=m_ffn1_w_down, mix_norm=m_mix_norm,
             w_in=m_w_in, a_q_norm=m_a_q_norm, a_k_norm=m_a_k_norm, a_rel_bias=m_a_rel_bias, b_q_norm=m_b_q_norm,
             b_k_norm=m_b_k_norm, b_sinks=m_b_sinks, w_gate=m_w_gate, w_proj_a=m_w_proj_a, w_proj_b=m_w_proj_b,
             w_out=m_w_out, ffn2_norm=m_ffn2_norm, ffn2_w_gu=m_ffn2_w_gu, ffn2_w_down=m_ffn2_w_down,
             ple_norm=m_ple_norm, w_ple_gate=m_w_ple_gate, w_ple_proj=m_w_ple_proj)
    v = dict(ffn1_norm=v_ffn1_norm, ffn1_w_gu=v_ffn1_w_gu, ffn1_w_down=v_ffn1_w_down, mix_norm=v_mix_norm,
             w_in=v_w_in, a_q_norm=v_a_q_norm, a_k_norm=v_a_k_norm, a_rel_bias=v_a_rel_bias, b_q_norm=v_b_q_norm,
             b_k_norm=v_b_k_norm, b_sinks=v_b_sinks, w_gate=v_w_gate, w_proj_a=v_w_proj_a, w_proj_b=v_w_proj_b,
             w_out=v_w_out, ffn2_norm=v_ffn2_norm, ffn2_w_gu=v_ffn2_w_gu, ffn2_w_down=v_ffn2_w_down,
             ple_norm=v_ple_norm, w_ple_gate=v_w_ple_gate, w_ple_proj=v_w_ple_proj)
    bl, s_len, d = x.shape

    dh0, loss_part, big, smalls, flight, finish, adam = _step(x, p[0], loss_target, w, m, v)
    dg_ffn1, dg_mix, dg_ffn2, dg_ple, dgqa, dgka, dgqb, dgkb, dtab, dsink = smalls

    fold = lambda a: a[:, :, 0, :].reshape(-1, HEAD_DIM).sum(axis=0)
    small_part = dict(
        ffn1_norm=dg_ffn1, mix_norm=dg_mix, ffn2_norm=dg_ffn2, ple_norm=dg_ple,
        a_q_norm=fold(dgqa), a_k_norm=fold(dgka), b_q_norm=fold(dgqb), b_k_norm=fold(dgkb),
        a_rel_bias=dtab,
        b_sinks=dsink.sum(axis=0)[:, 0, :GROUP].reshape(B_Q_HEADS),
        loss=loss_part[0, :1])
    zero1 = jnp.zeros((1,), F32)
    shapes = {nme: w[nme].shape for nme in SMALL_NAMES if nme != "loss"}
    shapes["loss"] = ()
    pk = lambda src: _pack_small({**{nme: src[nme] for nme in SMALL_NAMES if nme != "loss"}, "loss": zero1})
    sg, sd, sm, sv = _small_allreduce_adam(_pack_small(small_part), pk(w), pk(m), pk(v), "small_allreduce_adam")
    adam(finish(flight, sg, "ffn1_gu"), sg)
    sg, sd, sm, sv = (_unpack_small(a, shapes) for a in (sg, sd, sm, sv))

    def pick(i):
        out = []
        for nme in WEIGHT_ORDER:
            out.append(big[nme][i] if nme in big else (sg, sd, sm, sv)[i][nme])
        return out

    return (sg["loss"], dh0.reshape(bl, s_len, d), *pick(0), *pick(1), *pick(2), *pick(3))
```

```python
import functools

import jax
import jax.numpy as jnp
import numpy as np
from jax import lax
from jax.experimental import pallas as pl
from jax.experimental.pallas import tpu as pltpu

F32 = jnp.float32
BF16 = jnp.bfloat16

CHUNK = 64
HEAD_DIM = 64
A_HEADS = 8
A_PREV = 8
A_MAX_REL = 128
B_Q_HEADS = 8
B_KV_HEADS = 2
B_PREV = 2
A_WIDTH = A_HEADS * HEAD_DIM
B_Q_WIDTH = B_Q_HEADS * HEAD_DIM
B_KV_WIDTH = B_KV_HEADS * HEAD_DIM
IN_COLS = 3 * A_WIDTH + B_Q_WIDTH + 2 * B_KV_WIDTH
EPS = 1e-6
NEG_INF = -1e30
ADAM_LR = 0.001
ADAM_B1 = 0.9
ADAM_B2 = 0.999
ADAM_EPS = 1e-08
ADAM_WD = 0.01
ADAM_STEP = 10

N_DEV = 8
LANES = 128
QTILE = 2 * CHUNK
VMEM_LIMIT = 56 * 1024 * 1024
ADAM_TILE_ELEMS = 256 * 1024

MESH_ID = pl.DeviceIdType.MESH
ANY = pl.BlockSpec(memory_space=pl.ANY)
HBM = pl.BlockSpec(memory_space=pltpu.HBM)
SEM = pl.BlockSpec(memory_space=pltpu.SEMAPHORE)
SIDE_EFFECT = pltpu.SideEffectType.DATAFLOW_SIDE_EFFECTING


def _dot(a, b):
    return jnp.dot(a, b, preferred_element_type=F32)


def _dot_nt(a, b):
    return lax.dot_general(a, b, (((1,), (1,)), ((), ())), preferred_element_type=F32)


def _dot_tn(a, b):
    return lax.dot_general(a, b, (((0,), (0,)), ((), ())), preferred_element_type=F32)


def _params(sem=None, vmem=VMEM_LIMIT):
    return pltpu.CompilerParams(dimension_semantics=sem, vmem_limit_bytes=vmem)


def _row_tile(t, want):
    while t % want:
        want //= 2
    return want


def _place():
    return lax.axis_index("x"), lax.axis_index("y"), lax.axis_index("c")


def _all_gather(shards, name):
    n = len(shards)

    def body(*refs):
        ins, outs = refs[:n], refs[n:2 * n]
        send_sems, recv_sems, local_sems = refs[2 * n:]
        x, y, c = _place()
        me, sib = (x, y, c), (x, y, 1 - c)
        chips = [(1 - x, y), (x, 1 - y), (1 - x, 1 - y)]

        def copy(w, k, block, to, src=None):
            px, py, pc = block
            dst = outs[w].at[4 * px + 2 * py + pc]
            return pltpu.make_async_remote_copy(
                src_ref=dst if src is None else src, dst_ref=dst,
                send_sem=send_sems.at[w * 7 + k], recv_sem=recv_sems.at[w * 7 + k],
                device_id=to, device_id_type=MESH_ID)

        mine = [pltpu.make_async_copy(ins[w], outs[w].at[4 * x + 2 * y + c], local_sems.at[w]) for w in range(n)]
        for cp in mine:
            cp.start()
        first = []
        for w in range(n):
            first.append(copy(w, 0, me, sib, src=ins[w]))
            first += [copy(w, 1 + j, me, (*chip, c), src=ins[w]) for j, chip in enumerate(chips)]
        for cp in first:
            cp.start()
        passed = []
        for j, chip in enumerate(chips):
            for w in range(n):
                copy(w, 1 + j, (*chip, c), me).wait_recv()
                fwd = copy(w, 4 + j, (*chip, c), sib)
                fwd.start()
                passed.append(fwd)
        for w in range(n):
            copy(w, 0, sib, me).wait_recv()
        for j, chip in enumerate(chips):
            for w in range(n):
                copy(w, 4 + j, (*chip, 1 - c), me).wait_recv()
        for cp in first + passed:
            cp.wait_send()
        for cp in mine:
            cp.wait()

    return pl.pallas_call(
        body, name=name,
        out_shape=[jax.ShapeDtypeStruct((N_DEV,) + s.shape, s.dtype) for s in shards],
        in_specs=[ANY] * n, out_specs=[ANY] * n,
        scratch_shapes=[pltpu.SemaphoreType.DMA((7 * n,)), pltpu.SemaphoreType.DMA((7 * n,)),
                        pltpu.SemaphoreType.DMA((n,))],
    )(*shards)


def _gather_level(bufs, send_sems, recv_sems, level, shards=None):
    x, y, c = _place()
    me, sib = (x, y, c), (x, y, 1 - c)
    chips = [(1 - x, y), (x, 1 - y), (1 - x, 1 - y)]

    def copy(w, k, block, to):
        px, py, pc = block
        rows = bufs[w].at[4 * px + 2 * py + pc]
        src = shards[w] if shards is not None and block is me else rows
        return pltpu.make_async_remote_copy(src_ref=src, dst_ref=rows, send_sem=send_sems.at[k], recv_sem=recv_sems.at[k],
                                            device_id=to, device_id_type=MESH_ID)

    n = len(bufs)
    own = []
    if level == 1:
        own = [pltpu.make_async_copy(bufs[w].at[4 * x + 2 * y + c] if shards is None else shards[w],
                                     bufs[w].at[4 * x + 2 * y + c], send_sems.at[4 * n + w]) for w in range(n)]
    out, arriving = [], []
    for w in range(len(bufs)):
        if level == 1:
            out.append(copy(w, 4 * w, me, sib))
            arriving.append(copy(w, 4 * w, sib, me))
        for j, chip in enumerate(chips):
            if level == 1:
                out.append(copy(w, 4 * w + 1 + j, me, (*chip, c)))
                arriving.append(copy(w, 4 * w + 1 + j, (*chip, c), me))
            else:
                out.append(copy(w, 3 * w + j, (*chip, c), sib))
                arriving.append(copy(w, 3 * w + j, (*chip, 1 - c), me))
    return out, arriving, own


def _split_call(body, name, bufs, sems_in, after, n_sems_out, token, extra=()):
    n = len(bufs)
    out_shape = [pltpu.SemaphoreType.DMA((n_sems_out,))] * (2 if n_sems_out else 0)
    out_shape += [pltpu.HBM(a.shape, a.dtype) for a in bufs]
    out_specs = [SEM] * (2 if n_sems_out else 0) + [HBM] * n
    if token:
        out_shape.append(jax.ShapeDtypeStruct((8, LANES), F32))
        out_specs.append(pl.BlockSpec(memory_space=pltpu.VMEM))
    first = 2 if n_sems_out else 0
    return pl.pallas_call(
        body, name=name, out_shape=tuple(out_shape),
        in_specs=[HBM] * (n + len(extra)) + [SEM] * len(sems_in) + [ANY], out_specs=tuple(out_specs),
        input_output_aliases={i: first + i for i in range(n)},
        compiler_params=pltpu.CompilerParams(has_side_effects=SIDE_EFFECT),
    )(*bufs, *extra, *sems_in, after)


def _gather_start(shards, after, name):
    n = len(shards)
    hbm = lambda a: pltpu.with_memory_space_constraint(a, pltpu.HBM)
    bufs = [hbm(lax.empty((N_DEV,) + s.shape, s.dtype)) for s in shards]

    def body(*refs):
        out, _, own = _gather_level(refs[:n], refs[2 * n + 1], refs[2 * n + 2], 1, shards=refs[n:2 * n])
        for cp in own + out:
            cp.start()
        refs[-1][...] = jnp.zeros_like(refs[-1])

    outs = _split_call(body, name, bufs + [hbm(s) for s in shards], [], after, 5 * n, True)
    return outs[0], outs[1], list(outs[2:2 + 2 * n]), outs[-1]


def _gather_pass(send1, recv1, bufs_and_shards, after, name):
    n = len(bufs_and_shards) // 2
    bufs = bufs_and_shards

    def body(*refs):
        refs = refs[:n] + refs[2 * n:]
        out1, in1, own = _gather_level(refs[:n], refs[n], refs[n + 1], 1)
        out2, _, _ = _gather_level(refs[:n], refs[n + 3], refs[n + 4], 2)
        for cp in in1:
            cp.wait_recv()
        for cp in out2:
            cp.start()
        for cp in out1:
            cp.wait_send()
        for cp in own:
            cp.wait()
        refs[-1][...] = jnp.zeros_like(refs[-1])

    outs = _split_call(body, name, bufs, [send1, recv1], after, 3 * n, True)
    return outs[0], outs[1], list(outs[2:2 + n]), outs[-1]


def _gather_wait(send2, recv2, bufs, after, name):
    n = len(bufs)

    def body(*refs):
        out2, in2, _ = _gather_level(refs[:n], refs[n], refs[n + 1], 2)
        for cp in in2:
            cp.wait_recv()
        for cp in out2:
            cp.wait_send()

    return list(_split_call(body, name, bufs, [send2, recv2], after, 0, False))


def _scatter_copies(parts, lands, send_sems, recv_sems):
    x, y, c = _place()
    cps = []
    for w, (part, land) in enumerate(zip(parts, lands)):
        for k in range(1, N_DEV):
            px, py, pc = x ^ ((k >> 2) & 1), y ^ ((k >> 1) & 1), c ^ (k & 1)
            cps.append(pltpu.make_async_remote_copy(
                src_ref=part.at[4 * px + 2 * py + pc], dst_ref=land.at[k - 1],
                send_sem=send_sems.at[7 * w + k - 1], recv_sem=recv_sems.at[7 * w + k - 1],
                device_id=(px, py, pc), device_id_type=MESH_ID))
    return cps


def _scatter_start(parts, after, name):
    n = len(parts)

    def body(*refs):
        ins, lands = refs[:n], refs[n:2 * n]
        send_sems, recv_sems = refs[2 * n + 1], refs[2 * n + 2]
        token = refs[-1]
        for cp in _scatter_copies(ins, lands, send_sems, recv_sems):
            cp.start()
        token[...] = jnp.zeros_like(token)

    land_shapes = [(N_DEV - 1,) + p.shape[1:] for p in parts]
    in_hbm = [pltpu.with_memory_space_constraint(p, pltpu.HBM) for p in parts]
    in_hbm += [pltpu.with_memory_space_constraint(lax.empty(s, p.dtype), pltpu.HBM) for s, p in zip(land_shapes, parts)]
    outs = pl.pallas_call(
        body, name=name,
        out_shape=(pltpu.SemaphoreType.DMA((7 * n,)), pltpu.SemaphoreType.DMA((7 * n,)),
                   *[pltpu.HBM(p.shape, p.dtype) for p in parts],
                   *[pltpu.HBM(s, p.dtype) for s, p in zip(land_shapes, parts)],
                   jax.ShapeDtypeStruct((8, LANES), F32)),
        in_specs=[HBM] * (2 * n) + [ANY],
        out_specs=(SEM, SEM, *[HBM] * (2 * n), pl.BlockSpec(memory_space=pltpu.VMEM)),
        input_output_aliases={i: 2 + i for i in range(2 * n)},
        compiler_params=pltpu.CompilerParams(has_side_effects=SIDE_EFFECT),
    )(*in_hbm, after)
    return outs[0], outs[1], list(outs[2:2 + n]), list(outs[2 + n:2 + 2 * n]), outs[-1]


def _scatter_wait(send_sems, recv_sems, parts, lands, after, name):
    n = len(parts)

    def body(*refs):
        ins, lnd = refs[:n], refs[n:2 * n]
        for cp in _scatter_copies(ins, lnd, refs[2 * n], refs[2 * n + 1]):
            cp.wait_send()
            cp.wait_recv()

    outs = pl.pallas_call(
        body, name=name,
        out_shape=tuple(pltpu.HBM(a.shape, a.dtype) for a in parts + lands),
        in_specs=[HBM] * (2 * n) + [SEM, SEM, ANY],
        out_specs=tuple([HBM] * (2 * n)),
        input_output_aliases={i: i for i in range(2 * n)},
        compiler_params=pltpu.CompilerParams(has_side_effects=SIDE_EFFECT),
    )(*parts, *lands, send_sems, recv_sems, after)
    return list(outs[n:])


def _adam(w, g, m, v):
    m2 = ADAM_B1 * m + (1.0 - ADAM_B1) * g
    v2 = ADAM_B2 * v + (1.0 - ADAM_B2) * (g * g)
    m_hat = m2 / (1.0 - ADAM_B1 ** ADAM_STEP)
    v_hat = v2 / (1.0 - ADAM_B2 ** ADAM_STEP)
    delta = -ADAM_LR * (m_hat / (jnp.sqrt(v_hat) + ADAM_EPS) + ADAM_WD * w)
    return delta, m2, v2


def _small_allreduce_adam(part, w, m, v, name):
    rows = part.shape[0]

    def body(p_ref, w_ref, m_ref, v_ref, g_ref, d_ref, mo_ref, vo_ref, buf, send_sems, recv_sems):
        x, y, c = _place()
        buf[0] = p_ref[...]
        cps = []
        for k in range(1, N_DEV):
            kx, ky, kc = (k >> 2) & 1, (k >> 1) & 1, k & 1
            peer = (x ^ kx, y ^ ky, c ^ kc)
            cps.append(pltpu.make_async_remote_copy(
                src_ref=p_ref, dst_ref=buf.at[k], send_sem=send_sems.at[k - 1], recv_sem=recv_sems.at[k - 1],
                device_id=peer, device_id_type=MESH_ID))
        for cp in cps:
            cp.start()
        for cp in cps:
            cp.wait()
        me = 4 * x + 2 * y + c
        total = buf[me]
        for d in range(1, N_DEV):
            total = total + buf[d ^ me]
        g_ref[...] = total
        delta, m2, v2 = _adam(w_ref[...], total, m_ref[...], v_ref[...])
        d_ref[...] = delta
        mo_ref[...] = m2
        vo_ref[...] = v2

    vm = pl.BlockSpec(memory_space=pltpu.VMEM)
    return pl.pallas_call(
        body, name=name,
        out_shape=[jax.ShapeDtypeStruct(part.shape, F32)] * 4,
        in_specs=[vm] * 4, out_specs=[vm] * 4,
        scratch_shapes=[pltpu.VMEM((N_DEV, rows, LANES), F32),
                        pltpu.SemaphoreType.DMA((N_DEV - 1,)), pltpu.SemaphoreType.DMA((N_DEV - 1,))],
    )(part, w, m, v)


def _final_adam(g8, land, w, m, v, me, dep, name):
    _, r, c = g8.shape
    tr = max(q for q in range(16, r + 1, 16) if r % q == 0 and q * c <= ADAM_TILE_ELEMS)

    def body(me_ref, g_ref, land_ref, w_ref, m_ref, v_ref, dep_ref, go_ref, d_ref, mo_ref, vo_ref):
        del dep_ref
        g = g_ref[...]
        for k in range(N_DEV - 1):
            g = g + land_ref[k].astype(F32)
        go_ref[...] = g
        delta, m2, v2 = _adam(w_ref[...], g, m_ref[...], v_ref[...])
        d_ref[...] = delta
        mo_ref[...] = m2
        vo_ref[...] = v2

    plain = pl.BlockSpec((tr, c), lambda i, s: (i, 0))
    return pl.pallas_call(
        body, name=name,
        out_shape=[jax.ShapeDtypeStruct((r, c), F32)] * 4,
        grid_spec=pltpu.PrefetchScalarGridSpec(
            num_scalar_prefetch=1, grid=(r // tr,),
            in_specs=[pl.BlockSpec((None, tr, c), lambda i, s: (s[0], i, 0)),
                      pl.BlockSpec((N_DEV - 1, tr, c), lambda i, s: (0, i, 0)),
                      plain, plain, plain, ANY],
            out_specs=[plain] * 4),
        compiler_params=_params(("arbitrary",)),
    )(me, g8, land, w, m, v, dep)


def _rms(x, gain):
    r = lax.rsqrt(jnp.mean(x * x, axis=-1, keepdims=True) + EPS)
    xh = x * r
    return xh * gain, xh, r


def _rms_bwd(xh, r, gain, dy):
    gdy = gain * dy
    dx = r * (gdy - xh * jnp.mean(xh * gdy, axis=-1, keepdims=True))
    return dx, jnp.sum(dy * xh, axis=0, keepdims=True)


def _load_weights(pairs, sems):
    cps = [pltpu.make_async_copy(src, dst, sems.at[i]) for i, (src, dst) in enumerate(pairs)]
    for cp in cps:
        cp.start()
    for cp in cps:
        cp.wait()


def _ffn_fwd(h, gain, wgu, wd, name):
    t, d = h.shape
    nb, nf, _ = wgu.shape
    nh = nb // 2
    tm = _row_tile(t, 512)

    def body(h_ref, g_ref, wgu_hbm, wd_hbm, out_ref, gu_ref, wgu_v, wd_v, sems):
        @pl.when(pl.program_id(0) == 0)
        def _():
            _load_weights([(wgu_hbm, wgu_v), (wd_hbm, wd_v)], sems)

        x = h_ref[...]
        n, _, _ = _rms(x, g_ref[...])
        nbf = n.astype(BF16)
        acc = jnp.zeros((tm, d), F32)
        for j in range(nh):
            g = _dot_nt(nbf, wgu_v[j])
            u = _dot_nt(nbf, wgu_v[j + nh])
            gu_ref[j] = g.astype(BF16)
            gu_ref[j + nh] = u.astype(BF16)
            a = (g * jax.nn.sigmoid(g)) * u
            acc = acc + _dot(a.astype(BF16), wd_v[j])
        out_ref[...] = x + 0.5 * acc

    return pl.pallas_call(
        body, name=name, grid=(t // tm,),
        out_shape=[jax.ShapeDtypeStruct((t, d), F32), jax.ShapeDtypeStruct((nb, t, nf), BF16)],
        in_specs=[pl.BlockSpec((tm, d), lambda i: (i, 0)), pl.BlockSpec((1, d), lambda i: (0, 0)), ANY, ANY],
        out_specs=[pl.BlockSpec((tm, d), lambda i: (i, 0)), pl.BlockSpec((nb, tm, nf), lambda i: (0, i, 0))],
        scratch_shapes=[pltpu.VMEM(wgu.shape, BF16), pltpu.VMEM(wd.shape, BF16), pltpu.SemaphoreType.DMA((2,))],
        compiler_params=_params(("arbitrary",)),
    )(h, gain, wgu, wd)


def _ffn_bwd(dh, h, gain, gu, wgu, wd, name):
    t, d = h.shape
    nb, nf, _ = wgu.shape
    nh = nb // 2
    tm = _row_tile(t, 256)

    def body(dh_ref, h_ref, g_ref, gu_ref, wgu_hbm, wd_hbm, dhp_ref, dgu_ref, a_ref, n_ref, dgain_ref,
             wgu_v, wd_v, sems):
        @pl.when(pl.program_id(0) == 0)
        def _():
            _load_weights([(wgu_hbm, wgu_v), (wd_hbm, wd_v)], sems)
            dgain_ref[...] = jnp.zeros_like(dgain_ref)

        x = h_ref[...]
        gain_v = g_ref[...]
        n, xh, r = _rms(x, gain_v)
        n_ref[...] = n.astype(BF16)
        dh_v = dh_ref[...]
        dfb = (0.5 * dh_v).astype(BF16)
        dn = jnp.zeros((tm, d), F32)
        for j in range(nh):
            da = _dot_nt(dfb, wd_v[j])
            g = gu_ref[j].astype(F32)
            u = gu_ref[j + nh].astype(F32)
            sg = jax.nn.sigmoid(g)
            si = g * sg
            dg = (da * u * (sg * (1.0 + g * (1.0 - sg)))).astype(BF16)
            du = (da * si).astype(BF16)
            a_ref[j] = (si * u).astype(BF16)
            dgu_ref[j] = dg
            dgu_ref[j + nh] = du
            dn = dn + _dot(dg, wgu_v[j]) + _dot(du, wgu_v[j + nh])
        dx, dgain = _rms_bwd(xh, r, gain_v, dn)
        dhp_ref[...] = dh_v + dx
        dgain_ref[...] += dgain

    row = pl.BlockSpec((tm, d), lambda i: (i, 0))
    vec = pl.BlockSpec((1, d), lambda i: (0, 0))
    return pl.pallas_call(
        body, name=name, grid=(t // tm,),
        out_shape=[jax.ShapeDtypeStruct((t, d), F32), jax.ShapeDtypeStruct((nb, t, nf), BF16),
                   jax.ShapeDtypeStruct((nh, t, nf), BF16), jax.ShapeDtypeStruct((t, d), BF16),
                   jax.ShapeDtypeStruct((1, d), F32)],
        in_specs=[row, row, vec, pl.BlockSpec((nb, tm, nf), lambda i: (0, i, 0)), ANY, ANY],
        out_specs=[row, pl.BlockSpec((nb, tm, nf), lambda i: (0, i, 0)),
                   pl.BlockSpec((nh, tm, nf), lambda i: (0, i, 0)), row, vec],
        scratch_shapes=[pltpu.VMEM(wgu.shape, BF16), pltpu.VMEM(wd.shape, BF16), pltpu.SemaphoreType.DMA((2,))],
        compiler_params=_params(("arbitrary",)),
    )(dh, h, gain, gu, wgu, wd)


def _dw(xa, dy, nb, n, name, scale=1.0, dep=None):
    t, k = xa.shape[-2:]
    tt = _row_tile(t, 512)
    steps = t // tt
    wide = dy.ndim == 2 and xa.ndim == 2
    if xa.ndim == 3:
        x_spec = pl.BlockSpec((nb, tt, k), lambda i: (0, i, 0))
    else:
        x_spec = pl.BlockSpec((tt, k), lambda i: (i, 0))
    if dy.ndim == 3:
        dy_spec = pl.BlockSpec((nb, tt, n), lambda i: (0, i, 0))
    else:
        dy_spec = pl.BlockSpec((tt, dy.shape[1]), lambda i: (i, 0))
    acc_shape = (k, nb * n) if wide else (nb, k, n)
    stage_shape = (k, nb * n) if wide else (k, n)

    def body(x_ref, dy_ref, *rest):
        o_hbm, ob_hbm, acc, stage, sems = rest[-5:]

        @pl.when(pl.program_id(0) == 0)
        def _():
            acc[...] = jnp.zeros_like(acc)

        if wide:
            acc[...] += _dot(x_ref[...].astype(BF16).T, dy_ref[...].astype(BF16))
        elif xa.ndim == 2:
            xt = x_ref[...].astype(BF16).T
            for j in range(nb):
                acc[j] += _dot(xt, dy_ref[j].astype(BF16))
        else:
            dyb = dy_ref[...].astype(BF16)
            for j in range(nb):
                acc[j] += _dot_tn(x_ref[j].astype(BF16), dyb)

        @pl.when(pl.program_id(0) == steps - 1)
        def _():
            if scale != 1.0:
                acc[...] = acc[...] * scale
            if wide:
                cps = [pltpu.make_async_copy(acc.at[:, pl.ds(j * n, n)] if nb > 1 else acc, o_hbm.at[j], sems.at[j])
                       for j in range(nb)]
            else:
                cps = [pltpu.make_async_copy(acc, o_hbm, sems.at[0])]
            for cp in cps:
                cp.start()
            if wide:
                stage[...] = acc[...].astype(BF16)
                bcs = [pltpu.make_async_copy(stage.at[:, pl.ds(j * n, n)] if nb > 1 else stage, ob_hbm.at[j],
                                             sems.at[nb + j]) for j in range(nb)]
                for cp in bcs:
                    cp.start()
                for cp in bcs:
                    cp.wait()
            else:
                for j in range(nb):
                    stage[...] = acc[j].astype(BF16)
                    cp = pltpu.make_async_copy(stage, ob_hbm.at[j], sems.at[nb])
                    cp.start()
                    cp.wait()
            for cp in cps:
                cp.wait()

    return pl.pallas_call(
        body, name=name, grid=(steps,),
        out_shape=[jax.ShapeDtypeStruct((nb, k, n), F32), jax.ShapeDtypeStruct((nb, k, n), BF16)],
        in_specs=[x_spec, dy_spec] + ([] if dep is None else [ANY]),
        out_specs=[ANY, ANY],
        scratch_shapes=[pltpu.VMEM(acc_shape, F32), pltpu.VMEM(stage_shape, BF16),
                        pltpu.SemaphoreType.DMA((2 * nb,))],
        compiler_params=_params(("arbitrary",)),
    )(*((xa, dy) if dep is None else (xa, dy, dep)))


def _proj_fwd(h, gain, win, wgate, name):
    t, d = h.shape
    tm = _row_tile(t, 256)
    nq, ng = win.shape[0], wgate.shape[1]

    def body(h_ref, g_ref, win_ref, wg_ref, un_ref, qkv_ref, gate_ref):
        n, _, _ = _rms(h_ref[...], g_ref[...])
        nbf = n.astype(BF16)
        un_ref[...] = nbf
        qkv_ref[...] = _dot_nt(nbf, win_ref[...])
        gate_ref[...] = jax.nn.sigmoid(_dot(nbf, wg_ref[...]))

    full = lambda a: pl.BlockSpec(a.shape, lambda i: (0,) * a.ndim)
    return pl.pallas_call(
        body, name=name, grid=(t // tm,),
        out_shape=[jax.ShapeDtypeStruct((t, d), BF16), jax.ShapeDtypeStruct((t, nq), F32),
                   jax.ShapeDtypeStruct((t, ng), F32)],
        in_specs=[pl.BlockSpec((tm, d), lambda i: (i, 0)), full(gain), full(win), full(wgate)],
        out_specs=[pl.BlockSpec((tm, d), lambda i: (i, 0)), pl.BlockSpec((tm, nq), lambda i: (i, 0)),
                   pl.BlockSpec((tm, ng), lambda i: (i, 0))],
        compiler_params=_params(("arbitrary",)),
    )(h, gain, win, wgate)


def _proj_bwd(dh, h, gain, dzg, dqkv_parts, win, wgate, name):
    t, d = h.shape
    tm = _row_tile(t, 256)
    ng = wgate.shape[1]
    np_ = len(dqkv_parts)
    widths = [a.shape[1] for a in dqkv_parts]

    def body(dh_ref, h_ref, g_ref, dzg_ref, *rest):
        part_refs, (win_ref, wg_ref, dhp_ref, dgain_ref) = rest[:np_], rest[np_:]

        @pl.when(pl.program_id(0) == 0)
        def _():
            dgain_ref[...] = jnp.zeros_like(dgain_ref)

        gain_v = g_ref[...]
        _, xh, r = _rms(h_ref[...], gain_v)
        dun = _dot_nt(dzg_ref[...], wg_ref[...])
        off = 0
        for ref, wd in zip(part_refs, widths):
            dun = dun + _dot(ref[...].astype(BF16), win_ref[off:off + wd, :])
            off += wd
        dx, dgain = _rms_bwd(xh, r, gain_v, dun)
        dhp_ref[...] = dh_ref[...] + dx
        dgain_ref[...] += dgain

    full = lambda a: pl.BlockSpec(a.shape, lambda i: (0,) * a.ndim)
    row = pl.BlockSpec((tm, d), lambda i: (i, 0))
    return pl.pallas_call(
        body, name=name, grid=(t // tm,),
        out_shape=[jax.ShapeDtypeStruct((t, d), F32), jax.ShapeDtypeStruct((1, d), F32)],
        in_specs=[row, row, full(gain), pl.BlockSpec((tm, ng), lambda i: (i, 0))]
        + [pl.BlockSpec((tm, wd), lambda i: (i, 0)) for wd in widths] + [full(win), full(wgate)],
        out_specs=[row, pl.BlockSpec((1, d), lambda i: (0, 0))],
        compiler_params=_params(("arbitrary",)),
    )(dh, h, gain, dzg, *dqkv_parts, win, wgate)


def _dw_rows(parts, dy, name):
    t, n = dy.shape
    widths = [a.shape[1] for a in parts]
    k = sum(widths)
    tt = _row_tile(t, 512)
    steps = t // tt
    np_ = len(parts)

    def body(*refs):
        part_refs, dy_ref = refs[:np_], refs[np_]
        o_hbm, ob_hbm, acc, stage, sems = refs[np_ + 1:]

        @pl.when(pl.program_id(0) == 0)
        def _():
            acc[...] = jnp.zeros_like(acc)

        dyb = dy_ref[...].astype(BF16)
        off = 0
        for ref, wd in zip(part_refs, widths):
            acc[off:off + wd, :] += _dot(ref[...].astype(BF16).T, dyb)
            off += wd

        @pl.when(pl.program_id(0) == steps - 1)
        def _():
            stage[...] = acc[...].astype(BF16)
            cps = [pltpu.make_async_copy(acc, o_hbm.at[0], sems.at[0]),
                   pltpu.make_async_copy(stage, ob_hbm.at[0], sems.at[1])]
            for cp in cps:
                cp.start()
            for cp in cps:
                cp.wait()

    return pl.pallas_call(
        body, name=name, grid=(steps,),
        out_shape=[jax.ShapeDtypeStruct((1, k, n), F32), jax.ShapeDtypeStruct((1, k, n), BF16)],
        in_specs=[pl.BlockSpec((tt, wd), lambda i: (i, 0)) for wd in widths] + [pl.BlockSpec((tt, n), lambda i: (i, 0))],
        out_specs=[ANY, ANY],
        scratch_shapes=[pltpu.VMEM((k, n), F32), pltpu.VMEM((k, n), BF16), pltpu.SemaphoreType.DMA((2,))],
        compiler_params=_params(("arbitrary",)),
    )(*parts, dy)


def _merge_fwd(h, ya, yb, gate, wpa, wpb, wout, name):
    t, d = h.shape
    tm = _row_tile(t, 256)

    def body(h_ref, ya_ref, yb_ref, ga_ref, gb_ref, wpa_ref, wpb_ref, wout_ref, out_ref, mg_ref, pa_ref, pb_ref):
        pa = _dot(ya_ref[...].astype(BF16), wpa_ref[...])
        pb = _dot(yb_ref[...].astype(BF16), wpb_ref[...])
        merged = (ga_ref[...] * pa + gb_ref[...] * pb).astype(BF16)
        pa_ref[...] = pa.astype(BF16)
        pb_ref[...] = pb.astype(BF16)
        mg_ref[...] = merged
        out_ref[...] = h_ref[...] + _dot(merged, wout_ref[...])

    full = lambda a: pl.BlockSpec(a.shape, lambda i: (0,) * a.ndim)
    row = pl.BlockSpec((tm, d), lambda i: (i, 0))
    yrow = pl.BlockSpec((tm, ya.shape[1]), lambda i: (i, 0))
    return pl.pallas_call(
        body, name=name, grid=(t // tm,),
        out_shape=[jax.ShapeDtypeStruct((t, d), F32)] + [jax.ShapeDtypeStruct((t, d), BF16)] * 3,
        in_specs=[row, yrow, yrow, pl.BlockSpec((tm, d), lambda i: (i, 0)), pl.BlockSpec((tm, d), lambda i: (i, 1)),
                  full(wpa), full(wpb), full(wout)],
        out_specs=[row] * 4,
        compiler_params=_params(("arbitrary",)),
    )(h, ya, yb, gate, gate, wpa, wpb, wout)


def _merge_bwd(dh, pa, pb, gate, wpa, wpb, wout, name):
    t, d = dh.shape
    tm = _row_tile(t, 256)
    wy = wpa.shape[0]

    def body(dh_ref, pa_ref, pb_ref, ga_ref, gb_ref, wpa_ref, wpb_ref, wout_ref,
             dpa_ref, dpb_ref, dzg_ref, dya_ref, dyb_ref):
        dm = _dot_nt(dh_ref[...].astype(BF16), wout_ref[...])
        ga, gb = ga_ref[...], gb_ref[...]
        dpa = (dm * ga).astype(BF16)
        dpb = (dm * gb).astype(BF16)
        dpa_ref[...] = dpa
        dpb_ref[...] = dpb
        dzg_ref[:, :d] = (dm * pa_ref[...].astype(F32) * ga * (1.0 - ga)).astype(BF16)
        dzg_ref[:, d:] = (dm * pb_ref[...].astype(F32) * gb * (1.0 - gb)).astype(BF16)
        dya_ref[...] = _dot_nt(dpa, wpa_ref[...])
        dyb_ref[...] = _dot_nt(dpb, wpb_ref[...])

    full = lambda a: pl.BlockSpec(a.shape, lambda i: (0,) * a.ndim)
    row = pl.BlockSpec((tm, d), lambda i: (i, 0))
    yrow = pl.BlockSpec((tm, wy), lambda i: (i, 0))
    return pl.pallas_call(
        body, name=name, grid=(t // tm,),
        out_shape=[jax.ShapeDtypeStruct((t, d), BF16), jax.ShapeDtypeStruct((t, d), BF16),
                   jax.ShapeDtypeStruct((t, 2 * d), BF16), jax.ShapeDtypeStruct((t, wy), F32),
                   jax.ShapeDtypeStruct((t, wy), F32)],
        in_specs=[row, row, row, pl.BlockSpec((tm, d), lambda i: (i, 0)), pl.BlockSpec((tm, d), lambda i: (i, 1)),
                  full(wpa), full(wpb), full(wout)],
        out_specs=[row, row, pl.BlockSpec((tm, 2 * d), lambda i: (i, 0)), yrow, yrow],
        compiler_params=_params(("arbitrary",)),
    )(dh, pa, pb, gate, gate, wpa, wpb, wout)


def _ple_loss(h, gain, p, target, wpg, wpe, name):
    t, d = h.shape
    tm = _row_tile(t, 256)
    pd = p.shape[1]

    def body(h_ref, g_ref, p_ref, t_ref, wpg_ref, wpe_ref, dh_ref, dz_ref, dpp_ref, n_ref, dgain_ref, loss_ref):
        @pl.when(pl.program_id(0) == 0)
        def _():
            dgain_ref[...] = jnp.zeros_like(dgain_ref)
            loss_ref[...] = jnp.zeros_like(loss_ref)

        x = h_ref[...]
        gain_v = g_ref[...]
        n, xh, r = _rms(x, gain_v)
        nbf = n.astype(BF16)
        n_ref[...] = nbf
        pg = jax.nn.sigmoid(_dot(nbf, wpg_ref[...]))
        pp = _dot(p_ref[...].astype(BF16), wpe_ref[...])
        err = (x + pg * pp) - t_ref[...]
        loss_ref[...] += 0.5 * jnp.sum(jnp.mean(err * err, axis=-1, keepdims=True))
        dy = err * (1.0 / d)
        dpp_ref[...] = (dy * pg).astype(BF16)
        dz = (dy * pp * pg * (1.0 - pg)).astype(BF16)
        dz_ref[...] = dz
        dn = _dot_nt(dz, wpg_ref[...])
        dx, dgain = _rms_bwd(xh, r, gain_v, dn)
        dh_ref[...] = dy + dx
        dgain_ref[...] += dgain

    full = lambda a: pl.BlockSpec(a.shape, lambda i: (0,) * a.ndim)
    row = pl.BlockSpec((tm, d), lambda i: (i, 0))
    return pl.pallas_call(
        body, name=name, grid=(t // tm,),
        out_shape=[jax.ShapeDtypeStruct((t, d), F32), jax.ShapeDtypeStruct((t, d), BF16),
                   jax.ShapeDtypeStruct((t, d), BF16), jax.ShapeDtypeStruct((t, d), BF16),
                   jax.ShapeDtypeStruct((1, d), F32), jax.ShapeDtypeStruct((8, LANES), F32)],
        in_specs=[row, full(gain), pl.BlockSpec((tm, pd), lambda i: (i, 0)), row, full(wpg), full(wpe)],
        out_specs=[row, row, row, row, pl.BlockSpec((1, d), lambda i: (0, 0)),
                   pl.BlockSpec((8, LANES), lambda i: (0, 0))],
        compiler_params=_params(("arbitrary",)),
    )(h, gain, p, target, wpg, wpe)


def _head_masks():
    lane = lax.broadcasted_iota(jnp.int32, (1, LANES), 1)
    m0 = (lane < HEAD_DIM).astype(F32)
    return m0, 1.0 - m0


def _head_mean(v, m0, m1):
    del m0, m1
    width = v.shape[-1]
    shift = HEAD_DIM.bit_length() - 1
    r = jnp.right_shift(lax.broadcasted_iota(jnp.int32, (width, width), 0), shift)
    c = jnp.right_shift(lax.broadcasted_iota(jnp.int32, (width, width), 1), shift)
    same_head = (r == c).astype(BF16)
    hi = v.astype(BF16)
    lo = (v - hi.astype(F32)).astype(BF16)
    return (_dot(hi, same_head) + _dot(lo, same_head)) * (1.0 / HEAD_DIM)


def _head_norm(x, gain, m0, m1):
    r = lax.rsqrt(_head_mean(x * x, m0, m1) + EPS)
    xh = x * r
    return xh * gain, xh, r


def _head_norm_bwd(xh, r, gain, dy, m0, m1):
    gdy = gain * dy
    dx = r * (gdy - xh * _head_mean(xh * gdy, m0, m1))
    return dx, jnp.sum(dy * xh, axis=0, keepdims=True)


GROUP = 4
QW = GROUP * HEAD_DIM
STACK = GROUP * QTILE


def _kv_width(mode):
    return QW if mode == "A" else LANES


def _q_scratch_shape(mode, s_len):
    return (s_len, QW) if mode == "A" else (GROUP * s_len, LANES)


def _group_masks(dtype=F32):
    lane = lax.broadcasted_iota(jnp.int32, (1, QW), 1)
    return [((lane >= h * HEAD_DIM) & (lane < (h + 1) * HEAD_DIM)).astype(dtype) for h in range(GROUP)]


def _stack_heads(first_kv, x, m0, m1):
    out = []
    for half in range(GROUP // 2):
        xh = x[:, half * LANES:(half + 1) * LANES]
        a0, a1 = xh * m0, xh * m1
        r0, r1 = pltpu.roll(a0, HEAD_DIM, 1), pltpu.roll(a1, HEAD_DIM, 1)
        out += [jnp.where(first_kv, a0, r0), jnp.where(first_kv, r1, a1)]
    return out


def _unstack_heads(mode, first_kv, ts, m0, m1):
    if mode == "A":
        masks = _group_masks()
        return sum(t * mk for t, mk in zip(ts, masks))
    halves = []
    for half in range(GROUP // 2):
        t0 = jnp.where(first_kv, ts[2 * half], pltpu.roll(ts[2 * half], HEAD_DIM, 1))
        t1 = jnp.where(first_kv, pltpu.roll(ts[2 * half + 1], HEAD_DIM, 1), ts[2 * half + 1])
        halves.append(t0 * m0 + t1 * m1)
    return jnp.concatenate(halves, axis=1)


def _store_stacked(dst, i, heads):
    for half in range(2):
        rows = slice(half * QTILE, (half + 1) * QTILE)
        for h, x in enumerate(heads):
            dst[pl.ds((2 * i + half) * STACK + h * QTILE, QTILE), :] = x[rows].astype(dst.dtype)


def _load_stacked(mode, ref, m):
    if mode == "B":
        return ref[pl.ds(pl.multiple_of(m * STACK, STACK), STACK), :]
    x = ref[pl.ds(pl.multiple_of(m * QTILE, QTILE), QTILE), :]
    return jnp.concatenate([x * mk for mk in _group_masks(x.dtype)], axis=0)


def _attn_prep(mode, group, s_len, padk, q_ref, k_ref, v_ref, gq_ref, gk_ref, qs, k2, v2, do_ref=None, dos=None):
    m0, m1 = _head_masks()
    zpad = jnp.zeros((padk, k2.shape[1]), BF16)
    k2[pl.ds(0, padk), :] = zpad
    v2[pl.ds(0, padk), :] = zpad
    first_kv = group == 0
    rt = 2 * QTILE
    for i in range(s_len // rt):
        rows = pl.ds(i * rt, rt)
        qn, _, _ = _head_norm(q_ref[rows, :], gq_ref[...], m0, m1)
        kn, _, _ = _head_norm(k_ref[rows, :], gk_ref[...], m0, m1)
        qn = qn * (HEAD_DIM ** -0.5)
        if mode == "A":
            qs[rows, :] = qn.astype(BF16)
            if dos is not None:
                dos[rows, :] = do_ref[rows, :].astype(BF16)
        else:
            _store_stacked(qs, i, _stack_heads(first_kv, qn, m0, m1))
            if dos is not None:
                _store_stacked(dos, i, _stack_heads(first_kv, do_ref[rows, :], m0, m1))
        k2[pl.ds(padk + i * rt, rt), :] = kn.astype(BF16)
        v2[pl.ds(padk + i * rt, rt), :] = v_ref[rows, :].astype(BF16)


def _attn_probs(mode, q_st, kb, bias, ok, sink):
    s = _dot_nt(q_st, kb) + bias
    return _softmax_terms(mode, jnp.where(ok, s, NEG_INF), sink)


def _softmax_terms(mode, s, sink):
    mx = jnp.max(s, axis=-1, keepdims=True)
    if mode == "B":
        mx = jnp.maximum(mx, sink)
    e = jnp.exp(s - mx)
    l = jnp.sum(e, axis=-1, keepdims=True)
    if mode == "B":
        l = l + jnp.exp(sink - mx)
    return e, mx, l


def _sink_column(sink_ref, group):
    row = lax.broadcasted_iota(jnp.int32, (STACK, 1), 0)
    col = jnp.zeros((STACK, 1), F32)
    for h in range(GROUP):
        col = jnp.where((row >= h * QTILE) & (row < (h + 1) * QTILE), sink_ref[GROUP * group + h], col)
    return col


def _head_deltas(dd, m0, m1):
    cols = []
    for half in range(GROUP // 2):
        dh = dd[:, half * LANES:(half + 1) * LANES]
        cols += [jnp.sum(dh * m0, axis=-1, keepdims=True), jnp.sum(dh * m1, axis=-1, keepdims=True)]
    return jnp.concatenate(cols, axis=0)


def _attn_cols(mode):
    if mode == "A":
        return (lambda b, g: (b, g)), (lambda b, g: (b, 2 + g)), (lambda b, g: (b, 4 + g))
    return (lambda b, g: (b, 6 + g)), (lambda b, g: (b, 16)), (lambda b, g: (b, 17))


def _attn_fwd(mode, qkv, gq, gk, bias, sinks, bl, s_len, name):
    bw = bias.shape[-1]
    padk = bw - QTILE
    nt = s_len // QTILE
    qmap, kmap, vmap = _attn_cols(mode)

    kw = _kv_width(mode)

    def body(q_ref, k_ref, v_ref, gq_ref, gk_ref, bias_ref, sink_ref, o_ref, qs, k2, v2, s_buf):
        group = pl.program_id(1)
        m0, m1 = _head_masks()
        first_kv = group == 0
        _attn_prep(mode, group, s_len, padk, q_ref, k_ref, v_ref, gq_ref, gk_ref, qs, k2, v2)
        col = lax.broadcasted_iota(jnp.int32, (STACK, bw), 1)
        sink = _sink_column(sink_ref, group)

        def scores(m, slot):
            r0 = pl.multiple_of(m * QTILE, QTILE)
            s = _dot_nt(_load_stacked(mode, qs, m), k2[pl.ds(r0, bw), :]) + bias_ref[...]
            s_buf[slot] = jnp.where(col >= (padk - r0), s, NEG_INF)

        def finish_tile(m, slot):
            r0 = pl.multiple_of(m * QTILE, QTILE)
            e, _, l = _softmax_terms(mode, s_buf[slot], sink)
            o_st = _dot(e.astype(BF16), v2[pl.ds(r0, bw), :]) / l
            heads = [o_st[h * QTILE:(h + 1) * QTILE] for h in range(GROUP)]
            o_ref[pl.ds(r0, QTILE), :] = _unstack_heads(mode, first_kv, heads, m0, m1)

        scores(0, 0)

        def pair(j, carry):
            scores(2 * j + 1, 1)
            finish_tile(2 * j, 0)
            scores(jnp.minimum(2 * j + 2, nt - 1), 0)
            finish_tile(2 * j + 1, 1)
            return carry

        lax.fori_loop(0, nt // 2, pair, 0)

    blk = lambda w, f: pl.BlockSpec((s_len, w), f)
    return pl.pallas_call(
        body, name=name, grid=(bl, B_Q_HEADS // GROUP),
        out_shape=jax.ShapeDtypeStruct((bl * s_len, B_Q_HEADS * HEAD_DIM), F32),
        in_specs=[blk(QW, qmap), blk(kw, kmap), blk(kw, vmap),
                  pl.BlockSpec((1, QW), lambda b, g: (0, 0)), pl.BlockSpec((1, kw), lambda b, g: (0, 0)),
                  pl.BlockSpec((STACK, bw), lambda b, g: (g, 0)),
                  pl.BlockSpec(memory_space=pltpu.SMEM)],
        out_specs=blk(QW, lambda b, g: (b, g)),
        scratch_shapes=[pltpu.VMEM(_q_scratch_shape(mode, s_len), BF16)] + [pltpu.VMEM((s_len + padk, kw), BF16)] * 2
        + [pltpu.VMEM((2, STACK, bw), F32)],
        compiler_params=_params(("arbitrary", "arbitrary")),
    )(qkv, qkv, qkv, gq, gk, bias.reshape(-1, bw), sinks)


def _attn_bwd(mode, qkv, gq, gk, bias, sinks, y, dy, bl, s_len, name):
    bw = bias.shape[-1]
    padk = bw - QTILE
    nt = s_len // QTILE
    qmap, kmap, vmap = _attn_cols(mode)
    t = bl * s_len
    kw = _kv_width(mode)
    kvw = 4 * LANES if mode == "A" else LANES

    def body(q_ref, k_ref, v_ref, gq_ref, gk_ref, bias_ref, sink_ref, y_ref, dy_ref,
             dq_ref, dk_ref, dv_ref, dgq_ref, dgk_ref, dbias_ref, dsink_ref,
             qs, k2, v2, dos, dqs, dk, dv):
        group = pl.program_id(1)
        m0, m1 = _head_masks()
        first_kv = group == 0
        _attn_prep(mode, group, s_len, padk, q_ref, k_ref, v_ref, gq_ref, gk_ref, qs, k2, v2, dy_ref, dos)
        dk[...] = jnp.zeros_like(dk)
        dv[...] = jnp.zeros_like(dv)
        dbias_ref[...] = jnp.zeros_like(dbias_ref)
        col = lax.broadcasted_iota(jnp.int32, (STACK, bw), 1)
        lane8 = lax.broadcasted_iota(jnp.int32, (8, LANES), 1)
        sink = _sink_column(sink_ref, group)

        def tile(m, dsink):
            r0 = pl.multiple_of(m * QTILE, QTILE)
            rows = pl.ds(r0, QTILE)
            band = pl.ds(r0, bw)
            q_st = _load_stacked(mode, qs, m)
            do_st = _load_stacked(mode, dos, m)
            delta = _head_deltas(dy_ref[rows, :] * y_ref[rows, :], m0, m1)
            ok = col >= (padk - r0)
            kb = k2[band, :]
            e, mx, l = _attn_probs(mode, q_st, kb, bias_ref[...], ok, sink)
            inv = 1.0 / l
            pn = e * inv
            dp = _dot_nt(do_st, v2[band, :])
            ds = pn * (dp - delta)
            if mode == "A":
                dbias_ref[...] += ds
            else:
                part = jnp.exp(sink - mx) * inv * delta
                for h in range(GROUP):
                    dsink = dsink - jnp.where(lane8 == h, jnp.sum(part[h * QTILE:(h + 1) * QTILE]), 0.0)
            dsb = ds.astype(BF16)
            dv[band, :] += _dot_tn(pn.astype(BF16), do_st)
            dk[band, :] += _dot_tn(dsb, q_st)
            dq_st = _dot(dsb, kb)
            if mode == "A":
                heads = [dq_st[h * QTILE:(h + 1) * QTILE] for h in range(GROUP)]
                dqs[rows, :] = _unstack_heads(mode, first_kv, heads, m0, m1)
            else:
                dqs[pl.ds(pl.multiple_of(m * STACK, STACK), STACK), :] = dq_st
            return dsink

        dsink = lax.fori_loop(0, nt, tile, jnp.zeros((8, LANES), F32), unroll=2)
        dsink_ref[...] = dsink

        rt = 2 * QTILE
        dgq = jnp.zeros((1, QW), F32)
        dgk = jnp.zeros((1, kw), F32)
        for i in range(s_len // rt):
            rows = pl.ds(i * rt, rt)
            src = pl.ds(padk + i * rt, rt)
            gq_v, gk_v = gq_ref[...], gk_ref[...]
            _, qh, qr = _head_norm(q_ref[rows, :], gq_v, m0, m1)
            _, kh, kr = _head_norm(k_ref[rows, :], gk_v, m0, m1)
            if mode == "A":
                dqn = dqs[rows, :] * (HEAD_DIM ** -0.5)
            else:
                dqn = jnp.concatenate(
                    [_unstack_heads(mode, first_kv, [dqs[pl.ds((2 * i + half) * STACK + h * QTILE, QTILE), :]
                                                     for h in range(GROUP)], m0, m1)
                     for half in range(2)], axis=0) * (HEAD_DIM ** -0.5)
            dq_raw, dgq_i = _head_norm_bwd(qh, qr, gq_v, dqn, m0, m1)
            dk_raw, dgk_i = _head_norm_bwd(kh, kr, gk_v, dk[src, :], m0, m1)
            dvn = dv[src, :]
            dq_ref[rows, :] = dq_raw
            if mode == "A":
                dk_ref[rows, :] = dk_raw
                dv_ref[rows, :] = dvn
            else:
                @pl.when(group == 0)
                def _():
                    dk_ref[rows, :] = dk_raw
                    dv_ref[rows, :] = dvn

                @pl.when(group != 0)
                def _():
                    dk_ref[rows, :] += dk_raw
                    dv_ref[rows, :] += dvn
            dgq, dgk = dgq + dgq_i, dgk + dgk_i
        dgq_ref[...] = jnp.broadcast_to(dgq, (8, QW))
        dgk_ref[...] = jnp.broadcast_to(dgk, (8, kw))

    ng = B_Q_HEADS // GROUP
    blk = lambda w, f: pl.BlockSpec((s_len, w), f)
    small = lambda w: pl.BlockSpec((None, None, 8, w), lambda b, g: (b, g, 0, 0))
    own = lambda b, g: (b, g)
    kvmap = own if mode == "A" else (lambda b, g: (b, 0))
    pad_f32 = pltpu.VMEM((s_len + padk, kw), F32)
    pad_bf = pltpu.VMEM((s_len + padk, kw), BF16)
    stack_bf = pltpu.VMEM(_q_scratch_shape(mode, s_len), BF16)
    outs = pl.pallas_call(
        body, name=name, grid=(bl, ng),
        out_shape=[jax.ShapeDtypeStruct((t, ng * QW), F32), jax.ShapeDtypeStruct((t, kvw), F32),
                   jax.ShapeDtypeStruct((t, kvw), F32),
                   jax.ShapeDtypeStruct((bl, ng, 8, QW), F32), jax.ShapeDtypeStruct((bl, ng, 8, kw), F32),
                   jax.ShapeDtypeStruct((bl, ng * STACK, bw), F32), jax.ShapeDtypeStruct((bl, ng, 8, LANES), F32)],
        in_specs=[blk(QW, qmap), blk(kw, kmap), blk(kw, vmap),
                  pl.BlockSpec((1, QW), lambda b, g: (0, 0)), pl.BlockSpec((1, kw), lambda b, g: (0, 0)),
                  pl.BlockSpec((STACK, bw), lambda b, g: (g, 0)),
                  pl.BlockSpec(memory_space=pltpu.SMEM),
                  blk(QW, own), blk(QW, own)],
        out_specs=[blk(QW, own), blk(kw, kvmap), blk(kw, kvmap), small(QW), small(kw),
                   pl.BlockSpec((None, STACK, bw), lambda b, g: (b, g, 0)), small(LANES)],
        scratch_shapes=[stack_bf, pad_bf, pad_bf, stack_bf, pltpu.VMEM(_q_scratch_shape(mode, s_len), F32),
                        pad_f32, pad_f32],
        compiler_params=_params(("arbitrary", "arbitrary")),
    )(qkv, qkv, qkv, gq, gk, bias.reshape(-1, bw), sinks, y, dy)
    outs = list(outs)
    outs[5] = outs[5].reshape(bl, B_Q_HEADS, QTILE, bw)
    return outs


def _band_geometry(prev):
    bw = QTILE + prev * CHUNK
    i = np.arange(QTILE)[:, None]
    j = np.arange(bw)[None, :]
    dist = i + prev * CHUNK - j
    valid = (j // CHUNK >= i // CHUNK) & (j // CHUNK <= i // CHUNK + prev)
    return dist, valid


A_VAR0 = (A_PREV * CHUNK - A_MAX_REL) // LANES * LANES


A_NVAR = QTILE + A_PREV * CHUNK - A_VAR0


def _skew_rows(x, sign):
    rows, n = x.shape
    row = lax.broadcasted_iota(jnp.int32, x.shape, 0)
    b = 1
    while b < rows:
        x = jnp.where((row & b) != 0, pltpu.roll(x, (sign * b) % n, 1), x)
        b *= 2
    return x


def _rel_bias_expand(table, name):
    _, valid = _band_geometry(A_PREV)
    bw = valid.shape[1]
    valid_f = jnp.asarray(valid.astype(np.float32))
    rev = jnp.flip(table[:, 1:], axis=1).reshape(A_HEADS, 1, A_NVAR)

    def body(rev_ref, valid_ref, o_ref):
        rowv = jnp.broadcast_to(rev_ref[...], (QTILE, A_NVAR))
        top = rowv[:, 0:1]
        var = _skew_rows(rowv, 1)
        row = lax.broadcasted_iota(jnp.int32, (QTILE, A_NVAR), 0)
        colv = lax.broadcasted_iota(jnp.int32, (QTILE, A_NVAR), 1)
        var = jnp.where(colv < row, top, var)
        ok = valid_ref[...] > 0.5
        o_ref[:, :A_VAR0] = jnp.where(ok[:, :A_VAR0], top, NEG_INF)
        o_ref[:, A_VAR0:] = jnp.where(ok[:, A_VAR0:], var, NEG_INF)

    return pl.pallas_call(
        body, name=name, grid=(A_HEADS,),
        out_shape=jax.ShapeDtypeStruct((A_HEADS, QTILE, bw), F32),
        in_specs=[pl.BlockSpec((None, 1, A_NVAR), lambda h: (h, 0, 0)), pl.BlockSpec((QTILE, bw), lambda h: (0, 0))],
        out_specs=pl.BlockSpec((None, QTILE, bw), lambda h: (h, 0, 0)),
        compiler_params=_params(("arbitrary",)),
    )(rev, valid_f)


def _rel_bias_grad(dbias, name):
    bl = dbias.shape[0]
    bw = dbias.shape[-1]

    def body(db_ref, o_ref):
        g = db_ref[0]
        for b in range(1, bl):
            g = g + db_ref[b]
        sk = _skew_rows(g[:, A_VAR0:], -1)
        row = lax.broadcasted_iota(jnp.int32, (QTILE, A_NVAR), 0)
        colv = lax.broadcasted_iota(jnp.int32, (QTILE, A_NVAR), 1)
        wrapped = (row + colv) >= A_NVAR
        main = jnp.sum(jnp.where(wrapped, 0.0, sk), axis=0, keepdims=True)
        top = jnp.sum(g[:, :A_VAR0]) + jnp.sum(jnp.where(wrapped, sk, 0.0))
        o_ref[:, :A_NVAR] = jnp.broadcast_to(main, (8, A_NVAR))
        o_ref[:, A_NVAR:] = jnp.full((8, LANES), top, F32)

    out = pl.pallas_call(
        body, name=name, grid=(A_HEADS,),
        out_shape=jax.ShapeDtypeStruct((A_HEADS, 8, A_NVAR + LANES), F32),
        in_specs=[pl.BlockSpec((bl, None, QTILE, bw), lambda h: (0, h, 0, 0))],
        out_specs=pl.BlockSpec((None, 8, A_NVAR + LANES), lambda h: (h, 0, 0)),
        compiler_params=_params(("arbitrary",)),
    )(dbias)
    main, top = out[:, 0, :A_NVAR], out[:, 0, A_NVAR]
    fm = jnp.flip(main, axis=1)
    return jnp.concatenate([jnp.zeros((A_HEADS, 1), F32), fm[:, :-1], fm[:, -1:] + top[:, None]], axis=1)


def _alibi_bias():
    dist, valid = _band_geometry(B_PREV)
    slopes = np.array([2.0 ** (-8.0 * (h + 1) / B_Q_HEADS) for h in range(B_Q_HEADS)], dtype=np.float32)
    bias = -slopes[:, None, None] * np.abs(dist).astype(np.float32)[None]
    return jnp.asarray(np.where(valid[None], bias, np.float32(NEG_INF)).astype(np.float32))


SMALL_NAMES = ("ffn1_norm", "mix_norm", "ffn2_norm", "ple_norm", "a_q_norm", "a_k_norm", "b_q_norm", "b_k_norm",
               "a_rel_bias", "b_sinks", "loss")


def _pack_small(vals):
    rows = []
    for nme in SMALL_NAMES:
        v = vals[nme].astype(F32)
        if nme == "a_rel_bias":
            v = jnp.pad(v.reshape(A_HEADS, -1), ((0, 0), (0, 3 * LANES - (2 * A_MAX_REL + 1))))
        v = v.reshape(-1)
        v = jnp.pad(v, (0, (-v.shape[0]) % LANES))
        rows.append(v.reshape(-1, LANES))
    out = jnp.concatenate(rows, axis=0)
    return jnp.pad(out, ((0, (-out.shape[0]) % 8), (0, 0)))


def _unpack_small(packed, shapes):
    out, r = {}, 0
    for nme in SMALL_NAMES:
        shp = shapes[nme]
        if nme == "a_rel_bias":
            nr = A_HEADS * 3
            out[nme] = packed[r:r + nr].reshape(A_HEADS, 3 * LANES)[:, :2 * A_MAX_REL + 1].reshape(shp)
        else:
            size = int(np.prod(shp)) if shp else 1
            nr = -(-size // LANES)
            out[nme] = packed[r:r + nr].reshape(-1)[:size].reshape(shp)
        r += nr
    return out


BIG_NAMES = ("ffn1_w_gu", "ffn1_w_down", "w_in", "w_gate", "w_proj_a", "w_proj_b", "w_out",
             "ffn2_w_gu", "ffn2_w_down", "w_ple_gate", "w_ple_proj")
ROW_SHARDED = ("ffn1_w_down", "ffn2_w_down", "w_out", "w_ple_gate")
WEIGHT_ORDER = ("ffn1_norm", "ffn1_w_gu", "ffn1_w_down", "mix_norm", "w_in", "a_q_norm", "a_k_norm", "a_rel_bias",
                "b_q_norm", "b_k_norm", "b_sinks", "w_gate", "w_proj_a", "w_proj_b", "w_out", "ffn2_norm",
                "ffn2_w_gu", "ffn2_w_down", "ple_norm", "w_ple_gate", "w_ple_proj")


TRANSPOSED = ("ffn1_w_gu", "ffn2_w_gu", "w_in")


def _local(a, nme):
    return a[0].T if nme in TRANSPOSED else a[0]


def _full_cols(wg):
    nb, k, n = wg.shape
    return jnp.transpose(wg, (1, 0, 2)).reshape(k, nb * n)


def _col_blocks(g, nb):
    k, n = g.shape
    return jnp.transpose(g.reshape(k, nb, n // nb), (1, 0, 2))


def _step(x, p, target, w, m, v):
    bl, s_len, d = x.shape
    t = bl * s_len
    h0 = x.reshape(t, d)
    pt = p.reshape(t, p.shape[-1])
    tgt = target.reshape(t, d)

    g_ffn1, g_mix, g_ffn2, g_ple = w["ffn1_norm"], w["mix_norm"], w["ffn2_norm"], w["ple_norm"]
    tiled = lambda a, width: jnp.tile(a.reshape(1, HEAD_DIM), (1, width // HEAD_DIM))
    gqa, gka = tiled(w["a_q_norm"], QW), tiled(w["a_k_norm"], _kv_width("A"))
    gqb, gkb = tiled(w["b_q_norm"], QW), tiled(w["b_k_norm"], _kv_width("B"))
    sinks = w["b_sinks"].reshape(B_Q_HEADS)
    bias_a = _rel_bias_expand(w["a_rel_bias"][0], "rel_bias_expand")
    bias_b = _alibi_bias()

    shard = {nme: _local(w[nme], nme).astype(BF16) for nme in BIG_NAMES}
    wgu1, wd1 = _all_gather([shard["ffn1_w_gu"], shard["ffn1_w_down"]], "weights_gather_ffn1")
    nf = wgu1.shape[1]
    wd1 = wd1.reshape(N_DEV // 2, nf, d)
    mixer_names = ("w_in", "w_gate")
    rest_names = ("w_proj_a", "w_proj_b", "w_out", "ffn2_w_gu", "ffn2_w_down", "w_ple_gate", "w_ple_proj")
    send1, recv1, bufs, token = _gather_start([shard[nme] for nme in mixer_names], wgu1, "gather_start_mixer")
    rsend1, rrecv1, rest_bufs, token = _gather_start([shard[nme] for nme in rest_names], token, "gather_start_rest")

    h1, gu1 = _ffn_fwd(h0, g_ffn1 + token[0, 0], wgu1, wd1, "ffn1_fwd")
    send2, recv2, bufs, token = _gather_pass(send1, recv1, bufs, h1, "gather_pass_mixer")
    win, wgate = _gather_wait(send2, recv2, bufs, token, "gather_wait_mixer")
    win, wgate = win.reshape(IN_COLS, d), _full_cols(wgate)
    un, qkv, gate = _proj_fwd(h1, g_mix, win, wgate, "proj_fwd")
    ya = _attn_fwd("A", qkv, gqa, gka, bias_a, sinks, bl, s_len, "attn_a_fwd")
    rsend2, rrecv2, rest_bufs, token = _gather_pass(rsend1, rrecv1, rest_bufs, ya, "gather_pass_rest")
    yb = _attn_fwd("B", qkv, gqb + token[0, 0], gkb, bias_b, sinks, bl, s_len, "attn_b_fwd")
    gathered = dict(zip(rest_names, _gather_wait(rsend2, rrecv2, rest_bufs, yb, "gather_wait_rest")))
    wgu2 = gathered["ffn2_w_gu"]
    wd2 = gathered["ffn2_w_down"].reshape(N_DEV // 2, nf, d)
    wpa = _full_cols(gathered["w_proj_a"])
    wpb = _full_cols(gathered["w_proj_b"])
    wpe = _full_cols(gathered["w_ple_proj"])
    wout = gathered["w_out"].reshape(d, d)
    wpg = gathered["w_ple_gate"].reshape(d, d)
    h2, merged, pa, pb = _merge_fwd(h1, ya, yb, gate, wpa, wpb, wout, "merge_fwd")
    h3, gu2 = _ffn_fwd(h2, g_ffn2, wgu2, wd2, "ffn2_fwd")
    dh3, dz4, dpp, n4, dg_ple, loss_part = _ple_loss(h3, g_ple, pt, tgt, wpg, wpe, "ple_loss")

    xi, yi, ci = _place()
    me = jnp.stack([4 * xi + 2 * yi + ci]).astype(jnp.int32)
    g32, g16, big = {}, {}, {}

    def keep(nme, pair, rows=None):
        for store, g in zip((g32, g16), pair):
            store[nme] = g if rows is None else g.reshape(N_DEV, rows, d)

    def start(names, after, tag):
        send, recv, parts, lands, token = _scatter_start([g16[nme] for nme in names], after, "grads_start_" + tag)
        return names, send, recv, parts, lands, token

    def finish(state, after, tag):
        names, send, recv, parts, lands, _ = state
        lands = _scatter_wait(send, recv, parts, lands, after, "grads_wait_" + tag)
        return names, lands

    def adam(done, dep):
        for nme, land in zip(*done):
            outs = _final_adam(g32[nme], land, _local(w[nme], nme), _local(m[nme], nme), _local(v[nme], nme), me, dep,
                               "adam_" + nme)
            big[nme] = [(o.T if nme in TRANSPOSED else o)[None] for o in outs]

    keep("w_ple_gate", _dw(n4, dz4, 1, d, "dw_ple_gate"), d // N_DEV)
    keep("w_ple_proj", _dw(pt, dpp, N_DEV, d // N_DEV, "dw_ple_proj"))

    dh2, dgu2, a2, n3, dg_ffn2 = _ffn_bwd(dh3, h2, g_ffn2, gu2, wgu2, wd2, "ffn2_bwd")
    keep("ffn2_w_gu", _dw(dgu2, n3, N_DEV, d, "dw_ffn2_gu"))
    keep("ffn2_w_down", _dw(a2, dh3, N_DEV // 2, d, "dw_ffn2_down", 0.5), nf // 2)
    flight = start(("w_ple_gate", "w_ple_proj", "ffn2_w_gu", "ffn2_w_down"), dh2, "ffn2")

    dpa, dpb, dzg, dya, dyb = _merge_bwd(dh2, pa, pb, gate, wpa, wpb, wout, "merge_bwd")
    keep("w_out", _dw(merged, dh2, 1, d, "dw_out"), d // N_DEV)
    keep("w_proj_a", _dw(ya, dpa, N_DEV, d // N_DEV, "dw_proj_a"))
    keep("w_proj_b", _dw(yb, dpb, N_DEV, d // N_DEV, "dw_proj_b"))
    keep("w_gate", _dw(un, dzg, N_DEV, 2 * d // N_DEV, "dw_gate"))

    tok = flight[-1][0, 0]
    dqa, dka, dva, dgqa, dgka, dbias, _ = _attn_bwd("A", qkv, gqa + tok, gka, bias_a, sinks, ya, dya, bl, s_len,
                                                     "attn_a_bwd")
    dqb, dkb, dvb, dgqb, dgkb, _, dsink = _attn_bwd("B", qkv, gqb, gkb, bias_b, sinks, yb, dyb, bl, s_len, "attn_b_bwd")
    dqkv = [dqa, dka, dva, dqb, dkb, dvb]
    dtab = _rel_bias_grad(dbias, "rel_bias_grad")

    dh1, dg_mix = _proj_bwd(dh2, h1, g_mix, dzg, dqkv, win, wgate, "proj_bwd")
    keep("w_in", _dw_rows(dqkv, un, "dw_in"), IN_COLS // N_DEV)
    done = finish(flight, g32["w_in"], "ffn2")
    flight = start(("w_out", "w_proj_a", "w_proj_b", "w_gate", "w_in"), done[1][0], "mixer")
    waiting = [done]

    dh0, dgu1, a1, n1, dg_ffn1 = _ffn_bwd(dh1, h0, g_ffn1 + flight[-1][0, 0], gu1, wgu1, wd1, "ffn1_bwd")
    keep("ffn1_w_down", _dw(a1, dh1, N_DEV // 2, d, "dw_ffn1_down", 0.5), nf // 2)
    done = finish(flight, g32["ffn1_w_down"], "mixer")
    flight = start(("ffn1_w_down",), done[1][0], "ffn1_down")
    waiting.append(done)

    keep("ffn1_w_gu", _dw(dgu1, n1, N_DEV, d, "dw_ffn1_gu", dep=flight[-1]))
    done = finish(flight, g32["ffn1_w_gu"], "ffn1_down")
    flight = start(("ffn1_w_gu",), done[1][0], "ffn1_gu")
    for group in waiting + [done]:
        adam(group, flight[-1])
    behind = 0.0 * big["ffn1_w_down"][0][0, 0, :1]
    smalls = (dg_ffn1, dg_mix, dg_ffn2, dg_ple + behind, dgqa, dgka, dgqb, dgkb, dtab, dsink)
    return dh0, loss_part, big, smalls, flight, finish, adam


def kernel(x, p, ffn1_norm, ffn1_w_gu, ffn1_w_down, mix_norm, w_in, a_q_norm, a_k_norm, a_rel_bias, b_q_norm, b_k_norm, b_sinks, w_gate, w_proj_a, w_proj_b, w_out, ffn2_norm, ffn2_w_gu, ffn2_w_down, ple_norm, w_ple_gate, w_ple_proj, loss_target, m_ffn1_norm, m_ffn1_w_gu, m_ffn1_w_down, m_mix_norm, m_w_in, m_a_q_norm, m_a_k_norm, m_a_rel_bias, m_b_q_norm, m_b_k_norm, m_b_sinks, m_w_gate, m_w_proj_a, m_w_proj_b, m_w_out, m_ffn2_norm, m_ffn2_w_gu, m_ffn2_w_down, m_ple_norm, m_w_ple_gate, m_w_ple_proj, v_ffn1_norm, v_ffn1_w_gu, v_ffn1_w_down, v_mix_norm, v_w_in, v_a_q_norm, v_a_k_norm, v_a_rel_bias, v_b_q_norm, v_b_k_norm, v_b_sinks, v_w_gate, v_w_proj_a, v_w_proj_b, v_w_out, v_ffn2_norm, v_ffn2_w_gu, v_ffn2_w_down, v_ple_norm, v_w_ple_gate, v_w_ple_proj):
    w = dict(ffn1_norm=ffn1_norm, ffn1_w_gu=ffn1_w_gu, ffn1_w_down=ffn1_w_down, mix_norm=mix_norm, w_in=w_in,
             a_q_norm=a_q_norm, a_k_norm=a_k_norm, a_rel_bias=a_rel_bias, b_q_norm=b_q_norm, b_k_norm=b_k_norm,
             b_sinks=b_sinks, w_gate=w_gate, w_proj_a=w_proj_a, w_proj_b=w_proj_b, w_out=w_out, ffn2_norm=ffn2_norm,
             ffn2_w_gu=ffn2_w_gu, ffn2_w_down=ffn2_w_down, ple_norm=ple_norm, w_ple_gate=w_ple_gate,
             w_ple_proj=w_ple_proj)
    m = dict(ffn1_norm=m_ffn1_norm, ffn1_w_gu=m_ffn1_w_gu, ffn1_w_down=m_ffn1_w_down, mix_norm=m_mix_norm,
             w_in=m_w_in, a_q_norm=m_a_q_norm, a_k_norm=m_a_k_norm, a_rel_bias=m_a_rel_bias, b_q_norm=m_b_q_norm,
             b_k_norm=m_b_k_norm, b_sinks=m_b_sinks, w_gate=m_w_gate, w_proj_a=m_w_proj_a, w_proj_b=m_w_proj_b,
             w_out=m_w_out, ffn2_norm=m_ffn2_norm, ffn2_w_gu=m_ffn2_w_gu, ffn2_w_down=m_ffn2_w_down,
             ple_norm=m_ple_norm, w_ple_gate=m_w_ple_gate, w_ple_proj=m_w_ple_proj)
    v = dict(ffn1_norm=v_ffn1_norm, ffn1_w_gu=v_ffn1_w_gu, ffn1_w_down=v_ffn1_w_down, mix_norm=v_mix_norm,
             w_in=v_w_in, a_q_norm=v_a_q_norm, a_k_norm=v_a_k_norm, a_rel_bias=v_a_rel_bias, b_q_norm=v_b_q_norm,
             b_k_norm=v_b_k_norm, b_sinks=v_b_sinks, w_gate=v_w_gate, w_proj_a=v_w_proj_a, w_proj_b=v_w_proj_b,
             w_out=v_w_out, ffn2_norm=v_ffn2_norm, ffn2_w_gu=v_ffn2_w_gu, ffn2_w_down=v_ffn2_w_down,
             ple_norm=v_ple_norm, w_ple_gate=v_w_ple_gate, w_ple_proj=v_w_ple_proj)
    bl, s_len, d = x.shape

    dh0, loss_part, big, smalls, flight, finish, adam = _step(x, p[0], loss_target, w, m, v)
    dg_ffn1, dg_mix, dg_ffn2, dg_ple, dgqa, dgka, dgqb, dgkb, dtab, dsink = smalls

    fold = lambda a: a[:, :, 0, :].reshape(-1, HEAD_DIM).sum(axis=0)
    small_part = dict(
        ffn1_norm=dg_ffn1, mix_norm=dg_mix, ffn2_norm=dg_ffn2, ple_norm=dg_ple,
        a_q_norm=fold(dgqa), a_k_norm=fold(dgka), b_q_norm=fold(dgqb), b_k_norm=fold(dgkb),
        a_rel_bias=dtab,
        b_sinks=dsink.sum(axis=0)[:, 0, :GROUP].reshape(B_Q_HEADS),
        loss=loss_part[0, :1])
    zero1 = jnp.zeros((1,), F32)
    shapes = {nme: w[nme].shape for nme in SMALL_NAMES if nme != "loss"}
    shapes["loss"] = ()
    pk = lambda src: _pack_small({**{nme: src[nme] for nme in SMALL_NAMES if nme != "loss"}, "loss": zero1})
    sg, sd, sm, sv = _small_allreduce_adam(_pack_small(small_part), pk(w), pk(m), pk(v), "small_allreduce_adam")
    adam(finish(flight, sg, "ffn1_gu"), sg)
    sg, sd, sm, sv = (_unpack_small(a, shapes) for a in (sg, sd, sm, sv))

    def pick(i):
        out = []
        for nme in WEIGHT_ORDER:
            out.append(big[nme][i] if nme in big else (sg, sd, sm, sv)[i][nme])
        return out

    return (sg["loss"], dh0.reshape(bl, s_len, d), *pick(0), *pick(1), *pick(2), *pick(3))
```

```python
import functools

import jax
import jax.numpy as jnp
import numpy as np
from jax import lax
from jax.experimental import pallas as pl
from jax.experimental.pallas import tpu as pltpu

F32 = jnp.float32
BF16 = jnp.bfloat16

CHUNK = 64
HEAD_DIM = 64
A_HEADS = 8
A_PREV = 8
A_MAX_REL = 128
B_Q_HEADS = 8
B_KV_HEADS = 2
B_PREV = 2
A_WIDTH = A_HEADS * HEAD_DIM
B_Q_WIDTH = B_Q_HEADS * HEAD_DIM
B_KV_WIDTH = B_KV_HEADS * HEAD_DIM
IN_COLS = 3 * A_WIDTH + B_Q_WIDTH + 2 * B_KV_WIDTH
EPS = 1e-6
NEG_INF = -1e30
ADAM_LR = 0.001
ADAM_B1 = 0.9
ADAM_B2 = 0.999
ADAM_EPS = 1e-08
ADAM_WD = 0.01
ADAM_STEP = 10

N_DEV = 8
LANES = 128
QTILE = 2 * CHUNK
VMEM_LIMIT = 56 * 1024 * 1024
ADAM_TILE_ELEMS = 256 * 1024

MESH_ID = pl.DeviceIdType.MESH
ANY = pl.BlockSpec(memory_space=pl.ANY)
HBM = pl.BlockSpec(memory_space=pltpu.HBM)
SEM = pl.BlockSpec(memory_space=pltpu.SEMAPHORE)
SIDE_EFFECT = pltpu.SideEffectType.DATAFLOW_SIDE_EFFECTING


def _dot(a, b):
    return jnp.dot(a, b, preferred_element_type=F32)


def _dot_nt(a, b):
    return lax.dot_general(a, b, (((1,), (1,)), ((), ())), preferred_element_type=F32)


def _dot_tn(a, b):
    return lax.dot_general(a, b, (((0,), (0,)), ((), ())), preferred_element_type=F32)


def _params(sem=None, vmem=VMEM_LIMIT):
    return pltpu.CompilerParams(dimension_semantics=sem, vmem_limit_bytes=vmem)


def _row_tile(t, want):
    while t % want:
        want //= 2
    return want


def _place():
    return lax.axis_index("x"), lax.axis_index("y"), lax.axis_index("c")


def _all_gather(shards, name):
    n = len(shards)

    def body(*refs):
        ins, outs = refs[:n], refs[n:2 * n]
        send_sems, recv_sems, local_sems = refs[2 * n:]
        x, y, c = _place()
        me, sib = (x, y, c), (x, y, 1 - c)
        chips = [(1 - x, y), (x, 1 - y), (1 - x, 1 - y)]

        def copy(w, k, block, to, src=None):
            px, py, pc = block
            dst = outs[w].at[4 * px + 2 * py + pc]
            return pltpu.make_async_remote_copy(
                src_ref=dst if src is None else src, dst_ref=dst,
                send_sem=send_sems.at[w * 7 + k], recv_sem=recv_sems.at[w * 7 + k],
                device_id=to, device_id_type=MESH_ID)

        mine = [pltpu.make_async_copy(ins[w], outs[w].at[4 * x + 2 * y + c], local_sems.at[w]) for w in range(n)]
        for cp in mine:
            cp.start()
        first = []
        for w in range(n):
            first.append(copy(w, 0, me, sib, src=ins[w]))
            first += [copy(w, 1 + j, me, (*chip, c), src=ins[w]) for j, chip in enumerate(chips)]
        for cp in first:
            cp.start()
        passed = []
        for j, chip in enumerate(chips):
            for w in range(n):
                copy(w, 1 + j, (*chip, c), me).wait_recv()
                fwd = copy(w, 4 + j, (*chip, c), sib)
                fwd.start()
                passed.append(fwd)
        for w in range(n):
            copy(w, 0, sib, me).wait_recv()
        for j, chip in enumerate(chips):
            for w in range(n):
                copy(w, 4 + j, (*chip, 1 - c), me).wait_recv()
        for cp in first + passed:
            cp.wait_send()
        for cp in mine:
            cp.wait()

    return pl.pallas_call(
        body, name=name,
        out_shape=[jax.ShapeDtypeStruct((N_DEV,) + s.shape, s.dtype) for s in shards],
        in_specs=[ANY] * n, out_specs=[ANY] * n,
        scratch_shapes=[pltpu.SemaphoreType.DMA((7 * n,)), pltpu.SemaphoreType.DMA((7 * n,)),
                        pltpu.SemaphoreType.DMA((n,))],
    )(*shards)


def _gather_level(bufs, send_sems, recv_sems, level, shards=None):
    x, y, c = _place()
    me, sib = (x, y, c), (x, y, 1 - c)
    chips = [(1 - x, y), (x, 1 - y), (1 - x, 1 - y)]

    def copy(w, k, block, to):
        px, py, pc = block
        rows = bufs[w].at[4 * px + 2 * py + pc]
        src = shards[w] if shards is not None and block is me else rows
        return pltpu.make_async_remote_copy(src_ref=src, dst_ref=rows, send_sem=send_sems.at[k], recv_sem=recv_sems.at[k],
                                            device_id=to, device_id_type=MESH_ID)

    n = len(bufs)
    own = []
    if level == 1:
        own = [pltpu.make_async_copy(bufs[w].at[4 * x + 2 * y + c] if shards is None else shards[w],
                                     bufs[w].at[4 * x + 2 * y + c], send_sems.at[4 * n + w]) for w in range(n)]
    out, arriving = [], []
    for w in range(len(bufs)):
        if level == 1:
            out.append(copy(w, 4 * w, me, sib))
            arriving.append(copy(w, 4 * w, sib, me))
        for j, chip in enumerate(chips):
            if level == 1:
                out.append(copy(w, 4 * w + 1 + j, me, (*chip, c)))
                arriving.append(copy(w, 4 * w + 1 + j, (*chip, c), me))
            else:
                out.append(copy(w, 3 * w + j, (*chip, c), sib))
                arriving.append(copy(w, 3 * w + j, (*chip, 1 - c), me))
    return out, arriving, own


def _split_call(body, name, bufs, sems_in, after, n_sems_out, token, extra=()):
    n = len(bufs)
    out_shape = [pltpu.SemaphoreType.DMA((n_sems_out,))] * (2 if n_sems_out else 0)
    out_shape += [pltpu.HBM(a.shape, a.dtype) for a in bufs]
    out_specs = [SEM] * (2 if n_sems_out else 0) + [HBM] * n
    if token:
        out_shape.append(jax.ShapeDtypeStruct((8, LANES), F32))
        out_specs.append(pl.BlockSpec(memory_space=pltpu.VMEM))
    first = 2 if n_sems_out else 0
    return pl.pallas_call(
        body, name=name, out_shape=tuple(out_shape),
        in_specs=[HBM] * (n + len(extra)) + [SEM] * len(sems_in) + [ANY], out_specs=tuple(out_specs),
        input_output_aliases={i: first + i for i in range(n)},
        compiler_params=pltpu.CompilerParams(has_side_effects=SIDE_EFFECT),
    )(*bufs, *extra, *sems_in, after)


def _gather_start(shards, after, name):
    n = len(shards)
    hbm = lambda a: pltpu.with_memory_space_constraint(a, pltpu.HBM)
    bufs = [hbm(lax.empty((N_DEV,) + s.shape, s.dtype)) for s in shards]

    def body(*refs):
        out, _, own = _gather_level(refs[:n], refs[2 * n + 1], refs[2 * n + 2], 1, shards=refs[n:2 * n])
        for cp in own + out:
            cp.start()
        refs[-1][...] = jnp.zeros_like(refs[-1])

    outs = _split_call(body, name, bufs + [hbm(s) for s in shards], [], after, 5 * n, True)
    return outs[0], outs[1], list(outs[2:2 + 2 * n]), outs[-1]


def _gather_pass(send1, recv1, bufs_and_shards, after, name):
    n = len(bufs_and_shards) // 2
    bufs = bufs_and_shards

    def body(*refs):
        refs = refs[:n] + refs[2 * n:]
        out1, in1, own = _gather_level(refs[:n], refs[n], refs[n + 1], 1)
        out2, _, _ = _gather_level(refs[:n], refs[n + 3], refs[n + 4], 2)
        for cp in in1:
            cp.wait_recv()
        for cp in out2:
            cp.start()
        for cp in out1:
            cp.wait_send()
        for cp in own:
            cp.wait()
        refs[-1][...] = jnp.zeros_like(refs[-1])

    outs = _split_call(body, name, bufs, [send1, recv1], after, 3 * n, True)
    return outs[0], outs[1], list(outs[2:2 + n]), outs[-1]


def _gather_wait(send2, recv2, bufs, after, name):
    n = len(bufs)

    def body(*refs):
        out2, in2, _ = _gather_level(refs[:n], refs[n], refs[n + 1], 2)
        for cp in in2:
            cp.wait_recv()
        for cp in out2:
            cp.wait_send()

    return list(_split_call(body, name, bufs, [send2, recv2], after, 0, False))


def _scatter_copies(parts, lands, send_sems, recv_sems):
    x, y, c = _place()
    cps = []
    for w, (part, land) in enumerate(zip(parts, lands)):
        for k in range(1, N_DEV):
            px, py, pc = x ^ ((k >> 2) & 1), y ^ ((k >> 1) & 1), c ^ (k & 1)
            cps.append(pltpu.make_async_remote_copy(
                src_ref=part.at[4 * px + 2 * py + pc], dst_ref=land.at[k - 1],
                send_sem=send_sems.at[7 * w + k - 1], recv_sem=recv_sems.at[7 * w + k - 1],
                device_id=(px, py, pc), device_id_type=MESH_ID))
    return cps


def _scatter_start(parts, after, name):
    n = len(parts)

    def body(*refs):
        ins, lands = refs[:n], refs[n:2 * n]
        send_sems, recv_sems = refs[2 * n + 1], refs[2 * n + 2]
        token = refs[-1]
        for cp in _scatter_copies(ins, lands, send_sems, recv_sems):
            cp.start()
        token[...] = jnp.zeros_like(token)

    land_shapes = [(N_DEV - 1,) + p.shape[1:] for p in parts]
    in_hbm = [pltpu.with_memory_space_constraint(p, pltpu.HBM) for p in parts]
    in_hbm += [pltpu.with_memory_space_constraint(lax.empty(s, p.dtype), pltpu.HBM) for s, p in zip(land_shapes, parts)]
    outs = pl.pallas_call(
        body, name=name,
        out_shape=(pltpu.SemaphoreType.DMA((7 * n,)), pltpu.SemaphoreType.DMA((7 * n,)),
                   *[pltpu.HBM(p.shape, p.dtype) for p in parts],
                   *[pltpu.HBM(s, p.dtype) for s, p in zip(land_shapes, parts)],
                   jax.ShapeDtypeStruct((8, LANES), F32)),
        in_specs=[HBM] * (2 * n) + [ANY],
        out_specs=(SEM, SEM, *[HBM] * (2 * n), pl.BlockSpec(memory_space=pltpu.VMEM)),
        input_output_aliases={i: 2 + i for i in range(2 * n)},
        compiler_params=pltpu.CompilerParams(has_side_effects=SIDE_EFFECT),
    )(*in_hbm, after)
    return outs[0], outs[1], list(outs[2:2 + n]), list(outs[2 + n:2 + 2 * n]), outs[-1]


def _scatter_wait(send_sems, recv_sems, parts, lands, after, name):
    n = len(parts)

    def body(*refs):
        ins, lnd = refs[:n], refs[n:2 * n]
        for cp in _scatter_copies(ins, lnd, refs[2 * n], refs[2 * n + 1]):
            cp.wait_send()
            cp.wait_recv()

    outs = pl.pallas_call(
        body, name=name,
        out_shape=tuple(pltpu.HBM(a.shape, a.dtype) for a in parts + lands),
        in_specs=[HBM] * (2 * n) + [SEM, SEM, ANY],
        out_specs=tuple([HBM] * (2 * n)),
        input_output_aliases={i: i for i in range(2 * n)},
        compiler_params=pltpu.CompilerParams(has_side_effects=SIDE_EFFECT),
    )(*parts, *lands, send_sems, recv_sems, after)
    return list(outs[n:])


def _adam(w, g, m, v):
    m2 = ADAM_B1 * m + (1.0 - ADAM_B1) * g
    v2 = ADAM_B2 * v + (1.0 - ADAM_B2) * (g * g)
    m_hat = m2 / (1.0 - ADAM_B1 ** ADAM_STEP)
    v_hat = v2 / (1.0 - ADAM_B2 ** ADAM_STEP)
    delta = -ADAM_LR * (m_hat / (jnp.sqrt(v_hat) + ADAM_EPS) + ADAM_WD * w)
    return delta, m2, v2


def _small_allreduce_adam(part, w, m, v, name):
    rows = part.shape[0]

    def body(p_ref, w_ref, m_ref, v_ref, g_ref, d_ref, mo_ref, vo_ref, buf, send_sems, recv_sems):
        x, y, c = _place()
        buf[0] = p_ref[...]
        cps = []
        for k in range(1, N_DEV):
            kx, ky, kc = (k >> 2) & 1, (k >> 1) & 1, k & 1
            peer = (x ^ kx, y ^ ky, c ^ kc)
            cps.append(pltpu.make_async_remote_copy(
                src_ref=p_ref, dst_ref=buf.at[k], send_sem=send_sems.at[k - 1], recv_sem=recv_sems.at[k - 1],
                device_id=peer, device_id_type=MESH_ID))
        for cp in cps:
            cp.start()
        for cp in cps:
            cp.wait()
        me = 4 * x + 2 * y + c
        total = buf[me]
        for d in range(1, N_DEV):
            total = total + buf[d ^ me]
        g_ref[...] = total
        delta, m2, v2 = _adam(w_ref[...], total, m_ref[...], v_ref[...])
        d_ref[...] = delta
        mo_ref[...] = m2
        vo_ref[...] = v2

    vm = pl.BlockSpec(memory_space=pltpu.VMEM)
    return pl.pallas_call(
        body, name=name,
        out_shape=[jax.ShapeDtypeStruct(part.shape, F32)] * 4,
        in_specs=[vm] * 4, out_specs=[vm] * 4,
        scratch_shapes=[pltpu.VMEM((N_DEV, rows, LANES), F32),
                        pltpu.SemaphoreType.DMA((N_DEV - 1,)), pltpu.SemaphoreType.DMA((N_DEV - 1,))],
    )(part, w, m, v)


def _final_adam(g8, land, w, m, v, me, dep, name):
    _, r, c = g8.shape
    tr = max(q for q in range(16, r + 1, 16) if r % q == 0 and q * c <= ADAM_TILE_ELEMS)

    def body(me_ref, g_ref, land_ref, w_ref, m_ref, v_ref, dep_ref, go_ref, d_ref, mo_ref, vo_ref):
        del dep_ref
        g = g_ref[...]
        for k in range(N_DEV - 1):
            g = g + land_ref[k].astype(F32)
        go_ref[...] = g
        delta, m2, v2 = _adam(w_ref[...], g, m_ref[...], v_ref[...])
        d_ref[...] = delta
        mo_ref[...] = m2
        vo_ref[...] = v2

    plain = pl.BlockSpec((tr, c), lambda i, s: (i, 0))
    return pl.pallas_call(
        body, name=name,
        out_shape=[jax.ShapeDtypeStruct((r, c), F32)] * 4,
        grid_spec=pltpu.PrefetchScalarGridSpec(
            num_scalar_prefetch=1, grid=(r // tr,),
            in_specs=[pl.BlockSpec((None, tr, c), lambda i, s: (s[0], i, 0)),
                      pl.BlockSpec((N_DEV - 1, tr, c), lambda i, s: (0, i, 0)),
                      plain, plain, plain, ANY],
            out_specs=[plain] * 4),
        compiler_params=_params(("arbitrary",)),
    )(me, g8, land, w, m, v, dep)


def _rms(x, gain):
    r = lax.rsqrt(jnp.mean(x * x, axis=-1, keepdims=True) + EPS)
    xh = x * r
    return xh * gain, xh, r


def _rms_bwd(xh, r, gain, dy):
    gdy = gain * dy
    dx = r * (gdy - xh * jnp.mean(xh * gdy, axis=-1, keepdims=True))
    return dx, jnp.sum(dy * xh, axis=0, keepdims=True)


def _load_weights(pairs, sems):
    cps = [pltpu.make_async_copy(src, dst, sems.at[i]) for i, (src, dst) in enumerate(pairs)]
    for cp in cps:
        cp.start()
    for cp in cps:
        cp.wait()


def _ffn_fwd(h, gain, wgu, wd, name):
    t, d = h.shape
    nb, nf, _ = wgu.shape
    nh = nb // 2
    tm = _row_tile(t, 512)

    def body(h_ref, g_ref, wgu_hbm, wd_hbm, out_ref, gu_ref, wgu_v, wd_v, sems):
        @pl.when(pl.program_id(0) == 0)
        def _():
            _load_weights([(wgu_hbm, wgu_v), (wd_hbm, wd_v)], sems)

        x = h_ref[...]
        n, _, _ = _rms(x, g_ref[...])
        nbf = n.astype(BF16)
        acc = jnp.zeros((tm, d), F32)
        for j in range(nh):
            g = _dot_nt(nbf, wgu_v[j])
            u = _dot_nt(nbf, wgu_v[j + nh])
            gu_ref[j] = g.astype(BF16)
            gu_ref[j + nh] = u.astype(BF16)
            a = (g * jax.nn.sigmoid(g)) * u
            acc = acc + _dot(a.astype(BF16), wd_v[j])
        out_ref[...] = x + 0.5 * acc

    return pl.pallas_call(
        body, name=name, grid=(t // tm,),
        out_shape=[jax.ShapeDtypeStruct((t, d), F32), jax.ShapeDtypeStruct((nb, t, nf), BF16)],
        in_specs=[pl.BlockSpec((tm, d), lambda i: (i, 0)), pl.BlockSpec((1, d), lambda i: (0, 0)), ANY, ANY],
        out_specs=[pl.BlockSpec((tm, d), lambda i: (i, 0)), pl.BlockSpec((nb, tm, nf), lambda i: (0, i, 0))],
        scratch_shapes=[pltpu.VMEM(wgu.shape, BF16), pltpu.VMEM(wd.shape, BF16), pltpu.SemaphoreType.DMA((2,))],
        compiler_params=_params(("arbitrary",)),
    )(h, gain, wgu, wd)


def _ffn_bwd(dh, h, gain, gu, wgu, wd, name):
    t, d = h.shape
    nb, nf, _ = wgu.shape
    nh = nb // 2
    tm = _row_tile(t, 256)

    def body(dh_ref, h_ref, g_ref, gu_ref, wgu_hbm, wd_hbm, dhp_ref, dgu_ref, a_ref, n_ref, dgain_ref,
             wgu_v, wd_v, sems):
        @pl.when(pl.program_id(0) == 0)
        def _():
            _load_weights([(wgu_hbm, wgu_v), (wd_hbm, wd_v)], sems)
            dgain_ref[...] = jnp.zeros_like(dgain_ref)

        x = h_ref[...]
        gain_v = g_ref[...]
        n, xh, r = _rms(x, gain_v)
        n_ref[...] = n.astype(BF16)
        dh_v = dh_ref[...]
        dfb = (0.5 * dh_v).astype(BF16)
        dn = jnp.zeros((tm, d), F32)
        for j in range(nh):
            da = _dot_nt(dfb, wd_v[j])
            g = gu_ref[j].astype(F32)
            u = gu_ref[j + nh].astype(F32)
            sg = jax.nn.sigmoid(g)
            si = g * sg
            dg = (da * u * (sg * (1.0 + g * (1.0 - sg)))).astype(BF16)
            du = (da * si).astype(BF16)
            a_ref[j] = (si * u).astype(BF16)
            dgu_ref[j] = dg
            dgu_ref[j + nh] = du
            dn = dn + _dot(dg, wgu_v[j]) + _dot(du, wgu_v[j + nh])
        dx, dgain = _rms_bwd(xh, r, gain_v, dn)
        dhp_ref[...] = dh_v + dx
        dgain_ref[...] += dgain

    row = pl.BlockSpec((tm, d), lambda i: (i, 0))
    vec = pl.BlockSpec((1, d), lambda i: (0, 0))
    return pl.pallas_call(
        body, name=name, grid=(t // tm,),
        out_shape=[jax.ShapeDtypeStruct((t, d), F32), jax.ShapeDtypeStruct((nb, t, nf), BF16),
                   jax.ShapeDtypeStruct((nh, t, nf), BF16), jax.ShapeDtypeStruct((t, d), BF16),
                   jax.ShapeDtypeStruct((1, d), F32)],
        in_specs=[row, row, vec, pl.BlockSpec((nb, tm, nf), lambda i: (0, i, 0)), ANY, ANY],
        out_specs=[row, pl.BlockSpec((nb, tm, nf), lambda i: (0, i, 0)),
                   pl.BlockSpec((nh, tm, nf), lambda i: (0, i, 0)), row, vec],
        scratch_shapes=[pltpu.VMEM(wgu.shape, BF16), pltpu.VMEM(wd.shape, BF16), pltpu.SemaphoreType.DMA((2,))],
        compiler_params=_params(("arbitrary",)),
    )(dh, h, gain, gu, wgu, wd)


def _dw(xa, dy, nb, n, name, scale=1.0, dep=None):
    t, k = xa.shape[-2:]
    tt = _row_tile(t, 512)
    steps = t // tt
    wide = dy.ndim == 2 and xa.ndim == 2
    if xa.ndim == 3:
        x_spec = pl.BlockSpec((nb, tt, k), lambda i: (0, i, 0))
    else:
        x_spec = pl.BlockSpec((tt, k), lambda i: (i, 0))
    if dy.ndim == 3:
        dy_spec = pl.BlockSpec((nb, tt, n), lambda i: (0, i, 0))
    else:
        dy_spec = pl.BlockSpec((tt, dy.shape[1]), lambda i: (i, 0))
    acc_shape = (k, nb * n) if wide else (nb, k, n)
    stage_shape = (k, nb * n) if wide else (k, n)

    def body(x_ref, dy_ref, *rest):
        o_hbm, ob_hbm, acc, stage, sems = rest[-5:]

        @pl.when(pl.program_id(0) == 0)
        def _():
            acc[...] = jnp.zeros_like(acc)

        if wide:
            acc[...] += _dot(x_ref[...].astype(BF16).T, dy_ref[...].astype(BF16))
        elif xa.ndim == 2:
            xt = x_ref[...].astype(BF16).T
            for j in range(nb):
                acc[j] += _dot(xt, dy_ref[j].astype(BF16))
        else:
            dyb = dy_ref[...].astype(BF16)
            for j in range(nb):
                acc[j] += _dot_tn(x_ref[j].astype(BF16), dyb)

        @pl.when(pl.program_id(0) == steps - 1)
        def _():
            if scale != 1.0:
                acc[...] = acc[...] * scale
            if wide:
                cps = [pltpu.make_async_copy(acc.at[:, pl.ds(j * n, n)] if nb > 1 else acc, o_hbm.at[j], sems.at[j])
                       for j in range(nb)]
            else:
                cps = [pltpu.make_async_copy(acc, o_hbm, sems.at[0])]
            for cp in cps:
                cp.start()
            if wide:
                stage[...] = acc[...].astype(BF16)
                bcs = [pltpu.make_async_copy(stage.at[:, pl.ds(j * n, n)] if nb > 1 else stage, ob_hbm.at[j],
                                             sems.at[nb + j]) for j in range(nb)]
                for cp in bcs:
                    cp.start()
                for cp in bcs:
                    cp.wait()
            else:
                for j in range(nb):
                    stage[...] = acc[j].astype(BF16)
                    cp = pltpu.make_async_copy(stage, ob_hbm.at[j], sems.at[nb])
                    cp.start()
                    cp.wait()
            for cp in cps:
                cp.wait()

    return pl.pallas_call(
        body, name=name, grid=(steps,),
        out_shape=[jax.ShapeDtypeStruct((nb, k, n), F32), jax.ShapeDtypeStruct((nb, k, n), BF16)],
        in_specs=[x_spec, dy_spec] + ([] if dep is None else [ANY]),
        out_specs=[ANY, ANY],
        scratch_shapes=[pltpu.VMEM(acc_shape, F32), pltpu.VMEM(stage_shape, BF16),
                        pltpu.SemaphoreType.DMA((2 * nb,))],
        compiler_params=_params(("arbitrary",)),
    )(*((xa, dy) if dep is None else (xa, dy, dep)))


def _proj_fwd(h, gain, win, wgate, name):
    t, d = h.shape
    tm = _row_tile(t, 256)
    nq, ng = win.shape[0], wgate.shape[1]

    def body(h_ref, g_ref, win_ref, wg_ref, un_ref, qkv_ref, gate_ref):
        n, _, _ = _rms(h_ref[...], g_ref[...])
        nbf = n.astype(BF16)
        un_ref[...] = nbf
        qkv_ref[...] = _dot_nt(nbf, win_ref[...])
        gate_ref[...] = jax.nn.sigmoid(_dot(nbf, wg_ref[...]))

    full = lambda a: pl.BlockSpec(a.shape, lambda i: (0,) * a.ndim)
    return pl.pallas_call(
        body, name=name, grid=(t // tm,),
        out_shape=[jax.ShapeDtypeStruct((t, d), BF16), jax.ShapeDtypeStruct((t, nq), F32),
                   jax.ShapeDtypeStruct((t, ng), F32)],
        in_specs=[pl.BlockSpec((tm, d), lambda i: (i, 0)), full(gain), full(win), full(wgate)],
        out_specs=[pl.BlockSpec((tm, d), lambda i: (i, 0)), pl.BlockSpec((tm, nq), lambda i: (i, 0)),
                   pl.BlockSpec((tm, ng), lambda i: (i, 0))],
        compiler_params=_params(("arbitrary",)),
    )(h, gain, win, wgate)


def _proj_bwd(dh, h, gain, dzg, dqkv_parts, win, wgate, name):
    t, d = h.shape
    tm = _row_tile(t, 256)
    ng = wgate.shape[1]
    np_ = len(dqkv_parts)
    widths = [a.shape[1] for a in dqkv_parts]

    def body(dh_ref, h_ref, g_ref, dzg_ref, *rest):
        part_refs, (win_ref, wg_ref, dhp_ref, dgain_ref) = rest[:np_], rest[np_:]

        @pl.when(pl.program_id(0) == 0)
        def _():
            dgain_ref[...] = jnp.zeros_like(dgain_ref)

        gain_v = g_ref[...]
        _, xh, r = _rms(h_ref[...], gain_v)
        dun = _dot_nt(dzg_ref[...], wg_ref[...])
        off = 0
        for ref, wd in zip(part_refs, widths):
            dun = dun + _dot(ref[...].astype(BF16), win_ref[off:off + wd, :])
            off += wd
        dx, dgain = _rms_bwd(xh, r, gain_v, dun)
        dhp_ref[...] = dh_ref[...] + dx
        dgain_ref[...] += dgain

    full = lambda a: pl.BlockSpec(a.shape, lambda i: (0,) * a.ndim)
    row = pl.BlockSpec((tm, d), lambda i: (i, 0))
    return pl.pallas_call(
        body, name=name, grid=(t // tm,),
        out_shape=[jax.ShapeDtypeStruct((t, d), F32), jax.ShapeDtypeStruct((1, d), F32)],
        in_specs=[row, row, full(gain), pl.BlockSpec((tm, ng), lambda i: (i, 0))]
        + [pl.BlockSpec((tm, wd), lambda i: (i, 0)) for wd in widths] + [full(win), full(wgate)],
        out_specs=[row, pl.BlockSpec((1, d), lambda i: (0, 0))],
        compiler_params=_params(("arbitrary",)),
    )(dh, h, gain, dzg, *dqkv_parts, win, wgate)


def _dw_rows(parts, dy, name):
    t, n = dy.shape
    widths = [a.shape[1] for a in parts]
    k = sum(widths)
    tt = _row_tile(t, 512)
    steps = t // tt
    np_ = len(parts)

    def body(*refs):
        part_refs, dy_ref = refs[:np_], refs[np_]
        o_hbm, ob_hbm, acc, stage, sems = refs[np_ + 1:]

        @pl.when(pl.program_id(0) == 0)
        def _():
            acc[...] = jnp.zeros_like(acc)

        dyb = dy_ref[...].astype(BF16)
        off = 0
        for ref, wd in zip(part_refs, widths):
            acc[off:off + wd, :] += _dot(ref[...].astype(BF16).T, dyb)
            off += wd

        @pl.when(pl.program_id(0) == steps - 1)
        def _():
            stage[...] = acc[...].astype(BF16)
            cps = [pltpu.make_async_copy(acc, o_hbm.at[0], sems.at[0]),
                   pltpu.make_async_copy(stage, ob_hbm.at[0], sems.at[1])]
            for cp in cps:
                cp.start()
            for cp in cps:
                cp.wait()

    return pl.pallas_call(
        body, name=name, grid=(steps,),
        out_shape=[jax.ShapeDtypeStruct((1, k, n), F32), jax.ShapeDtypeStruct((1, k, n), BF16)],
        in_specs=[pl.BlockSpec((tt, wd), lambda i: (i, 0)) for wd in widths] + [pl.BlockSpec((tt, n), lambda i: (i, 0))],
        out_specs=[ANY, ANY],
        scratch_shapes=[pltpu.VMEM((k, n), F32), pltpu.VMEM((k, n), BF16), pltpu.SemaphoreType.DMA((2,))],
        compiler_params=_params(("arbitrary",)),
    )(*parts, dy)


def _merge_fwd(h, ya, yb, gate, wpa, wpb, wout, name):
    t, d = h.shape
    tm = _row_tile(t, 256)

    def body(h_ref, ya_ref, yb_ref, ga_ref, gb_ref, wpa_ref, wpb_ref, wout_ref, out_ref, mg_ref, pa_ref, pb_ref):
        pa = _dot(ya_ref[...].astype(BF16), wpa_ref[...])
        pb = _dot(yb_ref[...].astype(BF16), wpb_ref[...])
        merged = (ga_ref[...] * pa + gb_ref[...] * pb).astype(BF16)
        pa_ref[...] = pa.astype(BF16)
        pb_ref[...] = pb.astype(BF16)
        mg_ref[...] = merged
        out_ref[...] = h_ref[...] + _dot(merged, wout_ref[...])

    full = lambda a: pl.BlockSpec(a.shape, lambda i: (0,) * a.ndim)
    row = pl.BlockSpec((tm, d), lambda i: (i, 0))
    yrow = pl.BlockSpec((tm, ya.shape[1]), lambda i: (i, 0))
    return pl.pallas_call(
        body, name=name, grid=(t // tm,),
        out_shape=[jax.ShapeDtypeStruct((t, d), F32)] + [jax.ShapeDtypeStruct((t, d), BF16)] * 3,
        in_specs=[row, yrow, yrow, pl.BlockSpec((tm, d), lambda i: (i, 0)), pl.BlockSpec((tm, d), lambda i: (i, 1)),
                  full(wpa), full(wpb), full(wout)],
        out_specs=[row] * 4,
        compiler_params=_params(("arbitrary",)),
    )(h, ya, yb, gate, gate, wpa, wpb, wout)


def _merge_bwd(dh, pa, pb, gate, wpa, wpb, wout, name):
    t, d = dh.shape
    tm = _row_tile(t, 256)
    wy = wpa.shape[0]

    def body(dh_ref, pa_ref, pb_ref, ga_ref, gb_ref, wpa_ref, wpb_ref, wout_ref,
             dpa_ref, dpb_ref, dzg_ref, dya_ref, dyb_ref):
        dm = _dot_nt(dh_ref[...].astype(BF16), wout_ref[...])
        ga, gb = ga_ref[...], gb_ref[...]
        dpa = (dm * ga).astype(BF16)
        dpb = (dm * gb).astype(BF16)
        dpa_ref[...] = dpa
        dpb_ref[...] = dpb
        dzg_ref[:, :d] = (dm * pa_ref[...].astype(F32) * ga * (1.0 - ga)).astype(BF16)
        dzg_ref[:, d:] = (dm * pb_ref[...].astype(F32) * gb * (1.0 - gb)).astype(BF16)
        dya_ref[...] = _dot_nt(dpa, wpa_ref[...])
        dyb_ref[...] = _dot_nt(dpb, wpb_ref[...])

    full = lambda a: pl.BlockSpec(a.shape, lambda i: (0,) * a.ndim)
    row = pl.BlockSpec((tm, d), lambda i: (i, 0))
    yrow = pl.BlockSpec((tm, wy), lambda i: (i, 0))
    return pl.pallas_call(
        body, name=name, grid=(t // tm,),
        out_shape=[jax.ShapeDtypeStruct((t, d), BF16), jax.ShapeDtypeStruct((t, d), BF16),
                   jax.ShapeDtypeStruct((t, 2 * d), BF16), jax.ShapeDtypeStruct((t, wy), F32),
                   jax.ShapeDtypeStruct((t, wy), F32)],
        in_specs=[row, row, row, pl.BlockSpec((tm, d), lambda i: (i, 0)), pl.BlockSpec((tm, d), lambda i: (i, 1)),
                  full(wpa), full(wpb), full(wout)],
        out_specs=[row, row, pl.BlockSpec((tm, 2 * d), lambda i: (i, 0)), yrow, yrow],
        compiler_params=_params(("arbitrary",)),
    )(dh, pa, pb, gate, gate, wpa, wpb, wout)


def _ple_loss(h, gain, p, target, wpg, wpe, name):
    t, d = h.shape
    tm = _row_tile(t, 256)
    pd = p.shape[1]

    def body(h_ref, g_ref, p_ref, t_ref, wpg_ref, wpe_ref, dh_ref, dz_ref, dpp_ref, n_ref, dgain_ref, loss_ref):
        @pl.when(pl.program_id(0) == 0)
        def _():
            dgain_ref[...] = jnp.zeros_like(dgain_ref)
            loss_ref[...] = jnp.zeros_like(loss_ref)

        x = h_ref[...]
        gain_v = g_ref[...]
        n, xh, r = _rms(x, gain_v)
        nbf = n.astype(BF16)
        n_ref[...] = nbf
        pg = jax.nn.sigmoid(_dot(nbf, wpg_ref[...]))
        pp = _dot(p_ref[...].astype(BF16), wpe_ref[...])
        err = (x + pg * pp) - t_ref[...]
        loss_ref[...] += 0.5 * jnp.sum(jnp.mean(err * err, axis=-1, keepdims=True))
        dy = err * (1.0 / d)
        dpp_ref[...] = (dy * pg).astype(BF16)
        dz = (dy * pp * pg * (1.0 - pg)).astype(BF16)
        dz_ref[...] = dz
        dn = _dot_nt(dz, wpg_ref[...])
        dx, dgain = _rms_bwd(xh, r, gain_v, dn)
        dh_ref[...] = dy + dx
        dgain_ref[...] += dgain

    full = lambda a: pl.BlockSpec(a.shape, lambda i: (0,) * a.ndim)
    row = pl.BlockSpec((tm, d), lambda i: (i, 0))
    return pl.pallas_call(
        body, name=name, grid=(t // tm,),
        out_shape=[jax.ShapeDtypeStruct((t, d), F32), jax.ShapeDtypeStruct((t, d), BF16),
                   jax.ShapeDtypeStruct((t, d), BF16), jax.ShapeDtypeStruct((t, d), BF16),
                   jax.ShapeDtypeStruct((1, d), F32), jax.ShapeDtypeStruct((8, LANES), F32)],
        in_specs=[row, full(gain), pl.BlockSpec((tm, pd), lambda i: (i, 0)), row, full(wpg), full(wpe)],
        out_specs=[row, row, row, row, pl.BlockSpec((1, d), lambda i: (0, 0)),
                   pl.BlockSpec((8, LANES), lambda i: (0, 0))],
        compiler_params=_params(("arbitrary",)),
    )(h, gain, p, target, wpg, wpe)


def _head_masks():
    lane = lax.broadcasted_iota(jnp.int32, (1, LANES), 1)
    m0 = (lane < HEAD_DIM).astype(F32)
    return m0, 1.0 - m0


def _head_mean(v, m0, m1):
    del m0, m1
    width = v.shape[-1]
    shift = HEAD_DIM.bit_length() - 1
    r = jnp.right_shift(lax.broadcasted_iota(jnp.int32, (width, width), 0), shift)
    c = jnp.right_shift(lax.broadcasted_iota(jnp.int32, (width, width), 1), shift)
    same_head = (r == c).astype(BF16)
    hi = v.astype(BF16)
    lo = (v - hi.astype(F32)).astype(BF16)
    return (_dot(hi, same_head) + _dot(lo, same_head)) * (1.0 / HEAD_DIM)


def _head_norm(x, gain, m0, m1):
    r = lax.rsqrt(_head_mean(x * x, m0, m1) + EPS)
    xh = x * r
    return xh * gain, xh, r


def _head_norm_bwd(xh, r, gain, dy, m0, m1):
    gdy = gain * dy
    dx = r * (gdy - xh * _head_mean(xh * gdy, m0, m1))
    return dx, jnp.sum(dy * xh, axis=0, keepdims=True)


GROUP = 4
QW = GROUP * HEAD_DIM
STACK = GROUP * QTILE


def _kv_width(mode):
    return QW if mode == "A" else LANES


def _q_scratch_shape(mode, s_len):
    return (s_len, QW) if mode == "A" else (GROUP * s_len, LANES)


def _group_masks(dtype=F32):
    lane = lax.broadcasted_iota(jnp.int32, (1, QW), 1)
    return [((lane >= h * HEAD_DIM) & (lane < (h + 1) * HEAD_DIM)).astype(dtype) for h in range(GROUP)]


def _stack_heads(first_kv, x, m0, m1):
    out = []
    for half in range(GROUP // 2):
        xh = x[:, half * LANES:(half + 1) * LANES]
        a0, a1 = xh * m0, xh * m1
        r0, r1 = pltpu.roll(a0, HEAD_DIM, 1), pltpu.roll(a1, HEAD_DIM, 1)
        out += [jnp.where(first_kv, a0, r0), jnp.where(first_kv, r1, a1)]
    return out


def _unstack_heads(mode, first_kv, ts, m0, m1):
    if mode == "A":
        masks = _group_masks()
        return sum(t * mk for t, mk in zip(ts, masks))
    halves = []
    for half in range(GROUP // 2):
        t0 = jnp.where(first_kv, ts[2 * half], pltpu.roll(ts[2 * half], HEAD_DIM, 1))
        t1 = jnp.where(first_kv, pltpu.roll(ts[2 * half + 1], HEAD_DIM, 1), ts[2 * half + 1])
        halves.append(t0 * m0 + t1 * m1)
    return jnp.concatenate(halves, axis=1)


def _store_stacked(dst, i, heads):
    for half in range(2):
        rows = slice(half * QTILE, (half + 1) * QTILE)
        for h, x in enumerate(heads):
            dst[pl.ds((2 * i + half) * STACK + h * QTILE, QTILE), :] = x[rows].astype(dst.dtype)


def _load_stacked(mode, ref, m):
    if mode == "B":
        return ref[pl.ds(pl.multiple_of(m * STACK, STACK), STACK), :]
    x = ref[pl.ds(pl.multiple_of(m * QTILE, QTILE), QTILE), :]
    return jnp.concatenate([x * mk for mk in _group_masks(x.dtype)], axis=0)


def _attn_prep(mode, group, s_len, padk, q_ref, k_ref, v_ref, gq_ref, gk_ref, qs, k2, v2, do_ref=None, dos=None):
    m0, m1 = _head_masks()
    zpad = jnp.zeros((padk, k2.shape[1]), BF16)
    k2[pl.ds(0, padk), :] = zpad
    v2[pl.ds(0, padk), :] = zpad
    first_kv = group == 0
    rt = 2 * QTILE
    for i in range(s_len // rt):
        rows = pl.ds(i * rt, rt)
        qn, _, _ = _head_norm(q_ref[rows, :], gq_ref[...], m0, m1)
        kn, _, _ = _head_norm(k_ref[rows, :], gk_ref[...], m0, m1)
        qn = qn * (HEAD_DIM ** -0.5)
        if mode == "A":
            qs[rows, :] = qn.astype(BF16)
            if dos is not None:
                dos[rows, :] = do_ref[rows, :].astype(BF16)
        else:
            _store_stacked(qs, i, _stack_heads(first_kv, qn, m0, m1))
            if dos is not None:
                _store_stacked(dos, i, _stack_heads(first_kv, do_ref[rows, :], m0, m1))
        k2[pl.ds(padk + i * rt, rt), :] = kn.astype(BF16)
        v2[pl.ds(padk + i * rt, rt), :] = v_ref[rows, :].astype(BF16)


def _attn_probs(mode, q_st, kb, bias, ok, sink):
    s = _dot_nt(q_st, kb) + bias
    return _softmax_terms(mode, jnp.where(ok, s, NEG_INF), sink)


def _softmax_terms(mode, s, sink):
    mx = jnp.max(s, axis=-1, keepdims=True)
    if mode == "B":
        mx = jnp.maximum(mx, sink)
    e = jnp.exp(s - mx)
    l = jnp.sum(e, axis=-1, keepdims=True)
    if mode == "B":
        l = l + jnp.exp(sink - mx)
    return e, mx, l


def _sink_column(sink_ref, group):
    row = lax.broadcasted_iota(jnp.int32, (STACK, 1), 0)
    col = jnp.zeros((STACK, 1), F32)
    for h in range(GROUP):
        col = jnp.where((row >= h * QTILE) & (row < (h + 1) * QTILE), sink_ref[GROUP * group + h], col)
    return col


def _head_deltas(dd, m0, m1):
    cols = []
    for half in range(GROUP // 2):
        dh = dd[:, half * LANES:(half + 1) * LANES]
        cols += [jnp.sum(dh * m0, axis=-1, keepdims=True), jnp.sum(dh * m1, axis=-1, keepdims=True)]
    return jnp.concatenate(cols, axis=0)


def _attn_cols(mode):
    if mode == "A":
        return (lambda b, g: (b, g)), (lambda b, g: (b, 2 + g)), (lambda b, g: (b, 4 + g))
    return (lambda b, g: (b, 6 + g)), (lambda b, g: (b, 16)), (lambda b, g: (b, 17))


def _attn_fwd(mode, qkv, gq, gk, bias, sinks, bl, s_len, name):
    bw = bias.shape[-1]
    padk = bw - QTILE
    nt = s_len // QTILE
    qmap, kmap, vmap = _attn_cols(mode)

    kw = _kv_width(mode)

    def body(q_ref, k_ref, v_ref, gq_ref, gk_ref, bias_ref, sink_ref, o_ref, qs, k2, v2, s_buf, *rest):
        o_buf = rest[0] if rest else None
        group = pl.program_id(1)
        m0, m1 = _head_masks()
        first_kv = group == 0
        _attn_prep(mode, group, s_len, padk, q_ref, k_ref, v_ref, gq_ref, gk_ref, qs, k2, v2)
        col = lax.broadcasted_iota(jnp.int32, (STACK, bw), 1)
        sink = _sink_column(sink_ref, group)

        def scores(m, slot):
            r0 = pl.multiple_of(m * QTILE, QTILE)
            s = _dot_nt(_load_stacked(mode, qs, m), k2[pl.ds(r0, bw), :]) + bias_ref[...]
            s_buf[slot] = jnp.where(col >= (padk - r0), s, NEG_INF)

        def finish_tile(m, slot):
            r0 = pl.multiple_of(m * QTILE, QTILE)
            e, _, l = _softmax_terms(mode, s_buf[slot], sink)
            if mode == "A":
                o_st = _dot(e.astype(BF16), v2[pl.ds(r0, bw), :]) / l
                heads = [o_st[h * QTILE:(h + 1) * QTILE] for h in range(GROUP)]
                o_ref[pl.ds(r0, QTILE), :] = _unstack_heads(mode, first_kv, heads, m0, m1)
            else:
                o_buf[pl.ds(pl.multiple_of(m * STACK, STACK), STACK), :] = _dot((e * (1.0 / l)).astype(BF16),
                                                                                 v2[pl.ds(r0, bw), :])

        scores(0, 0)

        def pair(j, carry):
            scores(2 * j + 1, 1)
            finish_tile(2 * j, 0)
            scores(jnp.minimum(2 * j + 2, nt - 1), 0)
            finish_tile(2 * j + 1, 1)
            return carry

        lax.fori_loop(0, nt // 2, pair, 0)
        if mode == "B":
            for m in range(nt):
                heads = [o_buf[pl.ds(m * STACK + h * QTILE, QTILE), :] for h in range(GROUP)]
                o_ref[pl.ds(m * QTILE, QTILE), :] = _unstack_heads(mode, first_kv, heads, m0, m1)

    blk = lambda w, f: pl.BlockSpec((s_len, w), f)
    return pl.pallas_call(
        body, name=name, grid=(bl, B_Q_HEADS // GROUP),
        out_shape=jax.ShapeDtypeStruct((bl * s_len, B_Q_HEADS * HEAD_DIM), F32),
        in_specs=[blk(QW, qmap), blk(kw, kmap), blk(kw, vmap),
                  pl.BlockSpec((1, QW), lambda b, g: (0, 0)), pl.BlockSpec((1, kw), lambda b, g: (0, 0)),
                  pl.BlockSpec((STACK, bw), lambda b, g: (g, 0)),
                  pl.BlockSpec(memory_space=pltpu.SMEM)],
        out_specs=blk(QW, lambda b, g: (b, g)),
        scratch_shapes=[pltpu.VMEM(_q_scratch_shape(mode, s_len), BF16)] + [pltpu.VMEM((s_len + padk, kw), BF16)] * 2
        + [pltpu.VMEM((2, STACK, bw), F32)] + ([pltpu.VMEM((GROUP * s_len, LANES), F32)] if mode == "B" else []),
        compiler_params=_params(("arbitrary", "arbitrary")),
    )(qkv, qkv, qkv, gq, gk, bias.reshape(-1, bw), sinks)


def _attn_bwd(mode, qkv, gq, gk, bias, sinks, y, dy, bl, s_len, name):
    bw = bias.shape[-1]
    padk = bw - QTILE
    nt = s_len // QTILE
    qmap, kmap, vmap = _attn_cols(mode)
    t = bl * s_len
    kw = _kv_width(mode)
    kvw = 4 * LANES if mode == "A" else LANES

    def body(q_ref, k_ref, v_ref, gq_ref, gk_ref, bias_ref, sink_ref, y_ref, dy_ref,
             dq_ref, dk_ref, dv_ref, dgq_ref, dgk_ref, dbias_ref, dsink_ref,
             qs, k2, v2, dos, dqs, dk, dv):
        group = pl.program_id(1)
        m0, m1 = _head_masks()
        first_kv = group == 0
        _attn_prep(mode, group, s_len, padk, q_ref, k_ref, v_ref, gq_ref, gk_ref, qs, k2, v2, dy_ref, dos)
        dk[...] = jnp.zeros_like(dk)
        dv[...] = jnp.zeros_like(dv)
        dbias_ref[...] = jnp.zeros_like(dbias_ref)
        col = lax.broadcasted_iota(jnp.int32, (STACK, bw), 1)
        lane8 = lax.broadcasted_iota(jnp.int32, (8, LANES), 1)
        sink = _sink_column(sink_ref, group)

        def tile(m, dsink):
            r0 = pl.multiple_of(m * QTILE, QTILE)
            rows = pl.ds(r0, QTILE)
            band = pl.ds(r0, bw)
            q_st = _load_stacked(mode, qs, m)
            do_st = _load_stacked(mode, dos, m)
            delta = _head_deltas(dy_ref[rows, :] * y_ref[rows, :], m0, m1)
            ok = col >= (padk - r0)
            kb = k2[band, :]
            e, mx, l = _attn_probs(mode, q_st, kb, bias_ref[...], ok, sink)
            inv = 1.0 / l
            pn = e * inv
            dp = _dot_nt(do_st, v2[band, :])
            ds = pn * (dp - delta)
            if mode == "A":
                dbias_ref[...] += ds
            else:
                part = jnp.exp(sink - mx) * inv * delta
                for h in range(GROUP):
                    dsink = dsink - jnp.where(lane8 == h, jnp.sum(part[h * QTILE:(h + 1) * QTILE]), 0.0)
            dsb = ds.astype(BF16)
            dv[band, :] += _dot_tn(pn.astype(BF16), do_st)
            dk[band, :] += _dot_tn(dsb, q_st)
            dq_st = _dot(dsb, kb)
            if mode == "A":
                heads = [dq_st[h * QTILE:(h + 1) * QTILE] for h in range(GROUP)]
                dqs[rows, :] = _unstack_heads(mode, first_kv, heads, m0, m1)
            else:
                dqs[pl.ds(pl.multiple_of(m * STACK, STACK), STACK), :] = dq_st
            return dsink

        dsink = lax.fori_loop(0, nt, tile, jnp.zeros((8, LANES), F32), unroll=2)
        dsink_ref[...] = dsink

        rt = 2 * QTILE
        dgq = jnp.zeros((1, QW), F32)
        dgk = jnp.zeros((1, kw), F32)
        for i in range(s_len // rt):
            rows = pl.ds(i * rt, rt)
            src = pl.ds(padk + i * rt, rt)
            gq_v, gk_v = gq_ref[...], gk_ref[...]
            _, qh, qr = _head_norm(q_ref[rows, :], gq_v, m0, m1)
            _, kh, kr = _head_norm(k_ref[rows, :], gk_v, m0, m1)
            if mode == "A":
                dqn = dqs[rows, :] * (HEAD_DIM ** -0.5)
            else:
                dqn = jnp.concatenate(
                    [_unstack_heads(mode, first_kv, [dqs[pl.ds((2 * i + half) * STACK + h * QTILE, QTILE), :]
                                                     for h in range(GROUP)], m0, m1)
                     for half in range(2)], axis=0) * (HEAD_DIM ** -0.5)
            dq_raw, dgq_i = _head_norm_bwd(qh, qr, gq_v, dqn, m0, m1)
            dk_raw, dgk_i = _head_norm_bwd(kh, kr, gk_v, dk[src, :], m0, m1)
            dvn = dv[src, :]
            dq_ref[rows, :] = dq_raw
            if mode == "A":
                dk_ref[rows, :] = dk_raw
                dv_ref[rows, :] = dvn
            else:
                @pl.when(group == 0)
                def _():
                    dk_ref[rows, :] = dk_raw
                    dv_ref[rows, :] = dvn

                @pl.when(group != 0)
                def _():
                    dk_ref[rows, :] += dk_raw
                    dv_ref[rows, :] += dvn
            dgq, dgk = dgq + dgq_i, dgk + dgk_i
        dgq_ref[...] = jnp.broadcast_to(dgq, (8, QW))
        dgk_ref[...] = jnp.broadcast_to(dgk, (8, kw))

    ng = B_Q_HEADS // GROUP
    blk = lambda w, f: pl.BlockSpec((s_len, w), f)
    small = lambda w: pl.BlockSpec((None, None, 8, w), lambda b, g: (b, g, 0, 0))
    own = lambda b, g: (b, g)
    kvmap = own if mode == "A" else (lambda b, g: (b, 0))
    pad_f32 = pltpu.VMEM((s_len + padk, kw), F32)
    pad_bf = pltpu.VMEM((s_len + padk, kw), BF16)
    stack_bf = pltpu.VMEM(_q_scratch_shape(mode, s_len), BF16)
    outs = pl.pallas_call(
        body, name=name, grid=(bl, ng),
        out_shape=[jax.ShapeDtypeStruct((t, ng * QW), F32), jax.ShapeDtypeStruct((t, kvw), F32),
                   jax.ShapeDtypeStruct((t, kvw), F32),
                   jax.ShapeDtypeStruct((bl, ng, 8, QW), F32), jax.ShapeDtypeStruct((bl, ng, 8, kw), F32),
                   jax.ShapeDtypeStruct((bl, ng * STACK, bw), F32), jax.ShapeDtypeStruct((bl, ng, 8, LANES), F32)],
        in_specs=[blk(QW, qmap), blk(kw, kmap), blk(kw, vmap),
                  pl.BlockSpec((1, QW), lambda b, g: (0, 0)), pl.BlockSpec((1, kw), lambda b, g: (0, 0)),
                  pl.BlockSpec((STACK, bw), lambda b, g: (g, 0)),
                  pl.BlockSpec(memory_space=pltpu.SMEM),
                  blk(QW, own), blk(QW, own)],
        out_specs=[blk(QW, own), blk(kw, kvmap), blk(kw, kvmap), small(QW), small(kw),
                   pl.BlockSpec((None, STACK, bw), lambda b, g: (b, g, 0)), small(LANES)],
        scratch_shapes=[stack_bf, pad_bf, pad_bf, stack_bf, pltpu.VMEM(_q_scratch_shape(mode, s_len), F32),
                        pad_f32, pad_f32],
        compiler_params=_params(("arbitrary", "arbitrary")),
    )(qkv, qkv, qkv, gq, gk, bias.reshape(-1, bw), sinks, y, dy)
    outs = list(outs)
    outs[5] = outs[5].reshape(bl, B_Q_HEADS, QTILE, bw)
    return outs


def _band_geometry(prev):
    bw = QTILE + prev * CHUNK
    i = np.arange(QTILE)[:, None]
    j = np.arange(bw)[None, :]
    dist = i + prev * CHUNK - j
    valid = (j // CHUNK >= i // CHUNK) & (j // CHUNK <= i // CHUNK + prev)
    return dist, valid


A_VAR0 = (A_PREV * CHUNK - A_MAX_REL) // LANES * LANES


A_NVAR = QTILE + A_PREV * CHUNK - A_VAR0


def _skew_rows(x, sign):
    rows, n = x.shape
    row = lax.broadcasted_iota(jnp.int32, x.shape, 0)
    b = 1
    while b < rows:
        x = jnp.where((row & b) != 0, pltpu.roll(x, (sign * b) % n, 1), x)
        b *= 2
    return x


def _rel_bias_expand(table, name):
    _, valid = _band_geometry(A_PREV)
    bw = valid.shape[1]
    valid_f = jnp.asarray(valid.astype(np.float32))
    rev = jnp.flip(table[:, 1:], axis=1).reshape(A_HEADS, 1, A_NVAR)

    def body(rev_ref, valid_ref, o_ref):
        rowv = jnp.broadcast_to(rev_ref[...], (QTILE, A_NVAR))
        top = rowv[:, 0:1]
        var = _skew_rows(rowv, 1)
        row = lax.broadcasted_iota(jnp.int32, (QTILE, A_NVAR), 0)
        colv = lax.broadcasted_iota(jnp.int32, (QTILE, A_NVAR), 1)
        var = jnp.where(colv < row, top, var)
        ok = valid_ref[...] > 0.5
        o_ref[:, :A_VAR0] = jnp.where(ok[:, :A_VAR0], top, NEG_INF)
        o_ref[:, A_VAR0:] = jnp.where(ok[:, A_VAR0:], var, NEG_INF)

    return pl.pallas_call(
        body, name=name, grid=(A_HEADS,),
        out_shape=jax.ShapeDtypeStruct((A_HEADS, QTILE, bw), F32),
        in_specs=[pl.BlockSpec((None, 1, A_NVAR), lambda h: (h, 0, 0)), pl.BlockSpec((QTILE, bw), lambda h: (0, 0))],
        out_specs=pl.BlockSpec((None, QTILE, bw), lambda h: (h, 0, 0)),
        compiler_params=_params(("arbitrary",)),
    )(rev, valid_f)


def _rel_bias_grad(dbias, name):
    bl = dbias.shape[0]
    bw = dbias.shape[-1]

    def body(db_ref, o_ref):
        g = db_ref[0]
        for b in range(1, bl):
            g = g + db_ref[b]
        sk = _skew_rows(g[:, A_VAR0:], -1)
        row = lax.broadcasted_iota(jnp.int32, (QTILE, A_NVAR), 0)
        colv = lax.broadcasted_iota(jnp.int32, (QTILE, A_NVAR), 1)
        wrapped = (row + colv) >= A_NVAR
        main = jnp.sum(jnp.where(wrapped, 0.0, sk), axis=0, keepdims=True)
        top = jnp.sum(g[:, :A_VAR0]) + jnp.sum(jnp.where(wrapped, sk, 0.0))
        o_ref[:, :A_NVAR] = jnp.broadcast_to(main, (8, A_NVAR))
        o_ref[:, A_NVAR:] = jnp.full((8, LANES), top, F32)

    out = pl.pallas_call(
        body, name=name, grid=(A_HEADS,),
        out_shape=jax.ShapeDtypeStruct((A_HEADS, 8, A_NVAR + LANES), F32),
        in_specs=[pl.BlockSpec((bl, None, QTILE, bw), lambda h: (0, h, 0, 0))],
        out_specs=pl.BlockSpec((None, 8, A_NVAR + LANES), lambda h: (h, 0, 0)),
        compiler_params=_params(("arbitrary",)),
    )(dbias)
    main, top = out[:, 0, :A_NVAR], out[:, 0, A_NVAR]
    fm = jnp.flip(main, axis=1)
    return jnp.concatenate([jnp.zeros((A_HEADS, 1), F32), fm[:, :-1], fm[:, -1:] + top[:, None]], axis=1)


def _alibi_bias():
    dist, valid = _band_geometry(B_PREV)
    slopes = np.array([2.0 ** (-8.0 * (h + 1) / B_Q_HEADS) for h in range(B_Q_HEADS)], dtype=np.float32)
    bias = -slopes[:, None, None] * np.abs(dist).astype(np.float32)[None]
    return jnp.asarray(np.where(valid[None], bias, np.float32(NEG_INF)).astype(np.float32))


SMALL_NAMES = ("ffn1_norm", "mix_norm", "ffn2_norm", "ple_norm", "a_q_norm", "a_k_norm", "b_q_norm", "b_k_norm",
               "a_rel_bias", "b_sinks", "loss")


def _pack_small(vals):
    rows = []
    for nme in SMALL_NAMES:
        v = vals[nme].astype(F32)
        if nme == "a_rel_bias":
            v = jnp.pad(v.reshape(A_HEADS, -1), ((0, 0), (0, 3 * LANES - (2 * A_MAX_REL + 1))))
        v = v.reshape(-1)
        v = jnp.pad(v, (0, (-v.shape[0]) % LANES))
        rows.append(v.reshape(-1, LANES))
    out = jnp.concatenate(rows, axis=0)
    return jnp.pad(out, ((0, (-out.shape[0]) % 8), (0, 0)))


def _unpack_small(packed, shapes):
    out, r = {}, 0
    for nme in SMALL_NAMES:
        shp = shapes[nme]
        if nme == "a_rel_bias":
            nr = A_HEADS * 3
            out[nme] = packed[r:r + nr].reshape(A_HEADS, 3 * LANES)[:, :2 * A_MAX_REL + 1].reshape(shp)
        else:
            size = int(np.prod(shp)) if shp else 1
            nr = -(-size // LANES)
            out[nme] = packed[r:r + nr].reshape(-1)[:size].reshape(shp)
        r += nr
    return out


BIG_NAMES = ("ffn1_w_gu", "ffn1_w_down", "w_in", "w_gate", "w_proj_a", "w_proj_b", "w_out",
             "ffn2_w_gu", "ffn2_w_down", "w_ple_gate", "w_ple_proj")
ROW_SHARDED = ("ffn1_w_down", "ffn2_w_down", "w_out", "w_ple_gate")
WEIGHT_ORDER = ("ffn1_norm", "ffn1_w_gu", "ffn1_w_down", "mix_norm", "w_in", "a_q_norm", "a_k_norm", "a_rel_bias",
                "b_q_norm", "b_k_norm", "b_sinks", "w_gate", "w_proj_a", "w_proj_b", "w_out", "ffn2_norm",
                "ffn2_w_gu", "ffn2_w_down", "ple_norm", "w_ple_gate", "w_ple_proj")


TRANSPOSED = ("ffn1_w_gu", "ffn2_w_gu", "w_in")


def _local(a, nme):
    return a[0].T if nme in TRANSPOSED else a[0]


def _full_cols(wg):
    nb, k, n = wg.shape
    return jnp.transpose(wg, (1, 0, 2)).reshape(k, nb * n)


def _col_blocks(g, nb):
    k, n = g.shape
    return jnp.transpose(g.reshape(k, nb, n // nb), (1, 0, 2))


def _step(x, p, target, w, m, v):
    bl, s_len, d = x.shape
    t = bl * s_len
    h0 = x.reshape(t, d)
    pt = p.reshape(t, p.shape[-1])
    tgt = target.reshape(t, d)

    g_ffn1, g_mix, g_ffn2, g_ple = w["ffn1_norm"], w["mix_norm"], w["ffn2_norm"], w["ple_norm"]
    tiled = lambda a, width: jnp.tile(a.reshape(1, HEAD_DIM), (1, width // HEAD_DIM))
    gqa, gka = tiled(w["a_q_norm"], QW), tiled(w["a_k_norm"], _kv_width("A"))
    gqb, gkb = tiled(w["b_q_norm"], QW), tiled(w["b_k_norm"], _kv_width("B"))
    sinks = w["b_sinks"].reshape(B_Q_HEADS)
    bias_a = _rel_bias_expand(w["a_rel_bias"][0], "rel_bias_expand")
    bias_b = _alibi_bias()

    shard = {nme: _local(w[nme], nme).astype(BF16) for nme in BIG_NAMES}
    wgu1, wd1 = _all_gather([shard["ffn1_w_gu"], shard["ffn1_w_down"]], "weights_gather_ffn1")
    nf = wgu1.shape[1]
    wd1 = wd1.reshape(N_DEV // 2, nf, d)
    mixer_names = ("w_in", "w_gate")
    rest_names = ("w_proj_a", "w_proj_b", "w_out", "ffn2_w_gu", "ffn2_w_down", "w_ple_gate", "w_ple_proj")
    send1, recv1, bufs, token = _gather_start([shard[nme] for nme in mixer_names], wgu1, "gather_start_mixer")
    rsend1, rrecv1, rest_bufs, token = _gather_start([shard[nme] for nme in rest_names], token, "gather_start_rest")

    h1, gu1 = _ffn_fwd(h0, g_ffn1 + token[0, 0], wgu1, wd1, "ffn1_fwd")
    send2, recv2, bufs, token = _gather_pass(send1, recv1, bufs, h1, "gather_pass_mixer")
    win, wgate = _gather_wait(send2, recv2, bufs, token, "gather_wait_mixer")
    win, wgate = win.reshape(IN_COLS, d), _full_cols(wgate)
    un, qkv, gate = _proj_fwd(h1, g_mix, win, wgate, "proj_fwd")
    ya = _attn_fwd("A", qkv, gqa, gka, bias_a, sinks, bl, s_len, "attn_a_fwd")
    rsend2, rrecv2, rest_bufs, token = _gather_pass(rsend1, rrecv1, rest_bufs, ya, "gather_pass_rest")
    yb = _attn_fwd("B", qkv, gqb + token[0, 0], gkb, bias_b, sinks, bl, s_len, "attn_b_fwd")
    gathered = dict(zip(rest_names, _gather_wait(rsend2, rrecv2, rest_bufs, yb, "gather_wait_rest")))
    wgu2 = gathered["ffn2_w_gu"]
    wd2 = gathered["ffn2_w_down"].reshape(N_DEV // 2, nf, d)
    wpa = _full_cols(gathered["w_proj_a"])
    wpb = _full_cols(gathered["w_proj_b"])
    wpe = _full_cols(gathered["w_ple_proj"])
    wout = gathered["w_out"].reshape(d, d)
    wpg = gathered["w_ple_gate"].reshape(d, d)
    h2, merged, pa, pb = _merge_fwd(h1, ya, yb, gate, wpa, wpb, wout, "merge_fwd")
    h3, gu2 = _ffn_fwd(h2, g_ffn2, wgu2, wd2, "ffn2_fwd")
    dh3, dz4, dpp, n4, dg_ple, loss_part = _ple_loss(h3, g_ple, pt, tgt, wpg, wpe, "ple_loss")

    xi, yi, ci = _place()
    me = jnp.stack([4 * xi + 2 * yi + ci]).astype(jnp.int32)
    g32, g16, big = {}, {}, {}

    def keep(nme, pair, rows=None):
        for store, g in zip((g32, g16), pair):
            store[nme] = g if rows is None else g.reshape(N_DEV, rows, d)

    def start(names, after, tag):
        send, recv, parts, lands, token = _scatter_start([g16[nme] for nme in names], after, "grads_start_" + tag)
        return names, send, recv, parts, lands, token

    def finish(state, after, tag):
        names, send, recv, parts, lands, _ = state
        lands = _scatter_wait(send, recv, parts, lands, after, "grads_wait_" + tag)
        return names, lands

    def adam(done, dep):
        for nme, land in zip(*done):
            outs = _final_adam(g32[nme], land, _local(w[nme], nme), _local(m[nme], nme), _local(v[nme], nme), me, dep,
                               "adam_" + nme)
            big[nme] = [(o.T if nme in TRANSPOSED else o)[None] for o in outs]

    keep("w_ple_gate", _dw(n4, dz4, 1, d, "dw_ple_gate"), d // N_DEV)
    keep("w_ple_proj", _dw(pt, dpp, N_DEV, d // N_DEV, "dw_ple_proj"))
    early = [(start(("w_ple_gate", "w_ple_proj"), dh3, "ple"), "ple")]

    dh2, dgu2, a2, n3, dg_ffn2 = _ffn_bwd(dh3, h2, g_ffn2 + early[-1][0][-1][0, 0], gu2, wgu2, wd2, "ffn2_bwd")
    keep("ffn2_w_down", _dw(a2, dh3, N_DEV // 2, d, "dw_ffn2_down", 0.5), nf // 2)
    early.append((start(("ffn2_w_down",), dh2, "ffn2_down"), "ffn2_down"))
    keep("ffn2_w_gu", _dw(dgu2, n3, N_DEV, d, "dw_ffn2_gu", dep=early[-1][0][-1]))
    flight = start(("ffn2_w_gu",), dh2, "ffn2")

    dpa, dpb, dzg, dya, dyb = _merge_bwd(dh2, pa, pb, gate, wpa, wpb, wout, "merge_bwd")
    keep("w_out", _dw(merged, dh2, 1, d, "dw_out"), d // N_DEV)
    keep("w_proj_a", _dw(ya, dpa, N_DEV, d // N_DEV, "dw_proj_a"))
    keep("w_proj_b", _dw(yb, dpb, N_DEV, d // N_DEV, "dw_proj_b"))
    keep("w_gate", _dw(un, dzg, N_DEV, 2 * d // N_DEV, "dw_gate"))

    tok = flight[-1][0, 0]
    dqa, dka, dva, dgqa, dgka, dbias, _ = _attn_bwd("A", qkv, gqa + tok, gka, bias_a, sinks, ya, dya, bl, s_len,
                                                     "attn_a_bwd")
    dqb, dkb, dvb, dgqb, dgkb, _, dsink = _attn_bwd("B", qkv, gqb, gkb, bias_b, sinks, yb, dyb, bl, s_len, "attn_b_bwd")
    dqkv = [dqa, dka, dva, dqb, dkb, dvb]
    dtab = _rel_bias_grad(dbias, "rel_bias_grad")

    dh1, dg_mix = _proj_bwd(dh2, h1, g_mix, dzg, dqkv, win, wgate, "proj_bwd")
    keep("w_in", _dw_rows(dqkv, un, "dw_in"), IN_COLS // N_DEV)
    waiting = [finish(state, g32["w_in"], tag) for state, tag in early]
    done = finish(flight, waiting[-1][1][0], "ffn2")
    flight = start(("w_out", "w_proj_a", "w_proj_b", "w_gate", "w_in"), done[1][0], "mixer")
    waiting.append(done)

    dh0, dgu1, a1, n1, dg_ffn1 = _ffn_bwd(dh1, h0, g_ffn1 + flight[-1][0, 0], gu1, wgu1, wd1, "ffn1_bwd")
    keep("ffn1_w_down", _dw(a1, dh1, N_DEV // 2, d, "dw_ffn1_down", 0.5), nf // 2)
    done = finish(flight, g32["ffn1_w_down"], "mixer")
    flight = start(("ffn1_w_down",), done[1][0], "ffn1_down")
    waiting.append(done)

    keep("ffn1_w_gu", _dw(dgu1, n1, N_DEV, d, "dw_ffn1_gu", dep=flight[-1]))
    done = finish(flight, g32["ffn1_w_gu"], "ffn1_down")
    flight = start(("ffn1_w_gu",), done[1][0], "ffn1_gu")
    for group in waiting + [done]:
        adam(group, flight[-1])
    behind = 0.0 * big["ffn1_w_down"][0][0, 0, :1]
    smalls = (dg_ffn1, dg_mix, dg_ffn2, dg_ple + behind, dgqa, dgka, dgqb, dgkb, dtab, dsink)
    return dh0, loss_part, big, smalls, flight, finish, adam


def kernel(x, p, ffn1_norm, ffn1_w_gu, ffn1_w_down, mix_norm, w_in, a_q_norm, a_k_norm, a_rel_bias, b_q_norm, b_k_norm, b_sinks, w_gate, w_proj_a, w_proj_b, w_out, ffn2_norm, ffn2_w_gu, ffn2_w_down, ple_norm, w_ple_gate, w_ple_proj, loss_target, m_ffn1_norm, m_ffn1_w_gu, m_ffn1_w_down, m_mix_norm, m_w_in, m_a_q_norm, m_a_k_norm, m_a_rel_bias, m_b_q_norm, m_b_k_norm, m_b_sinks, m_w_gate, m_w_proj_a, m_w_proj_b, m_w_out, m_ffn2_norm, m_ffn2_w_gu, m_ffn2_w_down, m_ple_norm, m_w_ple_gate, m_w_ple_proj, v_ffn1_norm, v_ffn1_w_gu, v_ffn1_w_down, v_mix_norm, v_w_in, v_a_q_norm, v_a_k_norm, v_a_rel_bias, v_b_q_norm, v_b_k_norm, v_b_sinks, v_w_gate, v_w_proj_a, v_w_proj_b, v_w_out, v_ffn2_norm, v_ffn2_w_gu, v_ffn2_w_down, v_ple_norm, v_w_ple_gate, v_w_ple_proj):
    w = dict(ffn1_norm=ffn1_norm, ffn1_w_gu=ffn1_w_gu, ffn1_w_down=ffn1_w_down, mix_norm=mix_norm, w_in=w_in,
             a_q_norm=a_q_norm, a_k_norm=a_k_norm, a_rel_bias=a_rel_bias, b_q_norm=b_q_norm, b_k_norm=b_k_norm,
             b_sinks=b_sinks, w_gate=w_gate, w_proj_a=w_proj_a, w_proj_b=w_proj_b, w_out=w_out, ffn2_norm=ffn2_norm,
             ffn2_w_gu=ffn2_w_gu, ffn2_w_down=ffn2_w_down, ple_norm=ple_norm, w_ple_gate=w_ple_gate,
             w_ple_proj=w_ple_proj)
    m = dict(ffn1_norm=m_ffn1_norm, ffn1_w_gu=m_ffn1_w_gu, ffn1_w_down=m_ffn1_w_down, mix_norm=m_mix_norm,
             w_in=m_w_in, a_q_norm=m_a_q_norm, a_k_norm=m_a_k_norm, a_rel_bias=m_a_rel_bias, b_q_norm=m_b_q_norm,
             b_k_norm=m_b_k_norm, b_sinks=m_b_sinks, w_gate=m_w_gate, w_proj_a=m_w_proj_a, w_proj_b=m_w_proj_b,
             w_out=m_w_out, ffn2_norm=m_ffn2_norm, ffn2_w_gu=m_ffn2_w_gu, ffn2_w_down=m_ffn2_w_down,
             ple_norm=m_ple_norm, w_ple_gate=m_w_ple_gate, w_ple_proj=m_w_ple_proj)
    v = dict(ffn1_norm=v_ffn1_norm, ffn1_w_gu=v_ffn1_w_gu, ffn1_w_down=v_ffn1_w_down, mix_norm=v_mix_norm,
             w_in=v_w_in, a_q_norm=v_a_q_norm, a_k_norm=v_a_k_norm, a_rel_bias=v_a_rel_bias, b_q_norm=v_b_q_norm,
             b_k_norm=v_b_k_norm, b_sinks=v_b_sinks, w_gate=v_w_gate, w_proj_a=v_w_proj_a, w_proj_b=v_w_proj_b,
             w_out=v_w_out, ffn2_norm=v_ffn2_norm, ffn2_w_gu=v_ffn2_w_gu, ffn2_w_down=v_ffn2_w_down,
             ple_norm=v_ple_norm, w_ple_gate=v_w_ple_gate, w_ple_proj=v_w_ple_proj)
    bl, s_len, d = x.shape

    dh0, loss_part, big, smalls, flight, finish, adam = _step(x, p[0], loss_target, w, m, v)
    dg_ffn1, dg_mix, dg_ffn2, dg_ple, dgqa, dgka, dgqb, dgkb, dtab, dsink = smalls

    fold = lambda a: a[:, :, 0, :].reshape(-1, HEAD_DIM).sum(axis=0)
    small_part = dict(
        ffn1_norm=dg_ffn1, mix_norm=dg_mix, ffn2_norm=dg_ffn2, ple_norm=dg_ple,
        a_q_norm=fold(dgqa), a_k_norm=fold(dgka), b_q_norm=fold(dgqb), b_k_norm=fold(dgkb),
        a_rel_bias=dtab,
        b_sinks=dsink.sum(axis=0)[:, 0, :GROUP].reshape(B_Q_HEADS),
        loss=loss_part[0, :1])
    zero1 = jnp.zeros((1,), F32)
    shapes = {nme: w[nme].shape for nme in SMALL_NAMES if nme != "loss"}
    shapes["loss"] = ()
    pk = lambda src: _pack_small({**{nme: src[nme] for nme in SMALL_NAMES if nme != "loss"}, "loss": zero1})
    sg, sd, sm, sv = _small_allreduce_adam(_pack_small(small_part), pk(w), pk(m), pk(v), "small_allreduce_adam")
    adam(finish(flight, sg, "ffn1_gu"), sg)
    sg, sd, sm, sv = (_unpack_small(a, shapes) for a in (sg, sd, sm, sv))

    def pick(i):
        out = []
        for nme in WEIGHT_ORDER:
            out.append(big[nme][i] if nme in big else (sg, sd, sm, sv)[i][nme])
        return out

    return (sg["loss"], dh0.reshape(bl, s_len, d), *pick(0), *pick(1), *pick(2), *pick(3))
```

```python
import functools

import jax
import jax.numpy as jnp
import numpy as np
from jax import lax
from jax.experimental import pallas as pl
from jax.experimental.pallas import tpu as pltpu

F32 = jnp.float32
BF16 = jnp.bfloat16

CHUNK = 64
HEAD_DIM = 64
A_HEADS = 8
A_PREV = 8
A_MAX_REL = 128
B_Q_HEADS = 8
B_KV_HEADS = 2
B_PREV = 2
A_WIDTH = A_HEADS * HEAD_DIM
B_Q_WIDTH = B_Q_HEADS * HEAD_DIM
B_KV_WIDTH = B_KV_HEADS * HEAD_DIM
IN_COLS = 3 * A_WIDTH + B_Q_WIDTH + 2 * B_KV_WIDTH
EPS = 1e-6
NEG_INF = -1e30
ADAM_LR = 0.001
ADAM_B1 = 0.9
ADAM_B2 = 0.999
ADAM_EPS = 1e-08
ADAM_WD = 0.01
ADAM_STEP = 10

N_DEV = 8
LANES = 128
QTILE = 2 * CHUNK
VMEM_LIMIT = 56 * 1024 * 1024
ADAM_TILE_ELEMS = 256 * 1024

MESH_ID = pl.DeviceIdType.MESH
ANY = pl.BlockSpec(memory_space=pl.ANY)
HBM = pl.BlockSpec(memory_space=pltpu.HBM)
SEM = pl.BlockSpec(memory_space=pltpu.SEMAPHORE)
SIDE_EFFECT = pltpu.SideEffectType.DATAFLOW_SIDE_EFFECTING


def _dot(a, b):
    return jnp.dot(a, b, preferred_element_type=F32)


def _dot_nt(a, b):
    return lax.dot_general(a, b, (((1,), (1,)), ((), ())), preferred_element_type=F32)


def _dot_tn(a, b):
    return lax.dot_general(a, b, (((0,), (0,)), ((), ())), preferred_element_type=F32)


def _params(sem=None, vmem=VMEM_LIMIT):
    return pltpu.CompilerParams(dimension_semantics=sem, vmem_limit_bytes=vmem)


def _row_tile(t, want):
    while t % want:
        want //= 2
    return want


def _place():
    return lax.axis_index("x"), lax.axis_index("y"), lax.axis_index("c")


def _all_gather(shards, name):
    n = len(shards)

    def body(*refs):
        ins, outs = refs[:n], refs[n:2 * n]
        send_sems, recv_sems, local_sems = refs[2 * n:]
        x, y, c = _place()
        me, sib = (x, y, c), (x, y, 1 - c)
        chips = [(1 - x, y), (x, 1 - y), (1 - x, 1 - y)]

        def copy(w, k, block, to, src=None):
            px, py, pc = block
            dst = outs[w].at[4 * px + 2 * py + pc]
            return pltpu.make_async_remote_copy(
                src_ref=dst if src is None else src, dst_ref=dst,
                send_sem=send_sems.at[w * 7 + k], recv_sem=recv_sems.at[w * 7 + k],
                device_id=to, device_id_type=MESH_ID)

        mine = [pltpu.make_async_copy(ins[w], outs[w].at[4 * x + 2 * y + c], local_sems.at[w]) for w in range(n)]
        for cp in mine:
            cp.start()
        first = []
        for w in range(n):
            first.append(copy(w, 0, me, sib, src=ins[w]))
            first += [copy(w, 1 + j, me, (*chip, c), src=ins[w]) for j, chip in enumerate(chips)]
        for cp in first:
            cp.start()
        passed = []
        for j, chip in enumerate(chips):
            for w in range(n):
                copy(w, 1 + j, (*chip, c), me).wait_recv()
                fwd = copy(w, 4 + j, (*chip, c), sib)
                fwd.start()
                passed.append(fwd)
        for w in range(n):
            copy(w, 0, sib, me).wait_recv()
        for j, chip in enumerate(chips):
            for w in range(n):
                copy(w, 4 + j, (*chip, 1 - c), me).wait_recv()
        for cp in first + passed:
            cp.wait_send()
        for cp in mine:
            cp.wait()

    return pl.pallas_call(
        body, name=name,
        out_shape=[jax.ShapeDtypeStruct((N_DEV,) + s.shape, s.dtype) for s in shards],
        in_specs=[ANY] * n, out_specs=[ANY] * n,
        scratch_shapes=[pltpu.SemaphoreType.DMA((7 * n,)), pltpu.SemaphoreType.DMA((7 * n,)),
                        pltpu.SemaphoreType.DMA((n,))],
    )(*shards)


def _gather_level(bufs, send_sems, recv_sems, level, shards=None):
    x, y, c = _place()
    me, sib = (x, y, c), (x, y, 1 - c)
    chips = [(1 - x, y), (x, 1 - y), (1 - x, 1 - y)]

    def copy(w, k, block, to):
        px, py, pc = block
        rows = bufs[w].at[4 * px + 2 * py + pc]
        src = shards[w] if shards is not None and block is me else rows
        return pltpu.make_async_remote_copy(src_ref=src, dst_ref=rows, send_sem=send_sems.at[k], recv_sem=recv_sems.at[k],
                                            device_id=to, device_id_type=MESH_ID)

    n = len(bufs)
    own = []
    if level == 1:
        own = [pltpu.make_async_copy(bufs[w].at[4 * x + 2 * y + c] if shards is None else shards[w],
                                     bufs[w].at[4 * x + 2 * y + c], send_sems.at[4 * n + w]) for w in range(n)]
    out, arriving = [], []
    for w in range(len(bufs)):
        if level == 1:
            out.append(copy(w, 4 * w, me, sib))
            arriving.append(copy(w, 4 * w, sib, me))
        for j, chip in enumerate(chips):
            if level == 1:
                out.append(copy(w, 4 * w + 1 + j, me, (*chip, c)))
                arriving.append(copy(w, 4 * w + 1 + j, (*chip, c), me))
            else:
                out.append(copy(w, 3 * w + j, (*chip, c), sib))
                arriving.append(copy(w, 3 * w + j, (*chip, 1 - c), me))
    return out, arriving, own


def _split_call(body, name, bufs, sems_in, after, n_sems_out, token, extra=()):
    n = len(bufs)
    out_shape = [pltpu.SemaphoreType.DMA((n_sems_out,))] * (2 if n_sems_out else 0)
    out_shape += [pltpu.HBM(a.shape, a.dtype) for a in bufs]
    out_specs = [SEM] * (2 if n_sems_out else 0) + [HBM] * n
    if token:
        out_shape.append(jax.ShapeDtypeStruct((8, LANES), F32))
        out_specs.append(pl.BlockSpec(memory_space=pltpu.VMEM))
    first = 2 if n_sems_out else 0
    return pl.pallas_call(
        body, name=name, out_shape=tuple(out_shape),
        in_specs=[HBM] * (n + len(extra)) + [SEM] * len(sems_in) + [ANY], out_specs=tuple(out_specs),
        input_output_aliases={i: first + i for i in range(n)},
        compiler_params=pltpu.CompilerParams(has_side_effects=SIDE_EFFECT),
    )(*bufs, *extra, *sems_in, after)


def _gather_start(shards, after, name):
    n = len(shards)
    hbm = lambda a: pltpu.with_memory_space_constraint(a, pltpu.HBM)
    bufs = [hbm(lax.empty((N_DEV,) + s.shape, s.dtype)) for s in shards]

    def body(*refs):
        out, _, own = _gather_level(refs[:n], refs[2 * n + 1], refs[2 * n + 2], 1, shards=refs[n:2 * n])
        for cp in own + out:
            cp.start()
        refs[-1][...] = jnp.zeros_like(refs[-1])

    outs = _split_call(body, name, bufs + [hbm(s) for s in shards], [], after, 5 * n, True)
    return outs[0], outs[1], list(outs[2:2 + 2 * n]), outs[-1]


def _gather_pass(send1, recv1, bufs_and_shards, after, name):
    n = len(bufs_and_shards) // 2
    bufs = bufs_and_shards

    def body(*refs):
        refs = refs[:n] + refs[2 * n:]
        out1, in1, own = _gather_level(refs[:n], refs[n], refs[n + 1], 1)
        out2, _, _ = _gather_level(refs[:n], refs[n + 3], refs[n + 4], 2)
        for cp in in1:
            cp.wait_recv()
        for cp in out2:
            cp.start()
        for cp in out1:
            cp.wait_send()
        for cp in own:
            cp.wait()
        refs[-1][...] = jnp.zeros_like(refs[-1])

    outs = _split_call(body, name, bufs, [send1, recv1], after, 3 * n, True)
    return outs[0], outs[1], list(outs[2:2 + n]), outs[-1]


def _gather_wait(send2, recv2, bufs, after, name):
    n = len(bufs)

    def body(*refs):
        out2, in2, _ = _gather_level(refs[:n], refs[n], refs[n + 1], 2)
        for cp in in2:
            cp.wait_recv()
        for cp in out2:
            cp.wait_send()

    return list(_split_call(body, name, bufs, [send2, recv2], after, 0, False))


def _scatter_copies(parts, lands, send_sems, recv_sems):
    x, y, c = _place()
    cps = []
    for w, (part, land) in enumerate(zip(parts, lands)):
        for k in range(1, N_DEV):
            px, py, pc = x ^ ((k >> 2) & 1), y ^ ((k >> 1) & 1), c ^ (k & 1)
            cps.append(pltpu.make_async_remote_copy(
                src_ref=part.at[4 * px + 2 * py + pc], dst_ref=land.at[k - 1],
                send_sem=send_sems.at[7 * w + k - 1], recv_sem=recv_sems.at[7 * w + k - 1],
                device_id=(px, py, pc), device_id_type=MESH_ID))
    return cps


def _scatter_start(parts, after, name):
    n = len(parts)

    def body(*refs):
        ins, lands = refs[:n], refs[n:2 * n]
        send_sems, recv_sems = refs[2 * n + 1], refs[2 * n + 2]
        token = refs[-1]
        for cp in _scatter_copies(ins, lands, send_sems, recv_sems):
            cp.start()
        token[...] = jnp.zeros_like(token)

    land_shapes = [(N_DEV - 1,) + p.shape[1:] for p in parts]
    in_hbm = [pltpu.with_memory_space_constraint(p, pltpu.HBM) for p in parts]
    in_hbm += [pltpu.with_memory_space_constraint(lax.empty(s, p.dtype), pltpu.HBM) for s, p in zip(land_shapes, parts)]
    outs = pl.pallas_call(
        body, name=name,
        out_shape=(pltpu.SemaphoreType.DMA((7 * n,)), pltpu.SemaphoreType.DMA((7 * n,)),
                   *[pltpu.HBM(p.shape, p.dtype) for p in parts],
                   *[pltpu.HBM(s, p.dtype) for s, p in zip(land_shapes, parts)],
                   jax.ShapeDtypeStruct((8, LANES), F32)),
        in_specs=[HBM] * (2 * n) + [ANY],
        out_specs=(SEM, SEM, *[HBM] * (2 * n), pl.BlockSpec(memory_space=pltpu.VMEM)),
        input_output_aliases={i: 2 + i for i in range(2 * n)},
        compiler_params=pltpu.CompilerParams(has_side_effects=SIDE_EFFECT),
    )(*in_hbm, after)
    return outs[0], outs[1], list(outs[2:2 + n]), list(outs[2 + n:2 + 2 * n]), outs[-1]


def _scatter_wait(send_sems, recv_sems, parts, lands, after, name):
    n = len(parts)

    def body(*refs):
        ins, lnd = refs[:n], refs[n:2 * n]
        for cp in _scatter_copies(ins, lnd, refs[2 * n], refs[2 * n + 1]):
            cp.wait_send()
            cp.wait_recv()

    outs = pl.pallas_call(
        body, name=name,
        out_shape=tuple(pltpu.HBM(a.shape, a.dtype) for a in parts + lands),
        in_specs=[HBM] * (2 * n) + [SEM, SEM, ANY],
        out_specs=tuple([HBM] * (2 * n)),
        input_output_aliases={i: i for i in range(2 * n)},
        compiler_params=pltpu.CompilerParams(has_side_effects=SIDE_EFFECT),
    )(*parts, *lands, send_sems, recv_sems, after)
    return list(outs[n:])


def _adam(w, g, m, v):
    m2 = ADAM_B1 * m + (1.0 - ADAM_B1) * g
    v2 = ADAM_B2 * v + (1.0 - ADAM_B2) * (g * g)
    m_hat = m2 / (1.0 - ADAM_B1 ** ADAM_STEP)
    v_hat = v2 / (1.0 - ADAM_B2 ** ADAM_STEP)
    delta = -ADAM_LR * (m_hat / (jnp.sqrt(v_hat) + ADAM_EPS) + ADAM_WD * w)
    return delta, m2, v2


def _small_allreduce_adam(part, w, m, v, name):
    rows = part.shape[0]

    def body(p_ref, w_ref, m_ref, v_ref, g_ref, d_ref, mo_ref, vo_ref, buf, send_sems, recv_sems):
        x, y, c = _place()
        buf[0] = p_ref[...]
        cps = []
        for k in range(1, N_DEV):
            kx, ky, kc = (k >> 2) & 1, (k >> 1) & 1, k & 1
            peer = (x ^ kx, y ^ ky, c ^ kc)
            cps.append(pltpu.make_async_remote_copy(
                src_ref=p_ref, dst_ref=buf.at[k], send_sem=send_sems.at[k - 1], recv_sem=recv_sems.at[k - 1],
                device_id=peer, device_id_type=MESH_ID))
        for cp in cps:
            cp.start()
        for cp in cps:
            cp.wait()
        me = 4 * x + 2 * y + c
        total = buf[me]
        for d in range(1, N_DEV):
            total = total + buf[d ^ me]
        g_ref[...] = total
        delta, m2, v2 = _adam(w_ref[...], total, m_ref[...], v_ref[...])
        d_ref[...] = delta
        mo_ref[...] = m2
        vo_ref[...] = v2

    vm = pl.BlockSpec(memory_space=pltpu.VMEM)
    return pl.pallas_call(
        body, name=name,
        out_shape=[jax.ShapeDtypeStruct(part.shape, F32)] * 4,
        in_specs=[vm] * 4, out_specs=[vm] * 4,
        scratch_shapes=[pltpu.VMEM((N_DEV, rows, LANES), F32),
                        pltpu.SemaphoreType.DMA((N_DEV - 1,)), pltpu.SemaphoreType.DMA((N_DEV - 1,))],
    )(part, w, m, v)


def _final_adam(g8, land, w, m, v, me, dep, name):
    _, r, c = g8.shape
    tr = max(q for q in range(16, r + 1, 16) if r % q == 0 and q * c <= ADAM_TILE_ELEMS)

    def body(me_ref, g_ref, land_ref, w_ref, m_ref, v_ref, dep_ref, go_ref, d_ref, mo_ref, vo_ref):
        del dep_ref
        g = g_ref[...]
        for k in range(N_DEV - 1):
            g = g + land_ref[k].astype(F32)
        go_ref[...] = g
        delta, m2, v2 = _adam(w_ref[...], g, m_ref[...], v_ref[...])
        d_ref[...] = delta
        mo_ref[...] = m2
        vo_ref[...] = v2

    plain = pl.BlockSpec((tr, c), lambda i, s: (i, 0))
    return pl.pallas_call(
        body, name=name,
        out_shape=[jax.ShapeDtypeStruct((r, c), F32)] * 4,
        grid_spec=pltpu.PrefetchScalarGridSpec(
            num_scalar_prefetch=1, grid=(r // tr,),
            in_specs=[pl.BlockSpec((None, tr, c), lambda i, s: (s[0], i, 0)),
                      pl.BlockSpec((N_DEV - 1, tr, c), lambda i, s: (0, i, 0)),
                      plain, plain, plain, ANY],
            out_specs=[plain] * 4),
        compiler_params=_params(("arbitrary",)),
    )(me, g8, land, w, m, v, dep)


def _rms(x, gain):
    r = lax.rsqrt(jnp.mean(x * x, axis=-1, keepdims=True) + EPS)
    xh = x * r
    return xh * gain, xh, r


def _rms_bwd(xh, r, gain, dy):
    gdy = gain * dy
    dx = r * (gdy - xh * jnp.mean(xh * gdy, axis=-1, keepdims=True))
    return dx, jnp.sum(dy * xh, axis=0, keepdims=True)


def _load_weights(pairs, sems):
    cps = [pltpu.make_async_copy(src, dst, sems.at[i]) for i, (src, dst) in enumerate(pairs)]
    for cp in cps:
        cp.start()
    for cp in cps:
        cp.wait()


def _ffn_fwd(h, gain, wgu, wd, name):
    t, d = h.shape
    nb, nf, _ = wgu.shape
    nh = nb // 2
    tm = _row_tile(t, 512)

    def body(h_ref, g_ref, wgu_hbm, wd_hbm, out_ref, gu_ref, wgu_v, wd_v, sems):
        @pl.when(pl.program_id(0) == 0)
        def _():
            _load_weights([(wgu_hbm, wgu_v), (wd_hbm, wd_v)], sems)

        x = h_ref[...]
        n, _, _ = _rms(x, g_ref[...])
        nbf = n.astype(BF16)
        acc = jnp.zeros((tm, d), F32)
        for j in range(nh):
            g = _dot_nt(nbf, wgu_v[j])
            u = _dot_nt(nbf, wgu_v[j + nh])
            gu_ref[j] = g.astype(BF16)
            gu_ref[j + nh] = u.astype(BF16)
            a = (g * jax.nn.sigmoid(g)) * u
            acc = acc + _dot(a.astype(BF16), wd_v[j])
        out_ref[...] = x + 0.5 * acc

    return pl.pallas_call(
        body, name=name, grid=(t // tm,),
        out_shape=[jax.ShapeDtypeStruct((t, d), F32), jax.ShapeDtypeStruct((nb, t, nf), BF16)],
        in_specs=[pl.BlockSpec((tm, d), lambda i: (i, 0)), pl.BlockSpec((1, d), lambda i: (0, 0)), ANY, ANY],
        out_specs=[pl.BlockSpec((tm, d), lambda i: (i, 0)), pl.BlockSpec((nb, tm, nf), lambda i: (0, i, 0))],
        scratch_shapes=[pltpu.VMEM(wgu.shape, BF16), pltpu.VMEM(wd.shape, BF16), pltpu.SemaphoreType.DMA((2,))],
        compiler_params=_params(("arbitrary",)),
    )(h, gain, wgu, wd)


def _ffn_bwd(dh, h, gain, gu, wgu, wd, name):
    t, d = h.shape
    nb, nf, _ = wgu.shape
    nh = nb // 2
    tm = _row_tile(t, 256)

    def body(dh_ref, h_ref, g_ref, gu_ref, wgu_hbm, wd_hbm, dhp_ref, dgu_ref, a_ref, n_ref, dgain_ref,
             wgu_v, wd_v, sems):
        @pl.when(pl.program_id(0) == 0)
        def _():
            _load_weights([(wgu_hbm, wgu_v), (wd_hbm, wd_v)], sems)
            dgain_ref[...] = jnp.zeros_like(dgain_ref)

        x = h_ref[...]
        gain_v = g_ref[...]
        n, xh, r = _rms(x, gain_v)
        n_ref[...] = n.astype(BF16)
        dh_v = dh_ref[...]
        dfb = (0.5 * dh_v).astype(BF16)
        dn = jnp.zeros((tm, d), F32)
        for j in range(nh):
            da = _dot_nt(dfb, wd_v[j])
            g = gu_ref[j].astype(F32)
            u = gu_ref[j + nh].astype(F32)
            sg = jax.nn.sigmoid(g)
            si = g * sg
            dg = (da * u * (sg * (1.0 + g * (1.0 - sg)))).astype(BF16)
            du = (da * si).astype(BF16)
            a_ref[j] = (si * u).astype(BF16)
            dgu_ref[j] = dg
            dgu_ref[j + nh] = du
            dn = dn + _dot(dg, wgu_v[j]) + _dot(du, wgu_v[j + nh])
        dx, dgain = _rms_bwd(xh, r, gain_v, dn)
        dhp_ref[...] = dh_v + dx
        dgain_ref[...] += dgain

    row = pl.BlockSpec((tm, d), lambda i: (i, 0))
    vec = pl.BlockSpec((1, d), lambda i: (0, 0))
    return pl.pallas_call(
        body, name=name, grid=(t // tm,),
        out_shape=[jax.ShapeDtypeStruct((t, d), F32), jax.ShapeDtypeStruct((nb, t, nf), BF16),
                   jax.ShapeDtypeStruct((nh, t, nf), BF16), jax.ShapeDtypeStruct((t, d), BF16),
                   jax.ShapeDtypeStruct((1, d), F32)],
        in_specs=[row, row, vec, pl.BlockSpec((nb, tm, nf), lambda i: (0, i, 0)), ANY, ANY],
        out_specs=[row, pl.BlockSpec((nb, tm, nf), lambda i: (0, i, 0)),
                   pl.BlockSpec((nh, tm, nf), lambda i: (0, i, 0)), row, vec],
        scratch_shapes=[pltpu.VMEM(wgu.shape, BF16), pltpu.VMEM(wd.shape, BF16), pltpu.SemaphoreType.DMA((2,))],
        compiler_params=_params(("arbitrary",)),
    )(dh, h, gain, gu, wgu, wd)


def _dw(xa, dy, nb, n, name, scale=1.0, dep=None):
    t, k = xa.shape[-2:]
    tt = _row_tile(t, 512)
    steps = t // tt
    wide = dy.ndim == 2 and xa.ndim == 2
    if xa.ndim == 3:
        x_spec = pl.BlockSpec((nb, tt, k), lambda i: (0, i, 0))
    else:
        x_spec = pl.BlockSpec((tt, k), lambda i: (i, 0))
    if dy.ndim == 3:
        dy_spec = pl.BlockSpec((nb, tt, n), lambda i: (0, i, 0))
    else:
        dy_spec = pl.BlockSpec((tt, dy.shape[1]), lambda i: (i, 0))
    acc_shape = (k, nb * n) if wide else (nb, k, n)
    stage_shape = (k, nb * n) if wide else (k, n)

    def body(x_ref, dy_ref, *rest):
        o_hbm, ob_hbm, acc, stage, sems = rest[-5:]

        @pl.when(pl.program_id(0) == 0)
        def _():
            acc[...] = jnp.zeros_like(acc)

        if wide:
            acc[...] += _dot(x_ref[...].astype(BF16).T, dy_ref[...].astype(BF16))
        elif xa.ndim == 2:
            xt = x_ref[...].astype(BF16).T
            for j in range(nb):
                acc[j] += _dot(xt, dy_ref[j].astype(BF16))
        else:
            dyb = dy_ref[...].astype(BF16)
            for j in range(nb):
                acc[j] += _dot_tn(x_ref[j].astype(BF16), dyb)

        @pl.when(pl.program_id(0) == steps - 1)
        def _():
            if scale != 1.0:
                acc[...] = acc[...] * scale
            if wide:
                cps = [pltpu.make_async_copy(acc.at[:, pl.ds(j * n, n)] if nb > 1 else acc, o_hbm.at[j], sems.at[j])
                       for j in range(nb)]
            else:
                cps = [pltpu.make_async_copy(acc, o_hbm, sems.at[0])]
            for cp in cps:
                cp.start()
            if wide:
                stage[...] = acc[...].astype(BF16)
                bcs = [pltpu.make_async_copy(stage.at[:, pl.ds(j * n, n)] if nb > 1 else stage, ob_hbm.at[j],
                                             sems.at[nb + j]) for j in range(nb)]
                for cp in bcs:
                    cp.start()
                for cp in bcs:
                    cp.wait()
            else:
                for j in range(nb):
                    stage[...] = acc[j].astype(BF16)
                    cp = pltpu.make_async_copy(stage, ob_hbm.at[j], sems.at[nb])
                    cp.start()
                    cp.wait()
            for cp in cps:
                cp.wait()

    return pl.pallas_call(
        body, name=name, grid=(steps,),
        out_shape=[jax.ShapeDtypeStruct((nb, k, n), F32), jax.ShapeDtypeStruct((nb, k, n), BF16)],
        in_specs=[x_spec, dy_spec] + ([] if dep is None else [ANY]),
        out_specs=[ANY, ANY],
        scratch_shapes=[pltpu.VMEM(acc_shape, F32), pltpu.VMEM(stage_shape, BF16),
                        pltpu.SemaphoreType.DMA((2 * nb,))],
        compiler_params=_params(("arbitrary",)),
    )(*((xa, dy) if dep is None else (xa, dy, dep)))


def _proj_fwd(h, gain, win, wgate, name):
    t, d = h.shape
    tm = _row_tile(t, 256)
    nq, ng = win.shape[0], wgate.shape[1]

    def body(h_ref, g_ref, win_ref, wg_ref, un_ref, qkv_ref, gate_ref):
        n, _, _ = _rms(h_ref[...], g_ref[...])
        nbf = n.astype(BF16)
        un_ref[...] = nbf
        qkv_ref[...] = _dot_nt(nbf, win_ref[...])
        gate_ref[...] = jax.nn.sigmoid(_dot(nbf, wg_ref[...]))

    full = lambda a: pl.BlockSpec(a.shape, lambda i: (0,) * a.ndim)
    return pl.pallas_call(
        body, name=name, grid=(t // tm,),
        out_shape=[jax.ShapeDtypeStruct((t, d), BF16), jax.ShapeDtypeStruct((t, nq), F32),
                   jax.ShapeDtypeStruct((t, ng), F32)],
        in_specs=[pl.BlockSpec((tm, d), lambda i: (i, 0)), full(gain), full(win), full(wgate)],
        out_specs=[pl.BlockSpec((tm, d), lambda i: (i, 0)), pl.BlockSpec((tm, nq), lambda i: (i, 0)),
                   pl.BlockSpec((tm, ng), lambda i: (i, 0))],
        compiler_params=_params(("arbitrary",)),
    )(h, gain, win, wgate)


def _proj_bwd(dh, h, gain, dzg, dqkv_parts, win, wgate, name):
    t, d = h.shape
    tm = _row_tile(t, 256)
    ng = wgate.shape[1]
    np_ = len(dqkv_parts)
    widths = [a.shape[1] for a in dqkv_parts]

    def body(dh_ref, h_ref, g_ref, dzg_ref, *rest):
        part_refs, (win_ref, wg_ref, dhp_ref, dgain_ref) = rest[:np_], rest[np_:]

        @pl.when(pl.program_id(0) == 0)
        def _():
            dgain_ref[...] = jnp.zeros_like(dgain_ref)

        gain_v = g_ref[...]
        _, xh, r = _rms(h_ref[...], gain_v)
        dun = _dot_nt(dzg_ref[...], wg_ref[...])
        off = 0
        for ref, wd in zip(part_refs, widths):
            dun = dun + _dot(ref[...].astype(BF16), win_ref[off:off + wd, :])
            off += wd
        dx, dgain = _rms_bwd(xh, r, gain_v, dun)
        dhp_ref[...] = dh_ref[...] + dx
        dgain_ref[...] += dgain

    full = lambda a: pl.BlockSpec(a.shape, lambda i: (0,) * a.ndim)
    row = pl.BlockSpec((tm, d), lambda i: (i, 0))
    return pl.pallas_call(
        body, name=name, grid=(t // tm,),
        out_shape=[jax.ShapeDtypeStruct((t, d), F32), jax.ShapeDtypeStruct((1, d), F32)],
        in_specs=[row, row, full(gain), pl.BlockSpec((tm, ng), lambda i: (i, 0))]
        + [pl.BlockSpec((tm, wd), lambda i: (i, 0)) for wd in widths] + [full(win), full(wgate)],
        out_specs=[row, pl.BlockSpec((1, d), lambda i: (0, 0))],
        compiler_params=_params(("arbitrary",)),
    )(dh, h, gain, dzg, *dqkv_parts, win, wgate)


def _dw_rows(parts, dy, name):
    t, n = dy.shape
    widths = [a.shape[1] for a in parts]
    k = sum(widths)
    tt = _row_tile(t, 512)
    steps = t // tt
    np_ = len(parts)

    def body(*refs):
        part_refs, dy_ref = refs[:np_], refs[np_]
        o_hbm, ob_hbm, acc, stage, sems = refs[np_ + 1:]

        @pl.when(pl.program_id(0) == 0)
        def _():
            acc[...] = jnp.zeros_like(acc)

        dyb = dy_ref[...].astype(BF16)
        off = 0
        for ref, wd in zip(part_refs, widths):
            acc[off:off + wd, :] += _dot(ref[...].astype(BF16).T, dyb)
            off += wd

        @pl.when(pl.program_id(0) == steps - 1)
        def _():
            stage[...] = acc[...].astype(BF16)
            cps = [pltpu.make_async_copy(acc, o_hbm.at[0], sems.at[0]),
                   pltpu.make_async_copy(stage, ob_hbm.at[0], sems.at[1])]
            for cp in cps:
                cp.start()
            for cp in cps:
                cp.wait()

    return pl.pallas_call(
        body, name=name, grid=(steps,),
        out_shape=[jax.ShapeDtypeStruct((1, k, n), F32), jax.ShapeDtypeStruct((1, k, n), BF16)],
        in_specs=[pl.BlockSpec((tt, wd), lambda i: (i, 0)) for wd in widths] + [pl.BlockSpec((tt, n), lambda i: (i, 0))],
        out_specs=[ANY, ANY],
        scratch_shapes=[pltpu.VMEM((k, n), F32), pltpu.VMEM((k, n), BF16), pltpu.SemaphoreType.DMA((2,))],
        compiler_params=_params(("arbitrary",)),
    )(*parts, dy)


def _merge_fwd(h, ya, yb, gate, wpa, wpb, wout, name):
    t, d = h.shape
    tm = _row_tile(t, 256)

    def body(h_ref, ya_ref, yb_ref, ga_ref, gb_ref, wpa_ref, wpb_ref, wout_ref, out_ref, mg_ref, pa_ref, pb_ref):
        pa = _dot(ya_ref[...].astype(BF16), wpa_ref[...])
        pb = _dot(yb_ref[...].astype(BF16), wpb_ref[...])
        merged = (ga_ref[...] * pa + gb_ref[...] * pb).astype(BF16)
        pa_ref[...] = pa.astype(BF16)
        pb_ref[...] = pb.astype(BF16)
        mg_ref[...] = merged
        out_ref[...] = h_ref[...] + _dot(merged, wout_ref[...])

    full = lambda a: pl.BlockSpec(a.shape, lambda i: (0,) * a.ndim)
    row = pl.BlockSpec((tm, d), lambda i: (i, 0))
    yrow = pl.BlockSpec((tm, ya.shape[1]), lambda i: (i, 0))
    return pl.pallas_call(
        body, name=name, grid=(t // tm,),
        out_shape=[jax.ShapeDtypeStruct((t, d), F32)] + [jax.ShapeDtypeStruct((t, d), BF16)] * 3,
        in_specs=[row, yrow, yrow, pl.BlockSpec((tm, d), lambda i: (i, 0)), pl.BlockSpec((tm, d), lambda i: (i, 1)),
                  full(wpa), full(wpb), full(wout)],
        out_specs=[row] * 4,
        compiler_params=_params(("arbitrary",)),
    )(h, ya, yb, gate, gate, wpa, wpb, wout)


def _merge_bwd(dh, pa, pb, gate, wpa, wpb, wout, name):
    t, d = dh.shape
    tm = _row_tile(t, 256)
    wy = wpa.shape[0]

    def body(dh_ref, pa_ref, pb_ref, ga_ref, gb_ref, wpa_ref, wpb_ref, wout_ref,
             dpa_ref, dpb_ref, dzg_ref, dya_ref, dyb_ref):
        dm = _dot_nt(dh_ref[...].astype(BF16), wout_ref[...])
        ga, gb = ga_ref[...], gb_ref[...]
        dpa = (dm * ga).astype(BF16)
        dpb = (dm * gb).astype(BF16)
        dpa_ref[...] = dpa
        dpb_ref[...] = dpb
        dzg_ref[:, :d] = (dm * pa_ref[...].astype(F32) * ga * (1.0 - ga)).astype(BF16)
        dzg_ref[:, d:] = (dm * pb_ref[...].astype(F32) * gb * (1.0 - gb)).astype(BF16)
        dya_ref[...] = _dot_nt(dpa, wpa_ref[...])
        dyb_ref[...] = _dot_nt(dpb, wpb_ref[...])

    full = lambda a: pl.BlockSpec(a.shape, lambda i: (0,) * a.ndim)
    row = pl.BlockSpec((tm, d), lambda i: (i, 0))
    yrow = pl.BlockSpec((tm, wy), lambda i: (i, 0))
    return pl.pallas_call(
        body, name=name, grid=(t // tm,),
        out_shape=[jax.ShapeDtypeStruct((t, d), BF16), jax.ShapeDtypeStruct((t, d), BF16),
                   jax.ShapeDtypeStruct((t, 2 * d), BF16), jax.ShapeDtypeStruct((t, wy), F32),
                   jax.ShapeDtypeStruct((t, wy), F32)],
        in_specs=[row, row, row, pl.BlockSpec((tm, d), lambda i: (i, 0)), pl.BlockSpec((tm, d), lambda i: (i, 1)),
                  full(wpa), full(wpb), full(wout)],
        out_specs=[row, row, pl.BlockSpec((tm, 2 * d), lambda i: (i, 0)), yrow, yrow],
        compiler_params=_params(("arbitrary",)),
    )(dh, pa, pb, gate, gate, wpa, wpb, wout)


def _ple_loss(h, gain, p, target, wpg, wpe, name):
    t, d = h.shape
    tm = _row_tile(t, 256)
    pd = p.shape[1]

    def body(h_ref, g_ref, p_ref, t_ref, wpg_ref, wpe_ref, dh_ref, dz_ref, dpp_ref, n_ref, dgain_ref, loss_ref):
        @pl.when(pl.program_id(0) == 0)
        def _():
            dgain_ref[...] = jnp.zeros_like(dgain_ref)
            loss_ref[...] = jnp.zeros_like(loss_ref)

        x = h_ref[...]
        gain_v = g_ref[...]
        n, xh, r = _rms(x, gain_v)
        nbf = n.astype(BF16)
        n_ref[...] = nbf
        pg = jax.nn.sigmoid(_dot(nbf, wpg_ref[...]))
        pp = _dot(p_ref[...].astype(BF16), wpe_ref[...])
        err = (x + pg * pp) - t_ref[...]
        loss_ref[...] += 0.5 * jnp.sum(jnp.mean(err * err, axis=-1, keepdims=True))
        dy = err * (1.0 / d)
        dpp_ref[...] = (dy * pg).astype(BF16)
        dz = (dy * pp * pg * (1.0 - pg)).astype(BF16)
        dz_ref[...] = dz
        dn = _dot_nt(dz, wpg_ref[...])
        dx, dgain = _rms_bwd(xh, r, gain_v, dn)
        dh_ref[...] = dy + dx
        dgain_ref[...] += dgain

    full = lambda a: pl.BlockSpec(a.shape, lambda i: (0,) * a.ndim)
    row = pl.BlockSpec((tm, d), lambda i: (i, 0))
    return pl.pallas_call(
        body, name=name, grid=(t // tm,),
        out_shape=[jax.ShapeDtypeStruct((t, d), F32), jax.ShapeDtypeStruct((t, d), BF16),
                   jax.ShapeDtypeStruct((t, d), BF16), jax.ShapeDtypeStruct((t, d), BF16),
                   jax.ShapeDtypeStruct((1, d), F32), jax.ShapeDtypeStruct((8, LANES), F32)],
        in_specs=[row, full(gain), pl.BlockSpec((tm, pd), lambda i: (i, 0)), row, full(wpg), full(wpe)],
        out_specs=[row, row, row, row, pl.BlockSpec((1, d), lambda i: (0, 0)),
                   pl.BlockSpec((8, LANES), lambda i: (0, 0))],
        compiler_params=_params(("arbitrary",)),
    )(h, gain, p, target, wpg, wpe)


def _head_masks():
    lane = lax.broadcasted_iota(jnp.int32, (1, LANES), 1)
    m0 = (lane < HEAD_DIM).astype(F32)
    return m0, 1.0 - m0


def _head_mean(v, m0, m1):
    del m0, m1
    width = v.shape[-1]
    shift = HEAD_DIM.bit_length() - 1
    r = jnp.right_shift(lax.broadcasted_iota(jnp.int32, (width, width), 0), shift)
    c = jnp.right_shift(lax.broadcasted_iota(jnp.int32, (width, width), 1), shift)
    same_head = (r == c).astype(BF16)
    hi = v.astype(BF16)
    lo = (v - hi.astype(F32)).astype(BF16)
    return (_dot(hi, same_head) + _dot(lo, same_head)) * (1.0 / HEAD_DIM)


def _head_norm(x, gain, m0, m1):
    r = lax.rsqrt(_head_mean(x * x, m0, m1) + EPS)
    xh = x * r
    return xh * gain, xh, r


def _head_norm_bwd(xh, r, gain, dy, m0, m1):
    gdy = gain * dy
    dx = r * (gdy - xh * _head_mean(xh * gdy, m0, m1))
    return dx, jnp.sum(dy * xh, axis=0, keepdims=True)


GROUP = 4
QW = GROUP * HEAD_DIM
STACK = GROUP * QTILE


def _kv_width(mode):
    return QW if mode == "A" else LANES


def _q_scratch_shape(mode, s_len):
    return (s_len, QW) if mode == "A" else (GROUP * s_len, LANES)


def _group_masks(dtype=F32):
    lane = lax.broadcasted_iota(jnp.int32, (1, QW), 1)
    return [((lane >= h * HEAD_DIM) & (lane < (h + 1) * HEAD_DIM)).astype(dtype) for h in range(GROUP)]


def _stack_heads(first_kv, x, m0, m1):
    out = []
    for half in range(GROUP // 2):
        xh = x[:, half * LANES:(half + 1) * LANES]
        a0, a1 = xh * m0, xh * m1
        r0, r1 = pltpu.roll(a0, HEAD_DIM, 1), pltpu.roll(a1, HEAD_DIM, 1)
        out += [jnp.where(first_kv, a0, r0), jnp.where(first_kv, r1, a1)]
    return out


def _unstack_heads(mode, first_kv, ts, m0, m1):
    if mode == "A":
        masks = _group_masks()
        return sum(t * mk for t, mk in zip(ts, masks))
    halves = []
    for half in range(GROUP // 2):
        t0 = jnp.where(first_kv, ts[2 * half], pltpu.roll(ts[2 * half], HEAD_DIM, 1))
        t1 = jnp.where(first_kv, pltpu.roll(ts[2 * half + 1], HEAD_DIM, 1), ts[2 * half + 1])
        halves.append(t0 * m0 + t1 * m1)
    return jnp.concatenate(halves, axis=1)


def _store_stacked(dst, i, heads):
    for half in range(2):
        rows = slice(half * QTILE, (half + 1) * QTILE)
        for h, x in enumerate(heads):
            dst[pl.ds((2 * i + half) * STACK + h * QTILE, QTILE), :] = x[rows].astype(dst.dtype)


def _load_stacked(mode, ref, m):
    if mode == "B":
        return ref[pl.ds(pl.multiple_of(m * STACK, STACK), STACK), :]
    x = ref[pl.ds(pl.multiple_of(m * QTILE, QTILE), QTILE), :]
    return jnp.concatenate([x * mk for mk in _group_masks(x.dtype)], axis=0)


def _attn_prep(mode, group, s_len, padk, q_ref, k_ref, v_ref, gq_ref, gk_ref, qs, k2, v2, do_ref=None, dos=None):
    m0, m1 = _head_masks()
    zpad = jnp.zeros((padk, k2.shape[1]), BF16)
    k2[pl.ds(0, padk), :] = zpad
    v2[pl.ds(0, padk), :] = zpad
    first_kv = group == 0
    rt = 2 * QTILE
    for i in range(s_len // rt):
        rows = pl.ds(i * rt, rt)
        qn, _, _ = _head_norm(q_ref[rows, :], gq_ref[...], m0, m1)
        kn, _, _ = _head_norm(k_ref[rows, :], gk_ref[...], m0, m1)
        qn = qn * (HEAD_DIM ** -0.5)
        if mode == "A":
            qs[rows, :] = qn.astype(BF16)
            if dos is not None:
                dos[rows, :] = do_ref[rows, :].astype(BF16)
        else:
            _store_stacked(qs, i, _stack_heads(first_kv, qn, m0, m1))
            if dos is not None:
                _store_stacked(dos, i, _stack_heads(first_kv, do_ref[rows, :], m0, m1))
        k2[pl.ds(padk + i * rt, rt), :] = kn.astype(BF16)
        v2[pl.ds(padk + i * rt, rt), :] = v_ref[rows, :].astype(BF16)


def _softmax_terms(mode, s, sink):
    mx = jnp.max(s, axis=-1, keepdims=True)
    if mode == "B":
        mx = jnp.maximum(mx, sink)
    e = jnp.exp(s - mx)
    l = jnp.sum(e, axis=-1, keepdims=True)
    if mode == "B":
        l = l + jnp.exp(sink - mx)
    return e, mx, l


def _sink_column(sink_ref, group):
    row = lax.broadcasted_iota(jnp.int32, (STACK, 1), 0)
    col = jnp.zeros((STACK, 1), F32)
    for h in range(GROUP):
        col = jnp.where((row >= h * QTILE) & (row < (h + 1) * QTILE), sink_ref[GROUP * group + h], col)
    return col


def _head_deltas(dd, m0, m1):
    cols = []
    for half in range(GROUP // 2):
        dh = dd[:, half * LANES:(half + 1) * LANES]
        cols += [jnp.sum(dh * m0, axis=-1, keepdims=True), jnp.sum(dh * m1, axis=-1, keepdims=True)]
    return jnp.concatenate(cols, axis=0)


def _attn_cols(mode):
    if mode == "A":
        return (lambda b, g: (b, g)), (lambda b, g: (b, 2 + g)), (lambda b, g: (b, 4 + g))
    return (lambda b, g: (b, 6 + g)), (lambda b, g: (b, 16)), (lambda b, g: (b, 17))


def _attn_fwd(mode, qkv, gq, gk, bias, sinks, bl, s_len, name):
    bw = bias.shape[-1]
    padk = bw - QTILE
    nt = s_len // QTILE
    qmap, kmap, vmap = _attn_cols(mode)

    kw = _kv_width(mode)

    def body(q_ref, k_ref, v_ref, gq_ref, gk_ref, bias_ref, sink_ref, o_ref, qs, k2, v2, s_buf, *rest):
        o_buf = rest[0] if rest else None
        group = pl.program_id(1)
        m0, m1 = _head_masks()
        first_kv = group == 0
        _attn_prep(mode, group, s_len, padk, q_ref, k_ref, v_ref, gq_ref, gk_ref, qs, k2, v2)
        col = lax.broadcasted_iota(jnp.int32, (STACK, bw), 1)
        sink = _sink_column(sink_ref, group)

        def scores(m, slot):
            r0 = pl.multiple_of(m * QTILE, QTILE)
            s = _dot_nt(_load_stacked(mode, qs, m), k2[pl.ds(r0, bw), :]) + bias_ref[...]
            s_buf[slot] = jnp.where(col >= (padk - r0), s, NEG_INF)

        def finish_tile(m, slot):
            r0 = pl.multiple_of(m * QTILE, QTILE)
            e, _, l = _softmax_terms(mode, s_buf[slot], sink)
            if mode == "A":
                o_st = _dot(e.astype(BF16), v2[pl.ds(r0, bw), :]) / l
                heads = [o_st[h * QTILE:(h + 1) * QTILE] for h in range(GROUP)]
                o_ref[pl.ds(r0, QTILE), :] = _unstack_heads(mode, first_kv, heads, m0, m1)
            else:
                o_buf[pl.ds(pl.multiple_of(m * STACK, STACK), STACK), :] = _dot((e * (1.0 / l)).astype(BF16),
                                                                                 v2[pl.ds(r0, bw), :])

        scores(0, 0)

        def pair(j, carry):
            scores(2 * j + 1, 1)
            finish_tile(2 * j, 0)
            scores(jnp.minimum(2 * j + 2, nt - 1), 0)
            finish_tile(2 * j + 1, 1)
            return carry

        lax.fori_loop(0, nt // 2, pair, 0)
        if mode == "B":
            for m in range(nt):
                heads = [o_buf[pl.ds(m * STACK + h * QTILE, QTILE), :] for h in range(GROUP)]
                o_ref[pl.ds(m * QTILE, QTILE), :] = _unstack_heads(mode, first_kv, heads, m0, m1)

    blk = lambda w, f: pl.BlockSpec((s_len, w), f)
    return pl.pallas_call(
        body, name=name, grid=(bl, B_Q_HEADS // GROUP),
        out_shape=jax.ShapeDtypeStruct((bl * s_len, B_Q_HEADS * HEAD_DIM), F32),
        in_specs=[blk(QW, qmap), blk(kw, kmap), blk(kw, vmap),
                  pl.BlockSpec((1, QW), lambda b, g: (0, 0)), pl.BlockSpec((1, kw), lambda b, g: (0, 0)),
                  pl.BlockSpec((STACK, bw), lambda b, g: (g, 0)),
                  pl.BlockSpec(memory_space=pltpu.SMEM)],
        out_specs=blk(QW, lambda b, g: (b, g)),
        scratch_shapes=[pltpu.VMEM(_q_scratch_shape(mode, s_len), BF16)] + [pltpu.VMEM((s_len + padk, kw), BF16)] * 2
        + [pltpu.VMEM((2, STACK, bw), F32)] + ([pltpu.VMEM((GROUP * s_len, LANES), F32)] if mode == "B" else []),
        compiler_params=_params(("arbitrary", "arbitrary")),
    )(qkv, qkv, qkv, gq, gk, bias.reshape(-1, bw), sinks)


def _attn_bwd(mode, qkv, gq, gk, bias, sinks, y, dy, bl, s_len, name):
    bw = bias.shape[-1]
    padk = bw - QTILE
    nt = s_len // QTILE
    qmap, kmap, vmap = _attn_cols(mode)
    t = bl * s_len
    kw = _kv_width(mode)
    kvw = 4 * LANES if mode == "A" else LANES
    dp_ahead = mode == "B"

    def body(q_ref, k_ref, v_ref, gq_ref, gk_ref, bias_ref, sink_ref, y_ref, dy_ref,
             dq_ref, dk_ref, dv_ref, dgq_ref, dgk_ref, dbias_ref, dsink_ref,
             qs, k2, v2, dos, dqs, dk, dv, s_buf, dp_buf):
        group = pl.program_id(1)
        m0, m1 = _head_masks()
        first_kv = group == 0
        _attn_prep(mode, group, s_len, padk, q_ref, k_ref, v_ref, gq_ref, gk_ref, qs, k2, v2, dy_ref, dos)
        dk[...] = jnp.zeros_like(dk)
        dv[...] = jnp.zeros_like(dv)
        dbias_ref[...] = jnp.zeros_like(dbias_ref)
        col = lax.broadcasted_iota(jnp.int32, (STACK, bw), 1)
        lane8 = lax.broadcasted_iota(jnp.int32, (8, LANES), 1)
        sink = _sink_column(sink_ref, group)

        def ahead(m, slot):
            r0 = pl.multiple_of(m * QTILE, QTILE)
            band = pl.ds(r0, bw)
            s = _dot_nt(_load_stacked(mode, qs, m), k2[band, :]) + bias_ref[...]
            s_buf[slot] = jnp.where(col >= (padk - r0), s, NEG_INF)
            if dp_ahead:
                dp_buf[slot] = _dot_nt(_load_stacked(mode, dos, m), v2[band, :])

        def tile(m, slot, dsink):
            r0 = pl.multiple_of(m * QTILE, QTILE)
            rows = pl.ds(r0, QTILE)
            band = pl.ds(r0, bw)
            q_st = _load_stacked(mode, qs, m)
            do_st = _load_stacked(mode, dos, m)
            delta = _head_deltas(dy_ref[rows, :] * y_ref[rows, :], m0, m1)
            kb = k2[band, :]
            e, mx, l = _softmax_terms(mode, s_buf[slot], sink)
            inv = 1.0 / l
            pn = e * inv
            ds = pn * ((dp_buf[slot] if dp_ahead else _dot_nt(do_st, v2[band, :])) - delta)
            if mode == "A":
                dbias_ref[...] += ds
            else:
                part = jnp.exp(sink - mx) * inv * delta
                for h in range(GROUP):
                    dsink = dsink - jnp.where(lane8 == h, jnp.sum(part[h * QTILE:(h + 1) * QTILE]), 0.0)
            dsb = ds.astype(BF16)
            dv[band, :] += _dot_tn(pn.astype(BF16), do_st)
            dk[band, :] += _dot_tn(dsb, q_st)
            dq_st = _dot(dsb, kb)
            if mode == "A":
                heads = [dq_st[h * QTILE:(h + 1) * QTILE] for h in range(GROUP)]
                dqs[rows, :] = _unstack_heads(mode, first_kv, heads, m0, m1)
            else:
                dqs[pl.ds(pl.multiple_of(m * STACK, STACK), STACK), :] = dq_st
            return dsink

        ahead(0, 0)

        def pair(j, dsink):
            ahead(2 * j + 1, 1)
            dsink = tile(2 * j, 0, dsink)
            ahead(jnp.minimum(2 * j + 2, nt - 1), 0)
            return tile(2 * j + 1, 1, dsink)

        dsink = lax.fori_loop(0, nt // 2, pair, jnp.zeros((8, LANES), F32))
        dsink_ref[...] = dsink

        rt = 2 * QTILE
        dgq = jnp.zeros((1, QW), F32)
        dgk = jnp.zeros((1, kw), F32)
        for i in range(s_len // rt):
            rows = pl.ds(i * rt, rt)
            src = pl.ds(padk + i * rt, rt)
            gq_v, gk_v = gq_ref[...], gk_ref[...]
            _, qh, qr = _head_norm(q_ref[rows, :], gq_v, m0, m1)
            _, kh, kr = _head_norm(k_ref[rows, :], gk_v, m0, m1)
            if mode == "A":
                dqn = dqs[rows, :] * (HEAD_DIM ** -0.5)
            else:
                dqn = jnp.concatenate(
                    [_unstack_heads(mode, first_kv, [dqs[pl.ds((2 * i + half) * STACK + h * QTILE, QTILE), :]
                                                     for h in range(GROUP)], m0, m1)
                     for half in range(2)], axis=0) * (HEAD_DIM ** -0.5)
            dq_raw, dgq_i = _head_norm_bwd(qh, qr, gq_v, dqn, m0, m1)
            dk_raw, dgk_i = _head_norm_bwd(kh, kr, gk_v, dk[src, :], m0, m1)
            dvn = dv[src, :]
            dq_ref[rows, :] = dq_raw
            if mode == "A":
                dk_ref[rows, :] = dk_raw
                dv_ref[rows, :] = dvn
            else:
                @pl.when(group == 0)
                def _():
                    dk_ref[rows, :] = dk_raw
                    dv_ref[rows, :] = dvn

                @pl.when(group != 0)
                def _():
                    dk_ref[rows, :] += dk_raw
                    dv_ref[rows, :] += dvn
            dgq, dgk = dgq + dgq_i, dgk + dgk_i
        dgq_ref[...] = jnp.broadcast_to(dgq, (8, QW))
        dgk_ref[...] = jnp.broadcast_to(dgk, (8, kw))

    ng = B_Q_HEADS // GROUP
    blk = lambda w, f: pl.BlockSpec((s_len, w), f)
    small = lambda w: pl.BlockSpec((None, None, 8, w), lambda b, g: (b, g, 0, 0))
    own = lambda b, g: (b, g)
    kvmap = own if mode == "A" else (lambda b, g: (b, 0))
    pad_f32 = pltpu.VMEM((s_len + padk, kw), F32)
    pad_bf = pltpu.VMEM((s_len + padk, kw), BF16)
    stack_bf = pltpu.VMEM(_q_scratch_shape(mode, s_len), BF16)
    outs = pl.pallas_call(
        body, name=name, grid=(bl, ng),
        out_shape=[jax.ShapeDtypeStruct((t, ng * QW), F32), jax.ShapeDtypeStruct((t, kvw), F32),
                   jax.ShapeDtypeStruct((t, kvw), F32),
                   jax.ShapeDtypeStruct((bl, ng, 8, QW), F32), jax.ShapeDtypeStruct((bl, ng, 8, kw), F32),
                   jax.ShapeDtypeStruct((bl, ng * STACK, bw), F32), jax.ShapeDtypeStruct((bl, ng, 8, LANES), F32)],
        in_specs=[blk(QW, qmap), blk(kw, kmap), blk(kw, vmap),
                  pl.BlockSpec((1, QW), lambda b, g: (0, 0)), pl.BlockSpec((1, kw), lambda b, g: (0, 0)),
                  pl.BlockSpec((STACK, bw), lambda b, g: (g, 0)),
                  pl.BlockSpec(memory_space=pltpu.SMEM),
                  blk(QW, own), blk(QW, own)],
        out_specs=[blk(QW, own), blk(kw, kvmap), blk(kw, kvmap), small(QW), small(kw),
                   pl.BlockSpec((None, STACK, bw), lambda b, g: (b, g, 0)), small(LANES)],
        scratch_shapes=[stack_bf, pad_bf, pad_bf, stack_bf, pltpu.VMEM(_q_scratch_shape(mode, s_len), F32),
                        pad_f32, pad_f32, pltpu.VMEM((2, STACK, bw), F32),
                        pltpu.VMEM((2, STACK, bw) if dp_ahead else (8, LANES), F32)],
        compiler_params=_params(("arbitrary", "arbitrary")),
    )(qkv, qkv, qkv, gq, gk, bias.reshape(-1, bw), sinks, y, dy)
    outs = list(outs)
    outs[5] = outs[5].reshape(bl, B_Q_HEADS, QTILE, bw)
    return outs


def _band_geometry(prev):
    bw = QTILE + prev * CHUNK
    i = np.arange(QTILE)[:, None]
    j = np.arange(bw)[None, :]
    dist = i + prev * CHUNK - j
    valid = (j // CHUNK >= i // CHUNK) & (j // CHUNK <= i // CHUNK + prev)
    return dist, valid


A_VAR0 = (A_PREV * CHUNK - A_MAX_REL) // LANES * LANES


A_NVAR = QTILE + A_PREV * CHUNK - A_VAR0


def _skew_rows(x, sign):
    rows, n = x.shape
    row = lax.broadcasted_iota(jnp.int32, x.shape, 0)
    b = 1
    while b < rows:
        x = jnp.where((row & b) != 0, pltpu.roll(x, (sign * b) % n, 1), x)
        b *= 2
    return x


def _rel_bias_expand(table, name):
    _, valid = _band_geometry(A_PREV)
    bw = valid.shape[1]
    valid_f = jnp.asarray(valid.astype(np.float32))
    rev = jnp.flip(table[:, 1:], axis=1).reshape(A_HEADS, 1, A_NVAR)

    def body(rev_ref, valid_ref, o_ref):
        rowv = jnp.broadcast_to(rev_ref[...], (QTILE, A_NVAR))
        top = rowv[:, 0:1]
        var = _skew_rows(rowv, 1)
        row = lax.broadcasted_iota(jnp.int32, (QTILE, A_NVAR), 0)
        colv = lax.broadcasted_iota(jnp.int32, (QTILE, A_NVAR), 1)
        var = jnp.where(colv < row, top, var)
        ok = valid_ref[...] > 0.5
        o_ref[:, :A_VAR0] = jnp.where(ok[:, :A_VAR0], top, NEG_INF)
        o_ref[:, A_VAR0:] = jnp.where(ok[:, A_VAR0:], var, NEG_INF)

    return pl.pallas_call(
        body, name=name, grid=(A_HEADS,),
        out_shape=jax.ShapeDtypeStruct((A_HEADS, QTILE, bw), F32),
        in_specs=[pl.BlockSpec((None, 1, A_NVAR), lambda h: (h, 0, 0)), pl.BlockSpec((QTILE, bw), lambda h: (0, 0))],
        out_specs=pl.BlockSpec((None, QTILE, bw), lambda h: (h, 0, 0)),
        compiler_params=_params(("arbitrary",)),
    )(rev, valid_f)


def _rel_bias_grad(dbias, name):
    bl = dbias.shape[0]
    bw = dbias.shape[-1]

    def body(db_ref, o_ref):
        g = db_ref[0]
        for b in range(1, bl):
            g = g + db_ref[b]
        sk = _skew_rows(g[:, A_VAR0:], -1)
        row = lax.broadcasted_iota(jnp.int32, (QTILE, A_NVAR), 0)
        colv = lax.broadcasted_iota(jnp.int32, (QTILE, A_NVAR), 1)
        wrapped = (row + colv) >= A_NVAR
        main = jnp.sum(jnp.where(wrapped, 0.0, sk), axis=0, keepdims=True)
        top = jnp.sum(g[:, :A_VAR0]) + jnp.sum(jnp.where(wrapped, sk, 0.0))
        o_ref[:, :A_NVAR] = jnp.broadcast_to(main, (8, A_NVAR))
        o_ref[:, A_NVAR:] = jnp.full((8, LANES), top, F32)

    out = pl.pallas_call(
        body, name=name, grid=(A_HEADS,),
        out_shape=jax.ShapeDtypeStruct((A_HEADS, 8, A_NVAR + LANES), F32),
        in_specs=[pl.BlockSpec((bl, None, QTILE, bw), lambda h: (0, h, 0, 0))],
        out_specs=pl.BlockSpec((None, 8, A_NVAR + LANES), lambda h: (h, 0, 0)),
        compiler_params=_params(("arbitrary",)),
    )(dbias)
    main, top = out[:, 0, :A_NVAR], out[:, 0, A_NVAR]
    fm = jnp.flip(main, axis=1)
    return jnp.concatenate([jnp.zeros((A_HEADS, 1), F32), fm[:, :-1], fm[:, -1:] + top[:, None]], axis=1)


def _alibi_bias():
    dist, valid = _band_geometry(B_PREV)
    slopes = np.array([2.0 ** (-8.0 * (h + 1) / B_Q_HEADS) for h in range(B_Q_HEADS)], dtype=np.float32)
    bias = -slopes[:, None, None] * np.abs(dist).astype(np.float32)[None]
    return jnp.asarray(np.where(valid[None], bias, np.float32(NEG_INF)).astype(np.float32))


SMALL_NAMES = ("ffn1_norm", "mix_norm", "ffn2_norm", "ple_norm", "a_q_norm", "a_k_norm", "b_q_norm", "b_k_norm",
               "a_rel_bias", "b_sinks", "loss")


def _pack_small(vals):
    rows = []
    for nme in SMALL_NAMES:
        v = vals[nme].astype(F32)
        if nme == "a_rel_bias":
            v = jnp.pad(v.reshape(A_HEADS, -1), ((0, 0), (0, 3 * LANES - (2 * A_MAX_REL + 1))))
        v = v.reshape(-1)
        v = jnp.pad(v, (0, (-v.shape[0]) % LANES))
        rows.append(v.reshape(-1, LANES))
    out = jnp.concatenate(rows, axis=0)
    return jnp.pad(out, ((0, (-out.shape[0]) % 8), (0, 0)))


def _unpack_small(packed, shapes):
    out, r = {}, 0
    for nme in SMALL_NAMES:
        shp = shapes[nme]
        if nme == "a_rel_bias":
            nr = A_HEADS * 3
            out[nme] = packed[r:r + nr].reshape(A_HEADS, 3 * LANES)[:, :2 * A_MAX_REL + 1].reshape(shp)
        else:
            size = int(np.prod(shp)) if shp else 1
            nr = -(-size // LANES)
            out[nme] = packed[r:r + nr].reshape(-1)[:size].reshape(shp)
        r += nr
    return out


BIG_NAMES = ("ffn1_w_gu", "ffn1_w_down", "w_in", "w_gate", "w_proj_a", "w_proj_b", "w_out",
             "ffn2_w_gu", "ffn2_w_down", "w_ple_gate", "w_ple_proj")
ROW_SHARDED = ("ffn1_w_down", "ffn2_w_down", "w_out", "w_ple_gate")
WEIGHT_ORDER = ("ffn1_norm", "ffn1_w_gu", "ffn1_w_down", "mix_norm", "w_in", "a_q_norm", "a_k_norm", "a_rel_bias",
                "b_q_norm", "b_k_norm", "b_sinks", "w_gate", "w_proj_a", "w_proj_b", "w_out", "ffn2_norm",
                "ffn2_w_gu", "ffn2_w_down", "ple_norm", "w_ple_gate", "w_ple_proj")


TRANSPOSED = ("ffn1_w_gu", "ffn2_w_gu", "w_in")


def _local(a, nme):
    return a[0].T if nme in TRANSPOSED else a[0]


def _full_cols(wg):
    nb, k, n = wg.shape
    return jnp.transpose(wg, (1, 0, 2)).reshape(k, nb * n)


def _col_blocks(g, nb):
    k, n = g.shape
    return jnp.transpose(g.reshape(k, nb, n // nb), (1, 0, 2))


def _step(x, p, target, w, m, v):
    bl, s_len, d = x.shape
    t = bl * s_len
    h0 = x.reshape(t, d)
    pt = p.reshape(t, p.shape[-1])
    tgt = target.reshape(t, d)

    g_ffn1, g_mix, g_ffn2, g_ple = w["ffn1_norm"], w["mix_norm"], w["ffn2_norm"], w["ple_norm"]
    tiled = lambda a, width: jnp.tile(a.reshape(1, HEAD_DIM), (1, width // HEAD_DIM))
    gqa, gka = tiled(w["a_q_norm"], QW), tiled(w["a_k_norm"], _kv_width("A"))
    gqb, gkb = tiled(w["b_q_norm"], QW), tiled(w["b_k_norm"], _kv_width("B"))
    sinks = w["b_sinks"].reshape(B_Q_HEADS)
    bias_a = _rel_bias_expand(w["a_rel_bias"][0], "rel_bias_expand")
    bias_b = _alibi_bias()

    shard = {nme: _local(w[nme], nme).astype(BF16) for nme in BIG_NAMES}
    wgu1, wd1 = _all_gather([shard["ffn1_w_gu"], shard["ffn1_w_down"]], "weights_gather_ffn1")
    nf = wgu1.shape[1]
    wd1 = wd1.reshape(N_DEV // 2, nf, d)
    mixer_names = ("w_in", "w_gate")
    rest_names = ("w_proj_a", "w_proj_b", "w_out", "ffn2_w_gu", "ffn2_w_down", "w_ple_gate", "w_ple_proj")
    send1, recv1, bufs, token = _gather_start([shard[nme] for nme in mixer_names], wgu1, "gather_start_mixer")
    rsend1, rrecv1, rest_bufs, token = _gather_start([shard[nme] for nme in rest_names], token, "gather_start_rest")

    h1, gu1 = _ffn_fwd(h0, g_ffn1 + token[0, 0], wgu1, wd1, "ffn1_fwd")
    send2, recv2, bufs, token = _gather_pass(send1, recv1, bufs, h1, "gather_pass_mixer")
    win, wgate = _gather_wait(send2, recv2, bufs, token, "gather_wait_mixer")
    win, wgate = win.reshape(IN_COLS, d), _full_cols(wgate)
    un, qkv, gate = _proj_fwd(h1, g_mix, win, wgate, "proj_fwd")
    ya = _attn_fwd("A", qkv, gqa, gka, bias_a, sinks, bl, s_len, "attn_a_fwd")
    rsend2, rrecv2, rest_bufs, token = _gather_pass(rsend1, rrecv1, rest_bufs, ya, "gather_pass_rest")
    yb = _attn_fwd("B", qkv, gqb + token[0, 0], gkb, bias_b, sinks, bl, s_len, "attn_b_fwd")
    gathered = dict(zip(rest_names, _gather_wait(rsend2, rrecv2, rest_bufs, yb, "gather_wait_rest")))
    wgu2 = gathered["ffn2_w_gu"]
    wd2 = gathered["ffn2_w_down"].reshape(N_DEV // 2, nf, d)
    wpa = _full_cols(gathered["w_proj_a"])
    wpb = _full_cols(gathered["w_proj_b"])
    wpe = _full_cols(gathered["w_ple_proj"])
    wout = gathered["w_out"].reshape(d, d)
    wpg = gathered["w_ple_gate"].reshape(d, d)
    h2, merged, pa, pb = _merge_fwd(h1, ya, yb, gate, wpa, wpb, wout, "merge_fwd")
    h3, gu2 = _ffn_fwd(h2, g_ffn2, wgu2, wd2, "ffn2_fwd")
    dh3, dz4, dpp, n4, dg_ple, loss_part = _ple_loss(h3, g_ple, pt, tgt, wpg, wpe, "ple_loss")

    xi, yi, ci = _place()
    me = jnp.stack([4 * xi + 2 * yi + ci]).astype(jnp.int32)
    g32, g16, big = {}, {}, {}

    def keep(nme, pair, rows=None):
        for store, g in zip((g32, g16), pair):
            store[nme] = g if rows is None else g.reshape(N_DEV, rows, d)

    def start(names, after, tag):
        send, recv, parts, lands, token = _scatter_start([g16[nme] for nme in names], after, "grads_start_" + tag)
        return names, send, recv, parts, lands, token

    def finish(state, after, tag):
        names, send, recv, parts, lands, _ = state
        lands = _scatter_wait(send, recv, parts, lands, after, "grads_wait_" + tag)
        return names, lands

    def adam(done, dep):
        for nme, land in zip(*done):
            outs = _final_adam(g32[nme], land, _local(w[nme], nme), _local(m[nme], nme), _local(v[nme], nme), me, dep,
                               "adam_" + nme)
            big[nme] = [(o.T if nme in TRANSPOSED else o)[None] for o in outs]

    keep("w_ple_gate", _dw(n4, dz4, 1, d, "dw_ple_gate"), d // N_DEV)
    keep("w_ple_proj", _dw(pt, dpp, N_DEV, d // N_DEV, "dw_ple_proj"))
    early = [(start(("w_ple_gate", "w_ple_proj"), dh3, "ple"), "ple")]

    dh2, dgu2, a2, n3, dg_ffn2 = _ffn_bwd(dh3, h2, g_ffn2 + early[-1][0][-1][0, 0], gu2, wgu2, wd2, "ffn2_bwd")
    keep("ffn2_w_down", _dw(a2, dh3, N_DEV // 2, d, "dw_ffn2_down", 0.5), nf // 2)
    early.append((start(("ffn2_w_down",), dh2, "ffn2_down"), "ffn2_down"))
    keep("ffn2_w_gu", _dw(dgu2, n3, N_DEV, d, "dw_ffn2_gu", dep=early[-1][0][-1]))
    flight = start(("ffn2_w_gu",), dh2, "ffn2")

    dpa, dpb, dzg, dya, dyb = _merge_bwd(dh2, pa, pb, gate, wpa, wpb, wout, "merge_bwd")
    keep("w_out", _dw(merged, dh2, 1, d, "dw_out"), d // N_DEV)
    keep("w_proj_a", _dw(ya, dpa, N_DEV, d // N_DEV, "dw_proj_a"))
    keep("w_proj_b", _dw(yb, dpb, N_DEV, d // N_DEV, "dw_proj_b"))
    keep("w_gate", _dw(un, dzg, N_DEV, 2 * d // N_DEV, "dw_gate"))

    tok = flight[-1][0, 0]
    dqa, dka, dva, dgqa, dgka, dbias, _ = _attn_bwd("A", qkv, gqa + tok, gka, bias_a, sinks, ya, dya, bl, s_len,
                                                     "attn_a_bwd")
    dqb, dkb, dvb, dgqb, dgkb, _, dsink = _attn_bwd("B", qkv, gqb, gkb, bias_b, sinks, yb, dyb, bl, s_len, "attn_b_bwd")
    dqkv = [dqa, dka, dva, dqb, dkb, dvb]
    dtab = _rel_bias_grad(dbias, "rel_bias_grad")

    dh1, dg_mix = _proj_bwd(dh2, h1, g_mix, dzg, dqkv, win, wgate, "proj_bwd")
    keep("w_in", _dw_rows(dqkv, un, "dw_in"), IN_COLS // N_DEV)
    waiting = [finish(state, g32["w_in"], tag) for state, tag in early]
    done = finish(flight, waiting[-1][1][0], "ffn2")
    flight = start(("w_out", "w_proj_a", "w_proj_b", "w_gate", "w_in"), done[1][0], "mixer")
    waiting.append(done)

    dh0, dgu1, a1, n1, dg_ffn1 = _ffn_bwd(dh1, h0, g_ffn1 + flight[-1][0, 0], gu1, wgu1, wd1, "ffn1_bwd")
    keep("ffn1_w_down", _dw(a1, dh1, N_DEV // 2, d, "dw_ffn1_down", 0.5), nf // 2)
    done = finish(flight, g32["ffn1_w_down"], "mixer")
    flight = start(("ffn1_w_down",), done[1][0], "ffn1_down")
    waiting.append(done)

    keep("ffn1_w_gu", _dw(dgu1, n1, N_DEV, d, "dw_ffn1_gu", dep=flight[-1]))
    done = finish(flight, g32["ffn1_w_gu"], "ffn1_down")
    flight = start(("ffn1_w_gu",), done[1][0], "ffn1_gu")
    for group in waiting + [done]:
        adam(group, flight[-1])
    behind = 0.0 * big["ffn1_w_down"][0][0, 0, :1]
    smalls = (dg_ffn1, dg_mix, dg_ffn2, dg_ple + behind, dgqa, dgka, dgqb, dgkb, dtab, dsink)
    return dh0, loss_part, big, smalls, flight, finish, adam


def kernel(x, p, ffn1_norm, ffn1_w_gu, ffn1_w_down, mix_norm, w_in, a_q_norm, a_k_norm, a_rel_bias, b_q_norm, b_k_norm, b_sinks, w_gate, w_proj_a, w_proj_b, w_out, ffn2_norm, ffn2_w_gu, ffn2_w_down, ple_norm, w_ple_gate, w_ple_proj, loss_target, m_ffn1_norm, m_ffn1_w_gu, m_ffn1_w_down, m_mix_norm, m_w_in, m_a_q_norm, m_a_k_norm, m_a_rel_bias, m_b_q_norm, m_b_k_norm, m_b_sinks, m_w_gate, m_w_proj_a, m_w_proj_b, m_w_out, m_ffn2_norm, m_ffn2_w_gu, m_ffn2_w_down, m_ple_norm, m_w_ple_gate, m_w_ple_proj, v_ffn1_norm, v_ffn1_w_gu, v_ffn1_w_down, v_mix_norm, v_w_in, v_a_q_norm, v_a_k_norm, v_a_rel_bias, v_b_q_norm, v_b_k_norm, v_b_sinks, v_w_gate, v_w_proj_a, v_w_proj_b, v_w_out, v_ffn2_norm, v_ffn2_w_gu, v_ffn2_w_down, v_ple_norm, v_w_ple_gate, v_w_ple_proj):
    w = dict(ffn1_norm=ffn1_norm, ffn1_w_gu=ffn1_w_gu, ffn1_w_down=ffn1_w_down, mix_norm=mix_norm, w_in=w_in,
             a_q_norm=a_q_norm, a_k_norm=a_k_norm, a_rel_bias=a_rel_bias, b_q_norm=b_q_norm, b_k_norm=b_k_norm,
             b_sinks=b_sinks, w_gate=w_gate, w_proj_a=w_proj_a, w_proj_b=w_proj_b, w_out=w_out, ffn2_norm=ffn2_norm,
             ffn2_w_gu=ffn2_w_gu, ffn2_w_down=ffn2_w_down, ple_norm=ple_norm, w_ple_gate=w_ple_gate,
             w_ple_proj=w_ple_proj)
    m = dict(ffn1_norm=m_ffn1_norm, ffn1_w_gu=m_ffn1_w_gu, ffn1_w_down=m_ffn1_w_down, mix_norm=m_mix_norm,
             w_in=m_w_in, a_q_norm=m_a_q_norm, a_k_norm=m_a_k_norm, a_rel_bias=m_a_rel_bias, b_q_norm=m_b_q_norm,
             b_k_norm=m_b_k_norm, b_sinks=m_b_sinks, w_gate=m_w_gate, w_proj_a=m_w_proj_a, w_proj_b=m_w_proj_b,
             w_out=m_w_out, ffn2_norm=m_ffn2_norm, ffn2_w_gu=m_ffn2_w_gu, ffn2_w_down=m_ffn2_w_down,
             ple_norm=m_ple_norm, w_ple_gate=m_w_ple_gate, w_ple_proj=m_w_ple_proj)
    v = dict(ffn1_norm=v_ffn1_norm, ffn1_w_gu=v_ffn1_w_gu, ffn1_w_down=v_ffn1_w_down, mix_norm=v_mix_norm,
             w_in=v_w_in, a_q_norm=v_a_q_norm, a_k_norm=v_a_k_norm, a_rel_bias=v_a_rel_bias, b_q_norm=v_b_q_norm,
             b_k_norm=v_b_k_norm, b_sinks=v_b_sinks, w_gate=v_w_gate, w_proj_a=v_w_proj_a, w_proj_b=v_w_proj_b,
             w_out=v_w_out, ffn2_norm=v_ffn2_norm, ffn2_w_gu=v_ffn2_w_gu, ffn2_w_down=v_ffn2_w_down,
             ple_norm=v_ple_norm, w_ple_gate=v_w_ple_gate, w_ple_proj=v_w_ple_proj)
    bl, s_len, d = x.shape

    dh0, loss_part, big, smalls, flight, finish, adam = _step(x, p[0], loss_target, w, m, v)
    dg_ffn1, dg_mix, dg_ffn2, dg_ple, dgqa, dgka, dgqb, dgkb, dtab, dsink = smalls

    fold = lambda a: a[:, :, 0, :].reshape(-1, HEAD_DIM).sum(axis=0)
    small_part = dict(
        ffn1_norm=dg_ffn1, mix_norm=dg_mix, ffn2_norm=dg_ffn2, ple_norm=dg_ple,
        a_q_norm=fold(dgqa), a_k_norm=fold(dgka), b_q_norm=fold(dgqb), b_k_norm=fold(dgkb),
        a_rel_bias=dtab,
        b_sinks=dsink.sum(axis=0)[:, 0, :GROUP].reshape(B_Q_HEADS),
        loss=loss_part[0, :1])
    zero1 = jnp.zeros((1,), F32)
    shapes = {nme: w[nme].shape for nme in SMALL_NAMES if nme != "loss"}
    shapes["loss"] = ()
    pk = lambda src: _pack_small({**{nme: src[nme] for nme in SMALL_NAMES if nme != "loss"}, "loss": zero1})
    sg, sd, sm, sv = _small_allreduce_adam(_pack_small(small_part), pk(w), pk(m), pk(v), "small_allreduce_adam")
    adam(finish(flight, sg, "ffn1_gu"), sg)
    sg, sd, sm, sv = (_unpack_small(a, shapes) for a in (sg, sd, sm, sv))

    def pick(i):
        out = []
        for nme in WEIGHT_ORDER:
            out.append(big[nme][i] if nme in big else (sg, sd, sm, sv)[i][nme])
        return out

    return (sg["loss"], dh0.reshape(bl, s_len, d), *pick(0), *pick(1), *pick(2), *pick(3))
```

```python
import functools

import jax
import jax.numpy as jnp
import numpy as np
from jax import lax
from jax.experimental import pallas as pl
from jax.experimental.pallas import tpu as pltpu

F32 = jnp.float32
BF16 = jnp.bfloat16

CHUNK = 64
HEAD_DIM = 64
A_HEADS = 8
A_PREV = 8
A_MAX_REL = 128
B_Q_HEADS = 8
B_KV_HEADS = 2
B_PREV = 2
A_WIDTH = A_HEADS * HEAD_DIM
B_Q_WIDTH = B_Q_HEADS * HEAD_DIM
B_KV_WIDTH = B_KV_HEADS * HEAD_DIM
IN_COLS = 3 * A_WIDTH + B_Q_WIDTH + 2 * B_KV_WIDTH
EPS = 1e-6
NEG_INF = -1e30
ADAM_LR = 0.001
ADAM_B1 = 0.9
ADAM_B2 = 0.999
ADAM_EPS = 1e-08
ADAM_WD = 0.01
ADAM_STEP = 10

N_DEV = 8
LANES = 128
QTILE = 2 * CHUNK
VMEM_LIMIT = 56 * 1024 * 1024
ADAM_TILE_ELEMS = 256 * 1024

MESH_ID = pl.DeviceIdType.MESH
ANY = pl.BlockSpec(memory_space=pl.ANY)
HBM = pl.BlockSpec(memory_space=pltpu.HBM)
SEM = pl.BlockSpec(memory_space=pltpu.SEMAPHORE)
SIDE_EFFECT = pltpu.SideEffectType.DATAFLOW_SIDE_EFFECTING


def _dot(a, b):
    return jnp.dot(a, b, preferred_element_type=F32)


def _dot_nt(a, b):
    return lax.dot_general(a, b, (((1,), (1,)), ((), ())), preferred_element_type=F32)


def _dot_tn(a, b):
    return lax.dot_general(a, b, (((0,), (0,)), ((), ())), preferred_element_type=F32)


def _params(sem=None, vmem=VMEM_LIMIT):
    return pltpu.CompilerParams(dimension_semantics=sem, vmem_limit_bytes=vmem)


def _row_tile(t, want):
    while t % want:
        want //= 2
    return want


def _place():
    return lax.axis_index("x"), lax.axis_index("y"), lax.axis_index("c")


def _gather_level(bufs, send_sems, recv_sems, level, shards=None):
    x, y, c = _place()
    me, sib = (x, y, c), (x, y, 1 - c)
    chips = [(1 - x, y), (x, 1 - y), (1 - x, 1 - y)]

    def copy(w, k, block, to):
        px, py, pc = block
        rows = bufs[w].at[4 * px + 2 * py + pc]
        src = shards[w] if shards is not None and block is me else rows
        return pltpu.make_async_remote_copy(src_ref=src, dst_ref=rows, send_sem=send_sems.at[k], recv_sem=recv_sems.at[k],
                                            device_id=to, device_id_type=MESH_ID)

    n = len(bufs)
    own = []
    if level == 1:
        own = [pltpu.make_async_copy(bufs[w].at[4 * x + 2 * y + c] if shards is None else shards[w],
                                     bufs[w].at[4 * x + 2 * y + c], send_sems.at[4 * n + w]) for w in range(n)]
    out, arriving = [], []
    for w in range(len(bufs)):
        if level == 1:
            out.append(copy(w, 4 * w, me, sib))
            arriving.append(copy(w, 4 * w, sib, me))
        for j, chip in enumerate(chips):
            if level == 1:
                out.append(copy(w, 4 * w + 1 + j, me, (*chip, c)))
                arriving.append(copy(w, 4 * w + 1 + j, (*chip, c), me))
            else:
                out.append(copy(w, 3 * w + j, (*chip, c), sib))
                arriving.append(copy(w, 3 * w + j, (*chip, 1 - c), me))
    return out, arriving, own


def _split_call(body, name, bufs, sems_in, after, n_sems_out, token, extra=()):
    n = len(bufs)
    out_shape = [pltpu.SemaphoreType.DMA((n_sems_out,))] * (2 if n_sems_out else 0)
    out_shape += [pltpu.HBM(a.shape, a.dtype) for a in bufs]
    out_specs = [SEM] * (2 if n_sems_out else 0) + [HBM] * n
    if token:
        out_shape.append(jax.ShapeDtypeStruct((8, LANES), F32))
        out_specs.append(pl.BlockSpec(memory_space=pltpu.VMEM))
    first = 2 if n_sems_out else 0
    return pl.pallas_call(
        body, name=name, out_shape=tuple(out_shape),
        in_specs=[HBM] * (n + len(extra)) + [SEM] * len(sems_in) + [ANY], out_specs=tuple(out_specs),
        input_output_aliases={i: first + i for i in range(n)},
        compiler_params=pltpu.CompilerParams(has_side_effects=SIDE_EFFECT),
    )(*bufs, *extra, *sems_in, after)


def _gather_start(shards, after, name):
    n = len(shards)
    hbm = lambda a: pltpu.with_memory_space_constraint(a, pltpu.HBM)
    bufs = [hbm(lax.empty((N_DEV,) + s.shape, s.dtype)) for s in shards]

    def body(*refs):
        out, _, own = _gather_level(refs[:n], refs[2 * n + 1], refs[2 * n + 2], 1, shards=refs[n:2 * n])
        for cp in own + out:
            cp.start()
        refs[-1][...] = jnp.zeros_like(refs[-1])

    outs = _split_call(body, name, bufs + [hbm(s) for s in shards], [], after, 5 * n, True)
    return outs[0], outs[1], list(outs[2:2 + 2 * n]), outs[-1]


def _gather_pass(send1, recv1, bufs_and_shards, after, name):
    n = len(bufs_and_shards) // 2
    bufs = bufs_and_shards

    def body(*refs):
        refs = refs[:n] + refs[2 * n:]
        out1, in1, own = _gather_level(refs[:n], refs[n], refs[n + 1], 1)
        out2, _, _ = _gather_level(refs[:n], refs[n + 3], refs[n + 4], 2)
        for cp in in1:
            cp.wait_recv()
        for cp in out2:
            cp.start()
        for cp in out1:
            cp.wait_send()
        for cp in own:
            cp.wait()
        refs[-1][...] = jnp.zeros_like(refs[-1])

    outs = _split_call(body, name, bufs, [send1, recv1], after, 3 * n, True)
    return outs[0], outs[1], list(outs[2:2 + n]), outs[-1]


def _gather_wait(send2, recv2, bufs, after, name):
    n = len(bufs)

    def body(*refs):
        out2, in2, _ = _gather_level(refs[:n], refs[n], refs[n + 1], 2)
        for cp in in2:
            cp.wait_recv()
        for cp in out2:
            cp.wait_send()

    return list(_split_call(body, name, bufs, [send2, recv2], after, 0, False))


def _scatter_copies(parts, lands, send_sems, recv_sems):
    x, y, c = _place()
    cps = []
    for w, (part, land) in enumerate(zip(parts, lands)):
        for k in range(1, N_DEV):
            px, py, pc = x ^ ((k >> 2) & 1), y ^ ((k >> 1) & 1), c ^ (k & 1)
            cps.append(pltpu.make_async_remote_copy(
                src_ref=part.at[4 * px + 2 * py + pc], dst_ref=land.at[k - 1],
                send_sem=send_sems.at[7 * w + k - 1], recv_sem=recv_sems.at[7 * w + k - 1],
                device_id=(px, py, pc), device_id_type=MESH_ID))
    return cps


def _scatter_start(parts, after, name):
    n = len(parts)

    def body(*refs):
        ins, lands = refs[:n], refs[n:2 * n]
        send_sems, recv_sems = refs[2 * n + 1], refs[2 * n + 2]
        token = refs[-1]
        for cp in _scatter_copies(ins, lands, send_sems, recv_sems):
            cp.start()
        token[...] = jnp.zeros_like(token)

    land_shapes = [(N_DEV - 1,) + p.shape[1:] for p in parts]
    in_hbm = [pltpu.with_memory_space_constraint(p, pltpu.HBM) for p in parts]
    in_hbm += [pltpu.with_memory_space_constraint(lax.empty(s, p.dtype), pltpu.HBM) for s, p in zip(land_shapes, parts)]
    outs = pl.pallas_call(
        body, name=name,
        out_shape=(pltpu.SemaphoreType.DMA((7 * n,)), pltpu.SemaphoreType.DMA((7 * n,)),
                   *[pltpu.HBM(p.shape, p.dtype) for p in parts],
                   *[pltpu.HBM(s, p.dtype) for s, p in zip(land_shapes, parts)],
                   jax.ShapeDtypeStruct((8, LANES), F32)),
        in_specs=[HBM] * (2 * n) + [ANY],
        out_specs=(SEM, SEM, *[HBM] * (2 * n), pl.BlockSpec(memory_space=pltpu.VMEM)),
        input_output_aliases={i: 2 + i for i in range(2 * n)},
        compiler_params=pltpu.CompilerParams(has_side_effects=SIDE_EFFECT),
    )(*in_hbm, after)
    return outs[0], outs[1], list(outs[2:2 + n]), list(outs[2 + n:2 + 2 * n]), outs[-1]


def _scatter_wait(send_sems, recv_sems, parts, lands, after, name):
    n = len(parts)

    def body(*refs):
        ins, lnd = refs[:n], refs[n:2 * n]
        for cp in _scatter_copies(ins, lnd, refs[2 * n], refs[2 * n + 1]):
            cp.wait_send()
            cp.wait_recv()

    outs = pl.pallas_call(
        body, name=name,
        out_shape=tuple(pltpu.HBM(a.shape, a.dtype) for a in parts + lands),
        in_specs=[HBM] * (2 * n) + [SEM, SEM, ANY],
        out_specs=tuple([HBM] * (2 * n)),
        input_output_aliases={i: i for i in range(2 * n)},
        compiler_params=pltpu.CompilerParams(has_side_effects=SIDE_EFFECT),
    )(*parts, *lands, send_sems, recv_sems, after)
    return list(outs[n:])


def _adam(w, g, m, v):
    m2 = ADAM_B1 * m + (1.0 - ADAM_B1) * g
    v2 = ADAM_B2 * v + (1.0 - ADAM_B2) * (g * g)
    m_hat = m2 / (1.0 - ADAM_B1 ** ADAM_STEP)
    v_hat = v2 / (1.0 - ADAM_B2 ** ADAM_STEP)
    delta = -ADAM_LR * (m_hat / (jnp.sqrt(v_hat) + ADAM_EPS) + ADAM_WD * w)
    return delta, m2, v2


def _small_allreduce_adam(part, w, m, v, name):
    rows = part.shape[0]

    def body(p_ref, w_ref, m_ref, v_ref, g_ref, d_ref, mo_ref, vo_ref, buf, send_sems, recv_sems):
        x, y, c = _place()
        buf[0] = p_ref[...]
        cps = []
        for k in range(1, N_DEV):
            kx, ky, kc = (k >> 2) & 1, (k >> 1) & 1, k & 1
            peer = (x ^ kx, y ^ ky, c ^ kc)
            cps.append(pltpu.make_async_remote_copy(
                src_ref=p_ref, dst_ref=buf.at[k], send_sem=send_sems.at[k - 1], recv_sem=recv_sems.at[k - 1],
                device_id=peer, device_id_type=MESH_ID))
        for cp in cps:
            cp.start()
        for cp in cps:
            cp.wait()
        me = 4 * x + 2 * y + c
        total = buf[me]
        for d in range(1, N_DEV):
            total = total + buf[d ^ me]
        g_ref[...] = total
        delta, m2, v2 = _adam(w_ref[...], total, m_ref[...], v_ref[...])
        d_ref[...] = delta
        mo_ref[...] = m2
        vo_ref[...] = v2

    vm = pl.BlockSpec(memory_space=pltpu.VMEM)
    return pl.pallas_call(
        body, name=name,
        out_shape=[jax.ShapeDtypeStruct(part.shape, F32)] * 4,
        in_specs=[vm] * 4, out_specs=[vm] * 4,
        scratch_shapes=[pltpu.VMEM((N_DEV, rows, LANES), F32),
                        pltpu.SemaphoreType.DMA((N_DEV - 1,)), pltpu.SemaphoreType.DMA((N_DEV - 1,))],
    )(part, w, m, v)


def _final_adam(g8, land, w, m, v, me, dep, name):
    _, r, c = g8.shape
    tr = max(q for q in range(16, r + 1, 16) if r % q == 0 and q * c <= ADAM_TILE_ELEMS)

    def body(me_ref, g_ref, land_ref, w_ref, m_ref, v_ref, dep_ref, go_ref, d_ref, mo_ref, vo_ref):
        del dep_ref
        g = g_ref[...]
        for k in range(N_DEV - 1):
            g = g + land_ref[k].astype(F32)
        go_ref[...] = g
        delta, m2, v2 = _adam(w_ref[...], g, m_ref[...], v_ref[...])
        d_ref[...] = delta
        mo_ref[...] = m2
        vo_ref[...] = v2

    plain = pl.BlockSpec((tr, c), lambda i, s: (i, 0))
    return pl.pallas_call(
        body, name=name,
        out_shape=[jax.ShapeDtypeStruct((r, c), F32)] * 4,
        grid_spec=pltpu.PrefetchScalarGridSpec(
            num_scalar_prefetch=1, grid=(r // tr,),
            in_specs=[pl.BlockSpec((None, tr, c), lambda i, s: (s[0], i, 0)),
                      pl.BlockSpec((N_DEV - 1, tr, c), lambda i, s: (0, i, 0)),
                      plain, plain, plain, ANY],
            out_specs=[plain] * 4),
        compiler_params=_params(("arbitrary",)),
    )(me, g8, land, w, m, v, dep)


def _rms(x, gain):
    r = lax.rsqrt(jnp.mean(x * x, axis=-1, keepdims=True) + EPS)
    xh = x * r
    return xh * gain, xh, r


def _rms_bwd(xh, r, gain, dy):
    gdy = gain * dy
    dx = r * (gdy - xh * jnp.mean(xh * gdy, axis=-1, keepdims=True))
    return dx, jnp.sum(dy * xh, axis=0, keepdims=True)


def _load_weights(pairs, sems):
    cps = [pltpu.make_async_copy(src, dst, sems.at[i]) for i, (src, dst) in enumerate(pairs)]
    for cp in cps:
        cp.start()
    for cp in cps:
        cp.wait()


def _ffn_fwd(h, gain, wgu, wd, name):
    t, d = h.shape
    nb, nf, _ = wgu.shape
    nh = nb // 2
    tm = _row_tile(t, 512)

    def body(h_ref, g_ref, wgu_hbm, wd_hbm, out_ref, gu_ref, wgu_v, wd_v, sems):
        @pl.when(pl.program_id(0) == 0)
        def _():
            _load_weights([(wgu_hbm, wgu_v), (wd_hbm, wd_v)], sems)

        x = h_ref[...]
        n, _, _ = _rms(x, g_ref[...])
        nbf = n.astype(BF16)
        acc = jnp.zeros((tm, d), F32)
        for j in range(nh):
            g = _dot_nt(nbf, wgu_v[j])
            u = _dot_nt(nbf, wgu_v[j + nh])
            gu_ref[j] = g.astype(BF16)
            gu_ref[j + nh] = u.astype(BF16)
            a = (g * jax.nn.sigmoid(g)) * u
            acc = acc + _dot(a.astype(BF16), wd_v[j])
        out_ref[...] = x + 0.5 * acc

    return pl.pallas_call(
        body, name=name, grid=(t // tm,),
        out_shape=[jax.ShapeDtypeStruct((t, d), F32), jax.ShapeDtypeStruct((nb, t, nf), BF16)],
        in_specs=[pl.BlockSpec((tm, d), lambda i: (i, 0)), pl.BlockSpec((1, d), lambda i: (0, 0)), ANY, ANY],
        out_specs=[pl.BlockSpec((tm, d), lambda i: (i, 0)), pl.BlockSpec((nb, tm, nf), lambda i: (0, i, 0))],
        scratch_shapes=[pltpu.VMEM(wgu.shape, BF16), pltpu.VMEM(wd.shape, BF16), pltpu.SemaphoreType.DMA((2,))],
        compiler_params=_params(("arbitrary",)),
    )(h, gain, wgu, wd)


def _ffn_bwd(dh, h, gain, gu, wgu, wd, name):
    t, d = h.shape
    nb, nf, _ = wgu.shape
    nh = nb // 2
    tm = _row_tile(t, 256)

    def body(dh_ref, h_ref, g_ref, gu_ref, wgu_hbm, wd_hbm, dhp_ref, dgu_ref, a_ref, n_ref, dgain_ref,
             wgu_v, wd_v, sems):
        @pl.when(pl.program_id(0) == 0)
        def _():
            _load_weights([(wgu_hbm, wgu_v), (wd_hbm, wd_v)], sems)
            dgain_ref[...] = jnp.zeros_like(dgain_ref)

        x = h_ref[...]
        gain_v = g_ref[...]
        n, xh, r = _rms(x, gain_v)
        n_ref[...] = n.astype(BF16)
        dh_v = dh_ref[...]
        dfb = (0.5 * dh_v).astype(BF16)
        dn = jnp.zeros((tm, d), F32)
        for j in range(nh):
            da = _dot_nt(dfb, wd_v[j])
            g = gu_ref[j].astype(F32)
            u = gu_ref[j + nh].astype(F32)
            sg = jax.nn.sigmoid(g)
            si = g * sg
            dg = (da * u * (sg * (1.0 + g * (1.0 - sg)))).astype(BF16)
            du = (da * si).astype(BF16)
            a_ref[j] = (si * u).astype(BF16)
            dgu_ref[j] = dg
            dgu_ref[j + nh] = du
            dn = dn + _dot(dg, wgu_v[j]) + _dot(du, wgu_v[j + nh])
        dx, dgain = _rms_bwd(xh, r, gain_v, dn)
        dhp_ref[...] = dh_v + dx
        dgain_ref[...] += dgain

    row = pl.BlockSpec((tm, d), lambda i: (i, 0))
    vec = pl.BlockSpec((1, d), lambda i: (0, 0))
    return pl.pallas_call(
        body, name=name, grid=(t // tm,),
        out_shape=[jax.ShapeDtypeStruct((t, d), F32), jax.ShapeDtypeStruct((nb, t, nf), BF16),
                   jax.ShapeDtypeStruct((nh, t, nf), BF16), jax.ShapeDtypeStruct((t, d), BF16),
                   jax.ShapeDtypeStruct((1, d), F32)],
        in_specs=[row, row, vec, pl.BlockSpec((nb, tm, nf), lambda i: (0, i, 0)), ANY, ANY],
        out_specs=[row, pl.BlockSpec((nb, tm, nf), lambda i: (0, i, 0)),
                   pl.BlockSpec((nh, tm, nf), lambda i: (0, i, 0)), row, vec],
        scratch_shapes=[pltpu.VMEM(wgu.shape, BF16), pltpu.VMEM(wd.shape, BF16), pltpu.SemaphoreType.DMA((2,))],
        compiler_params=_params(("arbitrary",)),
    )(dh, h, gain, gu, wgu, wd)


def _dw(xa, dy, nb, n, name, scale=1.0, dep=None):
    t, k = xa.shape[-2:]
    tt = _row_tile(t, 512)
    steps = t // tt
    wide = dy.ndim == 2 and xa.ndim == 2
    if xa.ndim == 3:
        x_spec = pl.BlockSpec((nb, tt, k), lambda i: (0, i, 0))
    else:
        x_spec = pl.BlockSpec((tt, k), lambda i: (i, 0))
    if dy.ndim == 3:
        dy_spec = pl.BlockSpec((nb, tt, n), lambda i: (0, i, 0))
    else:
        dy_spec = pl.BlockSpec((tt, dy.shape[1]), lambda i: (i, 0))
    acc_shape = (k, nb * n) if wide else (nb, k, n)
    stage_shape = (k, nb * n) if wide else (k, n)

    def body(x_ref, dy_ref, *rest):
        o_hbm, ob_hbm, acc, stage, sems = rest[-5:]

        @pl.when(pl.program_id(0) == 0)
        def _():
            acc[...] = jnp.zeros_like(acc)

        if wide:
            acc[...] += _dot(x_ref[...].astype(BF16).T, dy_ref[...].astype(BF16))
        elif xa.ndim == 2:
            xt = x_ref[...].astype(BF16).T
            for j in range(nb):
                acc[j] += _dot(xt, dy_ref[j].astype(BF16))
        else:
            dyb = dy_ref[...].astype(BF16)
            for j in range(nb):
                acc[j] += _dot_tn(x_ref[j].astype(BF16), dyb)

        @pl.when(pl.program_id(0) == steps - 1)
        def _():
            if scale != 1.0:
                acc[...] = acc[...] * scale
            if wide:
                cps = [pltpu.make_async_copy(acc.at[:, pl.ds(j * n, n)] if nb > 1 else acc, o_hbm.at[j], sems.at[j])
                       for j in range(nb)]
            else:
                cps = [pltpu.make_async_copy(acc, o_hbm, sems.at[0])]
            for cp in cps:
                cp.start()
            if wide:
                stage[...] = acc[...].astype(BF16)
                bcs = [pltpu.make_async_copy(stage.at[:, pl.ds(j * n, n)] if nb > 1 else stage, ob_hbm.at[j],
                                             sems.at[nb + j]) for j in range(nb)]
                for cp in bcs:
                    cp.start()
                for cp in bcs:
                    cp.wait()
            else:
                for j in range(nb):
                    stage[...] = acc[j].astype(BF16)
                    cp = pltpu.make_async_copy(stage, ob_hbm.at[j], sems.at[nb])
                    cp.start()
                    cp.wait()
            for cp in cps:
                cp.wait()

    return pl.pallas_call(
        body, name=name, grid=(steps,),
        out_shape=[jax.ShapeDtypeStruct((nb, k, n), F32), jax.ShapeDtypeStruct((nb, k, n), BF16)],
        in_specs=[x_spec, dy_spec] + ([] if dep is None else [ANY]),
        out_specs=[ANY, ANY],
        scratch_shapes=[pltpu.VMEM(acc_shape, F32), pltpu.VMEM(stage_shape, BF16),
                        pltpu.SemaphoreType.DMA((2 * nb,))],
        compiler_params=_params(("arbitrary",)),
    )(*((xa, dy) if dep is None else (xa, dy, dep)))


def _proj_fwd(h, gain, win, wgate, name):
    t, d = h.shape
    tm = _row_tile(t, 256)
    nq, ng = win.shape[0], wgate.shape[1]

    def body(h_ref, g_ref, win_ref, wg_ref, un_ref, qkv_ref, gate_ref):
        n, _, _ = _rms(h_ref[...], g_ref[...])
        nbf = n.astype(BF16)
        un_ref[...] = nbf
        qkv_ref[...] = _dot_nt(nbf, win_ref[...])
        gate_ref[...] = jax.nn.sigmoid(_dot(nbf, wg_ref[...]))

    full = lambda a: pl.BlockSpec(a.shape, lambda i: (0,) * a.ndim)
    return pl.pallas_call(
        body, name=name, grid=(t // tm,),
        out_shape=[jax.ShapeDtypeStruct((t, d), BF16), jax.ShapeDtypeStruct((t, nq), F32),
                   jax.ShapeDtypeStruct((t, ng), F32)],
        in_specs=[pl.BlockSpec((tm, d), lambda i: (i, 0)), full(gain), full(win), full(wgate)],
        out_specs=[pl.BlockSpec((tm, d), lambda i: (i, 0)), pl.BlockSpec((tm, nq), lambda i: (i, 0)),
                   pl.BlockSpec((tm, ng), lambda i: (i, 0))],
        compiler_params=_params(("arbitrary",)),
    )(h, gain, win, wgate)


def _proj_bwd(dh, h, gain, dzg, dqkv_parts, win, wgate, name):
    t, d = h.shape
    tm = _row_tile(t, 256)
    ng = wgate.shape[1]
    np_ = len(dqkv_parts)
    widths = [a.shape[1] for a in dqkv_parts]

    def body(dh_ref, h_ref, g_ref, dzg_ref, *rest):
        part_refs, (win_ref, wg_ref, dhp_ref, dgain_ref) = rest[:np_], rest[np_:]

        @pl.when(pl.program_id(0) == 0)
        def _():
            dgain_ref[...] = jnp.zeros_like(dgain_ref)

        gain_v = g_ref[...]
        _, xh, r = _rms(h_ref[...], gain_v)
        dun = _dot_nt(dzg_ref[...], wg_ref[...])
        off = 0
        for ref, wd in zip(part_refs, widths):
            dun = dun + _dot(ref[...].astype(BF16), win_ref[off:off + wd, :])
            off += wd
        dx, dgain = _rms_bwd(xh, r, gain_v, dun)
        dhp_ref[...] = dh_ref[...] + dx
        dgain_ref[...] += dgain

    full = lambda a: pl.BlockSpec(a.shape, lambda i: (0,) * a.ndim)
    row = pl.BlockSpec((tm, d), lambda i: (i, 0))
    return pl.pallas_call(
        body, name=name, grid=(t // tm,),
        out_shape=[jax.ShapeDtypeStruct((t, d), F32), jax.ShapeDtypeStruct((1, d), F32)],
        in_specs=[row, row, full(gain), pl.BlockSpec((tm, ng), lambda i: (i, 0))]
        + [pl.BlockSpec((tm, wd), lambda i: (i, 0)) for wd in widths] + [full(win), full(wgate)],
        out_specs=[row, pl.BlockSpec((1, d), lambda i: (0, 0))],
        compiler_params=_params(("arbitrary",)),
    )(dh, h, gain, dzg, *dqkv_parts, win, wgate)


def _dw_rows(parts, dy, name):
    t, n = dy.shape
    widths = [a.shape[1] for a in parts]
    k = sum(widths)
    tt = _row_tile(t, 512)
    steps = t // tt
    np_ = len(parts)

    def body(*refs):
        part_refs, dy_ref = refs[:np_], refs[np_]
        o_hbm, ob_hbm, acc, stage, sems = refs[np_ + 1:]

        @pl.when(pl.program_id(0) == 0)
        def _():
            acc[...] = jnp.zeros_like(acc)

        dyb = dy_ref[...].astype(BF16)
        off = 0
        for ref, wd in zip(part_refs, widths):
            acc[off:off + wd, :] += _dot(ref[...].astype(BF16).T, dyb)
            off += wd

        @pl.when(pl.program_id(0) == steps - 1)
        def _():
            stage[...] = acc[...].astype(BF16)
            cps = [pltpu.make_async_copy(acc, o_hbm.at[0], sems.at[0]),
                   pltpu.make_async_copy(stage, ob_hbm.at[0], sems.at[1])]
            for cp in cps:
                cp.start()
            for cp in cps:
                cp.wait()

    return pl.pallas_call(
        body, name=name, grid=(steps,),
        out_shape=[jax.ShapeDtypeStruct((1, k, n), F32), jax.ShapeDtypeStruct((1, k, n), BF16)],
        in_specs=[pl.BlockSpec((tt, wd), lambda i: (i, 0)) for wd in widths] + [pl.BlockSpec((tt, n), lambda i: (i, 0))],
        out_specs=[ANY, ANY],
        scratch_shapes=[pltpu.VMEM((k, n), F32), pltpu.VMEM((k, n), BF16), pltpu.SemaphoreType.DMA((2,))],
        compiler_params=_params(("arbitrary",)),
    )(*parts, dy)


def _merge_fwd(h, ya, yb, gate, wpa, wpb, wout, name):
    t, d = h.shape
    tm = _row_tile(t, 256)

    def body(h_ref, ya_ref, yb_ref, ga_ref, gb_ref, wpa_ref, wpb_ref, wout_ref, out_ref, mg_ref, pa_ref, pb_ref):
        pa = _dot(ya_ref[...].astype(BF16), wpa_ref[...])
        pb = _dot(yb_ref[...].astype(BF16), wpb_ref[...])
        merged = (ga_ref[...] * pa + gb_ref[...] * pb).astype(BF16)
        pa_ref[...] = pa.astype(BF16)
        pb_ref[...] = pb.astype(BF16)
        mg_ref[...] = merged
        out_ref[...] = h_ref[...] + _dot(merged, wout_ref[...])

    full = lambda a: pl.BlockSpec(a.shape, lambda i: (0,) * a.ndim)
    row = pl.BlockSpec((tm, d), lambda i: (i, 0))
    yrow = pl.BlockSpec((tm, ya.shape[1]), lambda i: (i, 0))
    return pl.pallas_call(
        body, name=name, grid=(t // tm,),
        out_shape=[jax.ShapeDtypeStruct((t, d), F32)] + [jax.ShapeDtypeStruct((t, d), BF16)] * 3,
        in_specs=[row, yrow, yrow, pl.BlockSpec((tm, d), lambda i: (i, 0)), pl.BlockSpec((tm, d), lambda i: (i, 1)),
                  full(wpa), full(wpb), full(wout)],
        out_specs=[row] * 4,
        compiler_params=_params(("arbitrary",)),
    )(h, ya, yb, gate, gate, wpa, wpb, wout)


def _merge_bwd(dh, pa, pb, gate, wpa, wpb, wout, name):
    t, d = dh.shape
    tm = _row_tile(t, 256)
    wy = wpa.shape[0]

    def body(dh_ref, pa_ref, pb_ref, ga_ref, gb_ref, wpa_ref, wpb_ref, wout_ref,
             dpa_ref, dpb_ref, dzg_ref, dya_ref, dyb_ref):
        dm = _dot_nt(dh_ref[...].astype(BF16), wout_ref[...])
        ga, gb = ga_ref[...], gb_ref[...]
        dpa = (dm * ga).astype(BF16)
        dpb = (dm * gb).astype(BF16)
        dpa_ref[...] = dpa
        dpb_ref[...] = dpb
        dzg_ref[:, :d] = (dm * pa_ref[...].astype(F32) * ga * (1.0 - ga)).astype(BF16)
        dzg_ref[:, d:] = (dm * pb_ref[...].astype(F32) * gb * (1.0 - gb)).astype(BF16)
        dya_ref[...] = _dot_nt(dpa, wpa_ref[...])
        dyb_ref[...] = _dot_nt(dpb, wpb_ref[...])

    full = lambda a: pl.BlockSpec(a.shape, lambda i: (0,) * a.ndim)
    row = pl.BlockSpec((tm, d), lambda i: (i, 0))
    yrow = pl.BlockSpec((tm, wy), lambda i: (i, 0))
    return pl.pallas_call(
        body, name=name, grid=(t // tm,),
        out_shape=[jax.ShapeDtypeStruct((t, d), BF16), jax.ShapeDtypeStruct((t, d), BF16),
                   jax.ShapeDtypeStruct((t, 2 * d), BF16), jax.ShapeDtypeStruct((t, wy), F32),
                   jax.ShapeDtypeStruct((t, wy), F32)],
        in_specs=[row, row, row, pl.BlockSpec((tm, d), lambda i: (i, 0)), pl.BlockSpec((tm, d), lambda i: (i, 1)),
                  full(wpa), full(wpb), full(wout)],
        out_specs=[row, row, pl.BlockSpec((tm, 2 * d), lambda i: (i, 0)), yrow, yrow],
        compiler_params=_params(("arbitrary",)),
    )(dh, pa, pb, gate, gate, wpa, wpb, wout)


def _ple_loss(h, gain, p, target, wpg, wpe, name):
    t, d = h.shape
    tm = _row_tile(t, 256)
    pd = p.shape[1]

    def body(h_ref, g_ref, p_ref, t_ref, wpg_ref, wpe_ref, dh_ref, dz_ref, dpp_ref, n_ref, dgain_ref, loss_ref):
        @pl.when(pl.program_id(0) == 0)
        def _():
            dgain_ref[...] = jnp.zeros_like(dgain_ref)
            loss_ref[...] = jnp.zeros_like(loss_ref)

        x = h_ref[...]
        gain_v = g_ref[...]
        n, xh, r = _rms(x, gain_v)
        nbf = n.astype(BF16)
        n_ref[...] = nbf
        pg = jax.nn.sigmoid(_dot(nbf, wpg_ref[...]))
        pp = _dot(p_ref[...].astype(BF16), wpe_ref[...])
        err = (x + pg * pp) - t_ref[...]
        loss_ref[...] += 0.5 * jnp.sum(jnp.mean(err * err, axis=-1, keepdims=True))
        dy = err * (1.0 / d)
        dpp_ref[...] = (dy * pg).astype(BF16)
        dz = (dy * pp * pg * (1.0 - pg)).astype(BF16)
        dz_ref[...] = dz
        dn = _dot_nt(dz, wpg_ref[...])
        dx, dgain = _rms_bwd(xh, r, gain_v, dn)
        dh_ref[...] = dy + dx
        dgain_ref[...] += dgain

    full = lambda a: pl.BlockSpec(a.shape, lambda i: (0,) * a.ndim)
    row = pl.BlockSpec((tm, d), lambda i: (i, 0))
    return pl.pallas_call(
        body, name=name, grid=(t // tm,),
        out_shape=[jax.ShapeDtypeStruct((t, d), F32), jax.ShapeDtypeStruct((t, d), BF16),
                   jax.ShapeDtypeStruct((t, d), BF16), jax.ShapeDtypeStruct((t, d), BF16),
                   jax.ShapeDtypeStruct((1, d), F32), jax.ShapeDtypeStruct((8, LANES), F32)],
        in_specs=[row, full(gain), pl.BlockSpec((tm, pd), lambda i: (i, 0)), row, full(wpg), full(wpe)],
        out_specs=[row, row, row, row, pl.BlockSpec((1, d), lambda i: (0, 0)),
                   pl.BlockSpec((8, LANES), lambda i: (0, 0))],
        compiler_params=_params(("arbitrary",)),
    )(h, gain, p, target, wpg, wpe)


def _head_masks():
    lane = lax.broadcasted_iota(jnp.int32, (1, LANES), 1)
    m0 = (lane < HEAD_DIM).astype(F32)
    return m0, 1.0 - m0


def _head_mean(v, m0, m1):
    del m0, m1
    width = v.shape[-1]
    shift = HEAD_DIM.bit_length() - 1
    r = jnp.right_shift(lax.broadcasted_iota(jnp.int32, (width, width), 0), shift)
    c = jnp.right_shift(lax.broadcasted_iota(jnp.int32, (width, width), 1), shift)
    same_head = (r == c).astype(BF16)
    return _dot(v.astype(BF16), same_head) * (1.0 / HEAD_DIM)


def _head_norm(x, gain, m0, m1):
    r = lax.rsqrt(_head_mean(x * x, m0, m1) + EPS)
    xh = x * r
    return xh * gain, xh, r


def _head_norm_bwd(xh, r, gain, dy, m0, m1):
    gdy = gain * dy
    dx = r * (gdy - xh * _head_mean(xh * gdy, m0, m1))
    return dx, jnp.sum(dy * xh, axis=0, keepdims=True)


GROUP = 4
QW = GROUP * HEAD_DIM
STACK = GROUP * QTILE


def _kv_width(mode):
    return QW if mode == "A" else LANES


def _q_scratch_shape(mode, s_len):
    return (s_len, QW) if mode == "A" else (GROUP * s_len, LANES)


def _group_masks(dtype=F32):
    lane = lax.broadcasted_iota(jnp.int32, (1, QW), 1)
    return [((lane >= h * HEAD_DIM) & (lane < (h + 1) * HEAD_DIM)).astype(dtype) for h in range(GROUP)]


def _stack_heads(first_kv, x, m0, m1):
    out = []
    for half in range(GROUP // 2):
        xh = x[:, half * LANES:(half + 1) * LANES]
        a0, a1 = xh * m0, xh * m1
        r0, r1 = pltpu.roll(a0, HEAD_DIM, 1), pltpu.roll(a1, HEAD_DIM, 1)
        out += [jnp.where(first_kv, a0, r0), jnp.where(first_kv, r1, a1)]
    return out


def _unstack_heads(mode, first_kv, ts, m0, m1):
    if mode == "A":
        masks = _group_masks()
        return sum(t * mk for t, mk in zip(ts, masks))
    halves = []
    for half in range(GROUP // 2):
        t0 = jnp.where(first_kv, ts[2 * half], pltpu.roll(ts[2 * half], HEAD_DIM, 1))
        t1 = jnp.where(first_kv, pltpu.roll(ts[2 * half + 1], HEAD_DIM, 1), ts[2 * half + 1])
        halves.append(t0 * m0 + t1 * m1)
    return jnp.concatenate(halves, axis=1)


def _store_stacked(dst, i, heads):
    for half in range(2):
        rows = slice(half * QTILE, (half + 1) * QTILE)
        for h, x in enumerate(heads):
            dst[pl.ds((2 * i + half) * STACK + h * QTILE, QTILE), :] = x[rows].astype(dst.dtype)


def _load_stacked(mode, ref, m):
    if mode == "B":
        return ref[pl.ds(pl.multiple_of(m * STACK, STACK), STACK), :]
    x = ref[pl.ds(pl.multiple_of(m * QTILE, QTILE), QTILE), :]
    return jnp.concatenate([x * mk for mk in _group_masks(x.dtype)], axis=0)


def _attn_prep(mode, group, s_len, padk, q_ref, k_ref, v_ref, gq_ref, gk_ref, qs, k2, v2, do_ref=None, dos=None):
    m0, m1 = _head_masks()
    zpad = jnp.zeros((padk, k2.shape[1]), BF16)
    k2[pl.ds(0, padk), :] = zpad
    v2[pl.ds(0, padk), :] = zpad
    first_kv = group == 0
    rt = 2 * QTILE
    for i in range(s_len // rt):
        rows = pl.ds(i * rt, rt)
        qn, _, _ = _head_norm(q_ref[rows, :], gq_ref[...], m0, m1)
        kn, _, _ = _head_norm(k_ref[rows, :], gk_ref[...], m0, m1)
        qn = qn * (HEAD_DIM ** -0.5)
        if mode == "A":
            qs[rows, :] = qn.astype(BF16)
            if dos is not None:
                dos[rows, :] = do_ref[rows, :].astype(BF16)
        else:
            _store_stacked(qs, i, _stack_heads(first_kv, qn, m0, m1))
            if dos is not None:
                _store_stacked(dos, i, _stack_heads(first_kv, do_ref[rows, :], m0, m1))
        k2[pl.ds(padk + i * rt, rt), :] = kn.astype(BF16)
        v2[pl.ds(padk + i * rt, rt), :] = v_ref[rows, :].astype(BF16)


def _softmax_terms(mode, s, sink):
    mx = jnp.max(s, axis=-1, keepdims=True)
    if mode == "B":
        mx = jnp.maximum(mx, sink)
    e = jnp.exp(s - mx)
    l = jnp.sum(e, axis=-1, keepdims=True)
    if mode == "B":
        l = l + jnp.exp(sink - mx)
    return e, mx, l


def _sink_column(sink_ref, group):
    row = lax.broadcasted_iota(jnp.int32, (STACK, 1), 0)
    col = jnp.zeros((STACK, 1), F32)
    for h in range(GROUP):
        col = jnp.where((row >= h * QTILE) & (row < (h + 1) * QTILE), sink_ref[GROUP * group + h], col)
    return col


def _head_deltas(dd, m0, m1):
    cols = []
    for half in range(GROUP // 2):
        dh = dd[:, half * LANES:(half + 1) * LANES]
        cols += [jnp.sum(dh * m0, axis=-1, keepdims=True), jnp.sum(dh * m1, axis=-1, keepdims=True)]
    return jnp.concatenate(cols, axis=0)


def _attn_cols(mode):
    if mode == "A":
        return (lambda b, g: (b, g)), (lambda b, g: (b, 2 + g)), (lambda b, g: (b, 4 + g))
    return (lambda b, g: (b, 6 + g)), (lambda b, g: (b, 16)), (lambda b, g: (b, 17))


def _attn_fwd(mode, qkv, gq, gk, bias, sinks, bl, s_len, name):
    bw = bias.shape[-1]
    padk = bw - QTILE
    nt = s_len // QTILE
    qmap, kmap, vmap = _attn_cols(mode)

    kw = _kv_width(mode)

    def body(q_ref, k_ref, v_ref, gq_ref, gk_ref, bias_ref, sink_ref, o_ref, qs, k2, v2, s_buf, *rest):
        o_buf = rest[0] if rest else None
        group = pl.program_id(1)
        m0, m1 = _head_masks()
        first_kv = group == 0
        _attn_prep(mode, group, s_len, padk, q_ref, k_ref, v_ref, gq_ref, gk_ref, qs, k2, v2)
        col = lax.broadcasted_iota(jnp.int32, (STACK, bw), 1)
        sink = _sink_column(sink_ref, group)

        def scores(m, slot):
            r0 = pl.multiple_of(m * QTILE, QTILE)
            s = _dot_nt(_load_stacked(mode, qs, m), k2[pl.ds(r0, bw), :]) + bias_ref[...]
            s_buf[slot] = jnp.where(col >= (padk - r0), s, NEG_INF)

        def finish_tile(m, slot):
            r0 = pl.multiple_of(m * QTILE, QTILE)
            e, _, l = _softmax_terms(mode, s_buf[slot], sink)
            if mode == "A":
                o_st = _dot(e.astype(BF16), v2[pl.ds(r0, bw), :]) / l
                heads = [o_st[h * QTILE:(h + 1) * QTILE] for h in range(GROUP)]
                o_ref[pl.ds(r0, QTILE), :] = _unstack_heads(mode, first_kv, heads, m0, m1)
            else:
                o_buf[pl.ds(pl.multiple_of(m * STACK, STACK), STACK), :] = _dot((e * (1.0 / l)).astype(BF16),
                                                                                 v2[pl.ds(r0, bw), :])

        scores(0, 0)

        def pair(j, carry):
            scores(2 * j + 1, 1)
            finish_tile(2 * j, 0)
            scores(jnp.minimum(2 * j + 2, nt - 1), 0)
            finish_tile(2 * j + 1, 1)
            return carry

        lax.fori_loop(0, nt // 2, pair, 0)
        if mode == "B":
            for m in range(nt):
                heads = [o_buf[pl.ds(m * STACK + h * QTILE, QTILE), :] for h in range(GROUP)]
                o_ref[pl.ds(m * QTILE, QTILE), :] = _unstack_heads(mode, first_kv, heads, m0, m1)

    blk = lambda w, f: pl.BlockSpec((s_len, w), f)
    return pl.pallas_call(
        body, name=name, grid=(bl, B_Q_HEADS // GROUP),
        out_shape=jax.ShapeDtypeStruct((bl * s_len, B_Q_HEADS * HEAD_DIM), F32),
        in_specs=[blk(QW, qmap), blk(kw, kmap), blk(kw, vmap),
                  pl.BlockSpec((1, QW), lambda b, g: (0, 0)), pl.BlockSpec((1, kw), lambda b, g: (0, 0)),
                  pl.BlockSpec((STACK, bw), lambda b, g: (g, 0)),
                  pl.BlockSpec(memory_space=pltpu.SMEM)],
        out_specs=blk(QW, lambda b, g: (b, g)),
        scratch_shapes=[pltpu.VMEM(_q_scratch_shape(mode, s_len), BF16)] + [pltpu.VMEM((s_len + padk, kw), BF16)] * 2
        + [pltpu.VMEM((2, STACK, bw), F32)] + ([pltpu.VMEM((GROUP * s_len, LANES), F32)] if mode == "B" else []),
        compiler_params=_params(("arbitrary", "arbitrary")),
    )(qkv, qkv, qkv, gq, gk, bias.reshape(-1, bw), sinks)


def _attn_bwd(mode, qkv, gq, gk, bias, sinks, y, dy, bl, s_len, name):
    bw = bias.shape[-1]
    padk = bw - QTILE
    nt = s_len // QTILE
    qmap, kmap, vmap = _attn_cols(mode)
    t = bl * s_len
    kw = _kv_width(mode)
    kvw = 4 * LANES if mode == "A" else LANES
    dp_ahead = True

    def body(q_ref, k_ref, v_ref, gq_ref, gk_ref, bias_ref, sink_ref, y_ref, dy_ref,
             dq_ref, dk_ref, dv_ref, dgq_ref, dgk_ref, dbias_ref, dsink_ref,
             qs, k2, v2, dos, dqs, dk, dv, s_buf, dp_buf):
        group = pl.program_id(1)
        m0, m1 = _head_masks()
        first_kv = group == 0
        _attn_prep(mode, group, s_len, padk, q_ref, k_ref, v_ref, gq_ref, gk_ref, qs, k2, v2, dy_ref, dos)
        dk[...] = jnp.zeros_like(dk)
        dv[...] = jnp.zeros_like(dv)
        dbias_ref[...] = jnp.zeros_like(dbias_ref)
        col = lax.broadcasted_iota(jnp.int32, (STACK, bw), 1)
        lane8 = lax.broadcasted_iota(jnp.int32, (8, LANES), 1)
        sink = _sink_column(sink_ref, group)

        def ahead(m, slot):
            r0 = pl.multiple_of(m * QTILE, QTILE)
            band = pl.ds(r0, bw)
            s = _dot_nt(_load_stacked(mode, qs, m), k2[band, :]) + bias_ref[...]
            s_buf[slot] = jnp.where(col >= (padk - r0), s, NEG_INF)
            if dp_ahead:
                dp_buf[slot] = _dot_nt(_load_stacked(mode, dos, m), v2[band, :])

        def tile(m, slot, dsink):
            r0 = pl.multiple_of(m * QTILE, QTILE)
            rows = pl.ds(r0, QTILE)
            band = pl.ds(r0, bw)
            q_st = _load_stacked(mode, qs, m)
            do_st = _load_stacked(mode, dos, m)
            delta = _head_deltas(dy_ref[rows, :] * y_ref[rows, :], m0, m1)
            kb = k2[band, :]
            e, mx, l = _softmax_terms(mode, s_buf[slot], sink)
            inv = 1.0 / l
            pn = e * inv
            ds = pn * ((dp_buf[slot] if dp_ahead else _dot_nt(do_st, v2[band, :])) - delta)
            if mode == "A":
                dbias_ref[...] += ds
            else:
                part = jnp.exp(sink - mx) * inv * delta
                for h in range(GROUP):
                    dsink = dsink - jnp.where(lane8 == h, jnp.sum(part[h * QTILE:(h + 1) * QTILE]), 0.0)
            dsb = ds.astype(BF16)
            dv[band, :] += _dot_tn(pn.astype(BF16), do_st)
            dk[band, :] += _dot_tn(dsb, q_st)
            dq_st = _dot(dsb, kb)
            if mode == "A":
                heads = [dq_st[h * QTILE:(h + 1) * QTILE] for h in range(GROUP)]
                dq_ref[rows, :] = _unstack_heads(mode, first_kv, heads, m0, m1)
            else:
                dqs[pl.ds(pl.multiple_of(m * STACK, STACK), STACK), :] = dq_st
            return dsink

        ahead(0, 0)

        def pair(j, dsink):
            ahead(2 * j + 1, 1)
            dsink = tile(2 * j, 0, dsink)
            ahead(jnp.minimum(2 * j + 2, nt - 1), 0)
            return tile(2 * j + 1, 1, dsink)

        dsink = lax.fori_loop(0, nt // 2, pair, jnp.zeros((8, LANES), F32))
        dsink_ref[...] = dsink

        rt = 2 * QTILE
        dgq = jnp.zeros((1, QW), F32)
        dgk = jnp.zeros((1, kw), F32)
        for i in range(s_len // rt):
            rows = pl.ds(i * rt, rt)
            src = pl.ds(padk + i * rt, rt)
            gq_v, gk_v = gq_ref[...], gk_ref[...]
            _, qh, qr = _head_norm(q_ref[rows, :], gq_v, m0, m1)
            _, kh, kr = _head_norm(k_ref[rows, :], gk_v, m0, m1)
            if mode == "A":
                dqn = dq_ref[rows, :] * (HEAD_DIM ** -0.5)
            else:
                dqn = jnp.concatenate(
                    [_unstack_heads(mode, first_kv, [dqs[pl.ds((2 * i + half) * STACK + h * QTILE, QTILE), :]
                                                     for h in range(GROUP)], m0, m1)
                     for half in range(2)], axis=0) * (HEAD_DIM ** -0.5)
            dq_raw, dgq_i = _head_norm_bwd(qh, qr, gq_v, dqn, m0, m1)
            dk_raw, dgk_i = _head_norm_bwd(kh, kr, gk_v, dk[src, :], m0, m1)
            dvn = dv[src, :]
            dq_ref[rows, :] = dq_raw
            if mode == "A":
                dk_ref[rows, :] = dk_raw
                dv_ref[rows, :] = dvn
            else:
                @pl.when(group == 0)
                def _():
                    dk_ref[rows, :] = dk_raw
                    dv_ref[rows, :] = dvn

                @pl.when(group != 0)
                def _():
                    dk_ref[rows, :] += dk_raw
                    dv_ref[rows, :] += dvn
            dgq, dgk = dgq + dgq_i, dgk + dgk_i
        dgq_ref[...] = jnp.broadcast_to(dgq, (8, QW))
        dgk_ref[...] = jnp.broadcast_to(dgk, (8, kw))

    ng = B_Q_HEADS // GROUP
    blk = lambda w, f: pl.BlockSpec((s_len, w), f)
    small = lambda w: pl.BlockSpec((None, None, 8, w), lambda b, g: (b, g, 0, 0))
    own = lambda b, g: (b, g)
    kvmap = own if mode == "A" else (lambda b, g: (b, 0))
    pad_f32 = pltpu.VMEM((s_len + padk, kw), F32)
    pad_bf = pltpu.VMEM((s_len + padk, kw), BF16)
    stack_bf = pltpu.VMEM(_q_scratch_shape(mode, s_len), BF16)
    outs = pl.pallas_call(
        body, name=name, grid=(bl, ng),
        out_shape=[jax.ShapeDtypeStruct((t, ng * QW), F32), jax.ShapeDtypeStruct((t, kvw), F32),
                   jax.ShapeDtypeStruct((t, kvw), F32),
                   jax.ShapeDtypeStruct((bl, ng, 8, QW), F32), jax.ShapeDtypeStruct((bl, ng, 8, kw), F32),
                   jax.ShapeDtypeStruct((bl, ng * STACK, bw), F32), jax.ShapeDtypeStruct((bl, ng, 8, LANES), F32)],
        in_specs=[blk(QW, qmap), blk(kw, kmap), blk(kw, vmap),
                  pl.BlockSpec((1, QW), lambda b, g: (0, 0)), pl.BlockSpec((1, kw), lambda b, g: (0, 0)),
                  pl.BlockSpec((STACK, bw), lambda b, g: (g, 0)),
                  pl.BlockSpec(memory_space=pltpu.SMEM),
                  blk(QW, own), blk(QW, own)],
        out_specs=[blk(QW, own), blk(kw, kvmap), blk(kw, kvmap), small(QW), small(kw),
                   pl.BlockSpec((None, STACK, bw), lambda b, g: (b, g, 0)), small(LANES)],
        scratch_shapes=[stack_bf, pad_bf, pad_bf, stack_bf,
                        pltpu.VMEM((8, LANES) if mode == "A" else _q_scratch_shape(mode, s_len), F32),
                        pad_f32, pad_f32, pltpu.VMEM((2, STACK, bw), F32),
                        pltpu.VMEM((2, STACK, bw) if dp_ahead else (8, LANES), F32)],
        compiler_params=_params(("arbitrary", "arbitrary")),
    )(qkv, qkv, qkv, gq, gk, bias.reshape(-1, bw), sinks, y, dy)
    outs = list(outs)
    outs[5] = outs[5].reshape(bl, B_Q_HEADS, QTILE, bw)
    return outs


def _band_geometry(prev):
    bw = QTILE + prev * CHUNK
    i = np.arange(QTILE)[:, None]
    j = np.arange(bw)[None, :]
    dist = i + prev * CHUNK - j
    valid = (j // CHUNK >= i // CHUNK) & (j // CHUNK <= i // CHUNK + prev)
    return dist, valid


A_VAR0 = (A_PREV * CHUNK - A_MAX_REL) // LANES * LANES


A_NVAR = QTILE + A_PREV * CHUNK - A_VAR0


def _skew_rows(x, sign):
    rows, n = x.shape
    row = lax.broadcasted_iota(jnp.int32, x.shape, 0)
    b = 1
    while b < rows:
        x = jnp.where((row & b) != 0, pltpu.roll(x, (sign * b) % n, 1), x)
        b *= 2
    return x


def _rel_bias_expand(table, name):
    _, valid = _band_geometry(A_PREV)
    bw = valid.shape[1]
    valid_f = jnp.asarray(valid.astype(np.float32))
    rev = jnp.flip(table[:, 1:], axis=1).reshape(A_HEADS, 1, A_NVAR)

    def body(rev_ref, valid_ref, o_ref):
        rowv = jnp.broadcast_to(rev_ref[...], (QTILE, A_NVAR))
        top = rowv[:, 0:1]
        var = _skew_rows(rowv, 1)
        row = lax.broadcasted_iota(jnp.int32, (QTILE, A_NVAR), 0)
        colv = lax.broadcasted_iota(jnp.int32, (QTILE, A_NVAR), 1)
        var = jnp.where(colv < row, top, var)
        ok = valid_ref[...] > 0.5
        o_ref[:, :A_VAR0] = jnp.where(ok[:, :A_VAR0], top, NEG_INF)
        o_ref[:, A_VAR0:] = jnp.where(ok[:, A_VAR0:], var, NEG_INF)

    return pl.pallas_call(
        body, name=name, grid=(A_HEADS,),
        out_shape=jax.ShapeDtypeStruct((A_HEADS, QTILE, bw), F32),
        in_specs=[pl.BlockSpec((None, 1, A_NVAR), lambda h: (h, 0, 0)), pl.BlockSpec((QTILE, bw), lambda h: (0, 0))],
        out_specs=pl.BlockSpec((None, QTILE, bw), lambda h: (h, 0, 0)),
        compiler_params=_params(("arbitrary",)),
    )(rev, valid_f)


def _rel_bias_grad(dbias, name):
    bl = dbias.shape[0]
    bw = dbias.shape[-1]

    def body(db_ref, o_ref):
        g = db_ref[0]
        for b in range(1, bl):
            g = g + db_ref[b]
        sk = _skew_rows(g[:, A_VAR0:], -1)
        row = lax.broadcasted_iota(jnp.int32, (QTILE, A_NVAR), 0)
        colv = lax.broadcasted_iota(jnp.int32, (QTILE, A_NVAR), 1)
        wrapped = (row + colv) >= A_NVAR
        main = jnp.sum(jnp.where(wrapped, 0.0, sk), axis=0, keepdims=True)
        top = jnp.sum(g[:, :A_VAR0]) + jnp.sum(jnp.where(wrapped, sk, 0.0))
        o_ref[:, :A_NVAR] = jnp.broadcast_to(main, (8, A_NVAR))
        o_ref[:, A_NVAR:] = jnp.full((8, LANES), top, F32)

    out = pl.pallas_call(
        body, name=name, grid=(A_HEADS,),
        out_shape=jax.ShapeDtypeStruct((A_HEADS, 8, A_NVAR + LANES), F32),
        in_specs=[pl.BlockSpec((bl, None, QTILE, bw), lambda h: (0, h, 0, 0))],
        out_specs=pl.BlockSpec((None, 8, A_NVAR + LANES), lambda h: (h, 0, 0)),
        compiler_params=_params(("arbitrary",)),
    )(dbias)
    main, top = out[:, 0, :A_NVAR], out[:, 0, A_NVAR]
    fm = jnp.flip(main, axis=1)
    return jnp.concatenate([jnp.zeros((A_HEADS, 1), F32), fm[:, :-1], fm[:, -1:] + top[:, None]], axis=1)


def _alibi_bias():
    dist, valid = _band_geometry(B_PREV)
    slopes = np.array([2.0 ** (-8.0 * (h + 1) / B_Q_HEADS) for h in range(B_Q_HEADS)], dtype=np.float32)
    bias = -slopes[:, None, None] * np.abs(dist).astype(np.float32)[None]
    return jnp.asarray(np.where(valid[None], bias, np.float32(NEG_INF)).astype(np.float32))


SMALL_NAMES = ("ffn1_norm", "mix_norm", "ffn2_norm", "ple_norm", "a_q_norm", "a_k_norm", "b_q_norm", "b_k_norm",
               "a_rel_bias", "b_sinks", "loss")


def _pack_small(vals):
    rows = []
    for nme in SMALL_NAMES:
        v = vals[nme].astype(F32)
        if nme == "a_rel_bias":
            v = jnp.pad(v.reshape(A_HEADS, -1), ((0, 0), (0, 3 * LANES - (2 * A_MAX_REL + 1))))
        v = v.reshape(-1)
        v = jnp.pad(v, (0, (-v.shape[0]) % LANES))
        rows.append(v.reshape(-1, LANES))
    out = jnp.concatenate(rows, axis=0)
    return jnp.pad(out, ((0, (-out.shape[0]) % 8), (0, 0)))


def _unpack_small(packed, shapes):
    out, r = {}, 0
    for nme in SMALL_NAMES:
        shp = shapes[nme]
        if nme == "a_rel_bias":
            nr = A_HEADS * 3
            out[nme] = packed[r:r + nr].reshape(A_HEADS, 3 * LANES)[:, :2 * A_MAX_REL + 1].reshape(shp)
        else:
            size = int(np.prod(shp)) if shp else 1
            nr = -(-size // LANES)
            out[nme] = packed[r:r + nr].reshape(-1)[:size].reshape(shp)
        r += nr
    return out


BIG_NAMES = ("ffn1_w_gu", "ffn1_w_down", "w_in", "w_gate", "w_proj_a", "w_proj_b", "w_out",
             "ffn2_w_gu", "ffn2_w_down", "w_ple_gate", "w_ple_proj")
ROW_SHARDED = ("ffn1_w_down", "ffn2_w_down", "w_out", "w_ple_gate")
WEIGHT_ORDER = ("ffn1_norm", "ffn1_w_gu", "ffn1_w_down", "mix_norm", "w_in", "a_q_norm", "a_k_norm", "a_rel_bias",
                "b_q_norm", "b_k_norm", "b_sinks", "w_gate", "w_proj_a", "w_proj_b", "w_out", "ffn2_norm",
                "ffn2_w_gu", "ffn2_w_down", "ple_norm", "w_ple_gate", "w_ple_proj")


TRANSPOSED = ("ffn1_w_gu", "ffn2_w_gu", "w_in")


def _local(a, nme):
    return a[0].T if nme in TRANSPOSED else a[0]


def _full_cols(wg):
    nb, k, n = wg.shape
    return jnp.transpose(wg, (1, 0, 2)).reshape(k, nb * n)


def _col_blocks(g, nb):
    k, n = g.shape
    return jnp.transpose(g.reshape(k, nb, n // nb), (1, 0, 2))


def _step(x, p, target, w, m, v):
    bl, s_len, d = x.shape
    t = bl * s_len
    h0 = x.reshape(t, d)
    pt = p.reshape(t, p.shape[-1])
    tgt = target.reshape(t, d)

    g_ffn1, g_mix, g_ffn2, g_ple = w["ffn1_norm"], w["mix_norm"], w["ffn2_norm"], w["ple_norm"]
    tiled = lambda a, width: jnp.tile(a.reshape(1, HEAD_DIM), (1, width // HEAD_DIM))
    gqa, gka = tiled(w["a_q_norm"], QW), tiled(w["a_k_norm"], _kv_width("A"))
    gqb, gkb = tiled(w["b_q_norm"], QW), tiled(w["b_k_norm"], _kv_width("B"))
    sinks = w["b_sinks"].reshape(B_Q_HEADS)
    bias_b = _alibi_bias()

    ffn1_names = ("ffn1_w_gu", "ffn1_w_down")
    shard = {nme: _local(w[nme], nme).astype(BF16) for nme in ffn1_names}
    send1, recv1, bufs, token = _gather_start([shard[nme] for nme in ffn1_names], h0, "gather_start_ffn1")
    zero = token[0, 0]
    shard.update({nme: (_local(w[nme], nme) + zero).astype(BF16) for nme in BIG_NAMES if nme not in ffn1_names})
    bias_a = _rel_bias_expand(w["a_rel_bias"][0] + zero, "rel_bias_expand")
    send2, recv2, bufs, token = _gather_pass(send1, recv1, bufs, bias_a, "gather_pass_ffn1")
    wgu1, wd1 = _gather_wait(send2, recv2, bufs, shard["ffn2_w_gu"], "gather_wait_ffn1")
    nf = wgu1.shape[1]
    wd1 = wd1.reshape(N_DEV // 2, nf, d)
    mixer_names = ("w_in", "w_gate")
    rest_names = ("w_proj_a", "w_proj_b", "w_out", "ffn2_w_gu", "ffn2_w_down", "w_ple_gate", "w_ple_proj")
    send1, recv1, bufs, token = _gather_start([shard[nme] for nme in mixer_names], wgu1, "gather_start_mixer")
    rsend1, rrecv1, rest_bufs, token = _gather_start([shard[nme] for nme in rest_names], token, "gather_start_rest")

    h1, gu1 = _ffn_fwd(h0, g_ffn1 + token[0, 0], wgu1, wd1, "ffn1_fwd")
    send2, recv2, bufs, token = _gather_pass(send1, recv1, bufs, h1, "gather_pass_mixer")
    win, wgate = _gather_wait(send2, recv2, bufs, token, "gather_wait_mixer")
    win, wgate = win.reshape(IN_COLS, d), _full_cols(wgate)
    un, qkv, gate = _proj_fwd(h1, g_mix, win, wgate, "proj_fwd")
    ya = _attn_fwd("A", qkv, gqa, gka, bias_a, sinks, bl, s_len, "attn_a_fwd")
    rsend2, rrecv2, rest_bufs, token = _gather_pass(rsend1, rrecv1, rest_bufs, ya, "gather_pass_rest")
    yb = _attn_fwd("B", qkv, gqb + token[0, 0], gkb, bias_b, sinks, bl, s_len, "attn_b_fwd")
    gathered = dict(zip(rest_names, _gather_wait(rsend2, rrecv2, rest_bufs, yb, "gather_wait_rest")))
    wgu2 = gathered["ffn2_w_gu"]
    wd2 = gathered["ffn2_w_down"].reshape(N_DEV // 2, nf, d)
    wpa = _full_cols(gathered["w_proj_a"])
    wpb = _full_cols(gathered["w_proj_b"])
    wpe = _full_cols(gathered["w_ple_proj"])
    wout = gathered["w_out"].reshape(d, d)
    wpg = gathered["w_ple_gate"].reshape(d, d)
    h2, merged, pa, pb = _merge_fwd(h1, ya, yb, gate, wpa, wpb, wout, "merge_fwd")
    h3, gu2 = _ffn_fwd(h2, g_ffn2, wgu2, wd2, "ffn2_fwd")
    dh3, dz4, dpp, n4, dg_ple, loss_part = _ple_loss(h3, g_ple, pt, tgt, wpg, wpe, "ple_loss")

    xi, yi, ci = _place()
    me = jnp.stack([4 * xi + 2 * yi + ci]).astype(jnp.int32)
    g32, g16, big = {}, {}, {}

    def keep(nme, pair, rows=None):
        for store, g in zip((g32, g16), pair):
            store[nme] = g if rows is None else g.reshape(N_DEV, rows, d)

    def start(names, after, tag):
        send, recv, parts, lands, token = _scatter_start([g16[nme] for nme in names], after, "grads_start_" + tag)
        return names, send, recv, parts, lands, token

    def finish(state, after, tag):
        names, send, recv, parts, lands, _ = state
        lands = _scatter_wait(send, recv, parts, lands, after, "grads_wait_" + tag)
        return names, lands

    def adam(done, dep):
        for nme, land in zip(*done):
            outs = _final_adam(g32[nme], land, _local(w[nme], nme), _local(m[nme], nme), _local(v[nme], nme), me, dep,
                               "adam_" + nme)
            big[nme] = [(o.T if nme in TRANSPOSED else o)[None] for o in outs]

    keep("w_ple_gate", _dw(n4, dz4, 1, d, "dw_ple_gate"), d // N_DEV)
    keep("w_ple_proj", _dw(pt, dpp, N_DEV, d // N_DEV, "dw_ple_proj"))
    early = [(start(("w_ple_gate", "w_ple_proj"), dh3, "ple"), "ple")]

    dh2, dgu2, a2, n3, dg_ffn2 = _ffn_bwd(dh3, h2, g_ffn2 + early[-1][0][-1][0, 0], gu2, wgu2, wd2, "ffn2_bwd")
    keep("ffn2_w_down", _dw(a2, dh3, N_DEV // 2, d, "dw_ffn2_down", 0.5), nf // 2)
    early.append((start(("ffn2_w_down",), dh2, "ffn2_down"), "ffn2_down"))
    keep("ffn2_w_gu", _dw(dgu2, n3, N_DEV, d, "dw_ffn2_gu", dep=early[-1][0][-1]))
    flight = start(("ffn2_w_gu",), dh2, "ffn2")

    dpa, dpb, dzg, dya, dyb = _merge_bwd(dh2, pa, pb, gate, wpa, wpb, wout, "merge_bwd")
    keep("w_out", _dw(merged, dh2, 1, d, "dw_out"), d // N_DEV)
    keep("w_proj_a", _dw(ya, dpa, N_DEV, d // N_DEV, "dw_proj_a"))
    keep("w_proj_b", _dw(yb, dpb, N_DEV, d // N_DEV, "dw_proj_b"))
    keep("w_gate", _dw(un, dzg, N_DEV, 2 * d // N_DEV, "dw_gate"))

    tok = flight[-1][0, 0]
    dqa, dka, dva, dgqa, dgka, dbias, _ = _attn_bwd("A", qkv, gqa + tok, gka, bias_a, sinks, ya, dya, bl, s_len,
                                                     "attn_a_bwd")
    dqb, dkb, dvb, dgqb, dgkb, _, dsink = _attn_bwd("B", qkv, gqb, gkb, bias_b, sinks, yb, dyb, bl, s_len, "attn_b_bwd")
    dqkv = [dqa, dka, dva, dqb, dkb, dvb]
    dtab = _rel_bias_grad(dbias, "rel_bias_grad")

    dh1, dg_mix = _proj_bwd(dh2, h1, g_mix, dzg, dqkv, win, wgate, "proj_bwd")
    keep("w_in", _dw_rows(dqkv, un, "dw_in"), IN_COLS // N_DEV)
    waiting = [finish(state, g32["w_in"], tag) for state, tag in early]
    done = finish(flight, waiting[-1][1][0], "ffn2")
    flight = start(("w_out", "w_proj_a", "w_proj_b", "w_gate", "w_in"), done[1][0], "mixer")
    waiting.append(done)

    dh0, dgu1, a1, n1, dg_ffn1 = _ffn_bwd(dh1, h0, g_ffn1 + flight[-1][0, 0], gu1, wgu1, wd1, "ffn1_bwd")
    keep("ffn1_w_down", _dw(a1, dh1, N_DEV // 2, d, "dw_ffn1_down", 0.5), nf // 2)
    done = finish(flight, g32["ffn1_w_down"], "mixer")
    flight = start(("ffn1_w_down",), done[1][0], "ffn1_down")
    waiting.append(done)

    keep("ffn1_w_gu", _dw(dgu1, n1, N_DEV, d, "dw_ffn1_gu", dep=flight[-1]))
    done = finish(flight, g32["ffn1_w_gu"], "ffn1_down")
    flight = start(("ffn1_w_gu",), done[1][0], "ffn1_gu")
    for group in waiting + [done]:
        adam(group, flight[-1])
    behind = 0.0 * big["ffn1_w_down"][0][0, 0, :1]
    smalls = (dg_ffn1, dg_mix, dg_ffn2, dg_ple + behind, dgqa, dgka, dgqb, dgkb, dtab, dsink)
    return dh0, loss_part, big, smalls, flight, finish, adam


def kernel(x, p, ffn1_norm, ffn1_w_gu, ffn1_w_down, mix_norm, w_in, a_q_norm, a_k_norm, a_rel_bias, b_q_norm, b_k_norm, b_sinks, w_gate, w_proj_a, w_proj_b, w_out, ffn2_norm, ffn2_w_gu, ffn2_w_down, ple_norm, w_ple_gate, w_ple_proj, loss_target, m_ffn1_norm, m_ffn1_w_gu, m_ffn1_w_down, m_mix_norm, m_w_in, m_a_q_norm, m_a_k_norm, m_a_rel_bias, m_b_q_norm, m_b_k_norm, m_b_sinks, m_w_gate, m_w_proj_a, m_w_proj_b, m_w_out, m_ffn2_norm, m_ffn2_w_gu, m_ffn2_w_down, m_ple_norm, m_w_ple_gate, m_w_ple_proj, v_ffn1_norm, v_ffn1_w_gu, v_ffn1_w_down, v_mix_norm, v_w_in, v_a_q_norm, v_a_k_norm, v_a_rel_bias, v_b_q_norm, v_b_k_norm, v_b_sinks, v_w_gate, v_w_proj_a, v_w_proj_b, v_w_out, v_ffn2_norm, v_ffn2_w_gu, v_ffn2_w_down, v_ple_norm, v_w_ple_gate, v_w_ple_proj):
    w = dict(ffn1_norm=ffn1_norm, ffn1_w_gu=ffn1_w_gu, ffn1_w_down=ffn1_w_down, mix_norm=mix_norm, w_in=w_in,
             a_q_norm=a_q_norm, a_k_norm=a_k_norm, a_rel_bias=a_rel_bias, b_q_norm=b_q_norm, b_k_norm=b_k_norm,
             b_sinks=b_sinks, w_gate=w_gate, w_proj_a=w_proj_a, w_proj_b=w_proj_b, w_out=w_out, ffn2_norm=ffn2_norm,
             ffn2_w_gu=ffn2_w_gu, ffn2_w_down=ffn2_w_down, ple_norm=ple_norm, w_ple_gate=w_ple_gate,
             w_ple_proj=w_ple_proj)
    m = dict(ffn1_norm=m_ffn1_norm, ffn1_w_gu=m_ffn1_w_gu, ffn1_w_down=m_ffn1_w_down, mix_norm=m_mix_norm,
             w_in=m_w_in, a_q_norm=m_a_q_norm, a_k_norm=m_a_k_norm, a_rel_bias=m_a_rel_bias, b_q_norm=m_b_q_norm,
             b_k_norm=m_b_k_norm, b_sinks=m_b_sinks, w_gate=m_w_gate, w_proj_a=m_w_proj_a, w_proj_b=m_w_proj_b,
             w_out=m_w_out, ffn2_norm=m_ffn2_norm, ffn2_w_gu=m_ffn2_w_gu, ffn2_w_down=m_ffn2_w_down,
             ple_norm=m_ple_norm, w_ple_gate=m_w_ple_gate, w_ple_proj=m_w_ple_proj)
    v = dict(ffn1_norm=v_ffn1_norm, ffn1_w_gu=v_ffn1_w_gu, ffn1_w_down=v_ffn1_w_down, mix_norm=v_mix_norm,
             w_in=v_w_in, a_q_norm=v_a_q_norm, a_k_norm=v_a_k_norm, a_rel_bias=v_a_rel_bias, b_q_norm=v_b_q_norm,
             b_k_norm=v_b_k_norm, b_sinks=v_b_sinks, w_gate=v_w_gate, w_proj_a=v_w_proj_a, w_proj_b=v_w_proj_b,
             w_out=v_w_out, ffn2_norm=v_ffn2_norm, ffn2_w_gu=v_ffn2_w_gu, ffn2_w_down=v_ffn2_w_down,
             ple_norm=v_ple_norm, w_ple_gate=v_w_ple_gate, w_ple_proj=v_w_ple_proj)
    bl, s_len, d = x.shape

    dh0, loss_part, big, smalls, flight, finish, adam = _step(x, p[0], loss_target, w, m, v)
    dg_ffn1, dg_mix, dg_ffn2, dg_ple, dgqa, dgka, dgqb, dgkb, dtab, dsink = smalls

    fold = lambda a: a[:, :, 0, :].reshape(-1, HEAD_DIM).sum(axis=0)
    small_part = dict(
        ffn1_norm=dg_ffn1, mix_norm=dg_mix, ffn2_norm=dg_ffn2, ple_norm=dg_ple,
        a_q_norm=fold(dgqa), a_k_norm=fold(dgka), b_q_norm=fold(dgqb), b_k_norm=fold(dgkb),
        a_rel_bias=dtab,
        b_sinks=dsink.sum(axis=0)[:, 0, :GROUP].reshape(B_Q_HEADS),
        loss=loss_part[0, :1])
    zero1 = jnp.zeros((1,), F32)
    shapes = {nme: w[nme].shape for nme in SMALL_NAMES if nme != "loss"}
    shapes["loss"] = ()
    pk = lambda src: _pack_small({**{nme: src[nme] for nme in SMALL_NAMES if nme != "loss"}, "loss": zero1})
    sg, sd, sm, sv = _small_allreduce_adam(_pack_small(small_part), pk(w), pk(m), pk(v), "small_allreduce_adam")
    adam(finish(flight, sg, "ffn1_gu"), sg)
    sg, sd, sm, sv = (_unpack_small(a, shapes) for a in (sg, sd, sm, sv))

    def pick(i):
        out = []
        for nme in WEIGHT_ORDER:
            out.append(big[nme][i] if nme in big else (sg, sd, sm, sv)[i][nme])
        return out

    return (sg["loss"], dh0.reshape(bl, s_len, d), *pick(0), *pick(1), *pick(2), *pick(3))
```

```python
import jax
import jax.numpy as jnp
import numpy as np
from jax import lax
from jax.experimental import pallas as pl
from jax.experimental.pallas import tpu as pltpu

F32 = jnp.float32
BF16 = jnp.bfloat16

CHUNK = 64
HEAD_DIM = 64
A_HEADS = 8
A_PREV = 8
A_MAX_REL = 128
B_Q_HEADS = 8
B_KV_HEADS = 2
B_PREV = 2
A_WIDTH = A_HEADS * HEAD_DIM
B_Q_WIDTH = B_Q_HEADS * HEAD_DIM
B_KV_WIDTH = B_KV_HEADS * HEAD_DIM
IN_COLS = 3 * A_WIDTH + B_Q_WIDTH + 2 * B_KV_WIDTH
EPS = 1e-6
NEG_INF = -1e30
ADAM_LR = 0.001
ADAM_B1 = 0.9
ADAM_B2 = 0.999
ADAM_EPS = 1e-08
ADAM_WD = 0.01
ADAM_STEP = 10

N_DEV = 8
LANES = 128
QTILE = 2 * CHUNK
VMEM_LIMIT = 56 * 1024 * 1024
ADAM_TILE_ELEMS = 256 * 1024

MESH_ID = pl.DeviceIdType.MESH
ANY = pl.BlockSpec(memory_space=pl.ANY)
HBM = pl.BlockSpec(memory_space=pltpu.HBM)
SEM = pl.BlockSpec(memory_space=pltpu.SEMAPHORE)
SIDE_EFFECT = pltpu.SideEffectType.DATAFLOW_SIDE_EFFECTING


def _dot(a, b):
    return jnp.dot(a, b, preferred_element_type=F32)


def _dot_nt(a, b):
    return lax.dot_general(a, b, (((1,), (1,)), ((), ())), preferred_element_type=F32)


def _dot_tn(a, b):
    return lax.dot_general(a, b, (((0,), (0,)), ((), ())), preferred_element_type=F32)


def _params(sem=None, vmem=VMEM_LIMIT):
    return pltpu.CompilerParams(dimension_semantics=sem, vmem_limit_bytes=vmem)


def _row_tile(t, want):
    while t % want:
        want //= 2
    return want


def _place():
    return lax.axis_index("x"), lax.axis_index("y"), lax.axis_index("c")


def _gather_level(bufs, send_sems, recv_sems, level, shards=None):
    x, y, c = _place()
    me, sib = (x, y, c), (x, y, 1 - c)
    chips = [(1 - x, y), (x, 1 - y), (1 - x, 1 - y)]

    def copy(w, k, block, to):
        px, py, pc = block
        rows = bufs[w].at[4 * px + 2 * py + pc]
        src = shards[w] if shards is not None and block is me else rows
        return pltpu.make_async_remote_copy(src_ref=src, dst_ref=rows, send_sem=send_sems.at[k], recv_sem=recv_sems.at[k],
                                            device_id=to, device_id_type=MESH_ID)

    n = len(bufs)
    own = []
    if level == 1:
        own = [pltpu.make_async_copy(bufs[w].at[4 * x + 2 * y + c] if shards is None else shards[w],
                                     bufs[w].at[4 * x + 2 * y + c], send_sems.at[4 * n + w]) for w in range(n)]
    out, arriving = [], []
    for w in range(len(bufs)):
        if level == 1:
            out.append(copy(w, 4 * w, me, sib))
            arriving.append(copy(w, 4 * w, sib, me))
        for j, chip in enumerate(chips):
            if level == 1:
                out.append(copy(w, 4 * w + 1 + j, me, (*chip, c)))
                arriving.append(copy(w, 4 * w + 1 + j, (*chip, c), me))
            else:
                out.append(copy(w, 3 * w + j, (*chip, c), sib))
                arriving.append(copy(w, 3 * w + j, (*chip, 1 - c), me))
    return out, arriving, own


def _split_call(body, name, bufs, sems_in, after, n_sems_out, token, extra=()):
    n = len(bufs)
    out_shape = [pltpu.SemaphoreType.DMA((n_sems_out,))] * (2 if n_sems_out else 0)
    out_shape += [pltpu.HBM(a.shape, a.dtype) for a in bufs]
    out_specs = [SEM] * (2 if n_sems_out else 0) + [HBM] * n
    if token:
        out_shape.append(jax.ShapeDtypeStruct((8, LANES), F32))
        out_specs.append(pl.BlockSpec(memory_space=pltpu.VMEM))
    first = 2 if n_sems_out else 0
    return pl.pallas_call(
        body, name=name, out_shape=tuple(out_shape),
        in_specs=[HBM] * (n + len(extra)) + [SEM] * len(sems_in) + [ANY], out_specs=tuple(out_specs),
        input_output_aliases={i: first + i for i in range(n)},
        compiler_params=pltpu.CompilerParams(has_side_effects=SIDE_EFFECT),
    )(*bufs, *extra, *sems_in, after)


def _gather_start(shards, after, name):
    n = len(shards)
    hbm = lambda a: pltpu.with_memory_space_constraint(a, pltpu.HBM)
    bufs = [hbm(lax.empty((N_DEV,) + s.shape, s.dtype)) for s in shards]

    def body(*refs):
        out, _, own = _gather_level(refs[:n], refs[2 * n + 1], refs[2 * n + 2], 1, shards=refs[n:2 * n])
        for cp in own + out:
            cp.start()
        refs[-1][...] = jnp.zeros_like(refs[-1])

    outs = _split_call(body, name, bufs + [hbm(s) for s in shards], [], after, 5 * n, True)
    return outs[0], outs[1], list(outs[2:2 + 2 * n]), outs[-1]


def _gather_pass(send1, recv1, bufs_and_shards, after, name):
    n = len(bufs_and_shards) // 2
    bufs = bufs_and_shards

    def body(*refs):
        refs = refs[:n] + refs[2 * n:]
        out1, in1, own = _gather_level(refs[:n], refs[n], refs[n + 1], 1)
        out2, _, _ = _gather_level(refs[:n], refs[n + 3], refs[n + 4], 2)
        for cp in in1:
            cp.wait_recv()
        for cp in out2:
            cp.start()
        for cp in out1:
            cp.wait_send()
        for cp in own:
            cp.wait()
        refs[-1][...] = jnp.zeros_like(refs[-1])

    outs = _split_call(body, name, bufs, [send1, recv1], after, 3 * n, True)
    return outs[0], outs[1], list(outs[2:2 + n]), outs[-1]


def _gather_wait(send2, recv2, bufs, after, name):
    n = len(bufs)

    def body(*refs):
        out2, in2, _ = _gather_level(refs[:n], refs[n], refs[n + 1], 2)
        for cp in in2:
            cp.wait_recv()
        for cp in out2:
            cp.wait_send()

    return list(_split_call(body, name, bufs, [send2, recv2], after, 0, False))


ALL_PEERS = tuple(range(1, N_DEV))
SAME_CORE_CHIPS = (2, 4, 6)


def _scatter_copies(parts, lands, send_sems, recv_sems, relations):
    x, y, c = _place()
    ns = len(relations)
    cps = []
    for w, (part, land) in enumerate(zip(parts, lands)):
        for i, k in enumerate(relations):
            px, py, pc = x ^ ((k >> 2) & 1), y ^ ((k >> 1) & 1), c ^ (k & 1)
            block = 4 * px + 2 * py + pc if part.shape[0] == N_DEV else 2 * px + py
            cps.append(pltpu.make_async_remote_copy(
                src_ref=part.at[block], dst_ref=land.at[i],
                send_sem=send_sems.at[ns * w + i], recv_sem=recv_sems.at[ns * w + i],
                device_id=(px, py, pc), device_id_type=MESH_ID))
    return cps


def _scatter_start(parts, after, name, relations=ALL_PEERS):
    n = len(parts)
    ns = len(relations)

    def body(*refs):
        ins, lands = refs[:n], refs[n:2 * n]
        send_sems, recv_sems = refs[2 * n + 1], refs[2 * n + 2]
        token = refs[-1]
        for cp in _scatter_copies(ins, lands, send_sems, recv_sems, relations):
            cp.start()
        token[...] = jnp.zeros_like(token)

    land_shapes = [(ns,) + p.shape[1:] for p in parts]
    in_hbm = [pltpu.with_memory_space_constraint(p, pltpu.HBM) for p in parts]
    in_hbm += [pltpu.with_memory_space_constraint(lax.empty(s, p.dtype), pltpu.HBM) for s, p in zip(land_shapes, parts)]
    outs = pl.pallas_call(
        body, name=name,
        out_shape=(pltpu.SemaphoreType.DMA((ns * n,)), pltpu.SemaphoreType.DMA((ns * n,)),
                   *[pltpu.HBM(p.shape, p.dtype) for p in parts],
                   *[pltpu.HBM(s, p.dtype) for s, p in zip(land_shapes, parts)],
                   jax.ShapeDtypeStruct((8, LANES), F32)),
        in_specs=[HBM] * (2 * n) + [ANY],
        out_specs=(SEM, SEM, *[HBM] * (2 * n), pl.BlockSpec(memory_space=pltpu.VMEM)),
        input_output_aliases={i: 2 + i for i in range(2 * n)},
        compiler_params=pltpu.CompilerParams(has_side_effects=SIDE_EFFECT),
    )(*in_hbm, after)
    return outs[0], outs[1], list(outs[2:2 + n]), list(outs[2 + n:2 + 2 * n]), outs[-1]


def _scatter_wait(send_sems, recv_sems, parts, lands, after, name, relations=ALL_PEERS):
    n = len(parts)

    def body(*refs):
        ins, lnd = refs[:n], refs[n:2 * n]
        for cp in _scatter_copies(ins, lnd, refs[2 * n], refs[2 * n + 1], relations):
            cp.wait_send()
            cp.wait_recv()

    outs = pl.pallas_call(
        body, name=name,
        out_shape=tuple(pltpu.HBM(a.shape, a.dtype) for a in parts + lands),
        in_specs=[HBM] * (2 * n) + [SEM, SEM, ANY],
        out_specs=tuple([HBM] * (2 * n)),
        input_output_aliases={i: i for i in range(2 * n)},
        compiler_params=pltpu.CompilerParams(has_side_effects=SIDE_EFFECT),
    )(*parts, *lands, send_sems, recv_sems, after)
    return list(outs[n:])


def _pair_exchange(parts, name):
    n = len(parts)

    def body(*refs):
        ins, outs = refs[:n], refs[n:2 * n]
        send_sems, recv_sems = refs[2 * n:]
        x, y, c = _place()
        cps = [pltpu.make_async_remote_copy(
            src_ref=ins[w].at[:, pl.ds(1 - c, 1)], dst_ref=outs[w], send_sem=send_sems.at[w], recv_sem=recv_sems.at[w],
            device_id=(x, y, 1 - c), device_id_type=MESH_ID) for w in range(n)]
        for cp in cps:
            cp.start()
        for cp in cps:
            cp.wait()

    return pl.pallas_call(
        body, name=name,
        out_shape=[jax.ShapeDtypeStruct((4, 1) + p.shape[2:], p.dtype) for p in parts],
        in_specs=[ANY] * n, out_specs=[ANY] * n,
        scratch_shapes=[pltpu.SemaphoreType.DMA((n,)), pltpu.SemaphoreType.DMA((n,))],
    )(*parts)


def _pair_sum(g8, r1, core, name):
    _, r, c = g8.shape
    tr = max(q for q in range(16, r + 1, 16) if r % q == 0 and q * c <= ADAM_TILE_ELEMS)

    def body(core_ref, g_ref, r_ref, o_ref):
        o_ref[...] = (g_ref[...] + r_ref[...].astype(F32)).astype(BF16)

    return pl.pallas_call(
        body, name=name,
        out_shape=jax.ShapeDtypeStruct((4, r, c), BF16),
        grid_spec=pltpu.PrefetchScalarGridSpec(
            num_scalar_prefetch=1, grid=(4, r // tr),
            in_specs=[pl.BlockSpec((None, None, tr, c), lambda q, i, s: (q, s[0], i, 0)),
                      pl.BlockSpec((None, tr, c), lambda q, i, s: (q, i, 0))],
            out_specs=pl.BlockSpec((None, tr, c), lambda q, i, s: (q, i, 0))),
        compiler_params=_params(("arbitrary", "arbitrary")),
    )(core, g8.reshape((4, 2) + g8.shape[1:]), r1)


def _adam(w, g, m, v):
    m2 = ADAM_B1 * m + (1.0 - ADAM_B1) * g
    v2 = ADAM_B2 * v + (1.0 - ADAM_B2) * (g * g)
    m_hat = m2 / (1.0 - ADAM_B1 ** ADAM_STEP)
    v_hat = v2 / (1.0 - ADAM_B2 ** ADAM_STEP)
    delta = -ADAM_LR * (m_hat / (jnp.sqrt(v_hat) + ADAM_EPS) + ADAM_WD * w)
    return delta, m2, v2


def _small_allreduce_adam(part, w, m, v, name):
    rows = part.shape[0]

    def body(p_ref, w_ref, m_ref, v_ref, g_ref, d_ref, mo_ref, vo_ref, buf, send_sems, recv_sems):
        x, y, c = _place()
        buf[0] = p_ref[...]
        cps = []
        for k in range(1, N_DEV):
            kx, ky, kc = (k >> 2) & 1, (k >> 1) & 1, k & 1
            peer = (x ^ kx, y ^ ky, c ^ kc)
            cps.append(pltpu.make_async_remote_copy(
                src_ref=p_ref, dst_ref=buf.at[k], send_sem=send_sems.at[k - 1], recv_sem=recv_sems.at[k - 1],
                device_id=peer, device_id_type=MESH_ID))
        for cp in cps:
            cp.start()
        for cp in cps:
            cp.wait()
        me = 4 * x + 2 * y + c
        total = buf[me]
        for d in range(1, N_DEV):
            total = total + buf[d ^ me]
        g_ref[...] = total
        delta, m2, v2 = _adam(w_ref[...], total, m_ref[...], v_ref[...])
        d_ref[...] = delta
        mo_ref[...] = m2
        vo_ref[...] = v2

    vm = pl.BlockSpec(memory_space=pltpu.VMEM)
    return pl.pallas_call(
        body, name=name,
        out_shape=[jax.ShapeDtypeStruct(part.shape, F32)] * 4,
        in_specs=[vm] * 4, out_specs=[vm] * 4,
        scratch_shapes=[pltpu.VMEM((N_DEV, rows, LANES), F32),
                        pltpu.SemaphoreType.DMA((N_DEV - 1,)), pltpu.SemaphoreType.DMA((N_DEV - 1,))],
    )(part, w, m, v)


def _final_adam(g8, land, w, m, v, me, dep, name, pair=None):
    _, r, c = g8.shape
    tr = max(q for q in range(16, r + 1, 16) if r % q == 0 and q * c <= ADAM_TILE_ELEMS)
    nland = land.shape[0]

    def body(me_ref, g_ref, land_ref, *rest):
        pair_ref = rest[0] if pair is not None else None
        w_ref, m_ref, v_ref, _, go_ref, d_ref, mo_ref, vo_ref = rest[-8:]
        g = g_ref[...]
        if pair_ref is not None:
            g = g + pair_ref[...].astype(F32)
        for k in range(nland):
            g = g + land_ref[k].astype(F32)
        go_ref[...] = g
        delta, m2, v2 = _adam(w_ref[...], g, m_ref[...], v_ref[...])
        d_ref[...] = delta
        mo_ref[...] = m2
        vo_ref[...] = v2

    plain = pl.BlockSpec((tr, c), lambda i, s: (i, 0))
    return pl.pallas_call(
        body, name=name,
        out_shape=[jax.ShapeDtypeStruct((r, c), F32)] * 4,
        grid_spec=pltpu.PrefetchScalarGridSpec(
            num_scalar_prefetch=1, grid=(r // tr,),
            in_specs=[pl.BlockSpec((None, tr, c), lambda i, s: (s[0], i, 0)),
                      pl.BlockSpec((nland, tr, c), lambda i, s: (0, i, 0))]
            + ([] if pair is None else [pl.BlockSpec((None, tr, c), lambda i, s: (s[1], i, 0))])
            + [plain, plain, plain, ANY],
            out_specs=[plain] * 4),
        compiler_params=_params(("arbitrary",)),
    )(*((me, g8, land) + (() if pair is None else (pair,)) + (w, m, v, dep)))


def _rms(x, gain):
    r = lax.rsqrt(jnp.mean(x * x, axis=-1, keepdims=True) + EPS)
    xh = x * r
    return xh * gain, xh, r


def _rms_bwd(xh, r, gain, dy):
    gdy = gain * dy
    dx = r * (gdy - xh * jnp.mean(xh * gdy, axis=-1, keepdims=True))
    return dx, jnp.sum(dy * xh, axis=0, keepdims=True)


def _load_weights(pairs, sems):
    cps = [pltpu.make_async_copy(src, dst, sems.at[i]) for i, (src, dst) in enumerate(pairs)]
    for cp in cps:
        cp.start()
    for cp in cps:
        cp.wait()


def _ffn_fwd(h, gain, wgu, wd, name):
    t, d = h.shape
    nb, nf, _ = wgu.shape
    nh = nb // 2
    tm = _row_tile(t, 512)

    def body(h_ref, g_ref, wgu_hbm, wd_hbm, out_ref, gu_ref, wgu_v, wd_v, sems):
        @pl.when(pl.program_id(0) == 0)
        def _():
            _load_weights([(wgu_hbm, wgu_v), (wd_hbm, wd_v)], sems)

        x = h_ref[...]
        n, _, _ = _rms(x, g_ref[...])
        nbf = n.astype(BF16)
        acc = jnp.zeros((tm, d), F32)
        for j in range(nh):
            g = _dot_nt(nbf, wgu_v[j])
            u = _dot_nt(nbf, wgu_v[j + nh])
            gu_ref[j] = g.astype(BF16)
            gu_ref[j + nh] = u.astype(BF16)
            a = (g * jax.nn.sigmoid(g)) * u
            acc = acc + _dot(a.astype(BF16), wd_v[j])
        out_ref[...] = x + 0.5 * acc

    return pl.pallas_call(
        body, name=name, grid=(t // tm,),
        out_shape=[jax.ShapeDtypeStruct((t, d), F32), jax.ShapeDtypeStruct((nb, t, nf), BF16)],
        in_specs=[pl.BlockSpec((tm, d), lambda i: (i, 0)), pl.BlockSpec((1, d), lambda i: (0, 0)), ANY, ANY],
        out_specs=[pl.BlockSpec((tm, d), lambda i: (i, 0)), pl.BlockSpec((nb, tm, nf), lambda i: (0, i, 0))],
        scratch_shapes=[pltpu.VMEM(wgu.shape, BF16), pltpu.VMEM(wd.shape, BF16), pltpu.SemaphoreType.DMA((2,))],
        compiler_params=_params(("arbitrary",)),
    )(h, gain, wgu, wd)


def _ffn_bwd(dh, h, gain, gu, wgu, wd, name):
    t, d = h.shape
    nb, nf, _ = wgu.shape
    nh = nb // 2
    tm = _row_tile(t, 256)

    def body(dh_ref, h_ref, g_ref, gu_ref, wgu_hbm, wd_hbm, dhp_ref, dgu_ref, a_ref, n_ref, dgain_ref,
             wgu_v, wd_v, sems):
        @pl.when(pl.program_id(0) == 0)
        def _():
            _load_weights([(wgu_hbm, wgu_v), (wd_hbm, wd_v)], sems)
            dgain_ref[...] = jnp.zeros_like(dgain_ref)

        x = h_ref[...]
        gain_v = g_ref[...]
        n, xh, r = _rms(x, gain_v)
        n_ref[...] = n.astype(BF16)
        dh_v = dh_ref[...]
        dfb = (0.5 * dh_v).astype(BF16)
        dn = jnp.zeros((tm, d), F32)
        for j in range(nh):
            da = _dot_nt(dfb, wd_v[j])
            g = gu_ref[j].astype(F32)
            u = gu_ref[j + nh].astype(F32)
            sg = jax.nn.sigmoid(g)
            si = g * sg
            dg = (da * u * (sg * (1.0 + g * (1.0 - sg)))).astype(BF16)
            du = (da * si).astype(BF16)
            a_ref[j] = (si * u).astype(BF16)
            dgu_ref[j] = dg
            dgu_ref[j + nh] = du
            dn = dn + _dot(dg, wgu_v[j]) + _dot(du, wgu_v[j + nh])
        dx, dgain = _rms_bwd(xh, r, gain_v, dn)
        dhp_ref[...] = dh_v + dx
        dgain_ref[...] += dgain

    row = pl.BlockSpec((tm, d), lambda i: (i, 0))
    vec = pl.BlockSpec((1, d), lambda i: (0, 0))
    return pl.pallas_call(
        body, name=name, grid=(t // tm,),
        out_shape=[jax.ShapeDtypeStruct((t, d), F32), jax.ShapeDtypeStruct((nb, t, nf), BF16),
                   jax.ShapeDtypeStruct((nh, t, nf), BF16), jax.ShapeDtypeStruct((t, d), BF16),
                   jax.ShapeDtypeStruct((1, d), F32)],
        in_specs=[row, row, vec, pl.BlockSpec((nb, tm, nf), lambda i: (0, i, 0)), ANY, ANY],
        out_specs=[row, pl.BlockSpec((nb, tm, nf), lambda i: (0, i, 0)),
                   pl.BlockSpec((nh, tm, nf), lambda i: (0, i, 0)), row, vec],
        scratch_shapes=[pltpu.VMEM(wgu.shape, BF16), pltpu.VMEM(wd.shape, BF16), pltpu.SemaphoreType.DMA((2,))],
        compiler_params=_params(("arbitrary",)),
    )(dh, h, gain, gu, wgu, wd)


def _dw(xa, dy, nb, n, name, scale=1.0, dep=None):
    t, k = xa.shape[-2:]
    tt = _row_tile(t, 512)
    steps = t // tt
    wide = dy.ndim == 2 and xa.ndim == 2
    if xa.ndim == 3:
        x_spec = pl.BlockSpec((nb, tt, k), lambda i: (0, i, 0))
    else:
        x_spec = pl.BlockSpec((tt, k), lambda i: (i, 0))
    if dy.ndim == 3:
        dy_spec = pl.BlockSpec((nb, tt, n), lambda i: (0, i, 0))
    else:
        dy_spec = pl.BlockSpec((tt, dy.shape[1]), lambda i: (i, 0))
    acc_shape = (k, nb * n) if wide else (nb, k, n)
    stage_shape = (k, nb * n) if wide else (k, n)

    def body(x_ref, dy_ref, *rest):
        o_hbm, ob_hbm, acc, stage, sems = rest[-5:]

        @pl.when(pl.program_id(0) == 0)
        def _():
            acc[...] = jnp.zeros_like(acc)

        if wide:
            acc[...] += _dot(x_ref[...].astype(BF16).T, dy_ref[...].astype(BF16))
        elif xa.ndim == 2:
            xt = x_ref[...].astype(BF16).T
            for j in range(nb):
                acc[j] += _dot(xt, dy_ref[j].astype(BF16))
        else:
            dyb = dy_ref[...].astype(BF16)
            for j in range(nb):
                acc[j] += _dot_tn(x_ref[j].astype(BF16), dyb)

        @pl.when(pl.program_id(0) == steps - 1)
        def _():
            if scale != 1.0:
                acc[...] = acc[...] * scale
            if wide:
                cps = [pltpu.make_async_copy(acc.at[:, pl.ds(j * n, n)] if nb > 1 else acc, o_hbm.at[j], sems.at[j])
                       for j in range(nb)]
            else:
                cps = [pltpu.make_async_copy(acc, o_hbm, sems.at[0])]
            for cp in cps:
                cp.start()
            if wide:
                stage[...] = acc[...].astype(BF16)
                bcs = [pltpu.make_async_copy(stage.at[:, pl.ds(j * n, n)] if nb > 1 else stage, ob_hbm.at[j],
                                             sems.at[nb + j]) for j in range(nb)]
                for cp in bcs:
                    cp.start()
                for cp in bcs:
                    cp.wait()
            else:
                for j in range(nb):
                    stage[...] = acc[j].astype(BF16)
                    cp = pltpu.make_async_copy(stage, ob_hbm.at[j], sems.at[nb])
                    cp.start()
                    cp.wait()
            for cp in cps:
                cp.wait()

    return pl.pallas_call(
        body, name=name, grid=(steps,),
        out_shape=[jax.ShapeDtypeStruct((nb, k, n), F32), jax.ShapeDtypeStruct((nb, k, n), BF16)],
        in_specs=[x_spec, dy_spec] + ([] if dep is None else [ANY]),
        out_specs=[ANY, ANY],
        scratch_shapes=[pltpu.VMEM(acc_shape, F32), pltpu.VMEM(stage_shape, BF16),
                        pltpu.SemaphoreType.DMA((2 * nb,))],
        compiler_params=_params(("arbitrary",)),
    )(*((xa, dy) if dep is None else (xa, dy, dep)))


def _proj_fwd(h, gain, win, wgate, name):
    t, d = h.shape
    tm = _row_tile(t, 256)
    nq, ng = win.shape[0], wgate.shape[1]

    def body(h_ref, g_ref, win_ref, wg_ref, un_ref, qkv_ref, gate_ref):
        n, _, _ = _rms(h_ref[...], g_ref[...])
        nbf = n.astype(BF16)
        un_ref[...] = nbf
        qkv_ref[...] = _dot_nt(nbf, win_ref[...])
        gate_ref[...] = jax.nn.sigmoid(_dot(nbf, wg_ref[...]))

    full = lambda a: pl.BlockSpec(a.shape, lambda i: (0,) * a.ndim)
    return pl.pallas_call(
        body, name=name, grid=(t // tm,),
        out_shape=[jax.ShapeDtypeStruct((t, d), BF16), jax.ShapeDtypeStruct((t, nq), F32),
                   jax.ShapeDtypeStruct((t, ng), F32)],
        in_specs=[pl.BlockSpec((tm, d), lambda i: (i, 0)), full(gain), full(win), full(wgate)],
        out_specs=[pl.BlockSpec((tm, d), lambda i: (i, 0)), pl.BlockSpec((tm, nq), lambda i: (i, 0)),
                   pl.BlockSpec((tm, ng), lambda i: (i, 0))],
        compiler_params=_params(("arbitrary",)),
    )(h, gain, win, wgate)


def _proj_bwd(dh, h, gain, dzg, dqkv_parts, win, wgate, name):
    t, d = h.shape
    tm = _row_tile(t, 256)
    ng = wgate.shape[1]
    np_ = len(dqkv_parts)
    widths = [a.shape[1] for a in dqkv_parts]

    def body(dh_ref, h_ref, g_ref, dzg_ref, *rest):
        part_refs, (win_ref, wg_ref, dhp_ref, dgain_ref) = rest[:np_], rest[np_:]

        @pl.when(pl.program_id(0) == 0)
        def _():
            dgain_ref[...] = jnp.zeros_like(dgain_ref)

        gain_v = g_ref[...]
        _, xh, r = _rms(h_ref[...], gain_v)
        dun = _dot_nt(dzg_ref[...], wg_ref[...])
        off = 0
        for ref, wd in zip(part_refs, widths):
            dun = dun + _dot(ref[...].astype(BF16), win_ref[off:off + wd, :])
            off += wd
        dx, dgain = _rms_bwd(xh, r, gain_v, dun)
        dhp_ref[...] = dh_ref[...] + dx
        dgain_ref[...] += dgain

    full = lambda a: pl.BlockSpec(a.shape, lambda i: (0,) * a.ndim)
    row = pl.BlockSpec((tm, d), lambda i: (i, 0))
    return pl.pallas_call(
        body, name=name, grid=(t // tm,),
        out_shape=[jax.ShapeDtypeStruct((t, d), F32), jax.ShapeDtypeStruct((1, d), F32)],
        in_specs=[row, row, full(gain), pl.BlockSpec((tm, ng), lambda i: (i, 0))]
        + [pl.BlockSpec((tm, wd), lambda i: (i, 0)) for wd in widths] + [full(win), full(wgate)],
        out_specs=[row, pl.BlockSpec((1, d), lambda i: (0, 0))],
        compiler_params=_params(("arbitrary",)),
    )(dh, h, gain, dzg, *dqkv_parts, win, wgate)


def _dw_rows(parts, dy, name):
    t, n = dy.shape
    widths = [a.shape[1] for a in parts]
    k = sum(widths)
    tt = _row_tile(t, 512)
    steps = t // tt
    np_ = len(parts)

    def body(*refs):
        part_refs, dy_ref = refs[:np_], refs[np_]
        o_hbm, ob_hbm, acc, stage, sems = refs[np_ + 1:]

        @pl.when(pl.program_id(0) == 0)
        def _():
            acc[...] = jnp.zeros_like(acc)

        dyb = dy_ref[...].astype(BF16)
        off = 0
        for ref, wd in zip(part_refs, widths):
            acc[off:off + wd, :] += _dot(ref[...].astype(BF16).T, dyb)
            off += wd

        @pl.when(pl.program_id(0) == steps - 1)
        def _():
            stage[...] = acc[...].astype(BF16)
            cps = [pltpu.make_async_copy(acc, o_hbm.at[0], sems.at[0]),
                   pltpu.make_async_copy(stage, ob_hbm.at[0], sems.at[1])]
            for cp in cps:
                cp.start()
            for cp in cps:
                cp.wait()

    return pl.pallas_call(
        body, name=name, grid=(steps,),
        out_shape=[jax.ShapeDtypeStruct((1, k, n), F32), jax.ShapeDtypeStruct((1, k, n), BF16)],
        in_specs=[pl.BlockSpec((tt, wd), lambda i: (i, 0)) for wd in widths] + [pl.BlockSpec((tt, n), lambda i: (i, 0))],
        out_specs=[ANY, ANY],
        scratch_shapes=[pltpu.VMEM((k, n), F32), pltpu.VMEM((k, n), BF16), pltpu.SemaphoreType.DMA((2,))],
        compiler_params=_params(("arbitrary",)),
    )(*parts, dy)


def _merge_fwd(h, ya, yb, gate, wpa, wpb, wout, name):
    t, d = h.shape
    tm = _row_tile(t, 256)

    def body(h_ref, ya_ref, yb_ref, ga_ref, gb_ref, wpa_ref, wpb_ref, wout_ref, out_ref, mg_ref, pa_ref, pb_ref):
        pa = _dot(ya_ref[...].astype(BF16), wpa_ref[...])
        pb = _dot(yb_ref[...].astype(BF16), wpb_ref[...])
        merged = (ga_ref[...] * pa + gb_ref[...] * pb).astype(BF16)
        pa_ref[...] = pa.astype(BF16)
        pb_ref[...] = pb.astype(BF16)
        mg_ref[...] = merged
        out_ref[...] = h_ref[...] + _dot(merged, wout_ref[...])

    full = lambda a: pl.BlockSpec(a.shape, lambda i: (0,) * a.ndim)
    row = pl.BlockSpec((tm, d), lambda i: (i, 0))
    yrow = pl.BlockSpec((tm, ya.shape[1]), lambda i: (i, 0))
    return pl.pallas_call(
        body, name=name, grid=(t // tm,),
        out_shape=[jax.ShapeDtypeStruct((t, d), F32)] + [jax.ShapeDtypeStruct((t, d), BF16)] * 3,
        in_specs=[row, yrow, yrow, pl.BlockSpec((tm, d), lambda i: (i, 0)), pl.BlockSpec((tm, d), lambda i: (i, 1)),
                  full(wpa), full(wpb), full(wout)],
        out_specs=[row] * 4,
        compiler_params=_params(("arbitrary",)),
    )(h, ya, yb, gate, gate, wpa, wpb, wout)


def _merge_bwd(dh, pa, pb, gate, wpa, wpb, wout, name):
    t, d = dh.shape
    tm = _row_tile(t, 256)
    wy = wpa.shape[0]

    def body(dh_ref, pa_ref, pb_ref, ga_ref, gb_ref, wpa_ref, wpb_ref, wout_ref,
             dpa_ref, dpb_ref, dzg_ref, dya_ref, dyb_ref):
        dm = _dot_nt(dh_ref[...].astype(BF16), wout_ref[...])
        ga, gb = ga_ref[...], gb_ref[...]
        dpa = (dm * ga).astype(BF16)
        dpb = (dm * gb).astype(BF16)
        dpa_ref[...] = dpa
        dpb_ref[...] = dpb
        dzg_ref[:, :d] = (dm * pa_ref[...].astype(F32) * ga * (1.0 - ga)).astype(BF16)
        dzg_ref[:, d:] = (dm * pb_ref[...].astype(F32) * gb * (1.0 - gb)).astype(BF16)
        dya_ref[...] = _dot_nt(dpa, wpa_ref[...])
        dyb_ref[...] = _dot_nt(dpb, wpb_ref[...])

    full = lambda a: pl.BlockSpec(a.shape, lambda i: (0,) * a.ndim)
    row = pl.BlockSpec((tm, d), lambda i: (i, 0))
    yrow = pl.BlockSpec((tm, wy), lambda i: (i, 0))
    return pl.pallas_call(
        body, name=name, grid=(t // tm,),
        out_shape=[jax.ShapeDtypeStruct((t, d), BF16), jax.ShapeDtypeStruct((t, d), BF16),
                   jax.ShapeDtypeStruct((t, 2 * d), BF16), jax.ShapeDtypeStruct((t, wy), F32),
                   jax.ShapeDtypeStruct((t, wy), F32)],
        in_specs=[row, row, row, pl.BlockSpec((tm, d), lambda i: (i, 0)), pl.BlockSpec((tm, d), lambda i: (i, 1)),
                  full(wpa), full(wpb), full(wout)],
        out_specs=[row, row, pl.BlockSpec((tm, 2 * d), lambda i: (i, 0)), yrow, yrow],
        compiler_params=_params(("arbitrary",)),
    )(dh, pa, pb, gate, gate, wpa, wpb, wout)


def _ple_loss(h, gain, p, target, wpg, wpe, name):
    t, d = h.shape
    tm = _row_tile(t, 256)
    pd = p.shape[1]

    def body(h_ref, g_ref, p_ref, t_ref, wpg_ref, wpe_ref, dh_ref, dz_ref, dpp_ref, n_ref, dgain_ref, loss_ref):
        @pl.when(pl.program_id(0) == 0)
        def _():
            dgain_ref[...] = jnp.zeros_like(dgain_ref)
            loss_ref[...] = jnp.zeros_like(loss_ref)

        x = h_ref[...]
        gain_v = g_ref[...]
        n, xh, r = _rms(x, gain_v)
        nbf = n.astype(BF16)
        n_ref[...] = nbf
        pg = jax.nn.sigmoid(_dot(nbf, wpg_ref[...]))
        pp = _dot(p_ref[...].astype(BF16), wpe_ref[...])
        err = (x + pg * pp) - t_ref[...]
        loss_ref[...] += 0.5 * jnp.sum(jnp.mean(err * err, axis=-1, keepdims=True))
        dy = err * (1.0 / d)
        dpp_ref[...] = (dy * pg).astype(BF16)
        dz = (dy * pp * pg * (1.0 - pg)).astype(BF16)
        dz_ref[...] = dz
        dn = _dot_nt(dz, wpg_ref[...])
        dx, dgain = _rms_bwd(xh, r, gain_v, dn)
        dh_ref[...] = dy + dx
        dgain_ref[...] += dgain

    full = lambda a: pl.BlockSpec(a.shape, lambda i: (0,) * a.ndim)
    row = pl.BlockSpec((tm, d), lambda i: (i, 0))
    return pl.pallas_call(
        body, name=name, grid=(t // tm,),
        out_shape=[jax.ShapeDtypeStruct((t, d), F32), jax.ShapeDtypeStruct((t, d), BF16),
                   jax.ShapeDtypeStruct((t, d), BF16), jax.ShapeDtypeStruct((t, d), BF16),
                   jax.ShapeDtypeStruct((1, d), F32), jax.ShapeDtypeStruct((8, LANES), F32)],
        in_specs=[row, full(gain), pl.BlockSpec((tm, pd), lambda i: (i, 0)), row, full(wpg), full(wpe)],
        out_specs=[row, row, row, row, pl.BlockSpec((1, d), lambda i: (0, 0)),
                   pl.BlockSpec((8, LANES), lambda i: (0, 0))],
        compiler_params=_params(("arbitrary",)),
    )(h, gain, p, target, wpg, wpe)


def _head_masks():
    lane = lax.broadcasted_iota(jnp.int32, (1, LANES), 1)
    m0 = (lane < HEAD_DIM).astype(F32)
    return m0, 1.0 - m0


def _head_mean(v, m0, m1):
    del m0, m1
    width = v.shape[-1]
    shift = HEAD_DIM.bit_length() - 1
    r = jnp.right_shift(lax.broadcasted_iota(jnp.int32, (width, width), 0), shift)
    c = jnp.right_shift(lax.broadcasted_iota(jnp.int32, (width, width), 1), shift)
    same_head = (r == c).astype(BF16)
    return _dot(v.astype(BF16), same_head) * (1.0 / HEAD_DIM)


def _head_norm(x, gain, m0, m1):
    r = lax.rsqrt(_head_mean(x * x, m0, m1) + EPS)
    xh = x * r
    return xh * gain, xh, r


def _head_norm_bwd(xh, r, gain, dy, m0, m1):
    gdy = gain * dy
    dx = r * (gdy - xh * _head_mean(xh * gdy, m0, m1))
    return dx, jnp.sum(dy * xh, axis=0, keepdims=True)


GROUP = 4
QW = GROUP * HEAD_DIM
STACK = GROUP * QTILE


def _kv_width(mode):
    return QW if mode == "A" else LANES


def _q_scratch_shape(mode, s_len):
    return (s_len, QW) if mode == "A" else (GROUP * s_len, LANES)


def _group_masks(dtype=F32):
    lane = lax.broadcasted_iota(jnp.int32, (1, QW), 1)
    return [((lane >= h * HEAD_DIM) & (lane < (h + 1) * HEAD_DIM)).astype(dtype) for h in range(GROUP)]


def _stack_heads(first_kv, x, m0, m1):
    out = []
    for half in range(GROUP // 2):
        xh = x[:, half * LANES:(half + 1) * LANES]
        a0, a1 = xh * m0, xh * m1
        r0, r1 = pltpu.roll(a0, HEAD_DIM, 1), pltpu.roll(a1, HEAD_DIM, 1)
        out += [jnp.where(first_kv, a0, r0), jnp.where(first_kv, r1, a1)]
    return out


def _unstack_heads(mode, first_kv, ts, m0, m1):
    if mode == "A":
        masks = _group_masks()
        return sum(t * mk for t, mk in zip(ts, masks))
    halves = []
    for half in range(GROUP // 2):
        t0 = jnp.where(first_kv, ts[2 * half], pltpu.roll(ts[2 * half], HEAD_DIM, 1))
        t1 = jnp.where(first_kv, pltpu.roll(ts[2 * half + 1], HEAD_DIM, 1), ts[2 * half + 1])
        halves.append(t0 * m0 + t1 * m1)
    return jnp.concatenate(halves, axis=1)


def _store_stacked(dst, i, heads):
    for half in range(2):
        rows = slice(half * QTILE, (half + 1) * QTILE)
        for h, x in enumerate(heads):
            dst[pl.ds((2 * i + half) * STACK + h * QTILE, QTILE), :] = x[rows].astype(dst.dtype)


def _load_stacked(mode, ref, m):
    if mode == "B":
        return ref[pl.ds(pl.multiple_of(m * STACK, STACK), STACK), :]
    x = ref[pl.ds(pl.multiple_of(m * QTILE, QTILE), QTILE), :]
    return jnp.concatenate([x * mk for mk in _group_masks(x.dtype)], axis=0)


def _attn_prep(mode, group, s_len, padk, q_ref, k_ref, v_ref, gq_ref, gk_ref, qs, k2, v2, do_ref=None, dos=None):
    m0, m1 = _head_masks()
    zpad = jnp.zeros((padk, k2.shape[1]), BF16)
    k2[pl.ds(0, padk), :] = zpad
    v2[pl.ds(0, padk), :] = zpad
    first_kv = group == 0
    rt = 2 * QTILE
    for i in range(s_len // rt):
        rows = pl.ds(i * rt, rt)
        qn, _, _ = _head_norm(q_ref[rows, :], gq_ref[...], m0, m1)
        kn, _, _ = _head_norm(k_ref[rows, :], gk_ref[...], m0, m1)
        qn = qn * (HEAD_DIM ** -0.5)
        if mode == "A":
            qs[rows, :] = qn.astype(BF16)
            if dos is not None:
                dos[rows, :] = do_ref[rows, :].astype(BF16)
        else:
            _store_stacked(qs, i, _stack_heads(first_kv, qn, m0, m1))
            if dos is not None:
                _store_stacked(dos, i, _stack_heads(first_kv, do_ref[rows, :], m0, m1))
        k2[pl.ds(padk + i * rt, rt), :] = kn.astype(BF16)
        v2[pl.ds(padk + i * rt, rt), :] = v_ref[rows, :].astype(BF16)


def _softmax_terms(mode, s, sink):
    mx = jnp.max(s, axis=-1, keepdims=True)
    if mode == "B":
        mx = jnp.maximum(mx, sink)
    e = jnp.exp(s - mx)
    l = jnp.sum(e, axis=-1, keepdims=True)
    if mode == "B":
        l = l + jnp.exp(sink - mx)
    return e, mx, l


def _sink_column(sink_ref, group):
    row = lax.broadcasted_iota(jnp.int32, (STACK, 1), 0)
    col = jnp.zeros((STACK, 1), F32)
    for h in range(GROUP):
        col = jnp.where((row >= h * QTILE) & (row < (h + 1) * QTILE), sink_ref[GROUP * group + h], col)
    return col


def _head_deltas(dd, m0, m1):
    cols = []
    for half in range(GROUP // 2):
        dh = dd[:, half * LANES:(half + 1) * LANES]
        cols += [jnp.sum(dh * m0, axis=-1, keepdims=True), jnp.sum(dh * m1, axis=-1, keepdims=True)]
    return jnp.concatenate(cols, axis=0)


def _attn_cols(mode):
    if mode == "A":
        return (lambda b, g: (b, g)), (lambda b, g: (b, 2 + g)), (lambda b, g: (b, 4 + g))
    return (lambda b, g: (b, 6 + g)), (lambda b, g: (b, 16)), (lambda b, g: (b, 17))


def _attn_fwd(mode, qkv, gq, gk, bias, sinks, bl, s_len, name):
    bw = bias.shape[-1]
    padk = bw - QTILE
    nt = s_len // QTILE
    qmap, kmap, vmap = _attn_cols(mode)

    kw = _kv_width(mode)

    def body(q_ref, k_ref, v_ref, gq_ref, gk_ref, bias_ref, sink_ref, o_ref, qs, k2, v2, s_buf, *rest):
        o_buf = rest[0] if rest else None
        group = pl.program_id(1)
        m0, m1 = _head_masks()
        first_kv = group == 0
        _attn_prep(mode, group, s_len, padk, q_ref, k_ref, v_ref, gq_ref, gk_ref, qs, k2, v2)
        col = lax.broadcasted_iota(jnp.int32, (STACK, bw), 1)
        sink = _sink_column(sink_ref, group)

        def scores(m, slot):
            r0 = pl.multiple_of(m * QTILE, QTILE)
            s = _dot_nt(_load_stacked(mode, qs, m), k2[pl.ds(r0, bw), :]) + bias_ref[...]
            s_buf[slot] = jnp.where(col >= (padk - r0), s, NEG_INF)

        def finish_tile(m, slot):
            r0 = pl.multiple_of(m * QTILE, QTILE)
            e, _, l = _softmax_terms(mode, s_buf[slot], sink)
            if mode == "A":
                o_st = _dot(e.astype(BF16), v2[pl.ds(r0, bw), :]) / l
                heads = [o_st[h * QTILE:(h + 1) * QTILE] for h in range(GROUP)]
                o_ref[pl.ds(r0, QTILE), :] = _unstack_heads(mode, first_kv, heads, m0, m1)
            else:
                o_buf[pl.ds(pl.multiple_of(m * STACK, STACK), STACK), :] = _dot((e * (1.0 / l)).astype(BF16),
                                                                                 v2[pl.ds(r0, bw), :])

        scores(0, 0)

        def pair(j, carry):
            scores(2 * j + 1, 1)
            finish_tile(2 * j, 0)
            scores(jnp.minimum(2 * j + 2, nt - 1), 0)
            finish_tile(2 * j + 1, 1)
            return carry

        lax.fori_loop(0, nt // 2, pair, 0)
        if mode == "B":
            for m in range(nt):
                heads = [o_buf[pl.ds(m * STACK + h * QTILE, QTILE), :] for h in range(GROUP)]
                o_ref[pl.ds(m * QTILE, QTILE), :] = _unstack_heads(mode, first_kv, heads, m0, m1)

    blk = lambda w, f: pl.BlockSpec((s_len, w), f)
    return pl.pallas_call(
        body, name=name, grid=(bl, B_Q_HEADS // GROUP),
        out_shape=jax.ShapeDtypeStruct((bl * s_len, B_Q_HEADS * HEAD_DIM), F32),
        in_specs=[blk(QW, qmap), blk(kw, kmap), blk(kw, vmap),
                  pl.BlockSpec((1, QW), lambda b, g: (0, 0)), pl.BlockSpec((1, kw), lambda b, g: (0, 0)),
                  pl.BlockSpec((STACK, bw), lambda b, g: (g, 0)),
                  pl.BlockSpec(memory_space=pltpu.SMEM)],
        out_specs=blk(QW, lambda b, g: (b, g)),
        scratch_shapes=[pltpu.VMEM(_q_scratch_shape(mode, s_len), BF16)] + [pltpu.VMEM((s_len + padk, kw), BF16)] * 2
        + [pltpu.VMEM((2, STACK, bw), F32)] + ([pltpu.VMEM((GROUP * s_len, LANES), F32)] if mode == "B" else []),
        compiler_params=_params(("arbitrary", "arbitrary")),
    )(qkv, qkv, qkv, gq, gk, bias.reshape(-1, bw), sinks)


def _attn_bwd(mode, qkv, gq, gk, bias, sinks, y, dy, bl, s_len, name):
    bw = bias.shape[-1]
    padk = bw - QTILE
    nt = s_len // QTILE
    qmap, kmap, vmap = _attn_cols(mode)
    t = bl * s_len
    kw = _kv_width(mode)
    kvw = 4 * LANES if mode == "A" else LANES
    dp_ahead = True

    def body(q_ref, k_ref, v_ref, gq_ref, gk_ref, bias_ref, sink_ref, y_ref, dy_ref,
             dq_ref, dk_ref, dv_ref, dgq_ref, dgk_ref, dbias_ref, dsink_ref,
             qs, k2, v2, dos, dqs, dk, dv, s_buf, dp_buf):
        group = pl.program_id(1)
        m0, m1 = _head_masks()
        first_kv = group == 0
        _attn_prep(mode, group, s_len, padk, q_ref, k_ref, v_ref, gq_ref, gk_ref, qs, k2, v2, dy_ref, dos)
        dk[...] = jnp.zeros_like(dk)
        dv[...] = jnp.zeros_like(dv)
        dbias_ref[...] = jnp.zeros_like(dbias_ref)
        col = lax.broadcasted_iota(jnp.int32, (STACK, bw), 1)
        lane8 = lax.broadcasted_iota(jnp.int32, (8, LANES), 1)
        sink = _sink_column(sink_ref, group)

        def ahead(m, slot):
            r0 = pl.multiple_of(m * QTILE, QTILE)
            band = pl.ds(r0, bw)
            s = _dot_nt(_load_stacked(mode, qs, m), k2[band, :]) + bias_ref[...]
            s_buf[slot] = jnp.where(col >= (padk - r0), s, NEG_INF)
            if dp_ahead:
                dp_buf[slot] = _dot_nt(_load_stacked(mode, dos, m), v2[band, :])

        def tile(m, slot, dsink):
            r0 = pl.multiple_of(m * QTILE, QTILE)
            rows = pl.ds(r0, QTILE)
            band = pl.ds(r0, bw)
            q_st = _load_stacked(mode, qs, m)
            do_st = _load_stacked(mode, dos, m)
            delta = _head_deltas(dy_ref[rows, :] * y_ref[rows, :], m0, m1)
            kb = k2[band, :]
            e, mx, l = _softmax_terms(mode, s_buf[slot], sink)
            inv = 1.0 / l
            pn = e * inv
            ds = pn * ((dp_buf[slot] if dp_ahead else _dot_nt(do_st, v2[band, :])) - delta)
            if mode == "A":
                dbias_ref[...] += ds
            else:
                part = jnp.exp(sink - mx) * inv * delta
                for h in range(GROUP):
                    dsink = dsink - jnp.where(lane8 == h, jnp.sum(part[h * QTILE:(h + 1) * QTILE]), 0.0)
            dsb = ds.astype(BF16)
            dv[band, :] += _dot_tn(pn.astype(BF16), do_st)
            dk[band, :] += _dot_tn(dsb, q_st)
            dq_st = _dot(dsb, kb)
            if mode == "A":
                heads = [dq_st[h * QTILE:(h + 1) * QTILE] for h in range(GROUP)]
                dq_ref[rows, :] = _unstack_heads(mode, first_kv, heads, m0, m1)
            else:
                dqs[pl.ds(pl.multiple_of(m * STACK, STACK), STACK), :] = dq_st
            return dsink

        ahead(0, 0)

        def pair(j, dsink):
            ahead(2 * j + 1, 1)
            dsink = tile(2 * j, 0, dsink)
            ahead(jnp.minimum(2 * j + 2, nt - 1), 0)
            return tile(2 * j + 1, 1, dsink)

        dsink = lax.fori_loop(0, nt // 2, pair, jnp.zeros((8, LANES), F32))
        dsink_ref[...] = dsink

        rt = 2 * QTILE
        dgq = jnp.zeros((1, QW), F32)
        dgk = jnp.zeros((1, kw), F32)
        for i in range(s_len // rt):
            rows = pl.ds(i * rt, rt)
            src = pl.ds(padk + i * rt, rt)
            gq_v, gk_v = gq_ref[...], gk_ref[...]
            _, qh, qr = _head_norm(q_ref[rows, :], gq_v, m0, m1)
            _, kh, kr = _head_norm(k_ref[rows, :], gk_v, m0, m1)
            if mode == "A":
                dqn = dq_ref[rows, :] * (HEAD_DIM ** -0.5)
            else:
                dqn = jnp.concatenate(
                    [_unstack_heads(mode, first_kv, [dqs[pl.ds((2 * i + half) * STACK + h * QTILE, QTILE), :]
                                                     for h in range(GROUP)], m0, m1)
                     for half in range(2)], axis=0) * (HEAD_DIM ** -0.5)
            dq_raw, dgq_i = _head_norm_bwd(qh, qr, gq_v, dqn, m0, m1)
            dk_raw, dgk_i = _head_norm_bwd(kh, kr, gk_v, dk[src, :], m0, m1)
            dvn = dv[src, :]
            dq_ref[rows, :] = dq_raw
            if mode == "A":
                dk_ref[rows, :] = dk_raw
                dv_ref[rows, :] = dvn
            else:
                @pl.when(group == 0)
                def _():
                    dk_ref[rows, :] = dk_raw
                    dv_ref[rows, :] = dvn

                @pl.when(group != 0)
                def _():
                    dk_ref[rows, :] += dk_raw
                    dv_ref[rows, :] += dvn
            dgq, dgk = dgq + dgq_i, dgk + dgk_i
        dgq_ref[...] = jnp.broadcast_to(dgq, (8, QW))
        dgk_ref[...] = jnp.broadcast_to(dgk, (8, kw))

    ng = B_Q_HEADS // GROUP
    blk = lambda w, f: pl.BlockSpec((s_len, w), f)
    small = lambda w: pl.BlockSpec((None, None, 8, w), lambda b, g: (b, g, 0, 0))
    own = lambda b, g: (b, g)
    kvmap = own if mode == "A" else (lambda b, g: (b, 0))
    pad_f32 = pltpu.VMEM((s_len + padk, kw), F32)
    pad_bf = pltpu.VMEM((s_len + padk, kw), BF16)
    stack_bf = pltpu.VMEM(_q_scratch_shape(mode, s_len), BF16)
    outs = pl.pallas_call(
        body, name=name, grid=(bl, ng),
        out_shape=[jax.ShapeDtypeStruct((t, ng * QW), F32), jax.ShapeDtypeStruct((t, kvw), F32),
                   jax.ShapeDtypeStruct((t, kvw), F32),
                   jax.ShapeDtypeStruct((bl, ng, 8, QW), F32), jax.ShapeDtypeStruct((bl, ng, 8, kw), F32),
                   jax.ShapeDtypeStruct((bl, ng * STACK, bw), F32), jax.ShapeDtypeStruct((bl, ng, 8, LANES), F32)],
        in_specs=[blk(QW, qmap), blk(kw, kmap), blk(kw, vmap),
                  pl.BlockSpec((1, QW), lambda b, g: (0, 0)), pl.BlockSpec((1, kw), lambda b, g: (0, 0)),
                  pl.BlockSpec((STACK, bw), lambda b, g: (g, 0)),
                  pl.BlockSpec(memory_space=pltpu.SMEM),
                  blk(QW, own), blk(QW, own)],
        out_specs=[blk(QW, own), blk(kw, kvmap), blk(kw, kvmap), small(QW), small(kw),
                   pl.BlockSpec((None, STACK, bw), lambda b, g: (b, g, 0)), small(LANES)],
        scratch_shapes=[stack_bf, pad_bf, pad_bf, stack_bf,
                        pltpu.VMEM((8, LANES) if mode == "A" else _q_scratch_shape(mode, s_len), F32),
                        pad_f32, pad_f32, pltpu.VMEM((2, STACK, bw), F32),
                        pltpu.VMEM((2, STACK, bw) if dp_ahead else (8, LANES), F32)],
        compiler_params=_params(("arbitrary", "arbitrary")),
    )(qkv, qkv, qkv, gq, gk, bias.reshape(-1, bw), sinks, y, dy)
    outs = list(outs)
    outs[5] = outs[5].reshape(bl, B_Q_HEADS, QTILE, bw)
    return outs


def _band_geometry(prev):
    bw = QTILE + prev * CHUNK
    i = np.arange(QTILE)[:, None]
    j = np.arange(bw)[None, :]
    dist = i + prev * CHUNK - j
    valid = (j // CHUNK >= i // CHUNK) & (j // CHUNK <= i // CHUNK + prev)
    return dist, valid


A_VAR0 = (A_PREV * CHUNK - A_MAX_REL) // LANES * LANES


A_NVAR = QTILE + A_PREV * CHUNK - A_VAR0


def _skew_rows(x, sign):
    rows, n = x.shape
    row = lax.broadcasted_iota(jnp.int32, x.shape, 0)
    b = 1
    while b < rows:
        x = jnp.where((row & b) != 0, pltpu.roll(x, (sign * b) % n, 1), x)
        b *= 2
    return x


def _rel_bias_expand(table, name):
    _, valid = _band_geometry(A_PREV)
    bw = valid.shape[1]
    valid_f = jnp.asarray(valid.astype(np.float32))
    rev = jnp.flip(table[:, 1:], axis=1).reshape(A_HEADS, 1, A_NVAR)

    def body(rev_ref, valid_ref, o_ref):
        rowv = jnp.broadcast_to(rev_ref[...], (QTILE, A_NVAR))
        top = rowv[:, 0:1]
        var = _skew_rows(rowv, 1)
        row = lax.broadcasted_iota(jnp.int32, (QTILE, A_NVAR), 0)
        colv = lax.broadcasted_iota(jnp.int32, (QTILE, A_NVAR), 1)
        var = jnp.where(colv < row, top, var)
        ok = valid_ref[...] > 0.5
        o_ref[:, :A_VAR0] = jnp.where(ok[:, :A_VAR0], top, NEG_INF)
        o_ref[:, A_VAR0:] = jnp.where(ok[:, A_VAR0:], var, NEG_INF)

    return pl.pallas_call(
        body, name=name, grid=(A_HEADS,),
        out_shape=jax.ShapeDtypeStruct((A_HEADS, QTILE, bw), F32),
        in_specs=[pl.BlockSpec((None, 1, A_NVAR), lambda h: (h, 0, 0)), pl.BlockSpec((QTILE, bw), lambda h: (0, 0))],
        out_specs=pl.BlockSpec((None, QTILE, bw), lambda h: (h, 0, 0)),
        compiler_params=_params(("arbitrary",)),
    )(rev, valid_f)


def _rel_bias_grad(dbias, name):
    bl = dbias.shape[0]
    bw = dbias.shape[-1]

    def body(db_ref, o_ref):
        g = db_ref[0]
        for b in range(1, bl):
            g = g + db_ref[b]
        sk = _skew_rows(g[:, A_VAR0:], -1)
        row = lax.broadcasted_iota(jnp.int32, (QTILE, A_NVAR), 0)
        colv = lax.broadcasted_iota(jnp.int32, (QTILE, A_NVAR), 1)
        wrapped = (row + colv) >= A_NVAR
        main = jnp.sum(jnp.where(wrapped, 0.0, sk), axis=0, keepdims=True)
        top = jnp.sum(g[:, :A_VAR0]) + jnp.sum(jnp.where(wrapped, sk, 0.0))
        o_ref[:, :A_NVAR] = jnp.broadcast_to(main, (8, A_NVAR))
        o_ref[:, A_NVAR:] = jnp.full((8, LANES), top, F32)

    out = pl.pallas_call(
        body, name=name, grid=(A_HEADS,),
        out_shape=jax.ShapeDtypeStruct((A_HEADS, 8, A_NVAR + LANES), F32),
        in_specs=[pl.BlockSpec((bl, None, QTILE, bw), lambda h: (0, h, 0, 0))],
        out_specs=pl.BlockSpec((None, 8, A_NVAR + LANES), lambda h: (h, 0, 0)),
        compiler_params=_params(("arbitrary",)),
    )(dbias)
    main, top = out[:, 0, :A_NVAR], out[:, 0, A_NVAR]
    fm = jnp.flip(main, axis=1)
    return jnp.concatenate([jnp.zeros((A_HEADS, 1), F32), fm[:, :-1], fm[:, -1:] + top[:, None]], axis=1)


def _alibi_bias():
    dist, valid = _band_geometry(B_PREV)
    slopes = np.array([2.0 ** (-8.0 * (h + 1) / B_Q_HEADS) for h in range(B_Q_HEADS)], dtype=np.float32)
    bias = -slopes[:, None, None] * np.abs(dist).astype(np.float32)[None]
    return jnp.asarray(np.where(valid[None], bias, np.float32(NEG_INF)).astype(np.float32))


SMALL_NAMES = ("ffn1_norm", "mix_norm", "ffn2_norm", "ple_norm", "a_q_norm", "a_k_norm", "b_q_norm", "b_k_norm",
               "a_rel_bias", "b_sinks", "loss")


def _pack_small(vals):
    rows = []
    for nme in SMALL_NAMES:
        v = vals[nme].astype(F32)
        if nme == "a_rel_bias":
            v = jnp.pad(v.reshape(A_HEADS, -1), ((0, 0), (0, 3 * LANES - (2 * A_MAX_REL + 1))))
        v = v.reshape(-1)
        v = jnp.pad(v, (0, (-v.shape[0]) % LANES))
        rows.append(v.reshape(-1, LANES))
    out = jnp.concatenate(rows, axis=0)
    return jnp.pad(out, ((0, (-out.shape[0]) % 8), (0, 0)))


def _unpack_small(packed, shapes):
    out, r = {}, 0
    for nme in SMALL_NAMES:
        shp = shapes[nme]
        if nme == "a_rel_bias":
            nr = A_HEADS * 3
            out[nme] = packed[r:r + nr].reshape(A_HEADS, 3 * LANES)[:, :2 * A_MAX_REL + 1].reshape(shp)
        else:
            size = int(np.prod(shp)) if shp else 1
            nr = -(-size // LANES)
            out[nme] = packed[r:r + nr].reshape(-1)[:size].reshape(shp)
        r += nr
    return out


BIG_NAMES = ("ffn1_w_gu", "ffn1_w_down", "w_in", "w_gate", "w_proj_a", "w_proj_b", "w_out",
             "ffn2_w_gu", "ffn2_w_down", "w_ple_gate", "w_ple_proj")
WEIGHT_ORDER = ("ffn1_norm", "ffn1_w_gu", "ffn1_w_down", "mix_norm", "w_in", "a_q_norm", "a_k_norm", "a_rel_bias",
                "b_q_norm", "b_k_norm", "b_sinks", "w_gate", "w_proj_a", "w_proj_b", "w_out", "ffn2_norm",
                "ffn2_w_gu", "ffn2_w_down", "ple_norm", "w_ple_gate", "w_ple_proj")


TRANSPOSED = ("ffn1_w_gu", "ffn2_w_gu", "w_in")


def _local(a, nme):
    return a[0].T if nme in TRANSPOSED else a[0]


def _full_cols(wg):
    nb, k, n = wg.shape
    return jnp.transpose(wg, (1, 0, 2)).reshape(k, nb * n)


def _step(x, p, target, w, m, v):
    bl, s_len, d = x.shape
    t = bl * s_len
    h0 = x.reshape(t, d)
    pt = p.reshape(t, p.shape[-1])
    tgt = target.reshape(t, d)

    g_ffn1, g_mix, g_ffn2, g_ple = w["ffn1_norm"], w["mix_norm"], w["ffn2_norm"], w["ple_norm"]
    tiled = lambda a, width: jnp.tile(a.reshape(1, HEAD_DIM), (1, width // HEAD_DIM))
    gqa, gka = tiled(w["a_q_norm"], QW), tiled(w["a_k_norm"], _kv_width("A"))
    gqb, gkb = tiled(w["b_q_norm"], QW), tiled(w["b_k_norm"], _kv_width("B"))
    sinks = w["b_sinks"].reshape(B_Q_HEADS)
    bias_b = _alibi_bias()

    ffn1_names = ("ffn1_w_gu", "ffn1_w_down")
    shard = {nme: _local(w[nme], nme).astype(BF16) for nme in ffn1_names}
    send1, recv1, bufs, token = _gather_start([shard[nme] for nme in ffn1_names], h0, "gather_start_ffn1")
    zero = token[0, 0]
    shard.update({nme: (_local(w[nme], nme) + zero).astype(BF16) for nme in BIG_NAMES if nme not in ffn1_names})
    bias_a = _rel_bias_expand(w["a_rel_bias"][0] + zero, "rel_bias_expand")
    send2, recv2, bufs, token = _gather_pass(send1, recv1, bufs, bias_a, "gather_pass_ffn1")
    wgu1, wd1 = _gather_wait(send2, recv2, bufs, shard["ffn2_w_gu"], "gather_wait_ffn1")
    nf = wgu1.shape[1]
    wd1 = wd1.reshape(N_DEV // 2, nf, d)
    mixer_names = ("w_in", "w_gate")
    rest_names = ("w_proj_a", "w_proj_b", "w_out", "ffn2_w_gu", "ffn2_w_down", "w_ple_gate", "w_ple_proj")
    send1, recv1, bufs, token = _gather_start([shard[nme] for nme in mixer_names], wgu1, "gather_start_mixer")
    rsend1, rrecv1, rest_bufs, token = _gather_start([shard[nme] for nme in rest_names], token, "gather_start_rest")

    h1, gu1 = _ffn_fwd(h0, g_ffn1 + token[0, 0], wgu1, wd1, "ffn1_fwd")
    send2, recv2, bufs, token = _gather_pass(send1, recv1, bufs, h1, "gather_pass_mixer")
    win, wgate = _gather_wait(send2, recv2, bufs, token, "gather_wait_mixer")
    win, wgate = win.reshape(IN_COLS, d), _full_cols(wgate)
    un, qkv, gate = _proj_fwd(h1, g_mix, win, wgate, "proj_fwd")
    ya = _attn_fwd("A", qkv, gqa, gka, bias_a, sinks, bl, s_len, "attn_a_fwd")
    rsend2, rrecv2, rest_bufs, token = _gather_pass(rsend1, rrecv1, rest_bufs, ya, "gather_pass_rest")
    yb = _attn_fwd("B", qkv, gqb + token[0, 0], gkb, bias_b, sinks, bl, s_len, "attn_b_fwd")
    gathered = dict(zip(rest_names, _gather_wait(rsend2, rrecv2, rest_bufs, yb, "gather_wait_rest")))
    wgu2 = gathered["ffn2_w_gu"]
    wd2 = gathered["ffn2_w_down"].reshape(N_DEV // 2, nf, d)
    wpa = _full_cols(gathered["w_proj_a"])
    wpb = _full_cols(gathered["w_proj_b"])
    wpe = _full_cols(gathered["w_ple_proj"])
    wout = gathered["w_out"].reshape(d, d)
    wpg = gathered["w_ple_gate"].reshape(d, d)
    h2, merged, pa, pb = _merge_fwd(h1, ya, yb, gate, wpa, wpb, wout, "merge_fwd")
    h3, gu2 = _ffn_fwd(h2, g_ffn2, wgu2, wd2, "ffn2_fwd")
    dh3, dz4, dpp, n4, dg_ple, loss_part = _ple_loss(h3, g_ple, pt, tgt, wpg, wpe, "ple_loss")

    xi, yi, ci = _place()
    me = jnp.stack([4 * xi + 2 * yi + ci, 2 * xi + yi]).astype(jnp.int32)
    core = jnp.stack([ci]).astype(jnp.int32)
    g32, g16, big, pairs = {}, {}, {}, {}

    def keep(nme, pair, rows=None):
        for store, g in zip((g32, g16), pair):
            store[nme] = g if rows is None else g.reshape(N_DEV, rows, d)

    def start(names, after, tag):
        send, recv, parts, lands, token = _scatter_start([g16[nme] for nme in names], after, "grads_start_" + tag)
        return names, send, recv, parts, lands, token

    def start_two_level(names, after, tag):
        views = [g16[nme].reshape((4, 2) + g16[nme].shape[1:]) for nme in names]
        for nme, got in zip(names, _pair_exchange(views, "grads_pair_" + tag)):
            pairs[nme] = got.reshape((4,) + got.shape[2:])
        sums = [_pair_sum(g32[nme], pairs[nme], core, "pair_sum_" + nme) for nme in names]
        send, recv, parts, lands, token = _scatter_start(sums, after, "grads_start_" + tag, SAME_CORE_CHIPS)
        return names, send, recv, parts, lands, token

    def finish(state, after, tag):
        names, send, recv, parts, lands, _ = state
        relations = SAME_CORE_CHIPS if names[0] in pairs else ALL_PEERS
        lands = _scatter_wait(send, recv, parts, lands, after, "grads_wait_" + tag, relations)
        return names, lands

    def adam(done, dep):
        for nme, land in zip(*done):
            outs = _final_adam(g32[nme], land, _local(w[nme], nme), _local(m[nme], nme), _local(v[nme], nme), me, dep,
                               "adam_" + nme, pairs.get(nme))
            big[nme] = [(o.T if nme in TRANSPOSED else o)[None] for o in outs]

    keep("w_ple_gate", _dw(n4, dz4, 1, d, "dw_ple_gate"), d // N_DEV)
    keep("w_ple_proj", _dw(pt, dpp, N_DEV, d // N_DEV, "dw_ple_proj"))
    early = [(start(("w_ple_gate", "w_ple_proj"), dh3, "ple"), "ple")]

    dh2, dgu2, a2, n3, dg_ffn2 = _ffn_bwd(dh3, h2, g_ffn2 + early[-1][0][-1][0, 0], gu2, wgu2, wd2, "ffn2_bwd")
    keep("ffn2_w_down", _dw(a2, dh3, N_DEV // 2, d, "dw_ffn2_down", 0.5), nf // 2)
    early.append((start(("ffn2_w_down",), dh2, "ffn2_down"), "ffn2_down"))
    keep("ffn2_w_gu", _dw(dgu2, n3, N_DEV, d, "dw_ffn2_gu", dep=early[-1][0][-1]))
    flight = start(("ffn2_w_gu",), dh2, "ffn2")

    dpa, dpb, dzg, dya, dyb = _merge_bwd(dh2, pa, pb, gate, wpa, wpb, wout, "merge_bwd")
    keep("w_out", _dw(merged, dh2, 1, d, "dw_out"), d // N_DEV)
    keep("w_proj_a", _dw(ya, dpa, N_DEV, d // N_DEV, "dw_proj_a"))
    keep("w_proj_b", _dw(yb, dpb, N_DEV, d // N_DEV, "dw_proj_b"))
    keep("w_gate", _dw(un, dzg, N_DEV, 2 * d // N_DEV, "dw_gate"))

    tok = flight[-1][0, 0]
    dqa, dka, dva, dgqa, dgka, dbias, _ = _attn_bwd("A", qkv, gqa + tok, gka, bias_a, sinks, ya, dya, bl, s_len,
                                                     "attn_a_bwd")
    dqb, dkb, dvb, dgqb, dgkb, _, dsink = _attn_bwd("B", qkv, gqb, gkb, bias_b, sinks, yb, dyb, bl, s_len, "attn_b_bwd")
    dqkv = [dqa, dka, dva, dqb, dkb, dvb]
    dtab = _rel_bias_grad(dbias, "rel_bias_grad")

    dh1, dg_mix = _proj_bwd(dh2, h1, g_mix, dzg, dqkv, win, wgate, "proj_bwd")
    keep("w_in", _dw_rows(dqkv, un, "dw_in"), IN_COLS // N_DEV)
    waiting = [finish(state, g32["w_in"], tag) for state, tag in early]
    done = finish(flight, waiting[-1][1][0], "ffn2")
    flight = start(("w_out", "w_proj_a", "w_proj_b", "w_gate", "w_in"), done[1][0], "mixer")
    waiting.append(done)

    dh0, dgu1, a1, n1, dg_ffn1 = _ffn_bwd(dh1, h0, g_ffn1 + flight[-1][0, 0], gu1, wgu1, wd1, "ffn1_bwd")
    keep("ffn1_w_down", _dw(a1, dh1, N_DEV // 2, d, "dw_ffn1_down", 0.5), nf // 2)
    done = finish(flight, g32["ffn1_w_down"], "mixer")
    flight = start(("ffn1_w_down",), done[1][0], "ffn1_down")
    waiting.append(done)

    keep("ffn1_w_gu", _dw(dgu1, n1, N_DEV, d, "dw_ffn1_gu", dep=flight[-1]))
    done = finish(flight, g32["ffn1_w_gu"], "ffn1_down")
    flight = start_two_level(("ffn1_w_gu",), done[1][0], "ffn1_gu")
    for group in waiting + [done]:
        adam(group, flight[-1])
    behind = 0.0 * big["ffn1_w_down"][0][0, 0, :1]
    smalls = (dg_ffn1, dg_mix, dg_ffn2, dg_ple + behind, dgqa, dgka, dgqb, dgkb, dtab, dsink)
    return dh0, loss_part, big, smalls, flight, finish, adam


def kernel(x, p, ffn1_norm, ffn1_w_gu, ffn1_w_down, mix_norm, w_in, a_q_norm, a_k_norm, a_rel_bias, b_q_norm, b_k_norm, b_sinks, w_gate, w_proj_a, w_proj_b, w_out, ffn2_norm, ffn2_w_gu, ffn2_w_down, ple_norm, w_ple_gate, w_ple_proj, loss_target, m_ffn1_norm, m_ffn1_w_gu, m_ffn1_w_down, m_mix_norm, m_w_in, m_a_q_norm, m_a_k_norm, m_a_rel_bias, m_b_q_norm, m_b_k_norm, m_b_sinks, m_w_gate, m_w_proj_a, m_w_proj_b, m_w_out, m_ffn2_norm, m_ffn2_w_gu, m_ffn2_w_down, m_ple_norm, m_w_ple_gate, m_w_ple_proj, v_ffn1_norm, v_ffn1_w_gu, v_ffn1_w_down, v_mix_norm, v_w_in, v_a_q_norm, v_a_k_norm, v_a_rel_bias, v_b_q_norm, v_b_k_norm, v_b_sinks, v_w_gate, v_w_proj_a, v_w_proj_b, v_w_out, v_ffn2_norm, v_ffn2_w_gu, v_ffn2_w_down, v_ple_norm, v_w_ple_gate, v_w_ple_proj):
    w = dict(ffn1_norm=ffn1_norm, ffn1_w_gu=ffn1_w_gu, ffn1_w_down=ffn1_w_down, mix_norm=mix_norm, w_in=w_in,
             a_q_norm=a_q_norm, a_k_norm=a_k_norm, a_rel_bias=a_rel_bias, b_q_norm=b_q_norm, b_k_norm=b_k_norm,
             b_sinks=b_sinks, w_gate=w_gate, w_proj_a=w_proj_a, w_proj_b=w_proj_b, w_out=w_out, ffn2_norm=ffn2_norm,
             ffn2_w_gu=ffn2_w_gu, ffn2_w_down=ffn2_w_down, ple_norm=ple_norm, w_ple_gate=w_ple_gate,
             w_ple_proj=w_ple_proj)
    m = dict(ffn1_norm=m_ffn1_norm, ffn1_w_gu=m_ffn1_w_gu, ffn1_w_down=m_ffn1_w_down, mix_norm=m_mix_norm,
             w_in=m_w_in, a_q_norm=m_a_q_norm, a_k_norm=m_a_k_norm, a_rel_bias=m_a_rel_bias, b_q_norm=m_b_q_norm,
             b_k_norm=m_b_k_norm, b_sinks=m_b_sinks, w_gate=m_w_gate, w_proj_a=m_w_proj_a, w_proj_b=m_w_proj_b,
             w_out=m_w_out, ffn2_norm=m_ffn2_norm, ffn2_w_gu=m_ffn2_w_gu, ffn2_w_down=m_ffn2_w_down,
             ple_norm=m_ple_norm, w_ple_gate=m_w_ple_gate, w_ple_proj=m_w_ple_proj)
    v = dict(ffn1_norm=v_ffn1_norm, ffn1_w_gu=v_ffn1_w_gu, ffn1_w_down=v_ffn1_w_down, mix_norm=v_mix_norm,
             w_in=v_w_in, a_q_norm=v_a_q_norm, a_k_norm=v_a_k_norm, a_rel_bias=v_a_rel_bias, b_q_norm=v_b_q_norm,
             b_k_norm=v_b_k_norm, b_sinks=v_b_sinks, w_gate=v_w_gate, w_proj_a=v_w_proj_a, w_proj_b=v_w_proj_b,
             w_out=v_w_out, ffn2_norm=v_ffn2_norm, ffn2_w_gu=v_ffn2_w_gu, ffn2_w_down=v_ffn2_w_down,
             ple_norm=v_ple_norm, w_ple_gate=v_w_ple_gate, w_ple_proj=v_w_ple_proj)
    bl, s_len, d = x.shape

    dh0, loss_part, big, smalls, flight, finish, adam = _step(x, p[0], loss_target, w, m, v)
    dg_ffn1, dg_mix, dg_ffn2, dg_ple, dgqa, dgka, dgqb, dgkb, dtab, dsink = smalls

    fold = lambda a: a[:, :, 0, :].reshape(-1, HEAD_DIM).sum(axis=0)
    small_part = dict(
        ffn1_norm=dg_ffn1, mix_norm=dg_mix, ffn2_norm=dg_ffn2, ple_norm=dg_ple,
        a_q_norm=fold(dgqa), a_k_norm=fold(dgka), b_q_norm=fold(dgqb), b_k_norm=fold(dgkb),
        a_rel_bias=dtab,
        b_sinks=dsink.sum(axis=0)[:, 0, :GROUP].reshape(B_Q_HEADS),
        loss=loss_part[0, :1])
    zero1 = jnp.zeros((1,), F32)
    shapes = {nme: w[nme].shape for nme in SMALL_NAMES if nme != "loss"}
    shapes["loss"] = ()
    pk = lambda src: _pack_small({**{nme: src[nme] for nme in SMALL_NAMES if nme != "loss"}, "loss": zero1})
    sg, sd, sm, sv = _small_allreduce_adam(_pack_small(small_part), pk(w), pk(m), pk(v), "small_allreduce_adam")
    adam(finish(flight, sg, "ffn1_gu"), sg)
    sg, sd, sm, sv = (_unpack_small(a, shapes) for a in (sg, sd, sm, sv))

    def pick(i):
        out = []
        for nme in WEIGHT_ORDER:
            out.append(big[nme][i] if nme in big else (sg, sd, sm, sv)[i][nme])
        return out

    return (sg["loss"], dh0.reshape(bl, s_len, d), *pick(0), *pick(1), *pick(2), *pick(3))
```

```python
import jax
import jax.numpy as jnp
import numpy as np
from jax import lax
from jax.experimental import pallas as pl
from jax.experimental.pallas import tpu as pltpu

F32 = jnp.float32
BF16 = jnp.bfloat16

CHUNK = 64
HEAD_DIM = 64
A_HEADS = 8
A_PREV = 8
A_MAX_REL = 128
B_Q_HEADS = 8
B_KV_HEADS = 2
B_PREV = 2
A_WIDTH = A_HEADS * HEAD_DIM
B_Q_WIDTH = B_Q_HEADS * HEAD_DIM
B_KV_WIDTH = B_KV_HEADS * HEAD_DIM
IN_COLS = 3 * A_WIDTH + B_Q_WIDTH + 2 * B_KV_WIDTH
EPS = 1e-6
NEG_INF = -1e30
ADAM_LR = 0.001
ADAM_B1 = 0.9
ADAM_B2 = 0.999
ADAM_EPS = 1e-08
ADAM_WD = 0.01
ADAM_STEP = 10

N_DEV = 8
LANES = 128
QTILE = 2 * CHUNK
VMEM_LIMIT = 56 * 1024 * 1024
ADAM_TILE_ELEMS = 256 * 1024

MESH_ID = pl.DeviceIdType.MESH
ANY = pl.BlockSpec(memory_space=pl.ANY)
HBM = pl.BlockSpec(memory_space=pltpu.HBM)
SEM = pl.BlockSpec(memory_space=pltpu.SEMAPHORE)
SIDE_EFFECT = pltpu.SideEffectType.DATAFLOW_SIDE_EFFECTING


def _dot(a, b):
    return jnp.dot(a, b, preferred_element_type=F32)


def _dot_nt(a, b):
    return lax.dot_general(a, b, (((1,), (1,)), ((), ())), preferred_element_type=F32)


def _dot_tn(a, b):
    return lax.dot_general(a, b, (((0,), (0,)), ((), ())), preferred_element_type=F32)


def _params(sem=None, vmem=VMEM_LIMIT):
    return pltpu.CompilerParams(dimension_semantics=sem, vmem_limit_bytes=vmem)


def _row_tile(t, want):
    while t % want:
        want //= 2
    return want


def _place():
    return lax.axis_index("x"), lax.axis_index("y"), lax.axis_index("c")


def _gather_level(bufs, send_sems, recv_sems, level, shards=None):
    x, y, c = _place()
    me, sib = (x, y, c), (x, y, 1 - c)
    chips = [(1 - x, y), (x, 1 - y), (1 - x, 1 - y)]

    def copy(w, k, block, to):
        px, py, pc = block
        rows = bufs[w].at[4 * px + 2 * py + pc]
        src = shards[w] if shards is not None and block is me else rows
        return pltpu.make_async_remote_copy(src_ref=src, dst_ref=rows, send_sem=send_sems.at[k], recv_sem=recv_sems.at[k],
                                            device_id=to, device_id_type=MESH_ID)

    n = len(bufs)
    own = []
    if level == 1:
        own = [pltpu.make_async_copy(bufs[w].at[4 * x + 2 * y + c] if shards is None else shards[w],
                                     bufs[w].at[4 * x + 2 * y + c], send_sems.at[4 * n + w]) for w in range(n)]
    out, arriving = [], []
    for w in range(len(bufs)):
        if level == 1:
            out.append(copy(w, 4 * w, me, sib))
            arriving.append(copy(w, 4 * w, sib, me))
        for j, chip in enumerate(chips):
            if level == 1:
                out.append(copy(w, 4 * w + 1 + j, me, (*chip, c)))
                arriving.append(copy(w, 4 * w + 1 + j, (*chip, c), me))
            else:
                out.append(copy(w, 3 * w + j, (*chip, c), sib))
                arriving.append(copy(w, 3 * w + j, (*chip, 1 - c), me))
    return out, arriving, own


def _split_call(body, name, bufs, sems_in, after, n_sems_out, token, extra=()):
    n = len(bufs)
    out_shape = [pltpu.SemaphoreType.DMA((n_sems_out,))] * (2 if n_sems_out else 0)
    out_shape += [pltpu.HBM(a.shape, a.dtype) for a in bufs]
    out_specs = [SEM] * (2 if n_sems_out else 0) + [HBM] * n
    if token:
        out_shape.append(jax.ShapeDtypeStruct((8, LANES), F32))
        out_specs.append(pl.BlockSpec(memory_space=pltpu.VMEM))
    first = 2 if n_sems_out else 0
    return pl.pallas_call(
        body, name=name, out_shape=tuple(out_shape),
        in_specs=[HBM] * (n + len(extra)) + [SEM] * len(sems_in) + [ANY], out_specs=tuple(out_specs),
        input_output_aliases={i: first + i for i in range(n)},
        compiler_params=pltpu.CompilerParams(has_side_effects=SIDE_EFFECT),
    )(*bufs, *extra, *sems_in, after)


def _gather_start(shards, after, name):
    n = len(shards)
    hbm = lambda a: pltpu.with_memory_space_constraint(a, pltpu.HBM)
    bufs = [hbm(lax.empty((N_DEV,) + s.shape, s.dtype)) for s in shards]

    def body(*refs):
        out, _, own = _gather_level(refs[:n], refs[2 * n + 1], refs[2 * n + 2], 1, shards=refs[n:2 * n])
        for cp in own + out:
            cp.start()
        refs[-1][...] = jnp.zeros_like(refs[-1])

    outs = _split_call(body, name, bufs + [hbm(s) for s in shards], [], after, 5 * n, True)
    return outs[0], outs[1], list(outs[2:2 + 2 * n]), outs[-1]


def _gather_pass(send1, recv1, bufs_and_shards, after, name):
    n = len(bufs_and_shards) // 2
    bufs = bufs_and_shards

    def body(*refs):
        refs = refs[:n] + refs[2 * n:]
        out1, in1, own = _gather_level(refs[:n], refs[n], refs[n + 1], 1)
        out2, _, _ = _gather_level(refs[:n], refs[n + 3], refs[n + 4], 2)
        for cp in in1:
            cp.wait_recv()
        for cp in out2:
            cp.start()
        for cp in out1:
            cp.wait_send()
        for cp in own:
            cp.wait()
        refs[-1][...] = jnp.zeros_like(refs[-1])

    outs = _split_call(body, name, bufs, [send1, recv1], after, 3 * n, True)
    return outs[0], outs[1], list(outs[2:2 + n]), outs[-1]


def _gather_wait(send2, recv2, bufs, after, name):
    n = len(bufs)

    def body(*refs):
        out2, in2, _ = _gather_level(refs[:n], refs[n], refs[n + 1], 2)
        for cp in in2:
            cp.wait_recv()
        for cp in out2:
            cp.wait_send()

    return list(_split_call(body, name, bufs, [send2, recv2], after, 0, False))


ALL_PEERS = tuple(range(1, N_DEV))
SAME_CORE_CHIPS = (2, 4, 6)


def _scatter_copies(parts, lands, send_sems, recv_sems, relations):
    x, y, c = _place()
    ns = len(relations)
    cps = []
    for w, (part, land) in enumerate(zip(parts, lands)):
        for i, k in enumerate(relations):
            px, py, pc = x ^ ((k >> 2) & 1), y ^ ((k >> 1) & 1), c ^ (k & 1)
            block = 4 * px + 2 * py + pc if part.shape[0] == N_DEV else 2 * px + py
            cps.append(pltpu.make_async_remote_copy(
                src_ref=part.at[block], dst_ref=land.at[i],
                send_sem=send_sems.at[ns * w + i], recv_sem=recv_sems.at[ns * w + i],
                device_id=(px, py, pc), device_id_type=MESH_ID))
    return cps


def _scatter_start(parts, after, name, relations=ALL_PEERS):
    n = len(parts)
    ns = len(relations)

    def body(*refs):
        ins, lands = refs[:n], refs[n:2 * n]
        send_sems, recv_sems = refs[2 * n + 1], refs[2 * n + 2]
        token = refs[-1]
        for cp in _scatter_copies(ins, lands, send_sems, recv_sems, relations):
            cp.start()
        token[...] = jnp.zeros_like(token)

    land_shapes = [(ns,) + p.shape[1:] for p in parts]
    in_hbm = [pltpu.with_memory_space_constraint(p, pltpu.HBM) for p in parts]
    in_hbm += [pltpu.with_memory_space_constraint(lax.empty(s, p.dtype), pltpu.HBM) for s, p in zip(land_shapes, parts)]
    outs = pl.pallas_call(
        body, name=name,
        out_shape=(pltpu.SemaphoreType.DMA((ns * n,)), pltpu.SemaphoreType.DMA((ns * n,)),
                   *[pltpu.HBM(p.shape, p.dtype) for p in parts],
                   *[pltpu.HBM(s, p.dtype) for s, p in zip(land_shapes, parts)],
                   jax.ShapeDtypeStruct((8, LANES), F32)),
        in_specs=[HBM] * (2 * n) + [ANY],
        out_specs=(SEM, SEM, *[HBM] * (2 * n), pl.BlockSpec(memory_space=pltpu.VMEM)),
        input_output_aliases={i: 2 + i for i in range(2 * n)},
        compiler_params=pltpu.CompilerParams(has_side_effects=SIDE_EFFECT),
    )(*in_hbm, after)
    return outs[0], outs[1], list(outs[2:2 + n]), list(outs[2 + n:2 + 2 * n]), outs[-1]


def _scatter_wait(send_sems, recv_sems, parts, lands, after, name, relations=ALL_PEERS):
    n = len(parts)

    def body(*refs):
        ins, lnd = refs[:n], refs[n:2 * n]
        for cp in _scatter_copies(ins, lnd, refs[2 * n], refs[2 * n + 1], relations):
            cp.wait_send()
            cp.wait_recv()

    outs = pl.pallas_call(
        body, name=name,
        out_shape=tuple(pltpu.HBM(a.shape, a.dtype) for a in parts + lands),
        in_specs=[HBM] * (2 * n) + [SEM, SEM, ANY],
        out_specs=tuple([HBM] * (2 * n)),
        input_output_aliases={i: i for i in range(2 * n)},
        compiler_params=pltpu.CompilerParams(has_side_effects=SIDE_EFFECT),
    )(*parts, *lands, send_sems, recv_sems, after)
    return list(outs[n:])


def _pair_exchange(parts, name):
    n = len(parts)

    def body(*refs):
        ins, outs = refs[:n], refs[n:2 * n]
        send_sems, recv_sems = refs[2 * n:]
        x, y, c = _place()
        cps = [pltpu.make_async_remote_copy(
            src_ref=ins[w].at[:, pl.ds(1 - c, 1)], dst_ref=outs[w], send_sem=send_sems.at[w], recv_sem=recv_sems.at[w],
            device_id=(x, y, 1 - c), device_id_type=MESH_ID) for w in range(n)]
        for cp in cps:
            cp.start()
        for cp in cps:
            cp.wait()

    return pl.pallas_call(
        body, name=name,
        out_shape=[jax.ShapeDtypeStruct((4, 1) + p.shape[2:], p.dtype) for p in parts],
        in_specs=[ANY] * n, out_specs=[ANY] * n,
        scratch_shapes=[pltpu.SemaphoreType.DMA((n,)), pltpu.SemaphoreType.DMA((n,))],
    )(*parts)


def _pair_sum(g8, r1, me, name):
    _, r, c = g8.shape
    tr = max(q for q in range(16, r + 1, 16) if r % q == 0 and q * c <= ADAM_TILE_ELEMS)

    def body(me_ref, g_ref, r_ref, o_ref):
        o_ref[...] = (g_ref[...] + r_ref[...].astype(F32)).astype(BF16)

    chip = lambda k, s: s[1] ^ (k + 1)
    return pl.pallas_call(
        body, name=name,
        out_shape=jax.ShapeDtypeStruct((4, r, c), BF16),
        grid_spec=pltpu.PrefetchScalarGridSpec(
            num_scalar_prefetch=1, grid=(3, r // tr),
            in_specs=[pl.BlockSpec((None, None, tr, c), lambda k, i, s: (chip(k, s), s[0] % 2, i, 0)),
                      pl.BlockSpec((None, tr, c), lambda k, i, s: (chip(k, s), i, 0))],
            out_specs=pl.BlockSpec((None, tr, c), lambda k, i, s: (chip(k, s), i, 0))),
        compiler_params=_params(("arbitrary", "arbitrary")),
    )(me, g8.reshape((4, 2) + g8.shape[1:]), r1)


def _adam(w, g, m, v):
    m2 = ADAM_B1 * m + (1.0 - ADAM_B1) * g
    v2 = ADAM_B2 * v + (1.0 - ADAM_B2) * (g * g)
    m_hat = m2 / (1.0 - ADAM_B1 ** ADAM_STEP)
    v_hat = v2 / (1.0 - ADAM_B2 ** ADAM_STEP)
    delta = -ADAM_LR * (m_hat / (jnp.sqrt(v_hat) + ADAM_EPS) + ADAM_WD * w)
    return delta, m2, v2


def _small_allreduce_adam(part, w, m, v, name):
    rows = part.shape[0]

    def body(p_ref, w_ref, m_ref, v_ref, g_ref, d_ref, mo_ref, vo_ref, buf, send_sems, recv_sems):
        x, y, c = _place()
        buf[0] = p_ref[...]
        cps = []
        for k in range(1, N_DEV):
            kx, ky, kc = (k >> 2) & 1, (k >> 1) & 1, k & 1
            peer = (x ^ kx, y ^ ky, c ^ kc)
            cps.append(pltpu.make_async_remote_copy(
                src_ref=p_ref, dst_ref=buf.at[k], send_sem=send_sems.at[k - 1], recv_sem=recv_sems.at[k - 1],
                device_id=peer, device_id_type=MESH_ID))
        for cp in cps:
            cp.start()
        for cp in cps:
            cp.wait()
        me = 4 * x + 2 * y + c
        total = buf[me]
        for d in range(1, N_DEV):
            total = total + buf[d ^ me]
        g_ref[...] = total
        delta, m2, v2 = _adam(w_ref[...], total, m_ref[...], v_ref[...])
        d_ref[...] = delta
        mo_ref[...] = m2
        vo_ref[...] = v2

    vm = pl.BlockSpec(memory_space=pltpu.VMEM)
    return pl.pallas_call(
        body, name=name,
        out_shape=[jax.ShapeDtypeStruct(part.shape, F32)] * 4,
        in_specs=[vm] * 4, out_specs=[vm] * 4,
        scratch_shapes=[pltpu.VMEM((N_DEV, rows, LANES), F32),
                        pltpu.SemaphoreType.DMA((N_DEV - 1,)), pltpu.SemaphoreType.DMA((N_DEV - 1,))],
    )(part, w, m, v)


def _final_adam(g8, land, w, m, v, me, dep, name, pair=None):
    _, r, c = g8.shape
    tr = max(q for q in range(16, r + 1, 16) if r % q == 0 and q * c <= ADAM_TILE_ELEMS)
    nland = land.shape[0]

    def body(me_ref, g_ref, land_ref, *rest):
        pair_ref = rest[0] if pair is not None else None
        w_ref, m_ref, v_ref, _, go_ref, d_ref, mo_ref, vo_ref = rest[-8:]
        g = g_ref[...]
        if pair_ref is not None:
            g = g + pair_ref[...].astype(F32)
        for k in range(nland):
            g = g + land_ref[k].astype(F32)
        go_ref[...] = g
        delta, m2, v2 = _adam(w_ref[...], g, m_ref[...], v_ref[...])
        d_ref[...] = delta
        mo_ref[...] = m2
        vo_ref[...] = v2

    plain = pl.BlockSpec((tr, c), lambda i, s: (i, 0))
    return pl.pallas_call(
        body, name=name,
        out_shape=[jax.ShapeDtypeStruct((r, c), F32)] * 4,
        grid_spec=pltpu.PrefetchScalarGridSpec(
            num_scalar_prefetch=1, grid=(r // tr,),
            in_specs=[pl.BlockSpec((None, tr, c), lambda i, s: (s[0], i, 0)),
                      pl.BlockSpec((nland, tr, c), lambda i, s: (0, i, 0))]
            + ([] if pair is None else [pl.BlockSpec((None, tr, c), lambda i, s: (s[1], i, 0))])
            + [plain, plain, plain, ANY],
            out_specs=[plain] * 4),
        compiler_params=_params(("arbitrary",)),
    )(*((me, g8, land) + (() if pair is None else (pair,)) + (w, m, v, dep)))


def _rms(x, gain):
    r = lax.rsqrt(jnp.mean(x * x, axis=-1, keepdims=True) + EPS)
    xh = x * r
    return xh * gain, xh, r


def _rms_bwd(xh, r, gain, dy):
    gdy = gain * dy
    dx = r * (gdy - xh * jnp.mean(xh * gdy, axis=-1, keepdims=True))
    return dx, jnp.sum(dy * xh, axis=0, keepdims=True)


def _load_weights(pairs, sems):
    cps = [pltpu.make_async_copy(src, dst, sems.at[i]) for i, (src, dst) in enumerate(pairs)]
    for cp in cps:
        cp.start()
    for cp in cps:
        cp.wait()


def _ffn_fwd(h, gain, wgu, wd, name):
    t, d = h.shape
    nb, nf, _ = wgu.shape
    nh = nb // 2
    tm = _row_tile(t, 512)

    def body(h_ref, g_ref, wgu_hbm, wd_hbm, out_ref, gu_ref, wgu_v, wd_v, sems):
        @pl.when(pl.program_id(0) == 0)
        def _():
            _load_weights([(wgu_hbm, wgu_v), (wd_hbm, wd_v)], sems)

        x = h_ref[...]
        n, _, _ = _rms(x, g_ref[...])
        nbf = n.astype(BF16)
        acc = jnp.zeros((tm, d), F32)
        for j in range(nh):
            g = _dot_nt(nbf, wgu_v[j])
            u = _dot_nt(nbf, wgu_v[j + nh])
            gu_ref[j] = g.astype(BF16)
            gu_ref[j + nh] = u.astype(BF16)
            a = (g * jax.nn.sigmoid(g)) * u
            acc = acc + _dot(a.astype(BF16), wd_v[j])
        out_ref[...] = x + 0.5 * acc

    return pl.pallas_call(
        body, name=name, grid=(t // tm,),
        out_shape=[jax.ShapeDtypeStruct((t, d), F32), jax.ShapeDtypeStruct((nb, t, nf), BF16)],
        in_specs=[pl.BlockSpec((tm, d), lambda i: (i, 0)), pl.BlockSpec((1, d), lambda i: (0, 0)), ANY, ANY],
        out_specs=[pl.BlockSpec((tm, d), lambda i: (i, 0)), pl.BlockSpec((nb, tm, nf), lambda i: (0, i, 0))],
        scratch_shapes=[pltpu.VMEM(wgu.shape, BF16), pltpu.VMEM(wd.shape, BF16), pltpu.SemaphoreType.DMA((2,))],
        compiler_params=_params(("arbitrary",)),
    )(h, gain, wgu, wd)


def _ffn_bwd(dh, h, gain, gu, wgu, wd, name):
    t, d = h.shape
    nb, nf, _ = wgu.shape
    nh = nb // 2
    tm = _row_tile(t, 256)

    def body(dh_ref, h_ref, g_ref, gu_ref, wgu_hbm, wd_hbm, dhp_ref, dgu_ref, a_ref, n_ref, dgain_ref,
             wgu_v, wd_v, sems):
        @pl.when(pl.program_id(0) == 0)
        def _():
            _load_weights([(wgu_hbm, wgu_v), (wd_hbm, wd_v)], sems)
            dgain_ref[...] = jnp.zeros_like(dgain_ref)

        x = h_ref[...]
        gain_v = g_ref[...]
        n, xh, r = _rms(x, gain_v)
        n_ref[...] = n.astype(BF16)
        dh_v = dh_ref[...]
        dfb = (0.5 * dh_v).astype(BF16)
        dn = jnp.zeros((tm, d), F32)
        for j in range(nh):
            da = _dot_nt(dfb, wd_v[j])
            g = gu_ref[j].astype(F32)
            u = gu_ref[j + nh].astype(F32)
            sg = jax.nn.sigmoid(g)
            si = g * sg
            dg = (da * u * (sg * (1.0 + g * (1.0 - sg)))).astype(BF16)
            du = (da * si).astype(BF16)
            a_ref[j] = (si * u).astype(BF16)
            dgu_ref[j] = dg
            dgu_ref[j + nh] = du
            dn = dn + _dot(dg, wgu_v[j]) + _dot(du, wgu_v[j + nh])
        dx, dgain = _rms_bwd(xh, r, gain_v, dn)
        dhp_ref[...] = dh_v + dx
        dgain_ref[...] += dgain

    row = pl.BlockSpec((tm, d), lambda i: (i, 0))
    vec = pl.BlockSpec((1, d), lambda i: (0, 0))
    return pl.pallas_call(
        body, name=name, grid=(t // tm,),
        out_shape=[jax.ShapeDtypeStruct((t, d), F32), jax.ShapeDtypeStruct((nb, t, nf), BF16),
                   jax.ShapeDtypeStruct((nh, t, nf), BF16), jax.ShapeDtypeStruct((t, d), BF16),
                   jax.ShapeDtypeStruct((1, d), F32)],
        in_specs=[row, row, vec, pl.BlockSpec((nb, tm, nf), lambda i: (0, i, 0)), ANY, ANY],
        out_specs=[row, pl.BlockSpec((nb, tm, nf), lambda i: (0, i, 0)),
                   pl.BlockSpec((nh, tm, nf), lambda i: (0, i, 0)), row, vec],
        scratch_shapes=[pltpu.VMEM(wgu.shape, BF16), pltpu.VMEM(wd.shape, BF16), pltpu.SemaphoreType.DMA((2,))],
        compiler_params=_params(("arbitrary",)),
    )(dh, h, gain, gu, wgu, wd)


def _dw(xa, dy, nb, n, name, scale=1.0, dep=None):
    t, k = xa.shape[-2:]
    tt = _row_tile(t, 512)
    steps = t // tt
    wide = dy.ndim == 2 and xa.ndim == 2
    if xa.ndim == 3:
        x_spec = pl.BlockSpec((nb, tt, k), lambda i: (0, i, 0))
    else:
        x_spec = pl.BlockSpec((tt, k), lambda i: (i, 0))
    if dy.ndim == 3:
        dy_spec = pl.BlockSpec((nb, tt, n), lambda i: (0, i, 0))
    else:
        dy_spec = pl.BlockSpec((tt, dy.shape[1]), lambda i: (i, 0))
    acc_shape = (k, nb * n) if wide else (nb, k, n)
    stage_shape = (k, nb * n) if wide else (k, n)

    def body(x_ref, dy_ref, *rest):
        o_hbm, ob_hbm, acc, stage, sems = rest[-5:]

        @pl.when(pl.program_id(0) == 0)
        def _():
            acc[...] = jnp.zeros_like(acc)

        if wide:
            acc[...] += _dot(x_ref[...].astype(BF16).T, dy_ref[...].astype(BF16))
        elif xa.ndim == 2:
            xt = x_ref[...].astype(BF16).T
            for j in range(nb):
                acc[j] += _dot(xt, dy_ref[j].astype(BF16))
        else:
            dyb = dy_ref[...].astype(BF16)
            for j in range(nb):
                acc[j] += _dot_tn(x_ref[j].astype(BF16), dyb)

        @pl.when(pl.program_id(0) == steps - 1)
        def _():
            if scale != 1.0:
                acc[...] = acc[...] * scale
            if wide:
                cps = [pltpu.make_async_copy(acc.at[:, pl.ds(j * n, n)] if nb > 1 else acc, o_hbm.at[j], sems.at[j])
                       for j in range(nb)]
            else:
                cps = [pltpu.make_async_copy(acc, o_hbm, sems.at[0])]
            for cp in cps:
                cp.start()
            if wide:
                stage[...] = acc[...].astype(BF16)
                bcs = [pltpu.make_async_copy(stage.at[:, pl.ds(j * n, n)] if nb > 1 else stage, ob_hbm.at[j],
                                             sems.at[nb + j]) for j in range(nb)]
                for cp in bcs:
                    cp.start()
                for cp in bcs:
                    cp.wait()
            else:
                for j in range(nb):
                    stage[...] = acc[j].astype(BF16)
                    cp = pltpu.make_async_copy(stage, ob_hbm.at[j], sems.at[nb])
                    cp.start()
                    cp.wait()
            for cp in cps:
                cp.wait()

    return pl.pallas_call(
        body, name=name, grid=(steps,),
        out_shape=[jax.ShapeDtypeStruct((nb, k, n), F32), jax.ShapeDtypeStruct((nb, k, n), BF16)],
        in_specs=[x_spec, dy_spec] + ([] if dep is None else [ANY]),
        out_specs=[ANY, ANY],
        scratch_shapes=[pltpu.VMEM(acc_shape, F32), pltpu.VMEM(stage_shape, BF16),
                        pltpu.SemaphoreType.DMA((2 * nb,))],
        compiler_params=_params(("arbitrary",)),
    )(*((xa, dy) if dep is None else (xa, dy, dep)))


def _proj_fwd(h, gain, win, wgate, name):
    t, d = h.shape
    tm = _row_tile(t, 256)
    nq, ng = win.shape[0], wgate.shape[1]

    def body(h_ref, g_ref, win_ref, wg_ref, un_ref, qkv_ref, gate_ref):
        n, _, _ = _rms(h_ref[...], g_ref[...])
        nbf = n.astype(BF16)
        un_ref[...] = nbf
        qkv_ref[...] = _dot_nt(nbf, win_ref[...])
        gate_ref[...] = jax.nn.sigmoid(_dot(nbf, wg_ref[...]))

    full = lambda a: pl.BlockSpec(a.shape, lambda i: (0,) * a.ndim)
    return pl.pallas_call(
        body, name=name, grid=(t // tm,),
        out_shape=[jax.ShapeDtypeStruct((t, d), BF16), jax.ShapeDtypeStruct((t, nq), F32),
                   jax.ShapeDtypeStruct((t, ng), F32)],
        in_specs=[pl.BlockSpec((tm, d), lambda i: (i, 0)), full(gain), full(win), full(wgate)],
        out_specs=[pl.BlockSpec((tm, d), lambda i: (i, 0)), pl.BlockSpec((tm, nq), lambda i: (i, 0)),
                   pl.BlockSpec((tm, ng), lambda i: (i, 0))],
        compiler_params=_params(("arbitrary",)),
    )(h, gain, win, wgate)


def _proj_bwd(dh, h, gain, dzg, dqkv_parts, win, wgate, name):
    t, d = h.shape
    tm = _row_tile(t, 256)
    ng = wgate.shape[1]
    np_ = len(dqkv_parts)
    widths = [a.shape[1] for a in dqkv_parts]

    def body(dh_ref, h_ref, g_ref, dzg_ref, *rest):
        part_refs, (win_ref, wg_ref, dhp_ref, dgain_ref) = rest[:np_], rest[np_:]

        @pl.when(pl.program_id(0) == 0)
        def _():
            dgain_ref[...] = jnp.zeros_like(dgain_ref)

        gain_v = g_ref[...]
        _, xh, r = _rms(h_ref[...], gain_v)
        dun = _dot_nt(dzg_ref[...], wg_ref[...])
        off = 0
        for ref, wd in zip(part_refs, widths):
            dun = dun + _dot(ref[...].astype(BF16), win_ref[off:off + wd, :])
            off += wd
        dx, dgain = _rms_bwd(xh, r, gain_v, dun)
        dhp_ref[...] = dh_ref[...] + dx
        dgain_ref[...] += dgain

    full = lambda a: pl.BlockSpec(a.shape, lambda i: (0,) * a.ndim)
    row = pl.BlockSpec((tm, d), lambda i: (i, 0))
    return pl.pallas_call(
        body, name=name, grid=(t // tm,),
        out_shape=[jax.ShapeDtypeStruct((t, d), F32), jax.ShapeDtypeStruct((1, d), F32)],
        in_specs=[row, row, full(gain), pl.BlockSpec((tm, ng), lambda i: (i, 0))]
        + [pl.BlockSpec((tm, wd), lambda i: (i, 0)) for wd in widths] + [full(win), full(wgate)],
        out_specs=[row, pl.BlockSpec((1, d), lambda i: (0, 0))],
        compiler_params=_params(("arbitrary",)),
    )(dh, h, gain, dzg, *dqkv_parts, win, wgate)


def _dw_rows(parts, dy, name):
    t, n = dy.shape
    widths = [a.shape[1] for a in parts]
    k = sum(widths)
    tt = _row_tile(t, 512)
    steps = t // tt
    np_ = len(parts)

    def body(*refs):
        part_refs, dy_ref = refs[:np_], refs[np_]
        o_hbm, ob_hbm, acc, stage, sems = refs[np_ + 1:]

        @pl.when(pl.program_id(0) == 0)
        def _():
            acc[...] = jnp.zeros_like(acc)

        dyb = dy_ref[...].astype(BF16)
        off = 0
        for ref, wd in zip(part_refs, widths):
            acc[off:off + wd, :] += _dot(ref[...].astype(BF16).T, dyb)
            off += wd

        @pl.when(pl.program_id(0) == steps - 1)
        def _():
            stage[...] = acc[...].astype(BF16)
            cps = [pltpu.make_async_copy(acc, o_hbm.at[0], sems.at[0]),
                   pltpu.make_async_copy(stage, ob_hbm.at[0], sems.at[1])]
            for cp in cps:
                cp.start()
            for cp in cps:
                cp.wait()

    return pl.pallas_call(
        body, name=name, grid=(steps,),
        out_shape=[jax.ShapeDtypeStruct((1, k, n), F32), jax.ShapeDtypeStruct((1, k, n), BF16)],
        in_specs=[pl.BlockSpec((tt, wd), lambda i: (i, 0)) for wd in widths] + [pl.BlockSpec((tt, n), lambda i: (i, 0))],
        out_specs=[ANY, ANY],
        scratch_shapes=[pltpu.VMEM((k, n), F32), pltpu.VMEM((k, n), BF16), pltpu.SemaphoreType.DMA((2,))],
        compiler_params=_params(("arbitrary",)),
    )(*parts, dy)


def _merge_fwd(h, ya, yb, gate, wpa, wpb, wout, name):
    t, d = h.shape
    tm = _row_tile(t, 256)

    def body(h_ref, ya_ref, yb_ref, ga_ref, gb_ref, wpa_ref, wpb_ref, wout_ref, out_ref, mg_ref, pa_ref, pb_ref):
        pa = _dot(ya_ref[...].astype(BF16), wpa_ref[...])
        pb = _dot(yb_ref[...].astype(BF16), wpb_ref[...])
        merged = (ga_ref[...] * pa + gb_ref[...] * pb).astype(BF16)
        pa_ref[...] = pa.astype(BF16)
        pb_ref[...] = pb.astype(BF16)
        mg_ref[...] = merged
        out_ref[...] = h_ref[...] + _dot(merged, wout_ref[...])

    full = lambda a: pl.BlockSpec(a.shape, lambda i: (0,) * a.ndim)
    row = pl.BlockSpec((tm, d), lambda i: (i, 0))
    yrow = pl.BlockSpec((tm, ya.shape[1]), lambda i: (i, 0))
    return pl.pallas_call(
        body, name=name, grid=(t // tm,),
        out_shape=[jax.ShapeDtypeStruct((t, d), F32)] + [jax.ShapeDtypeStruct((t, d), BF16)] * 3,
        in_specs=[row, yrow, yrow, pl.BlockSpec((tm, d), lambda i: (i, 0)), pl.BlockSpec((tm, d), lambda i: (i, 1)),
                  full(wpa), full(wpb), full(wout)],
        out_specs=[row] * 4,
        compiler_params=_params(("arbitrary",)),
    )(h, ya, yb, gate, gate, wpa, wpb, wout)


def _merge_bwd(dh, pa, pb, gate, wpa, wpb, wout, name):
    t, d = dh.shape
    tm = _row_tile(t, 256)
    wy = wpa.shape[0]

    def body(dh_ref, pa_ref, pb_ref, ga_ref, gb_ref, wpa_ref, wpb_ref, wout_ref,
             dpa_ref, dpb_ref, dzg_ref, dya_ref, dyb_ref):
        dm = _dot_nt(dh_ref[...].astype(BF16), wout_ref[...])
        ga, gb = ga_ref[...], gb_ref[...]
        dpa = (dm * ga).astype(BF16)
        dpb = (dm * gb).astype(BF16)
        dpa_ref[...] = dpa
        dpb_ref[...] = dpb
        dzg_ref[:, :d] = (dm * pa_ref[...].astype(F32) * ga * (1.0 - ga)).astype(BF16)
        dzg_ref[:, d:] = (dm * pb_ref[...].astype(F32) * gb * (1.0 - gb)).astype(BF16)
        dya_ref[...] = _dot_nt(dpa, wpa_ref[...])
        dyb_ref[...] = _dot_nt(dpb, wpb_ref[...])

    full = lambda a: pl.BlockSpec(a.shape, lambda i: (0,) * a.ndim)
    row = pl.BlockSpec((tm, d), lambda i: (i, 0))
    yrow = pl.BlockSpec((tm, wy), lambda i: (i, 0))
    return pl.pallas_call(
        body, name=name, grid=(t // tm,),
        out_shape=[jax.ShapeDtypeStruct((t, d), BF16), jax.ShapeDtypeStruct((t, d), BF16),
                   jax.ShapeDtypeStruct((t, 2 * d), BF16), jax.ShapeDtypeStruct((t, wy), F32),
                   jax.ShapeDtypeStruct((t, wy), F32)],
        in_specs=[row, row, row, pl.BlockSpec((tm, d), lambda i: (i, 0)), pl.BlockSpec((tm, d), lambda i: (i, 1)),
                  full(wpa), full(wpb), full(wout)],
        out_specs=[row, row, pl.BlockSpec((tm, 2 * d), lambda i: (i, 0)), yrow, yrow],
        compiler_params=_params(("arbitrary",)),
    )(dh, pa, pb, gate, gate, wpa, wpb, wout)


def _ple_loss(h, gain, p, target, wpg, wpe, name):
    t, d = h.shape
    tm = _row_tile(t, 256)
    pd = p.shape[1]

    def body(h_ref, g_ref, p_ref, t_ref, wpg_ref, wpe_ref, dh_ref, dz_ref, dpp_ref, n_ref, dgain_ref, loss_ref):
        @pl.when(pl.program_id(0) == 0)
        def _():
            dgain_ref[...] = jnp.zeros_like(dgain_ref)
            loss_ref[...] = jnp.zeros_like(loss_ref)

        x = h_ref[...]
        gain_v = g_ref[...]
        n, xh, r = _rms(x, gain_v)
        nbf = n.astype(BF16)
        n_ref[...] = nbf
        pg = jax.nn.sigmoid(_dot(nbf, wpg_ref[...]))
        pp = _dot(p_ref[...].astype(BF16), wpe_ref[...])
        err = (x + pg * pp) - t_ref[...]
        loss_ref[...] += 0.5 * jnp.sum(jnp.mean(err * err, axis=-1, keepdims=True))
        dy = err * (1.0 / d)
        dpp_ref[...] = (dy * pg).astype(BF16)
        dz = (dy * pp * pg * (1.0 - pg)).astype(BF16)
        dz_ref[...] = dz
        dn = _dot_nt(dz, wpg_ref[...])
        dx, dgain = _rms_bwd(xh, r, gain_v, dn)
        dh_ref[...] = dy + dx
        dgain_ref[...] += dgain

    full = lambda a: pl.BlockSpec(a.shape, lambda i: (0,) * a.ndim)
    row = pl.BlockSpec((tm, d), lambda i: (i, 0))
    return pl.pallas_call(
        body, name=name, grid=(t // tm,),
        out_shape=[jax.ShapeDtypeStruct((t, d), F32), jax.ShapeDtypeStruct((t, d), BF16),
                   jax.ShapeDtypeStruct((t, d), BF16), jax.ShapeDtypeStruct((t, d), BF16),
                   jax.ShapeDtypeStruct((1, d), F32), jax.ShapeDtypeStruct((8, LANES), F32)],
        in_specs=[row, full(gain), pl.BlockSpec((tm, pd), lambda i: (i, 0)), row, full(wpg), full(wpe)],
        out_specs=[row, row, row, row, pl.BlockSpec((1, d), lambda i: (0, 0)),
                   pl.BlockSpec((8, LANES), lambda i: (0, 0))],
        compiler_params=_params(("arbitrary",)),
    )(h, gain, p, target, wpg, wpe)


def _head_masks():
    lane = lax.broadcasted_iota(jnp.int32, (1, LANES), 1)
    m0 = (lane < HEAD_DIM).astype(F32)
    return m0, 1.0 - m0


def _head_mean(v, m0, m1):
    del m0, m1
    width = v.shape[-1]
    shift = HEAD_DIM.bit_length() - 1
    r = jnp.right_shift(lax.broadcasted_iota(jnp.int32, (width, width), 0), shift)
    c = jnp.right_shift(lax.broadcasted_iota(jnp.int32, (width, width), 1), shift)
    same_head = (r == c).astype(BF16)
    return _dot(v.astype(BF16), same_head) * (1.0 / HEAD_DIM)


def _head_norm(x, gain, m0, m1):
    r = lax.rsqrt(_head_mean(x * x, m0, m1) + EPS)
    xh = x * r
    return xh * gain, xh, r


def _head_norm_bwd(xh, r, gain, dy, m0, m1):
    gdy = gain * dy
    dx = r * (gdy - xh * _head_mean(xh * gdy, m0, m1))
    return dx, jnp.sum(dy * xh, axis=0, keepdims=True)


GROUP = 4
QW = GROUP * HEAD_DIM
STACK = GROUP * QTILE


def _kv_width(mode):
    return QW if mode == "A" else LANES


def _q_scratch_shape(mode, s_len):
    return (s_len, QW) if mode == "A" else (GROUP * s_len, LANES)


def _group_masks(dtype=F32):
    lane = lax.broadcasted_iota(jnp.int32, (1, QW), 1)
    return [((lane >= h * HEAD_DIM) & (lane < (h + 1) * HEAD_DIM)).astype(dtype) for h in range(GROUP)]


def _stack_heads(first_kv, x, m0, m1):
    out = []
    for half in range(GROUP // 2):
        xh = x[:, half * LANES:(half + 1) * LANES]
        a0, a1 = xh * m0, xh * m1
        r0, r1 = pltpu.roll(a0, HEAD_DIM, 1), pltpu.roll(a1, HEAD_DIM, 1)
        out += [jnp.where(first_kv, a0, r0), jnp.where(first_kv, r1, a1)]
    return out


def _unstack_heads(mode, first_kv, ts, m0, m1):
    if mode == "A":
        masks = _group_masks()
        return sum(t * mk for t, mk in zip(ts, masks))
    halves = []
    for half in range(GROUP // 2):
        t0 = jnp.where(first_kv, ts[2 * half], pltpu.roll(ts[2 * half], HEAD_DIM, 1))
        t1 = jnp.where(first_kv, pltpu.roll(ts[2 * half + 1], HEAD_DIM, 1), ts[2 * half + 1])
        halves.append(t0 * m0 + t1 * m1)
    return jnp.concatenate(halves, axis=1)


def _store_stacked(dst, i, heads):
    for half in range(2):
        rows = slice(half * QTILE, (half + 1) * QTILE)
        for h, x in enumerate(heads):
            dst[pl.ds((2 * i + half) * STACK + h * QTILE, QTILE), :] = x[rows].astype(dst.dtype)


def _load_stacked(mode, ref, m):
    if mode == "B":
        return ref[pl.ds(pl.multiple_of(m * STACK, STACK), STACK), :]
    x = ref[pl.ds(pl.multiple_of(m * QTILE, QTILE), QTILE), :]
    return jnp.concatenate([x * mk for mk in _group_masks(x.dtype)], axis=0)


def _attn_prep(mode, group, s_len, padk, q_ref, k_ref, v_ref, gq_ref, gk_ref, qs, k2, v2, do_ref=None, dos=None):
    m0, m1 = _head_masks()
    zpad = jnp.zeros((padk, k2.shape[1]), BF16)
    k2[pl.ds(0, padk), :] = zpad
    v2[pl.ds(0, padk), :] = zpad
    first_kv = group == 0
    rt = 2 * QTILE
    for i in range(s_len // rt):
        rows = pl.ds(i * rt, rt)
        qn, _, _ = _head_norm(q_ref[rows, :], gq_ref[...], m0, m1)
        kn, _, _ = _head_norm(k_ref[rows, :], gk_ref[...], m0, m1)
        qn = qn * (HEAD_DIM ** -0.5)
        if mode == "A":
            qs[rows, :] = qn.astype(BF16)
            if dos is not None:
                dos[rows, :] = do_ref[rows, :].astype(BF16)
        else:
            _store_stacked(qs, i, _stack_heads(first_kv, qn, m0, m1))
            if dos is not None:
                _store_stacked(dos, i, _stack_heads(first_kv, do_ref[rows, :], m0, m1))
        k2[pl.ds(padk + i * rt, rt), :] = kn.astype(BF16)
        v2[pl.ds(padk + i * rt, rt), :] = v_ref[rows, :].astype(BF16)


def _softmax_terms(mode, s, sink):
    mx = jnp.max(s, axis=-1, keepdims=True)
    if mode == "B":
        mx = jnp.maximum(mx, sink)
    e = jnp.exp(s - mx)
    l = jnp.sum(e, axis=-1, keepdims=True)
    if mode == "B":
        l = l + jnp.exp(sink - mx)
    return e, mx, l


def _sink_column(sink_ref, group):
    row = lax.broadcasted_iota(jnp.int32, (STACK, 1), 0)
    col = jnp.zeros((STACK, 1), F32)
    for h in range(GROUP):
        col = jnp.where((row >= h * QTILE) & (row < (h + 1) * QTILE), sink_ref[GROUP * group + h], col)
    return col


def _head_deltas(dd, m0, m1):
    cols = []
    for half in range(GROUP // 2):
        dh = dd[:, half * LANES:(half + 1) * LANES]
        cols += [jnp.sum(dh * m0, axis=-1, keepdims=True), jnp.sum(dh * m1, axis=-1, keepdims=True)]
    return jnp.concatenate(cols, axis=0)


def _attn_cols(mode):
    if mode == "A":
        return (lambda b, g: (b, g)), (lambda b, g: (b, 2 + g)), (lambda b, g: (b, 4 + g))
    return (lambda b, g: (b, 6 + g)), (lambda b, g: (b, 16)), (lambda b, g: (b, 17))


def _attn_fwd(mode, qkv, gq, gk, bias, sinks, bl, s_len, name):
    bw = bias.shape[-1]
    padk = bw - QTILE
    nt = s_len // QTILE
    qmap, kmap, vmap = _attn_cols(mode)

    kw = _kv_width(mode)

    def body(q_ref, k_ref, v_ref, gq_ref, gk_ref, bias_ref, sink_ref, o_ref, qs, k2, v2, s_buf, *rest):
        o_buf = rest[0] if rest else None
        group = pl.program_id(1)
        m0, m1 = _head_masks()
        first_kv = group == 0
        _attn_prep(mode, group, s_len, padk, q_ref, k_ref, v_ref, gq_ref, gk_ref, qs, k2, v2)
        col = lax.broadcasted_iota(jnp.int32, (STACK, bw), 1)
        sink = _sink_column(sink_ref, group)

        def scores(m, slot):
            r0 = pl.multiple_of(m * QTILE, QTILE)
            s = _dot_nt(_load_stacked(mode, qs, m), k2[pl.ds(r0, bw), :]) + bias_ref[...]
            s_buf[slot] = jnp.where(col >= (padk - r0), s, NEG_INF)

        def finish_tile(m, slot):
            r0 = pl.multiple_of(m * QTILE, QTILE)
            e, _, l = _softmax_terms(mode, s_buf[slot], sink)
            if mode == "A":
                o_st = _dot(e.astype(BF16), v2[pl.ds(r0, bw), :]) / l
                heads = [o_st[h * QTILE:(h + 1) * QTILE] for h in range(GROUP)]
                o_ref[pl.ds(r0, QTILE), :] = _unstack_heads(mode, first_kv, heads, m0, m1)
            else:
                o_buf[pl.ds(pl.multiple_of(m * STACK, STACK), STACK), :] = _dot((e * (1.0 / l)).astype(BF16),
                                                                                 v2[pl.ds(r0, bw), :])

        scores(0, 0)

        def pair(j, carry):
            scores(2 * j + 1, 1)
            finish_tile(2 * j, 0)
            scores(jnp.minimum(2 * j + 2, nt - 1), 0)
            finish_tile(2 * j + 1, 1)
            return carry

        lax.fori_loop(0, nt // 2, pair, 0)
        if mode == "B":
            for m in range(nt):
                heads = [o_buf[pl.ds(m * STACK + h * QTILE, QTILE), :] for h in range(GROUP)]
                o_ref[pl.ds(m * QTILE, QTILE), :] = _unstack_heads(mode, first_kv, heads, m0, m1)

    blk = lambda w, f: pl.BlockSpec((s_len, w), f)
    return pl.pallas_call(
        body, name=name, grid=(bl, B_Q_HEADS // GROUP),
        out_shape=jax.ShapeDtypeStruct((bl * s_len, B_Q_HEADS * HEAD_DIM), F32),
        in_specs=[blk(QW, qmap), blk(kw, kmap), blk(kw, vmap),
                  pl.BlockSpec((1, QW), lambda b, g: (0, 0)), pl.BlockSpec((1, kw), lambda b, g: (0, 0)),
                  pl.BlockSpec((STACK, bw), lambda b, g: (g, 0)),
                  pl.BlockSpec(memory_space=pltpu.SMEM)],
        out_specs=blk(QW, lambda b, g: (b, g)),
        scratch_shapes=[pltpu.VMEM(_q_scratch_shape(mode, s_len), BF16)] + [pltpu.VMEM((s_len + padk, kw), BF16)] * 2
        + [pltpu.VMEM((2, STACK, bw), F32)] + ([pltpu.VMEM((GROUP * s_len, LANES), F32)] if mode == "B" else []),
        compiler_params=_params(("arbitrary", "arbitrary")),
    )(qkv, qkv, qkv, gq, gk, bias.reshape(-1, bw), sinks)


def _attn_bwd(mode, qkv, gq, gk, bias, sinks, y, dy, bl, s_len, name):
    bw = bias.shape[-1]
    padk = bw - QTILE
    nt = s_len // QTILE
    qmap, kmap, vmap = _attn_cols(mode)
    t = bl * s_len
    kw = _kv_width(mode)
    kvw = 4 * LANES if mode == "A" else LANES
    dp_ahead = True

    def body(q_ref, k_ref, v_ref, gq_ref, gk_ref, bias_ref, sink_ref, y_ref, dy_ref,
             dq_ref, dk_ref, dv_ref, dgq_ref, dgk_ref, dbias_ref, dsink_ref,
             qs, k2, v2, dos, dqs, dk, dv, s_buf, dp_buf):
        group = pl.program_id(1)
        m0, m1 = _head_masks()
        first_kv = group == 0
        _attn_prep(mode, group, s_len, padk, q_ref, k_ref, v_ref, gq_ref, gk_ref, qs, k2, v2, dy_ref, dos)
        dk[...] = jnp.zeros_like(dk)
        dv[...] = jnp.zeros_like(dv)
        dbias_ref[...] = jnp.zeros_like(dbias_ref)
        col = lax.broadcasted_iota(jnp.int32, (STACK, bw), 1)
        lane8 = lax.broadcasted_iota(jnp.int32, (8, LANES), 1)
        sink = _sink_column(sink_ref, group)

        def ahead(m, slot):
            r0 = pl.multiple_of(m * QTILE, QTILE)
            band = pl.ds(r0, bw)
            s = _dot_nt(_load_stacked(mode, qs, m), k2[band, :]) + bias_ref[...]
            s_buf[slot] = jnp.where(col >= (padk - r0), s, NEG_INF)
            if dp_ahead:
                dp_buf[slot] = _dot_nt(_load_stacked(mode, dos, m), v2[band, :])

        def tile(m, slot, dsink):
            r0 = pl.multiple_of(m * QTILE, QTILE)
            rows = pl.ds(r0, QTILE)
            band = pl.ds(r0, bw)
            q_st = _load_stacked(mode, qs, m)
            do_st = _load_stacked(mode, dos, m)
            delta = _head_deltas(dy_ref[rows, :] * y_ref[rows, :], m0, m1)
            kb = k2[band, :]
            e, mx, l = _softmax_terms(mode, s_buf[slot], sink)
            inv = 1.0 / l
            pn = e * inv
            ds = pn * ((dp_buf[slot] if dp_ahead else _dot_nt(do_st, v2[band, :])) - delta)
            if mode == "A":
                dbias_ref[...] += ds
            else:
                part = jnp.exp(sink - mx) * inv * delta
                for h in range(GROUP):
                    dsink = dsink - jnp.where(lane8 == h, jnp.sum(part[h * QTILE:(h + 1) * QTILE]), 0.0)
            dsb = ds.astype(BF16)
            dv[band, :] += _dot_tn(pn.astype(BF16), do_st)
            dk[band, :] += _dot_tn(dsb, q_st)
            dq_st = _dot(dsb, kb)
            if mode == "A":
                heads = [dq_st[h * QTILE:(h + 1) * QTILE] for h in range(GROUP)]
                dq_ref[rows, :] = _unstack_heads(mode, first_kv, heads, m0, m1)
            else:
                dqs[pl.ds(pl.multiple_of(m * STACK, STACK), STACK), :] = dq_st
            return dsink

        ahead(0, 0)

        def pair(j, dsink):
            ahead(2 * j + 1, 1)
            dsink = tile(2 * j, 0, dsink)
            ahead(jnp.minimum(2 * j + 2, nt - 1), 0)
            return tile(2 * j + 1, 1, dsink)

        dsink = lax.fori_loop(0, nt // 2, pair, jnp.zeros((8, LANES), F32))
        dsink_ref[...] = dsink

        rt = 2 * QTILE
        dgq = jnp.zeros((1, QW), F32)
        dgk = jnp.zeros((1, kw), F32)
        for i in range(s_len // rt):
            rows = pl.ds(i * rt, rt)
            src = pl.ds(padk + i * rt, rt)
            gq_v, gk_v = gq_ref[...], gk_ref[...]
            _, qh, qr = _head_norm(q_ref[rows, :], gq_v, m0, m1)
            _, kh, kr = _head_norm(k_ref[rows, :], gk_v, m0, m1)
            if mode == "A":
                dqn = dq_ref[rows, :] * (HEAD_DIM ** -0.5)
            else:
                dqn = jnp.concatenate(
                    [_unstack_heads(mode, first_kv, [dqs[pl.ds((2 * i + half) * STACK + h * QTILE, QTILE), :]
                                                     for h in range(GROUP)], m0, m1)
                     for half in range(2)], axis=0) * (HEAD_DIM ** -0.5)
            dq_raw, dgq_i = _head_norm_bwd(qh, qr, gq_v, dqn, m0, m1)
            dk_raw, dgk_i = _head_norm_bwd(kh, kr, gk_v, dk[src, :], m0, m1)
            dvn = dv[src, :]
            dq_ref[rows, :] = dq_raw
            if mode == "A":
                dk_ref[rows, :] = dk_raw
                dv_ref[rows, :] = dvn
            else:
                @pl.when(group == 0)
                def _():
                    dk_ref[rows, :] = dk_raw
                    dv_ref[rows, :] = dvn

                @pl.when(group != 0)
                def _():
                    dk_ref[rows, :] += dk_raw
                    dv_ref[rows, :] += dvn
            dgq, dgk = dgq + dgq_i, dgk + dgk_i
        dgq_ref[...] = jnp.broadcast_to(dgq, (8, QW))
        dgk_ref[...] = jnp.broadcast_to(dgk, (8, kw))

    ng = B_Q_HEADS // GROUP
    blk = lambda w, f: pl.BlockSpec((s_len, w), f)
    small = lambda w: pl.BlockSpec((None, None, 8, w), lambda b, g: (b, g, 0, 0))
    own = lambda b, g: (b, g)
    kvmap = own if mode == "A" else (lambda b, g: (b, 0))
    pad_f32 = pltpu.VMEM((s_len + padk, kw), F32)
    pad_bf = pltpu.VMEM((s_len + padk, kw), BF16)
    stack_bf = pltpu.VMEM(_q_scratch_shape(mode, s_len), BF16)
    outs = pl.pallas_call(
        body, name=name, grid=(bl, ng),
        out_shape=[jax.ShapeDtypeStruct((t, ng * QW), F32), jax.ShapeDtypeStruct((t, kvw), F32),
                   jax.ShapeDtypeStruct((t, kvw), F32),
                   jax.ShapeDtypeStruct((bl, ng, 8, QW), F32), jax.ShapeDtypeStruct((bl, ng, 8, kw), F32),
                   jax.ShapeDtypeStruct((bl, ng * STACK, bw), F32), jax.ShapeDtypeStruct((bl, ng, 8, LANES), F32)],
        in_specs=[blk(QW, qmap), blk(kw, kmap), blk(kw, vmap),
                  pl.BlockSpec((1, QW), lambda b, g: (0, 0)), pl.BlockSpec((1, kw), lambda b, g: (0, 0)),
                  pl.BlockSpec((STACK, bw), lambda b, g: (g, 0)),
                  pl.BlockSpec(memory_space=pltpu.SMEM),
                  blk(QW, own), blk(QW, own)],
        out_specs=[blk(QW, own), blk(kw, kvmap), blk(kw, kvmap), small(QW), small(kw),
                   pl.BlockSpec((None, STACK, bw), lambda b, g: (b, g, 0)), small(LANES)],
        scratch_shapes=[stack_bf, pad_bf, pad_bf, stack_bf,
                        pltpu.VMEM((8, LANES) if mode == "A" else _q_scratch_shape(mode, s_len), F32),
                        pad_f32, pad_f32, pltpu.VMEM((2, STACK, bw), F32),
                        pltpu.VMEM((2, STACK, bw) if dp_ahead else (8, LANES), F32)],
        compiler_params=_params(("arbitrary", "arbitrary")),
    )(qkv, qkv, qkv, gq, gk, bias.reshape(-1, bw), sinks, y, dy)
    outs = list(outs)
    outs[5] = outs[5].reshape(bl, B_Q_HEADS, QTILE, bw)
    return outs


def _band_geometry(prev):
    bw = QTILE + prev * CHUNK
    i = np.arange(QTILE)[:, None]
    j = np.arange(bw)[None, :]
    dist = i + prev * CHUNK - j
    valid = (j // CHUNK >= i // CHUNK) & (j // CHUNK <= i // CHUNK + prev)
    return dist, valid


A_VAR0 = (A_PREV * CHUNK - A_MAX_REL) // LANES * LANES


A_NVAR = QTILE + A_PREV * CHUNK - A_VAR0


def _skew_rows(x, sign):
    rows, n = x.shape
    row = lax.broadcasted_iota(jnp.int32, x.shape, 0)
    b = 1
    while b < rows:
        x = jnp.where((row & b) != 0, pltpu.roll(x, (sign * b) % n, 1), x)
        b *= 2
    return x


def _rel_bias_expand(table, name):
    _, valid = _band_geometry(A_PREV)
    bw = valid.shape[1]
    valid_f = jnp.asarray(valid.astype(np.float32))
    rev = jnp.flip(table[:, 1:], axis=1).reshape(A_HEADS, 1, A_NVAR)

    def body(rev_ref, valid_ref, o_ref):
        rowv = jnp.broadcast_to(rev_ref[...], (QTILE, A_NVAR))
        top = rowv[:, 0:1]
        var = _skew_rows(rowv, 1)
        row = lax.broadcasted_iota(jnp.int32, (QTILE, A_NVAR), 0)
        colv = lax.broadcasted_iota(jnp.int32, (QTILE, A_NVAR), 1)
        var = jnp.where(colv < row, top, var)
        ok = valid_ref[...] > 0.5
        o_ref[:, :A_VAR0] = jnp.where(ok[:, :A_VAR0], top, NEG_INF)
        o_ref[:, A_VAR0:] = jnp.where(ok[:, A_VAR0:], var, NEG_INF)

    return pl.pallas_call(
        body, name=name, grid=(A_HEADS,),
        out_shape=jax.ShapeDtypeStruct((A_HEADS, QTILE, bw), F32),
        in_specs=[pl.BlockSpec((None, 1, A_NVAR), lambda h: (h, 0, 0)), pl.BlockSpec((QTILE, bw), lambda h: (0, 0))],
        out_specs=pl.BlockSpec((None, QTILE, bw), lambda h: (h, 0, 0)),
        compiler_params=_params(("arbitrary",)),
    )(rev, valid_f)


def _rel_bias_grad(dbias, name):
    bl = dbias.shape[0]
    bw = dbias.shape[-1]

    def body(db_ref, o_ref):
        g = db_ref[0]
        for b in range(1, bl):
            g = g + db_ref[b]
        sk = _skew_rows(g[:, A_VAR0:], -1)
        row = lax.broadcasted_iota(jnp.int32, (QTILE, A_NVAR), 0)
        colv = lax.broadcasted_iota(jnp.int32, (QTILE, A_NVAR), 1)
        wrapped = (row + colv) >= A_NVAR
        main = jnp.sum(jnp.where(wrapped, 0.0, sk), axis=0, keepdims=True)
        top = jnp.sum(g[:, :A_VAR0]) + jnp.sum(jnp.where(wrapped, sk, 0.0))
        o_ref[:, :A_NVAR] = jnp.broadcast_to(main, (8, A_NVAR))
        o_ref[:, A_NVAR:] = jnp.full((8, LANES), top, F32)

    out = pl.pallas_call(
        body, name=name, grid=(A_HEADS,),
        out_shape=jax.ShapeDtypeStruct((A_HEADS, 8, A_NVAR + LANES), F32),
        in_specs=[pl.BlockSpec((bl, None, QTILE, bw), lambda h: (0, h, 0, 0))],
        out_specs=pl.BlockSpec((None, 8, A_NVAR + LANES), lambda h: (h, 0, 0)),
        compiler_params=_params(("arbitrary",)),
    )(dbias)
    main, top = out[:, 0, :A_NVAR], out[:, 0, A_NVAR]
    fm = jnp.flip(main, axis=1)
    return jnp.concatenate([jnp.zeros((A_HEADS, 1), F32), fm[:, :-1], fm[:, -1:] + top[:, None]], axis=1)


def _alibi_bias():
    dist, valid = _band_geometry(B_PREV)
    slopes = np.array([2.0 ** (-8.0 * (h + 1) / B_Q_HEADS) for h in range(B_Q_HEADS)], dtype=np.float32)
    bias = -slopes[:, None, None] * np.abs(dist).astype(np.float32)[None]
    return jnp.asarray(np.where(valid[None], bias, np.float32(NEG_INF)).astype(np.float32))


SMALL_NAMES = ("ffn1_norm", "mix_norm", "ffn2_norm", "ple_norm", "a_q_norm", "a_k_norm", "b_q_norm", "b_k_norm",
               "a_rel_bias", "b_sinks", "loss")


def _pack_small(vals):
    rows = []
    for nme in SMALL_NAMES:
        v = vals[nme].astype(F32)
        if nme == "a_rel_bias":
            v = jnp.pad(v.reshape(A_HEADS, -1), ((0, 0), (0, 3 * LANES - (2 * A_MAX_REL + 1))))
        v = v.reshape(-1)
        v = jnp.pad(v, (0, (-v.shape[0]) % LANES))
        rows.append(v.reshape(-1, LANES))
    out = jnp.concatenate(rows, axis=0)
    return jnp.pad(out, ((0, (-out.shape[0]) % 8), (0, 0)))


def _unpack_small(packed, shapes):
    out, r = {}, 0
    for nme in SMALL_NAMES:
        shp = shapes[nme]
        if nme == "a_rel_bias":
            nr = A_HEADS * 3
            out[nme] = packed[r:r + nr].reshape(A_HEADS, 3 * LANES)[:, :2 * A_MAX_REL + 1].reshape(shp)
        else:
            size = int(np.prod(shp)) if shp else 1
            nr = -(-size // LANES)
            out[nme] = packed[r:r + nr].reshape(-1)[:size].reshape(shp)
        r += nr
    return out


BIG_NAMES = ("ffn1_w_gu", "ffn1_w_down", "w_in", "w_gate", "w_proj_a", "w_proj_b", "w_out",
             "ffn2_w_gu", "ffn2_w_down", "w_ple_gate", "w_ple_proj")
WEIGHT_ORDER = ("ffn1_norm", "ffn1_w_gu", "ffn1_w_down", "mix_norm", "w_in", "a_q_norm", "a_k_norm", "a_rel_bias",
                "b_q_norm", "b_k_norm", "b_sinks", "w_gate", "w_proj_a", "w_proj_b", "w_out", "ffn2_norm",
                "ffn2_w_gu", "ffn2_w_down", "ple_norm", "w_ple_gate", "w_ple_proj")


TRANSPOSED = ("ffn1_w_gu", "ffn2_w_gu", "w_in")


def _local(a, nme):
    return a[0].T if nme in TRANSPOSED else a[0]


def _full_cols(wg):
    nb, k, n = wg.shape
    return jnp.transpose(wg, (1, 0, 2)).reshape(k, nb * n)


def _step(x, p, target, w, m, v):
    bl, s_len, d = x.shape
    t = bl * s_len
    h0 = x.reshape(t, d)
    pt = p.reshape(t, p.shape[-1])
    tgt = target.reshape(t, d)

    g_ffn1, g_mix, g_ffn2, g_ple = w["ffn1_norm"], w["mix_norm"], w["ffn2_norm"], w["ple_norm"]
    tiled = lambda a, width: jnp.tile(a.reshape(1, HEAD_DIM), (1, width // HEAD_DIM))
    gqa, gka = tiled(w["a_q_norm"], QW), tiled(w["a_k_norm"], _kv_width("A"))
    gqb, gkb = tiled(w["b_q_norm"], QW), tiled(w["b_k_norm"], _kv_width("B"))
    sinks = w["b_sinks"].reshape(B_Q_HEADS)
    bias_b = _alibi_bias()

    ffn1_names = ("ffn1_w_gu", "ffn1_w_down")
    shard = {nme: _local(w[nme], nme).astype(BF16) for nme in ffn1_names}
    send1, recv1, bufs, token = _gather_start([shard[nme] for nme in ffn1_names], h0, "gather_start_ffn1")
    zero = token[0, 0]
    shard.update({nme: (_local(w[nme], nme) + zero).astype(BF16) for nme in BIG_NAMES if nme not in ffn1_names})
    bias_a = _rel_bias_expand(w["a_rel_bias"][0] + zero, "rel_bias_expand")
    send2, recv2, bufs, token = _gather_pass(send1, recv1, bufs, bias_a, "gather_pass_ffn1")
    wgu1, wd1 = _gather_wait(send2, recv2, bufs, shard["ffn2_w_gu"], "gather_wait_ffn1")
    nf = wgu1.shape[1]
    wd1 = wd1.reshape(N_DEV // 2, nf, d)
    mixer_names = ("w_in", "w_gate")
    rest_names = ("w_proj_a", "w_proj_b", "w_out", "ffn2_w_gu", "ffn2_w_down", "w_ple_gate", "w_ple_proj")
    send1, recv1, bufs, token = _gather_start([shard[nme] for nme in mixer_names], wgu1, "gather_start_mixer")
    rsend1, rrecv1, rest_bufs, token = _gather_start([shard[nme] for nme in rest_names], token, "gather_start_rest")

    h1, gu1 = _ffn_fwd(h0, g_ffn1 + token[0, 0], wgu1, wd1, "ffn1_fwd")
    send2, recv2, bufs, token = _gather_pass(send1, recv1, bufs, h1, "gather_pass_mixer")
    win, wgate = _gather_wait(send2, recv2, bufs, token, "gather_wait_mixer")
    win, wgate = win.reshape(IN_COLS, d), _full_cols(wgate)
    un, qkv, gate = _proj_fwd(h1, g_mix, win, wgate, "proj_fwd")
    ya = _attn_fwd("A", qkv, gqa, gka, bias_a, sinks, bl, s_len, "attn_a_fwd")
    rsend2, rrecv2, rest_bufs, token = _gather_pass(rsend1, rrecv1, rest_bufs, ya, "gather_pass_rest")
    yb = _attn_fwd("B", qkv, gqb + token[0, 0], gkb, bias_b, sinks, bl, s_len, "attn_b_fwd")
    gathered = dict(zip(rest_names, _gather_wait(rsend2, rrecv2, rest_bufs, yb, "gather_wait_rest")))
    wgu2 = gathered["ffn2_w_gu"]
    wd2 = gathered["ffn2_w_down"].reshape(N_DEV // 2, nf, d)
    wpa = _full_cols(gathered["w_proj_a"])
    wpb = _full_cols(gathered["w_proj_b"])
    wpe = _full_cols(gathered["w_ple_proj"])
    wout = gathered["w_out"].reshape(d, d)
    wpg = gathered["w_ple_gate"].reshape(d, d)
    h2, merged, pa, pb = _merge_fwd(h1, ya, yb, gate, wpa, wpb, wout, "merge_fwd")
    h3, gu2 = _ffn_fwd(h2, g_ffn2, wgu2, wd2, "ffn2_fwd")
    dh3, dz4, dpp, n4, dg_ple, loss_part = _ple_loss(h3, g_ple, pt, tgt, wpg, wpe, "ple_loss")

    xi, yi, ci = _place()
    me = jnp.stack([4 * xi + 2 * yi + ci, 2 * xi + yi]).astype(jnp.int32)
    g32, g16, big, pairs = {}, {}, {}, {}

    def keep(nme, pair, rows=None):
        for store, g in zip((g32, g16), pair):
            store[nme] = g if rows is None else g.reshape(N_DEV, rows, d)

    def start(names, after, tag):
        send, recv, parts, lands, token = _scatter_start([g16[nme] for nme in names], after, "grads_start_" + tag)
        return names, send, recv, parts, lands, token

    def start_two_level(names, after, tag):
        views = [g16[nme].reshape((4, 2) + g16[nme].shape[1:]) for nme in names]
        for nme, got in zip(names, _pair_exchange(views, "grads_pair_" + tag)):
            pairs[nme] = got.reshape((4,) + got.shape[2:])
        sums = [_pair_sum(g32[nme], pairs[nme], me, "pair_sum_" + nme) for nme in names]
        send, recv, parts, lands, token = _scatter_start(sums, after, "grads_start_" + tag, SAME_CORE_CHIPS)
        return names, send, recv, parts, lands, token

    def finish(state, after, tag):
        names, send, recv, parts, lands, _ = state
        relations = SAME_CORE_CHIPS if names[0] in pairs else ALL_PEERS
        lands = _scatter_wait(send, recv, parts, lands, after, "grads_wait_" + tag, relations)
        return names, lands

    def adam(done, dep):
        for nme, land in zip(*done):
            outs = _final_adam(g32[nme], land, _local(w[nme], nme), _local(m[nme], nme), _local(v[nme], nme), me, dep,
                               "adam_" + nme, pairs.get(nme))
            big[nme] = [(o.T if nme in TRANSPOSED else o)[None] for o in outs]

    keep("w_ple_gate", _dw(n4, dz4, 1, d, "dw_ple_gate"), d // N_DEV)
    keep("w_ple_proj", _dw(pt, dpp, N_DEV, d // N_DEV, "dw_ple_proj"))
    early = [(start(("w_ple_gate", "w_ple_proj"), dh3, "ple"), "ple")]

    dh2, dgu2, a2, n3, dg_ffn2 = _ffn_bwd(dh3, h2, g_ffn2 + early[-1][0][-1][0, 0], gu2, wgu2, wd2, "ffn2_bwd")
    keep("ffn2_w_down", _dw(a2, dh3, N_DEV // 2, d, "dw_ffn2_down", 0.5), nf // 2)
    early.append((start(("ffn2_w_down",), dh2, "ffn2_down"), "ffn2_down"))
    keep("ffn2_w_gu", _dw(dgu2, n3, N_DEV, d, "dw_ffn2_gu", dep=early[-1][0][-1]))
    flight = start(("ffn2_w_gu",), dh2, "ffn2")

    dpa, dpb, dzg, dya, dyb = _merge_bwd(dh2, pa, pb, gate, wpa, wpb, wout, "merge_bwd")
    keep("w_out", _dw(merged, dh2, 1, d, "dw_out"), d // N_DEV)
    keep("w_proj_a", _dw(ya, dpa, N_DEV, d // N_DEV, "dw_proj_a"))
    keep("w_proj_b", _dw(yb, dpb, N_DEV, d // N_DEV, "dw_proj_b"))
    keep("w_gate", _dw(un, dzg, N_DEV, 2 * d // N_DEV, "dw_gate"))

    tok = flight[-1][0, 0]
    dqa, dka, dva, dgqa, dgka, dbias, _ = _attn_bwd("A", qkv, gqa + tok, gka, bias_a, sinks, ya, dya, bl, s_len,
                                                     "attn_a_bwd")
    dqb, dkb, dvb, dgqb, dgkb, _, dsink = _attn_bwd("B", qkv, gqb, gkb, bias_b, sinks, yb, dyb, bl, s_len, "attn_b_bwd")
    dqkv = [dqa, dka, dva, dqb, dkb, dvb]
    dtab = _rel_bias_grad(dbias, "rel_bias_grad")

    dh1, dg_mix = _proj_bwd(dh2, h1, g_mix, dzg, dqkv, win, wgate, "proj_bwd")
    keep("w_in", _dw_rows(dqkv, un, "dw_in"), IN_COLS // N_DEV)
    waiting = [finish(state, g32["w_in"], tag) for state, tag in early]
    done = finish(flight, waiting[-1][1][0], "ffn2")
    flight = start(("w_out", "w_proj_a", "w_proj_b", "w_gate", "w_in"), done[1][0], "mixer")
    waiting.append(done)

    dh0, dgu1, a1, n1, dg_ffn1 = _ffn_bwd(dh1, h0, g_ffn1 + flight[-1][0, 0], gu1, wgu1, wd1, "ffn1_bwd")
    keep("ffn1_w_down", _dw(a1, dh1, N_DEV // 2, d, "dw_ffn1_down", 0.5), nf // 2)
    done = finish(flight, g32["ffn1_w_down"], "mixer")
    flight = start(("ffn1_w_down",), done[1][0], "ffn1_down")
    waiting.append(done)

    keep("ffn1_w_gu", _dw(dgu1, n1, N_DEV, d, "dw_ffn1_gu", dep=flight[-1]))
    done = finish(flight, g32["ffn1_w_gu"], "ffn1_down")
    flight = start_two_level(("ffn1_w_gu",), done[1][0], "ffn1_gu")
    for group in waiting + [done]:
        adam(group, flight[-1])
    behind = 0.0 * big["ffn1_w_down"][0][0, 0, :1]
    smalls = (dg_ffn1, dg_mix, dg_ffn2, dg_ple + behind, dgqa, dgka, dgqb, dgkb, dtab, dsink)
    return dh0, loss_part, big, smalls, flight, finish, adam


def kernel(x, p, ffn1_norm, ffn1_w_gu, ffn1_w_down, mix_norm, w_in, a_q_norm, a_k_norm, a_rel_bias, b_q_norm, b_k_norm, b_sinks, w_gate, w_proj_a, w_proj_b, w_out, ffn2_norm, ffn2_w_gu, ffn2_w_down, ple_norm, w_ple_gate, w_ple_proj, loss_target, m_ffn1_norm, m_ffn1_w_gu, m_ffn1_w_down, m_mix_norm, m_w_in, m_a_q_norm, m_a_k_norm, m_a_rel_bias, m_b_q_norm, m_b_k_norm, m_b_sinks, m_w_gate, m_w_proj_a, m_w_proj_b, m_w_out, m_ffn2_norm, m_ffn2_w_gu, m_ffn2_w_down, m_ple_norm, m_w_ple_gate, m_w_ple_proj, v_ffn1_norm, v_ffn1_w_gu, v_ffn1_w_down, v_mix_norm, v_w_in, v_a_q_norm, v_a_k_norm, v_a_rel_bias, v_b_q_norm, v_b_k_norm, v_b_sinks, v_w_gate, v_w_proj_a, v_w_proj_b, v_w_out, v_ffn2_norm, v_ffn2_w_gu, v_ffn2_w_down, v_ple_norm, v_w_ple_gate, v_w_ple_proj):
    w = dict(ffn1_norm=ffn1_norm, ffn1_w_gu=ffn1_w_gu, ffn1_w_down=ffn1_w_down, mix_norm=mix_norm, w_in=w_in,
             a_q_norm=a_q_norm, a_k_norm=a_k_norm, a_rel_bias=a_rel_bias, b_q_norm=b_q_norm, b_k_norm=b_k_norm,
             b_sinks=b_sinks, w_gate=w_gate, w_proj_a=w_proj_a, w_proj_b=w_proj_b, w_out=w_out, ffn2_norm=ffn2_norm,
             ffn2_w_gu=ffn2_w_gu, ffn2_w_down=ffn2_w_down, ple_norm=ple_norm, w_ple_gate=w_ple_gate,
             w_ple_proj=w_ple_proj)
    m = dict(ffn1_norm=m_ffn1_norm, ffn1_w_gu=m_ffn1_w_gu, ffn1_w_down=m_ffn1_w_down, mix_norm=m_mix_norm,
             w_in=m_w_in, a_q_norm=m_a_q_norm, a_k_norm=m_a_k_norm, a_rel_bias=m_a_rel_bias, b_q_norm=m_b_q_norm,
             b_k_norm=m_b_k_norm, b_sinks=m_b_sinks, w_gate=m_w_gate, w_proj_a=m_w_proj_a, w_proj_b=m_w_proj_b,
             w_out=m_w_out, ffn2_norm=m_ffn2_norm, ffn2_w_gu=m_ffn2_w_gu, ffn2_w_down=m_ffn2_w_down,
             ple_norm=m_ple_norm, w_ple_gate=m_w_ple_gate, w_ple_proj=m_w_ple_proj)
    v = dict(ffn1_norm=v_ffn1_norm, ffn1_w_gu=v_ffn1_w_gu, ffn1_w_down=v_ffn1_w_down, mix_norm=v_mix_norm,
             w_in=v_w_in, a_q_norm=v_a_q_norm, a_k_norm=v_a_k_norm, a_rel_bias=v_a_rel_bias, b_q_norm=v_b_q_norm,
             b_k_norm=v_b_k_norm, b_sinks=v_b_sinks, w_gate=v_w_gate, w_proj_a=v_w_proj_a, w_proj_b=v_w_proj_b,
             w_out=v_w_out, ffn2_norm=v_ffn2_norm, ffn2_w_gu=v_ffn2_w_gu, ffn2_w_down=v_ffn2_w_down,
             ple_norm=v_ple_norm, w_ple_gate=v_w_ple_gate, w_ple_proj=v_w_ple_proj)
    bl, s_len, d = x.shape

    dh0, loss_part, big, smalls, flight, finish, adam = _step(x, p[0], loss_target, w, m, v)
    dg_ffn1, dg_mix, dg_ffn2, dg_ple, dgqa, dgka, dgqb, dgkb, dtab, dsink = smalls

    fold = lambda a: a[:, :, 0, :].reshape(-1, HEAD_DIM).sum(axis=0)
    small_part = dict(
        ffn1_norm=dg_ffn1, mix_norm=dg_mix, ffn2_norm=dg_ffn2, ple_norm=dg_ple,
        a_q_norm=fold(dgqa), a_k_norm=fold(dgka), b_q_norm=fold(dgqb), b_k_norm=fold(dgkb),
        a_rel_bias=dtab,
        b_sinks=dsink.sum(axis=0)[:, 0, :GROUP].reshape(B_Q_HEADS),
        loss=loss_part[0, :1])
    zero1 = jnp.zeros((1,), F32)
    shapes = {nme: w[nme].shape for nme in SMALL_NAMES if nme != "loss"}
    shapes["loss"] = ()
    pk = lambda src: _pack_small({**{nme: src[nme] for nme in SMALL_NAMES if nme != "loss"}, "loss": zero1})
    sg, sd, sm, sv = _small_allreduce_adam(_pack_small(small_part), pk(w), pk(m), pk(v), "small_allreduce_adam")
    adam(finish(flight, sg, "ffn1_gu"), sg)
    sg, sd, sm, sv = (_unpack_small(a, shapes) for a in (sg, sd, sm, sv))

    def pick(i):
        out = []
        for nme in WEIGHT_ORDER:
            out.append(big[nme][i] if nme in big else (sg, sd, sm, sv)[i][nme])
        return out

    return (sg["loss"], dh0.reshape(bl, s_len, d), *pick(0), *pick(1), *pick(2), *pick(3))
```

```python
import jax
import jax.numpy as jnp
import numpy as np
from jax import lax
from jax.experimental import pallas as pl
from jax.experimental.pallas import tpu as pltpu

F32 = jnp.float32
BF16 = jnp.bfloat16

CHUNK = 64
HEAD_DIM = 64
A_HEADS = 8
A_PREV = 8
A_MAX_REL = 128
B_Q_HEADS = 8
B_KV_HEADS = 2
B_PREV = 2
A_WIDTH = A_HEADS * HEAD_DIM
B_Q_WIDTH = B_Q_HEADS * HEAD_DIM
B_KV_WIDTH = B_KV_HEADS * HEAD_DIM
IN_COLS = 3 * A_WIDTH + B_Q_WIDTH + 2 * B_KV_WIDTH
EPS = 1e-6
NEG_INF = -1e30
ADAM_LR = 0.001
ADAM_B1 = 0.9
ADAM_B2 = 0.999
ADAM_EPS = 1e-08
ADAM_WD = 0.01
ADAM_STEP = 10

N_DEV = 8
LANES = 128
QTILE = 2 * CHUNK
VMEM_LIMIT = 56 * 1024 * 1024
ADAM_TILE_ELEMS = 256 * 1024

MESH_ID = pl.DeviceIdType.MESH
ANY = pl.BlockSpec(memory_space=pl.ANY)
HBM = pl.BlockSpec(memory_space=pltpu.HBM)
SEM = pl.BlockSpec(memory_space=pltpu.SEMAPHORE)
SIDE_EFFECT = pltpu.SideEffectType.DATAFLOW_SIDE_EFFECTING


def _dot(a, b):
    return jnp.dot(a, b, preferred_element_type=F32)


def _dot_nt(a, b):
    return lax.dot_general(a, b, (((1,), (1,)), ((), ())), preferred_element_type=F32)


def _dot_tn(a, b):
    return lax.dot_general(a, b, (((0,), (0,)), ((), ())), preferred_element_type=F32)


def _params(sem=None, vmem=VMEM_LIMIT):
    return pltpu.CompilerParams(dimension_semantics=sem, vmem_limit_bytes=vmem)


def _row_tile(t, want):
    while t % want:
        want //= 2
    return want


def _place():
    return lax.axis_index("x"), lax.axis_index("y"), lax.axis_index("c")


def _gather_level(bufs, send_sems, recv_sems, level, shards=None):
    x, y, c = _place()
    me, sib = (x, y, c), (x, y, 1 - c)
    chips = [(1 - x, y), (x, 1 - y), (1 - x, 1 - y)]

    def copy(w, k, block, to):
        px, py, pc = block
        rows = bufs[w].at[4 * px + 2 * py + pc]
        src = shards[w] if shards is not None and block is me else rows
        return pltpu.make_async_remote_copy(src_ref=src, dst_ref=rows, send_sem=send_sems.at[k], recv_sem=recv_sems.at[k],
                                            device_id=to, device_id_type=MESH_ID)

    n = len(bufs)
    own = []
    if level == 1:
        own = [pltpu.make_async_copy(bufs[w].at[4 * x + 2 * y + c] if shards is None else shards[w],
                                     bufs[w].at[4 * x + 2 * y + c], send_sems.at[4 * n + w]) for w in range(n)]
    out, arriving = [], []
    for w in range(len(bufs)):
        if level == 1:
            out.append(copy(w, 4 * w, me, sib))
            arriving.append(copy(w, 4 * w, sib, me))
        for j, chip in enumerate(chips):
            if level == 1:
                out.append(copy(w, 4 * w + 1 + j, me, (*chip, c)))
                arriving.append(copy(w, 4 * w + 1 + j, (*chip, c), me))
            else:
                out.append(copy(w, 3 * w + j, (*chip, c), sib))
                arriving.append(copy(w, 3 * w + j, (*chip, 1 - c), me))
    return out, arriving, own


def _split_call(body, name, bufs, sems_in, after, n_sems_out, token, extra=()):
    n = len(bufs)
    out_shape = [pltpu.SemaphoreType.DMA((n_sems_out,))] * (2 if n_sems_out else 0)
    out_shape += [pltpu.HBM(a.shape, a.dtype) for a in bufs]
    out_specs = [SEM] * (2 if n_sems_out else 0) + [HBM] * n
    if token:
        out_shape.append(jax.ShapeDtypeStruct((8, LANES), F32))
        out_specs.append(pl.BlockSpec(memory_space=pltpu.VMEM))
    first = 2 if n_sems_out else 0
    return pl.pallas_call(
        body, name=name, out_shape=tuple(out_shape),
        in_specs=[HBM] * (n + len(extra)) + [SEM] * len(sems_in) + [ANY], out_specs=tuple(out_specs),
        input_output_aliases={i: first + i for i in range(n)},
        compiler_params=pltpu.CompilerParams(has_side_effects=SIDE_EFFECT),
    )(*bufs, *extra, *sems_in, after)


def _gather_start(shards, after, name):
    n = len(shards)
    hbm = lambda a: pltpu.with_memory_space_constraint(a, pltpu.HBM)
    bufs = [hbm(lax.empty((N_DEV,) + s.shape, s.dtype)) for s in shards]

    def body(*refs):
        out, _, own = _gather_level(refs[:n], refs[2 * n + 1], refs[2 * n + 2], 1, shards=refs[n:2 * n])
        for cp in own + out:
            cp.start()
        refs[-1][...] = jnp.zeros_like(refs[-1])

    outs = _split_call(body, name, bufs + [hbm(s) for s in shards], [], after, 5 * n, True)
    return outs[0], outs[1], list(outs[2:2 + 2 * n]), outs[-1]


def _gather_pass(send1, recv1, bufs_and_shards, after, name):
    n = len(bufs_and_shards) // 2
    bufs = bufs_and_shards

    def body(*refs):
        refs = refs[:n] + refs[2 * n:]
        out1, in1, own = _gather_level(refs[:n], refs[n], refs[n + 1], 1)
        out2, _, _ = _gather_level(refs[:n], refs[n + 3], refs[n + 4], 2)
        for cp in in1:
            cp.wait_recv()
        for cp in out2:
            cp.start()
        for cp in out1:
            cp.wait_send()
        for cp in own:
            cp.wait()
        refs[-1][...] = jnp.zeros_like(refs[-1])

    outs = _split_call(body, name, bufs, [send1, recv1], after, 3 * n, True)
    return outs[0], outs[1], list(outs[2:2 + n]), outs[-1]


def _gather_wait(send2, recv2, bufs, after, name):
    n = len(bufs)

    def body(*refs):
        out2, in2, _ = _gather_level(refs[:n], refs[n], refs[n + 1], 2)
        for cp in in2:
            cp.wait_recv()
        for cp in out2:
            cp.wait_send()

    return list(_split_call(body, name, bufs, [send2, recv2], after, 0, False))


ALL_PEERS = tuple(range(1, N_DEV))
SAME_CORE_CHIPS = (2, 4, 6)


def _scatter_copies(parts, lands, send_sems, recv_sems, relations):
    x, y, c = _place()
    ns = len(relations)
    cps = []
    for w, (part, land) in enumerate(zip(parts, lands)):
        for i, k in enumerate(relations):
            px, py, pc = x ^ ((k >> 2) & 1), y ^ ((k >> 1) & 1), c ^ (k & 1)
            block = 4 * px + 2 * py + pc if part.shape[0] == N_DEV else 2 * px + py
            cps.append(pltpu.make_async_remote_copy(
                src_ref=part.at[block], dst_ref=land.at[i],
                send_sem=send_sems.at[ns * w + i], recv_sem=recv_sems.at[ns * w + i],
                device_id=(px, py, pc), device_id_type=MESH_ID))
    return cps


def _scatter_start(parts, after, name, relations=ALL_PEERS):
    n = len(parts)
    ns = len(relations)

    def body(*refs):
        ins, lands = refs[:n], refs[n:2 * n]
        send_sems, recv_sems = refs[2 * n + 1], refs[2 * n + 2]
        token = refs[-1]
        for cp in _scatter_copies(ins, lands, send_sems, recv_sems, relations):
            cp.start()
        token[...] = jnp.zeros_like(token)

    land_shapes = [(ns,) + p.shape[1:] for p in parts]
    in_hbm = [pltpu.with_memory_space_constraint(p, pltpu.HBM) for p in parts]
    in_hbm += [pltpu.with_memory_space_constraint(lax.empty(s, p.dtype), pltpu.HBM) for s, p in zip(land_shapes, parts)]
    outs = pl.pallas_call(
        body, name=name,
        out_shape=(pltpu.SemaphoreType.DMA((ns * n,)), pltpu.SemaphoreType.DMA((ns * n,)),
                   *[pltpu.HBM(p.shape, p.dtype) for p in parts],
                   *[pltpu.HBM(s, p.dtype) for s, p in zip(land_shapes, parts)],
                   jax.ShapeDtypeStruct((8, LANES), F32)),
        in_specs=[HBM] * (2 * n) + [ANY],
        out_specs=(SEM, SEM, *[HBM] * (2 * n), pl.BlockSpec(memory_space=pltpu.VMEM)),
        input_output_aliases={i: 2 + i for i in range(2 * n)},
        compiler_params=pltpu.CompilerParams(has_side_effects=SIDE_EFFECT),
    )(*in_hbm, after)
    return outs[0], outs[1], list(outs[2:2 + n]), list(outs[2 + n:2 + 2 * n]), outs[-1]


def _scatter_wait(send_sems, recv_sems, parts, lands, after, name, relations=ALL_PEERS):
    n = len(parts)

    def body(*refs):
        ins, lnd = refs[:n], refs[n:2 * n]
        for cp in _scatter_copies(ins, lnd, refs[2 * n], refs[2 * n + 1], relations):
            cp.wait_send()
            cp.wait_recv()

    outs = pl.pallas_call(
        body, name=name,
        out_shape=tuple(pltpu.HBM(a.shape, a.dtype) for a in parts + lands),
        in_specs=[HBM] * (2 * n) + [SEM, SEM, ANY],
        out_specs=tuple([HBM] * (2 * n)),
        input_output_aliases={i: i for i in range(2 * n)},
        compiler_params=pltpu.CompilerParams(has_side_effects=SIDE_EFFECT),
    )(*parts, *lands, send_sems, recv_sems, after)
    return list(outs[n:])


def _pair_exchange(parts, name):
    n = len(parts)

    def body(*refs):
        ins, outs = refs[:n], refs[n:2 * n]
        send_sems, recv_sems = refs[2 * n:]
        x, y, c = _place()
        cps = [pltpu.make_async_remote_copy(
            src_ref=ins[w].at[:, pl.ds(1 - c, 1)], dst_ref=outs[w], send_sem=send_sems.at[w], recv_sem=recv_sems.at[w],
            device_id=(x, y, 1 - c), device_id_type=MESH_ID) for w in range(n)]
        for cp in cps:
            cp.start()
        for cp in cps:
            cp.wait()

    return pl.pallas_call(
        body, name=name,
        out_shape=[jax.ShapeDtypeStruct((4, 1) + p.shape[2:], p.dtype) for p in parts],
        in_specs=[ANY] * n, out_specs=[ANY] * n,
        scratch_shapes=[pltpu.SemaphoreType.DMA((n,)), pltpu.SemaphoreType.DMA((n,))],
    )(*parts)


def _pair_sum(g8, r1, me, name):
    _, r, c = g8.shape
    tr = max(q for q in range(16, r + 1, 16) if r % q == 0 and q * c <= ADAM_TILE_ELEMS)

    def body(me_ref, g_ref, r_ref, o_ref):
        o_ref[...] = (g_ref[...] + r_ref[...].astype(F32)).astype(BF16)

    chip = lambda k, s: s[1] ^ (k + 1)
    return pl.pallas_call(
        body, name=name,
        out_shape=jax.ShapeDtypeStruct((4, r, c), BF16),
        grid_spec=pltpu.PrefetchScalarGridSpec(
            num_scalar_prefetch=1, grid=(3, r // tr),
            in_specs=[pl.BlockSpec((None, None, tr, c), lambda k, i, s: (chip(k, s), s[0] % 2, i, 0)),
                      pl.BlockSpec((None, tr, c), lambda k, i, s: (chip(k, s), i, 0))],
            out_specs=pl.BlockSpec((None, tr, c), lambda k, i, s: (chip(k, s), i, 0))),
        compiler_params=_params(("arbitrary", "arbitrary")),
    )(me, g8.reshape((4, 2) + g8.shape[1:]), r1)


def _adam(w, g, m, v):
    m2 = ADAM_B1 * m + (1.0 - ADAM_B1) * g
    v2 = ADAM_B2 * v + (1.0 - ADAM_B2) * (g * g)
    m_hat = m2 / (1.0 - ADAM_B1 ** ADAM_STEP)
    v_hat = v2 / (1.0 - ADAM_B2 ** ADAM_STEP)
    delta = -ADAM_LR * (m_hat / (jnp.sqrt(v_hat) + ADAM_EPS) + ADAM_WD * w)
    return delta, m2, v2


def _small_allreduce_adam(part, w, m, v, name):
    rows = part.shape[0]

    def body(p_ref, w_ref, m_ref, v_ref, g_ref, d_ref, mo_ref, vo_ref, buf, send_sems, recv_sems):
        x, y, c = _place()
        buf[0] = p_ref[...]
        cps = []
        for k in range(1, N_DEV):
            kx, ky, kc = (k >> 2) & 1, (k >> 1) & 1, k & 1
            peer = (x ^ kx, y ^ ky, c ^ kc)
            cps.append(pltpu.make_async_remote_copy(
                src_ref=p_ref, dst_ref=buf.at[k], send_sem=send_sems.at[k - 1], recv_sem=recv_sems.at[k - 1],
                device_id=peer, device_id_type=MESH_ID))
        for cp in cps:
            cp.start()
        for cp in cps:
            cp.wait()
        me = 4 * x + 2 * y + c
        total = buf[me]
        for d in range(1, N_DEV):
            total = total + buf[d ^ me]
        g_ref[...] = total
        delta, m2, v2 = _adam(w_ref[...], total, m_ref[...], v_ref[...])
        d_ref[...] = delta
        mo_ref[...] = m2
        vo_ref[...] = v2

    vm = pl.BlockSpec(memory_space=pltpu.VMEM)
    return pl.pallas_call(
        body, name=name,
        out_shape=[jax.ShapeDtypeStruct(part.shape, F32)] * 4,
        in_specs=[vm] * 4, out_specs=[vm] * 4,
        scratch_shapes=[pltpu.VMEM((N_DEV, rows, LANES), F32),
                        pltpu.SemaphoreType.DMA((N_DEV - 1,)), pltpu.SemaphoreType.DMA((N_DEV - 1,))],
    )(part, w, m, v)


def _final_adam(g8, land, w, m, v, me, dep, name, pair=None):
    _, r, c = g8.shape
    tr = max(q for q in range(16, r + 1, 16) if r % q == 0 and q * c <= ADAM_TILE_ELEMS)
    nland = land.shape[0]

    def body(me_ref, g_ref, land_ref, *rest):
        pair_ref = rest[0] if pair is not None else None
        w_ref, m_ref, v_ref, _, go_ref, d_ref, mo_ref, vo_ref = rest[-8:]
        g = g_ref[...]
        if pair_ref is not None:
            g = g + pair_ref[...].astype(F32)
        for k in range(nland):
            g = g + land_ref[k].astype(F32)
        go_ref[...] = g
        delta, m2, v2 = _adam(w_ref[...], g, m_ref[...], v_ref[...])
        d_ref[...] = delta
        mo_ref[...] = m2
        vo_ref[...] = v2

    plain = pl.BlockSpec((tr, c), lambda i, s: (i, 0))
    return pl.pallas_call(
        body, name=name,
        out_shape=[jax.ShapeDtypeStruct((r, c), F32)] * 4,
        grid_spec=pltpu.PrefetchScalarGridSpec(
            num_scalar_prefetch=1, grid=(r // tr,),
            in_specs=[pl.BlockSpec((None, tr, c), lambda i, s: (s[0], i, 0)),
                      pl.BlockSpec((nland, tr, c), lambda i, s: (0, i, 0))]
            + ([] if pair is None else [pl.BlockSpec((None, tr, c), lambda i, s: (s[1], i, 0))])
            + [plain, plain, plain, ANY],
            out_specs=[plain] * 4),
        compiler_params=_params(("arbitrary",)),
    )(*((me, g8, land) + (() if pair is None else (pair,)) + (w, m, v, dep)))


def _rms(x, gain):
    r = lax.rsqrt(jnp.mean(x * x, axis=-1, keepdims=True) + EPS)
    xh = x * r
    return xh * gain, xh, r


def _rms_bwd(xh, r, gain, dy):
    gdy = gain * dy
    dx = r * (gdy - xh * jnp.mean(xh * gdy, axis=-1, keepdims=True))
    return dx, jnp.sum(dy * xh, axis=0, keepdims=True)


def _load_weights(pairs, sems):
    cps = [pltpu.make_async_copy(src, dst, sems.at[i]) for i, (src, dst) in enumerate(pairs)]
    for cp in cps:
        cp.start()
    for cp in cps:
        cp.wait()


def _ffn_fwd(h, gain, wgu, wd, name):
    t, d = h.shape
    nb, nf, _ = wgu.shape
    nh = nb // 2
    tm = _row_tile(t, 512)

    def body(h_ref, g_ref, wgu_hbm, wd_hbm, out_ref, gu_ref, wgu_v, wd_v, sems):
        @pl.when(pl.program_id(0) == 0)
        def _():
            _load_weights([(wgu_hbm, wgu_v), (wd_hbm, wd_v)], sems)

        x = h_ref[...]
        n, _, _ = _rms(x, g_ref[...])
        nbf = n.astype(BF16)
        acc = jnp.zeros((tm, d), F32)
        for j in range(nh):
            g = _dot_nt(nbf, wgu_v[j])
            u = _dot_nt(nbf, wgu_v[j + nh])
            gu_ref[j] = g.astype(BF16)
            gu_ref[j + nh] = u.astype(BF16)
            a = (g * jax.nn.sigmoid(g)) * u
            acc = acc + _dot(a.astype(BF16), wd_v[j])
        out_ref[...] = x + 0.5 * acc

    return pl.pallas_call(
        body, name=name, grid=(t // tm,),
        out_shape=[jax.ShapeDtypeStruct((t, d), F32), jax.ShapeDtypeStruct((nb, t, nf), BF16)],
        in_specs=[pl.BlockSpec((tm, d), lambda i: (i, 0)), pl.BlockSpec((1, d), lambda i: (0, 0)), ANY, ANY],
        out_specs=[pl.BlockSpec((tm, d), lambda i: (i, 0)), pl.BlockSpec((nb, tm, nf), lambda i: (0, i, 0))],
        scratch_shapes=[pltpu.VMEM(wgu.shape, BF16), pltpu.VMEM(wd.shape, BF16), pltpu.SemaphoreType.DMA((2,))],
        compiler_params=_params(("arbitrary",)),
    )(h, gain, wgu, wd)


def _ffn_bwd(dh, h, gain, gu, wgu, wd, name):
    t, d = h.shape
    nb, nf, _ = wgu.shape
    nh = nb // 2
    tm = _row_tile(t, 256)

    def body(dh_ref, h_ref, g_ref, gu_ref, wgu_hbm, wd_hbm, dhp_ref, dgu_ref, a_ref, n_ref, dgain_ref,
             wgu_v, wd_v, sems):
        @pl.when(pl.program_id(0) == 0)
        def _():
            _load_weights([(wgu_hbm, wgu_v), (wd_hbm, wd_v)], sems)
            dgain_ref[...] = jnp.zeros_like(dgain_ref)

        x = h_ref[...]
        gain_v = g_ref[...]
        n, xh, r = _rms(x, gain_v)
        n_ref[...] = n.astype(BF16)
        dh_v = dh_ref[...]
        dfb = (0.5 * dh_v).astype(BF16)
        dn = jnp.zeros((tm, d), F32)
        for j in range(nh):
            da = _dot_nt(dfb, wd_v[j])
            g = gu_ref[j].astype(F32)
            u = gu_ref[j + nh].astype(F32)
            sg = jax.nn.sigmoid(g)
            si = g * sg
            dg = (da * u * (sg * (1.0 + g * (1.0 - sg)))).astype(BF16)
            du = (da * si).astype(BF16)
            a_ref[j] = (si * u).astype(BF16)
            dgu_ref[j] = dg
            dgu_ref[j + nh] = du
            dn = dn + _dot(dg, wgu_v[j]) + _dot(du, wgu_v[j + nh])
        dx, dgain = _rms_bwd(xh, r, gain_v, dn)
        dhp_ref[...] = dh_v + dx
        dgain_ref[...] += dgain

    row = pl.BlockSpec((tm, d), lambda i: (i, 0))
    vec = pl.BlockSpec((1, d), lambda i: (0, 0))
    return pl.pallas_call(
        body, name=name, grid=(t // tm,),
        out_shape=[jax.ShapeDtypeStruct((t, d), F32), jax.ShapeDtypeStruct((nb, t, nf), BF16),
                   jax.ShapeDtypeStruct((nh, t, nf), BF16), jax.ShapeDtypeStruct((t, d), BF16),
                   jax.ShapeDtypeStruct((1, d), F32)],
        in_specs=[row, row, vec, pl.BlockSpec((nb, tm, nf), lambda i: (0, i, 0)), ANY, ANY],
        out_specs=[row, pl.BlockSpec((nb, tm, nf), lambda i: (0, i, 0)),
                   pl.BlockSpec((nh, tm, nf), lambda i: (0, i, 0)), row, vec],
        scratch_shapes=[pltpu.VMEM(wgu.shape, BF16), pltpu.VMEM(wd.shape, BF16), pltpu.SemaphoreType.DMA((2,))],
        compiler_params=_params(("arbitrary",)),
    )(dh, h, gain, gu, wgu, wd)


def _dw(xa, dy, nb, n, name, scale=1.0, dep=None):
    t, k = xa.shape[-2:]
    tt = _row_tile(t, 512 if xa.ndim == 3 and nb == N_DEV else 1024)
    steps = t // tt
    wide = dy.ndim == 2 and xa.ndim == 2
    if xa.ndim == 3:
        x_spec = pl.BlockSpec((nb, tt, k), lambda i: (0, i, 0))
    else:
        x_spec = pl.BlockSpec((tt, k), lambda i: (i, 0))
    if dy.ndim == 3:
        dy_spec = pl.BlockSpec((nb, tt, n), lambda i: (0, i, 0))
    else:
        dy_spec = pl.BlockSpec((tt, dy.shape[1]), lambda i: (i, 0))
    acc_shape = (k, nb * n) if wide else (nb, k, n)
    stage_shape = (k, nb * n) if wide else (k, n)

    def body(x_ref, dy_ref, *rest):
        o_hbm, ob_hbm, acc, stage, sems = rest[-5:]

        @pl.when(pl.program_id(0) == 0)
        def _():
            acc[...] = jnp.zeros_like(acc)

        if wide:
            acc[...] += _dot(x_ref[...].astype(BF16).T, dy_ref[...].astype(BF16))
        elif xa.ndim == 2:
            xt = x_ref[...].astype(BF16).T
            for j in range(nb):
                acc[j] += _dot(xt, dy_ref[j].astype(BF16))
        else:
            dyb = dy_ref[...].astype(BF16)
            for j in range(nb):
                acc[j] += _dot_tn(x_ref[j].astype(BF16), dyb)

        @pl.when(pl.program_id(0) == steps - 1)
        def _():
            if scale != 1.0:
                acc[...] = acc[...] * scale
            if wide:
                cps = [pltpu.make_async_copy(acc.at[:, pl.ds(j * n, n)] if nb > 1 else acc, o_hbm.at[j], sems.at[j])
                       for j in range(nb)]
            else:
                cps = [pltpu.make_async_copy(acc, o_hbm, sems.at[0])]
            for cp in cps:
                cp.start()
            if wide:
                stage[...] = acc[...].astype(BF16)
                bcs = [pltpu.make_async_copy(stage.at[:, pl.ds(j * n, n)] if nb > 1 else stage, ob_hbm.at[j],
                                             sems.at[nb + j]) for j in range(nb)]
                for cp in bcs:
                    cp.start()
                for cp in bcs:
                    cp.wait()
            else:
                for j in range(nb):
                    stage[...] = acc[j].astype(BF16)
                    cp = pltpu.make_async_copy(stage, ob_hbm.at[j], sems.at[nb])
                    cp.start()
                    cp.wait()
            for cp in cps:
                cp.wait()

    return pl.pallas_call(
        body, name=name, grid=(steps,),
        out_shape=[jax.ShapeDtypeStruct((nb, k, n), F32), jax.ShapeDtypeStruct((nb, k, n), BF16)],
        in_specs=[x_spec, dy_spec] + ([] if dep is None else [ANY]),
        out_specs=[ANY, ANY],
        scratch_shapes=[pltpu.VMEM(acc_shape, F32), pltpu.VMEM(stage_shape, BF16),
                        pltpu.SemaphoreType.DMA((2 * nb,))],
        compiler_params=_params(("arbitrary",)),
    )(*((xa, dy) if dep is None else (xa, dy, dep)))


def _proj_fwd(h, gain, win, wgate, name):
    t, d = h.shape
    tm = _row_tile(t, 512)
    nq, ng = win.shape[0], wgate.shape[1]

    def body(h_ref, g_ref, win_ref, wg_ref, un_ref, qkv_ref, gate_ref):
        n, _, _ = _rms(h_ref[...], g_ref[...])
        nbf = n.astype(BF16)
        un_ref[...] = nbf
        qkv_ref[...] = _dot_nt(nbf, win_ref[...])
        gate_ref[...] = jax.nn.sigmoid(_dot(nbf, wg_ref[...]))

    full = lambda a: pl.BlockSpec(a.shape, lambda i: (0,) * a.ndim)
    return pl.pallas_call(
        body, name=name, grid=(t // tm,),
        out_shape=[jax.ShapeDtypeStruct((t, d), BF16), jax.ShapeDtypeStruct((t, nq), F32),
                   jax.ShapeDtypeStruct((t, ng), F32)],
        in_specs=[pl.BlockSpec((tm, d), lambda i: (i, 0)), full(gain), full(win), full(wgate)],
        out_specs=[pl.BlockSpec((tm, d), lambda i: (i, 0)), pl.BlockSpec((tm, nq), lambda i: (i, 0)),
                   pl.BlockSpec((tm, ng), lambda i: (i, 0))],
        compiler_params=_params(("arbitrary",)),
    )(h, gain, win, wgate)


def _proj_bwd(dh, h, gain, dzg, dqkv_parts, win, wgate, name):
    t, d = h.shape
    tm = _row_tile(t, 512)
    ng = wgate.shape[1]
    np_ = len(dqkv_parts)
    widths = [a.shape[1] for a in dqkv_parts]

    def body(dh_ref, h_ref, g_ref, dzg_ref, *rest):
        part_refs, (win_ref, wg_ref, dhp_ref, dgain_ref) = rest[:np_], rest[np_:]

        @pl.when(pl.program_id(0) == 0)
        def _():
            dgain_ref[...] = jnp.zeros_like(dgain_ref)

        gain_v = g_ref[...]
        _, xh, r = _rms(h_ref[...], gain_v)
        dun = _dot_nt(dzg_ref[...], wg_ref[...])
        off = 0
        for ref, wd in zip(part_refs, widths):
            dun = dun + _dot(ref[...].astype(BF16), win_ref[off:off + wd, :])
            off += wd
        dx, dgain = _rms_bwd(xh, r, gain_v, dun)
        dhp_ref[...] = dh_ref[...] + dx
        dgain_ref[...] += dgain

    full = lambda a: pl.BlockSpec(a.shape, lambda i: (0,) * a.ndim)
    row = pl.BlockSpec((tm, d), lambda i: (i, 0))
    return pl.pallas_call(
        body, name=name, grid=(t // tm,),
        out_shape=[jax.ShapeDtypeStruct((t, d), F32), jax.ShapeDtypeStruct((1, d), F32)],
        in_specs=[row, row, full(gain), pl.BlockSpec((tm, ng), lambda i: (i, 0))]
        + [pl.BlockSpec((tm, wd), lambda i: (i, 0)) for wd in widths] + [full(win), full(wgate)],
        out_specs=[row, pl.BlockSpec((1, d), lambda i: (0, 0))],
        compiler_params=_params(("arbitrary",)),
    )(dh, h, gain, dzg, *dqkv_parts, win, wgate)


def _dw_rows(parts, dy, name):
    t, n = dy.shape
    widths = [a.shape[1] for a in parts]
    k = sum(widths)
    tt = _row_tile(t, 1024)
    steps = t // tt
    np_ = len(parts)

    def body(*refs):
        part_refs, dy_ref = refs[:np_], refs[np_]
        o_hbm, ob_hbm, acc, stage, sems = refs[np_ + 1:]

        @pl.when(pl.program_id(0) == 0)
        def _():
            acc[...] = jnp.zeros_like(acc)

        dyb = dy_ref[...].astype(BF16)
        off = 0
        for ref, wd in zip(part_refs, widths):
            acc[off:off + wd, :] += _dot(ref[...].astype(BF16).T, dyb)
            off += wd

        @pl.when(pl.program_id(0) == steps - 1)
        def _():
            stage[...] = acc[...].astype(BF16)
            cps = [pltpu.make_async_copy(acc, o_hbm.at[0], sems.at[0]),
                   pltpu.make_async_copy(stage, ob_hbm.at[0], sems.at[1])]
            for cp in cps:
                cp.start()
            for cp in cps:
                cp.wait()

    return pl.pallas_call(
        body, name=name, grid=(steps,),
        out_shape=[jax.ShapeDtypeStruct((1, k, n), F32), jax.ShapeDtypeStruct((1, k, n), BF16)],
        in_specs=[pl.BlockSpec((tt, wd), lambda i: (i, 0)) for wd in widths] + [pl.BlockSpec((tt, n), lambda i: (i, 0))],
        out_specs=[ANY, ANY],
        scratch_shapes=[pltpu.VMEM((k, n), F32), pltpu.VMEM((k, n), BF16), pltpu.SemaphoreType.DMA((2,))],
        compiler_params=_params(("arbitrary",)),
    )(*parts, dy)


def _merge_fwd(h, ya, yb, gate, wpa, wpb, wout, name):
    t, d = h.shape
    tm = _row_tile(t, 512)

    def body(h_ref, ya_ref, yb_ref, ga_ref, gb_ref, wpa_ref, wpb_ref, wout_ref, out_ref, mg_ref, pa_ref, pb_ref):
        pa = _dot(ya_ref[...].astype(BF16), wpa_ref[...])
        pb = _dot(yb_ref[...].astype(BF16), wpb_ref[...])
        merged = (ga_ref[...] * pa + gb_ref[...] * pb).astype(BF16)
        pa_ref[...] = pa.astype(BF16)
        pb_ref[...] = pb.astype(BF16)
        mg_ref[...] = merged
        out_ref[...] = h_ref[...] + _dot(merged, wout_ref[...])

    full = lambda a: pl.BlockSpec(a.shape, lambda i: (0,) * a.ndim)
    row = pl.BlockSpec((tm, d), lambda i: (i, 0))
    yrow = pl.BlockSpec((tm, ya.shape[1]), lambda i: (i, 0))
    return pl.pallas_call(
        body, name=name, grid=(t // tm,),
        out_shape=[jax.ShapeDtypeStruct((t, d), F32)] + [jax.ShapeDtypeStruct((t, d), BF16)] * 3,
        in_specs=[row, yrow, yrow, pl.BlockSpec((tm, d), lambda i: (i, 0)), pl.BlockSpec((tm, d), lambda i: (i, 1)),
                  full(wpa), full(wpb), full(wout)],
        out_specs=[row] * 4,
        compiler_params=_params(("arbitrary",)),
    )(h, ya, yb, gate, gate, wpa, wpb, wout)


def _merge_bwd(dh, pa, pb, gate, wpa, wpb, wout, name):
    t, d = dh.shape
    tm = _row_tile(t, 512)
    wy = wpa.shape[0]

    def body(dh_ref, pa_ref, pb_ref, ga_ref, gb_ref, wpa_ref, wpb_ref, wout_ref,
             dpa_ref, dpb_ref, dzg_ref, dya_ref, dyb_ref):
        dm = _dot_nt(dh_ref[...].astype(BF16), wout_ref[...])
        ga, gb = ga_ref[...], gb_ref[...]
        dpa = (dm * ga).astype(BF16)
        dpb = (dm * gb).astype(BF16)
        dpa_ref[...] = dpa
        dpb_ref[...] = dpb
        dzg_ref[:, :d] = (dm * pa_ref[...].astype(F32) * ga * (1.0 - ga)).astype(BF16)
        dzg_ref[:, d:] = (dm * pb_ref[...].astype(F32) * gb * (1.0 - gb)).astype(BF16)
        dya_ref[...] = _dot_nt(dpa, wpa_ref[...])
        dyb_ref[...] = _dot_nt(dpb, wpb_ref[...])

    full = lambda a: pl.BlockSpec(a.shape, lambda i: (0,) * a.ndim)
    row = pl.BlockSpec((tm, d), lambda i: (i, 0))
    yrow = pl.BlockSpec((tm, wy), lambda i: (i, 0))
    return pl.pallas_call(
        body, name=name, grid=(t // tm,),
        out_shape=[jax.ShapeDtypeStruct((t, d), BF16), jax.ShapeDtypeStruct((t, d), BF16),
                   jax.ShapeDtypeStruct((t, 2 * d), BF16), jax.ShapeDtypeStruct((t, wy), F32),
                   jax.ShapeDtypeStruct((t, wy), F32)],
        in_specs=[row, row, row, pl.BlockSpec((tm, d), lambda i: (i, 0)), pl.BlockSpec((tm, d), lambda i: (i, 1)),
                  full(wpa), full(wpb), full(wout)],
        out_specs=[row, row, pl.BlockSpec((tm, 2 * d), lambda i: (i, 0)), yrow, yrow],
        compiler_params=_params(("arbitrary",)),
    )(dh, pa, pb, gate, gate, wpa, wpb, wout)


def _ple_loss(h, gain, p, target, wpg, wpe, name):
    t, d = h.shape
    tm = _row_tile(t, 512)
    pd = p.shape[1]

    def body(h_ref, g_ref, p_ref, t_ref, wpg_ref, wpe_ref, dh_ref, dz_ref, dpp_ref, n_ref, dgain_ref, loss_ref):
        @pl.when(pl.program_id(0) == 0)
        def _():
            dgain_ref[...] = jnp.zeros_like(dgain_ref)
            loss_ref[...] = jnp.zeros_like(loss_ref)

        x = h_ref[...]
        gain_v = g_ref[...]
        n, xh, r = _rms(x, gain_v)
        nbf = n.astype(BF16)
        n_ref[...] = nbf
        pg = jax.nn.sigmoid(_dot(nbf, wpg_ref[...]))
        pp = _dot(p_ref[...].astype(BF16), wpe_ref[...])
        err = (x + pg * pp) - t_ref[...]
        loss_ref[...] += 0.5 * jnp.sum(jnp.mean(err * err, axis=-1, keepdims=True))
        dy = err * (1.0 / d)
        dpp_ref[...] = (dy * pg).astype(BF16)
        dz = (dy * pp * pg * (1.0 - pg)).astype(BF16)
        dz_ref[...] = dz
        dn = _dot_nt(dz, wpg_ref[...])
        dx, dgain = _rms_bwd(xh, r, gain_v, dn)
        dh_ref[...] = dy + dx
        dgain_ref[...] += dgain

    full = lambda a: pl.BlockSpec(a.shape, lambda i: (0,) * a.ndim)
    row = pl.BlockSpec((tm, d), lambda i: (i, 0))
    return pl.pallas_call(
        body, name=name, grid=(t // tm,),
        out_shape=[jax.ShapeDtypeStruct((t, d), F32), jax.ShapeDtypeStruct((t, d), BF16),
                   jax.ShapeDtypeStruct((t, d), BF16), jax.ShapeDtypeStruct((t, d), BF16),
                   jax.ShapeDtypeStruct((1, d), F32), jax.ShapeDtypeStruct((8, LANES), F32)],
        in_specs=[row, full(gain), pl.BlockSpec((tm, pd), lambda i: (i, 0)), row, full(wpg), full(wpe)],
        out_specs=[row, row, row, row, pl.BlockSpec((1, d), lambda i: (0, 0)),
                   pl.BlockSpec((8, LANES), lambda i: (0, 0))],
        compiler_params=_params(("arbitrary",)),
    )(h, gain, p, target, wpg, wpe)


def _head_masks():
    lane = lax.broadcasted_iota(jnp.int32, (1, LANES), 1)
    m0 = (lane < HEAD_DIM).astype(F32)
    return m0, 1.0 - m0


def _head_mean(v, m0, m1):
    del m0, m1
    width = v.shape[-1]
    shift = HEAD_DIM.bit_length() - 1
    r = jnp.right_shift(lax.broadcasted_iota(jnp.int32, (width, width), 0), shift)
    c = jnp.right_shift(lax.broadcasted_iota(jnp.int32, (width, width), 1), shift)
    same_head = (r == c).astype(BF16)
    return _dot(v.astype(BF16), same_head) * (1.0 / HEAD_DIM)


def _head_norm(x, gain, m0, m1):
    r = lax.rsqrt(_head_mean(x * x, m0, m1) + EPS)
    xh = x * r
    return xh * gain, xh, r


def _head_norm_bwd(xh, r, gain, dy, m0, m1):
    gdy = gain * dy
    dx = r * (gdy - xh * _head_mean(xh * gdy, m0, m1))
    return dx, jnp.sum(dy * xh, axis=0, keepdims=True)


GROUP = 4
QW = GROUP * HEAD_DIM
STACK = GROUP * QTILE


def _kv_width(mode):
    return QW if mode == "A" else LANES


def _q_scratch_shape(mode, s_len):
    return (s_len, QW) if mode == "A" else (GROUP * s_len, LANES)


def _group_masks(dtype=F32):
    lane = lax.broadcasted_iota(jnp.int32, (1, QW), 1)
    return [((lane >= h * HEAD_DIM) & (lane < (h + 1) * HEAD_DIM)).astype(dtype) for h in range(GROUP)]


def _stack_heads(first_kv, x, m0, m1):
    out = []
    for half in range(GROUP // 2):
        xh = x[:, half * LANES:(half + 1) * LANES]
        a0, a1 = xh * m0, xh * m1
        r0, r1 = pltpu.roll(a0, HEAD_DIM, 1), pltpu.roll(a1, HEAD_DIM, 1)
        out += [jnp.where(first_kv, a0, r0), jnp.where(first_kv, r1, a1)]
    return out


def _unstack_heads(mode, first_kv, ts, m0, m1):
    if mode == "A":
        masks = _group_masks()
        return sum(t * mk for t, mk in zip(ts, masks))
    halves = []
    for half in range(GROUP // 2):
        t0 = jnp.where(first_kv, ts[2 * half], pltpu.roll(ts[2 * half], HEAD_DIM, 1))
        t1 = jnp.where(first_kv, pltpu.roll(ts[2 * half + 1], HEAD_DIM, 1), ts[2 * half + 1])
        halves.append(t0 * m0 + t1 * m1)
    return jnp.concatenate(halves, axis=1)


def _store_stacked(dst, i, heads):
    for half in range(2):
        rows = slice(half * QTILE, (half + 1) * QTILE)
        for h, x in enumerate(heads):
            dst[pl.ds((2 * i + half) * STACK + h * QTILE, QTILE), :] = x[rows].astype(dst.dtype)


def _load_stacked(mode, ref, m):
    if mode == "B":
        return ref[pl.ds(pl.multiple_of(m * STACK, STACK), STACK), :]
    x = ref[pl.ds(pl.multiple_of(m * QTILE, QTILE), QTILE), :]
    return jnp.concatenate([x * mk for mk in _group_masks(x.dtype)], axis=0)


def _attn_prep(mode, group, s_len, padk, q_ref, k_ref, v_ref, gq_ref, gk_ref, qs, k2, v2, do_ref=None, dos=None):
    m0, m1 = _head_masks()
    zpad = jnp.zeros((padk, k2.shape[1]), BF16)
    k2[pl.ds(0, padk), :] = zpad
    v2[pl.ds(0, padk), :] = zpad
    first_kv = group == 0
    rt = 2 * QTILE
    for i in range(s_len // rt):
        rows = pl.ds(i * rt, rt)
        qn, _, _ = _head_norm(q_ref[rows, :], gq_ref[...], m0, m1)
        kn, _, _ = _head_norm(k_ref[rows, :], gk_ref[...], m0, m1)
        qn = qn * (HEAD_DIM ** -0.5)
        if mode == "A":
            qs[rows, :] = qn.astype(BF16)
            if dos is not None:
                dos[rows, :] = do_ref[rows, :].astype(BF16)
        else:
            _store_stacked(qs, i, _stack_heads(first_kv, qn, m0, m1))
            if dos is not None:
                _store_stacked(dos, i, _stack_heads(first_kv, do_ref[rows, :], m0, m1))
        k2[pl.ds(padk + i * rt, rt), :] = kn.astype(BF16)
        v2[pl.ds(padk + i * rt, rt), :] = v_ref[rows, :].astype(BF16)


def _softmax_terms(mode, s, sink):
    mx = jnp.max(s, axis=-1, keepdims=True)
    if mode == "B":
        mx = jnp.maximum(mx, sink)
    e = jnp.exp(s - mx)
    l = jnp.sum(e, axis=-1, keepdims=True)
    if mode == "B":
        l = l + jnp.exp(sink - mx)
    return e, mx, l


def _sink_column(sink_ref, group):
    row = lax.broadcasted_iota(jnp.int32, (STACK, 1), 0)
    col = jnp.zeros((STACK, 1), F32)
    for h in range(GROUP):
        col = jnp.where((row >= h * QTILE) & (row < (h + 1) * QTILE), sink_ref[GROUP * group + h], col)
    return col


def _head_deltas(dd, m0, m1):
    cols = []
    for half in range(GROUP // 2):
        dh = dd[:, half * LANES:(half + 1) * LANES]
        cols += [jnp.sum(dh * m0, axis=-1, keepdims=True), jnp.sum(dh * m1, axis=-1, keepdims=True)]
    return jnp.concatenate(cols, axis=0)


def _attn_cols(mode):
    if mode == "A":
        return (lambda b, g: (b, g)), (lambda b, g: (b, 2 + g)), (lambda b, g: (b, 4 + g))
    return (lambda b, g: (b, 6 + g)), (lambda b, g: (b, 16)), (lambda b, g: (b, 17))


def _attn_fwd(mode, qkv, gq, gk, bias, sinks, bl, s_len, name):
    bw = bias.shape[-1]
    padk = bw - QTILE
    nt = s_len // QTILE
    qmap, kmap, vmap = _attn_cols(mode)

    kw = _kv_width(mode)

    def body(q_ref, k_ref, v_ref, gq_ref, gk_ref, bias_ref, sink_ref, o_ref, qs, k2, v2, s_buf, *rest):
        o_buf = rest[0] if rest else None
        group = pl.program_id(1)
        m0, m1 = _head_masks()
        first_kv = group == 0
        _attn_prep(mode, group, s_len, padk, q_ref, k_ref, v_ref, gq_ref, gk_ref, qs, k2, v2)
        col = lax.broadcasted_iota(jnp.int32, (STACK, bw), 1)
        sink = _sink_column(sink_ref, group)

        def scores(m, slot):
            r0 = pl.multiple_of(m * QTILE, QTILE)
            s = _dot_nt(_load_stacked(mode, qs, m), k2[pl.ds(r0, bw), :]) + bias_ref[...]
            s_buf[slot] = jnp.where(col >= (padk - r0), s, NEG_INF)

        def finish_tile(m, slot):
            r0 = pl.multiple_of(m * QTILE, QTILE)
            e, _, l = _softmax_terms(mode, s_buf[slot], sink)
            if mode == "A":
                o_st = _dot(e.astype(BF16), v2[pl.ds(r0, bw), :]) / l
                heads = [o_st[h * QTILE:(h + 1) * QTILE] for h in range(GROUP)]
                o_ref[pl.ds(r0, QTILE), :] = _unstack_heads(mode, first_kv, heads, m0, m1)
            else:
                o_buf[pl.ds(pl.multiple_of(m * STACK, STACK), STACK), :] = _dot((e * (1.0 / l)).astype(BF16),
                                                                                 v2[pl.ds(r0, bw), :])

        scores(0, 0)

        def pair(j, carry):
            scores(2 * j + 1, 1)
            finish_tile(2 * j, 0)
            scores(jnp.minimum(2 * j + 2, nt - 1), 0)
            finish_tile(2 * j + 1, 1)
            return carry

        lax.fori_loop(0, nt // 2, pair, 0)
        if mode == "B":
            for m in range(nt):
                heads = [o_buf[pl.ds(m * STACK + h * QTILE, QTILE), :] for h in range(GROUP)]
                o_ref[pl.ds(m * QTILE, QTILE), :] = _unstack_heads(mode, first_kv, heads, m0, m1)

    blk = lambda w, f: pl.BlockSpec((s_len, w), f)
    return pl.pallas_call(
        body, name=name, grid=(bl, B_Q_HEADS // GROUP),
        out_shape=jax.ShapeDtypeStruct((bl * s_len, B_Q_HEADS * HEAD_DIM), F32),
        in_specs=[blk(QW, qmap), blk(kw, kmap), blk(kw, vmap),
                  pl.BlockSpec((1, QW), lambda b, g: (0, 0)), pl.BlockSpec((1, kw), lambda b, g: (0, 0)),
                  pl.BlockSpec((STACK, bw), lambda b, g: (g, 0)),
                  pl.BlockSpec(memory_space=pltpu.SMEM)],
        out_specs=blk(QW, lambda b, g: (b, g)),
        scratch_shapes=[pltpu.VMEM(_q_scratch_shape(mode, s_len), BF16)] + [pltpu.VMEM((s_len + padk, kw), BF16)] * 2
        + [pltpu.VMEM((2, STACK, bw), F32)] + ([pltpu.VMEM((GROUP * s_len, LANES), F32)] if mode == "B" else []),
        compiler_params=_params(("arbitrary", "arbitrary")),
    )(qkv, qkv, qkv, gq, gk, bias.reshape(-1, bw), sinks)


def _attn_bwd(mode, qkv, gq, gk, bias, sinks, y, dy, bl, s_len, name):
    bw = bias.shape[-1]
    padk = bw - QTILE
    nt = s_len // QTILE
    qmap, kmap, vmap = _attn_cols(mode)
    t = bl * s_len
    kw = _kv_width(mode)
    kvw = 4 * LANES if mode == "A" else LANES
    dp_ahead = True

    def body(q_ref, k_ref, v_ref, gq_ref, gk_ref, bias_ref, sink_ref, y_ref, dy_ref,
             dq_ref, dk_ref, dv_ref, dgq_ref, dgk_ref, dbias_ref, dsink_ref,
             qs, k2, v2, dos, dqs, dk, dv, s_buf, dp_buf):
        group = pl.program_id(1)
        m0, m1 = _head_masks()
        first_kv = group == 0
        _attn_prep(mode, group, s_len, padk, q_ref, k_ref, v_ref, gq_ref, gk_ref, qs, k2, v2, dy_ref, dos)
        dk[...] = jnp.zeros_like(dk)
        dv[...] = jnp.zeros_like(dv)
        dbias_ref[...] = jnp.zeros_like(dbias_ref)
        col = lax.broadcasted_iota(jnp.int32, (STACK, bw), 1)
        lane8 = lax.broadcasted_iota(jnp.int32, (8, LANES), 1)
        sink = _sink_column(sink_ref, group)

        def ahead(m, slot):
            r0 = pl.multiple_of(m * QTILE, QTILE)
            band = pl.ds(r0, bw)
            s = _dot_nt(_load_stacked(mode, qs, m), k2[band, :]) + bias_ref[...]
            s_buf[slot] = jnp.where(col >= (padk - r0), s, NEG_INF)
            if dp_ahead:
                dp_buf[slot] = _dot_nt(_load_stacked(mode, dos, m), v2[band, :])

        def tile(m, slot, dsink):
            r0 = pl.multiple_of(m * QTILE, QTILE)
            rows = pl.ds(r0, QTILE)
            band = pl.ds(r0, bw)
            q_st = _load_stacked(mode, qs, m)
            do_st = _load_stacked(mode, dos, m)
            delta = _head_deltas(dy_ref[rows, :] * y_ref[rows, :], m0, m1)
            kb = k2[band, :]
            e, mx, l = _softmax_terms(mode, s_buf[slot], sink)
            inv = 1.0 / l
            pn = e * inv
            ds = pn * ((dp_buf[slot] if dp_ahead else _dot_nt(do_st, v2[band, :])) - delta)
            if mode == "A":
                dbias_ref[...] += ds
            else:
                part = jnp.exp(sink - mx) * inv * delta
                for h in range(GROUP):
                    dsink = dsink - jnp.where(lane8 == h, jnp.sum(part[h * QTILE:(h + 1) * QTILE]), 0.0)
            dsb = ds.astype(BF16)
            dv[band, :] += _dot_tn(pn.astype(BF16), do_st)
            dk[band, :] += _dot_tn(dsb, q_st)
            dq_st = _dot(dsb, kb)
            if mode == "A":
                heads = [dq_st[h * QTILE:(h + 1) * QTILE] for h in range(GROUP)]
                dq_ref[rows, :] = _unstack_heads(mode, first_kv, heads, m0, m1)
            else:
                dqs[pl.ds(pl.multiple_of(m * STACK, STACK), STACK), :] = dq_st
            return dsink

        ahead(0, 0)

        def pair(j, dsink):
            ahead(2 * j + 1, 1)
            dsink = tile(2 * j, 0, dsink)
            ahead(jnp.minimum(2 * j + 2, nt - 1), 0)
            return tile(2 * j + 1, 1, dsink)

        dsink = lax.fori_loop(0, nt // 2, pair, jnp.zeros((8, LANES), F32))
        dsink_ref[...] = dsink

        rt = 2 * QTILE
        dgq = jnp.zeros((1, QW), F32)
        dgk = jnp.zeros((1, kw), F32)
        for i in range(s_len // rt):
            rows = pl.ds(i * rt, rt)
            src = pl.ds(padk + i * rt, rt)
            gq_v, gk_v = gq_ref[...], gk_ref[...]
            _, qh, qr = _head_norm(q_ref[rows, :], gq_v, m0, m1)
            _, kh, kr = _head_norm(k_ref[rows, :], gk_v, m0, m1)
            if mode == "A":
                dqn = dq_ref[rows, :] * (HEAD_DIM ** -0.5)
            else:
                dqn = jnp.concatenate(
                    [_unstack_heads(mode, first_kv, [dqs[pl.ds((2 * i + half) * STACK + h * QTILE, QTILE), :]
                                                     for h in range(GROUP)], m0, m1)
                     for half in range(2)], axis=0) * (HEAD_DIM ** -0.5)
            dq_raw, dgq_i = _head_norm_bwd(qh, qr, gq_v, dqn, m0, m1)
            dk_raw, dgk_i = _head_norm_bwd(kh, kr, gk_v, dk[src, :], m0, m1)
            dvn = dv[src, :]
            dq_ref[rows, :] = dq_raw
            if mode == "A":
                dk_ref[rows, :] = dk_raw
                dv_ref[rows, :] = dvn
            else:
                @pl.when(group == 0)
                def _():
                    dk_ref[rows, :] = dk_raw
                    dv_ref[rows, :] = dvn

                @pl.when(group != 0)
                def _():
                    dk_ref[rows, :] += dk_raw
                    dv_ref[rows, :] += dvn
            dgq, dgk = dgq + dgq_i, dgk + dgk_i
        dgq_ref[...] = jnp.broadcast_to(dgq, (8, QW))
        dgk_ref[...] = jnp.broadcast_to(dgk, (8, kw))

    ng = B_Q_HEADS // GROUP
    blk = lambda w, f: pl.BlockSpec((s_len, w), f)
    small = lambda w: pl.BlockSpec((None, None, 8, w), lambda b, g: (b, g, 0, 0))
    own = lambda b, g: (b, g)
    kvmap = own if mode == "A" else (lambda b, g: (b, 0))
    pad_f32 = pltpu.VMEM((s_len + padk, kw), F32)
    pad_bf = pltpu.VMEM((s_len + padk, kw), BF16)
    stack_bf = pltpu.VMEM(_q_scratch_shape(mode, s_len), BF16)
    outs = pl.pallas_call(
        body, name=name, grid=(bl, ng),
        out_shape=[jax.ShapeDtypeStruct((t, ng * QW), F32), jax.ShapeDtypeStruct((t, kvw), F32),
                   jax.ShapeDtypeStruct((t, kvw), F32),
                   jax.ShapeDtypeStruct((bl, ng, 8, QW), F32), jax.ShapeDtypeStruct((bl, ng, 8, kw), F32),
                   jax.ShapeDtypeStruct((bl, ng * STACK, bw), F32), jax.ShapeDtypeStruct((bl, ng, 8, LANES), F32)],
        in_specs=[blk(QW, qmap), blk(kw, kmap), blk(kw, vmap),
                  pl.BlockSpec((1, QW), lambda b, g: (0, 0)), pl.BlockSpec((1, kw), lambda b, g: (0, 0)),
                  pl.BlockSpec((STACK, bw), lambda b, g: (g, 0)),
                  pl.BlockSpec(memory_space=pltpu.SMEM),
                  blk(QW, own), blk(QW, own)],
        out_specs=[blk(QW, own), blk(kw, kvmap), blk(kw, kvmap), small(QW), small(kw),
                   pl.BlockSpec((None, STACK, bw), lambda b, g: (b, g, 0)), small(LANES)],
        scratch_shapes=[stack_bf, pad_bf, pad_bf, stack_bf,
                        pltpu.VMEM((8, LANES) if mode == "A" else _q_scratch_shape(mode, s_len), F32),
                        pad_f32, pad_f32, pltpu.VMEM((2, STACK, bw), F32),
                        pltpu.VMEM((2, STACK, bw) if dp_ahead else (8, LANES), F32)],
        compiler_params=_params(("arbitrary", "arbitrary")),
    )(qkv, qkv, qkv, gq, gk, bias.reshape(-1, bw), sinks, y, dy)
    outs = list(outs)
    outs[5] = outs[5].reshape(bl, B_Q_HEADS, QTILE, bw)
    return outs


def _band_geometry(prev):
    bw = QTILE + prev * CHUNK
    i = np.arange(QTILE)[:, None]
    j = np.arange(bw)[None, :]
    dist = i + prev * CHUNK - j
    valid = (j // CHUNK >= i // CHUNK) & (j // CHUNK <= i // CHUNK + prev)
    return dist, valid


A_VAR0 = (A_PREV * CHUNK - A_MAX_REL) // LANES * LANES


A_NVAR = QTILE + A_PREV * CHUNK - A_VAR0


def _skew_rows(x, sign):
    rows, n = x.shape
    row = lax.broadcasted_iota(jnp.int32, x.shape, 0)
    b = 1
    while b < rows:
        x = jnp.where((row & b) != 0, pltpu.roll(x, (sign * b) % n, 1), x)
        b *= 2
    return x


def _rel_bias_expand(table, name):
    _, valid = _band_geometry(A_PREV)
    bw = valid.shape[1]
    valid_f = jnp.asarray(valid.astype(np.float32))
    rev = jnp.flip(table[:, 1:], axis=1).reshape(A_HEADS, 1, A_NVAR)

    def body(rev_ref, valid_ref, o_ref):
        rowv = jnp.broadcast_to(rev_ref[...], (QTILE, A_NVAR))
        top = rowv[:, 0:1]
        var = _skew_rows(rowv, 1)
        row = lax.broadcasted_iota(jnp.int32, (QTILE, A_NVAR), 0)
        colv = lax.broadcasted_iota(jnp.int32, (QTILE, A_NVAR), 1)
        var = jnp.where(colv < row, top, var)
        ok = valid_ref[...] > 0.5
        o_ref[:, :A_VAR0] = jnp.where(ok[:, :A_VAR0], top, NEG_INF)
        o_ref[:, A_VAR0:] = jnp.where(ok[:, A_VAR0:], var, NEG_INF)

    return pl.pallas_call(
        body, name=name, grid=(A_HEADS,),
        out_shape=jax.ShapeDtypeStruct((A_HEADS, QTILE, bw), F32),
        in_specs=[pl.BlockSpec((None, 1, A_NVAR), lambda h: (h, 0, 0)), pl.BlockSpec((QTILE, bw), lambda h: (0, 0))],
        out_specs=pl.BlockSpec((None, QTILE, bw), lambda h: (h, 0, 0)),
        compiler_params=_params(("arbitrary",)),
    )(rev, valid_f)


def _rel_bias_grad(dbias, name):
    bl = dbias.shape[0]
    bw = dbias.shape[-1]

    def body(db_ref, o_ref):
        g = db_ref[0]
        for b in range(1, bl):
            g = g + db_ref[b]
        sk = _skew_rows(g[:, A_VAR0:], -1)
        row = lax.broadcasted_iota(jnp.int32, (QTILE, A_NVAR), 0)
        colv = lax.broadcasted_iota(jnp.int32, (QTILE, A_NVAR), 1)
        wrapped = (row + colv) >= A_NVAR
        main = jnp.sum(jnp.where(wrapped, 0.0, sk), axis=0, keepdims=True)
        top = jnp.sum(g[:, :A_VAR0]) + jnp.sum(jnp.where(wrapped, sk, 0.0))
        o_ref[:, :A_NVAR] = jnp.broadcast_to(main, (8, A_NVAR))
        o_ref[:, A_NVAR:] = jnp.full((8, LANES), top, F32)

    out = pl.pallas_call(
        body, name=name, grid=(A_HEADS,),
        out_shape=jax.ShapeDtypeStruct((A_HEADS, 8, A_NVAR + LANES), F32),
        in_specs=[pl.BlockSpec((bl, None, QTILE, bw), lambda h: (0, h, 0, 0))],
        out_specs=pl.BlockSpec((None, 8, A_NVAR + LANES), lambda h: (h, 0, 0)),
        compiler_params=_params(("arbitrary",)),
    )(dbias)
    main, top = out[:, 0, :A_NVAR], out[:, 0, A_NVAR]
    fm = jnp.flip(main, axis=1)
    return jnp.concatenate([jnp.zeros((A_HEADS, 1), F32), fm[:, :-1], fm[:, -1:] + top[:, None]], axis=1)


def _alibi_bias():
    dist, valid = _band_geometry(B_PREV)
    slopes = np.array([2.0 ** (-8.0 * (h + 1) / B_Q_HEADS) for h in range(B_Q_HEADS)], dtype=np.float32)
    bias = -slopes[:, None, None] * np.abs(dist).astype(np.float32)[None]
    return jnp.asarray(np.where(valid[None], bias, np.float32(NEG_INF)).astype(np.float32))


SMALL_NAMES = ("ffn1_norm", "mix_norm", "ffn2_norm", "ple_norm", "a_q_norm", "a_k_norm", "b_q_norm", "b_k_norm",
               "a_rel_bias", "b_sinks", "loss")


def _pack_small(vals):
    rows = []
    for nme in SMALL_NAMES:
        v = vals[nme].astype(F32)
        if nme == "a_rel_bias":
            v = jnp.pad(v.reshape(A_HEADS, -1), ((0, 0), (0, 3 * LANES - (2 * A_MAX_REL + 1))))
        v = v.reshape(-1)
        v = jnp.pad(v, (0, (-v.shape[0]) % LANES))
        rows.append(v.reshape(-1, LANES))
    out = jnp.concatenate(rows, axis=0)
    return jnp.pad(out, ((0, (-out.shape[0]) % 8), (0, 0)))


def _unpack_small(packed, shapes):
    out, r = {}, 0
    for nme in SMALL_NAMES:
        shp = shapes[nme]
        if nme == "a_rel_bias":
            nr = A_HEADS * 3
            out[nme] = packed[r:r + nr].reshape(A_HEADS, 3 * LANES)[:, :2 * A_MAX_REL + 1].reshape(shp)
        else:
            size = int(np.prod(shp)) if shp else 1
            nr = -(-size // LANES)
            out[nme] = packed[r:r + nr].reshape(-1)[:size].reshape(shp)
        r += nr
    return out


BIG_NAMES = ("ffn1_w_gu", "ffn1_w_down", "w_in", "w_gate", "w_proj_a", "w_proj_b", "w_out",
             "ffn2_w_gu", "ffn2_w_down", "w_ple_gate", "w_ple_proj")
WEIGHT_ORDER = ("ffn1_norm", "ffn1_w_gu", "ffn1_w_down", "mix_norm", "w_in", "a_q_norm", "a_k_norm", "a_rel_bias",
                "b_q_norm", "b_k_norm", "b_sinks", "w_gate", "w_proj_a", "w_proj_b", "w_out", "ffn2_norm",
                "ffn2_w_gu", "ffn2_w_down", "ple_norm", "w_ple_gate", "w_ple_proj")


TRANSPOSED = ("ffn1_w_gu", "ffn2_w_gu", "w_in")


def _local(a, nme):
    return a[0].T if nme in TRANSPOSED else a[0]


def _full_cols(wg):
    nb, k, n = wg.shape
    return jnp.transpose(wg, (1, 0, 2)).reshape(k, nb * n)


def _step(x, p, target, w, m, v):
    bl, s_len, d = x.shape
    t = bl * s_len
    h0 = x.reshape(t, d)
    pt = p.reshape(t, p.shape[-1])
    tgt = target.reshape(t, d)

    g_ffn1, g_mix, g_ffn2, g_ple = w["ffn1_norm"], w["mix_norm"], w["ffn2_norm"], w["ple_norm"]
    tiled = lambda a, width: jnp.tile(a.reshape(1, HEAD_DIM), (1, width // HEAD_DIM))
    gqa, gka = tiled(w["a_q_norm"], QW), tiled(w["a_k_norm"], _kv_width("A"))
    gqb, gkb = tiled(w["b_q_norm"], QW), tiled(w["b_k_norm"], _kv_width("B"))
    sinks = w["b_sinks"].reshape(B_Q_HEADS)
    bias_b = _alibi_bias()

    ffn1_names = ("ffn1_w_gu", "ffn1_w_down")
    shard = {nme: _local(w[nme], nme).astype(BF16) for nme in ffn1_names}
    send1, recv1, bufs, token = _gather_start([shard[nme] for nme in ffn1_names], h0, "gather_start_ffn1")
    zero = token[0, 0]
    shard.update({nme: (_local(w[nme], nme) + zero).astype(BF16) for nme in BIG_NAMES if nme not in ffn1_names})
    bias_a = _rel_bias_expand(w["a_rel_bias"][0] + zero, "rel_bias_expand")
    send2, recv2, bufs, token = _gather_pass(send1, recv1, bufs, bias_a, "gather_pass_ffn1")
    wgu1, wd1 = _gather_wait(send2, recv2, bufs, shard["ffn2_w_gu"], "gather_wait_ffn1")
    nf = wgu1.shape[1]
    wd1 = wd1.reshape(N_DEV // 2, nf, d)
    mixer_names = ("w_in", "w_gate")
    rest_names = ("w_proj_a", "w_proj_b", "w_out", "ffn2_w_gu", "ffn2_w_down", "w_ple_gate", "w_ple_proj")
    send1, recv1, bufs, token = _gather_start([shard[nme] for nme in mixer_names], wgu1, "gather_start_mixer")
    rsend1, rrecv1, rest_bufs, token = _gather_start([shard[nme] for nme in rest_names], token, "gather_start_rest")

    h1, gu1 = _ffn_fwd(h0, g_ffn1 + token[0, 0], wgu1, wd1, "ffn1_fwd")
    send2, recv2, bufs, token = _gather_pass(send1, recv1, bufs, h1, "gather_pass_mixer")
    win, wgate = _gather_wait(send2, recv2, bufs, token, "gather_wait_mixer")
    win, wgate = win.reshape(IN_COLS, d), _full_cols(wgate)
    un, qkv, gate = _proj_fwd(h1, g_mix, win, wgate, "proj_fwd")
    ya = _attn_fwd("A", qkv, gqa, gka, bias_a, sinks, bl, s_len, "attn_a_fwd")
    rsend2, rrecv2, rest_bufs, token = _gather_pass(rsend1, rrecv1, rest_bufs, ya, "gather_pass_rest")
    yb = _attn_fwd("B", qkv, gqb + token[0, 0], gkb, bias_b, sinks, bl, s_len, "attn_b_fwd")
    gathered = dict(zip(rest_names, _gather_wait(rsend2, rrecv2, rest_bufs, yb, "gather_wait_rest")))
    wgu2 = gathered["ffn2_w_gu"]
    wd2 = gathered["ffn2_w_down"].reshape(N_DEV // 2, nf, d)
    wpa = _full_cols(gathered["w_proj_a"])
    wpb = _full_cols(gathered["w_proj_b"])
    wpe = _full_cols(gathered["w_ple_proj"])
    wout = gathered["w_out"].reshape(d, d)
    wpg = gathered["w_ple_gate"].reshape(d, d)
    h2, merged, pa, pb = _merge_fwd(h1, ya, yb, gate, wpa, wpb, wout, "merge_fwd")
    h3, gu2 = _ffn_fwd(h2, g_ffn2, wgu2, wd2, "ffn2_fwd")
    dh3, dz4, dpp, n4, dg_ple, loss_part = _ple_loss(h3, g_ple, pt, tgt, wpg, wpe, "ple_loss")

    xi, yi, ci = _place()
    me = jnp.stack([4 * xi + 2 * yi + ci, 2 * xi + yi]).astype(jnp.int32)
    g32, g16, big, pairs = {}, {}, {}, {}

    def keep(nme, pair, rows=None):
        for store, g in zip((g32, g16), pair):
            store[nme] = g if rows is None else g.reshape(N_DEV, rows, d)

    def start(names, after, tag):
        send, recv, parts, lands, token = _scatter_start([g16[nme] for nme in names], after, "grads_start_" + tag)
        return names, send, recv, parts, lands, token

    def start_two_level(names, after, tag):
        views = [g16[nme].reshape((4, 2) + g16[nme].shape[1:]) for nme in names]
        for nme, got in zip(names, _pair_exchange(views, "grads_pair_" + tag)):
            pairs[nme] = got.reshape((4,) + got.shape[2:])
        sums = [_pair_sum(g32[nme], pairs[nme], me, "pair_sum_" + nme) for nme in names]
        send, recv, parts, lands, token = _scatter_start(sums, after, "grads_start_" + tag, SAME_CORE_CHIPS)
        return names, send, recv, parts, lands, token

    def finish(state, after, tag):
        names, send, recv, parts, lands, _ = state
        relations = SAME_CORE_CHIPS if names[0] in pairs else ALL_PEERS
        lands = _scatter_wait(send, recv, parts, lands, after, "grads_wait_" + tag, relations)
        return names, lands

    def adam(done, dep):
        for nme, land in zip(*done):
            outs = _final_adam(g32[nme], land, _local(w[nme], nme), _local(m[nme], nme), _local(v[nme], nme), me, dep,
                               "adam_" + nme, pairs.get(nme))
            big[nme] = [(o.T if nme in TRANSPOSED else o)[None] for o in outs]

    keep("w_ple_gate", _dw(n4, dz4, 1, d, "dw_ple_gate"), d // N_DEV)
    keep("w_ple_proj", _dw(pt, dpp, N_DEV, d // N_DEV, "dw_ple_proj"))
    early = [(start(("w_ple_gate", "w_ple_proj"), dh3, "ple"), "ple")]

    dh2, dgu2, a2, n3, dg_ffn2 = _ffn_bwd(dh3, h2, g_ffn2 + early[-1][0][-1][0, 0], gu2, wgu2, wd2, "ffn2_bwd")
    keep("ffn2_w_down", _dw(a2, dh3, N_DEV // 2, d, "dw_ffn2_down", 0.5), nf // 2)
    early.append((start(("ffn2_w_down",), dh2, "ffn2_down"), "ffn2_down"))
    keep("ffn2_w_gu", _dw(dgu2, n3, N_DEV, d, "dw_ffn2_gu", dep=early[-1][0][-1]))
    flight = start(("ffn2_w_gu",), dh2, "ffn2")

    dpa, dpb, dzg, dya, dyb = _merge_bwd(dh2, pa, pb, gate, wpa, wpb, wout, "merge_bwd")
    keep("w_out", _dw(merged, dh2, 1, d, "dw_out"), d // N_DEV)
    keep("w_proj_a", _dw(ya, dpa, N_DEV, d // N_DEV, "dw_proj_a"))
    keep("w_proj_b", _dw(yb, dpb, N_DEV, d // N_DEV, "dw_proj_b"))
    keep("w_gate", _dw(un, dzg, N_DEV, 2 * d // N_DEV, "dw_gate"))

    tok = flight[-1][0, 0]
    dqa, dka, dva, dgqa, dgka, dbias, _ = _attn_bwd("A", qkv, gqa + tok, gka, bias_a, sinks, ya, dya, bl, s_len,
                                                     "attn_a_bwd")
    dqb, dkb, dvb, dgqb, dgkb, _, dsink = _attn_bwd("B", qkv, gqb, gkb, bias_b, sinks, yb, dyb, bl, s_len, "attn_b_bwd")
    dqkv = [dqa, dka, dva, dqb, dkb, dvb]
    dtab = _rel_bias_grad(dbias, "rel_bias_grad")

    dh1, dg_mix = _proj_bwd(dh2, h1, g_mix, dzg, dqkv, win, wgate, "proj_bwd")
    keep("w_in", _dw_rows(dqkv, un, "dw_in"), IN_COLS // N_DEV)
    waiting = [finish(state, g32["w_in"], tag) for state, tag in early]
    done = finish(flight, waiting[-1][1][0], "ffn2")
    flight = start(("w_out", "w_proj_a", "w_proj_b", "w_gate", "w_in"), done[1][0], "mixer")
    waiting.append(done)

    dh0, dgu1, a1, n1, dg_ffn1 = _ffn_bwd(dh1, h0, g_ffn1 + flight[-1][0, 0], gu1, wgu1, wd1, "ffn1_bwd")
    keep("ffn1_w_down", _dw(a1, dh1, N_DEV // 2, d, "dw_ffn1_down", 0.5), nf // 2)
    done = finish(flight, g32["ffn1_w_down"], "mixer")
    flight = start(("ffn1_w_down",), done[1][0], "ffn1_down")
    waiting.append(done)

    keep("ffn1_w_gu", _dw(dgu1, n1, N_DEV, d, "dw_ffn1_gu", dep=flight[-1]))
    done = finish(flight, g32["ffn1_w_gu"], "ffn1_down")
    flight = start_two_level(("ffn1_w_gu",), done[1][0], "ffn1_gu")
    for group in waiting + [done]:
        adam(group, flight[-1])
    behind = 0.0 * big["ffn1_w_down"][0][0, 0, :1]
    smalls = (dg_ffn1, dg_mix, dg_ffn2, dg_ple + behind, dgqa, dgka, dgqb, dgkb, dtab, dsink)
    return dh0, loss_part, big, smalls, flight, finish, adam


def kernel(x, p, ffn1_norm, ffn1_w_gu, ffn1_w_down, mix_norm, w_in, a_q_norm, a_k_norm, a_rel_bias, b_q_norm, b_k_norm, b_sinks, w_gate, w_proj_a, w_proj_b, w_out, ffn2_norm, ffn2_w_gu, ffn2_w_down, ple_norm, w_ple_gate, w_ple_proj, loss_target, m_ffn1_norm, m_ffn1_w_gu, m_ffn1_w_down, m_mix_norm, m_w_in, m_a_q_norm, m_a_k_norm, m_a_rel_bias, m_b_q_norm, m_b_k_norm, m_b_sinks, m_w_gate, m_w_proj_a, m_w_proj_b, m_w_out, m_ffn2_norm, m_ffn2_w_gu, m_ffn2_w_down, m_ple_norm, m_w_ple_gate, m_w_ple_proj, v_ffn1_norm, v_ffn1_w_gu, v_ffn1_w_down, v_mix_norm, v_w_in, v_a_q_norm, v_a_k_norm, v_a_rel_bias, v_b_q_norm, v_b_k_norm, v_b_sinks, v_w_gate, v_w_proj_a, v_w_proj_b, v_w_out, v_ffn2_norm, v_ffn2_w_gu, v_ffn2_w_down, v_ple_norm, v_w_ple_gate, v_w_ple_proj):
    w = dict(ffn1_norm=ffn1_norm, ffn1_w_gu=ffn1_w_gu, ffn1_w_down=ffn1_w_down, mix_norm=mix_norm, w_in=w_in,
             a_q_norm=a_q_norm, a_k_norm=a_k_norm, a_rel_bias=a_rel_bias, b_q_norm=b_q_norm, b_k_norm=b_k_norm,
             b_sinks=b_sinks, w_gate=w_gate, w_proj_a=w_proj_a, w_proj_b=w_proj_b, w_out=w_out, ffn2_norm=ffn2_norm,
             ffn2_w_gu=ffn2_w_gu, ffn2_w_down=ffn2_w_down, ple_norm=ple_norm, w_ple_gate=w_ple_gate,
             w_ple_proj=w_ple_proj)
    m = dict(ffn1_norm=m_ffn1_norm, ffn1_w_gu=m_ffn1_w_gu, ffn1_w_down=m_ffn1_w_down, mix_norm=m_mix_norm,
             w_in=m_w_in, a_q_norm=m_a_q_norm, a_k_norm=m_a_k_norm, a_rel_bias=m_a_rel_bias, b_q_norm=m_b_q_norm,
             b_k_norm=m_b_k_norm, b_sinks=m_b_sinks, w_gate=m_w_gate, w_proj_a=m_w_proj_a, w_proj_b=m_w_proj_b,
             w_out=m_w_out, ffn2_norm=m_ffn2_norm, ffn2_w_gu=m_ffn2_w_gu, ffn2_w_down=m_ffn2_w_down,
             ple_norm=m_ple_norm, w_ple_gate=m_w_ple_gate, w_ple_proj=m_w_ple_proj)
    v = dict(ffn1_norm=v_ffn1_norm, ffn1_w_gu=v_ffn1_w_gu, ffn1_w_down=v_ffn1_w_down, mix_norm=v_mix_norm,
             w_in=v_w_in, a_q_norm=v_a_q_norm, a_k_norm=v_a_k_norm, a_rel_bias=v_a_rel_bias, b_q_norm=v_b_q_norm,
             b_k_norm=v_b_k_norm, b_sinks=v_b_sinks, w_gate=v_w_gate, w_proj_a=v_w_proj_a, w_proj_b=v_w_proj_b,
             w_out=v_w_out, ffn2_norm=v_ffn2_norm, ffn2_w_gu=v_ffn2_w_gu, ffn2_w_down=v_ffn2_w_down,
             ple_norm=v_ple_norm, w_ple_gate=v_w_ple_gate, w_ple_proj=v_w_ple_proj)
    bl, s_len, d = x.shape

    dh0, loss_part, big, smalls, flight, finish, adam = _step(x, p[0], loss_target, w, m, v)
    dg_ffn1, dg_mix, dg_ffn2, dg_ple, dgqa, dgka, dgqb, dgkb, dtab, dsink = smalls

    fold = lambda a: a[:, :, 0, :].reshape(-1, HEAD_DIM).sum(axis=0)
    small_part = dict(
        ffn1_norm=dg_ffn1, mix_norm=dg_mix, ffn2_norm=dg_ffn2, ple_norm=dg_ple,
        a_q_norm=fold(dgqa), a_k_norm=fold(dgka), b_q_norm=fold(dgqb), b_k_norm=fold(dgkb),
        a_rel_bias=dtab,
        b_sinks=dsink.sum(axis=0)[:, 0, :GROUP].reshape(B_Q_HEADS),
        loss=loss_part[0, :1])
    zero1 = jnp.zeros((1,), F32)
    shapes = {nme: w[nme].shape for nme in SMALL_NAMES if nme != "loss"}
    shapes["loss"] = ()
    pk = lambda src: _pack_small({**{nme: src[nme] for nme in SMALL_NAMES if nme != "loss"}, "loss": zero1})
    sg, sd, sm, sv = _small_allreduce_adam(_pack_small(small_part), pk(w), pk(m), pk(v), "small_allreduce_adam")
    adam(finish(flight, sg, "ffn1_gu"), sg)
    sg, sd, sm, sv = (_unpack_small(a, shapes) for a in (sg, sd, sm, sv))

    def pick(i):
        out = []
        for nme in WEIGHT_ORDER:
            out.append(big[nme][i] if nme in big else (sg, sd, sm, sv)[i][nme])
        return out

    return (sg["loss"], dh0.reshape(bl, s_len, d), *pick(0), *pick(1), *pick(2), *pick(3))
```

```python
import jax
import jax.numpy as jnp
import numpy as np
from jax import lax
from jax.experimental import pallas as pl
from jax.experimental.pallas import tpu as pltpu

F32 = jnp.float32
BF16 = jnp.bfloat16

CHUNK = 64
HEAD_DIM = 64
A_HEADS = 8
A_PREV = 8
A_MAX_REL = 128
B_Q_HEADS = 8
B_KV_HEADS = 2
B_PREV = 2
A_WIDTH = A_HEADS * HEAD_DIM
B_Q_WIDTH = B_Q_HEADS * HEAD_DIM
B_KV_WIDTH = B_KV_HEADS * HEAD_DIM
IN_COLS = 3 * A_WIDTH + B_Q_WIDTH + 2 * B_KV_WIDTH
EPS = 1e-6
NEG_INF = -1e30
ADAM_LR = 0.001
ADAM_B1 = 0.9
ADAM_B2 = 0.999
ADAM_EPS = 1e-08
ADAM_WD = 0.01
ADAM_STEP = 10

N_DEV = 8
LANES = 128
QTILE = 2 * CHUNK
VMEM_LIMIT = 56 * 1024 * 1024
ADAM_TILE_ELEMS = 256 * 1024

MESH_ID = pl.DeviceIdType.MESH
ANY = pl.BlockSpec(memory_space=pl.ANY)
HBM = pl.BlockSpec(memory_space=pltpu.HBM)
SEM = pl.BlockSpec(memory_space=pltpu.SEMAPHORE)
SIDE_EFFECT = pltpu.SideEffectType.DATAFLOW_SIDE_EFFECTING


def _dot(a, b):
    return jnp.dot(a, b, preferred_element_type=F32)


def _dot_nt(a, b):
    return lax.dot_general(a, b, (((1,), (1,)), ((), ())), preferred_element_type=F32)


def _dot_tn(a, b):
    return lax.dot_general(a, b, (((0,), (0,)), ((), ())), preferred_element_type=F32)


def _params(sem=None, vmem=VMEM_LIMIT):
    return pltpu.CompilerParams(dimension_semantics=sem, vmem_limit_bytes=vmem)


def _row_tile(t, want):
    while t % want:
        want //= 2
    return want


def _place():
    return lax.axis_index("x"), lax.axis_index("y"), lax.axis_index("c")


def _gather_level(bufs, send_sems, recv_sems, level, shards=None):
    x, y, c = _place()
    me, sib = (x, y, c), (x, y, 1 - c)
    chips = [(1 - x, y), (x, 1 - y), (1 - x, 1 - y)]

    def copy(w, k, block, to):
        px, py, pc = block
        rows = bufs[w].at[4 * px + 2 * py + pc]
        src = shards[w] if shards is not None and block is me else rows
        return pltpu.make_async_remote_copy(src_ref=src, dst_ref=rows, send_sem=send_sems.at[k], recv_sem=recv_sems.at[k],
                                            device_id=to, device_id_type=MESH_ID)

    n = len(bufs)
    own = []
    if level == 1:
        own = [pltpu.make_async_copy(bufs[w].at[4 * x + 2 * y + c] if shards is None else shards[w],
                                     bufs[w].at[4 * x + 2 * y + c], send_sems.at[4 * n + w]) for w in range(n)]
    out, arriving = [], []
    for w in range(len(bufs)):
        if level == 1:
            out.append(copy(w, 4 * w, me, sib))
            arriving.append(copy(w, 4 * w, sib, me))
        for j, chip in enumerate(chips):
            if level == 1:
                out.append(copy(w, 4 * w + 1 + j, me, (*chip, c)))
                arriving.append(copy(w, 4 * w + 1 + j, (*chip, c), me))
            else:
                out.append(copy(w, 3 * w + j, (*chip, c), sib))
                arriving.append(copy(w, 3 * w + j, (*chip, 1 - c), me))
    return out, arriving, own


def _split_call(body, name, bufs, sems_in, after, n_sems_out, token, extra=()):
    n = len(bufs)
    out_shape = [pltpu.SemaphoreType.DMA((n_sems_out,))] * (2 if n_sems_out else 0)
    out_shape += [pltpu.HBM(a.shape, a.dtype) for a in bufs]
    out_specs = [SEM] * (2 if n_sems_out else 0) + [HBM] * n
    if token:
        out_shape.append(jax.ShapeDtypeStruct((8, LANES), F32))
        out_specs.append(pl.BlockSpec(memory_space=pltpu.VMEM))
    first = 2 if n_sems_out else 0
    return pl.pallas_call(
        body, name=name, out_shape=tuple(out_shape),
        in_specs=[HBM] * (n + len(extra)) + [SEM] * len(sems_in) + [ANY], out_specs=tuple(out_specs),
        input_output_aliases={i: first + i for i in range(n)},
        compiler_params=pltpu.CompilerParams(has_side_effects=SIDE_EFFECT),
    )(*bufs, *extra, *sems_in, after)


def _gather_start(shards, after, name):
    n = len(shards)
    hbm = lambda a: pltpu.with_memory_space_constraint(a, pltpu.HBM)
    bufs = [hbm(lax.empty((N_DEV,) + s.shape, s.dtype)) for s in shards]

    def body(*refs):
        out, _, own = _gather_level(refs[:n], refs[2 * n + 1], refs[2 * n + 2], 1, shards=refs[n:2 * n])
        for cp in own + out:
            cp.start()
        refs[-1][...] = jnp.zeros_like(refs[-1])

    outs = _split_call(body, name, bufs + [hbm(s) for s in shards], [], after, 5 * n, True)
    return outs[0], outs[1], list(outs[2:2 + 2 * n]), outs[-1]


def _gather_pass(send1, recv1, bufs_and_shards, after, name):
    n = len(bufs_and_shards) // 2
    bufs = bufs_and_shards

    def body(*refs):
        refs = refs[:n] + refs[2 * n:]
        out1, in1, own = _gather_level(refs[:n], refs[n], refs[n + 1], 1)
        out2, _, _ = _gather_level(refs[:n], refs[n + 3], refs[n + 4], 2)
        for cp in in1:
            cp.wait_recv()
        for cp in out2:
            cp.start()
        for cp in out1:
            cp.wait_send()
        for cp in own:
            cp.wait()
        refs[-1][...] = jnp.zeros_like(refs[-1])

    outs = _split_call(body, name, bufs, [send1, recv1], after, 3 * n, True)
    return outs[0], outs[1], list(outs[2:2 + n]), outs[-1]


def _gather_wait(send2, recv2, bufs, after, name):
    n = len(bufs)

    def body(*refs):
        out2, in2, _ = _gather_level(refs[:n], refs[n], refs[n + 1], 2)
        for cp in in2:
            cp.wait_recv()
        for cp in out2:
            cp.wait_send()

    return list(_split_call(body, name, bufs, [send2, recv2], after, 0, False))


ALL_PEERS = tuple(range(1, N_DEV))
SAME_CORE_CHIPS = (2, 4, 6)


def _scatter_copies(parts, lands, send_sems, recv_sems, relations):
    x, y, c = _place()
    ns = len(relations)
    cps = []
    for w, (part, land) in enumerate(zip(parts, lands)):
        for i, k in enumerate(relations):
            px, py, pc = x ^ ((k >> 2) & 1), y ^ ((k >> 1) & 1), c ^ (k & 1)
            block = 4 * px + 2 * py + pc if part.shape[0] == N_DEV else 2 * px + py
            cps.append(pltpu.make_async_remote_copy(
                src_ref=part.at[block], dst_ref=land.at[i],
                send_sem=send_sems.at[ns * w + i], recv_sem=recv_sems.at[ns * w + i],
                device_id=(px, py, pc), device_id_type=MESH_ID))
    return cps


def _scatter_start(parts, after, name, relations=ALL_PEERS):
    n = len(parts)
    ns = len(relations)

    def body(*refs):
        ins, lands = refs[:n], refs[n:2 * n]
        send_sems, recv_sems = refs[2 * n + 1], refs[2 * n + 2]
        token = refs[-1]
        for cp in _scatter_copies(ins, lands, send_sems, recv_sems, relations):
            cp.start()
        token[...] = jnp.zeros_like(token)

    land_shapes = [(ns,) + p.shape[1:] for p in parts]
    in_hbm = [pltpu.with_memory_space_constraint(p, pltpu.HBM) for p in parts]
    in_hbm += [pltpu.with_memory_space_constraint(lax.empty(s, p.dtype), pltpu.HBM) for s, p in zip(land_shapes, parts)]
    outs = pl.pallas_call(
        body, name=name,
        out_shape=(pltpu.SemaphoreType.DMA((ns * n,)), pltpu.SemaphoreType.DMA((ns * n,)),
                   *[pltpu.HBM(p.shape, p.dtype) for p in parts],
                   *[pltpu.HBM(s, p.dtype) for s, p in zip(land_shapes, parts)],
                   jax.ShapeDtypeStruct((8, LANES), F32)),
        in_specs=[HBM] * (2 * n) + [ANY],
        out_specs=(SEM, SEM, *[HBM] * (2 * n), pl.BlockSpec(memory_space=pltpu.VMEM)),
        input_output_aliases={i: 2 + i for i in range(2 * n)},
        compiler_params=pltpu.CompilerParams(has_side_effects=SIDE_EFFECT),
    )(*in_hbm, after)
    return outs[0], outs[1], list(outs[2:2 + n]), list(outs[2 + n:2 + 2 * n]), outs[-1]


def _scatter_wait(send_sems, recv_sems, parts, lands, after, name, relations=ALL_PEERS):
    n = len(parts)

    def body(*refs):
        ins, lnd = refs[:n], refs[n:2 * n]
        for cp in _scatter_copies(ins, lnd, refs[2 * n], refs[2 * n + 1], relations):
            cp.wait_send()
            cp.wait_recv()

    outs = pl.pallas_call(
        body, name=name,
        out_shape=tuple(pltpu.HBM(a.shape, a.dtype) for a in parts + lands),
        in_specs=[HBM] * (2 * n) + [SEM, SEM, ANY],
        out_specs=tuple([HBM] * (2 * n)),
        input_output_aliases={i: i for i in range(2 * n)},
        compiler_params=pltpu.CompilerParams(has_side_effects=SIDE_EFFECT),
    )(*parts, *lands, send_sems, recv_sems, after)
    return list(outs[n:])


def _pair_exchange(parts, name):
    n = len(parts)

    def body(*refs):
        ins, outs = refs[:n], refs[n:2 * n]
        send_sems, recv_sems = refs[2 * n:]
        x, y, c = _place()
        cps = [pltpu.make_async_remote_copy(
            src_ref=ins[w].at[:, pl.ds(1 - c, 1)], dst_ref=outs[w], send_sem=send_sems.at[w], recv_sem=recv_sems.at[w],
            device_id=(x, y, 1 - c), device_id_type=MESH_ID) for w in range(n)]
        for cp in cps:
            cp.start()
        for cp in cps:
            cp.wait()

    return pl.pallas_call(
        body, name=name,
        out_shape=[jax.ShapeDtypeStruct((4, 1) + p.shape[2:], p.dtype) for p in parts],
        in_specs=[ANY] * n, out_specs=[ANY] * n,
        scratch_shapes=[pltpu.SemaphoreType.DMA((n,)), pltpu.SemaphoreType.DMA((n,))],
    )(*parts)


def _pair_sum(g8, r1, me, name):
    _, r, c = g8.shape
    tr = max(q for q in range(16, r + 1, 16) if r % q == 0 and q * c <= ADAM_TILE_ELEMS)

    def body(me_ref, g_ref, r_ref, o_ref):
        o_ref[...] = (g_ref[...] + r_ref[...].astype(F32)).astype(BF16)

    chip = lambda k, s: s[1] ^ (k + 1)
    return pl.pallas_call(
        body, name=name,
        out_shape=jax.ShapeDtypeStruct((4, r, c), BF16),
        grid_spec=pltpu.PrefetchScalarGridSpec(
            num_scalar_prefetch=1, grid=(3, r // tr),
            in_specs=[pl.BlockSpec((None, None, tr, c), lambda k, i, s: (chip(k, s), s[0] % 2, i, 0)),
                      pl.BlockSpec((None, tr, c), lambda k, i, s: (chip(k, s), i, 0))],
            out_specs=pl.BlockSpec((None, tr, c), lambda k, i, s: (chip(k, s), i, 0))),
        compiler_params=_params(("arbitrary", "arbitrary")),
    )(me, g8.reshape((4, 2) + g8.shape[1:]), r1)


def _adam(w, g, m, v):
    m2 = ADAM_B1 * m + (1.0 - ADAM_B1) * g
    v2 = ADAM_B2 * v + (1.0 - ADAM_B2) * (g * g)
    m_hat = m2 / (1.0 - ADAM_B1 ** ADAM_STEP)
    v_hat = v2 / (1.0 - ADAM_B2 ** ADAM_STEP)
    delta = -ADAM_LR * (m_hat / (jnp.sqrt(v_hat) + ADAM_EPS) + ADAM_WD * w)
    return delta, m2, v2


def _small_allreduce_adam(part, w, m, v, name):
    rows = part.shape[0]

    def body(p_ref, w_ref, m_ref, v_ref, g_ref, d_ref, mo_ref, vo_ref, buf, send_sems, recv_sems):
        x, y, c = _place()
        buf[0] = p_ref[...]
        cps = []
        for k in range(1, N_DEV):
            kx, ky, kc = (k >> 2) & 1, (k >> 1) & 1, k & 1
            peer = (x ^ kx, y ^ ky, c ^ kc)
            cps.append(pltpu.make_async_remote_copy(
                src_ref=p_ref, dst_ref=buf.at[k], send_sem=send_sems.at[k - 1], recv_sem=recv_sems.at[k - 1],
                device_id=peer, device_id_type=MESH_ID))
        for cp in cps:
            cp.start()
        for cp in cps:
            cp.wait()
        me = 4 * x + 2 * y + c
        total = buf[me]
        for d in range(1, N_DEV):
            total = total + buf[d ^ me]
        g_ref[...] = total
        delta, m2, v2 = _adam(w_ref[...], total, m_ref[...], v_ref[...])
        d_ref[...] = delta
        mo_ref[...] = m2
        vo_ref[...] = v2

    vm = pl.BlockSpec(memory_space=pltpu.VMEM)
    return pl.pallas_call(
        body, name=name,
        out_shape=[jax.ShapeDtypeStruct(part.shape, F32)] * 4,
        in_specs=[vm] * 4, out_specs=[vm] * 4,
        scratch_shapes=[pltpu.VMEM((N_DEV, rows, LANES), F32),
                        pltpu.SemaphoreType.DMA((N_DEV - 1,)), pltpu.SemaphoreType.DMA((N_DEV - 1,))],
    )(part, w, m, v)


def _final_adam(g8, land, w, m, v, me, dep, name, pair=None):
    _, r, c = g8.shape
    tr = max(q for q in range(16, r + 1, 16) if r % q == 0 and q * c <= ADAM_TILE_ELEMS)
    nland = land.shape[0]

    def body(me_ref, g_ref, land_ref, *rest):
        pair_ref = rest[0] if pair is not None else None
        w_ref, m_ref, v_ref, _, go_ref, d_ref, mo_ref, vo_ref = rest[-8:]
        g = g_ref[...]
        if pair_ref is not None:
            g = g + pair_ref[...].astype(F32)
        for k in range(nland):
            g = g + land_ref[k].astype(F32)
        go_ref[...] = g
        delta, m2, v2 = _adam(w_ref[...], g, m_ref[...], v_ref[...])
        d_ref[...] = delta
        mo_ref[...] = m2
        vo_ref[...] = v2

    plain = pl.BlockSpec((tr, c), lambda i, s: (i, 0))
    return pl.pallas_call(
        body, name=name,
        out_shape=[jax.ShapeDtypeStruct((r, c), F32)] * 4,
        grid_spec=pltpu.PrefetchScalarGridSpec(
            num_scalar_prefetch=1, grid=(r // tr,),
            in_specs=[pl.BlockSpec((None, tr, c), lambda i, s: (s[0], i, 0)),
                      pl.BlockSpec((nland, tr, c), lambda i, s: (0, i, 0))]
            + ([] if pair is None else [pl.BlockSpec((None, tr, c), lambda i, s: (s[1], i, 0))])
            + [plain, plain, plain, ANY],
            out_specs=[plain] * 4),
        compiler_params=_params(("arbitrary",)),
    )(*((me, g8, land) + (() if pair is None else (pair,)) + (w, m, v, dep)))


def _rms(x, gain):
    r = lax.rsqrt(jnp.mean(x * x, axis=-1, keepdims=True) + EPS)
    xh = x * r
    return xh * gain, xh, r


def _rms_bwd(xh, r, gain, dy):
    gdy = gain * dy
    dx = r * (gdy - xh * jnp.mean(xh * gdy, axis=-1, keepdims=True))
    return dx, jnp.sum(dy * xh, axis=0, keepdims=True)


def _load_weights(pairs, sems):
    cps = [pltpu.make_async_copy(src, dst, sems.at[i]) for i, (src, dst) in enumerate(pairs)]
    for cp in cps:
        cp.start()
    for cp in cps:
        cp.wait()


def _ffn_fwd(h, gain, wgu, wd, name):
    t, d = h.shape
    nb, nf, _ = wgu.shape
    nh = nb // 2
    tm = _row_tile(t, 512)

    def body(h_ref, g_ref, wgu_hbm, wd_hbm, out_ref, gu_ref, wgu_v, wd_v, sems):
        @pl.when(pl.program_id(0) == 0)
        def _():
            _load_weights([(wgu_hbm, wgu_v), (wd_hbm, wd_v)], sems)

        x = h_ref[...]
        n, _, _ = _rms(x, g_ref[...])
        nbf = n.astype(BF16)
        acc = jnp.zeros((tm, d), F32)
        for j in range(nh):
            g = _dot_nt(nbf, wgu_v[j])
            u = _dot_nt(nbf, wgu_v[j + nh])
            gu_ref[j] = g.astype(BF16)
            gu_ref[j + nh] = u.astype(BF16)
            a = (g * jax.nn.sigmoid(g)) * u
            acc = acc + _dot(a.astype(BF16), wd_v[j])
        out_ref[...] = x + 0.5 * acc

    return pl.pallas_call(
        body, name=name, grid=(t // tm,),
        out_shape=[jax.ShapeDtypeStruct((t, d), F32), jax.ShapeDtypeStruct((nb, t, nf), BF16)],
        in_specs=[pl.BlockSpec((tm, d), lambda i: (i, 0)), pl.BlockSpec((1, d), lambda i: (0, 0)), ANY, ANY],
        out_specs=[pl.BlockSpec((tm, d), lambda i: (i, 0)), pl.BlockSpec((nb, tm, nf), lambda i: (0, i, 0))],
        scratch_shapes=[pltpu.VMEM(wgu.shape, BF16), pltpu.VMEM(wd.shape, BF16), pltpu.SemaphoreType.DMA((2,))],
        compiler_params=_params(("arbitrary",)),
    )(h, gain, wgu, wd)


def _ffn_bwd(dh, h, gain, gu, wgu, wd, name):
    t, d = h.shape
    nb, nf, _ = wgu.shape
    nh = nb // 2
    tm = _row_tile(t, 256)

    def body(dh_ref, h_ref, g_ref, gu_ref, wgu_hbm, wd_hbm, dhp_ref, dgu_ref, a_ref, n_ref, dgain_ref,
             wgu_v, wd_v, sems):
        @pl.when(pl.program_id(0) == 0)
        def _():
            _load_weights([(wgu_hbm, wgu_v), (wd_hbm, wd_v)], sems)
            dgain_ref[...] = jnp.zeros_like(dgain_ref)

        x = h_ref[...]
        gain_v = g_ref[...]
        n, xh, r = _rms(x, gain_v)
        n_ref[...] = n.astype(BF16)
        dh_v = dh_ref[...]
        dfb = (0.5 * dh_v).astype(BF16)
        dn = jnp.zeros((tm, d), F32)
        for j in range(nh):
            da = _dot_nt(dfb, wd_v[j])
            g = gu_ref[j].astype(F32)
            u = gu_ref[j + nh].astype(F32)
            sg = jax.nn.sigmoid(g)
            si = g * sg
            dg = (da * u * (sg * (1.0 + g * (1.0 - sg)))).astype(BF16)
            du = (da * si).astype(BF16)
            a_ref[j] = (si * u).astype(BF16).T
            dgu_ref[j] = dg.T
            dgu_ref[j + nh] = du.T
            dn = dn + _dot(dg, wgu_v[j]) + _dot(du, wgu_v[j + nh])
        dx, dgain = _rms_bwd(xh, r, gain_v, dn)
        dhp_ref[...] = dh_v + dx
        dgain_ref[...] += dgain

    row = pl.BlockSpec((tm, d), lambda i: (i, 0))
    vec = pl.BlockSpec((1, d), lambda i: (0, 0))
    return pl.pallas_call(
        body, name=name, grid=(t // tm,),
        out_shape=[jax.ShapeDtypeStruct((t, d), F32), jax.ShapeDtypeStruct((nb, nf, t), BF16),
                   jax.ShapeDtypeStruct((nh, nf, t), BF16), jax.ShapeDtypeStruct((t, d), BF16),
                   jax.ShapeDtypeStruct((1, d), F32)],
        in_specs=[row, row, vec, pl.BlockSpec((nb, tm, nf), lambda i: (0, i, 0)), ANY, ANY],
        out_specs=[row, pl.BlockSpec((nb, nf, tm), lambda i: (0, 0, i)),
                   pl.BlockSpec((nh, nf, tm), lambda i: (0, 0, i)), row, vec],
        scratch_shapes=[pltpu.VMEM(wgu.shape, BF16), pltpu.VMEM(wd.shape, BF16), pltpu.SemaphoreType.DMA((2,))],
        compiler_params=_params(("arbitrary",)),
    )(dh, h, gain, gu, wgu, wd)


def _dw(xa, dy, nb, n, name, scale=1.0, dep=None):
    wide = xa.ndim == 2
    if wide:
        t, k = xa.shape
    else:
        k, t = xa.shape[-2:]
    tt = _row_tile(t, 1024 if wide or nb < N_DEV else 512)
    steps = t // tt
    x_spec = pl.BlockSpec((tt, k), lambda i: (i, 0)) if wide else pl.BlockSpec((nb, k, tt), lambda i: (0, 0, i))
    dy_spec = pl.BlockSpec((tt, dy.shape[1]), lambda i: (i, 0))
    acc_shape = (k, nb * n) if wide else (nb, k, n)
    stage_shape = (k, nb * n) if wide else (k, n)

    def body(x_ref, dy_ref, *rest):
        o_hbm, ob_hbm, acc, stage, sems = rest[-5:]

        @pl.when(pl.program_id(0) == 0)
        def _():
            acc[...] = jnp.zeros_like(acc)

        dyb = dy_ref[...].astype(BF16)
        if wide:
            acc[...] += _dot(x_ref[...].astype(BF16).T, dyb)
        else:
            for j in range(nb):
                acc[j] += _dot(x_ref[j], dyb)

        @pl.when(pl.program_id(0) == steps - 1)
        def _():
            if scale != 1.0:
                acc[...] = acc[...] * scale
            if wide:
                cps = [pltpu.make_async_copy(acc.at[:, pl.ds(j * n, n)] if nb > 1 else acc, o_hbm.at[j], sems.at[j])
                       for j in range(nb)]
            else:
                cps = [pltpu.make_async_copy(acc, o_hbm, sems.at[0])]
            for cp in cps:
                cp.start()
            if wide:
                stage[...] = acc[...].astype(BF16)
                bcs = [pltpu.make_async_copy(stage.at[:, pl.ds(j * n, n)] if nb > 1 else stage, ob_hbm.at[j],
                                             sems.at[nb + j]) for j in range(nb)]
                for cp in bcs:
                    cp.start()
                for cp in bcs:
                    cp.wait()
            else:
                for j in range(nb):
                    stage[...] = acc[j].astype(BF16)
                    cp = pltpu.make_async_copy(stage, ob_hbm.at[j], sems.at[nb])
                    cp.start()
                    cp.wait()
            for cp in cps:
                cp.wait()

    return pl.pallas_call(
        body, name=name, grid=(steps,),
        out_shape=[jax.ShapeDtypeStruct((nb, k, n), F32), jax.ShapeDtypeStruct((nb, k, n), BF16)],
        in_specs=[x_spec, dy_spec] + ([] if dep is None else [ANY]),
        out_specs=[ANY, ANY],
        scratch_shapes=[pltpu.VMEM(acc_shape, F32), pltpu.VMEM(stage_shape, BF16),
                        pltpu.SemaphoreType.DMA((2 * nb,))],
        compiler_params=_params(("arbitrary",)),
    )(*((xa, dy) if dep is None else (xa, dy, dep)))


def _proj_fwd(h, gain, win, wgate, name):
    t, d = h.shape
    tm = _row_tile(t, 512)
    nq, ng = win.shape[0], wgate.shape[1]

    def body(h_ref, g_ref, win_ref, wg_ref, un_ref, qkv_ref, gate_ref):
        n, _, _ = _rms(h_ref[...], g_ref[...])
        nbf = n.astype(BF16)
        un_ref[...] = nbf
        qkv_ref[...] = _dot_nt(nbf, win_ref[...])
        gate_ref[...] = jax.nn.sigmoid(_dot(nbf, wg_ref[...]))

    full = lambda a: pl.BlockSpec(a.shape, lambda i: (0,) * a.ndim)
    return pl.pallas_call(
        body, name=name, grid=(t // tm,),
        out_shape=[jax.ShapeDtypeStruct((t, d), BF16), jax.ShapeDtypeStruct((t, nq), F32),
                   jax.ShapeDtypeStruct((t, ng), F32)],
        in_specs=[pl.BlockSpec((tm, d), lambda i: (i, 0)), full(gain), full(win), full(wgate)],
        out_specs=[pl.BlockSpec((tm, d), lambda i: (i, 0)), pl.BlockSpec((tm, nq), lambda i: (i, 0)),
                   pl.BlockSpec((tm, ng), lambda i: (i, 0))],
        compiler_params=_params(("arbitrary",)),
    )(h, gain, win, wgate)


def _proj_bwd(dh, h, gain, dzg, dqkv_parts, win, wgate, name):
    t, d = h.shape
    tm = _row_tile(t, 512)
    ng = wgate.shape[1]
    np_ = len(dqkv_parts)
    widths = [a.shape[1] for a in dqkv_parts]

    def body(dh_ref, h_ref, g_ref, dzg_ref, *rest):
        part_refs, (win_ref, wg_ref, dhp_ref, dgain_ref) = rest[:np_], rest[np_:]

        @pl.when(pl.program_id(0) == 0)
        def _():
            dgain_ref[...] = jnp.zeros_like(dgain_ref)

        gain_v = g_ref[...]
        _, xh, r = _rms(h_ref[...], gain_v)
        dun = _dot_nt(dzg_ref[...], wg_ref[...])
        off = 0
        for ref, wd in zip(part_refs, widths):
            dun = dun + _dot(ref[...].astype(BF16), win_ref[off:off + wd, :])
            off += wd
        dx, dgain = _rms_bwd(xh, r, gain_v, dun)
        dhp_ref[...] = dh_ref[...] + dx
        dgain_ref[...] += dgain

    full = lambda a: pl.BlockSpec(a.shape, lambda i: (0,) * a.ndim)
    row = pl.BlockSpec((tm, d), lambda i: (i, 0))
    return pl.pallas_call(
        body, name=name, grid=(t // tm,),
        out_shape=[jax.ShapeDtypeStruct((t, d), F32), jax.ShapeDtypeStruct((1, d), F32)],
        in_specs=[row, row, full(gain), pl.BlockSpec((tm, ng), lambda i: (i, 0))]
        + [pl.BlockSpec((tm, wd), lambda i: (i, 0)) for wd in widths] + [full(win), full(wgate)],
        out_specs=[row, pl.BlockSpec((1, d), lambda i: (0, 0))],
        compiler_params=_params(("arbitrary",)),
    )(dh, h, gain, dzg, *dqkv_parts, win, wgate)


def _dw_rows(parts, dy, name):
    t, n = dy.shape
    widths = [a.shape[1] for a in parts]
    k = sum(widths)
    tt = _row_tile(t, 1024)
    steps = t // tt
    np_ = len(parts)

    def body(*refs):
        part_refs, dy_ref = refs[:np_], refs[np_]
        o_hbm, ob_hbm, acc, stage, sems = refs[np_ + 1:]

        @pl.when(pl.program_id(0) == 0)
        def _():
            acc[...] = jnp.zeros_like(acc)

        dyb = dy_ref[...].astype(BF16)
        off = 0
        for ref, wd in zip(part_refs, widths):
            acc[off:off + wd, :] += _dot(ref[...].astype(BF16).T, dyb)
            off += wd

        @pl.when(pl.program_id(0) == steps - 1)
        def _():
            stage[...] = acc[...].astype(BF16)
            cps = [pltpu.make_async_copy(acc, o_hbm.at[0], sems.at[0]),
                   pltpu.make_async_copy(stage, ob_hbm.at[0], sems.at[1])]
            for cp in cps:
                cp.start()
            for cp in cps:
                cp.wait()

    return pl.pallas_call(
        body, name=name, grid=(steps,),
        out_shape=[jax.ShapeDtypeStruct((1, k, n), F32), jax.ShapeDtypeStruct((1, k, n), BF16)],
        in_specs=[pl.BlockSpec((tt, wd), lambda i: (i, 0)) for wd in widths] + [pl.BlockSpec((tt, n), lambda i: (i, 0))],
        out_specs=[ANY, ANY],
        scratch_shapes=[pltpu.VMEM((k, n), F32), pltpu.VMEM((k, n), BF16), pltpu.SemaphoreType.DMA((2,))],
        compiler_params=_params(("arbitrary",)),
    )(*parts, dy)


def _merge_fwd(h, ya, yb, gate, wpa, wpb, wout, name):
    t, d = h.shape
    tm = _row_tile(t, 512)

    def body(h_ref, ya_ref, yb_ref, ga_ref, gb_ref, wpa_ref, wpb_ref, wout_ref, out_ref, mg_ref, pa_ref, pb_ref):
        pa = _dot(ya_ref[...].astype(BF16), wpa_ref[...])
        pb = _dot(yb_ref[...].astype(BF16), wpb_ref[...])
        merged = (ga_ref[...] * pa + gb_ref[...] * pb).astype(BF16)
        pa_ref[...] = pa.astype(BF16)
        pb_ref[...] = pb.astype(BF16)
        mg_ref[...] = merged
        out_ref[...] = h_ref[...] + _dot(merged, wout_ref[...])

    full = lambda a: pl.BlockSpec(a.shape, lambda i: (0,) * a.ndim)
    row = pl.BlockSpec((tm, d), lambda i: (i, 0))
    yrow = pl.BlockSpec((tm, ya.shape[1]), lambda i: (i, 0))
    return pl.pallas_call(
        body, name=name, grid=(t // tm,),
        out_shape=[jax.ShapeDtypeStruct((t, d), F32)] + [jax.ShapeDtypeStruct((t, d), BF16)] * 3,
        in_specs=[row, yrow, yrow, pl.BlockSpec((tm, d), lambda i: (i, 0)), pl.BlockSpec((tm, d), lambda i: (i, 1)),
                  full(wpa), full(wpb), full(wout)],
        out_specs=[row] * 4,
        compiler_params=_params(("arbitrary",)),
    )(h, ya, yb, gate, gate, wpa, wpb, wout)


def _merge_bwd(dh, pa, pb, gate, wpa, wpb, wout, name):
    t, d = dh.shape
    tm = _row_tile(t, 512)
    wy = wpa.shape[0]

    def body(dh_ref, pa_ref, pb_ref, ga_ref, gb_ref, wpa_ref, wpb_ref, wout_ref,
             dpa_ref, dpb_ref, dzg_ref, dya_ref, dyb_ref):
        dm = _dot_nt(dh_ref[...].astype(BF16), wout_ref[...])
        ga, gb = ga_ref[...], gb_ref[...]
        dpa = (dm * ga).astype(BF16)
        dpb = (dm * gb).astype(BF16)
        dpa_ref[...] = dpa
        dpb_ref[...] = dpb
        dzg_ref[:, :d] = (dm * pa_ref[...].astype(F32) * ga * (1.0 - ga)).astype(BF16)
        dzg_ref[:, d:] = (dm * pb_ref[...].astype(F32) * gb * (1.0 - gb)).astype(BF16)
        dya_ref[...] = _dot_nt(dpa, wpa_ref[...])
        dyb_ref[...] = _dot_nt(dpb, wpb_ref[...])

    full = lambda a: pl.BlockSpec(a.shape, lambda i: (0,) * a.ndim)
    row = pl.BlockSpec((tm, d), lambda i: (i, 0))
    yrow = pl.BlockSpec((tm, wy), lambda i: (i, 0))
    return pl.pallas_call(
        body, name=name, grid=(t // tm,),
        out_shape=[jax.ShapeDtypeStruct((t, d), BF16), jax.ShapeDtypeStruct((t, d), BF16),
                   jax.ShapeDtypeStruct((t, 2 * d), BF16), jax.ShapeDtypeStruct((t, wy), F32),
                   jax.ShapeDtypeStruct((t, wy), F32)],
        in_specs=[row, row, row, pl.BlockSpec((tm, d), lambda i: (i, 0)), pl.BlockSpec((tm, d), lambda i: (i, 1)),
                  full(wpa), full(wpb), full(wout)],
        out_specs=[row, row, pl.BlockSpec((tm, 2 * d), lambda i: (i, 0)), yrow, yrow],
        compiler_params=_params(("arbitrary",)),
    )(dh, pa, pb, gate, gate, wpa, wpb, wout)


def _ple_loss(h, gain, p, target, wpg, wpe, name):
    t, d = h.shape
    tm = _row_tile(t, 512)
    pd = p.shape[1]

    def body(h_ref, g_ref, p_ref, t_ref, wpg_ref, wpe_ref, dh_ref, dz_ref, dpp_ref, n_ref, dgain_ref, loss_ref):
        @pl.when(pl.program_id(0) == 0)
        def _():
            dgain_ref[...] = jnp.zeros_like(dgain_ref)
            loss_ref[...] = jnp.zeros_like(loss_ref)

        x = h_ref[...]
        gain_v = g_ref[...]
        n, xh, r = _rms(x, gain_v)
        nbf = n.astype(BF16)
        n_ref[...] = nbf
        pg = jax.nn.sigmoid(_dot(nbf, wpg_ref[...]))
        pp = _dot(p_ref[...].astype(BF16), wpe_ref[...])
        err = (x + pg * pp) - t_ref[...]
        loss_ref[...] += 0.5 * jnp.sum(jnp.mean(err * err, axis=-1, keepdims=True))
        dy = err * (1.0 / d)
        dpp_ref[...] = (dy * pg).astype(BF16)
        dz = (dy * pp * pg * (1.0 - pg)).astype(BF16)
        dz_ref[...] = dz
        dn = _dot_nt(dz, wpg_ref[...])
        dx, dgain = _rms_bwd(xh, r, gain_v, dn)
        dh_ref[...] = dy + dx
        dgain_ref[...] += dgain

    full = lambda a: pl.BlockSpec(a.shape, lambda i: (0,) * a.ndim)
    row = pl.BlockSpec((tm, d), lambda i: (i, 0))
    return pl.pallas_call(
        body, name=name, grid=(t // tm,),
        out_shape=[jax.ShapeDtypeStruct((t, d), F32), jax.ShapeDtypeStruct((t, d), BF16),
                   jax.ShapeDtypeStruct((t, d), BF16), jax.ShapeDtypeStruct((t, d), BF16),
                   jax.ShapeDtypeStruct((1, d), F32), jax.ShapeDtypeStruct((8, LANES), F32)],
        in_specs=[row, full(gain), pl.BlockSpec((tm, pd), lambda i: (i, 0)), row, full(wpg), full(wpe)],
        out_specs=[row, row, row, row, pl.BlockSpec((1, d), lambda i: (0, 0)),
                   pl.BlockSpec((8, LANES), lambda i: (0, 0))],
        compiler_params=_params(("arbitrary",)),
    )(h, gain, p, target, wpg, wpe)


def _head_masks():
    lane = lax.broadcasted_iota(jnp.int32, (1, LANES), 1)
    m0 = (lane < HEAD_DIM).astype(F32)
    return m0, 1.0 - m0


def _head_mean(v, m0, m1):
    del m0, m1
    width = v.shape[-1]
    shift = HEAD_DIM.bit_length() - 1
    r = jnp.right_shift(lax.broadcasted_iota(jnp.int32, (width, width), 0), shift)
    c = jnp.right_shift(lax.broadcasted_iota(jnp.int32, (width, width), 1), shift)
    same_head = (r == c).astype(BF16)
    return _dot(v.astype(BF16), same_head) * (1.0 / HEAD_DIM)


def _head_norm(x, gain, m0, m1):
    r = lax.rsqrt(_head_mean(x * x, m0, m1) + EPS)
    xh = x * r
    return xh * gain, xh, r


def _head_norm_bwd(xh, r, gain, dy, m0, m1):
    gdy = gain * dy
    dx = r * (gdy - xh * _head_mean(xh * gdy, m0, m1))
    return dx, jnp.sum(dy * xh, axis=0, keepdims=True)


GROUP = 4
QW = GROUP * HEAD_DIM
STACK = GROUP * QTILE


def _kv_width(mode):
    return QW if mode == "A" else LANES


def _q_scratch_shape(mode, s_len):
    return (s_len, QW) if mode == "A" else (GROUP * s_len, LANES)


def _group_masks(dtype=F32):
    lane = lax.broadcasted_iota(jnp.int32, (1, QW), 1)
    return [((lane >= h * HEAD_DIM) & (lane < (h + 1) * HEAD_DIM)).astype(dtype) for h in range(GROUP)]


def _stack_heads(first_kv, x, m0, m1):
    out = []
    for half in range(GROUP // 2):
        xh = x[:, half * LANES:(half + 1) * LANES]
        a0, a1 = xh * m0, xh * m1
        r0, r1 = pltpu.roll(a0, HEAD_DIM, 1), pltpu.roll(a1, HEAD_DIM, 1)
        out += [jnp.where(first_kv, a0, r0), jnp.where(first_kv, r1, a1)]
    return out


def _unstack_heads(mode, first_kv, ts, m0, m1):
    if mode == "A":
        masks = _group_masks()
        return sum(t * mk for t, mk in zip(ts, masks))
    halves = []
    for half in range(GROUP // 2):
        t0 = jnp.where(first_kv, ts[2 * half], pltpu.roll(ts[2 * half], HEAD_DIM, 1))
        t1 = jnp.where(first_kv, pltpu.roll(ts[2 * half + 1], HEAD_DIM, 1), ts[2 * half + 1])
        halves.append(t0 * m0 + t1 * m1)
    return jnp.concatenate(halves, axis=1)


def _store_stacked(dst, i, heads):
    for half in range(2):
        rows = slice(half * QTILE, (half + 1) * QTILE)
        for h, x in enumerate(heads):
            dst[pl.ds((2 * i + half) * STACK + h * QTILE, QTILE), :] = x[rows].astype(dst.dtype)


def _load_stacked(mode, ref, m):
    if mode == "B":
        return ref[pl.ds(pl.multiple_of(m * STACK, STACK), STACK), :]
    x = ref[pl.ds(pl.multiple_of(m * QTILE, QTILE), QTILE), :]
    return jnp.concatenate([x * mk for mk in _group_masks(x.dtype)], axis=0)


def _attn_prep(mode, group, s_len, padk, q_ref, k_ref, v_ref, gq_ref, gk_ref, qs, k2, v2, do_ref=None, dos=None):
    m0, m1 = _head_masks()
    zpad = jnp.zeros((padk, k2.shape[1]), BF16)
    k2[pl.ds(0, padk), :] = zpad
    v2[pl.ds(0, padk), :] = zpad
    first_kv = group == 0
    rt = 2 * QTILE
    for i in range(s_len // rt):
        rows = pl.ds(i * rt, rt)
        qn, _, _ = _head_norm(q_ref[rows, :], gq_ref[...], m0, m1)
        kn, _, _ = _head_norm(k_ref[rows, :], gk_ref[...], m0, m1)
        qn = qn * (HEAD_DIM ** -0.5)
        if mode == "A":
            qs[rows, :] = qn.astype(BF16)
            if dos is not None:
                dos[rows, :] = do_ref[rows, :].astype(BF16)
        else:
            _store_stacked(qs, i, _stack_heads(first_kv, qn, m0, m1))
            if dos is not None:
                _store_stacked(dos, i, _stack_heads(first_kv, do_ref[rows, :], m0, m1))
        k2[pl.ds(padk + i * rt, rt), :] = kn.astype(BF16)
        v2[pl.ds(padk + i * rt, rt), :] = v_ref[rows, :].astype(BF16)


def _softmax_terms(mode, s, sink):
    mx = jnp.max(s, axis=-1, keepdims=True)
    if mode == "B":
        mx = jnp.maximum(mx, sink)
    e = jnp.exp(s - mx)
    l = jnp.sum(e, axis=-1, keepdims=True)
    if mode == "B":
        l = l + jnp.exp(sink - mx)
    return e, mx, l


def _sink_column(sink_ref, group):
    row = lax.broadcasted_iota(jnp.int32, (STACK, 1), 0)
    col = jnp.zeros((STACK, 1), F32)
    for h in range(GROUP):
        col = jnp.where((row >= h * QTILE) & (row < (h + 1) * QTILE), sink_ref[GROUP * group + h], col)
    return col


def _head_deltas(dd, m0, m1):
    cols = []
    for half in range(GROUP // 2):
        dh = dd[:, half * LANES:(half + 1) * LANES]
        cols += [jnp.sum(dh * m0, axis=-1, keepdims=True), jnp.sum(dh * m1, axis=-1, keepdims=True)]
    return jnp.concatenate(cols, axis=0)


def _attn_cols(mode):
    if mode == "A":
        return (lambda b, g: (b, g)), (lambda b, g: (b, 2 + g)), (lambda b, g: (b, 4 + g))
    return (lambda b, g: (b, 6 + g)), (lambda b, g: (b, 16)), (lambda b, g: (b, 17))


def _attn_fwd(mode, qkv, gq, gk, bias, sinks, bl, s_len, name):
    bw = bias.shape[-1]
    padk = bw - QTILE
    nt = s_len // QTILE
    qmap, kmap, vmap = _attn_cols(mode)

    kw = _kv_width(mode)

    def body(q_ref, k_ref, v_ref, gq_ref, gk_ref, bias_ref, sink_ref, o_ref, qs, k2, v2, s_buf, *rest):
        o_buf = rest[0] if rest else None
        group = pl.program_id(1)
        m0, m1 = _head_masks()
        first_kv = group == 0
        _attn_prep(mode, group, s_len, padk, q_ref, k_ref, v_ref, gq_ref, gk_ref, qs, k2, v2)
        col = lax.broadcasted_iota(jnp.int32, (STACK, bw), 1)
        sink = _sink_column(sink_ref, group)

        def scores(m, slot):
            r0 = pl.multiple_of(m * QTILE, QTILE)
            s = _dot_nt(_load_stacked(mode, qs, m), k2[pl.ds(r0, bw), :]) + bias_ref[...]
            s_buf[slot] = jnp.where(col >= (padk - r0), s, NEG_INF)

        def finish_tile(m, slot):
            r0 = pl.multiple_of(m * QTILE, QTILE)
            e, _, l = _softmax_terms(mode, s_buf[slot], sink)
            if mode == "A":
                o_st = _dot(e.astype(BF16), v2[pl.ds(r0, bw), :]) / l
                heads = [o_st[h * QTILE:(h + 1) * QTILE] for h in range(GROUP)]
                o_ref[pl.ds(r0, QTILE), :] = _unstack_heads(mode, first_kv, heads, m0, m1)
            else:
                o_buf[pl.ds(pl.multiple_of(m * STACK, STACK), STACK), :] = _dot((e * (1.0 / l)).astype(BF16),
                                                                                 v2[pl.ds(r0, bw), :])

        scores(0, 0)

        def pair(j, carry):
            scores(2 * j + 1, 1)
            finish_tile(2 * j, 0)
            scores(jnp.minimum(2 * j + 2, nt - 1), 0)
            finish_tile(2 * j + 1, 1)
            return carry

        lax.fori_loop(0, nt // 2, pair, 0)
        if mode == "B":
            for m in range(nt):
                heads = [o_buf[pl.ds(m * STACK + h * QTILE, QTILE), :] for h in range(GROUP)]
                o_ref[pl.ds(m * QTILE, QTILE), :] = _unstack_heads(mode, first_kv, heads, m0, m1)

    blk = lambda w, f: pl.BlockSpec((s_len, w), f)
    return pl.pallas_call(
        body, name=name, grid=(bl, B_Q_HEADS // GROUP),
        out_shape=jax.ShapeDtypeStruct((bl * s_len, B_Q_HEADS * HEAD_DIM), F32),
        in_specs=[blk(QW, qmap), blk(kw, kmap), blk(kw, vmap),
                  pl.BlockSpec((1, QW), lambda b, g: (0, 0)), pl.BlockSpec((1, kw), lambda b, g: (0, 0)),
                  pl.BlockSpec((STACK, bw), lambda b, g: (g, 0)),
                  pl.BlockSpec(memory_space=pltpu.SMEM)],
        out_specs=blk(QW, lambda b, g: (b, g)),
        scratch_shapes=[pltpu.VMEM(_q_scratch_shape(mode, s_len), BF16)] + [pltpu.VMEM((s_len + padk, kw), BF16)] * 2
        + [pltpu.VMEM((2, STACK, bw), F32)] + ([pltpu.VMEM((GROUP * s_len, LANES), F32)] if mode == "B" else []),
        compiler_params=_params(("arbitrary", "arbitrary")),
    )(qkv, qkv, qkv, gq, gk, bias.reshape(-1, bw), sinks)


def _attn_bwd(mode, qkv, gq, gk, bias, sinks, y, dy, bl, s_len, name):
    bw = bias.shape[-1]
    padk = bw - QTILE
    nt = s_len // QTILE
    qmap, kmap, vmap = _attn_cols(mode)
    t = bl * s_len
    kw = _kv_width(mode)
    kvw = 4 * LANES if mode == "A" else LANES
    dp_ahead = True

    def body(q_ref, k_ref, v_ref, gq_ref, gk_ref, bias_ref, sink_ref, y_ref, dy_ref,
             dq_ref, dk_ref, dv_ref, dgq_ref, dgk_ref, dbias_ref, dsink_ref,
             qs, k2, v2, dos, dqs, dk, dv, s_buf, dp_buf):
        group = pl.program_id(1)
        m0, m1 = _head_masks()
        first_kv = group == 0
        _attn_prep(mode, group, s_len, padk, q_ref, k_ref, v_ref, gq_ref, gk_ref, qs, k2, v2, dy_ref, dos)
        dk[...] = jnp.zeros_like(dk)
        dv[...] = jnp.zeros_like(dv)
        dbias_ref[...] = jnp.zeros_like(dbias_ref)
        col = lax.broadcasted_iota(jnp.int32, (STACK, bw), 1)
        lane8 = lax.broadcasted_iota(jnp.int32, (8, LANES), 1)
        sink = _sink_column(sink_ref, group)

        def ahead(m, slot):
            r0 = pl.multiple_of(m * QTILE, QTILE)
            band = pl.ds(r0, bw)
            s = _dot_nt(_load_stacked(mode, qs, m), k2[band, :]) + bias_ref[...]
            s_buf[slot] = jnp.where(col >= (padk - r0), s, NEG_INF)
            if dp_ahead:
                dp_buf[slot] = _dot_nt(_load_stacked(mode, dos, m), v2[band, :])

        def tile(m, slot, dsink):
            r0 = pl.multiple_of(m * QTILE, QTILE)
            rows = pl.ds(r0, QTILE)
            band = pl.ds(r0, bw)
            q_st = _load_stacked(mode, qs, m)
            do_st = _load_stacked(mode, dos, m)
            delta = _head_deltas(dy_ref[rows, :] * y_ref[rows, :], m0, m1)
            kb = k2[band, :]
            e, mx, l = _softmax_terms(mode, s_buf[slot], sink)
            inv = 1.0 / l
            pn = e * inv
            ds = pn * ((dp_buf[slot] if dp_ahead else _dot_nt(do_st, v2[band, :])) - delta)
            if mode == "A":
                dbias_ref[...] += ds
            else:
                part = jnp.exp(sink - mx) * inv * delta
                for h in range(GROUP):
                    dsink = dsink - jnp.where(lane8 == h, jnp.sum(part[h * QTILE:(h + 1) * QTILE]), 0.0)
            dsb = ds.astype(BF16)
            dv[band, :] += _dot_tn(pn.astype(BF16), do_st)
            dk[band, :] += _dot_tn(dsb, q_st)
            dq_st = _dot(dsb, kb)
            if mode == "A":
                heads = [dq_st[h * QTILE:(h + 1) * QTILE] for h in range(GROUP)]
                dq_ref[rows, :] = _unstack_heads(mode, first_kv, heads, m0, m1)
            else:
                dqs[pl.ds(pl.multiple_of(m * STACK, STACK), STACK), :] = dq_st
            return dsink

        ahead(0, 0)

        def pair(j, dsink):
            ahead(2 * j + 1, 1)
            dsink = tile(2 * j, 0, dsink)
            ahead(jnp.minimum(2 * j + 2, nt - 1), 0)
            return tile(2 * j + 1, 1, dsink)

        dsink = lax.fori_loop(0, nt // 2, pair, jnp.zeros((8, LANES), F32))
        dsink_ref[...] = dsink

        rt = 2 * QTILE
        dgq = jnp.zeros((1, QW), F32)
        dgk = jnp.zeros((1, kw), F32)
        for i in range(s_len // rt):
            rows = pl.ds(i * rt, rt)
            src = pl.ds(padk + i * rt, rt)
            gq_v, gk_v = gq_ref[...], gk_ref[...]
            _, qh, qr = _head_norm(q_ref[rows, :], gq_v, m0, m1)
            _, kh, kr = _head_norm(k_ref[rows, :], gk_v, m0, m1)
            if mode == "A":
                dqn = dq_ref[rows, :] * (HEAD_DIM ** -0.5)
            else:
                dqn = jnp.concatenate(
                    [_unstack_heads(mode, first_kv, [dqs[pl.ds((2 * i + half) * STACK + h * QTILE, QTILE), :]
                                                     for h in range(GROUP)], m0, m1)
                     for half in range(2)], axis=0) * (HEAD_DIM ** -0.5)
            dq_raw, dgq_i = _head_norm_bwd(qh, qr, gq_v, dqn, m0, m1)
            dk_raw, dgk_i = _head_norm_bwd(kh, kr, gk_v, dk[src, :], m0, m1)
            dvn = dv[src, :]
            dq_ref[rows, :] = dq_raw
            if mode == "A":
                dk_ref[rows, :] = dk_raw
                dv_ref[rows, :] = dvn
            else:
                @pl.when(group == 0)
                def _():
                    dk_ref[rows, :] = dk_raw
                    dv_ref[rows, :] = dvn

                @pl.when(group != 0)
                def _():
                    dk_ref[rows, :] += dk_raw
                    dv_ref[rows, :] += dvn
            dgq, dgk = dgq + dgq_i, dgk + dgk_i
        dgq_ref[...] = jnp.broadcast_to(dgq, (8, QW))
        dgk_ref[...] = jnp.broadcast_to(dgk, (8, kw))

    ng = B_Q_HEADS // GROUP
    blk = lambda w, f: pl.BlockSpec((s_len, w), f)
    small = lambda w: pl.BlockSpec((None, None, 8, w), lambda b, g: (b, g, 0, 0))
    own = lambda b, g: (b, g)
    kvmap = own if mode == "A" else (lambda b, g: (b, 0))
    pad_f32 = pltpu.VMEM((s_len + padk, kw), F32)
    pad_bf = pltpu.VMEM((s_len + padk, kw), BF16)
    stack_bf = pltpu.VMEM(_q_scratch_shape(mode, s_len), BF16)
    outs = pl.pallas_call(
        body, name=name, grid=(bl, ng),
        out_shape=[jax.ShapeDtypeStruct((t, ng * QW), F32), jax.ShapeDtypeStruct((t, kvw), F32),
                   jax.ShapeDtypeStruct((t, kvw), F32),
                   jax.ShapeDtypeStruct((bl, ng, 8, QW), F32), jax.ShapeDtypeStruct((bl, ng, 8, kw), F32),
                   jax.ShapeDtypeStruct((bl, ng * STACK, bw), F32), jax.ShapeDtypeStruct((bl, ng, 8, LANES), F32)],
        in_specs=[blk(QW, qmap), blk(kw, kmap), blk(kw, vmap),
                  pl.BlockSpec((1, QW), lambda b, g: (0, 0)), pl.BlockSpec((1, kw), lambda b, g: (0, 0)),
                  pl.BlockSpec((STACK, bw), lambda b, g: (g, 0)),
                  pl.BlockSpec(memory_space=pltpu.SMEM),
                  blk(QW, own), blk(QW, own)],
        out_specs=[blk(QW, own), blk(kw, kvmap), blk(kw, kvmap), small(QW), small(kw),
                   pl.BlockSpec((None, STACK, bw), lambda b, g: (b, g, 0)), small(LANES)],
        scratch_shapes=[stack_bf, pad_bf, pad_bf, stack_bf,
                        pltpu.VMEM((8, LANES) if mode == "A" else _q_scratch_shape(mode, s_len), F32),
                        pad_f32, pad_f32, pltpu.VMEM((2, STACK, bw), F32),
                        pltpu.VMEM((2, STACK, bw) if dp_ahead else (8, LANES), F32)],
        compiler_params=_params(("arbitrary", "arbitrary")),
    )(qkv, qkv, qkv, gq, gk, bias.reshape(-1, bw), sinks, y, dy)
    outs = list(outs)
    outs[5] = outs[5].reshape(bl, B_Q_HEADS, QTILE, bw)
    return outs


def _band_geometry(prev):
    bw = QTILE + prev * CHUNK
    i = np.arange(QTILE)[:, None]
    j = np.arange(bw)[None, :]
    dist = i + prev * CHUNK - j
    valid = (j // CHUNK >= i // CHUNK) & (j // CHUNK <= i // CHUNK + prev)
    return dist, valid


A_VAR0 = (A_PREV * CHUNK - A_MAX_REL) // LANES * LANES


A_NVAR = QTILE + A_PREV * CHUNK - A_VAR0


def _skew_rows(x, sign):
    rows, n = x.shape
    row = lax.broadcasted_iota(jnp.int32, x.shape, 0)
    b = 1
    while b < rows:
        x = jnp.where((row & b) != 0, pltpu.roll(x, (sign * b) % n, 1), x)
        b *= 2
    return x


def _rel_bias_expand(table, name):
    _, valid = _band_geometry(A_PREV)
    bw = valid.shape[1]
    valid_f = jnp.asarray(valid.astype(np.float32))
    rev = jnp.flip(table[:, 1:], axis=1).reshape(A_HEADS, 1, A_NVAR)

    def body(rev_ref, valid_ref, o_ref):
        rowv = jnp.broadcast_to(rev_ref[...], (QTILE, A_NVAR))
        top = rowv[:, 0:1]
        var = _skew_rows(rowv, 1)
        row = lax.broadcasted_iota(jnp.int32, (QTILE, A_NVAR), 0)
        colv = lax.broadcasted_iota(jnp.int32, (QTILE, A_NVAR), 1)
        var = jnp.where(colv < row, top, var)
        ok = valid_ref[...] > 0.5
        o_ref[:, :A_VAR0] = jnp.where(ok[:, :A_VAR0], top, NEG_INF)
        o_ref[:, A_VAR0:] = jnp.where(ok[:, A_VAR0:], var, NEG_INF)

    return pl.pallas_call(
        body, name=name, grid=(A_HEADS,),
        out_shape=jax.ShapeDtypeStruct((A_HEADS, QTILE, bw), F32),
        in_specs=[pl.BlockSpec((None, 1, A_NVAR), lambda h: (h, 0, 0)), pl.BlockSpec((QTILE, bw), lambda h: (0, 0))],
        out_specs=pl.BlockSpec((None, QTILE, bw), lambda h: (h, 0, 0)),
        compiler_params=_params(("arbitrary",)),
    )(rev, valid_f)


def _rel_bias_grad(dbias, name):
    bl = dbias.shape[0]
    bw = dbias.shape[-1]

    def body(db_ref, o_ref):
        g = db_ref[0]
        for b in range(1, bl):
            g = g + db_ref[b]
        sk = _skew_rows(g[:, A_VAR0:], -1)
        row = lax.broadcasted_iota(jnp.int32, (QTILE, A_NVAR), 0)
        colv = lax.broadcasted_iota(jnp.int32, (QTILE, A_NVAR), 1)
        wrapped = (row + colv) >= A_NVAR
        main = jnp.sum(jnp.where(wrapped, 0.0, sk), axis=0, keepdims=True)
        top = jnp.sum(g[:, :A_VAR0]) + jnp.sum(jnp.where(wrapped, sk, 0.0))
        o_ref[:, :A_NVAR] = jnp.broadcast_to(main, (8, A_NVAR))
        o_ref[:, A_NVAR:] = jnp.full((8, LANES), top, F32)

    out = pl.pallas_call(
        body, name=name, grid=(A_HEADS,),
        out_shape=jax.ShapeDtypeStruct((A_HEADS, 8, A_NVAR + LANES), F32),
        in_specs=[pl.BlockSpec((bl, None, QTILE, bw), lambda h: (0, h, 0, 0))],
        out_specs=pl.BlockSpec((None, 8, A_NVAR + LANES), lambda h: (h, 0, 0)),
        compiler_params=_params(("arbitrary",)),
    )(dbias)
    main, top = out[:, 0, :A_NVAR], out[:, 0, A_NVAR]
    fm = jnp.flip(main, axis=1)
    return jnp.concatenate([jnp.zeros((A_HEADS, 1), F32), fm[:, :-1], fm[:, -1:] + top[:, None]], axis=1)


def _alibi_bias():
    dist, valid = _band_geometry(B_PREV)
    slopes = np.array([2.0 ** (-8.0 * (h + 1) / B_Q_HEADS) for h in range(B_Q_HEADS)], dtype=np.float32)
    bias = -slopes[:, None, None] * np.abs(dist).astype(np.float32)[None]
    return jnp.asarray(np.where(valid[None], bias, np.float32(NEG_INF)).astype(np.float32))


SMALL_NAMES = ("ffn1_norm", "mix_norm", "ffn2_norm", "ple_norm", "a_q_norm", "a_k_norm", "b_q_norm", "b_k_norm",
               "a_rel_bias", "b_sinks", "loss")


def _pack_small(vals):
    rows = []
    for nme in SMALL_NAMES:
        v = vals[nme].astype(F32)
        if nme == "a_rel_bias":
            v = jnp.pad(v.reshape(A_HEADS, -1), ((0, 0), (0, 3 * LANES - (2 * A_MAX_REL + 1))))
        v = v.reshape(-1)
        v = jnp.pad(v, (0, (-v.shape[0]) % LANES))
        rows.append(v.reshape(-1, LANES))
    out = jnp.concatenate(rows, axis=0)
    return jnp.pad(out, ((0, (-out.shape[0]) % 8), (0, 0)))


def _unpack_small(packed, shapes):
    out, r = {}, 0
    for nme in SMALL_NAMES:
        shp = shapes[nme]
        if nme == "a_rel_bias":
            nr = A_HEADS * 3
            out[nme] = packed[r:r + nr].reshape(A_HEADS, 3 * LANES)[:, :2 * A_MAX_REL + 1].reshape(shp)
        else:
            size = int(np.prod(shp)) if shp else 1
            nr = -(-size // LANES)
            out[nme] = packed[r:r + nr].reshape(-1)[:size].reshape(shp)
        r += nr
    return out


BIG_NAMES = ("ffn1_w_gu", "ffn1_w_down", "w_in", "w_gate", "w_proj_a", "w_proj_b", "w_out",
             "ffn2_w_gu", "ffn2_w_down", "w_ple_gate", "w_ple_proj")
WEIGHT_ORDER = ("ffn1_norm", "ffn1_w_gu", "ffn1_w_down", "mix_norm", "w_in", "a_q_norm", "a_k_norm", "a_rel_bias",
                "b_q_norm", "b_k_norm", "b_sinks", "w_gate", "w_proj_a", "w_proj_b", "w_out", "ffn2_norm",
                "ffn2_w_gu", "ffn2_w_down", "ple_norm", "w_ple_gate", "w_ple_proj")


TRANSPOSED = ("ffn1_w_gu", "ffn2_w_gu", "w_in")


def _local(a, nme):
    return a[0].T if nme in TRANSPOSED else a[0]


def _full_cols(wg):
    nb, k, n = wg.shape
    return jnp.transpose(wg, (1, 0, 2)).reshape(k, nb * n)


def _step(x, p, target, w, m, v):
    bl, s_len, d = x.shape
    t = bl * s_len
    h0 = x.reshape(t, d)
    pt = p.reshape(t, p.shape[-1])
    tgt = target.reshape(t, d)

    g_ffn1, g_mix, g_ffn2, g_ple = w["ffn1_norm"], w["mix_norm"], w["ffn2_norm"], w["ple_norm"]
    tiled = lambda a, width: jnp.tile(a.reshape(1, HEAD_DIM), (1, width // HEAD_DIM))
    gqa, gka = tiled(w["a_q_norm"], QW), tiled(w["a_k_norm"], _kv_width("A"))
    gqb, gkb = tiled(w["b_q_norm"], QW), tiled(w["b_k_norm"], _kv_width("B"))
    sinks = w["b_sinks"].reshape(B_Q_HEADS)
    bias_b = _alibi_bias()

    ffn1_names = ("ffn1_w_gu", "ffn1_w_down")
    shard = {nme: _local(w[nme], nme).astype(BF16) for nme in ffn1_names}
    send1, recv1, bufs, token = _gather_start([shard[nme] for nme in ffn1_names], h0, "gather_start_ffn1")
    zero = token[0, 0]
    shard.update({nme: (_local(w[nme], nme) + zero).astype(BF16) for nme in BIG_NAMES if nme not in ffn1_names})
    bias_a = _rel_bias_expand(w["a_rel_bias"][0] + zero, "rel_bias_expand")
    send2, recv2, bufs, token = _gather_pass(send1, recv1, bufs, bias_a, "gather_pass_ffn1")
    wgu1, wd1 = _gather_wait(send2, recv2, bufs, shard["ffn2_w_gu"], "gather_wait_ffn1")
    nf = wgu1.shape[1]
    wd1 = wd1.reshape(N_DEV // 2, nf, d)
    mixer_names = ("w_in", "w_gate")
    rest_names = ("w_proj_a", "w_proj_b", "w_out", "ffn2_w_gu", "ffn2_w_down", "w_ple_gate", "w_ple_proj")
    send1, recv1, bufs, token = _gather_start([shard[nme] for nme in mixer_names], wgu1, "gather_start_mixer")
    rsend1, rrecv1, rest_bufs, token = _gather_start([shard[nme] for nme in rest_names], token, "gather_start_rest")

    h1, gu1 = _ffn_fwd(h0, g_ffn1 + token[0, 0], wgu1, wd1, "ffn1_fwd")
    send2, recv2, bufs, token = _gather_pass(send1, recv1, bufs, h1, "gather_pass_mixer")
    win, wgate = _gather_wait(send2, recv2, bufs, token, "gather_wait_mixer")
    win, wgate = win.reshape(IN_COLS, d), _full_cols(wgate)
    un, qkv, gate = _proj_fwd(h1, g_mix, win, wgate, "proj_fwd")
    ya = _attn_fwd("A", qkv, gqa, gka, bias_a, sinks, bl, s_len, "attn_a_fwd")
    rsend2, rrecv2, rest_bufs, token = _gather_pass(rsend1, rrecv1, rest_bufs, ya, "gather_pass_rest")
    yb = _attn_fwd("B", qkv, gqb + token[0, 0], gkb, bias_b, sinks, bl, s_len, "attn_b_fwd")
    gathered = dict(zip(rest_names, _gather_wait(rsend2, rrecv2, rest_bufs, yb, "gather_wait_rest")))
    wgu2 = gathered["ffn2_w_gu"]
    wd2 = gathered["ffn2_w_down"].reshape(N_DEV // 2, nf, d)
    wpa = _full_cols(gathered["w_proj_a"])
    wpb = _full_cols(gathered["w_proj_b"])
    wpe = _full_cols(gathered["w_ple_proj"])
    wout = gathered["w_out"].reshape(d, d)
    wpg = gathered["w_ple_gate"].reshape(d, d)
    h2, merged, pa, pb = _merge_fwd(h1, ya, yb, gate, wpa, wpb, wout, "merge_fwd")
    h3, gu2 = _ffn_fwd(h2, g_ffn2, wgu2, wd2, "ffn2_fwd")
    dh3, dz4, dpp, n4, dg_ple, loss_part = _ple_loss(h3, g_ple, pt, tgt, wpg, wpe, "ple_loss")

    xi, yi, ci = _place()
    me = jnp.stack([4 * xi + 2 * yi + ci, 2 * xi + yi]).astype(jnp.int32)
    g32, g16, big, pairs = {}, {}, {}, {}

    def keep(nme, pair, rows=None):
        for store, g in zip((g32, g16), pair):
            store[nme] = g if rows is None else g.reshape(N_DEV, rows, d)

    def start(names, after, tag):
        send, recv, parts, lands, token = _scatter_start([g16[nme] for nme in names], after, "grads_start_" + tag)
        return names, send, recv, parts, lands, token

    def start_two_level(names, after, tag):
        views = [g16[nme].reshape((4, 2) + g16[nme].shape[1:]) for nme in names]
        for nme, got in zip(names, _pair_exchange(views, "grads_pair_" + tag)):
            pairs[nme] = got.reshape((4,) + got.shape[2:])
        sums = [_pair_sum(g32[nme], pairs[nme], me, "pair_sum_" + nme) for nme in names]
        send, recv, parts, lands, token = _scatter_start(sums, after, "grads_start_" + tag, SAME_CORE_CHIPS)
        return names, send, recv, parts, lands, token

    def finish(state, after, tag):
        names, send, recv, parts, lands, _ = state
        relations = SAME_CORE_CHIPS if names[0] in pairs else ALL_PEERS
        lands = _scatter_wait(send, recv, parts, lands, after, "grads_wait_" + tag, relations)
        return names, lands

    def adam(done, dep):
        for nme, land in zip(*done):
            outs = _final_adam(g32[nme], land, _local(w[nme], nme), _local(m[nme], nme), _local(v[nme], nme), me, dep,
                               "adam_" + nme, pairs.get(nme))
            big[nme] = [(o.T if nme in TRANSPOSED else o)[None] for o in outs]

    keep("w_ple_gate", _dw(n4, dz4, 1, d, "dw_ple_gate"), d // N_DEV)
    keep("w_ple_proj", _dw(pt, dpp, N_DEV, d // N_DEV, "dw_ple_proj"))
    early = [(start(("w_ple_gate", "w_ple_proj"), dh3, "ple"), "ple")]

    dh2, dgu2, a2, n3, dg_ffn2 = _ffn_bwd(dh3, h2, g_ffn2 + early[-1][0][-1][0, 0], gu2, wgu2, wd2, "ffn2_bwd")
    keep("ffn2_w_down", _dw(a2, dh3, N_DEV // 2, d, "dw_ffn2_down", 0.5), nf // 2)
    early.append((start(("ffn2_w_down",), dh2, "ffn2_down"), "ffn2_down"))
    keep("ffn2_w_gu", _dw(dgu2, n3, N_DEV, d, "dw_ffn2_gu", dep=early[-1][0][-1]))
    flight = start(("ffn2_w_gu",), dh2, "ffn2")

    dpa, dpb, dzg, dya, dyb = _merge_bwd(dh2, pa, pb, gate, wpa, wpb, wout, "merge_bwd")
    keep("w_out", _dw(merged, dh2, 1, d, "dw_out"), d // N_DEV)
    keep("w_proj_a", _dw(ya, dpa, N_DEV, d // N_DEV, "dw_proj_a"))
    keep("w_proj_b", _dw(yb, dpb, N_DEV, d // N_DEV, "dw_proj_b"))
    keep("w_gate", _dw(un, dzg, N_DEV, 2 * d // N_DEV, "dw_gate"))

    tok = flight[-1][0, 0]
    dqa, dka, dva, dgqa, dgka, dbias, _ = _attn_bwd("A", qkv, gqa + tok, gka, bias_a, sinks, ya, dya, bl, s_len,
                                                     "attn_a_bwd")
    dqb, dkb, dvb, dgqb, dgkb, _, dsink = _attn_bwd("B", qkv, gqb, gkb, bias_b, sinks, yb, dyb, bl, s_len, "attn_b_bwd")
    dqkv = [dqa, dka, dva, dqb, dkb, dvb]
    dtab = _rel_bias_grad(dbias, "rel_bias_grad")

    dh1, dg_mix = _proj_bwd(dh2, h1, g_mix, dzg, dqkv, win, wgate, "proj_bwd")
    keep("w_in", _dw_rows(dqkv, un, "dw_in"), IN_COLS // N_DEV)
    waiting = [finish(state, g32["w_in"], tag) for state, tag in early]
    done = finish(flight, waiting[-1][1][0], "ffn2")
    flight = start(("w_out", "w_proj_a", "w_proj_b", "w_gate", "w_in"), done[1][0], "mixer")
    waiting.append(done)

    dh0, dgu1, a1, n1, dg_ffn1 = _ffn_bwd(dh1, h0, g_ffn1 + flight[-1][0, 0], gu1, wgu1, wd1, "ffn1_bwd")
    keep("ffn1_w_down", _dw(a1, dh1, N_DEV // 2, d, "dw_ffn1_down", 0.5), nf // 2)
    done = finish(flight, g32["ffn1_w_down"], "mixer")
    flight = start(("ffn1_w_down",), done[1][0], "ffn1_down")
    waiting.append(done)

    keep("ffn1_w_gu", _dw(dgu1, n1, N_DEV, d, "dw_ffn1_gu", dep=flight[-1]))
    done = finish(flight, g32["ffn1_w_gu"], "ffn1_down")
    flight = start_two_level(("ffn1_w_gu",), done[1][0], "ffn1_gu")
    for group in waiting + [done]:
        adam(group, flight[-1])
    behind = 0.0 * big["ffn1_w_down"][0][0, 0, :1]
    smalls = (dg_ffn1, dg_mix, dg_ffn2, dg_ple + behind, dgqa, dgka, dgqb, dgkb, dtab, dsink)
    return dh0, loss_part, big, smalls, flight, finish, adam


def kernel(x, p, ffn1_norm, ffn1_w_gu, ffn1_w_down, mix_norm, w_in, a_q_norm, a_k_norm, a_rel_bias, b_q_norm, b_k_norm, b_sinks, w_gate, w_proj_a, w_proj_b, w_out, ffn2_norm, ffn2_w_gu, ffn2_w_down, ple_norm, w_ple_gate, w_ple_proj, loss_target, m_ffn1_norm, m_ffn1_w_gu, m_ffn1_w_down, m_mix_norm, m_w_in, m_a_q_norm, m_a_k_norm, m_a_rel_bias, m_b_q_norm, m_b_k_norm, m_b_sinks, m_w_gate, m_w_proj_a, m_w_proj_b, m_w_out, m_ffn2_norm, m_ffn2_w_gu, m_ffn2_w_down, m_ple_norm, m_w_ple_gate, m_w_ple_proj, v_ffn1_norm, v_ffn1_w_gu, v_ffn1_w_down, v_mix_norm, v_w_in, v_a_q_norm, v_a_k_norm, v_a_rel_bias, v_b_q_norm, v_b_k_norm, v_b_sinks, v_w_gate, v_w_proj_a, v_w_proj_b, v_w_out, v_ffn2_norm, v_ffn2_w_gu, v_ffn2_w_down, v_ple_norm, v_w_ple_gate, v_w_ple_proj):
    w = dict(ffn1_norm=ffn1_norm, ffn1_w_gu=ffn1_w_gu, ffn1_w_down=ffn1_w_down, mix_norm=mix_norm, w_in=w_in,
             a_q_norm=a_q_norm, a_k_norm=a_k_norm, a_rel_bias=a_rel_bias, b_q_norm=b_q_norm, b_k_norm=b_k_norm,
             b_sinks=b_sinks, w_gate=w_gate, w_proj_a=w_proj_a, w_proj_b=w_proj_b, w_out=w_out, ffn2_norm=ffn2_norm,
             ffn2_w_gu=ffn2_w_gu, ffn2_w_down=ffn2_w_down, ple_norm=ple_norm, w_ple_gate=w_ple_gate,
             w_ple_proj=w_ple_proj)
    m = dict(ffn1_norm=m_ffn1_norm, ffn1_w_gu=m_ffn1_w_gu, ffn1_w_down=m_ffn1_w_down, mix_norm=m_mix_norm,
             w_in=m_w_in, a_q_norm=m_a_q_norm, a_k_norm=m_a_k_norm, a_rel_bias=m_a_rel_bias, b_q_norm=m_b_q_norm,
             b_k_norm=m_b_k_norm, b_sinks=m_b_sinks, w_gate=m_w_gate, w_proj_a=m_w_proj_a, w_proj_b=m_w_proj_b,
             w_out=m_w_out, ffn2_norm=m_ffn2_norm, ffn2_w_gu=m_ffn2_w_gu, ffn2_w_down=m_ffn2_w_down,
             ple_norm=m_ple_norm, w_ple_gate=m_w_ple_gate, w_ple_proj=m_w_ple_proj)
    v = dict(ffn1_norm=v_ffn1_norm, ffn1_w_gu=v_ffn1_w_gu, ffn1_w_down=v_ffn1_w_down, mix_norm=v_mix_norm,
             w_in=v_w_in, a_q_norm=v_a_q_norm, a_k_norm=v_a_k_norm, a_rel_bias=v_a_rel_bias, b_q_norm=v_b_q_norm,
             b_k_norm=v_b_k_norm, b_sinks=v_b_sinks, w_gate=v_w_gate, w_proj_a=v_w_proj_a, w_proj_b=v_w_proj_b,
             w_out=v_w_out, ffn2_norm=v_ffn2_norm, ffn2_w_gu=v_ffn2_w_gu, ffn2_w_down=v_ffn2_w_down,
             ple_norm=v_ple_norm, w_ple_gate=v_w_ple_gate, w_ple_proj=v_w_ple_proj)
    bl, s_len, d = x.shape

    dh0, loss_part, big, smalls, flight, finish, adam = _step(x, p[0], loss_target, w, m, v)
    dg_ffn1, dg_mix, dg_ffn2, dg_ple, dgqa, dgka, dgqb, dgkb, dtab, dsink = smalls

    fold = lambda a: a[:, :, 0, :].reshape(-1, HEAD_DIM).sum(axis=0)
    small_part = dict(
        ffn1_norm=dg_ffn1, mix_norm=dg_mix, ffn2_norm=dg_ffn2, ple_norm=dg_ple,
        a_q_norm=fold(dgqa), a_k_norm=fold(dgka), b_q_norm=fold(dgqb), b_k_norm=fold(dgkb),
        a_rel_bias=dtab,
        b_sinks=dsink.sum(axis=0)[:, 0, :GROUP].reshape(B_Q_HEADS),
        loss=loss_part[0, :1])
    zero1 = jnp.zeros((1,), F32)
    shapes = {nme: w[nme].shape for nme in SMALL_NAMES if nme != "loss"}
    shapes["loss"] = ()
    pk = lambda src: _pack_small({**{nme: src[nme] for nme in SMALL_NAMES if nme != "loss"}, "loss": zero1})
    sg, sd, sm, sv = _small_allreduce_adam(_pack_small(small_part), pk(w), pk(m), pk(v), "small_allreduce_adam")
    adam(finish(flight, sg, "ffn1_gu"), sg)
    sg, sd, sm, sv = (_unpack_small(a, shapes) for a in (sg, sd, sm, sv))

    def pick(i):
        out = []
        for nme in WEIGHT_ORDER:
            out.append(big[nme][i] if nme in big else (sg, sd, sm, sv)[i][nme])
        return out

    return (sg["loss"], dh0.reshape(bl, s_len, d), *pick(0), *pick(1), *pick(2), *pick(3))
```

```python
import jax
import jax.numpy as jnp
import numpy as np
from jax import lax
from jax.experimental import pallas as pl
from jax.experimental.pallas import tpu as pltpu

F32 = jnp.float32
BF16 = jnp.bfloat16

CHUNK = 64
HEAD_DIM = 64
A_HEADS = 8
A_PREV = 8
A_MAX_REL = 128
B_Q_HEADS = 8
B_KV_HEADS = 2
B_PREV = 2
A_WIDTH = A_HEADS * HEAD_DIM
B_Q_WIDTH = B_Q_HEADS * HEAD_DIM
B_KV_WIDTH = B_KV_HEADS * HEAD_DIM
IN_COLS = 3 * A_WIDTH + B_Q_WIDTH + 2 * B_KV_WIDTH
EPS = 1e-6
NEG_INF = -1e30
ADAM_LR = 0.001
ADAM_B1 = 0.9
ADAM_B2 = 0.999
ADAM_EPS = 1e-08
ADAM_WD = 0.01
ADAM_STEP = 10

N_DEV = 8
LANES = 128
QTILE = 2 * CHUNK
VMEM_LIMIT = 56 * 1024 * 1024
DW_VMEM_LIMIT = 60 * 1024 * 1024
ADAM_TILE_ELEMS = 256 * 1024

MESH_ID = pl.DeviceIdType.MESH
ANY = pl.BlockSpec(memory_space=pl.ANY)
HBM = pl.BlockSpec(memory_space=pltpu.HBM)
SEM = pl.BlockSpec(memory_space=pltpu.SEMAPHORE)
SIDE_EFFECT = pltpu.SideEffectType.DATAFLOW_SIDE_EFFECTING


def _dot(a, b):
    return jnp.dot(a, b, preferred_element_type=F32)


def _dot_nt(a, b):
    return lax.dot_general(a, b, (((1,), (1,)), ((), ())), preferred_element_type=F32)


def _dot_tn(a, b):
    return lax.dot_general(a, b, (((0,), (0,)), ((), ())), preferred_element_type=F32)


def _params(sem=None, vmem=VMEM_LIMIT):
    return pltpu.CompilerParams(dimension_semantics=sem, vmem_limit_bytes=vmem)


def _row_tile(t, want):
    while t % want:
        want //= 2
    return want


def _place():
    return lax.axis_index("x"), lax.axis_index("y"), lax.axis_index("c")


def _gather_level(bufs, send_sems, recv_sems, level, shards=None):
    x, y, c = _place()
    me, sib = (x, y, c), (x, y, 1 - c)
    chips = [(1 - x, y), (x, 1 - y), (1 - x, 1 - y)]

    def copy(w, k, block, to):
        px, py, pc = block
        rows = bufs[w].at[4 * px + 2 * py + pc]
        src = shards[w] if shards is not None and block is me else rows
        return pltpu.make_async_remote_copy(src_ref=src, dst_ref=rows, send_sem=send_sems.at[k], recv_sem=recv_sems.at[k],
                                            device_id=to, device_id_type=MESH_ID)

    n = len(bufs)
    own = []
    if level == 1:
        own = [pltpu.make_async_copy(bufs[w].at[4 * x + 2 * y + c] if shards is None else shards[w],
                                     bufs[w].at[4 * x + 2 * y + c], send_sems.at[4 * n + w]) for w in range(n)]
    out, arriving = [], []
    for w in range(len(bufs)):
        if level == 1:
            out.append(copy(w, 4 * w, me, sib))
            arriving.append(copy(w, 4 * w, sib, me))
        for j, chip in enumerate(chips):
            if level == 1:
                out.append(copy(w, 4 * w + 1 + j, me, (*chip, c)))
                arriving.append(copy(w, 4 * w + 1 + j, (*chip, c), me))
            else:
                out.append(copy(w, 3 * w + j, (*chip, c), sib))
                arriving.append(copy(w, 3 * w + j, (*chip, 1 - c), me))
    return out, arriving, own


def _split_call(body, name, bufs, sems_in, after, n_sems_out, token, extra=()):
    n = len(bufs)
    out_shape = [pltpu.SemaphoreType.DMA((n_sems_out,))] * (2 if n_sems_out else 0)
    out_shape += [pltpu.HBM(a.shape, a.dtype) for a in bufs]
    out_specs = [SEM] * (2 if n_sems_out else 0) + [HBM] * n
    if token:
        out_shape.append(jax.ShapeDtypeStruct((8, LANES), F32))
        out_specs.append(pl.BlockSpec(memory_space=pltpu.VMEM))
    first = 2 if n_sems_out else 0
    return pl.pallas_call(
        body, name=name, out_shape=tuple(out_shape),
        in_specs=[HBM] * (n + len(extra)) + [SEM] * len(sems_in) + [ANY], out_specs=tuple(out_specs),
        input_output_aliases={i: first + i for i in range(n)},
        compiler_params=pltpu.CompilerParams(has_side_effects=SIDE_EFFECT),
    )(*bufs, *extra, *sems_in, after)


def _gather_start(shards, after, name):
    n = len(shards)
    hbm = lambda a: pltpu.with_memory_space_constraint(a, pltpu.HBM)
    bufs = [hbm(lax.empty((N_DEV,) + s.shape, s.dtype)) for s in shards]

    def body(*refs):
        out, _, own = _gather_level(refs[:n], refs[2 * n + 1], refs[2 * n + 2], 1, shards=refs[n:2 * n])
        for cp in own + out:
            cp.start()
        refs[-1][...] = jnp.zeros_like(refs[-1])

    outs = _split_call(body, name, bufs + [hbm(s) for s in shards], [], after, 5 * n, True)
    return outs[0], outs[1], list(outs[2:2 + 2 * n]), outs[-1]


def _gather_pass(send1, recv1, bufs_and_shards, after, name):
    n = len(bufs_and_shards) // 2
    bufs = bufs_and_shards

    def body(*refs):
        refs = refs[:n] + refs[2 * n:]
        out1, in1, own = _gather_level(refs[:n], refs[n], refs[n + 1], 1)
        out2, _, _ = _gather_level(refs[:n], refs[n + 3], refs[n + 4], 2)
        for cp in in1:
            cp.wait_recv()
        for cp in out2:
            cp.start()
        for cp in out1:
            cp.wait_send()
        for cp in own:
            cp.wait()
        refs[-1][...] = jnp.zeros_like(refs[-1])

    outs = _split_call(body, name, bufs, [send1, recv1], after, 3 * n, True)
    return outs[0], outs[1], list(outs[2:2 + n]), outs[-1]


def _gather_wait(send2, recv2, bufs, after, name):
    n = len(bufs)

    def body(*refs):
        out2, in2, _ = _gather_level(refs[:n], refs[n], refs[n + 1], 2)
        for cp in in2:
            cp.wait_recv()
        for cp in out2:
            cp.wait_send()

    return list(_split_call(body, name, bufs, [send2, recv2], after, 0, False))


ALL_PEERS = tuple(range(1, N_DEV))
SAME_CORE_CHIPS = (2, 4, 6)


def _scatter_copies(parts, lands, send_sems, recv_sems, relations):
    x, y, c = _place()
    ns = len(relations)
    cps = []
    for w, (part, land) in enumerate(zip(parts, lands)):
        for i, k in enumerate(relations):
            px, py, pc = x ^ ((k >> 2) & 1), y ^ ((k >> 1) & 1), c ^ (k & 1)
            block = 4 * px + 2 * py + pc if part.shape[0] == N_DEV else 2 * px + py
            cps.append(pltpu.make_async_remote_copy(
                src_ref=part.at[block], dst_ref=land.at[i],
                send_sem=send_sems.at[ns * w + i], recv_sem=recv_sems.at[ns * w + i],
                device_id=(px, py, pc), device_id_type=MESH_ID))
    return cps


def _scatter_start(parts, after, name, relations=ALL_PEERS):
    n = len(parts)
    ns = len(relations)

    def body(*refs):
        ins, lands = refs[:n], refs[n:2 * n]
        send_sems, recv_sems = refs[2 * n + 1], refs[2 * n + 2]
        token = refs[-1]
        for cp in _scatter_copies(ins, lands, send_sems, recv_sems, relations):
            cp.start()
        token[...] = jnp.zeros_like(token)

    land_shapes = [(ns,) + p.shape[1:] for p in parts]
    in_hbm = [pltpu.with_memory_space_constraint(p, pltpu.HBM) for p in parts]
    in_hbm += [pltpu.with_memory_space_constraint(lax.empty(s, p.dtype), pltpu.HBM) for s, p in zip(land_shapes, parts)]
    outs = pl.pallas_call(
        body, name=name,
        out_shape=(pltpu.SemaphoreType.DMA((ns * n,)), pltpu.SemaphoreType.DMA((ns * n,)),
                   *[pltpu.HBM(p.shape, p.dtype) for p in parts],
                   *[pltpu.HBM(s, p.dtype) for s, p in zip(land_shapes, parts)],
                   jax.ShapeDtypeStruct((8, LANES), F32)),
        in_specs=[HBM] * (2 * n) + [ANY],
        out_specs=(SEM, SEM, *[HBM] * (2 * n), pl.BlockSpec(memory_space=pltpu.VMEM)),
        input_output_aliases={i: 2 + i for i in range(2 * n)},
        compiler_params=pltpu.CompilerParams(has_side_effects=SIDE_EFFECT),
    )(*in_hbm, after)
    return outs[0], outs[1], list(outs[2:2 + n]), list(outs[2 + n:2 + 2 * n]), outs[-1]


def _scatter_wait(send_sems, recv_sems, parts, lands, after, name, relations=ALL_PEERS):
    n = len(parts)

    def body(*refs):
        ins, lnd = refs[:n], refs[n:2 * n]
        for cp in _scatter_copies(ins, lnd, refs[2 * n], refs[2 * n + 1], relations):
            cp.wait_send()
            cp.wait_recv()

    outs = pl.pallas_call(
        body, name=name,
        out_shape=tuple(pltpu.HBM(a.shape, a.dtype) for a in parts + lands),
        in_specs=[HBM] * (2 * n) + [SEM, SEM, ANY],
        out_specs=tuple([HBM] * (2 * n)),
        input_output_aliases={i: i for i in range(2 * n)},
        compiler_params=pltpu.CompilerParams(has_side_effects=SIDE_EFFECT),
    )(*parts, *lands, send_sems, recv_sems, after)
    return list(outs[n:])


def _pair_exchange(parts, name):
    n = len(parts)

    def body(*refs):
        ins, outs = refs[:n], refs[n:2 * n]
        send_sems, recv_sems = refs[2 * n:]
        x, y, c = _place()
        cps = [pltpu.make_async_remote_copy(
            src_ref=ins[w].at[:, pl.ds(1 - c, 1)], dst_ref=outs[w], send_sem=send_sems.at[w], recv_sem=recv_sems.at[w],
            device_id=(x, y, 1 - c), device_id_type=MESH_ID) for w in range(n)]
        for cp in cps:
            cp.start()
        for cp in cps:
            cp.wait()

    return pl.pallas_call(
        body, name=name,
        out_shape=[jax.ShapeDtypeStruct((4, 1) + p.shape[2:], p.dtype) for p in parts],
        in_specs=[ANY] * n, out_specs=[ANY] * n,
        scratch_shapes=[pltpu.SemaphoreType.DMA((n,)), pltpu.SemaphoreType.DMA((n,))],
    )(*parts)


def _pair_sum(g8, r1, me, name):
    _, r, c = g8.shape
    tr = max(q for q in range(16, r + 1, 16) if r % q == 0 and q * c <= ADAM_TILE_ELEMS)

    def body(me_ref, g_ref, r_ref, o_ref):
        o_ref[...] = (g_ref[...] + r_ref[...].astype(F32)).astype(BF16)

    chip = lambda k, s: s[1] ^ (k + 1)
    return pl.pallas_call(
        body, name=name,
        out_shape=jax.ShapeDtypeStruct((4, r, c), BF16),
        grid_spec=pltpu.PrefetchScalarGridSpec(
            num_scalar_prefetch=1, grid=(3, r // tr),
            in_specs=[pl.BlockSpec((None, None, tr, c), lambda k, i, s: (chip(k, s), s[0] % 2, i, 0)),
                      pl.BlockSpec((None, tr, c), lambda k, i, s: (chip(k, s), i, 0))],
            out_specs=pl.BlockSpec((None, tr, c), lambda k, i, s: (chip(k, s), i, 0))),
        compiler_params=_params(("arbitrary", "arbitrary")),
    )(me, g8.reshape((4, 2) + g8.shape[1:]), r1)


def _adam(w, g, m, v):
    m2 = ADAM_B1 * m + (1.0 - ADAM_B1) * g
    v2 = ADAM_B2 * v + (1.0 - ADAM_B2) * (g * g)
    m_hat = m2 / (1.0 - ADAM_B1 ** ADAM_STEP)
    v_hat = v2 / (1.0 - ADAM_B2 ** ADAM_STEP)
    delta = -ADAM_LR * (m_hat / (jnp.sqrt(v_hat) + ADAM_EPS) + ADAM_WD * w)
    return delta, m2, v2


def _small_allreduce_adam(part, w, m, v, name):
    rows = part.shape[0]

    def body(p_ref, w_ref, m_ref, v_ref, g_ref, d_ref, mo_ref, vo_ref, buf, send_sems, recv_sems):
        x, y, c = _place()
        buf[0] = p_ref[...]
        cps = []
        for k in range(1, N_DEV):
            kx, ky, kc = (k >> 2) & 1, (k >> 1) & 1, k & 1
            peer = (x ^ kx, y ^ ky, c ^ kc)
            cps.append(pltpu.make_async_remote_copy(
                src_ref=p_ref, dst_ref=buf.at[k], send_sem=send_sems.at[k - 1], recv_sem=recv_sems.at[k - 1],
                device_id=peer, device_id_type=MESH_ID))
        for cp in cps:
            cp.start()
        for cp in cps:
            cp.wait()
        me = 4 * x + 2 * y + c
        total = buf[me]
        for d in range(1, N_DEV):
            total = total + buf[d ^ me]
        g_ref[...] = total
        delta, m2, v2 = _adam(w_ref[...], total, m_ref[...], v_ref[...])
        d_ref[...] = delta
        mo_ref[...] = m2
        vo_ref[...] = v2

    vm = pl.BlockSpec(memory_space=pltpu.VMEM)
    return pl.pallas_call(
        body, name=name,
        out_shape=[jax.ShapeDtypeStruct(part.shape, F32)] * 4,
        in_specs=[vm] * 4, out_specs=[vm] * 4,
        scratch_shapes=[pltpu.VMEM((N_DEV, rows, LANES), F32),
                        pltpu.SemaphoreType.DMA((N_DEV - 1,)), pltpu.SemaphoreType.DMA((N_DEV - 1,))],
    )(part, w, m, v)


def _final_adam(g8, land, w, m, v, me, dep, name, pair=None):
    _, r, c = g8.shape
    tr = max(q for q in range(16, r + 1, 16) if r % q == 0 and q * c <= ADAM_TILE_ELEMS)
    nland = land.shape[0]

    def body(me_ref, g_ref, land_ref, *rest):
        pair_ref = rest[0] if pair is not None else None
        w_ref, m_ref, v_ref, _, go_ref, d_ref, mo_ref, vo_ref = rest[-8:]
        g = g_ref[...]
        if pair_ref is not None:
            g = g + pair_ref[...].astype(F32)
        for k in range(nland):
            g = g + land_ref[k].astype(F32)
        go_ref[...] = g
        delta, m2, v2 = _adam(w_ref[...], g, m_ref[...], v_ref[...])
        d_ref[...] = delta
        mo_ref[...] = m2
        vo_ref[...] = v2

    plain = pl.BlockSpec((tr, c), lambda i, s: (i, 0))
    return pl.pallas_call(
        body, name=name,
        out_shape=[jax.ShapeDtypeStruct((r, c), F32)] * 4,
        grid_spec=pltpu.PrefetchScalarGridSpec(
            num_scalar_prefetch=1, grid=(r // tr,),
            in_specs=[pl.BlockSpec((None, tr, c), lambda i, s: (s[0], i, 0)),
                      pl.BlockSpec((nland, tr, c), lambda i, s: (0, i, 0))]
            + ([] if pair is None else [pl.BlockSpec((None, tr, c), lambda i, s: (s[1], i, 0))])
            + [plain, plain, plain, ANY],
            out_specs=[plain] * 4),
        compiler_params=_params(("arbitrary",)),
    )(*((me, g8, land) + (() if pair is None else (pair,)) + (w, m, v, dep)))


def _rms(x, gain):
    r = lax.rsqrt(jnp.mean(x * x, axis=-1, keepdims=True) + EPS)
    xh = x * r
    return xh * gain, xh, r


def _rms_bwd(xh, r, gain, dy):
    gdy = gain * dy
    dx = r * (gdy - xh * jnp.mean(xh * gdy, axis=-1, keepdims=True))
    return dx, jnp.sum(dy * xh, axis=0, keepdims=True)


def _load_weights(pairs, sems):
    cps = [pltpu.make_async_copy(src, dst, sems.at[i]) for i, (src, dst) in enumerate(pairs)]
    for cp in cps:
        cp.start()
    for cp in cps:
        cp.wait()


def _ffn_fwd(h, gain, wgu, wd, name):
    t, d = h.shape
    nb, nf, _ = wgu.shape
    nh = nb // 2
    tm = _row_tile(t, 512)

    def body(h_ref, g_ref, wgu_hbm, wd_hbm, out_ref, gu_ref, wgu_v, wd_v, sems):
        @pl.when(pl.program_id(0) == 0)
        def _():
            _load_weights([(wgu_hbm, wgu_v), (wd_hbm, wd_v)], sems)

        x = h_ref[...]
        n, _, _ = _rms(x, g_ref[...])
        nbf = n.astype(BF16)
        acc = jnp.zeros((tm, d), F32)
        for j in range(nh):
            g = _dot_nt(nbf, wgu_v[j])
            u = _dot_nt(nbf, wgu_v[j + nh])
            gu_ref[j] = g.astype(BF16)
            gu_ref[j + nh] = u.astype(BF16)
            a = (g * jax.nn.sigmoid(g)) * u
            acc = acc + _dot(a.astype(BF16), wd_v[j])
        out_ref[...] = x + 0.5 * acc

    return pl.pallas_call(
        body, name=name, grid=(t // tm,),
        out_shape=[jax.ShapeDtypeStruct((t, d), F32), jax.ShapeDtypeStruct((nb, t, nf), BF16)],
        in_specs=[pl.BlockSpec((tm, d), lambda i: (i, 0)), pl.BlockSpec((1, d), lambda i: (0, 0)), ANY, ANY],
        out_specs=[pl.BlockSpec((tm, d), lambda i: (i, 0)), pl.BlockSpec((nb, tm, nf), lambda i: (0, i, 0))],
        scratch_shapes=[pltpu.VMEM(wgu.shape, BF16), pltpu.VMEM(wd.shape, BF16), pltpu.SemaphoreType.DMA((2,))],
        compiler_params=_params(("arbitrary",)),
    )(h, gain, wgu, wd)


def _ffn_bwd(dh, h, gain, gu, wgu, wd, name):
    t, d = h.shape
    nb, nf, _ = wgu.shape
    nh = nb // 2
    tm = _row_tile(t, 256)

    def body(dh_ref, h_ref, g_ref, gu_ref, wgu_hbm, wd_hbm, dhp_ref, dgu_ref, a_ref, n_ref, dgain_ref,
             wgu_v, wd_v, sems):
        @pl.when(pl.program_id(0) == 0)
        def _():
            _load_weights([(wgu_hbm, wgu_v), (wd_hbm, wd_v)], sems)
            dgain_ref[...] = jnp.zeros_like(dgain_ref)

        x = h_ref[...]
        gain_v = g_ref[...]
        n, xh, r = _rms(x, gain_v)
        n_ref[...] = n.astype(BF16)
        dh_v = dh_ref[...]
        dfb = (0.5 * dh_v).astype(BF16)
        dn = jnp.zeros((tm, d), F32)
        for j in range(nh):
            da = _dot_nt(dfb, wd_v[j])
            g = gu_ref[j].astype(F32)
            u = gu_ref[j + nh].astype(F32)
            sg = jax.nn.sigmoid(g)
            si = g * sg
            dg = (da * u * (sg * (1.0 + g * (1.0 - sg)))).astype(BF16)
            du = (da * si).astype(BF16)
            a_ref[j] = (si * u).astype(BF16)
            dgu_ref[j] = dg
            dgu_ref[j + nh] = du
            dn = dn + _dot(dg, wgu_v[j]) + _dot(du, wgu_v[j + nh])
        dx, dgain = _rms_bwd(xh, r, gain_v, dn)
        dhp_ref[...] = dh_v + dx
        dgain_ref[...] += dgain

    row = pl.BlockSpec((tm, d), lambda i: (i, 0))
    vec = pl.BlockSpec((1, d), lambda i: (0, 0))
    return pl.pallas_call(
        body, name=name, grid=(t // tm,),
        out_shape=[jax.ShapeDtypeStruct((t, d), F32), jax.ShapeDtypeStruct((nb, t, nf), BF16),
                   jax.ShapeDtypeStruct((nh, t, nf), BF16), jax.ShapeDtypeStruct((t, d), BF16),
                   jax.ShapeDtypeStruct((1, d), F32)],
        in_specs=[row, row, vec, pl.BlockSpec((nb, tm, nf), lambda i: (0, i, 0)), ANY, ANY],
        out_specs=[row, pl.BlockSpec((nb, tm, nf), lambda i: (0, i, 0)),
                   pl.BlockSpec((nh, tm, nf), lambda i: (0, i, 0)), row, vec],
        scratch_shapes=[pltpu.VMEM(wgu.shape, BF16), pltpu.VMEM(wd.shape, BF16), pltpu.SemaphoreType.DMA((2,))],
        compiler_params=_params(("arbitrary",)),
    )(dh, h, gain, gu, wgu, wd)


def _dw(xa, dy, nb, n, name, scale=1.0, dep=None):
    t, k = xa.shape[-2:]
    wide = xa.ndim == 2
    tt = _row_tile(t, 1024)
    steps = t // tt
    x_spec = pl.BlockSpec((tt, k), lambda i: (i, 0)) if wide else pl.BlockSpec((nb, tt, k), lambda i: (0, i, 0))
    dy_spec = pl.BlockSpec((tt, dy.shape[1]), lambda i: (i, 0))
    acc_shape = (k, nb * n) if wide else (nb, k, n)
    stage_shape = (k, nb * n) if wide else (k, n)

    def body(x_ref, dy_ref, *rest):
        o_hbm, ob_hbm, acc, stage, sems = rest[-5:]

        @pl.when(pl.program_id(0) == 0)
        def _():
            acc[...] = jnp.zeros_like(acc)

        dyb = dy_ref[...].astype(BF16)
        if wide:
            acc[...] += _dot(x_ref[...].astype(BF16).T, dyb)
        else:
            for j in range(nb):
                acc[j] += _dot_tn(x_ref[j].astype(BF16), dyb)

        @pl.when(pl.program_id(0) == steps - 1)
        def _():
            if scale != 1.0:
                acc[...] = acc[...] * scale
            if wide:
                cps = [pltpu.make_async_copy(acc.at[:, pl.ds(j * n, n)] if nb > 1 else acc, o_hbm.at[j], sems.at[j])
                       for j in range(nb)]
            else:
                cps = [pltpu.make_async_copy(acc, o_hbm, sems.at[0])]
            for cp in cps:
                cp.start()
            if wide:
                stage[...] = acc[...].astype(BF16)
                bcs = [pltpu.make_async_copy(stage.at[:, pl.ds(j * n, n)] if nb > 1 else stage, ob_hbm.at[j],
                                             sems.at[nb + j]) for j in range(nb)]
                for cp in bcs:
                    cp.start()
                for cp in bcs:
                    cp.wait()
            else:
                for j in range(nb):
                    stage[...] = acc[j].astype(BF16)
                    cp = pltpu.make_async_copy(stage, ob_hbm.at[j], sems.at[nb])
                    cp.start()
                    cp.wait()
            for cp in cps:
                cp.wait()

    return pl.pallas_call(
        body, name=name, grid=(steps,),
        out_shape=[jax.ShapeDtypeStruct((nb, k, n), F32), jax.ShapeDtypeStruct((nb, k, n), BF16)],
        in_specs=[x_spec, dy_spec] + ([] if dep is None else [ANY]),
        out_specs=[ANY, ANY],
        scratch_shapes=[pltpu.VMEM(acc_shape, F32), pltpu.VMEM(stage_shape, BF16),
                        pltpu.SemaphoreType.DMA((2 * nb,))],
        compiler_params=_params(("arbitrary",), DW_VMEM_LIMIT),
    )(*((xa, dy) if dep is None else (xa, dy, dep)))


def _proj_fwd(h, gain, win, wgate, name):
    t, d = h.shape
    tm = _row_tile(t, 512)
    nq, ng = win.shape[0], wgate.shape[1]

    def body(h_ref, g_ref, win_ref, wg_ref, un_ref, qkv_ref, gate_ref):
        n, _, _ = _rms(h_ref[...], g_ref[...])
        nbf = n.astype(BF16)
        un_ref[...] = nbf
        qkv_ref[...] = _dot_nt(nbf, win_ref[...])
        gate_ref[...] = jax.nn.sigmoid(_dot(nbf, wg_ref[...]))

    full = lambda a: pl.BlockSpec(a.shape, lambda i: (0,) * a.ndim)
    return pl.pallas_call(
        body, name=name, grid=(t // tm,),
        out_shape=[jax.ShapeDtypeStruct((t, d), BF16), jax.ShapeDtypeStruct((t, nq), F32),
                   jax.ShapeDtypeStruct((t, ng), F32)],
        in_specs=[pl.BlockSpec((tm, d), lambda i: (i, 0)), full(gain), full(win), full(wgate)],
        out_specs=[pl.BlockSpec((tm, d), lambda i: (i, 0)), pl.BlockSpec((tm, nq), lambda i: (i, 0)),
                   pl.BlockSpec((tm, ng), lambda i: (i, 0))],
        compiler_params=_params(("arbitrary",)),
    )(h, gain, win, wgate)


def _proj_bwd(dh, h, gain, dzg, dqkv_parts, win, wgate, name):
    t, d = h.shape
    tm = _row_tile(t, 512)
    ng = wgate.shape[1]
    np_ = len(dqkv_parts)
    widths = [a.shape[1] for a in dqkv_parts]

    def body(dh_ref, h_ref, g_ref, dzg_ref, *rest):
        part_refs, (win_ref, wg_ref, dhp_ref, dgain_ref) = rest[:np_], rest[np_:]

        @pl.when(pl.program_id(0) == 0)
        def _():
            dgain_ref[...] = jnp.zeros_like(dgain_ref)

        gain_v = g_ref[...]
        _, xh, r = _rms(h_ref[...], gain_v)
        dun = _dot_nt(dzg_ref[...], wg_ref[...])
        off = 0
        for ref, wd in zip(part_refs, widths):
            dun = dun + _dot(ref[...].astype(BF16), win_ref[off:off + wd, :])
            off += wd
        dx, dgain = _rms_bwd(xh, r, gain_v, dun)
        dhp_ref[...] = dh_ref[...] + dx
        dgain_ref[...] += dgain

    full = lambda a: pl.BlockSpec(a.shape, lambda i: (0,) * a.ndim)
    row = pl.BlockSpec((tm, d), lambda i: (i, 0))
    return pl.pallas_call(
        body, name=name, grid=(t // tm,),
        out_shape=[jax.ShapeDtypeStruct((t, d), F32), jax.ShapeDtypeStruct((1, d), F32)],
        in_specs=[row, row, full(gain), pl.BlockSpec((tm, ng), lambda i: (i, 0))]
        + [pl.BlockSpec((tm, wd), lambda i: (i, 0)) for wd in widths] + [full(win), full(wgate)],
        out_specs=[row, pl.BlockSpec((1, d), lambda i: (0, 0))],
        compiler_params=_params(("arbitrary",)),
    )(dh, h, gain, dzg, *dqkv_parts, win, wgate)


def _dw_rows(parts, dy, name):
    t, n = dy.shape
    widths = [a.shape[1] for a in parts]
    k = sum(widths)
    tt = _row_tile(t, 1024)
    steps = t // tt
    np_ = len(parts)

    def body(*refs):
        part_refs, dy_ref = refs[:np_], refs[np_]
        o_hbm, ob_hbm, acc, stage, sems = refs[np_ + 1:]

        @pl.when(pl.program_id(0) == 0)
        def _():
            acc[...] = jnp.zeros_like(acc)

        dyb = dy_ref[...].astype(BF16)
        off = 0
        for ref, wd in zip(part_refs, widths):
            acc[off:off + wd, :] += _dot(ref[...].astype(BF16).T, dyb)
            off += wd

        @pl.when(pl.program_id(0) == steps - 1)
        def _():
            stage[...] = acc[...].astype(BF16)
            cps = [pltpu.make_async_copy(acc, o_hbm.at[0], sems.at[0]),
                   pltpu.make_async_copy(stage, ob_hbm.at[0], sems.at[1])]
            for cp in cps:
                cp.start()
            for cp in cps:
                cp.wait()

    return pl.pallas_call(
        body, name=name, grid=(steps,),
        out_shape=[jax.ShapeDtypeStruct((1, k, n), F32), jax.ShapeDtypeStruct((1, k, n), BF16)],
        in_specs=[pl.BlockSpec((tt, wd), lambda i: (i, 0)) for wd in widths] + [pl.BlockSpec((tt, n), lambda i: (i, 0))],
        out_specs=[ANY, ANY],
        scratch_shapes=[pltpu.VMEM((k, n), F32), pltpu.VMEM((k, n), BF16), pltpu.SemaphoreType.DMA((2,))],
        compiler_params=_params(("arbitrary",)),
    )(*parts, dy)


def _merge_fwd(h, ya, yb, gate, wpa, wpb, wout, name):
    t, d = h.shape
    tm = _row_tile(t, 512)

    def body(h_ref, ya_ref, yb_ref, ga_ref, gb_ref, wpa_ref, wpb_ref, wout_ref, out_ref, mg_ref, pa_ref, pb_ref):
        pa = _dot(ya_ref[...].astype(BF16), wpa_ref[...])
        pb = _dot(yb_ref[...].astype(BF16), wpb_ref[...])
        merged = (ga_ref[...] * pa + gb_ref[...] * pb).astype(BF16)
        pa_ref[...] = pa.astype(BF16)
        pb_ref[...] = pb.astype(BF16)
        mg_ref[...] = merged
        out_ref[...] = h_ref[...] + _dot(merged, wout_ref[...])

    full = lambda a: pl.BlockSpec(a.shape, lambda i: (0,) * a.ndim)
    row = pl.BlockSpec((tm, d), lambda i: (i, 0))
    yrow = pl.BlockSpec((tm, ya.shape[1]), lambda i: (i, 0))
    return pl.pallas_call(
        body, name=name, grid=(t // tm,),
        out_shape=[jax.ShapeDtypeStruct((t, d), F32)] + [jax.ShapeDtypeStruct((t, d), BF16)] * 3,
        in_specs=[row, yrow, yrow, pl.BlockSpec((tm, d), lambda i: (i, 0)), pl.BlockSpec((tm, d), lambda i: (i, 1)),
                  full(wpa), full(wpb), full(wout)],
        out_specs=[row] * 4,
        compiler_params=_params(("arbitrary",)),
    )(h, ya, yb, gate, gate, wpa, wpb, wout)


def _merge_bwd(dh, pa, pb, gate, wpa, wpb, wout, name):
    t, d = dh.shape
    tm = _row_tile(t, 512)
    wy = wpa.shape[0]

    def body(dh_ref, pa_ref, pb_ref, ga_ref, gb_ref, wpa_ref, wpb_ref, wout_ref,
             dpa_ref, dpb_ref, dzg_ref, dya_ref, dyb_ref):
        dm = _dot_nt(dh_ref[...].astype(BF16), wout_ref[...])
        ga, gb = ga_ref[...], gb_ref[...]
        dpa = (dm * ga).astype(BF16)
        dpb = (dm * gb).astype(BF16)
        dpa_ref[...] = dpa
        dpb_ref[...] = dpb
        dzg_ref[:, :d] = (dm * pa_ref[...].astype(F32) * ga * (1.0 - ga)).astype(BF16)
        dzg_ref[:, d:] = (dm * pb_ref[...].astype(F32) * gb * (1.0 - gb)).astype(BF16)
        dya_ref[...] = _dot_nt(dpa, wpa_ref[...])
        dyb_ref[...] = _dot_nt(dpb, wpb_ref[...])

    full = lambda a: pl.BlockSpec(a.shape, lambda i: (0,) * a.ndim)
    row = pl.BlockSpec((tm, d), lambda i: (i, 0))
    yrow = pl.BlockSpec((tm, wy), lambda i: (i, 0))
    return pl.pallas_call(
        body, name=name, grid=(t // tm,),
        out_shape=[jax.ShapeDtypeStruct((t, d), BF16), jax.ShapeDtypeStruct((t, d), BF16),
                   jax.ShapeDtypeStruct((t, 2 * d), BF16), jax.ShapeDtypeStruct((t, wy), F32),
                   jax.ShapeDtypeStruct((t, wy), F32)],
        in_specs=[row, row, row, pl.BlockSpec((tm, d), lambda i: (i, 0)), pl.BlockSpec((tm, d), lambda i: (i, 1)),
                  full(wpa), full(wpb), full(wout)],
        out_specs=[row, row, pl.BlockSpec((tm, 2 * d), lambda i: (i, 0)), yrow, yrow],
        compiler_params=_params(("arbitrary",)),
    )(dh, pa, pb, gate, gate, wpa, wpb, wout)


def _ple_loss(h, gain, p, target, wpg, wpe, name):
    t, d = h.shape
    tm = _row_tile(t, 512)
    pd = p.shape[1]

    def body(h_ref, g_ref, p_ref, t_ref, wpg_ref, wpe_ref, dh_ref, dz_ref, dpp_ref, n_ref, dgain_ref, loss_ref):
        @pl.when(pl.program_id(0) == 0)
        def _():
            dgain_ref[...] = jnp.zeros_like(dgain_ref)
            loss_ref[...] = jnp.zeros_like(loss_ref)

        x = h_ref[...]
        gain_v = g_ref[...]
        n, xh, r = _rms(x, gain_v)
        nbf = n.astype(BF16)
        n_ref[...] = nbf
        pg = jax.nn.sigmoid(_dot(nbf, wpg_ref[...]))
        pp = _dot(p_ref[...].astype(BF16), wpe_ref[...])
        err = (x + pg * pp) - t_ref[...]
        loss_ref[...] += 0.5 * jnp.sum(jnp.mean(err * err, axis=-1, keepdims=True))
        dy = err * (1.0 / d)
        dpp_ref[...] = (dy * pg).astype(BF16)
        dz = (dy * pp * pg * (1.0 - pg)).astype(BF16)
        dz_ref[...] = dz
        dn = _dot_nt(dz, wpg_ref[...])
        dx, dgain = _rms_bwd(xh, r, gain_v, dn)
        dh_ref[...] = dy + dx
        dgain_ref[...] += dgain

    full = lambda a: pl.BlockSpec(a.shape, lambda i: (0,) * a.ndim)
    row = pl.BlockSpec((tm, d), lambda i: (i, 0))
    return pl.pallas_call(
        body, name=name, grid=(t // tm,),
        out_shape=[jax.ShapeDtypeStruct((t, d), F32), jax.ShapeDtypeStruct((t, d), BF16),
                   jax.ShapeDtypeStruct((t, d), BF16), jax.ShapeDtypeStruct((t, d), BF16),
                   jax.ShapeDtypeStruct((1, d), F32), jax.ShapeDtypeStruct((8, LANES), F32)],
        in_specs=[row, full(gain), pl.BlockSpec((tm, pd), lambda i: (i, 0)), row, full(wpg), full(wpe)],
        out_specs=[row, row, row, row, pl.BlockSpec((1, d), lambda i: (0, 0)),
                   pl.BlockSpec((8, LANES), lambda i: (0, 0))],
        compiler_params=_params(("arbitrary",)),
    )(h, gain, p, target, wpg, wpe)


def _head_masks():
    lane = lax.broadcasted_iota(jnp.int32, (1, LANES), 1)
    m0 = (lane < HEAD_DIM).astype(F32)
    return m0, 1.0 - m0


def _head_mean(v, m0, m1):
    del m0, m1
    width = v.shape[-1]
    shift = HEAD_DIM.bit_length() - 1
    r = jnp.right_shift(lax.broadcasted_iota(jnp.int32, (width, width), 0), shift)
    c = jnp.right_shift(lax.broadcasted_iota(jnp.int32, (width, width), 1), shift)
    same_head = (r == c).astype(BF16)
    return _dot(v.astype(BF16), same_head) * (1.0 / HEAD_DIM)


def _head_norm(x, gain, m0, m1):
    r = lax.rsqrt(_head_mean(x * x, m0, m1) + EPS)
    xh = x * r
    return xh * gain, xh, r


def _head_norm_bwd(xh, r, gain, dy, m0, m1):
    gdy = gain * dy
    dx = r * (gdy - xh * _head_mean(xh * gdy, m0, m1))
    return dx, jnp.sum(dy * xh, axis=0, keepdims=True)


GROUP = 4
QW = GROUP * HEAD_DIM
STACK = GROUP * QTILE


def _kv_width(mode):
    return QW if mode == "A" else LANES


def _q_scratch_shape(mode, s_len):
    return (s_len, QW) if mode == "A" else (GROUP * s_len, LANES)


def _group_masks(dtype=F32):
    lane = lax.broadcasted_iota(jnp.int32, (1, QW), 1)
    return [((lane >= h * HEAD_DIM) & (lane < (h + 1) * HEAD_DIM)).astype(dtype) for h in range(GROUP)]


def _stack_heads(first_kv, x, m0, m1):
    out = []
    for half in range(GROUP // 2):
        xh = x[:, half * LANES:(half + 1) * LANES]
        a0, a1 = xh * m0, xh * m1
        r0, r1 = pltpu.roll(a0, HEAD_DIM, 1), pltpu.roll(a1, HEAD_DIM, 1)
        out += [jnp.where(first_kv, a0, r0), jnp.where(first_kv, r1, a1)]
    return out


def _unstack_heads(mode, first_kv, ts, m0, m1):
    if mode == "A":
        masks = _group_masks()
        return sum(t * mk for t, mk in zip(ts, masks))
    halves = []
    for half in range(GROUP // 2):
        t0 = jnp.where(first_kv, ts[2 * half], pltpu.roll(ts[2 * half], HEAD_DIM, 1))
        t1 = jnp.where(first_kv, pltpu.roll(ts[2 * half + 1], HEAD_DIM, 1), ts[2 * half + 1])
        halves.append(t0 * m0 + t1 * m1)
    return jnp.concatenate(halves, axis=1)


def _store_stacked(dst, i, heads):
    for half in range(2):
        rows = slice(half * QTILE, (half + 1) * QTILE)
        for h, x in enumerate(heads):
            dst[pl.ds((2 * i + half) * STACK + h * QTILE, QTILE), :] = x[rows].astype(dst.dtype)


def _load_stacked(mode, ref, m):
    if mode == "B":
        return ref[pl.ds(pl.multiple_of(m * STACK, STACK), STACK), :]
    x = ref[pl.ds(pl.multiple_of(m * QTILE, QTILE), QTILE), :]
    return jnp.concatenate([x * mk for mk in _group_masks(x.dtype)], axis=0)


def _attn_prep(mode, group, s_len, padk, q_ref, k_ref, v_ref, gq_ref, gk_ref, qs, k2, v2, do_ref=None, dos=None):
    m0, m1 = _head_masks()
    zpad = jnp.zeros((padk, k2.shape[1]), BF16)
    k2[pl.ds(0, padk), :] = zpad
    v2[pl.ds(0, padk), :] = zpad
    first_kv = group == 0
    rt = 2 * QTILE
    for i in range(s_len // rt):
        rows = pl.ds(i * rt, rt)
        qn, _, _ = _head_norm(q_ref[rows, :], gq_ref[...], m0, m1)
        kn, _, _ = _head_norm(k_ref[rows, :], gk_ref[...], m0, m1)
        qn = qn * (HEAD_DIM ** -0.5)
        if mode == "A":
            qs[rows, :] = qn.astype(BF16)
            if dos is not None:
                dos[rows, :] = do_ref[rows, :].astype(BF16)
        else:
            _store_stacked(qs, i, _stack_heads(first_kv, qn, m0, m1))
            if dos is not None:
                _store_stacked(dos, i, _stack_heads(first_kv, do_ref[rows, :], m0, m1))
        k2[pl.ds(padk + i * rt, rt), :] = kn.astype(BF16)
        v2[pl.ds(padk + i * rt, rt), :] = v_ref[rows, :].astype(BF16)


def _softmax_terms(mode, s, sink):
    mx = jnp.max(s, axis=-1, keepdims=True)
    if mode == "B":
        mx = jnp.maximum(mx, sink)
    e = jnp.exp(s - mx)
    l = jnp.sum(e, axis=-1, keepdims=True)
    if mode == "B":
        l = l + jnp.exp(sink - mx)
    return e, mx, l


def _sink_column(sink_ref, group):
    row = lax.broadcasted_iota(jnp.int32, (STACK, 1), 0)
    col = jnp.zeros((STACK, 1), F32)
    for h in range(GROUP):
        col = jnp.where((row >= h * QTILE) & (row < (h + 1) * QTILE), sink_ref[GROUP * group + h], col)
    return col


def _head_deltas(dd, m0, m1):
    cols = []
    for half in range(GROUP // 2):
        dh = dd[:, half * LANES:(half + 1) * LANES]
        cols += [jnp.sum(dh * m0, axis=-1, keepdims=True), jnp.sum(dh * m1, axis=-1, keepdims=True)]
    return jnp.concatenate(cols, axis=0)


def _attn_cols(mode):
    if mode == "A":
        return (lambda b, g: (b, g)), (lambda b, g: (b, 2 + g)), (lambda b, g: (b, 4 + g))
    return (lambda b, g: (b, 6 + g)), (lambda b, g: (b, 16)), (lambda b, g: (b, 17))


def _attn_fwd(mode, qkv, gq, gk, bias, sinks, bl, s_len, name):
    bw = bias.shape[-1]
    padk = bw - QTILE
    nt = s_len // QTILE
    qmap, kmap, vmap = _attn_cols(mode)

    kw = _kv_width(mode)

    def body(q_ref, k_ref, v_ref, gq_ref, gk_ref, bias_ref, sink_ref, o_ref, qs, k2, v2, s_buf, *rest):
        o_buf = rest[0] if rest else None
        group = pl.program_id(1)
        m0, m1 = _head_masks()
        first_kv = group == 0
        _attn_prep(mode, group, s_len, padk, q_ref, k_ref, v_ref, gq_ref, gk_ref, qs, k2, v2)
        col = lax.broadcasted_iota(jnp.int32, (STACK, bw), 1)
        sink = _sink_column(sink_ref, group)

        def scores(m, slot):
            r0 = pl.multiple_of(m * QTILE, QTILE)
            s = _dot_nt(_load_stacked(mode, qs, m), k2[pl.ds(r0, bw), :]) + bias_ref[...]
            s_buf[slot] = jnp.where(col >= (padk - r0), s, NEG_INF)

        def finish_tile(m, slot):
            r0 = pl.multiple_of(m * QTILE, QTILE)
            e, _, l = _softmax_terms(mode, s_buf[slot], sink)
            if mode == "A":
                o_st = _dot(e.astype(BF16), v2[pl.ds(r0, bw), :]) / l
                heads = [o_st[h * QTILE:(h + 1) * QTILE] for h in range(GROUP)]
                o_ref[pl.ds(r0, QTILE), :] = _unstack_heads(mode, first_kv, heads, m0, m1)
            else:
                o_buf[pl.ds(pl.multiple_of(m * STACK, STACK), STACK), :] = _dot((e * (1.0 / l)).astype(BF16),
                                                                                 v2[pl.ds(r0, bw), :])

        scores(0, 0)

        def pair(j, carry):
            scores(2 * j + 1, 1)
            finish_tile(2 * j, 0)
            scores(jnp.minimum(2 * j + 2, nt - 1), 0)
            finish_tile(2 * j + 1, 1)
            return carry

        lax.fori_loop(0, nt // 2, pair, 0)
        if mode == "B":
            for m in range(nt):
                heads = [o_buf[pl.ds(m * STACK + h * QTILE, QTILE), :] for h in range(GROUP)]
                o_ref[pl.ds(m * QTILE, QTILE), :] = _unstack_heads(mode, first_kv, heads, m0, m1)

    blk = lambda w, f: pl.BlockSpec((s_len, w), f)
    return pl.pallas_call(
        body, name=name, grid=(bl, B_Q_HEADS // GROUP),
        out_shape=jax.ShapeDtypeStruct((bl * s_len, B_Q_HEADS * HEAD_DIM), F32),
        in_specs=[blk(QW, qmap), blk(kw, kmap), blk(kw, vmap),
                  pl.BlockSpec((1, QW), lambda b, g: (0, 0)), pl.BlockSpec((1, kw), lambda b, g: (0, 0)),
                  pl.BlockSpec((STACK, bw), lambda b, g: (g, 0)),
                  pl.BlockSpec(memory_space=pltpu.SMEM)],
        out_specs=blk(QW, lambda b, g: (b, g)),
        scratch_shapes=[pltpu.VMEM(_q_scratch_shape(mode, s_len), BF16)] + [pltpu.VMEM((s_len + padk, kw), BF16)] * 2
        + [pltpu.VMEM((2, STACK, bw), F32)] + ([pltpu.VMEM((GROUP * s_len, LANES), F32)] if mode == "B" else []),
        compiler_params=_params(("arbitrary", "arbitrary")),
    )(qkv, qkv, qkv, gq, gk, bias.reshape(-1, bw), sinks)


def _attn_bwd(mode, qkv, gq, gk, bias, sinks, y, dy, bl, s_len, name):
    bw = bias.shape[-1]
    padk = bw - QTILE
    nt = s_len // QTILE
    qmap, kmap, vmap = _attn_cols(mode)
    t = bl * s_len
    kw = _kv_width(mode)
    kvw = 4 * LANES if mode == "A" else LANES
    dp_ahead = True

    def body(q_ref, k_ref, v_ref, gq_ref, gk_ref, bias_ref, sink_ref, y_ref, dy_ref,
             dq_ref, dk_ref, dv_ref, dgq_ref, dgk_ref, dbias_ref, dsink_ref,
             qs, k2, v2, dos, dqs, dk, dv, s_buf, dp_buf):
        group = pl.program_id(1)
        m0, m1 = _head_masks()
        first_kv = group == 0
        _attn_prep(mode, group, s_len, padk, q_ref, k_ref, v_ref, gq_ref, gk_ref, qs, k2, v2, dy_ref, dos)
        dk[...] = jnp.zeros_like(dk)
        dv[...] = jnp.zeros_like(dv)
        dbias_ref[...] = jnp.zeros_like(dbias_ref)
        col = lax.broadcasted_iota(jnp.int32, (STACK, bw), 1)
        lane8 = lax.broadcasted_iota(jnp.int32, (8, LANES), 1)
        sink = _sink_column(sink_ref, group)

        def ahead(m, slot):
            r0 = pl.multiple_of(m * QTILE, QTILE)
            band = pl.ds(r0, bw)
            s = _dot_nt(_load_stacked(mode, qs, m), k2[band, :]) + bias_ref[...]
            s_buf[slot] = jnp.where(col >= (padk - r0), s, NEG_INF)
            if dp_ahead:
                dp_buf[slot] = _dot_nt(_load_stacked(mode, dos, m), v2[band, :])

        def tile(m, slot, dsink):
            r0 = pl.multiple_of(m * QTILE, QTILE)
            rows = pl.ds(r0, QTILE)
            band = pl.ds(r0, bw)
            q_st = _load_stacked(mode, qs, m)
            do_st = _load_stacked(mode, dos, m)
            delta = _head_deltas(dy_ref[rows, :] * y_ref[rows, :], m0, m1)
            kb = k2[band, :]
            e, mx, l = _softmax_terms(mode, s_buf[slot], sink)
            inv = 1.0 / l
            pn = e * inv
            ds = pn * ((dp_buf[slot] if dp_ahead else _dot_nt(do_st, v2[band, :])) - delta)
            if mode == "A":
                dbias_ref[...] += ds
            else:
                part = jnp.exp(sink - mx) * inv * delta
                for h in range(GROUP):
                    dsink = dsink - jnp.where(lane8 == h, jnp.sum(part[h * QTILE:(h + 1) * QTILE]), 0.0)
            dsb = ds.astype(BF16)
            dv[band, :] += _dot_tn(pn.astype(BF16), do_st)
            dk[band, :] += _dot_tn(dsb, q_st)
            dq_st = _dot(dsb, kb)
            if mode == "A":
                heads = [dq_st[h * QTILE:(h + 1) * QTILE] for h in range(GROUP)]
                dq_ref[rows, :] = _unstack_heads(mode, first_kv, heads, m0, m1)
            else:
                dqs[pl.ds(pl.multiple_of(m * STACK, STACK), STACK), :] = dq_st
            return dsink

        ahead(0, 0)

        def pair(j, dsink):
            ahead(2 * j + 1, 1)
            dsink = tile(2 * j, 0, dsink)
            ahead(jnp.minimum(2 * j + 2, nt - 1), 0)
            return tile(2 * j + 1, 1, dsink)

        dsink = lax.fori_loop(0, nt // 2, pair, jnp.zeros((8, LANES), F32))
        dsink_ref[...] = dsink

        rt = 2 * QTILE
        dgq = jnp.zeros((1, QW), F32)
        dgk = jnp.zeros((1, kw), F32)
        for i in range(s_len // rt):
            rows = pl.ds(i * rt, rt)
            src = pl.ds(padk + i * rt, rt)
            gq_v, gk_v = gq_ref[...], gk_ref[...]
            _, qh, qr = _head_norm(q_ref[rows, :], gq_v, m0, m1)
            _, kh, kr = _head_norm(k_ref[rows, :], gk_v, m0, m1)
            if mode == "A":
                dqn = dq_ref[rows, :] * (HEAD_DIM ** -0.5)
            else:
                dqn = jnp.concatenate(
                    [_unstack_heads(mode, first_kv, [dqs[pl.ds((2 * i + half) * STACK + h * QTILE, QTILE), :]
                                                     for h in range(GROUP)], m0, m1)
                     for half in range(2)], axis=0) * (HEAD_DIM ** -0.5)
            dq_raw, dgq_i = _head_norm_bwd(qh, qr, gq_v, dqn, m0, m1)
            dk_raw, dgk_i = _head_norm_bwd(kh, kr, gk_v, dk[src, :], m0, m1)
            dvn = dv[src, :]
            dq_ref[rows, :] = dq_raw
            if mode == "A":
                dk_ref[rows, :] = dk_raw
                dv_ref[rows, :] = dvn
            else:
                @pl.when(group == 0)
                def _():
                    dk_ref[rows, :] = dk_raw
                    dv_ref[rows, :] = dvn

                @pl.when(group != 0)
                def _():
                    dk_ref[rows, :] += dk_raw
                    dv_ref[rows, :] += dvn
            dgq, dgk = dgq + dgq_i, dgk + dgk_i
        dgq_ref[...] = jnp.broadcast_to(dgq, (8, QW))
        dgk_ref[...] = jnp.broadcast_to(dgk, (8, kw))

    ng = B_Q_HEADS // GROUP
    blk = lambda w, f: pl.BlockSpec((s_len, w), f)
    small = lambda w: pl.BlockSpec((None, None, 8, w), lambda b, g: (b, g, 0, 0))
    own = lambda b, g: (b, g)
    kvmap = own if mode == "A" else (lambda b, g: (b, 0))
    pad_f32 = pltpu.VMEM((s_len + padk, kw), F32)
    pad_bf = pltpu.VMEM((s_len + padk, kw), BF16)
    stack_bf = pltpu.VMEM(_q_scratch_shape(mode, s_len), BF16)
    outs = pl.pallas_call(
        body, name=name, grid=(bl, ng),
        out_shape=[jax.ShapeDtypeStruct((t, ng * QW), F32), jax.ShapeDtypeStruct((t, kvw), F32),
                   jax.ShapeDtypeStruct((t, kvw), F32),
                   jax.ShapeDtypeStruct((bl, ng, 8, QW), F32), jax.ShapeDtypeStruct((bl, ng, 8, kw), F32),
                   jax.ShapeDtypeStruct((bl, ng * STACK, bw), F32), jax.ShapeDtypeStruct((bl, ng, 8, LANES), F32)],
        in_specs=[blk(QW, qmap), blk(kw, kmap), blk(kw, vmap),
                  pl.BlockSpec((1, QW), lambda b, g: (0, 0)), pl.BlockSpec((1, kw), lambda b, g: (0, 0)),
                  pl.BlockSpec((STACK, bw), lambda b, g: (g, 0)),
                  pl.BlockSpec(memory_space=pltpu.SMEM),
                  blk(QW, own), blk(QW, own)],
        out_specs=[blk(QW, own), blk(kw, kvmap), blk(kw, kvmap), small(QW), small(kw),
                   pl.BlockSpec((None, STACK, bw), lambda b, g: (b, g, 0)), small(LANES)],
        scratch_shapes=[stack_bf, pad_bf, pad_bf, stack_bf,
                        pltpu.VMEM((8, LANES) if mode == "A" else _q_scratch_shape(mode, s_len), F32),
                        pad_f32, pad_f32, pltpu.VMEM((2, STACK, bw), F32),
                        pltpu.VMEM((2, STACK, bw) if dp_ahead else (8, LANES), F32)],
        compiler_params=_params(("arbitrary", "arbitrary")),
    )(qkv, qkv, qkv, gq, gk, bias.reshape(-1, bw), sinks, y, dy)
    outs = list(outs)
    outs[5] = outs[5].reshape(bl, B_Q_HEADS, QTILE, bw)
    return outs


def _band_geometry(prev):
    bw = QTILE + prev * CHUNK
    i = np.arange(QTILE)[:, None]
    j = np.arange(bw)[None, :]
    dist = i + prev * CHUNK - j
    valid = (j // CHUNK >= i // CHUNK) & (j // CHUNK <= i // CHUNK + prev)
    return dist, valid


A_VAR0 = (A_PREV * CHUNK - A_MAX_REL) // LANES * LANES


A_NVAR = QTILE + A_PREV * CHUNK - A_VAR0


def _skew_rows(x, sign):
    rows, n = x.shape
    row = lax.broadcasted_iota(jnp.int32, x.shape, 0)
    b = 1
    while b < rows:
        x = jnp.where((row & b) != 0, pltpu.roll(x, (sign * b) % n, 1), x)
        b *= 2
    return x


def _rel_bias_expand(table, name):
    _, valid = _band_geometry(A_PREV)
    bw = valid.shape[1]
    valid_f = jnp.asarray(valid.astype(np.float32))
    rev = jnp.flip(table[:, 1:], axis=1).reshape(A_HEADS, 1, A_NVAR)

    def body(rev_ref, valid_ref, o_ref):
        rowv = jnp.broadcast_to(rev_ref[...], (QTILE, A_NVAR))
        top = rowv[:, 0:1]
        var = _skew_rows(rowv, 1)
        row = lax.broadcasted_iota(jnp.int32, (QTILE, A_NVAR), 0)
        colv = lax.broadcasted_iota(jnp.int32, (QTILE, A_NVAR), 1)
        var = jnp.where(colv < row, top, var)
        ok = valid_ref[...] > 0.5
        o_ref[:, :A_VAR0] = jnp.where(ok[:, :A_VAR0], top, NEG_INF)
        o_ref[:, A_VAR0:] = jnp.where(ok[:, A_VAR0:], var, NEG_INF)

    return pl.pallas_call(
        body, name=name, grid=(A_HEADS,),
        out_shape=jax.ShapeDtypeStruct((A_HEADS, QTILE, bw), F32),
        in_specs=[pl.BlockSpec((None, 1, A_NVAR), lambda h: (h, 0, 0)), pl.BlockSpec((QTILE, bw), lambda h: (0, 0))],
        out_specs=pl.BlockSpec((None, QTILE, bw), lambda h: (h, 0, 0)),
        compiler_params=_params(("arbitrary",)),
    )(rev, valid_f)


def _rel_bias_grad(dbias, name):
    bl = dbias.shape[0]
    bw = dbias.shape[-1]

    def body(db_ref, o_ref):
        g = db_ref[0]
        for b in range(1, bl):
            g = g + db_ref[b]
        sk = _skew_rows(g[:, A_VAR0:], -1)
        row = lax.broadcasted_iota(jnp.int32, (QTILE, A_NVAR), 0)
        colv = lax.broadcasted_iota(jnp.int32, (QTILE, A_NVAR), 1)
        wrapped = (row + colv) >= A_NVAR
        main = jnp.sum(jnp.where(wrapped, 0.0, sk), axis=0, keepdims=True)
        top = jnp.sum(g[:, :A_VAR0]) + jnp.sum(jnp.where(wrapped, sk, 0.0))
        o_ref[:, :A_NVAR] = jnp.broadcast_to(main, (8, A_NVAR))
        o_ref[:, A_NVAR:] = jnp.full((8, LANES), top, F32)

    out = pl.pallas_call(
        body, name=name, grid=(A_HEADS,),
        out_shape=jax.ShapeDtypeStruct((A_HEADS, 8, A_NVAR + LANES), F32),
        in_specs=[pl.BlockSpec((bl, None, QTILE, bw), lambda h: (0, h, 0, 0))],
        out_specs=pl.BlockSpec((None, 8, A_NVAR + LANES), lambda h: (h, 0, 0)),
        compiler_params=_params(("arbitrary",)),
    )(dbias)
    main, top = out[:, 0, :A_NVAR], out[:, 0, A_NVAR]
    fm = jnp.flip(main, axis=1)
    return jnp.concatenate([jnp.zeros((A_HEADS, 1), F32), fm[:, :-1], fm[:, -1:] + top[:, None]], axis=1)


def _alibi_bias():
    dist, valid = _band_geometry(B_PREV)
    slopes = np.array([2.0 ** (-8.0 * (h + 1) / B_Q_HEADS) for h in range(B_Q_HEADS)], dtype=np.float32)
    bias = -slopes[:, None, None] * np.abs(dist).astype(np.float32)[None]
    return jnp.asarray(np.where(valid[None], bias, np.float32(NEG_INF)).astype(np.float32))


SMALL_NAMES = ("ffn1_norm", "mix_norm", "ffn2_norm", "ple_norm", "a_q_norm", "a_k_norm", "b_q_norm", "b_k_norm",
               "a_rel_bias", "b_sinks", "loss")


def _pack_small(vals):
    rows = []
    for nme in SMALL_NAMES:
        v = vals[nme].astype(F32)
        if nme == "a_rel_bias":
            v = jnp.pad(v.reshape(A_HEADS, -1), ((0, 0), (0, 3 * LANES - (2 * A_MAX_REL + 1))))
        v = v.reshape(-1)
        v = jnp.pad(v, (0, (-v.shape[0]) % LANES))
        rows.append(v.reshape(-1, LANES))
    out = jnp.concatenate(rows, axis=0)
    return jnp.pad(out, ((0, (-out.shape[0]) % 8), (0, 0)))


def _unpack_small(packed, shapes):
    out, r = {}, 0
    for nme in SMALL_NAMES:
        shp = shapes[nme]
        if nme == "a_rel_bias":
            nr = A_HEADS * 3
            out[nme] = packed[r:r + nr].reshape(A_HEADS, 3 * LANES)[:, :2 * A_MAX_REL + 1].reshape(shp)
        else:
            size = int(np.prod(shp)) if shp else 1
            nr = -(-size // LANES)
            out[nme] = packed[r:r + nr].reshape(-1)[:size].reshape(shp)
        r += nr
    return out


BIG_NAMES = ("ffn1_w_gu", "ffn1_w_down", "w_in", "w_gate", "w_proj_a", "w_proj_b", "w_out",
             "ffn2_w_gu", "ffn2_w_down", "w_ple_gate", "w_ple_proj")
WEIGHT_ORDER = ("ffn1_norm", "ffn1_w_gu", "ffn1_w_down", "mix_norm", "w_in", "a_q_norm", "a_k_norm", "a_rel_bias",
                "b_q_norm", "b_k_norm", "b_sinks", "w_gate", "w_proj_a", "w_proj_b", "w_out", "ffn2_norm",
                "ffn2_w_gu", "ffn2_w_down", "ple_norm", "w_ple_gate", "w_ple_proj")


TRANSPOSED = ("ffn1_w_gu", "ffn2_w_gu", "w_in")


def _local(a, nme):
    return a[0].T if nme in TRANSPOSED else a[0]


def _full_cols(wg):
    nb, k, n = wg.shape
    return jnp.transpose(wg, (1, 0, 2)).reshape(k, nb * n)


def _step(x, p, target, w, m, v):
    bl, s_len, d = x.shape
    t = bl * s_len
    h0 = x.reshape(t, d)
    pt = p.reshape(t, p.shape[-1])
    tgt = target.reshape(t, d)

    g_ffn1, g_mix, g_ffn2, g_ple = w["ffn1_norm"], w["mix_norm"], w["ffn2_norm"], w["ple_norm"]
    tiled = lambda a, width: jnp.tile(a.reshape(1, HEAD_DIM), (1, width // HEAD_DIM))
    gqa, gka = tiled(w["a_q_norm"], QW), tiled(w["a_k_norm"], _kv_width("A"))
    gqb, gkb = tiled(w["b_q_norm"], QW), tiled(w["b_k_norm"], _kv_width("B"))
    sinks = w["b_sinks"].reshape(B_Q_HEADS)
    bias_b = _alibi_bias()

    ffn1_names = ("ffn1_w_gu", "ffn1_w_down")
    shard = {nme: _local(w[nme], nme).astype(BF16) for nme in ffn1_names}
    send1, recv1, bufs, token = _gather_start([shard[nme] for nme in ffn1_names], h0, "gather_start_ffn1")
    zero = token[0, 0]
    shard.update({nme: (_local(w[nme], nme) + zero).astype(BF16) for nme in BIG_NAMES if nme not in ffn1_names})
    bias_a = _rel_bias_expand(w["a_rel_bias"][0] + zero, "rel_bias_expand")
    send2, recv2, bufs, token = _gather_pass(send1, recv1, bufs, bias_a, "gather_pass_ffn1")
    wgu1, wd1 = _gather_wait(send2, recv2, bufs, shard["ffn2_w_gu"], "gather_wait_ffn1")
    nf = wgu1.shape[1]
    wd1 = wd1.reshape(N_DEV // 2, nf, d)
    mixer_names = ("w_in", "w_gate")
    rest_names = ("w_proj_a", "w_proj_b", "w_out", "ffn2_w_gu", "ffn2_w_down", "w_ple_gate", "w_ple_proj")
    send1, recv1, bufs, token = _gather_start([shard[nme] for nme in mixer_names], wgu1, "gather_start_mixer")
    rsend1, rrecv1, rest_bufs, token = _gather_start([shard[nme] for nme in rest_names], token, "gather_start_rest")

    h1, gu1 = _ffn_fwd(h0, g_ffn1 + token[0, 0], wgu1, wd1, "ffn1_fwd")
    send2, recv2, bufs, token = _gather_pass(send1, recv1, bufs, h1, "gather_pass_mixer")
    win, wgate = _gather_wait(send2, recv2, bufs, token, "gather_wait_mixer")
    win, wgate = win.reshape(IN_COLS, d), _full_cols(wgate)
    un, qkv, gate = _proj_fwd(h1, g_mix, win, wgate, "proj_fwd")
    ya = _attn_fwd("A", qkv, gqa, gka, bias_a, sinks, bl, s_len, "attn_a_fwd")
    rsend2, rrecv2, rest_bufs, token = _gather_pass(rsend1, rrecv1, rest_bufs, ya, "gather_pass_rest")
    yb = _attn_fwd("B", qkv, gqb + token[0, 0], gkb, bias_b, sinks, bl, s_len, "attn_b_fwd")
    gathered = dict(zip(rest_names, _gather_wait(rsend2, rrecv2, rest_bufs, yb, "gather_wait_rest")))
    wgu2 = gathered["ffn2_w_gu"]
    wd2 = gathered["ffn2_w_down"].reshape(N_DEV // 2, nf, d)
    wpa = _full_cols(gathered["w_proj_a"])
    wpb = _full_cols(gathered["w_proj_b"])
    wpe = _full_cols(gathered["w_ple_proj"])
    wout = gathered["w_out"].reshape(d, d)
    wpg = gathered["w_ple_gate"].reshape(d, d)
    h2, merged, pa, pb = _merge_fwd(h1, ya, yb, gate, wpa, wpb, wout, "merge_fwd")
    h3, gu2 = _ffn_fwd(h2, g_ffn2, wgu2, wd2, "ffn2_fwd")
    dh3, dz4, dpp, n4, dg_ple, loss_part = _ple_loss(h3, g_ple, pt, tgt, wpg, wpe, "ple_loss")

    xi, yi, ci = _place()
    me = jnp.stack([4 * xi + 2 * yi + ci, 2 * xi + yi]).astype(jnp.int32)
    g32, g16, big, pairs = {}, {}, {}, {}

    def keep(nme, pair, rows=None):
        for store, g in zip((g32, g16), pair):
            store[nme] = g if rows is None else g.reshape(N_DEV, rows, d)

    def start(names, after, tag):
        send, recv, parts, lands, token = _scatter_start([g16[nme] for nme in names], after, "grads_start_" + tag)
        return names, send, recv, parts, lands, token

    def start_two_level(names, after, tag):
        views = [g16[nme].reshape((4, 2) + g16[nme].shape[1:]) for nme in names]
        for nme, got in zip(names, _pair_exchange(views, "grads_pair_" + tag)):
            pairs[nme] = got.reshape((4,) + got.shape[2:])
        sums = [_pair_sum(g32[nme], pairs[nme], me, "pair_sum_" + nme) for nme in names]
        send, recv, parts, lands, token = _scatter_start(sums, after, "grads_start_" + tag, SAME_CORE_CHIPS)
        return names, send, recv, parts, lands, token

    def finish(state, after, tag):
        names, send, recv, parts, lands, _ = state
        relations = SAME_CORE_CHIPS if names[0] in pairs else ALL_PEERS
        lands = _scatter_wait(send, recv, parts, lands, after, "grads_wait_" + tag, relations)
        return names, lands

    def adam(done, dep):
        for nme, land in zip(*done):
            outs = _final_adam(g32[nme], land, _local(w[nme], nme), _local(m[nme], nme), _local(v[nme], nme), me, dep,
                               "adam_" + nme, pairs.get(nme))
            big[nme] = [(o.T if nme in TRANSPOSED else o)[None] for o in outs]

    keep("w_ple_gate", _dw(n4, dz4, 1, d, "dw_ple_gate"), d // N_DEV)
    keep("w_ple_proj", _dw(pt, dpp, N_DEV, d // N_DEV, "dw_ple_proj"))
    early = [(start(("w_ple_gate", "w_ple_proj"), dh3, "ple"), "ple")]

    dh2, dgu2, a2, n3, dg_ffn2 = _ffn_bwd(dh3, h2, g_ffn2 + early[-1][0][-1][0, 0], gu2, wgu2, wd2, "ffn2_bwd")
    keep("ffn2_w_down", _dw(a2, dh3, N_DEV // 2, d, "dw_ffn2_down", 0.5), nf // 2)
    early.append((start(("ffn2_w_down",), dh2, "ffn2_down"), "ffn2_down"))
    keep("ffn2_w_gu", _dw(dgu2, n3, N_DEV, d, "dw_ffn2_gu", dep=early[-1][0][-1]))
    flight = start(("ffn2_w_gu",), dh2, "ffn2")

    dpa, dpb, dzg, dya, dyb = _merge_bwd(dh2, pa, pb, gate, wpa, wpb, wout, "merge_bwd")
    keep("w_out", _dw(merged, dh2, 1, d, "dw_out"), d // N_DEV)
    keep("w_proj_a", _dw(ya, dpa, N_DEV, d // N_DEV, "dw_proj_a"))
    keep("w_proj_b", _dw(yb, dpb, N_DEV, d // N_DEV, "dw_proj_b"))
    keep("w_gate", _dw(un, dzg, N_DEV, 2 * d // N_DEV, "dw_gate"))

    tok = flight[-1][0, 0]
    dqa, dka, dva, dgqa, dgka, dbias, _ = _attn_bwd("A", qkv, gqa + tok, gka, bias_a, sinks, ya, dya, bl, s_len,
                                                     "attn_a_bwd")
    dqb, dkb, dvb, dgqb, dgkb, _, dsink = _attn_bwd("B", qkv, gqb, gkb, bias_b, sinks, yb, dyb, bl, s_len, "attn_b_bwd")
    dqkv = [dqa, dka, dva, dqb, dkb, dvb]
    dtab = _rel_bias_grad(dbias, "rel_bias_grad")

    dh1, dg_mix = _proj_bwd(dh2, h1, g_mix, dzg, dqkv, win, wgate, "proj_bwd")
    keep("w_in", _dw_rows(dqkv, un, "dw_in"), IN_COLS // N_DEV)
    waiting = [finish(state, g32["w_in"], tag) for state, tag in early]
    done = finish(flight, waiting[-1][1][0], "ffn2")
    flight = start(("w_out", "w_proj_a", "w_proj_b", "w_gate", "w_in"), done[1][0], "mixer")
    waiting.append(done)

    dh0, dgu1, a1, n1, dg_ffn1 = _ffn_bwd(dh1, h0, g_ffn1 + flight[-1][0, 0], gu1, wgu1, wd1, "ffn1_bwd")
    keep("ffn1_w_down", _dw(a1, dh1, N_DEV // 2, d, "dw_ffn1_down", 0.5), nf // 2)
    done = finish(flight, g32["ffn1_w_down"], "mixer")
    flight = start(("ffn1_w_down",), done[1][0], "ffn1_down")
    waiting.append(done)

    keep("ffn1_w_gu", _dw(dgu1, n1, N_DEV, d, "dw_ffn1_gu", dep=flight[-1]))
    done = finish(flight, g32["ffn1_w_gu"], "ffn1_down")
    flight = start_two_level(("ffn1_w_gu",), done[1][0], "ffn1_gu")
    for group in waiting + [done]:
        adam(group, flight[-1])
    behind = 0.0 * big["ffn1_w_down"][0][0, 0, :1]
    smalls = (dg_ffn1, dg_mix, dg_ffn2, dg_ple + behind, dgqa, dgka, dgqb, dgkb, dtab, dsink)
    return dh0, loss_part, big, smalls, flight, finish, adam


def kernel(x, p, ffn1_norm, ffn1_w_gu, ffn1_w_down, mix_norm, w_in, a_q_norm, a_k_norm, a_rel_bias, b_q_norm, b_k_norm, b_sinks, w_gate, w_proj_a, w_proj_b, w_out, ffn2_norm, ffn2_w_gu, ffn2_w_down, ple_norm, w_ple_gate, w_ple_proj, loss_target, m_ffn1_norm, m_ffn1_w_gu, m_ffn1_w_down, m_mix_norm, m_w_in, m_a_q_norm, m_a_k_norm, m_a_rel_bias, m_b_q_norm, m_b_k_norm, m_b_sinks, m_w_gate, m_w_proj_a, m_w_proj_b, m_w_out, m_ffn2_norm, m_ffn2_w_gu, m_ffn2_w_down, m_ple_norm, m_w_ple_gate, m_w_ple_proj, v_ffn1_norm, v_ffn1_w_gu, v_ffn1_w_down, v_mix_norm, v_w_in, v_a_q_norm, v_a_k_norm, v_a_rel_bias, v_b_q_norm, v_b_k_norm, v_b_sinks, v_w_gate, v_w_proj_a, v_w_proj_b, v_w_out, v_ffn2_norm, v_ffn2_w_gu, v_ffn2_w_down, v_ple_norm, v_w_ple_gate, v_w_ple_proj):
    w = dict(ffn1_norm=ffn1_norm, ffn1_w_gu=ffn1_w_gu, ffn1_w_down=ffn1_w_down, mix_norm=mix_norm, w_in=w_in,
             a_q_norm=a_q_norm, a_k_norm=a_k_norm, a_rel_bias=a_rel_bias, b_q_norm=b_q_norm, b_k_norm=b_k_norm,
             b_sinks=b_sinks, w_gate=w_gate, w_proj_a=w_proj_a, w_proj_b=w_proj_b, w_out=w_out, ffn2_norm=ffn2_norm,
             ffn2_w_gu=ffn2_w_gu, ffn2_w_down=ffn2_w_down, ple_norm=ple_norm, w_ple_gate=w_ple_gate,
             w_ple_proj=w_ple_proj)
    m = dict(ffn1_norm=m_ffn1_norm, ffn1_w_gu=m_ffn1_w_gu, ffn1_w_down=m_ffn1_w_down, mix_norm=m_mix_norm,
             w_in=m_w_in, a_q_norm=m_a_q_norm, a_k_norm=m_a_k_norm, a_rel_bias=m_a_rel_bias, b_q_norm=m_b_q_norm,
             b_k_norm=m_b_k_norm, b_sinks=m_b_sinks, w_gate=m_w_gate, w_proj_a=m_w_proj_a, w_proj_b=m_w_proj_b,
             w_out=m_w_out, ffn2_norm=m_ffn2_norm, ffn2_w_gu=m_ffn2_w_gu, ffn2_w_down=m_ffn2_w_down,
             ple_norm=m_ple_norm, w_ple_gate=m_w_ple_gate, w_ple_proj=m_w_ple_proj)
    v = dict(ffn1_norm=v_ffn1_norm, ffn1_w_gu=v_ffn1_w_gu, ffn1_w_down=v_ffn1_w_down, mix_norm=v_mix_norm,
             w_in=v_w_in, a_q_norm=v_a_q_norm, a_k_norm=v_a_k_norm, a_rel_bias=v_a_rel_bias, b_q_norm=v_b_q_norm,
             b_k_norm=v_b_k_norm, b_sinks=v_b_sinks, w_gate=v_w_gate, w_proj_a=v_w_proj_a, w_proj_b=v_w_proj_b,
             w_out=v_w_out, ffn2_norm=v_ffn2_norm, ffn2_w_gu=v_ffn2_w_gu, ffn2_w_down=v_ffn2_w_down,
             ple_norm=v_ple_norm, w_ple_gate=v_w_ple_gate, w_ple_proj=v_w_ple_proj)
    bl, s_len, d = x.shape

    dh0, loss_part, big, smalls, flight, finish, adam = _step(x, p[0], loss_target, w, m, v)
    dg_ffn1, dg_mix, dg_ffn2, dg_ple, dgqa, dgka, dgqb, dgkb, dtab, dsink = smalls

    fold = lambda a: a[:, :, 0, :].reshape(-1, HEAD_DIM).sum(axis=0)
    small_part = dict(
        ffn1_norm=dg_ffn1, mix_norm=dg_mix, ffn2_norm=dg_ffn2, ple_norm=dg_ple,
        a_q_norm=fold(dgqa), a_k_norm=fold(dgka), b_q_norm=fold(dgqb), b_k_norm=fold(dgkb),
        a_rel_bias=dtab,
        b_sinks=dsink.sum(axis=0)[:, 0, :GROUP].reshape(B_Q_HEADS),
        loss=loss_part[0, :1])
    zero1 = jnp.zeros((1,), F32)
    shapes = {nme: w[nme].shape for nme in SMALL_NAMES if nme != "loss"}
    shapes["loss"] = ()
    pk = lambda src: _pack_small({**{nme: src[nme] for nme in SMALL_NAMES if nme != "loss"}, "loss": zero1})
    sg, sd, sm, sv = _small_allreduce_adam(_pack_small(small_part), pk(w), pk(m), pk(v), "small_allreduce_adam")
    adam(finish(flight, sg, "ffn1_gu"), sg)
    sg, sd, sm, sv = (_unpack_small(a, shapes) for a in (sg, sd, sm, sv))

    def pick(i):
        out = []
        for nme in WEIGHT_ORDER:
            out.append(big[nme][i] if nme in big else (sg, sd, sm, sv)[i][nme])
        return out

    return (sg["loss"], dh0.reshape(bl, s_len, d), *pick(0), *pick(1), *pick(2), *pick(3))
```

```python
import jax
import jax.numpy as jnp
import numpy as np
from jax import lax
from jax.experimental import pallas as pl
from jax.experimental.pallas import tpu as pltpu

F32 = jnp.float32
BF16 = jnp.bfloat16

CHUNK = 64
HEAD_DIM = 64
A_HEADS = 8
A_PREV = 8
A_MAX_REL = 128
B_Q_HEADS = 8
B_KV_HEADS = 2
B_PREV = 2
A_WIDTH = A_HEADS * HEAD_DIM
B_Q_WIDTH = B_Q_HEADS * HEAD_DIM
B_KV_WIDTH = B_KV_HEADS * HEAD_DIM
IN_COLS = 3 * A_WIDTH + B_Q_WIDTH + 2 * B_KV_WIDTH
EPS = 1e-6
NEG_INF = -1e30
ADAM_LR = 0.001
ADAM_B1 = 0.9
ADAM_B2 = 0.999
ADAM_EPS = 1e-08
ADAM_WD = 0.01
ADAM_STEP = 10

N_DEV = 8
LANES = 128
QTILE = 2 * CHUNK
VMEM_LIMIT = 56 * 1024 * 1024
DW_VMEM_LIMIT = 60 * 1024 * 1024
ADAM_TILE_ELEMS = 256 * 1024

MESH_ID = pl.DeviceIdType.MESH
ANY = pl.BlockSpec(memory_space=pl.ANY)
HBM = pl.BlockSpec(memory_space=pltpu.HBM)
SEM = pl.BlockSpec(memory_space=pltpu.SEMAPHORE)
SIDE_EFFECT = pltpu.SideEffectType.DATAFLOW_SIDE_EFFECTING


def _dot(a, b):
    return jnp.dot(a, b, preferred_element_type=F32)


def _dot_nt(a, b):
    return lax.dot_general(a, b, (((1,), (1,)), ((), ())), preferred_element_type=F32)


def _dot_tn(a, b):
    return lax.dot_general(a, b, (((0,), (0,)), ((), ())), preferred_element_type=F32)


def _params(sem=None, vmem=VMEM_LIMIT):
    return pltpu.CompilerParams(dimension_semantics=sem, vmem_limit_bytes=vmem)


def _row_tile(t, want):
    while t % want:
        want //= 2
    return want


def _place():
    return lax.axis_index("x"), lax.axis_index("y"), lax.axis_index("c")


def _gather_level(bufs, send_sems, recv_sems, level, shards=None):
    x, y, c = _place()
    me, sib = (x, y, c), (x, y, 1 - c)
    chips = [(1 - x, y), (x, 1 - y), (1 - x, 1 - y)]

    def copy(w, k, block, to):
        px, py, pc = block
        rows = bufs[w].at[4 * px + 2 * py + pc]
        src = shards[w] if shards is not None and block is me else rows
        return pltpu.make_async_remote_copy(src_ref=src, dst_ref=rows, send_sem=send_sems.at[k], recv_sem=recv_sems.at[k],
                                            device_id=to, device_id_type=MESH_ID)

    n = len(bufs)
    own = []
    if level == 1:
        own = [pltpu.make_async_copy(bufs[w].at[4 * x + 2 * y + c] if shards is None else shards[w],
                                     bufs[w].at[4 * x + 2 * y + c], send_sems.at[4 * n + w]) for w in range(n)]
    out, arriving = [], []
    for w in range(len(bufs)):
        if level == 1:
            out.append(copy(w, 4 * w, me, sib))
            arriving.append(copy(w, 4 * w, sib, me))
        for j, chip in enumerate(chips):
            if level == 1:
                out.append(copy(w, 4 * w + 1 + j, me, (*chip, c)))
                arriving.append(copy(w, 4 * w + 1 + j, (*chip, c), me))
            else:
                out.append(copy(w, 3 * w + j, (*chip, c), sib))
                arriving.append(copy(w, 3 * w + j, (*chip, 1 - c), me))
    return out, arriving, own


def _split_call(body, name, bufs, sems_in, after, n_sems_out, token, extra=()):
    n = len(bufs)
    out_shape = [pltpu.SemaphoreType.DMA((n_sems_out,))] * (2 if n_sems_out else 0)
    out_shape += [pltpu.HBM(a.shape, a.dtype) for a in bufs]
    out_specs = [SEM] * (2 if n_sems_out else 0) + [HBM] * n
    if token:
        out_shape.append(jax.ShapeDtypeStruct((8, LANES), F32))
        out_specs.append(pl.BlockSpec(memory_space=pltpu.VMEM))
    first = 2 if n_sems_out else 0
    return pl.pallas_call(
        body, name=name, out_shape=tuple(out_shape),
        in_specs=[HBM] * (n + len(extra)) + [SEM] * len(sems_in) + [ANY], out_specs=tuple(out_specs),
        input_output_aliases={i: first + i for i in range(n)},
        compiler_params=pltpu.CompilerParams(has_side_effects=SIDE_EFFECT),
    )(*bufs, *extra, *sems_in, after)


def _gather_start(shards, after, name):
    n = len(shards)
    hbm = lambda a: pltpu.with_memory_space_constraint(a, pltpu.HBM)
    bufs = [hbm(lax.empty((N_DEV,) + s.shape, s.dtype)) for s in shards]

    def body(*refs):
        out, _, own = _gather_level(refs[:n], refs[2 * n + 1], refs[2 * n + 2], 1, shards=refs[n:2 * n])
        for cp in own + out:
            cp.start()
        refs[-1][...] = jnp.zeros_like(refs[-1])

    outs = _split_call(body, name, bufs + [hbm(s) for s in shards], [], after, 5 * n, True)
    return outs[0], outs[1], list(outs[2:2 + 2 * n]), outs[-1]


def _gather_pass(send1, recv1, bufs_and_shards, after, name):
    n = len(bufs_and_shards) // 2
    bufs = bufs_and_shards

    def body(*refs):
        refs = refs[:n] + refs[2 * n:]
        out1, in1, own = _gather_level(refs[:n], refs[n], refs[n + 1], 1)
        out2, _, _ = _gather_level(refs[:n], refs[n + 3], refs[n + 4], 2)
        for cp in in1:
            cp.wait_recv()
        for cp in out2:
            cp.start()
        for cp in out1:
            cp.wait_send()
        for cp in own:
            cp.wait()
        refs[-1][...] = jnp.zeros_like(refs[-1])

    outs = _split_call(body, name, bufs, [send1, recv1], after, 3 * n, True)
    return outs[0], outs[1], list(outs[2:2 + n]), outs[-1]


def _gather_wait(send2, recv2, bufs, after, name):
    n = len(bufs)

    def body(*refs):
        out2, in2, _ = _gather_level(refs[:n], refs[n], refs[n + 1], 2)
        for cp in in2:
            cp.wait_recv()
        for cp in out2:
            cp.wait_send()

    return list(_split_call(body, name, bufs, [send2, recv2], after, 0, False))


ALL_PEERS = tuple(range(1, N_DEV))
SAME_CORE_CHIPS = (2, 4, 6)


def _scatter_copies(parts, lands, send_sems, recv_sems, relations):
    x, y, c = _place()
    ns = len(relations)
    cps = []
    for w, (part, land) in enumerate(zip(parts, lands)):
        for i, k in enumerate(relations):
            px, py, pc = x ^ ((k >> 2) & 1), y ^ ((k >> 1) & 1), c ^ (k & 1)
            block = 4 * px + 2 * py + pc if part.shape[0] == N_DEV else 2 * px + py
            cps.append(pltpu.make_async_remote_copy(
                src_ref=part.at[block], dst_ref=land.at[i],
                send_sem=send_sems.at[ns * w + i], recv_sem=recv_sems.at[ns * w + i],
                device_id=(px, py, pc), device_id_type=MESH_ID))
    return cps


def _scatter_start(parts, after, name, relations=ALL_PEERS):
    n = len(parts)
    ns = len(relations)

    def body(*refs):
        ins, lands = refs[:n], refs[n:2 * n]
        send_sems, recv_sems = refs[2 * n + 1], refs[2 * n + 2]
        token = refs[-1]
        for cp in _scatter_copies(ins, lands, send_sems, recv_sems, relations):
            cp.start()
        token[...] = jnp.zeros_like(token)

    land_shapes = [(ns,) + p.shape[1:] for p in parts]
    in_hbm = [pltpu.with_memory_space_constraint(p, pltpu.HBM) for p in parts]
    in_hbm += [pltpu.with_memory_space_constraint(lax.empty(s, p.dtype), pltpu.HBM) for s, p in zip(land_shapes, parts)]
    outs = pl.pallas_call(
        body, name=name,
        out_shape=(pltpu.SemaphoreType.DMA((ns * n,)), pltpu.SemaphoreType.DMA((ns * n,)),
                   *[pltpu.HBM(p.shape, p.dtype) for p in parts],
                   *[pltpu.HBM(s, p.dtype) for s, p in zip(land_shapes, parts)],
                   jax.ShapeDtypeStruct((8, LANES), F32)),
        in_specs=[HBM] * (2 * n) + [ANY],
        out_specs=(SEM, SEM, *[HBM] * (2 * n), pl.BlockSpec(memory_space=pltpu.VMEM)),
        input_output_aliases={i: 2 + i for i in range(2 * n)},
        compiler_params=pltpu.CompilerParams(has_side_effects=SIDE_EFFECT),
    )(*in_hbm, after)
    return outs[0], outs[1], list(outs[2:2 + n]), list(outs[2 + n:2 + 2 * n]), outs[-1]


def _scatter_wait(send_sems, recv_sems, parts, lands, after, name, relations=ALL_PEERS):
    n = len(parts)

    def body(*refs):
        ins, lnd = refs[:n], refs[n:2 * n]
        for cp in _scatter_copies(ins, lnd, refs[2 * n], refs[2 * n + 1], relations):
            cp.wait_send()
            cp.wait_recv()

    outs = pl.pallas_call(
        body, name=name,
        out_shape=tuple(pltpu.HBM(a.shape, a.dtype) for a in parts + lands),
        in_specs=[HBM] * (2 * n) + [SEM, SEM, ANY],
        out_specs=tuple([HBM] * (2 * n)),
        input_output_aliases={i: i for i in range(2 * n)},
        compiler_params=pltpu.CompilerParams(has_side_effects=SIDE_EFFECT),
    )(*parts, *lands, send_sems, recv_sems, after)
    return list(outs[n:])


def _pair_exchange(parts, name):
    n = len(parts)

    def body(*refs):
        ins, outs = refs[:n], refs[n:2 * n]
        send_sems, recv_sems = refs[2 * n:]
        x, y, c = _place()
        cps = [pltpu.make_async_remote_copy(
            src_ref=ins[w].at[:, pl.ds(1 - c, 1)], dst_ref=outs[w], send_sem=send_sems.at[w], recv_sem=recv_sems.at[w],
            device_id=(x, y, 1 - c), device_id_type=MESH_ID) for w in range(n)]
        for cp in cps:
            cp.start()
        for cp in cps:
            cp.wait()

    return pl.pallas_call(
        body, name=name,
        out_shape=[jax.ShapeDtypeStruct((4, 1) + p.shape[2:], p.dtype) for p in parts],
        in_specs=[ANY] * n, out_specs=[ANY] * n,
        scratch_shapes=[pltpu.SemaphoreType.DMA((n,)), pltpu.SemaphoreType.DMA((n,))],
    )(*parts)


def _pair_sum(g8, r1, me, name):
    _, r, c = g8.shape
    tr = max(q for q in range(16, r + 1, 16) if r % q == 0 and q * c <= ADAM_TILE_ELEMS)

    def body(me_ref, g_ref, r_ref, o_ref):
        o_ref[...] = (g_ref[...] + r_ref[...].astype(F32)).astype(BF16)

    chip = lambda k, s: s[1] ^ (k + 1)
    return pl.pallas_call(
        body, name=name,
        out_shape=jax.ShapeDtypeStruct((4, r, c), BF16),
        grid_spec=pltpu.PrefetchScalarGridSpec(
            num_scalar_prefetch=1, grid=(3, r // tr),
            in_specs=[pl.BlockSpec((None, None, tr, c), lambda k, i, s: (chip(k, s), s[0] % 2, i, 0)),
                      pl.BlockSpec((None, tr, c), lambda k, i, s: (chip(k, s), i, 0))],
            out_specs=pl.BlockSpec((None, tr, c), lambda k, i, s: (chip(k, s), i, 0))),
        compiler_params=_params(("arbitrary", "arbitrary")),
    )(me, g8.reshape((4, 2) + g8.shape[1:]), r1)


def _adam(w, g, m, v):
    m2 = ADAM_B1 * m + (1.0 - ADAM_B1) * g
    v2 = ADAM_B2 * v + (1.0 - ADAM_B2) * (g * g)
    m_hat = m2 / (1.0 - ADAM_B1 ** ADAM_STEP)
    v_hat = v2 / (1.0 - ADAM_B2 ** ADAM_STEP)
    delta = -ADAM_LR * (m_hat / (jnp.sqrt(v_hat) + ADAM_EPS) + ADAM_WD * w)
    return delta, m2, v2


def _small_allreduce_adam(part, w, m, v, name):
    rows = part.shape[0]

    def body(p_ref, w_ref, m_ref, v_ref, g_ref, d_ref, mo_ref, vo_ref, buf, send_sems, recv_sems):
        x, y, c = _place()
        buf[0] = p_ref[...]
        cps = []
        for k in range(1, N_DEV):
            kx, ky, kc = (k >> 2) & 1, (k >> 1) & 1, k & 1
            peer = (x ^ kx, y ^ ky, c ^ kc)
            cps.append(pltpu.make_async_remote_copy(
                src_ref=p_ref, dst_ref=buf.at[k], send_sem=send_sems.at[k - 1], recv_sem=recv_sems.at[k - 1],
                device_id=peer, device_id_type=MESH_ID))
        for cp in cps:
            cp.start()
        for cp in cps:
            cp.wait()
        me = 4 * x + 2 * y + c
        total = buf[me]
        for d in range(1, N_DEV):
            total = total + buf[d ^ me]
        g_ref[...] = total
        delta, m2, v2 = _adam(w_ref[...], total, m_ref[...], v_ref[...])
        d_ref[...] = delta
        mo_ref[...] = m2
        vo_ref[...] = v2

    vm = pl.BlockSpec(memory_space=pltpu.VMEM)
    return pl.pallas_call(
        body, name=name,
        out_shape=[jax.ShapeDtypeStruct(part.shape, F32)] * 4,
        in_specs=[vm] * 4, out_specs=[vm] * 4,
        scratch_shapes=[pltpu.VMEM((N_DEV, rows, LANES), F32),
                        pltpu.SemaphoreType.DMA((N_DEV - 1,)), pltpu.SemaphoreType.DMA((N_DEV - 1,))],
    )(part, w, m, v)


def _final_adam(g8, land, w, m, v, me, dep, name, pair=None):
    _, r, c = g8.shape
    tr = max(q for q in range(16, r + 1, 16) if r % q == 0 and q * c <= ADAM_TILE_ELEMS)
    nland = land.shape[0]

    def body(me_ref, g_ref, land_ref, *rest):
        pair_ref = rest[0] if pair is not None else None
        w_ref, m_ref, v_ref, _, go_ref, d_ref, mo_ref, vo_ref = rest[-8:]
        g = g_ref[...]
        if pair_ref is not None:
            g = g + pair_ref[...].astype(F32)
        for k in range(nland):
            g = g + land_ref[k].astype(F32)
        go_ref[...] = g
        delta, m2, v2 = _adam(w_ref[...], g, m_ref[...], v_ref[...])
        d_ref[...] = delta
        mo_ref[...] = m2
        vo_ref[...] = v2

    plain = pl.BlockSpec((tr, c), lambda i, s: (i, 0))
    return pl.pallas_call(
        body, name=name,
        out_shape=[jax.ShapeDtypeStruct((r, c), F32)] * 4,
        grid_spec=pltpu.PrefetchScalarGridSpec(
            num_scalar_prefetch=1, grid=(r // tr,),
            in_specs=[pl.BlockSpec((None, tr, c), lambda i, s: (s[0], i, 0)),
                      pl.BlockSpec((nland, tr, c), lambda i, s: (0, i, 0))]
            + ([] if pair is None else [pl.BlockSpec((None, tr, c), lambda i, s: (s[1], i, 0))])
            + [plain, plain, plain, ANY],
            out_specs=[plain] * 4),
        compiler_params=_params(("arbitrary",)),
    )(*((me, g8, land) + (() if pair is None else (pair,)) + (w, m, v, dep)))


def _rms(x, gain):
    r = lax.rsqrt(jnp.mean(x * x, axis=-1, keepdims=True) + EPS)
    xh = x * r
    return xh * gain, xh, r


def _rms_bwd(xh, r, gain, dy):
    gdy = gain * dy
    dx = r * (gdy - xh * jnp.mean(xh * gdy, axis=-1, keepdims=True))
    return dx, jnp.sum(dy * xh, axis=0, keepdims=True)


def _load_weights(pairs, sems):
    cps = [pltpu.make_async_copy(src, dst, sems.at[i]) for i, (src, dst) in enumerate(pairs)]
    for cp in cps:
        cp.start()
    for cp in cps:
        cp.wait()


def _ffn_fwd(h, gain, wgu, wd, name):
    t, d = h.shape
    nb, nf, _ = wgu.shape
    nh = nb // 2
    tm = _row_tile(t, 512)

    def body(h_ref, g_ref, wgu_hbm, wd_hbm, out_ref, gu_ref, wgu_v, wd_v, sems):
        @pl.when(pl.program_id(0) == 0)
        def _():
            _load_weights([(wgu_hbm, wgu_v), (wd_hbm, wd_v)], sems)

        x = h_ref[...]
        n, _, _ = _rms(x, g_ref[...])
        nbf = n.astype(BF16)
        acc = jnp.zeros((tm, d), F32)
        for j in range(nh):
            g = _dot_nt(nbf, wgu_v[j])
            u = _dot_nt(nbf, wgu_v[j + nh])
            gu_ref[j] = g.astype(BF16)
            gu_ref[j + nh] = u.astype(BF16)
            a = (g * jax.nn.sigmoid(g)) * u
            acc = acc + _dot(a.astype(BF16), wd_v[j])
        out_ref[...] = x + 0.5 * acc

    return pl.pallas_call(
        body, name=name, grid=(t // tm,),
        out_shape=[jax.ShapeDtypeStruct((t, d), F32), jax.ShapeDtypeStruct((nb, t, nf), BF16)],
        in_specs=[pl.BlockSpec((tm, d), lambda i: (i, 0)), pl.BlockSpec((1, d), lambda i: (0, 0)), ANY, ANY],
        out_specs=[pl.BlockSpec((tm, d), lambda i: (i, 0)), pl.BlockSpec((nb, tm, nf), lambda i: (0, i, 0))],
        scratch_shapes=[pltpu.VMEM(wgu.shape, BF16), pltpu.VMEM(wd.shape, BF16), pltpu.SemaphoreType.DMA((2,))],
        compiler_params=_params(("arbitrary",)),
    )(h, gain, wgu, wd)


def _ffn_bwd(dh, h, gain, gu, wgu, wd, name):
    t, d = h.shape
    nb, nf, _ = wgu.shape
    nh = nb // 2
    tm = _row_tile(t, 256)

    def body(dh_ref, h_ref, g_ref, gu_ref, wgu_hbm, wd_hbm, dhp_ref, dgu_ref, a_ref, n_ref, dgain_ref,
             wgu_v, wd_v, sems):
        @pl.when(pl.program_id(0) == 0)
        def _():
            _load_weights([(wgu_hbm, wgu_v), (wd_hbm, wd_v)], sems)
            dgain_ref[...] = jnp.zeros_like(dgain_ref)

        x = h_ref[...]
        gain_v = g_ref[...]
        n, xh, r = _rms(x, gain_v)
        n_ref[...] = n.astype(BF16)
        dh_v = dh_ref[...]
        dfb = (0.5 * dh_v).astype(BF16)
        dn = jnp.zeros((tm, d), F32)
        for j in range(nh):
            da = _dot_nt(dfb, wd_v[j])
            g = gu_ref[j].astype(F32)
            u = gu_ref[j + nh].astype(F32)
            sg = jax.nn.sigmoid(g)
            si = g * sg
            dg = (da * u * (sg * (1.0 + g * (1.0 - sg)))).astype(BF16)
            du = (da * si).astype(BF16)
            a_ref[j] = (si * u).astype(BF16)
            dgu_ref[j] = dg
            dgu_ref[j + nh] = du
            dn = dn + _dot(dg, wgu_v[j]) + _dot(du, wgu_v[j + nh])
        dx, dgain = _rms_bwd(xh, r, gain_v, dn)
        dhp_ref[...] = dh_v + dx
        dgain_ref[...] += dgain

    row = pl.BlockSpec((tm, d), lambda i: (i, 0))
    vec = pl.BlockSpec((1, d), lambda i: (0, 0))
    return pl.pallas_call(
        body, name=name, grid=(t // tm,),
        out_shape=[jax.ShapeDtypeStruct((t, d), F32), jax.ShapeDtypeStruct((nb, t, nf), BF16),
                   jax.ShapeDtypeStruct((nh, t, nf), BF16), jax.ShapeDtypeStruct((t, d), BF16),
                   jax.ShapeDtypeStruct((1, d), F32)],
        in_specs=[row, row, vec, pl.BlockSpec((nb, tm, nf), lambda i: (0, i, 0)), ANY, ANY],
        out_specs=[row, pl.BlockSpec((nb, tm, nf), lambda i: (0, i, 0)),
                   pl.BlockSpec((nh, tm, nf), lambda i: (0, i, 0)), row, vec],
        scratch_shapes=[pltpu.VMEM(wgu.shape, BF16), pltpu.VMEM(wd.shape, BF16), pltpu.SemaphoreType.DMA((2,))],
        compiler_params=_params(("arbitrary",)),
    )(dh, h, gain, gu, wgu, wd)


def _dw(xa, dy, nb, n, name, scale=1.0, dep=None):
    t, k = xa.shape[-2:]
    wide = xa.ndim == 2
    tt = _row_tile(t, 1024)
    steps = t // tt
    x_spec = pl.BlockSpec((tt, k), lambda i: (i, 0)) if wide else pl.BlockSpec((nb, tt, k), lambda i: (0, i, 0))
    dy_spec = pl.BlockSpec((tt, dy.shape[1]), lambda i: (i, 0))
    acc_shape = (k, nb * n) if wide else (nb, k, n)
    stage_shape = (k, nb * n) if wide else (k, n)

    def body(x_ref, dy_ref, *rest):
        o_hbm, ob_hbm, acc, stage, sems = rest[-5:]

        @pl.when(pl.program_id(0) == 0)
        def _():
            acc[...] = jnp.zeros_like(acc)

        dyb = dy_ref[...].astype(BF16)
        if wide:
            acc[...] += _dot(x_ref[...].astype(BF16).T, dyb)
        else:
            for j in range(nb):
                acc[j] += _dot_tn(x_ref[j].astype(BF16), dyb)

        @pl.when(pl.program_id(0) == steps - 1)
        def _():
            if scale != 1.0:
                acc[...] = acc[...] * scale
            if wide:
                cps = [pltpu.make_async_copy(acc.at[:, pl.ds(j * n, n)] if nb > 1 else acc, o_hbm.at[j], sems.at[j])
                       for j in range(nb)]
            else:
                cps = [pltpu.make_async_copy(acc, o_hbm, sems.at[0])]
            for cp in cps:
                cp.start()
            if wide:
                stage[...] = acc[...].astype(BF16)
                bcs = [pltpu.make_async_copy(stage.at[:, pl.ds(j * n, n)] if nb > 1 else stage, ob_hbm.at[j],
                                             sems.at[nb + j]) for j in range(nb)]
                for cp in bcs:
                    cp.start()
                for cp in bcs:
                    cp.wait()
            else:
                for j in range(nb):
                    stage[...] = acc[j].astype(BF16)
                    cp = pltpu.make_async_copy(stage, ob_hbm.at[j], sems.at[nb])
                    cp.start()
                    cp.wait()
            for cp in cps:
                cp.wait()

    return pl.pallas_call(
        body, name=name, grid=(steps,),
        out_shape=[jax.ShapeDtypeStruct((nb, k, n), F32), jax.ShapeDtypeStruct((nb, k, n), BF16)],
        in_specs=[x_spec, dy_spec] + ([] if dep is None else [ANY]),
        out_specs=[ANY, ANY],
        scratch_shapes=[pltpu.VMEM(acc_shape, F32), pltpu.VMEM(stage_shape, BF16),
                        pltpu.SemaphoreType.DMA((2 * nb,))],
        compiler_params=_params(("arbitrary",), DW_VMEM_LIMIT),
    )(*((xa, dy) if dep is None else (xa, dy, dep)))


def _proj_fwd(h, gain, win, wgate, name):
    t, d = h.shape
    tm = _row_tile(t, 512)
    nq, ng = win.shape[0], wgate.shape[1]

    def body(h_ref, g_ref, win_ref, wg_ref, un_ref, qkv_ref, gate_ref):
        n, _, _ = _rms(h_ref[...], g_ref[...])
        nbf = n.astype(BF16)
        un_ref[...] = nbf
        qkv_ref[...] = _dot_nt(nbf, win_ref[...])
        gate_ref[...] = jax.nn.sigmoid(_dot(nbf, wg_ref[...]))

    full = lambda a: pl.BlockSpec(a.shape, lambda i: (0,) * a.ndim)
    return pl.pallas_call(
        body, name=name, grid=(t // tm,),
        out_shape=[jax.ShapeDtypeStruct((t, d), BF16), jax.ShapeDtypeStruct((t, nq), F32),
                   jax.ShapeDtypeStruct((t, ng), F32)],
        in_specs=[pl.BlockSpec((tm, d), lambda i: (i, 0)), full(gain), full(win), full(wgate)],
        out_specs=[pl.BlockSpec((tm, d), lambda i: (i, 0)), pl.BlockSpec((tm, nq), lambda i: (i, 0)),
                   pl.BlockSpec((tm, ng), lambda i: (i, 0))],
        compiler_params=_params(("arbitrary",)),
    )(h, gain, win, wgate)


def _proj_bwd(dh, h, gain, dzg, dqkv_parts, win, wgate, name):
    t, d = h.shape
    tm = _row_tile(t, 512)
    ng = wgate.shape[1]
    np_ = len(dqkv_parts)
    widths = [a.shape[1] for a in dqkv_parts]

    def body(dh_ref, h_ref, g_ref, dzg_ref, *rest):
        part_refs, (win_ref, wg_ref, dhp_ref, dgain_ref) = rest[:np_], rest[np_:]

        @pl.when(pl.program_id(0) == 0)
        def _():
            dgain_ref[...] = jnp.zeros_like(dgain_ref)

        gain_v = g_ref[...]
        _, xh, r = _rms(h_ref[...], gain_v)
        dun = _dot_nt(dzg_ref[...], wg_ref[...])
        off = 0
        for ref, wd in zip(part_refs, widths):
            dun = dun + _dot(ref[...].astype(BF16), win_ref[off:off + wd, :])
            off += wd
        dx, dgain = _rms_bwd(xh, r, gain_v, dun)
        dhp_ref[...] = dh_ref[...] + dx
        dgain_ref[...] += dgain

    full = lambda a: pl.BlockSpec(a.shape, lambda i: (0,) * a.ndim)
    row = pl.BlockSpec((tm, d), lambda i: (i, 0))
    return pl.pallas_call(
        body, name=name, grid=(t // tm,),
        out_shape=[jax.ShapeDtypeStruct((t, d), F32), jax.ShapeDtypeStruct((1, d), F32)],
        in_specs=[row, row, full(gain), pl.BlockSpec((tm, ng), lambda i: (i, 0))]
        + [pl.BlockSpec((tm, wd), lambda i: (i, 0)) for wd in widths] + [full(win), full(wgate)],
        out_specs=[row, pl.BlockSpec((1, d), lambda i: (0, 0))],
        compiler_params=_params(("arbitrary",)),
    )(dh, h, gain, dzg, *dqkv_parts, win, wgate)


def _dw_rows(parts, dy, name):
    t, n = dy.shape
    widths = [a.shape[1] for a in parts]
    k = sum(widths)
    tt = _row_tile(t, 1024)
    steps = t // tt
    np_ = len(parts)

    def body(*refs):
        part_refs, dy_ref = refs[:np_], refs[np_]
        o_hbm, ob_hbm, acc, stage, sems = refs[np_ + 1:]

        @pl.when(pl.program_id(0) == 0)
        def _():
            acc[...] = jnp.zeros_like(acc)

        dyb = dy_ref[...].astype(BF16)
        off = 0
        for ref, wd in zip(part_refs, widths):
            acc[off:off + wd, :] += _dot(ref[...].astype(BF16).T, dyb)
            off += wd

        @pl.when(pl.program_id(0) == steps - 1)
        def _():
            stage[...] = acc[...].astype(BF16)
            cps = [pltpu.make_async_copy(acc, o_hbm.at[0], sems.at[0]),
                   pltpu.make_async_copy(stage, ob_hbm.at[0], sems.at[1])]
            for cp in cps:
                cp.start()
            for cp in cps:
                cp.wait()

    return pl.pallas_call(
        body, name=name, grid=(steps,),
        out_shape=[jax.ShapeDtypeStruct((1, k, n), F32), jax.ShapeDtypeStruct((1, k, n), BF16)],
        in_specs=[pl.BlockSpec((tt, wd), lambda i: (i, 0)) for wd in widths] + [pl.BlockSpec((tt, n), lambda i: (i, 0))],
        out_specs=[ANY, ANY],
        scratch_shapes=[pltpu.VMEM((k, n), F32), pltpu.VMEM((k, n), BF16), pltpu.SemaphoreType.DMA((2,))],
        compiler_params=_params(("arbitrary",)),
    )(*parts, dy)


def _merge_fwd(h, ya, yb, gate, wpa, wpb, wout, name):
    t, d = h.shape
    tm = _row_tile(t, 512)

    def body(h_ref, ya_ref, yb_ref, ga_ref, gb_ref, wpa_ref, wpb_ref, wout_ref, out_ref, mg_ref, pa_ref, pb_ref):
        pa = _dot(ya_ref[...].astype(BF16), wpa_ref[...])
        pb = _dot(yb_ref[...].astype(BF16), wpb_ref[...])
        merged = (ga_ref[...] * pa + gb_ref[...] * pb).astype(BF16)
        pa_ref[...] = pa.astype(BF16)
        pb_ref[...] = pb.astype(BF16)
        mg_ref[...] = merged
        out_ref[...] = h_ref[...] + _dot(merged, wout_ref[...])

    full = lambda a: pl.BlockSpec(a.shape, lambda i: (0,) * a.ndim)
    row = pl.BlockSpec((tm, d), lambda i: (i, 0))
    yrow = pl.BlockSpec((tm, ya.shape[1]), lambda i: (i, 0))
    return pl.pallas_call(
        body, name=name, grid=(t // tm,),
        out_shape=[jax.ShapeDtypeStruct((t, d), F32)] + [jax.ShapeDtypeStruct((t, d), BF16)] * 3,
        in_specs=[row, yrow, yrow, pl.BlockSpec((tm, d), lambda i: (i, 0)), pl.BlockSpec((tm, d), lambda i: (i, 1)),
                  full(wpa), full(wpb), full(wout)],
        out_specs=[row] * 4,
        compiler_params=_params(("arbitrary",)),
    )(h, ya, yb, gate, gate, wpa, wpb, wout)


def _merge_bwd(dh, pa, pb, gate, wpa, wpb, wout, name):
    t, d = dh.shape
    tm = _row_tile(t, 512)
    wy = wpa.shape[0]

    def body(dh_ref, pa_ref, pb_ref, ga_ref, gb_ref, wpa_ref, wpb_ref, wout_ref,
             dpa_ref, dpb_ref, dzg_ref, dya_ref, dyb_ref):
        dm = _dot_nt(dh_ref[...].astype(BF16), wout_ref[...])
        ga, gb = ga_ref[...], gb_ref[...]
        dpa = (dm * ga).astype(BF16)
        dpb = (dm * gb).astype(BF16)
        dpa_ref[...] = dpa
        dpb_ref[...] = dpb
        dzg_ref[:, :d] = (dm * pa_ref[...].astype(F32) * ga * (1.0 - ga)).astype(BF16)
        dzg_ref[:, d:] = (dm * pb_ref[...].astype(F32) * gb * (1.0 - gb)).astype(BF16)
        dya_ref[...] = _dot_nt(dpa, wpa_ref[...])
        dyb_ref[...] = _dot_nt(dpb, wpb_ref[...])

    full = lambda a: pl.BlockSpec(a.shape, lambda i: (0,) * a.ndim)
    row = pl.BlockSpec((tm, d), lambda i: (i, 0))
    yrow = pl.BlockSpec((tm, wy), lambda i: (i, 0))
    return pl.pallas_call(
        body, name=name, grid=(t // tm,),
        out_shape=[jax.ShapeDtypeStruct((t, d), BF16), jax.ShapeDtypeStruct((t, d), BF16),
                   jax.ShapeDtypeStruct((t, 2 * d), BF16), jax.ShapeDtypeStruct((t, wy), F32),
                   jax.ShapeDtypeStruct((t, wy), F32)],
        in_specs=[row, row, row, pl.BlockSpec((tm, d), lambda i: (i, 0)), pl.BlockSpec((tm, d), lambda i: (i, 1)),
                  full(wpa), full(wpb), full(wout)],
        out_specs=[row, row, pl.BlockSpec((tm, 2 * d), lambda i: (i, 0)), yrow, yrow],
        compiler_params=_params(("arbitrary",)),
    )(dh, pa, pb, gate, gate, wpa, wpb, wout)


def _ple_loss(h, gain, p, target, wpg, wpe, name):
    t, d = h.shape
    tm = _row_tile(t, 512)
    pd = p.shape[1]

    def body(h_ref, g_ref, p_ref, t_ref, wpg_ref, wpe_ref, dh_ref, dz_ref, dpp_ref, n_ref, dgain_ref, loss_ref):
        @pl.when(pl.program_id(0) == 0)
        def _():
            dgain_ref[...] = jnp.zeros_like(dgain_ref)
            loss_ref[...] = jnp.zeros_like(loss_ref)

        x = h_ref[...]
        gain_v = g_ref[...]
        n, xh, r = _rms(x, gain_v)
        nbf = n.astype(BF16)
        n_ref[...] = nbf
        pg = jax.nn.sigmoid(_dot(nbf, wpg_ref[...]))
        pp = _dot(p_ref[...].astype(BF16), wpe_ref[...])
        err = (x + pg * pp) - t_ref[...]
        loss_ref[...] += 0.5 * jnp.sum(jnp.mean(err * err, axis=-1, keepdims=True))
        dy = err * (1.0 / d)
        dpp_ref[...] = (dy * pg).astype(BF16)
        dz = (dy * pp * pg * (1.0 - pg)).astype(BF16)
        dz_ref[...] = dz
        dn = _dot_nt(dz, wpg_ref[...])
        dx, dgain = _rms_bwd(xh, r, gain_v, dn)
        dh_ref[...] = dy + dx
        dgain_ref[...] += dgain

    full = lambda a: pl.BlockSpec(a.shape, lambda i: (0,) * a.ndim)
    row = pl.BlockSpec((tm, d), lambda i: (i, 0))
    return pl.pallas_call(
        body, name=name, grid=(t // tm,),
        out_shape=[jax.ShapeDtypeStruct((t, d), F32), jax.ShapeDtypeStruct((t, d), BF16),
                   jax.ShapeDtypeStruct((t, d), BF16), jax.ShapeDtypeStruct((t, d), BF16),
                   jax.ShapeDtypeStruct((1, d), F32), jax.ShapeDtypeStruct((8, LANES), F32)],
        in_specs=[row, full(gain), pl.BlockSpec((tm, pd), lambda i: (i, 0)), row, full(wpg), full(wpe)],
        out_specs=[row, row, row, row, pl.BlockSpec((1, d), lambda i: (0, 0)),
                   pl.BlockSpec((8, LANES), lambda i: (0, 0))],
        compiler_params=_params(("arbitrary",)),
    )(h, gain, p, target, wpg, wpe)


def _head_masks():
    lane = lax.broadcasted_iota(jnp.int32, (1, LANES), 1)
    m0 = (lane < HEAD_DIM).astype(F32)
    return m0, 1.0 - m0


def _head_mean(v, m0, m1):
    del m0, m1
    width = v.shape[-1]
    shift = HEAD_DIM.bit_length() - 1
    r = jnp.right_shift(lax.broadcasted_iota(jnp.int32, (width, width), 0), shift)
    c = jnp.right_shift(lax.broadcasted_iota(jnp.int32, (width, width), 1), shift)
    same_head = (r == c).astype(BF16)
    return _dot(v.astype(BF16), same_head) * (1.0 / HEAD_DIM)


def _head_norm(x, gain, m0, m1):
    r = lax.rsqrt(_head_mean(x * x, m0, m1) + EPS)
    xh = x * r
    return xh * gain, xh, r


def _head_norm_bwd(xh, r, gain, dy, m0, m1):
    gdy = gain * dy
    dx = r * (gdy - xh * _head_mean(xh * gdy, m0, m1))
    return dx, jnp.sum(dy * xh, axis=0, keepdims=True)


GROUP = 4
QW = GROUP * HEAD_DIM
STACK = GROUP * QTILE


def _kv_width(mode):
    return QW if mode == "A" else LANES


def _q_scratch_shape(mode, s_len):
    return (s_len, QW) if mode == "A" else (GROUP * s_len, LANES)


def _group_masks(dtype=F32):
    lane = lax.broadcasted_iota(jnp.int32, (1, QW), 1)
    return [((lane >= h * HEAD_DIM) & (lane < (h + 1) * HEAD_DIM)).astype(dtype) for h in range(GROUP)]


def _stack_heads(first_kv, x, m0, m1):
    out = []
    for half in range(GROUP // 2):
        xh = x[:, half * LANES:(half + 1) * LANES]
        a0, a1 = xh * m0, xh * m1
        r0, r1 = pltpu.roll(a0, HEAD_DIM, 1), pltpu.roll(a1, HEAD_DIM, 1)
        out += [jnp.where(first_kv, a0, r0), jnp.where(first_kv, r1, a1)]
    return out


def _unstack_heads(mode, first_kv, ts, m0, m1):
    if mode == "A":
        masks = _group_masks()
        return sum(t * mk for t, mk in zip(ts, masks))
    halves = []
    for half in range(GROUP // 2):
        t0 = jnp.where(first_kv, ts[2 * half], pltpu.roll(ts[2 * half], HEAD_DIM, 1))
        t1 = jnp.where(first_kv, pltpu.roll(ts[2 * half + 1], HEAD_DIM, 1), ts[2 * half + 1])
        halves.append(t0 * m0 + t1 * m1)
    return jnp.concatenate(halves, axis=1)


def _store_stacked(dst, i, heads):
    for half in range(2):
        rows = slice(half * QTILE, (half + 1) * QTILE)
        for h, x in enumerate(heads):
            dst[pl.ds((2 * i + half) * STACK + h * QTILE, QTILE), :] = x[rows].astype(dst.dtype)


def _load_stacked(mode, ref, m):
    if mode == "B":
        return ref[pl.ds(pl.multiple_of(m * STACK, STACK), STACK), :]
    x = ref[pl.ds(pl.multiple_of(m * QTILE, QTILE), QTILE), :]
    return jnp.concatenate([x * mk for mk in _group_masks(x.dtype)], axis=0)


def _attn_prep(mode, group, s_len, padk, q_ref, k_ref, v_ref, gq_ref, gk_ref, qs, k2, v2, do_ref=None, dos=None):
    m0, m1 = _head_masks()
    zpad = jnp.zeros((padk, k2.shape[1]), BF16)
    k2[pl.ds(0, padk), :] = zpad
    v2[pl.ds(0, padk), :] = zpad
    first_kv = group == 0
    rt = 2 * QTILE
    for i in range(s_len // rt):
        rows = pl.ds(i * rt, rt)
        qn, _, _ = _head_norm(q_ref[rows, :], gq_ref[...], m0, m1)
        kn, _, _ = _head_norm(k_ref[rows, :], gk_ref[...], m0, m1)
        qn = qn * (HEAD_DIM ** -0.5)
        if mode == "A":
            qs[rows, :] = qn.astype(BF16)
            if dos is not None:
                dos[rows, :] = do_ref[rows, :].astype(BF16)
        else:
            _store_stacked(qs, i, _stack_heads(first_kv, qn, m0, m1))
            if dos is not None:
                _store_stacked(dos, i, _stack_heads(first_kv, do_ref[rows, :], m0, m1))
        k2[pl.ds(padk + i * rt, rt), :] = kn.astype(BF16)
        v2[pl.ds(padk + i * rt, rt), :] = v_ref[rows, :].astype(BF16)


def _softmax_terms(mode, s, sink):
    mx = jnp.max(s, axis=-1, keepdims=True)
    if mode == "B":
        mx = jnp.maximum(mx, sink)
    e = jnp.exp(s - mx)
    l = jnp.sum(e, axis=-1, keepdims=True)
    if mode == "B":
        l = l + jnp.exp(sink - mx)
    return e, mx, l


def _sink_column(sink_ref, group):
    row = lax.broadcasted_iota(jnp.int32, (STACK, 1), 0)
    col = jnp.zeros((STACK, 1), F32)
    for h in range(GROUP):
        col = jnp.where((row >= h * QTILE) & (row < (h + 1) * QTILE), sink_ref[GROUP * group + h], col)
    return col


def _head_deltas(dd, m0, m1):
    cols = []
    for half in range(GROUP // 2):
        dh = dd[:, half * LANES:(half + 1) * LANES]
        cols += [jnp.sum(dh * m0, axis=-1, keepdims=True), jnp.sum(dh * m1, axis=-1, keepdims=True)]
    return jnp.concatenate(cols, axis=0)


def _attn_cols(mode):
    if mode == "A":
        return (lambda b, g: (b, g)), (lambda b, g: (b, 2 + g)), (lambda b, g: (b, 4 + g))
    return (lambda b, g: (b, 6 + g)), (lambda b, g: (b, 16)), (lambda b, g: (b, 17))


def _attn_fwd(mode, qkv, gq, gk, bias, sinks, bl, s_len, name):
    bw = bias.shape[-1]
    padk = bw - QTILE
    nt = s_len // QTILE
    qmap, kmap, vmap = _attn_cols(mode)

    kw = _kv_width(mode)

    def body(q_ref, k_ref, v_ref, gq_ref, gk_ref, bias_ref, sink_ref, o_ref, qs, k2, v2, s_buf, *rest):
        o_buf = rest[0] if rest else None
        group = pl.program_id(1)
        m0, m1 = _head_masks()
        first_kv = group == 0
        _attn_prep(mode, group, s_len, padk, q_ref, k_ref, v_ref, gq_ref, gk_ref, qs, k2, v2)
        col = lax.broadcasted_iota(jnp.int32, (STACK, bw), 1)
        sink = _sink_column(sink_ref, group)

        def scores(m, slot):
            r0 = pl.multiple_of(m * QTILE, QTILE)
            s = _dot_nt(_load_stacked(mode, qs, m), k2[pl.ds(r0, bw), :]) + bias_ref[...]
            s_buf[slot] = jnp.where(col >= (padk - r0), s, NEG_INF)

        def finish_tile(m, slot):
            r0 = pl.multiple_of(m * QTILE, QTILE)
            e, _, l = _softmax_terms(mode, s_buf[slot], sink)
            if mode == "A":
                o_st = _dot(e.astype(BF16), v2[pl.ds(r0, bw), :]) / l
                heads = [o_st[h * QTILE:(h + 1) * QTILE] for h in range(GROUP)]
                o_ref[pl.ds(r0, QTILE), :] = _unstack_heads(mode, first_kv, heads, m0, m1)
            else:
                o_buf[pl.ds(pl.multiple_of(m * STACK, STACK), STACK), :] = _dot((e * (1.0 / l)).astype(BF16),
                                                                                 v2[pl.ds(r0, bw), :])

        scores(0, 0)

        def pair(j, carry):
            scores(2 * j + 1, 1)
            finish_tile(2 * j, 0)
            scores(jnp.minimum(2 * j + 2, nt - 1), 0)
            finish_tile(2 * j + 1, 1)
            return carry

        lax.fori_loop(0, nt // 2, pair, 0)
        if mode == "B":
            for m in range(nt):
                heads = [o_buf[pl.ds(m * STACK + h * QTILE, QTILE), :] for h in range(GROUP)]
                o_ref[pl.ds(m * QTILE, QTILE), :] = _unstack_heads(mode, first_kv, heads, m0, m1)

    blk = lambda w, f: pl.BlockSpec((s_len, w), f)
    return pl.pallas_call(
        body, name=name, grid=(bl, B_Q_HEADS // GROUP),
        out_shape=jax.ShapeDtypeStruct((bl * s_len, B_Q_HEADS * HEAD_DIM), F32),
        in_specs=[blk(QW, qmap), blk(kw, kmap), blk(kw, vmap),
                  pl.BlockSpec((1, QW), lambda b, g: (0, 0)), pl.BlockSpec((1, kw), lambda b, g: (0, 0)),
                  pl.BlockSpec((STACK, bw), lambda b, g: (g, 0)),
                  pl.BlockSpec(memory_space=pltpu.SMEM)],
        out_specs=blk(QW, lambda b, g: (b, g)),
        scratch_shapes=[pltpu.VMEM(_q_scratch_shape(mode, s_len), BF16)] + [pltpu.VMEM((s_len + padk, kw), BF16)] * 2
        + [pltpu.VMEM((2, STACK, bw), F32)] + ([pltpu.VMEM((GROUP * s_len, LANES), F32)] if mode == "B" else []),
        compiler_params=_params(("arbitrary", "arbitrary")),
    )(qkv, qkv, qkv, gq, gk, bias.reshape(-1, bw), sinks)


def _attn_bwd(mode, qkv, gq, gk, bias, sinks, y, dy, bl, s_len, name):
    bw = bias.shape[-1]
    padk = bw - QTILE
    nt = s_len // QTILE
    qmap, kmap, vmap = _attn_cols(mode)
    t = bl * s_len
    kw = _kv_width(mode)
    kvw = 4 * LANES if mode == "A" else LANES
    dp_ahead = True

    def body(q_ref, k_ref, v_ref, gq_ref, gk_ref, bias_ref, sink_ref, y_ref, dy_ref,
             dq_ref, dk_ref, dv_ref, dgq_ref, dgk_ref, dbias_ref, dsink_ref,
             qs, k2, v2, dos, dqs, dk, dv, s_buf, dp_buf):
        group = pl.program_id(1)
        m0, m1 = _head_masks()
        first_kv = group == 0
        _attn_prep(mode, group, s_len, padk, q_ref, k_ref, v_ref, gq_ref, gk_ref, qs, k2, v2, dy_ref, dos)
        dk[...] = jnp.zeros_like(dk)
        dv[...] = jnp.zeros_like(dv)
        dbias_ref[...] = jnp.zeros_like(dbias_ref)
        col = lax.broadcasted_iota(jnp.int32, (STACK, bw), 1)
        lane8 = lax.broadcasted_iota(jnp.int32, (8, LANES), 1)
        sink = _sink_column(sink_ref, group)

        def ahead(m, slot):
            r0 = pl.multiple_of(m * QTILE, QTILE)
            band = pl.ds(r0, bw)
            s = _dot_nt(_load_stacked(mode, qs, m), k2[band, :]) + bias_ref[...]
            s_buf[slot] = jnp.where(col >= (padk - r0), s, NEG_INF)
            if dp_ahead:
                dp_buf[slot] = _dot_nt(_load_stacked(mode, dos, m), v2[band, :])

        def tile(m, slot, dsink):
            r0 = pl.multiple_of(m * QTILE, QTILE)
            rows = pl.ds(r0, QTILE)
            band = pl.ds(r0, bw)
            q_st = _load_stacked(mode, qs, m)
            do_st = _load_stacked(mode, dos, m)
            delta = _head_deltas(dy_ref[rows, :] * y_ref[rows, :], m0, m1)
            kb = k2[band, :]
            e, mx, l = _softmax_terms(mode, s_buf[slot], sink)
            inv = 1.0 / l
            pn = e * inv
            ds = pn * ((dp_buf[slot] if dp_ahead else _dot_nt(do_st, v2[band, :])) - delta)
            if mode == "A":
                dbias_ref[...] += ds
            else:
                part = jnp.exp(sink - mx) * inv * delta
                for h in range(GROUP):
                    dsink = dsink - jnp.where(lane8 == h, jnp.sum(part[h * QTILE:(h + 1) * QTILE]), 0.0)
            dsb = ds.astype(BF16)
            dv[band, :] += _dot_tn(pn.astype(BF16), do_st)
            dk[band, :] += _dot_tn(dsb, q_st)
            dq_st = _dot(dsb, kb)
            if mode == "A":
                heads = [dq_st[h * QTILE:(h + 1) * QTILE] for h in range(GROUP)]
                dq_ref[rows, :] = _unstack_heads(mode, first_kv, heads, m0, m1)
            else:
                dqs[pl.ds(pl.multiple_of(m * STACK, STACK), STACK), :] = dq_st
            return dsink

        ahead(0, 0)

        def pair(j, dsink):
            ahead(2 * j + 1, 1)
            dsink = tile(2 * j, 0, dsink)
            ahead(jnp.minimum(2 * j + 2, nt - 1), 0)
            return tile(2 * j + 1, 1, dsink)

        dsink = lax.fori_loop(0, nt // 2, pair, jnp.zeros((8, LANES), F32))
        dsink_ref[...] = dsink

        rt = 2 * QTILE
        dgq = jnp.zeros((1, QW), F32)
        dgk = jnp.zeros((1, kw), F32)
        for i in range(s_len // rt):
            rows = pl.ds(i * rt, rt)
            src = pl.ds(padk + i * rt, rt)
            gq_v, gk_v = gq_ref[...], gk_ref[...]
            _, qh, qr = _head_norm(q_ref[rows, :], gq_v, m0, m1)
            _, kh, kr = _head_norm(k_ref[rows, :], gk_v, m0, m1)
            if mode == "A":
                dqn = dq_ref[rows, :] * (HEAD_DIM ** -0.5)
            else:
                dqn = jnp.concatenate(
                    [_unstack_heads(mode, first_kv, [dqs[pl.ds((2 * i + half) * STACK + h * QTILE, QTILE), :]
                                                     for h in range(GROUP)], m0, m1)
                     for half in range(2)], axis=0) * (HEAD_DIM ** -0.5)
            dq_raw, dgq_i = _head_norm_bwd(qh, qr, gq_v, dqn, m0, m1)
            dk_raw, dgk_i = _head_norm_bwd(kh, kr, gk_v, dk[src, :], m0, m1)
            dvn = dv[src, :]
            dq_ref[rows, :] = dq_raw
            if mode == "A":
                dk_ref[rows, :] = dk_raw
                dv_ref[rows, :] = dvn
            else:
                @pl.when(group == 0)
                def _():
                    dk_ref[rows, :] = dk_raw
                    dv_ref[rows, :] = dvn

                @pl.when(group != 0)
                def _():
                    dk_ref[rows, :] += dk_raw
                    dv_ref[rows, :] += dvn
            dgq, dgk = dgq + dgq_i, dgk + dgk_i
        dgq_ref[...] = jnp.broadcast_to(dgq, (8, QW))
        dgk_ref[...] = jnp.broadcast_to(dgk, (8, kw))

    ng = B_Q_HEADS // GROUP
    blk = lambda w, f: pl.BlockSpec((s_len, w), f)
    small = lambda w: pl.BlockSpec((None, None, 8, w), lambda b, g: (b, g, 0, 0))
    own = lambda b, g: (b, g)
    kvmap = own if mode == "A" else (lambda b, g: (b, 0))
    pad_f32 = pltpu.VMEM((s_len + padk, kw), F32)
    pad_bf = pltpu.VMEM((s_len + padk, kw), BF16)
    stack_bf = pltpu.VMEM(_q_scratch_shape(mode, s_len), BF16)
    outs = pl.pallas_call(
        body, name=name, grid=(bl, ng),
        out_shape=[jax.ShapeDtypeStruct((t, ng * QW), F32), jax.ShapeDtypeStruct((t, kvw), F32),
                   jax.ShapeDtypeStruct((t, kvw), F32),
                   jax.ShapeDtypeStruct((bl, ng, 8, QW), F32), jax.ShapeDtypeStruct((bl, ng, 8, kw), F32),
                   jax.ShapeDtypeStruct((bl, ng * STACK, bw), F32), jax.ShapeDtypeStruct((bl, ng, 8, LANES), F32)],
        in_specs=[blk(QW, qmap), blk(kw, kmap), blk(kw, vmap),
                  pl.BlockSpec((1, QW), lambda b, g: (0, 0)), pl.BlockSpec((1, kw), lambda b, g: (0, 0)),
                  pl.BlockSpec((STACK, bw), lambda b, g: (g, 0)),
                  pl.BlockSpec(memory_space=pltpu.SMEM),
                  blk(QW, own), blk(QW, own)],
        out_specs=[blk(QW, own), blk(kw, kvmap), blk(kw, kvmap), small(QW), small(kw),
                   pl.BlockSpec((None, STACK, bw), lambda b, g: (b, g, 0)), small(LANES)],
        scratch_shapes=[stack_bf, pad_bf, pad_bf, stack_bf,
                        pltpu.VMEM((8, LANES) if mode == "A" else _q_scratch_shape(mode, s_len), F32),
                        pad_f32, pad_f32, pltpu.VMEM((2, STACK, bw), F32),
                        pltpu.VMEM((2, STACK, bw) if dp_ahead else (8, LANES), F32)],
        compiler_params=_params(("arbitrary", "arbitrary")),
    )(qkv, qkv, qkv, gq, gk, bias.reshape(-1, bw), sinks, y, dy)
    outs = list(outs)
    outs[5] = outs[5].reshape(bl, B_Q_HEADS, QTILE, bw)
    return outs


def _band_geometry(prev):
    bw = QTILE + prev * CHUNK
    i = np.arange(QTILE)[:, None]
    j = np.arange(bw)[None, :]
    dist = i + prev * CHUNK - j
    valid = (j // CHUNK >= i // CHUNK) & (j // CHUNK <= i // CHUNK + prev)
    return dist, valid


A_VAR0 = (A_PREV * CHUNK - A_MAX_REL) // LANES * LANES


A_NVAR = QTILE + A_PREV * CHUNK - A_VAR0


def _skew_rows(x, sign):
    rows, n = x.shape
    row = lax.broadcasted_iota(jnp.int32, x.shape, 0)
    b = 1
    while b < rows:
        x = jnp.where((row & b) != 0, pltpu.roll(x, (sign * b) % n, 1), x)
        b *= 2
    return x


def _rel_bias_expand(table, name):
    _, valid = _band_geometry(A_PREV)
    bw = valid.shape[1]
    valid_f = jnp.asarray(valid.astype(np.float32))
    rev = jnp.flip(table[:, 1:], axis=1).reshape(A_HEADS, 1, A_NVAR)

    def body(rev_ref, valid_ref, o_ref):
        rowv = jnp.broadcast_to(rev_ref[...], (QTILE, A_NVAR))
        top = rowv[:, 0:1]
        var = _skew_rows(rowv, 1)
        row = lax.broadcasted_iota(jnp.int32, (QTILE, A_NVAR), 0)
        colv = lax.broadcasted_iota(jnp.int32, (QTILE, A_NVAR), 1)
        var = jnp.where(colv < row, top, var)
        ok = valid_ref[...] > 0.5
        o_ref[:, :A_VAR0] = jnp.where(ok[:, :A_VAR0], top, NEG_INF)
        o_ref[:, A_VAR0:] = jnp.where(ok[:, A_VAR0:], var, NEG_INF)

    return pl.pallas_call(
        body, name=name, grid=(A_HEADS,),
        out_shape=jax.ShapeDtypeStruct((A_HEADS, QTILE, bw), F32),
        in_specs=[pl.BlockSpec((None, 1, A_NVAR), lambda h: (h, 0, 0)), pl.BlockSpec((QTILE, bw), lambda h: (0, 0))],
        out_specs=pl.BlockSpec((None, QTILE, bw), lambda h: (h, 0, 0)),
        compiler_params=_params(("arbitrary",)),
    )(rev, valid_f)


def _rel_bias_grad(dbias, name):
    bl = dbias.shape[0]
    bw = dbias.shape[-1]

    def body(db_ref, o_ref):
        g = db_ref[0]
        for b in range(1, bl):
            g = g + db_ref[b]
        sk = _skew_rows(g[:, A_VAR0:], -1)
        row = lax.broadcasted_iota(jnp.int32, (QTILE, A_NVAR), 0)
        colv = lax.broadcasted_iota(jnp.int32, (QTILE, A_NVAR), 1)
        wrapped = (row + colv) >= A_NVAR
        main = jnp.sum(jnp.where(wrapped, 0.0, sk), axis=0, keepdims=True)
        top = jnp.sum(g[:, :A_VAR0]) + jnp.sum(jnp.where(wrapped, sk, 0.0))
        o_ref[:, :A_NVAR] = jnp.broadcast_to(main, (8, A_NVAR))
        o_ref[:, A_NVAR:] = jnp.full((8, LANES), top, F32)

    out = pl.pallas_call(
        body, name=name, grid=(A_HEADS,),
        out_shape=jax.ShapeDtypeStruct((A_HEADS, 8, A_NVAR + LANES), F32),
        in_specs=[pl.BlockSpec((bl, None, QTILE, bw), lambda h: (0, h, 0, 0))],
        out_specs=pl.BlockSpec((None, 8, A_NVAR + LANES), lambda h: (h, 0, 0)),
        compiler_params=_params(("arbitrary",)),
    )(dbias)
    main, top = out[:, 0, :A_NVAR], out[:, 0, A_NVAR]
    fm = jnp.flip(main, axis=1)
    return jnp.concatenate([jnp.zeros((A_HEADS, 1), F32), fm[:, :-1], fm[:, -1:] + top[:, None]], axis=1)


def _alibi_bias():
    dist, valid = _band_geometry(B_PREV)
    slopes = np.array([2.0 ** (-8.0 * (h + 1) / B_Q_HEADS) for h in range(B_Q_HEADS)], dtype=np.float32)
    bias = -slopes[:, None, None] * np.abs(dist).astype(np.float32)[None]
    return jnp.asarray(np.where(valid[None], bias, np.float32(NEG_INF)).astype(np.float32))


SMALL_NAMES = ("ffn1_norm", "mix_norm", "ffn2_norm", "ple_norm", "a_q_norm", "a_k_norm", "b_q_norm", "b_k_norm",
               "a_rel_bias", "b_sinks", "loss")


def _pack_small(vals):
    rows = []
    for nme in SMALL_NAMES:
        v = vals[nme].astype(F32)
        if nme == "a_rel_bias":
            v = jnp.pad(v.reshape(A_HEADS, -1), ((0, 0), (0, 3 * LANES - (2 * A_MAX_REL + 1))))
        v = v.reshape(-1)
        v = jnp.pad(v, (0, (-v.shape[0]) % LANES))
        rows.append(v.reshape(-1, LANES))
    out = jnp.concatenate(rows, axis=0)
    return jnp.pad(out, ((0, (-out.shape[0]) % 8), (0, 0)))


def _unpack_small(packed, shapes):
    out, r = {}, 0
    for nme in SMALL_NAMES:
        shp = shapes[nme]
        if nme == "a_rel_bias":
            nr = A_HEADS * 3
            out[nme] = packed[r:r + nr].reshape(A_HEADS, 3 * LANES)[:, :2 * A_MAX_REL + 1].reshape(shp)
        else:
            size = int(np.prod(shp)) if shp else 1
            nr = -(-size // LANES)
            out[nme] = packed[r:r + nr].reshape(-1)[:size].reshape(shp)
        r += nr
    return out


BIG_NAMES = ("ffn1_w_gu", "ffn1_w_down", "w_in", "w_gate", "w_proj_a", "w_proj_b", "w_out",
             "ffn2_w_gu", "ffn2_w_down", "w_ple_gate", "w_ple_proj")
WEIGHT_ORDER = ("ffn1_norm", "ffn1_w_gu", "ffn1_w_down", "mix_norm", "w_in", "a_q_norm", "a_k_norm", "a_rel_bias",
                "b_q_norm", "b_k_norm", "b_sinks", "w_gate", "w_proj_a", "w_proj_b", "w_out", "ffn2_norm",
                "ffn2_w_gu", "ffn2_w_down", "ple_norm", "w_ple_gate", "w_ple_proj")


TRANSPOSED = ("ffn1_w_gu", "ffn2_w_gu", "w_in")


def _local(a, nme):
    return a[0].T if nme in TRANSPOSED else a[0]


def _full_cols(wg):
    nb, k, n = wg.shape
    return jnp.transpose(wg, (1, 0, 2)).reshape(k, nb * n)


def _step(x, p, target, w, m, v, packed):
    bl, s_len, d = x.shape
    t = bl * s_len
    h0 = x.reshape(t, d)
    pt = p.reshape(t, p.shape[-1])
    tgt = target.reshape(t, d)

    g_ffn1, g_mix, g_ffn2, g_ple = w["ffn1_norm"], w["mix_norm"], w["ffn2_norm"], w["ple_norm"]
    tiled = lambda a, width: jnp.tile(a.reshape(1, HEAD_DIM), (1, width // HEAD_DIM))
    gqa, gka = tiled(w["a_q_norm"], QW), tiled(w["a_k_norm"], _kv_width("A"))
    gqb, gkb = tiled(w["b_q_norm"], QW), tiled(w["b_k_norm"], _kv_width("B"))
    sinks = w["b_sinks"].reshape(B_Q_HEADS)
    bias_b = _alibi_bias()

    ffn1_names = ("ffn1_w_gu", "ffn1_w_down")
    shard = {nme: _local(w[nme], nme).astype(BF16) for nme in ffn1_names}
    send1, recv1, bufs, token = _gather_start([shard[nme] for nme in ffn1_names], h0, "gather_start_ffn1")
    zero = token[0, 0]
    shard.update({nme: (_local(w[nme], nme) + zero).astype(BF16) for nme in BIG_NAMES if nme not in ffn1_names})
    bias_a = _rel_bias_expand(w["a_rel_bias"][0] + zero, "rel_bias_expand")
    send2, recv2, bufs, token = _gather_pass(send1, recv1, bufs, bias_a, "gather_pass_ffn1")
    packed[:] = [a + zero for a in packed]
    early = shard["ffn2_w_gu"][:8, :LANES].astype(F32) + sum(a[:8] for a in packed)
    wgu1, wd1 = _gather_wait(send2, recv2, bufs, early, "gather_wait_ffn1")
    nf = wgu1.shape[1]
    wd1 = wd1.reshape(N_DEV // 2, nf, d)
    mixer_names = ("w_in", "w_gate")
    rest_names = ("w_proj_a", "w_proj_b", "w_out", "ffn2_w_gu", "ffn2_w_down", "w_ple_gate", "w_ple_proj")
    send1, recv1, bufs, token = _gather_start([shard[nme] for nme in mixer_names], wgu1, "gather_start_mixer")
    rsend1, rrecv1, rest_bufs, token = _gather_start([shard[nme] for nme in rest_names], token, "gather_start_rest")

    h1, gu1 = _ffn_fwd(h0, g_ffn1 + token[0, 0], wgu1, wd1, "ffn1_fwd")
    send2, recv2, bufs, token = _gather_pass(send1, recv1, bufs, h1, "gather_pass_mixer")
    win, wgate = _gather_wait(send2, recv2, bufs, token, "gather_wait_mixer")
    win, wgate = win.reshape(IN_COLS, d), _full_cols(wgate)
    un, qkv, gate = _proj_fwd(h1, g_mix, win, wgate, "proj_fwd")
    ya = _attn_fwd("A", qkv, gqa, gka, bias_a, sinks, bl, s_len, "attn_a_fwd")
    rsend2, rrecv2, rest_bufs, token = _gather_pass(rsend1, rrecv1, rest_bufs, ya, "gather_pass_rest")
    yb = _attn_fwd("B", qkv, gqb + token[0, 0], gkb, bias_b, sinks, bl, s_len, "attn_b_fwd")
    gathered = dict(zip(rest_names, _gather_wait(rsend2, rrecv2, rest_bufs, yb, "gather_wait_rest")))
    wgu2 = gathered["ffn2_w_gu"]
    wd2 = gathered["ffn2_w_down"].reshape(N_DEV // 2, nf, d)
    wpa = _full_cols(gathered["w_proj_a"])
    wpb = _full_cols(gathered["w_proj_b"])
    wpe = _full_cols(gathered["w_ple_proj"])
    wout = gathered["w_out"].reshape(d, d)
    wpg = gathered["w_ple_gate"].reshape(d, d)
    h2, merged, pa, pb = _merge_fwd(h1, ya, yb, gate, wpa, wpb, wout, "merge_fwd")
    h3, gu2 = _ffn_fwd(h2, g_ffn2, wgu2, wd2, "ffn2_fwd")
    dh3, dz4, dpp, n4, dg_ple, loss_part = _ple_loss(h3, g_ple, pt, tgt, wpg, wpe, "ple_loss")

    xi, yi, ci = _place()
    me = jnp.stack([4 * xi + 2 * yi + ci, 2 * xi + yi]).astype(jnp.int32)
    g32, g16, big, pairs = {}, {}, {}, {}

    def keep(nme, pair, rows=None):
        for store, g in zip((g32, g16), pair):
            store[nme] = g if rows is None else g.reshape(N_DEV, rows, d)

    def start(names, after, tag):
        send, recv, parts, lands, token = _scatter_start([g16[nme] for nme in names], after, "grads_start_" + tag)
        return names, send, recv, parts, lands, token

    def start_two_level(names, after, tag):
        views = [g16[nme].reshape((4, 2) + g16[nme].shape[1:]) for nme in names]
        for nme, got in zip(names, _pair_exchange(views, "grads_pair_" + tag)):
            pairs[nme] = got.reshape((4,) + got.shape[2:])
        sums = [_pair_sum(g32[nme], pairs[nme], me, "pair_sum_" + nme) for nme in names]
        send, recv, parts, lands, token = _scatter_start(sums, after, "grads_start_" + tag, SAME_CORE_CHIPS)
        return names, send, recv, parts, lands, token

    def finish(state, after, tag):
        names, send, recv, parts, lands, _ = state
        relations = SAME_CORE_CHIPS if names[0] in pairs else ALL_PEERS
        lands = _scatter_wait(send, recv, parts, lands, after, "grads_wait_" + tag, relations)
        return names, lands

    def adam(done, dep):
        for nme, land in zip(*done):
            outs = _final_adam(g32[nme], land, _local(w[nme], nme), _local(m[nme], nme), _local(v[nme], nme), me, dep,
                               "adam_" + nme, pairs.get(nme))
            big[nme] = [(o.T if nme in TRANSPOSED else o)[None] for o in outs]

    keep("w_ple_gate", _dw(n4, dz4, 1, d, "dw_ple_gate"), d // N_DEV)
    keep("w_ple_proj", _dw(pt, dpp, N_DEV, d // N_DEV, "dw_ple_proj"))
    early = [(start(("w_ple_gate", "w_ple_proj"), dh3, "ple"), "ple")]

    dh2, dgu2, a2, n3, dg_ffn2 = _ffn_bwd(dh3, h2, g_ffn2 + early[-1][0][-1][0, 0], gu2, wgu2, wd2, "ffn2_bwd")
    keep("ffn2_w_down", _dw(a2, dh3, N_DEV // 2, d, "dw_ffn2_down", 0.5), nf // 2)
    early.append((start(("ffn2_w_down",), dh2, "ffn2_down"), "ffn2_down"))
    keep("ffn2_w_gu", _dw(dgu2, n3, N_DEV, d, "dw_ffn2_gu", dep=early[-1][0][-1]))
    flight = start(("ffn2_w_gu",), dh2, "ffn2")

    dpa, dpb, dzg, dya, dyb = _merge_bwd(dh2, pa, pb, gate, wpa, wpb, wout, "merge_bwd")
    keep("w_out", _dw(merged, dh2, 1, d, "dw_out"), d // N_DEV)
    keep("w_proj_a", _dw(ya, dpa, N_DEV, d // N_DEV, "dw_proj_a"))
    keep("w_proj_b", _dw(yb, dpb, N_DEV, d // N_DEV, "dw_proj_b"))
    keep("w_gate", _dw(un, dzg, N_DEV, 2 * d // N_DEV, "dw_gate"))

    tok = flight[-1][0, 0]
    dqa, dka, dva, dgqa, dgka, dbias, _ = _attn_bwd("A", qkv, gqa + tok, gka, bias_a, sinks, ya, dya, bl, s_len,
                                                     "attn_a_bwd")
    dqb, dkb, dvb, dgqb, dgkb, _, dsink = _attn_bwd("B", qkv, gqb, gkb, bias_b, sinks, yb, dyb, bl, s_len, "attn_b_bwd")
    dqkv = [dqa, dka, dva, dqb, dkb, dvb]
    dtab = _rel_bias_grad(dbias, "rel_bias_grad")

    dh1, dg_mix = _proj_bwd(dh2, h1, g_mix, dzg, dqkv, win, wgate, "proj_bwd")
    keep("w_in", _dw_rows(dqkv, un, "dw_in"), IN_COLS // N_DEV)
    waiting = [finish(state, g32["w_in"], tag) for state, tag in early]
    done = finish(flight, waiting[-1][1][0], "ffn2")
    flight = start(("w_out", "w_proj_a", "w_proj_b", "w_gate", "w_in"), done[1][0], "mixer")
    waiting.append(done)

    dh0, dgu1, a1, n1, dg_ffn1 = _ffn_bwd(dh1, h0, g_ffn1 + flight[-1][0, 0], gu1, wgu1, wd1, "ffn1_bwd")
    keep("ffn1_w_down", _dw(a1, dh1, N_DEV // 2, d, "dw_ffn1_down", 0.5), nf // 2)
    done = finish(flight, g32["ffn1_w_down"], "mixer")
    flight = start(("ffn1_w_down",), done[1][0], "ffn1_down")
    waiting.append(done)

    keep("ffn1_w_gu", _dw(dgu1, n1, N_DEV, d, "dw_ffn1_gu", dep=flight[-1]))
    done = finish(flight, g32["ffn1_w_gu"], "ffn1_down")
    flight = start_two_level(("ffn1_w_gu",), done[1][0], "ffn1_gu")
    for group in waiting + [done]:
        adam(group, flight[-1])
    behind = 0.0 * big["ffn1_w_down"][0][0, 0, :1]
    smalls = (dg_ffn1, dg_mix, dg_ffn2, dg_ple + behind, dgqa, dgka, dgqb, dgkb, dtab, dsink)
    return dh0, loss_part, big, smalls, flight, finish, adam


def kernel(x, p, ffn1_norm, ffn1_w_gu, ffn1_w_down, mix_norm, w_in, a_q_norm, a_k_norm, a_rel_bias, b_q_norm, b_k_norm, b_sinks, w_gate, w_proj_a, w_proj_b, w_out, ffn2_norm, ffn2_w_gu, ffn2_w_down, ple_norm, w_ple_gate, w_ple_proj, loss_target, m_ffn1_norm, m_ffn1_w_gu, m_ffn1_w_down, m_mix_norm, m_w_in, m_a_q_norm, m_a_k_norm, m_a_rel_bias, m_b_q_norm, m_b_k_norm, m_b_sinks, m_w_gate, m_w_proj_a, m_w_proj_b, m_w_out, m_ffn2_norm, m_ffn2_w_gu, m_ffn2_w_down, m_ple_norm, m_w_ple_gate, m_w_ple_proj, v_ffn1_norm, v_ffn1_w_gu, v_ffn1_w_down, v_mix_norm, v_w_in, v_a_q_norm, v_a_k_norm, v_a_rel_bias, v_b_q_norm, v_b_k_norm, v_b_sinks, v_w_gate, v_w_proj_a, v_w_proj_b, v_w_out, v_ffn2_norm, v_ffn2_w_gu, v_ffn2_w_down, v_ple_norm, v_w_ple_gate, v_w_ple_proj):
    w = dict(ffn1_norm=ffn1_norm, ffn1_w_gu=ffn1_w_gu, ffn1_w_down=ffn1_w_down, mix_norm=mix_norm, w_in=w_in,
             a_q_norm=a_q_norm, a_k_norm=a_k_norm, a_rel_bias=a_rel_bias, b_q_norm=b_q_norm, b_k_norm=b_k_norm,
             b_sinks=b_sinks, w_gate=w_gate, w_proj_a=w_proj_a, w_proj_b=w_proj_b, w_out=w_out, ffn2_norm=ffn2_norm,
             ffn2_w_gu=ffn2_w_gu, ffn2_w_down=ffn2_w_down, ple_norm=ple_norm, w_ple_gate=w_ple_gate,
             w_ple_proj=w_ple_proj)
    m = dict(ffn1_norm=m_ffn1_norm, ffn1_w_gu=m_ffn1_w_gu, ffn1_w_down=m_ffn1_w_down, mix_norm=m_mix_norm,
             w_in=m_w_in, a_q_norm=m_a_q_norm, a_k_norm=m_a_k_norm, a_rel_bias=m_a_rel_bias, b_q_norm=m_b_q_norm,
             b_k_norm=m_b_k_norm, b_sinks=m_b_sinks, w_gate=m_w_gate, w_proj_a=m_w_proj_a, w_proj_b=m_w_proj_b,
             w_out=m_w_out, ffn2_norm=m_ffn2_norm, ffn2_w_gu=m_ffn2_w_gu, ffn2_w_down=m_ffn2_w_down,
             ple_norm=m_ple_norm, w_ple_gate=m_w_ple_gate, w_ple_proj=m_w_ple_proj)
    v = dict(ffn1_norm=v_ffn1_norm, ffn1_w_gu=v_ffn1_w_gu, ffn1_w_down=v_ffn1_w_down, mix_norm=v_mix_norm,
             w_in=v_w_in, a_q_norm=v_a_q_norm, a_k_norm=v_a_k_norm, a_rel_bias=v_a_rel_bias, b_q_norm=v_b_q_norm,
             b_k_norm=v_b_k_norm, b_sinks=v_b_sinks, w_gate=v_w_gate, w_proj_a=v_w_proj_a, w_proj_b=v_w_proj_b,
             w_out=v_w_out, ffn2_norm=v_ffn2_norm, ffn2_w_gu=v_ffn2_w_gu, ffn2_w_down=v_ffn2_w_down,
             ple_norm=v_ple_norm, w_ple_gate=v_w_ple_gate, w_ple_proj=v_w_ple_proj)
    bl, s_len, d = x.shape

    zero1 = jnp.zeros((1,), F32)
    pk = lambda src: _pack_small({**{nme: src[nme] for nme in SMALL_NAMES if nme != "loss"}, "loss": zero1})
    packed = [pk(w), pk(m), pk(v)]
    dh0, loss_part, big, smalls, flight, finish, adam = _step(x, p[0], loss_target, w, m, v, packed)
    dg_ffn1, dg_mix, dg_ffn2, dg_ple, dgqa, dgka, dgqb, dgkb, dtab, dsink = smalls

    fold = lambda a: a[:, :, 0, :].reshape(-1, HEAD_DIM).sum(axis=0)
    small_part = dict(
        ffn1_norm=dg_ffn1, mix_norm=dg_mix, ffn2_norm=dg_ffn2, ple_norm=dg_ple,
        a_q_norm=fold(dgqa), a_k_norm=fold(dgka), b_q_norm=fold(dgqb), b_k_norm=fold(dgkb),
        a_rel_bias=dtab,
        b_sinks=dsink.sum(axis=0)[:, 0, :GROUP].reshape(B_Q_HEADS),
        loss=loss_part[0, :1])
    shapes = {nme: w[nme].shape for nme in SMALL_NAMES if nme != "loss"}
    shapes["loss"] = ()
    sg, sd, sm, sv = _small_allreduce_adam(_pack_small(small_part), *packed, "small_allreduce_adam")
    adam(finish(flight, sg, "ffn1_gu"), sg)
    sg, sd, sm, sv = (_unpack_small(a, shapes) for a in (sg, sd, sm, sv))

    def pick(i):
        out = []
        for nme in WEIGHT_ORDER:
            out.append(big[nme][i] if nme in big else (sg, sd, sm, sv)[i][nme])
        return out

    return (sg["loss"], dh0.reshape(bl, s_len, d), *pick(0), *pick(1), *pick(2), *pick(3))
```

```python
import jax
import jax.numpy as jnp
import numpy as np
from jax import lax
from jax.experimental import pallas as pl
from jax.experimental.pallas import tpu as pltpu

F32 = jnp.float32
BF16 = jnp.bfloat16

CHUNK = 64
HEAD_DIM = 64
A_HEADS = 8
A_PREV = 8
A_MAX_REL = 128
B_Q_HEADS = 8
B_KV_HEADS = 2
B_PREV = 2
A_WIDTH = A_HEADS * HEAD_DIM
B_Q_WIDTH = B_Q_HEADS * HEAD_DIM
B_KV_WIDTH = B_KV_HEADS * HEAD_DIM
IN_COLS = 3 * A_WIDTH + B_Q_WIDTH + 2 * B_KV_WIDTH
EPS = 1e-6
NEG_INF = -1e30
ADAM_LR = 0.001
ADAM_B1 = 0.9
ADAM_B2 = 0.999
ADAM_EPS = 1e-08
ADAM_WD = 0.01
ADAM_STEP = 10

N_DEV = 8
LANES = 128
QTILE = 2 * CHUNK
VMEM_LIMIT = 56 * 1024 * 1024
DW_VMEM_LIMIT = 60 * 1024 * 1024
ADAM_TILE_ELEMS = 256 * 1024

MESH_ID = pl.DeviceIdType.MESH
ANY = pl.BlockSpec(memory_space=pl.ANY)
HBM = pl.BlockSpec(memory_space=pltpu.HBM)
SEM = pl.BlockSpec(memory_space=pltpu.SEMAPHORE)
SIDE_EFFECT = pltpu.SideEffectType.DATAFLOW_SIDE_EFFECTING


def _dot(a, b):
    return jnp.dot(a, b, preferred_element_type=F32)


def _dot_nt(a, b):
    return lax.dot_general(a, b, (((1,), (1,)), ((), ())), preferred_element_type=F32)


def _dot_tn(a, b):
    return lax.dot_general(a, b, (((0,), (0,)), ((), ())), preferred_element_type=F32)


def _params(sem=None, vmem=VMEM_LIMIT):
    return pltpu.CompilerParams(dimension_semantics=sem, vmem_limit_bytes=vmem)


def _row_tile(t, want):
    while t % want:
        want //= 2
    return want


def _place():
    return lax.axis_index("x"), lax.axis_index("y"), lax.axis_index("c")


def _gather_level(bufs, send_sems, recv_sems, level, shards=None):
    x, y, c = _place()
    me, sib = (x, y, c), (x, y, 1 - c)
    chips = [(1 - x, y), (x, 1 - y), (1 - x, 1 - y)]

    def copy(w, k, block, to):
        px, py, pc = block
        rows = bufs[w].at[4 * px + 2 * py + pc]
        src = shards[w] if shards is not None and block is me else rows
        return pltpu.make_async_remote_copy(src_ref=src, dst_ref=rows, send_sem=send_sems.at[k], recv_sem=recv_sems.at[k],
                                            device_id=to, device_id_type=MESH_ID)

    n = len(bufs)
    own = []
    if level == 1:
        own = [pltpu.make_async_copy(bufs[w].at[4 * x + 2 * y + c] if shards is None else shards[w],
                                     bufs[w].at[4 * x + 2 * y + c], send_sems.at[4 * n + w]) for w in range(n)]
    out, arriving = [], []
    for w in range(len(bufs)):
        if level == 1:
            out.append(copy(w, 4 * w, me, sib))
            arriving.append(copy(w, 4 * w, sib, me))
        for j, chip in enumerate(chips):
            if level == 1:
                out.append(copy(w, 4 * w + 1 + j, me, (*chip, c)))
                arriving.append(copy(w, 4 * w + 1 + j, (*chip, c), me))
            else:
                out.append(copy(w, 3 * w + j, (*chip, c), sib))
                arriving.append(copy(w, 3 * w + j, (*chip, 1 - c), me))
    return out, arriving, own


def _split_call(body, name, bufs, sems_in, after, n_sems_out, token, extra=()):
    n = len(bufs)
    out_shape = [pltpu.SemaphoreType.DMA((n_sems_out,))] * (2 if n_sems_out else 0)
    out_shape += [pltpu.HBM(a.shape, a.dtype) for a in bufs]
    out_specs = [SEM] * (2 if n_sems_out else 0) + [HBM] * n
    if token:
        out_shape.append(jax.ShapeDtypeStruct((8, LANES), F32))
        out_specs.append(pl.BlockSpec(memory_space=pltpu.VMEM))
    first = 2 if n_sems_out else 0
    return pl.pallas_call(
        body, name=name, out_shape=tuple(out_shape),
        in_specs=[HBM] * (n + len(extra)) + [SEM] * len(sems_in) + [ANY], out_specs=tuple(out_specs),
        input_output_aliases={i: first + i for i in range(n)},
        compiler_params=pltpu.CompilerParams(has_side_effects=SIDE_EFFECT),
    )(*bufs, *extra, *sems_in, after)


def _gather_start(shards, after, name):
    n = len(shards)
    hbm = lambda a: pltpu.with_memory_space_constraint(a, pltpu.HBM)
    bufs = [hbm(lax.empty((N_DEV,) + s.shape, s.dtype)) for s in shards]

    def body(*refs):
        out, _, own = _gather_level(refs[:n], refs[2 * n + 1], refs[2 * n + 2], 1, shards=refs[n:2 * n])
        for cp in own + out:
            cp.start()
        refs[-1][...] = jnp.zeros_like(refs[-1])

    outs = _split_call(body, name, bufs + [hbm(s) for s in shards], [], after, 5 * n, True)
    return outs[0], outs[1], list(outs[2:2 + 2 * n]), outs[-1]


def _gather_pass(send1, recv1, bufs_and_shards, after, name):
    n = len(bufs_and_shards) // 2
    bufs = bufs_and_shards

    def body(*refs):
        refs = refs[:n] + refs[2 * n:]
        out1, in1, own = _gather_level(refs[:n], refs[n], refs[n + 1], 1)
        out2, _, _ = _gather_level(refs[:n], refs[n + 3], refs[n + 4], 2)
        for cp in in1:
            cp.wait_recv()
        for cp in out2:
            cp.start()
        for cp in out1:
            cp.wait_send()
        for cp in own:
            cp.wait()
        refs[-1][...] = jnp.zeros_like(refs[-1])

    outs = _split_call(body, name, bufs, [send1, recv1], after, 3 * n, True)
    return outs[0], outs[1], list(outs[2:2 + n]), outs[-1]


def _gather_wait(send2, recv2, bufs, after, name):
    n = len(bufs)

    def body(*refs):
        out2, in2, _ = _gather_level(refs[:n], refs[n], refs[n + 1], 2)
        for cp in in2:
            cp.wait_recv()
        for cp in out2:
            cp.wait_send()

    return list(_split_call(body, name, bufs, [send2, recv2], after, 0, False))


ALL_PEERS = tuple(range(1, N_DEV))
SAME_CORE_CHIPS = (2, 4, 6)


def _scatter_copies(parts, lands, send_sems, recv_sems, relations):
    x, y, c = _place()
    ns = len(relations)
    cps = []
    for w, (part, land) in enumerate(zip(parts, lands)):
        for i, k in enumerate(relations):
            px, py, pc = x ^ ((k >> 2) & 1), y ^ ((k >> 1) & 1), c ^ (k & 1)
            block = 4 * px + 2 * py + pc if part.shape[0] == N_DEV else 2 * px + py
            cps.append(pltpu.make_async_remote_copy(
                src_ref=part.at[block], dst_ref=land.at[i],
                send_sem=send_sems.at[ns * w + i], recv_sem=recv_sems.at[ns * w + i],
                device_id=(px, py, pc), device_id_type=MESH_ID))
    return cps


def _scatter_start(parts, after, name, relations=ALL_PEERS):
    n = len(parts)
    ns = len(relations)

    def body(*refs):
        ins, lands = refs[:n], refs[n:2 * n]
        send_sems, recv_sems = refs[2 * n + 1], refs[2 * n + 2]
        token = refs[-1]
        for cp in _scatter_copies(ins, lands, send_sems, recv_sems, relations):
            cp.start()
        token[...] = jnp.zeros_like(token)

    land_shapes = [(ns,) + p.shape[1:] for p in parts]
    in_hbm = [pltpu.with_memory_space_constraint(p, pltpu.HBM) for p in parts]
    in_hbm += [pltpu.with_memory_space_constraint(lax.empty(s, p.dtype), pltpu.HBM) for s, p in zip(land_shapes, parts)]
    outs = pl.pallas_call(
        body, name=name,
        out_shape=(pltpu.SemaphoreType.DMA((ns * n,)), pltpu.SemaphoreType.DMA((ns * n,)),
                   *[pltpu.HBM(p.shape, p.dtype) for p in parts],
                   *[pltpu.HBM(s, p.dtype) for s, p in zip(land_shapes, parts)],
                   jax.ShapeDtypeStruct((8, LANES), F32)),
        in_specs=[HBM] * (2 * n) + [ANY],
        out_specs=(SEM, SEM, *[HBM] * (2 * n), pl.BlockSpec(memory_space=pltpu.VMEM)),
        input_output_aliases={i: 2 + i for i in range(2 * n)},
        compiler_params=pltpu.CompilerParams(has_side_effects=SIDE_EFFECT),
    )(*in_hbm, after)
    return outs[0], outs[1], list(outs[2:2 + n]), list(outs[2 + n:2 + 2 * n]), outs[-1]


def _scatter_wait(send_sems, recv_sems, parts, lands, after, name, relations=ALL_PEERS):
    n = len(parts)

    def body(*refs):
        ins, lnd = refs[:n], refs[n:2 * n]
        for cp in _scatter_copies(ins, lnd, refs[2 * n], refs[2 * n + 1], relations):
            cp.wait_send()
            cp.wait_recv()

    outs = pl.pallas_call(
        body, name=name,
        out_shape=tuple(pltpu.HBM(a.shape, a.dtype) for a in parts + lands),
        in_specs=[HBM] * (2 * n) + [SEM, SEM, ANY],
        out_specs=tuple([HBM] * (2 * n)),
        input_output_aliases={i: i for i in range(2 * n)},
        compiler_params=pltpu.CompilerParams(has_side_effects=SIDE_EFFECT),
    )(*parts, *lands, send_sems, recv_sems, after)
    return list(outs[n:])


def _pair_exchange(parts, name):
    n = len(parts)

    def body(*refs):
        ins, outs = refs[:n], refs[n:2 * n]
        send_sems, recv_sems = refs[2 * n:]
        x, y, c = _place()
        cps = [pltpu.make_async_remote_copy(
            src_ref=ins[w].at[:, pl.ds(1 - c, 1)], dst_ref=outs[w], send_sem=send_sems.at[w], recv_sem=recv_sems.at[w],
            device_id=(x, y, 1 - c), device_id_type=MESH_ID) for w in range(n)]
        for cp in cps:
            cp.start()
        for cp in cps:
            cp.wait()

    return pl.pallas_call(
        body, name=name,
        out_shape=[jax.ShapeDtypeStruct((4, 1) + p.shape[2:], p.dtype) for p in parts],
        in_specs=[ANY] * n, out_specs=[ANY] * n,
        scratch_shapes=[pltpu.SemaphoreType.DMA((n,)), pltpu.SemaphoreType.DMA((n,))],
    )(*parts)


def _pair_sum(g8, r1, me, name):
    _, r, c = g8.shape
    tr = max(q for q in range(16, r + 1, 16) if r % q == 0 and q * c <= ADAM_TILE_ELEMS)

    def body(me_ref, g_ref, r_ref, o_ref):
        o_ref[...] = (g_ref[...] + r_ref[...].astype(F32)).astype(BF16)

    chip = lambda k, s: s[1] ^ (k + 1)
    return pl.pallas_call(
        body, name=name,
        out_shape=jax.ShapeDtypeStruct((4, r, c), BF16),
        grid_spec=pltpu.PrefetchScalarGridSpec(
            num_scalar_prefetch=1, grid=(3, r // tr),
            in_specs=[pl.BlockSpec((None, None, tr, c), lambda k, i, s: (chip(k, s), s[0] % 2, i, 0)),
                      pl.BlockSpec((None, tr, c), lambda k, i, s: (chip(k, s), i, 0))],
            out_specs=pl.BlockSpec((None, tr, c), lambda k, i, s: (chip(k, s), i, 0))),
        compiler_params=_params(("arbitrary", "arbitrary")),
    )(me, g8.reshape((4, 2) + g8.shape[1:]), r1)


def _adam(w, g, m, v):
    m2 = ADAM_B1 * m + (1.0 - ADAM_B1) * g
    v2 = ADAM_B2 * v + (1.0 - ADAM_B2) * (g * g)
    m_hat = m2 / (1.0 - ADAM_B1 ** ADAM_STEP)
    v_hat = v2 / (1.0 - ADAM_B2 ** ADAM_STEP)
    delta = -ADAM_LR * (m_hat / (jnp.sqrt(v_hat) + ADAM_EPS) + ADAM_WD * w)
    return delta, m2, v2


def _small_allreduce_adam(part, w, m, v, name):
    rows = part.shape[0]

    def body(p_ref, w_ref, m_ref, v_ref, g_ref, d_ref, mo_ref, vo_ref, buf, send_sems, recv_sems):
        x, y, c = _place()
        buf[0] = p_ref[...]
        cps = []
        for k in range(1, N_DEV):
            kx, ky, kc = (k >> 2) & 1, (k >> 1) & 1, k & 1
            peer = (x ^ kx, y ^ ky, c ^ kc)
            cps.append(pltpu.make_async_remote_copy(
                src_ref=p_ref, dst_ref=buf.at[k], send_sem=send_sems.at[k - 1], recv_sem=recv_sems.at[k - 1],
                device_id=peer, device_id_type=MESH_ID))
        for cp in cps:
            cp.start()
        for cp in cps:
            cp.wait()
        me = 4 * x + 2 * y + c
        total = buf[me]
        for d in range(1, N_DEV):
            total = total + buf[d ^ me]
        g_ref[...] = total
        delta, m2, v2 = _adam(w_ref[...], total, m_ref[...], v_ref[...])
        d_ref[...] = delta
        mo_ref[...] = m2
        vo_ref[...] = v2

    vm = pl.BlockSpec(memory_space=pltpu.VMEM)
    return pl.pallas_call(
        body, name=name,
        out_shape=[jax.ShapeDtypeStruct(part.shape, F32)] * 4,
        in_specs=[vm] * 4, out_specs=[vm] * 4,
        scratch_shapes=[pltpu.VMEM((N_DEV, rows, LANES), F32),
                        pltpu.SemaphoreType.DMA((N_DEV - 1,)), pltpu.SemaphoreType.DMA((N_DEV - 1,))],
    )(part, w, m, v)


def _final_adam(g8, land, w, m, v, me, dep, name, pair=None):
    _, r, c = g8.shape
    tr = max(q for q in range(16, r + 1, 16) if r % q == 0 and q * c <= ADAM_TILE_ELEMS)
    nland = land.shape[0]

    def body(me_ref, g_ref, land_ref, *rest):
        pair_ref = rest[0] if pair is not None else None
        w_ref, m_ref, v_ref, _, go_ref, d_ref, mo_ref, vo_ref = rest[-8:]
        g = g_ref[...]
        if pair_ref is not None:
            g = g + pair_ref[...].astype(F32)
        for k in range(nland):
            g = g + land_ref[k].astype(F32)
        go_ref[...] = g
        delta, m2, v2 = _adam(w_ref[...], g, m_ref[...], v_ref[...])
        d_ref[...] = delta
        mo_ref[...] = m2
        vo_ref[...] = v2

    plain = pl.BlockSpec((tr, c), lambda i, s: (i, 0))
    return pl.pallas_call(
        body, name=name,
        out_shape=[jax.ShapeDtypeStruct((r, c), F32)] * 4,
        grid_spec=pltpu.PrefetchScalarGridSpec(
            num_scalar_prefetch=1, grid=(r // tr,),
            in_specs=[pl.BlockSpec((None, tr, c), lambda i, s: (s[0], i, 0)),
                      pl.BlockSpec((nland, tr, c), lambda i, s: (0, i, 0))]
            + ([] if pair is None else [pl.BlockSpec((None, tr, c), lambda i, s: (s[1], i, 0))])
            + [plain, plain, plain, ANY],
            out_specs=[plain] * 4),
        compiler_params=_params(("arbitrary",)),
    )(*((me, g8, land) + (() if pair is None else (pair,)) + (w, m, v, dep)))


def _rms(x, gain):
    r = lax.rsqrt(jnp.mean(x * x, axis=-1, keepdims=True) + EPS)
    xh = x * r
    return xh * gain, xh, r


def _rms_bwd(xh, r, gain, dy):
    gdy = gain * dy
    dx = r * (gdy - xh * jnp.mean(xh * gdy, axis=-1, keepdims=True))
    return dx, jnp.sum(dy * xh, axis=0, keepdims=True)


def _load_weights(pairs, sems):
    cps = [pltpu.make_async_copy(src, dst, sems.at[i]) for i, (src, dst) in enumerate(pairs)]
    for cp in cps:
        cp.start()
    for cp in cps:
        cp.wait()


def _ffn_fwd(h, gain, wgu, wd, name):
    t, d = h.shape
    nb, nf, _ = wgu.shape
    nh = nb // 2
    tm = _row_tile(t, 512)

    def body(h_ref, g_ref, wgu_hbm, wd_hbm, out_ref, gu_ref, wgu_v, wd_v, sems):
        @pl.when(pl.program_id(0) == 0)
        def _():
            _load_weights([(wgu_hbm, wgu_v), (wd_hbm, wd_v)], sems)

        x = h_ref[...]
        n, _, _ = _rms(x, g_ref[...])
        nbf = n.astype(BF16)
        acc = jnp.zeros((tm, d), F32)
        for j in range(nh):
            g = _dot_nt(nbf, wgu_v[j])
            u = _dot_nt(nbf, wgu_v[j + nh])
            gu_ref[j] = g.astype(BF16)
            gu_ref[j + nh] = u.astype(BF16)
            a = (g * jax.nn.sigmoid(g)) * u
            acc = acc + _dot(a.astype(BF16), wd_v[j])
        out_ref[...] = x + 0.5 * acc

    return pl.pallas_call(
        body, name=name, grid=(t // tm,),
        out_shape=[jax.ShapeDtypeStruct((t, d), F32), jax.ShapeDtypeStruct((nb, t, nf), BF16)],
        in_specs=[pl.BlockSpec((tm, d), lambda i: (i, 0)), pl.BlockSpec((1, d), lambda i: (0, 0)), ANY, ANY],
        out_specs=[pl.BlockSpec((tm, d), lambda i: (i, 0)), pl.BlockSpec((nb, tm, nf), lambda i: (0, i, 0))],
        scratch_shapes=[pltpu.VMEM(wgu.shape, BF16), pltpu.VMEM(wd.shape, BF16), pltpu.SemaphoreType.DMA((2,))],
        compiler_params=_params(("arbitrary",)),
    )(h, gain, wgu, wd)


def _ffn_up(h, gain, wgu, name):
    t, d = h.shape
    nb, nf, _ = wgu.shape
    nh = nb // 2
    tm = _row_tile(t, 512)

    def body(h_ref, g_ref, wgu_hbm, gu_ref, a_ref, wgu_v, sems):
        @pl.when(pl.program_id(0) == 0)
        def _():
            _load_weights([(wgu_hbm, wgu_v)], sems)

        n, _, _ = _rms(h_ref[...], g_ref[...])
        nbf = n.astype(BF16)
        for j in range(nh):
            g = _dot_nt(nbf, wgu_v[j])
            u = _dot_nt(nbf, wgu_v[j + nh])
            gu_ref[j] = g.astype(BF16)
            gu_ref[j + nh] = u.astype(BF16)
            a_ref[j] = ((g * jax.nn.sigmoid(g)) * u).astype(BF16)

    return pl.pallas_call(
        body, name=name, grid=(t // tm,),
        out_shape=[jax.ShapeDtypeStruct((nb, t, nf), BF16), jax.ShapeDtypeStruct((nh, t, nf), BF16)],
        in_specs=[pl.BlockSpec((tm, d), lambda i: (i, 0)), pl.BlockSpec((1, d), lambda i: (0, 0)), ANY],
        out_specs=[pl.BlockSpec((nb, tm, nf), lambda i: (0, i, 0)), pl.BlockSpec((nh, tm, nf), lambda i: (0, i, 0))],
        scratch_shapes=[pltpu.VMEM(wgu.shape, BF16), pltpu.SemaphoreType.DMA((1,))],
        compiler_params=_params(("arbitrary",)),
    )(h, gain, wgu)


def _ffn_down(h, a, wd, name):
    t, d = h.shape
    nh, nf, _ = wd.shape
    tm = _row_tile(t, 512)

    def body(h_ref, a_ref, wd_ref, out_ref):
        acc = jnp.zeros((tm, d), F32)
        for j in range(nh):
            acc = acc + _dot(a_ref[j], wd_ref[j])
        out_ref[...] = h_ref[...] + 0.5 * acc

    row = pl.BlockSpec((tm, d), lambda i: (i, 0))
    return pl.pallas_call(
        body, name=name, grid=(t // tm,),
        out_shape=jax.ShapeDtypeStruct((t, d), F32),
        in_specs=[row, pl.BlockSpec((nh, tm, nf), lambda i: (0, i, 0)), pl.BlockSpec(wd.shape, lambda i: (0, 0, 0))],
        out_specs=row,
        compiler_params=_params(("arbitrary",)),
    )(h, a, wd)


def _ffn_bwd(dh, h, gain, gu, wgu, wd, name):
    t, d = h.shape
    nb, nf, _ = wgu.shape
    nh = nb // 2
    tm = _row_tile(t, 256)

    def body(dh_ref, h_ref, g_ref, gu_ref, wgu_hbm, wd_hbm, dhp_ref, dgu_ref, a_ref, n_ref, dgain_ref,
             wgu_v, wd_v, sems):
        @pl.when(pl.program_id(0) == 0)
        def _():
            _load_weights([(wgu_hbm, wgu_v), (wd_hbm, wd_v)], sems)
            dgain_ref[...] = jnp.zeros_like(dgain_ref)

        x = h_ref[...]
        gain_v = g_ref[...]
        n, xh, r = _rms(x, gain_v)
        n_ref[...] = n.astype(BF16)
        dh_v = dh_ref[...]
        dfb = (0.5 * dh_v).astype(BF16)
        dn = jnp.zeros((tm, d), F32)
        for j in range(nh):
            da = _dot_nt(dfb, wd_v[j])
            g = gu_ref[j].astype(F32)
            u = gu_ref[j + nh].astype(F32)
            sg = jax.nn.sigmoid(g)
            si = g * sg
            dg = (da * u * (sg * (1.0 + g * (1.0 - sg)))).astype(BF16)
            du = (da * si).astype(BF16)
            a_ref[j] = (si * u).astype(BF16)
            dgu_ref[j] = dg
            dgu_ref[j + nh] = du
            dn = dn + _dot(dg, wgu_v[j]) + _dot(du, wgu_v[j + nh])
        dx, dgain = _rms_bwd(xh, r, gain_v, dn)
        dhp_ref[...] = dh_v + dx
        dgain_ref[...] += dgain

    row = pl.BlockSpec((tm, d), lambda i: (i, 0))
    vec = pl.BlockSpec((1, d), lambda i: (0, 0))
    return pl.pallas_call(
        body, name=name, grid=(t // tm,),
        out_shape=[jax.ShapeDtypeStruct((t, d), F32), jax.ShapeDtypeStruct((nb, t, nf), BF16),
                   jax.ShapeDtypeStruct((nh, t, nf), BF16), jax.ShapeDtypeStruct((t, d), BF16),
                   jax.ShapeDtypeStruct((1, d), F32)],
        in_specs=[row, row, vec, pl.BlockSpec((nb, tm, nf), lambda i: (0, i, 0)), ANY, ANY],
        out_specs=[row, pl.BlockSpec((nb, tm, nf), lambda i: (0, i, 0)),
                   pl.BlockSpec((nh, tm, nf), lambda i: (0, i, 0)), row, vec],
        scratch_shapes=[pltpu.VMEM(wgu.shape, BF16), pltpu.VMEM(wd.shape, BF16), pltpu.SemaphoreType.DMA((2,))],
        compiler_params=_params(("arbitrary",)),
    )(dh, h, gain, gu, wgu, wd)


def _dw(xa, dy, nb, n, name, scale=1.0, dep=None):
    t, k = xa.shape[-2:]
    wide = xa.ndim == 2
    tt = _row_tile(t, 1024)
    steps = t // tt
    x_spec = pl.BlockSpec((tt, k), lambda i: (i, 0)) if wide else pl.BlockSpec((nb, tt, k), lambda i: (0, i, 0))
    dy_spec = pl.BlockSpec((tt, dy.shape[1]), lambda i: (i, 0))
    acc_shape = (k, nb * n) if wide else (nb, k, n)
    stage_shape = (k, nb * n) if wide else (k, n)

    def body(x_ref, dy_ref, *rest):
        o_hbm, ob_hbm, acc, stage, sems = rest[-5:]

        @pl.when(pl.program_id(0) == 0)
        def _():
            acc[...] = jnp.zeros_like(acc)

        dyb = dy_ref[...].astype(BF16)
        if wide:
            acc[...] += _dot(x_ref[...].astype(BF16).T, dyb)
        else:
            for j in range(nb):
                acc[j] += _dot_tn(x_ref[j].astype(BF16), dyb)

        @pl.when(pl.program_id(0) == steps - 1)
        def _():
            if scale != 1.0:
                acc[...] = acc[...] * scale
            if wide:
                cps = [pltpu.make_async_copy(acc.at[:, pl.ds(j * n, n)] if nb > 1 else acc, o_hbm.at[j], sems.at[j])
                       for j in range(nb)]
            else:
                cps = [pltpu.make_async_copy(acc, o_hbm, sems.at[0])]
            for cp in cps:
                cp.start()
            if wide:
                stage[...] = acc[...].astype(BF16)
                bcs = [pltpu.make_async_copy(stage.at[:, pl.ds(j * n, n)] if nb > 1 else stage, ob_hbm.at[j],
                                             sems.at[nb + j]) for j in range(nb)]
                for cp in bcs:
                    cp.start()
                for cp in bcs:
                    cp.wait()
            else:
                for j in range(nb):
                    stage[...] = acc[j].astype(BF16)
                    cp = pltpu.make_async_copy(stage, ob_hbm.at[j], sems.at[nb])
                    cp.start()
                    cp.wait()
            for cp in cps:
                cp.wait()

    return pl.pallas_call(
        body, name=name, grid=(steps,),
        out_shape=[jax.ShapeDtypeStruct((nb, k, n), F32), jax.ShapeDtypeStruct((nb, k, n), BF16)],
        in_specs=[x_spec, dy_spec] + ([] if dep is None else [ANY]),
        out_specs=[ANY, ANY],
        scratch_shapes=[pltpu.VMEM(acc_shape, F32), pltpu.VMEM(stage_shape, BF16),
                        pltpu.SemaphoreType.DMA((2 * nb,))],
        compiler_params=_params(("arbitrary",), DW_VMEM_LIMIT),
    )(*((xa, dy) if dep is None else (xa, dy, dep)))


def _proj_fwd(h, gain, win, wgate, name):
    t, d = h.shape
    tm = _row_tile(t, 512)
    nq, ng = win.shape[0], wgate.shape[1]

    def body(h_ref, g_ref, win_ref, wg_ref, un_ref, qkv_ref, gate_ref):
        n, _, _ = _rms(h_ref[...], g_ref[...])
        nbf = n.astype(BF16)
        un_ref[...] = nbf
        qkv_ref[...] = _dot_nt(nbf, win_ref[...])
        gate_ref[...] = jax.nn.sigmoid(_dot(nbf, wg_ref[...]))

    full = lambda a: pl.BlockSpec(a.shape, lambda i: (0,) * a.ndim)
    return pl.pallas_call(
        body, name=name, grid=(t // tm,),
        out_shape=[jax.ShapeDtypeStruct((t, d), BF16), jax.ShapeDtypeStruct((t, nq), F32),
                   jax.ShapeDtypeStruct((t, ng), F32)],
        in_specs=[pl.BlockSpec((tm, d), lambda i: (i, 0)), full(gain), full(win), full(wgate)],
        out_specs=[pl.BlockSpec((tm, d), lambda i: (i, 0)), pl.BlockSpec((tm, nq), lambda i: (i, 0)),
                   pl.BlockSpec((tm, ng), lambda i: (i, 0))],
        compiler_params=_params(("arbitrary",)),
    )(h, gain, win, wgate)


def _proj_bwd(dh, h, gain, dzg, dqkv_parts, win, wgate, name):
    t, d = h.shape
    tm = _row_tile(t, 512)
    ng = wgate.shape[1]
    np_ = len(dqkv_parts)
    widths = [a.shape[1] for a in dqkv_parts]

    def body(dh_ref, h_ref, g_ref, dzg_ref, *rest):
        part_refs, (win_ref, wg_ref, dhp_ref, dgain_ref) = rest[:np_], rest[np_:]

        @pl.when(pl.program_id(0) == 0)
        def _():
            dgain_ref[...] = jnp.zeros_like(dgain_ref)

        gain_v = g_ref[...]
        _, xh, r = _rms(h_ref[...], gain_v)
        dun = _dot_nt(dzg_ref[...], wg_ref[...])
        off = 0
        for ref, wd in zip(part_refs, widths):
            dun = dun + _dot(ref[...].astype(BF16), win_ref[off:off + wd, :])
            off += wd
        dx, dgain = _rms_bwd(xh, r, gain_v, dun)
        dhp_ref[...] = dh_ref[...] + dx
        dgain_ref[...] += dgain

    full = lambda a: pl.BlockSpec(a.shape, lambda i: (0,) * a.ndim)
    row = pl.BlockSpec((tm, d), lambda i: (i, 0))
    return pl.pallas_call(
        body, name=name, grid=(t // tm,),
        out_shape=[jax.ShapeDtypeStruct((t, d), F32), jax.ShapeDtypeStruct((1, d), F32)],
        in_specs=[row, row, full(gain), pl.BlockSpec((tm, ng), lambda i: (i, 0))]
        + [pl.BlockSpec((tm, wd), lambda i: (i, 0)) for wd in widths] + [full(win), full(wgate)],
        out_specs=[row, pl.BlockSpec((1, d), lambda i: (0, 0))],
        compiler_params=_params(("arbitrary",)),
    )(dh, h, gain, dzg, *dqkv_parts, win, wgate)


def _dw_rows(parts, dy, name):
    t, n = dy.shape
    widths = [a.shape[1] for a in parts]
    k = sum(widths)
    tt = _row_tile(t, 1024)
    steps = t // tt
    np_ = len(parts)

    def body(*refs):
        part_refs, dy_ref = refs[:np_], refs[np_]
        o_hbm, ob_hbm, acc, stage, sems = refs[np_ + 1:]

        @pl.when(pl.program_id(0) == 0)
        def _():
            acc[...] = jnp.zeros_like(acc)

        dyb = dy_ref[...].astype(BF16)
        off = 0
        for ref, wd in zip(part_refs, widths):
            acc[off:off + wd, :] += _dot(ref[...].astype(BF16).T, dyb)
            off += wd

        @pl.when(pl.program_id(0) == steps - 1)
        def _():
            stage[...] = acc[...].astype(BF16)
            cps = [pltpu.make_async_copy(acc, o_hbm.at[0], sems.at[0]),
                   pltpu.make_async_copy(stage, ob_hbm.at[0], sems.at[1])]
            for cp in cps:
                cp.start()
            for cp in cps:
                cp.wait()

    return pl.pallas_call(
        body, name=name, grid=(steps,),
        out_shape=[jax.ShapeDtypeStruct((1, k, n), F32), jax.ShapeDtypeStruct((1, k, n), BF16)],
        in_specs=[pl.BlockSpec((tt, wd), lambda i: (i, 0)) for wd in widths] + [pl.BlockSpec((tt, n), lambda i: (i, 0))],
        out_specs=[ANY, ANY],
        scratch_shapes=[pltpu.VMEM((k, n), F32), pltpu.VMEM((k, n), BF16), pltpu.SemaphoreType.DMA((2,))],
        compiler_params=_params(("arbitrary",)),
    )(*parts, dy)


def _merge_fwd(h, ya, yb, gate, wpa, wpb, wout, name):
    t, d = h.shape
    tm = _row_tile(t, 512)

    def body(h_ref, ya_ref, yb_ref, ga_ref, gb_ref, wpa_ref, wpb_ref, wout_ref, out_ref, mg_ref, pa_ref, pb_ref):
        pa = _dot(ya_ref[...].astype(BF16), wpa_ref[...])
        pb = _dot(yb_ref[...].astype(BF16), wpb_ref[...])
        merged = (ga_ref[...] * pa + gb_ref[...] * pb).astype(BF16)
        pa_ref[...] = pa.astype(BF16)
        pb_ref[...] = pb.astype(BF16)
        mg_ref[...] = merged
        out_ref[...] = h_ref[...] + _dot(merged, wout_ref[...])

    full = lambda a: pl.BlockSpec(a.shape, lambda i: (0,) * a.ndim)
    row = pl.BlockSpec((tm, d), lambda i: (i, 0))
    yrow = pl.BlockSpec((tm, ya.shape[1]), lambda i: (i, 0))
    return pl.pallas_call(
        body, name=name, grid=(t // tm,),
        out_shape=[jax.ShapeDtypeStruct((t, d), F32)] + [jax.ShapeDtypeStruct((t, d), BF16)] * 3,
        in_specs=[row, yrow, yrow, pl.BlockSpec((tm, d), lambda i: (i, 0)), pl.BlockSpec((tm, d), lambda i: (i, 1)),
                  full(wpa), full(wpb), full(wout)],
        out_specs=[row] * 4,
        compiler_params=_params(("arbitrary",)),
    )(h, ya, yb, gate, gate, wpa, wpb, wout)


def _merge_bwd(dh, pa, pb, gate, wpa, wpb, wout, name):
    t, d = dh.shape
    tm = _row_tile(t, 512)
    wy = wpa.shape[0]

    def body(dh_ref, pa_ref, pb_ref, ga_ref, gb_ref, wpa_ref, wpb_ref, wout_ref,
             dpa_ref, dpb_ref, dzg_ref, dya_ref, dyb_ref):
        dm = _dot_nt(dh_ref[...].astype(BF16), wout_ref[...])
        ga, gb = ga_ref[...], gb_ref[...]
        dpa = (dm * ga).astype(BF16)
        dpb = (dm * gb).astype(BF16)
        dpa_ref[...] = dpa
        dpb_ref[...] = dpb
        dzg_ref[:, :d] = (dm * pa_ref[...].astype(F32) * ga * (1.0 - ga)).astype(BF16)
        dzg_ref[:, d:] = (dm * pb_ref[...].astype(F32) * gb * (1.0 - gb)).astype(BF16)
        dya_ref[...] = _dot_nt(dpa, wpa_ref[...])
        dyb_ref[...] = _dot_nt(dpb, wpb_ref[...])

    full = lambda a: pl.BlockSpec(a.shape, lambda i: (0,) * a.ndim)
    row = pl.BlockSpec((tm, d), lambda i: (i, 0))
    yrow = pl.BlockSpec((tm, wy), lambda i: (i, 0))
    return pl.pallas_call(
        body, name=name, grid=(t // tm,),
        out_shape=[jax.ShapeDtypeStruct((t, d), BF16), jax.ShapeDtypeStruct((t, d), BF16),
                   jax.ShapeDtypeStruct((t, 2 * d), BF16), jax.ShapeDtypeStruct((t, wy), F32),
                   jax.ShapeDtypeStruct((t, wy), F32)],
        in_specs=[row, row, row, pl.BlockSpec((tm, d), lambda i: (i, 0)), pl.BlockSpec((tm, d), lambda i: (i, 1)),
                  full(wpa), full(wpb), full(wout)],
        out_specs=[row, row, pl.BlockSpec((tm, 2 * d), lambda i: (i, 0)), yrow, yrow],
        compiler_params=_params(("arbitrary",)),
    )(dh, pa, pb, gate, gate, wpa, wpb, wout)


def _ple_loss(h, gain, p, target, wpg, wpe, name):
    t, d = h.shape
    tm = _row_tile(t, 512)
    pd = p.shape[1]

    def body(h_ref, g_ref, p_ref, t_ref, wpg_ref, wpe_ref, dh_ref, dz_ref, dpp_ref, n_ref, dgain_ref, loss_ref):
        @pl.when(pl.program_id(0) == 0)
        def _():
            dgain_ref[...] = jnp.zeros_like(dgain_ref)
            loss_ref[...] = jnp.zeros_like(loss_ref)

        x = h_ref[...]
        gain_v = g_ref[...]
        n, xh, r = _rms(x, gain_v)
        nbf = n.astype(BF16)
        n_ref[...] = nbf
        pg = jax.nn.sigmoid(_dot(nbf, wpg_ref[...]))
        pp = _dot(p_ref[...].astype(BF16), wpe_ref[...])
        err = (x + pg * pp) - t_ref[...]
        loss_ref[...] += 0.5 * jnp.sum(jnp.mean(err * err, axis=-1, keepdims=True))
        dy = err * (1.0 / d)
        dpp_ref[...] = (dy * pg).astype(BF16)
        dz = (dy * pp * pg * (1.0 - pg)).astype(BF16)
        dz_ref[...] = dz
        dn = _dot_nt(dz, wpg_ref[...])
        dx, dgain = _rms_bwd(xh, r, gain_v, dn)
        dh_ref[...] = dy + dx
        dgain_ref[...] += dgain

    full = lambda a: pl.BlockSpec(a.shape, lambda i: (0,) * a.ndim)
    row = pl.BlockSpec((tm, d), lambda i: (i, 0))
    return pl.pallas_call(
        body, name=name, grid=(t // tm,),
        out_shape=[jax.ShapeDtypeStruct((t, d), F32), jax.ShapeDtypeStruct((t, d), BF16),
                   jax.ShapeDtypeStruct((t, d), BF16), jax.ShapeDtypeStruct((t, d), BF16),
                   jax.ShapeDtypeStruct((1, d), F32), jax.ShapeDtypeStruct((8, LANES), F32)],
        in_specs=[row, full(gain), pl.BlockSpec((tm, pd), lambda i: (i, 0)), row, full(wpg), full(wpe)],
        out_specs=[row, row, row, row, pl.BlockSpec((1, d), lambda i: (0, 0)),
                   pl.BlockSpec((8, LANES), lambda i: (0, 0))],
        compiler_params=_params(("arbitrary",)),
    )(h, gain, p, target, wpg, wpe)


def _head_masks():
    lane = lax.broadcasted_iota(jnp.int32, (1, LANES), 1)
    m0 = (lane < HEAD_DIM).astype(F32)
    return m0, 1.0 - m0


def _head_mean(v, m0, m1):
    del m0, m1
    width = v.shape[-1]
    shift = HEAD_DIM.bit_length() - 1
    r = jnp.right_shift(lax.broadcasted_iota(jnp.int32, (width, width), 0), shift)
    c = jnp.right_shift(lax.broadcasted_iota(jnp.int32, (width, width), 1), shift)
    same_head = (r == c).astype(BF16)
    return _dot(v.astype(BF16), same_head) * (1.0 / HEAD_DIM)


def _head_norm(x, gain, m0, m1):
    r = lax.rsqrt(_head_mean(x * x, m0, m1) + EPS)
    xh = x * r
    return xh * gain, xh, r


def _head_norm_bwd(xh, r, gain, dy, m0, m1):
    gdy = gain * dy
    dx = r * (gdy - xh * _head_mean(xh * gdy, m0, m1))
    return dx, jnp.sum(dy * xh, axis=0, keepdims=True)


GROUP = 4
QW = GROUP * HEAD_DIM
STACK = GROUP * QTILE


def _kv_width(mode):
    return QW if mode == "A" else LANES


def _q_scratch_shape(mode, s_len):
    return (s_len, QW) if mode == "A" else (GROUP * s_len, LANES)


def _group_masks(dtype=F32):
    lane = lax.broadcasted_iota(jnp.int32, (1, QW), 1)
    return [((lane >= h * HEAD_DIM) & (lane < (h + 1) * HEAD_DIM)).astype(dtype) for h in range(GROUP)]


def _stack_heads(first_kv, x, m0, m1):
    out = []
    for half in range(GROUP // 2):
        xh = x[:, half * LANES:(half + 1) * LANES]
        a0, a1 = xh * m0, xh * m1
        r0, r1 = pltpu.roll(a0, HEAD_DIM, 1), pltpu.roll(a1, HEAD_DIM, 1)
        out += [jnp.where(first_kv, a0, r0), jnp.where(first_kv, r1, a1)]
    return out


def _unstack_heads(mode, first_kv, ts, m0, m1):
    if mode == "A":
        masks = _group_masks()
        return sum(t * mk for t, mk in zip(ts, masks))
    halves = []
    for half in range(GROUP // 2):
        t0 = jnp.where(first_kv, ts[2 * half], pltpu.roll(ts[2 * half], HEAD_DIM, 1))
        t1 = jnp.where(first_kv, pltpu.roll(ts[2 * half + 1], HEAD_DIM, 1), ts[2 * half + 1])
        halves.append(t0 * m0 + t1 * m1)
    return jnp.concatenate(halves, axis=1)


def _store_stacked(dst, i, heads):
    for half in range(2):
        rows = slice(half * QTILE, (half + 1) * QTILE)
        for h, x in enumerate(heads):
            dst[pl.ds((2 * i + half) * STACK + h * QTILE, QTILE), :] = x[rows].astype(dst.dtype)


def _load_stacked(mode, ref, m):
    if mode == "B":
        return ref[pl.ds(pl.multiple_of(m * STACK, STACK), STACK), :]
    x = ref[pl.ds(pl.multiple_of(m * QTILE, QTILE), QTILE), :]
    return jnp.concatenate([x * mk for mk in _group_masks(x.dtype)], axis=0)


def _attn_prep(mode, group, s_len, padk, q_ref, k_ref, v_ref, gq_ref, gk_ref, qs, k2, v2, do_ref=None, dos=None):
    m0, m1 = _head_masks()
    zpad = jnp.zeros((padk, k2.shape[1]), BF16)
    k2[pl.ds(0, padk), :] = zpad
    v2[pl.ds(0, padk), :] = zpad
    first_kv = group == 0
    rt = 2 * QTILE
    for i in range(s_len // rt):
        rows = pl.ds(i * rt, rt)
        qn, _, _ = _head_norm(q_ref[rows, :], gq_ref[...], m0, m1)
        kn, _, _ = _head_norm(k_ref[rows, :], gk_ref[...], m0, m1)
        qn = qn * (HEAD_DIM ** -0.5)
        if mode == "A":
            qs[rows, :] = qn.astype(BF16)
            if dos is not None:
                dos[rows, :] = do_ref[rows, :].astype(BF16)
        else:
            _store_stacked(qs, i, _stack_heads(first_kv, qn, m0, m1))
            if dos is not None:
                _store_stacked(dos, i, _stack_heads(first_kv, do_ref[rows, :], m0, m1))
        k2[pl.ds(padk + i * rt, rt), :] = kn.astype(BF16)
        v2[pl.ds(padk + i * rt, rt), :] = v_ref[rows, :].astype(BF16)


def _softmax_terms(mode, s, sink):
    mx = jnp.max(s, axis=-1, keepdims=True)
    if mode == "B":
        mx = jnp.maximum(mx, sink)
    e = jnp.exp(s - mx)
    l = jnp.sum(e, axis=-1, keepdims=True)
    if mode == "B":
        l = l + jnp.exp(sink - mx)
    return e, mx, l


def _sink_column(sink_ref, group):
    row = lax.broadcasted_iota(jnp.int32, (STACK, 1), 0)
    col = jnp.zeros((STACK, 1), F32)
    for h in range(GROUP):
        col = jnp.where((row >= h * QTILE) & (row < (h + 1) * QTILE), sink_ref[GROUP * group + h], col)
    return col


def _head_deltas(dd, m0, m1):
    cols = []
    for half in range(GROUP // 2):
        dh = dd[:, half * LANES:(half + 1) * LANES]
        cols += [jnp.sum(dh * m0, axis=-1, keepdims=True), jnp.sum(dh * m1, axis=-1, keepdims=True)]
    return jnp.concatenate(cols, axis=0)


def _attn_cols(mode):
    if mode == "A":
        return (lambda b, g: (b, g)), (lambda b, g: (b, 2 + g)), (lambda b, g: (b, 4 + g))
    return (lambda b, g: (b, 6 + g)), (lambda b, g: (b, 16)), (lambda b, g: (b, 17))


def _attn_fwd(mode, qkv, gq, gk, bias, sinks, bl, s_len, name):
    bw = bias.shape[-1]
    padk = bw - QTILE
    nt = s_len // QTILE
    qmap, kmap, vmap = _attn_cols(mode)

    kw = _kv_width(mode)

    def body(q_ref, k_ref, v_ref, gq_ref, gk_ref, bias_ref, sink_ref, o_ref, qs, k2, v2, s_buf, *rest):
        o_buf = rest[0] if rest else None
        group = pl.program_id(1)
        m0, m1 = _head_masks()
        first_kv = group == 0
        _attn_prep(mode, group, s_len, padk, q_ref, k_ref, v_ref, gq_ref, gk_ref, qs, k2, v2)
        col = lax.broadcasted_iota(jnp.int32, (STACK, bw), 1)
        sink = _sink_column(sink_ref, group)

        def scores(m, slot):
            r0 = pl.multiple_of(m * QTILE, QTILE)
            s = _dot_nt(_load_stacked(mode, qs, m), k2[pl.ds(r0, bw), :]) + bias_ref[...]
            s_buf[slot] = jnp.where(col >= (padk - r0), s, NEG_INF)

        def finish_tile(m, slot):
            r0 = pl.multiple_of(m * QTILE, QTILE)
            e, _, l = _softmax_terms(mode, s_buf[slot], sink)
            if mode == "A":
                o_st = _dot(e.astype(BF16), v2[pl.ds(r0, bw), :]) / l
                heads = [o_st[h * QTILE:(h + 1) * QTILE] for h in range(GROUP)]
                o_ref[pl.ds(r0, QTILE), :] = _unstack_heads(mode, first_kv, heads, m0, m1)
            else:
                o_buf[pl.ds(pl.multiple_of(m * STACK, STACK), STACK), :] = _dot((e * (1.0 / l)).astype(BF16),
                                                                                 v2[pl.ds(r0, bw), :])

        scores(0, 0)

        def pair(j, carry):
            scores(2 * j + 1, 1)
            finish_tile(2 * j, 0)
            scores(jnp.minimum(2 * j + 2, nt - 1), 0)
            finish_tile(2 * j + 1, 1)
            return carry

        lax.fori_loop(0, nt // 2, pair, 0)
        if mode == "B":
            for m in range(nt):
                heads = [o_buf[pl.ds(m * STACK + h * QTILE, QTILE), :] for h in range(GROUP)]
                o_ref[pl.ds(m * QTILE, QTILE), :] = _unstack_heads(mode, first_kv, heads, m0, m1)

    blk = lambda w, f: pl.BlockSpec((s_len, w), f)
    return pl.pallas_call(
        body, name=name, grid=(bl, B_Q_HEADS // GROUP),
        out_shape=jax.ShapeDtypeStruct((bl * s_len, B_Q_HEADS * HEAD_DIM), F32),
        in_specs=[blk(QW, qmap), blk(kw, kmap), blk(kw, vmap),
                  pl.BlockSpec((1, QW), lambda b, g: (0, 0)), pl.BlockSpec((1, kw), lambda b, g: (0, 0)),
                  pl.BlockSpec((STACK, bw), lambda b, g: (g, 0)),
                  pl.BlockSpec(memory_space=pltpu.SMEM)],
        out_specs=blk(QW, lambda b, g: (b, g)),
        scratch_shapes=[pltpu.VMEM(_q_scratch_shape(mode, s_len), BF16)] + [pltpu.VMEM((s_len + padk, kw), BF16)] * 2
        + [pltpu.VMEM((2, STACK, bw), F32)] + ([pltpu.VMEM((GROUP * s_len, LANES), F32)] if mode == "B" else []),
        compiler_params=_params(("arbitrary", "arbitrary")),
    )(qkv, qkv, qkv, gq, gk, bias.reshape(-1, bw), sinks)


def _attn_bwd(mode, qkv, gq, gk, bias, sinks, y, dy, bl, s_len, name):
    bw = bias.shape[-1]
    padk = bw - QTILE
    nt = s_len // QTILE
    qmap, kmap, vmap = _attn_cols(mode)
    t = bl * s_len
    kw = _kv_width(mode)
    kvw = 4 * LANES if mode == "A" else LANES
    dp_ahead = True

    def body(q_ref, k_ref, v_ref, gq_ref, gk_ref, bias_ref, sink_ref, y_ref, dy_ref,
             dq_ref, dk_ref, dv_ref, dgq_ref, dgk_ref, dbias_ref, dsink_ref,
             qs, k2, v2, dos, dqs, dk, dv, s_buf, dp_buf):
        group = pl.program_id(1)
        m0, m1 = _head_masks()
        first_kv = group == 0
        _attn_prep(mode, group, s_len, padk, q_ref, k_ref, v_ref, gq_ref, gk_ref, qs, k2, v2, dy_ref, dos)
        dk[...] = jnp.zeros_like(dk)
        dv[...] = jnp.zeros_like(dv)
        dbias_ref[...] = jnp.zeros_like(dbias_ref)
        col = lax.broadcasted_iota(jnp.int32, (STACK, bw), 1)
        lane8 = lax.broadcasted_iota(jnp.int32, (8, LANES), 1)
        sink = _sink_column(sink_ref, group)

        def ahead(m, slot):
            r0 = pl.multiple_of(m * QTILE, QTILE)
            band = pl.ds(r0, bw)
            s = _dot_nt(_load_stacked(mode, qs, m), k2[band, :]) + bias_ref[...]
            s_buf[slot] = jnp.where(col >= (padk - r0), s, NEG_INF)
            if dp_ahead:
                dp_buf[slot] = _dot_nt(_load_stacked(mode, dos, m), v2[band, :])

        def tile(m, slot, dsink):
            r0 = pl.multiple_of(m * QTILE, QTILE)
            rows = pl.ds(r0, QTILE)
            band = pl.ds(r0, bw)
            q_st = _load_stacked(mode, qs, m)
            do_st = _load_stacked(mode, dos, m)
            delta = _head_deltas(dy_ref[rows, :] * y_ref[rows, :], m0, m1)
            kb = k2[band, :]
            e, mx, l = _softmax_terms(mode, s_buf[slot], sink)
            inv = 1.0 / l
            pn = e * inv
            ds = pn * ((dp_buf[slot] if dp_ahead else _dot_nt(do_st, v2[band, :])) - delta)
            if mode == "A":
                dbias_ref[...] += ds
            else:
                part = jnp.exp(sink - mx) * inv * delta
                for h in range(GROUP):
                    dsink = dsink - jnp.where(lane8 == h, jnp.sum(part[h * QTILE:(h + 1) * QTILE]), 0.0)
            dsb = ds.astype(BF16)
            dv[band, :] += _dot_tn(pn.astype(BF16), do_st)
            dk[band, :] += _dot_tn(dsb, q_st)
            dq_st = _dot(dsb, kb)
            if mode == "A":
                heads = [dq_st[h * QTILE:(h + 1) * QTILE] for h in range(GROUP)]
                dq_ref[rows, :] = _unstack_heads(mode, first_kv, heads, m0, m1)
            else:
                dqs[pl.ds(pl.multiple_of(m * STACK, STACK), STACK), :] = dq_st
            return dsink

        ahead(0, 0)

        def pair(j, dsink):
            ahead(2 * j + 1, 1)
            dsink = tile(2 * j, 0, dsink)
            ahead(jnp.minimum(2 * j + 2, nt - 1), 0)
            return tile(2 * j + 1, 1, dsink)

        dsink = lax.fori_loop(0, nt // 2, pair, jnp.zeros((8, LANES), F32))
        dsink_ref[...] = dsink

        rt = 2 * QTILE
        dgq = jnp.zeros((1, QW), F32)
        dgk = jnp.zeros((1, kw), F32)
        for i in range(s_len // rt):
            rows = pl.ds(i * rt, rt)
            src = pl.ds(padk + i * rt, rt)
            gq_v, gk_v = gq_ref[...], gk_ref[...]
            _, qh, qr = _head_norm(q_ref[rows, :], gq_v, m0, m1)
            _, kh, kr = _head_norm(k_ref[rows, :], gk_v, m0, m1)
            if mode == "A":
                dqn = dq_ref[rows, :] * (HEAD_DIM ** -0.5)
            else:
                dqn = jnp.concatenate(
                    [_unstack_heads(mode, first_kv, [dqs[pl.ds((2 * i + half) * STACK + h * QTILE, QTILE), :]
                                                     for h in range(GROUP)], m0, m1)
                     for half in range(2)], axis=0) * (HEAD_DIM ** -0.5)
            dq_raw, dgq_i = _head_norm_bwd(qh, qr, gq_v, dqn, m0, m1)
            dk_raw, dgk_i = _head_norm_bwd(kh, kr, gk_v, dk[src, :], m0, m1)
            dvn = dv[src, :]
            dq_ref[rows, :] = dq_raw
            if mode == "A":
                dk_ref[rows, :] = dk_raw
                dv_ref[rows, :] = dvn
            else:
                @pl.when(group == 0)
                def _():
                    dk_ref[rows, :] = dk_raw
                    dv_ref[rows, :] = dvn

                @pl.when(group != 0)
                def _():
                    dk_ref[rows, :] += dk_raw
                    dv_ref[rows, :] += dvn
            dgq, dgk = dgq + dgq_i, dgk + dgk_i
        dgq_ref[...] = jnp.broadcast_to(dgq, (8, QW))
        dgk_ref[...] = jnp.broadcast_to(dgk, (8, kw))

    ng = B_Q_HEADS // GROUP
    blk = lambda w, f: pl.BlockSpec((s_len, w), f)
    small = lambda w: pl.BlockSpec((None, None, 8, w), lambda b, g: (b, g, 0, 0))
    own = lambda b, g: (b, g)
    kvmap = own if mode == "A" else (lambda b, g: (b, 0))
    pad_f32 = pltpu.VMEM((s_len + padk, kw), F32)
    pad_bf = pltpu.VMEM((s_len + padk, kw), BF16)
    stack_bf = pltpu.VMEM(_q_scratch_shape(mode, s_len), BF16)
    outs = pl.pallas_call(
        body, name=name, grid=(bl, ng),
        out_shape=[jax.ShapeDtypeStruct((t, ng * QW), F32), jax.ShapeDtypeStruct((t, kvw), F32),
                   jax.ShapeDtypeStruct((t, kvw), F32),
                   jax.ShapeDtypeStruct((bl, ng, 8, QW), F32), jax.ShapeDtypeStruct((bl, ng, 8, kw), F32),
                   jax.ShapeDtypeStruct((bl, ng * STACK, bw), F32), jax.ShapeDtypeStruct((bl, ng, 8, LANES), F32)],
        in_specs=[blk(QW, qmap), blk(kw, kmap), blk(kw, vmap),
                  pl.BlockSpec((1, QW), lambda b, g: (0, 0)), pl.BlockSpec((1, kw), lambda b, g: (0, 0)),
                  pl.BlockSpec((STACK, bw), lambda b, g: (g, 0)),
                  pl.BlockSpec(memory_space=pltpu.SMEM),
                  blk(QW, own), blk(QW, own)],
        out_specs=[blk(QW, own), blk(kw, kvmap), blk(kw, kvmap), small(QW), small(kw),
                   pl.BlockSpec((None, STACK, bw), lambda b, g: (b, g, 0)), small(LANES)],
        scratch_shapes=[stack_bf, pad_bf, pad_bf, stack_bf,
                        pltpu.VMEM((8, LANES) if mode == "A" else _q_scratch_shape(mode, s_len), F32),
                        pad_f32, pad_f32, pltpu.VMEM((2, STACK, bw), F32),
                        pltpu.VMEM((2, STACK, bw) if dp_ahead else (8, LANES), F32)],
        compiler_params=_params(("arbitrary", "arbitrary")),
    )(qkv, qkv, qkv, gq, gk, bias.reshape(-1, bw), sinks, y, dy)
    outs = list(outs)
    outs[5] = outs[5].reshape(bl, B_Q_HEADS, QTILE, bw)
    return outs


def _band_geometry(prev):
    bw = QTILE + prev * CHUNK
    i = np.arange(QTILE)[:, None]
    j = np.arange(bw)[None, :]
    dist = i + prev * CHUNK - j
    valid = (j // CHUNK >= i // CHUNK) & (j // CHUNK <= i // CHUNK + prev)
    return dist, valid


A_VAR0 = (A_PREV * CHUNK - A_MAX_REL) // LANES * LANES


A_NVAR = QTILE + A_PREV * CHUNK - A_VAR0


def _skew_rows(x, sign):
    rows, n = x.shape
    row = lax.broadcasted_iota(jnp.int32, x.shape, 0)
    b = 1
    while b < rows:
        x = jnp.where((row & b) != 0, pltpu.roll(x, (sign * b) % n, 1), x)
        b *= 2
    return x


def _rel_bias_expand(table, name):
    _, valid = _band_geometry(A_PREV)
    bw = valid.shape[1]
    valid_f = jnp.asarray(valid.astype(np.float32))
    rev = jnp.flip(table[:, 1:], axis=1).reshape(A_HEADS, 1, A_NVAR)

    def body(rev_ref, valid_ref, o_ref):
        rowv = jnp.broadcast_to(rev_ref[...], (QTILE, A_NVAR))
        top = rowv[:, 0:1]
        var = _skew_rows(rowv, 1)
        row = lax.broadcasted_iota(jnp.int32, (QTILE, A_NVAR), 0)
        colv = lax.broadcasted_iota(jnp.int32, (QTILE, A_NVAR), 1)
        var = jnp.where(colv < row, top, var)
        ok = valid_ref[...] > 0.5
        o_ref[:, :A_VAR0] = jnp.where(ok[:, :A_VAR0], top, NEG_INF)
        o_ref[:, A_VAR0:] = jnp.where(ok[:, A_VAR0:], var, NEG_INF)

    return pl.pallas_call(
        body, name=name, grid=(A_HEADS,),
        out_shape=jax.ShapeDtypeStruct((A_HEADS, QTILE, bw), F32),
        in_specs=[pl.BlockSpec((None, 1, A_NVAR), lambda h: (h, 0, 0)), pl.BlockSpec((QTILE, bw), lambda h: (0, 0))],
        out_specs=pl.BlockSpec((None, QTILE, bw), lambda h: (h, 0, 0)),
        compiler_params=_params(("arbitrary",)),
    )(rev, valid_f)


def _rel_bias_grad(dbias, name):
    bl = dbias.shape[0]
    bw = dbias.shape[-1]

    def body(db_ref, o_ref):
        g = db_ref[0]
        for b in range(1, bl):
            g = g + db_ref[b]
        sk = _skew_rows(g[:, A_VAR0:], -1)
        row = lax.broadcasted_iota(jnp.int32, (QTILE, A_NVAR), 0)
        colv = lax.broadcasted_iota(jnp.int32, (QTILE, A_NVAR), 1)
        wrapped = (row + colv) >= A_NVAR
        main = jnp.sum(jnp.where(wrapped, 0.0, sk), axis=0, keepdims=True)
        top = jnp.sum(g[:, :A_VAR0]) + jnp.sum(jnp.where(wrapped, sk, 0.0))
        o_ref[:, :A_NVAR] = jnp.broadcast_to(main, (8, A_NVAR))
        o_ref[:, A_NVAR:] = jnp.full((8, LANES), top, F32)

    out = pl.pallas_call(
        body, name=name, grid=(A_HEADS,),
        out_shape=jax.ShapeDtypeStruct((A_HEADS, 8, A_NVAR + LANES), F32),
        in_specs=[pl.BlockSpec((bl, None, QTILE, bw), lambda h: (0, h, 0, 0))],
        out_specs=pl.BlockSpec((None, 8, A_NVAR + LANES), lambda h: (h, 0, 0)),
        compiler_params=_params(("arbitrary",)),
    )(dbias)
    main, top = out[:, 0, :A_NVAR], out[:, 0, A_NVAR]
    fm = jnp.flip(main, axis=1)
    return jnp.concatenate([jnp.zeros((A_HEADS, 1), F32), fm[:, :-1], fm[:, -1:] + top[:, None]], axis=1)


def _alibi_bias():
    dist, valid = _band_geometry(B_PREV)
    slopes = np.array([2.0 ** (-8.0 * (h + 1) / B_Q_HEADS) for h in range(B_Q_HEADS)], dtype=np.float32)
    bias = -slopes[:, None, None] * np.abs(dist).astype(np.float32)[None]
    return jnp.asarray(np.where(valid[None], bias, np.float32(NEG_INF)).astype(np.float32))


SMALL_NAMES = ("ffn1_norm", "mix_norm", "ffn2_norm", "ple_norm", "a_q_norm", "a_k_norm", "b_q_norm", "b_k_norm",
               "a_rel_bias", "b_sinks", "loss")


def _pack_small(vals):
    rows = []
    for nme in SMALL_NAMES:
        v = vals[nme].astype(F32)
        if nme == "a_rel_bias":
            v = jnp.pad(v.reshape(A_HEADS, -1), ((0, 0), (0, 3 * LANES - (2 * A_MAX_REL + 1))))
        v = v.reshape(-1)
        v = jnp.pad(v, (0, (-v.shape[0]) % LANES))
        rows.append(v.reshape(-1, LANES))
    out = jnp.concatenate(rows, axis=0)
    return jnp.pad(out, ((0, (-out.shape[0]) % 8), (0, 0)))


def _unpack_small(packed, shapes):
    out, r = {}, 0
    for nme in SMALL_NAMES:
        shp = shapes[nme]
        if nme == "a_rel_bias":
            nr = A_HEADS * 3
            out[nme] = packed[r:r + nr].reshape(A_HEADS, 3 * LANES)[:, :2 * A_MAX_REL + 1].reshape(shp)
        else:
            size = int(np.prod(shp)) if shp else 1
            nr = -(-size // LANES)
            out[nme] = packed[r:r + nr].reshape(-1)[:size].reshape(shp)
        r += nr
    return out


BIG_NAMES = ("ffn1_w_gu", "ffn1_w_down", "w_in", "w_gate", "w_proj_a", "w_proj_b", "w_out",
             "ffn2_w_gu", "ffn2_w_down", "w_ple_gate", "w_ple_proj")
WEIGHT_ORDER = ("ffn1_norm", "ffn1_w_gu", "ffn1_w_down", "mix_norm", "w_in", "a_q_norm", "a_k_norm", "a_rel_bias",
                "b_q_norm", "b_k_norm", "b_sinks", "w_gate", "w_proj_a", "w_proj_b", "w_out", "ffn2_norm",
                "ffn2_w_gu", "ffn2_w_down", "ple_norm", "w_ple_gate", "w_ple_proj")


TRANSPOSED = ("ffn1_w_gu", "ffn2_w_gu", "w_in")


def _local(a, nme):
    return a[0].T if nme in TRANSPOSED else a[0]


def _full_cols(wg):
    nb, k, n = wg.shape
    return jnp.transpose(wg, (1, 0, 2)).reshape(k, nb * n)


def _step(x, p, target, w, m, v):
    bl, s_len, d = x.shape
    t = bl * s_len
    h0 = x.reshape(t, d)
    pt = p.reshape(t, p.shape[-1])
    tgt = target.reshape(t, d)

    g_ffn1, g_mix, g_ffn2, g_ple = w["ffn1_norm"], w["mix_norm"], w["ffn2_norm"], w["ple_norm"]
    tiled = lambda a, width: jnp.tile(a.reshape(1, HEAD_DIM), (1, width // HEAD_DIM))
    gqa, gka = tiled(w["a_q_norm"], QW), tiled(w["a_k_norm"], _kv_width("A"))
    gqb, gkb = tiled(w["b_q_norm"], QW), tiled(w["b_k_norm"], _kv_width("B"))
    sinks = w["b_sinks"].reshape(B_Q_HEADS)
    bias_b = _alibi_bias()

    ffn1_names = ("ffn1_w_gu", "ffn1_w_down")
    shard = {nme: _local(w[nme], nme).astype(BF16) for nme in ffn1_names}
    send1, recv1, bufs, token = _gather_start([shard["ffn1_w_gu"]], h0, "gather_start_ffn1_gu")
    dsend1, drecv1, dbufs, token = _gather_start([shard["ffn1_w_down"]], token, "gather_start_ffn1_down")
    zero = token[0, 0]
    shard.update({nme: (_local(w[nme], nme) + zero).astype(BF16) for nme in BIG_NAMES if nme not in ffn1_names})
    bias_a = _rel_bias_expand(w["a_rel_bias"][0] + zero, "rel_bias_expand")
    send2, recv2, bufs, token = _gather_pass(send1, recv1, bufs, bias_a, "gather_pass_ffn1_gu")
    (wgu1,) = _gather_wait(send2, recv2, bufs, shard["ffn2_w_gu"], "gather_wait_ffn1_gu")
    nf = wgu1.shape[1]
    mixer_names = ("w_in", "w_gate")
    rest_names = ("w_proj_a", "w_proj_b", "w_out", "ffn2_w_gu", "ffn2_w_down", "w_ple_gate", "w_ple_proj")
    send1, recv1, bufs, token = _gather_start([shard[nme] for nme in mixer_names], wgu1, "gather_start_mixer")
    rsend1, rrecv1, rest_bufs, token = _gather_start([shard[nme] for nme in rest_names], token, "gather_start_rest")

    gu1, a1f = _ffn_up(h0, g_ffn1 + token[0, 0], wgu1, "ffn1_up")
    dsend2, drecv2, dbufs, token = _gather_pass(dsend1, drecv1, dbufs, a1f, "gather_pass_ffn1_down")
    (wd1,) = _gather_wait(dsend2, drecv2, dbufs, token, "gather_wait_ffn1_down")
    wd1 = wd1.reshape(N_DEV // 2, nf, d)
    h1 = _ffn_down(h0, a1f, wd1, "ffn1_down")
    send2, recv2, bufs, token = _gather_pass(send1, recv1, bufs, h1, "gather_pass_mixer")
    win, wgate = _gather_wait(send2, recv2, bufs, token, "gather_wait_mixer")
    win, wgate = win.reshape(IN_COLS, d), _full_cols(wgate)
    un, qkv, gate = _proj_fwd(h1, g_mix, win, wgate, "proj_fwd")
    ya = _attn_fwd("A", qkv, gqa, gka, bias_a, sinks, bl, s_len, "attn_a_fwd")
    rsend2, rrecv2, rest_bufs, token = _gather_pass(rsend1, rrecv1, rest_bufs, ya, "gather_pass_rest")
    yb = _attn_fwd("B", qkv, gqb + token[0, 0], gkb, bias_b, sinks, bl, s_len, "attn_b_fwd")
    gathered = dict(zip(rest_names, _gather_wait(rsend2, rrecv2, rest_bufs, yb, "gather_wait_rest")))
    wgu2 = gathered["ffn2_w_gu"]
    wd2 = gathered["ffn2_w_down"].reshape(N_DEV // 2, nf, d)
    wpa = _full_cols(gathered["w_proj_a"])
    wpb = _full_cols(gathered["w_proj_b"])
    wpe = _full_cols(gathered["w_ple_proj"])
    wout = gathered["w_out"].reshape(d, d)
    wpg = gathered["w_ple_gate"].reshape(d, d)
    h2, merged, pa, pb = _merge_fwd(h1, ya, yb, gate, wpa, wpb, wout, "merge_fwd")
    h3, gu2 = _ffn_fwd(h2, g_ffn2, wgu2, wd2, "ffn2_fwd")
    dh3, dz4, dpp, n4, dg_ple, loss_part = _ple_loss(h3, g_ple, pt, tgt, wpg, wpe, "ple_loss")

    xi, yi, ci = _place()
    me = jnp.stack([4 * xi + 2 * yi + ci, 2 * xi + yi]).astype(jnp.int32)
    g32, g16, big, pairs = {}, {}, {}, {}

    def keep(nme, pair, rows=None):
        for store, g in zip((g32, g16), pair):
            store[nme] = g if rows is None else g.reshape(N_DEV, rows, d)

    def start(names, after, tag):
        send, recv, parts, lands, token = _scatter_start([g16[nme] for nme in names], after, "grads_start_" + tag)
        return names, send, recv, parts, lands, token

    def start_two_level(names, after, tag):
        views = [g16[nme].reshape((4, 2) + g16[nme].shape[1:]) for nme in names]
        for nme, got in zip(names, _pair_exchange(views, "grads_pair_" + tag)):
            pairs[nme] = got.reshape((4,) + got.shape[2:])
        sums = [_pair_sum(g32[nme], pairs[nme], me, "pair_sum_" + nme) for nme in names]
        send, recv, parts, lands, token = _scatter_start(sums, after, "grads_start_" + tag, SAME_CORE_CHIPS)
        return names, send, recv, parts, lands, token

    def finish(state, after, tag):
        names, send, recv, parts, lands, _ = state
        relations = SAME_CORE_CHIPS if names[0] in pairs else ALL_PEERS
        lands = _scatter_wait(send, recv, parts, lands, after, "grads_wait_" + tag, relations)
        return names, lands

    def adam(done, dep):
        for nme, land in zip(*done):
            outs = _final_adam(g32[nme], land, _local(w[nme], nme), _local(m[nme], nme), _local(v[nme], nme), me, dep,
                               "adam_" + nme, pairs.get(nme))
            big[nme] = [(o.T if nme in TRANSPOSED else o)[None] for o in outs]

    keep("w_ple_gate", _dw(n4, dz4, 1, d, "dw_ple_gate"), d // N_DEV)
    keep("w_ple_proj", _dw(pt, dpp, N_DEV, d // N_DEV, "dw_ple_proj"))
    early = [(start(("w_ple_gate", "w_ple_proj"), dh3, "ple"), "ple")]

    dh2, dgu2, a2, n3, dg_ffn2 = _ffn_bwd(dh3, h2, g_ffn2 + early[-1][0][-1][0, 0], gu2, wgu2, wd2, "ffn2_bwd")
    keep("ffn2_w_down", _dw(a2, dh3, N_DEV // 2, d, "dw_ffn2_down", 0.5), nf // 2)
    early.append((start(("ffn2_w_down",), dh2, "ffn2_down"), "ffn2_down"))
    keep("ffn2_w_gu", _dw(dgu2, n3, N_DEV, d, "dw_ffn2_gu", dep=early[-1][0][-1]))
    flight = start(("ffn2_w_gu",), dh2, "ffn2")

    dpa, dpb, dzg, dya, dyb = _merge_bwd(dh2, pa, pb, gate, wpa, wpb, wout, "merge_bwd")
    keep("w_out", _dw(merged, dh2, 1, d, "dw_out"), d // N_DEV)
    keep("w_proj_a", _dw(ya, dpa, N_DEV, d // N_DEV, "dw_proj_a"))
    keep("w_proj_b", _dw(yb, dpb, N_DEV, d // N_DEV, "dw_proj_b"))
    keep("w_gate", _dw(un, dzg, N_DEV, 2 * d // N_DEV, "dw_gate"))

    tok = flight[-1][0, 0]
    dqa, dka, dva, dgqa, dgka, dbias, _ = _attn_bwd("A", qkv, gqa + tok, gka, bias_a, sinks, ya, dya, bl, s_len,
                                                     "attn_a_bwd")
    dqb, dkb, dvb, dgqb, dgkb, _, dsink = _attn_bwd("B", qkv, gqb, gkb, bias_b, sinks, yb, dyb, bl, s_len, "attn_b_bwd")
    dqkv = [dqa, dka, dva, dqb, dkb, dvb]
    dtab = _rel_bias_grad(dbias, "rel_bias_grad")

    dh1, dg_mix = _proj_bwd(dh2, h1, g_mix, dzg, dqkv, win, wgate, "proj_bwd")
    keep("w_in", _dw_rows(dqkv, un, "dw_in"), IN_COLS // N_DEV)
    waiting = [finish(state, g32["w_in"], tag) for state, tag in early]
    done = finish(flight, waiting[-1][1][0], "ffn2")
    flight = start(("w_out", "w_proj_a", "w_proj_b", "w_gate", "w_in"), done[1][0], "mixer")
    waiting.append(done)

    dh0, dgu1, a1, n1, dg_ffn1 = _ffn_bwd(dh1, h0, g_ffn1 + flight[-1][0, 0], gu1, wgu1, wd1, "ffn1_bwd")
    keep("ffn1_w_down", _dw(a1, dh1, N_DEV // 2, d, "dw_ffn1_down", 0.5), nf // 2)
    done = finish(flight, g32["ffn1_w_down"], "mixer")
    flight = start(("ffn1_w_down",), done[1][0], "ffn1_down")
    waiting.append(done)

    keep("ffn1_w_gu", _dw(dgu1, n1, N_DEV, d, "dw_ffn1_gu", dep=flight[-1]))
    done = finish(flight, g32["ffn1_w_gu"], "ffn1_down")
    flight = start_two_level(("ffn1_w_gu",), done[1][0], "ffn1_gu")
    for group in waiting + [done]:
        adam(group, flight[-1])
    behind = 0.0 * big["ffn1_w_down"][0][0, 0, :1]
    smalls = (dg_ffn1, dg_mix, dg_ffn2, dg_ple + behind, dgqa, dgka, dgqb, dgkb, dtab, dsink)
    return dh0, loss_part, big, smalls, flight, finish, adam


def kernel(x, p, ffn1_norm, ffn1_w_gu, ffn1_w_down, mix_norm, w_in, a_q_norm, a_k_norm, a_rel_bias, b_q_norm, b_k_norm, b_sinks, w_gate, w_proj_a, w_proj_b, w_out, ffn2_norm, ffn2_w_gu, ffn2_w_down, ple_norm, w_ple_gate, w_ple_proj, loss_target, m_ffn1_norm, m_ffn1_w_gu, m_ffn1_w_down, m_mix_norm, m_w_in, m_a_q_norm, m_a_k_norm, m_a_rel_bias, m_b_q_norm, m_b_k_norm, m_b_sinks, m_w_gate, m_w_proj_a, m_w_proj_b, m_w_out, m_ffn2_norm, m_ffn2_w_gu, m_ffn2_w_down, m_ple_norm, m_w_ple_gate, m_w_ple_proj, v_ffn1_norm, v_ffn1_w_gu, v_ffn1_w_down, v_mix_norm, v_w_in, v_a_q_norm, v_a_k_norm, v_a_rel_bias, v_b_q_norm, v_b_k_norm, v_b_sinks, v_w_gate, v_w_proj_a, v_w_proj_b, v_w_out, v_ffn2_norm, v_ffn2_w_gu, v_ffn2_w_down, v_ple_norm, v_w_ple_gate, v_w_ple_proj):
    w = dict(ffn1_norm=ffn1_norm, ffn1_w_gu=ffn1_w_gu, ffn1_w_down=ffn1_w_down, mix_norm=mix_norm, w_in=w_in,
             a_q_norm=a_q_norm, a_k_norm=a_k_norm, a_rel_bias=a_rel_bias, b_q_norm=b_q_norm, b_k_norm=b_k_norm,
             b_sinks=b_sinks, w_gate=w_gate, w_proj_a=w_proj_a, w_proj_b=w_proj_b, w_out=w_out, ffn2_norm=ffn2_norm,
             ffn2_w_gu=ffn2_w_gu, ffn2_w_down=ffn2_w_down, ple_norm=ple_norm, w_ple_gate=w_ple_gate,
             w_ple_proj=w_ple_proj)
    m = dict(ffn1_norm=m_ffn1_norm, ffn1_w_gu=m_ffn1_w_gu, ffn1_w_down=m_ffn1_w_down, mix_norm=m_mix_norm,
             w_in=m_w_in, a_q_norm=m_a_q_norm, a_k_norm=m_a_k_norm, a_rel_bias=m_a_rel_bias, b_q_norm=m_b_q_norm,
             b_k_norm=m_b_k_norm, b_sinks=m_b_sinks, w_gate=m_w_gate, w_proj_a=m_w_proj_a, w_proj_b=m_w_proj_b,
             w_out=m_w_out, ffn2_norm=m_ffn2_norm, ffn2_w_gu=m_ffn2_w_gu, ffn2_w_down=m_ffn2_w_down,
             ple_norm=m_ple_norm, w_ple_gate=m_w_ple_gate, w_ple_proj=m_w_ple_proj)
    v = dict(ffn1_norm=v_ffn1_norm, ffn1_w_gu=v_ffn1_w_gu, ffn1_w_down=v_ffn1_w_down, mix_norm=v_mix_norm,
             w_in=v_w_in, a_q_norm=v_a_q_norm, a_k_norm=v_a_k_norm, a_rel_bias=v_a_rel_bias, b_q_norm=v_b_q_norm,
             b_k_norm=v_b_k_norm, b_sinks=v_b_sinks, w_gate=v_w_gate, w_proj_a=v_w_proj_a, w_proj_b=v_w_proj_b,
             w_out=v_w_out, ffn2_norm=v_ffn2_norm, ffn2_w_gu=v_ffn2_w_gu, ffn2_w_down=v_ffn2_w_down,
             ple_norm=v_ple_norm, w_ple_gate=v_w_ple_gate, w_ple_proj=v_w_ple_proj)
    bl, s_len, d = x.shape

    dh0, loss_part, big, smalls, flight, finish, adam = _step(x, p[0], loss_target, w, m, v)
    dg_ffn1, dg_mix, dg_ffn2, dg_ple, dgqa, dgka, dgqb, dgkb, dtab, dsink = smalls

    fold = lambda a: a[:, :, 0, :].reshape(-1, HEAD_DIM).sum(axis=0)
    small_part = dict(
        ffn1_norm=dg_ffn1, mix_norm=dg_mix, ffn2_norm=dg_ffn2, ple_norm=dg_ple,
        a_q_norm=fold(dgqa), a_k_norm=fold(dgka), b_q_norm=fold(dgqb), b_k_norm=fold(dgkb),
        a_rel_bias=dtab,
        b_sinks=dsink.sum(axis=0)[:, 0, :GROUP].reshape(B_Q_HEADS),
        loss=loss_part[0, :1])
    zero1 = jnp.zeros((1,), F32)
    shapes = {nme: w[nme].shape for nme in SMALL_NAMES if nme != "loss"}
    shapes["loss"] = ()
    pk = lambda src: _pack_small({**{nme: src[nme] for nme in SMALL_NAMES if nme != "loss"}, "loss": zero1})
    sg, sd, sm, sv = _small_allreduce_adam(_pack_small(small_part), pk(w), pk(m), pk(v), "small_allreduce_adam")
    adam(finish(flight, sg, "ffn1_gu"), sg)
    sg, sd, sm, sv = (_unpack_small(a, shapes) for a in (sg, sd, sm, sv))

    def pick(i):
        out = []
        for nme in WEIGHT_ORDER:
            out.append(big[nme][i] if nme in big else (sg, sd, sm, sv)[i][nme])
        return out

    return (sg["loss"], dh0.reshape(bl, s_len, d), *pick(0), *pick(1), *pick(2), *pick(3))
```

```python
import jax
import jax.numpy as jnp
import numpy as np
from jax import lax
from jax.experimental import pallas as pl
from jax.experimental.pallas import tpu as pltpu

F32 = jnp.float32
BF16 = jnp.bfloat16

CHUNK = 64
HEAD_DIM = 64
A_HEADS = 8
A_PREV = 8
A_MAX_REL = 128
B_Q_HEADS = 8
B_KV_HEADS = 2
B_PREV = 2
A_WIDTH = A_HEADS * HEAD_DIM
B_Q_WIDTH = B_Q_HEADS * HEAD_DIM
B_KV_WIDTH = B_KV_HEADS * HEAD_DIM
IN_COLS = 3 * A_WIDTH + B_Q_WIDTH + 2 * B_KV_WIDTH
EPS = 1e-6
NEG_INF = -1e30
ADAM_LR = 0.001
ADAM_B1 = 0.9
ADAM_B2 = 0.999
ADAM_EPS = 1e-08
ADAM_WD = 0.01
ADAM_STEP = 10

N_DEV = 8
LANES = 128
QTILE = 2 * CHUNK
VMEM_LIMIT = 56 * 1024 * 1024
DW_VMEM_LIMIT = 60 * 1024 * 1024
ADAM_TILE_ELEMS = 256 * 1024

MESH_ID = pl.DeviceIdType.MESH
ANY = pl.BlockSpec(memory_space=pl.ANY)
HBM = pl.BlockSpec(memory_space=pltpu.HBM)
SEM = pl.BlockSpec(memory_space=pltpu.SEMAPHORE)
SIDE_EFFECT = pltpu.SideEffectType.DATAFLOW_SIDE_EFFECTING


def _dot(a, b):
    return jnp.dot(a, b, preferred_element_type=F32)


def _dot_nt(a, b):
    return lax.dot_general(a, b, (((1,), (1,)), ((), ())), preferred_element_type=F32)


def _dot_tn(a, b):
    return lax.dot_general(a, b, (((0,), (0,)), ((), ())), preferred_element_type=F32)


def _params(sem=None, vmem=VMEM_LIMIT):
    return pltpu.CompilerParams(dimension_semantics=sem, vmem_limit_bytes=vmem)


def _row_tile(t, want):
    while t % want:
        want //= 2
    return want


def _place():
    return lax.axis_index("x"), lax.axis_index("y"), lax.axis_index("c")


def _gather_level(bufs, send_sems, recv_sems, level, shards=None):
    x, y, c = _place()
    me, sib = (x, y, c), (x, y, 1 - c)
    chips = [(1 - x, y), (x, 1 - y), (1 - x, 1 - y)]

    def copy(w, k, block, to):
        px, py, pc = block
        rows = bufs[w].at[4 * px + 2 * py + pc]
        src = shards[w] if shards is not None and block is me else rows
        return pltpu.make_async_remote_copy(src_ref=src, dst_ref=rows, send_sem=send_sems.at[k], recv_sem=recv_sems.at[k],
                                            device_id=to, device_id_type=MESH_ID)

    n = len(bufs)
    own = []
    if level == 1:
        own = [pltpu.make_async_copy(bufs[w].at[4 * x + 2 * y + c] if shards is None else shards[w],
                                     bufs[w].at[4 * x + 2 * y + c], send_sems.at[4 * n + w]) for w in range(n)]
    out, arriving = [], []
    for w in range(len(bufs)):
        if level == 1:
            out.append(copy(w, 4 * w, me, sib))
            arriving.append(copy(w, 4 * w, sib, me))
        for j, chip in enumerate(chips):
            if level == 1:
                out.append(copy(w, 4 * w + 1 + j, me, (*chip, c)))
                arriving.append(copy(w, 4 * w + 1 + j, (*chip, c), me))
            else:
                out.append(copy(w, 3 * w + j, (*chip, c), sib))
                arriving.append(copy(w, 3 * w + j, (*chip, 1 - c), me))
    return out, arriving, own


def _split_call(body, name, bufs, sems_in, after, n_sems_out, token, extra=()):
    n = len(bufs)
    out_shape = [pltpu.SemaphoreType.DMA((n_sems_out,))] * (2 if n_sems_out else 0)
    out_shape += [pltpu.HBM(a.shape, a.dtype) for a in bufs]
    out_specs = [SEM] * (2 if n_sems_out else 0) + [HBM] * n
    if token:
        out_shape.append(jax.ShapeDtypeStruct((8, LANES), F32))
        out_specs.append(pl.BlockSpec(memory_space=pltpu.VMEM))
    first = 2 if n_sems_out else 0
    return pl.pallas_call(
        body, name=name, out_shape=tuple(out_shape),
        in_specs=[HBM] * (n + len(extra)) + [SEM] * len(sems_in) + [ANY], out_specs=tuple(out_specs),
        input_output_aliases={i: first + i for i in range(n)},
        compiler_params=pltpu.CompilerParams(has_side_effects=SIDE_EFFECT),
    )(*bufs, *extra, *sems_in, after)


def _gather_start(shards, after, name):
    n = len(shards)
    hbm = lambda a: pltpu.with_memory_space_constraint(a, pltpu.HBM)
    bufs = [hbm(lax.empty((N_DEV,) + s.shape, s.dtype)) for s in shards]

    def body(*refs):
        out, _, own = _gather_level(refs[:n], refs[2 * n + 1], refs[2 * n + 2], 1, shards=refs[n:2 * n])
        for cp in own + out:
            cp.start()
        refs[-1][...] = jnp.zeros_like(refs[-1])

    outs = _split_call(body, name, bufs + [hbm(s) for s in shards], [], after, 5 * n, True)
    return outs[0], outs[1], list(outs[2:2 + 2 * n]), outs[-1]


def _gather_pass(send1, recv1, bufs_and_shards, after, name):
    n = len(bufs_and_shards) // 2
    bufs = bufs_and_shards

    def body(*refs):
        refs = refs[:n] + refs[2 * n:]
        out1, in1, own = _gather_level(refs[:n], refs[n], refs[n + 1], 1)
        out2, _, _ = _gather_level(refs[:n], refs[n + 3], refs[n + 4], 2)
        for cp in in1:
            cp.wait_recv()
        for cp in out2:
            cp.start()
        for cp in out1:
            cp.wait_send()
        for cp in own:
            cp.wait()
        refs[-1][...] = jnp.zeros_like(refs[-1])

    outs = _split_call(body, name, bufs, [send1, recv1], after, 3 * n, True)
    return outs[0], outs[1], list(outs[2:2 + n]), outs[-1]


def _gather_wait(send2, recv2, bufs, after, name):
    n = len(bufs)

    def body(*refs):
        out2, in2, _ = _gather_level(refs[:n], refs[n], refs[n + 1], 2)
        for cp in in2:
            cp.wait_recv()
        for cp in out2:
            cp.wait_send()

    return list(_split_call(body, name, bufs, [send2, recv2], after, 0, False))


ALL_PEERS = tuple(range(1, N_DEV))
SAME_CORE_CHIPS = (2, 4, 6)


def _scatter_copies(parts, lands, send_sems, recv_sems, relations):
    x, y, c = _place()
    ns = len(relations)
    cps = []
    for w, (part, land) in enumerate(zip(parts, lands)):
        for i, k in enumerate(relations):
            px, py, pc = x ^ ((k >> 2) & 1), y ^ ((k >> 1) & 1), c ^ (k & 1)
            block = 4 * px + 2 * py + pc if part.shape[0] == N_DEV else 2 * px + py
            cps.append(pltpu.make_async_remote_copy(
                src_ref=part.at[block], dst_ref=land.at[i],
                send_sem=send_sems.at[ns * w + i], recv_sem=recv_sems.at[ns * w + i],
                device_id=(px, py, pc), device_id_type=MESH_ID))
    return cps


def _scatter_start(parts, after, name, relations=ALL_PEERS):
    n = len(parts)
    ns = len(relations)

    def body(*refs):
        ins, lands = refs[:n], refs[n:2 * n]
        send_sems, recv_sems = refs[2 * n + 1], refs[2 * n + 2]
        token = refs[-1]
        for cp in _scatter_copies(ins, lands, send_sems, recv_sems, relations):
            cp.start()
        token[...] = jnp.zeros_like(token)

    land_shapes = [(ns,) + p.shape[1:] for p in parts]
    in_hbm = [pltpu.with_memory_space_constraint(p, pltpu.HBM) for p in parts]
    in_hbm += [pltpu.with_memory_space_constraint(lax.empty(s, p.dtype), pltpu.HBM) for s, p in zip(land_shapes, parts)]
    outs = pl.pallas_call(
        body, name=name,
        out_shape=(pltpu.SemaphoreType.DMA((ns * n,)), pltpu.SemaphoreType.DMA((ns * n,)),
                   *[pltpu.HBM(p.shape, p.dtype) for p in parts],
                   *[pltpu.HBM(s, p.dtype) for s, p in zip(land_shapes, parts)],
                   jax.ShapeDtypeStruct((8, LANES), F32)),
        in_specs=[HBM] * (2 * n) + [ANY],
        out_specs=(SEM, SEM, *[HBM] * (2 * n), pl.BlockSpec(memory_space=pltpu.VMEM)),
        input_output_aliases={i: 2 + i for i in range(2 * n)},
        compiler_params=pltpu.CompilerParams(has_side_effects=SIDE_EFFECT),
    )(*in_hbm, after)
    return outs[0], outs[1], list(outs[2:2 + n]), list(outs[2 + n:2 + 2 * n]), outs[-1]


def _scatter_wait(send_sems, recv_sems, parts, lands, after, name, relations=ALL_PEERS):
    n = len(parts)

    def body(*refs):
        ins, lnd = refs[:n], refs[n:2 * n]
        for cp in _scatter_copies(ins, lnd, refs[2 * n], refs[2 * n + 1], relations):
            cp.wait_send()
            cp.wait_recv()

    outs = pl.pallas_call(
        body, name=name,
        out_shape=tuple(pltpu.HBM(a.shape, a.dtype) for a in parts + lands),
        in_specs=[HBM] * (2 * n) + [SEM, SEM, ANY],
        out_specs=tuple([HBM] * (2 * n)),
        input_output_aliases={i: i for i in range(2 * n)},
        compiler_params=pltpu.CompilerParams(has_side_effects=SIDE_EFFECT),
    )(*parts, *lands, send_sems, recv_sems, after)
    return list(outs[n:])


def _pair_exchange(parts, name):
    n = len(parts)

    def body(*refs):
        ins, outs = refs[:n], refs[n:2 * n]
        send_sems, recv_sems = refs[2 * n:]
        x, y, c = _place()
        cps = [pltpu.make_async_remote_copy(
            src_ref=ins[w].at[:, pl.ds(1 - c, 1)], dst_ref=outs[w], send_sem=send_sems.at[w], recv_sem=recv_sems.at[w],
            device_id=(x, y, 1 - c), device_id_type=MESH_ID) for w in range(n)]
        for cp in cps:
            cp.start()
        for cp in cps:
            cp.wait()

    return pl.pallas_call(
        body, name=name,
        out_shape=[jax.ShapeDtypeStruct((4, 1) + p.shape[2:], p.dtype) for p in parts],
        in_specs=[ANY] * n, out_specs=[ANY] * n,
        scratch_shapes=[pltpu.SemaphoreType.DMA((n,)), pltpu.SemaphoreType.DMA((n,))],
    )(*parts)


def _pair_sum(g8, r1, me, name):
    _, r, c = g8.shape
    tr = max(q for q in range(16, r + 1, 16) if r % q == 0 and q * c <= ADAM_TILE_ELEMS)

    def body(me_ref, g_ref, r_ref, o_ref):
        o_ref[...] = (g_ref[...] + r_ref[...].astype(F32)).astype(BF16)

    chip = lambda k, s: s[1] ^ (k + 1)
    return pl.pallas_call(
        body, name=name,
        out_shape=jax.ShapeDtypeStruct((4, r, c), BF16),
        grid_spec=pltpu.PrefetchScalarGridSpec(
            num_scalar_prefetch=1, grid=(3, r // tr),
            in_specs=[pl.BlockSpec((None, None, tr, c), lambda k, i, s: (chip(k, s), s[0] % 2, i, 0)),
                      pl.BlockSpec((None, tr, c), lambda k, i, s: (chip(k, s), i, 0))],
            out_specs=pl.BlockSpec((None, tr, c), lambda k, i, s: (chip(k, s), i, 0))),
        compiler_params=_params(("arbitrary", "arbitrary")),
    )(me, g8.reshape((4, 2) + g8.shape[1:]), r1)


def _adam(w, g, m, v):
    m2 = ADAM_B1 * m + (1.0 - ADAM_B1) * g
    v2 = ADAM_B2 * v + (1.0 - ADAM_B2) * (g * g)
    m_hat = m2 / (1.0 - ADAM_B1 ** ADAM_STEP)
    v_hat = v2 / (1.0 - ADAM_B2 ** ADAM_STEP)
    delta = -ADAM_LR * (m_hat / (jnp.sqrt(v_hat) + ADAM_EPS) + ADAM_WD * w)
    return delta, m2, v2


def _small_allreduce_adam(part, w, m, v, name):
    rows = part.shape[0]

    def body(p_ref, w_ref, m_ref, v_ref, g_ref, d_ref, mo_ref, vo_ref, buf, send_sems, recv_sems):
        x, y, c = _place()
        buf[0] = p_ref[...]
        cps = []
        for k in range(1, N_DEV):
            kx, ky, kc = (k >> 2) & 1, (k >> 1) & 1, k & 1
            peer = (x ^ kx, y ^ ky, c ^ kc)
            cps.append(pltpu.make_async_remote_copy(
                src_ref=p_ref, dst_ref=buf.at[k], send_sem=send_sems.at[k - 1], recv_sem=recv_sems.at[k - 1],
                device_id=peer, device_id_type=MESH_ID))
        for cp in cps:
            cp.start()
        for cp in cps:
            cp.wait()
        me = 4 * x + 2 * y + c
        total = buf[me]
        for d in range(1, N_DEV):
            total = total + buf[d ^ me]
        g_ref[...] = total
        delta, m2, v2 = _adam(w_ref[...], total, m_ref[...], v_ref[...])
        d_ref[...] = delta
        mo_ref[...] = m2
        vo_ref[...] = v2

    vm = pl.BlockSpec(memory_space=pltpu.VMEM)
    return pl.pallas_call(
        body, name=name,
        out_shape=[jax.ShapeDtypeStruct(part.shape, F32)] * 4,
        in_specs=[vm] * 4, out_specs=[vm] * 4,
        scratch_shapes=[pltpu.VMEM((N_DEV, rows, LANES), F32),
                        pltpu.SemaphoreType.DMA((N_DEV - 1,)), pltpu.SemaphoreType.DMA((N_DEV - 1,))],
    )(part, w, m, v)


def _final_adam(g8, land, w, m, v, me, dep, name, pair=None):
    _, r, c = g8.shape
    tr = max(q for q in range(16, r + 1, 16) if r % q == 0 and q * c <= ADAM_TILE_ELEMS)
    nland = land.shape[0]

    def body(me_ref, g_ref, land_ref, *rest):
        pair_ref = rest[0] if pair is not None else None
        w_ref, m_ref, v_ref, _, go_ref, d_ref, mo_ref, vo_ref = rest[-8:]
        g = g_ref[...]
        if pair_ref is not None:
            g = g + pair_ref[...].astype(F32)
        for k in range(nland):
            g = g + land_ref[k].astype(F32)
        go_ref[...] = g
        delta, m2, v2 = _adam(w_ref[...], g, m_ref[...], v_ref[...])
        d_ref[...] = delta
        mo_ref[...] = m2
        vo_ref[...] = v2

    plain = pl.BlockSpec((tr, c), lambda i, s: (i, 0))
    return pl.pallas_call(
        body, name=name,
        out_shape=[jax.ShapeDtypeStruct((r, c), F32)] * 4,
        grid_spec=pltpu.PrefetchScalarGridSpec(
            num_scalar_prefetch=1, grid=(r // tr,),
            in_specs=[pl.BlockSpec((None, tr, c), lambda i, s: (s[0], i, 0)),
                      pl.BlockSpec((nland, tr, c), lambda i, s: (0, i, 0))]
            + ([] if pair is None else [pl.BlockSpec((None, tr, c), lambda i, s: (s[1], i, 0))])
            + [plain, plain, plain, ANY],
            out_specs=[plain] * 4),
        compiler_params=_params(("arbitrary",)),
    )(*((me, g8, land) + (() if pair is None else (pair,)) + (w, m, v, dep)))


def _rms(x, gain):
    r = lax.rsqrt(jnp.mean(x * x, axis=-1, keepdims=True) + EPS)
    xh = x * r
    return xh * gain, xh, r


def _rms_bwd(xh, r, gain, dy):
    gdy = gain * dy
    dx = r * (gdy - xh * jnp.mean(xh * gdy, axis=-1, keepdims=True))
    return dx, jnp.sum(dy * xh, axis=0, keepdims=True)


def _load_weights(pairs, sems):
    cps = [pltpu.make_async_copy(src, dst, sems.at[i]) for i, (src, dst) in enumerate(pairs)]
    for cp in cps:
        cp.start()
    for cp in cps:
        cp.wait()


def _ffn_fwd(h, gain, wgu, wd, name):
    t, d = h.shape
    nb, nf, _ = wgu.shape
    nh = nb // 2
    tm = _row_tile(t, 512)

    def body(h_ref, g_ref, wgu_hbm, wd_hbm, out_ref, gu_ref, wgu_v, wd_v, sems):
        @pl.when(pl.program_id(0) == 0)
        def _():
            _load_weights([(wgu_hbm, wgu_v), (wd_hbm, wd_v)], sems)

        x = h_ref[...]
        n, _, _ = _rms(x, g_ref[...])
        nbf = n.astype(BF16)
        acc = jnp.zeros((tm, d), F32)
        for j in range(nh):
            g = _dot_nt(nbf, wgu_v[j])
            u = _dot_nt(nbf, wgu_v[j + nh])
            gu_ref[j] = g.astype(BF16)
            gu_ref[j + nh] = u.astype(BF16)
            a = (g * jax.nn.sigmoid(g)) * u
            acc = acc + _dot(a.astype(BF16), wd_v[j])
        out_ref[...] = x + 0.5 * acc

    return pl.pallas_call(
        body, name=name, grid=(t // tm,),
        out_shape=[jax.ShapeDtypeStruct((t, d), F32), jax.ShapeDtypeStruct((nb, t, nf), BF16)],
        in_specs=[pl.BlockSpec((tm, d), lambda i: (i, 0)), pl.BlockSpec((1, d), lambda i: (0, 0)), ANY, ANY],
        out_specs=[pl.BlockSpec((tm, d), lambda i: (i, 0)), pl.BlockSpec((nb, tm, nf), lambda i: (0, i, 0))],
        scratch_shapes=[pltpu.VMEM(wgu.shape, BF16), pltpu.VMEM(wd.shape, BF16), pltpu.SemaphoreType.DMA((2,))],
        compiler_params=_params(("arbitrary",)),
    )(h, gain, wgu, wd)


def _ffn_up(h, gain, wgu, name):
    t, d = h.shape
    nb, nf, _ = wgu.shape
    nh = nb // 2
    tm = _row_tile(t, 512)

    def body(h_ref, g_ref, wgu_hbm, gu_ref, a_ref, wgu_v, sems):
        @pl.when(pl.program_id(0) == 0)
        def _():
            _load_weights([(wgu_hbm, wgu_v)], sems)

        n, _, _ = _rms(h_ref[...], g_ref[...])
        nbf = n.astype(BF16)
        for j in range(nh):
            g = _dot_nt(nbf, wgu_v[j])
            u = _dot_nt(nbf, wgu_v[j + nh])
            gu_ref[j] = g.astype(BF16)
            gu_ref[j + nh] = u.astype(BF16)
            a_ref[j] = ((g * jax.nn.sigmoid(g)) * u).astype(BF16)

    return pl.pallas_call(
        body, name=name, grid=(t // tm,),
        out_shape=[jax.ShapeDtypeStruct((nb, t, nf), BF16), jax.ShapeDtypeStruct((nh, t, nf), BF16)],
        in_specs=[pl.BlockSpec((tm, d), lambda i: (i, 0)), pl.BlockSpec((1, d), lambda i: (0, 0)), ANY],
        out_specs=[pl.BlockSpec((nb, tm, nf), lambda i: (0, i, 0)), pl.BlockSpec((nh, tm, nf), lambda i: (0, i, 0))],
        scratch_shapes=[pltpu.VMEM(wgu.shape, BF16), pltpu.SemaphoreType.DMA((1,))],
        compiler_params=_params(("arbitrary",)),
    )(h, gain, wgu)


def _ffn_down(h, a, wd, name):
    t, d = h.shape
    nh, nf, _ = wd.shape
    tm = _row_tile(t, 512)

    def body(h_ref, a_ref, wd_ref, out_ref):
        acc = jnp.zeros((tm, d), F32)
        for j in range(nh):
            acc = acc + _dot(a_ref[j], wd_ref[j])
        out_ref[...] = h_ref[...] + 0.5 * acc

    row = pl.BlockSpec((tm, d), lambda i: (i, 0))
    return pl.pallas_call(
        body, name=name, grid=(t // tm,),
        out_shape=jax.ShapeDtypeStruct((t, d), F32),
        in_specs=[row, pl.BlockSpec((nh, tm, nf), lambda i: (0, i, 0)), pl.BlockSpec(wd.shape, lambda i: (0, 0, 0))],
        out_specs=row,
        compiler_params=_params(("arbitrary",)),
    )(h, a, wd)


def _ffn_bwd(dh, h, gain, gu, wgu, wd, name, emit_a=True):
    t, d = h.shape
    nb, nf, _ = wgu.shape
    nh = nb // 2
    tm = _row_tile(t, 256)

    def body(dh_ref, h_ref, g_ref, gu_ref, wgu_hbm, wd_hbm, dhp_ref, dgu_ref, *rest):
        a_ref = rest[0] if emit_a else None
        n_ref, dgain_ref, wgu_v, wd_v, sems = rest[-5:]

        @pl.when(pl.program_id(0) == 0)
        def _():
            _load_weights([(wgu_hbm, wgu_v), (wd_hbm, wd_v)], sems)
            dgain_ref[...] = jnp.zeros_like(dgain_ref)

        x = h_ref[...]
        gain_v = g_ref[...]
        n, xh, r = _rms(x, gain_v)
        n_ref[...] = n.astype(BF16)
        dh_v = dh_ref[...]
        dfb = (0.5 * dh_v).astype(BF16)
        dn = jnp.zeros((tm, d), F32)
        for j in range(nh):
            da = _dot_nt(dfb, wd_v[j])
            g = gu_ref[j].astype(F32)
            u = gu_ref[j + nh].astype(F32)
            sg = jax.nn.sigmoid(g)
            si = g * sg
            dg = (da * u * (sg * (1.0 + g * (1.0 - sg)))).astype(BF16)
            du = (da * si).astype(BF16)
            if emit_a:
                a_ref[j] = (si * u).astype(BF16)
            dgu_ref[j] = dg
            dgu_ref[j + nh] = du
            dn = dn + _dot(dg, wgu_v[j]) + _dot(du, wgu_v[j + nh])
        dx, dgain = _rms_bwd(xh, r, gain_v, dn)
        dhp_ref[...] = dh_v + dx
        dgain_ref[...] += dgain

    row = pl.BlockSpec((tm, d), lambda i: (i, 0))
    vec = pl.BlockSpec((1, d), lambda i: (0, 0))
    a_shape = [jax.ShapeDtypeStruct((nh, t, nf), BF16)] if emit_a else []
    a_spec = [pl.BlockSpec((nh, tm, nf), lambda i: (0, i, 0))] if emit_a else []
    outs = pl.pallas_call(
        body, name=name, grid=(t // tm,),
        out_shape=[jax.ShapeDtypeStruct((t, d), F32), jax.ShapeDtypeStruct((nb, t, nf), BF16)] + a_shape
        + [jax.ShapeDtypeStruct((t, d), BF16), jax.ShapeDtypeStruct((1, d), F32)],
        in_specs=[row, row, vec, pl.BlockSpec((nb, tm, nf), lambda i: (0, i, 0)), ANY, ANY],
        out_specs=[row, pl.BlockSpec((nb, tm, nf), lambda i: (0, i, 0))] + a_spec + [row, vec],
        scratch_shapes=[pltpu.VMEM(wgu.shape, BF16), pltpu.VMEM(wd.shape, BF16), pltpu.SemaphoreType.DMA((2,))],
        compiler_params=_params(("arbitrary",)),
    )(dh, h, gain, gu, wgu, wd)
    return outs if emit_a else (outs[0], outs[1], None, outs[2], outs[3])


def _dw(xa, dy, nb, n, name, scale=1.0, dep=None):
    t, k = xa.shape[-2:]
    wide = xa.ndim == 2
    tt = _row_tile(t, 1024)
    steps = t // tt
    x_spec = pl.BlockSpec((tt, k), lambda i: (i, 0)) if wide else pl.BlockSpec((nb, tt, k), lambda i: (0, i, 0))
    dy_spec = pl.BlockSpec((tt, dy.shape[1]), lambda i: (i, 0))
    acc_shape = (k, nb * n) if wide else (nb, k, n)
    stage_shape = (k, nb * n) if wide else (k, n)

    def body(x_ref, dy_ref, *rest):
        o_hbm, ob_hbm, acc, stage, sems = rest[-5:]

        @pl.when(pl.program_id(0) == 0)
        def _():
            acc[...] = jnp.zeros_like(acc)

        dyb = dy_ref[...].astype(BF16)
        if wide:
            acc[...] += _dot(x_ref[...].astype(BF16).T, dyb)
        else:
            for j in range(nb):
                acc[j] += _dot_tn(x_ref[j].astype(BF16), dyb)

        @pl.when(pl.program_id(0) == steps - 1)
        def _():
            if scale != 1.0:
                acc[...] = acc[...] * scale
            if wide:
                cps = [pltpu.make_async_copy(acc.at[:, pl.ds(j * n, n)] if nb > 1 else acc, o_hbm.at[j], sems.at[j])
                       for j in range(nb)]
            else:
                cps = [pltpu.make_async_copy(acc, o_hbm, sems.at[0])]
            for cp in cps:
                cp.start()
            if wide:
                stage[...] = acc[...].astype(BF16)
                bcs = [pltpu.make_async_copy(stage.at[:, pl.ds(j * n, n)] if nb > 1 else stage, ob_hbm.at[j],
                                             sems.at[nb + j]) for j in range(nb)]
                for cp in bcs:
                    cp.start()
                for cp in bcs:
                    cp.wait()
            else:
                for j in range(nb):
                    stage[...] = acc[j].astype(BF16)
                    cp = pltpu.make_async_copy(stage, ob_hbm.at[j], sems.at[nb])
                    cp.start()
                    cp.wait()
            for cp in cps:
                cp.wait()

    return pl.pallas_call(
        body, name=name, grid=(steps,),
        out_shape=[jax.ShapeDtypeStruct((nb, k, n), F32), jax.ShapeDtypeStruct((nb, k, n), BF16)],
        in_specs=[x_spec, dy_spec] + ([] if dep is None else [ANY]),
        out_specs=[ANY, ANY],
        scratch_shapes=[pltpu.VMEM(acc_shape, F32), pltpu.VMEM(stage_shape, BF16),
                        pltpu.SemaphoreType.DMA((2 * nb,))],
        compiler_params=_params(("arbitrary",), DW_VMEM_LIMIT),
    )(*((xa, dy) if dep is None else (xa, dy, dep)))


def _proj_fwd(h, gain, win, wgate, name):
    t, d = h.shape
    tm = _row_tile(t, 512)
    nq, ng = win.shape[0], wgate.shape[1]

    def body(h_ref, g_ref, win_ref, wg_ref, un_ref, qkv_ref, gate_ref):
        n, _, _ = _rms(h_ref[...], g_ref[...])
        nbf = n.astype(BF16)
        un_ref[...] = nbf
        qkv_ref[...] = _dot_nt(nbf, win_ref[...])
        gate_ref[...] = jax.nn.sigmoid(_dot(nbf, wg_ref[...]))

    full = lambda a: pl.BlockSpec(a.shape, lambda i: (0,) * a.ndim)
    return pl.pallas_call(
        body, name=name, grid=(t // tm,),
        out_shape=[jax.ShapeDtypeStruct((t, d), BF16), jax.ShapeDtypeStruct((t, nq), F32),
                   jax.ShapeDtypeStruct((t, ng), F32)],
        in_specs=[pl.BlockSpec((tm, d), lambda i: (i, 0)), full(gain), full(win), full(wgate)],
        out_specs=[pl.BlockSpec((tm, d), lambda i: (i, 0)), pl.BlockSpec((tm, nq), lambda i: (i, 0)),
                   pl.BlockSpec((tm, ng), lambda i: (i, 0))],
        compiler_params=_params(("arbitrary",)),
    )(h, gain, win, wgate)


def _proj_bwd(dh, h, gain, dzg, dqkv_parts, win, wgate, name):
    t, d = h.shape
    tm = _row_tile(t, 512)
    ng = wgate.shape[1]
    np_ = len(dqkv_parts)
    widths = [a.shape[1] for a in dqkv_parts]

    def body(dh_ref, h_ref, g_ref, dzg_ref, *rest):
        part_refs, (win_ref, wg_ref, dhp_ref, dgain_ref) = rest[:np_], rest[np_:]

        @pl.when(pl.program_id(0) == 0)
        def _():
            dgain_ref[...] = jnp.zeros_like(dgain_ref)

        gain_v = g_ref[...]
        _, xh, r = _rms(h_ref[...], gain_v)
        dun = _dot_nt(dzg_ref[...], wg_ref[...])
        off = 0
        for ref, wd in zip(part_refs, widths):
            dun = dun + _dot(ref[...].astype(BF16), win_ref[off:off + wd, :])
            off += wd
        dx, dgain = _rms_bwd(xh, r, gain_v, dun)
        dhp_ref[...] = dh_ref[...] + dx
        dgain_ref[...] += dgain

    full = lambda a: pl.BlockSpec(a.shape, lambda i: (0,) * a.ndim)
    row = pl.BlockSpec((tm, d), lambda i: (i, 0))
    return pl.pallas_call(
        body, name=name, grid=(t // tm,),
        out_shape=[jax.ShapeDtypeStruct((t, d), F32), jax.ShapeDtypeStruct((1, d), F32)],
        in_specs=[row, row, full(gain), pl.BlockSpec((tm, ng), lambda i: (i, 0))]
        + [pl.BlockSpec((tm, wd), lambda i: (i, 0)) for wd in widths] + [full(win), full(wgate)],
        out_specs=[row, pl.BlockSpec((1, d), lambda i: (0, 0))],
        compiler_params=_params(("arbitrary",)),
    )(dh, h, gain, dzg, *dqkv_parts, win, wgate)


def _dw_rows(parts, dy, name):
    t, n = dy.shape
    widths = [a.shape[1] for a in parts]
    k = sum(widths)
    tt = _row_tile(t, 1024)
    steps = t // tt
    np_ = len(parts)

    def body(*refs):
        part_refs, dy_ref = refs[:np_], refs[np_]
        o_hbm, ob_hbm, acc, stage, sems = refs[np_ + 1:]

        @pl.when(pl.program_id(0) == 0)
        def _():
            acc[...] = jnp.zeros_like(acc)

        dyb = dy_ref[...].astype(BF16)
        off = 0
        for ref, wd in zip(part_refs, widths):
            acc[off:off + wd, :] += _dot(ref[...].astype(BF16).T, dyb)
            off += wd

        @pl.when(pl.program_id(0) == steps - 1)
        def _():
            stage[...] = acc[...].astype(BF16)
            cps = [pltpu.make_async_copy(acc, o_hbm.at[0], sems.at[0]),
                   pltpu.make_async_copy(stage, ob_hbm.at[0], sems.at[1])]
            for cp in cps:
                cp.start()
            for cp in cps:
                cp.wait()

    return pl.pallas_call(
        body, name=name, grid=(steps,),
        out_shape=[jax.ShapeDtypeStruct((1, k, n), F32), jax.ShapeDtypeStruct((1, k, n), BF16)],
        in_specs=[pl.BlockSpec((tt, wd), lambda i: (i, 0)) for wd in widths] + [pl.BlockSpec((tt, n), lambda i: (i, 0))],
        out_specs=[ANY, ANY],
        scratch_shapes=[pltpu.VMEM((k, n), F32), pltpu.VMEM((k, n), BF16), pltpu.SemaphoreType.DMA((2,))],
        compiler_params=_params(("arbitrary",)),
    )(*parts, dy)


def _merge_fwd(h, ya, yb, gate, wpa, wpb, wout, name):
    t, d = h.shape
    tm = _row_tile(t, 512)

    def body(h_ref, ya_ref, yb_ref, ga_ref, gb_ref, wpa_ref, wpb_ref, wout_ref, out_ref, mg_ref, pa_ref, pb_ref):
        pa = _dot(ya_ref[...].astype(BF16), wpa_ref[...])
        pb = _dot(yb_ref[...].astype(BF16), wpb_ref[...])
        merged = (ga_ref[...] * pa + gb_ref[...] * pb).astype(BF16)
        pa_ref[...] = pa.astype(BF16)
        pb_ref[...] = pb.astype(BF16)
        mg_ref[...] = merged
        out_ref[...] = h_ref[...] + _dot(merged, wout_ref[...])

    full = lambda a: pl.BlockSpec(a.shape, lambda i: (0,) * a.ndim)
    row = pl.BlockSpec((tm, d), lambda i: (i, 0))
    yrow = pl.BlockSpec((tm, ya.shape[1]), lambda i: (i, 0))
    return pl.pallas_call(
        body, name=name, grid=(t // tm,),
        out_shape=[jax.ShapeDtypeStruct((t, d), F32)] + [jax.ShapeDtypeStruct((t, d), BF16)] * 3,
        in_specs=[row, yrow, yrow, pl.BlockSpec((tm, d), lambda i: (i, 0)), pl.BlockSpec((tm, d), lambda i: (i, 1)),
                  full(wpa), full(wpb), full(wout)],
        out_specs=[row] * 4,
        compiler_params=_params(("arbitrary",)),
    )(h, ya, yb, gate, gate, wpa, wpb, wout)


def _merge_bwd(dh, pa, pb, gate, wpa, wpb, wout, name):
    t, d = dh.shape
    tm = _row_tile(t, 512)
    wy = wpa.shape[0]

    def body(dh_ref, pa_ref, pb_ref, ga_ref, gb_ref, wpa_ref, wpb_ref, wout_ref,
             dpa_ref, dpb_ref, dzg_ref, dya_ref, dyb_ref):
        dm = _dot_nt(dh_ref[...].astype(BF16), wout_ref[...])
        ga, gb = ga_ref[...], gb_ref[...]
        dpa = (dm * ga).astype(BF16)
        dpb = (dm * gb).astype(BF16)
        dpa_ref[...] = dpa
        dpb_ref[...] = dpb
        dzg_ref[:, :d] = (dm * pa_ref[...].astype(F32) * ga * (1.0 - ga)).astype(BF16)
        dzg_ref[:, d:] = (dm * pb_ref[...].astype(F32) * gb * (1.0 - gb)).astype(BF16)
        dya_ref[...] = _dot_nt(dpa, wpa_ref[...])
        dyb_ref[...] = _dot_nt(dpb, wpb_ref[...])

    full = lambda a: pl.BlockSpec(a.shape, lambda i: (0,) * a.ndim)
    row = pl.BlockSpec((tm, d), lambda i: (i, 0))
    yrow = pl.BlockSpec((tm, wy), lambda i: (i, 0))
    return pl.pallas_call(
        body, name=name, grid=(t // tm,),
        out_shape=[jax.ShapeDtypeStruct((t, d), BF16), jax.ShapeDtypeStruct((t, d), BF16),
                   jax.ShapeDtypeStruct((t, 2 * d), BF16), jax.ShapeDtypeStruct((t, wy), F32),
                   jax.ShapeDtypeStruct((t, wy), F32)],
        in_specs=[row, row, row, pl.BlockSpec((tm, d), lambda i: (i, 0)), pl.BlockSpec((tm, d), lambda i: (i, 1)),
                  full(wpa), full(wpb), full(wout)],
        out_specs=[row, row, pl.BlockSpec((tm, 2 * d), lambda i: (i, 0)), yrow, yrow],
        compiler_params=_params(("arbitrary",)),
    )(dh, pa, pb, gate, gate, wpa, wpb, wout)


def _ple_loss(h, gain, p, target, wpg, wpe, name):
    t, d = h.shape
    tm = _row_tile(t, 512)
    pd = p.shape[1]

    def body(h_ref, g_ref, p_ref, t_ref, wpg_ref, wpe_ref, dh_ref, dz_ref, dpp_ref, n_ref, dgain_ref, loss_ref):
        @pl.when(pl.program_id(0) == 0)
        def _():
            dgain_ref[...] = jnp.zeros_like(dgain_ref)
            loss_ref[...] = jnp.zeros_like(loss_ref)

        x = h_ref[...]
        gain_v = g_ref[...]
        n, xh, r = _rms(x, gain_v)
        nbf = n.astype(BF16)
        n_ref[...] = nbf
        pg = jax.nn.sigmoid(_dot(nbf, wpg_ref[...]))
        pp = _dot(p_ref[...].astype(BF16), wpe_ref[...])
        err = (x + pg * pp) - t_ref[...]
        loss_ref[...] += 0.5 * jnp.sum(jnp.mean(err * err, axis=-1, keepdims=True))
        dy = err * (1.0 / d)
        dpp_ref[...] = (dy * pg).astype(BF16)
        dz = (dy * pp * pg * (1.0 - pg)).astype(BF16)
        dz_ref[...] = dz
        dn = _dot_nt(dz, wpg_ref[...])
        dx, dgain = _rms_bwd(xh, r, gain_v, dn)
        dh_ref[...] = dy + dx
        dgain_ref[...] += dgain

    full = lambda a: pl.BlockSpec(a.shape, lambda i: (0,) * a.ndim)
    row = pl.BlockSpec((tm, d), lambda i: (i, 0))
    return pl.pallas_call(
        body, name=name, grid=(t // tm,),
        out_shape=[jax.ShapeDtypeStruct((t, d), F32), jax.ShapeDtypeStruct((t, d), BF16),
                   jax.ShapeDtypeStruct((t, d), BF16), jax.ShapeDtypeStruct((t, d), BF16),
                   jax.ShapeDtypeStruct((1, d), F32), jax.ShapeDtypeStruct((8, LANES), F32)],
        in_specs=[row, full(gain), pl.BlockSpec((tm, pd), lambda i: (i, 0)), row, full(wpg), full(wpe)],
        out_specs=[row, row, row, row, pl.BlockSpec((1, d), lambda i: (0, 0)),
                   pl.BlockSpec((8, LANES), lambda i: (0, 0))],
        compiler_params=_params(("arbitrary",)),
    )(h, gain, p, target, wpg, wpe)


def _head_masks():
    lane = lax.broadcasted_iota(jnp.int32, (1, LANES), 1)
    m0 = (lane < HEAD_DIM).astype(F32)
    return m0, 1.0 - m0


def _head_mean(v, m0, m1):
    del m0, m1
    width = v.shape[-1]
    shift = HEAD_DIM.bit_length() - 1
    r = jnp.right_shift(lax.broadcasted_iota(jnp.int32, (width, width), 0), shift)
    c = jnp.right_shift(lax.broadcasted_iota(jnp.int32, (width, width), 1), shift)
    same_head = (r == c).astype(BF16)
    return _dot(v.astype(BF16), same_head) * (1.0 / HEAD_DIM)


def _head_norm(x, gain, m0, m1):
    r = lax.rsqrt(_head_mean(x * x, m0, m1) + EPS)
    xh = x * r
    return xh * gain, xh, r


def _head_norm_bwd(xh, r, gain, dy, m0, m1):
    gdy = gain * dy
    dx = r * (gdy - xh * _head_mean(xh * gdy, m0, m1))
    return dx, jnp.sum(dy * xh, axis=0, keepdims=True)


GROUP = 4
QW = GROUP * HEAD_DIM
STACK = GROUP * QTILE


def _kv_width(mode):
    return QW if mode == "A" else LANES


def _q_scratch_shape(mode, s_len):
    return (s_len, QW) if mode == "A" else (GROUP * s_len, LANES)


def _group_masks(dtype=F32):
    lane = lax.broadcasted_iota(jnp.int32, (1, QW), 1)
    return [((lane >= h * HEAD_DIM) & (lane < (h + 1) * HEAD_DIM)).astype(dtype) for h in range(GROUP)]


def _stack_heads(first_kv, x, m0, m1):
    out = []
    for half in range(GROUP // 2):
        xh = x[:, half * LANES:(half + 1) * LANES]
        a0, a1 = xh * m0, xh * m1
        r0, r1 = pltpu.roll(a0, HEAD_DIM, 1), pltpu.roll(a1, HEAD_DIM, 1)
        out += [jnp.where(first_kv, a0, r0), jnp.where(first_kv, r1, a1)]
    return out


def _unstack_heads(mode, first_kv, ts, m0, m1):
    if mode == "A":
        masks = _group_masks()
        return sum(t * mk for t, mk in zip(ts, masks))
    halves = []
    for half in range(GROUP // 2):
        t0 = jnp.where(first_kv, ts[2 * half], pltpu.roll(ts[2 * half], HEAD_DIM, 1))
        t1 = jnp.where(first_kv, pltpu.roll(ts[2 * half + 1], HEAD_DIM, 1), ts[2 * half + 1])
        halves.append(t0 * m0 + t1 * m1)
    return jnp.concatenate(halves, axis=1)


def _store_stacked(dst, i, heads):
    for half in range(2):
        rows = slice(half * QTILE, (half + 1) * QTILE)
        for h, x in enumerate(heads):
            dst[pl.ds((2 * i + half) * STACK + h * QTILE, QTILE), :] = x[rows].astype(dst.dtype)


def _load_stacked(mode, ref, m):
    if mode == "B":
        return ref[pl.ds(pl.multiple_of(m * STACK, STACK), STACK), :]
    x = ref[pl.ds(pl.multiple_of(m * QTILE, QTILE), QTILE), :]
    return jnp.concatenate([x * mk for mk in _group_masks(x.dtype)], axis=0)


def _attn_prep(mode, group, s_len, padk, q_ref, k_ref, v_ref, gq_ref, gk_ref, qs, k2, v2, do_ref=None, dos=None):
    m0, m1 = _head_masks()
    zpad = jnp.zeros((padk, k2.shape[1]), BF16)
    k2[pl.ds(0, padk), :] = zpad
    v2[pl.ds(0, padk), :] = zpad
    first_kv = group == 0
    rt = 2 * QTILE
    for i in range(s_len // rt):
        rows = pl.ds(i * rt, rt)
        qn, _, _ = _head_norm(q_ref[rows, :], gq_ref[...], m0, m1)
        kn, _, _ = _head_norm(k_ref[rows, :], gk_ref[...], m0, m1)
        qn = qn * (HEAD_DIM ** -0.5)
        if mode == "A":
            qs[rows, :] = qn.astype(BF16)
            if dos is not None:
                dos[rows, :] = do_ref[rows, :].astype(BF16)
        else:
            _store_stacked(qs, i, _stack_heads(first_kv, qn, m0, m1))
            if dos is not None:
                _store_stacked(dos, i, _stack_heads(first_kv, do_ref[rows, :], m0, m1))
        k2[pl.ds(padk + i * rt, rt), :] = kn.astype(BF16)
        v2[pl.ds(padk + i * rt, rt), :] = v_ref[rows, :].astype(BF16)


def _softmax_terms(mode, s, sink):
    mx = jnp.max(s, axis=-1, keepdims=True)
    if mode == "B":
        mx = jnp.maximum(mx, sink)
    e = jnp.exp(s - mx)
    l = jnp.sum(e, axis=-1, keepdims=True)
    if mode == "B":
        l = l + jnp.exp(sink - mx)
    return e, mx, l


def _sink_column(sink_ref, group):
    row = lax.broadcasted_iota(jnp.int32, (STACK, 1), 0)
    col = jnp.zeros((STACK, 1), F32)
    for h in range(GROUP):
        col = jnp.where((row >= h * QTILE) & (row < (h + 1) * QTILE), sink_ref[GROUP * group + h], col)
    return col


def _head_deltas(dd, m0, m1):
    cols = []
    for half in range(GROUP // 2):
        dh = dd[:, half * LANES:(half + 1) * LANES]
        cols += [jnp.sum(dh * m0, axis=-1, keepdims=True), jnp.sum(dh * m1, axis=-1, keepdims=True)]
    return jnp.concatenate(cols, axis=0)


def _attn_cols(mode):
    if mode == "A":
        return (lambda b, g: (b, g)), (lambda b, g: (b, 2 + g)), (lambda b, g: (b, 4 + g))
    return (lambda b, g: (b, 6 + g)), (lambda b, g: (b, 16)), (lambda b, g: (b, 17))


def _attn_fwd(mode, qkv, gq, gk, bias, sinks, bl, s_len, name):
    bw = bias.shape[-1]
    padk = bw - QTILE
    nt = s_len // QTILE
    qmap, kmap, vmap = _attn_cols(mode)

    kw = _kv_width(mode)

    def body(q_ref, k_ref, v_ref, gq_ref, gk_ref, bias_ref, sink_ref, o_ref, qs, k2, v2, s_buf, *rest):
        o_buf = rest[0] if rest else None
        group = pl.program_id(1)
        m0, m1 = _head_masks()
        first_kv = group == 0
        _attn_prep(mode, group, s_len, padk, q_ref, k_ref, v_ref, gq_ref, gk_ref, qs, k2, v2)
        col = lax.broadcasted_iota(jnp.int32, (STACK, bw), 1)
        sink = _sink_column(sink_ref, group)

        def scores(m, slot):
            r0 = pl.multiple_of(m * QTILE, QTILE)
            s = _dot_nt(_load_stacked(mode, qs, m), k2[pl.ds(r0, bw), :]) + bias_ref[...]
            s_buf[slot] = jnp.where(col >= (padk - r0), s, NEG_INF)

        def finish_tile(m, slot):
            r0 = pl.multiple_of(m * QTILE, QTILE)
            e, _, l = _softmax_terms(mode, s_buf[slot], sink)
            if mode == "A":
                o_st = _dot(e.astype(BF16), v2[pl.ds(r0, bw), :]) / l
                heads = [o_st[h * QTILE:(h + 1) * QTILE] for h in range(GROUP)]
                o_ref[pl.ds(r0, QTILE), :] = _unstack_heads(mode, first_kv, heads, m0, m1)
            else:
                o_buf[pl.ds(pl.multiple_of(m * STACK, STACK), STACK), :] = _dot((e * (1.0 / l)).astype(BF16),
                                                                                 v2[pl.ds(r0, bw), :])

        scores(0, 0)

        def pair(j, carry):
            scores(2 * j + 1, 1)
            finish_tile(2 * j, 0)
            scores(jnp.minimum(2 * j + 2, nt - 1), 0)
            finish_tile(2 * j + 1, 1)
            return carry

        lax.fori_loop(0, nt // 2, pair, 0)
        if mode == "B":
            for m in range(nt):
                heads = [o_buf[pl.ds(m * STACK + h * QTILE, QTILE), :] for h in range(GROUP)]
                o_ref[pl.ds(m * QTILE, QTILE), :] = _unstack_heads(mode, first_kv, heads, m0, m1)

    blk = lambda w, f: pl.BlockSpec((s_len, w), f)
    return pl.pallas_call(
        body, name=name, grid=(bl, B_Q_HEADS // GROUP),
        out_shape=jax.ShapeDtypeStruct((bl * s_len, B_Q_HEADS * HEAD_DIM), F32),
        in_specs=[blk(QW, qmap), blk(kw, kmap), blk(kw, vmap),
                  pl.BlockSpec((1, QW), lambda b, g: (0, 0)), pl.BlockSpec((1, kw), lambda b, g: (0, 0)),
                  pl.BlockSpec((STACK, bw), lambda b, g: (g, 0)),
                  pl.BlockSpec(memory_space=pltpu.SMEM)],
        out_specs=blk(QW, lambda b, g: (b, g)),
        scratch_shapes=[pltpu.VMEM(_q_scratch_shape(mode, s_len), BF16)] + [pltpu.VMEM((s_len + padk, kw), BF16)] * 2
        + [pltpu.VMEM((2, STACK, bw), F32)] + ([pltpu.VMEM((GROUP * s_len, LANES), F32)] if mode == "B" else []),
        compiler_params=_params(("arbitrary", "arbitrary")),
    )(qkv, qkv, qkv, gq, gk, bias.reshape(-1, bw), sinks)


def _attn_bwd(mode, qkv, gq, gk, bias, sinks, y, dy, bl, s_len, name):
    bw = bias.shape[-1]
    padk = bw - QTILE
    nt = s_len // QTILE
    qmap, kmap, vmap = _attn_cols(mode)
    t = bl * s_len
    kw = _kv_width(mode)
    kvw = 4 * LANES if mode == "A" else LANES
    dp_ahead = True

    def body(q_ref, k_ref, v_ref, gq_ref, gk_ref, bias_ref, sink_ref, y_ref, dy_ref,
             dq_ref, dk_ref, dv_ref, dgq_ref, dgk_ref, dbias_ref, dsink_ref,
             qs, k2, v2, dos, dqs, dk, dv, s_buf, dp_buf):
        group = pl.program_id(1)
        m0, m1 = _head_masks()
        first_kv = group == 0
        _attn_prep(mode, group, s_len, padk, q_ref, k_ref, v_ref, gq_ref, gk_ref, qs, k2, v2, dy_ref, dos)
        dk[...] = jnp.zeros_like(dk)
        dv[...] = jnp.zeros_like(dv)
        dbias_ref[...] = jnp.zeros_like(dbias_ref)
        col = lax.broadcasted_iota(jnp.int32, (STACK, bw), 1)
        lane8 = lax.broadcasted_iota(jnp.int32, (8, LANES), 1)
        sink = _sink_column(sink_ref, group)

        def ahead(m, slot):
            r0 = pl.multiple_of(m * QTILE, QTILE)
            band = pl.ds(r0, bw)
            s = _dot_nt(_load_stacked(mode, qs, m), k2[band, :]) + bias_ref[...]
            s_buf[slot] = jnp.where(col >= (padk - r0), s, NEG_INF)
            if dp_ahead:
                dp_buf[slot] = _dot_nt(_load_stacked(mode, dos, m), v2[band, :])

        def tile(m, slot, dsink):
            r0 = pl.multiple_of(m * QTILE, QTILE)
            rows = pl.ds(r0, QTILE)
            band = pl.ds(r0, bw)
            q_st = _load_stacked(mode, qs, m)
            do_st = _load_stacked(mode, dos, m)
            delta = _head_deltas(dy_ref[rows, :] * y_ref[rows, :], m0, m1)
            kb = k2[band, :]
            e, mx, l = _softmax_terms(mode, s_buf[slot], sink)
            inv = 1.0 / l
            pn = e * inv
            ds = pn * ((dp_buf[slot] if dp_ahead else _dot_nt(do_st, v2[band, :])) - delta)
            if mode == "A":
                dbias_ref[...] += ds
            else:
                part = jnp.exp(sink - mx) * inv * delta
                for h in range(GROUP):
                    dsink = dsink - jnp.where(lane8 == h, jnp.sum(part[h * QTILE:(h + 1) * QTILE]), 0.0)
            dsb = ds.astype(BF16)
            dv[band, :] += _dot_tn(pn.astype(BF16), do_st)
            dk[band, :] += _dot_tn(dsb, q_st)
            dq_st = _dot(dsb, kb)
            if mode == "A":
                heads = [dq_st[h * QTILE:(h + 1) * QTILE] for h in range(GROUP)]
                dq_ref[rows, :] = _unstack_heads(mode, first_kv, heads, m0, m1)
            else:
                dqs[pl.ds(pl.multiple_of(m * STACK, STACK), STACK), :] = dq_st
            return dsink

        ahead(0, 0)

        def pair(j, dsink):
            ahead(2 * j + 1, 1)
            dsink = tile(2 * j, 0, dsink)
            ahead(jnp.minimum(2 * j + 2, nt - 1), 0)
            return tile(2 * j + 1, 1, dsink)

        dsink = lax.fori_loop(0, nt // 2, pair, jnp.zeros((8, LANES), F32))
        dsink_ref[...] = dsink

        rt = 2 * QTILE
        dgq = jnp.zeros((1, QW), F32)
        dgk = jnp.zeros((1, kw), F32)
        for i in range(s_len // rt):
            rows = pl.ds(i * rt, rt)
            src = pl.ds(padk + i * rt, rt)
            gq_v, gk_v = gq_ref[...], gk_ref[...]
            _, qh, qr = _head_norm(q_ref[rows, :], gq_v, m0, m1)
            _, kh, kr = _head_norm(k_ref[rows, :], gk_v, m0, m1)
            if mode == "A":
                dqn = dq_ref[rows, :] * (HEAD_DIM ** -0.5)
            else:
                dqn = jnp.concatenate(
                    [_unstack_heads(mode, first_kv, [dqs[pl.ds((2 * i + half) * STACK + h * QTILE, QTILE), :]
                                                     for h in range(GROUP)], m0, m1)
                     for half in range(2)], axis=0) * (HEAD_DIM ** -0.5)
            dq_raw, dgq_i = _head_norm_bwd(qh, qr, gq_v, dqn, m0, m1)
            dk_raw, dgk_i = _head_norm_bwd(kh, kr, gk_v, dk[src, :], m0, m1)
            dvn = dv[src, :]
            dq_ref[rows, :] = dq_raw
            if mode == "A":
                dk_ref[rows, :] = dk_raw
                dv_ref[rows, :] = dvn
            else:
                @pl.when(group == 0)
                def _():
                    dk_ref[rows, :] = dk_raw
                    dv_ref[rows, :] = dvn

                @pl.when(group != 0)
                def _():
                    dk_ref[rows, :] += dk_raw
                    dv_ref[rows, :] += dvn
            dgq, dgk = dgq + dgq_i, dgk + dgk_i
        dgq_ref[...] = jnp.broadcast_to(dgq, (8, QW))
        dgk_ref[...] = jnp.broadcast_to(dgk, (8, kw))

    ng = B_Q_HEADS // GROUP
    blk = lambda w, f: pl.BlockSpec((s_len, w), f)
    small = lambda w: pl.BlockSpec((None, None, 8, w), lambda b, g: (b, g, 0, 0))
    own = lambda b, g: (b, g)
    kvmap = own if mode == "A" else (lambda b, g: (b, 0))
    pad_f32 = pltpu.VMEM((s_len + padk, kw), F32)
    pad_bf = pltpu.VMEM((s_len + padk, kw), BF16)
    stack_bf = pltpu.VMEM(_q_scratch_shape(mode, s_len), BF16)
    outs = pl.pallas_call(
        body, name=name, grid=(bl, ng),
        out_shape=[jax.ShapeDtypeStruct((t, ng * QW), F32), jax.ShapeDtypeStruct((t, kvw), F32),
                   jax.ShapeDtypeStruct((t, kvw), F32),
                   jax.ShapeDtypeStruct((bl, ng, 8, QW), F32), jax.ShapeDtypeStruct((bl, ng, 8, kw), F32),
                   jax.ShapeDtypeStruct((bl, ng * STACK, bw), F32), jax.ShapeDtypeStruct((bl, ng, 8, LANES), F32)],
        in_specs=[blk(QW, qmap), blk(kw, kmap), blk(kw, vmap),
                  pl.BlockSpec((1, QW), lambda b, g: (0, 0)), pl.BlockSpec((1, kw), lambda b, g: (0, 0)),
                  pl.BlockSpec((STACK, bw), lambda b, g: (g, 0)),
                  pl.BlockSpec(memory_space=pltpu.SMEM),
                  blk(QW, own), blk(QW, own)],
        out_specs=[blk(QW, own), blk(kw, kvmap), blk(kw, kvmap), small(QW), small(kw),
                   pl.BlockSpec((None, STACK, bw), lambda b, g: (b, g, 0)), small(LANES)],
        scratch_shapes=[stack_bf, pad_bf, pad_bf, stack_bf,
                        pltpu.VMEM((8, LANES) if mode == "A" else _q_scratch_shape(mode, s_len), F32),
                        pad_f32, pad_f32, pltpu.VMEM((2, STACK, bw), F32),
                        pltpu.VMEM((2, STACK, bw) if dp_ahead else (8, LANES), F32)],
        compiler_params=_params(("arbitrary", "arbitrary")),
    )(qkv, qkv, qkv, gq, gk, bias.reshape(-1, bw), sinks, y, dy)
    outs = list(outs)
    outs[5] = outs[5].reshape(bl, B_Q_HEADS, QTILE, bw)
    return outs


def _band_geometry(prev):
    bw = QTILE + prev * CHUNK
    i = np.arange(QTILE)[:, None]
    j = np.arange(bw)[None, :]
    dist = i + prev * CHUNK - j
    valid = (j // CHUNK >= i // CHUNK) & (j // CHUNK <= i // CHUNK + prev)
    return dist, valid


A_VAR0 = (A_PREV * CHUNK - A_MAX_REL) // LANES * LANES


A_NVAR = QTILE + A_PREV * CHUNK - A_VAR0


def _skew_rows(x, sign):
    rows, n = x.shape
    row = lax.broadcasted_iota(jnp.int32, x.shape, 0)
    b = 1
    while b < rows:
        x = jnp.where((row & b) != 0, pltpu.roll(x, (sign * b) % n, 1), x)
        b *= 2
    return x


def _rel_bias_expand(table, name):
    _, valid = _band_geometry(A_PREV)
    bw = valid.shape[1]
    valid_f = jnp.asarray(valid.astype(np.float32))
    rev = jnp.flip(table[:, 1:], axis=1).reshape(A_HEADS, 1, A_NVAR)

    def body(rev_ref, valid_ref, o_ref):
        rowv = jnp.broadcast_to(rev_ref[...], (QTILE, A_NVAR))
        top = rowv[:, 0:1]
        var = _skew_rows(rowv, 1)
        row = lax.broadcasted_iota(jnp.int32, (QTILE, A_NVAR), 0)
        colv = lax.broadcasted_iota(jnp.int32, (QTILE, A_NVAR), 1)
        var = jnp.where(colv < row, top, var)
        ok = valid_ref[...] > 0.5
        o_ref[:, :A_VAR0] = jnp.where(ok[:, :A_VAR0], top, NEG_INF)
        o_ref[:, A_VAR0:] = jnp.where(ok[:, A_VAR0:], var, NEG_INF)

    return pl.pallas_call(
        body, name=name, grid=(A_HEADS,),
        out_shape=jax.ShapeDtypeStruct((A_HEADS, QTILE, bw), F32),
        in_specs=[pl.BlockSpec((None, 1, A_NVAR), lambda h: (h, 0, 0)), pl.BlockSpec((QTILE, bw), lambda h: (0, 0))],
        out_specs=pl.BlockSpec((None, QTILE, bw), lambda h: (h, 0, 0)),
        compiler_params=_params(("arbitrary",)),
    )(rev, valid_f)


def _rel_bias_grad(dbias, name):
    bl = dbias.shape[0]
    bw = dbias.shape[-1]

    def body(db_ref, o_ref):
        g = db_ref[0]
        for b in range(1, bl):
            g = g + db_ref[b]
        sk = _skew_rows(g[:, A_VAR0:], -1)
        row = lax.broadcasted_iota(jnp.int32, (QTILE, A_NVAR), 0)
        colv = lax.broadcasted_iota(jnp.int32, (QTILE, A_NVAR), 1)
        wrapped = (row + colv) >= A_NVAR
        main = jnp.sum(jnp.where(wrapped, 0.0, sk), axis=0, keepdims=True)
        top = jnp.sum(g[:, :A_VAR0]) + jnp.sum(jnp.where(wrapped, sk, 0.0))
        o_ref[:, :A_NVAR] = jnp.broadcast_to(main, (8, A_NVAR))
        o_ref[:, A_NVAR:] = jnp.full((8, LANES), top, F32)

    out = pl.pallas_call(
        body, name=name, grid=(A_HEADS,),
        out_shape=jax.ShapeDtypeStruct((A_HEADS, 8, A_NVAR + LANES), F32),
        in_specs=[pl.BlockSpec((bl, None, QTILE, bw), lambda h: (0, h, 0, 0))],
        out_specs=pl.BlockSpec((None, 8, A_NVAR + LANES), lambda h: (h, 0, 0)),
        compiler_params=_params(("arbitrary",)),
    )(dbias)
    main, top = out[:, 0, :A_NVAR], out[:, 0, A_NVAR]
    fm = jnp.flip(main, axis=1)
    return jnp.concatenate([jnp.zeros((A_HEADS, 1), F32), fm[:, :-1], fm[:, -1:] + top[:, None]], axis=1)


def _alibi_bias():
    dist, valid = _band_geometry(B_PREV)
    slopes = np.array([2.0 ** (-8.0 * (h + 1) / B_Q_HEADS) for h in range(B_Q_HEADS)], dtype=np.float32)
    bias = -slopes[:, None, None] * np.abs(dist).astype(np.float32)[None]
    return jnp.asarray(np.where(valid[None], bias, np.float32(NEG_INF)).astype(np.float32))


SMALL_NAMES = ("ffn1_norm", "mix_norm", "ffn2_norm", "ple_norm", "a_q_norm", "a_k_norm", "b_q_norm", "b_k_norm",
               "a_rel_bias", "b_sinks", "loss")


def _pack_small(vals):
    rows = []
    for nme in SMALL_NAMES:
        v = vals[nme].astype(F32)
        if nme == "a_rel_bias":
            v = jnp.pad(v.reshape(A_HEADS, -1), ((0, 0), (0, 3 * LANES - (2 * A_MAX_REL + 1))))
        v = v.reshape(-1)
        v = jnp.pad(v, (0, (-v.shape[0]) % LANES))
        rows.append(v.reshape(-1, LANES))
    out = jnp.concatenate(rows, axis=0)
    return jnp.pad(out, ((0, (-out.shape[0]) % 8), (0, 0)))


def _unpack_small(packed, shapes):
    out, r = {}, 0
    for nme in SMALL_NAMES:
        shp = shapes[nme]
        if nme == "a_rel_bias":
            nr = A_HEADS * 3
            out[nme] = packed[r:r + nr].reshape(A_HEADS, 3 * LANES)[:, :2 * A_MAX_REL + 1].reshape(shp)
        else:
            size = int(np.prod(shp)) if shp else 1
            nr = -(-size // LANES)
            out[nme] = packed[r:r + nr].reshape(-1)[:size].reshape(shp)
        r += nr
    return out


BIG_NAMES = ("ffn1_w_gu", "ffn1_w_down", "w_in", "w_gate", "w_proj_a", "w_proj_b", "w_out",
             "ffn2_w_gu", "ffn2_w_down", "w_ple_gate", "w_ple_proj")
WEIGHT_ORDER = ("ffn1_norm", "ffn1_w_gu", "ffn1_w_down", "mix_norm", "w_in", "a_q_norm", "a_k_norm", "a_rel_bias",
                "b_q_norm", "b_k_norm", "b_sinks", "w_gate", "w_proj_a", "w_proj_b", "w_out", "ffn2_norm",
                "ffn2_w_gu", "ffn2_w_down", "ple_norm", "w_ple_gate", "w_ple_proj")


TRANSPOSED = ("ffn1_w_gu", "ffn2_w_gu", "w_in")


def _local(a, nme):
    return a[0].T if nme in TRANSPOSED else a[0]


def _full_cols(wg):
    nb, k, n = wg.shape
    return jnp.transpose(wg, (1, 0, 2)).reshape(k, nb * n)


def _step(x, p, target, w, m, v):
    bl, s_len, d = x.shape
    t = bl * s_len
    h0 = x.reshape(t, d)
    pt = p.reshape(t, p.shape[-1])
    tgt = target.reshape(t, d)

    g_ffn1, g_mix, g_ffn2, g_ple = w["ffn1_norm"], w["mix_norm"], w["ffn2_norm"], w["ple_norm"]
    tiled = lambda a, width: jnp.tile(a.reshape(1, HEAD_DIM), (1, width // HEAD_DIM))
    gqa, gka = tiled(w["a_q_norm"], QW), tiled(w["a_k_norm"], _kv_width("A"))
    gqb, gkb = tiled(w["b_q_norm"], QW), tiled(w["b_k_norm"], _kv_width("B"))
    sinks = w["b_sinks"].reshape(B_Q_HEADS)
    bias_b = _alibi_bias()

    ffn1_names = ("ffn1_w_gu", "ffn1_w_down")
    shard = {nme: _local(w[nme], nme).astype(BF16) for nme in ffn1_names}
    send1, recv1, bufs, token = _gather_start([shard["ffn1_w_gu"]], h0, "gather_start_ffn1_gu")
    dsend1, drecv1, dbufs, token = _gather_start([shard["ffn1_w_down"]], token, "gather_start_ffn1_down")
    zero = token[0, 0]
    shard.update({nme: (_local(w[nme], nme) + zero).astype(BF16) for nme in BIG_NAMES if nme not in ffn1_names})
    bias_a = _rel_bias_expand(w["a_rel_bias"][0] + zero, "rel_bias_expand")
    send2, recv2, bufs, token = _gather_pass(send1, recv1, bufs, bias_a, "gather_pass_ffn1_gu")
    (wgu1,) = _gather_wait(send2, recv2, bufs, shard["ffn2_w_gu"], "gather_wait_ffn1_gu")
    nf = wgu1.shape[1]
    mixer_names = ("w_in", "w_gate")
    rest_names = ("w_proj_a", "w_proj_b", "w_out", "ffn2_w_gu", "ffn2_w_down", "w_ple_gate", "w_ple_proj")
    send1, recv1, bufs, token = _gather_start([shard[nme] for nme in mixer_names], wgu1, "gather_start_mixer")
    rsend1, rrecv1, rest_bufs, token = _gather_start([shard[nme] for nme in rest_names], token, "gather_start_rest")

    gu1, a1f = _ffn_up(h0, g_ffn1 + token[0, 0], wgu1, "ffn1_up")
    dsend2, drecv2, dbufs, token = _gather_pass(dsend1, drecv1, dbufs, a1f, "gather_pass_ffn1_down")
    (wd1,) = _gather_wait(dsend2, drecv2, dbufs, token, "gather_wait_ffn1_down")
    wd1 = wd1.reshape(N_DEV // 2, nf, d)
    h1 = _ffn_down(h0, a1f, wd1, "ffn1_down")
    send2, recv2, bufs, token = _gather_pass(send1, recv1, bufs, h1, "gather_pass_mixer")
    win, wgate = _gather_wait(send2, recv2, bufs, token, "gather_wait_mixer")
    win, wgate = win.reshape(IN_COLS, d), _full_cols(wgate)
    un, qkv, gate = _proj_fwd(h1, g_mix, win, wgate, "proj_fwd")
    ya = _attn_fwd("A", qkv, gqa, gka, bias_a, sinks, bl, s_len, "attn_a_fwd")
    rsend2, rrecv2, rest_bufs, token = _gather_pass(rsend1, rrecv1, rest_bufs, ya, "gather_pass_rest")
    yb = _attn_fwd("B", qkv, gqb + token[0, 0], gkb, bias_b, sinks, bl, s_len, "attn_b_fwd")
    gathered = dict(zip(rest_names, _gather_wait(rsend2, rrecv2, rest_bufs, yb, "gather_wait_rest")))
    wgu2 = gathered["ffn2_w_gu"]
    wd2 = gathered["ffn2_w_down"].reshape(N_DEV // 2, nf, d)
    wpa = _full_cols(gathered["w_proj_a"])
    wpb = _full_cols(gathered["w_proj_b"])
    wpe = _full_cols(gathered["w_ple_proj"])
    wout = gathered["w_out"].reshape(d, d)
    wpg = gathered["w_ple_gate"].reshape(d, d)
    h2, merged, pa, pb = _merge_fwd(h1, ya, yb, gate, wpa, wpb, wout, "merge_fwd")
    h3, gu2 = _ffn_fwd(h2, g_ffn2, wgu2, wd2, "ffn2_fwd")
    dh3, dz4, dpp, n4, dg_ple, loss_part = _ple_loss(h3, g_ple, pt, tgt, wpg, wpe, "ple_loss")

    xi, yi, ci = _place()
    me = jnp.stack([4 * xi + 2 * yi + ci, 2 * xi + yi]).astype(jnp.int32)
    g32, g16, big, pairs = {}, {}, {}, {}

    def keep(nme, pair, rows=None):
        for store, g in zip((g32, g16), pair):
            store[nme] = g if rows is None else g.reshape(N_DEV, rows, d)

    def start(names, after, tag):
        send, recv, parts, lands, token = _scatter_start([g16[nme] for nme in names], after, "grads_start_" + tag)
        return names, send, recv, parts, lands, token

    def start_two_level(names, after, tag):
        views = [g16[nme].reshape((4, 2) + g16[nme].shape[1:]) for nme in names]
        for nme, got in zip(names, _pair_exchange(views, "grads_pair_" + tag)):
            pairs[nme] = got.reshape((4,) + got.shape[2:])
        sums = [_pair_sum(g32[nme], pairs[nme], me, "pair_sum_" + nme) for nme in names]
        send, recv, parts, lands, token = _scatter_start(sums, after, "grads_start_" + tag, SAME_CORE_CHIPS)
        return names, send, recv, parts, lands, token

    def finish(state, after, tag):
        names, send, recv, parts, lands, _ = state
        relations = SAME_CORE_CHIPS if names[0] in pairs else ALL_PEERS
        lands = _scatter_wait(send, recv, parts, lands, after, "grads_wait_" + tag, relations)
        return names, lands

    def adam(done, dep):
        for nme, land in zip(*done):
            outs = _final_adam(g32[nme], land, _local(w[nme], nme), _local(m[nme], nme), _local(v[nme], nme), me, dep,
                               "adam_" + nme, pairs.get(nme))
            big[nme] = [(o.T if nme in TRANSPOSED else o)[None] for o in outs]

    keep("w_ple_gate", _dw(n4, dz4, 1, d, "dw_ple_gate"), d // N_DEV)
    keep("w_ple_proj", _dw(pt, dpp, N_DEV, d // N_DEV, "dw_ple_proj"))
    early = [(start(("w_ple_gate", "w_ple_proj"), dh3, "ple"), "ple")]

    dh2, dgu2, a2, n3, dg_ffn2 = _ffn_bwd(dh3, h2, g_ffn2 + early[-1][0][-1][0, 0], gu2, wgu2, wd2, "ffn2_bwd")
    keep("ffn2_w_down", _dw(a2, dh3, N_DEV // 2, d, "dw_ffn2_down", 0.5), nf // 2)
    early.append((start(("ffn2_w_down",), dh2, "ffn2_down"), "ffn2_down"))
    keep("ffn2_w_gu", _dw(dgu2, n3, N_DEV, d, "dw_ffn2_gu", dep=early[-1][0][-1]))
    flight = start(("ffn2_w_gu",), dh2, "ffn2")

    dpa, dpb, dzg, dya, dyb = _merge_bwd(dh2, pa, pb, gate, wpa, wpb, wout, "merge_bwd")
    keep("w_out", _dw(merged, dh2, 1, d, "dw_out"), d // N_DEV)
    keep("w_proj_a", _dw(ya, dpa, N_DEV, d // N_DEV, "dw_proj_a"))
    keep("w_proj_b", _dw(yb, dpb, N_DEV, d // N_DEV, "dw_proj_b"))
    keep("w_gate", _dw(un, dzg, N_DEV, 2 * d // N_DEV, "dw_gate"))

    tok = flight[-1][0, 0]
    dqa, dka, dva, dgqa, dgka, dbias, _ = _attn_bwd("A", qkv, gqa + tok, gka, bias_a, sinks, ya, dya, bl, s_len,
                                                     "attn_a_bwd")
    dqb, dkb, dvb, dgqb, dgkb, _, dsink = _attn_bwd("B", qkv, gqb, gkb, bias_b, sinks, yb, dyb, bl, s_len, "attn_b_bwd")
    dqkv = [dqa, dka, dva, dqb, dkb, dvb]
    dtab = _rel_bias_grad(dbias, "rel_bias_grad")

    dh1, dg_mix = _proj_bwd(dh2, h1, g_mix, dzg, dqkv, win, wgate, "proj_bwd")
    keep("w_in", _dw_rows(dqkv, un, "dw_in"), IN_COLS // N_DEV)
    waiting = [finish(state, g32["w_in"], tag) for state, tag in early]
    done = finish(flight, waiting[-1][1][0], "ffn2")
    flight = start(("w_out", "w_proj_a", "w_proj_b", "w_gate", "w_in"), done[1][0], "mixer")
    waiting.append(done)

    dh0, dgu1, _, n1, dg_ffn1 = _ffn_bwd(dh1, h0, g_ffn1 + flight[-1][0, 0], gu1, wgu1, wd1, "ffn1_bwd", emit_a=False)
    keep("ffn1_w_down", _dw(a1f, dh1, N_DEV // 2, d, "dw_ffn1_down", 0.5), nf // 2)
    done = finish(flight, g32["ffn1_w_down"], "mixer")
    flight = start(("ffn1_w_down",), done[1][0], "ffn1_down")
    waiting.append(done)

    keep("ffn1_w_gu", _dw(dgu1, n1, N_DEV, d, "dw_ffn1_gu", dep=flight[-1]))
    done = finish(flight, g32["ffn1_w_gu"], "ffn1_down")
    flight = start_two_level(("ffn1_w_gu",), done[1][0], "ffn1_gu")
    for group in waiting + [done]:
        adam(group, flight[-1])
    behind = 0.0 * big["ffn1_w_down"][0][0, 0, :1]
    smalls = (dg_ffn1, dg_mix, dg_ffn2, dg_ple + behind, dgqa, dgka, dgqb, dgkb, dtab, dsink)
    return dh0, loss_part, big, smalls, flight, finish, adam


def kernel(x, p, ffn1_norm, ffn1_w_gu, ffn1_w_down, mix_norm, w_in, a_q_norm, a_k_norm, a_rel_bias, b_q_norm, b_k_norm, b_sinks, w_gate, w_proj_a, w_proj_b, w_out, ffn2_norm, ffn2_w_gu, ffn2_w_down, ple_norm, w_ple_gate, w_ple_proj, loss_target, m_ffn1_norm, m_ffn1_w_gu, m_ffn1_w_down, m_mix_norm, m_w_in, m_a_q_norm, m_a_k_norm, m_a_rel_bias, m_b_q_norm, m_b_k_norm, m_b_sinks, m_w_gate, m_w_proj_a, m_w_proj_b, m_w_out, m_ffn2_norm, m_ffn2_w_gu, m_ffn2_w_down, m_ple_norm, m_w_ple_gate, m_w_ple_proj, v_ffn1_norm, v_ffn1_w_gu, v_ffn1_w_down, v_mix_norm, v_w_in, v_a_q_norm, v_a_k_norm, v_a_rel_bias, v_b_q_norm, v_b_k_norm, v_b_sinks, v_w_gate, v_w_proj_a, v_w_proj_b, v_w_out, v_ffn2_norm, v_ffn2_w_gu, v_ffn2_w_down, v_ple_norm, v_w_ple_gate, v_w_ple_proj):
    w = dict(ffn1_norm=ffn1_norm, ffn1_w_gu=ffn1_w_gu, ffn1_w_down=ffn1_w_down, mix_norm=mix_norm, w_in=w_in,
             a_q_norm=a_q_norm, a_k_norm=a_k_norm, a_rel_bias=a_rel_bias, b_q_norm=b_q_norm, b_k_norm=b_k_norm,
             b_sinks=b_sinks, w_gate=w_gate, w_proj_a=w_proj_a, w_proj_b=w_proj_b, w_out=w_out, ffn2_norm=ffn2_norm,
             ffn2_w_gu=ffn2_w_gu, ffn2_w_down=ffn2_w_down, ple_norm=ple_norm, w_ple_gate=w_ple_gate,
             w_ple_proj=w_ple_proj)
    m = dict(ffn1_norm=m_ffn1_norm, ffn1_w_gu=m_ffn1_w_gu, ffn1_w_down=m_ffn1_w_down, mix_norm=m_mix_norm,
             w_in=m_w_in, a_q_norm=m_a_q_norm, a_k_norm=m_a_k_norm, a_rel_bias=m_a_rel_bias, b_q_norm=m_b_q_norm,
             b_k_norm=m_b_k_norm, b_sinks=m_b_sinks, w_gate=m_w_gate, w_proj_a=m_w_proj_a, w_proj_b=m_w_proj_b,
             w_out=m_w_out, ffn2_norm=m_ffn2_norm, ffn2_w_gu=m_ffn2_w_gu, ffn2_w_down=m_ffn2_w_down,
             ple_norm=m_ple_norm, w_ple_gate=m_w_ple_gate, w_ple_proj=m_w_ple_proj)
    v = dict(ffn1_norm=v_ffn1_norm, ffn1_w_gu=v_ffn1_w_gu, ffn1_w_down=v_ffn1_w_down, mix_norm=v_mix_norm,
             w_in=v_w_in, a_q_norm=v_a_q_norm, a_k_norm=v_a_k_norm, a_rel_bias=v_a_rel_bias, b_q_norm=v_b_q_norm,
             b_k_norm=v_b_k_norm, b_sinks=v_b_sinks, w_gate=v_w_gate, w_proj_a=v_w_proj_a, w_proj_b=v_w_proj_b,
             w_out=v_w_out, ffn2_norm=v_ffn2_norm, ffn2_w_gu=v_ffn2_w_gu, ffn2_w_down=v_ffn2_w_down,
             ple_norm=v_ple_norm, w_ple_gate=v_w_ple_gate, w_ple_proj=v_w_ple_proj)
    bl, s_len, d = x.shape

    dh0, loss_part, big, smalls, flight, finish, adam = _step(x, p[0], loss_target, w, m, v)
    dg_ffn1, dg_mix, dg_ffn2, dg_ple, dgqa, dgka, dgqb, dgkb, dtab, dsink = smalls

    fold = lambda a: a[:, :, 0, :].reshape(-1, HEAD_DIM).sum(axis=0)
    small_part = dict(
        ffn1_norm=dg_ffn1, mix_norm=dg_mix, ffn2_norm=dg_ffn2, ple_norm=dg_ple,
        a_q_norm=fold(dgqa), a_k_norm=fold(dgka), b_q_norm=fold(dgqb), b_k_norm=fold(dgkb),
        a_rel_bias=dtab,
        b_sinks=dsink.sum(axis=0)[:, 0, :GROUP].reshape(B_Q_HEADS),
        loss=loss_part[0, :1])
    zero1 = jnp.zeros((1,), F32)
    shapes = {nme: w[nme].shape for nme in SMALL_NAMES if nme != "loss"}
    shapes["loss"] = ()
    pk = lambda src: _pack_small({**{nme: src[nme] for nme in SMALL_NAMES if nme != "loss"}, "loss": zero1})
    sg, sd, sm, sv = _small_allreduce_adam(_pack_small(small_part), pk(w), pk(m), pk(v), "small_allreduce_adam")
    adam(finish(flight, sg, "ffn1_gu"), sg)
    sg, sd, sm, sv = (_unpack_small(a, shapes) for a in (sg, sd, sm, sv))

    def pick(i):
        out = []
        for nme in WEIGHT_ORDER:
            out.append(big[nme][i] if nme in big else (sg, sd, sm, sv)[i][nme])
        return out

    return (sg["loss"], dh0.reshape(bl, s_len, d), *pick(0), *pick(1), *pick(2), *pick(3))
```

```python
import jax
import jax.numpy as jnp
import numpy as np
from jax import lax
from jax.experimental import pallas as pl
from jax.experimental.pallas import tpu as pltpu

F32 = jnp.float32
BF16 = jnp.bfloat16

CHUNK = 64
HEAD_DIM = 64
A_HEADS = 8
A_PREV = 8
A_MAX_REL = 128
B_Q_HEADS = 8
B_KV_HEADS = 2
B_PREV = 2
A_WIDTH = A_HEADS * HEAD_DIM
B_Q_WIDTH = B_Q_HEADS * HEAD_DIM
B_KV_WIDTH = B_KV_HEADS * HEAD_DIM
IN_COLS = 3 * A_WIDTH + B_Q_WIDTH + 2 * B_KV_WIDTH
EPS = 1e-6
NEG_INF = -1e30
ADAM_LR = 0.001
ADAM_B1 = 0.9
ADAM_B2 = 0.999
ADAM_EPS = 1e-08
ADAM_WD = 0.01
ADAM_STEP = 10

N_DEV = 8
LANES = 128
QTILE = 2 * CHUNK
VMEM_LIMIT = 56 * 1024 * 1024
DW_VMEM_LIMIT = 60 * 1024 * 1024
ADAM_TILE_ELEMS = 256 * 1024

MESH_ID = pl.DeviceIdType.MESH
ANY = pl.BlockSpec(memory_space=pl.ANY)
HBM = pl.BlockSpec(memory_space=pltpu.HBM)
SEM = pl.BlockSpec(memory_space=pltpu.SEMAPHORE)
SIDE_EFFECT = pltpu.SideEffectType.DATAFLOW_SIDE_EFFECTING


def _dot(a, b):
    return jnp.dot(a, b, preferred_element_type=F32)


def _dot_nt(a, b):
    return lax.dot_general(a, b, (((1,), (1,)), ((), ())), preferred_element_type=F32)


def _dot_tn(a, b):
    return lax.dot_general(a, b, (((0,), (0,)), ((), ())), preferred_element_type=F32)


def _params(sem=None, vmem=VMEM_LIMIT):
    return pltpu.CompilerParams(dimension_semantics=sem, vmem_limit_bytes=vmem)


def _row_tile(t, want):
    while t % want:
        want //= 2
    return want


def _place():
    return lax.axis_index("x"), lax.axis_index("y"), lax.axis_index("c")


def _gather_level(bufs, send_sems, recv_sems, level, shards=None):
    x, y, c = _place()
    me, sib = (x, y, c), (x, y, 1 - c)
    chips = [(1 - x, y), (x, 1 - y), (1 - x, 1 - y)]

    def copy(w, k, block, to):
        px, py, pc = block
        rows = bufs[w].at[4 * px + 2 * py + pc]
        src = shards[w] if shards is not None and block is me else rows
        return pltpu.make_async_remote_copy(src_ref=src, dst_ref=rows, send_sem=send_sems.at[k], recv_sem=recv_sems.at[k],
                                            device_id=to, device_id_type=MESH_ID)

    n = len(bufs)
    own = []
    if level == 1:
        own = [pltpu.make_async_copy(bufs[w].at[4 * x + 2 * y + c] if shards is None else shards[w],
                                     bufs[w].at[4 * x + 2 * y + c], send_sems.at[4 * n + w]) for w in range(n)]
    out, arriving = [], []
    for w in range(len(bufs)):
        if level == 1:
            out.append(copy(w, 4 * w, me, sib))
            arriving.append(copy(w, 4 * w, sib, me))
        for j, chip in enumerate(chips):
            if level == 1:
                out.append(copy(w, 4 * w + 1 + j, me, (*chip, c)))
                arriving.append(copy(w, 4 * w + 1 + j, (*chip, c), me))
            else:
                out.append(copy(w, 3 * w + j, (*chip, c), sib))
                arriving.append(copy(w, 3 * w + j, (*chip, 1 - c), me))
    return out, arriving, own


def _split_call(body, name, bufs, sems_in, after, n_sems_out, token, extra=()):
    n = len(bufs)
    out_shape = [pltpu.SemaphoreType.DMA((n_sems_out,))] * (2 if n_sems_out else 0)
    out_shape += [pltpu.HBM(a.shape, a.dtype) for a in bufs]
    out_specs = [SEM] * (2 if n_sems_out else 0) + [HBM] * n
    if token:
        out_shape.append(jax.ShapeDtypeStruct((8, LANES), F32))
        out_specs.append(pl.BlockSpec(memory_space=pltpu.VMEM))
    first = 2 if n_sems_out else 0
    return pl.pallas_call(
        body, name=name, out_shape=tuple(out_shape),
        in_specs=[HBM] * (n + len(extra)) + [SEM] * len(sems_in) + [ANY], out_specs=tuple(out_specs),
        input_output_aliases={i: first + i for i in range(n)},
        compiler_params=pltpu.CompilerParams(has_side_effects=SIDE_EFFECT),
    )(*bufs, *extra, *sems_in, after)


def _gather_start(shards, after, name):
    n = len(shards)
    hbm = lambda a: pltpu.with_memory_space_constraint(a, pltpu.HBM)
    bufs = [hbm(lax.empty((N_DEV,) + s.shape, s.dtype)) for s in shards]

    def body(*refs):
        out, _, own = _gather_level(refs[:n], refs[2 * n + 1], refs[2 * n + 2], 1, shards=refs[n:2 * n])
        for cp in own + out:
            cp.start()
        refs[-1][...] = jnp.zeros_like(refs[-1])

    outs = _split_call(body, name, bufs + [hbm(s) for s in shards], [], after, 5 * n, True)
    return outs[0], outs[1], list(outs[2:2 + 2 * n]), outs[-1]


def _gather_pass(send1, recv1, bufs_and_shards, after, name):
    n = len(bufs_and_shards) // 2
    bufs = bufs_and_shards

    def body(*refs):
        refs = refs[:n] + refs[2 * n:]
        out1, in1, own = _gather_level(refs[:n], refs[n], refs[n + 1], 1)
        out2, _, _ = _gather_level(refs[:n], refs[n + 3], refs[n + 4], 2)
        for cp in in1:
            cp.wait_recv()
        for cp in out2:
            cp.start()
        for cp in out1:
            cp.wait_send()
        for cp in own:
            cp.wait()
        refs[-1][...] = jnp.zeros_like(refs[-1])

    outs = _split_call(body, name, bufs, [send1, recv1], after, 3 * n, True)
    return outs[0], outs[1], list(outs[2:2 + n]), outs[-1]


def _gather_wait(send2, recv2, bufs, after, name):
    n = len(bufs)

    def body(*refs):
        out2, in2, _ = _gather_level(refs[:n], refs[n], refs[n + 1], 2)
        for cp in in2:
            cp.wait_recv()
        for cp in out2:
            cp.wait_send()

    return list(_split_call(body, name, bufs, [send2, recv2], after, 0, False))


ALL_PEERS = tuple(range(1, N_DEV))
SAME_CORE_CHIPS = (2, 4, 6)


def _scatter_copies(parts, lands, send_sems, recv_sems, relations):
    x, y, c = _place()
    ns = len(relations)
    cps = []
    for w, (part, land) in enumerate(zip(parts, lands)):
        for i, k in enumerate(relations):
            px, py, pc = x ^ ((k >> 2) & 1), y ^ ((k >> 1) & 1), c ^ (k & 1)
            block = 4 * px + 2 * py + pc if part.shape[0] == N_DEV else 2 * px + py
            cps.append(pltpu.make_async_remote_copy(
                src_ref=part.at[block], dst_ref=land.at[i],
                send_sem=send_sems.at[ns * w + i], recv_sem=recv_sems.at[ns * w + i],
                device_id=(px, py, pc), device_id_type=MESH_ID))
    return cps


def _scatter_start(parts, after, name, relations=ALL_PEERS):
    n = len(parts)
    ns = len(relations)

    def body(*refs):
        ins, lands = refs[:n], refs[n:2 * n]
        send_sems, recv_sems = refs[2 * n + 1], refs[2 * n + 2]
        token = refs[-1]
        for cp in _scatter_copies(ins, lands, send_sems, recv_sems, relations):
            cp.start()
        token[...] = jnp.zeros_like(token)

    land_shapes = [(ns,) + p.shape[1:] for p in parts]
    in_hbm = [pltpu.with_memory_space_constraint(p, pltpu.HBM) for p in parts]
    in_hbm += [pltpu.with_memory_space_constraint(lax.empty(s, p.dtype), pltpu.HBM) for s, p in zip(land_shapes, parts)]
    outs = pl.pallas_call(
        body, name=name,
        out_shape=(pltpu.SemaphoreType.DMA((ns * n,)), pltpu.SemaphoreType.DMA((ns * n,)),
                   *[pltpu.HBM(p.shape, p.dtype) for p in parts],
                   *[pltpu.HBM(s, p.dtype) for s, p in zip(land_shapes, parts)],
                   jax.ShapeDtypeStruct((8, LANES), F32)),
        in_specs=[HBM] * (2 * n) + [ANY],
        out_specs=(SEM, SEM, *[HBM] * (2 * n), pl.BlockSpec(memory_space=pltpu.VMEM)),
        input_output_aliases={i: 2 + i for i in range(2 * n)},
        compiler_params=pltpu.CompilerParams(has_side_effects=SIDE_EFFECT),
    )(*in_hbm, after)
    return outs[0], outs[1], list(outs[2:2 + n]), list(outs[2 + n:2 + 2 * n]), outs[-1]


def _scatter_wait(send_sems, recv_sems, parts, lands, after, name, relations=ALL_PEERS):
    n = len(parts)

    def body(*refs):
        ins, lnd = refs[:n], refs[n:2 * n]
        for cp in _scatter_copies(ins, lnd, refs[2 * n], refs[2 * n + 1], relations):
            cp.wait_send()
            cp.wait_recv()

    outs = pl.pallas_call(
        body, name=name,
        out_shape=tuple(pltpu.HBM(a.shape, a.dtype) for a in parts + lands),
        in_specs=[HBM] * (2 * n) + [SEM, SEM, ANY],
        out_specs=tuple([HBM] * (2 * n)),
        input_output_aliases={i: i for i in range(2 * n)},
        compiler_params=pltpu.CompilerParams(has_side_effects=SIDE_EFFECT),
    )(*parts, *lands, send_sems, recv_sems, after)
    return list(outs[n:])


def _pair_exchange(parts, name):
    n = len(parts)

    def body(*refs):
        ins, outs = refs[:n], refs[n:2 * n]
        send_sems, recv_sems = refs[2 * n:]
        x, y, c = _place()
        cps = [pltpu.make_async_remote_copy(
            src_ref=ins[w].at[:, pl.ds(1 - c, 1)], dst_ref=outs[w], send_sem=send_sems.at[w], recv_sem=recv_sems.at[w],
            device_id=(x, y, 1 - c), device_id_type=MESH_ID) for w in range(n)]
        for cp in cps:
            cp.start()
        for cp in cps:
            cp.wait()

    return pl.pallas_call(
        body, name=name,
        out_shape=[jax.ShapeDtypeStruct((4, 1) + p.shape[2:], p.dtype) for p in parts],
        in_specs=[ANY] * n, out_specs=[ANY] * n,
        scratch_shapes=[pltpu.SemaphoreType.DMA((n,)), pltpu.SemaphoreType.DMA((n,))],
    )(*parts)


def _pair_sum(g8, r1, me, name):
    _, r, c = g8.shape
    tr = max(q for q in range(16, r + 1, 16) if r % q == 0 and q * c <= ADAM_TILE_ELEMS)

    def body(me_ref, g_ref, r_ref, o_ref):
        o_ref[...] = (g_ref[...] + r_ref[...].astype(F32)).astype(BF16)

    chip = lambda k, s: s[1] ^ (k + 1)
    return pl.pallas_call(
        body, name=name,
        out_shape=jax.ShapeDtypeStruct((4, r, c), BF16),
        grid_spec=pltpu.PrefetchScalarGridSpec(
            num_scalar_prefetch=1, grid=(3, r // tr),
            in_specs=[pl.BlockSpec((None, None, tr, c), lambda k, i, s: (chip(k, s), s[0] % 2, i, 0)),
                      pl.BlockSpec((None, tr, c), lambda k, i, s: (chip(k, s), i, 0))],
            out_specs=pl.BlockSpec((None, tr, c), lambda k, i, s: (chip(k, s), i, 0))),
        compiler_params=_params(("arbitrary", "arbitrary")),
    )(me, g8.reshape((4, 2) + g8.shape[1:]), r1)


def _adam(w, g, m, v):
    m2 = ADAM_B1 * m + (1.0 - ADAM_B1) * g
    v2 = ADAM_B2 * v + (1.0 - ADAM_B2) * (g * g)
    m_hat = m2 / (1.0 - ADAM_B1 ** ADAM_STEP)
    v_hat = v2 / (1.0 - ADAM_B2 ** ADAM_STEP)
    delta = -ADAM_LR * (m_hat / (jnp.sqrt(v_hat) + ADAM_EPS) + ADAM_WD * w)
    return delta, m2, v2


def _small_allreduce_adam(part, w, m, v, name):
    rows = part.shape[0]

    def body(p_ref, w_ref, m_ref, v_ref, g_ref, d_ref, mo_ref, vo_ref, buf, send_sems, recv_sems):
        x, y, c = _place()
        buf[0] = p_ref[...]
        cps = []
        for k in range(1, N_DEV):
            kx, ky, kc = (k >> 2) & 1, (k >> 1) & 1, k & 1
            peer = (x ^ kx, y ^ ky, c ^ kc)
            cps.append(pltpu.make_async_remote_copy(
                src_ref=p_ref, dst_ref=buf.at[k], send_sem=send_sems.at[k - 1], recv_sem=recv_sems.at[k - 1],
                device_id=peer, device_id_type=MESH_ID))
        for cp in cps:
            cp.start()
        for cp in cps:
            cp.wait()
        me = 4 * x + 2 * y + c
        total = buf[me]
        for d in range(1, N_DEV):
            total = total + buf[d ^ me]
        g_ref[...] = total
        delta, m2, v2 = _adam(w_ref[...], total, m_ref[...], v_ref[...])
        d_ref[...] = delta
        mo_ref[...] = m2
        vo_ref[...] = v2

    vm = pl.BlockSpec(memory_space=pltpu.VMEM)
    return pl.pallas_call(
        body, name=name,
        out_shape=[jax.ShapeDtypeStruct(part.shape, F32)] * 4,
        in_specs=[vm] * 4, out_specs=[vm] * 4,
        scratch_shapes=[pltpu.VMEM((N_DEV, rows, LANES), F32),
                        pltpu.SemaphoreType.DMA((N_DEV - 1,)), pltpu.SemaphoreType.DMA((N_DEV - 1,))],
    )(part, w, m, v)


def _final_adam(g8, land, w, m, v, me, dep, name, pair=None):
    _, r, c = g8.shape
    tr = max(q for q in range(16, r + 1, 16) if r % q == 0 and q * c <= ADAM_TILE_ELEMS)
    nland = land.shape[0]

    def body(me_ref, g_ref, land_ref, *rest):
        pair_ref = rest[0] if pair is not None else None
        w_ref, m_ref, v_ref, _, go_ref, d_ref, mo_ref, vo_ref = rest[-8:]
        g = g_ref[...]
        if pair_ref is not None:
            g = g + pair_ref[...].astype(F32)
        for k in range(nland):
            g = g + land_ref[k].astype(F32)
        go_ref[...] = g
        delta, m2, v2 = _adam(w_ref[...], g, m_ref[...], v_ref[...])
        d_ref[...] = delta
        mo_ref[...] = m2
        vo_ref[...] = v2

    plain = pl.BlockSpec((tr, c), lambda i, s: (i, 0))
    return pl.pallas_call(
        body, name=name,
        out_shape=[jax.ShapeDtypeStruct((r, c), F32)] * 4,
        grid_spec=pltpu.PrefetchScalarGridSpec(
            num_scalar_prefetch=1, grid=(r // tr,),
            in_specs=[pl.BlockSpec((None, tr, c), lambda i, s: (s[0], i, 0)),
                      pl.BlockSpec((nland, tr, c), lambda i, s: (0, i, 0))]
            + ([] if pair is None else [pl.BlockSpec((None, tr, c), lambda i, s: (s[1], i, 0))])
            + [plain, plain, plain, ANY],
            out_specs=[plain] * 4),
        compiler_params=_params(("arbitrary",)),
    )(*((me, g8, land) + (() if pair is None else (pair,)) + (w, m, v, dep)))


def _rms(x, gain):
    r = lax.rsqrt(jnp.mean(x * x, axis=-1, keepdims=True) + EPS)
    xh = x * r
    return xh * gain, xh, r


def _rms_bwd(xh, r, gain, dy):
    gdy = gain * dy
    dx = r * (gdy - xh * jnp.mean(xh * gdy, axis=-1, keepdims=True))
    return dx, jnp.sum(dy * xh, axis=0, keepdims=True)


def _load_weights(pairs, sems):
    cps = [pltpu.make_async_copy(src, dst, sems.at[i]) for i, (src, dst) in enumerate(pairs)]
    for cp in cps:
        cp.start()
    for cp in cps:
        cp.wait()


def _ffn_fwd(h, gain, wgu, wd, name):
    t, d = h.shape
    nb, nf, _ = wgu.shape
    nh = nb // 2
    tm = _row_tile(t, 512)

    def body(h_ref, g_ref, wgu_hbm, wd_hbm, out_ref, gu_ref, wgu_v, wd_v, sems):
        @pl.when(pl.program_id(0) == 0)
        def _():
            _load_weights([(wgu_hbm, wgu_v), (wd_hbm, wd_v)], sems)

        x = h_ref[...]
        n, _, _ = _rms(x, g_ref[...])
        nbf = n.astype(BF16)
        acc = jnp.zeros((tm, d), F32)
        for j in range(nh):
            g = _dot_nt(nbf, wgu_v[j])
            u = _dot_nt(nbf, wgu_v[j + nh])
            gu_ref[j] = g.astype(BF16)
            gu_ref[j + nh] = u.astype(BF16)
            a = (g * jax.nn.sigmoid(g)) * u
            acc = acc + _dot(a.astype(BF16), wd_v[j])
        out_ref[...] = x + 0.5 * acc

    return pl.pallas_call(
        body, name=name, grid=(t // tm,),
        out_shape=[jax.ShapeDtypeStruct((t, d), F32), jax.ShapeDtypeStruct((nb, t, nf), BF16)],
        in_specs=[pl.BlockSpec((tm, d), lambda i: (i, 0)), pl.BlockSpec((1, d), lambda i: (0, 0)), ANY, ANY],
        out_specs=[pl.BlockSpec((tm, d), lambda i: (i, 0)), pl.BlockSpec((nb, tm, nf), lambda i: (0, i, 0))],
        scratch_shapes=[pltpu.VMEM(wgu.shape, BF16), pltpu.VMEM(wd.shape, BF16), pltpu.SemaphoreType.DMA((2,))],
        compiler_params=_params(("arbitrary",)),
    )(h, gain, wgu, wd)


def _ffn_up(h, gain, wgu, name):
    t, d = h.shape
    nb, nf, _ = wgu.shape
    nh = nb // 2
    tm = _row_tile(t, 512)

    def body(h_ref, g_ref, wgu_hbm, gu_ref, a_ref, wgu_v, sems):
        @pl.when(pl.program_id(0) == 0)
        def _():
            _load_weights([(wgu_hbm, wgu_v)], sems)

        n, _, _ = _rms(h_ref[...], g_ref[...])
        nbf = n.astype(BF16)
        for j in range(nh):
            g = _dot_nt(nbf, wgu_v[j])
            u = _dot_nt(nbf, wgu_v[j + nh])
            gu_ref[j] = g.astype(BF16)
            gu_ref[j + nh] = u.astype(BF16)
            a_ref[j] = ((g * jax.nn.sigmoid(g)) * u).astype(BF16)

    return pl.pallas_call(
        body, name=name, grid=(t // tm,),
        out_shape=[jax.ShapeDtypeStruct((nb, t, nf), BF16), jax.ShapeDtypeStruct((nh, t, nf), BF16)],
        in_specs=[pl.BlockSpec((tm, d), lambda i: (i, 0)), pl.BlockSpec((1, d), lambda i: (0, 0)), ANY],
        out_specs=[pl.BlockSpec((nb, tm, nf), lambda i: (0, i, 0)), pl.BlockSpec((nh, tm, nf), lambda i: (0, i, 0))],
        scratch_shapes=[pltpu.VMEM(wgu.shape, BF16), pltpu.SemaphoreType.DMA((1,))],
        compiler_params=_params(("arbitrary",)),
    )(h, gain, wgu)


def _ffn_down(h, a, wd, name):
    t, d = h.shape
    nh, nf, _ = wd.shape
    tm = _row_tile(t, 512)

    def body(h_ref, a_ref, wd_ref, out_ref):
        acc = jnp.zeros((tm, d), F32)
        for j in range(nh):
            acc = acc + _dot(a_ref[j], wd_ref[j])
        out_ref[...] = h_ref[...] + 0.5 * acc

    row = pl.BlockSpec((tm, d), lambda i: (i, 0))
    return pl.pallas_call(
        body, name=name, grid=(t // tm,),
        out_shape=jax.ShapeDtypeStruct((t, d), F32),
        in_specs=[row, pl.BlockSpec((nh, tm, nf), lambda i: (0, i, 0)), pl.BlockSpec(wd.shape, lambda i: (0, 0, 0))],
        out_specs=row,
        compiler_params=_params(("arbitrary",)),
    )(h, a, wd)


def _ffn_bwd(dh, h, gain, gu, wgu, wd, name):
    t, d = h.shape
    nb, nf, _ = wgu.shape
    nh = nb // 2
    tm = _row_tile(t, 256)

    def body(dh_ref, h_ref, g_ref, gu_ref, wgu_hbm, wd_hbm, dhp_ref, dgu_ref, a_ref, n_ref, dgain_ref,
             wgu_v, wd_v, sems):
        @pl.when(pl.program_id(0) == 0)
        def _():
            _load_weights([(wgu_hbm, wgu_v), (wd_hbm, wd_v)], sems)
            dgain_ref[...] = jnp.zeros_like(dgain_ref)

        x = h_ref[...]
        gain_v = g_ref[...]
        n, xh, r = _rms(x, gain_v)
        n_ref[...] = n.astype(BF16)
        dh_v = dh_ref[...]
        dfb = (0.5 * dh_v).astype(BF16)
        dn = jnp.zeros((tm, d), F32)
        for j in range(nh):
            da = _dot_nt(dfb, wd_v[j])
            g = gu_ref[j].astype(F32)
            u = gu_ref[j + nh].astype(F32)
            sg = jax.nn.sigmoid(g)
            si = g * sg
            dg = (da * u * (sg * (1.0 + g * (1.0 - sg)))).astype(BF16)
            du = (da * si).astype(BF16)
            a_ref[j] = (si * u).astype(BF16)
            dgu_ref[j] = dg
            dgu_ref[j + nh] = du
            dn = dn + _dot(dg, wgu_v[j]) + _dot(du, wgu_v[j + nh])
        dx, dgain = _rms_bwd(xh, r, gain_v, dn)
        dhp_ref[...] = dh_v + dx
        dgain_ref[...] += dgain

    row = pl.BlockSpec((tm, d), lambda i: (i, 0))
    vec = pl.BlockSpec((1, d), lambda i: (0, 0))
    return pl.pallas_call(
        body, name=name, grid=(t // tm,),
        out_shape=[jax.ShapeDtypeStruct((t, d), F32), jax.ShapeDtypeStruct((nb, t, nf), BF16),
                   jax.ShapeDtypeStruct((nh, t, nf), BF16), jax.ShapeDtypeStruct((t, d), BF16),
                   jax.ShapeDtypeStruct((1, d), F32)],
        in_specs=[row, row, vec, pl.BlockSpec((nb, tm, nf), lambda i: (0, i, 0)), ANY, ANY],
        out_specs=[row, pl.BlockSpec((nb, tm, nf), lambda i: (0, i, 0)),
                   pl.BlockSpec((nh, tm, nf), lambda i: (0, i, 0)), row, vec],
        scratch_shapes=[pltpu.VMEM(wgu.shape, BF16), pltpu.VMEM(wd.shape, BF16), pltpu.SemaphoreType.DMA((2,))],
        compiler_params=_params(("arbitrary",)),
    )(dh, h, gain, gu, wgu, wd)


def _dw(xa, dy, nb, n, name, scale=1.0, dep=None):
    t, k = xa.shape[-2:]
    wide = xa.ndim == 2
    tt = _row_tile(t, 1024)
    steps = t // tt
    x_spec = pl.BlockSpec((tt, k), lambda i: (i, 0)) if wide else pl.BlockSpec((nb, tt, k), lambda i: (0, i, 0))
    dy_spec = pl.BlockSpec((tt, dy.shape[1]), lambda i: (i, 0))
    acc_shape = (k, nb * n) if wide else (nb, k, n)
    stage_shape = (k, nb * n) if wide else (k, n)

    def body(x_ref, dy_ref, *rest):
        o_hbm, ob_hbm, acc, stage, sems = rest[-5:]

        @pl.when(pl.program_id(0) == 0)
        def _():
            acc[...] = jnp.zeros_like(acc)

        dyb = dy_ref[...].astype(BF16)
        if wide:
            acc[...] += _dot(x_ref[...].astype(BF16).T, dyb)
        else:
            for j in range(nb):
                acc[j] += _dot_tn(x_ref[j].astype(BF16), dyb)

        @pl.when(pl.program_id(0) == steps - 1)
        def _():
            if scale != 1.0:
                acc[...] = acc[...] * scale
            if wide:
                cps = [pltpu.make_async_copy(acc.at[:, pl.ds(j * n, n)] if nb > 1 else acc, o_hbm.at[j], sems.at[j])
                       for j in range(nb)]
            else:
                cps = [pltpu.make_async_copy(acc, o_hbm, sems.at[0])]
            for cp in cps:
                cp.start()
            if wide:
                stage[...] = acc[...].astype(BF16)
                bcs = [pltpu.make_async_copy(stage.at[:, pl.ds(j * n, n)] if nb > 1 else stage, ob_hbm.at[j],
                                             sems.at[nb + j]) for j in range(nb)]
                for cp in bcs:
                    cp.start()
                for cp in bcs:
                    cp.wait()
            else:
                for j in range(nb):
                    stage[...] = acc[j].astype(BF16)
                    cp = pltpu.make_async_copy(stage, ob_hbm.at[j], sems.at[nb])
                    cp.start()
                    cp.wait()
            for cp in cps:
                cp.wait()

    return pl.pallas_call(
        body, name=name, grid=(steps,),
        out_shape=[jax.ShapeDtypeStruct((nb, k, n), F32), jax.ShapeDtypeStruct((nb, k, n), BF16)],
        in_specs=[x_spec, dy_spec] + ([] if dep is None else [ANY]),
        out_specs=[ANY, ANY],
        scratch_shapes=[pltpu.VMEM(acc_shape, F32), pltpu.VMEM(stage_shape, BF16),
                        pltpu.SemaphoreType.DMA((2 * nb,))],
        compiler_params=_params(("arbitrary",), DW_VMEM_LIMIT),
    )(*((xa, dy) if dep is None else (xa, dy, dep)))


def _proj_fwd(h, gain, win, wgate, name):
    t, d = h.shape
    tm = _row_tile(t, 512)
    nq, ng = win.shape[0], wgate.shape[1]

    def body(h_ref, g_ref, win_ref, wg_ref, un_ref, qkv_ref, gate_ref):
        n, _, _ = _rms(h_ref[...], g_ref[...])
        nbf = n.astype(BF16)
        un_ref[...] = nbf
        qkv_ref[...] = _dot_nt(nbf, win_ref[...])
        gate_ref[...] = jax.nn.sigmoid(_dot(nbf, wg_ref[...])).astype(BF16)

    full = lambda a: pl.BlockSpec(a.shape, lambda i: (0,) * a.ndim)
    return pl.pallas_call(
        body, name=name, grid=(t // tm,),
        out_shape=[jax.ShapeDtypeStruct((t, d), BF16), jax.ShapeDtypeStruct((t, nq), F32),
                   jax.ShapeDtypeStruct((t, ng), BF16)],
        in_specs=[pl.BlockSpec((tm, d), lambda i: (i, 0)), full(gain), full(win), full(wgate)],
        out_specs=[pl.BlockSpec((tm, d), lambda i: (i, 0)), pl.BlockSpec((tm, nq), lambda i: (i, 0)),
                   pl.BlockSpec((tm, ng), lambda i: (i, 0))],
        compiler_params=_params(("arbitrary",)),
    )(h, gain, win, wgate)


def _proj_bwd(dh, h, gain, dzg, dqkv_parts, win, wgate, name):
    t, d = h.shape
    tm = _row_tile(t, 512)
    ng = wgate.shape[1]
    np_ = len(dqkv_parts)
    widths = [a.shape[1] for a in dqkv_parts]

    def body(dh_ref, h_ref, g_ref, dzg_ref, *rest):
        part_refs, (win_ref, wg_ref, dhp_ref, dgain_ref) = rest[:np_], rest[np_:]

        @pl.when(pl.program_id(0) == 0)
        def _():
            dgain_ref[...] = jnp.zeros_like(dgain_ref)

        gain_v = g_ref[...]
        _, xh, r = _rms(h_ref[...], gain_v)
        dun = _dot_nt(dzg_ref[...], wg_ref[...])
        off = 0
        for ref, wd in zip(part_refs, widths):
            dun = dun + _dot(ref[...].astype(BF16), win_ref[off:off + wd, :])
            off += wd
        dx, dgain = _rms_bwd(xh, r, gain_v, dun)
        dhp_ref[...] = dh_ref[...] + dx
        dgain_ref[...] += dgain

    full = lambda a: pl.BlockSpec(a.shape, lambda i: (0,) * a.ndim)
    row = pl.BlockSpec((tm, d), lambda i: (i, 0))
    return pl.pallas_call(
        body, name=name, grid=(t // tm,),
        out_shape=[jax.ShapeDtypeStruct((t, d), F32), jax.ShapeDtypeStruct((1, d), F32)],
        in_specs=[row, row, full(gain), pl.BlockSpec((tm, ng), lambda i: (i, 0))]
        + [pl.BlockSpec((tm, wd), lambda i: (i, 0)) for wd in widths] + [full(win), full(wgate)],
        out_specs=[row, pl.BlockSpec((1, d), lambda i: (0, 0))],
        compiler_params=_params(("arbitrary",)),
    )(dh, h, gain, dzg, *dqkv_parts, win, wgate)


def _dw_rows(parts, dy, name):
    t, n = dy.shape
    widths = [a.shape[1] for a in parts]
    k = sum(widths)
    tt = _row_tile(t, 1024)
    steps = t // tt
    np_ = len(parts)

    def body(*refs):
        part_refs, dy_ref = refs[:np_], refs[np_]
        o_hbm, ob_hbm, acc, stage, sems = refs[np_ + 1:]

        @pl.when(pl.program_id(0) == 0)
        def _():
            acc[...] = jnp.zeros_like(acc)

        dyb = dy_ref[...].astype(BF16)
        off = 0
        for ref, wd in zip(part_refs, widths):
            acc[off:off + wd, :] += _dot(ref[...].astype(BF16).T, dyb)
            off += wd

        @pl.when(pl.program_id(0) == steps - 1)
        def _():
            stage[...] = acc[...].astype(BF16)
            cps = [pltpu.make_async_copy(acc, o_hbm.at[0], sems.at[0]),
                   pltpu.make_async_copy(stage, ob_hbm.at[0], sems.at[1])]
            for cp in cps:
                cp.start()
            for cp in cps:
                cp.wait()

    return pl.pallas_call(
        body, name=name, grid=(steps,),
        out_shape=[jax.ShapeDtypeStruct((1, k, n), F32), jax.ShapeDtypeStruct((1, k, n), BF16)],
        in_specs=[pl.BlockSpec((tt, wd), lambda i: (i, 0)) for wd in widths] + [pl.BlockSpec((tt, n), lambda i: (i, 0))],
        out_specs=[ANY, ANY],
        scratch_shapes=[pltpu.VMEM((k, n), F32), pltpu.VMEM((k, n), BF16), pltpu.SemaphoreType.DMA((2,))],
        compiler_params=_params(("arbitrary",)),
    )(*parts, dy)


def _merge_fwd(h, ya, yb, gate, wpa, wpb, wout, name):
    t, d = h.shape
    tm = _row_tile(t, 512)

    def body(h_ref, ya_ref, yb_ref, ga_ref, gb_ref, wpa_ref, wpb_ref, wout_ref, out_ref, mg_ref, pa_ref, pb_ref):
        pa = _dot(ya_ref[...].astype(BF16), wpa_ref[...])
        pb = _dot(yb_ref[...].astype(BF16), wpb_ref[...])
        merged = (ga_ref[...].astype(F32) * pa + gb_ref[...].astype(F32) * pb).astype(BF16)
        pa_ref[...] = pa.astype(BF16)
        pb_ref[...] = pb.astype(BF16)
        mg_ref[...] = merged
        out_ref[...] = h_ref[...] + _dot(merged, wout_ref[...])

    full = lambda a: pl.BlockSpec(a.shape, lambda i: (0,) * a.ndim)
    row = pl.BlockSpec((tm, d), lambda i: (i, 0))
    yrow = pl.BlockSpec((tm, ya.shape[1]), lambda i: (i, 0))
    return pl.pallas_call(
        body, name=name, grid=(t // tm,),
        out_shape=[jax.ShapeDtypeStruct((t, d), F32)] + [jax.ShapeDtypeStruct((t, d), BF16)] * 3,
        in_specs=[row, yrow, yrow, pl.BlockSpec((tm, d), lambda i: (i, 0)), pl.BlockSpec((tm, d), lambda i: (i, 1)),
                  full(wpa), full(wpb), full(wout)],
        out_specs=[row] * 4,
        compiler_params=_params(("arbitrary",)),
    )(h, ya, yb, gate, gate, wpa, wpb, wout)


def _merge_bwd(dh, pa, pb, gate, wpa, wpb, wout, name):
    t, d = dh.shape
    tm = _row_tile(t, 512)
    wy = wpa.shape[0]

    def body(dh_ref, pa_ref, pb_ref, ga_ref, gb_ref, wpa_ref, wpb_ref, wout_ref,
             dpa_ref, dpb_ref, dzg_ref, dya_ref, dyb_ref):
        dm = _dot_nt(dh_ref[...].astype(BF16), wout_ref[...])
        ga, gb = ga_ref[...].astype(F32), gb_ref[...].astype(F32)
        dpa = (dm * ga).astype(BF16)
        dpb = (dm * gb).astype(BF16)
        dpa_ref[...] = dpa
        dpb_ref[...] = dpb
        dzg_ref[:, :d] = (dm * pa_ref[...].astype(F32) * ga * (1.0 - ga)).astype(BF16)
        dzg_ref[:, d:] = (dm * pb_ref[...].astype(F32) * gb * (1.0 - gb)).astype(BF16)
        dya_ref[...] = _dot_nt(dpa, wpa_ref[...])
        dyb_ref[...] = _dot_nt(dpb, wpb_ref[...])

    full = lambda a: pl.BlockSpec(a.shape, lambda i: (0,) * a.ndim)
    row = pl.BlockSpec((tm, d), lambda i: (i, 0))
    yrow = pl.BlockSpec((tm, wy), lambda i: (i, 0))
    return pl.pallas_call(
        body, name=name, grid=(t // tm,),
        out_shape=[jax.ShapeDtypeStruct((t, d), BF16), jax.ShapeDtypeStruct((t, d), BF16),
                   jax.ShapeDtypeStruct((t, 2 * d), BF16), jax.ShapeDtypeStruct((t, wy), F32),
                   jax.ShapeDtypeStruct((t, wy), F32)],
        in_specs=[row, row, row, pl.BlockSpec((tm, d), lambda i: (i, 0)), pl.BlockSpec((tm, d), lambda i: (i, 1)),
                  full(wpa), full(wpb), full(wout)],
        out_specs=[row, row, pl.BlockSpec((tm, 2 * d), lambda i: (i, 0)), yrow, yrow],
        compiler_params=_params(("arbitrary",)),
    )(dh, pa, pb, gate, gate, wpa, wpb, wout)


def _ple_loss(h, gain, p, target, wpg, wpe, name):
    t, d = h.shape
    tm = _row_tile(t, 512)
    pd = p.shape[1]

    def body(h_ref, g_ref, p_ref, t_ref, wpg_ref, wpe_ref, dh_ref, dz_ref, dpp_ref, n_ref, dgain_ref, loss_ref):
        @pl.when(pl.program_id(0) == 0)
        def _():
            dgain_ref[...] = jnp.zeros_like(dgain_ref)
            loss_ref[...] = jnp.zeros_like(loss_ref)

        x = h_ref[...]
        gain_v = g_ref[...]
        n, xh, r = _rms(x, gain_v)
        nbf = n.astype(BF16)
        n_ref[...] = nbf
        pg = jax.nn.sigmoid(_dot(nbf, wpg_ref[...]))
        pp = _dot(p_ref[...].astype(BF16), wpe_ref[...])
        err = (x + pg * pp) - t_ref[...]
        loss_ref[...] += 0.5 * jnp.sum(jnp.mean(err * err, axis=-1, keepdims=True))
        dy = err * (1.0 / d)
        dpp_ref[...] = (dy * pg).astype(BF16)
        dz = (dy * pp * pg * (1.0 - pg)).astype(BF16)
        dz_ref[...] = dz
        dn = _dot_nt(dz, wpg_ref[...])
        dx, dgain = _rms_bwd(xh, r, gain_v, dn)
        dh_ref[...] = dy + dx
        dgain_ref[...] += dgain

    full = lambda a: pl.BlockSpec(a.shape, lambda i: (0,) * a.ndim)
    row = pl.BlockSpec((tm, d), lambda i: (i, 0))
    return pl.pallas_call(
        body, name=name, grid=(t // tm,),
        out_shape=[jax.ShapeDtypeStruct((t, d), F32), jax.ShapeDtypeStruct((t, d), BF16),
                   jax.ShapeDtypeStruct((t, d), BF16), jax.ShapeDtypeStruct((t, d), BF16),
                   jax.ShapeDtypeStruct((1, d), F32), jax.ShapeDtypeStruct((8, LANES), F32)],
        in_specs=[row, full(gain), pl.BlockSpec((tm, pd), lambda i: (i, 0)), row, full(wpg), full(wpe)],
        out_specs=[row, row, row, row, pl.BlockSpec((1, d), lambda i: (0, 0)),
                   pl.BlockSpec((8, LANES), lambda i: (0, 0))],
        compiler_params=_params(("arbitrary",)),
    )(h, gain, p, target, wpg, wpe)


def _head_masks():
    lane = lax.broadcasted_iota(jnp.int32, (1, LANES), 1)
    m0 = (lane < HEAD_DIM).astype(F32)
    return m0, 1.0 - m0


def _head_mean(v, m0, m1):
    del m0, m1
    width = v.shape[-1]
    shift = HEAD_DIM.bit_length() - 1
    r = jnp.right_shift(lax.broadcasted_iota(jnp.int32, (width, width), 0), shift)
    c = jnp.right_shift(lax.broadcasted_iota(jnp.int32, (width, width), 1), shift)
    same_head = (r == c).astype(BF16)
    return _dot(v.astype(BF16), same_head) * (1.0 / HEAD_DIM)


def _head_norm(x, gain, m0, m1):
    r = lax.rsqrt(_head_mean(x * x, m0, m1) + EPS)
    xh = x * r
    return xh * gain, xh, r


def _head_norm_bwd(xh, r, gain, dy, m0, m1):
    gdy = gain * dy
    dx = r * (gdy - xh * _head_mean(xh * gdy, m0, m1))
    return dx, jnp.sum(dy * xh, axis=0, keepdims=True)


GROUP = 4
QW = GROUP * HEAD_DIM
STACK = GROUP * QTILE


def _kv_width(mode):
    return QW if mode == "A" else LANES


def _q_scratch_shape(mode, s_len):
    return (s_len, QW) if mode == "A" else (GROUP * s_len, LANES)


def _group_masks(dtype=F32):
    lane = lax.broadcasted_iota(jnp.int32, (1, QW), 1)
    return [((lane >= h * HEAD_DIM) & (lane < (h + 1) * HEAD_DIM)).astype(dtype) for h in range(GROUP)]


def _stack_heads(first_kv, x, m0, m1):
    out = []
    for half in range(GROUP // 2):
        xh = x[:, half * LANES:(half + 1) * LANES]
        a0, a1 = xh * m0, xh * m1
        r0, r1 = pltpu.roll(a0, HEAD_DIM, 1), pltpu.roll(a1, HEAD_DIM, 1)
        out += [jnp.where(first_kv, a0, r0), jnp.where(first_kv, r1, a1)]
    return out


def _unstack_heads(mode, first_kv, ts, m0, m1):
    if mode == "A":
        masks = _group_masks()
        return sum(t * mk for t, mk in zip(ts, masks))
    halves = []
    for half in range(GROUP // 2):
        t0 = jnp.where(first_kv, ts[2 * half], pltpu.roll(ts[2 * half], HEAD_DIM, 1))
        t1 = jnp.where(first_kv, pltpu.roll(ts[2 * half + 1], HEAD_DIM, 1), ts[2 * half + 1])
        halves.append(t0 * m0 + t1 * m1)
    return jnp.concatenate(halves, axis=1)


def _store_stacked(dst, i, heads):
    for half in range(2):
        rows = slice(half * QTILE, (half + 1) * QTILE)
        for h, x in enumerate(heads):
            dst[pl.ds((2 * i + half) * STACK + h * QTILE, QTILE), :] = x[rows].astype(dst.dtype)


def _load_stacked(mode, ref, m):
    if mode == "B":
        return ref[pl.ds(pl.multiple_of(m * STACK, STACK), STACK), :]
    x = ref[pl.ds(pl.multiple_of(m * QTILE, QTILE), QTILE), :]
    return jnp.concatenate([x * mk for mk in _group_masks(x.dtype)], axis=0)


def _attn_prep(mode, group, s_len, padk, q_ref, k_ref, v_ref, gq_ref, gk_ref, qs, k2, v2, do_ref=None, dos=None):
    m0, m1 = _head_masks()
    zpad = jnp.zeros((padk, k2.shape[1]), BF16)
    k2[pl.ds(0, padk), :] = zpad
    v2[pl.ds(0, padk), :] = zpad
    first_kv = group == 0
    rt = 2 * QTILE
    for i in range(s_len // rt):
        rows = pl.ds(i * rt, rt)
        qn, _, _ = _head_norm(q_ref[rows, :], gq_ref[...], m0, m1)
        kn, _, _ = _head_norm(k_ref[rows, :], gk_ref[...], m0, m1)
        qn = qn * (HEAD_DIM ** -0.5)
        if mode == "A":
            qs[rows, :] = qn.astype(BF16)
            if dos is not None:
                dos[rows, :] = do_ref[rows, :].astype(BF16)
        else:
            _store_stacked(qs, i, _stack_heads(first_kv, qn, m0, m1))
            if dos is not None:
                _store_stacked(dos, i, _stack_heads(first_kv, do_ref[rows, :], m0, m1))
        k2[pl.ds(padk + i * rt, rt), :] = kn.astype(BF16)
        v2[pl.ds(padk + i * rt, rt), :] = v_ref[rows, :].astype(BF16)


def _softmax_terms(mode, s, sink):
    mx = jnp.max(s, axis=-1, keepdims=True)
    if mode == "B":
        mx = jnp.maximum(mx, sink)
    e = jnp.exp(s - mx)
    l = jnp.sum(e, axis=-1, keepdims=True)
    if mode == "B":
        l = l + jnp.exp(sink - mx)
    return e, mx, l


def _sink_column(sink_ref, group):
    row = lax.broadcasted_iota(jnp.int32, (STACK, 1), 0)
    col = jnp.zeros((STACK, 1), F32)
    for h in range(GROUP):
        col = jnp.where((row >= h * QTILE) & (row < (h + 1) * QTILE), sink_ref[GROUP * group + h], col)
    return col


def _head_deltas(dd, m0, m1):
    cols = []
    for half in range(GROUP // 2):
        dh = dd[:, half * LANES:(half + 1) * LANES]
        cols += [jnp.sum(dh * m0, axis=-1, keepdims=True), jnp.sum(dh * m1, axis=-1, keepdims=True)]
    return jnp.concatenate(cols, axis=0)


def _attn_cols(mode):
    if mode == "A":
        return (lambda b, g: (b, g)), (lambda b, g: (b, 2 + g)), (lambda b, g: (b, 4 + g))
    return (lambda b, g: (b, 6 + g)), (lambda b, g: (b, 16)), (lambda b, g: (b, 17))


def _attn_fwd(mode, qkv, gq, gk, bias, sinks, bl, s_len, name):
    bw = bias.shape[-1]
    padk = bw - QTILE
    nt = s_len // QTILE
    qmap, kmap, vmap = _attn_cols(mode)

    kw = _kv_width(mode)

    def body(q_ref, k_ref, v_ref, gq_ref, gk_ref, bias_ref, sink_ref, o_ref, qs, k2, v2, s_buf, *rest):
        o_buf = rest[0] if rest else None
        group = pl.program_id(1)
        m0, m1 = _head_masks()
        first_kv = group == 0
        _attn_prep(mode, group, s_len, padk, q_ref, k_ref, v_ref, gq_ref, gk_ref, qs, k2, v2)
        col = lax.broadcasted_iota(jnp.int32, (STACK, bw), 1)
        sink = _sink_column(sink_ref, group)

        def scores(m, slot):
            r0 = pl.multiple_of(m * QTILE, QTILE)
            s = _dot_nt(_load_stacked(mode, qs, m), k2[pl.ds(r0, bw), :]) + bias_ref[...]
            s_buf[slot] = jnp.where(col >= (padk - r0), s, NEG_INF)

        def finish_tile(m, slot):
            r0 = pl.multiple_of(m * QTILE, QTILE)
            e, _, l = _softmax_terms(mode, s_buf[slot], sink)
            if mode == "A":
                o_st = _dot(e.astype(BF16), v2[pl.ds(r0, bw), :]) / l
                heads = [o_st[h * QTILE:(h + 1) * QTILE] for h in range(GROUP)]
                o_ref[pl.ds(r0, QTILE), :] = _unstack_heads(mode, first_kv, heads, m0, m1)
            else:
                o_buf[pl.ds(pl.multiple_of(m * STACK, STACK), STACK), :] = _dot((e * (1.0 / l)).astype(BF16),
                                                                                 v2[pl.ds(r0, bw), :])

        scores(0, 0)

        def pair(j, carry):
            scores(2 * j + 1, 1)
            finish_tile(2 * j, 0)
            scores(jnp.minimum(2 * j + 2, nt - 1), 0)
            finish_tile(2 * j + 1, 1)
            return carry

        lax.fori_loop(0, nt // 2, pair, 0)
        if mode == "B":
            for m in range(nt):
                heads = [o_buf[pl.ds(m * STACK + h * QTILE, QTILE), :] for h in range(GROUP)]
                o_ref[pl.ds(m * QTILE, QTILE), :] = _unstack_heads(mode, first_kv, heads, m0, m1)

    blk = lambda w, f: pl.BlockSpec((s_len, w), f)
    return pl.pallas_call(
        body, name=name, grid=(bl, B_Q_HEADS // GROUP),
        out_shape=jax.ShapeDtypeStruct((bl * s_len, B_Q_HEADS * HEAD_DIM), F32),
        in_specs=[blk(QW, qmap), blk(kw, kmap), blk(kw, vmap),
                  pl.BlockSpec((1, QW), lambda b, g: (0, 0)), pl.BlockSpec((1, kw), lambda b, g: (0, 0)),
                  pl.BlockSpec((STACK, bw), lambda b, g: (g, 0)),
                  pl.BlockSpec(memory_space=pltpu.SMEM)],
        out_specs=blk(QW, lambda b, g: (b, g)),
        scratch_shapes=[pltpu.VMEM(_q_scratch_shape(mode, s_len), BF16)] + [pltpu.VMEM((s_len + padk, kw), BF16)] * 2
        + [pltpu.VMEM((2, STACK, bw), F32)] + ([pltpu.VMEM((GROUP * s_len, LANES), F32)] if mode == "B" else []),
        compiler_params=_params(("arbitrary", "arbitrary")),
    )(qkv, qkv, qkv, gq, gk, bias.reshape(-1, bw), sinks)


def _attn_bwd(mode, qkv, gq, gk, bias, sinks, y, dy, bl, s_len, name):
    bw = bias.shape[-1]
    padk = bw - QTILE
    nt = s_len // QTILE
    qmap, kmap, vmap = _attn_cols(mode)
    t = bl * s_len
    kw = _kv_width(mode)
    kvw = 4 * LANES if mode == "A" else LANES
    dp_ahead = True

    def body(q_ref, k_ref, v_ref, gq_ref, gk_ref, bias_ref, sink_ref, y_ref, dy_ref,
             dq_ref, dk_ref, dv_ref, dgq_ref, dgk_ref, dbias_ref, dsink_ref,
             qs, k2, v2, dos, dqs, dk, dv, s_buf, dp_buf):
        group = pl.program_id(1)
        m0, m1 = _head_masks()
        first_kv = group == 0
        _attn_prep(mode, group, s_len, padk, q_ref, k_ref, v_ref, gq_ref, gk_ref, qs, k2, v2, dy_ref, dos)
        dk[...] = jnp.zeros_like(dk)
        dv[...] = jnp.zeros_like(dv)
        dbias_ref[...] = jnp.zeros_like(dbias_ref)
        col = lax.broadcasted_iota(jnp.int32, (STACK, bw), 1)
        lane8 = lax.broadcasted_iota(jnp.int32, (8, LANES), 1)
        sink = _sink_column(sink_ref, group)

        def ahead(m, slot):
            r0 = pl.multiple_of(m * QTILE, QTILE)
            band = pl.ds(r0, bw)
            s = _dot_nt(_load_stacked(mode, qs, m), k2[band, :]) + bias_ref[...]
            s_buf[slot] = jnp.where(col >= (padk - r0), s, NEG_INF)
            if dp_ahead:
                dp_buf[slot] = _dot_nt(_load_stacked(mode, dos, m), v2[band, :])

        def tile(m, slot, dsink):
            r0 = pl.multiple_of(m * QTILE, QTILE)
            rows = pl.ds(r0, QTILE)
            band = pl.ds(r0, bw)
            q_st = _load_stacked(mode, qs, m)
            do_st = _load_stacked(mode, dos, m)
            delta = _head_deltas(dy_ref[rows, :] * y_ref[rows, :], m0, m1)
            kb = k2[band, :]
            e, mx, l = _softmax_terms(mode, s_buf[slot], sink)
            inv = 1.0 / l
            pn = e * inv
            ds = pn * ((dp_buf[slot] if dp_ahead else _dot_nt(do_st, v2[band, :])) - delta)
            if mode == "A":
                dbias_ref[...] += ds
            else:
                part = jnp.exp(sink - mx) * inv * delta
                for h in range(GROUP):
                    dsink = dsink - jnp.where(lane8 == h, jnp.sum(part[h * QTILE:(h + 1) * QTILE]), 0.0)
            dsb = ds.astype(BF16)
            dv[band, :] += _dot_tn(pn.astype(BF16), do_st)
            dk[band, :] += _dot_tn(dsb, q_st)
            dq_st = _dot(dsb, kb)
            if mode == "A":
                heads = [dq_st[h * QTILE:(h + 1) * QTILE] for h in range(GROUP)]
                dq_ref[rows, :] = _unstack_heads(mode, first_kv, heads, m0, m1)
            else:
                dqs[pl.ds(pl.multiple_of(m * STACK, STACK), STACK), :] = dq_st
            return dsink

        ahead(0, 0)

        def pair(j, dsink):
            ahead(2 * j + 1, 1)
            dsink = tile(2 * j, 0, dsink)
            ahead(jnp.minimum(2 * j + 2, nt - 1), 0)
            return tile(2 * j + 1, 1, dsink)

        dsink = lax.fori_loop(0, nt // 2, pair, jnp.zeros((8, LANES), F32))
        dsink_ref[...] = dsink

        rt = 2 * QTILE
        dgq = jnp.zeros((1, QW), F32)
        dgk = jnp.zeros((1, kw), F32)
        for i in range(s_len // rt):
            rows = pl.ds(i * rt, rt)
            src = pl.ds(padk + i * rt, rt)
            gq_v, gk_v = gq_ref[...], gk_ref[...]
            _, qh, qr = _head_norm(q_ref[rows, :], gq_v, m0, m1)
            _, kh, kr = _head_norm(k_ref[rows, :], gk_v, m0, m1)
            if mode == "A":
                dqn = dq_ref[rows, :] * (HEAD_DIM ** -0.5)
            else:
                dqn = jnp.concatenate(
                    [_unstack_heads(mode, first_kv, [dqs[pl.ds((2 * i + half) * STACK + h * QTILE, QTILE), :]
                                                     for h in range(GROUP)], m0, m1)
                     for half in range(2)], axis=0) * (HEAD_DIM ** -0.5)
            dq_raw, dgq_i = _head_norm_bwd(qh, qr, gq_v, dqn, m0, m1)
            dk_raw, dgk_i = _head_norm_bwd(kh, kr, gk_v, dk[src, :], m0, m1)
            dvn = dv[src, :]
            dq_ref[rows, :] = dq_raw
            if mode == "A":
                dk_ref[rows, :] = dk_raw
                dv_ref[rows, :] = dvn
            else:
                @pl.when(group == 0)
                def _():
                    dk_ref[rows, :] = dk_raw
                    dv_ref[rows, :] = dvn

                @pl.when(group != 0)
                def _():
                    dk_ref[rows, :] += dk_raw
                    dv_ref[rows, :] += dvn
            dgq, dgk = dgq + dgq_i, dgk + dgk_i
        dgq_ref[...] = jnp.broadcast_to(dgq, (8, QW))
        dgk_ref[...] = jnp.broadcast_to(dgk, (8, kw))

    ng = B_Q_HEADS // GROUP
    blk = lambda w, f: pl.BlockSpec((s_len, w), f)
    small = lambda w: pl.BlockSpec((None, None, 8, w), lambda b, g: (b, g, 0, 0))
    own = lambda b, g: (b, g)
    kvmap = own if mode == "A" else (lambda b, g: (b, 0))
    pad_f32 = pltpu.VMEM((s_len + padk, kw), F32)
    pad_bf = pltpu.VMEM((s_len + padk, kw), BF16)
    stack_bf = pltpu.VMEM(_q_scratch_shape(mode, s_len), BF16)
    outs = pl.pallas_call(
        body, name=name, grid=(bl, ng),
        out_shape=[jax.ShapeDtypeStruct((t, ng * QW), F32), jax.ShapeDtypeStruct((t, kvw), F32),
                   jax.ShapeDtypeStruct((t, kvw), F32),
                   jax.ShapeDtypeStruct((bl, ng, 8, QW), F32), jax.ShapeDtypeStruct((bl, ng, 8, kw), F32),
                   jax.ShapeDtypeStruct((bl, ng * STACK, bw), F32), jax.ShapeDtypeStruct((bl, ng, 8, LANES), F32)],
        in_specs=[blk(QW, qmap), blk(kw, kmap), blk(kw, vmap),
                  pl.BlockSpec((1, QW), lambda b, g: (0, 0)), pl.BlockSpec((1, kw), lambda b, g: (0, 0)),
                  pl.BlockSpec((STACK, bw), lambda b, g: (g, 0)),
                  pl.BlockSpec(memory_space=pltpu.SMEM),
                  blk(QW, own), blk(QW, own)],
        out_specs=[blk(QW, own), blk(kw, kvmap), blk(kw, kvmap), small(QW), small(kw),
                   pl.BlockSpec((None, STACK, bw), lambda b, g: (b, g, 0)), small(LANES)],
        scratch_shapes=[stack_bf, pad_bf, pad_bf, stack_bf,
                        pltpu.VMEM((8, LANES) if mode == "A" else _q_scratch_shape(mode, s_len), F32),
                        pad_f32, pad_f32, pltpu.VMEM((2, STACK, bw), F32),
                        pltpu.VMEM((2, STACK, bw) if dp_ahead else (8, LANES), F32)],
        compiler_params=_params(("arbitrary", "arbitrary")),
    )(qkv, qkv, qkv, gq, gk, bias.reshape(-1, bw), sinks, y, dy)
    outs = list(outs)
    outs[5] = outs[5].reshape(bl, B_Q_HEADS, QTILE, bw)
    return outs


def _band_geometry(prev):
    bw = QTILE + prev * CHUNK
    i = np.arange(QTILE)[:, None]
    j = np.arange(bw)[None, :]
    dist = i + prev * CHUNK - j
    valid = (j // CHUNK >= i // CHUNK) & (j // CHUNK <= i // CHUNK + prev)
    return dist, valid


A_VAR0 = (A_PREV * CHUNK - A_MAX_REL) // LANES * LANES


A_NVAR = QTILE + A_PREV * CHUNK - A_VAR0


def _skew_rows(x, sign):
    rows, n = x.shape
    row = lax.broadcasted_iota(jnp.int32, x.shape, 0)
    b = 1
    while b < rows:
        x = jnp.where((row & b) != 0, pltpu.roll(x, (sign * b) % n, 1), x)
        b *= 2
    return x


def _rel_bias_expand(table, name):
    _, valid = _band_geometry(A_PREV)
    bw = valid.shape[1]
    valid_f = jnp.asarray(valid.astype(np.float32))
    rev = jnp.flip(table[:, 1:], axis=1).reshape(A_HEADS, 1, A_NVAR)

    def body(rev_ref, valid_ref, o_ref):
        rowv = jnp.broadcast_to(rev_ref[...], (QTILE, A_NVAR))
        top = rowv[:, 0:1]
        var = _skew_rows(rowv, 1)
        row = lax.broadcasted_iota(jnp.int32, (QTILE, A_NVAR), 0)
        colv = lax.broadcasted_iota(jnp.int32, (QTILE, A_NVAR), 1)
        var = jnp.where(colv < row, top, var)
        ok = valid_ref[...] > 0.5
        o_ref[:, :A_VAR0] = jnp.where(ok[:, :A_VAR0], top, NEG_INF)
        o_ref[:, A_VAR0:] = jnp.where(ok[:, A_VAR0:], var, NEG_INF)

    return pl.pallas_call(
        body, name=name, grid=(A_HEADS,),
        out_shape=jax.ShapeDtypeStruct((A_HEADS, QTILE, bw), F32),
        in_specs=[pl.BlockSpec((None, 1, A_NVAR), lambda h: (h, 0, 0)), pl.BlockSpec((QTILE, bw), lambda h: (0, 0))],
        out_specs=pl.BlockSpec((None, QTILE, bw), lambda h: (h, 0, 0)),
        compiler_params=_params(("arbitrary",)),
    )(rev, valid_f)


def _rel_bias_grad(dbias, name):
    bl = dbias.shape[0]
    bw = dbias.shape[-1]

    def body(db_ref, o_ref):
        g = db_ref[0]
        for b in range(1, bl):
            g = g + db_ref[b]
        sk = _skew_rows(g[:, A_VAR0:], -1)
        row = lax.broadcasted_iota(jnp.int32, (QTILE, A_NVAR), 0)
        colv = lax.broadcasted_iota(jnp.int32, (QTILE, A_NVAR), 1)
        wrapped = (row + colv) >= A_NVAR
        main = jnp.sum(jnp.where(wrapped, 0.0, sk), axis=0, keepdims=True)
        top = jnp.sum(g[:, :A_VAR0]) + jnp.sum(jnp.where(wrapped, sk, 0.0))
        o_ref[:, :A_NVAR] = jnp.broadcast_to(main, (8, A_NVAR))
        o_ref[:, A_NVAR:] = jnp.full((8, LANES), top, F32)

    out = pl.pallas_call(
        body, name=name, grid=(A_HEADS,),
        out_shape=jax.ShapeDtypeStruct((A_HEADS, 8, A_NVAR + LANES), F32),
        in_specs=[pl.BlockSpec((bl, None, QTILE, bw), lambda h: (0, h, 0, 0))],
        out_specs=pl.BlockSpec((None, 8, A_NVAR + LANES), lambda h: (h, 0, 0)),
        compiler_params=_params(("arbitrary",)),
    )(dbias)
    main, top = out[:, 0, :A_NVAR], out[:, 0, A_NVAR]
    fm = jnp.flip(main, axis=1)
    return jnp.concatenate([jnp.zeros((A_HEADS, 1), F32), fm[:, :-1], fm[:, -1:] + top[:, None]], axis=1)


def _alibi_bias():
    dist, valid = _band_geometry(B_PREV)
    slopes = np.array([2.0 ** (-8.0 * (h + 1) / B_Q_HEADS) for h in range(B_Q_HEADS)], dtype=np.float32)
    bias = -slopes[:, None, None] * np.abs(dist).astype(np.float32)[None]
    return jnp.asarray(np.where(valid[None], bias, np.float32(NEG_INF)).astype(np.float32))


SMALL_NAMES = ("ffn1_norm", "mix_norm", "ffn2_norm", "ple_norm", "a_q_norm", "a_k_norm", "b_q_norm", "b_k_norm",
               "a_rel_bias", "b_sinks", "loss")


def _pack_small(vals):
    rows = []
    for nme in SMALL_NAMES:
        v = vals[nme].astype(F32)
        if nme == "a_rel_bias":
            v = jnp.pad(v.reshape(A_HEADS, -1), ((0, 0), (0, 3 * LANES - (2 * A_MAX_REL + 1))))
        v = v.reshape(-1)
        v = jnp.pad(v, (0, (-v.shape[0]) % LANES))
        rows.append(v.reshape(-1, LANES))
    out = jnp.concatenate(rows, axis=0)
    return jnp.pad(out, ((0, (-out.shape[0]) % 8), (0, 0)))


def _unpack_small(packed, shapes):
    out, r = {}, 0
    for nme in SMALL_NAMES:
        shp = shapes[nme]
        if nme == "a_rel_bias":
            nr = A_HEADS * 3
            out[nme] = packed[r:r + nr].reshape(A_HEADS, 3 * LANES)[:, :2 * A_MAX_REL + 1].reshape(shp)
        else:
            size = int(np.prod(shp)) if shp else 1
            nr = -(-size // LANES)
            out[nme] = packed[r:r + nr].reshape(-1)[:size].reshape(shp)
        r += nr
    return out


BIG_NAMES = ("ffn1_w_gu", "ffn1_w_down", "w_in", "w_gate", "w_proj_a", "w_proj_b", "w_out",
             "ffn2_w_gu", "ffn2_w_down", "w_ple_gate", "w_ple_proj")
WEIGHT_ORDER = ("ffn1_norm", "ffn1_w_gu", "ffn1_w_down", "mix_norm", "w_in", "a_q_norm", "a_k_norm", "a_rel_bias",
                "b_q_norm", "b_k_norm", "b_sinks", "w_gate", "w_proj_a", "w_proj_b", "w_out", "ffn2_norm",
                "ffn2_w_gu", "ffn2_w_down", "ple_norm", "w_ple_gate", "w_ple_proj")


TRANSPOSED = ("ffn1_w_gu", "ffn2_w_gu", "w_in")


def _local(a, nme):
    return a[0].T if nme in TRANSPOSED else a[0]


def _full_cols(wg):
    nb, k, n = wg.shape
    return jnp.transpose(wg, (1, 0, 2)).reshape(k, nb * n)


def _step(x, p, target, w, m, v):
    bl, s_len, d = x.shape
    t = bl * s_len
    h0 = x.reshape(t, d)
    pt = p.reshape(t, p.shape[-1])
    tgt = target.reshape(t, d)

    g_ffn1, g_mix, g_ffn2, g_ple = w["ffn1_norm"], w["mix_norm"], w["ffn2_norm"], w["ple_norm"]
    tiled = lambda a, width: jnp.tile(a.reshape(1, HEAD_DIM), (1, width // HEAD_DIM))
    gqa, gka = tiled(w["a_q_norm"], QW), tiled(w["a_k_norm"], _kv_width("A"))
    gqb, gkb = tiled(w["b_q_norm"], QW), tiled(w["b_k_norm"], _kv_width("B"))
    sinks = w["b_sinks"].reshape(B_Q_HEADS)
    bias_b = _alibi_bias()

    ffn1_names = ("ffn1_w_gu", "ffn1_w_down")
    shard = {nme: _local(w[nme], nme).astype(BF16) for nme in ffn1_names}
    send1, recv1, bufs, token = _gather_start([shard["ffn1_w_gu"]], h0, "gather_start_ffn1_gu")
    dsend1, drecv1, dbufs, token = _gather_start([shard["ffn1_w_down"]], token, "gather_start_ffn1_down")
    zero = token[0, 0]
    shard.update({nme: (_local(w[nme], nme) + zero).astype(BF16) for nme in BIG_NAMES if nme not in ffn1_names})
    bias_a = _rel_bias_expand(w["a_rel_bias"][0] + zero, "rel_bias_expand")
    send2, recv2, bufs, token = _gather_pass(send1, recv1, bufs, bias_a, "gather_pass_ffn1_gu")
    (wgu1,) = _gather_wait(send2, recv2, bufs, shard["ffn2_w_gu"], "gather_wait_ffn1_gu")
    nf = wgu1.shape[1]
    mixer_names = ("w_in", "w_gate")
    rest_names = ("w_proj_a", "w_proj_b", "w_out", "ffn2_w_gu", "ffn2_w_down", "w_ple_gate", "w_ple_proj")
    send1, recv1, bufs, token = _gather_start([shard[nme] for nme in mixer_names], wgu1, "gather_start_mixer")
    rsend1, rrecv1, rest_bufs, token = _gather_start([shard[nme] for nme in rest_names], token, "gather_start_rest")

    gu1, a1f = _ffn_up(h0, g_ffn1 + token[0, 0], wgu1, "ffn1_up")
    dsend2, drecv2, dbufs, token = _gather_pass(dsend1, drecv1, dbufs, a1f, "gather_pass_ffn1_down")
    (wd1,) = _gather_wait(dsend2, drecv2, dbufs, token, "gather_wait_ffn1_down")
    wd1 = wd1.reshape(N_DEV // 2, nf, d)
    h1 = _ffn_down(h0, a1f, wd1, "ffn1_down")
    send2, recv2, bufs, token = _gather_pass(send1, recv1, bufs, h1, "gather_pass_mixer")
    win, wgate = _gather_wait(send2, recv2, bufs, token, "gather_wait_mixer")
    win, wgate = win.reshape(IN_COLS, d), _full_cols(wgate)
    un, qkv, gate = _proj_fwd(h1, g_mix, win, wgate, "proj_fwd")
    ya = _attn_fwd("A", qkv, gqa, gka, bias_a, sinks, bl, s_len, "attn_a_fwd")
    rsend2, rrecv2, rest_bufs, token = _gather_pass(rsend1, rrecv1, rest_bufs, ya, "gather_pass_rest")
    yb = _attn_fwd("B", qkv, gqb + token[0, 0], gkb, bias_b, sinks, bl, s_len, "attn_b_fwd")
    gathered = dict(zip(rest_names, _gather_wait(rsend2, rrecv2, rest_bufs, yb, "gather_wait_rest")))
    wgu2 = gathered["ffn2_w_gu"]
    wd2 = gathered["ffn2_w_down"].reshape(N_DEV // 2, nf, d)
    wpa = _full_cols(gathered["w_proj_a"])
    wpb = _full_cols(gathered["w_proj_b"])
    wpe = _full_cols(gathered["w_ple_proj"])
    wout = gathered["w_out"].reshape(d, d)
    wpg = gathered["w_ple_gate"].reshape(d, d)
    h2, merged, pa, pb = _merge_fwd(h1, ya, yb, gate, wpa, wpb, wout, "merge_fwd")
    h3, gu2 = _ffn_fwd(h2, g_ffn2, wgu2, wd2, "ffn2_fwd")
    dh3, dz4, dpp, n4, dg_ple, loss_part = _ple_loss(h3, g_ple, pt, tgt, wpg, wpe, "ple_loss")

    xi, yi, ci = _place()
    me = jnp.stack([4 * xi + 2 * yi + ci, 2 * xi + yi]).astype(jnp.int32)
    g32, g16, big, pairs = {}, {}, {}, {}

    def keep(nme, pair, rows=None):
        for store, g in zip((g32, g16), pair):
            store[nme] = g if rows is None else g.reshape(N_DEV, rows, d)

    def start(names, after, tag):
        send, recv, parts, lands, token = _scatter_start([g16[nme] for nme in names], after, "grads_start_" + tag)
        return names, send, recv, parts, lands, token

    def start_two_level(names, after, tag):
        views = [g16[nme].reshape((4, 2) + g16[nme].shape[1:]) for nme in names]
        for nme, got in zip(names, _pair_exchange(views, "grads_pair_" + tag)):
            pairs[nme] = got.reshape((4,) + got.shape[2:])
        sums = [_pair_sum(g32[nme], pairs[nme], me, "pair_sum_" + nme) for nme in names]
        send, recv, parts, lands, token = _scatter_start(sums, after, "grads_start_" + tag, SAME_CORE_CHIPS)
        return names, send, recv, parts, lands, token

    def finish(state, after, tag):
        names, send, recv, parts, lands, _ = state
        relations = SAME_CORE_CHIPS if names[0] in pairs else ALL_PEERS
        lands = _scatter_wait(send, recv, parts, lands, after, "grads_wait_" + tag, relations)
        return names, lands

    def adam(done, dep):
        for nme, land in zip(*done):
            outs = _final_adam(g32[nme], land, _local(w[nme], nme), _local(m[nme], nme), _local(v[nme], nme), me, dep,
                               "adam_" + nme, pairs.get(nme))
            big[nme] = [(o.T if nme in TRANSPOSED else o)[None] for o in outs]

    keep("w_ple_gate", _dw(n4, dz4, 1, d, "dw_ple_gate"), d // N_DEV)
    keep("w_ple_proj", _dw(pt, dpp, N_DEV, d // N_DEV, "dw_ple_proj"))

    dh2, dgu2, a2, n3, dg_ffn2 = _ffn_bwd(dh3, h2, g_ffn2, gu2, wgu2, wd2, "ffn2_bwd")
    keep("ffn2_w_down", _dw(a2, dh3, N_DEV // 2, d, "dw_ffn2_down", 0.5), nf // 2)
    early = [(start(("w_ple_gate", "w_ple_proj", "ffn2_w_down"), dh2, "ffn2_down"), "ffn2_down")]
    keep("ffn2_w_gu", _dw(dgu2, n3, N_DEV, d, "dw_ffn2_gu", dep=early[-1][0][-1]))
    flight = start(("ffn2_w_gu",), dh2, "ffn2")

    dpa, dpb, dzg, dya, dyb = _merge_bwd(dh2, pa, pb, gate, wpa, wpb, wout, "merge_bwd")
    keep("w_out", _dw(merged, dh2, 1, d, "dw_out"), d // N_DEV)
    keep("w_proj_a", _dw(ya, dpa, N_DEV, d // N_DEV, "dw_proj_a"))
    keep("w_proj_b", _dw(yb, dpb, N_DEV, d // N_DEV, "dw_proj_b"))
    keep("w_gate", _dw(un, dzg, N_DEV, 2 * d // N_DEV, "dw_gate"))

    tok = flight[-1][0, 0]
    dqa, dka, dva, dgqa, dgka, dbias, _ = _attn_bwd("A", qkv, gqa + tok, gka, bias_a, sinks, ya, dya, bl, s_len,
                                                     "attn_a_bwd")
    dqb, dkb, dvb, dgqb, dgkb, _, dsink = _attn_bwd("B", qkv, gqb, gkb, bias_b, sinks, yb, dyb, bl, s_len, "attn_b_bwd")
    dqkv = [dqa, dka, dva, dqb, dkb, dvb]
    dtab = _rel_bias_grad(dbias, "rel_bias_grad")

    dh1, dg_mix = _proj_bwd(dh2, h1, g_mix, dzg, dqkv, win, wgate, "proj_bwd")
    keep("w_in", _dw_rows(dqkv, un, "dw_in"), IN_COLS // N_DEV)
    waiting = [finish(state, g32["w_in"], tag) for state, tag in early]
    done = finish(flight, waiting[-1][1][0], "ffn2")
    flight = start(("w_out", "w_proj_a", "w_proj_b", "w_gate", "w_in"), done[1][0], "mixer")
    waiting.append(done)

    dh0, dgu1, a1, n1, dg_ffn1 = _ffn_bwd(dh1, h0, g_ffn1 + flight[-1][0, 0], gu1, wgu1, wd1, "ffn1_bwd")
    keep("ffn1_w_down", _dw(a1, dh1, N_DEV // 2, d, "dw_ffn1_down", 0.5), nf // 2)
    done = finish(flight, g32["ffn1_w_down"], "mixer")
    flight = start(("ffn1_w_down",), done[1][0], "ffn1_down")
    waiting.append(done)

    keep("ffn1_w_gu", _dw(dgu1, n1, N_DEV, d, "dw_ffn1_gu", dep=flight[-1]))
    done = finish(flight, g32["ffn1_w_gu"], "ffn1_down")
    flight = start_two_level(("ffn1_w_gu",), done[1][0], "ffn1_gu")
    for group in waiting + [done]:
        adam(group, flight[-1])
    behind = 0.0 * big["ffn1_w_down"][0][0, 0, :1]
    smalls = (dg_ffn1, dg_mix, dg_ffn2, dg_ple + behind, dgqa, dgka, dgqb, dgkb, dtab, dsink)
    return dh0, loss_part, big, smalls, flight, finish, adam


def kernel(x, p, ffn1_norm, ffn1_w_gu, ffn1_w_down, mix_norm, w_in, a_q_norm, a_k_norm, a_rel_bias, b_q_norm, b_k_norm, b_sinks, w_gate, w_proj_a, w_proj_b, w_out, ffn2_norm, ffn2_w_gu, ffn2_w_down, ple_norm, w_ple_gate, w_ple_proj, loss_target, m_ffn1_norm, m_ffn1_w_gu, m_ffn1_w_down, m_mix_norm, m_w_in, m_a_q_norm, m_a_k_norm, m_a_rel_bias, m_b_q_norm, m_b_k_norm, m_b_sinks, m_w_gate, m_w_proj_a, m_w_proj_b, m_w_out, m_ffn2_norm, m_ffn2_w_gu, m_ffn2_w_down, m_ple_norm, m_w_ple_gate, m_w_ple_proj, v_ffn1_norm, v_ffn1_w_gu, v_ffn1_w_down, v_mix_norm, v_w_in, v_a_q_norm, v_a_k_norm, v_a_rel_bias, v_b_q_norm, v_b_k_norm, v_b_sinks, v_w_gate, v_w_proj_a, v_w_proj_b, v_w_out, v_ffn2_norm, v_ffn2_w_gu, v_ffn2_w_down, v_ple_norm, v_w_ple_gate, v_w_ple_proj):
    w = dict(ffn1_norm=ffn1_norm, ffn1_w_gu=ffn1_w_gu, ffn1_w_down=ffn1_w_down, mix_norm=mix_norm, w_in=w_in,
             a_q_norm=a_q_norm, a_k_norm=a_k_norm, a_rel_bias=a_rel_bias, b_q_norm=b_q_norm, b_k_norm=b_k_norm,
             b_sinks=b_sinks, w_gate=w_gate, w_proj_a=w_proj_a, w_proj_b=w_proj_b, w_out=w_out, ffn2_norm=ffn2_norm,
             ffn2_w_gu=ffn2_w_gu, ffn2_w_down=ffn2_w_down, ple_norm=ple_norm, w_ple_gate=w_ple_gate,
             w_ple_proj=w_ple_proj)
    m = dict(ffn1_norm=m_ffn1_norm, ffn1_w_gu=m_ffn1_w_gu, ffn1_w_down=m_ffn1_w_down, mix_norm=m_mix_norm,
             w_in=m_w_in, a_q_norm=m_a_q_norm, a_k_norm=m_a_k_norm, a_rel_bias=m_a_rel_bias, b_q_norm=m_b_q_norm,
             b_k_norm=m_b_k_norm, b_sinks=m_b_sinks, w_gate=m_w_gate, w_proj_a=m_w_proj_a, w_proj_b=m_w_proj_b,
             w_out=m_w_out, ffn2_norm=m_ffn2_norm, ffn2_w_gu=m_ffn2_w_gu, ffn2_w_down=m_ffn2_w_down,
             ple_norm=m_ple_norm, w_ple_gate=m_w_ple_gate, w_ple_proj=m_w_ple_proj)
    v = dict(ffn1_norm=v_ffn1_norm, ffn1_w_gu=v_ffn1_w_gu, ffn1_w_down=v_ffn1_w_down, mix_norm=v_mix_norm,
             w_in=v_w_in, a_q_norm=v_a_q_norm, a_k_norm=v_a_k_norm, a_rel_bias=v_a_rel_bias, b_q_norm=v_b_q_norm,
             b_k_norm=v_b_k_norm, b_sinks=v_b_sinks, w_gate=v_w_gate, w_proj_a=v_w_proj_a, w_proj_b=v_w_proj_b,
             w_out=v_w_out, ffn2_norm=v_ffn2_norm, ffn2_w_gu=v_ffn2_w_gu, ffn2_w_down=v_ffn2_w_down,
             ple_norm=v_ple_norm, w_ple_gate=v_w_ple_gate, w_ple_proj=v_w_ple_proj)
    bl, s_len, d = x.shape

    dh0, loss_part, big, smalls, flight, finish, adam = _step(x, p[0], loss_target, w, m, v)
    dg_ffn1, dg_mix, dg_ffn2, dg_ple, dgqa, dgka, dgqb, dgkb, dtab, dsink = smalls

    fold = lambda a: a[:, :, 0, :].reshape(-1, HEAD_DIM).sum(axis=0)
    small_part = dict(
        ffn1_norm=dg_ffn1, mix_norm=dg_mix, ffn2_norm=dg_ffn2, ple_norm=dg_ple,
        a_q_norm=fold(dgqa), a_k_norm=fold(dgka), b_q_norm=fold(dgqb), b_k_norm=fold(dgkb),
        a_rel_bias=dtab,
        b_sinks=dsink.sum(axis=0)[:, 0, :GROUP].reshape(B_Q_HEADS),
        loss=loss_part[0, :1])
    zero1 = jnp.zeros((1,), F32)
    shapes = {nme: w[nme].shape for nme in SMALL_NAMES if nme != "loss"}
    shapes["loss"] = ()
    pk = lambda src: _pack_small({**{nme: src[nme] for nme in SMALL_NAMES if nme != "loss"}, "loss": zero1})
    sg, sd, sm, sv = _small_allreduce_adam(_pack_small(small_part), pk(w), pk(m), pk(v), "small_allreduce_adam")
    adam(finish(flight, sg, "ffn1_gu"), sg)
    sg, sd, sm, sv = (_unpack_small(a, shapes) for a in (sg, sd, sm, sv))

    def pick(i):
        out = []
        for nme in WEIGHT_ORDER:
            out.append(big[nme][i] if nme in big else (sg, sd, sm, sv)[i][nme])
        return out

    return (sg["loss"], dh0.reshape(bl, s_len, d), *pick(0), *pick(1), *pick(2), *pick(3))
```

```python
import jax
import jax.numpy as jnp
import numpy as np
from jax import lax
from jax.experimental import pallas as pl
from jax.experimental.pallas import tpu as pltpu

F32 = jnp.float32
BF16 = jnp.bfloat16

CHUNK = 64
HEAD_DIM = 64
A_HEADS = 8
A_PREV = 8
A_MAX_REL = 128
B_Q_HEADS = 8
B_KV_HEADS = 2
B_PREV = 2
A_WIDTH = A_HEADS * HEAD_DIM
B_Q_WIDTH = B_Q_HEADS * HEAD_DIM
B_KV_WIDTH = B_KV_HEADS * HEAD_DIM
IN_COLS = 3 * A_WIDTH + B_Q_WIDTH + 2 * B_KV_WIDTH
EPS = 1e-6
NEG_INF = -1e30
ADAM_LR = 0.001
ADAM_B1 = 0.9
ADAM_B2 = 0.999
ADAM_EPS = 1e-08
ADAM_WD = 0.01
ADAM_STEP = 10

N_DEV = 8
LANES = 128
QTILE = 2 * CHUNK
VMEM_LIMIT = 56 * 1024 * 1024
DW_VMEM_LIMIT = 60 * 1024 * 1024
ADAM_TILE_ELEMS = 256 * 1024

MESH_ID = pl.DeviceIdType.MESH
ANY = pl.BlockSpec(memory_space=pl.ANY)
HBM = pl.BlockSpec(memory_space=pltpu.HBM)
SEM = pl.BlockSpec(memory_space=pltpu.SEMAPHORE)
SIDE_EFFECT = pltpu.SideEffectType.DATAFLOW_SIDE_EFFECTING


def _dot(a, b):
    return jnp.dot(a, b, preferred_element_type=F32)


def _dot_nt(a, b):
    return lax.dot_general(a, b, (((1,), (1,)), ((), ())), preferred_element_type=F32)


def _dot_tn(a, b):
    return lax.dot_general(a, b, (((0,), (0,)), ((), ())), preferred_element_type=F32)


def _params(sem=None, vmem=VMEM_LIMIT):
    return pltpu.CompilerParams(dimension_semantics=sem, vmem_limit_bytes=vmem)


def _row_tile(t, want):
    while t % want:
        want //= 2
    return want


def _place():
    return lax.axis_index("x"), lax.axis_index("y"), lax.axis_index("c")


def _gather_level(bufs, send_sems, recv_sems, level, shards=None):
    x, y, c = _place()
    me, sib = (x, y, c), (x, y, 1 - c)
    chips = [(1 - x, y), (x, 1 - y), (1 - x, 1 - y)]

    def copy(w, k, block, to):
        px, py, pc = block
        rows = bufs[w].at[4 * px + 2 * py + pc]
        src = shards[w] if shards is not None and block is me else rows
        return pltpu.make_async_remote_copy(src_ref=src, dst_ref=rows, send_sem=send_sems.at[k], recv_sem=recv_sems.at[k],
                                            device_id=to, device_id_type=MESH_ID)

    n = len(bufs)
    own = []
    if level == 1:
        own = [pltpu.make_async_copy(bufs[w].at[4 * x + 2 * y + c] if shards is None else shards[w],
                                     bufs[w].at[4 * x + 2 * y + c], send_sems.at[4 * n + w]) for w in range(n)]
    out, arriving = [], []
    for w in range(len(bufs)):
        if level == 1:
            out.append(copy(w, 4 * w, me, sib))
            arriving.append(copy(w, 4 * w, sib, me))
        for j, chip in enumerate(chips):
            if level == 1:
                out.append(copy(w, 4 * w + 1 + j, me, (*chip, c)))
                arriving.append(copy(w, 4 * w + 1 + j, (*chip, c), me))
            else:
                out.append(copy(w, 3 * w + j, (*chip, c), sib))
                arriving.append(copy(w, 3 * w + j, (*chip, 1 - c), me))
    return out, arriving, own


def _split_call(body, name, bufs, sems_in, after, n_sems_out, token, extra=()):
    n = len(bufs)
    out_shape = [pltpu.SemaphoreType.DMA((n_sems_out,))] * (2 if n_sems_out else 0)
    out_shape += [pltpu.HBM(a.shape, a.dtype) for a in bufs]
    out_specs = [SEM] * (2 if n_sems_out else 0) + [HBM] * n
    if token:
        out_shape.append(jax.ShapeDtypeStruct((8, LANES), F32))
        out_specs.append(pl.BlockSpec(memory_space=pltpu.VMEM))
    first = 2 if n_sems_out else 0
    return pl.pallas_call(
        body, name=name, out_shape=tuple(out_shape),
        in_specs=[HBM] * (n + len(extra)) + [SEM] * len(sems_in) + [ANY], out_specs=tuple(out_specs),
        input_output_aliases={i: first + i for i in range(n)},
        compiler_params=pltpu.CompilerParams(has_side_effects=SIDE_EFFECT),
    )(*bufs, *extra, *sems_in, after)


def _gather_start(shards, after, name):
    n = len(shards)
    hbm = lambda a: pltpu.with_memory_space_constraint(a, pltpu.HBM)
    bufs = [hbm(lax.empty((N_DEV,) + s.shape, s.dtype)) for s in shards]

    def body(*refs):
        out, _, own = _gather_level(refs[:n], refs[2 * n + 1], refs[2 * n + 2], 1, shards=refs[n:2 * n])
        for cp in own + out:
            cp.start()
        refs[-1][...] = jnp.zeros_like(refs[-1])

    outs = _split_call(body, name, bufs + [hbm(s) for s in shards], [], after, 5 * n, True)
    return outs[0], outs[1], list(outs[2:2 + 2 * n]), outs[-1]


def _gather_pass(send1, recv1, bufs_and_shards, after, name):
    n = len(bufs_and_shards) // 2
    bufs = bufs_and_shards

    def body(*refs):
        refs = refs[:n] + refs[2 * n:]
        out1, in1, own = _gather_level(refs[:n], refs[n], refs[n + 1], 1)
        out2, _, _ = _gather_level(refs[:n], refs[n + 3], refs[n + 4], 2)
        for cp in in1:
            cp.wait_recv()
        for cp in out2:
            cp.start()
        for cp in out1:
            cp.wait_send()
        for cp in own:
            cp.wait()
        refs[-1][...] = jnp.zeros_like(refs[-1])

    outs = _split_call(body, name, bufs, [send1, recv1], after, 3 * n, True)
    return outs[0], outs[1], list(outs[2:2 + n]), outs[-1]


def _gather_wait(send2, recv2, bufs, after, name):
    n = len(bufs)

    def body(*refs):
        out2, in2, _ = _gather_level(refs[:n], refs[n], refs[n + 1], 2)
        for cp in in2:
            cp.wait_recv()
        for cp in out2:
            cp.wait_send()

    return list(_split_call(body, name, bufs, [send2, recv2], after, 0, False))


ALL_PEERS = tuple(range(1, N_DEV))
SAME_CORE_CHIPS = (2, 4, 6)


def _scatter_copies(parts, lands, send_sems, recv_sems, relations):
    x, y, c = _place()
    ns = len(relations)
    cps = []
    for w, (part, land) in enumerate(zip(parts, lands)):
        for i, k in enumerate(relations):
            px, py, pc = x ^ ((k >> 2) & 1), y ^ ((k >> 1) & 1), c ^ (k & 1)
            block = 4 * px + 2 * py + pc if part.shape[0] == N_DEV else 2 * px + py
            cps.append(pltpu.make_async_remote_copy(
                src_ref=part.at[block], dst_ref=land.at[i],
                send_sem=send_sems.at[ns * w + i], recv_sem=recv_sems.at[ns * w + i],
                device_id=(px, py, pc), device_id_type=MESH_ID))
    return cps


def _scatter_start(parts, after, name, relations=ALL_PEERS):
    n = len(parts)
    ns = len(relations)

    def body(*refs):
        ins, lands = refs[:n], refs[n:2 * n]
        send_sems, recv_sems = refs[2 * n + 1], refs[2 * n + 2]
        token = refs[-1]
        for cp in _scatter_copies(ins, lands, send_sems, recv_sems, relations):
            cp.start()
        token[...] = jnp.zeros_like(token)

    land_shapes = [(ns,) + p.shape[1:] for p in parts]
    in_hbm = [pltpu.with_memory_space_constraint(p, pltpu.HBM) for p in parts]
    in_hbm += [pltpu.with_memory_space_constraint(lax.empty(s, p.dtype), pltpu.HBM) for s, p in zip(land_shapes, parts)]
    outs = pl.pallas_call(
        body, name=name,
        out_shape=(pltpu.SemaphoreType.DMA((ns * n,)), pltpu.SemaphoreType.DMA((ns * n,)),
                   *[pltpu.HBM(p.shape, p.dtype) for p in parts],
                   *[pltpu.HBM(s, p.dtype) for s, p in zip(land_shapes, parts)],
                   jax.ShapeDtypeStruct((8, LANES), F32)),
        in_specs=[HBM] * (2 * n) + [ANY],
        out_specs=(SEM, SEM, *[HBM] * (2 * n), pl.BlockSpec(memory_space=pltpu.VMEM)),
        input_output_aliases={i: 2 + i for i in range(2 * n)},
        compiler_params=pltpu.CompilerParams(has_side_effects=SIDE_EFFECT),
    )(*in_hbm, after)
    return outs[0], outs[1], list(outs[2:2 + n]), list(outs[2 + n:2 + 2 * n]), outs[-1]


def _scatter_wait(send_sems, recv_sems, parts, lands, after, name, relations=ALL_PEERS):
    n = len(parts)

    def body(*refs):
        ins, lnd = refs[:n], refs[n:2 * n]
        for cp in _scatter_copies(ins, lnd, refs[2 * n], refs[2 * n + 1], relations):
            cp.wait_send()
            cp.wait_recv()

    outs = pl.pallas_call(
        body, name=name,
        out_shape=tuple(pltpu.HBM(a.shape, a.dtype) for a in parts + lands),
        in_specs=[HBM] * (2 * n) + [SEM, SEM, ANY],
        out_specs=tuple([HBM] * (2 * n)),
        input_output_aliases={i: i for i in range(2 * n)},
        compiler_params=pltpu.CompilerParams(has_side_effects=SIDE_EFFECT),
    )(*parts, *lands, send_sems, recv_sems, after)
    return list(outs[n:])


def _pair_exchange(parts, name):
    n = len(parts)

    def body(*refs):
        ins, outs = refs[:n], refs[n:2 * n]
        send_sems, recv_sems = refs[2 * n:]
        x, y, c = _place()
        cps = [pltpu.make_async_remote_copy(
            src_ref=ins[w].at[:, pl.ds(1 - c, 1)], dst_ref=outs[w], send_sem=send_sems.at[w], recv_sem=recv_sems.at[w],
            device_id=(x, y, 1 - c), device_id_type=MESH_ID) for w in range(n)]
        for cp in cps:
            cp.start()
        for cp in cps:
            cp.wait()

    return pl.pallas_call(
        body, name=name,
        out_shape=[jax.ShapeDtypeStruct((4, 1) + p.shape[2:], p.dtype) for p in parts],
        in_specs=[ANY] * n, out_specs=[ANY] * n,
        scratch_shapes=[pltpu.SemaphoreType.DMA((n,)), pltpu.SemaphoreType.DMA((n,))],
    )(*parts)


def _pair_sum(g8, r1, me, name):
    _, r, c = g8.shape
    tr = max(q for q in range(16, r + 1, 16) if r % q == 0 and q * c <= ADAM_TILE_ELEMS)

    def body(me_ref, g_ref, r_ref, o_ref):
        o_ref[...] = (g_ref[...] + r_ref[...].astype(F32)).astype(BF16)

    chip = lambda k, s: s[1] ^ (k + 1)
    return pl.pallas_call(
        body, name=name,
        out_shape=jax.ShapeDtypeStruct((4, r, c), BF16),
        grid_spec=pltpu.PrefetchScalarGridSpec(
            num_scalar_prefetch=1, grid=(3, r // tr),
            in_specs=[pl.BlockSpec((None, None, tr, c), lambda k, i, s: (chip(k, s), s[0] % 2, i, 0)),
                      pl.BlockSpec((None, tr, c), lambda k, i, s: (chip(k, s), i, 0))],
            out_specs=pl.BlockSpec((None, tr, c), lambda k, i, s: (chip(k, s), i, 0))),
        compiler_params=_params(("arbitrary", "arbitrary")),
    )(me, g8.reshape((4, 2) + g8.shape[1:]), r1)


def _adam(w, g, m, v):
    m2 = ADAM_B1 * m + (1.0 - ADAM_B1) * g
    v2 = ADAM_B2 * v + (1.0 - ADAM_B2) * (g * g)
    m_hat = m2 / (1.0 - ADAM_B1 ** ADAM_STEP)
    v_hat = v2 / (1.0 - ADAM_B2 ** ADAM_STEP)
    delta = -ADAM_LR * (m_hat / (jnp.sqrt(v_hat) + ADAM_EPS) + ADAM_WD * w)
    return delta, m2, v2


def _small_allreduce_adam(part, w, m, v, name):
    rows = part.shape[0]

    def body(p_ref, w_ref, m_ref, v_ref, g_ref, d_ref, mo_ref, vo_ref, buf, send_sems, recv_sems):
        x, y, c = _place()
        buf[0] = p_ref[...]
        cps = []
        for k in range(1, N_DEV):
            kx, ky, kc = (k >> 2) & 1, (k >> 1) & 1, k & 1
            peer = (x ^ kx, y ^ ky, c ^ kc)
            cps.append(pltpu.make_async_remote_copy(
                src_ref=p_ref, dst_ref=buf.at[k], send_sem=send_sems.at[k - 1], recv_sem=recv_sems.at[k - 1],
                device_id=peer, device_id_type=MESH_ID))
        for cp in cps:
            cp.start()
        for cp in cps:
            cp.wait()
        me = 4 * x + 2 * y + c
        total = buf[me]
        for d in range(1, N_DEV):
            total = total + buf[d ^ me]
        g_ref[...] = total
        delta, m2, v2 = _adam(w_ref[...], total, m_ref[...], v_ref[...])
        d_ref[...] = delta
        mo_ref[...] = m2
        vo_ref[...] = v2

    vm = pl.BlockSpec(memory_space=pltpu.VMEM)
    return pl.pallas_call(
        body, name=name,
        out_shape=[jax.ShapeDtypeStruct(part.shape, F32)] * 4,
        in_specs=[vm] * 4, out_specs=[vm] * 4,
        scratch_shapes=[pltpu.VMEM((N_DEV, rows, LANES), F32),
                        pltpu.SemaphoreType.DMA((N_DEV - 1,)), pltpu.SemaphoreType.DMA((N_DEV - 1,))],
    )(part, w, m, v)


def _final_adam(g8, land, w, m, v, me, dep, name, pair=None):
    _, r, c = g8.shape
    tr = max(q for q in range(16, r + 1, 16) if r % q == 0 and q * c <= ADAM_TILE_ELEMS)
    nland = land.shape[0]

    def body(me_ref, g_ref, land_ref, *rest):
        pair_ref = rest[0] if pair is not None else None
        w_ref, m_ref, v_ref, _, go_ref, d_ref, mo_ref, vo_ref = rest[-8:]
        g = g_ref[...]
        if pair_ref is not None:
            g = g + pair_ref[...].astype(F32)
        for k in range(nland):
            g = g + land_ref[k].astype(F32)
        go_ref[...] = g
        delta, m2, v2 = _adam(w_ref[...], g, m_ref[...], v_ref[...])
        d_ref[...] = delta
        mo_ref[...] = m2
        vo_ref[...] = v2

    plain = pl.BlockSpec((tr, c), lambda i, s: (i, 0))
    return pl.pallas_call(
        body, name=name,
        out_shape=[jax.ShapeDtypeStruct((r, c), F32)] * 4,
        grid_spec=pltpu.PrefetchScalarGridSpec(
            num_scalar_prefetch=1, grid=(r // tr,),
            in_specs=[pl.BlockSpec((None, tr, c), lambda i, s: (s[0], i, 0)),
                      pl.BlockSpec((nland, tr, c), lambda i, s: (0, i, 0))]
            + ([] if pair is None else [pl.BlockSpec((None, tr, c), lambda i, s: (s[1], i, 0))])
            + [plain, plain, plain, ANY],
            out_specs=[plain] * 4),
        compiler_params=_params(("arbitrary",)),
    )(*((me, g8, land) + (() if pair is None else (pair,)) + (w, m, v, dep)))


def _rms(x, gain):
    r = lax.rsqrt(jnp.mean(x * x, axis=-1, keepdims=True) + EPS)
    xh = x * r
    return xh * gain, xh, r


def _rms_bwd(xh, r, gain, dy):
    gdy = gain * dy
    dx = r * (gdy - xh * jnp.mean(xh * gdy, axis=-1, keepdims=True))
    return dx, jnp.sum(dy * xh, axis=0, keepdims=True)


def _load_weights(pairs, sems):
    cps = [pltpu.make_async_copy(src, dst, sems.at[i]) for i, (src, dst) in enumerate(pairs)]
    for cp in cps:
        cp.start()
    for cp in cps:
        cp.wait()


def _ffn_fwd(h, gain, wgu, wd, name):
    t, d = h.shape
    nb, nf, _ = wgu.shape
    nh = nb // 2
    tm = _row_tile(t, 512)

    def body(h_ref, g_ref, wgu_hbm, wd_hbm, out_ref, gu_ref, wgu_v, wd_v, sems):
        @pl.when(pl.program_id(0) == 0)
        def _():
            _load_weights([(wgu_hbm, wgu_v), (wd_hbm, wd_v)], sems)

        x = h_ref[...]
        n, _, _ = _rms(x, g_ref[...])
        nbf = n.astype(BF16)
        acc = jnp.zeros((tm, d), F32)
        for j in range(nh):
            g = _dot_nt(nbf, wgu_v[j])
            u = _dot_nt(nbf, wgu_v[j + nh])
            gu_ref[j] = g.astype(BF16)
            gu_ref[j + nh] = u.astype(BF16)
            a = (g * jax.nn.sigmoid(g)) * u
            acc = acc + _dot(a.astype(BF16), wd_v[j])
        out_ref[...] = x + 0.5 * acc

    return pl.pallas_call(
        body, name=name, grid=(t // tm,),
        out_shape=[jax.ShapeDtypeStruct((t, d), F32), jax.ShapeDtypeStruct((nb, t, nf), BF16)],
        in_specs=[pl.BlockSpec((tm, d), lambda i: (i, 0)), pl.BlockSpec((1, d), lambda i: (0, 0)), ANY, ANY],
        out_specs=[pl.BlockSpec((tm, d), lambda i: (i, 0)), pl.BlockSpec((nb, tm, nf), lambda i: (0, i, 0))],
        scratch_shapes=[pltpu.VMEM(wgu.shape, BF16), pltpu.VMEM(wd.shape, BF16), pltpu.SemaphoreType.DMA((2,))],
        compiler_params=_params(("arbitrary",)),
    )(h, gain, wgu, wd)


def _ffn_up(h, gain, wgu, name):
    t, d = h.shape
    nb, nf, _ = wgu.shape
    nh = nb // 2
    tm = _row_tile(t, 512)

    def body(h_ref, g_ref, wgu_hbm, gu_ref, a_ref, wgu_v, sems):
        @pl.when(pl.program_id(0) == 0)
        def _():
            _load_weights([(wgu_hbm, wgu_v)], sems)

        n, _, _ = _rms(h_ref[...], g_ref[...])
        nbf = n.astype(BF16)
        for j in range(nh):
            g = _dot_nt(nbf, wgu_v[j])
            u = _dot_nt(nbf, wgu_v[j + nh])
            gu_ref[j] = g.astype(BF16)
            gu_ref[j + nh] = u.astype(BF16)
            a_ref[j] = ((g * jax.nn.sigmoid(g)) * u).astype(BF16)

    return pl.pallas_call(
        body, name=name, grid=(t // tm,),
        out_shape=[jax.ShapeDtypeStruct((nb, t, nf), BF16), jax.ShapeDtypeStruct((nh, t, nf), BF16)],
        in_specs=[pl.BlockSpec((tm, d), lambda i: (i, 0)), pl.BlockSpec((1, d), lambda i: (0, 0)), ANY],
        out_specs=[pl.BlockSpec((nb, tm, nf), lambda i: (0, i, 0)), pl.BlockSpec((nh, tm, nf), lambda i: (0, i, 0))],
        scratch_shapes=[pltpu.VMEM(wgu.shape, BF16), pltpu.SemaphoreType.DMA((1,))],
        compiler_params=_params(("arbitrary",)),
    )(h, gain, wgu)


def _ffn_down(h, a, wd, name):
    t, d = h.shape
    nh, nf, _ = wd.shape
    tm = _row_tile(t, 512)

    def body(h_ref, a_ref, wd_ref, out_ref):
        acc = jnp.zeros((tm, d), F32)
        for j in range(nh):
            acc = acc + _dot(a_ref[j], wd_ref[j])
        out_ref[...] = h_ref[...] + 0.5 * acc

    row = pl.BlockSpec((tm, d), lambda i: (i, 0))
    return pl.pallas_call(
        body, name=name, grid=(t // tm,),
        out_shape=jax.ShapeDtypeStruct((t, d), F32),
        in_specs=[row, pl.BlockSpec((nh, tm, nf), lambda i: (0, i, 0)), pl.BlockSpec(wd.shape, lambda i: (0, 0, 0))],
        out_specs=row,
        compiler_params=_params(("arbitrary",)),
    )(h, a, wd)


def _ffn_bwd(dh, h, gain, gu, wgu, wd, name):
    t, d = h.shape
    nb, nf, _ = wgu.shape
    nh = nb // 2
    tm = _row_tile(t, 256)

    def body(dh_ref, h_ref, g_ref, gu_ref, wgu_hbm, wd_hbm, dhp_ref, dgu_ref, a_ref, n_ref, dgain_ref,
             wgu_v, wd_v, sems):
        @pl.when(pl.program_id(0) == 0)
        def _():
            _load_weights([(wgu_hbm, wgu_v), (wd_hbm, wd_v)], sems)
            dgain_ref[...] = jnp.zeros_like(dgain_ref)

        x = h_ref[...]
        gain_v = g_ref[...]
        n, xh, r = _rms(x, gain_v)
        n_ref[...] = n.astype(BF16)
        dh_v = dh_ref[...]
        dfb = (0.5 * dh_v).astype(BF16)
        dn = jnp.zeros((tm, d), F32)
        for j in range(nh):
            da = _dot_nt(dfb, wd_v[j])
            g = gu_ref[j].astype(F32)
            u = gu_ref[j + nh].astype(F32)
            sg = jax.nn.sigmoid(g)
            si = g * sg
            dg = (da * u * (sg * (1.0 + g * (1.0 - sg)))).astype(BF16)
            du = (da * si).astype(BF16)
            a_ref[j] = (si * u).astype(BF16)
            dgu_ref[j] = dg
            dgu_ref[j + nh] = du
            dn = dn + _dot(dg, wgu_v[j]) + _dot(du, wgu_v[j + nh])
        dx, dgain = _rms_bwd(xh, r, gain_v, dn)
        dhp_ref[...] = dh_v + dx
        dgain_ref[...] += dgain

    row = pl.BlockSpec((tm, d), lambda i: (i, 0))
    vec = pl.BlockSpec((1, d), lambda i: (0, 0))
    return pl.pallas_call(
        body, name=name, grid=(t // tm,),
        out_shape=[jax.ShapeDtypeStruct((t, d), F32), jax.ShapeDtypeStruct((nb, t, nf), BF16),
                   jax.ShapeDtypeStruct((nh, t, nf), BF16), jax.ShapeDtypeStruct((t, d), BF16),
                   jax.ShapeDtypeStruct((1, d), F32)],
        in_specs=[row, row, vec, pl.BlockSpec((nb, tm, nf), lambda i: (0, i, 0)), ANY, ANY],
        out_specs=[row, pl.BlockSpec((nb, tm, nf), lambda i: (0, i, 0)),
                   pl.BlockSpec((nh, tm, nf), lambda i: (0, i, 0)), row, vec],
        scratch_shapes=[pltpu.VMEM(wgu.shape, BF16), pltpu.VMEM(wd.shape, BF16), pltpu.SemaphoreType.DMA((2,))],
        compiler_params=_params(("arbitrary",)),
    )(dh, h, gain, gu, wgu, wd)


def _dw(xa, dy, nb, n, name, scale=1.0, dep=None):
    t, k = xa.shape[-2:]
    wide = xa.ndim == 2
    tt = _row_tile(t, 1024)
    steps = t // tt
    x_spec = pl.BlockSpec((tt, k), lambda i: (i, 0)) if wide else pl.BlockSpec((nb, tt, k), lambda i: (0, i, 0))
    dy_spec = pl.BlockSpec((tt, dy.shape[1]), lambda i: (i, 0))
    acc_shape = (k, nb * n) if wide else (nb, k, n)
    stage_shape = (k, nb * n) if wide else (k, n)

    def body(x_ref, dy_ref, *rest):
        o_hbm, ob_hbm, acc, stage, sems = rest[-5:]

        @pl.when(pl.program_id(0) == 0)
        def _():
            acc[...] = jnp.zeros_like(acc)

        dyb = dy_ref[...].astype(BF16)
        if wide:
            acc[...] += _dot(x_ref[...].astype(BF16).T, dyb)
        else:
            for j in range(nb):
                acc[j] += _dot_tn(x_ref[j].astype(BF16), dyb)

        @pl.when(pl.program_id(0) == steps - 1)
        def _():
            if scale != 1.0:
                acc[...] = acc[...] * scale
            if wide:
                cps = [pltpu.make_async_copy(acc.at[:, pl.ds(j * n, n)] if nb > 1 else acc, o_hbm.at[j], sems.at[j])
                       for j in range(nb)]
            else:
                cps = [pltpu.make_async_copy(acc, o_hbm, sems.at[0])]
            for cp in cps:
                cp.start()
            if wide:
                stage[...] = acc[...].astype(BF16)
                bcs = [pltpu.make_async_copy(stage.at[:, pl.ds(j * n, n)] if nb > 1 else stage, ob_hbm.at[j],
                                             sems.at[nb + j]) for j in range(nb)]
                for cp in bcs:
                    cp.start()
                for cp in bcs:
                    cp.wait()
            else:
                for j in range(nb):
                    stage[...] = acc[j].astype(BF16)
                    cp = pltpu.make_async_copy(stage, ob_hbm.at[j], sems.at[nb])
                    cp.start()
                    cp.wait()
            for cp in cps:
                cp.wait()

    return pl.pallas_call(
        body, name=name, grid=(steps,),
        out_shape=[jax.ShapeDtypeStruct((nb, k, n), F32), jax.ShapeDtypeStruct((nb, k, n), BF16)],
        in_specs=[x_spec, dy_spec] + ([] if dep is None else [ANY]),
        out_specs=[ANY, ANY],
        scratch_shapes=[pltpu.VMEM(acc_shape, F32), pltpu.VMEM(stage_shape, BF16),
                        pltpu.SemaphoreType.DMA((2 * nb,))],
        compiler_params=_params(("arbitrary",), DW_VMEM_LIMIT),
    )(*((xa, dy) if dep is None else (xa, dy, dep)))


def _proj_fwd(h, gain, win, wgate, name):
    t, d = h.shape
    tm = _row_tile(t, 512)
    nq, ng = win.shape[0], wgate.shape[1]

    def body(h_ref, g_ref, win_ref, wg_ref, un_ref, qkv_ref, gate_ref):
        n, _, _ = _rms(h_ref[...], g_ref[...])
        nbf = n.astype(BF16)
        un_ref[...] = nbf
        qkv_ref[...] = _dot_nt(nbf, win_ref[...])
        gate_ref[...] = jax.nn.sigmoid(_dot(nbf, wg_ref[...])).astype(BF16)

    full = lambda a: pl.BlockSpec(a.shape, lambda i: (0,) * a.ndim)
    return pl.pallas_call(
        body, name=name, grid=(t // tm,),
        out_shape=[jax.ShapeDtypeStruct((t, d), BF16), jax.ShapeDtypeStruct((t, nq), F32),
                   jax.ShapeDtypeStruct((t, ng), BF16)],
        in_specs=[pl.BlockSpec((tm, d), lambda i: (i, 0)), full(gain), full(win), full(wgate)],
        out_specs=[pl.BlockSpec((tm, d), lambda i: (i, 0)), pl.BlockSpec((tm, nq), lambda i: (i, 0)),
                   pl.BlockSpec((tm, ng), lambda i: (i, 0))],
        compiler_params=_params(("arbitrary",)),
    )(h, gain, win, wgate)


def _proj_bwd(dh, h, gain, dzg, dqkv_parts, win, wgate, name):
    t, d = h.shape
    tm = _row_tile(t, 512)
    ng = wgate.shape[1]
    np_ = len(dqkv_parts)
    widths = [a.shape[1] for a in dqkv_parts]

    def body(dh_ref, h_ref, g_ref, dzg_ref, *rest):
        part_refs, (win_ref, wg_ref, dhp_ref, dgain_ref) = rest[:np_], rest[np_:]

        @pl.when(pl.program_id(0) == 0)
        def _():
            dgain_ref[...] = jnp.zeros_like(dgain_ref)

        gain_v = g_ref[...]
        _, xh, r = _rms(h_ref[...], gain_v)
        dun = _dot_nt(dzg_ref[...], wg_ref[...])
        off = 0
        for ref, wd in zip(part_refs, widths):
            dun = dun + _dot(ref[...].astype(BF16), win_ref[off:off + wd, :])
            off += wd
        dx, dgain = _rms_bwd(xh, r, gain_v, dun)
        dhp_ref[...] = dh_ref[...] + dx
        dgain_ref[...] += dgain

    full = lambda a: pl.BlockSpec(a.shape, lambda i: (0,) * a.ndim)
    row = pl.BlockSpec((tm, d), lambda i: (i, 0))
    return pl.pallas_call(
        body, name=name, grid=(t // tm,),
        out_shape=[jax.ShapeDtypeStruct((t, d), F32), jax.ShapeDtypeStruct((1, d), F32)],
        in_specs=[row, row, full(gain), pl.BlockSpec((tm, ng), lambda i: (i, 0))]
        + [pl.BlockSpec((tm, wd), lambda i: (i, 0)) for wd in widths] + [full(win), full(wgate)],
        out_specs=[row, pl.BlockSpec((1, d), lambda i: (0, 0))],
        compiler_params=_params(("arbitrary",)),
    )(dh, h, gain, dzg, *dqkv_parts, win, wgate)


def _dw_rows(parts, dy, name):
    t, n = dy.shape
    widths = [a.shape[1] for a in parts]
    k = sum(widths)
    tt = _row_tile(t, 1024)
    steps = t // tt
    np_ = len(parts)

    def body(*refs):
        part_refs, dy_ref = refs[:np_], refs[np_]
        o_hbm, ob_hbm, acc, stage, sems = refs[np_ + 1:]

        @pl.when(pl.program_id(0) == 0)
        def _():
            acc[...] = jnp.zeros_like(acc)

        dyb = dy_ref[...].astype(BF16)
        off = 0
        for ref, wd in zip(part_refs, widths):
            acc[off:off + wd, :] += _dot(ref[...].astype(BF16).T, dyb)
            off += wd

        @pl.when(pl.program_id(0) == steps - 1)
        def _():
            stage[...] = acc[...].astype(BF16)
            cps = [pltpu.make_async_copy(acc, o_hbm.at[0], sems.at[0]),
                   pltpu.make_async_copy(stage, ob_hbm.at[0], sems.at[1])]
            for cp in cps:
                cp.start()
            for cp in cps:
                cp.wait()

    return pl.pallas_call(
        body, name=name, grid=(steps,),
        out_shape=[jax.ShapeDtypeStruct((1, k, n), F32), jax.ShapeDtypeStruct((1, k, n), BF16)],
        in_specs=[pl.BlockSpec((tt, wd), lambda i: (i, 0)) for wd in widths] + [pl.BlockSpec((tt, n), lambda i: (i, 0))],
        out_specs=[ANY, ANY],
        scratch_shapes=[pltpu.VMEM((k, n), F32), pltpu.VMEM((k, n), BF16), pltpu.SemaphoreType.DMA((2,))],
        compiler_params=_params(("arbitrary",)),
    )(*parts, dy)


def _merge_fwd(h, ya, yb, gate, wpa, wpb, wout, name):
    t, d = h.shape
    tm = _row_tile(t, 512)

    def body(h_ref, ya_ref, yb_ref, ga_ref, gb_ref, wpa_ref, wpb_ref, wout_ref, out_ref, mg_ref, pa_ref, pb_ref):
        pa = _dot(ya_ref[...].astype(BF16), wpa_ref[...])
        pb = _dot(yb_ref[...].astype(BF16), wpb_ref[...])
        merged = (ga_ref[...].astype(F32) * pa + gb_ref[...].astype(F32) * pb).astype(BF16)
        pa_ref[...] = pa.astype(BF16)
        pb_ref[...] = pb.astype(BF16)
        mg_ref[...] = merged
        out_ref[...] = h_ref[...] + _dot(merged, wout_ref[...])

    full = lambda a: pl.BlockSpec(a.shape, lambda i: (0,) * a.ndim)
    row = pl.BlockSpec((tm, d), lambda i: (i, 0))
    yrow = pl.BlockSpec((tm, ya.shape[1]), lambda i: (i, 0))
    return pl.pallas_call(
        body, name=name, grid=(t // tm,),
        out_shape=[jax.ShapeDtypeStruct((t, d), F32)] + [jax.ShapeDtypeStruct((t, d), BF16)] * 3,
        in_specs=[row, yrow, yrow, pl.BlockSpec((tm, d), lambda i: (i, 0)), pl.BlockSpec((tm, d), lambda i: (i, 1)),
                  full(wpa), full(wpb), full(wout)],
        out_specs=[row] * 4,
        compiler_params=_params(("arbitrary",)),
    )(h, ya, yb, gate, gate, wpa, wpb, wout)


def _merge_bwd(dh, pa, pb, gate, wpa, wpb, wout, name):
    t, d = dh.shape
    tm = _row_tile(t, 512)
    wy = wpa.shape[0]

    def body(dh_ref, pa_ref, pb_ref, ga_ref, gb_ref, wpa_ref, wpb_ref, wout_ref,
             dpa_ref, dpb_ref, dzg_ref, dya_ref, dyb_ref):
        dm = _dot_nt(dh_ref[...].astype(BF16), wout_ref[...])
        ga, gb = ga_ref[...].astype(F32), gb_ref[...].astype(F32)
        dpa = (dm * ga).astype(BF16)
        dpb = (dm * gb).astype(BF16)
        dpa_ref[...] = dpa
        dpb_ref[...] = dpb
        dzg_ref[:, :d] = (dm * pa_ref[...].astype(F32) * ga * (1.0 - ga)).astype(BF16)
        dzg_ref[:, d:] = (dm * pb_ref[...].astype(F32) * gb * (1.0 - gb)).astype(BF16)
        dya_ref[...] = _dot_nt(dpa, wpa_ref[...])
        dyb_ref[...] = _dot_nt(dpb, wpb_ref[...])

    full = lambda a: pl.BlockSpec(a.shape, lambda i: (0,) * a.ndim)
    row = pl.BlockSpec((tm, d), lambda i: (i, 0))
    yrow = pl.BlockSpec((tm, wy), lambda i: (i, 0))
    return pl.pallas_call(
        body, name=name, grid=(t // tm,),
        out_shape=[jax.ShapeDtypeStruct((t, d), BF16), jax.ShapeDtypeStruct((t, d), BF16),
                   jax.ShapeDtypeStruct((t, 2 * d), BF16), jax.ShapeDtypeStruct((t, wy), F32),
                   jax.ShapeDtypeStruct((t, wy), F32)],
        in_specs=[row, row, row, pl.BlockSpec((tm, d), lambda i: (i, 0)), pl.BlockSpec((tm, d), lambda i: (i, 1)),
                  full(wpa), full(wpb), full(wout)],
        out_specs=[row, row, pl.BlockSpec((tm, 2 * d), lambda i: (i, 0)), yrow, yrow],
        compiler_params=_params(("arbitrary",)),
    )(dh, pa, pb, gate, gate, wpa, wpb, wout)


def _ple_loss(h, gain, p, target, wpg, wpe, name):
    t, d = h.shape
    tm = _row_tile(t, 512)
    pd = p.shape[1]

    def body(h_ref, g_ref, p_ref, t_ref, wpg_ref, wpe_ref, dh_ref, dz_ref, dpp_ref, n_ref, dgain_ref, loss_ref):
        @pl.when(pl.program_id(0) == 0)
        def _():
            dgain_ref[...] = jnp.zeros_like(dgain_ref)
            loss_ref[...] = jnp.zeros_like(loss_ref)

        x = h_ref[...]
        gain_v = g_ref[...]
        n, xh, r = _rms(x, gain_v)
        nbf = n.astype(BF16)
        n_ref[...] = nbf
        pg = jax.nn.sigmoid(_dot(nbf, wpg_ref[...]))
        pp = _dot(p_ref[...].astype(BF16), wpe_ref[...])
        err = (x + pg * pp) - t_ref[...]
        loss_ref[...] += 0.5 * jnp.sum(jnp.mean(err * err, axis=-1, keepdims=True))
        dy = err * (1.0 / d)
        dpp_ref[...] = (dy * pg).astype(BF16)
        dz = (dy * pp * pg * (1.0 - pg)).astype(BF16)
        dz_ref[...] = dz
        dn = _dot_nt(dz, wpg_ref[...])
        dx, dgain = _rms_bwd(xh, r, gain_v, dn)
        dh_ref[...] = dy + dx
        dgain_ref[...] += dgain

    full = lambda a: pl.BlockSpec(a.shape, lambda i: (0,) * a.ndim)
    row = pl.BlockSpec((tm, d), lambda i: (i, 0))
    return pl.pallas_call(
        body, name=name, grid=(t // tm,),
        out_shape=[jax.ShapeDtypeStruct((t, d), F32), jax.ShapeDtypeStruct((t, d), BF16),
                   jax.ShapeDtypeStruct((t, d), BF16), jax.ShapeDtypeStruct((t, d), BF16),
                   jax.ShapeDtypeStruct((1, d), F32), jax.ShapeDtypeStruct((8, LANES), F32)],
        in_specs=[row, full(gain), pl.BlockSpec((tm, pd), lambda i: (i, 0)), row, full(wpg), full(wpe)],
        out_specs=[row, row, row, row, pl.BlockSpec((1, d), lambda i: (0, 0)),
                   pl.BlockSpec((8, LANES), lambda i: (0, 0))],
        compiler_params=_params(("arbitrary",)),
    )(h, gain, p, target, wpg, wpe)


def _head_masks():
    lane = lax.broadcasted_iota(jnp.int32, (1, LANES), 1)
    m0 = (lane < HEAD_DIM).astype(F32)
    return m0, 1.0 - m0


def _head_mean(v, m0, m1):
    del m0, m1
    width = v.shape[-1]
    shift = HEAD_DIM.bit_length() - 1
    r = jnp.right_shift(lax.broadcasted_iota(jnp.int32, (width, width), 0), shift)
    c = jnp.right_shift(lax.broadcasted_iota(jnp.int32, (width, width), 1), shift)
    same_head = (r == c).astype(BF16)
    return _dot(v.astype(BF16), same_head) * (1.0 / HEAD_DIM)


def _head_norm(x, gain, m0, m1):
    r = lax.rsqrt(_head_mean(x * x, m0, m1) + EPS)
    xh = x * r
    return xh * gain, xh, r


def _head_norm_bwd(xh, r, gain, dy, m0, m1):
    gdy = gain * dy
    dx = r * (gdy - xh * _head_mean(xh * gdy, m0, m1))
    return dx, jnp.sum(dy * xh, axis=0, keepdims=True)


GROUP = 4
QW = GROUP * HEAD_DIM
STACK = GROUP * QTILE


def _kv_width(mode):
    return QW if mode == "A" else LANES


def _q_scratch_shape(mode, s_len):
    return (s_len, QW) if mode == "A" else (GROUP * s_len, LANES)


def _group_masks(dtype=F32):
    lane = lax.broadcasted_iota(jnp.int32, (1, QW), 1)
    return [((lane >= h * HEAD_DIM) & (lane < (h + 1) * HEAD_DIM)).astype(dtype) for h in range(GROUP)]


def _stack_heads(first_kv, x, m0, m1):
    out = []
    for half in range(GROUP // 2):
        xh = x[:, half * LANES:(half + 1) * LANES]
        a0, a1 = xh * m0, xh * m1
        r0, r1 = pltpu.roll(a0, HEAD_DIM, 1), pltpu.roll(a1, HEAD_DIM, 1)
        out += [jnp.where(first_kv, a0, r0), jnp.where(first_kv, r1, a1)]
    return out


def _unstack_heads(mode, first_kv, ts, m0, m1):
    if mode == "A":
        masks = _group_masks()
        return sum(t * mk for t, mk in zip(ts, masks))
    halves = []
    for half in range(GROUP // 2):
        t0 = jnp.where(first_kv, ts[2 * half], pltpu.roll(ts[2 * half], HEAD_DIM, 1))
        t1 = jnp.where(first_kv, pltpu.roll(ts[2 * half + 1], HEAD_DIM, 1), ts[2 * half + 1])
        halves.append(t0 * m0 + t1 * m1)
    return jnp.concatenate(halves, axis=1)


def _store_stacked(dst, i, heads):
    for half in range(2):
        rows = slice(half * QTILE, (half + 1) * QTILE)
        for h, x in enumerate(heads):
            dst[pl.ds((2 * i + half) * STACK + h * QTILE, QTILE), :] = x[rows].astype(dst.dtype)


def _load_stacked(mode, ref, m):
    if mode == "B":
        return ref[pl.ds(pl.multiple_of(m * STACK, STACK), STACK), :]
    x = ref[pl.ds(pl.multiple_of(m * QTILE, QTILE), QTILE), :]
    return jnp.concatenate([x * mk for mk in _group_masks(x.dtype)], axis=0)


def _attn_prep(mode, group, s_len, padk, q_ref, k_ref, v_ref, gq_ref, gk_ref, qs, k2, v2, do_ref=None, dos=None):
    m0, m1 = _head_masks()
    zpad = jnp.zeros((padk, k2.shape[1]), BF16)
    k2[pl.ds(0, padk), :] = zpad
    v2[pl.ds(0, padk), :] = zpad
    first_kv = group == 0
    rt = 2 * QTILE
    for i in range(s_len // rt):
        rows = pl.ds(i * rt, rt)
        qn, _, _ = _head_norm(q_ref[rows, :], gq_ref[...], m0, m1)
        kn, _, _ = _head_norm(k_ref[rows, :], gk_ref[...], m0, m1)
        qn = qn * (HEAD_DIM ** -0.5)
        if mode == "A":
            qs[rows, :] = qn.astype(BF16)
            if dos is not None:
                dos[rows, :] = do_ref[rows, :].astype(BF16)
        else:
            _store_stacked(qs, i, _stack_heads(first_kv, qn, m0, m1))
            if dos is not None:
                _store_stacked(dos, i, _stack_heads(first_kv, do_ref[rows, :], m0, m1))
        k2[pl.ds(padk + i * rt, rt), :] = kn.astype(BF16)
        v2[pl.ds(padk + i * rt, rt), :] = v_ref[rows, :].astype(BF16)


def _softmax_terms(mode, s, sink):
    mx = jnp.max(s, axis=-1, keepdims=True)
    if mode == "B":
        mx = jnp.maximum(mx, sink)
    e = jnp.exp(s - mx)
    l = jnp.sum(e, axis=-1, keepdims=True)
    if mode == "B":
        l = l + jnp.exp(sink - mx)
    return e, mx, l


def _sink_column(sink_ref, group):
    row = lax.broadcasted_iota(jnp.int32, (STACK, 1), 0)
    col = jnp.zeros((STACK, 1), F32)
    for h in range(GROUP):
        col = jnp.where((row >= h * QTILE) & (row < (h + 1) * QTILE), sink_ref[GROUP * group + h], col)
    return col


def _head_deltas(dd, m0, m1):
    cols = []
    for half in range(GROUP // 2):
        dh = dd[:, half * LANES:(half + 1) * LANES]
        cols += [jnp.sum(dh * m0, axis=-1, keepdims=True), jnp.sum(dh * m1, axis=-1, keepdims=True)]
    return jnp.concatenate(cols, axis=0)


def _attn_cols(mode):
    if mode == "A":
        return (lambda b, g: (b, g)), (lambda b, g: (b, 2 + g)), (lambda b, g: (b, 4 + g))
    return (lambda b, g: (b, 6 + g)), (lambda b, g: (b, 16)), (lambda b, g: (b, 17))


def _attn_fwd(mode, qkv, gq, gk, bias, sinks, bl, s_len, name):
    bw = bias.shape[-1]
    padk = bw - QTILE
    nt = s_len // QTILE
    qmap, kmap, vmap = _attn_cols(mode)

    kw = _kv_width(mode)

    def body(q_ref, k_ref, v_ref, gq_ref, gk_ref, bias_ref, sink_ref, o_ref, qs, k2, v2, s_buf, *rest):
        o_buf = rest[0] if rest else None
        group = pl.program_id(1)
        m0, m1 = _head_masks()
        first_kv = group == 0
        _attn_prep(mode, group, s_len, padk, q_ref, k_ref, v_ref, gq_ref, gk_ref, qs, k2, v2)
        col = lax.broadcasted_iota(jnp.int32, (STACK, bw), 1)
        sink = _sink_column(sink_ref, group)

        def scores(m, slot):
            r0 = pl.multiple_of(m * QTILE, QTILE)
            s = _dot_nt(_load_stacked(mode, qs, m), k2[pl.ds(r0, bw), :]) + bias_ref[...]
            s_buf[slot] = jnp.where(col >= (padk - r0), s, NEG_INF)

        def finish_tile(m, slot):
            r0 = pl.multiple_of(m * QTILE, QTILE)
            e, _, l = _softmax_terms(mode, s_buf[slot], sink)
            if mode == "A":
                o_st = _dot(e.astype(BF16), v2[pl.ds(r0, bw), :]) / l
                heads = [o_st[h * QTILE:(h + 1) * QTILE] for h in range(GROUP)]
                o_ref[pl.ds(r0, QTILE), :] = _unstack_heads(mode, first_kv, heads, m0, m1)
            else:
                o_buf[pl.ds(pl.multiple_of(m * STACK, STACK), STACK), :] = _dot((e * (1.0 / l)).astype(BF16),
                                                                                 v2[pl.ds(r0, bw), :])

        scores(0, 0)

        def pair(j, carry):
            scores(2 * j + 1, 1)
            finish_tile(2 * j, 0)
            scores(jnp.minimum(2 * j + 2, nt - 1), 0)
            finish_tile(2 * j + 1, 1)
            return carry

        lax.fori_loop(0, nt // 2, pair, 0)
        if mode == "B":
            for m in range(nt):
                heads = [o_buf[pl.ds(m * STACK + h * QTILE, QTILE), :] for h in range(GROUP)]
                o_ref[pl.ds(m * QTILE, QTILE), :] = _unstack_heads(mode, first_kv, heads, m0, m1)

    blk = lambda w, f: pl.BlockSpec((s_len, w), f)
    return pl.pallas_call(
        body, name=name, grid=(bl, B_Q_HEADS // GROUP),
        out_shape=jax.ShapeDtypeStruct((bl * s_len, B_Q_HEADS * HEAD_DIM), F32),
        in_specs=[blk(QW, qmap), blk(kw, kmap), blk(kw, vmap),
                  pl.BlockSpec((1, QW), lambda b, g: (0, 0)), pl.BlockSpec((1, kw), lambda b, g: (0, 0)),
                  pl.BlockSpec((STACK, bw), lambda b, g: (g, 0)),
                  pl.BlockSpec(memory_space=pltpu.SMEM)],
        out_specs=blk(QW, lambda b, g: (b, g)),
        scratch_shapes=[pltpu.VMEM(_q_scratch_shape(mode, s_len), BF16)] + [pltpu.VMEM((s_len + padk, kw), BF16)] * 2
        + [pltpu.VMEM((2, STACK, bw), F32)] + ([pltpu.VMEM((GROUP * s_len, LANES), F32)] if mode == "B" else []),
        compiler_params=_params(("arbitrary", "arbitrary")),
    )(qkv, qkv, qkv, gq, gk, bias.reshape(-1, bw), sinks)


def _attn_bwd(mode, qkv, gq, gk, bias, sinks, y, dy, bl, s_len, name):
    bw = bias.shape[-1]
    padk = bw - QTILE
    nt = s_len // QTILE
    qmap, kmap, vmap = _attn_cols(mode)
    t = bl * s_len
    kw = _kv_width(mode)
    kvw = 4 * LANES if mode == "A" else LANES
    dp_ahead = True

    def body(q_ref, k_ref, v_ref, gq_ref, gk_ref, bias_ref, sink_ref, y_ref, dy_ref,
             dq_ref, dk_ref, dv_ref, dgq_ref, dgk_ref, dbias_ref, dsink_ref,
             qs, k2, v2, dos, dqs, dk, dv, s_buf, dp_buf):
        group = pl.program_id(1)
        m0, m1 = _head_masks()
        first_kv = group == 0
        _attn_prep(mode, group, s_len, padk, q_ref, k_ref, v_ref, gq_ref, gk_ref, qs, k2, v2, dy_ref, dos)
        dk[...] = jnp.zeros_like(dk)
        dv[...] = jnp.zeros_like(dv)
        dbias_ref[...] = jnp.zeros_like(dbias_ref)
        col = lax.broadcasted_iota(jnp.int32, (STACK, bw), 1)
        lane8 = lax.broadcasted_iota(jnp.int32, (8, LANES), 1)
        sink = _sink_column(sink_ref, group)

        def ahead(m, slot):
            r0 = pl.multiple_of(m * QTILE, QTILE)
            band = pl.ds(r0, bw)
            s = _dot_nt(_load_stacked(mode, qs, m), k2[band, :]) + bias_ref[...]
            s_buf[slot] = jnp.where(col >= (padk - r0), s, NEG_INF)
            if dp_ahead:
                dp_buf[slot] = _dot_nt(_load_stacked(mode, dos, m), v2[band, :])

        def tile(m, slot, dsink):
            r0 = pl.multiple_of(m * QTILE, QTILE)
            rows = pl.ds(r0, QTILE)
            band = pl.ds(r0, bw)
            q_st = _load_stacked(mode, qs, m)
            do_st = _load_stacked(mode, dos, m)
            delta = _head_deltas(dy_ref[rows, :] * y_ref[rows, :], m0, m1)
            kb = k2[band, :]
            e, mx, l = _softmax_terms(mode, s_buf[slot], sink)
            inv = 1.0 / l
            pn = e * inv
            ds = pn * ((dp_buf[slot] if dp_ahead else _dot_nt(do_st, v2[band, :])) - delta)
            if mode == "A":
                dbias_ref[...] += ds
            else:
                part = jnp.exp(sink - mx) * inv * delta
                for h in range(GROUP):
                    dsink = dsink - jnp.where(lane8 == h, jnp.sum(part[h * QTILE:(h + 1) * QTILE]), 0.0)
            dsb = ds.astype(BF16)
            dv[band, :] += _dot_tn(pn.astype(BF16), do_st)
            dk[band, :] += _dot_tn(dsb, q_st)
            dq_st = _dot(dsb, kb)
            if mode == "A":
                heads = [dq_st[h * QTILE:(h + 1) * QTILE] for h in range(GROUP)]
                dq_ref[rows, :] = _unstack_heads(mode, first_kv, heads, m0, m1)
            else:
                dqs[pl.ds(pl.multiple_of(m * STACK, STACK), STACK), :] = dq_st
            return dsink

        ahead(0, 0)

        def pair(j, dsink):
            ahead(2 * j + 1, 1)
            dsink = tile(2 * j, 0, dsink)
            ahead(jnp.minimum(2 * j + 2, nt - 1), 0)
            return tile(2 * j + 1, 1, dsink)

        dsink = lax.fori_loop(0, nt // 2, pair, jnp.zeros((8, LANES), F32))
        dsink_ref[...] = dsink

        rt = 2 * QTILE
        dgq = jnp.zeros((1, QW), F32)
        dgk = jnp.zeros((1, kw), F32)
        for i in range(s_len // rt):
            rows = pl.ds(i * rt, rt)
            src = pl.ds(padk + i * rt, rt)
            gq_v, gk_v = gq_ref[...], gk_ref[...]
            _, qh, qr = _head_norm(q_ref[rows, :], gq_v, m0, m1)
            _, kh, kr = _head_norm(k_ref[rows, :], gk_v, m0, m1)
            if mode == "A":
                dqn = dq_ref[rows, :] * (HEAD_DIM ** -0.5)
            else:
                dqn = jnp.concatenate(
                    [_unstack_heads(mode, first_kv, [dqs[pl.ds((2 * i + half) * STACK + h * QTILE, QTILE), :]
                                                     for h in range(GROUP)], m0, m1)
                     for half in range(2)], axis=0) * (HEAD_DIM ** -0.5)
            dq_raw, dgq_i = _head_norm_bwd(qh, qr, gq_v, dqn, m0, m1)
            dk_raw, dgk_i = _head_norm_bwd(kh, kr, gk_v, dk[src, :], m0, m1)
            dvn = dv[src, :]
            dq_ref[rows, :] = dq_raw.astype(dq_ref.dtype)
            if mode == "A":
                dk_ref[rows, :] = dk_raw.astype(dk_ref.dtype)
                dv_ref[rows, :] = dvn.astype(dv_ref.dtype)
            else:
                @pl.when(group == 0)
                def _():
                    dk_ref[rows, :] = dk_raw
                    dv_ref[rows, :] = dvn

                @pl.when(group != 0)
                def _():
                    dk_ref[rows, :] += dk_raw
                    dv_ref[rows, :] += dvn
            dgq, dgk = dgq + dgq_i, dgk + dgk_i
        dgq_ref[...] = jnp.broadcast_to(dgq, (8, QW))
        dgk_ref[...] = jnp.broadcast_to(dgk, (8, kw))

    ng = B_Q_HEADS // GROUP
    blk = lambda w, f: pl.BlockSpec((s_len, w), f)
    small = lambda w: pl.BlockSpec((None, None, 8, w), lambda b, g: (b, g, 0, 0))
    own = lambda b, g: (b, g)
    kvmap = own if mode == "A" else (lambda b, g: (b, 0))
    pad_f32 = pltpu.VMEM((s_len + padk, kw), F32)
    pad_bf = pltpu.VMEM((s_len + padk, kw), BF16)
    stack_bf = pltpu.VMEM(_q_scratch_shape(mode, s_len), BF16)
    outs = pl.pallas_call(
        body, name=name, grid=(bl, ng),
        out_shape=[jax.ShapeDtypeStruct((t, ng * QW), F32 if mode == "A" else BF16),
                   jax.ShapeDtypeStruct((t, kvw), BF16 if mode == "A" else F32),
                   jax.ShapeDtypeStruct((t, kvw), BF16 if mode == "A" else F32),
                   jax.ShapeDtypeStruct((bl, ng, 8, QW), F32), jax.ShapeDtypeStruct((bl, ng, 8, kw), F32),
                   jax.ShapeDtypeStruct((bl, ng * STACK, bw), F32), jax.ShapeDtypeStruct((bl, ng, 8, LANES), F32)],
        in_specs=[blk(QW, qmap), blk(kw, kmap), blk(kw, vmap),
                  pl.BlockSpec((1, QW), lambda b, g: (0, 0)), pl.BlockSpec((1, kw), lambda b, g: (0, 0)),
                  pl.BlockSpec((STACK, bw), lambda b, g: (g, 0)),
                  pl.BlockSpec(memory_space=pltpu.SMEM),
                  blk(QW, own), blk(QW, own)],
        out_specs=[blk(QW, own), blk(kw, kvmap), blk(kw, kvmap), small(QW), small(kw),
                   pl.BlockSpec((None, STACK, bw), lambda b, g: (b, g, 0)), small(LANES)],
        scratch_shapes=[stack_bf, pad_bf, pad_bf, stack_bf,
                        pltpu.VMEM((8, LANES) if mode == "A" else _q_scratch_shape(mode, s_len), F32),
                        pad_f32, pad_f32, pltpu.VMEM((2, STACK, bw), F32),
                        pltpu.VMEM((2, STACK, bw) if dp_ahead else (8, LANES), F32)],
        compiler_params=_params(("arbitrary", "arbitrary")),
    )(qkv, qkv, qkv, gq, gk, bias.reshape(-1, bw), sinks, y, dy)
    outs = list(outs)
    outs[5] = outs[5].reshape(bl, B_Q_HEADS, QTILE, bw)
    return outs


def _band_geometry(prev):
    bw = QTILE + prev * CHUNK
    i = np.arange(QTILE)[:, None]
    j = np.arange(bw)[None, :]
    dist = i + prev * CHUNK - j
    valid = (j // CHUNK >= i // CHUNK) & (j // CHUNK <= i // CHUNK + prev)
    return dist, valid


A_VAR0 = (A_PREV * CHUNK - A_MAX_REL) // LANES * LANES


A_NVAR = QTILE + A_PREV * CHUNK - A_VAR0


def _skew_rows(x, sign):
    rows, n = x.shape
    row = lax.broadcasted_iota(jnp.int32, x.shape, 0)
    b = 1
    while b < rows:
        x = jnp.where((row & b) != 0, pltpu.roll(x, (sign * b) % n, 1), x)
        b *= 2
    return x


def _rel_bias_expand(table, name):
    _, valid = _band_geometry(A_PREV)
    bw = valid.shape[1]
    valid_f = jnp.asarray(valid.astype(np.float32))
    rev = jnp.flip(table[:, 1:], axis=1).reshape(A_HEADS, 1, A_NVAR)

    def body(rev_ref, valid_ref, o_ref):
        rowv = jnp.broadcast_to(rev_ref[...], (QTILE, A_NVAR))
        top = rowv[:, 0:1]
        var = _skew_rows(rowv, 1)
        row = lax.broadcasted_iota(jnp.int32, (QTILE, A_NVAR), 0)
        colv = lax.broadcasted_iota(jnp.int32, (QTILE, A_NVAR), 1)
        var = jnp.where(colv < row, top, var)
        ok = valid_ref[...] > 0.5
        o_ref[:, :A_VAR0] = jnp.where(ok[:, :A_VAR0], top, NEG_INF)
        o_ref[:, A_VAR0:] = jnp.where(ok[:, A_VAR0:], var, NEG_INF)

    return pl.pallas_call(
        body, name=name, grid=(A_HEADS,),
        out_shape=jax.ShapeDtypeStruct((A_HEADS, QTILE, bw), F32),
        in_specs=[pl.BlockSpec((None, 1, A_NVAR), lambda h: (h, 0, 0)), pl.BlockSpec((QTILE, bw), lambda h: (0, 0))],
        out_specs=pl.BlockSpec((None, QTILE, bw), lambda h: (h, 0, 0)),
        compiler_params=_params(("arbitrary",)),
    )(rev, valid_f)


def _rel_bias_grad(dbias, name):
    bl = dbias.shape[0]
    bw = dbias.shape[-1]

    def body(db_ref, o_ref):
        g = db_ref[0]
        for b in range(1, bl):
            g = g + db_ref[b]
        sk = _skew_rows(g[:, A_VAR0:], -1)
        row = lax.broadcasted_iota(jnp.int32, (QTILE, A_NVAR), 0)
        colv = lax.broadcasted_iota(jnp.int32, (QTILE, A_NVAR), 1)
        wrapped = (row + colv) >= A_NVAR
        main = jnp.sum(jnp.where(wrapped, 0.0, sk), axis=0, keepdims=True)
        top = jnp.sum(g[:, :A_VAR0]) + jnp.sum(jnp.where(wrapped, sk, 0.0))
        o_ref[:, :A_NVAR] = jnp.broadcast_to(main, (8, A_NVAR))
        o_ref[:, A_NVAR:] = jnp.full((8, LANES), top, F32)

    out = pl.pallas_call(
        body, name=name, grid=(A_HEADS,),
        out_shape=jax.ShapeDtypeStruct((A_HEADS, 8, A_NVAR + LANES), F32),
        in_specs=[pl.BlockSpec((bl, None, QTILE, bw), lambda h: (0, h, 0, 0))],
        out_specs=pl.BlockSpec((None, 8, A_NVAR + LANES), lambda h: (h, 0, 0)),
        compiler_params=_params(("arbitrary",)),
    )(dbias)
    main, top = out[:, 0, :A_NVAR], out[:, 0, A_NVAR]
    fm = jnp.flip(main, axis=1)
    return jnp.concatenate([jnp.zeros((A_HEADS, 1), F32), fm[:, :-1], fm[:, -1:] + top[:, None]], axis=1)


def _alibi_bias():
    dist, valid = _band_geometry(B_PREV)
    slopes = np.array([2.0 ** (-8.0 * (h + 1) / B_Q_HEADS) for h in range(B_Q_HEADS)], dtype=np.float32)
    bias = -slopes[:, None, None] * np.abs(dist).astype(np.float32)[None]
    return jnp.asarray(np.where(valid[None], bias, np.float32(NEG_INF)).astype(np.float32))


SMALL_NAMES = ("ffn1_norm", "mix_norm", "ffn2_norm", "ple_norm", "a_q_norm", "a_k_norm", "b_q_norm", "b_k_norm",
               "a_rel_bias", "b_sinks", "loss")


def _pack_small(vals):
    rows = []
    for nme in SMALL_NAMES:
        v = vals[nme].astype(F32)
        if nme == "a_rel_bias":
            v = jnp.pad(v.reshape(A_HEADS, -1), ((0, 0), (0, 3 * LANES - (2 * A_MAX_REL + 1))))
        v = v.reshape(-1)
        v = jnp.pad(v, (0, (-v.shape[0]) % LANES))
        rows.append(v.reshape(-1, LANES))
    out = jnp.concatenate(rows, axis=0)
    return jnp.pad(out, ((0, (-out.shape[0]) % 8), (0, 0)))


def _unpack_small(packed, shapes):
    out, r = {}, 0
    for nme in SMALL_NAMES:
        shp = shapes[nme]
        if nme == "a_rel_bias":
            nr = A_HEADS * 3
            out[nme] = packed[r:r + nr].reshape(A_HEADS, 3 * LANES)[:, :2 * A_MAX_REL + 1].reshape(shp)
        else:
            size = int(np.prod(shp)) if shp else 1
            nr = -(-size // LANES)
            out[nme] = packed[r:r + nr].reshape(-1)[:size].reshape(shp)
        r += nr
    return out


BIG_NAMES = ("ffn1_w_gu", "ffn1_w_down", "w_in", "w_gate", "w_proj_a", "w_proj_b", "w_out",
             "ffn2_w_gu", "ffn2_w_down", "w_ple_gate", "w_ple_proj")
WEIGHT_ORDER = ("ffn1_norm", "ffn1_w_gu", "ffn1_w_down", "mix_norm", "w_in", "a_q_norm", "a_k_norm", "a_rel_bias",
                "b_q_norm", "b_k_norm", "b_sinks", "w_gate", "w_proj_a", "w_proj_b", "w_out", "ffn2_norm",
                "ffn2_w_gu", "ffn2_w_down", "ple_norm", "w_ple_gate", "w_ple_proj")


TRANSPOSED = ("ffn1_w_gu", "ffn2_w_gu", "w_in")


def _local(a, nme):
    return a[0].T if nme in TRANSPOSED else a[0]


def _full_cols(wg):
    nb, k, n = wg.shape
    return jnp.transpose(wg, (1, 0, 2)).reshape(k, nb * n)


def _step(x, p, target, w, m, v):
    bl, s_len, d = x.shape
    t = bl * s_len
    h0 = x.reshape(t, d)
    pt = p.reshape(t, p.shape[-1])
    tgt = target.reshape(t, d)

    g_ffn1, g_mix, g_ffn2, g_ple = w["ffn1_norm"], w["mix_norm"], w["ffn2_norm"], w["ple_norm"]
    tiled = lambda a, width: jnp.tile(a.reshape(1, HEAD_DIM), (1, width // HEAD_DIM))
    gqa, gka = tiled(w["a_q_norm"], QW), tiled(w["a_k_norm"], _kv_width("A"))
    gqb, gkb = tiled(w["b_q_norm"], QW), tiled(w["b_k_norm"], _kv_width("B"))
    sinks = w["b_sinks"].reshape(B_Q_HEADS)
    bias_b = _alibi_bias()

    ffn1_names = ("ffn1_w_gu", "ffn1_w_down")
    shard = {nme: _local(w[nme], nme).astype(BF16) for nme in ffn1_names}
    send1, recv1, bufs, token = _gather_start([shard["ffn1_w_gu"]], h0, "gather_start_ffn1_gu")
    dsend1, drecv1, dbufs, token = _gather_start([shard["ffn1_w_down"]], token, "gather_start_ffn1_down")
    zero = token[0, 0]
    shard.update({nme: (_local(w[nme], nme) + zero).astype(BF16) for nme in BIG_NAMES if nme not in ffn1_names})
    bias_a = _rel_bias_expand(w["a_rel_bias"][0] + zero, "rel_bias_expand")
    send2, recv2, bufs, token = _gather_pass(send1, recv1, bufs, bias_a, "gather_pass_ffn1_gu")
    (wgu1,) = _gather_wait(send2, recv2, bufs, shard["ffn2_w_gu"], "gather_wait_ffn1_gu")
    nf = wgu1.shape[1]
    mixer_names = ("w_in", "w_gate")
    rest_names = ("w_proj_a", "w_proj_b", "w_out", "ffn2_w_gu", "ffn2_w_down", "w_ple_gate", "w_ple_proj")
    send1, recv1, bufs, token = _gather_start([shard[nme] for nme in mixer_names], wgu1, "gather_start_mixer")
    rsend1, rrecv1, rest_bufs, token = _gather_start([shard[nme] for nme in rest_names], token, "gather_start_rest")

    gu1, a1f = _ffn_up(h0, g_ffn1 + token[0, 0], wgu1, "ffn1_up")
    dsend2, drecv2, dbufs, token = _gather_pass(dsend1, drecv1, dbufs, a1f, "gather_pass_ffn1_down")
    (wd1,) = _gather_wait(dsend2, drecv2, dbufs, token, "gather_wait_ffn1_down")
    wd1 = wd1.reshape(N_DEV // 2, nf, d)
    h1 = _ffn_down(h0, a1f, wd1, "ffn1_down")
    send2, recv2, bufs, token = _gather_pass(send1, recv1, bufs, h1, "gather_pass_mixer")
    win, wgate = _gather_wait(send2, recv2, bufs, token, "gather_wait_mixer")
    win, wgate = win.reshape(IN_COLS, d), _full_cols(wgate)
    un, qkv, gate = _proj_fwd(h1, g_mix, win, wgate, "proj_fwd")
    ya = _attn_fwd("A", qkv, gqa, gka, bias_a, sinks, bl, s_len, "attn_a_fwd")
    rsend2, rrecv2, rest_bufs, token = _gather_pass(rsend1, rrecv1, rest_bufs, ya, "gather_pass_rest")
    yb = _attn_fwd("B", qkv, gqb + token[0, 0], gkb, bias_b, sinks, bl, s_len, "attn_b_fwd")
    gathered = dict(zip(rest_names, _gather_wait(rsend2, rrecv2, rest_bufs, yb, "gather_wait_rest")))
    wgu2 = gathered["ffn2_w_gu"]
    wd2 = gathered["ffn2_w_down"].reshape(N_DEV // 2, nf, d)
    wpa = _full_cols(gathered["w_proj_a"])
    wpb = _full_cols(gathered["w_proj_b"])
    wpe = _full_cols(gathered["w_ple_proj"])
    wout = gathered["w_out"].reshape(d, d)
    wpg = gathered["w_ple_gate"].reshape(d, d)
    h2, merged, pa, pb = _merge_fwd(h1, ya, yb, gate, wpa, wpb, wout, "merge_fwd")
    h3, gu2 = _ffn_fwd(h2, g_ffn2, wgu2, wd2, "ffn2_fwd")
    dh3, dz4, dpp, n4, dg_ple, loss_part = _ple_loss(h3, g_ple, pt, tgt, wpg, wpe, "ple_loss")

    xi, yi, ci = _place()
    me = jnp.stack([4 * xi + 2 * yi + ci, 2 * xi + yi]).astype(jnp.int32)
    g32, g16, big, pairs = {}, {}, {}, {}

    def keep(nme, pair, rows=None):
        for store, g in zip((g32, g16), pair):
            store[nme] = g if rows is None else g.reshape(N_DEV, rows, d)

    def start(names, after, tag):
        send, recv, parts, lands, token = _scatter_start([g16[nme] for nme in names], after, "grads_start_" + tag)
        return names, send, recv, parts, lands, token

    def start_two_level(names, after, tag):
        views = [g16[nme].reshape((4, 2) + g16[nme].shape[1:]) for nme in names]
        for nme, got in zip(names, _pair_exchange(views, "grads_pair_" + tag)):
            pairs[nme] = got.reshape((4,) + got.shape[2:])
        sums = [_pair_sum(g32[nme], pairs[nme], me, "pair_sum_" + nme) for nme in names]
        send, recv, parts, lands, token = _scatter_start(sums, after, "grads_start_" + tag, SAME_CORE_CHIPS)
        return names, send, recv, parts, lands, token

    def finish(state, after, tag):
        names, send, recv, parts, lands, _ = state
        relations = SAME_CORE_CHIPS if names[0] in pairs else ALL_PEERS
        lands = _scatter_wait(send, recv, parts, lands, after, "grads_wait_" + tag, relations)
        return names, lands

    def adam(done, dep):
        for nme, land in zip(*done):
            outs = _final_adam(g32[nme], land, _local(w[nme], nme), _local(m[nme], nme), _local(v[nme], nme), me, dep,
                               "adam_" + nme, pairs.get(nme))
            big[nme] = [(o.T if nme in TRANSPOSED else o)[None] for o in outs]

    keep("w_ple_gate", _dw(n4, dz4, 1, d, "dw_ple_gate"), d // N_DEV)
    keep("w_ple_proj", _dw(pt, dpp, N_DEV, d // N_DEV, "dw_ple_proj"))

    dh2, dgu2, a2, n3, dg_ffn2 = _ffn_bwd(dh3, h2, g_ffn2, gu2, wgu2, wd2, "ffn2_bwd")
    keep("ffn2_w_down", _dw(a2, dh3, N_DEV // 2, d, "dw_ffn2_down", 0.5), nf // 2)
    early = [(start(("w_ple_gate", "w_ple_proj", "ffn2_w_down"), dh2, "ffn2_down"), "ffn2_down")]
    keep("ffn2_w_gu", _dw(dgu2, n3, N_DEV, d, "dw_ffn2_gu", dep=early[-1][0][-1]))
    flight = start(("ffn2_w_gu",), dh2, "ffn2")

    dpa, dpb, dzg, dya, dyb = _merge_bwd(dh2, pa, pb, gate, wpa, wpb, wout, "merge_bwd")
    keep("w_out", _dw(merged, dh2, 1, d, "dw_out"), d // N_DEV)
    keep("w_proj_a", _dw(ya, dpa, N_DEV, d // N_DEV, "dw_proj_a"))
    keep("w_proj_b", _dw(yb, dpb, N_DEV, d // N_DEV, "dw_proj_b"))
    keep("w_gate", _dw(un, dzg, N_DEV, 2 * d // N_DEV, "dw_gate"))

    tok = flight[-1][0, 0]
    dqa, dka, dva, dgqa, dgka, dbias, _ = _attn_bwd("A", qkv, gqa + tok, gka, bias_a, sinks, ya, dya, bl, s_len,
                                                     "attn_a_bwd")
    dqb, dkb, dvb, dgqb, dgkb, _, dsink = _attn_bwd("B", qkv, gqb, gkb, bias_b, sinks, yb, dyb, bl, s_len, "attn_b_bwd")
    dqkv = [dqa, dka, dva, dqb, dkb, dvb]
    dtab = _rel_bias_grad(dbias, "rel_bias_grad")

    dh1, dg_mix = _proj_bwd(dh2, h1, g_mix, dzg, dqkv, win, wgate, "proj_bwd")
    keep("w_in", _dw_rows(dqkv, un, "dw_in"), IN_COLS // N_DEV)
    waiting = [finish(state, g32["w_in"], tag) for state, tag in early]
    done = finish(flight, waiting[-1][1][0], "ffn2")
    flight = start(("w_out", "w_proj_a", "w_proj_b", "w_gate", "w_in"), done[1][0], "mixer")
    waiting.append(done)

    dh0, dgu1, a1, n1, dg_ffn1 = _ffn_bwd(dh1, h0, g_ffn1 + flight[-1][0, 0], gu1, wgu1, wd1, "ffn1_bwd")
    keep("ffn1_w_down", _dw(a1, dh1, N_DEV // 2, d, "dw_ffn1_down", 0.5), nf // 2)
    done = finish(flight, g32["ffn1_w_down"], "mixer")
    flight = start(("ffn1_w_down",), done[1][0], "ffn1_down")
    waiting.append(done)

    keep("ffn1_w_gu", _dw(dgu1, n1, N_DEV, d, "dw_ffn1_gu", dep=flight[-1]))
    done = finish(flight, g32["ffn1_w_gu"], "ffn1_down")
    flight = start_two_level(("ffn1_w_gu",), done[1][0], "ffn1_gu")
    for group in waiting + [done]:
        adam(group, flight[-1])
    behind = 0.0 * big["ffn1_w_down"][0][0, 0, :1]
    smalls = (dg_ffn1, dg_mix, dg_ffn2, dg_ple + behind, dgqa, dgka, dgqb, dgkb, dtab, dsink)
    return dh0, loss_part, big, smalls, flight, finish, adam


def kernel(x, p, ffn1_norm, ffn1_w_gu, ffn1_w_down, mix_norm, w_in, a_q_norm, a_k_norm, a_rel_bias, b_q_norm, b_k_norm, b_sinks, w_gate, w_proj_a, w_proj_b, w_out, ffn2_norm, ffn2_w_gu, ffn2_w_down, ple_norm, w_ple_gate, w_ple_proj, loss_target, m_ffn1_norm, m_ffn1_w_gu, m_ffn1_w_down, m_mix_norm, m_w_in, m_a_q_norm, m_a_k_norm, m_a_rel_bias, m_b_q_norm, m_b_k_norm, m_b_sinks, m_w_gate, m_w_proj_a, m_w_proj_b, m_w_out, m_ffn2_norm, m_ffn2_w_gu, m_ffn2_w_down, m_ple_norm, m_w_ple_gate, m_w_ple_proj, v_ffn1_norm, v_ffn1_w_gu, v_ffn1_w_down, v_mix_norm, v_w_in, v_a_q_norm, v_a_k_norm, v_a_rel_bias, v_b_q_norm, v_b_k_norm, v_b_sinks, v_w_gate, v_w_proj_a, v_w_proj_b, v_w_out, v_ffn2_norm, v_ffn2_w_gu, v_ffn2_w_down, v_ple_norm, v_w_ple_gate, v_w_ple_proj):
    w = dict(ffn1_norm=ffn1_norm, ffn1_w_gu=ffn1_w_gu, ffn1_w_down=ffn1_w_down, mix_norm=mix_norm, w_in=w_in,
             a_q_norm=a_q_norm, a_k_norm=a_k_norm, a_rel_bias=a_rel_bias, b_q_norm=b_q_norm, b_k_norm=b_k_norm,
             b_sinks=b_sinks, w_gate=w_gate, w_proj_a=w_proj_a, w_proj_b=w_proj_b, w_out=w_out, ffn2_norm=ffn2_norm,
             ffn2_w_gu=ffn2_w_gu, ffn2_w_down=ffn2_w_down, ple_norm=ple_norm, w_ple_gate=w_ple_gate,
             w_ple_proj=w_ple_proj)
    m = dict(ffn1_norm=m_ffn1_norm, ffn1_w_gu=m_ffn1_w_gu, ffn1_w_down=m_ffn1_w_down, mix_norm=m_mix_norm,
             w_in=m_w_in, a_q_norm=m_a_q_norm, a_k_norm=m_a_k_norm, a_rel_bias=m_a_rel_bias, b_q_norm=m_b_q_norm,
             b_k_norm=m_b_k_norm, b_sinks=m_b_sinks, w_gate=m_w_gate, w_proj_a=m_w_proj_a, w_proj_b=m_w_proj_b,
             w_out=m_w_out, ffn2_norm=m_ffn2_norm, ffn2_w_gu=m_ffn2_w_gu, ffn2_w_down=m_ffn2_w_down,
             ple_norm=m_ple_norm, w_ple_gate=m_w_ple_gate, w_ple_proj=m_w_ple_proj)
    v = dict(ffn1_norm=v_ffn1_norm, ffn1_w_gu=v_ffn1_w_gu, ffn1_w_down=v_ffn1_w_down, mix_norm=v_mix_norm,
             w_in=v_w_in, a_q_norm=v_a_q_norm, a_k_norm=v_a_k_norm, a_rel_bias=v_a_rel_bias, b_q_norm=v_b_q_norm,
             b_k_norm=v_b_k_norm, b_sinks=v_b_sinks, w_gate=v_w_gate, w_proj_a=v_w_proj_a, w_proj_b=v_w_proj_b,
             w_out=v_w_out, ffn2_norm=v_ffn2_norm, ffn2_w_gu=v_ffn2_w_gu, ffn2_w_down=v_ffn2_w_down,
             ple_norm=v_ple_norm, w_ple_gate=v_w_ple_gate, w_ple_proj=v_w_ple_proj)
    bl, s_len, d = x.shape

    dh0, loss_part, big, smalls, flight, finish, adam = _step(x, p[0], loss_target, w, m, v)
    dg_ffn1, dg_mix, dg_ffn2, dg_ple, dgqa, dgka, dgqb, dgkb, dtab, dsink = smalls

    fold = lambda a: a[:, :, 0, :].reshape(-1, HEAD_DIM).sum(axis=0)
    small_part = dict(
        ffn1_norm=dg_ffn1, mix_norm=dg_mix, ffn2_norm=dg_ffn2, ple_norm=dg_ple,
        a_q_norm=fold(dgqa), a_k_norm=fold(dgka), b_q_norm=fold(dgqb), b_k_norm=fold(dgkb),
        a_rel_bias=dtab,
        b_sinks=dsink.sum(axis=0)[:, 0, :GROUP].reshape(B_Q_HEADS),
        loss=loss_part[0, :1])
    zero1 = jnp.zeros((1,), F32)
    shapes = {nme: w[nme].shape for nme in SMALL_NAMES if nme != "loss"}
    shapes["loss"] = ()
    pk = lambda src: _pack_small({**{nme: src[nme] for nme in SMALL_NAMES if nme != "loss"}, "loss": zero1})
    sg, sd, sm, sv = _small_allreduce_adam(_pack_small(small_part), pk(w), pk(m), pk(v), "small_allreduce_adam")
    adam(finish(flight, sg, "ffn1_gu"), sg)
    sg, sd, sm, sv = (_unpack_small(a, shapes) for a in (sg, sd, sm, sv))

    def pick(i):
        out = []
        for nme in WEIGHT_ORDER:
            out.append(big[nme][i] if nme in big else (sg, sd, sm, sv)[i][nme])
        return out

    return (sg["loss"], dh0.reshape(bl, s_len, d), *pick(0), *pick(1), *pick(2), *pick(3))
```

```python
import jax
import jax.numpy as jnp
import numpy as np
from jax import lax
from jax.experimental import pallas as pl
from jax.experimental.pallas import tpu as pltpu

F32 = jnp.float32
BF16 = jnp.bfloat16

CHUNK = 64
HEAD_DIM = 64
A_HEADS = 8
A_PREV = 8
A_MAX_REL = 128
B_Q_HEADS = 8
B_KV_HEADS = 2
B_PREV = 2
A_WIDTH = A_HEADS * HEAD_DIM
B_Q_WIDTH = B_Q_HEADS * HEAD_DIM
B_KV_WIDTH = B_KV_HEADS * HEAD_DIM
IN_COLS = 3 * A_WIDTH + B_Q_WIDTH + 2 * B_KV_WIDTH
EPS = 1e-6
NEG_INF = -1e30
ADAM_LR = 0.001
ADAM_B1 = 0.9
ADAM_B2 = 0.999
ADAM_EPS = 1e-08
ADAM_WD = 0.01
ADAM_STEP = 10

N_DEV = 8
LANES = 128
QTILE = 2 * CHUNK
VMEM_LIMIT = 56 * 1024 * 1024
DW_VMEM_LIMIT = 60 * 1024 * 1024
ADAM_TILE_ELEMS = 256 * 1024

MESH_ID = pl.DeviceIdType.MESH
ANY = pl.BlockSpec(memory_space=pl.ANY)
HBM = pl.BlockSpec(memory_space=pltpu.HBM)
SEM = pl.BlockSpec(memory_space=pltpu.SEMAPHORE)
SIDE_EFFECT = pltpu.SideEffectType.DATAFLOW_SIDE_EFFECTING


def _dot(a, b):
    return jnp.dot(a, b, preferred_element_type=F32)


def _dot_nt(a, b):
    return lax.dot_general(a, b, (((1,), (1,)), ((), ())), preferred_element_type=F32)


def _dot_tn(a, b):
    return lax.dot_general(a, b, (((0,), (0,)), ((), ())), preferred_element_type=F32)


def _params(sem=None, vmem=VMEM_LIMIT):
    return pltpu.CompilerParams(dimension_semantics=sem, vmem_limit_bytes=vmem)


def _row_tile(t, want):
    while t % want:
        want //= 2
    return want


def _place():
    return lax.axis_index("x"), lax.axis_index("y"), lax.axis_index("c")


def _gather_level(bufs, send_sems, recv_sems, level, shards=None):
    x, y, c = _place()
    me, sib = (x, y, c), (x, y, 1 - c)
    chips = [(1 - x, y), (x, 1 - y), (1 - x, 1 - y)]

    def copy(w, k, block, to):
        px, py, pc = block
        rows = bufs[w].at[4 * px + 2 * py + pc]
        src = shards[w] if shards is not None and block is me else rows
        return pltpu.make_async_remote_copy(src_ref=src, dst_ref=rows, send_sem=send_sems.at[k], recv_sem=recv_sems.at[k],
                                            device_id=to, device_id_type=MESH_ID)

    n = len(bufs)
    own = []
    if level == 1:
        own = [pltpu.make_async_copy(bufs[w].at[4 * x + 2 * y + c] if shards is None else shards[w],
                                     bufs[w].at[4 * x + 2 * y + c], send_sems.at[4 * n + w]) for w in range(n)]
    out, arriving = [], []
    for w in range(len(bufs)):
        if level == 1:
            out.append(copy(w, 4 * w, me, sib))
            arriving.append(copy(w, 4 * w, sib, me))
        for j, chip in enumerate(chips):
            if level == 1:
                out.append(copy(w, 4 * w + 1 + j, me, (*chip, c)))
                arriving.append(copy(w, 4 * w + 1 + j, (*chip, c), me))
            else:
                out.append(copy(w, 3 * w + j, (*chip, c), sib))
                arriving.append(copy(w, 3 * w + j, (*chip, 1 - c), me))
    return out, arriving, own


def _split_call(body, name, bufs, sems_in, after, n_sems_out, token, extra=()):
    n = len(bufs)
    out_shape = [pltpu.SemaphoreType.DMA((n_sems_out,))] * (2 if n_sems_out else 0)
    out_shape += [pltpu.HBM(a.shape, a.dtype) for a in bufs]
    out_specs = [SEM] * (2 if n_sems_out else 0) + [HBM] * n
    if token:
        out_shape.append(jax.ShapeDtypeStruct((8, LANES), F32))
        out_specs.append(pl.BlockSpec(memory_space=pltpu.VMEM))
    first = 2 if n_sems_out else 0
    return pl.pallas_call(
        body, name=name, out_shape=tuple(out_shape),
        in_specs=[HBM] * (n + len(extra)) + [SEM] * len(sems_in) + [ANY], out_specs=tuple(out_specs),
        input_output_aliases={i: first + i for i in range(n)},
        compiler_params=pltpu.CompilerParams(has_side_effects=SIDE_EFFECT),
    )(*bufs, *extra, *sems_in, after)


def _gather_start(shards, after, name):
    n = len(shards)
    hbm = lambda a: pltpu.with_memory_space_constraint(a, pltpu.HBM)
    bufs = [hbm(lax.empty((N_DEV,) + s.shape, s.dtype)) for s in shards]

    def body(*refs):
        out, _, own = _gather_level(refs[:n], refs[2 * n + 1], refs[2 * n + 2], 1, shards=refs[n:2 * n])
        for cp in own + out:
            cp.start()
        refs[-1][...] = jnp.zeros_like(refs[-1])

    outs = _split_call(body, name, bufs + [hbm(s) for s in shards], [], after, 5 * n, True)
    return outs[0], outs[1], list(outs[2:2 + 2 * n]), outs[-1]


def _gather_pass(send1, recv1, bufs_and_shards, after, name):
    n = len(bufs_and_shards) // 2
    bufs = bufs_and_shards

    def body(*refs):
        refs = refs[:n] + refs[2 * n:]
        out1, in1, own = _gather_level(refs[:n], refs[n], refs[n + 1], 1)
        out2, _, _ = _gather_level(refs[:n], refs[n + 3], refs[n + 4], 2)
        for cp in in1:
            cp.wait_recv()
        for cp in out2:
            cp.start()
        for cp in out1:
            cp.wait_send()
        for cp in own:
            cp.wait()
        refs[-1][...] = jnp.zeros_like(refs[-1])

    outs = _split_call(body, name, bufs, [send1, recv1], after, 3 * n, True)
    return outs[0], outs[1], list(outs[2:2 + n]), outs[-1]


def _gather_wait(send2, recv2, bufs, after, name):
    n = len(bufs)

    def body(*refs):
        out2, in2, _ = _gather_level(refs[:n], refs[n], refs[n + 1], 2)
        for cp in in2:
            cp.wait_recv()
        for cp in out2:
            cp.wait_send()

    return list(_split_call(body, name, bufs, [send2, recv2], after, 0, False))


ALL_PEERS = tuple(range(1, N_DEV))
SAME_CORE_CHIPS = (2, 4, 6)


def _scatter_copies(parts, lands, send_sems, recv_sems, relations):
    x, y, c = _place()
    ns = len(relations)
    cps = []
    for w, (part, land) in enumerate(zip(parts, lands)):
        for i, k in enumerate(relations):
            px, py, pc = x ^ ((k >> 2) & 1), y ^ ((k >> 1) & 1), c ^ (k & 1)
            block = 4 * px + 2 * py + pc if part.shape[0] == N_DEV else 2 * px + py
            cps.append(pltpu.make_async_remote_copy(
                src_ref=part.at[block], dst_ref=land.at[i],
                send_sem=send_sems.at[ns * w + i], recv_sem=recv_sems.at[ns * w + i],
                device_id=(px, py, pc), device_id_type=MESH_ID))
    return cps


def _scatter_start(parts, after, name, relations=ALL_PEERS):
    n = len(parts)
    ns = len(relations)

    def body(*refs):
        ins, lands = refs[:n], refs[n:2 * n]
        send_sems, recv_sems = refs[2 * n + 1], refs[2 * n + 2]
        token = refs[-1]
        for cp in _scatter_copies(ins, lands, send_sems, recv_sems, relations):
            cp.start()
        token[...] = jnp.zeros_like(token)

    land_shapes = [(ns,) + p.shape[1:] for p in parts]
    in_hbm = [pltpu.with_memory_space_constraint(p, pltpu.HBM) for p in parts]
    in_hbm += [pltpu.with_memory_space_constraint(lax.empty(s, p.dtype), pltpu.HBM) for s, p in zip(land_shapes, parts)]
    outs = pl.pallas_call(
        body, name=name,
        out_shape=(pltpu.SemaphoreType.DMA((ns * n,)), pltpu.SemaphoreType.DMA((ns * n,)),
                   *[pltpu.HBM(p.shape, p.dtype) for p in parts],
                   *[pltpu.HBM(s, p.dtype) for s, p in zip(land_shapes, parts)],
                   jax.ShapeDtypeStruct((8, LANES), F32)),
        in_specs=[HBM] * (2 * n) + [ANY],
        out_specs=(SEM, SEM, *[HBM] * (2 * n), pl.BlockSpec(memory_space=pltpu.VMEM)),
        input_output_aliases={i: 2 + i for i in range(2 * n)},
        compiler_params=pltpu.CompilerParams(has_side_effects=SIDE_EFFECT),
    )(*in_hbm, after)
    return outs[0], outs[1], list(outs[2:2 + n]), list(outs[2 + n:2 + 2 * n]), outs[-1]


def _scatter_wait(send_sems, recv_sems, parts, lands, after, name, relations=ALL_PEERS):
    n = len(parts)

    def body(*refs):
        ins, lnd = refs[:n], refs[n:2 * n]
        for cp in _scatter_copies(ins, lnd, refs[2 * n], refs[2 * n + 1], relations):
            cp.wait_send()
            cp.wait_recv()

    outs = pl.pallas_call(
        body, name=name,
        out_shape=tuple(pltpu.HBM(a.shape, a.dtype) for a in parts + lands),
        in_specs=[HBM] * (2 * n) + [SEM, SEM, ANY],
        out_specs=tuple([HBM] * (2 * n)),
        input_output_aliases={i: i for i in range(2 * n)},
        compiler_params=pltpu.CompilerParams(has_side_effects=SIDE_EFFECT),
    )(*parts, *lands, send_sems, recv_sems, after)
    return list(outs[n:])


def _pair_exchange(parts, name):
    n = len(parts)

    def body(*refs):
        ins, outs = refs[:n], refs[n:2 * n]
        send_sems, recv_sems = refs[2 * n:]
        x, y, c = _place()
        cps = [pltpu.make_async_remote_copy(
            src_ref=ins[w].at[:, pl.ds(1 - c, 1)], dst_ref=outs[w], send_sem=send_sems.at[w], recv_sem=recv_sems.at[w],
            device_id=(x, y, 1 - c), device_id_type=MESH_ID) for w in range(n)]
        for cp in cps:
            cp.start()
        for cp in cps:
            cp.wait()

    return pl.pallas_call(
        body, name=name,
        out_shape=[jax.ShapeDtypeStruct((4, 1) + p.shape[2:], p.dtype) for p in parts],
        in_specs=[ANY] * n, out_specs=[ANY] * n,
        scratch_shapes=[pltpu.SemaphoreType.DMA((n,)), pltpu.SemaphoreType.DMA((n,))],
    )(*parts)


def _pair_sum(g8, r1, me, name):
    _, r, c = g8.shape
    tr = max(q for q in range(16, r + 1, 16) if r % q == 0 and q * c <= ADAM_TILE_ELEMS)

    def body(me_ref, g_ref, r_ref, o_ref):
        o_ref[...] = (g_ref[...] + r_ref[...].astype(F32)).astype(BF16)

    chip = lambda k, s: s[1] ^ (k + 1)
    return pl.pallas_call(
        body, name=name,
        out_shape=jax.ShapeDtypeStruct((4, r, c), BF16),
        grid_spec=pltpu.PrefetchScalarGridSpec(
            num_scalar_prefetch=1, grid=(3, r // tr),
            in_specs=[pl.BlockSpec((None, None, tr, c), lambda k, i, s: (chip(k, s), s[0] % 2, i, 0)),
                      pl.BlockSpec((None, tr, c), lambda k, i, s: (chip(k, s), i, 0))],
            out_specs=pl.BlockSpec((None, tr, c), lambda k, i, s: (chip(k, s), i, 0))),
        compiler_params=_params(("arbitrary", "arbitrary")),
    )(me, g8.reshape((4, 2) + g8.shape[1:]), r1)


def _adam(w, g, m, v):
    m2 = ADAM_B1 * m + (1.0 - ADAM_B1) * g
    v2 = ADAM_B2 * v + (1.0 - ADAM_B2) * (g * g)
    m_hat = m2 / (1.0 - ADAM_B1 ** ADAM_STEP)
    v_hat = v2 / (1.0 - ADAM_B2 ** ADAM_STEP)
    delta = -ADAM_LR * (m_hat / (jnp.sqrt(v_hat) + ADAM_EPS) + ADAM_WD * w)
    return delta, m2, v2


def _small_allreduce_adam(part, w, m, v, name):
    rows = part.shape[0]

    def body(p_ref, w_ref, m_ref, v_ref, g_ref, d_ref, mo_ref, vo_ref, buf, send_sems, recv_sems):
        x, y, c = _place()
        buf[0] = p_ref[...]
        cps = []
        for k in range(1, N_DEV):
            kx, ky, kc = (k >> 2) & 1, (k >> 1) & 1, k & 1
            peer = (x ^ kx, y ^ ky, c ^ kc)
            cps.append(pltpu.make_async_remote_copy(
                src_ref=p_ref, dst_ref=buf.at[k], send_sem=send_sems.at[k - 1], recv_sem=recv_sems.at[k - 1],
                device_id=peer, device_id_type=MESH_ID))
        for cp in cps:
            cp.start()
        for cp in cps:
            cp.wait()
        me = 4 * x + 2 * y + c
        total = buf[me]
        for d in range(1, N_DEV):
            total = total + buf[d ^ me]
        g_ref[...] = total
        delta, m2, v2 = _adam(w_ref[...], total, m_ref[...], v_ref[...])
        d_ref[...] = delta
        mo_ref[...] = m2
        vo_ref[...] = v2

    vm = pl.BlockSpec(memory_space=pltpu.VMEM)
    return pl.pallas_call(
        body, name=name,
        out_shape=[jax.ShapeDtypeStruct(part.shape, F32)] * 4,
        in_specs=[vm] * 4, out_specs=[vm] * 4,
        scratch_shapes=[pltpu.VMEM((N_DEV, rows, LANES), F32),
                        pltpu.SemaphoreType.DMA((N_DEV - 1,)), pltpu.SemaphoreType.DMA((N_DEV - 1,))],
    )(part, w, m, v)


def _final_adam(g8, land, w, m, v, me, dep, name, pair=None):
    _, r, c = g8.shape
    tr = max(q for q in range(16, r + 1, 16) if r % q == 0 and q * c <= ADAM_TILE_ELEMS)
    nland = land.shape[0]

    def body(me_ref, g_ref, land_ref, *rest):
        pair_ref = rest[0] if pair is not None else None
        w_ref, m_ref, v_ref, _, go_ref, d_ref, mo_ref, vo_ref = rest[-8:]
        g = g_ref[...]
        if pair_ref is not None:
            g = g + pair_ref[...].astype(F32)
        for k in range(nland):
            g = g + land_ref[k].astype(F32)
        go_ref[...] = g
        delta, m2, v2 = _adam(w_ref[...], g, m_ref[...], v_ref[...])
        d_ref[...] = delta
        mo_ref[...] = m2
        vo_ref[...] = v2

    plain = pl.BlockSpec((tr, c), lambda i, s: (i, 0))
    return pl.pallas_call(
        body, name=name,
        out_shape=[jax.ShapeDtypeStruct((r, c), F32)] * 4,
        grid_spec=pltpu.PrefetchScalarGridSpec(
            num_scalar_prefetch=1, grid=(r // tr,),
            in_specs=[pl.BlockSpec((None, tr, c), lambda i, s: (s[0], i, 0)),
                      pl.BlockSpec((nland, tr, c), lambda i, s: (0, i, 0))]
            + ([] if pair is None else [pl.BlockSpec((None, tr, c), lambda i, s: (s[1], i, 0))])
            + [plain, plain, plain, ANY],
            out_specs=[plain] * 4),
        compiler_params=_params(("arbitrary",)),
    )(*((me, g8, land) + (() if pair is None else (pair,)) + (w, m, v, dep)))


def _rms(x, gain):
    r = lax.rsqrt(jnp.mean(x * x, axis=-1, keepdims=True) + EPS)
    xh = x * r
    return xh * gain, xh, r


def _rms_bwd(xh, r, gain, dy):
    gdy = gain * dy
    dx = r * (gdy - xh * jnp.mean(xh * gdy, axis=-1, keepdims=True))
    return dx, jnp.sum(dy * xh, axis=0, keepdims=True)


def _load_weights(pairs, sems):
    cps = [pltpu.make_async_copy(src, dst, sems.at[i]) for i, (src, dst) in enumerate(pairs)]
    for cp in cps:
        cp.start()
    for cp in cps:
        cp.wait()


def _ffn_fwd(h, gain, wgu, wd, name):
    t, d = h.shape
    nb, nf, _ = wgu.shape
    nh = nb // 2
    tm = _row_tile(t, 512)

    def body(h_ref, g_ref, wgu_hbm, wd_hbm, out_ref, gu_ref, wgu_v, wd_v, sems):
        @pl.when(pl.program_id(0) == 0)
        def _():
            _load_weights([(wgu_hbm, wgu_v), (wd_hbm, wd_v)], sems)

        x = h_ref[...]
        n, _, _ = _rms(x, g_ref[...])
        nbf = n.astype(BF16)
        acc = jnp.zeros((tm, d), F32)
        for j in range(nh):
            g = _dot_nt(nbf, wgu_v[j])
            u = _dot_nt(nbf, wgu_v[j + nh])
            gu_ref[j] = g.astype(BF16)
            gu_ref[j + nh] = u.astype(BF16)
            a = (g * jax.nn.sigmoid(g)) * u
            acc = acc + _dot(a.astype(BF16), wd_v[j])
        out_ref[...] = x + 0.5 * acc

    return pl.pallas_call(
        body, name=name, grid=(t // tm,),
        out_shape=[jax.ShapeDtypeStruct((t, d), F32), jax.ShapeDtypeStruct((nb, t, nf), BF16)],
        in_specs=[pl.BlockSpec((tm, d), lambda i: (i, 0)), pl.BlockSpec((1, d), lambda i: (0, 0)), ANY, ANY],
        out_specs=[pl.BlockSpec((tm, d), lambda i: (i, 0)), pl.BlockSpec((nb, tm, nf), lambda i: (0, i, 0))],
        scratch_shapes=[pltpu.VMEM(wgu.shape, BF16), pltpu.VMEM(wd.shape, BF16), pltpu.SemaphoreType.DMA((2,))],
        compiler_params=_params(("arbitrary",)),
    )(h, gain, wgu, wd)


def _ffn_up(h, gain, wgu, name):
    t, d = h.shape
    nb, nf, _ = wgu.shape
    nh = nb // 2
    tm = _row_tile(t, 512)

    def body(h_ref, g_ref, wgu_hbm, gu_ref, a_ref, wgu_v, sems):
        @pl.when(pl.program_id(0) == 0)
        def _():
            _load_weights([(wgu_hbm, wgu_v)], sems)

        n, _, _ = _rms(h_ref[...], g_ref[...])
        nbf = n.astype(BF16)
        for j in range(nh):
            g = _dot_nt(nbf, wgu_v[j])
            u = _dot_nt(nbf, wgu_v[j + nh])
            gu_ref[j] = g.astype(BF16)
            gu_ref[j + nh] = u.astype(BF16)
            a_ref[j] = ((g * jax.nn.sigmoid(g)) * u).astype(BF16)

    return pl.pallas_call(
        body, name=name, grid=(t // tm,),
        out_shape=[jax.ShapeDtypeStruct((nb, t, nf), BF16), jax.ShapeDtypeStruct((nh, t, nf), BF16)],
        in_specs=[pl.BlockSpec((tm, d), lambda i: (i, 0)), pl.BlockSpec((1, d), lambda i: (0, 0)), ANY],
        out_specs=[pl.BlockSpec((nb, tm, nf), lambda i: (0, i, 0)), pl.BlockSpec((nh, tm, nf), lambda i: (0, i, 0))],
        scratch_shapes=[pltpu.VMEM(wgu.shape, BF16), pltpu.SemaphoreType.DMA((1,))],
        compiler_params=_params(("arbitrary",)),
    )(h, gain, wgu)


def _ffn_down(h, a, wd, name):
    t, d = h.shape
    nh, nf, _ = wd.shape
    tm = _row_tile(t, 512)

    def body(h_ref, a_ref, wd_ref, out_ref):
        acc = jnp.zeros((tm, d), F32)
        for j in range(nh):
            acc = acc + _dot(a_ref[j], wd_ref[j])
        out_ref[...] = h_ref[...] + 0.5 * acc

    row = pl.BlockSpec((tm, d), lambda i: (i, 0))
    return pl.pallas_call(
        body, name=name, grid=(t // tm,),
        out_shape=jax.ShapeDtypeStruct((t, d), F32),
        in_specs=[row, pl.BlockSpec((nh, tm, nf), lambda i: (0, i, 0)), pl.BlockSpec(wd.shape, lambda i: (0, 0, 0))],
        out_specs=row,
        compiler_params=_params(("arbitrary",)),
    )(h, a, wd)


def _ffn_bwd(dh, h, gain, gu, wgu, wd, name):
    t, d = h.shape
    nb, nf, _ = wgu.shape
    nh = nb // 2
    tm = _row_tile(t, 256)

    def body(dh_ref, h_ref, g_ref, gu_ref, wgu_hbm, wd_hbm, dhp_ref, dgu_ref, a_ref, n_ref, dgain_ref,
             wgu_v, wd_v, sems):
        @pl.when(pl.program_id(0) == 0)
        def _():
            _load_weights([(wgu_hbm, wgu_v), (wd_hbm, wd_v)], sems)
            dgain_ref[...] = jnp.zeros_like(dgain_ref)

        x = h_ref[...]
        gain_v = g_ref[...]
        n, xh, r = _rms(x, gain_v)
        n_ref[...] = n.astype(BF16)
        dh_v = dh_ref[...]
        dfb = (0.5 * dh_v).astype(BF16)
        dn = jnp.zeros((tm, d), F32)
        for j in range(nh):
            da = _dot_nt(dfb, wd_v[j])
            g = gu_ref[j].astype(F32)
            u = gu_ref[j + nh].astype(F32)
            sg = jax.nn.sigmoid(g)
            si = g * sg
            dg = (da * u * (sg * (1.0 + g * (1.0 - sg)))).astype(BF16)
            du = (da * si).astype(BF16)
            a_ref[j] = (si * u).astype(BF16)
            dgu_ref[j] = dg
            dgu_ref[j + nh] = du
            dn = dn + _dot(dg, wgu_v[j]) + _dot(du, wgu_v[j + nh])
        dx, dgain = _rms_bwd(xh, r, gain_v, dn)
        dhp_ref[...] = dh_v + dx
        dgain_ref[...] += dgain

    row = pl.BlockSpec((tm, d), lambda i: (i, 0))
    vec = pl.BlockSpec((1, d), lambda i: (0, 0))
    return pl.pallas_call(
        body, name=name, grid=(t // tm,),
        out_shape=[jax.ShapeDtypeStruct((t, d), F32), jax.ShapeDtypeStruct((nb, t, nf), BF16),
                   jax.ShapeDtypeStruct((nh, t, nf), BF16), jax.ShapeDtypeStruct((t, d), BF16),
                   jax.ShapeDtypeStruct((1, d), F32)],
        in_specs=[row, row, vec, pl.BlockSpec((nb, tm, nf), lambda i: (0, i, 0)), ANY, ANY],
        out_specs=[row, pl.BlockSpec((nb, tm, nf), lambda i: (0, i, 0)),
                   pl.BlockSpec((nh, tm, nf), lambda i: (0, i, 0)), row, vec],
        scratch_shapes=[pltpu.VMEM(wgu.shape, BF16), pltpu.VMEM(wd.shape, BF16), pltpu.SemaphoreType.DMA((2,))],
        compiler_params=_params(("arbitrary",)),
    )(dh, h, gain, gu, wgu, wd)


def _dw(xa, dy, nb, n, name, scale=1.0, dep=None):
    t, k = xa.shape[-2:]
    wide = xa.ndim == 2
    tt = _row_tile(t, 1024)
    steps = t // tt
    x_spec = pl.BlockSpec((tt, k), lambda i: (i, 0)) if wide else pl.BlockSpec((nb, tt, k), lambda i: (0, i, 0))
    dy_spec = pl.BlockSpec((tt, dy.shape[1]), lambda i: (i, 0))
    acc_shape = (k, nb * n) if wide else (nb, k, n)
    stage_shape = (k, nb * n) if wide else (k, n)

    def body(x_ref, dy_ref, *rest):
        o_hbm, ob_hbm, acc, stage, sems = rest[-5:]

        @pl.when(pl.program_id(0) == 0)
        def _():
            acc[...] = jnp.zeros_like(acc)

        dyb = dy_ref[...].astype(BF16)
        if wide:
            acc[...] += _dot(x_ref[...].astype(BF16).T, dyb)
        else:
            for j in range(nb):
                acc[j] += _dot_tn(x_ref[j].astype(BF16), dyb)

        @pl.when(pl.program_id(0) == steps - 1)
        def _():
            if scale != 1.0:
                acc[...] = acc[...] * scale
            if wide:
                cps = [pltpu.make_async_copy(acc.at[:, pl.ds(j * n, n)] if nb > 1 else acc, o_hbm.at[j], sems.at[j])
                       for j in range(nb)]
            else:
                cps = [pltpu.make_async_copy(acc, o_hbm, sems.at[0])]
            for cp in cps:
                cp.start()
            if wide:
                stage[...] = acc[...].astype(BF16)
                bcs = [pltpu.make_async_copy(stage.at[:, pl.ds(j * n, n)] if nb > 1 else stage, ob_hbm.at[j],
                                             sems.at[nb + j]) for j in range(nb)]
                for cp in bcs:
                    cp.start()
                for cp in bcs:
                    cp.wait()
            else:
                for j in range(nb):
                    stage[...] = acc[j].astype(BF16)
                    cp = pltpu.make_async_copy(stage, ob_hbm.at[j], sems.at[nb])
                    cp.start()
                    cp.wait()
            for cp in cps:
                cp.wait()

    return pl.pallas_call(
        body, name=name, grid=(steps,),
        out_shape=[jax.ShapeDtypeStruct((nb, k, n), F32), jax.ShapeDtypeStruct((nb, k, n), BF16)],
        in_specs=[x_spec, dy_spec] + ([] if dep is None else [ANY]),
        out_specs=[ANY, ANY],
        scratch_shapes=[pltpu.VMEM(acc_shape, F32), pltpu.VMEM(stage_shape, BF16),
                        pltpu.SemaphoreType.DMA((2 * nb,))],
        compiler_params=_params(("arbitrary",), DW_VMEM_LIMIT),
    )(*((xa, dy) if dep is None else (xa, dy, dep)))


def _proj_fwd(h, gain, win, wgate, name):
    t, d = h.shape
    tm = _row_tile(t, 512)
    nq = win.shape[0]
    nbk, _, nc = wgate.shape
    ng = nbk * nc

    def body(h_ref, g_ref, win_ref, wg_ref, un_ref, qkv_ref, gate_ref):
        n, _, _ = _rms(h_ref[...], g_ref[...])
        nbf = n.astype(BF16)
        un_ref[...] = nbf
        qkv_ref[...] = _dot_nt(nbf, win_ref[...])
        for j in range(nbk):
            gate_ref[:, j * nc:(j + 1) * nc] = jax.nn.sigmoid(_dot(nbf, wg_ref[j])).astype(BF16)

    full = lambda a: pl.BlockSpec(a.shape, lambda i: (0,) * a.ndim)
    return pl.pallas_call(
        body, name=name, grid=(t // tm,),
        out_shape=[jax.ShapeDtypeStruct((t, d), BF16), jax.ShapeDtypeStruct((t, nq), F32),
                   jax.ShapeDtypeStruct((t, ng), BF16)],
        in_specs=[pl.BlockSpec((tm, d), lambda i: (i, 0)), full(gain), full(win), full(wgate)],
        out_specs=[pl.BlockSpec((tm, d), lambda i: (i, 0)), pl.BlockSpec((tm, nq), lambda i: (i, 0)),
                   pl.BlockSpec((tm, ng), lambda i: (i, 0))],
        compiler_params=_params(("arbitrary",)),
    )(h, gain, win, wgate)


def _proj_bwd(dh, h, gain, dzg, dqkv_parts, win, wgate, name):
    t, d = h.shape
    tm = _row_tile(t, 512)
    nbk, _, nc = wgate.shape
    ng = nbk * nc
    np_ = len(dqkv_parts)
    widths = [a.shape[1] for a in dqkv_parts]

    def body(dh_ref, h_ref, g_ref, dzg_ref, *rest):
        part_refs, (win_ref, wg_ref, dhp_ref, dgain_ref) = rest[:np_], rest[np_:]

        @pl.when(pl.program_id(0) == 0)
        def _():
            dgain_ref[...] = jnp.zeros_like(dgain_ref)

        gain_v = g_ref[...]
        _, xh, r = _rms(h_ref[...], gain_v)
        dun = jnp.zeros((tm, d), F32)
        for j in range(nbk):
            dun = dun + _dot_nt(dzg_ref[:, j * nc:(j + 1) * nc], wg_ref[j])
        off = 0
        for ref, wd in zip(part_refs, widths):
            dun = dun + _dot(ref[...].astype(BF16), win_ref[off:off + wd, :])
            off += wd
        dx, dgain = _rms_bwd(xh, r, gain_v, dun)
        dhp_ref[...] = dh_ref[...] + dx
        dgain_ref[...] += dgain

    full = lambda a: pl.BlockSpec(a.shape, lambda i: (0,) * a.ndim)
    row = pl.BlockSpec((tm, d), lambda i: (i, 0))
    return pl.pallas_call(
        body, name=name, grid=(t // tm,),
        out_shape=[jax.ShapeDtypeStruct((t, d), F32), jax.ShapeDtypeStruct((1, d), F32)],
        in_specs=[row, row, full(gain), pl.BlockSpec((tm, ng), lambda i: (i, 0))]
        + [pl.BlockSpec((tm, wd), lambda i: (i, 0)) for wd in widths] + [full(win), full(wgate)],
        out_specs=[row, pl.BlockSpec((1, d), lambda i: (0, 0))],
        compiler_params=_params(("arbitrary",)),
    )(dh, h, gain, dzg, *dqkv_parts, win, wgate)


def _dw_rows(parts, dy, name):
    t, n = dy.shape
    widths = [a.shape[1] for a in parts]
    k = sum(widths)
    tt = _row_tile(t, 1024)
    steps = t // tt
    np_ = len(parts)

    def body(*refs):
        part_refs, dy_ref = refs[:np_], refs[np_]
        o_hbm, ob_hbm, acc, stage, sems = refs[np_ + 1:]

        @pl.when(pl.program_id(0) == 0)
        def _():
            acc[...] = jnp.zeros_like(acc)

        dyb = dy_ref[...].astype(BF16)
        off = 0
        for ref, wd in zip(part_refs, widths):
            acc[off:off + wd, :] += _dot(ref[...].astype(BF16).T, dyb)
            off += wd

        @pl.when(pl.program_id(0) == steps - 1)
        def _():
            stage[...] = acc[...].astype(BF16)
            cps = [pltpu.make_async_copy(acc, o_hbm.at[0], sems.at[0]),
                   pltpu.make_async_copy(stage, ob_hbm.at[0], sems.at[1])]
            for cp in cps:
                cp.start()
            for cp in cps:
                cp.wait()

    return pl.pallas_call(
        body, name=name, grid=(steps,),
        out_shape=[jax.ShapeDtypeStruct((1, k, n), F32), jax.ShapeDtypeStruct((1, k, n), BF16)],
        in_specs=[pl.BlockSpec((tt, wd), lambda i: (i, 0)) for wd in widths] + [pl.BlockSpec((tt, n), lambda i: (i, 0))],
        out_specs=[ANY, ANY],
        scratch_shapes=[pltpu.VMEM((k, n), F32), pltpu.VMEM((k, n), BF16), pltpu.SemaphoreType.DMA((2,))],
        compiler_params=_params(("arbitrary",)),
    )(*parts, dy)


def _merge_fwd(h, ya, yb, gate, wpa, wpb, wout, name):
    t, d = h.shape
    tm = _row_tile(t, 512)

    def body(h_ref, ya_ref, yb_ref, ga_ref, gb_ref, wpa_ref, wpb_ref, wout_ref, out_ref, mg_ref, pa_ref, pb_ref):
        pa = _dot(ya_ref[...].astype(BF16), wpa_ref[...])
        pb = _dot(yb_ref[...].astype(BF16), wpb_ref[...])
        merged = (ga_ref[...].astype(F32) * pa + gb_ref[...].astype(F32) * pb).astype(BF16)
        pa_ref[...] = pa.astype(BF16)
        pb_ref[...] = pb.astype(BF16)
        mg_ref[...] = merged
        out_ref[...] = h_ref[...] + _dot(merged, wout_ref[...])

    full = lambda a: pl.BlockSpec(a.shape, lambda i: (0,) * a.ndim)
    row = pl.BlockSpec((tm, d), lambda i: (i, 0))
    yrow = pl.BlockSpec((tm, ya.shape[1]), lambda i: (i, 0))
    return pl.pallas_call(
        body, name=name, grid=(t // tm,),
        out_shape=[jax.ShapeDtypeStruct((t, d), F32)] + [jax.ShapeDtypeStruct((t, d), BF16)] * 3,
        in_specs=[row, yrow, yrow, pl.BlockSpec((tm, d), lambda i: (i, 0)), pl.BlockSpec((tm, d), lambda i: (i, 1)),
                  full(wpa), full(wpb), full(wout)],
        out_specs=[row] * 4,
        compiler_params=_params(("arbitrary",)),
    )(h, ya, yb, gate, gate, wpa, wpb, wout)


def _merge_bwd(dh, pa, pb, gate, wpa, wpb, wout, name):
    t, d = dh.shape
    tm = _row_tile(t, 512)
    wy = wpa.shape[0]

    def body(dh_ref, pa_ref, pb_ref, ga_ref, gb_ref, wpa_ref, wpb_ref, wout_ref,
             dpa_ref, dpb_ref, dzg_ref, dya_ref, dyb_ref):
        dm = _dot_nt(dh_ref[...].astype(BF16), wout_ref[...])
        ga, gb = ga_ref[...].astype(F32), gb_ref[...].astype(F32)
        dpa = (dm * ga).astype(BF16)
        dpb = (dm * gb).astype(BF16)
        dpa_ref[...] = dpa
        dpb_ref[...] = dpb
        dzg_ref[:, :d] = (dm * pa_ref[...].astype(F32) * ga * (1.0 - ga)).astype(BF16)
        dzg_ref[:, d:] = (dm * pb_ref[...].astype(F32) * gb * (1.0 - gb)).astype(BF16)
        dya_ref[...] = _dot_nt(dpa, wpa_ref[...])
        dyb_ref[...] = _dot_nt(dpb, wpb_ref[...])

    full = lambda a: pl.BlockSpec(a.shape, lambda i: (0,) * a.ndim)
    row = pl.BlockSpec((tm, d), lambda i: (i, 0))
    yrow = pl.BlockSpec((tm, wy), lambda i: (i, 0))
    return pl.pallas_call(
        body, name=name, grid=(t // tm,),
        out_shape=[jax.ShapeDtypeStruct((t, d), BF16), jax.ShapeDtypeStruct((t, d), BF16),
                   jax.ShapeDtypeStruct((t, 2 * d), BF16), jax.ShapeDtypeStruct((t, wy), F32),
                   jax.ShapeDtypeStruct((t, wy), F32)],
        in_specs=[row, row, row, pl.BlockSpec((tm, d), lambda i: (i, 0)), pl.BlockSpec((tm, d), lambda i: (i, 1)),
                  full(wpa), full(wpb), full(wout)],
        out_specs=[row, row, pl.BlockSpec((tm, 2 * d), lambda i: (i, 0)), yrow, yrow],
        compiler_params=_params(("arbitrary",)),
    )(dh, pa, pb, gate, gate, wpa, wpb, wout)


def _ple_loss(h, gain, p, target, wpg, wpe, name):
    t, d = h.shape
    tm = _row_tile(t, 512)
    pd = p.shape[1]

    def body(h_ref, g_ref, p_ref, t_ref, wpg_ref, wpe_ref, dh_ref, dz_ref, dpp_ref, n_ref, dgain_ref, loss_ref):
        @pl.when(pl.program_id(0) == 0)
        def _():
            dgain_ref[...] = jnp.zeros_like(dgain_ref)
            loss_ref[...] = jnp.zeros_like(loss_ref)

        x = h_ref[...]
        gain_v = g_ref[...]
        n, xh, r = _rms(x, gain_v)
        nbf = n.astype(BF16)
        n_ref[...] = nbf
        pg = jax.nn.sigmoid(_dot(nbf, wpg_ref[...]))
        pp = _dot(p_ref[...].astype(BF16), wpe_ref[...])
        err = (x + pg * pp) - t_ref[...]
        loss_ref[...] += 0.5 * jnp.sum(jnp.mean(err * err, axis=-1, keepdims=True))
        dy = err * (1.0 / d)
        dpp_ref[...] = (dy * pg).astype(BF16)
        dz = (dy * pp * pg * (1.0 - pg)).astype(BF16)
        dz_ref[...] = dz
        dn = _dot_nt(dz, wpg_ref[...])
        dx, dgain = _rms_bwd(xh, r, gain_v, dn)
        dh_ref[...] = dy + dx
        dgain_ref[...] += dgain

    full = lambda a: pl.BlockSpec(a.shape, lambda i: (0,) * a.ndim)
    row = pl.BlockSpec((tm, d), lambda i: (i, 0))
    return pl.pallas_call(
        body, name=name, grid=(t // tm,),
        out_shape=[jax.ShapeDtypeStruct((t, d), F32), jax.ShapeDtypeStruct((t, d), BF16),
                   jax.ShapeDtypeStruct((t, d), BF16), jax.ShapeDtypeStruct((t, d), BF16),
                   jax.ShapeDtypeStruct((1, d), F32), jax.ShapeDtypeStruct((8, LANES), F32)],
        in_specs=[row, full(gain), pl.BlockSpec((tm, pd), lambda i: (i, 0)), row, full(wpg), full(wpe)],
        out_specs=[row, row, row, row, pl.BlockSpec((1, d), lambda i: (0, 0)),
                   pl.BlockSpec((8, LANES), lambda i: (0, 0))],
        compiler_params=_params(("arbitrary",)),
    )(h, gain, p, target, wpg, wpe)


def _head_masks():
    lane = lax.broadcasted_iota(jnp.int32, (1, LANES), 1)
    m0 = (lane < HEAD_DIM).astype(F32)
    return m0, 1.0 - m0


def _head_mean(v, m0, m1):
    del m0, m1
    width = v.shape[-1]
    shift = HEAD_DIM.bit_length() - 1
    r = jnp.right_shift(lax.broadcasted_iota(jnp.int32, (width, width), 0), shift)
    c = jnp.right_shift(lax.broadcasted_iota(jnp.int32, (width, width), 1), shift)
    same_head = (r == c).astype(BF16)
    return _dot(v.astype(BF16), same_head) * (1.0 / HEAD_DIM)


def _head_norm(x, gain, m0, m1):
    r = lax.rsqrt(_head_mean(x * x, m0, m1) + EPS)
    xh = x * r
    return xh * gain, xh, r


def _head_norm_bwd(xh, r, gain, dy, m0, m1):
    gdy = gain * dy
    dx = r * (gdy - xh * _head_mean(xh * gdy, m0, m1))
    return dx, jnp.sum(dy * xh, axis=0, keepdims=True)


GROUP = 4
QW = GROUP * HEAD_DIM
STACK = GROUP * QTILE


def _kv_width(mode):
    return QW if mode == "A" else LANES


def _q_scratch_shape(mode, s_len):
    return (s_len, QW) if mode == "A" else (GROUP * s_len, LANES)


def _group_masks(dtype=F32):
    lane = lax.broadcasted_iota(jnp.int32, (1, QW), 1)
    return [((lane >= h * HEAD_DIM) & (lane < (h + 1) * HEAD_DIM)).astype(dtype) for h in range(GROUP)]


def _stack_heads(first_kv, x, m0, m1):
    out = []
    for half in range(GROUP // 2):
        xh = x[:, half * LANES:(half + 1) * LANES]
        a0, a1 = xh * m0, xh * m1
        r0, r1 = pltpu.roll(a0, HEAD_DIM, 1), pltpu.roll(a1, HEAD_DIM, 1)
        out += [jnp.where(first_kv, a0, r0), jnp.where(first_kv, r1, a1)]
    return out


def _unstack_heads(mode, first_kv, ts, m0, m1):
    if mode == "A":
        masks = _group_masks()
        return sum(t * mk for t, mk in zip(ts, masks))
    halves = []
    for half in range(GROUP // 2):
        t0 = jnp.where(first_kv, ts[2 * half], pltpu.roll(ts[2 * half], HEAD_DIM, 1))
        t1 = jnp.where(first_kv, pltpu.roll(ts[2 * half + 1], HEAD_DIM, 1), ts[2 * half + 1])
        halves.append(t0 * m0 + t1 * m1)
    return jnp.concatenate(halves, axis=1)


def _store_stacked(dst, i, heads):
    for half in range(2):
        rows = slice(half * QTILE, (half + 1) * QTILE)
        for h, x in enumerate(heads):
            dst[pl.ds((2 * i + half) * STACK + h * QTILE, QTILE), :] = x[rows].astype(dst.dtype)


def _load_stacked(mode, ref, m):
    if mode == "B":
        return ref[pl.ds(pl.multiple_of(m * STACK, STACK), STACK), :]
    x = ref[pl.ds(pl.multiple_of(m * QTILE, QTILE), QTILE), :]
    return jnp.concatenate([x * mk for mk in _group_masks(x.dtype)], axis=0)


def _attn_prep(mode, group, s_len, padk, q_ref, k_ref, v_ref, gq_ref, gk_ref, qs, k2, v2, do_ref=None, dos=None):
    m0, m1 = _head_masks()
    zpad = jnp.zeros((padk, k2.shape[1]), BF16)
    k2[pl.ds(0, padk), :] = zpad
    v2[pl.ds(0, padk), :] = zpad
    first_kv = group == 0
    rt = 2 * QTILE
    for i in range(s_len // rt):
        rows = pl.ds(i * rt, rt)
        qn, _, _ = _head_norm(q_ref[rows, :], gq_ref[...], m0, m1)
        kn, _, _ = _head_norm(k_ref[rows, :], gk_ref[...], m0, m1)
        qn = qn * (HEAD_DIM ** -0.5)
        if mode == "A":
            qs[rows, :] = qn.astype(BF16)
            if dos is not None:
                dos[rows, :] = do_ref[rows, :].astype(BF16)
        else:
            _store_stacked(qs, i, _stack_heads(first_kv, qn, m0, m1))
            if dos is not None:
                _store_stacked(dos, i, _stack_heads(first_kv, do_ref[rows, :], m0, m1))
        k2[pl.ds(padk + i * rt, rt), :] = kn.astype(BF16)
        v2[pl.ds(padk + i * rt, rt), :] = v_ref[rows, :].astype(BF16)


def _softmax_terms(mode, s, sink):
    mx = jnp.max(s, axis=-1, keepdims=True)
    if mode == "B":
        mx = jnp.maximum(mx, sink)
    e = jnp.exp(s - mx)
    l = jnp.sum(e, axis=-1, keepdims=True)
    if mode == "B":
        l = l + jnp.exp(sink - mx)
    return e, mx, l


def _sink_column(sink_ref, group):
    row = lax.broadcasted_iota(jnp.int32, (STACK, 1), 0)
    col = jnp.zeros((STACK, 1), F32)
    for h in range(GROUP):
        col = jnp.where((row >= h * QTILE) & (row < (h + 1) * QTILE), sink_ref[GROUP * group + h], col)
    return col


def _head_deltas(dd, m0, m1):
    cols = []
    for half in range(GROUP // 2):
        dh = dd[:, half * LANES:(half + 1) * LANES]
        cols += [jnp.sum(dh * m0, axis=-1, keepdims=True), jnp.sum(dh * m1, axis=-1, keepdims=True)]
    return jnp.concatenate(cols, axis=0)


def _attn_cols(mode):
    if mode == "A":
        return (lambda b, g: (b, g)), (lambda b, g: (b, 2 + g)), (lambda b, g: (b, 4 + g))
    return (lambda b, g: (b, 6 + g)), (lambda b, g: (b, 16)), (lambda b, g: (b, 17))


def _attn_fwd(mode, qkv, gq, gk, bias, sinks, bl, s_len, name):
    bw = bias.shape[-1]
    padk = bw - QTILE
    nt = s_len // QTILE
    qmap, kmap, vmap = _attn_cols(mode)

    kw = _kv_width(mode)

    def body(q_ref, k_ref, v_ref, gq_ref, gk_ref, bias_ref, sink_ref, o_ref, qs, k2, v2, s_buf, *rest):
        o_buf = rest[0] if rest else None
        group = pl.program_id(1)
        m0, m1 = _head_masks()
        first_kv = group == 0
        _attn_prep(mode, group, s_len, padk, q_ref, k_ref, v_ref, gq_ref, gk_ref, qs, k2, v2)
        col = lax.broadcasted_iota(jnp.int32, (STACK, bw), 1)
        sink = _sink_column(sink_ref, group)

        def scores(m, slot):
            r0 = pl.multiple_of(m * QTILE, QTILE)
            s = _dot_nt(_load_stacked(mode, qs, m), k2[pl.ds(r0, bw), :]) + bias_ref[...]
            s_buf[slot] = jnp.where(col >= (padk - r0), s, NEG_INF)

        def finish_tile(m, slot):
            r0 = pl.multiple_of(m * QTILE, QTILE)
            e, _, l = _softmax_terms(mode, s_buf[slot], sink)
            if mode == "A":
                o_st = _dot(e.astype(BF16), v2[pl.ds(r0, bw), :]) / l
                heads = [o_st[h * QTILE:(h + 1) * QTILE] for h in range(GROUP)]
                o_ref[pl.ds(r0, QTILE), :] = _unstack_heads(mode, first_kv, heads, m0, m1)
            else:
                o_buf[pl.ds(pl.multiple_of(m * STACK, STACK), STACK), :] = _dot((e * (1.0 / l)).astype(BF16),
                                                                                 v2[pl.ds(r0, bw), :])

        scores(0, 0)

        def pair(j, carry):
            scores(2 * j + 1, 1)
            finish_tile(2 * j, 0)
            scores(jnp.minimum(2 * j + 2, nt - 1), 0)
            finish_tile(2 * j + 1, 1)
            return carry

        lax.fori_loop(0, nt // 2, pair, 0)
        if mode == "B":
            for m in range(nt):
                heads = [o_buf[pl.ds(m * STACK + h * QTILE, QTILE), :] for h in range(GROUP)]
                o_ref[pl.ds(m * QTILE, QTILE), :] = _unstack_heads(mode, first_kv, heads, m0, m1)

    blk = lambda w, f: pl.BlockSpec((s_len, w), f)
    return pl.pallas_call(
        body, name=name, grid=(bl, B_Q_HEADS // GROUP),
        out_shape=jax.ShapeDtypeStruct((bl * s_len, B_Q_HEADS * HEAD_DIM), F32),
        in_specs=[blk(QW, qmap), blk(kw, kmap), blk(kw, vmap),
                  pl.BlockSpec((1, QW), lambda b, g: (0, 0)), pl.BlockSpec((1, kw), lambda b, g: (0, 0)),
                  pl.BlockSpec((STACK, bw), lambda b, g: (g, 0)),
                  pl.BlockSpec(memory_space=pltpu.SMEM)],
        out_specs=blk(QW, lambda b, g: (b, g)),
        scratch_shapes=[pltpu.VMEM(_q_scratch_shape(mode, s_len), BF16)] + [pltpu.VMEM((s_len + padk, kw), BF16)] * 2
        + [pltpu.VMEM((2, STACK, bw), F32)] + ([pltpu.VMEM((GROUP * s_len, LANES), F32)] if mode == "B" else []),
        compiler_params=_params(("arbitrary", "arbitrary")),
    )(qkv, qkv, qkv, gq, gk, bias.reshape(-1, bw), sinks)


def _attn_bwd(mode, qkv, gq, gk, bias, sinks, y, dy, bl, s_len, name):
    bw = bias.shape[-1]
    padk = bw - QTILE
    nt = s_len // QTILE
    qmap, kmap, vmap = _attn_cols(mode)
    t = bl * s_len
    kw = _kv_width(mode)
    kvw = 4 * LANES if mode == "A" else LANES
    dp_ahead = True

    def body(q_ref, k_ref, v_ref, gq_ref, gk_ref, bias_ref, sink_ref, y_ref, dy_ref,
             dq_ref, dk_ref, dv_ref, dgq_ref, dgk_ref, dbias_ref, dsink_ref,
             qs, k2, v2, dos, dqs, dk, dv, s_buf, dp_buf):
        group = pl.program_id(1)
        m0, m1 = _head_masks()
        first_kv = group == 0
        _attn_prep(mode, group, s_len, padk, q_ref, k_ref, v_ref, gq_ref, gk_ref, qs, k2, v2, dy_ref, dos)
        dk[...] = jnp.zeros_like(dk)
        dv[...] = jnp.zeros_like(dv)
        dbias_ref[...] = jnp.zeros_like(dbias_ref)
        col = lax.broadcasted_iota(jnp.int32, (STACK, bw), 1)
        lane8 = lax.broadcasted_iota(jnp.int32, (8, LANES), 1)
        sink = _sink_column(sink_ref, group)

        def ahead(m, slot):
            r0 = pl.multiple_of(m * QTILE, QTILE)
            band = pl.ds(r0, bw)
            s = _dot_nt(_load_stacked(mode, qs, m), k2[band, :]) + bias_ref[...]
            s_buf[slot] = jnp.where(col >= (padk - r0), s, NEG_INF)
            if dp_ahead:
                dp_buf[slot] = _dot_nt(_load_stacked(mode, dos, m), v2[band, :])

        def tile(m, slot, dsink):
            r0 = pl.multiple_of(m * QTILE, QTILE)
            rows = pl.ds(r0, QTILE)
            band = pl.ds(r0, bw)
            q_st = _load_stacked(mode, qs, m)
            do_st = _load_stacked(mode, dos, m)
            delta = _head_deltas(dy_ref[rows, :] * y_ref[rows, :], m0, m1)
            kb = k2[band, :]
            e, mx, l = _softmax_terms(mode, s_buf[slot], sink)
            inv = 1.0 / l
            pn = e * inv
            ds = pn * ((dp_buf[slot] if dp_ahead else _dot_nt(do_st, v2[band, :])) - delta)
            if mode == "A":
                dbias_ref[...] += ds
            else:
                part = jnp.exp(sink - mx) * inv * delta
                for h in range(GROUP):
                    dsink = dsink - jnp.where(lane8 == h, jnp.sum(part[h * QTILE:(h + 1) * QTILE]), 0.0)
            dsb = ds.astype(BF16)
            dv[band, :] += _dot_tn(pn.astype(BF16), do_st)
            dk[band, :] += _dot_tn(dsb, q_st)
            dq_st = _dot(dsb, kb)
            if mode == "A":
                heads = [dq_st[h * QTILE:(h + 1) * QTILE] for h in range(GROUP)]
                dq_ref[rows, :] = _unstack_heads(mode, first_kv, heads, m0, m1)
            else:
                dqs[pl.ds(pl.multiple_of(m * STACK, STACK), STACK), :] = dq_st
            return dsink

        ahead(0, 0)

        def pair(j, dsink):
            ahead(2 * j + 1, 1)
            dsink = tile(2 * j, 0, dsink)
            ahead(jnp.minimum(2 * j + 2, nt - 1), 0)
            return tile(2 * j + 1, 1, dsink)

        dsink = lax.fori_loop(0, nt // 2, pair, jnp.zeros((8, LANES), F32))
        dsink_ref[...] = dsink

        rt = 2 * QTILE
        dgq = jnp.zeros((1, QW), F32)
        dgk = jnp.zeros((1, kw), F32)
        for i in range(s_len // rt):
            rows = pl.ds(i * rt, rt)
            src = pl.ds(padk + i * rt, rt)
            gq_v, gk_v = gq_ref[...], gk_ref[...]
            _, qh, qr = _head_norm(q_ref[rows, :], gq_v, m0, m1)
            _, kh, kr = _head_norm(k_ref[rows, :], gk_v, m0, m1)
            if mode == "A":
                dqn = dq_ref[rows, :] * (HEAD_DIM ** -0.5)
            else:
                dqn = jnp.concatenate(
                    [_unstack_heads(mode, first_kv, [dqs[pl.ds((2 * i + half) * STACK + h * QTILE, QTILE), :]
                                                     for h in range(GROUP)], m0, m1)
                     for half in range(2)], axis=0) * (HEAD_DIM ** -0.5)
            dq_raw, dgq_i = _head_norm_bwd(qh, qr, gq_v, dqn, m0, m1)
            dk_raw, dgk_i = _head_norm_bwd(kh, kr, gk_v, dk[src, :], m0, m1)
            dvn = dv[src, :]
            dq_ref[rows, :] = dq_raw.astype(dq_ref.dtype)
            if mode == "A":
                dk_ref[rows, :] = dk_raw.astype(dk_ref.dtype)
                dv_ref[rows, :] = dvn.astype(dv_ref.dtype)
            else:
                @pl.when(group == 0)
                def _():
                    dk_ref[rows, :] = dk_raw
                    dv_ref[rows, :] = dvn

                @pl.when(group != 0)
                def _():
                    dk_ref[rows, :] += dk_raw
                    dv_ref[rows, :] += dvn
            dgq, dgk = dgq + dgq_i, dgk + dgk_i
        dgq_ref[...] = jnp.broadcast_to(dgq, (8, QW))
        dgk_ref[...] = jnp.broadcast_to(dgk, (8, kw))

    ng = B_Q_HEADS // GROUP
    blk = lambda w, f: pl.BlockSpec((s_len, w), f)
    small = lambda w: pl.BlockSpec((None, None, 8, w), lambda b, g: (b, g, 0, 0))
    own = lambda b, g: (b, g)
    kvmap = own if mode == "A" else (lambda b, g: (b, 0))
    pad_f32 = pltpu.VMEM((s_len + padk, kw), F32)
    pad_bf = pltpu.VMEM((s_len + padk, kw), BF16)
    stack_bf = pltpu.VMEM(_q_scratch_shape(mode, s_len), BF16)
    outs = pl.pallas_call(
        body, name=name, grid=(bl, ng),
        out_shape=[jax.ShapeDtypeStruct((t, ng * QW), F32 if mode == "A" else BF16),
                   jax.ShapeDtypeStruct((t, kvw), BF16 if mode == "A" else F32),
                   jax.ShapeDtypeStruct((t, kvw), BF16 if mode == "A" else F32),
                   jax.ShapeDtypeStruct((bl, ng, 8, QW), F32), jax.ShapeDtypeStruct((bl, ng, 8, kw), F32),
                   jax.ShapeDtypeStruct((bl, ng * STACK, bw), F32), jax.ShapeDtypeStruct((bl, ng, 8, LANES), F32)],
        in_specs=[blk(QW, qmap), blk(kw, kmap), blk(kw, vmap),
                  pl.BlockSpec((1, QW), lambda b, g: (0, 0)), pl.BlockSpec((1, kw), lambda b, g: (0, 0)),
                  pl.BlockSpec((STACK, bw), lambda b, g: (g, 0)),
                  pl.BlockSpec(memory_space=pltpu.SMEM),
                  blk(QW, own), blk(QW, own)],
        out_specs=[blk(QW, own), blk(kw, kvmap), blk(kw, kvmap), small(QW), small(kw),
                   pl.BlockSpec((None, STACK, bw), lambda b, g: (b, g, 0)), small(LANES)],
        scratch_shapes=[stack_bf, pad_bf, pad_bf, stack_bf,
                        pltpu.VMEM((8, LANES) if mode == "A" else _q_scratch_shape(mode, s_len), F32),
                        pad_f32, pad_f32, pltpu.VMEM((2, STACK, bw), F32),
                        pltpu.VMEM((2, STACK, bw) if dp_ahead else (8, LANES), F32)],
        compiler_params=_params(("arbitrary", "arbitrary")),
    )(qkv, qkv, qkv, gq, gk, bias.reshape(-1, bw), sinks, y, dy)
    outs = list(outs)
    outs[5] = outs[5].reshape(bl, B_Q_HEADS, QTILE, bw)
    return outs


def _band_geometry(prev):
    bw = QTILE + prev * CHUNK
    i = np.arange(QTILE)[:, None]
    j = np.arange(bw)[None, :]
    dist = i + prev * CHUNK - j
    valid = (j // CHUNK >= i // CHUNK) & (j // CHUNK <= i // CHUNK + prev)
    return dist, valid


A_VAR0 = (A_PREV * CHUNK - A_MAX_REL) // LANES * LANES


A_NVAR = QTILE + A_PREV * CHUNK - A_VAR0


def _skew_rows(x, sign):
    rows, n = x.shape
    row = lax.broadcasted_iota(jnp.int32, x.shape, 0)
    b = 1
    while b < rows:
        x = jnp.where((row & b) != 0, pltpu.roll(x, (sign * b) % n, 1), x)
        b *= 2
    return x


def _rel_bias_expand(table, name):
    _, valid = _band_geometry(A_PREV)
    bw = valid.shape[1]
    valid_f = jnp.asarray(valid.astype(np.float32))
    rev = jnp.flip(table[:, 1:], axis=1).reshape(A_HEADS, 1, A_NVAR)

    def body(rev_ref, valid_ref, o_ref):
        rowv = jnp.broadcast_to(rev_ref[...], (QTILE, A_NVAR))
        top = rowv[:, 0:1]
        var = _skew_rows(rowv, 1)
        row = lax.broadcasted_iota(jnp.int32, (QTILE, A_NVAR), 0)
        colv = lax.broadcasted_iota(jnp.int32, (QTILE, A_NVAR), 1)
        var = jnp.where(colv < row, top, var)
        ok = valid_ref[...] > 0.5
        o_ref[:, :A_VAR0] = jnp.where(ok[:, :A_VAR0], top, NEG_INF)
        o_ref[:, A_VAR0:] = jnp.where(ok[:, A_VAR0:], var, NEG_INF)

    return pl.pallas_call(
        body, name=name, grid=(A_HEADS,),
        out_shape=jax.ShapeDtypeStruct((A_HEADS, QTILE, bw), F32),
        in_specs=[pl.BlockSpec((None, 1, A_NVAR), lambda h: (h, 0, 0)), pl.BlockSpec((QTILE, bw), lambda h: (0, 0))],
        out_specs=pl.BlockSpec((None, QTILE, bw), lambda h: (h, 0, 0)),
        compiler_params=_params(("arbitrary",)),
    )(rev, valid_f)


def _rel_bias_grad(dbias, name):
    bl = dbias.shape[0]
    bw = dbias.shape[-1]

    def body(db_ref, o_ref):
        g = db_ref[0]
        for b in range(1, bl):
            g = g + db_ref[b]
        sk = _skew_rows(g[:, A_VAR0:], -1)
        row = lax.broadcasted_iota(jnp.int32, (QTILE, A_NVAR), 0)
        colv = lax.broadcasted_iota(jnp.int32, (QTILE, A_NVAR), 1)
        wrapped = (row + colv) >= A_NVAR
        main = jnp.sum(jnp.where(wrapped, 0.0, sk), axis=0, keepdims=True)
        top = jnp.sum(g[:, :A_VAR0]) + jnp.sum(jnp.where(wrapped, sk, 0.0))
        o_ref[:, :A_NVAR] = jnp.broadcast_to(main, (8, A_NVAR))
        o_ref[:, A_NVAR:] = jnp.full((8, LANES), top, F32)

    out = pl.pallas_call(
        body, name=name, grid=(A_HEADS,),
        out_shape=jax.ShapeDtypeStruct((A_HEADS, 8, A_NVAR + LANES), F32),
        in_specs=[pl.BlockSpec((bl, None, QTILE, bw), lambda h: (0, h, 0, 0))],
        out_specs=pl.BlockSpec((None, 8, A_NVAR + LANES), lambda h: (h, 0, 0)),
        compiler_params=_params(("arbitrary",)),
    )(dbias)
    main, top = out[:, 0, :A_NVAR], out[:, 0, A_NVAR]
    fm = jnp.flip(main, axis=1)
    return jnp.concatenate([jnp.zeros((A_HEADS, 1), F32), fm[:, :-1], fm[:, -1:] + top[:, None]], axis=1)


def _alibi_bias():
    dist, valid = _band_geometry(B_PREV)
    slopes = np.array([2.0 ** (-8.0 * (h + 1) / B_Q_HEADS) for h in range(B_Q_HEADS)], dtype=np.float32)
    bias = -slopes[:, None, None] * np.abs(dist).astype(np.float32)[None]
    return jnp.asarray(np.where(valid[None], bias, np.float32(NEG_INF)).astype(np.float32))


SMALL_NAMES = ("ffn1_norm", "mix_norm", "ffn2_norm", "ple_norm", "a_q_norm", "a_k_norm", "b_q_norm", "b_k_norm",
               "a_rel_bias", "b_sinks", "loss")


def _pack_small(vals):
    rows = []
    for nme in SMALL_NAMES:
        v = vals[nme].astype(F32)
        if nme == "a_rel_bias":
            v = jnp.pad(v.reshape(A_HEADS, -1), ((0, 0), (0, 3 * LANES - (2 * A_MAX_REL + 1))))
        v = v.reshape(-1)
        v = jnp.pad(v, (0, (-v.shape[0]) % LANES))
        rows.append(v.reshape(-1, LANES))
    out = jnp.concatenate(rows, axis=0)
    return jnp.pad(out, ((0, (-out.shape[0]) % 8), (0, 0)))


def _unpack_small(packed, shapes):
    out, r = {}, 0
    for nme in SMALL_NAMES:
        shp = shapes[nme]
        if nme == "a_rel_bias":
            nr = A_HEADS * 3
            out[nme] = packed[r:r + nr].reshape(A_HEADS, 3 * LANES)[:, :2 * A_MAX_REL + 1].reshape(shp)
        else:
            size = int(np.prod(shp)) if shp else 1
            nr = -(-size // LANES)
            out[nme] = packed[r:r + nr].reshape(-1)[:size].reshape(shp)
        r += nr
    return out


BIG_NAMES = ("ffn1_w_gu", "ffn1_w_down", "w_in", "w_gate", "w_proj_a", "w_proj_b", "w_out",
             "ffn2_w_gu", "ffn2_w_down", "w_ple_gate", "w_ple_proj")
WEIGHT_ORDER = ("ffn1_norm", "ffn1_w_gu", "ffn1_w_down", "mix_norm", "w_in", "a_q_norm", "a_k_norm", "a_rel_bias",
                "b_q_norm", "b_k_norm", "b_sinks", "w_gate", "w_proj_a", "w_proj_b", "w_out", "ffn2_norm",
                "ffn2_w_gu", "ffn2_w_down", "ple_norm", "w_ple_gate", "w_ple_proj")


TRANSPOSED = ("ffn1_w_gu", "ffn2_w_gu", "w_in")


def _local(a, nme):
    return a[0].T if nme in TRANSPOSED else a[0]


def _full_cols(wg):
    nb, k, n = wg.shape
    return jnp.transpose(wg, (1, 0, 2)).reshape(k, nb * n)


def _step(x, p, target, w, m, v):
    bl, s_len, d = x.shape
    t = bl * s_len
    h0 = x.reshape(t, d)
    pt = p.reshape(t, p.shape[-1])
    tgt = target.reshape(t, d)

    g_ffn1, g_mix, g_ffn2, g_ple = w["ffn1_norm"], w["mix_norm"], w["ffn2_norm"], w["ple_norm"]
    tiled = lambda a, width: jnp.tile(a.reshape(1, HEAD_DIM), (1, width // HEAD_DIM))
    gqa, gka = tiled(w["a_q_norm"], QW), tiled(w["a_k_norm"], _kv_width("A"))
    gqb, gkb = tiled(w["b_q_norm"], QW), tiled(w["b_k_norm"], _kv_width("B"))
    sinks = w["b_sinks"].reshape(B_Q_HEADS)
    bias_b = _alibi_bias()

    ffn1_names = ("ffn1_w_gu", "ffn1_w_down")
    shard = {nme: _local(w[nme], nme).astype(BF16) for nme in ffn1_names}
    send1, recv1, bufs, token = _gather_start([shard["ffn1_w_gu"]], h0, "gather_start_ffn1_gu")
    dsend1, drecv1, dbufs, token = _gather_start([shard["ffn1_w_down"]], token, "gather_start_ffn1_down")
    zero = token[0, 0]
    shard.update({nme: (_local(w[nme], nme) + zero).astype(BF16) for nme in BIG_NAMES if nme not in ffn1_names})
    bias_a = _rel_bias_expand(w["a_rel_bias"][0] + zero, "rel_bias_expand")
    send2, recv2, bufs, token = _gather_pass(send1, recv1, bufs, bias_a, "gather_pass_ffn1_gu")
    (wgu1,) = _gather_wait(send2, recv2, bufs, shard["ffn2_w_gu"], "gather_wait_ffn1_gu")
    nf = wgu1.shape[1]
    mixer_names = ("w_in", "w_gate")
    rest_names = ("w_proj_a", "w_proj_b", "w_out", "ffn2_w_gu", "ffn2_w_down", "w_ple_gate", "w_ple_proj")
    send1, recv1, bufs, token = _gather_start([shard[nme] for nme in mixer_names], wgu1, "gather_start_mixer")
    rsend1, rrecv1, rest_bufs, token = _gather_start([shard[nme] for nme in rest_names], token, "gather_start_rest")

    gu1, a1f = _ffn_up(h0, g_ffn1 + token[0, 0], wgu1, "ffn1_up")
    dsend2, drecv2, dbufs, token = _gather_pass(dsend1, drecv1, dbufs, a1f, "gather_pass_ffn1_down")
    (wd1,) = _gather_wait(dsend2, drecv2, dbufs, token, "gather_wait_ffn1_down")
    wd1 = wd1.reshape(N_DEV // 2, nf, d)
    h1 = _ffn_down(h0, a1f, wd1, "ffn1_down")
    send2, recv2, bufs, token = _gather_pass(send1, recv1, bufs, h1, "gather_pass_mixer")
    win, wgate = _gather_wait(send2, recv2, bufs, token, "gather_wait_mixer")
    win = win.reshape(IN_COLS, d)
    un, qkv, gate = _proj_fwd(h1, g_mix, win, wgate, "proj_fwd")
    ya = _attn_fwd("A", qkv, gqa, gka, bias_a, sinks, bl, s_len, "attn_a_fwd")
    rsend2, rrecv2, rest_bufs, token = _gather_pass(rsend1, rrecv1, rest_bufs, ya, "gather_pass_rest")
    yb = _attn_fwd("B", qkv, gqb + token[0, 0], gkb, bias_b, sinks, bl, s_len, "attn_b_fwd")
    gathered = dict(zip(rest_names, _gather_wait(rsend2, rrecv2, rest_bufs, yb, "gather_wait_rest")))
    wgu2 = gathered["ffn2_w_gu"]
    wd2 = gathered["ffn2_w_down"].reshape(N_DEV // 2, nf, d)
    wpa = _full_cols(gathered["w_proj_a"])
    wpb = _full_cols(gathered["w_proj_b"])
    wpe = _full_cols(gathered["w_ple_proj"])
    wout = gathered["w_out"].reshape(d, d)
    wpg = gathered["w_ple_gate"].reshape(d, d)
    h2, merged, pa, pb = _merge_fwd(h1, ya, yb, gate, wpa, wpb, wout, "merge_fwd")
    h3, gu2 = _ffn_fwd(h2, g_ffn2, wgu2, wd2, "ffn2_fwd")
    dh3, dz4, dpp, n4, dg_ple, loss_part = _ple_loss(h3, g_ple, pt, tgt, wpg, wpe, "ple_loss")

    xi, yi, ci = _place()
    me = jnp.stack([4 * xi + 2 * yi + ci, 2 * xi + yi]).astype(jnp.int32)
    g32, g16, big, pairs = {}, {}, {}, {}

    def keep(nme, pair, rows=None):
        for store, g in zip((g32, g16), pair):
            store[nme] = g if rows is None else g.reshape(N_DEV, rows, d)

    def start(names, after, tag):
        send, recv, parts, lands, token = _scatter_start([g16[nme] for nme in names], after, "grads_start_" + tag)
        return names, send, recv, parts, lands, token

    def start_two_level(names, after, tag):
        views = [g16[nme].reshape((4, 2) + g16[nme].shape[1:]) for nme in names]
        for nme, got in zip(names, _pair_exchange(views, "grads_pair_" + tag)):
            pairs[nme] = got.reshape((4,) + got.shape[2:])
        sums = [_pair_sum(g32[nme], pairs[nme], me, "pair_sum_" + nme) for nme in names]
        send, recv, parts, lands, token = _scatter_start(sums, after, "grads_start_" + tag, SAME_CORE_CHIPS)
        return names, send, recv, parts, lands, token

    def finish(state, after, tag):
        names, send, recv, parts, lands, _ = state
        relations = SAME_CORE_CHIPS if names[0] in pairs else ALL_PEERS
        lands = _scatter_wait(send, recv, parts, lands, after, "grads_wait_" + tag, relations)
        return names, lands

    def adam(done, dep):
        for nme, land in zip(*done):
            outs = _final_adam(g32[nme], land, _local(w[nme], nme), _local(m[nme], nme), _local(v[nme], nme), me, dep,
                               "adam_" + nme, pairs.get(nme))
            big[nme] = [(o.T if nme in TRANSPOSED else o)[None] for o in outs]

    keep("w_ple_gate", _dw(n4, dz4, 1, d, "dw_ple_gate"), d // N_DEV)
    keep("w_ple_proj", _dw(pt, dpp, N_DEV, d // N_DEV, "dw_ple_proj"))

    dh2, dgu2, a2, n3, dg_ffn2 = _ffn_bwd(dh3, h2, g_ffn2, gu2, wgu2, wd2, "ffn2_bwd")
    keep("ffn2_w_down", _dw(a2, dh3, N_DEV // 2, d, "dw_ffn2_down", 0.5), nf // 2)
    early = [(start(("w_ple_gate", "w_ple_proj", "ffn2_w_down"), dh2, "ffn2_down"), "ffn2_down")]
    keep("ffn2_w_gu", _dw(dgu2, n3, N_DEV, d, "dw_ffn2_gu", dep=early[-1][0][-1]))
    flight = start(("ffn2_w_gu",), dh2, "ffn2")

    dpa, dpb, dzg, dya, dyb = _merge_bwd(dh2, pa, pb, gate, wpa, wpb, wout, "merge_bwd")
    keep("w_out", _dw(merged, dh2, 1, d, "dw_out"), d // N_DEV)
    keep("w_proj_a", _dw(ya, dpa, N_DEV, d // N_DEV, "dw_proj_a"))
    keep("w_proj_b", _dw(yb, dpb, N_DEV, d // N_DEV, "dw_proj_b"))
    keep("w_gate", _dw(un, dzg, N_DEV, 2 * d // N_DEV, "dw_gate"))

    tok = flight[-1][0, 0]
    dqa, dka, dva, dgqa, dgka, dbias, _ = _attn_bwd("A", qkv, gqa + tok, gka, bias_a, sinks, ya, dya, bl, s_len,
                                                     "attn_a_bwd")
    dqb, dkb, dvb, dgqb, dgkb, _, dsink = _attn_bwd("B", qkv, gqb, gkb, bias_b, sinks, yb, dyb, bl, s_len, "attn_b_bwd")
    dqkv = [dqa, dka, dva, dqb, dkb, dvb]
    dtab = _rel_bias_grad(dbias, "rel_bias_grad")

    dh1, dg_mix = _proj_bwd(dh2, h1, g_mix, dzg, dqkv, win, wgate, "proj_bwd")
    keep("w_in", _dw_rows(dqkv, un, "dw_in"), IN_COLS // N_DEV)
    waiting = [finish(state, g32["w_in"], tag) for state, tag in early]
    done = finish(flight, waiting[-1][1][0], "ffn2")
    flight = start(("w_out", "w_proj_a", "w_proj_b", "w_gate", "w_in"), done[1][0], "mixer")
    waiting.append(done)

    dh0, dgu1, a1, n1, dg_ffn1 = _ffn_bwd(dh1, h0, g_ffn1 + flight[-1][0, 0], gu1, wgu1, wd1, "ffn1_bwd")
    keep("ffn1_w_down", _dw(a1, dh1, N_DEV // 2, d, "dw_ffn1_down", 0.5), nf // 2)
    done = finish(flight, g32["ffn1_w_down"], "mixer")
    flight = start(("ffn1_w_down",), done[1][0], "ffn1_down")
    waiting.append(done)

    keep("ffn1_w_gu", _dw(dgu1, n1, N_DEV, d, "dw_ffn1_gu", dep=flight[-1]))
    done = finish(flight, g32["ffn1_w_gu"], "ffn1_down")
    flight = start_two_level(("ffn1_w_gu",), done[1][0], "ffn1_gu")
    for group in waiting + [done]:
        adam(group, flight[-1])
    behind = 0.0 * big["ffn1_w_down"][0][0, 0, :1]
    smalls = (dg_ffn1, dg_mix, dg_ffn2, dg_ple + behind, dgqa, dgka, dgqb, dgkb, dtab, dsink)
    return dh0, loss_part, big, smalls, flight, finish, adam


def kernel(x, p, ffn1_norm, ffn1_w_gu, ffn1_w_down, mix_norm, w_in, a_q_norm, a_k_norm, a_rel_bias, b_q_norm, b_k_norm, b_sinks, w_gate, w_proj_a, w_proj_b, w_out, ffn2_norm, ffn2_w_gu, ffn2_w_down, ple_norm, w_ple_gate, w_ple_proj, loss_target, m_ffn1_norm, m_ffn1_w_gu, m_ffn1_w_down, m_mix_norm, m_w_in, m_a_q_norm, m_a_k_norm, m_a_rel_bias, m_b_q_norm, m_b_k_norm, m_b_sinks, m_w_gate, m_w_proj_a, m_w_proj_b, m_w_out, m_ffn2_norm, m_ffn2_w_gu, m_ffn2_w_down, m_ple_norm, m_w_ple_gate, m_w_ple_proj, v_ffn1_norm, v_ffn1_w_gu, v_ffn1_w_down, v_mix_norm, v_w_in, v_a_q_norm, v_a_k_norm, v_a_rel_bias, v_b_q_norm, v_b_k_norm, v_b_sinks, v_w_gate, v_w_proj_a, v_w_proj_b, v_w_out, v_ffn2_norm, v_ffn2_w_gu, v_ffn2_w_down, v_ple_norm, v_w_ple_gate, v_w_ple_proj):
    w = dict(ffn1_norm=ffn1_norm, ffn1_w_gu=ffn1_w_gu, ffn1_w_down=ffn1_w_down, mix_norm=mix_norm, w_in=w_in,
             a_q_norm=a_q_norm, a_k_norm=a_k_norm, a_rel_bias=a_rel_bias, b_q_norm=b_q_norm, b_k_norm=b_k_norm,
             b_sinks=b_sinks, w_gate=w_gate, w_proj_a=w_proj_a, w_proj_b=w_proj_b, w_out=w_out, ffn2_norm=ffn2_norm,
             ffn2_w_gu=ffn2_w_gu, ffn2_w_down=ffn2_w_down, ple_norm=ple_norm, w_ple_gate=w_ple_gate,
             w_ple_proj=w_ple_proj)
    m = dict(ffn1_norm=m_ffn1_norm, ffn1_w_gu=m_ffn1_w_gu, ffn1_w_down=m_ffn1_w_down, mix_norm=m_mix_norm,
             w_in=m_w_in, a_q_norm=m_a_q_norm, a_k_norm=m_a_k_norm, a_rel_bias=m_a_rel_bias, b_q_norm=m_b_q_norm,
             b_k_norm=m_b_k_norm, b_sinks=m_b_sinks, w_gate=m_w_gate, w_proj_a=m_w_proj_a, w_proj_b=m_w_proj_b,
             w_out=m_w_out, ffn2_norm=m_ffn2_norm, ffn2_w_gu=m_ffn2_w_gu, ffn2_w_down=m_ffn2_w_down,
             ple_norm=m_ple_norm, w_ple_gate=m_w_ple_gate, w_ple_proj=m_w_ple_proj)
    v = dict(ffn1_norm=v_ffn1_norm, ffn1_w_gu=v_ffn1_w_gu, ffn1_w_down=v_ffn1_w_down, mix_norm=v_mix_norm,
             w_in=v_w_in, a_q_norm=v_a_q_norm, a_k_norm=v_a_k_norm, a_rel_bias=v_a_rel_bias, b_q_norm=v_b_q_norm,
             b_k_norm=v_b_k_norm, b_sinks=v_b_sinks, w_gate=v_w_gate, w_proj_a=v_w_proj_a, w_proj_b=v_w_proj_b,
             w_out=v_w_out, ffn2_norm=v_ffn2_norm, ffn2_w_gu=v_ffn2_w_gu, ffn2_w_down=v_ffn2_w_down,
             ple_norm=v_ple_norm, w_ple_gate=v_w_ple_gate, w_ple_proj=v_w_ple_proj)
    bl, s_len, d = x.shape

    dh0, loss_part, big, smalls, flight, finish, adam = _step(x, p[0], loss_target, w, m, v)
    dg_ffn1, dg_mix, dg_ffn2, dg_ple, dgqa, dgka, dgqb, dgkb, dtab, dsink = smalls

    fold = lambda a: a[:, :, 0, :].reshape(-1, HEAD_DIM).sum(axis=0)
    small_part = dict(
        ffn1_norm=dg_ffn1, mix_norm=dg_mix, ffn2_norm=dg_ffn2, ple_norm=dg_ple,
        a_q_norm=fold(dgqa), a_k_norm=fold(dgka), b_q_norm=fold(dgqb), b_k_norm=fold(dgkb),
        a_rel_bias=dtab,
        b_sinks=dsink.sum(axis=0)[:, 0, :GROUP].reshape(B_Q_HEADS),
        loss=loss_part[0, :1])
    zero1 = jnp.zeros((1,), F32)
    shapes = {nme: w[nme].shape for nme in SMALL_NAMES if nme != "loss"}
    shapes["loss"] = ()
    pk = lambda src: _pack_small({**{nme: src[nme] for nme in SMALL_NAMES if nme != "loss"}, "loss": zero1})
    sg, sd, sm, sv = _small_allreduce_adam(_pack_small(small_part), pk(w), pk(m), pk(v), "small_allreduce_adam")
    adam(finish(flight, sg, "ffn1_gu"), sg)
    sg, sd, sm, sv = (_unpack_small(a, shapes) for a in (sg, sd, sm, sv))

    def pick(i):
        out = []
        for nme in WEIGHT_ORDER:
            out.append(big[nme][i] if nme in big else (sg, sd, sm, sv)[i][nme])
        return out

    return (sg["loss"], dh0.reshape(bl, s_len, d), *pick(0), *pick(1), *pick(2), *pick(3))
```

```python
import jax
import jax.numpy as jnp
import numpy as np
from jax import lax
from jax.experimental import pallas as pl
from jax.experimental.pallas import tpu as pltpu

F32 = jnp.float32
BF16 = jnp.bfloat16

CHUNK = 64
HEAD_DIM = 64
A_HEADS = 8
A_PREV = 8
A_MAX_REL = 128
B_Q_HEADS = 8
B_KV_HEADS = 2
B_PREV = 2
A_WIDTH = A_HEADS * HEAD_DIM
B_Q_WIDTH = B_Q_HEADS * HEAD_DIM
B_KV_WIDTH = B_KV_HEADS * HEAD_DIM
IN_COLS = 3 * A_WIDTH + B_Q_WIDTH + 2 * B_KV_WIDTH
EPS = 1e-6
NEG_INF = -1e30
ADAM_LR = 0.001
ADAM_B1 = 0.9
ADAM_B2 = 0.999
ADAM_EPS = 1e-08
ADAM_WD = 0.01
ADAM_STEP = 10

N_DEV = 8
LANES = 128
QTILE = 2 * CHUNK
VMEM_LIMIT = 56 * 1024 * 1024
DW_VMEM_LIMIT = 60 * 1024 * 1024
ADAM_TILE_ELEMS = 256 * 1024

MESH_ID = pl.DeviceIdType.MESH
ANY = pl.BlockSpec(memory_space=pl.ANY)
HBM = pl.BlockSpec(memory_space=pltpu.HBM)
SEM = pl.BlockSpec(memory_space=pltpu.SEMAPHORE)
SIDE_EFFECT = pltpu.SideEffectType.DATAFLOW_SIDE_EFFECTING


def _dot(a, b):
    return jnp.dot(a, b, preferred_element_type=F32)


def _dot_nt(a, b):
    return lax.dot_general(a, b, (((1,), (1,)), ((), ())), preferred_element_type=F32)


def _dot_tn(a, b):
    return lax.dot_general(a, b, (((0,), (0,)), ((), ())), preferred_element_type=F32)


def _params(sem=None, vmem=VMEM_LIMIT):
    return pltpu.CompilerParams(dimension_semantics=sem, vmem_limit_bytes=vmem)


def _row_tile(t, want):
    while t % want:
        want //= 2
    return want


def _place():
    return lax.axis_index("x"), lax.axis_index("y"), lax.axis_index("c")


def _gather_level(bufs, send_sems, recv_sems, level, shards=None):
    x, y, c = _place()
    me, sib = (x, y, c), (x, y, 1 - c)
    chips = [(1 - x, y), (x, 1 - y), (1 - x, 1 - y)]

    def copy(w, k, block, to):
        px, py, pc = block
        rows = bufs[w].at[4 * px + 2 * py + pc]
        src = shards[w] if shards is not None and block is me else rows
        return pltpu.make_async_remote_copy(src_ref=src, dst_ref=rows, send_sem=send_sems.at[k], recv_sem=recv_sems.at[k],
                                            device_id=to, device_id_type=MESH_ID)

    n = len(bufs)
    own = []
    if level == 1:
        own = [pltpu.make_async_copy(bufs[w].at[4 * x + 2 * y + c] if shards is None else shards[w],
                                     bufs[w].at[4 * x + 2 * y + c], send_sems.at[4 * n + w]) for w in range(n)]
    out, arriving = [], []
    for w in range(len(bufs)):
        if level == 1:
            out.append(copy(w, 4 * w, me, sib))
            arriving.append(copy(w, 4 * w, sib, me))
        for j, chip in enumerate(chips):
            if level == 1:
                out.append(copy(w, 4 * w + 1 + j, me, (*chip, c)))
                arriving.append(copy(w, 4 * w + 1 + j, (*chip, c), me))
            else:
                out.append(copy(w, 3 * w + j, (*chip, c), sib))
                arriving.append(copy(w, 3 * w + j, (*chip, 1 - c), me))
    return out, arriving, own


def _split_call(body, name, bufs, sems_in, after, n_sems_out, token, extra=()):
    n = len(bufs)
    out_shape = [pltpu.SemaphoreType.DMA((n_sems_out,))] * (2 if n_sems_out else 0)
    out_shape += [pltpu.HBM(a.shape, a.dtype) for a in bufs]
    out_specs = [SEM] * (2 if n_sems_out else 0) + [HBM] * n
    if token:
        out_shape.append(jax.ShapeDtypeStruct((8, LANES), F32))
        out_specs.append(pl.BlockSpec(memory_space=pltpu.VMEM))
    first = 2 if n_sems_out else 0
    return pl.pallas_call(
        body, name=name, out_shape=tuple(out_shape),
        in_specs=[HBM] * (n + len(extra)) + [SEM] * len(sems_in) + [ANY], out_specs=tuple(out_specs),
        input_output_aliases={i: first + i for i in range(n)},
        compiler_params=pltpu.CompilerParams(has_side_effects=SIDE_EFFECT),
    )(*bufs, *extra, *sems_in, after)


def _gather_start(shards, after, name):
    n = len(shards)
    hbm = lambda a: pltpu.with_memory_space_constraint(a, pltpu.HBM)
    bufs = [hbm(lax.empty((N_DEV,) + s.shape, s.dtype)) for s in shards]

    def body(*refs):
        out, _, own = _gather_level(refs[:n], refs[2 * n + 1], refs[2 * n + 2], 1, shards=refs[n:2 * n])
        for cp in own + out:
            cp.start()
        refs[-1][...] = jnp.zeros_like(refs[-1])

    outs = _split_call(body, name, bufs + [hbm(s) for s in shards], [], after, 5 * n, True)
    return outs[0], outs[1], list(outs[2:2 + 2 * n]), outs[-1]


def _gather_pass(send1, recv1, bufs_and_shards, after, name):
    n = len(bufs_and_shards) // 2
    bufs = bufs_and_shards

    def body(*refs):
        refs = refs[:n] + refs[2 * n:]
        out1, in1, own = _gather_level(refs[:n], refs[n], refs[n + 1], 1)
        out2, _, _ = _gather_level(refs[:n], refs[n + 3], refs[n + 4], 2)
        for cp in in1:
            cp.wait_recv()
        for cp in out2:
            cp.start()
        for cp in out1:
            cp.wait_send()
        for cp in own:
            cp.wait()
        refs[-1][...] = jnp.zeros_like(refs[-1])

    outs = _split_call(body, name, bufs, [send1, recv1], after, 3 * n, True)
    return outs[0], outs[1], list(outs[2:2 + n]), outs[-1]


def _gather_wait(send2, recv2, bufs, after, name):
    n = len(bufs)

    def body(*refs):
        out2, in2, _ = _gather_level(refs[:n], refs[n], refs[n + 1], 2)
        for cp in in2:
            cp.wait_recv()
        for cp in out2:
            cp.wait_send()

    return list(_split_call(body, name, bufs, [send2, recv2], after, 0, False))


ALL_PEERS = tuple(range(1, N_DEV))
SAME_CORE_CHIPS = (2, 4, 6)


def _scatter_copies(parts, lands, send_sems, recv_sems, relations):
    x, y, c = _place()
    ns = len(relations)
    cps = []
    for w, (part, land) in enumerate(zip(parts, lands)):
        for i, k in enumerate(relations):
            px, py, pc = x ^ ((k >> 2) & 1), y ^ ((k >> 1) & 1), c ^ (k & 1)
            block = 4 * px + 2 * py + pc if part.shape[0] == N_DEV else 2 * px + py
            cps.append(pltpu.make_async_remote_copy(
                src_ref=part.at[block], dst_ref=land.at[i],
                send_sem=send_sems.at[ns * w + i], recv_sem=recv_sems.at[ns * w + i],
                device_id=(px, py, pc), device_id_type=MESH_ID))
    return cps


def _scatter_start(parts, after, name, relations=ALL_PEERS):
    n = len(parts)
    ns = len(relations)

    def body(*refs):
        ins, lands = refs[:n], refs[n:2 * n]
        send_sems, recv_sems = refs[2 * n + 1], refs[2 * n + 2]
        token = refs[-1]
        for cp in _scatter_copies(ins, lands, send_sems, recv_sems, relations):
            cp.start()
        token[...] = jnp.zeros_like(token)

    land_shapes = [(ns,) + p.shape[1:] for p in parts]
    in_hbm = [pltpu.with_memory_space_constraint(p, pltpu.HBM) for p in parts]
    in_hbm += [pltpu.with_memory_space_constraint(lax.empty(s, p.dtype), pltpu.HBM) for s, p in zip(land_shapes, parts)]
    outs = pl.pallas_call(
        body, name=name,
        out_shape=(pltpu.SemaphoreType.DMA((ns * n,)), pltpu.SemaphoreType.DMA((ns * n,)),
                   *[pltpu.HBM(p.shape, p.dtype) for p in parts],
                   *[pltpu.HBM(s, p.dtype) for s, p in zip(land_shapes, parts)],
                   jax.ShapeDtypeStruct((8, LANES), F32)),
        in_specs=[HBM] * (2 * n) + [ANY],
        out_specs=(SEM, SEM, *[HBM] * (2 * n), pl.BlockSpec(memory_space=pltpu.VMEM)),
        input_output_aliases={i: 2 + i for i in range(2 * n)},
        compiler_params=pltpu.CompilerParams(has_side_effects=SIDE_EFFECT),
    )(*in_hbm, after)
    return outs[0], outs[1], list(outs[2:2 + n]), list(outs[2 + n:2 + 2 * n]), outs[-1]


def _scatter_wait(send_sems, recv_sems, parts, lands, after, name, relations=ALL_PEERS):
    n = len(parts)

    def body(*refs):
        ins, lnd = refs[:n], refs[n:2 * n]
        for cp in _scatter_copies(ins, lnd, refs[2 * n], refs[2 * n + 1], relations):
            cp.wait_send()
            cp.wait_recv()

    outs = pl.pallas_call(
        body, name=name,
        out_shape=tuple(pltpu.HBM(a.shape, a.dtype) for a in parts + lands),
        in_specs=[HBM] * (2 * n) + [SEM, SEM, ANY],
        out_specs=tuple([HBM] * (2 * n)),
        input_output_aliases={i: i for i in range(2 * n)},
        compiler_params=pltpu.CompilerParams(has_side_effects=SIDE_EFFECT),
    )(*parts, *lands, send_sems, recv_sems, after)
    return list(outs[n:])


def _pair_exchange(parts, name):
    n = len(parts)

    def body(*refs):
        ins, outs = refs[:n], refs[n:2 * n]
        send_sems, recv_sems = refs[2 * n:]
        x, y, c = _place()
        cps = [pltpu.make_async_remote_copy(
            src_ref=ins[w].at[:, pl.ds(1 - c, 1)], dst_ref=outs[w], send_sem=send_sems.at[w], recv_sem=recv_sems.at[w],
            device_id=(x, y, 1 - c), device_id_type=MESH_ID) for w in range(n)]
        for cp in cps:
            cp.start()
        for cp in cps:
            cp.wait()

    return pl.pallas_call(
        body, name=name,
        out_shape=[jax.ShapeDtypeStruct((4, 1) + p.shape[2:], p.dtype) for p in parts],
        in_specs=[ANY] * n, out_specs=[ANY] * n,
        scratch_shapes=[pltpu.SemaphoreType.DMA((n,)), pltpu.SemaphoreType.DMA((n,))],
    )(*parts)


def _pair_sum(g8, r1, me, name):
    _, r, c = g8.shape
    tr = max(q for q in range(16, r + 1, 16) if r % q == 0 and q * c <= ADAM_TILE_ELEMS)

    def body(me_ref, g_ref, r_ref, o_ref):
        o_ref[...] = (g_ref[...] + r_ref[...].astype(F32)).astype(BF16)

    chip = lambda k, s: s[1] ^ (k + 1)
    return pl.pallas_call(
        body, name=name,
        out_shape=jax.ShapeDtypeStruct((4, r, c), BF16),
        grid_spec=pltpu.PrefetchScalarGridSpec(
            num_scalar_prefetch=1, grid=(3, r // tr),
            in_specs=[pl.BlockSpec((None, None, tr, c), lambda k, i, s: (chip(k, s), s[0] % 2, i, 0)),
                      pl.BlockSpec((None, tr, c), lambda k, i, s: (chip(k, s), i, 0))],
            out_specs=pl.BlockSpec((None, tr, c), lambda k, i, s: (chip(k, s), i, 0))),
        compiler_params=_params(("arbitrary", "arbitrary")),
    )(me, g8.reshape((4, 2) + g8.shape[1:]), r1)


def _adam(w, g, m, v):
    m2 = ADAM_B1 * m + (1.0 - ADAM_B1) * g
    v2 = ADAM_B2 * v + (1.0 - ADAM_B2) * (g * g)
    m_hat = m2 / (1.0 - ADAM_B1 ** ADAM_STEP)
    v_hat = v2 / (1.0 - ADAM_B2 ** ADAM_STEP)
    delta = -ADAM_LR * (m_hat / (jnp.sqrt(v_hat) + ADAM_EPS) + ADAM_WD * w)
    return delta, m2, v2


def _small_allreduce_adam(part, w, m, v, name):
    rows = part.shape[0]

    def body(p_ref, w_ref, m_ref, v_ref, g_ref, d_ref, mo_ref, vo_ref, buf, send_sems, recv_sems):
        x, y, c = _place()
        buf[0] = p_ref[...]
        cps = []
        for k in range(1, N_DEV):
            kx, ky, kc = (k >> 2) & 1, (k >> 1) & 1, k & 1
            peer = (x ^ kx, y ^ ky, c ^ kc)
            cps.append(pltpu.make_async_remote_copy(
                src_ref=p_ref, dst_ref=buf.at[k], send_sem=send_sems.at[k - 1], recv_sem=recv_sems.at[k - 1],
                device_id=peer, device_id_type=MESH_ID))
        for cp in cps:
            cp.start()
        for cp in cps:
            cp.wait()
        me = 4 * x + 2 * y + c
        total = buf[me]
        for d in range(1, N_DEV):
            total = total + buf[d ^ me]
        g_ref[...] = total
        delta, m2, v2 = _adam(w_ref[...], total, m_ref[...], v_ref[...])
        d_ref[...] = delta
        mo_ref[...] = m2
        vo_ref[...] = v2

    vm = pl.BlockSpec(memory_space=pltpu.VMEM)
    return pl.pallas_call(
        body, name=name,
        out_shape=[jax.ShapeDtypeStruct(part.shape, F32)] * 4,
        in_specs=[vm] * 4, out_specs=[vm] * 4,
        scratch_shapes=[pltpu.VMEM((N_DEV, rows, LANES), F32),
                        pltpu.SemaphoreType.DMA((N_DEV - 1,)), pltpu.SemaphoreType.DMA((N_DEV - 1,))],
    )(part, w, m, v)


def _final_adam(g8, land, w, m, v, me, dep, name, pair=None):
    _, r, c = g8.shape
    tr = max(q for q in range(16, r + 1, 16) if r % q == 0 and q * c <= ADAM_TILE_ELEMS)
    nland = land.shape[0]

    def body(me_ref, g_ref, land_ref, *rest):
        pair_ref = rest[0] if pair is not None else None
        w_ref, m_ref, v_ref, _, go_ref, d_ref, mo_ref, vo_ref = rest[-8:]
        g = g_ref[...]
        if pair_ref is not None:
            g = g + pair_ref[...].astype(F32)
        for k in range(nland):
            g = g + land_ref[k].astype(F32)
        go_ref[...] = g
        delta, m2, v2 = _adam(w_ref[...], g, m_ref[...], v_ref[...])
        d_ref[...] = delta
        mo_ref[...] = m2
        vo_ref[...] = v2

    plain = pl.BlockSpec((tr, c), lambda i, s: (i, 0))
    return pl.pallas_call(
        body, name=name,
        out_shape=[jax.ShapeDtypeStruct((r, c), F32)] * 4,
        grid_spec=pltpu.PrefetchScalarGridSpec(
            num_scalar_prefetch=1, grid=(r // tr,),
            in_specs=[pl.BlockSpec((None, tr, c), lambda i, s: (s[0], i, 0)),
                      pl.BlockSpec((nland, tr, c), lambda i, s: (0, i, 0))]
            + ([] if pair is None else [pl.BlockSpec((None, tr, c), lambda i, s: (s[1], i, 0))])
            + [plain, plain, plain, ANY],
            out_specs=[plain] * 4),
        compiler_params=_params(("arbitrary",)),
    )(*((me, g8, land) + (() if pair is None else (pair,)) + (w, m, v, dep)))


def _rms(x, gain):
    r = lax.rsqrt(jnp.mean(x * x, axis=-1, keepdims=True) + EPS)
    xh = x * r
    return xh * gain, xh, r


def _rms_bwd(xh, r, gain, dy):
    gdy = gain * dy
    dx = r * (gdy - xh * jnp.mean(xh * gdy, axis=-1, keepdims=True))
    return dx, jnp.sum(dy * xh, axis=0, keepdims=True)


def _load_weights(pairs, sems):
    cps = [pltpu.make_async_copy(src, dst, sems.at[i]) for i, (src, dst) in enumerate(pairs)]
    for cp in cps:
        cp.start()
    for cp in cps:
        cp.wait()


def _ffn_fwd(h, gain, wgu, wd, name):
    t, d = h.shape
    nb, nf, _ = wgu.shape
    nh = nb // 2
    tm = _row_tile(t, 512)

    def body(h_ref, g_ref, wgu_hbm, wd_hbm, out_ref, gu_ref, wgu_v, wd_v, sems):
        @pl.when(pl.program_id(0) == 0)
        def _():
            _load_weights([(wgu_hbm, wgu_v), (wd_hbm, wd_v)], sems)

        x = h_ref[...]
        n, _, _ = _rms(x, g_ref[...])
        nbf = n.astype(BF16)
        acc = jnp.zeros((tm, d), F32)
        for j in range(nh):
            g = _dot_nt(nbf, wgu_v[j])
            u = _dot_nt(nbf, wgu_v[j + nh])
            gu_ref[j] = g.astype(BF16)
            gu_ref[j + nh] = u.astype(BF16)
            a = (g * jax.nn.sigmoid(g)) * u
            acc = acc + _dot(a.astype(BF16), wd_v[j])
        out_ref[...] = x + 0.5 * acc

    return pl.pallas_call(
        body, name=name, grid=(t // tm,),
        out_shape=[jax.ShapeDtypeStruct((t, d), F32), jax.ShapeDtypeStruct((nb, t, nf), BF16)],
        in_specs=[pl.BlockSpec((tm, d), lambda i: (i, 0)), pl.BlockSpec((1, d), lambda i: (0, 0)), ANY, ANY],
        out_specs=[pl.BlockSpec((tm, d), lambda i: (i, 0)), pl.BlockSpec((nb, tm, nf), lambda i: (0, i, 0))],
        scratch_shapes=[pltpu.VMEM(wgu.shape, BF16), pltpu.VMEM(wd.shape, BF16), pltpu.SemaphoreType.DMA((2,))],
        compiler_params=_params(("arbitrary",)),
    )(h, gain, wgu, wd)


def _ffn_up(h, gain, wgu, name):
    t, d = h.shape
    nb, nf, _ = wgu.shape
    nh = nb // 2
    tm = _row_tile(t, 512)

    def body(h_ref, g_ref, wgu_hbm, gu_ref, a_ref, wgu_v, sems):
        @pl.when(pl.program_id(0) == 0)
        def _():
            _load_weights([(wgu_hbm, wgu_v)], sems)

        n, _, _ = _rms(h_ref[...], g_ref[...])
        nbf = n.astype(BF16)
        for j in range(nh):
            g = _dot_nt(nbf, wgu_v[j])
            u = _dot_nt(nbf, wgu_v[j + nh])
            gu_ref[j] = g.astype(BF16)
            gu_ref[j + nh] = u.astype(BF16)
            a_ref[j] = ((g * jax.nn.sigmoid(g)) * u).astype(BF16)

    return pl.pallas_call(
        body, name=name, grid=(t // tm,),
        out_shape=[jax.ShapeDtypeStruct((nb, t, nf), BF16), jax.ShapeDtypeStruct((nh, t, nf), BF16)],
        in_specs=[pl.BlockSpec((tm, d), lambda i: (i, 0)), pl.BlockSpec((1, d), lambda i: (0, 0)), ANY],
        out_specs=[pl.BlockSpec((nb, tm, nf), lambda i: (0, i, 0)), pl.BlockSpec((nh, tm, nf), lambda i: (0, i, 0))],
        scratch_shapes=[pltpu.VMEM(wgu.shape, BF16), pltpu.SemaphoreType.DMA((1,))],
        compiler_params=_params(("arbitrary",)),
    )(h, gain, wgu)


def _ffn_down(h, a, wd, name):
    t, d = h.shape
    nh, nf, _ = wd.shape
    tm = _row_tile(t, 512)

    def body(h_ref, a_ref, wd_ref, out_ref):
        acc = jnp.zeros((tm, d), F32)
        for j in range(nh):
            acc = acc + _dot(a_ref[j], wd_ref[j])
        out_ref[...] = h_ref[...] + 0.5 * acc

    row = pl.BlockSpec((tm, d), lambda i: (i, 0))
    return pl.pallas_call(
        body, name=name, grid=(t // tm,),
        out_shape=jax.ShapeDtypeStruct((t, d), F32),
        in_specs=[row, pl.BlockSpec((nh, tm, nf), lambda i: (0, i, 0)), pl.BlockSpec(wd.shape, lambda i: (0, 0, 0))],
        out_specs=row,
        compiler_params=_params(("arbitrary",)),
    )(h, a, wd)


def _ffn_bwd(dh, h, gain, gu, wgu, wd, name):
    t, d = h.shape
    nb, nf, _ = wgu.shape
    nh = nb // 2
    tm = _row_tile(t, 256)

    def body(dh_ref, h_ref, g_ref, gu_ref, wgu_hbm, wd_hbm, dhp_ref, dgu_ref, a_ref, n_ref, dgain_ref,
             wgu_v, wd_v, sems):
        @pl.when(pl.program_id(0) == 0)
        def _():
            _load_weights([(wgu_hbm, wgu_v), (wd_hbm, wd_v)], sems)
            dgain_ref[...] = jnp.zeros_like(dgain_ref)

        x = h_ref[...]
        gain_v = g_ref[...]
        n, xh, r = _rms(x, gain_v)
        n_ref[...] = n.astype(BF16)
        dh_v = dh_ref[...]
        dfb = (0.5 * dh_v).astype(BF16)
        dn = jnp.zeros((tm, d), F32)
        for j in range(nh):
            da = _dot_nt(dfb, wd_v[j])
            g = gu_ref[j].astype(F32)
            u = gu_ref[j + nh].astype(F32)
            sg = jax.nn.sigmoid(g)
            si = g * sg
            dg = (da * u * (sg * (1.0 + g * (1.0 - sg)))).astype(BF16)
            du = (da * si).astype(BF16)
            a_ref[j] = (si * u).astype(BF16)
            dgu_ref[j] = dg
            dgu_ref[j + nh] = du
            dn = dn + _dot(dg, wgu_v[j]) + _dot(du, wgu_v[j + nh])
        dx, dgain = _rms_bwd(xh, r, gain_v, dn)
        dhp_ref[...] = dh_v + dx
        dgain_ref[...] += dgain

    row = pl.BlockSpec((tm, d), lambda i: (i, 0))
    vec = pl.BlockSpec((1, d), lambda i: (0, 0))
    return pl.pallas_call(
        body, name=name, grid=(t // tm,),
        out_shape=[jax.ShapeDtypeStruct((t, d), F32), jax.ShapeDtypeStruct((nb, t, nf), BF16),
                   jax.ShapeDtypeStruct((nh, t, nf), BF16), jax.ShapeDtypeStruct((t, d), BF16),
                   jax.ShapeDtypeStruct((1, d), F32)],
        in_specs=[row, row, vec, pl.BlockSpec((nb, tm, nf), lambda i: (0, i, 0)), ANY, ANY],
        out_specs=[row, pl.BlockSpec((nb, tm, nf), lambda i: (0, i, 0)),
                   pl.BlockSpec((nh, tm, nf), lambda i: (0, i, 0)), row, vec],
        scratch_shapes=[pltpu.VMEM(wgu.shape, BF16), pltpu.VMEM(wd.shape, BF16), pltpu.SemaphoreType.DMA((2,))],
        compiler_params=_params(("arbitrary",)),
    )(dh, h, gain, gu, wgu, wd)


def _dw(xa, dy, nb, n, name, scale=1.0, dep=None):
    t, k = xa.shape[-2:]
    wide = xa.ndim == 2
    tt = _row_tile(t, 1024)
    steps = t // tt
    x_spec = pl.BlockSpec((tt, k), lambda i: (i, 0)) if wide else pl.BlockSpec((nb, tt, k), lambda i: (0, i, 0))
    dy_spec = pl.BlockSpec((tt, dy.shape[1]), lambda i: (i, 0))
    acc_shape = (k, nb * n) if wide else (nb, k, n)
    stage_shape = (k, nb * n) if wide else (k, n)

    def body(x_ref, dy_ref, *rest):
        o_hbm, ob_hbm, acc, stage, sems = rest[-5:]

        @pl.when(pl.program_id(0) == 0)
        def _():
            acc[...] = jnp.zeros_like(acc)

        dyb = dy_ref[...].astype(BF16)
        if wide:
            acc[...] += _dot(x_ref[...].astype(BF16).T, dyb)
        else:
            for j in range(nb):
                acc[j] += _dot_tn(x_ref[j].astype(BF16), dyb)

        @pl.when(pl.program_id(0) == steps - 1)
        def _():
            if scale != 1.0:
                acc[...] = acc[...] * scale
            if wide:
                cps = [pltpu.make_async_copy(acc.at[:, pl.ds(j * n, n)] if nb > 1 else acc, o_hbm.at[j], sems.at[j])
                       for j in range(nb)]
            else:
                cps = [pltpu.make_async_copy(acc, o_hbm, sems.at[0])]
            for cp in cps:
                cp.start()
            if wide:
                stage[...] = acc[...].astype(BF16)
                bcs = [pltpu.make_async_copy(stage.at[:, pl.ds(j * n, n)] if nb > 1 else stage, ob_hbm.at[j],
                                             sems.at[nb + j]) for j in range(nb)]
                for cp in bcs:
                    cp.start()
                for cp in bcs:
                    cp.wait()
            else:
                for j in range(nb):
                    stage[...] = acc[j].astype(BF16)
                    cp = pltpu.make_async_copy(stage, ob_hbm.at[j], sems.at[nb])
                    cp.start()
                    cp.wait()
            for cp in cps:
                cp.wait()

    return pl.pallas_call(
        body, name=name, grid=(steps,),
        out_shape=[jax.ShapeDtypeStruct((nb, k, n), F32), jax.ShapeDtypeStruct((nb, k, n), BF16)],
        in_specs=[x_spec, dy_spec] + ([] if dep is None else [ANY]),
        out_specs=[ANY, ANY],
        scratch_shapes=[pltpu.VMEM(acc_shape, F32), pltpu.VMEM(stage_shape, BF16),
                        pltpu.SemaphoreType.DMA((2 * nb,))],
        compiler_params=_params(("arbitrary",), DW_VMEM_LIMIT),
    )(*((xa, dy) if dep is None else (xa, dy, dep)))


def _proj_fwd(h, gain, win, wgate, name):
    t, d = h.shape
    tm = _row_tile(t, 512)
    nq = win.shape[0]
    nbk, _, nc = wgate.shape
    ng = nbk * nc

    def body(h_ref, g_ref, win_ref, wg_ref, un_ref, qkv_ref, gate_ref):
        n, _, _ = _rms(h_ref[...], g_ref[...])
        nbf = n.astype(BF16)
        un_ref[...] = nbf
        qkv_ref[...] = _dot_nt(nbf, win_ref[...])
        for j in range(nbk):
            gate_ref[:, j * nc:(j + 1) * nc] = jax.nn.sigmoid(_dot(nbf, wg_ref[j])).astype(BF16)

    full = lambda a: pl.BlockSpec(a.shape, lambda i: (0,) * a.ndim)
    return pl.pallas_call(
        body, name=name, grid=(t // tm,),
        out_shape=[jax.ShapeDtypeStruct((t, d), BF16), jax.ShapeDtypeStruct((t, nq), F32),
                   jax.ShapeDtypeStruct((t, ng), BF16)],
        in_specs=[pl.BlockSpec((tm, d), lambda i: (i, 0)), full(gain), full(win), full(wgate)],
        out_specs=[pl.BlockSpec((tm, d), lambda i: (i, 0)), pl.BlockSpec((tm, nq), lambda i: (i, 0)),
                   pl.BlockSpec((tm, ng), lambda i: (i, 0))],
        compiler_params=_params(("arbitrary",)),
    )(h, gain, win, wgate)


def _proj_bwd(dh, h, gain, dzg, dqkv_parts, win, wgate, name):
    t, d = h.shape
    tm = _row_tile(t, 512)
    nbk, _, nc = wgate.shape
    ng = nbk * nc
    np_ = len(dqkv_parts)
    widths = [a.shape[1] for a in dqkv_parts]

    def body(dh_ref, h_ref, g_ref, dzg_ref, *rest):
        part_refs, (win_ref, wg_ref, dhp_ref, dgain_ref) = rest[:np_], rest[np_:]

        @pl.when(pl.program_id(0) == 0)
        def _():
            dgain_ref[...] = jnp.zeros_like(dgain_ref)

        gain_v = g_ref[...]
        _, xh, r = _rms(h_ref[...], gain_v)
        dun = jnp.zeros((tm, d), F32)
        for j in range(nbk):
            dun = dun + _dot_nt(dzg_ref[:, j * nc:(j + 1) * nc], wg_ref[j])
        off = 0
        for ref, wd in zip(part_refs, widths):
            dun = dun + _dot(ref[...].astype(BF16), win_ref[off:off + wd, :])
            off += wd
        dx, dgain = _rms_bwd(xh, r, gain_v, dun)
        dhp_ref[...] = dh_ref[...] + dx
        dgain_ref[...] += dgain

    full = lambda a: pl.BlockSpec(a.shape, lambda i: (0,) * a.ndim)
    row = pl.BlockSpec((tm, d), lambda i: (i, 0))
    return pl.pallas_call(
        body, name=name, grid=(t // tm,),
        out_shape=[jax.ShapeDtypeStruct((t, d), F32), jax.ShapeDtypeStruct((1, d), F32)],
        in_specs=[row, row, full(gain), pl.BlockSpec((tm, ng), lambda i: (i, 0))]
        + [pl.BlockSpec((tm, wd), lambda i: (i, 0)) for wd in widths] + [full(win), full(wgate)],
        out_specs=[row, pl.BlockSpec((1, d), lambda i: (0, 0))],
        compiler_params=_params(("arbitrary",)),
    )(dh, h, gain, dzg, *dqkv_parts, win, wgate)


def _dw_rows(parts, dy, name):
    t, n = dy.shape
    widths = [a.shape[1] for a in parts]
    k = sum(widths)
    tt = _row_tile(t, 1024)
    steps = t // tt
    np_ = len(parts)

    def body(*refs):
        part_refs, dy_ref = refs[:np_], refs[np_]
        o_hbm, ob_hbm, acc, stage, sems = refs[np_ + 1:]

        @pl.when(pl.program_id(0) == 0)
        def _():
            acc[...] = jnp.zeros_like(acc)

        dyb = dy_ref[...].astype(BF16)
        off = 0
        for ref, wd in zip(part_refs, widths):
            acc[off:off + wd, :] += _dot(ref[...].astype(BF16).T, dyb)
            off += wd

        @pl.when(pl.program_id(0) == steps - 1)
        def _():
            stage[...] = acc[...].astype(BF16)
            cps = [pltpu.make_async_copy(acc, o_hbm.at[0], sems.at[0]),
                   pltpu.make_async_copy(stage, ob_hbm.at[0], sems.at[1])]
            for cp in cps:
                cp.start()
            for cp in cps:
                cp.wait()

    return pl.pallas_call(
        body, name=name, grid=(steps,),
        out_shape=[jax.ShapeDtypeStruct((1, k, n), F32), jax.ShapeDtypeStruct((1, k, n), BF16)],
        in_specs=[pl.BlockSpec((tt, wd), lambda i: (i, 0)) for wd in widths] + [pl.BlockSpec((tt, n), lambda i: (i, 0))],
        out_specs=[ANY, ANY],
        scratch_shapes=[pltpu.VMEM((k, n), F32), pltpu.VMEM((k, n), BF16), pltpu.SemaphoreType.DMA((2,))],
        compiler_params=_params(("arbitrary",)),
    )(*parts, dy)


def _merge_fwd(h, ya, yb, gate, wpa, wpb, wout, name):
    t, d = h.shape
    tm = _row_tile(t, 512)

    def body(h_ref, ya_ref, yb_ref, ga_ref, gb_ref, wpa_ref, wpb_ref, wout_ref, out_ref, mg_ref, pa_ref, pb_ref):
        pa = _dot(ya_ref[...].astype(BF16), wpa_ref[...])
        pb = _dot(yb_ref[...].astype(BF16), wpb_ref[...])
        merged = (ga_ref[...].astype(F32) * pa + gb_ref[...].astype(F32) * pb).astype(BF16)
        pa_ref[...] = pa.astype(BF16)
        pb_ref[...] = pb.astype(BF16)
        mg_ref[...] = merged
        out_ref[...] = h_ref[...] + _dot(merged, wout_ref[...])

    full = lambda a: pl.BlockSpec(a.shape, lambda i: (0,) * a.ndim)
    row = pl.BlockSpec((tm, d), lambda i: (i, 0))
    yrow = pl.BlockSpec((tm, ya.shape[1]), lambda i: (i, 0))
    return pl.pallas_call(
        body, name=name, grid=(t // tm,),
        out_shape=[jax.ShapeDtypeStruct((t, d), F32)] + [jax.ShapeDtypeStruct((t, d), BF16)] * 3,
        in_specs=[row, yrow, yrow, pl.BlockSpec((tm, d), lambda i: (i, 0)), pl.BlockSpec((tm, d), lambda i: (i, 1)),
                  full(wpa), full(wpb), full(wout)],
        out_specs=[row] * 4,
        compiler_params=_params(("arbitrary",)),
    )(h, ya, yb, gate, gate, wpa, wpb, wout)


def _merge_bwd(dh, pa, pb, gate, wpa, wpb, wout, name):
    t, d = dh.shape
    tm = _row_tile(t, 512)
    wy = wpa.shape[0]

    def body(dh_ref, pa_ref, pb_ref, ga_ref, gb_ref, wpa_ref, wpb_ref, wout_ref,
             dpa_ref, dpb_ref, dzg_ref, dya_ref, dyb_ref):
        dm = _dot_nt(dh_ref[...].astype(BF16), wout_ref[...])
        ga, gb = ga_ref[...].astype(F32), gb_ref[...].astype(F32)
        dpa = (dm * ga).astype(BF16)
        dpb = (dm * gb).astype(BF16)
        dpa_ref[...] = dpa
        dpb_ref[...] = dpb
        dzg_ref[:, :d] = (dm * pa_ref[...].astype(F32) * ga * (1.0 - ga)).astype(BF16)
        dzg_ref[:, d:] = (dm * pb_ref[...].astype(F32) * gb * (1.0 - gb)).astype(BF16)
        dya_ref[...] = _dot_nt(dpa, wpa_ref[...])
        dyb_ref[...] = _dot_nt(dpb, wpb_ref[...])

    full = lambda a: pl.BlockSpec(a.shape, lambda i: (0,) * a.ndim)
    row = pl.BlockSpec((tm, d), lambda i: (i, 0))
    yrow = pl.BlockSpec((tm, wy), lambda i: (i, 0))
    return pl.pallas_call(
        body, name=name, grid=(t // tm,),
        out_shape=[jax.ShapeDtypeStruct((t, d), BF16), jax.ShapeDtypeStruct((t, d), BF16),
                   jax.ShapeDtypeStruct((t, 2 * d), BF16), jax.ShapeDtypeStruct((t, wy), F32),
                   jax.ShapeDtypeStruct((t, wy), F32)],
        in_specs=[row, row, row, pl.BlockSpec((tm, d), lambda i: (i, 0)), pl.BlockSpec((tm, d), lambda i: (i, 1)),
                  full(wpa), full(wpb), full(wout)],
        out_specs=[row, row, pl.BlockSpec((tm, 2 * d), lambda i: (i, 0)), yrow, yrow],
        compiler_params=_params(("arbitrary",)),
    )(dh, pa, pb, gate, gate, wpa, wpb, wout)


def _ple_loss(h, gain, p, target, wpg, wpe, name):
    t, d = h.shape
    tm = _row_tile(t, 512)
    pd = p.shape[1]

    def body(h_ref, g_ref, p_ref, t_ref, wpg_ref, wpe_ref, dh_ref, dz_ref, dpp_ref, n_ref, dgain_ref, loss_ref):
        @pl.when(pl.program_id(0) == 0)
        def _():
            dgain_ref[...] = jnp.zeros_like(dgain_ref)
            loss_ref[...] = jnp.zeros_like(loss_ref)

        x = h_ref[...]
        gain_v = g_ref[...]
        n, xh, r = _rms(x, gain_v)
        nbf = n.astype(BF16)
        n_ref[...] = nbf
        pg = jax.nn.sigmoid(_dot(nbf, wpg_ref[...]))
        pp = _dot(p_ref[...].astype(BF16), wpe_ref[...])
        err = (x + pg * pp) - t_ref[...]
        loss_ref[...] += 0.5 * jnp.sum(jnp.mean(err * err, axis=-1, keepdims=True))
        dy = err * (1.0 / d)
        dpp_ref[...] = (dy * pg).astype(BF16)
        dz = (dy * pp * pg * (1.0 - pg)).astype(BF16)
        dz_ref[...] = dz
        dn = _dot_nt(dz, wpg_ref[...])
        dx, dgain = _rms_bwd(xh, r, gain_v, dn)
        dh_ref[...] = dy + dx
        dgain_ref[...] += dgain

    full = lambda a: pl.BlockSpec(a.shape, lambda i: (0,) * a.ndim)
    row = pl.BlockSpec((tm, d), lambda i: (i, 0))
    return pl.pallas_call(
        body, name=name, grid=(t // tm,),
        out_shape=[jax.ShapeDtypeStruct((t, d), F32), jax.ShapeDtypeStruct((t, d), BF16),
                   jax.ShapeDtypeStruct((t, d), BF16), jax.ShapeDtypeStruct((t, d), BF16),
                   jax.ShapeDtypeStruct((1, d), F32), jax.ShapeDtypeStruct((8, LANES), F32)],
        in_specs=[row, full(gain), pl.BlockSpec((tm, pd), lambda i: (i, 0)), row, full(wpg), full(wpe)],
        out_specs=[row, row, row, row, pl.BlockSpec((1, d), lambda i: (0, 0)),
                   pl.BlockSpec((8, LANES), lambda i: (0, 0))],
        compiler_params=_params(("arbitrary",)),
    )(h, gain, p, target, wpg, wpe)


def _head_masks():
    lane = lax.broadcasted_iota(jnp.int32, (1, LANES), 1)
    m0 = (lane < HEAD_DIM).astype(F32)
    return m0, 1.0 - m0


def _head_mean(v, m0, m1):
    del m0, m1
    width = v.shape[-1]
    shift = HEAD_DIM.bit_length() - 1
    r = jnp.right_shift(lax.broadcasted_iota(jnp.int32, (width, width), 0), shift)
    c = jnp.right_shift(lax.broadcasted_iota(jnp.int32, (width, width), 1), shift)
    same_head = (r == c).astype(BF16)
    return _dot(v.astype(BF16), same_head) * (1.0 / HEAD_DIM)


def _head_norm(x, gain, m0, m1):
    r = lax.rsqrt(_head_mean(x * x, m0, m1) + EPS)
    xh = x * r
    return xh * gain, xh, r


def _head_norm_bwd(xh, r, gain, dy, m0, m1):
    gdy = gain * dy
    dx = r * (gdy - xh * _head_mean(xh * gdy, m0, m1))
    return dx, jnp.sum(dy * xh, axis=0, keepdims=True)


GROUP = 4
QW = GROUP * HEAD_DIM
STACK = GROUP * QTILE


def _kv_width(mode):
    return QW if mode == "A" else LANES


def _q_scratch_shape(mode, s_len):
    return (s_len, QW) if mode == "A" else (GROUP * s_len, LANES)


def _group_masks(dtype=F32):
    lane = lax.broadcasted_iota(jnp.int32, (1, QW), 1)
    return [((lane >= h * HEAD_DIM) & (lane < (h + 1) * HEAD_DIM)).astype(dtype) for h in range(GROUP)]


def _stack_heads(first_kv, x, m0, m1):
    out = []
    for half in range(GROUP // 2):
        xh = x[:, half * LANES:(half + 1) * LANES]
        a0, a1 = xh * m0, xh * m1
        r0, r1 = pltpu.roll(a0, HEAD_DIM, 1), pltpu.roll(a1, HEAD_DIM, 1)
        out += [jnp.where(first_kv, a0, r0), jnp.where(first_kv, r1, a1)]
    return out


def _unstack_heads(mode, first_kv, ts, m0, m1):
    if mode == "A":
        masks = _group_masks()
        return sum(t * mk for t, mk in zip(ts, masks))
    halves = []
    for half in range(GROUP // 2):
        t0 = jnp.where(first_kv, ts[2 * half], pltpu.roll(ts[2 * half], HEAD_DIM, 1))
        t1 = jnp.where(first_kv, pltpu.roll(ts[2 * half + 1], HEAD_DIM, 1), ts[2 * half + 1])
        halves.append(t0 * m0 + t1 * m1)
    return jnp.concatenate(halves, axis=1)


def _store_stacked(dst, i, heads):
    for half in range(2):
        rows = slice(half * QTILE, (half + 1) * QTILE)
        for h, x in enumerate(heads):
            dst[pl.ds((2 * i + half) * STACK + h * QTILE, QTILE), :] = x[rows].astype(dst.dtype)


def _load_stacked(mode, ref, m):
    if mode == "B":
        return ref[pl.ds(pl.multiple_of(m * STACK, STACK), STACK), :]
    x = ref[pl.ds(pl.multiple_of(m * QTILE, QTILE), QTILE), :]
    return jnp.concatenate([x * mk for mk in _group_masks(x.dtype)], axis=0)


def _attn_prep(mode, group, s_len, padk, q_ref, k_ref, v_ref, gq_ref, gk_ref, qs, k2, v2, do_ref=None, dos=None):
    m0, m1 = _head_masks()
    zpad = jnp.zeros((padk, k2.shape[1]), BF16)
    k2[pl.ds(0, padk), :] = zpad
    v2[pl.ds(0, padk), :] = zpad
    first_kv = group == 0
    rt = 2 * QTILE
    for i in range(s_len // rt):
        rows = pl.ds(i * rt, rt)
        qn, _, _ = _head_norm(q_ref[rows, :], gq_ref[...], m0, m1)
        kn, _, _ = _head_norm(k_ref[rows, :], gk_ref[...], m0, m1)
        qn = qn * (HEAD_DIM ** -0.5)
        if mode == "A":
            qs[rows, :] = qn.astype(BF16)
            if dos is not None:
                dos[rows, :] = do_ref[rows, :].astype(BF16)
        else:
            _store_stacked(qs, i, _stack_heads(first_kv, qn, m0, m1))
            if dos is not None:
                _store_stacked(dos, i, _stack_heads(first_kv, do_ref[rows, :], m0, m1))
        k2[pl.ds(padk + i * rt, rt), :] = kn.astype(BF16)
        v2[pl.ds(padk + i * rt, rt), :] = v_ref[rows, :].astype(BF16)


def _softmax_terms(mode, s, sink):
    mx = jnp.max(s, axis=-1, keepdims=True)
    if mode == "B":
        mx = jnp.maximum(mx, sink)
    e = jnp.exp(s - mx)
    l = jnp.sum(e, axis=-1, keepdims=True)
    if mode == "B":
        l = l + jnp.exp(sink - mx)
    return e, mx, l


def _sink_column(sink_ref, group):
    row = lax.broadcasted_iota(jnp.int32, (STACK, 1), 0)
    col = jnp.zeros((STACK, 1), F32)
    for h in range(GROUP):
        col = jnp.where((row >= h * QTILE) & (row < (h + 1) * QTILE), sink_ref[GROUP * group + h], col)
    return col


def _head_deltas(dd, m0, m1):
    cols = []
    for half in range(GROUP // 2):
        dh = dd[:, half * LANES:(half + 1) * LANES]
        cols += [jnp.sum(dh * m0, axis=-1, keepdims=True), jnp.sum(dh * m1, axis=-1, keepdims=True)]
    return jnp.concatenate(cols, axis=0)


def _attn_cols(mode):
    if mode == "A":
        return (lambda b, g: (b, g)), (lambda b, g: (b, 2 + g)), (lambda b, g: (b, 4 + g))
    return (lambda b, g: (b, 6 + g)), (lambda b, g: (b, 16)), (lambda b, g: (b, 17))


def _attn_fwd(mode, qkv, gq, gk, bias, sinks, bl, s_len, name):
    bw = bias.shape[-1]
    padk = bw - QTILE
    nt = s_len // QTILE
    qmap, kmap, vmap = _attn_cols(mode)

    kw = _kv_width(mode)

    def body(q_ref, k_ref, v_ref, gq_ref, gk_ref, bias_ref, sink_ref, o_ref, qs, k2, v2, s_buf, *rest):
        o_buf = rest[0] if rest else None
        group = pl.program_id(1)
        m0, m1 = _head_masks()
        first_kv = group == 0
        _attn_prep(mode, group, s_len, padk, q_ref, k_ref, v_ref, gq_ref, gk_ref, qs, k2, v2)
        col = lax.broadcasted_iota(jnp.int32, (STACK, bw), 1)
        sink = _sink_column(sink_ref, group)

        def scores(m, slot):
            r0 = pl.multiple_of(m * QTILE, QTILE)
            s = _dot_nt(_load_stacked(mode, qs, m), k2[pl.ds(r0, bw), :]) + bias_ref[...]
            s_buf[slot] = jnp.where(col >= (padk - r0), s, NEG_INF)

        def finish_tile(m, slot):
            r0 = pl.multiple_of(m * QTILE, QTILE)
            e, _, l = _softmax_terms(mode, s_buf[slot], sink)
            if mode == "A":
                o_st = _dot(e.astype(BF16), v2[pl.ds(r0, bw), :]) / l
                heads = [o_st[h * QTILE:(h + 1) * QTILE] for h in range(GROUP)]
                o_ref[pl.ds(r0, QTILE), :] = _unstack_heads(mode, first_kv, heads, m0, m1)
            else:
                o_buf[pl.ds(pl.multiple_of(m * STACK, STACK), STACK), :] = _dot((e * (1.0 / l)).astype(BF16),
                                                                                 v2[pl.ds(r0, bw), :])

        scores(0, 0)

        def pair(j, carry):
            scores(2 * j + 1, 1)
            finish_tile(2 * j, 0)
            scores(jnp.minimum(2 * j + 2, nt - 1), 0)
            finish_tile(2 * j + 1, 1)
            return carry

        lax.fori_loop(0, nt // 2, pair, 0, unroll=2)
        if mode == "B":
            for m in range(nt):
                heads = [o_buf[pl.ds(m * STACK + h * QTILE, QTILE), :] for h in range(GROUP)]
                o_ref[pl.ds(m * QTILE, QTILE), :] = _unstack_heads(mode, first_kv, heads, m0, m1)

    blk = lambda w, f: pl.BlockSpec((s_len, w), f)
    return pl.pallas_call(
        body, name=name, grid=(bl, B_Q_HEADS // GROUP),
        out_shape=jax.ShapeDtypeStruct((bl * s_len, B_Q_HEADS * HEAD_DIM), F32),
        in_specs=[blk(QW, qmap), blk(kw, kmap), blk(kw, vmap),
                  pl.BlockSpec((1, QW), lambda b, g: (0, 0)), pl.BlockSpec((1, kw), lambda b, g: (0, 0)),
                  pl.BlockSpec((STACK, bw), lambda b, g: (g, 0)),
                  pl.BlockSpec(memory_space=pltpu.SMEM)],
        out_specs=blk(QW, lambda b, g: (b, g)),
        scratch_shapes=[pltpu.VMEM(_q_scratch_shape(mode, s_len), BF16)] + [pltpu.VMEM((s_len + padk, kw), BF16)] * 2
        + [pltpu.VMEM((2, STACK, bw), F32)] + ([pltpu.VMEM((GROUP * s_len, LANES), F32)] if mode == "B" else []),
        compiler_params=_params(("arbitrary", "arbitrary")),
    )(qkv, qkv, qkv, gq, gk, bias.reshape(-1, bw), sinks)


def _attn_bwd(mode, qkv, gq, gk, bias, sinks, y, dy, bl, s_len, name):
    bw = bias.shape[-1]
    padk = bw - QTILE
    nt = s_len // QTILE
    qmap, kmap, vmap = _attn_cols(mode)
    t = bl * s_len
    kw = _kv_width(mode)
    kvw = 4 * LANES if mode == "A" else LANES
    dp_ahead = True

    def body(q_ref, k_ref, v_ref, gq_ref, gk_ref, bias_ref, sink_ref, y_ref, dy_ref,
             dq_ref, dk_ref, dv_ref, dgq_ref, dgk_ref, dbias_ref, dsink_ref,
             qs, k2, v2, dos, dqs, dk, dv, s_buf, dp_buf):
        group = pl.program_id(1)
        m0, m1 = _head_masks()
        first_kv = group == 0
        _attn_prep(mode, group, s_len, padk, q_ref, k_ref, v_ref, gq_ref, gk_ref, qs, k2, v2, dy_ref, dos)
        dk[...] = jnp.zeros_like(dk)
        dv[...] = jnp.zeros_like(dv)
        dbias_ref[...] = jnp.zeros_like(dbias_ref)
        col = lax.broadcasted_iota(jnp.int32, (STACK, bw), 1)
        lane8 = lax.broadcasted_iota(jnp.int32, (8, LANES), 1)
        sink = _sink_column(sink_ref, group)

        def ahead(m, slot):
            r0 = pl.multiple_of(m * QTILE, QTILE)
            band = pl.ds(r0, bw)
            s = _dot_nt(_load_stacked(mode, qs, m), k2[band, :]) + bias_ref[...]
            s_buf[slot] = jnp.where(col >= (padk - r0), s, NEG_INF)
            if dp_ahead:
                dp_buf[slot] = _dot_nt(_load_stacked(mode, dos, m), v2[band, :])

        def tile(m, slot, dsink):
            r0 = pl.multiple_of(m * QTILE, QTILE)
            rows = pl.ds(r0, QTILE)
            band = pl.ds(r0, bw)
            q_st = _load_stacked(mode, qs, m)
            do_st = _load_stacked(mode, dos, m)
            delta = _head_deltas(dy_ref[rows, :] * y_ref[rows, :], m0, m1)
            kb = k2[band, :]
            e, mx, l = _softmax_terms(mode, s_buf[slot], sink)
            inv = 1.0 / l
            pn = e * inv
            ds = pn * ((dp_buf[slot] if dp_ahead else _dot_nt(do_st, v2[band, :])) - delta)
            if mode == "A":
                dbias_ref[...] += ds
            else:
                part = jnp.exp(sink - mx) * inv * delta
                for h in range(GROUP):
                    dsink = dsink - jnp.where(lane8 == h, jnp.sum(part[h * QTILE:(h + 1) * QTILE]), 0.0)
            dsb = ds.astype(BF16)
            dv[band, :] += _dot_tn(pn.astype(BF16), do_st)
            dk[band, :] += _dot_tn(dsb, q_st)
            dq_st = _dot(dsb, kb)
            if mode == "A":
                heads = [dq_st[h * QTILE:(h + 1) * QTILE] for h in range(GROUP)]
                dq_ref[rows, :] = _unstack_heads(mode, first_kv, heads, m0, m1)
            else:
                dqs[pl.ds(pl.multiple_of(m * STACK, STACK), STACK), :] = dq_st
            return dsink

        ahead(0, 0)

        def pair(j, dsink):
            ahead(2 * j + 1, 1)
            dsink = tile(2 * j, 0, dsink)
            ahead(jnp.minimum(2 * j + 2, nt - 1), 0)
            return tile(2 * j + 1, 1, dsink)

        dsink = lax.fori_loop(0, nt // 2, pair, jnp.zeros((8, LANES), F32), unroll=2)
        dsink_ref[...] = dsink

        rt = 2 * QTILE
        dgq = jnp.zeros((1, QW), F32)
        dgk = jnp.zeros((1, kw), F32)
        for i in range(s_len // rt):
            rows = pl.ds(i * rt, rt)
            src = pl.ds(padk + i * rt, rt)
            gq_v, gk_v = gq_ref[...], gk_ref[...]
            _, qh, qr = _head_norm(q_ref[rows, :], gq_v, m0, m1)
            _, kh, kr = _head_norm(k_ref[rows, :], gk_v, m0, m1)
            if mode == "A":
                dqn = dq_ref[rows, :] * (HEAD_DIM ** -0.5)
            else:
                dqn = jnp.concatenate(
                    [_unstack_heads(mode, first_kv, [dqs[pl.ds((2 * i + half) * STACK + h * QTILE, QTILE), :]
                                                     for h in range(GROUP)], m0, m1)
                     for half in range(2)], axis=0) * (HEAD_DIM ** -0.5)
            dq_raw, dgq_i = _head_norm_bwd(qh, qr, gq_v, dqn, m0, m1)
            dk_raw, dgk_i = _head_norm_bwd(kh, kr, gk_v, dk[src, :], m0, m1)
            dvn = dv[src, :]
            dq_ref[rows, :] = dq_raw.astype(dq_ref.dtype)
            if mode == "A":
                dk_ref[rows, :] = dk_raw.astype(dk_ref.dtype)
                dv_ref[rows, :] = dvn.astype(dv_ref.dtype)
            else:
                @pl.when(group == 0)
                def _():
                    dk_ref[rows, :] = dk_raw
                    dv_ref[rows, :] = dvn

                @pl.when(group != 0)
                def _():
                    dk_ref[rows, :] += dk_raw
                    dv_ref[rows, :] += dvn
            dgq, dgk = dgq + dgq_i, dgk + dgk_i
        dgq_ref[...] = jnp.broadcast_to(dgq, (8, QW))
        dgk_ref[...] = jnp.broadcast_to(dgk, (8, kw))

    ng = B_Q_HEADS // GROUP
    blk = lambda w, f: pl.BlockSpec((s_len, w), f)
    small = lambda w: pl.BlockSpec((None, None, 8, w), lambda b, g: (b, g, 0, 0))
    own = lambda b, g: (b, g)
    kvmap = own if mode == "A" else (lambda b, g: (b, 0))
    pad_f32 = pltpu.VMEM((s_len + padk, kw), F32)
    pad_bf = pltpu.VMEM((s_len + padk, kw), BF16)
    stack_bf = pltpu.VMEM(_q_scratch_shape(mode, s_len), BF16)
    outs = pl.pallas_call(
        body, name=name, grid=(bl, ng),
        out_shape=[jax.ShapeDtypeStruct((t, ng * QW), F32 if mode == "A" else BF16),
                   jax.ShapeDtypeStruct((t, kvw), BF16 if mode == "A" else F32),
                   jax.ShapeDtypeStruct((t, kvw), BF16 if mode == "A" else F32),
                   jax.ShapeDtypeStruct((bl, ng, 8, QW), F32), jax.ShapeDtypeStruct((bl, ng, 8, kw), F32),
                   jax.ShapeDtypeStruct((bl, ng * STACK, bw), F32), jax.ShapeDtypeStruct((bl, ng, 8, LANES), F32)],
        in_specs=[blk(QW, qmap), blk(kw, kmap), blk(kw, vmap),
                  pl.BlockSpec((1, QW), lambda b, g: (0, 0)), pl.BlockSpec((1, kw), lambda b, g: (0, 0)),
                  pl.BlockSpec((STACK, bw), lambda b, g: (g, 0)),
                  pl.BlockSpec(memory_space=pltpu.SMEM),
                  blk(QW, own), blk(QW, own)],
        out_specs=[blk(QW, own), blk(kw, kvmap), blk(kw, kvmap), small(QW), small(kw),
                   pl.BlockSpec((None, STACK, bw), lambda b, g: (b, g, 0)), small(LANES)],
        scratch_shapes=[stack_bf, pad_bf, pad_bf, stack_bf,
                        pltpu.VMEM((8, LANES) if mode == "A" else _q_scratch_shape(mode, s_len), F32),
                        pad_f32, pad_f32, pltpu.VMEM((2, STACK, bw), F32),
                        pltpu.VMEM((2, STACK, bw) if dp_ahead else (8, LANES), F32)],
        compiler_params=_params(("arbitrary", "arbitrary")),
    )(qkv, qkv, qkv, gq, gk, bias.reshape(-1, bw), sinks, y, dy)
    outs = list(outs)
    outs[5] = outs[5].reshape(bl, B_Q_HEADS, QTILE, bw)
    return outs


def _band_geometry(prev):
    bw = QTILE + prev * CHUNK
    i = np.arange(QTILE)[:, None]
    j = np.arange(bw)[None, :]
    dist = i + prev * CHUNK - j
    valid = (j // CHUNK >= i // CHUNK) & (j // CHUNK <= i // CHUNK + prev)
    return dist, valid


A_VAR0 = (A_PREV * CHUNK - A_MAX_REL) // LANES * LANES


A_NVAR = QTILE + A_PREV * CHUNK - A_VAR0


def _skew_rows(x, sign):
    rows, n = x.shape
    row = lax.broadcasted_iota(jnp.int32, x.shape, 0)
    b = 1
    while b < rows:
        x = jnp.where((row & b) != 0, pltpu.roll(x, (sign * b) % n, 1), x)
        b *= 2
    return x


def _rel_bias_expand(table, name):
    _, valid = _band_geometry(A_PREV)
    bw = valid.shape[1]
    valid_f = jnp.asarray(valid.astype(np.float32))
    rev = jnp.flip(table[:, 1:], axis=1).reshape(A_HEADS, 1, A_NVAR)

    def body(rev_ref, valid_ref, o_ref):
        rowv = jnp.broadcast_to(rev_ref[...], (QTILE, A_NVAR))
        top = rowv[:, 0:1]
        var = _skew_rows(rowv, 1)
        row = lax.broadcasted_iota(jnp.int32, (QTILE, A_NVAR), 0)
        colv = lax.broadcasted_iota(jnp.int32, (QTILE, A_NVAR), 1)
        var = jnp.where(colv < row, top, var)
        ok = valid_ref[...] > 0.5
        o_ref[:, :A_VAR0] = jnp.where(ok[:, :A_VAR0], top, NEG_INF)
        o_ref[:, A_VAR0:] = jnp.where(ok[:, A_VAR0:], var, NEG_INF)

    return pl.pallas_call(
        body, name=name, grid=(A_HEADS,),
        out_shape=jax.ShapeDtypeStruct((A_HEADS, QTILE, bw), F32),
        in_specs=[pl.BlockSpec((None, 1, A_NVAR), lambda h: (h, 0, 0)), pl.BlockSpec((QTILE, bw), lambda h: (0, 0))],
        out_specs=pl.BlockSpec((None, QTILE, bw), lambda h: (h, 0, 0)),
        compiler_params=_params(("arbitrary",)),
    )(rev, valid_f)


def _rel_bias_grad(dbias, name):
    bl = dbias.shape[0]
    bw = dbias.shape[-1]

    def body(db_ref, o_ref):
        g = db_ref[0]
        for b in range(1, bl):
            g = g + db_ref[b]
        sk = _skew_rows(g[:, A_VAR0:], -1)
        row = lax.broadcasted_iota(jnp.int32, (QTILE, A_NVAR), 0)
        colv = lax.broadcasted_iota(jnp.int32, (QTILE, A_NVAR), 1)
        wrapped = (row + colv) >= A_NVAR
        main = jnp.sum(jnp.where(wrapped, 0.0, sk), axis=0, keepdims=True)
        top = jnp.sum(g[:, :A_VAR0]) + jnp.sum(jnp.where(wrapped, sk, 0.0))
        o_ref[:, :A_NVAR] = jnp.broadcast_to(main, (8, A_NVAR))
        o_ref[:, A_NVAR:] = jnp.full((8, LANES), top, F32)

    out = pl.pallas_call(
        body, name=name, grid=(A_HEADS,),
        out_shape=jax.ShapeDtypeStruct((A_HEADS, 8, A_NVAR + LANES), F32),
        in_specs=[pl.BlockSpec((bl, None, QTILE, bw), lambda h: (0, h, 0, 0))],
        out_specs=pl.BlockSpec((None, 8, A_NVAR + LANES), lambda h: (h, 0, 0)),
        compiler_params=_params(("arbitrary",)),
    )(dbias)
    main, top = out[:, 0, :A_NVAR], out[:, 0, A_NVAR]
    fm = jnp.flip(main, axis=1)
    return jnp.concatenate([jnp.zeros((A_HEADS, 1), F32), fm[:, :-1], fm[:, -1:] + top[:, None]], axis=1)


def _alibi_bias():
    dist, valid = _band_geometry(B_PREV)
    slopes = np.array([2.0 ** (-8.0 * (h + 1) / B_Q_HEADS) for h in range(B_Q_HEADS)], dtype=np.float32)
    bias = -slopes[:, None, None] * np.abs(dist).astype(np.float32)[None]
    return jnp.asarray(np.where(valid[None], bias, np.float32(NEG_INF)).astype(np.float32))


SMALL_NAMES = ("ffn1_norm", "mix_norm", "ffn2_norm", "ple_norm", "a_q_norm", "a_k_norm", "b_q_norm", "b_k_norm",
               "a_rel_bias", "b_sinks", "loss")


def _pack_small(vals):
    rows = []
    for nme in SMALL_NAMES:
        v = vals[nme].astype(F32)
        if nme == "a_rel_bias":
            v = jnp.pad(v.reshape(A_HEADS, -1), ((0, 0), (0, 3 * LANES - (2 * A_MAX_REL + 1))))
        v = v.reshape(-1)
        v = jnp.pad(v, (0, (-v.shape[0]) % LANES))
        rows.append(v.reshape(-1, LANES))
    out = jnp.concatenate(rows, axis=0)
    return jnp.pad(out, ((0, (-out.shape[0]) % 8), (0, 0)))


def _unpack_small(packed, shapes):
    out, r = {}, 0
    for nme in SMALL_NAMES:
        shp = shapes[nme]
        if nme == "a_rel_bias":
            nr = A_HEADS * 3
            out[nme] = packed[r:r + nr].reshape(A_HEADS, 3 * LANES)[:, :2 * A_MAX_REL + 1].reshape(shp)
        else:
            size = int(np.prod(shp)) if shp else 1
            nr = -(-size // LANES)
            out[nme] = packed[r:r + nr].reshape(-1)[:size].reshape(shp)
        r += nr
    return out


BIG_NAMES = ("ffn1_w_gu", "ffn1_w_down", "w_in", "w_gate", "w_proj_a", "w_proj_b", "w_out",
             "ffn2_w_gu", "ffn2_w_down", "w_ple_gate", "w_ple_proj")
WEIGHT_ORDER = ("ffn1_norm", "ffn1_w_gu", "ffn1_w_down", "mix_norm", "w_in", "a_q_norm", "a_k_norm", "a_rel_bias",
                "b_q_norm", "b_k_norm", "b_sinks", "w_gate", "w_proj_a", "w_proj_b", "w_out", "ffn2_norm",
                "ffn2_w_gu", "ffn2_w_down", "ple_norm", "w_ple_gate", "w_ple_proj")


TRANSPOSED = ("ffn1_w_gu", "ffn2_w_gu", "w_in")


def _local(a, nme):
    return a[0].T if nme in TRANSPOSED else a[0]


def _full_cols(wg):
    nb, k, n = wg.shape
    return jnp.transpose(wg, (1, 0, 2)).reshape(k, nb * n)


def _step(x, p, target, w, m, v):
    bl, s_len, d = x.shape
    t = bl * s_len
    h0 = x.reshape(t, d)
    pt = p.reshape(t, p.shape[-1])
    tgt = target.reshape(t, d)

    g_ffn1, g_mix, g_ffn2, g_ple = w["ffn1_norm"], w["mix_norm"], w["ffn2_norm"], w["ple_norm"]
    tiled = lambda a, width: jnp.tile(a.reshape(1, HEAD_DIM), (1, width // HEAD_DIM))
    gqa, gka = tiled(w["a_q_norm"], QW), tiled(w["a_k_norm"], _kv_width("A"))
    gqb, gkb = tiled(w["b_q_norm"], QW), tiled(w["b_k_norm"], _kv_width("B"))
    sinks = w["b_sinks"].reshape(B_Q_HEADS)
    bias_b = _alibi_bias()

    ffn1_names = ("ffn1_w_gu", "ffn1_w_down")
    shard = {nme: _local(w[nme], nme).astype(BF16) for nme in ffn1_names}
    send1, recv1, bufs, token = _gather_start([shard["ffn1_w_gu"]], h0, "gather_start_ffn1_gu")
    dsend1, drecv1, dbufs, token = _gather_start([shard["ffn1_w_down"]], token, "gather_start_ffn1_down")
    zero = token[0, 0]
    shard.update({nme: (_local(w[nme], nme) + zero).astype(BF16) for nme in BIG_NAMES if nme not in ffn1_names})
    bias_a = _rel_bias_expand(w["a_rel_bias"][0] + zero, "rel_bias_expand")
    send2, recv2, bufs, token = _gather_pass(send1, recv1, bufs, bias_a, "gather_pass_ffn1_gu")
    (wgu1,) = _gather_wait(send2, recv2, bufs, shard["ffn2_w_gu"], "gather_wait_ffn1_gu")
    nf = wgu1.shape[1]
    mixer_names = ("w_in", "w_gate")
    rest_names = ("w_proj_a", "w_proj_b", "w_out", "ffn2_w_gu", "ffn2_w_down", "w_ple_gate", "w_ple_proj")
    send1, recv1, bufs, token = _gather_start([shard[nme] for nme in mixer_names], wgu1, "gather_start_mixer")
    rsend1, rrecv1, rest_bufs, token = _gather_start([shard[nme] for nme in rest_names], token, "gather_start_rest")

    gu1, a1f = _ffn_up(h0, g_ffn1 + token[0, 0], wgu1, "ffn1_up")
    dsend2, drecv2, dbufs, token = _gather_pass(dsend1, drecv1, dbufs, a1f, "gather_pass_ffn1_down")
    (wd1,) = _gather_wait(dsend2, drecv2, dbufs, token, "gather_wait_ffn1_down")
    wd1 = wd1.reshape(N_DEV // 2, nf, d)
    h1 = _ffn_down(h0, a1f, wd1, "ffn1_down")
    send2, recv2, bufs, token = _gather_pass(send1, recv1, bufs, h1, "gather_pass_mixer")
    win, wgate = _gather_wait(send2, recv2, bufs, token, "gather_wait_mixer")
    win = win.reshape(IN_COLS, d)
    un, qkv, gate = _proj_fwd(h1, g_mix, win, wgate, "proj_fwd")
    ya = _attn_fwd("A", qkv, gqa, gka, bias_a, sinks, bl, s_len, "attn_a_fwd")
    rsend2, rrecv2, rest_bufs, token = _gather_pass(rsend1, rrecv1, rest_bufs, ya, "gather_pass_rest")
    yb = _attn_fwd("B", qkv, gqb + token[0, 0], gkb, bias_b, sinks, bl, s_len, "attn_b_fwd")
    gathered = dict(zip(rest_names, _gather_wait(rsend2, rrecv2, rest_bufs, yb, "gather_wait_rest")))
    wgu2 = gathered["ffn2_w_gu"]
    wd2 = gathered["ffn2_w_down"].reshape(N_DEV // 2, nf, d)
    wpa = _full_cols(gathered["w_proj_a"])
    wpb = _full_cols(gathered["w_proj_b"])
    wpe = _full_cols(gathered["w_ple_proj"])
    wout = gathered["w_out"].reshape(d, d)
    wpg = gathered["w_ple_gate"].reshape(d, d)
    h2, merged, pa, pb = _merge_fwd(h1, ya, yb, gate, wpa, wpb, wout, "merge_fwd")
    h3, gu2 = _ffn_fwd(h2, g_ffn2, wgu2, wd2, "ffn2_fwd")
    dh3, dz4, dpp, n4, dg_ple, loss_part = _ple_loss(h3, g_ple, pt, tgt, wpg, wpe, "ple_loss")

    xi, yi, ci = _place()
    me = jnp.stack([4 * xi + 2 * yi + ci, 2 * xi + yi]).astype(jnp.int32)
    g32, g16, big, pairs = {}, {}, {}, {}

    def keep(nme, pair, rows=None):
        for store, g in zip((g32, g16), pair):
            store[nme] = g if rows is None else g.reshape(N_DEV, rows, d)

    def start(names, after, tag):
        send, recv, parts, lands, token = _scatter_start([g16[nme] for nme in names], after, "grads_start_" + tag)
        return names, send, recv, parts, lands, token

    def start_two_level(names, after, tag):
        views = [g16[nme].reshape((4, 2) + g16[nme].shape[1:]) for nme in names]
        for nme, got in zip(names, _pair_exchange(views, "grads_pair_" + tag)):
            pairs[nme] = got.reshape((4,) + got.shape[2:])
        sums = [_pair_sum(g32[nme], pairs[nme], me, "pair_sum_" + nme) for nme in names]
        send, recv, parts, lands, token = _scatter_start(sums, after, "grads_start_" + tag, SAME_CORE_CHIPS)
        return names, send, recv, parts, lands, token

    def finish(state, after, tag):
        names, send, recv, parts, lands, _ = state
        relations = SAME_CORE_CHIPS if names[0] in pairs else ALL_PEERS
        lands = _scatter_wait(send, recv, parts, lands, after, "grads_wait_" + tag, relations)
        return names, lands

    def adam(done, dep):
        for nme, land in zip(*done):
            outs = _final_adam(g32[nme], land, _local(w[nme], nme), _local(m[nme], nme), _local(v[nme], nme), me, dep,
                               "adam_" + nme, pairs.get(nme))
            big[nme] = [(o.T if nme in TRANSPOSED else o)[None] for o in outs]

    keep("w_ple_gate", _dw(n4, dz4, 1, d, "dw_ple_gate"), d // N_DEV)
    keep("w_ple_proj", _dw(pt, dpp, N_DEV, d // N_DEV, "dw_ple_proj"))

    dh2, dgu2, a2, n3, dg_ffn2 = _ffn_bwd(dh3, h2, g_ffn2, gu2, wgu2, wd2, "ffn2_bwd")
    keep("ffn2_w_down", _dw(a2, dh3, N_DEV // 2, d, "dw_ffn2_down", 0.5), nf // 2)
    early = [(start(("w_ple_gate", "w_ple_proj", "ffn2_w_down"), dh2, "ffn2_down"), "ffn2_down")]
    keep("ffn2_w_gu", _dw(dgu2, n3, N_DEV, d, "dw_ffn2_gu", dep=early[-1][0][-1]))
    flight = start(("ffn2_w_gu",), dh2, "ffn2")

    dpa, dpb, dzg, dya, dyb = _merge_bwd(dh2, pa, pb, gate, wpa, wpb, wout, "merge_bwd")
    keep("w_out", _dw(merged, dh2, 1, d, "dw_out"), d // N_DEV)
    keep("w_proj_a", _dw(ya, dpa, N_DEV, d // N_DEV, "dw_proj_a"))
    keep("w_proj_b", _dw(yb, dpb, N_DEV, d // N_DEV, "dw_proj_b"))
    keep("w_gate", _dw(un, dzg, N_DEV, 2 * d // N_DEV, "dw_gate"))

    tok = flight[-1][0, 0]
    dqa, dka, dva, dgqa, dgka, dbias, _ = _attn_bwd("A", qkv, gqa + tok, gka, bias_a, sinks, ya, dya, bl, s_len,
                                                     "attn_a_bwd")
    dqb, dkb, dvb, dgqb, dgkb, _, dsink = _attn_bwd("B", qkv, gqb, gkb, bias_b, sinks, yb, dyb, bl, s_len, "attn_b_bwd")
    dqkv = [dqa, dka, dva, dqb, dkb, dvb]
    dtab = _rel_bias_grad(dbias, "rel_bias_grad")

    dh1, dg_mix = _proj_bwd(dh2, h1, g_mix, dzg, dqkv, win, wgate, "proj_bwd")
    keep("w_in", _dw_rows(dqkv, un, "dw_in"), IN_COLS // N_DEV)
    waiting = [finish(state, g32["w_in"], tag) for state, tag in early]
    done = finish(flight, waiting[-1][1][0], "ffn2")
    flight = start(("w_out", "w_proj_a", "w_proj_b", "w_gate", "w_in"), done[1][0], "mixer")
    waiting.append(done)

    dh0, dgu1, a1, n1, dg_ffn1 = _ffn_bwd(dh1, h0, g_ffn1 + flight[-1][0, 0], gu1, wgu1, wd1, "ffn1_bwd")
    keep("ffn1_w_down", _dw(a1, dh1, N_DEV // 2, d, "dw_ffn1_down", 0.5), nf // 2)
    done = finish(flight, g32["ffn1_w_down"], "mixer")
    flight = start(("ffn1_w_down",), done[1][0], "ffn1_down")
    waiting.append(done)

    keep("ffn1_w_gu", _dw(dgu1, n1, N_DEV, d, "dw_ffn1_gu", dep=flight[-1]))
    done = finish(flight, g32["ffn1_w_gu"], "ffn1_down")
    flight = start_two_level(("ffn1_w_gu",), done[1][0], "ffn1_gu")
    for group in waiting + [done]:
        adam(group, flight[-1])
    behind = 0.0 * big["ffn1_w_down"][0][0, 0, :1]
    smalls = (dg_ffn1, dg_mix, dg_ffn2, dg_ple + behind, dgqa, dgka, dgqb, dgkb, dtab, dsink)
    return dh0, loss_part, big, smalls, flight, finish, adam


def kernel(x, p, ffn1_norm, ffn1_w_gu, ffn1_w_down, mix_norm, w_in, a_q_norm, a_k_norm, a_rel_bias, b_q_norm, b_k_norm, b_sinks, w_gate, w_proj_a, w_proj_b, w_out, ffn2_norm, ffn2_w_gu, ffn2_w_down, ple_norm, w_ple_gate, w_ple_proj, loss_target, m_ffn1_norm, m_ffn1_w_gu, m_ffn1_w_down, m_mix_norm, m_w_in, m_a_q_norm, m_a_k_norm, m_a_rel_bias, m_b_q_norm, m_b_k_norm, m_b_sinks, m_w_gate, m_w_proj_a, m_w_proj_b, m_w_out, m_ffn2_norm, m_ffn2_w_gu, m_ffn2_w_down, m_ple_norm, m_w_ple_gate, m_w_ple_proj, v_ffn1_norm, v_ffn1_w_gu, v_ffn1_w_down, v_mix_norm, v_w_in, v_a_q_norm, v_a_k_norm, v_a_rel_bias, v_b_q_norm, v_b_k_norm, v_b_sinks, v_w_gate, v_w_proj_a, v_w_proj_b, v_w_out, v_ffn2_norm, v_ffn2_w_gu, v_ffn2_w_down, v_ple_norm, v_w_ple_gate, v_w_ple_proj):
    w = dict(ffn1_norm=ffn1_norm, ffn1_w_gu=ffn1_w_gu, ffn1_w_down=ffn1_w_down, mix_norm=mix_norm, w_in=w_in,
             a_q_norm=a_q_norm, a_k_norm=a_k_norm, a_rel_bias=a_rel_bias, b_q_norm=b_q_norm, b_k_norm=b_k_norm,
             b_sinks=b_sinks, w_gate=w_gate, w_proj_a=w_proj_a, w_proj_b=w_proj_b, w_out=w_out, ffn2_norm=ffn2_norm,
             ffn2_w_gu=ffn2_w_gu, ffn2_w_down=ffn2_w_down, ple_norm=ple_norm, w_ple_gate=w_ple_gate,
             w_ple_proj=w_ple_proj)
    m = dict(ffn1_norm=m_ffn1_norm, ffn1_w_gu=m_ffn1_w_gu, ffn1_w_down=m_ffn1_w_down, mix_norm=m_mix_norm,
             w_in=m_w_in, a_q_norm=m_a_q_norm, a_k_norm=m_a_k_norm, a_rel_bias=m_a_rel_bias, b_q_norm=m_b_q_norm,
             b_k_norm=m_b_k_norm, b_sinks=m_b_sinks, w_gate=m_w_gate, w_proj_a=m_w_proj_a, w_proj_b=m_w_proj_b,
             w_out=m_w_out, ffn2_norm=m_ffn2_norm, ffn2_w_gu=m_ffn2_w_gu, ffn2_w_down=m_ffn2_w_down,
             ple_norm=m_ple_norm, w_ple_gate=m_w_ple_gate, w_ple_proj=m_w_ple_proj)
    v = dict(ffn1_norm=v_ffn1_norm, ffn1_w_gu=v_ffn1_w_gu, ffn1_w_down=v_ffn1_w_down, mix_norm=v_mix_norm,
             w_in=v_w_in, a_q_norm=v_a_q_norm, a_k_norm=v_a_k_norm, a_rel_bias=v_a_rel_bias, b_q_norm=v_b_q_norm,
             b_k_norm=v_b_k_norm, b_sinks=v_b_sinks, w_gate=v_w_gate, w_proj_a=v_w_proj_a, w_proj_b=v_w_proj_b,
             w_out=v_w_out, ffn2_norm=v_ffn2_norm, ffn2_w_gu=v_ffn2_w_gu, ffn2_w_down=v_ffn2_w_down,
             ple_norm=v_ple_norm, w_ple_gate=v_w_ple_gate, w_ple_proj=v_w_ple_proj)
    bl, s_len, d = x.shape

    dh0, loss_part, big, smalls, flight, finish, adam = _step(x, p[0], loss_target, w, m, v)
    dg_ffn1, dg_mix, dg_ffn2, dg_ple, dgqa, dgka, dgqb, dgkb, dtab, dsink = smalls

    fold = lambda a: a[:, :, 0, :].reshape(-1, HEAD_DIM).sum(axis=0)
    small_part = dict(
        ffn1_norm=dg_ffn1, mix_norm=dg_mix, ffn2_norm=dg_ffn2, ple_norm=dg_ple,
        a_q_norm=fold(dgqa), a_k_norm=fold(dgka), b_q_norm=fold(dgqb), b_k_norm=fold(dgkb),
        a_rel_bias=dtab,
        b_sinks=dsink.sum(axis=0)[:, 0, :GROUP].reshape(B_Q_HEADS),
        loss=loss_part[0, :1])
    zero1 = jnp.zeros((1,), F32)
    shapes = {nme: w[nme].shape for nme in SMALL_NAMES if nme != "loss"}
    shapes["loss"] = ()
    pk = lambda src: _pack_small({**{nme: src[nme] for nme in SMALL_NAMES if nme != "loss"}, "loss": zero1})
    sg, sd, sm, sv = _small_allreduce_adam(_pack_small(small_part), pk(w), pk(m), pk(v), "small_allreduce_adam")
    adam(finish(flight, sg, "ffn1_gu"), sg)
    sg, sd, sm, sv = (_unpack_small(a, shapes) for a in (sg, sd, sm, sv))

    def pick(i):
        out = []
        for nme in WEIGHT_ORDER:
            out.append(big[nme][i] if nme in big else (sg, sd, sm, sv)[i][nme])
        return out

    return (sg["loss"], dh0.reshape(bl, s_len, d), *pick(0), *pick(1), *pick(2), *pick(3))
```

```python
import jax
import jax.numpy as jnp
import numpy as np
from jax import lax
from jax.experimental import pallas as pl
from jax.experimental.pallas import tpu as pltpu

F32 = jnp.float32
BF16 = jnp.bfloat16

CHUNK = 64
HEAD_DIM = 64
A_HEADS = 8
A_PREV = 8
A_MAX_REL = 128
B_Q_HEADS = 8
B_KV_HEADS = 2
B_PREV = 2
A_WIDTH = A_HEADS * HEAD_DIM
B_Q_WIDTH = B_Q_HEADS * HEAD_DIM
B_KV_WIDTH = B_KV_HEADS * HEAD_DIM
IN_COLS = 3 * A_WIDTH + B_Q_WIDTH + 2 * B_KV_WIDTH
EPS = 1e-6
NEG_INF = -1e30
ADAM_LR = 0.001
ADAM_B1 = 0.9
ADAM_B2 = 0.999
ADAM_EPS = 1e-08
ADAM_WD = 0.01
ADAM_STEP = 10

N_DEV = 8
LANES = 128
QTILE = 2 * CHUNK
VMEM_LIMIT = 56 * 1024 * 1024
DW_VMEM_LIMIT = 60 * 1024 * 1024
ADAM_TILE_ELEMS = 256 * 1024

MESH_ID = pl.DeviceIdType.MESH
ANY = pl.BlockSpec(memory_space=pl.ANY)
HBM = pl.BlockSpec(memory_space=pltpu.HBM)
SEM = pl.BlockSpec(memory_space=pltpu.SEMAPHORE)
SIDE_EFFECT = pltpu.SideEffectType.DATAFLOW_SIDE_EFFECTING


def _dot(a, b):
    return jnp.dot(a, b, preferred_element_type=F32)


def _dot_nt(a, b):
    return lax.dot_general(a, b, (((1,), (1,)), ((), ())), preferred_element_type=F32)


def _dot_tn(a, b):
    return lax.dot_general(a, b, (((0,), (0,)), ((), ())), preferred_element_type=F32)


def _params(sem=None, vmem=VMEM_LIMIT):
    return pltpu.CompilerParams(dimension_semantics=sem, vmem_limit_bytes=vmem)


def _row_tile(t, want):
    while t % want:
        want //= 2
    return want


def _place():
    return lax.axis_index("x"), lax.axis_index("y"), lax.axis_index("c")


def _gather_level(bufs, send_sems, recv_sems, level, shards=None):
    x, y, c = _place()
    me, sib = (x, y, c), (x, y, 1 - c)
    chips = [(1 - x, y), (x, 1 - y), (1 - x, 1 - y)]

    def copy(w, k, block, to):
        px, py, pc = block
        rows = bufs[w].at[4 * px + 2 * py + pc]
        src = shards[w] if shards is not None and block is me else rows
        return pltpu.make_async_remote_copy(src_ref=src, dst_ref=rows, send_sem=send_sems.at[k], recv_sem=recv_sems.at[k],
                                            device_id=to, device_id_type=MESH_ID)

    n = len(bufs)
    own = []
    if level == 1:
        own = [pltpu.make_async_copy(bufs[w].at[4 * x + 2 * y + c] if shards is None else shards[w],
                                     bufs[w].at[4 * x + 2 * y + c], send_sems.at[4 * n + w]) for w in range(n)]
    out, arriving = [], []
    for w in range(len(bufs)):
        if level == 1:
            out.append(copy(w, 4 * w, me, sib))
            arriving.append(copy(w, 4 * w, sib, me))
        for j, chip in enumerate(chips):
            if level == 1:
                out.append(copy(w, 4 * w + 1 + j, me, (*chip, c)))
                arriving.append(copy(w, 4 * w + 1 + j, (*chip, c), me))
            else:
                out.append(copy(w, 3 * w + j, (*chip, c), sib))
                arriving.append(copy(w, 3 * w + j, (*chip, 1 - c), me))
    return out, arriving, own


def _split_call(body, name, bufs, sems_in, after, n_sems_out, token, extra=()):
    n = len(bufs)
    out_shape = [pltpu.SemaphoreType.DMA((n_sems_out,))] * (2 if n_sems_out else 0)
    out_shape += [pltpu.HBM(a.shape, a.dtype) for a in bufs]
    out_specs = [SEM] * (2 if n_sems_out else 0) + [HBM] * n
    if token:
        out_shape.append(jax.ShapeDtypeStruct((8, LANES), F32))
        out_specs.append(pl.BlockSpec(memory_space=pltpu.VMEM))
    first = 2 if n_sems_out else 0
    return pl.pallas_call(
        body, name=name, out_shape=tuple(out_shape),
        in_specs=[HBM] * (n + len(extra)) + [SEM] * len(sems_in) + [ANY], out_specs=tuple(out_specs),
        input_output_aliases={i: first + i for i in range(n)},
        compiler_params=pltpu.CompilerParams(has_side_effects=SIDE_EFFECT),
    )(*bufs, *extra, *sems_in, after)


def _gather_start(shards, after, name):
    n = len(shards)
    hbm = lambda a: pltpu.with_memory_space_constraint(a, pltpu.HBM)
    bufs = [hbm(lax.empty((N_DEV,) + s.shape, s.dtype)) for s in shards]

    def body(*refs):
        out, _, own = _gather_level(refs[:n], refs[2 * n + 1], refs[2 * n + 2], 1, shards=refs[n:2 * n])
        for cp in own + out:
            cp.start()
        refs[-1][...] = jnp.zeros_like(refs[-1])

    outs = _split_call(body, name, bufs + [hbm(s) for s in shards], [], after, 5 * n, True)
    return outs[0], outs[1], list(outs[2:2 + 2 * n]), outs[-1]


def _gather_pass(send1, recv1, bufs_and_shards, after, name):
    n = len(bufs_and_shards) // 2
    bufs = bufs_and_shards

    def body(*refs):
        refs = refs[:n] + refs[2 * n:]
        out1, in1, own = _gather_level(refs[:n], refs[n], refs[n + 1], 1)
        out2, _, _ = _gather_level(refs[:n], refs[n + 3], refs[n + 4], 2)
        for cp in in1:
            cp.wait_recv()
        for cp in out2:
            cp.start()
        for cp in out1:
            cp.wait_send()
        for cp in own:
            cp.wait()
        refs[-1][...] = jnp.zeros_like(refs[-1])

    outs = _split_call(body, name, bufs, [send1, recv1], after, 3 * n, True)
    return outs[0], outs[1], list(outs[2:2 + n]), outs[-1]


def _gather_wait(send2, recv2, bufs, after, name):
    n = len(bufs)

    def body(*refs):
        out2, in2, _ = _gather_level(refs[:n], refs[n], refs[n + 1], 2)
        for cp in in2:
            cp.wait_recv()
        for cp in out2:
            cp.wait_send()

    return list(_split_call(body, name, bufs, [send2, recv2], after, 0, False))


ALL_PEERS = tuple(range(1, N_DEV))
SAME_CORE_CHIPS = (2, 4, 6)


def _scatter_copies(parts, lands, send_sems, recv_sems, relations):
    x, y, c = _place()
    ns = len(relations)
    cps = []
    for w, (part, land) in enumerate(zip(parts, lands)):
        for i, k in enumerate(relations):
            px, py, pc = x ^ ((k >> 2) & 1), y ^ ((k >> 1) & 1), c ^ (k & 1)
            block = 4 * px + 2 * py + pc if part.shape[0] == N_DEV else 2 * px + py
            cps.append(pltpu.make_async_remote_copy(
                src_ref=part.at[block], dst_ref=land.at[i],
                send_sem=send_sems.at[ns * w + i], recv_sem=recv_sems.at[ns * w + i],
                device_id=(px, py, pc), device_id_type=MESH_ID))
    return cps


def _scatter_start(parts, after, name, relations=ALL_PEERS):
    n = len(parts)
    ns = len(relations)

    def body(*refs):
        ins, lands = refs[:n], refs[n:2 * n]
        send_sems, recv_sems = refs[2 * n + 1], refs[2 * n + 2]
        token = refs[-1]
        for cp in _scatter_copies(ins, lands, send_sems, recv_sems, relations):
            cp.start()
        token[...] = jnp.zeros_like(token)

    land_shapes = [(ns,) + p.shape[1:] for p in parts]
    in_hbm = [pltpu.with_memory_space_constraint(p, pltpu.HBM) for p in parts]
    in_hbm += [pltpu.with_memory_space_constraint(lax.empty(s, p.dtype), pltpu.HBM) for s, p in zip(land_shapes, parts)]
    outs = pl.pallas_call(
        body, name=name,
        out_shape=(pltpu.SemaphoreType.DMA((ns * n,)), pltpu.SemaphoreType.DMA((ns * n,)),
                   *[pltpu.HBM(p.shape, p.dtype) for p in parts],
                   *[pltpu.HBM(s, p.dtype) for s, p in zip(land_shapes, parts)],
                   jax.ShapeDtypeStruct((8, LANES), F32)),
        in_specs=[HBM] * (2 * n) + [ANY],
        out_specs=(SEM, SEM, *[HBM] * (2 * n), pl.BlockSpec(memory_space=pltpu.VMEM)),
        input_output_aliases={i: 2 + i for i in range(2 * n)},
        compiler_params=pltpu.CompilerParams(has_side_effects=SIDE_EFFECT),
    )(*in_hbm, after)
    return outs[0], outs[1], list(outs[2:2 + n]), list(outs[2 + n:2 + 2 * n]), outs[-1]


def _scatter_wait(send_sems, recv_sems, parts, lands, after, name, relations=ALL_PEERS):
    n = len(parts)

    def body(*refs):
        ins, lnd = refs[:n], refs[n:2 * n]
        for cp in _scatter_copies(ins, lnd, refs[2 * n], refs[2 * n + 1], relations):
            cp.wait_send()
            cp.wait_recv()

    outs = pl.pallas_call(
        body, name=name,
        out_shape=tuple(pltpu.HBM(a.shape, a.dtype) for a in parts + lands),
        in_specs=[HBM] * (2 * n) + [SEM, SEM, ANY],
        out_specs=tuple([HBM] * (2 * n)),
        input_output_aliases={i: i for i in range(2 * n)},
        compiler_params=pltpu.CompilerParams(has_side_effects=SIDE_EFFECT),
    )(*parts, *lands, send_sems, recv_sems, after)
    return list(outs[n:])


def _pair_exchange(parts, name):
    n = len(parts)

    def body(*refs):
        ins, outs = refs[:n], refs[n:2 * n]
        send_sems, recv_sems = refs[2 * n:]
        x, y, c = _place()
        cps = [pltpu.make_async_remote_copy(
            src_ref=ins[w].at[:, pl.ds(1 - c, 1)], dst_ref=outs[w], send_sem=send_sems.at[w], recv_sem=recv_sems.at[w],
            device_id=(x, y, 1 - c), device_id_type=MESH_ID) for w in range(n)]
        for cp in cps:
            cp.start()
        for cp in cps:
            cp.wait()

    return pl.pallas_call(
        body, name=name,
        out_shape=[jax.ShapeDtypeStruct((4, 1) + p.shape[2:], p.dtype) for p in parts],
        in_specs=[ANY] * n, out_specs=[ANY] * n,
        scratch_shapes=[pltpu.SemaphoreType.DMA((n,)), pltpu.SemaphoreType.DMA((n,))],
    )(*parts)


def _pair_sum(g8, r1, me, name):
    _, r, c = g8.shape
    tr = max(q for q in range(16, r + 1, 16) if r % q == 0 and q * c <= ADAM_TILE_ELEMS)

    def body(me_ref, g_ref, r_ref, o_ref):
        o_ref[...] = (g_ref[...] + r_ref[...].astype(F32)).astype(BF16)

    chip = lambda k, s: s[1] ^ (k + 1)
    return pl.pallas_call(
        body, name=name,
        out_shape=jax.ShapeDtypeStruct((4, r, c), BF16),
        grid_spec=pltpu.PrefetchScalarGridSpec(
            num_scalar_prefetch=1, grid=(3, r // tr),
            in_specs=[pl.BlockSpec((None, None, tr, c), lambda k, i, s: (chip(k, s), s[0] % 2, i, 0)),
                      pl.BlockSpec((None, tr, c), lambda k, i, s: (chip(k, s), i, 0))],
            out_specs=pl.BlockSpec((None, tr, c), lambda k, i, s: (chip(k, s), i, 0))),
        compiler_params=_params(("arbitrary", "arbitrary")),
    )(me, g8.reshape((4, 2) + g8.shape[1:]), r1)


def _adam(w, g, m, v):
    m2 = ADAM_B1 * m + (1.0 - ADAM_B1) * g
    v2 = ADAM_B2 * v + (1.0 - ADAM_B2) * (g * g)
    m_hat = m2 / (1.0 - ADAM_B1 ** ADAM_STEP)
    v_hat = v2 / (1.0 - ADAM_B2 ** ADAM_STEP)
    delta = -ADAM_LR * (m_hat / (jnp.sqrt(v_hat) + ADAM_EPS) + ADAM_WD * w)
    return delta, m2, v2


def _small_allreduce_adam(part, w, m, v, name):
    rows = part.shape[0]

    def body(p_ref, w_ref, m_ref, v_ref, g_ref, d_ref, mo_ref, vo_ref, buf, send_sems, recv_sems):
        x, y, c = _place()
        buf[0] = p_ref[...]
        cps = []
        for k in range(1, N_DEV):
            kx, ky, kc = (k >> 2) & 1, (k >> 1) & 1, k & 1
            peer = (x ^ kx, y ^ ky, c ^ kc)
            cps.append(pltpu.make_async_remote_copy(
                src_ref=p_ref, dst_ref=buf.at[k], send_sem=send_sems.at[k - 1], recv_sem=recv_sems.at[k - 1],
                device_id=peer, device_id_type=MESH_ID))
        for cp in cps:
            cp.start()
        for cp in cps:
            cp.wait()
        me = 4 * x + 2 * y + c
        total = buf[me]
        for d in range(1, N_DEV):
            total = total + buf[d ^ me]
        g_ref[...] = total
        delta, m2, v2 = _adam(w_ref[...], total, m_ref[...], v_ref[...])
        d_ref[...] = delta
        mo_ref[...] = m2
        vo_ref[...] = v2

    vm = pl.BlockSpec(memory_space=pltpu.VMEM)
    return pl.pallas_call(
        body, name=name,
        out_shape=[jax.ShapeDtypeStruct(part.shape, F32)] * 4,
        in_specs=[vm] * 4, out_specs=[vm] * 4,
        scratch_shapes=[pltpu.VMEM((N_DEV, rows, LANES), F32),
                        pltpu.SemaphoreType.DMA((N_DEV - 1,)), pltpu.SemaphoreType.DMA((N_DEV - 1,))],
    )(part, w, m, v)


def _final_adam(g8, land, w, m, v, me, dep, name, pair=None):
    _, r, c = g8.shape
    tr = max(q for q in range(16, r + 1, 16) if r % q == 0 and q * c <= ADAM_TILE_ELEMS)
    nland = land.shape[0]

    def body(me_ref, g_ref, land_ref, *rest):
        pair_ref = rest[0] if pair is not None else None
        w_ref, m_ref, v_ref, _, go_ref, d_ref, mo_ref, vo_ref = rest[-8:]
        g = g_ref[...]
        if pair_ref is not None:
            g = g + pair_ref[...].astype(F32)
        for k in range(nland):
            g = g + land_ref[k].astype(F32)
        go_ref[...] = g
        delta, m2, v2 = _adam(w_ref[...], g, m_ref[...], v_ref[...])
        d_ref[...] = delta
        mo_ref[...] = m2
        vo_ref[...] = v2

    plain = pl.BlockSpec((tr, c), lambda i, s: (i, 0))
    return pl.pallas_call(
        body, name=name,
        out_shape=[jax.ShapeDtypeStruct((r, c), F32)] * 4,
        grid_spec=pltpu.PrefetchScalarGridSpec(
            num_scalar_prefetch=1, grid=(r // tr,),
            in_specs=[pl.BlockSpec((None, tr, c), lambda i, s: (s[0], i, 0)),
                      pl.BlockSpec((nland, tr, c), lambda i, s: (0, i, 0))]
            + ([] if pair is None else [pl.BlockSpec((None, tr, c), lambda i, s: (s[1], i, 0))])
            + [plain, plain, plain, ANY],
            out_specs=[plain] * 4),
        compiler_params=_params(("arbitrary",)),
    )(*((me, g8, land) + (() if pair is None else (pair,)) + (w, m, v, dep)))


def _rms(x, gain):
    r = lax.rsqrt(jnp.mean(x * x, axis=-1, keepdims=True) + EPS)
    xh = x * r
    return xh * gain, xh, r


def _rms_bwd(xh, r, gain, dy):
    gdy = gain * dy
    dx = r * (gdy - xh * jnp.mean(xh * gdy, axis=-1, keepdims=True))
    return dx, jnp.sum(dy * xh, axis=0, keepdims=True)


def _load_weights(pairs, sems):
    cps = [pltpu.make_async_copy(src, dst, sems.at[i]) for i, (src, dst) in enumerate(pairs)]
    for cp in cps:
        cp.start()
    for cp in cps:
        cp.wait()


def _ffn_fwd(h, gain, wgu, wd, name):
    t, d = h.shape
    nb, nf, _ = wgu.shape
    nh = nb // 2
    tm = _row_tile(t, 512)

    def body(h_ref, g_ref, wgu_hbm, wd_hbm, out_ref, gu_ref, wgu_v, wd_v, sems):
        @pl.when(pl.program_id(0) == 0)
        def _():
            _load_weights([(wgu_hbm, wgu_v), (wd_hbm, wd_v)], sems)

        x = h_ref[...]
        n, _, _ = _rms(x, g_ref[...])
        nbf = n.astype(BF16)
        acc = jnp.zeros((tm, d), F32)
        for j in range(nh):
            g = _dot_nt(nbf, wgu_v[j])
            u = _dot_nt(nbf, wgu_v[j + nh])
            gu_ref[j] = g.astype(BF16)
            gu_ref[j + nh] = u.astype(BF16)
            a = (g * jax.nn.sigmoid(g)) * u
            acc = acc + _dot(a.astype(BF16), wd_v[j])
        out_ref[...] = x + 0.5 * acc

    return pl.pallas_call(
        body, name=name, grid=(t // tm,),
        out_shape=[jax.ShapeDtypeStruct((t, d), F32), jax.ShapeDtypeStruct((nb, t, nf), BF16)],
        in_specs=[pl.BlockSpec((tm, d), lambda i: (i, 0)), pl.BlockSpec((1, d), lambda i: (0, 0)), ANY, ANY],
        out_specs=[pl.BlockSpec((tm, d), lambda i: (i, 0)), pl.BlockSpec((nb, tm, nf), lambda i: (0, i, 0))],
        scratch_shapes=[pltpu.VMEM(wgu.shape, BF16), pltpu.VMEM(wd.shape, BF16), pltpu.SemaphoreType.DMA((2,))],
        compiler_params=_params(("arbitrary",)),
    )(h, gain, wgu, wd)


def _ffn_up(h, gain, wgu, name):
    t, d = h.shape
    nb, nf, _ = wgu.shape
    nh = nb // 2
    tm = _row_tile(t, 512)

    def body(h_ref, g_ref, wgu_hbm, gu_ref, a_ref, wgu_v, sems):
        @pl.when(pl.program_id(0) == 0)
        def _():
            _load_weights([(wgu_hbm, wgu_v)], sems)

        n, _, _ = _rms(h_ref[...], g_ref[...])
        nbf = n.astype(BF16)
        for j in range(nh):
            g = _dot_nt(nbf, wgu_v[j])
            u = _dot_nt(nbf, wgu_v[j + nh])
            gu_ref[j] = g.astype(BF16)
            gu_ref[j + nh] = u.astype(BF16)
            a_ref[j] = ((g * jax.nn.sigmoid(g)) * u).astype(BF16)

    return pl.pallas_call(
        body, name=name, grid=(t // tm,),
        out_shape=[jax.ShapeDtypeStruct((nb, t, nf), BF16), jax.ShapeDtypeStruct((nh, t, nf), BF16)],
        in_specs=[pl.BlockSpec((tm, d), lambda i: (i, 0)), pl.BlockSpec((1, d), lambda i: (0, 0)), ANY],
        out_specs=[pl.BlockSpec((nb, tm, nf), lambda i: (0, i, 0)), pl.BlockSpec((nh, tm, nf), lambda i: (0, i, 0))],
        scratch_shapes=[pltpu.VMEM(wgu.shape, BF16), pltpu.SemaphoreType.DMA((1,))],
        compiler_params=_params(("arbitrary",)),
    )(h, gain, wgu)


def _ffn_down(h, a, wd, name):
    t, d = h.shape
    nh, nf, _ = wd.shape
    tm = _row_tile(t, 512)

    def body(h_ref, a_ref, wd_ref, out_ref):
        acc = jnp.zeros((tm, d), F32)
        for j in range(nh):
            acc = acc + _dot(a_ref[j], wd_ref[j])
        out_ref[...] = h_ref[...] + 0.5 * acc

    row = pl.BlockSpec((tm, d), lambda i: (i, 0))
    return pl.pallas_call(
        body, name=name, grid=(t // tm,),
        out_shape=jax.ShapeDtypeStruct((t, d), F32),
        in_specs=[row, pl.BlockSpec((nh, tm, nf), lambda i: (0, i, 0)), pl.BlockSpec(wd.shape, lambda i: (0, 0, 0))],
        out_specs=row,
        compiler_params=_params(("arbitrary",)),
    )(h, a, wd)


def _ffn_bwd(dh, h, gain, gu, wgu, wd, name):
    t, d = h.shape
    nb, nf, _ = wgu.shape
    nh = nb // 2
    tm = _row_tile(t, 256)

    def body(dh_ref, h_ref, g_ref, gu_ref, wgu_hbm, wd_hbm, dhp_ref, dgu_ref, a_ref, n_ref, dgain_ref,
             wgu_v, wd_v, sems):
        @pl.when(pl.program_id(0) == 0)
        def _():
            _load_weights([(wgu_hbm, wgu_v), (wd_hbm, wd_v)], sems)
            dgain_ref[...] = jnp.zeros_like(dgain_ref)

        x = h_ref[...]
        gain_v = g_ref[...]
        n, xh, r = _rms(x, gain_v)
        n_ref[...] = n.astype(BF16)
        dh_v = dh_ref[...]
        dfb = (0.5 * dh_v).astype(BF16)
        dn = jnp.zeros((tm, d), F32)
        for j in range(nh):
            da = _dot_nt(dfb, wd_v[j])
            g = gu_ref[j].astype(F32)
            u = gu_ref[j + nh].astype(F32)
            sg = jax.nn.sigmoid(g)
            si = g * sg
            dg = (da * u * (sg * (1.0 + g * (1.0 - sg)))).astype(BF16)
            du = (da * si).astype(BF16)
            a_ref[j] = (si * u).astype(BF16)
            dgu_ref[j] = dg
            dgu_ref[j + nh] = du
            dn = dn + _dot(dg, wgu_v[j]) + _dot(du, wgu_v[j + nh])
        dx, dgain = _rms_bwd(xh, r, gain_v, dn)
        dhp_ref[...] = dh_v + dx
        dgain_ref[...] += dgain

    row = pl.BlockSpec((tm, d), lambda i: (i, 0))
    vec = pl.BlockSpec((1, d), lambda i: (0, 0))
    return pl.pallas_call(
        body, name=name, grid=(t // tm,),
        out_shape=[jax.ShapeDtypeStruct((t, d), F32), jax.ShapeDtypeStruct((nb, t, nf), BF16),
                   jax.ShapeDtypeStruct((nh, t, nf), BF16), jax.ShapeDtypeStruct((t, d), BF16),
                   jax.ShapeDtypeStruct((1, d), F32)],
        in_specs=[row, row, vec, pl.BlockSpec((nb, tm, nf), lambda i: (0, i, 0)), ANY, ANY],
        out_specs=[row, pl.BlockSpec((nb, tm, nf), lambda i: (0, i, 0)),
                   pl.BlockSpec((nh, tm, nf), lambda i: (0, i, 0)), row, vec],
        scratch_shapes=[pltpu.VMEM(wgu.shape, BF16), pltpu.VMEM(wd.shape, BF16), pltpu.SemaphoreType.DMA((2,))],
        compiler_params=_params(("arbitrary",)),
    )(dh, h, gain, gu, wgu, wd)


def _dw(xa, dy, nb, n, name, scale=1.0, dep=None):
    t, k = xa.shape[-2:]
    wide = xa.ndim == 2
    tt = _row_tile(t, 1024)
    steps = t // tt
    x_spec = pl.BlockSpec((tt, k), lambda i: (i, 0)) if wide else pl.BlockSpec((nb, tt, k), lambda i: (0, i, 0))
    dy_spec = pl.BlockSpec((tt, dy.shape[1]), lambda i: (i, 0))
    acc_shape = (k, nb * n) if wide else (nb, k, n)
    stage_shape = (k, nb * n) if wide else (k, n)

    def body(x_ref, dy_ref, *rest):
        o_hbm, ob_hbm, acc, stage, sems = rest[-5:]

        @pl.when(pl.program_id(0) == 0)
        def _():
            acc[...] = jnp.zeros_like(acc)

        dyb = dy_ref[...].astype(BF16)
        if wide:
            acc[...] += _dot(x_ref[...].astype(BF16).T, dyb)
        else:
            for j in range(nb):
                acc[j] += _dot_tn(x_ref[j].astype(BF16), dyb)

        @pl.when(pl.program_id(0) == steps - 1)
        def _():
            if scale != 1.0:
                acc[...] = acc[...] * scale
            if wide:
                cps = [pltpu.make_async_copy(acc.at[:, pl.ds(j * n, n)] if nb > 1 else acc, o_hbm.at[j], sems.at[j])
                       for j in range(nb)]
            else:
                cps = [pltpu.make_async_copy(acc, o_hbm, sems.at[0])]
            for cp in cps:
                cp.start()
            if wide:
                stage[...] = acc[...].astype(BF16)
                bcs = [pltpu.make_async_copy(stage.at[:, pl.ds(j * n, n)] if nb > 1 else stage, ob_hbm.at[j],
                                             sems.at[nb + j]) for j in range(nb)]
                for cp in bcs:
                    cp.start()
                for cp in bcs:
                    cp.wait()
            else:
                for j in range(nb):
                    stage[...] = acc[j].astype(BF16)
                    cp = pltpu.make_async_copy(stage, ob_hbm.at[j], sems.at[nb])
                    cp.start()
                    cp.wait()
            for cp in cps:
                cp.wait()

    return pl.pallas_call(
        body, name=name, grid=(steps,),
        out_shape=[jax.ShapeDtypeStruct((nb, k, n), F32), jax.ShapeDtypeStruct((nb, k, n), BF16)],
        in_specs=[x_spec, dy_spec] + ([] if dep is None else [ANY]),
        out_specs=[ANY, ANY],
        scratch_shapes=[pltpu.VMEM(acc_shape, F32), pltpu.VMEM(stage_shape, BF16),
                        pltpu.SemaphoreType.DMA((2 * nb,))],
        compiler_params=_params(("arbitrary",), DW_VMEM_LIMIT),
    )(*((xa, dy) if dep is None else (xa, dy, dep)))


def _proj_fwd(h, gain, win, wgate, name):
    t, d = h.shape
    tm = _row_tile(t, 512)
    nq = win.shape[0]
    nbk, _, nc = wgate.shape
    ng = nbk * nc

    def body(h_ref, g_ref, win_ref, wg_ref, un_ref, qkv_ref, gate_ref):
        n, _, _ = _rms(h_ref[...], g_ref[...])
        nbf = n.astype(BF16)
        un_ref[...] = nbf
        qkv_ref[...] = _dot_nt(nbf, win_ref[...])
        for j in range(nbk):
            gate_ref[:, j * nc:(j + 1) * nc] = jax.nn.sigmoid(_dot(nbf, wg_ref[j])).astype(BF16)

    full = lambda a: pl.BlockSpec(a.shape, lambda i: (0,) * a.ndim)
    return pl.pallas_call(
        body, name=name, grid=(t // tm,),
        out_shape=[jax.ShapeDtypeStruct((t, d), BF16), jax.ShapeDtypeStruct((t, nq), F32),
                   jax.ShapeDtypeStruct((t, ng), BF16)],
        in_specs=[pl.BlockSpec((tm, d), lambda i: (i, 0)), full(gain), full(win), full(wgate)],
        out_specs=[pl.BlockSpec((tm, d), lambda i: (i, 0)), pl.BlockSpec((tm, nq), lambda i: (i, 0)),
                   pl.BlockSpec((tm, ng), lambda i: (i, 0))],
        compiler_params=_params(("arbitrary",)),
    )(h, gain, win, wgate)


def _proj_bwd(dh, h, gain, dzg, dqkv_parts, win, wgate, name):
    t, d = h.shape
    tm = _row_tile(t, 512)
    nbk, _, nc = wgate.shape
    ng = nbk * nc
    np_ = len(dqkv_parts)
    widths = [a.shape[1] for a in dqkv_parts]

    def body(dh_ref, h_ref, g_ref, dzg_ref, *rest):
        part_refs, (win_ref, wg_ref, dhp_ref, dgain_ref) = rest[:np_], rest[np_:]

        @pl.when(pl.program_id(0) == 0)
        def _():
            dgain_ref[...] = jnp.zeros_like(dgain_ref)

        gain_v = g_ref[...]
        _, xh, r = _rms(h_ref[...], gain_v)
        dun = jnp.zeros((tm, d), F32)
        for j in range(nbk):
            dun = dun + _dot_nt(dzg_ref[:, j * nc:(j + 1) * nc], wg_ref[j])
        off = 0
        for ref, wd in zip(part_refs, widths):
            dun = dun + _dot(ref[...].astype(BF16), win_ref[off:off + wd, :])
            off += wd
        dx, dgain = _rms_bwd(xh, r, gain_v, dun)
        dhp_ref[...] = dh_ref[...] + dx
        dgain_ref[...] += dgain

    full = lambda a: pl.BlockSpec(a.shape, lambda i: (0,) * a.ndim)
    row = pl.BlockSpec((tm, d), lambda i: (i, 0))
    return pl.pallas_call(
        body, name=name, grid=(t // tm,),
        out_shape=[jax.ShapeDtypeStruct((t, d), F32), jax.ShapeDtypeStruct((1, d), F32)],
        in_specs=[row, row, full(gain), pl.BlockSpec((tm, ng), lambda i: (i, 0))]
        + [pl.BlockSpec((tm, wd), lambda i: (i, 0)) for wd in widths] + [full(win), full(wgate)],
        out_specs=[row, pl.BlockSpec((1, d), lambda i: (0, 0))],
        compiler_params=_params(("arbitrary",)),
    )(dh, h, gain, dzg, *dqkv_parts, win, wgate)


def _dw_rows(parts, dy, name):
    t, n = dy.shape
    widths = [a.shape[1] for a in parts]
    k = sum(widths)
    tt = _row_tile(t, 1024)
    steps = t // tt
    np_ = len(parts)

    def body(*refs):
        part_refs, dy_ref = refs[:np_], refs[np_]
        o_hbm, ob_hbm, acc, stage, sems = refs[np_ + 1:]

        @pl.when(pl.program_id(0) == 0)
        def _():
            acc[...] = jnp.zeros_like(acc)

        dyb = dy_ref[...].astype(BF16)
        off = 0
        for ref, wd in zip(part_refs, widths):
            acc[off:off + wd, :] += _dot(ref[...].astype(BF16).T, dyb)
            off += wd

        @pl.when(pl.program_id(0) == steps - 1)
        def _():
            stage[...] = acc[...].astype(BF16)
            cps = [pltpu.make_async_copy(acc, o_hbm.at[0], sems.at[0]),
                   pltpu.make_async_copy(stage, ob_hbm.at[0], sems.at[1])]
            for cp in cps:
                cp.start()
            for cp in cps:
                cp.wait()

    return pl.pallas_call(
        body, name=name, grid=(steps,),
        out_shape=[jax.ShapeDtypeStruct((1, k, n), F32), jax.ShapeDtypeStruct((1, k, n), BF16)],
        in_specs=[pl.BlockSpec((tt, wd), lambda i: (i, 0)) for wd in widths] + [pl.BlockSpec((tt, n), lambda i: (i, 0))],
        out_specs=[ANY, ANY],
        scratch_shapes=[pltpu.VMEM((k, n), F32), pltpu.VMEM((k, n), BF16), pltpu.SemaphoreType.DMA((2,))],
        compiler_params=_params(("arbitrary",)),
    )(*parts, dy)


def _merge_fwd(h, ya, yb, gate, wpa, wpb, wout, name):
    t, d = h.shape
    tm = _row_tile(t, 512)

    def body(h_ref, ya_ref, yb_ref, ga_ref, gb_ref, wpa_ref, wpb_ref, wout_ref, out_ref, mg_ref, pa_ref, pb_ref):
        pa = _dot(ya_ref[...].astype(BF16), wpa_ref[...])
        pb = _dot(yb_ref[...].astype(BF16), wpb_ref[...])
        merged = (ga_ref[...].astype(F32) * pa + gb_ref[...].astype(F32) * pb).astype(BF16)
        pa_ref[...] = pa.astype(BF16)
        pb_ref[...] = pb.astype(BF16)
        mg_ref[...] = merged
        out_ref[...] = h_ref[...] + _dot(merged, wout_ref[...])

    full = lambda a: pl.BlockSpec(a.shape, lambda i: (0,) * a.ndim)
    row = pl.BlockSpec((tm, d), lambda i: (i, 0))
    yrow = pl.BlockSpec((tm, ya.shape[1]), lambda i: (i, 0))
    return pl.pallas_call(
        body, name=name, grid=(t // tm,),
        out_shape=[jax.ShapeDtypeStruct((t, d), F32)] + [jax.ShapeDtypeStruct((t, d), BF16)] * 3,
        in_specs=[row, yrow, yrow, pl.BlockSpec((tm, d), lambda i: (i, 0)), pl.BlockSpec((tm, d), lambda i: (i, 1)),
                  full(wpa), full(wpb), full(wout)],
        out_specs=[row] * 4,
        compiler_params=_params(("arbitrary",)),
    )(h, ya, yb, gate, gate, wpa, wpb, wout)


def _merge_bwd(dh, pa, pb, gate, wpa, wpb, wout, name):
    t, d = dh.shape
    tm = _row_tile(t, 512)
    wy = wpa.shape[0]

    def body(dh_ref, pa_ref, pb_ref, ga_ref, gb_ref, wpa_ref, wpb_ref, wout_ref,
             dpa_ref, dpb_ref, dzg_ref, dya_ref, dyb_ref):
        dm = _dot_nt(dh_ref[...].astype(BF16), wout_ref[...])
        ga, gb = ga_ref[...].astype(F32), gb_ref[...].astype(F32)
        dpa = (dm * ga).astype(BF16)
        dpb = (dm * gb).astype(BF16)
        dpa_ref[...] = dpa
        dpb_ref[...] = dpb
        dzg_ref[:, :d] = (dm * pa_ref[...].astype(F32) * ga * (1.0 - ga)).astype(BF16)
        dzg_ref[:, d:] = (dm * pb_ref[...].astype(F32) * gb * (1.0 - gb)).astype(BF16)
        dya_ref[...] = _dot_nt(dpa, wpa_ref[...])
        dyb_ref[...] = _dot_nt(dpb, wpb_ref[...])

    full = lambda a: pl.BlockSpec(a.shape, lambda i: (0,) * a.ndim)
    row = pl.BlockSpec((tm, d), lambda i: (i, 0))
    yrow = pl.BlockSpec((tm, wy), lambda i: (i, 0))
    return pl.pallas_call(
        body, name=name, grid=(t // tm,),
        out_shape=[jax.ShapeDtypeStruct((t, d), BF16), jax.ShapeDtypeStruct((t, d), BF16),
                   jax.ShapeDtypeStruct((t, 2 * d), BF16), jax.ShapeDtypeStruct((t, wy), F32),
                   jax.ShapeDtypeStruct((t, wy), F32)],
        in_specs=[row, row, row, pl.BlockSpec((tm, d), lambda i: (i, 0)), pl.BlockSpec((tm, d), lambda i: (i, 1)),
                  full(wpa), full(wpb), full(wout)],
        out_specs=[row, row, pl.BlockSpec((tm, 2 * d), lambda i: (i, 0)), yrow, yrow],
        compiler_params=_params(("arbitrary",)),
    )(dh, pa, pb, gate, gate, wpa, wpb, wout)


def _ple_loss(h, gain, p, target, wpg, wpe, name):
    t, d = h.shape
    tm = _row_tile(t, 512)
    pd = p.shape[1]

    def body(h_ref, g_ref, p_ref, t_ref, wpg_ref, wpe_ref, dh_ref, dz_ref, dpp_ref, n_ref, dgain_ref, loss_ref):
        @pl.when(pl.program_id(0) == 0)
        def _():
            dgain_ref[...] = jnp.zeros_like(dgain_ref)
            loss_ref[...] = jnp.zeros_like(loss_ref)

        x = h_ref[...]
        gain_v = g_ref[...]
        n, xh, r = _rms(x, gain_v)
        nbf = n.astype(BF16)
        n_ref[...] = nbf
        pg = jax.nn.sigmoid(_dot(nbf, wpg_ref[...]))
        pp = _dot(p_ref[...].astype(BF16), wpe_ref[...])
        err = (x + pg * pp) - t_ref[...]
        loss_ref[...] += 0.5 * jnp.sum(jnp.mean(err * err, axis=-1, keepdims=True))
        dy = err * (1.0 / d)
        dpp_ref[...] = (dy * pg).astype(BF16)
        dz = (dy * pp * pg * (1.0 - pg)).astype(BF16)
        dz_ref[...] = dz
        dn = _dot_nt(dz, wpg_ref[...])
        dx, dgain = _rms_bwd(xh, r, gain_v, dn)
        dh_ref[...] = dy + dx
        dgain_ref[...] += dgain

    full = lambda a: pl.BlockSpec(a.shape, lambda i: (0,) * a.ndim)
    row = pl.BlockSpec((tm, d), lambda i: (i, 0))
    return pl.pallas_call(
        body, name=name, grid=(t // tm,),
        out_shape=[jax.ShapeDtypeStruct((t, d), F32), jax.ShapeDtypeStruct((t, d), BF16),
                   jax.ShapeDtypeStruct((t, d), BF16), jax.ShapeDtypeStruct((t, d), BF16),
                   jax.ShapeDtypeStruct((1, d), F32), jax.ShapeDtypeStruct((8, LANES), F32)],
        in_specs=[row, full(gain), pl.BlockSpec((tm, pd), lambda i: (i, 0)), row, full(wpg), full(wpe)],
        out_specs=[row, row, row, row, pl.BlockSpec((1, d), lambda i: (0, 0)),
                   pl.BlockSpec((8, LANES), lambda i: (0, 0))],
        compiler_params=_params(("arbitrary",)),
    )(h, gain, p, target, wpg, wpe)


def _head_masks():
    lane = lax.broadcasted_iota(jnp.int32, (1, LANES), 1)
    m0 = (lane < HEAD_DIM).astype(F32)
    return m0, 1.0 - m0


def _head_mean(v, m0, m1):
    del m0, m1
    width = v.shape[-1]
    shift = HEAD_DIM.bit_length() - 1
    r = jnp.right_shift(lax.broadcasted_iota(jnp.int32, (width, width), 0), shift)
    c = jnp.right_shift(lax.broadcasted_iota(jnp.int32, (width, width), 1), shift)
    same_head = (r == c).astype(BF16)
    return _dot(v.astype(BF16), same_head) * (1.0 / HEAD_DIM)


def _head_norm(x, gain, m0, m1):
    r = lax.rsqrt(_head_mean(x * x, m0, m1) + EPS)
    xh = x * r
    return xh * gain, xh, r


def _head_norm_bwd(xh, r, gain, dy, m0, m1):
    gdy = gain * dy
    dx = r * (gdy - xh * _head_mean(xh * gdy, m0, m1))
    return dx, jnp.sum(dy * xh, axis=0, keepdims=True)


GROUP = 4
QW = GROUP * HEAD_DIM
STACK = GROUP * QTILE


def _kv_width(mode):
    return QW if mode == "A" else LANES


def _q_scratch_shape(mode, s_len):
    return (s_len, QW) if mode == "A" else (GROUP * s_len, LANES)


def _group_masks(dtype=F32):
    lane = lax.broadcasted_iota(jnp.int32, (1, QW), 1)
    return [((lane >= h * HEAD_DIM) & (lane < (h + 1) * HEAD_DIM)).astype(dtype) for h in range(GROUP)]


def _stack_heads(first_kv, x, m0, m1):
    out = []
    for half in range(GROUP // 2):
        xh = x[:, half * LANES:(half + 1) * LANES]
        a0, a1 = xh * m0, xh * m1
        r0, r1 = pltpu.roll(a0, HEAD_DIM, 1), pltpu.roll(a1, HEAD_DIM, 1)
        out += [jnp.where(first_kv, a0, r0), jnp.where(first_kv, r1, a1)]
    return out


def _unstack_heads(mode, first_kv, ts, m0, m1):
    if mode == "A":
        masks = _group_masks()
        return sum(t * mk for t, mk in zip(ts, masks))
    halves = []
    for half in range(GROUP // 2):
        t0 = jnp.where(first_kv, ts[2 * half], pltpu.roll(ts[2 * half], HEAD_DIM, 1))
        t1 = jnp.where(first_kv, pltpu.roll(ts[2 * half + 1], HEAD_DIM, 1), ts[2 * half + 1])
        halves.append(t0 * m0 + t1 * m1)
    return jnp.concatenate(halves, axis=1)


def _store_stacked(dst, i, heads):
    for half in range(2):
        rows = slice(half * QTILE, (half + 1) * QTILE)
        for h, x in enumerate(heads):
            dst[pl.ds((2 * i + half) * STACK + h * QTILE, QTILE), :] = x[rows].astype(dst.dtype)


def _load_stacked(mode, ref, m):
    if mode == "B":
        return ref[pl.ds(pl.multiple_of(m * STACK, STACK), STACK), :]
    x = ref[pl.ds(pl.multiple_of(m * QTILE, QTILE), QTILE), :]
    return jnp.concatenate([x * mk for mk in _group_masks(x.dtype)], axis=0)


def _attn_prep(mode, group, s_len, padk, q_ref, k_ref, v_ref, gq_ref, gk_ref, qs, k2, v2, do_ref=None, dos=None):
    m0, m1 = _head_masks()
    first_kv = group == 0
    rt = 2 * QTILE

    def keys_and_values():
        zpad = jnp.zeros((padk, k2.shape[1]), BF16)
        k2[pl.ds(0, padk), :] = zpad
        v2[pl.ds(0, padk), :] = zpad
        for i in range(s_len // rt):
            rows = pl.ds(i * rt, rt)
            kn, _, _ = _head_norm(k_ref[rows, :], gk_ref[...], m0, m1)
            k2[pl.ds(padk + i * rt, rt), :] = kn.astype(BF16)
            v2[pl.ds(padk + i * rt, rt), :] = v_ref[rows, :].astype(BF16)

    if mode == "A":
        keys_and_values()
    else:
        pl.when(first_kv)(keys_and_values)

    for i in range(s_len // rt):
        rows = pl.ds(i * rt, rt)
        qn, _, _ = _head_norm(q_ref[rows, :], gq_ref[...], m0, m1)
        qn = qn * (HEAD_DIM ** -0.5)
        if mode == "A":
            qs[rows, :] = qn.astype(BF16)
            if dos is not None:
                dos[rows, :] = do_ref[rows, :].astype(BF16)
        else:
            _store_stacked(qs, i, _stack_heads(first_kv, qn, m0, m1))
            if dos is not None:
                _store_stacked(dos, i, _stack_heads(first_kv, do_ref[rows, :], m0, m1))


def _softmax_terms(mode, s, sink):
    mx = jnp.max(s, axis=-1, keepdims=True)
    if mode == "B":
        mx = jnp.maximum(mx, sink)
    e = jnp.exp(s - mx)
    l = jnp.sum(e, axis=-1, keepdims=True)
    if mode == "B":
        l = l + jnp.exp(sink - mx)
    return e, mx, l


def _sink_column(sink_ref, group):
    row = lax.broadcasted_iota(jnp.int32, (STACK, 1), 0)
    col = jnp.zeros((STACK, 1), F32)
    for h in range(GROUP):
        col = jnp.where((row >= h * QTILE) & (row < (h + 1) * QTILE), sink_ref[GROUP * group + h], col)
    return col


def _head_deltas(dd, m0, m1):
    cols = []
    for half in range(GROUP // 2):
        dh = dd[:, half * LANES:(half + 1) * LANES]
        cols += [jnp.sum(dh * m0, axis=-1, keepdims=True), jnp.sum(dh * m1, axis=-1, keepdims=True)]
    return jnp.concatenate(cols, axis=0)


def _attn_cols(mode):
    if mode == "A":
        return (lambda b, g: (b, g)), (lambda b, g: (b, 2 + g)), (lambda b, g: (b, 4 + g))
    return (lambda b, g: (b, 6 + g)), (lambda b, g: (b, 16)), (lambda b, g: (b, 17))


def _attn_fwd(mode, qkv, gq, gk, bias, sinks, bl, s_len, name):
    bw = bias.shape[-1]
    padk = bw - QTILE
    nt = s_len // QTILE
    qmap, kmap, vmap = _attn_cols(mode)

    kw = _kv_width(mode)

    def body(q_ref, k_ref, v_ref, gq_ref, gk_ref, bias_ref, sink_ref, o_ref, qs, k2, v2, s_buf, *rest):
        o_buf = rest[0] if rest else None
        group = pl.program_id(1)
        m0, m1 = _head_masks()
        first_kv = group == 0
        _attn_prep(mode, group, s_len, padk, q_ref, k_ref, v_ref, gq_ref, gk_ref, qs, k2, v2)
        col = lax.broadcasted_iota(jnp.int32, (STACK, bw), 1)
        sink = _sink_column(sink_ref, group)

        def scores(m, slot):
            r0 = pl.multiple_of(m * QTILE, QTILE)
            s = _dot_nt(_load_stacked(mode, qs, m), k2[pl.ds(r0, bw), :]) + bias_ref[...]
            s_buf[slot] = jnp.where(col >= (padk - r0), s, NEG_INF)

        def finish_tile(m, slot):
            r0 = pl.multiple_of(m * QTILE, QTILE)
            e, _, l = _softmax_terms(mode, s_buf[slot], sink)
            if mode == "A":
                o_st = _dot(e.astype(BF16), v2[pl.ds(r0, bw), :]) / l
                heads = [o_st[h * QTILE:(h + 1) * QTILE] for h in range(GROUP)]
                o_ref[pl.ds(r0, QTILE), :] = _unstack_heads(mode, first_kv, heads, m0, m1)
            else:
                o_buf[pl.ds(pl.multiple_of(m * STACK, STACK), STACK), :] = _dot((e * (1.0 / l)).astype(BF16),
                                                                                 v2[pl.ds(r0, bw), :])

        scores(0, 0)

        def pair(j, carry):
            scores(2 * j + 1, 1)
            finish_tile(2 * j, 0)
            scores(jnp.minimum(2 * j + 2, nt - 1), 0)
            finish_tile(2 * j + 1, 1)
            return carry

        lax.fori_loop(0, nt // 2, pair, 0, unroll=2)
        if mode == "B":
            for m in range(nt):
                heads = [o_buf[pl.ds(m * STACK + h * QTILE, QTILE), :] for h in range(GROUP)]
                o_ref[pl.ds(m * QTILE, QTILE), :] = _unstack_heads(mode, first_kv, heads, m0, m1)

    blk = lambda w, f: pl.BlockSpec((s_len, w), f)
    return pl.pallas_call(
        body, name=name, grid=(bl, B_Q_HEADS // GROUP),
        out_shape=jax.ShapeDtypeStruct((bl * s_len, B_Q_HEADS * HEAD_DIM), F32),
        in_specs=[blk(QW, qmap), blk(kw, kmap), blk(kw, vmap),
                  pl.BlockSpec((1, QW), lambda b, g: (0, 0)), pl.BlockSpec((1, kw), lambda b, g: (0, 0)),
                  pl.BlockSpec((STACK, bw), lambda b, g: (g, 0)),
                  pl.BlockSpec(memory_space=pltpu.SMEM)],
        out_specs=blk(QW, lambda b, g: (b, g)),
        scratch_shapes=[pltpu.VMEM(_q_scratch_shape(mode, s_len), BF16)] + [pltpu.VMEM((s_len + padk, kw), BF16)] * 2
        + [pltpu.VMEM((2, STACK, bw), F32)] + ([pltpu.VMEM((GROUP * s_len, LANES), F32)] if mode == "B" else []),
        compiler_params=_params(("arbitrary", "arbitrary")),
    )(qkv, qkv, qkv, gq, gk, bias.reshape(-1, bw), sinks)


def _attn_bwd(mode, qkv, gq, gk, bias, sinks, y, dy, bl, s_len, name):
    bw = bias.shape[-1]
    padk = bw - QTILE
    nt = s_len // QTILE
    qmap, kmap, vmap = _attn_cols(mode)
    t = bl * s_len
    kw = _kv_width(mode)
    kvw = 4 * LANES if mode == "A" else LANES
    dp_ahead = True

    def body(q_ref, k_ref, v_ref, gq_ref, gk_ref, bias_ref, sink_ref, y_ref, dy_ref,
             dq_ref, dk_ref, dv_ref, dgq_ref, dgk_ref, dbias_ref, dsink_ref,
             qs, k2, v2, dos, dqs, dk, dv, s_buf, dp_buf):
        group = pl.program_id(1)
        m0, m1 = _head_masks()
        first_kv = group == 0
        _attn_prep(mode, group, s_len, padk, q_ref, k_ref, v_ref, gq_ref, gk_ref, qs, k2, v2, dy_ref, dos)
        dk[...] = jnp.zeros_like(dk)
        dv[...] = jnp.zeros_like(dv)
        dbias_ref[...] = jnp.zeros_like(dbias_ref)
        col = lax.broadcasted_iota(jnp.int32, (STACK, bw), 1)
        lane8 = lax.broadcasted_iota(jnp.int32, (8, LANES), 1)
        sink = _sink_column(sink_ref, group)

        def ahead(m, slot):
            r0 = pl.multiple_of(m * QTILE, QTILE)
            band = pl.ds(r0, bw)
            s = _dot_nt(_load_stacked(mode, qs, m), k2[band, :]) + bias_ref[...]
            s_buf[slot] = jnp.where(col >= (padk - r0), s, NEG_INF)
            if dp_ahead:
                dp_buf[slot] = _dot_nt(_load_stacked(mode, dos, m), v2[band, :])

        def tile(m, slot, dsink):
            r0 = pl.multiple_of(m * QTILE, QTILE)
            rows = pl.ds(r0, QTILE)
            band = pl.ds(r0, bw)
            q_st = _load_stacked(mode, qs, m)
            do_st = _load_stacked(mode, dos, m)
            delta = _head_deltas(dy_ref[rows, :] * y_ref[rows, :], m0, m1)
            kb = k2[band, :]
            e, mx, l = _softmax_terms(mode, s_buf[slot], sink)
            inv = 1.0 / l
            pn = e * inv
            ds = pn * ((dp_buf[slot] if dp_ahead else _dot_nt(do_st, v2[band, :])) - delta)
            if mode == "A":
                dbias_ref[...] += ds
            else:
                part = jnp.exp(sink - mx) * inv * delta
                for h in range(GROUP):
                    dsink = dsink - jnp.where(lane8 == h, jnp.sum(part[h * QTILE:(h + 1) * QTILE]), 0.0)
            dsb = ds.astype(BF16)
            dv[band, :] += _dot_tn(pn.astype(BF16), do_st)
            dk[band, :] += _dot_tn(dsb, q_st)
            dq_st = _dot(dsb, kb)
            if mode == "A":
                heads = [dq_st[h * QTILE:(h + 1) * QTILE] for h in range(GROUP)]
                dq_ref[rows, :] = _unstack_heads(mode, first_kv, heads, m0, m1)
            else:
                dqs[pl.ds(pl.multiple_of(m * STACK, STACK), STACK), :] = dq_st
            return dsink

        ahead(0, 0)

        def pair(j, dsink):
            ahead(2 * j + 1, 1)
            dsink = tile(2 * j, 0, dsink)
            ahead(jnp.minimum(2 * j + 2, nt - 1), 0)
            return tile(2 * j + 1, 1, dsink)

        dsink = lax.fori_loop(0, nt // 2, pair, jnp.zeros((8, LANES), F32), unroll=2)
        dsink_ref[...] = dsink

        rt = 2 * QTILE
        dgq = jnp.zeros((1, QW), F32)
        dgk = jnp.zeros((1, kw), F32)
        for i in range(s_len // rt):
            rows = pl.ds(i * rt, rt)
            src = pl.ds(padk + i * rt, rt)
            gq_v, gk_v = gq_ref[...], gk_ref[...]
            _, qh, qr = _head_norm(q_ref[rows, :], gq_v, m0, m1)
            _, kh, kr = _head_norm(k_ref[rows, :], gk_v, m0, m1)
            if mode == "A":
                dqn = dq_ref[rows, :] * (HEAD_DIM ** -0.5)
            else:
                dqn = jnp.concatenate(
                    [_unstack_heads(mode, first_kv, [dqs[pl.ds((2 * i + half) * STACK + h * QTILE, QTILE), :]
                                                     for h in range(GROUP)], m0, m1)
                     for half in range(2)], axis=0) * (HEAD_DIM ** -0.5)
            dq_raw, dgq_i = _head_norm_bwd(qh, qr, gq_v, dqn, m0, m1)
            dk_raw, dgk_i = _head_norm_bwd(kh, kr, gk_v, dk[src, :], m0, m1)
            dvn = dv[src, :]
            dq_ref[rows, :] = dq_raw.astype(dq_ref.dtype)
            if mode == "A":
                dk_ref[rows, :] = dk_raw.astype(dk_ref.dtype)
                dv_ref[rows, :] = dvn.astype(dv_ref.dtype)
            else:
                @pl.when(group == 0)
                def _():
                    dk_ref[rows, :] = dk_raw
                    dv_ref[rows, :] = dvn

                @pl.when(group != 0)
                def _():
                    dk_ref[rows, :] += dk_raw
                    dv_ref[rows, :] += dvn
            dgq, dgk = dgq + dgq_i, dgk + dgk_i
        dgq_ref[...] = jnp.broadcast_to(dgq, (8, QW))
        dgk_ref[...] = jnp.broadcast_to(dgk, (8, kw))

    ng = B_Q_HEADS // GROUP
    blk = lambda w, f: pl.BlockSpec((s_len, w), f)
    small = lambda w: pl.BlockSpec((None, None, 8, w), lambda b, g: (b, g, 0, 0))
    own = lambda b, g: (b, g)
    kvmap = own if mode == "A" else (lambda b, g: (b, 0))
    pad_f32 = pltpu.VMEM((s_len + padk, kw), F32)
    pad_bf = pltpu.VMEM((s_len + padk, kw), BF16)
    stack_bf = pltpu.VMEM(_q_scratch_shape(mode, s_len), BF16)
    outs = pl.pallas_call(
        body, name=name, grid=(bl, ng),
        out_shape=[jax.ShapeDtypeStruct((t, ng * QW), F32 if mode == "A" else BF16),
                   jax.ShapeDtypeStruct((t, kvw), BF16 if mode == "A" else F32),
                   jax.ShapeDtypeStruct((t, kvw), BF16 if mode == "A" else F32),
                   jax.ShapeDtypeStruct((bl, ng, 8, QW), F32), jax.ShapeDtypeStruct((bl, ng, 8, kw), F32),
                   jax.ShapeDtypeStruct((bl, ng * STACK, bw), F32), jax.ShapeDtypeStruct((bl, ng, 8, LANES), F32)],
        in_specs=[blk(QW, qmap), blk(kw, kmap), blk(kw, vmap),
                  pl.BlockSpec((1, QW), lambda b, g: (0, 0)), pl.BlockSpec((1, kw), lambda b, g: (0, 0)),
                  pl.BlockSpec((STACK, bw), lambda b, g: (g, 0)),
                  pl.BlockSpec(memory_space=pltpu.SMEM),
                  blk(QW, own), blk(QW, own)],
        out_specs=[blk(QW, own), blk(kw, kvmap), blk(kw, kvmap), small(QW), small(kw),
                   pl.BlockSpec((None, STACK, bw), lambda b, g: (b, g, 0)), small(LANES)],
        scratch_shapes=[stack_bf, pad_bf, pad_bf, stack_bf,
                        pltpu.VMEM((8, LANES) if mode == "A" else _q_scratch_shape(mode, s_len), F32),
                        pad_f32, pad_f32, pltpu.VMEM((2, STACK, bw), F32),
                        pltpu.VMEM((2, STACK, bw) if dp_ahead else (8, LANES), F32)],
        compiler_params=_params(("arbitrary", "arbitrary")),
    )(qkv, qkv, qkv, gq, gk, bias.reshape(-1, bw), sinks, y, dy)
    outs = list(outs)
    outs[5] = outs[5].reshape(bl, B_Q_HEADS, QTILE, bw)
    return outs


def _band_geometry(prev):
    bw = QTILE + prev * CHUNK
    i = np.arange(QTILE)[:, None]
    j = np.arange(bw)[None, :]
    dist = i + prev * CHUNK - j
    valid = (j // CHUNK >= i // CHUNK) & (j // CHUNK <= i // CHUNK + prev)
    return dist, valid


A_VAR0 = (A_PREV * CHUNK - A_MAX_REL) // LANES * LANES


A_NVAR = QTILE + A_PREV * CHUNK - A_VAR0


def _skew_rows(x, sign):
    rows, n = x.shape
    row = lax.broadcasted_iota(jnp.int32, x.shape, 0)
    b = 1
    while b < rows:
        x = jnp.where((row & b) != 0, pltpu.roll(x, (sign * b) % n, 1), x)
        b *= 2
    return x


def _rel_bias_expand(table, name):
    _, valid = _band_geometry(A_PREV)
    bw = valid.shape[1]
    valid_f = jnp.asarray(valid.astype(np.float32))
    rev = jnp.flip(table[:, 1:], axis=1).reshape(A_HEADS, 1, A_NVAR)

    def body(rev_ref, valid_ref, o_ref):
        rowv = jnp.broadcast_to(rev_ref[...], (QTILE, A_NVAR))
        top = rowv[:, 0:1]
        var = _skew_rows(rowv, 1)
        row = lax.broadcasted_iota(jnp.int32, (QTILE, A_NVAR), 0)
        colv = lax.broadcasted_iota(jnp.int32, (QTILE, A_NVAR), 1)
        var = jnp.where(colv < row, top, var)
        ok = valid_ref[...] > 0.5
        o_ref[:, :A_VAR0] = jnp.where(ok[:, :A_VAR0], top, NEG_INF)
        o_ref[:, A_VAR0:] = jnp.where(ok[:, A_VAR0:], var, NEG_INF)

    return pl.pallas_call(
        body, name=name, grid=(A_HEADS,),
        out_shape=jax.ShapeDtypeStruct((A_HEADS, QTILE, bw), F32),
        in_specs=[pl.BlockSpec((None, 1, A_NVAR), lambda h: (h, 0, 0)), pl.BlockSpec((QTILE, bw), lambda h: (0, 0))],
        out_specs=pl.BlockSpec((None, QTILE, bw), lambda h: (h, 0, 0)),
        compiler_params=_params(("arbitrary",)),
    )(rev, valid_f)


def _rel_bias_grad(dbias, name):
    bl = dbias.shape[0]
    bw = dbias.shape[-1]

    def body(db_ref, o_ref):
        g = db_ref[0]
        for b in range(1, bl):
            g = g + db_ref[b]
        sk = _skew_rows(g[:, A_VAR0:], -1)
        row = lax.broadcasted_iota(jnp.int32, (QTILE, A_NVAR), 0)
        colv = lax.broadcasted_iota(jnp.int32, (QTILE, A_NVAR), 1)
        wrapped = (row + colv) >= A_NVAR
        main = jnp.sum(jnp.where(wrapped, 0.0, sk), axis=0, keepdims=True)
        top = jnp.sum(g[:, :A_VAR0]) + jnp.sum(jnp.where(wrapped, sk, 0.0))
        o_ref[:, :A_NVAR] = jnp.broadcast_to(main, (8, A_NVAR))
        o_ref[:, A_NVAR:] = jnp.full((8, LANES), top, F32)

    out = pl.pallas_call(
        body, name=name, grid=(A_HEADS,),
        out_shape=jax.ShapeDtypeStruct((A_HEADS, 8, A_NVAR + LANES), F32),
        in_specs=[pl.BlockSpec((bl, None, QTILE, bw), lambda h: (0, h, 0, 0))],
        out_specs=pl.BlockSpec((None, 8, A_NVAR + LANES), lambda h: (h, 0, 0)),
        compiler_params=_params(("arbitrary",)),
    )(dbias)
    main, top = out[:, 0, :A_NVAR], out[:, 0, A_NVAR]
    fm = jnp.flip(main, axis=1)
    return jnp.concatenate([jnp.zeros((A_HEADS, 1), F32), fm[:, :-1], fm[:, -1:] + top[:, None]], axis=1)


def _alibi_bias():
    dist, valid = _band_geometry(B_PREV)
    slopes = np.array([2.0 ** (-8.0 * (h + 1) / B_Q_HEADS) for h in range(B_Q_HEADS)], dtype=np.float32)
    bias = -slopes[:, None, None] * np.abs(dist).astype(np.float32)[None]
    return jnp.asarray(np.where(valid[None], bias, np.float32(NEG_INF)).astype(np.float32))


SMALL_NAMES = ("ffn1_norm", "mix_norm", "ffn2_norm", "ple_norm", "a_q_norm", "a_k_norm", "b_q_norm", "b_k_norm",
               "a_rel_bias", "b_sinks", "loss")


def _pack_small(vals):
    rows = []
    for nme in SMALL_NAMES:
        v = vals[nme].astype(F32)
        if nme == "a_rel_bias":
            v = jnp.pad(v.reshape(A_HEADS, -1), ((0, 0), (0, 3 * LANES - (2 * A_MAX_REL + 1))))
        v = v.reshape(-1)
        v = jnp.pad(v, (0, (-v.shape[0]) % LANES))
        rows.append(v.reshape(-1, LANES))
    out = jnp.concatenate(rows, axis=0)
    return jnp.pad(out, ((0, (-out.shape[0]) % 8), (0, 0)))


def _unpack_small(packed, shapes):
    out, r = {}, 0
    for nme in SMALL_NAMES:
        shp = shapes[nme]
        if nme == "a_rel_bias":
            nr = A_HEADS * 3
            out[nme] = packed[r:r + nr].reshape(A_HEADS, 3 * LANES)[:, :2 * A_MAX_REL + 1].reshape(shp)
        else:
            size = int(np.prod(shp)) if shp else 1
            nr = -(-size // LANES)
            out[nme] = packed[r:r + nr].reshape(-1)[:size].reshape(shp)
        r += nr
    return out


BIG_NAMES = ("ffn1_w_gu", "ffn1_w_down", "w_in", "w_gate", "w_proj_a", "w_proj_b", "w_out",
             "ffn2_w_gu", "ffn2_w_down", "w_ple_gate", "w_ple_proj")
WEIGHT_ORDER = ("ffn1_norm", "ffn1_w_gu", "ffn1_w_down", "mix_norm", "w_in", "a_q_norm", "a_k_norm", "a_rel_bias",
                "b_q_norm", "b_k_norm", "b_sinks", "w_gate", "w_proj_a", "w_proj_b", "w_out", "ffn2_norm",
                "ffn2_w_gu", "ffn2_w_down", "ple_norm", "w_ple_gate", "w_ple_proj")


TRANSPOSED = ("ffn1_w_gu", "ffn2_w_gu", "w_in")


def _local(a, nme):
    return a[0].T if nme in TRANSPOSED else a[0]


def _full_cols(wg):
    nb, k, n = wg.shape
    return jnp.transpose(wg, (1, 0, 2)).reshape(k, nb * n)


def _step(x, p, target, w, m, v):
    bl, s_len, d = x.shape
    t = bl * s_len
    h0 = x.reshape(t, d)
    pt = p.reshape(t, p.shape[-1])
    tgt = target.reshape(t, d)

    g_ffn1, g_mix, g_ffn2, g_ple = w["ffn1_norm"], w["mix_norm"], w["ffn2_norm"], w["ple_norm"]
    tiled = lambda a, width: jnp.tile(a.reshape(1, HEAD_DIM), (1, width // HEAD_DIM))
    gqa, gka = tiled(w["a_q_norm"], QW), tiled(w["a_k_norm"], _kv_width("A"))
    gqb, gkb = tiled(w["b_q_norm"], QW), tiled(w["b_k_norm"], _kv_width("B"))
    sinks = w["b_sinks"].reshape(B_Q_HEADS)
    bias_b = _alibi_bias()

    ffn1_names = ("ffn1_w_gu", "ffn1_w_down")
    shard = {nme: _local(w[nme], nme).astype(BF16) for nme in ffn1_names}
    send1, recv1, bufs, token = _gather_start([shard["ffn1_w_gu"]], h0, "gather_start_ffn1_gu")
    dsend1, drecv1, dbufs, token = _gather_start([shard["ffn1_w_down"]], token, "gather_start_ffn1_down")
    zero = token[0, 0]
    shard.update({nme: (_local(w[nme], nme) + zero).astype(BF16) for nme in BIG_NAMES if nme not in ffn1_names})
    bias_a = _rel_bias_expand(w["a_rel_bias"][0] + zero, "rel_bias_expand")
    send2, recv2, bufs, token = _gather_pass(send1, recv1, bufs, bias_a, "gather_pass_ffn1_gu")
    (wgu1,) = _gather_wait(send2, recv2, bufs, shard["ffn2_w_gu"], "gather_wait_ffn1_gu")
    nf = wgu1.shape[1]
    mixer_names = ("w_in", "w_gate")
    rest_names = ("w_proj_a", "w_proj_b", "w_out", "ffn2_w_gu", "ffn2_w_down", "w_ple_gate", "w_ple_proj")
    send1, recv1, bufs, token = _gather_start([shard[nme] for nme in mixer_names], wgu1, "gather_start_mixer")
    rsend1, rrecv1, rest_bufs, token = _gather_start([shard[nme] for nme in rest_names], token, "gather_start_rest")

    gu1, a1f = _ffn_up(h0, g_ffn1 + token[0, 0], wgu1, "ffn1_up")
    dsend2, drecv2, dbufs, token = _gather_pass(dsend1, drecv1, dbufs, a1f, "gather_pass_ffn1_down")
    (wd1,) = _gather_wait(dsend2, drecv2, dbufs, token, "gather_wait_ffn1_down")
    wd1 = wd1.reshape(N_DEV // 2, nf, d)
    h1 = _ffn_down(h0, a1f, wd1, "ffn1_down")
    send2, recv2, bufs, token = _gather_pass(send1, recv1, bufs, h1, "gather_pass_mixer")
    win, wgate = _gather_wait(send2, recv2, bufs, token, "gather_wait_mixer")
    win = win.reshape(IN_COLS, d)
    un, qkv, gate = _proj_fwd(h1, g_mix, win, wgate, "proj_fwd")
    ya = _attn_fwd("A", qkv, gqa, gka, bias_a, sinks, bl, s_len, "attn_a_fwd")
    rsend2, rrecv2, rest_bufs, token = _gather_pass(rsend1, rrecv1, rest_bufs, ya, "gather_pass_rest")
    yb = _attn_fwd("B", qkv, gqb + token[0, 0], gkb, bias_b, sinks, bl, s_len, "attn_b_fwd")
    gathered = dict(zip(rest_names, _gather_wait(rsend2, rrecv2, rest_bufs, yb, "gather_wait_rest")))
    wgu2 = gathered["ffn2_w_gu"]
    wd2 = gathered["ffn2_w_down"].reshape(N_DEV // 2, nf, d)
    wpa = _full_cols(gathered["w_proj_a"])
    wpb = _full_cols(gathered["w_proj_b"])
    wpe = _full_cols(gathered["w_ple_proj"])
    wout = gathered["w_out"].reshape(d, d)
    wpg = gathered["w_ple_gate"].reshape(d, d)
    h2, merged, pa, pb = _merge_fwd(h1, ya, yb, gate, wpa, wpb, wout, "merge_fwd")
    h3, gu2 = _ffn_fwd(h2, g_ffn2, wgu2, wd2, "ffn2_fwd")
    dh3, dz4, dpp, n4, dg_ple, loss_part = _ple_loss(h3, g_ple, pt, tgt, wpg, wpe, "ple_loss")

    xi, yi, ci = _place()
    me = jnp.stack([4 * xi + 2 * yi + ci, 2 * xi + yi]).astype(jnp.int32)
    g32, g16, big, pairs = {}, {}, {}, {}

    def keep(nme, pair, rows=None):
        for store, g in zip((g32, g16), pair):
            store[nme] = g if rows is None else g.reshape(N_DEV, rows, d)

    def start(names, after, tag):
        send, recv, parts, lands, token = _scatter_start([g16[nme] for nme in names], after, "grads_start_" + tag)
        return names, send, recv, parts, lands, token

    def start_two_level(names, after, tag):
        views = [g16[nme].reshape((4, 2) + g16[nme].shape[1:]) for nme in names]
        for nme, got in zip(names, _pair_exchange(views, "grads_pair_" + tag)):
            pairs[nme] = got.reshape((4,) + got.shape[2:])
        sums = [_pair_sum(g32[nme], pairs[nme], me, "pair_sum_" + nme) for nme in names]
        send, recv, parts, lands, token = _scatter_start(sums, after, "grads_start_" + tag, SAME_CORE_CHIPS)
        return names, send, recv, parts, lands, token

    def finish(state, after, tag):
        names, send, recv, parts, lands, _ = state
        relations = SAME_CORE_CHIPS if names[0] in pairs else ALL_PEERS
        lands = _scatter_wait(send, recv, parts, lands, after, "grads_wait_" + tag, relations)
        return names, lands

    def adam(done, dep):
        for nme, land in zip(*done):
            outs = _final_adam(g32[nme], land, _local(w[nme], nme), _local(m[nme], nme), _local(v[nme], nme), me, dep,
                               "adam_" + nme, pairs.get(nme))
            big[nme] = [(o.T if nme in TRANSPOSED else o)[None] for o in outs]

    keep("w_ple_gate", _dw(n4, dz4, 1, d, "dw_ple_gate"), d // N_DEV)
    keep("w_ple_proj", _dw(pt, dpp, N_DEV, d // N_DEV, "dw_ple_proj"))

    dh2, dgu2, a2, n3, dg_ffn2 = _ffn_bwd(dh3, h2, g_ffn2, gu2, wgu2, wd2, "ffn2_bwd")
    keep("ffn2_w_down", _dw(a2, dh3, N_DEV // 2, d, "dw_ffn2_down", 0.5), nf // 2)
    early = [(start(("w_ple_gate", "w_ple_proj", "ffn2_w_down"), dh2, "ffn2_down"), "ffn2_down")]
    keep("ffn2_w_gu", _dw(dgu2, n3, N_DEV, d, "dw_ffn2_gu", dep=early[-1][0][-1]))
    flight = start(("ffn2_w_gu",), dh2, "ffn2")

    dpa, dpb, dzg, dya, dyb = _merge_bwd(dh2, pa, pb, gate, wpa, wpb, wout, "merge_bwd")
    keep("w_out", _dw(merged, dh2, 1, d, "dw_out"), d // N_DEV)
    keep("w_proj_a", _dw(ya, dpa, N_DEV, d // N_DEV, "dw_proj_a"))
    keep("w_proj_b", _dw(yb, dpb, N_DEV, d // N_DEV, "dw_proj_b"))
    keep("w_gate", _dw(un, dzg, N_DEV, 2 * d // N_DEV, "dw_gate"))

    tok = flight[-1][0, 0]
    dqa, dka, dva, dgqa, dgka, dbias, _ = _attn_bwd("A", qkv, gqa + tok, gka, bias_a, sinks, ya, dya, bl, s_len,
                                                     "attn_a_bwd")
    dqb, dkb, dvb, dgqb, dgkb, _, dsink = _attn_bwd("B", qkv, gqb, gkb, bias_b, sinks, yb, dyb, bl, s_len, "attn_b_bwd")
    dqkv = [dqa, dka, dva, dqb, dkb, dvb]
    dtab = _rel_bias_grad(dbias, "rel_bias_grad")

    dh1, dg_mix = _proj_bwd(dh2, h1, g_mix, dzg, dqkv, win, wgate, "proj_bwd")
    keep("w_in", _dw_rows(dqkv, un, "dw_in"), IN_COLS // N_DEV)
    waiting = [finish(state, g32["w_in"], tag) for state, tag in early]
    done = finish(flight, waiting[-1][1][0], "ffn2")
    flight = start(("w_out", "w_proj_a", "w_proj_b", "w_gate", "w_in"), done[1][0], "mixer")
    waiting.append(done)

    dh0, dgu1, a1, n1, dg_ffn1 = _ffn_bwd(dh1, h0, g_ffn1 + flight[-1][0, 0], gu1, wgu1, wd1, "ffn1_bwd")
    keep("ffn1_w_down", _dw(a1, dh1, N_DEV // 2, d, "dw_ffn1_down", 0.5), nf // 2)
    done = finish(flight, g32["ffn1_w_down"], "mixer")
    flight = start(("ffn1_w_down",), done[1][0], "ffn1_down")
    waiting.append(done)

    keep("ffn1_w_gu", _dw(dgu1, n1, N_DEV, d, "dw_ffn1_gu", dep=flight[-1]))
    done = finish(flight, g32["ffn1_w_gu"], "ffn1_down")
    flight = start_two_level(("ffn1_w_gu",), done[1][0], "ffn1_gu")
    for group in waiting + [done]:
        adam(group, flight[-1])
    behind = 0.0 * big["ffn1_w_down"][0][0, 0, :1]
    smalls = (dg_ffn1, dg_mix, dg_ffn2, dg_ple + behind, dgqa, dgka, dgqb, dgkb, dtab, dsink)
    return dh0, loss_part, big, smalls, flight, finish, adam


def kernel(x, p, ffn1_norm, ffn1_w_gu, ffn1_w_down, mix_norm, w_in, a_q_norm, a_k_norm, a_rel_bias, b_q_norm, b_k_norm, b_sinks, w_gate, w_proj_a, w_proj_b, w_out, ffn2_norm, ffn2_w_gu, ffn2_w_down, ple_norm, w_ple_gate, w_ple_proj, loss_target, m_ffn1_norm, m_ffn1_w_gu, m_ffn1_w_down, m_mix_norm, m_w_in, m_a_q_norm, m_a_k_norm, m_a_rel_bias, m_b_q_norm, m_b_k_norm, m_b_sinks, m_w_gate, m_w_proj_a, m_w_proj_b, m_w_out, m_ffn2_norm, m_ffn2_w_gu, m_ffn2_w_down, m_ple_norm, m_w_ple_gate, m_w_ple_proj, v_ffn1_norm, v_ffn1_w_gu, v_ffn1_w_down, v_mix_norm, v_w_in, v_a_q_norm, v_a_k_norm, v_a_rel_bias, v_b_q_norm, v_b_k_norm, v_b_sinks, v_w_gate, v_w_proj_a, v_w_proj_b, v_w_out, v_ffn2_norm, v_ffn2_w_gu, v_ffn2_w_down, v_ple_norm, v_w_ple_gate, v_w_ple_proj):
    w = dict(ffn1_norm=ffn1_norm, ffn1_w_gu=ffn1_w_gu, ffn1_w_down=ffn1_w_down, mix_norm=mix_norm, w_in=w_in,
             a_q_norm=a_q_norm, a_k_norm=a_k_norm, a_rel_bias=a_rel_bias, b_q_norm=b_q_norm, b_k_norm=b_k_norm,
             b_sinks=b_sinks, w_gate=w_gate, w_proj_a=w_proj_a, w_proj_b=w_proj_b, w_out=w_out, ffn2_norm=ffn2_norm,
             ffn2_w_gu=ffn2_w_gu, ffn2_w_down=ffn2_w_down, ple_norm=ple_norm, w_ple_gate=w_ple_gate,
             w_ple_proj=w_ple_proj)
    m = dict(ffn1_norm=m_ffn1_norm, ffn1_w_gu=m_ffn1_w_gu, ffn1_w_down=m_ffn1_w_down, mix_norm=m_mix_norm,
             w_in=m_w_in, a_q_norm=m_a_q_norm, a_k_norm=m_a_k_norm, a_rel_bias=m_a_rel_bias, b_q_norm=m_b_q_norm,
             b_k_norm=m_b_k_norm, b_sinks=m_b_sinks, w_gate=m_w_gate, w_proj_a=m_w_proj_a, w_proj_b=m_w_proj_b,
             w_out=m_w_out, ffn2_norm=m_ffn2_norm, ffn2_w_gu=m_ffn2_w_gu, ffn2_w_down=m_ffn2_w_down,
             ple_norm=m_ple_norm, w_ple_gate=m_w_ple_gate, w_ple_proj=m_w_ple_proj)
    v = dict(ffn1_norm=v_ffn1_norm, ffn1_w_gu=v_ffn1_w_gu, ffn1_w_down=v_ffn1_w_down, mix_norm=v_mix_norm,
             w_in=v_w_in, a_q_norm=v_a_q_norm, a_k_norm=v_a_k_norm, a_rel_bias=v_a_rel_bias, b_q_norm=v_b_q_norm,
             b_k_norm=v_b_k_norm, b_sinks=v_b_sinks, w_gate=v_w_gate, w_proj_a=v_w_proj_a, w_proj_b=v_w_proj_b,
             w_out=v_w_out, ffn2_norm=v_ffn2_norm, ffn2_w_gu=v_ffn2_w_gu, ffn2_w_down=v_ffn2_w_down,
             ple_norm=v_ple_norm, w_ple_gate=v_w_ple_gate, w_ple_proj=v_w_ple_proj)
    bl, s_len, d = x.shape

    dh0, loss_part, big, smalls, flight, finish, adam = _step(x, p[0], loss_target, w, m, v)
    dg_ffn1, dg_mix, dg_ffn2, dg_ple, dgqa, dgka, dgqb, dgkb, dtab, dsink = smalls

    fold = lambda a: a[:, :, 0, :].reshape(-1, HEAD_DIM).sum(axis=0)
    small_part = dict(
        ffn1_norm=dg_ffn1, mix_norm=dg_mix, ffn2_norm=dg_ffn2, ple_norm=dg_ple,
        a_q_norm=fold(dgqa), a_k_norm=fold(dgka), b_q_norm=fold(dgqb), b_k_norm=fold(dgkb),
        a_rel_bias=dtab,
        b_sinks=dsink.sum(axis=0)[:, 0, :GROUP].reshape(B_Q_HEADS),
        loss=loss_part[0, :1])
    zero1 = jnp.zeros((1,), F32)
    shapes = {nme: w[nme].shape for nme in SMALL_NAMES if nme != "loss"}
    shapes["loss"] = ()
    pk = lambda src: _pack_small({**{nme: src[nme] for nme in SMALL_NAMES if nme != "loss"}, "loss": zero1})
    sg, sd, sm, sv = _small_allreduce_adam(_pack_small(small_part), pk(w), pk(m), pk(v), "small_allreduce_adam")
    adam(finish(flight, sg, "ffn1_gu"), sg)
    sg, sd, sm, sv = (_unpack_small(a, shapes) for a in (sg, sd, sm, sv))

    def pick(i):
        out = []
        for nme in WEIGHT_ORDER:
            out.append(big[nme][i] if nme in big else (sg, sd, sm, sv)[i][nme])
        return out

    return (sg["loss"], dh0.reshape(bl, s_len, d), *pick(0), *pick(1), *pick(2), *pick(3))
```
